```python
import jax
import jax.numpy as jnp
from jax import lax
import numpy as np

D_MODEL = 1024
BATCH = 16
SEQ = 2048
DEPTH = 2

CONV_WIDTH = 512
CONV_GROUPS = 8
CONV_K = 3
MLA_HEADS = 8
MLA_Q_LORA = 256
MLA_KV_LORA = 128
MLA_NOPE = 64
MLA_ROPE = 32
MLA_V = 64
MLA_QK = MLA_NOPE + MLA_ROPE
ROPE_THETA = 10000.0
DIL_PATTERNS = ((128, 1), (512, 4), (2048, 16))
DIL_GROUPS = len(DIL_PATTERNS)
DIL_HEADS = 8
DIL_HEAD_DIM = 64
DIL_WIDTH = DIL_HEADS * DIL_HEAD_DIM
N_BRANCH = 3
Q_BLOCK = 128
EPS = 1e-6

SPLIT_SIZES = ((CONV_WIDTH,) * 4
               + (MLA_Q_LORA, MLA_KV_LORA, MLA_ROPE, MLA_HEADS * MLA_V)
               + (DIL_GROUPS * DIL_WIDTH,) * 3 + (DIL_WIDTH,)
               + (N_BRANCH * D_MODEL,))
SPLIT_POINTS = tuple(int(v) for v in np.cumsum(SPLIT_SIZES)[:-1])
N_IN = int(sum(SPLIT_SIZES))

kernel_name = 'hybrid_gatedconv_mla_dilated_layer'


def rmsnorm(x, g):
    xf = x.astype(jnp.float32)
    y = xf * lax.rsqrt(jnp.mean(xf * xf, axis=-1, keepdims=True) + EPS)
    return (y * g.astype(jnp.float32)).astype(x.dtype)


def rope_tables(S):
    inv = ROPE_THETA ** (-jnp.arange(0, MLA_ROPE, 2, dtype=jnp.float32) / MLA_ROPE)
    ang = jnp.arange(S, dtype=jnp.float32)[:, None] * inv[None, :]
    return jnp.cos(ang), jnp.sin(ang)


def apply_rope(t, cos, sin):
    tf = t.astype(jnp.float32)
    t1, t2 = jnp.split(tf, 2, axis=-1)
    c = cos[None, :, None, :]
    s = sin[None, :, None, :]
    return jnp.concatenate([t1 * c - t2 * s, t2 * c + t1 * s], axis=-1).astype(t.dtype)


def alibi_slopes():
    n = DIL_GROUPS * DIL_HEADS
    m = 2.0 ** (-8.0 * jnp.arange(1, n + 1, dtype=jnp.float32) / n)
    return m.reshape(DIL_GROUPS, DIL_HEADS)


def causal_block_attention(q, k, v, scale):
    B, H, S, Dk = q.shape
    Dv = v.shape[-1]
    nqb = S // Q_BLOCK
    qb = q.reshape(B, H, nqb, Q_BLOCK, Dk).transpose(2, 0, 1, 3, 4)
    kf = k.astype(jnp.float32)
    vf = v.astype(jnp.float32)
    kpos = jnp.arange(S)

    def one_block(args):
        qblk, i = args
        s = jnp.einsum('bhqc,bhkc->bhqk', qblk.astype(jnp.float32), kf) * scale
        qpos = i * Q_BLOCK + jnp.arange(Q_BLOCK)
        s = jnp.where(kpos[None, :] <= qpos[:, None], s, -jnp.inf)
        p = jax.nn.softmax(s, axis=-1)
        return jnp.einsum('bhqk,bhkc->bhqc', p, vf)

    o = lax.map(one_block, (qb, jnp.arange(nqb)))
    return o.transpose(1, 2, 0, 3, 4).reshape(B, H, S, Dv)


def dilated_group_attention(q, k, v, slopes, dilation, n_back):
    B, S, H, hd = q.shape
    L = S // dilation
    nb = -(-L // Q_BLOCK)
    Lp = nb * Q_BLOCK

    def to_sub(t):
        return t.reshape(B, L, dilation, H, hd).transpose(0, 2, 3, 1, 4)

    qs = jnp.pad(to_sub(q).astype(jnp.float32), ((0, 0), (0, 0), (0, 0), (0, Lp - L), (0, 0)))
    qs = qs.reshape(B, dilation, H, nb, Q_BLOCK, hd)

    def windows(t):
        tp = jnp.pad(to_sub(t).astype(jnp.float32), ((0, 0), (0, 0), (0, 0), (Q_BLOCK, Lp - L), (0, 0)))
        tb = tp.reshape(B, dilation, H, nb + 1, Q_BLOCK, hd)
        return jnp.concatenate([tb[:, :, :, :-1], tb[:, :, :, 1:]], axis=4)

    kw = windows(k)
    vw = windows(v)
    qq = jnp.arange(Q_BLOCK)[:, None]
    kk = jnp.arange(2 * Q_BLOCK)[None, :]
    j = Q_BLOCK + qq - kk
    key_idx = (jnp.arange(nb)[:, None, None] - 1) * Q_BLOCK + kk[None]
    valid = (j >= 0) & (j <= n_back) & (key_idx >= 0)
    dist = (dilation * j).astype(jnp.float32)
    bias = -slopes.reshape(1, 1, H, 1, 1, 1) * dist
    scale = DIL_HEAD_DIM ** -0.5
    s = jnp.einsum('bdhnqc,bdhnkc->bdhnqk', qs, kw) * scale + bias
    s = jnp.where(valid, s, -jnp.inf)
    lse = jax.nn.logsumexp(s, axis=-1)
    p = jnp.exp(s - lse[..., None])
    o = jnp.einsum('bdhnqk,bdhnkc->bdhnqc', p, vw)
    o = o.reshape(B, dilation, H, Lp, hd)[:, :, :, :L]
    lse = lse.reshape(B, dilation, H, Lp)[:, :, :, :L]
    o = o.transpose(0, 3, 1, 2, 4).reshape(B, S, H, hd)
    lse = lse.transpose(0, 3, 1, 2).reshape(B, S, H)
    return o, lse


def hybrid_layer(x, norm_g, w_in, b_gate, conv_w, conv_b, q_a_norm_g, w_uq, kv_a_norm_g, w_ukv,
                 mla_q_norm_g, mla_k_norm_g, dil_q_norm_g, dil_k_norm_g,
                 w_out_a, w_out_b, w_out_c, w_o, cos, sin, slopes):
    B, S, _ = x.shape
    h = rmsnorm(x, norm_g)
    proj = h @ w_in
    (a_b, a_c, a_x, a_z, c_q, c_kv, k_pe, b_z, dq, dk, dv, c_z, gate_pre) = jnp.split(proj, SPLIT_POINTS, axis=-1)

    u = a_c * a_x
    up = jnp.pad(u, ((0, 0), (CONV_K - 1, 0), (0, 0)))
    conv = conv_b
    for tap in range(CONV_K):
        conv = conv + up[:, tap:tap + S] * conv_w[tap]
    y_a = a_b * conv * jax.nn.silu(a_z)

    q = (rmsnorm(c_q, q_a_norm_g) @ w_uq).reshape(B, S, MLA_HEADS, MLA_QK)
    kv = (rmsnorm(c_kv, kv_a_norm_g) @ w_ukv).reshape(B, S, MLA_HEADS, MLA_NOPE + MLA_V)
    k_nope, v = jnp.split(kv, [MLA_NOPE], axis=-1)
    k = jnp.concatenate([k_nope, jnp.broadcast_to(k_pe[:, :, None, :], (B, S, MLA_HEADS, MLA_ROPE))], axis=-1)
    q = rmsnorm(q, mla_q_norm_g)
    k = rmsnorm(k, mla_k_norm_g)
    q = jnp.concatenate([q[..., :MLA_NOPE], apply_rope(q[..., MLA_NOPE:], cos, sin)], axis=-1)
    k = jnp.concatenate([k[..., :MLA_NOPE], apply_rope(k[..., MLA_NOPE:], cos, sin)], axis=-1)
    o_b = causal_block_attention(q.transpose(0, 2, 1, 3), k.transpose(0, 2, 1, 3), v.transpose(0, 2, 1, 3),
                                 MLA_QK ** -0.5)
    o_b = o_b.transpose(0, 2, 1, 3).reshape(B, S, MLA_HEADS * MLA_V).astype(x.dtype)
    y_b = o_b * jax.nn.silu(b_z)

    dq = rmsnorm(dq.reshape(B, S, DIL_GROUPS, DIL_HEADS, DIL_HEAD_DIM), dil_q_norm_g[:, None, :])
    dk = rmsnorm(dk.reshape(B, S, DIL_GROUPS, DIL_HEADS, DIL_HEAD_DIM), dil_k_norm_g[:, None, :])
    dv = dv.reshape(B, S, DIL_GROUPS, DIL_HEADS, DIL_HEAD_DIM)
    outs = []
    lses = []
    for gi, (window, dilation) in enumerate(DIL_PATTERNS):
        o_g, lse_g = dilated_group_attention(dq[:, :, gi], dk[:, :, gi], dv[:, :, gi], slopes[gi],
                                             dilation, window // dilation)
        outs.append(o_g)
        lses.append(lse_g)
    alpha = jax.nn.softmax(jnp.stack(lses, axis=0), axis=0)
    o_c = jnp.sum(alpha[..., None] * jnp.stack(outs, axis=0), axis=0)
    o_c = o_c.reshape(B, S, DIL_WIDTH).astype(x.dtype)
    y_c = o_c * jax.nn.silu(c_z)

    g_a, g_b, g_c = jnp.split(jax.nn.sigmoid(gate_pre + b_gate), N_BRANCH, axis=-1)
    merged = g_a * (y_a @ w_out_a) + g_b * (y_b @ w_out_b) + g_c * (y_c @ w_out_c)
    return x + merged @ w_o


def _fwd_setup_inputs(seed: int = 0) -> dict:
    key = jax.random.key(seed)
    ks = jax.random.split(key, 18)
    f32 = jnp.float32

    def nrm(k, shape, scale):
        return jax.random.normal(k, shape, f32) * scale

    def gain(k, shape):
        return 1.0 + 0.02 * jax.random.normal(k, shape, f32)

    Ld = DEPTH
    return {
        'x': jax.random.normal(ks[0], (BATCH, SEQ, D_MODEL), f32),
        'norm_g': gain(ks[1], (Ld, D_MODEL)),
        'w_in': nrm(ks[2], (Ld, D_MODEL, N_IN), D_MODEL ** -0.5),
        'b_gate': nrm(ks[3], (Ld, N_BRANCH * D_MODEL), 0.1),
        'conv_w': nrm(ks[4], (Ld, CONV_K, CONV_WIDTH), CONV_K ** -0.5),
        'conv_b': nrm(ks[5], (Ld, CONV_WIDTH), 0.02),
        'q_a_norm_g': gain(ks[6], (Ld, MLA_Q_LORA)),
        'w_uq': nrm(ks[7], (Ld, MLA_Q_LORA, MLA_HEADS * MLA_QK), MLA_Q_LORA ** -0.5),
        'kv_a_norm_g': gain(ks[8], (Ld, MLA_KV_LORA)),
        'w_ukv': nrm(ks[9], (Ld, MLA_KV_LORA, MLA_HEADS * (MLA_NOPE + MLA_V)), MLA_KV_LORA ** -0.5),
        'mla_q_norm_g': gain(ks[10], (Ld, MLA_QK)),
        'mla_k_norm_g': gain(ks[11], (Ld, MLA_QK)),
        'dil_q_norm_g': gain(ks[12], (Ld, DIL_GROUPS, DIL_HEAD_DIM)),
        'dil_k_norm_g': gain(ks[13], (Ld, DIL_GROUPS, DIL_HEAD_DIM)),
        'w_out_a': nrm(ks[14], (Ld, CONV_WIDTH, D_MODEL), CONV_WIDTH ** -0.5),
        'w_out_b': nrm(ks[15], (Ld, MLA_HEADS * MLA_V, D_MODEL), (MLA_HEADS * MLA_V) ** -0.5),
        'w_out_c': nrm(ks[16], (Ld, DIL_WIDTH, D_MODEL), DIL_WIDTH ** -0.5),
        'w_o': nrm(ks[17], (Ld, D_MODEL, D_MODEL), D_MODEL ** -0.5),
    }


def _fwd_reference(x, norm_g, w_in, b_gate, conv_w, conv_b, q_a_norm_g, w_uq, kv_a_norm_g, w_ukv,
              mla_q_norm_g, mla_k_norm_g, dil_q_norm_g, dil_k_norm_g, w_out_a, w_out_b, w_out_c, w_o):
    cos, sin = rope_tables(x.shape[1])
    slopes = alibi_slopes()
    for l in range(DEPTH):
        x = hybrid_layer(x, norm_g[l], w_in[l], b_gate[l], conv_w[l], conv_b[l], q_a_norm_g[l], w_uq[l],
                         kv_a_norm_g[l], w_ukv[l], mla_q_norm_g[l], mla_k_norm_g[l], dil_q_norm_g[l],
                         dil_k_norm_g[l], w_out_a[l], w_out_b[l], w_out_c[l], w_o[l], cos, sin, slopes)
    return x


import jax as _jax
import jax.numpy as _jnp

TWIN_FORMAT = 'train_step'
FWD_PARAMS = ['x', 'norm_g', 'w_in', 'b_gate', 'conv_w', 'conv_b', 'q_a_norm_g', 'w_uq', 'kv_a_norm_g', 'w_ukv', 'mla_q_norm_g', 'mla_k_norm_g', 'dil_q_norm_g', 'dil_k_norm_g', 'w_out_a', 'w_out_b', 'w_out_c', 'w_o']
TWIN_WEIGHTS = ['norm_g', 'w_in', 'b_gate', 'conv_w', 'conv_b', 'q_a_norm_g', 'w_uq', 'kv_a_norm_g', 'w_ukv', 'mla_q_norm_g', 'mla_k_norm_g', 'dil_q_norm_g', 'dil_k_norm_g', 'w_out_a', 'w_out_b', 'w_out_c', 'w_o']
TWIN_DIFF_INPUT = 'x'
TWIN_INPUTS = ['x', 'norm_g', 'w_in', 'b_gate', 'conv_w', 'conv_b', 'q_a_norm_g', 'w_uq', 'kv_a_norm_g', 'w_ukv', 'mla_q_norm_g', 'mla_k_norm_g', 'dil_q_norm_g', 'dil_k_norm_g', 'w_out_a', 'w_out_b', 'w_out_c', 'w_o', 'loss_target', 'm_norm_g', 'm_w_in', 'm_b_gate', 'm_conv_w', 'm_conv_b', 'm_q_a_norm_g', 'm_w_uq', 'm_kv_a_norm_g', 'm_w_ukv', 'm_mla_q_norm_g', 'm_mla_k_norm_g', 'm_dil_q_norm_g', 'm_dil_k_norm_g', 'm_w_out_a', 'm_w_out_b', 'm_w_out_c', 'm_w_o', 'v_norm_g', 'v_w_in', 'v_b_gate', 'v_conv_w', 'v_conv_b', 'v_q_a_norm_g', 'v_w_uq', 'v_kv_a_norm_g', 'v_w_ukv', 'v_mla_q_norm_g', 'v_mla_k_norm_g', 'v_dil_q_norm_g', 'v_dil_k_norm_g', 'v_w_out_a', 'v_w_out_b', 'v_w_out_c', 'v_w_o']
TWIN_OUTPUTS = ['loss', 'grad_x', 'grad_norm_g', 'grad_w_in', 'grad_b_gate', 'grad_conv_w', 'grad_conv_b', 'grad_q_a_norm_g', 'grad_w_uq', 'grad_kv_a_norm_g', 'grad_w_ukv', 'grad_mla_q_norm_g', 'grad_mla_k_norm_g', 'grad_dil_q_norm_g', 'grad_dil_k_norm_g', 'grad_w_out_a', 'grad_w_out_b', 'grad_w_out_c', 'grad_w_o', 'delta_norm_g', 'delta_w_in', 'delta_b_gate', 'delta_conv_w', 'delta_conv_b', 'delta_q_a_norm_g', 'delta_w_uq', 'delta_kv_a_norm_g', 'delta_w_ukv', 'delta_mla_q_norm_g', 'delta_mla_k_norm_g', 'delta_dil_q_norm_g', 'delta_dil_k_norm_g', 'delta_w_out_a', 'delta_w_out_b', 'delta_w_out_c', 'delta_w_o', 'new_m_norm_g', 'new_m_w_in', 'new_m_b_gate', 'new_m_conv_w', 'new_m_conv_b', 'new_m_q_a_norm_g', 'new_m_w_uq', 'new_m_kv_a_norm_g', 'new_m_w_ukv', 'new_m_mla_q_norm_g', 'new_m_mla_k_norm_g', 'new_m_dil_q_norm_g', 'new_m_dil_k_norm_g', 'new_m_w_out_a', 'new_m_w_out_b', 'new_m_w_out_c', 'new_m_w_o', 'new_v_norm_g', 'new_v_w_in', 'new_v_b_gate', 'new_v_conv_w', 'new_v_conv_b', 'new_v_q_a_norm_g', 'new_v_w_uq', 'new_v_kv_a_norm_g', 'new_v_w_ukv', 'new_v_mla_q_norm_g', 'new_v_mla_k_norm_g', 'new_v_dil_q_norm_g', 'new_v_dil_k_norm_g', 'new_v_w_out_a', 'new_v_w_out_b', 'new_v_w_out_c', 'new_v_w_o']
TWIN_LEAF_KINDS = {'loss': 'loss', 'grad_x': 'grad_x', 'grad_norm_g': 'grad_w', 'grad_w_in': 'grad_w', 'grad_b_gate': 'grad_w', 'grad_conv_w': 'grad_w', 'grad_conv_b': 'grad_w', 'grad_q_a_norm_g': 'grad_w', 'grad_w_uq': 'grad_w', 'grad_kv_a_norm_g': 'grad_w', 'grad_w_ukv': 'grad_w', 'grad_mla_q_norm_g': 'grad_w', 'grad_mla_k_norm_g': 'grad_w', 'grad_dil_q_norm_g': 'grad_w', 'grad_dil_k_norm_g': 'grad_w', 'grad_w_out_a': 'grad_w', 'grad_w_out_b': 'grad_w', 'grad_w_out_c': 'grad_w', 'grad_w_o': 'grad_w', 'delta_norm_g': 'delta_w', 'delta_w_in': 'delta_w', 'delta_b_gate': 'delta_w', 'delta_conv_w': 'delta_w', 'delta_conv_b': 'delta_w', 'delta_q_a_norm_g': 'delta_w', 'delta_w_uq': 'delta_w', 'delta_kv_a_norm_g': 'delta_w', 'delta_w_ukv': 'delta_w', 'delta_mla_q_norm_g': 'delta_w', 'delta_mla_k_norm_g': 'delta_w', 'delta_dil_q_norm_g': 'delta_w', 'delta_dil_k_norm_g': 'delta_w', 'delta_w_out_a': 'delta_w', 'delta_w_out_b': 'delta_w', 'delta_w_out_c': 'delta_w', 'delta_w_o': 'delta_w', 'new_m_norm_g': 'new_m', 'new_m_w_in': 'new_m', 'new_m_b_gate': 'new_m', 'new_m_conv_w': 'new_m', 'new_m_conv_b': 'new_m', 'new_m_q_a_norm_g': 'new_m', 'new_m_w_uq': 'new_m', 'new_m_kv_a_norm_g': 'new_m', 'new_m_w_ukv': 'new_m', 'new_m_mla_q_norm_g': 'new_m', 'new_m_mla_k_norm_g': 'new_m', 'new_m_dil_q_norm_g': 'new_m', 'new_m_dil_k_norm_g': 'new_m', 'new_m_w_out_a': 'new_m', 'new_m_w_out_b': 'new_m', 'new_m_w_out_c': 'new_m', 'new_m_w_o': 'new_m', 'new_v_norm_g': 'new_v', 'new_v_w_in': 'new_v', 'new_v_b_gate': 'new_v', 'new_v_conv_w': 'new_v', 'new_v_conv_b': 'new_v', 'new_v_q_a_norm_g': 'new_v', 'new_v_w_uq': 'new_v', 'new_v_kv_a_norm_g': 'new_v', 'new_v_w_ukv': 'new_v', 'new_v_mla_q_norm_g': 'new_v', 'new_v_mla_k_norm_g': 'new_v', 'new_v_dil_q_norm_g': 'new_v', 'new_v_dil_k_norm_g': 'new_v', 'new_v_w_out_a': 'new_v', 'new_v_w_out_b': 'new_v', 'new_v_w_out_c': 'new_v', 'new_v_w_o': 'new_v'}


def _forward(args):
    return _fwd_reference(*[args[k] for k in FWD_PARAMS])


def _output_shape():
    out = _jax.eval_shape(lambda: _forward(_fwd_setup_inputs(0)))
    return out.shape, out.dtype

N_MICROBATCH = 1
ADAM_LR = 0.001
ADAM_B1 = 0.9
ADAM_B2 = 0.999
ADAM_EPS = 1e-08
ADAM_WD = 0.01
ADAM_STEP = 10
PER_EXAMPLE_BATCH_AXIS = {'x': 0, 'loss_target': 0}
SHARED_INPUTS = []
_WEIGHT_DTYPES = {'norm_g': _jnp.float32, 'w_in': _jnp.float32, 'b_gate': _jnp.float32, 'conv_w': _jnp.float32, 'conv_b': _jnp.float32, 'q_a_norm_g': _jnp.float32, 'w_uq': _jnp.float32, 'kv_a_norm_g': _jnp.float32, 'w_ukv': _jnp.float32, 'mla_q_norm_g': _jnp.float32, 'mla_k_norm_g': _jnp.float32, 'dil_q_norm_g': _jnp.float32, 'dil_k_norm_g': _jnp.float32, 'w_out_a': _jnp.float32, 'w_out_b': _jnp.float32, 'w_out_c': _jnp.float32, 'w_o': _jnp.float32}
MOMENT_SCALE = {'norm_g': 1.472056e+01, 'w_in': 1.785351e-01, 'b_gate': 6.654887e-01, 'conv_w': 3.566083e+00, 'conv_b': 2.824124e-01, 'q_a_norm_g': 6.160354e-02, 'w_uq': 3.323323e-02, 'kv_a_norm_g': 2.651518e-01, 'w_ukv': 4.257573e-02, 'mla_q_norm_g': 2.054498e-01, 'mla_k_norm_g': 2.052218e-01, 'dil_q_norm_g': 6.928381e-01, 'dil_k_norm_g': 6.940953e-01, 'w_out_a': 2.040332e-01, 'w_out_b': 3.405340e-02, 'w_out_c': 4.905363e-02, 'w_o': 1.798336e-01}


def _to_microbatches(a, axis):
    t = _jnp.moveaxis(a, axis, 0)
    t = t.reshape((N_MICROBATCH, t.shape[0] // N_MICROBATCH) + t.shape[1:])
    return _jnp.moveaxis(t, 1, axis + 1)


def setup_inputs(seed: int = 0) -> dict:
    inp = _fwd_setup_inputs(seed)
    key = _jax.random.fold_in(_jax.random.key(seed), 7919)
    shape, _ = _output_shape()
    out = dict(inp)
    out["loss_target"] = _jax.random.normal(_jax.random.fold_in(key, 0), shape, _jnp.float32)
    for i, name in enumerate(TWIN_WEIGHTS):
        w = inp[name].astype(_jnp.float32)
        if MOMENT_SCALE is None:
            s = _jnp.sqrt(_jnp.mean(_jnp.square(w)) + 1e-30)
        else:
            s = MOMENT_SCALE[name]
        km, kv = _jax.random.split(_jax.random.fold_in(key, i + 1))
        out[name] = w
        out["m_" + name] = s * _jax.random.normal(km, w.shape, _jnp.float32)
        out["v_" + name] = (s * s) * _jax.random.uniform(kv, w.shape, _jnp.float32, 0.5, 1.5)
    if N_MICROBATCH > 1:
        for name, axis in PER_EXAMPLE_BATCH_AXIS.items():
            out[name] = _to_microbatches(out[name], axis)
    return {'x': out['x'], 'norm_g': out['norm_g'], 'w_in': out['w_in'], 'b_gate': out['b_gate'], 'conv_w': out['conv_w'], 'conv_b': out['conv_b'], 'q_a_norm_g': out['q_a_norm_g'], 'w_uq': out['w_uq'], 'kv_a_norm_g': out['kv_a_norm_g'], 'w_ukv': out['w_ukv'], 'mla_q_norm_g': out['mla_q_norm_g'], 'mla_k_norm_g': out['mla_k_norm_g'], 'dil_q_norm_g': out['dil_q_norm_g'], 'dil_k_norm_g': out['dil_k_norm_g'], 'w_out_a': out['w_out_a'], 'w_out_b': out['w_out_b'], 'w_out_c': out['w_out_c'], 'w_o': out['w_o'], 'loss_target': out['loss_target'], 'm_norm_g': out['m_norm_g'], 'm_w_in': out['m_w_in'], 'm_b_gate': out['m_b_gate'], 'm_conv_w': out['m_conv_w'], 'm_conv_b': out['m_conv_b'], 'm_q_a_norm_g': out['m_q_a_norm_g'], 'm_w_uq': out['m_w_uq'], 'm_kv_a_norm_g': out['m_kv_a_norm_g'], 'm_w_ukv': out['m_w_ukv'], 'm_mla_q_norm_g': out['m_mla_q_norm_g'], 'm_mla_k_norm_g': out['m_mla_k_norm_g'], 'm_dil_q_norm_g': out['m_dil_q_norm_g'], 'm_dil_k_norm_g': out['m_dil_k_norm_g'], 'm_w_out_a': out['m_w_out_a'], 'm_w_out_b': out['m_w_out_b'], 'm_w_out_c': out['m_w_out_c'], 'm_w_o': out['m_w_o'], 'v_norm_g': out['v_norm_g'], 'v_w_in': out['v_w_in'], 'v_b_gate': out['v_b_gate'], 'v_conv_w': out['v_conv_w'], 'v_conv_b': out['v_conv_b'], 'v_q_a_norm_g': out['v_q_a_norm_g'], 'v_w_uq': out['v_w_uq'], 'v_kv_a_norm_g': out['v_kv_a_norm_g'], 'v_w_ukv': out['v_w_ukv'], 'v_mla_q_norm_g': out['v_mla_q_norm_g'], 'v_mla_k_norm_g': out['v_mla_k_norm_g'], 'v_dil_q_norm_g': out['v_dil_q_norm_g'], 'v_dil_k_norm_g': out['v_dil_k_norm_g'], 'v_w_out_a': out['v_w_out_a'], 'v_w_out_b': out['v_w_out_b'], 'v_w_out_c': out['v_w_out_c'], 'v_w_o': out['v_w_o']}


def _loss(weights, diff, rest, loss_target):
    with _jax.named_scope("forward"):
        args = {**rest, TWIN_DIFF_INPUT: diff, **{k: w.astype(_WEIGHT_DTYPES[k]) for k, w in weights.items()}}
        y = _forward(args)
    with _jax.named_scope("loss_head"):
        err = _jnp.square(y.astype(_jnp.float32) - loss_target)
        return 0.5 * _jnp.sum(_jnp.mean(err, axis=-1)) if err.ndim else 0.5 * err


def _adamw(w, g, m, v):
    m = ADAM_B1 * m + (1.0 - ADAM_B1) * g
    v = ADAM_B2 * v + (1.0 - ADAM_B2) * _jnp.square(g)
    m_hat = m / (1.0 - ADAM_B1 ** ADAM_STEP)
    v_hat = v / (1.0 - ADAM_B2 ** ADAM_STEP)
    delta = -ADAM_LR * (m_hat / (_jnp.sqrt(v_hat) + ADAM_EPS) + ADAM_WD * w)
    return delta, m, v


def reference(x, norm_g, w_in, b_gate, conv_w, conv_b, q_a_norm_g, w_uq, kv_a_norm_g, w_ukv, mla_q_norm_g, mla_k_norm_g, dil_q_norm_g, dil_k_norm_g, w_out_a, w_out_b, w_out_c, w_o, loss_target, m_norm_g, m_w_in, m_b_gate, m_conv_w, m_conv_b, m_q_a_norm_g, m_w_uq, m_kv_a_norm_g, m_w_ukv, m_mla_q_norm_g, m_mla_k_norm_g, m_dil_q_norm_g, m_dil_k_norm_g, m_w_out_a, m_w_out_b, m_w_out_c, m_w_o, v_norm_g, v_w_in, v_b_gate, v_conv_w, v_conv_b, v_q_a_norm_g, v_w_uq, v_kv_a_norm_g, v_w_ukv, v_mla_q_norm_g, v_mla_k_norm_g, v_dil_q_norm_g, v_dil_k_norm_g, v_w_out_a, v_w_out_b, v_w_out_c, v_w_o):
    given = dict(x=x, norm_g=norm_g, w_in=w_in, b_gate=b_gate, conv_w=conv_w, conv_b=conv_b, q_a_norm_g=q_a_norm_g, w_uq=w_uq, kv_a_norm_g=kv_a_norm_g, w_ukv=w_ukv, mla_q_norm_g=mla_q_norm_g, mla_k_norm_g=mla_k_norm_g, dil_q_norm_g=dil_q_norm_g, dil_k_norm_g=dil_k_norm_g, w_out_a=w_out_a, w_out_b=w_out_b, w_out_c=w_out_c, w_o=w_o, loss_target=loss_target, m_norm_g=m_norm_g, m_w_in=m_w_in, m_b_gate=m_b_gate, m_conv_w=m_conv_w, m_conv_b=m_conv_b, m_q_a_norm_g=m_q_a_norm_g, m_w_uq=m_w_uq, m_kv_a_norm_g=m_kv_a_norm_g, m_w_ukv=m_w_ukv, m_mla_q_norm_g=m_mla_q_norm_g, m_mla_k_norm_g=m_mla_k_norm_g, m_dil_q_norm_g=m_dil_q_norm_g, m_dil_k_norm_g=m_dil_k_norm_g, m_w_out_a=m_w_out_a, m_w_out_b=m_w_out_b, m_w_out_c=m_w_out_c, m_w_o=m_w_o, v_norm_g=v_norm_g, v_w_in=v_w_in, v_b_gate=v_b_gate, v_conv_w=v_conv_w, v_conv_b=v_conv_b, v_q_a_norm_g=v_q_a_norm_g, v_w_uq=v_w_uq, v_kv_a_norm_g=v_kv_a_norm_g, v_w_ukv=v_w_ukv, v_mla_q_norm_g=v_mla_q_norm_g, v_mla_k_norm_g=v_mla_k_norm_g, v_dil_q_norm_g=v_dil_q_norm_g, v_dil_k_norm_g=v_dil_k_norm_g, v_w_out_a=v_w_out_a, v_w_out_b=v_w_out_b, v_w_out_c=v_w_out_c, v_w_o=v_w_o)
    weights = {n: given[n] for n in TWIN_WEIGHTS}
    shared = {n: given[n] for n in SHARED_INPUTS}
    per_example = {n: given[n] for n in ['x']}
    grad_fn = _jax.value_and_grad(_loss, argnums=(0, 1))

    def one_microbatch(ex, loss_target):
        ex = dict(ex)
        diff = ex.pop(TWIN_DIFF_INPUT)
        return grad_fn(weights, diff, {**shared, **ex}, loss_target)

    if N_MICROBATCH == 1:
        loss, (grad_w, grad_x) = one_microbatch(per_example, given["loss_target"])
    else:
        def body(carry, xs):
            loss_sum, grad_sum = carry
            l_k, (gw_k, gx_k) = one_microbatch(xs[0], xs[1])
            with _jax.named_scope("update"):
                return (loss_sum + l_k, _jax.tree.map(_jnp.add, grad_sum, gw_k)), gx_k

        init = (_jnp.zeros((), _jnp.float32), _jax.tree.map(_jnp.zeros_like, weights))
        (loss, grad_w), grad_x = _jax.lax.scan(body, init, (per_example, given["loss_target"]))
    with _jax.named_scope("update"):
        delta_w, new_m, new_v = {}, {}, {}
        for n in TWIN_WEIGHTS:
            delta_w[n], new_m[n], new_v[n] = _adamw(weights[n], grad_w[n], given["m_" + n], given["v_" + n])
    return (loss, grad_x, *[grad_w[n] for n in TWIN_WEIGHTS], *[delta_w[n] for n in TWIN_WEIGHTS],
            *[new_m[n] for n in TWIN_WEIGHTS], *[new_v[n] for n in TWIN_WEIGHTS])
```

```python
import jax
import jax.numpy as jnp
from jax import lax
from jax.experimental import pallas as pl
from jax.experimental.pallas import tpu as pltpu

F32 = jnp.float32
BF16 = jnp.bfloat16

D_MODEL = 1024
DEPTH = 2
CONV_WIDTH = 512
CONV_K = 3
MLA_HEADS = 8
MLA_Q_LORA = 256
MLA_KV_LORA = 128
MLA_NOPE = 64
MLA_ROPE = 32
MLA_V = 64
MLA_QK = MLA_NOPE + MLA_ROPE
ROPE_THETA = 10000.0
DIL_PATTERNS = ((128, 1), (512, 4), (2048, 16))
DIL_GROUPS = 3
DIL_HEADS = 8
DIL_HEAD_DIM = 64
DIL_WIDTH = DIL_HEADS * DIL_HEAD_DIM
DIL_QK = DIL_GROUPS * DIL_WIDTH
EPS = 1e-6
N_IN = 11168

ADAM_LR = 0.001
ADAM_B1 = 0.9
ADAM_B2 = 0.999
ADAM_EPS = 1e-08
ADAM_WD = 0.01
ADAM_STEP = 10

N_DEV = 8
AXES = ("x", "y", "c")
LANE = 128
HALF = 64
NPAIR = 4

CB_BZ, CB_CZ, CB_GATE = 0, 4, 8
CB_A = 32
CB_CQ, CB_CKV, CB_KPE = 48, 50, 51
CB_QK = 52
CB_DV = 76
NCB = 88
PP = NCB * LANE
NEG = -1e30
VMEM_LIMIT = 56 * 1024 * 1024


def _column_chunks():
    out = []
    col = 0

    def seg(nblocks, block_of):
        nonlocal col
        for i in range(nblocks):
            out.append((col, LANE, block_of(i)))
            col += LANE

    seg(4, lambda j: CB_A + 4 * j)
    seg(4, lambda j: CB_A + 4 * j + 1)
    seg(4, lambda j: CB_A + 4 * j + 2)
    seg(4, lambda j: CB_A + 4 * j + 3)
    seg(2, lambda i: CB_CQ + i)
    seg(1, lambda i: CB_CKV)
    out.append((col, MLA_ROPE, CB_KPE))
    col += MLA_ROPE
    seg(4, lambda j: CB_BZ + j)
    seg(12, lambda c: CB_QK + 2 * c)
    seg(12, lambda c: CB_QK + 2 * c + 1)
    seg(12, lambda c: CB_DV + c)
    seg(4, lambda j: CB_CZ + j)
    seg(24, lambda i: CB_GATE + i)
    assert col == N_IN and sorted(c[2] for c in out) == list(range(NCB))
    return out


COLUMN_CHUNKS = _column_chunks()


def _pad_columns(w):
    parts = []
    for start, width, _ in sorted(COLUMN_CHUNKS, key=lambda c: c[2]):
        parts.append(w[:, start:start + width])
        if width < LANE:
            parts.append(jnp.zeros((w.shape[0], LANE - width), w.dtype))
    return jnp.concatenate(parts, axis=1)


def _unpad_columns(wp):
    return jnp.concatenate([wp[:, b * LANE:b * LANE + width] for _, width, b in COLUMN_CHUNKS], axis=1)


def _cp():
    return pltpu.CompilerParams(vmem_limit_bytes=VMEM_LIMIT)


def _rstd(x, n):
    return lax.rsqrt(jnp.sum(x * x, axis=-1, keepdims=True) * (1.0 / n) + EPS)


def _sigmoid(z):
    return 1.0 / (1.0 + jnp.exp(-z))


def _silu(z):
    return z * _sigmoid(z)


def _dsilu(z):
    s = _sigmoid(z)
    return s * (1.0 + z * (1.0 - s))


def _mm(a, b):
    return jnp.dot(a.astype(BF16), b.astype(BF16), preferred_element_type=F32)


def _mm_nt(a, b):
    return lax.dot_general(a.astype(BF16), b.astype(BF16), (((1,), (1,)), ((), ())), preferred_element_type=F32)


def _mm_tn(a, b):
    return lax.dot_general(a.astype(BF16), b.astype(BF16), (((0,), (0,)), ((), ())), preferred_element_type=F32)


def _lane_lo(shape):
    return lax.broadcasted_iota(jnp.int32, shape, len(shape) - 1) < HALF


def _head_bcast_sum(x):
    lo = _lane_lo(x.shape)
    a = jnp.sum(jnp.where(lo, x, 0.0), axis=-1, keepdims=True)
    b = jnp.sum(jnp.where(lo, 0.0, x), axis=-1, keepdims=True)
    return jnp.where(lo, a, b)


def _rope(t, cos, sa, sb):
    return t * cos + pltpu.roll(t, LANE - 16, axis=1) * sa + pltpu.roll(t, 16, axis=1) * sb


def _rope_t(d, cos, sa, sb):
    return d * cos + pltpu.roll(d * sa, 16, axis=1) + pltpu.roll(d * sb, LANE - 16, axis=1)


def _shift_down(u, k):
    rows = lax.broadcasted_iota(jnp.int32, u.shape, 0)
    return jnp.where(rows >= k, pltpu.roll(u, k, axis=0), 0.0)


def _shift_up(u, k):
    n = u.shape[0]
    rows = lax.broadcasted_iota(jnp.int32, u.shape, 0)
    return jnp.where(rows < n - k, pltpu.roll(u, n - k, axis=0), 0.0)


def _tile(n, want):
    t = min(n, want)
    assert n % t == 0, (n, want)
    return t


def _inproj_fwd(x, g, wp):
    T = x.shape[0]
    tm, tn = _tile(T, 512), 512

    def body(x_ref, g_ref, w_ref, proj_ref, h_ref):
        @pl.when(pl.program_id(1) == 0)
        def _():
            xv = x_ref[...]
            h_ref[...] = (xv * _rstd(xv, D_MODEL) * g_ref[...]).astype(BF16)

        proj_ref[...] = jnp.dot(h_ref[...], w_ref[...], preferred_element_type=F32)

    return pl.pallas_call(
        body, name="inproj_fwd", grid=(T // tm, PP // tn),
        in_specs=[pl.BlockSpec((tm, D_MODEL), lambda i, j: (i, 0)),
                  pl.BlockSpec((1, D_MODEL), lambda i, j: (0, 0)),
                  pl.BlockSpec((D_MODEL, tn), lambda i, j: (0, j))],
        out_specs=[pl.BlockSpec((tm, tn), lambda i, j: (i, j)),
                   pl.BlockSpec((tm, D_MODEL), lambda i, j: (i, 0))],
        out_shape=[jax.ShapeDtypeStruct((T, PP), F32), jax.ShapeDtypeStruct((T, D_MODEL), BF16)],
        compiler_params=_cp(),
    )(x, g, wp)


def _matmul_tn(a, b, name):
    T, K = a.shape
    N = b.shape[1]
    tt, tn = _tile(T, 512), _tile(N, 512)

    def body(a_ref, b_ref, o_ref):
        @pl.when(pl.program_id(1) == 0)
        def _():
            o_ref[...] = jnp.zeros_like(o_ref)

        o_ref[...] += _mm_tn(a_ref[...], b_ref[...])

    return pl.pallas_call(
        body, name=name, grid=(N // tn, T // tt),
        in_specs=[pl.BlockSpec((tt, K), lambda j, k: (k, 0)),
                  pl.BlockSpec((tt, tn), lambda j, k: (k, j))],
        out_specs=pl.BlockSpec((K, tn), lambda j, k: (0, j)),
        out_shape=jax.ShapeDtypeStruct((K, N), F32),
        compiler_params=_cp(),
    )(a, b)


def _inproj_bwd_x(dproj, wp, x, g, dout):
    T = x.shape[0]
    tm, tk = _tile(T, 512), 512
    nk = PP // tk

    def body(dp_ref, w_ref, x_ref, g_ref, do_ref, dx_ref, dg_ref, acc_ref):
        i, k = pl.program_id(0), pl.program_id(1)

        @pl.when(k == 0)
        def _():
            acc_ref[...] = jnp.zeros_like(acc_ref)

        @pl.when((k == 0) & (i == 0))
        def _():
            dg_ref[...] = jnp.zeros_like(dg_ref)

        acc_ref[...] += _mm_nt(dp_ref[...], w_ref[...])

        @pl.when(k == nk - 1)
        def _():
            dh = acc_ref[...]
            xv = x_ref[...]
            r = _rstd(xv, D_MODEL)
            gy = dh * g_ref[...]
            dot = jnp.sum(xv * gy, axis=-1, keepdims=True) * (1.0 / D_MODEL)
            dx_ref[...] = do_ref[...] + r * gy - xv * (r * r * r) * dot
            dg_ref[...] += jnp.sum(dh * xv * r, axis=0, keepdims=True)

    return pl.pallas_call(
        body, name="inproj_bwd_x", grid=(T // tm, nk),
        in_specs=[pl.BlockSpec((tm, tk), lambda i, k: (i, k)),
                  pl.BlockSpec((D_MODEL, tk), lambda i, k: (0, k)),
                  pl.BlockSpec((tm, D_MODEL), lambda i, k: (i, 0)),
                  pl.BlockSpec((1, D_MODEL), lambda i, k: (0, 0)),
                  pl.BlockSpec((tm, D_MODEL), lambda i, k: (i, 0))],
        out_specs=[pl.BlockSpec((tm, D_MODEL), lambda i, k: (i, 0)),
                   pl.BlockSpec((1, D_MODEL), lambda i, k: (0, 0))],
        out_shape=[jax.ShapeDtypeStruct((T, D_MODEL), F32), jax.ShapeDtypeStruct((1, D_MODEL), F32)],
        scratch_shapes=[pltpu.VMEM((tm, D_MODEL), F32)],
        compiler_params=_cp(),
    )(dproj, wp, x, g, dout)


def _mixa_fwd(proj, cw, cb, B, S):
    nc = CONV_WIDTH // LANE
    ca = CB_A // 4

    def body(p_ref, cw_ref, cb_ref, y_ref):
        ab, ac, ax, az = (p_ref[:, i * LANE:(i + 1) * LANE] for i in range(4))
        u = ac * ax
        conv = cb_ref[...] + cw_ref[0:1, :] * _shift_down(u, 2) + cw_ref[1:2, :] * _shift_down(u, 1) + cw_ref[2:3, :] * u
        y_ref[...] = (ab * conv * _silu(az)).astype(BF16)

    return pl.pallas_call(
        body, name="mixa_fwd", grid=(B, nc),
        in_specs=[pl.BlockSpec((S, 4 * LANE), lambda b, j: (b, ca + j)),
                  pl.BlockSpec((CONV_K, LANE), lambda b, j: (0, j)),
                  pl.BlockSpec((1, LANE), lambda b, j: (0, j))],
        out_specs=pl.BlockSpec((S, LANE), lambda b, j: (b, j)),
        out_shape=jax.ShapeDtypeStruct((B * S, CONV_WIDTH), BF16),
        compiler_params=_cp(),
    )(proj, cw, cb)


def _mixa_bwd(dproj, dy, proj, cw, cb, B, S):
    nc = CONV_WIDTH // LANE
    ca = CB_A // 4

    def body(dpin_ref, dy_ref, p_ref, cw_ref, cb_ref, dp_ref, st_ref):
        del dpin_ref
        ab, ac, ax, az = (p_ref[:, i * LANE:(i + 1) * LANE] for i in range(4))
        u = ac * ax
        u1, u2 = _shift_down(u, 1), _shift_down(u, 2)
        w0, w1, w2 = cw_ref[0:1, :], cw_ref[1:2, :], cw_ref[2:3, :]
        conv = cb_ref[...] + w0 * u2 + w1 * u1 + w2 * u
        s = _silu(az)
        d = dy_ref[...]
        dconv = d * ab * s
        du = w2 * dconv + w1 * _shift_up(dconv, 1) + w0 * _shift_up(dconv, 2)
        dp_ref[:, 0:LANE] = (d * conv * s).astype(BF16)
        dp_ref[:, LANE:2 * LANE] = (du * ax).astype(BF16)
        dp_ref[:, 2 * LANE:3 * LANE] = (du * ac).astype(BF16)
        dp_ref[:, 3 * LANE:4 * LANE] = (d * ab * conv * _dsilu(az)).astype(BF16)
        row = lax.broadcasted_iota(jnp.int32, (8, LANE), 0)
        st = jnp.zeros((8, LANE), F32)
        for r, v in enumerate((dconv * u2, dconv * u1, dconv * u, dconv)):
            st = st + jnp.where(row == r, jnp.sum(v, axis=0, keepdims=True), 0.0)

        @pl.when(pl.program_id(1) == 0)
        def _():
            st_ref[...] = st

        @pl.when(pl.program_id(1) != 0)
        def _():
            st_ref[...] += st

    return pl.pallas_call(
        body, name="mixa_bwd", grid=(nc, B),
        in_specs=[pl.BlockSpec(memory_space=pl.ANY),
                  pl.BlockSpec((S, LANE), lambda j, b: (b, j)),
                  pl.BlockSpec((S, 4 * LANE), lambda j, b: (b, ca + j)),
                  pl.BlockSpec((CONV_K, LANE), lambda j, b: (0, j)),
                  pl.BlockSpec((1, LANE), lambda j, b: (0, j))],
        out_specs=[pl.BlockSpec((S, 4 * LANE), lambda j, b: (b, ca + j)),
                   pl.BlockSpec((8, LANE), lambda j, b: (0, j))],
        out_shape=[jax.ShapeDtypeStruct(dproj.shape, BF16), jax.ShapeDtypeStruct((8, CONV_WIDTH), F32)],
        input_output_aliases={0: 0},
        compiler_params=_cp(),
    )(dproj, dy, proj, cw, cb)


def _mla_prep_fwd(proj, gq, gkv, wuqp, wkp, wv, gmq, gmk, cos, sa, sb, S):
    T = proj.shape[0]
    ts = _tile(S, 512)
    ns = S // ts
    W = MLA_HEADS * LANE

    def body(p_ref, gq_ref, gkv_ref, wuq_ref, wk_ref, wv_ref, gmq_ref, gmk_ref, cos_ref, sa_ref, sb_ref,
             q_ref, k_ref, v_ref):
        cq = p_ref[:, 0:2 * LANE]
        ckv = p_ref[:, 2 * LANE:3 * LANE]
        kpe = pltpu.roll(p_ref[:, 3 * LANE:4 * LANE], HALF, axis=1)
        cqn = cq * _rstd(cq, MLA_Q_LORA) * gq_ref[...]
        ckn = (ckv * _rstd(ckv, MLA_KV_LORA) * gkv_ref[...]).astype(BF16)
        q0 = _mm(cqn, wuq_ref[...])
        kn = _mm(ckn, wk_ref[...])
        v_ref[...] = _mm(ckn, wv_ref[...]).astype(BF16)
        c, a, b = cos_ref[...], sa_ref[...], sb_ref[...]
        for h in range(MLA_HEADS):
            q0h = q0[:, h * LANE:(h + 1) * LANE]
            q_ref[h] = _rope(q0h * _rstd(q0h, MLA_QK) * gmq_ref[...], c, a, b).astype(BF16)
            k0h = kn[:, h * LANE:(h + 1) * LANE] + kpe
            k_ref[h] = _rope(k0h * _rstd(k0h, MLA_QK) * gmk_ref[...], c, a, b).astype(BF16)

    def whole(r, c):
        return pl.BlockSpec((r, c), lambda i: (0, 0))

    tab = pl.BlockSpec((ts, LANE), lambda i: (i % ns, 0))
    return pl.pallas_call(
        body, name="mla_prep_fwd", grid=(T // ts,),
        in_specs=[pl.BlockSpec((ts, 4 * LANE), lambda i: (i, CB_CQ // 4)),
                  whole(1, MLA_Q_LORA), whole(1, MLA_KV_LORA), whole(MLA_Q_LORA, W), whole(MLA_KV_LORA, W),
                  whole(MLA_KV_LORA, MLA_HEADS * MLA_V), whole(1, LANE), whole(1, LANE), tab, tab, tab],
        out_specs=[pl.BlockSpec((MLA_HEADS, ts, LANE), lambda i: (0, i, 0)),
                   pl.BlockSpec((MLA_HEADS, ts, LANE), lambda i: (0, i, 0)),
                   pl.BlockSpec((ts, MLA_HEADS * MLA_V), lambda i: (i, 0))],
        out_shape=[jax.ShapeDtypeStruct((MLA_HEADS, T, LANE), BF16), jax.ShapeDtypeStruct((MLA_HEADS, T, LANE), BF16),
                   jax.ShapeDtypeStruct((T, MLA_HEADS * MLA_V), BF16)],
        compiler_params=_cp(),
    )(proj, gq, gkv, wuqp, wkp, wv, gmq, gmk, cos, sa, sb)


def _mla_prep_bwd(dproj, dq, dk, dv, proj, gq, gkv, wuqp, wkp, wv, gmq, gmk, cos, sa, sb, S):
    T = proj.shape[0]
    ts = _tile(S, 256)
    ns = S // ts
    W = MLA_HEADS * LANE

    def body(dpin_ref, dq_ref, dk_ref, dv_ref, p_ref, gq_ref, gkv_ref, wuq_ref, wk_ref, wv_ref, gmq_ref, gmk_ref,
             cos_ref, sa_ref, sb_ref,
             dp_ref, dwuq_ref, dwk_ref, dwv_ref, dgq_ref, dgkv_ref, dgmq_ref, dgmk_ref, dq0_ref, dkn_ref):
        del dpin_ref

        @pl.when(pl.program_id(0) == 0)
        def _():
            for r in (dwuq_ref, dwk_ref, dwv_ref, dgq_ref, dgkv_ref, dgmq_ref, dgmk_ref):
                r[...] = jnp.zeros_like(r)

        cq = p_ref[:, 0:2 * LANE]
        ckv = p_ref[:, 2 * LANE:3 * LANE]
        kpe = pltpu.roll(p_ref[:, 3 * LANE:4 * LANE], HALF, axis=1)
        rq = _rstd(cq, MLA_Q_LORA)
        rkv = _rstd(ckv, MLA_KV_LORA)
        gq, gkv, gmq, gmk = gq_ref[...], gkv_ref[...], gmq_ref[...], gmk_ref[...]
        cqn = (cq * rq * gq).astype(BF16)
        ckn = (ckv * rkv * gkv).astype(BF16)
        q0 = _mm(cqn, wuq_ref[...])
        kn = _mm(ckn, wk_ref[...])
        c, a, b = cos_ref[...], sa_ref[...], sb_ref[...]
        lane = lax.broadcasted_iota(jnp.int32, (ts, LANE), 1)
        dgmq = jnp.zeros((1, LANE), F32)
        dgmk = jnp.zeros((1, LANE), F32)
        dkpe = jnp.zeros((ts, LANE), F32)
        for h in range(MLA_HEADS):
            q0h = q0[:, h * LANE:(h + 1) * LANE]
            r = _rstd(q0h, MLA_QK)
            d1 = _rope_t(dq_ref[h], c, a, b)
            gy = d1 * gmq
            dq0_ref[:, h * LANE:(h + 1) * LANE] = (
                r * gy - q0h * (r * r * r) * (jnp.sum(q0h * gy, axis=-1, keepdims=True) * (1.0 / MLA_QK))).astype(BF16)
            dgmq = dgmq + jnp.sum(d1 * q0h * r, axis=0, keepdims=True)
            k0h = kn[:, h * LANE:(h + 1) * LANE] + kpe
            r = _rstd(k0h, MLA_QK)
            d1 = _rope_t(dk_ref[h], c, a, b)
            gy = d1 * gmk
            dk0 = r * gy - k0h * (r * r * r) * (jnp.sum(k0h * gy, axis=-1, keepdims=True) * (1.0 / MLA_QK))
            dgmk = dgmk + jnp.sum(d1 * k0h * r, axis=0, keepdims=True)
            dkn_ref[:, h * LANE:(h + 1) * LANE] = jnp.where(lane < MLA_NOPE, dk0, 0.0).astype(BF16)
            dkpe = dkpe + jnp.where((lane >= MLA_NOPE) & (lane < MLA_QK), dk0, 0.0)
        dq0 = dq0_ref[...]
        dkn = dkn_ref[...]
        dvv = dv_ref[...]
        dwuq_ref[...] += _mm_tn(cqn, dq0)
        dwk_ref[...] += _mm_tn(ckn, dkn)
        dwv_ref[...] += _mm_tn(ckn, dvv)
        dgmq_ref[...] += dgmq
        dgmk_ref[...] += dgmk
        dcqn = _mm_nt(dq0, wuq_ref[...])
        gy = dcqn * gq
        dp_ref[:, 0:2 * LANE] = (
            rq * gy - cq * (rq * rq * rq) * (jnp.sum(cq * gy, axis=-1, keepdims=True) * (1.0 / MLA_Q_LORA))).astype(BF16)
        dgq_ref[...] += jnp.sum(dcqn * cq * rq, axis=0, keepdims=True)
        dckn = _mm_nt(dkn, wk_ref[...]) + _mm_nt(dvv, wv_ref[...])
        gy = dckn * gkv
        dp_ref[:, 2 * LANE:3 * LANE] = (
            rkv * gy - ckv * (rkv * rkv * rkv) * (jnp.sum(ckv * gy, axis=-1, keepdims=True) * (1.0 / MLA_KV_LORA))).astype(BF16)
        dgkv_ref[...] += jnp.sum(dckn * ckv * rkv, axis=0, keepdims=True)
        dp_ref[:, 3 * LANE:4 * LANE] = pltpu.roll(dkpe, HALF, axis=1).astype(BF16)

    def whole(r, c):
        return pl.BlockSpec((r, c), lambda i: (0, 0))

    tab = pl.BlockSpec((ts, LANE), lambda i: (i % ns, 0))
    heads = pl.BlockSpec((MLA_HEADS, ts, LANE), lambda i: (0, i, 0))
    return pl.pallas_call(
        body, name="mla_prep_bwd", grid=(T // ts,),
        in_specs=[pl.BlockSpec(memory_space=pl.ANY), heads, heads,
                  pl.BlockSpec((ts, MLA_HEADS * MLA_V), lambda i: (i, 0)),
                  pl.BlockSpec((ts, 4 * LANE), lambda i: (i, CB_CQ // 4)),
                  whole(1, MLA_Q_LORA), whole(1, MLA_KV_LORA), whole(MLA_Q_LORA, W), whole(MLA_KV_LORA, W),
                  whole(MLA_KV_LORA, MLA_HEADS * MLA_V), whole(1, LANE), whole(1, LANE), tab, tab, tab],
        out_specs=[pl.BlockSpec((ts, 4 * LANE), lambda i: (i, CB_CQ // 4)),
                   whole(MLA_Q_LORA, W), whole(MLA_KV_LORA, W), whole(MLA_KV_LORA, MLA_HEADS * MLA_V),
                   whole(1, MLA_Q_LORA), whole(1, MLA_KV_LORA), whole(1, LANE), whole(1, LANE)],
        out_shape=[jax.ShapeDtypeStruct(dproj.shape, BF16),
                   jax.ShapeDtypeStruct((MLA_Q_LORA, W), F32), jax.ShapeDtypeStruct((MLA_KV_LORA, W), F32),
                   jax.ShapeDtypeStruct((MLA_KV_LORA, MLA_HEADS * MLA_V), F32),
                   jax.ShapeDtypeStruct((1, MLA_Q_LORA), F32), jax.ShapeDtypeStruct((1, MLA_KV_LORA), F32),
                   jax.ShapeDtypeStruct((1, LANE), F32), jax.ShapeDtypeStruct((1, LANE), F32)],
        scratch_shapes=[pltpu.VMEM((ts, W), BF16), pltpu.VMEM((ts, W), BF16)],
        input_output_aliases={0: 0},
        compiler_params=_cp(),
    )(dproj, dq, dk, dv, proj, gq, gkv, wuqp, wkp, wv, gmq, gmk, cos, sa, sb)


def _dil_prep_fwd(proj, gq, gk):
    T = proj.shape[0]
    ts = _tile(T, 512)
    nc = DIL_QK // LANE

    def body(p_ref, gq_ref, gk_ref, q_ref, k_ref):
        for i, (g_ref, o_ref) in enumerate(((gq_ref, q_ref), (gk_ref, k_ref))):
            t = p_ref[:, i * LANE:(i + 1) * LANE]
            r = lax.rsqrt(_head_bcast_sum(t * t) * (1.0 / DIL_HEAD_DIM) + EPS)
            o_ref[...] = (t * r * g_ref[...]).astype(BF16)

    col = pl.BlockSpec((1, LANE), lambda i, c: (0, c))
    out = pl.BlockSpec((ts, LANE), lambda i, c: (i, c))
    return pl.pallas_call(
        body, name="dil_prep_fwd", grid=(T // ts, nc),
        in_specs=[pl.BlockSpec((ts, 2 * LANE), lambda i, c: (i, CB_QK // 2 + c)), col, col],
        out_specs=[out, out],
        out_shape=[jax.ShapeDtypeStruct((T, DIL_QK), BF16)] * 2,
        compiler_params=_cp(),
    )(proj, gq, gk)


def _dil_prep_bwd(dproj, ddq, ddk, proj, gq, gk):
    T = proj.shape[0]
    ts = _tile(T, 512)
    nc = DIL_QK // LANE

    def body(dpin_ref, ddq_ref, ddk_ref, p_ref, gq_ref, gk_ref, dp_ref, dgq_ref, dgk_ref):
        del dpin_ref
        first = pl.program_id(1) == 0
        for i, (d_ref, g_ref, dg_ref) in enumerate(((ddq_ref, gq_ref, dgq_ref), (ddk_ref, gk_ref, dgk_ref))):
            t = p_ref[:, i * LANE:(i + 1) * LANE]
            r = lax.rsqrt(_head_bcast_sum(t * t) * (1.0 / DIL_HEAD_DIM) + EPS)
            d = d_ref[...]
            gy = d * g_ref[...]
            dp_ref[:, i * LANE:(i + 1) * LANE] = (
                r * gy - t * (r * r * r) * (_head_bcast_sum(t * gy) * (1.0 / DIL_HEAD_DIM))).astype(BF16)
            part = jnp.sum(d * t * r, axis=0, keepdims=True)

            @pl.when(first)
            def _():
                dg_ref[...] = part

            @pl.when(jnp.logical_not(first))
            def _():
                dg_ref[...] += part

    col = pl.BlockSpec((1, LANE), lambda c, i: (0, c))
    tok = pl.BlockSpec((ts, LANE), lambda c, i: (i, c))
    return pl.pallas_call(
        body, name="dil_prep_bwd", grid=(nc, T // ts),
        in_specs=[pl.BlockSpec(memory_space=pl.ANY), tok, tok,
                  pl.BlockSpec((ts, 2 * LANE), lambda c, i: (i, CB_QK // 2 + c)), col, col],
        out_specs=[pl.BlockSpec((ts, 2 * LANE), lambda c, i: (i, CB_QK // 2 + c)), col, col],
        out_shape=[jax.ShapeDtypeStruct(dproj.shape, BF16), jax.ShapeDtypeStruct((1, DIL_QK), F32),
                   jax.ShapeDtypeStruct((1, DIL_QK), F32)],
        input_output_aliases={0: 0},
        compiler_params=_cp(),
    )(dproj, ddq, ddk, proj, gq, gk)


def _scores(qh, kh, q0, k0, t, scale, slope, window):
    s = lax.dot_general(qh, kh, (((1,), (1,)), ((), ())), preferred_element_type=F32) * scale
    dist = (q0 - k0) + lax.broadcasted_iota(jnp.int32, (t, t), 0) - lax.broadcasted_iota(jnp.int32, (t, t), 1)
    if slope is not None:
        s = s - slope * dist.astype(F32)
    valid = dist >= 0
    if window is not None:
        valid = valid & (dist <= window)
    return jnp.where(valid, s, NEG)


def _attn_fwd(name, slopes, q, k, v, out_shape, specs, grid, *, packed, t, scale, window):
    has_bias = slopes is not None
    q_spec, k_spec, v_spec, o_spec = specs

    def body(*refs):
        if has_bias:
            sl_ref, q_ref, k_ref, v_ref, o_ref, lse_ref = refs
        else:
            q_ref, k_ref, v_ref, o_ref, lse_ref = refs
        j, qi = pl.program_id(1), pl.program_id(2)
        lo = _lane_lo((t, LANE))
        q0 = qi * t
        kb_lo = 0 if window is None else jnp.maximum(qi - (window + t - 1) // t, 0)
        res = []
        for hh in range(2):
            sel = lo if hh == 0 else jnp.logical_not(lo)
            qh = jnp.where(sel, q_ref[...], jnp.zeros((), BF16)) if packed else q_ref[hh]
            slope = sl_ref[j, hh] if has_bias else None

            def step(kb, carry, hh=hh, qh=qh, slope=slope):
                m, l, acc = carry
                off = pl.multiple_of(kb * t, t)
                kh = k_ref[pl.ds(off, t), :] if packed else k_ref[hh, pl.ds(off, t), :]
                s = _scores(qh, kh, q0, off, t, scale, slope, window)
                m_new = jnp.maximum(m, jnp.max(s, axis=-1, keepdims=True))
                p = jnp.exp(s - m_new)
                a = jnp.exp(m - m_new)
                l = a * l + jnp.sum(p, axis=-1, keepdims=True)
                acc = a * acc + _mm(p, v_ref[pl.ds(off, t), :])
                return m_new, l, acc

            m, l, acc = lax.fori_loop(kb_lo, qi + 1, step,
                                      (jnp.full((t, 1), NEG, F32), jnp.zeros((t, 1), F32), jnp.zeros((t, LANE), F32)))
            res.append((acc / l, m + jnp.log(l)))
        o_ref[...] = jnp.where(lo, res[0][0], res[1][0])
        lse_ref[...] = jnp.where(lo, res[0][1], res[1][1])

    in_specs = [q_spec, k_spec, v_spec]
    args = [q, k, v]
    if has_bias:
        in_specs = [pl.BlockSpec(memory_space=pltpu.SMEM)] + in_specs
        args = [slopes] + args
    return pl.pallas_call(
        body, name=name, grid=grid, in_specs=in_specs, out_specs=[o_spec, o_spec],
        out_shape=[jax.ShapeDtypeStruct(out_shape, F32)] * 2,
        compiler_params=_cp(),
    )(*args)


def _attn_bwd(name, slopes, q, k, v, do, lse, delta, dv_alias, outs, specs, grid, *, packed, t, L, scale, window):
    has_bias = slopes is not None
    q_spec, k_spec, v_spec, do_spec, dq_spec, dk_spec, dv_spec = specs
    nq = L // t
    nacc = 1 if packed else 2

    def body(*refs):
        refs = list(refs)
        sl_ref = refs.pop(0) if has_bias else None
        q_ref, k_ref, v_ref, do_ref, lse_ref, dl_ref = refs[:6]
        refs = refs[6:]
        if dv_alias is not None:
            refs.pop(0)
        dq_ref, dk_ref, dv_ref, dk_acc, dv_acc = refs
        j = pl.program_id(1)
        lo = _lane_lo((t, LANE))
        dk_acc[...] = jnp.zeros_like(dk_acc)
        dv_acc[...] = jnp.zeros_like(dv_acc)
        for hh in range(2):
            sel = lo if hh == 0 else jnp.logical_not(lo)
            slope = sl_ref[j, hh] if has_bias else None

            def qloop(qb, _, hh=hh, sel=sel, slope=slope):
                qoff = pl.multiple_of(qb * t, t)
                rows = pl.ds(qoff, t)
                qh = jnp.where(sel, q_ref[rows, :], jnp.zeros((), BF16)) if packed else q_ref[hh, rows, :]
                doh = jnp.where(sel, do_ref[rows, :], jnp.zeros((), BF16))
                lse_h = jnp.max(jnp.where(sel, lse_ref[rows, :], NEG), axis=-1, keepdims=True)
                dl_h = jnp.max(jnp.where(sel, dl_ref[rows, :], NEG), axis=-1, keepdims=True)
                kb_lo = 0 if window is None else jnp.maximum(qb - (window + t - 1) // t, 0)

                def kloop(kb, dq_acc):
                    koff = pl.multiple_of(kb * t, t)
                    cols = pl.ds(koff, t)
                    kh = k_ref[cols, :] if packed else k_ref[hh, cols, :]
                    s = _scores(qh, kh, qoff, koff, t, scale, slope, window)
                    p = jnp.exp(s - lse_h)
                    dp = _mm_nt(doh, v_ref[cols, :])
                    ds = (p * (dp - dl_h) * scale).astype(BF16)
                    ia = 0 if packed else hh
                    dk_acc[ia, cols, :] += _mm_tn(ds, qh)
                    dv_acc[cols, :] += _mm_tn(p, doh)
                    return dq_acc + _mm(ds, kh)

                dq_h = lax.fori_loop(kb_lo, qb + 1, kloop, jnp.zeros((t, LANE), F32))
                if not packed:
                    dq_ref[hh, rows, :] = dq_h
                elif hh == 0:
                    dq_ref[rows, :] = jnp.where(sel, dq_h, 0.0)
                else:
                    dq_ref[rows, :] += jnp.where(sel, dq_h, 0.0)
                return 0

            lax.fori_loop(0, nq, qloop, 0)
        if packed:
            dk_ref[...] = dk_acc[0]
        else:
            dk_ref[...] = dk_acc[...]
        dv_ref[...] = dv_acc[...].astype(dv_ref.dtype)

    in_specs = [q_spec, k_spec, v_spec, do_spec, do_spec, do_spec]
    args = [q, k, v, do, lse, delta]
    if has_bias:
        in_specs = [pl.BlockSpec(memory_space=pltpu.SMEM)] + in_specs
        args = [slopes] + args
    aliases = {}
    if dv_alias is not None:
        aliases = {len(args): 2}
        in_specs = in_specs + [pl.BlockSpec(memory_space=pl.ANY)]
        args = args + [dv_alias]
    return pl.pallas_call(
        body, name=name, grid=grid, in_specs=in_specs, out_specs=[dq_spec, dk_spec, dv_spec],
        out_shape=list(outs),
        scratch_shapes=[pltpu.VMEM((nacc, L, LANE), F32), pltpu.VMEM((L, LANE), F32)],
        input_output_aliases=aliases,
        compiler_params=_cp(),
    )(*args)


def _mla_attn_fwd(q, k, v, B, S):
    T = B * S
    t = _tile(S, 256)
    nq = S // t
    specs = (pl.BlockSpec((2, t, LANE), lambda b, j, i: (j, b * nq + i, 0)),
             pl.BlockSpec((2, S, LANE), lambda b, j, i: (j, b, 0)),
             pl.BlockSpec((S, LANE), lambda b, j, i: (b, j)),
             pl.BlockSpec((t, LANE), lambda b, j, i: (b * nq + i, j)))
    return _attn_fwd("mla_attn_fwd", None, q, k, v, (T, MLA_HEADS * MLA_V), specs, (B, NPAIR, nq),
                     packed=False, t=t, scale=MLA_QK ** -0.5, window=None)


def _mla_attn_bwd(q, k, v, do, lse, delta, B, S):
    T = B * S
    t = _tile(S, 256)
    heads = pl.BlockSpec((2, S, LANE), lambda b, j: (j, b, 0))
    pair = pl.BlockSpec((S, LANE), lambda b, j: (b, j))
    outs = (jax.ShapeDtypeStruct((MLA_HEADS, T, LANE), F32), jax.ShapeDtypeStruct((MLA_HEADS, T, LANE), F32),
            jax.ShapeDtypeStruct((T, MLA_HEADS * MLA_V), F32))
    return _attn_bwd("mla_attn_bwd", None, q, k, v, do, lse, delta, None, outs,
                     (heads, heads, pair, pair, heads, heads, pair), (B, NPAIR),
                     packed=False, t=t, L=S, scale=MLA_QK ** -0.5, window=None)


def _dil_views(gi, B, S):
    window, d = DIL_PATTERNS[gi]
    L = S // d
    return d, L, _tile(L, 128), window // d


def _dil_attn_fwd(gi, slopes, qn, kn, proj, B, S):
    d, L, t, window = _dil_views(gi, B, S)
    nq = L // t
    nqk, nv = DIL_QK // LANE, NCB

    def qcol(r, j):
        return r * nqk + NPAIR * gi + j

    specs = (pl.BlockSpec((None, t, LANE), lambda n, j, i: (n // d, i, qcol(n % d, j))),
             pl.BlockSpec((None, L, LANE), lambda n, j, i: (n // d, 0, qcol(n % d, j))),
             pl.BlockSpec((None, L, LANE), lambda n, j, i: (n // d, 0, (n % d) * nv + CB_DV + NPAIR * gi + j)),
             pl.BlockSpec((None, t, LANE), lambda n, j, i: (n // d, i, (n % d) * NPAIR + j)))
    o, lse = _attn_fwd(f"dil_attn_fwd_{gi}", slopes, qn.reshape(B, L, d * DIL_QK), kn.reshape(B, L, d * DIL_QK),
                       proj.reshape(B, L, d * PP), (B, L, d * DIL_WIDTH), specs, (B * d, NPAIR, nq),
                       packed=True, t=t, scale=DIL_HEAD_DIM ** -0.5, window=window)
    return o.reshape(B * S, DIL_WIDTH), lse.reshape(B * S, DIL_WIDTH)


def _dil_attn_bwd(gi, slopes, qn, kn, proj, do, lse, delta, dproj, B, S):
    d, L, t, window = _dil_views(gi, B, S)
    nqk, nv = DIL_QK // LANE, NCB

    def qk(n, j):
        return (n // d, 0, (n % d) * nqk + NPAIR * gi + j)

    def vcol(n, j):
        return (n // d, 0, (n % d) * nv + CB_DV + NPAIR * gi + j)

    def ocol(n, j):
        return (n // d, 0, (n % d) * NPAIR + j)

    blk = (None, L, LANE)
    specs = (pl.BlockSpec(blk, qk), pl.BlockSpec(blk, qk), pl.BlockSpec(blk, vcol), pl.BlockSpec(blk, ocol),
             pl.BlockSpec(blk, ocol), pl.BlockSpec(blk, ocol), pl.BlockSpec(blk, vcol))
    outs = (jax.ShapeDtypeStruct((B, L, d * DIL_WIDTH), F32), jax.ShapeDtypeStruct((B, L, d * DIL_WIDTH), F32),
            jax.ShapeDtypeStruct((B, L, d * PP), BF16))
    v3 = lambda a, w: a.reshape(B, L, d * w)
    dq, dk, dproj = _attn_bwd(f"dil_attn_bwd_{gi}", slopes, v3(qn, DIL_QK), v3(kn, DIL_QK), v3(proj, PP),
                              v3(do, DIL_WIDTH), v3(lse, DIL_WIDTH), v3(delta, DIL_WIDTH), v3(dproj, PP), outs, specs,
                              (B * d, NPAIR), packed=True, t=t, L=L, scale=DIL_HEAD_DIM ** -0.5, window=window)
    return dq.reshape(B * S, DIL_WIDTH), dk.reshape(B * S, DIL_WIDTH), dproj.reshape(B * S, PP)


def _merge_common(p_ref, bg_ref, ob_ref, og_refs, lse_refs):
    bz = p_ref[:, CB_BZ * LANE:(CB_BZ + 4) * LANE]
    cz = p_ref[:, CB_CZ * LANE:(CB_CZ + 4) * LANE]
    gates = [_sigmoid(p_ref[:, (CB_GATE + 8 * i) * LANE:(CB_GATE + 8 * i + 8) * LANE]
                      + bg_ref[:, i * D_MODEL:(i + 1) * D_MODEL]) for i in range(3)]
    ob = ob_ref[...]
    lses = [r[...] for r in lse_refs]
    mx = jnp.maximum(jnp.maximum(lses[0], lses[1]), lses[2])
    es = [jnp.exp(v - mx) for v in lses]
    inv = 1.0 / (es[0] + es[1] + es[2])
    alphas = [e * inv for e in es]
    oc = alphas[0] * og_refs[0][...] + alphas[1] * og_refs[1][...] + alphas[2] * og_refs[2][...]
    return bz, cz, gates, ob, alphas, oc


def _merge_fwd(x, proj, b_gate, ya, ob, ogs, lses, woa, wob, woc, wo):
    T = x.shape[0]
    ts = _tile(T, 256)
    MW = 32 * LANE

    def body(x_ref, p_ref, bg_ref, ya_ref, ob_ref, og0, og1, og2, l0, l1, l2, woa_ref, wob_ref, woc_ref, wo_ref, out_ref):
        bz, cz, gates, obv, alphas, oc = _merge_common(p_ref, bg_ref, ob_ref, (og0, og1, og2), (l0, l1, l2))
        yb = obv * _silu(bz)
        yc = oc * _silu(cz)
        merged = (gates[0] * _mm(ya_ref[...], woa_ref[...]) + gates[1] * _mm(yb, wob_ref[...])
                  + gates[2] * _mm(yc, woc_ref[...]))
        out_ref[...] = x_ref[...] + _mm(merged, wo_ref[...])

    def whole(r, c):
        return pl.BlockSpec((r, c), lambda i: (0, 0))

    tok = lambda w: pl.BlockSpec((ts, w), lambda i: (i, 0))
    return pl.pallas_call(
        body, name="merge_fwd", grid=(T // ts,),
        in_specs=[tok(D_MODEL), tok(MW), whole(1, 3 * D_MODEL), tok(CONV_WIDTH)] + [tok(DIL_WIDTH)] * 7
                 + [whole(CONV_WIDTH, D_MODEL)] * 3 + [whole(D_MODEL, D_MODEL)],
        out_specs=tok(D_MODEL),
        out_shape=jax.ShapeDtypeStruct((T, D_MODEL), F32),
        compiler_params=_cp(),
    )(x, proj, b_gate, ya, ob, *ogs, *lses, woa, wob, woc, wo)


def _merge_bwd(dout, proj, b_gate, ya, ob, ogs, lses, woa, wob, woc, wo):
    T = dout.shape[0]
    ts = _tile(T, 256)
    MW = 32 * LANE

    def body(do_ref, p_ref, bg_ref, ya_ref, ob_ref, og0, og1, og2, l0, l1, l2, woa_ref, wob_ref, woc_ref, wo_ref,
             dp_ref, dya_ref, dob_ref, dlb_ref, dg0, dg1, dg2, dl0, dl1, dl2,
             mg_ref, dpa_ref, dpb_ref, dpc_ref, yb_ref, yc_ref, dbg_ref):
        bz, cz, gates, obv, alphas, oc = _merge_common(p_ref, bg_ref, ob_ref, (og0, og1, og2), (l0, l1, l2))
        sb, sc = _silu(bz), _silu(cz)
        yb = obv * sb
        yc = oc * sc
        ps = [_mm(ya_ref[...], woa_ref[...]), _mm(yb, wob_ref[...]), _mm(yc, woc_ref[...])]
        mg_ref[...] = (gates[0] * ps[0] + gates[1] * ps[1] + gates[2] * ps[2]).astype(BF16)
        yb_ref[...] = yb.astype(BF16)
        yc_ref[...] = yc.astype(BF16)
        dm = _mm_nt(do_ref[...], wo_ref[...])
        dps = []
        first = pl.program_id(0) == 0
        for i, dref in enumerate((dpa_ref, dpb_ref, dpc_ref)):
            g = gates[i]
            dpi = (dm * g).astype(BF16)
            dref[...] = dpi
            dps.append(dpi)
            dgp = dm * ps[i] * g * (1.0 - g)
            dp_ref[:, (CB_GATE + 8 * i) * LANE:(CB_GATE + 8 * i + 8) * LANE] = dgp.astype(BF16)
            part = jnp.sum(dgp, axis=0, keepdims=True)

            @pl.when(first)
            def _():
                dbg_ref[:, i * D_MODEL:(i + 1) * D_MODEL] = part

            @pl.when(jnp.logical_not(first))
            def _():
                dbg_ref[:, i * D_MODEL:(i + 1) * D_MODEL] += part

        dya_ref[...] = _mm_nt(dps[0], woa_ref[...])
        dyb = _mm_nt(dps[1], wob_ref[...])
        dyc = _mm_nt(dps[2], woc_ref[...])
        dp_ref[:, CB_BZ * LANE:(CB_BZ + 4) * LANE] = (dyb * obv * _dsilu(bz)).astype(BF16)
        dp_ref[:, CB_CZ * LANE:(CB_CZ + 4) * LANE] = (dyc * oc * _dsilu(cz)).astype(BF16)
        dob = dyb * sb
        doc = dyc * sc
        dob_ref[...] = dob.astype(BF16)
        for c in range(NPAIR):
            cs = slice(c * LANE, (c + 1) * LANE)
            dlb_ref[:, cs] = _head_bcast_sum(dob[:, cs] * obv[:, cs])
            dd = _head_bcast_sum(doc[:, cs] * oc[:, cs])
            for a, dref, lref in zip(alphas, (dg0, dg1, dg2), (dl0, dl1, dl2)):
                dref[:, cs] = (a[:, cs] * doc[:, cs]).astype(BF16)
                lref[:, cs] = a[:, cs] * dd

    def whole(r, c):
        return pl.BlockSpec((r, c), lambda i: (0, 0))

    tok = lambda w: pl.BlockSpec((ts, w), lambda i: (i, 0))
    sd = jax.ShapeDtypeStruct
    W = DIL_WIDTH
    return pl.pallas_call(
        body, name="merge_bwd", grid=(T // ts,),
        in_specs=[tok(D_MODEL), tok(MW), whole(1, 3 * D_MODEL), tok(CONV_WIDTH)] + [tok(W)] * 7
                 + [whole(CONV_WIDTH, D_MODEL)] * 3 + [whole(D_MODEL, D_MODEL)],
        out_specs=[tok(MW), tok(CONV_WIDTH), tok(W), tok(W)] + [tok(W)] * 6
                  + [tok(D_MODEL)] * 4 + [tok(W), tok(W), whole(1, 3 * D_MODEL)],
        out_shape=[sd((T, PP), BF16), sd((T, CONV_WIDTH), F32), sd((T, W), BF16), sd((T, W), F32)]
                  + [sd((T, W), BF16)] * 3 + [sd((T, W), F32)] * 3
                  + [sd((T, D_MODEL), BF16)] * 4 + [sd((T, W), BF16)] * 2 + [sd((1, 3 * D_MODEL), F32)],
        compiler_params=_cp(),
    )(dout, proj, b_gate, ya, ob, *ogs, *lses, woa, wob, woc, wo)


def _loss_head(y, target):
    T = y.shape[0]
    ts = _tile(T, 512)

    def body(y_ref, t_ref, d_ref, l_ref):
        e = y_ref[...] - t_ref[...]
        d_ref[...] = e * (1.0 / D_MODEL)
        l_ref[...] = jnp.zeros((1, 8, LANE), F32) + jnp.sum(e * e)

    tok = pl.BlockSpec((ts, D_MODEL), lambda i: (i, 0))
    return pl.pallas_call(
        body, name="loss_head", grid=(T // ts,), in_specs=[tok, tok],
        out_specs=[tok, pl.BlockSpec((1, 8, LANE), lambda i: (i, 0, 0))],
        out_shape=[jax.ShapeDtypeStruct((T, D_MODEL), F32), jax.ShapeDtypeStruct((T // ts, 8, LANE), F32)],
        compiler_params=_cp(),
    )(y, target)


def _my_index():
    return 4 * lax.axis_index("x") + 2 * lax.axis_index("y") + lax.axis_index("c")


def _peers():
    x, y, c = (lax.axis_index(a) for a in AXES)
    out = []
    for kk in range(1, N_DEV):
        px = 1 - x if kk & 4 else x
        py = 1 - y if kk & 2 else y
        pc = 1 - c if kk & 1 else c
        out.append(((px, py, pc), 4 * px + 2 * py + pc))
    return out


def _all_gather(pack, name):
    def body(src_ref, out_ref, send_sems, recv_sems, local_sem):
        me = _my_index()
        mine = pltpu.make_async_copy(src_ref, out_ref.at[me], local_sem)
        mine.start()
        peers = _peers()
        sends = [pltpu.make_async_remote_copy(src_ref=src_ref, dst_ref=out_ref.at[me], send_sem=send_sems.at[i],
                                              recv_sem=recv_sems.at[i], device_id=pos, device_id_type=pl.DeviceIdType.MESH)
                 for i, (pos, _) in enumerate(peers)]
        for cp in sends:
            cp.start()
        for i, (pos, idx) in enumerate(peers):
            pltpu.make_async_remote_copy(src_ref=src_ref, dst_ref=out_ref.at[idx], send_sem=send_sems.at[i],
                                         recv_sem=recv_sems.at[i], device_id=pos,
                                         device_id_type=pl.DeviceIdType.MESH).wait_recv()
        for cp in sends:
            cp.wait_send()
        mine.wait()

    return pl.pallas_call(
        body, name=name,
        in_specs=[pl.BlockSpec(memory_space=pl.ANY)], out_specs=pl.BlockSpec(memory_space=pl.ANY),
        out_shape=jax.ShapeDtypeStruct((N_DEV,) + pack.shape, pack.dtype),
        scratch_shapes=[pltpu.SemaphoreType.DMA((N_DEV - 1,)), pltpu.SemaphoreType.DMA((N_DEV - 1,)),
                        pltpu.SemaphoreType.DMA],
    )(pack)


def _exchange_slots(pack, name):
    def body(src_ref, out_ref, send_sems, recv_sems, local_sem):
        me = _my_index()
        mine = pltpu.make_async_copy(src_ref.at[me], out_ref.at[me], local_sem)
        mine.start()
        peers = _peers()
        sends = [pltpu.make_async_remote_copy(src_ref=src_ref.at[idx], dst_ref=out_ref.at[me], send_sem=send_sems.at[i],
                                              recv_sem=recv_sems.at[i], device_id=pos, device_id_type=pl.DeviceIdType.MESH)
                 for i, (pos, idx) in enumerate(peers)]
        for cp in sends:
            cp.start()
        for i, (pos, idx) in enumerate(peers):
            pltpu.make_async_remote_copy(src_ref=src_ref.at[idx], dst_ref=out_ref.at[idx], send_sem=send_sems.at[i],
                                         recv_sem=recv_sems.at[i], device_id=pos,
                                         device_id_type=pl.DeviceIdType.MESH).wait_recv()
        for cp in sends:
            cp.wait_send()
        mine.wait()

    return pl.pallas_call(
        body, name=name,
        in_specs=[pl.BlockSpec(memory_space=pl.ANY)], out_specs=pl.BlockSpec(memory_space=pl.ANY),
        out_shape=jax.ShapeDtypeStruct(pack.shape, pack.dtype),
        scratch_shapes=[pltpu.SemaphoreType.DMA((N_DEV - 1,)), pltpu.SemaphoreType.DMA((N_DEV - 1,)),
                        pltpu.SemaphoreType.DMA],
    )(pack)


def _adamw(w, g, m, v):
    m = ADAM_B1 * m + (1.0 - ADAM_B1) * g
    v = ADAM_B2 * v + (1.0 - ADAM_B2) * (g * g)
    m_hat = m / (1.0 - ADAM_B1 ** ADAM_STEP)
    v_hat = v / (1.0 - ADAM_B2 ** ADAM_STEP)
    delta = -ADAM_LR * (m_hat / (jnp.sqrt(v_hat) + ADAM_EPS) + ADAM_WD * w)
    return delta, m, v


def _reduce_adamw(parts, w, m, v, name):
    R = w.shape[0]
    tr = _tile(R, PACK_ROWS)

    def body(p_ref, w_ref, m_ref, v_ref, g_ref, d_ref, nm_ref, nv_ref):
        g = p_ref[0].astype(F32)
        for s in range(1, N_DEV):
            g = g + p_ref[s].astype(F32)
        g_ref[...] = g
        d_ref[...], nm_ref[...], nv_ref[...] = _adamw(w_ref[...], g, m_ref[...], v_ref[...])

    row = pl.BlockSpec((tr, LANE), lambda i: (i, 0))
    return pl.pallas_call(
        body, name=name, grid=(R // tr,),
        in_specs=[pl.BlockSpec((N_DEV, tr, LANE), lambda i: (0, i, 0)), row, row, row],
        out_specs=[row] * 4, out_shape=[jax.ShapeDtypeStruct((R, LANE), F32)] * 4,
        compiler_params=_cp(),
    )(parts, w, m, v)


BIG = ("w_in", "w_uq", "w_ukv", "w_out_a", "w_out_b", "w_out_c", "w_o")
SMALL = ("norm_g", "b_gate", "conv_w", "conv_b", "q_a_norm_g", "kv_a_norm_g", "mla_q_norm_g", "mla_k_norm_g",
         "dil_q_norm_g", "dil_k_norm_g")
ROW_SHARDED = ("w_o",)
PACK_ROWS = 512


def _to_rows(flat):
    n = flat.shape[-1]
    pad = (-n) % (PACK_ROWS * LANE)
    if pad:
        flat = jnp.concatenate([flat, jnp.zeros(flat.shape[:-1] + (pad,), flat.dtype)], axis=-1)
    return flat.reshape(flat.shape[:-1] + (-1, LANE))


def _pack_local(tensors):
    return _to_rows(jnp.concatenate([t.reshape(-1) for t in tensors]))


def _unpack_local(rows, like):
    flat = rows.reshape(-1)
    out, off = [], 0
    for t in like:
        out.append(flat[off:off + t.size].reshape(t.shape))
        off += t.size
    return out


def _shards_of(name, full):
    d, r, c = full.shape
    if name in ROW_SHARDED:
        return full.reshape(d, N_DEV, r // N_DEV, c).transpose(1, 0, 2, 3).reshape(N_DEV, -1)
    return full.reshape(d, r, N_DEV, c // N_DEV).transpose(2, 0, 1, 3).reshape(N_DEV, -1)


def _full_of(name, shards, shape):
    d, r, c = shape
    if name in ROW_SHARDED:
        return shards.reshape(N_DEV, d, r, c).transpose(1, 0, 2, 3).reshape(d, N_DEV * r, c)
    return shards.reshape(N_DEV, d, r, c).transpose(1, 2, 0, 3).reshape(d, r, N_DEV * c)


def _rope_tables(S):
    inv = ROPE_THETA ** (-jnp.arange(0, MLA_ROPE, 2, dtype=F32) / MLA_ROPE)
    ang = jnp.arange(S, dtype=F32)[:, None] * inv[None, :]
    cos, sin = jnp.cos(ang), jnp.sin(ang)
    one = jnp.ones((S, MLA_NOPE), F32)
    z16, z32, z64 = (jnp.zeros((S, n), F32) for n in (16, 32, 64))
    cosp = jnp.concatenate([one, cos, cos, jnp.ones((S, 32), F32)], axis=1)
    sa = jnp.concatenate([z64, -sin, z16, z32], axis=1)
    sb = jnp.concatenate([z64, z16, sin, z32], axis=1)
    return cosp, sa, sb


def _alibi_slopes():
    n = DIL_GROUPS * DIL_HEADS
    m = 2.0 ** (-8.0 * jnp.arange(1, n + 1, dtype=F32) / n)
    return m.reshape(DIL_GROUPS, NPAIR, 2)


def _pad_heads(w, width):
    k = w.shape[0]
    w3 = w.reshape(k, MLA_HEADS, width)
    return jnp.concatenate([w3, jnp.zeros((k, MLA_HEADS, LANE - width), w.dtype)], axis=2).reshape(k, MLA_HEADS * LANE)


def _layer_params(full, l):
    p = {}
    p["wp"] = _pad_columns(full["w_in"][l])
    p["norm_g"] = full["norm_g"][l][None]
    p["b_gate"] = full["b_gate"][l][None]
    p["conv_w"] = full["conv_w"][l]
    p["conv_b"] = full["conv_b"][l][None]
    p["gq"] = full["q_a_norm_g"][l][None]
    p["gkv"] = full["kv_a_norm_g"][l][None]
    p["wuqp"] = _pad_heads(full["w_uq"][l], MLA_QK)
    kv = full["w_ukv"][l].reshape(MLA_KV_LORA, MLA_HEADS, MLA_NOPE + MLA_V)
    p["wkp"] = _pad_heads(kv[:, :, :MLA_NOPE].reshape(MLA_KV_LORA, -1), MLA_NOPE)
    p["wv"] = kv[:, :, MLA_NOPE:].reshape(MLA_KV_LORA, -1)
    zpad = jnp.zeros((1, LANE - MLA_QK), F32)
    p["gmq"] = jnp.concatenate([full["mla_q_norm_g"][l][None], zpad], axis=1)
    p["gmk"] = jnp.concatenate([full["mla_k_norm_g"][l][None], zpad], axis=1)
    tile = lambda g: jnp.broadcast_to(g[:, None, :], (DIL_GROUPS, DIL_HEADS, DIL_HEAD_DIM)).reshape(1, DIL_QK)
    p["gdq"] = tile(full["dil_q_norm_g"][l])
    p["gdk"] = tile(full["dil_k_norm_g"][l])
    p["woa"], p["wob"], p["woc"], p["wo"] = (full[n][l] for n in ("w_out_a", "w_out_b", "w_out_c", "w_o"))
    return p


def _layer_fwd(x, p, tabs, slopes, B, S):
    proj, h = _inproj_fwd(x, p["norm_g"], p["wp"])
    ya = _mixa_fwd(proj, p["conv_w"], p["conv_b"], B, S)
    q, k, v = _mla_prep_fwd(proj, p["gq"], p["gkv"], p["wuqp"], p["wkp"], p["wv"], p["gmq"], p["gmk"], *tabs, S)
    ob, lse_b = _mla_attn_fwd(q, k, v, B, S)
    qn, kn = _dil_prep_fwd(proj, p["gdq"], p["gdk"])
    ogs, lses = [], []
    for gi in range(DIL_GROUPS):
        o, lse = _dil_attn_fwd(gi, slopes[gi], qn, kn, proj, B, S)
        ogs.append(o)
        lses.append(lse)
    out = _merge_fwd(x, proj, p["b_gate"], ya, ob, ogs, lses, p["woa"], p["wob"], p["woc"], p["wo"])
    saved = dict(x=x, proj=proj, h=h, ya=ya, q=q, k=k, v=v, ob=ob, lse_b=lse_b, qn=qn, kn=kn, ogs=ogs, lses=lses)
    return out, saved


def _layer_bwd(dout, sv, p, tabs, slopes, B, S):
    proj = sv["proj"]
    (dproj, dya, dob, dlb, dg0, dg1, dg2, dl0, dl1, dl2, merged, dpa, dpb, dpc, yb, yc, dbg) = _merge_bwd(
        dout, proj, p["b_gate"], sv["ya"], sv["ob"], sv["ogs"], sv["lses"], p["woa"], p["wob"], p["woc"], p["wo"])
    g = {}
    g["w_o"] = _matmul_tn(merged, dout, "dw_o")
    g["w_out_a"] = _matmul_tn(sv["ya"], dpa, "dw_out_a")
    g["w_out_b"] = _matmul_tn(yb, dpb, "dw_out_b")
    g["w_out_c"] = _matmul_tn(yc, dpc, "dw_out_c")
    g["b_gate"] = dbg[0]
    dproj, st = _mixa_bwd(dproj, dya, proj, p["conv_w"], p["conv_b"], B, S)
    g["conv_w"] = st[0:CONV_K]
    g["conv_b"] = st[CONV_K]
    dq, dk, dv = _mla_attn_bwd(sv["q"], sv["k"], sv["v"], dob, sv["lse_b"], dlb, B, S)
    dproj, dwuqp, dwkp, dwv, dgq, dgkv, dgmq, dgmk = _mla_prep_bwd(
        dproj, dq, dk, dv, proj, p["gq"], p["gkv"], p["wuqp"], p["wkp"], p["wv"], p["gmq"], p["gmk"], *tabs, S)
    g["w_uq"] = dwuqp.reshape(MLA_Q_LORA, MLA_HEADS, LANE)[:, :, :MLA_QK].reshape(MLA_Q_LORA, -1)
    g["w_ukv"] = jnp.concatenate([dwkp.reshape(MLA_KV_LORA, MLA_HEADS, LANE)[:, :, :MLA_NOPE],
                                  dwv.reshape(MLA_KV_LORA, MLA_HEADS, MLA_V)], axis=2).reshape(MLA_KV_LORA, -1)
    g["q_a_norm_g"], g["kv_a_norm_g"] = dgq[0], dgkv[0]
    g["mla_q_norm_g"], g["mla_k_norm_g"] = dgmq[0, :MLA_QK], dgmk[0, :MLA_QK]
    ddq, ddk = [], []
    for gi, (dog, dlg) in enumerate(((dg0, dl0), (dg1, dl1), (dg2, dl2))):
        a, b, dproj = _dil_attn_bwd(gi, slopes[gi], sv["qn"], sv["kn"], proj, dog, sv["lses"][gi], dlg, dproj, B, S)
        ddq.append(a)
        ddk.append(b)
    dproj, dgdq, dgdk = _dil_prep_bwd(dproj, jnp.concatenate(ddq, axis=1), jnp.concatenate(ddk, axis=1), proj,
                                      p["gdq"], p["gdk"])
    g["dil_q_norm_g"] = dgdq.reshape(DIL_GROUPS, DIL_HEADS, DIL_HEAD_DIM).sum(axis=1)
    g["dil_k_norm_g"] = dgdk.reshape(DIL_GROUPS, DIL_HEADS, DIL_HEAD_DIM).sum(axis=1)
    g["w_in"] = _unpad_columns(_matmul_tn(sv["h"], dproj, "dw_in"))
    dx, dng = _inproj_bwd_x(dproj, p["wp"], sv["x"], p["norm_g"], dout)
    g["norm_g"] = dng[0]
    return dx, g


def _local_step(x, target, full, B, S):
    tabs = _rope_tables(S)
    sl = _alibi_slopes()
    slopes = [sl[gi] * float(DIL_PATTERNS[gi][1]) for gi in range(DIL_GROUPS)]
    params = [_layer_params(full, l) for l in range(DEPTH)]
    saved = []
    for l in range(DEPTH):
        x, sv = _layer_fwd(x, params[l], tabs, slopes, B, S)
        saved.append(sv)
    dout, lparts = _loss_head(x, target)
    sq = jnp.sum(lparts[:, 0, 0])
    grads = [None] * DEPTH
    for l in reversed(range(DEPTH)):
        dout, grads[l] = _layer_bwd(dout, saved[l], params[l], tabs, slopes, B, S)
    stacked = {n: jnp.stack([grads[l][n] for l in range(DEPTH)]) for n in BIG + SMALL}
    return sq, dout, stacked


def kernel(x, norm_g, w_in, b_gate, conv_w, conv_b, q_a_norm_g, w_uq, kv_a_norm_g, w_ukv, mla_q_norm_g, mla_k_norm_g, dil_q_norm_g, dil_k_norm_g, w_out_a, w_out_b, w_out_c, w_o, loss_target, m_norm_g, m_w_in, m_b_gate, m_conv_w, m_conv_b, m_q_a_norm_g, m_w_uq, m_kv_a_norm_g, m_w_ukv, m_mla_q_norm_g, m_mla_k_norm_g, m_dil_q_norm_g, m_dil_k_norm_g, m_w_out_a, m_w_out_b, m_w_out_c, m_w_o, v_norm_g, v_w_in, v_b_gate, v_conv_w, v_conv_b, v_q_a_norm_g, v_w_uq, v_kv_a_norm_g, v_w_ukv, v_mla_q_norm_g, v_mla_k_norm_g, v_dil_q_norm_g, v_dil_k_norm_g, v_w_out_a, v_w_out_b, v_w_out_c, v_w_o):
    names = ("norm_g", "w_in", "b_gate", "conv_w", "conv_b", "q_a_norm_g", "w_uq", "kv_a_norm_g", "w_ukv",
             "mla_q_norm_g", "mla_k_norm_g", "dil_q_norm_g", "dil_k_norm_g", "w_out_a", "w_out_b", "w_out_c", "w_o")
    w = dict(zip(names, (norm_g, w_in, b_gate, conv_w, conv_b, q_a_norm_g, w_uq, kv_a_norm_g, w_ukv, mla_q_norm_g,
                         mla_k_norm_g, dil_q_norm_g, dil_k_norm_g, w_out_a, w_out_b, w_out_c, w_o)))
    m = dict(zip(names, (m_norm_g, m_w_in, m_b_gate, m_conv_w, m_conv_b, m_q_a_norm_g, m_w_uq, m_kv_a_norm_g, m_w_ukv,
                         m_mla_q_norm_g, m_mla_k_norm_g, m_dil_q_norm_g, m_dil_k_norm_g, m_w_out_a, m_w_out_b,
                         m_w_out_c, m_w_o)))
    v = dict(zip(names, (v_norm_g, v_w_in, v_b_gate, v_conv_w, v_conv_b, v_q_a_norm_g, v_w_uq, v_kv_a_norm_g, v_w_ukv,
                         v_mla_q_norm_g, v_mla_k_norm_g, v_dil_q_norm_g, v_dil_k_norm_g, v_w_out_a, v_w_out_b,
                         v_w_out_c, v_w_o)))
    B, S, _ = x.shape
    me = _my_index()
    cshard = CONV_WIDTH // N_DEV

    conv_bits = lax.bitcast_convert_type(conv_w, BF16)
    gathered = _all_gather(_pack_local([w[n].astype(BF16) for n in BIG] + [conv_bits]), "all_gather_weights")
    gathered = gathered.reshape(N_DEV, -1)
    full, off = {}, 0
    for n in BIG:
        full[n] = _full_of(n, gathered[:, off:off + w[n].size], w[n].shape)
        off += w[n].size
    bits = gathered[:, off:off + conv_bits.size].reshape((N_DEV,) + conv_bits.shape)
    full["conv_w"] = _full_of("conv_w", lax.bitcast_convert_type(bits, F32).reshape(N_DEV, -1), conv_w.shape)
    for n in SMALL:
        if n != "conv_w":
            full[n] = w[n]

    sq, grad_x, part = _local_step(x.reshape(B * S, D_MODEL), loss_target.reshape(B * S, D_MODEL), full, B, S)
    loss = lax.psum(sq * (0.5 / D_MODEL), AXES)

    send = _to_rows(jnp.concatenate([_shards_of(n, part[n]) for n in BIG], axis=1).astype(BF16))
    recv = _exchange_slots(send, "exchange_weight_grads")
    big_like = [w[n] for n in BIG]
    gb, db, mb, vb = _reduce_adamw(recv, _pack_local(big_like), _pack_local([m[n] for n in BIG]),
                                   _pack_local([v[n] for n in BIG]), "reduce_adamw_weights")
    res = {n: t for n, t in zip(BIG, zip(*(_unpack_local(a, big_like) for a in (gb, db, mb, vb))))}

    def widen(t):
        return lax.dynamic_update_slice(jnp.zeros((DEPTH, CONV_K, CONV_WIDTH), F32), t, (0, 0, me * cshard))

    small_like = [part[n] for n in SMALL]
    pick = lambda d: [widen(d[n]) if n == "conv_w" else d[n] for n in SMALL]
    parts = _all_gather(_pack_local(small_like), "all_gather_small_grads")
    gs, ds, ms, vs = _reduce_adamw(parts, _pack_local(pick(w)), _pack_local(pick(m)), _pack_local(pick(v)),
                                   "reduce_adamw_small")
    for n, t in zip(SMALL, zip(*(_unpack_local(a, small_like) for a in (gs, ds, ms, vs)))):
        if n == "conv_w":
            t = tuple(lax.dynamic_slice(a, (0, 0, me * cshard), (DEPTH, CONV_K, cshard)) for a in t)
        res[n] = t

    out = [loss, grad_x.reshape(B, S, D_MODEL)]
    for i in range(4):
        out += [res[n][i] for n in names]
    return tuple(out)
```

```python
import jax
import jax.numpy as jnp
from jax import lax
from jax.experimental import pallas as pl
from jax.experimental.pallas import tpu as pltpu

F32 = jnp.float32
BF16 = jnp.bfloat16

D_MODEL = 1024
DEPTH = 2
CONV_WIDTH = 512
CONV_K = 3
MLA_HEADS = 8
MLA_Q_LORA = 256
MLA_KV_LORA = 128
MLA_NOPE = 64
MLA_ROPE = 32
MLA_V = 64
MLA_QK = MLA_NOPE + MLA_ROPE
ROPE_THETA = 10000.0
DIL_PATTERNS = ((128, 1), (512, 4), (2048, 16))
DIL_GROUPS = 3
DIL_HEADS = 8
DIL_HEAD_DIM = 64
DIL_WIDTH = DIL_HEADS * DIL_HEAD_DIM
DIL_QK = DIL_GROUPS * DIL_WIDTH
EPS = 1e-6
N_IN = 11168

ADAM_LR = 0.001
ADAM_B1 = 0.9
ADAM_B2 = 0.999
ADAM_EPS = 1e-08
ADAM_WD = 0.01
ADAM_STEP = 10

N_DEV = 8
AXES = ("x", "y", "c")
LANE = 128
HALF = 64
NPAIR = 4

CB_BZ, CB_CZ, CB_GATE = 0, 4, 8
CB_A = 32
CB_QKV = 48
CB_CQ, CB_CKV, CB_KPE = 84, 86, 87
NCB = 88
PP = NCB * LANE
SHARD_COLS = N_IN // N_DEV
NEG = -1e30
VMEM_LIMIT = 56 * 1024 * 1024


def _column_chunks():
    out = []
    col = 0

    def seg(nblocks, block_of):
        nonlocal col
        for i in range(nblocks):
            out.append((col, LANE, block_of(i)))
            col += LANE

    seg(4, lambda j: CB_A + 4 * j)
    seg(4, lambda j: CB_A + 4 * j + 1)
    seg(4, lambda j: CB_A + 4 * j + 2)
    seg(4, lambda j: CB_A + 4 * j + 3)
    seg(2, lambda i: CB_CQ + i)
    seg(1, lambda i: CB_CKV)
    out.append((col, MLA_ROPE, CB_KPE))
    col += MLA_ROPE
    seg(4, lambda j: CB_BZ + j)
    seg(12, lambda c: CB_QKV + 3 * c)
    seg(12, lambda c: CB_QKV + 3 * c + 1)
    seg(12, lambda c: CB_QKV + 3 * c + 2)
    seg(4, lambda j: CB_CZ + j)
    seg(24, lambda i: CB_GATE + i)
    assert col == N_IN and sorted(c[2] for c in out) == list(range(NCB))
    return out


COLUMN_CHUNKS = _column_chunks()


def _pad_columns(shards):
    parts = []
    for start, width, _ in sorted(COLUMN_CHUNKS, key=lambda c: c[2]):
        fill = LANE - width
        while width:
            p, off = divmod(start, SHARD_COLS)
            n = min(width, SHARD_COLS - off)
            parts.append(shards[p, :, off:off + n])
            start, width = start + n, width - n
        if fill:
            parts.append(jnp.zeros((shards.shape[1], fill), shards.dtype))
    return jnp.concatenate(parts, axis=1)


def _unpad_columns(wp):
    pieces = [[] for _ in range(N_DEV)]
    for start, width, b in COLUMN_CHUNKS:
        src = b * LANE
        while width:
            p, off = divmod(start, SHARD_COLS)
            n = min(width, SHARD_COLS - off)
            pieces[p].append(wp[:, src:src + n])
            start, width, src = start + n, width - n, src + n
    return jnp.stack([jnp.concatenate(ps, axis=1) for ps in pieces])


def _cp():
    return pltpu.CompilerParams(vmem_limit_bytes=VMEM_LIMIT)


def _rstd(x, n):
    return lax.rsqrt(jnp.sum(x * x, axis=-1, keepdims=True) * (1.0 / n) + EPS)


def _sigmoid(z):
    return 1.0 / (1.0 + jnp.exp(-z))


def _silu(z):
    return z * _sigmoid(z)


def _dsilu(z):
    s = _sigmoid(z)
    return s * (1.0 + z * (1.0 - s))


def _mm(a, b):
    return jnp.dot(a.astype(BF16), b.astype(BF16), preferred_element_type=F32)


def _mm_nt(a, b):
    return lax.dot_general(a.astype(BF16), b.astype(BF16), (((1,), (1,)), ((), ())), preferred_element_type=F32)


def _mm_tn(a, b):
    return lax.dot_general(a.astype(BF16), b.astype(BF16), (((0,), (0,)), ((), ())), preferred_element_type=F32)


def _lane_lo(shape):
    return lax.broadcasted_iota(jnp.int32, shape, len(shape) - 1) < HALF


def _head_bcast_sum(x):
    lo = _lane_lo(x.shape)
    a = jnp.sum(jnp.where(lo, x, 0.0), axis=-1, keepdims=True)
    b = jnp.sum(jnp.where(lo, 0.0, x), axis=-1, keepdims=True)
    return jnp.where(lo, a, b)


def _rope(t, cos, sa, sb):
    return t * cos + pltpu.roll(t, LANE - 16, axis=1) * sa + pltpu.roll(t, 16, axis=1) * sb


def _rope_t(d, cos, sa, sb):
    return d * cos + pltpu.roll(d * sa, 16, axis=1) + pltpu.roll(d * sb, LANE - 16, axis=1)


def _shift_down(u, k):
    rows = lax.broadcasted_iota(jnp.int32, u.shape, 0)
    return jnp.where(rows >= k, pltpu.roll(u, k, axis=0), 0.0)


def _shift_up(u, k):
    n = u.shape[0]
    rows = lax.broadcasted_iota(jnp.int32, u.shape, 0)
    return jnp.where(rows < n - k, pltpu.roll(u, n - k, axis=0), 0.0)


def _tile(n, want):
    t = min(n, want)
    assert n % t == 0, (n, want)
    return t


def _inproj_fwd(x, g, wp):
    T = x.shape[0]
    tm, tn = _tile(T, 1024), 512

    def body(x_ref, g_ref, w_ref, proj_ref, h_ref):
        @pl.when(pl.program_id(1) == 0)
        def _():
            xv = x_ref[...]
            h_ref[...] = (xv * _rstd(xv, D_MODEL) * g_ref[...]).astype(BF16)

        proj_ref[...] = jnp.dot(h_ref[...], w_ref[...], preferred_element_type=F32)

    return pl.pallas_call(
        body, name="inproj_fwd", grid=(T // tm, PP // tn),
        in_specs=[pl.BlockSpec((tm, D_MODEL), lambda i, j: (i, 0)),
                  pl.BlockSpec((1, D_MODEL), lambda i, j: (0, 0)),
                  pl.BlockSpec((D_MODEL, tn), lambda i, j: (0, j))],
        out_specs=[pl.BlockSpec((tm, tn), lambda i, j: (i, j)),
                   pl.BlockSpec((tm, D_MODEL), lambda i, j: (i, 0))],
        out_shape=[jax.ShapeDtypeStruct((T, PP), F32), jax.ShapeDtypeStruct((T, D_MODEL), BF16)],
        compiler_params=_cp(),
    )(x, g, wp)


def _matmul_tn(a, b, name):
    T, K = a.shape
    N = b.shape[1]
    tt, tn = _tile(T, 512), _tile(N, 1024)

    def body(a_ref, b_ref, o_ref):
        @pl.when(pl.program_id(1) == 0)
        def _():
            o_ref[...] = jnp.zeros_like(o_ref)

        o_ref[...] += _mm_tn(a_ref[...], b_ref[...])

    return pl.pallas_call(
        body, name=name, grid=(N // tn, T // tt),
        in_specs=[pl.BlockSpec((tt, K), lambda j, k: (k, 0)),
                  pl.BlockSpec((tt, tn), lambda j, k: (k, j))],
        out_specs=pl.BlockSpec((K, tn), lambda j, k: (0, j)),
        out_shape=jax.ShapeDtypeStruct((K, N), F32),
        compiler_params=_cp(),
    )(a, b)


def _inproj_bwd_x(dproj, wp, x, g, dout):
    T = x.shape[0]
    tm, tk = _tile(T, 1024), 512
    nk = PP // tk

    def body(dp_ref, w_ref, x_ref, g_ref, do_ref, dx_ref, dg_ref, acc_ref):
        i, k = pl.program_id(0), pl.program_id(1)

        @pl.when(k == 0)
        def _():
            acc_ref[...] = jnp.zeros_like(acc_ref)

        @pl.when((k == 0) & (i == 0))
        def _():
            dg_ref[...] = jnp.zeros_like(dg_ref)

        acc_ref[...] += _mm_nt(dp_ref[...], w_ref[...])

        @pl.when(k == nk - 1)
        def _():
            dh = acc_ref[...]
            xv = x_ref[...]
            r = _rstd(xv, D_MODEL)
            gy = dh * g_ref[...]
            dot = jnp.sum(xv * gy, axis=-1, keepdims=True) * (1.0 / D_MODEL)
            dx_ref[...] = do_ref[...] + r * gy - xv * (r * r * r) * dot
            dg_ref[...] += jnp.sum(dh * xv * r, axis=0, keepdims=True)

    return pl.pallas_call(
        body, name="inproj_bwd_x", grid=(T // tm, nk),
        in_specs=[pl.BlockSpec((tm, tk), lambda i, k: (i, k)),
                  pl.BlockSpec((D_MODEL, tk), lambda i, k: (0, k)),
                  pl.BlockSpec((tm, D_MODEL), lambda i, k: (i, 0)),
                  pl.BlockSpec((1, D_MODEL), lambda i, k: (0, 0)),
                  pl.BlockSpec((tm, D_MODEL), lambda i, k: (i, 0))],
        out_specs=[pl.BlockSpec((tm, D_MODEL), lambda i, k: (i, 0)),
                   pl.BlockSpec((1, D_MODEL), lambda i, k: (0, 0))],
        out_shape=[jax.ShapeDtypeStruct((T, D_MODEL), F32), jax.ShapeDtypeStruct((1, D_MODEL), F32)],
        scratch_shapes=[pltpu.VMEM((tm, D_MODEL), F32)],
        compiler_params=_cp(),
    )(dproj, wp, x, g, dout)


def _mixa_fwd(proj, cw, cb, B, S):
    nc = CONV_WIDTH // LANE
    ca = CB_A // 4

    def body(p_ref, cw_ref, cb_ref, y_ref):
        ab, ac, ax, az = (p_ref[:, i * LANE:(i + 1) * LANE] for i in range(4))
        u = ac * ax
        conv = cb_ref[...] + cw_ref[0:1, :] * _shift_down(u, 2) + cw_ref[1:2, :] * _shift_down(u, 1) + cw_ref[2:3, :] * u
        y_ref[...] = (ab * conv * _silu(az)).astype(BF16)

    return pl.pallas_call(
        body, name="mixa_fwd", grid=(B, nc),
        in_specs=[pl.BlockSpec((S, 4 * LANE), lambda b, j: (b, ca + j)),
                  pl.BlockSpec((CONV_K, LANE), lambda b, j: (0, j)),
                  pl.BlockSpec((1, LANE), lambda b, j: (0, j))],
        out_specs=pl.BlockSpec((S, LANE), lambda b, j: (b, j)),
        out_shape=jax.ShapeDtypeStruct((B * S, CONV_WIDTH), BF16),
        compiler_params=_cp(),
    )(proj, cw, cb)


def _mixa_bwd(dproj, dy, proj, cw, cb, B, S):
    nc = CONV_WIDTH // LANE
    ca = CB_A // 4

    def body(dpin_ref, dy_ref, p_ref, cw_ref, cb_ref, dp_ref, st_ref):
        del dpin_ref
        ab, ac, ax, az = (p_ref[:, i * LANE:(i + 1) * LANE] for i in range(4))
        u = ac * ax
        u1, u2 = _shift_down(u, 1), _shift_down(u, 2)
        w0, w1, w2 = cw_ref[0:1, :], cw_ref[1:2, :], cw_ref[2:3, :]
        conv = cb_ref[...] + w0 * u2 + w1 * u1 + w2 * u
        s = _silu(az)
        d = dy_ref[...]
        dconv = d * ab * s
        du = w2 * dconv + w1 * _shift_up(dconv, 1) + w0 * _shift_up(dconv, 2)
        dp_ref[:, 0:LANE] = (d * conv * s).astype(BF16)
        dp_ref[:, LANE:2 * LANE] = (du * ax).astype(BF16)
        dp_ref[:, 2 * LANE:3 * LANE] = (du * ac).astype(BF16)
        dp_ref[:, 3 * LANE:4 * LANE] = (d * ab * conv * _dsilu(az)).astype(BF16)
        row = lax.broadcasted_iota(jnp.int32, (8, LANE), 0)
        st = jnp.zeros((8, LANE), F32)
        for r, v in enumerate((dconv * u2, dconv * u1, dconv * u, dconv)):
            st = st + jnp.where(row == r, jnp.sum(v, axis=0, keepdims=True), 0.0)

        @pl.when(pl.program_id(1) == 0)
        def _():
            st_ref[...] = st

        @pl.when(pl.program_id(1) != 0)
        def _():
            st_ref[...] += st

    return pl.pallas_call(
        body, name="mixa_bwd", grid=(nc, B),
        in_specs=[pl.BlockSpec(memory_space=pl.ANY),
                  pl.BlockSpec((S, LANE), lambda j, b: (b, j)),
                  pl.BlockSpec((S, 4 * LANE), lambda j, b: (b, ca + j)),
                  pl.BlockSpec((CONV_K, LANE), lambda j, b: (0, j)),
                  pl.BlockSpec((1, LANE), lambda j, b: (0, j))],
        out_specs=[pl.BlockSpec((S, 4 * LANE), lambda j, b: (b, ca + j)),
                   pl.BlockSpec((8, LANE), lambda j, b: (0, j))],
        out_shape=[jax.ShapeDtypeStruct(dproj.shape, BF16), jax.ShapeDtypeStruct((8, CONV_WIDTH), F32)],
        input_output_aliases={0: 0},
        compiler_params=_cp(),
    )(dproj, dy, proj, cw, cb)


def _mla_prep_fwd(proj, gq, gkv, wuqp, wkp, wv, gmq, gmk, cos, sa, sb, S):
    T = proj.shape[0]
    ts = _tile(S, 512)
    ns = S // ts
    W = MLA_HEADS * LANE

    def body(p_ref, gq_ref, gkv_ref, wuq_ref, wk_ref, wv_ref, gmq_ref, gmk_ref, cos_ref, sa_ref, sb_ref,
             q_ref, k_ref, v_ref):
        cq = p_ref[:, 0:2 * LANE]
        ckv = p_ref[:, 2 * LANE:3 * LANE]
        kpe = pltpu.roll(p_ref[:, 3 * LANE:4 * LANE], HALF, axis=1)
        cqn = cq * _rstd(cq, MLA_Q_LORA) * gq_ref[...]
        ckn = (ckv * _rstd(ckv, MLA_KV_LORA) * gkv_ref[...]).astype(BF16)
        q0 = _mm(cqn, wuq_ref[...])
        kn = _mm(ckn, wk_ref[...])
        v_ref[...] = _mm(ckn, wv_ref[...]).astype(BF16)
        c, a, b = cos_ref[...], sa_ref[...], sb_ref[...]
        for h in range(MLA_HEADS):
            q0h = q0[:, h * LANE:(h + 1) * LANE]
            q_ref[h] = _rope(q0h * _rstd(q0h, MLA_QK) * gmq_ref[...], c, a, b).astype(BF16)
            k0h = kn[:, h * LANE:(h + 1) * LANE] + kpe
            k_ref[h] = _rope(k0h * _rstd(k0h, MLA_QK) * gmk_ref[...], c, a, b).astype(BF16)

    def whole(r, c):
        return pl.BlockSpec((r, c), lambda i: (0, 0))

    tab = pl.BlockSpec((ts, LANE), lambda i: (i % ns, 0))
    return pl.pallas_call(
        body, name="mla_prep_fwd", grid=(T // ts,),
        in_specs=[pl.BlockSpec((ts, 4 * LANE), lambda i: (i, CB_CQ // 4)),
                  whole(1, MLA_Q_LORA), whole(1, MLA_KV_LORA), whole(MLA_Q_LORA, W), whole(MLA_KV_LORA, W),
                  whole(MLA_KV_LORA, MLA_HEADS * MLA_V), whole(1, LANE), whole(1, LANE), tab, tab, tab],
        out_specs=[pl.BlockSpec((MLA_HEADS, ts, LANE), lambda i: (0, i, 0)),
                   pl.BlockSpec((MLA_HEADS, ts, LANE), lambda i: (0, i, 0)),
                   pl.BlockSpec((ts, MLA_HEADS * MLA_V), lambda i: (i, 0))],
        out_shape=[jax.ShapeDtypeStruct((MLA_HEADS, T, LANE), BF16), jax.ShapeDtypeStruct((MLA_HEADS, T, LANE), BF16),
                   jax.ShapeDtypeStruct((T, MLA_HEADS * MLA_V), BF16)],
        compiler_params=_cp(),
    )(proj, gq, gkv, wuqp, wkp, wv, gmq, gmk, cos, sa, sb)


def _mla_prep_bwd(dproj, dq, dk, dv, proj, gq, gkv, wuqp, wkp, wv, gmq, gmk, cos, sa, sb, S):
    T = proj.shape[0]
    ts = _tile(S, 256)
    ns = S // ts
    W = MLA_HEADS * LANE

    def body(dpin_ref, dq_ref, dk_ref, dv_ref, p_ref, gq_ref, gkv_ref, wuq_ref, wk_ref, wv_ref, gmq_ref, gmk_ref,
             cos_ref, sa_ref, sb_ref,
             dp_ref, dwuq_ref, dwk_ref, dwv_ref, dgq_ref, dgkv_ref, dgmq_ref, dgmk_ref, dq0_ref, dkn_ref):
        del dpin_ref

        @pl.when(pl.program_id(0) == 0)
        def _():
            for r in (dwuq_ref, dwk_ref, dwv_ref, dgq_ref, dgkv_ref, dgmq_ref, dgmk_ref):
                r[...] = jnp.zeros_like(r)

        cq = p_ref[:, 0:2 * LANE]
        ckv = p_ref[:, 2 * LANE:3 * LANE]
        kpe = pltpu.roll(p_ref[:, 3 * LANE:4 * LANE], HALF, axis=1)
        rq = _rstd(cq, MLA_Q_LORA)
        rkv = _rstd(ckv, MLA_KV_LORA)
        gq, gkv, gmq, gmk = gq_ref[...], gkv_ref[...], gmq_ref[...], gmk_ref[...]
        cqn = (cq * rq * gq).astype(BF16)
        ckn = (ckv * rkv * gkv).astype(BF16)
        q0 = _mm(cqn, wuq_ref[...])
        kn = _mm(ckn, wk_ref[...])
        c, a, b = cos_ref[...], sa_ref[...], sb_ref[...]
        lane = lax.broadcasted_iota(jnp.int32, (ts, LANE), 1)
        dgmq = jnp.zeros((1, LANE), F32)
        dgmk = jnp.zeros((1, LANE), F32)
        dkpe = jnp.zeros((ts, LANE), F32)
        for h in range(MLA_HEADS):
            q0h = q0[:, h * LANE:(h + 1) * LANE]
            r = _rstd(q0h, MLA_QK)
            d1 = _rope_t(dq_ref[h], c, a, b)
            gy = d1 * gmq
            dq0_ref[:, h * LANE:(h + 1) * LANE] = (
                r * gy - q0h * (r * r * r) * (jnp.sum(q0h * gy, axis=-1, keepdims=True) * (1.0 / MLA_QK))).astype(BF16)
            dgmq = dgmq + jnp.sum(d1 * q0h * r, axis=0, keepdims=True)
            k0h = kn[:, h * LANE:(h + 1) * LANE] + kpe
            r = _rstd(k0h, MLA_QK)
            d1 = _rope_t(dk_ref[h], c, a, b)
            gy = d1 * gmk
            dk0 = r * gy - k0h * (r * r * r) * (jnp.sum(k0h * gy, axis=-1, keepdims=True) * (1.0 / MLA_QK))
            dgmk = dgmk + jnp.sum(d1 * k0h * r, axis=0, keepdims=True)
            dkn_ref[:, h * LANE:(h + 1) * LANE] = jnp.where(lane < MLA_NOPE, dk0, 0.0).astype(BF16)
            dkpe = dkpe + jnp.where((lane >= MLA_NOPE) & (lane < MLA_QK), dk0, 0.0)
        dq0 = dq0_ref[...]
        dkn = dkn_ref[...]
        dvv = dv_ref[...]
        dwuq_ref[...] += _mm_tn(cqn, dq0)
        dwk_ref[...] += _mm_tn(ckn, dkn)
        dwv_ref[...] += _mm_tn(ckn, dvv)
        dgmq_ref[...] += dgmq
        dgmk_ref[...] += dgmk
        dcqn = _mm_nt(dq0, wuq_ref[...])
        gy = dcqn * gq
        dp_ref[:, 0:2 * LANE] = (
            rq * gy - cq * (rq * rq * rq) * (jnp.sum(cq * gy, axis=-1, keepdims=True) * (1.0 / MLA_Q_LORA))).astype(BF16)
        dgq_ref[...] += jnp.sum(dcqn * cq * rq, axis=0, keepdims=True)
        dckn = _mm_nt(dkn, wk_ref[...]) + _mm_nt(dvv, wv_ref[...])
        gy = dckn * gkv
        dp_ref[:, 2 * LANE:3 * LANE] = (
            rkv * gy - ckv * (rkv * rkv * rkv) * (jnp.sum(ckv * gy, axis=-1, keepdims=True) * (1.0 / MLA_KV_LORA))).astype(BF16)
        dgkv_ref[...] += jnp.sum(dckn * ckv * rkv, axis=0, keepdims=True)
        dp_ref[:, 3 * LANE:4 * LANE] = pltpu.roll(dkpe, HALF, axis=1).astype(BF16)

    def whole(r, c):
        return pl.BlockSpec((r, c), lambda i: (0, 0))

    tab = pl.BlockSpec((ts, LANE), lambda i: (i % ns, 0))
    heads = pl.BlockSpec((MLA_HEADS, ts, LANE), lambda i: (0, i, 0))
    return pl.pallas_call(
        body, name="mla_prep_bwd", grid=(T // ts,),
        in_specs=[pl.BlockSpec(memory_space=pl.ANY), heads, heads,
                  pl.BlockSpec((ts, MLA_HEADS * MLA_V), lambda i: (i, 0)),
                  pl.BlockSpec((ts, 4 * LANE), lambda i: (i, CB_CQ // 4)),
                  whole(1, MLA_Q_LORA), whole(1, MLA_KV_LORA), whole(MLA_Q_LORA, W), whole(MLA_KV_LORA, W),
                  whole(MLA_KV_LORA, MLA_HEADS * MLA_V), whole(1, LANE), whole(1, LANE), tab, tab, tab],
        out_specs=[pl.BlockSpec((ts, 4 * LANE), lambda i: (i, CB_CQ // 4)),
                   whole(MLA_Q_LORA, W), whole(MLA_KV_LORA, W), whole(MLA_KV_LORA, MLA_HEADS * MLA_V),
                   whole(1, MLA_Q_LORA), whole(1, MLA_KV_LORA), whole(1, LANE), whole(1, LANE)],
        out_shape=[jax.ShapeDtypeStruct(dproj.shape, BF16),
                   jax.ShapeDtypeStruct((MLA_Q_LORA, W), F32), jax.ShapeDtypeStruct((MLA_KV_LORA, W), F32),
                   jax.ShapeDtypeStruct((MLA_KV_LORA, MLA_HEADS * MLA_V), F32),
                   jax.ShapeDtypeStruct((1, MLA_Q_LORA), F32), jax.ShapeDtypeStruct((1, MLA_KV_LORA), F32),
                   jax.ShapeDtypeStruct((1, LANE), F32), jax.ShapeDtypeStruct((1, LANE), F32)],
        scratch_shapes=[pltpu.VMEM((ts, W), BF16), pltpu.VMEM((ts, W), BF16)],
        input_output_aliases={0: 0},
        compiler_params=_cp(),
    )(dproj, dq, dk, dv, proj, gq, gkv, wuqp, wkp, wv, gmq, gmk, cos, sa, sb)


def _dil_prep_fwd(proj, gq, gk):
    T = proj.shape[0]
    ts = _tile(T, 512)
    nc = DIL_QK // LANE

    def body(p_ref, gq_ref, gk_ref, q_ref, k_ref):
        for i, (g_ref, o_ref) in enumerate(((gq_ref, q_ref), (gk_ref, k_ref))):
            t = p_ref[:, i * LANE:(i + 1) * LANE]
            r = lax.rsqrt(_head_bcast_sum(t * t) * (1.0 / DIL_HEAD_DIM) + EPS)
            o_ref[...] = t * r * g_ref[...]

    col = pl.BlockSpec((1, LANE), lambda i, c: (0, c))
    out = pl.BlockSpec((ts, LANE), lambda i, c: (i, c))
    return pl.pallas_call(
        body, name="dil_prep_fwd", grid=(T // ts, nc),
        in_specs=[pl.BlockSpec((ts, 3 * LANE), lambda i, c: (i, CB_QKV // 3 + c)), col, col],
        out_specs=[out, out],
        out_shape=[jax.ShapeDtypeStruct((T, DIL_QK), F32)] * 2,
        compiler_params=_cp(),
    )(proj, gq, gk)


def _dil_prep_bwd(dproj, ddq, ddk, ddv, proj, gq, gk):
    T = proj.shape[0]
    ts = _tile(T, 512)
    nc = DIL_QK // LANE

    def body(dpin_ref, ddq_ref, ddk_ref, ddv_ref, p_ref, gq_ref, gk_ref, dp_ref, dgq_ref, dgk_ref):
        del dpin_ref
        first = pl.program_id(1) == 0
        dp_ref[:, 2 * LANE:3 * LANE] = ddv_ref[...].astype(BF16)
        for i, (d_ref, g_ref, dg_ref) in enumerate(((ddq_ref, gq_ref, dgq_ref), (ddk_ref, gk_ref, dgk_ref))):
            t = p_ref[:, i * LANE:(i + 1) * LANE]
            r = lax.rsqrt(_head_bcast_sum(t * t) * (1.0 / DIL_HEAD_DIM) + EPS)
            d = d_ref[...]
            gy = d * g_ref[...]
            dp_ref[:, i * LANE:(i + 1) * LANE] = (
                r * gy - t * (r * r * r) * (_head_bcast_sum(t * gy) * (1.0 / DIL_HEAD_DIM))).astype(BF16)
            part = jnp.sum(d * t * r, axis=0, keepdims=True)

            @pl.when(first)
            def _():
                dg_ref[...] = part

            @pl.when(jnp.logical_not(first))
            def _():
                dg_ref[...] += part

    col = pl.BlockSpec((1, LANE), lambda c, i: (0, c))
    tok = pl.BlockSpec((ts, LANE), lambda c, i: (i, c))
    return pl.pallas_call(
        body, name="dil_prep_bwd", grid=(nc, T // ts),
        in_specs=[pl.BlockSpec(memory_space=pl.ANY), tok, tok, tok,
                  pl.BlockSpec((ts, 3 * LANE), lambda c, i: (i, CB_QKV // 3 + c)), col, col],
        out_specs=[pl.BlockSpec((ts, 3 * LANE), lambda c, i: (i, CB_QKV // 3 + c)), col, col],
        out_shape=[jax.ShapeDtypeStruct(dproj.shape, BF16), jax.ShapeDtypeStruct((1, DIL_QK), F32),
                   jax.ShapeDtypeStruct((1, DIL_QK), F32)],
        input_output_aliases={0: 0},
        compiler_params=_cp(),
    )(dproj, ddq, ddk, ddv, proj, gq, gk)


def _scores(qh, kh, q0, k0, t, scale, slope, window):
    s = lax.dot_general(qh, kh, (((1,), (1,)), ((), ())), preferred_element_type=F32) * scale
    dist = (q0 - k0) + lax.broadcasted_iota(jnp.int32, (t, t), 0) - lax.broadcasted_iota(jnp.int32, (t, t), 1)
    if slope is not None:
        s = s - slope * dist.astype(F32)
    valid = dist >= 0
    if window is not None:
        valid = valid & (dist <= window)
    return jnp.where(valid, s, NEG)


COPY_ROWS = 256


def _to_classes(src_ref, dst_ref, d, L):
    n = min(L, COPY_ROWS)
    for r in range(d):
        for c0 in range(0, L, n):
            rows = pl.ds(r + c0 * d, n, stride=d) if d > 1 else pl.ds(c0, n)
            dst_ref[r * L + c0:r * L + c0 + n, :] = src_ref[rows, :].astype(dst_ref.dtype)


def _from_classes(src_ref, dst_ref, d, L):
    n = min(L, COPY_ROWS)
    for r in range(d):
        for c0 in range(0, L, n):
            rows = pl.ds(r + c0 * d, n, stride=d) if d > 1 else pl.ds(c0, n)
            dst_ref[rows, :] = src_ref[r * L + c0:r * L + c0 + n, :].astype(dst_ref.dtype)


def _attn_fwd(name, slopes, q, k, v, out_shape, specs, B, *, packed, d, L, t, scale, window):
    has_bias = slopes is not None
    q_spec, k_spec, v_spec, o_spec = specs
    S, nq = d * L, L // t
    wb = None if window is None else -(-window // t)

    def body(*refs):
        refs = list(refs)
        sl_ref = refs.pop(0) if has_bias else None
        q_ref, k_ref, v_ref, o_ref, lse_ref = refs[:5]
        if packed:
            qs, ks, vs, os_, ls = refs[5:]
            for src, dst in ((q_ref, qs), (k_ref, ks), (v_ref, vs)):
                _to_classes(src, dst, d, L)
        else:
            vs, os_, ls = v_ref, o_ref, lse_ref
        j = pl.program_id(1)
        lo = _lane_lo((t, LANE))

        def block(g, _):
            qb = g % nq if d > 1 else g
            nk = qb if wb is None else jnp.minimum(qb, wb)
            row0 = pl.multiple_of(g * t, t)
            rows = pl.ds(row0, t)
            res = []
            for hh in range(2):
                sel = lo if hh == 0 else jnp.logical_not(lo)
                qh = jnp.where(sel, qs[rows, :], jnp.zeros((), BF16)) if packed else q_ref[hh, rows, :]
                slope = sl_ref[j, hh] if has_bias else None

                def step(i, carry, hh=hh, qh=qh, slope=slope):
                    m, l, acc = carry
                    off = pl.multiple_of((g - nk + i) * t, t)
                    kh = ks[pl.ds(off, t), :] if packed else k_ref[hh, pl.ds(off, t), :]
                    s = _scores(qh, kh, row0, off, t, scale, slope, window)
                    m_new = jnp.maximum(m, jnp.max(s, axis=-1, keepdims=True))
                    p = jnp.exp(s - m_new)
                    a = jnp.exp(m - m_new)
                    l = a * l + jnp.sum(p, axis=-1, keepdims=True)
                    acc = a * acc + _mm(p, vs[pl.ds(off, t), :])
                    return m_new, l, acc

                m, l, acc = lax.fori_loop(0, nk + 1, step, (jnp.full((t, 1), NEG, F32), jnp.zeros((t, 1), F32),
                                                            jnp.zeros((t, LANE), F32)))
                res.append((acc / l, m + jnp.log(l)))
            os_[rows, :] = jnp.where(lo, res[0][0], res[1][0])
            ls[rows, :] = jnp.where(lo, res[0][1], res[1][1])
            return 0

        lax.fori_loop(0, d * nq, block, 0)
        if packed:
            _from_classes(os_, o_ref, d, L)
            _from_classes(ls, lse_ref, d, L)

    in_specs = [q_spec, k_spec, v_spec]
    args = [q, k, v]
    if has_bias:
        in_specs = [pl.BlockSpec(memory_space=pltpu.SMEM)] + in_specs
        args = [slopes] + args
    scratch = [pltpu.VMEM((S, LANE), BF16)] * 3 + [pltpu.VMEM((S, LANE), F32)] * 2 if packed else []
    return pl.pallas_call(
        body, name=name, grid=(B, NPAIR), in_specs=in_specs, out_specs=[o_spec, o_spec],
        out_shape=[jax.ShapeDtypeStruct(out_shape, F32)] * 2, scratch_shapes=scratch,
        compiler_params=_cp(),
    )(*args)


def _attn_bwd(name, slopes, q, k, v, do, lse, delta, through, outs, specs, B, *, packed, d, L, t, scale, window):
    has_bias = slopes is not None
    q_spec, k_spec, v_spec, do_spec, dq_spec, dk_spec, dv_spec = specs
    S, nq = d * L, L // t
    wb = None if window is None else -(-window // t)

    def body(*refs):
        refs = list(refs)
        sl_ref = refs.pop(0) if has_bias else None
        q_ref, k_ref, v_ref, do_ref, lse_ref, dl_ref = refs[:6]
        refs = refs[6 + (0 if through is None else 3):]
        dq_ref, dk_ref, dv_ref = refs[:3]
        if packed:
            qs, ks, vs, dos, lss, dls, dqs, dks, dvs = refs[3:]
            for src, dst in ((q_ref, qs), (k_ref, ks), (v_ref, vs), (do_ref, dos), (lse_ref, lss), (dl_ref, dls)):
                _to_classes(src, dst, d, L)
        else:
            dks, dvs = refs[3:]
            vs, dos, lss, dls = v_ref, do_ref, lse_ref, dl_ref
        dks[...] = jnp.zeros_like(dks)
        dvs[...] = jnp.zeros_like(dvs)
        j = pl.program_id(1)
        lo = _lane_lo((t, LANE))

        def block(g, _):
            qb = g % nq if d > 1 else g
            nk = qb if wb is None else jnp.minimum(qb, wb)
            row0 = pl.multiple_of(g * t, t)
            rows = pl.ds(row0, t)
            dqt = []
            for hh in range(2):
                sel = lo if hh == 0 else jnp.logical_not(lo)
                slope = sl_ref[j, hh] if has_bias else None
                qh = jnp.where(sel, qs[rows, :], jnp.zeros((), BF16)) if packed else q_ref[hh, rows, :]
                doh = jnp.where(sel, dos[rows, :], jnp.zeros((), BF16))
                lse_h = jnp.max(jnp.where(sel, lss[rows, :], NEG), axis=-1, keepdims=True)
                dl_h = jnp.max(jnp.where(sel, dls[rows, :], NEG), axis=-1, keepdims=True)

                def step(i, dq_acc, hh=hh, qh=qh, doh=doh, lse_h=lse_h, dl_h=dl_h, slope=slope):
                    off = pl.multiple_of((g - nk + i) * t, t)
                    cols = pl.ds(off, t)
                    kh = ks[cols, :] if packed else k_ref[hh, cols, :]
                    s = _scores(qh, kh, row0, off, t, scale, slope, window)
                    p = jnp.exp(s - lse_h)
                    dp = _mm_nt(doh, vs[cols, :])
                    ds = (p * (dp - dl_h) * scale).astype(BF16)
                    if packed:
                        dks[cols, :] += _mm_tn(ds, qh)
                    else:
                        dks[hh, cols, :] += _mm_tn(ds, qh)
                    dvs[cols, :] += _mm_tn(p, doh)
                    return dq_acc + _mm(ds, kh)

                dqt.append(lax.fori_loop(0, nk + 1, step, jnp.zeros((t, LANE), F32)))
            if packed:
                dqs[rows, :] = jnp.where(lo, dqt[0], dqt[1])
            else:
                dq_ref[0, rows, :] = dqt[0]
                dq_ref[1, rows, :] = dqt[1]
            return 0

        lax.fori_loop(0, d * nq, block, 0)
        if packed:
            for src, dst in ((dqs, dq_ref), (dks, dk_ref), (dvs, dv_ref)):
                _from_classes(src, dst, d, L)
        else:
            dk_ref[...] = dks[...]
            dv_ref[...] = dvs[...]

    in_specs = [q_spec, k_spec, v_spec, do_spec, do_spec, do_spec]
    args = [q, k, v, do, lse, delta]
    if has_bias:
        in_specs = [pl.BlockSpec(memory_space=pltpu.SMEM)] + in_specs
        args = [slopes] + args
    aliases = {}
    if through is not None:
        aliases = {len(args) + i: i for i in range(3)}
        in_specs = in_specs + [pl.BlockSpec(memory_space=pl.ANY)] * 3
        args = args + list(through)
    if packed:
        scratch = [pltpu.VMEM((S, LANE), BF16)] * 4 + [pltpu.VMEM((S, LANE), F32)] * 5
    else:
        scratch = [pltpu.VMEM((2, S, LANE), F32), pltpu.VMEM((S, LANE), F32)]
    return pl.pallas_call(
        body, name=name, grid=(B, NPAIR), in_specs=in_specs, out_specs=[dq_spec, dk_spec, dv_spec],
        out_shape=list(outs), scratch_shapes=scratch, input_output_aliases=aliases,
        compiler_params=_cp(),
    )(*args)


def _mla_specs(S):
    heads = pl.BlockSpec((2, S, LANE), lambda b, j: (j, b, 0))
    pair = pl.BlockSpec((S, LANE), lambda b, j: (b, j))
    return heads, pair


def _mla_attn_fwd(q, k, v, B, S):
    heads, pair = _mla_specs(S)
    return _attn_fwd("mla_attn_fwd", None, q, k, v, (B * S, MLA_HEADS * MLA_V), (heads, heads, pair, pair), B,
                     packed=False, d=1, L=S, t=_tile(S, 256), scale=MLA_QK ** -0.5, window=None)


def _mla_attn_bwd(q, k, v, do, lse, delta, B, S):
    T = B * S
    heads, pair = _mla_specs(S)
    outs = (jax.ShapeDtypeStruct((MLA_HEADS, T, LANE), F32), jax.ShapeDtypeStruct((MLA_HEADS, T, LANE), F32),
            jax.ShapeDtypeStruct((T, MLA_HEADS * MLA_V), F32))
    return _attn_bwd("mla_attn_bwd", None, q, k, v, do, lse, delta, None, outs,
                     (heads, heads, pair, pair, heads, heads, pair), B,
                     packed=False, d=1, L=S, t=_tile(S, 256), scale=MLA_QK ** -0.5, window=None)


def _dil_geometry(gi, S):
    span, d = DIL_PATTERNS[gi]
    L = S // d
    return dict(d=d, L=L, t=_tile(L, 128), window=span // d, scale=DIL_HEAD_DIM ** -0.5, packed=True)


def _dil_specs(gi, S):
    qk = pl.BlockSpec((S, LANE), lambda b, j: (b, NPAIR * gi + j))
    v = pl.BlockSpec((S, LANE), lambda b, j: (b, CB_QKV + 3 * (NPAIR * gi + j) + 2))
    pair = pl.BlockSpec((S, LANE), lambda b, j: (b, j))
    return qk, v, pair


def _dil_attn_fwd(gi, slopes, qn, kn, proj, B, S):
    qk, v, pair = _dil_specs(gi, S)
    return _attn_fwd(f"dil_attn_fwd_{gi}", slopes, qn, kn, proj, (B * S, DIL_WIDTH), (qk, qk, v, pair), B,
                     **_dil_geometry(gi, S))


def _dil_attn_bwd(gi, slopes, qn, kn, proj, do, lse, delta, through, B, S):
    qk, v, pair = _dil_specs(gi, S)
    outs = [jax.ShapeDtypeStruct((B * S, DIL_QK), F32)] * 3
    return _attn_bwd(f"dil_attn_bwd_{gi}", slopes, qn, kn, proj, do, lse, delta, through, outs,
                     (qk, qk, v, pair, qk, qk, qk), B, **_dil_geometry(gi, S))


def _merge_common(p_ref, bg_ref, ob_ref, og_refs, lse_refs):
    bz = p_ref[:, CB_BZ * LANE:(CB_BZ + 4) * LANE]
    cz = p_ref[:, CB_CZ * LANE:(CB_CZ + 4) * LANE]
    gates = [_sigmoid(p_ref[:, (CB_GATE + 8 * i) * LANE:(CB_GATE + 8 * i + 8) * LANE]
                      + bg_ref[:, i * D_MODEL:(i + 1) * D_MODEL]) for i in range(3)]
    ob = ob_ref[...]
    lses = [r[...] for r in lse_refs]
    mx = jnp.maximum(jnp.maximum(lses[0], lses[1]), lses[2])
    es = [jnp.exp(v - mx) for v in lses]
    inv = 1.0 / (es[0] + es[1] + es[2])
    alphas = [e * inv for e in es]
    oc = alphas[0] * og_refs[0][...] + alphas[1] * og_refs[1][...] + alphas[2] * og_refs[2][...]
    return bz, cz, gates, ob, alphas, oc


def _merge_fwd(x, proj, b_gate, ya, ob, ogs, lses, woa, wob, woc, wo):
    T = x.shape[0]
    ts = _tile(T, 256)
    MW = 32 * LANE

    def body(x_ref, p_ref, bg_ref, ya_ref, ob_ref, og0, og1, og2, l0, l1, l2, woa_ref, wob_ref, woc_ref, wo_ref, out_ref):
        bz, cz, gates, obv, alphas, oc = _merge_common(p_ref, bg_ref, ob_ref, (og0, og1, og2), (l0, l1, l2))
        yb = obv * _silu(bz)
        yc = oc * _silu(cz)
        merged = (gates[0] * _mm(ya_ref[...], woa_ref[...]) + gates[1] * _mm(yb, wob_ref[...])
                  + gates[2] * _mm(yc, woc_ref[...]))
        out_ref[...] = x_ref[...] + _mm(merged, wo_ref[...])

    def whole(r, c):
        return pl.BlockSpec((r, c), lambda i: (0, 0))

    tok = lambda w: pl.BlockSpec((ts, w), lambda i: (i, 0))
    return pl.pallas_call(
        body, name="merge_fwd", grid=(T // ts,),
        in_specs=[tok(D_MODEL), tok(MW), whole(1, 3 * D_MODEL), tok(CONV_WIDTH)] + [tok(DIL_WIDTH)] * 7
                 + [whole(CONV_WIDTH, D_MODEL)] * 3 + [whole(D_MODEL, D_MODEL)],
        out_specs=tok(D_MODEL),
        out_shape=jax.ShapeDtypeStruct((T, D_MODEL), F32),
        compiler_params=_cp(),
    )(x, proj, b_gate, ya, ob, *ogs, *lses, woa, wob, woc, wo)


def _merge_bwd(dout, proj, b_gate, ya, ob, ogs, lses, woa, wob, woc, wo):
    T = dout.shape[0]
    ts = _tile(T, 256)
    MW = 32 * LANE

    def body(do_ref, p_ref, bg_ref, ya_ref, ob_ref, og0, og1, og2, l0, l1, l2, woa_ref, wob_ref, woc_ref, wo_ref,
             dp_ref, dya_ref, dob_ref, dlb_ref, dg0, dg1, dg2, dl0, dl1, dl2,
             mg_ref, dpa_ref, dpb_ref, dpc_ref, yb_ref, yc_ref, dbg_ref):
        bz, cz, gates, obv, alphas, oc = _merge_common(p_ref, bg_ref, ob_ref, (og0, og1, og2), (l0, l1, l2))
        sb, sc = _silu(bz), _silu(cz)
        yb = obv * sb
        yc = oc * sc
        ps = [_mm(ya_ref[...], woa_ref[...]), _mm(yb, wob_ref[...]), _mm(yc, woc_ref[...])]
        mg_ref[...] = (gates[0] * ps[0] + gates[1] * ps[1] + gates[2] * ps[2]).astype(BF16)
        yb_ref[...] = yb.astype(BF16)
        yc_ref[...] = yc.astype(BF16)
        dm = _mm_nt(do_ref[...], wo_ref[...])
        dps = []
        first = pl.program_id(0) == 0
        for i, dref in enumerate((dpa_ref, dpb_ref, dpc_ref)):
            g = gates[i]
            dpi = (dm * g).astype(BF16)
            dref[...] = dpi
            dps.append(dpi)
            dgp = dm * ps[i] * g * (1.0 - g)
            dp_ref[:, (CB_GATE + 8 * i) * LANE:(CB_GATE + 8 * i + 8) * LANE] = dgp.astype(BF16)
            part = jnp.sum(dgp, axis=0, keepdims=True)

            @pl.when(first)
            def _():
                dbg_ref[:, i * D_MODEL:(i + 1) * D_MODEL] = part

            @pl.when(jnp.logical_not(first))
            def _():
                dbg_ref[:, i * D_MODEL:(i + 1) * D_MODEL] += part

        dya_ref[...] = _mm_nt(dps[0], woa_ref[...])
        dyb = _mm_nt(dps[1], wob_ref[...])
        dyc = _mm_nt(dps[2], woc_ref[...])
        dp_ref[:, CB_BZ * LANE:(CB_BZ + 4) * LANE] = (dyb * obv * _dsilu(bz)).astype(BF16)
        dp_ref[:, CB_CZ * LANE:(CB_CZ + 4) * LANE] = (dyc * oc * _dsilu(cz)).astype(BF16)
        dob = dyb * sb
        doc = dyc * sc
        dob_ref[...] = dob.astype(BF16)
        for c in range(NPAIR):
            cs = slice(c * LANE, (c + 1) * LANE)
            dlb_ref[:, cs] = _head_bcast_sum(dob[:, cs] * obv[:, cs])
            dd = _head_bcast_sum(doc[:, cs] * oc[:, cs])
            for a, dref, lref in zip(alphas, (dg0, dg1, dg2), (dl0, dl1, dl2)):
                dref[:, cs] = a[:, cs] * doc[:, cs]
                lref[:, cs] = a[:, cs] * dd

    def whole(r, c):
        return pl.BlockSpec((r, c), lambda i: (0, 0))

    tok = lambda w: pl.BlockSpec((ts, w), lambda i: (i, 0))
    sd = jax.ShapeDtypeStruct
    W = DIL_WIDTH
    return pl.pallas_call(
        body, name="merge_bwd", grid=(T // ts,),
        in_specs=[tok(D_MODEL), tok(MW), whole(1, 3 * D_MODEL), tok(CONV_WIDTH)] + [tok(W)] * 7
                 + [whole(CONV_WIDTH, D_MODEL)] * 3 + [whole(D_MODEL, D_MODEL)],
        out_specs=[tok(MW), tok(CONV_WIDTH), tok(W), tok(W)] + [tok(W)] * 6
                  + [tok(D_MODEL)] * 4 + [tok(W), tok(W), whole(1, 3 * D_MODEL)],
        out_shape=[sd((T, PP), BF16), sd((T, CONV_WIDTH), F32), sd((T, W), BF16), sd((T, W), F32)]
                  + [sd((T, W), F32)] * 6
                  + [sd((T, D_MODEL), BF16)] * 4 + [sd((T, W), BF16)] * 2 + [sd((1, 3 * D_MODEL), F32)],
        compiler_params=_cp(),
    )(dout, proj, b_gate, ya, ob, *ogs, *lses, woa, wob, woc, wo)


def _loss_head(y, target):
    T = y.shape[0]
    ts = _tile(T, 512)

    def body(y_ref, t_ref, d_ref, l_ref):
        e = y_ref[...] - t_ref[...]
        d_ref[...] = e * (1.0 / D_MODEL)
        l_ref[...] = jnp.zeros((1, 8, LANE), F32) + jnp.sum(e * e)

    tok = pl.BlockSpec((ts, D_MODEL), lambda i: (i, 0))
    return pl.pallas_call(
        body, name="loss_head", grid=(T // ts,), in_specs=[tok, tok],
        out_specs=[tok, pl.BlockSpec((1, 8, LANE), lambda i: (i, 0, 0))],
        out_shape=[jax.ShapeDtypeStruct((T, D_MODEL), F32), jax.ShapeDtypeStruct((T // ts, 8, LANE), F32)],
        compiler_params=_cp(),
    )(y, target)


def _my_index():
    return 4 * lax.axis_index("x") + 2 * lax.axis_index("y") + lax.axis_index("c")


def _peers():
    x, y, c = (lax.axis_index(a) for a in AXES)
    out = []
    for kk in range(1, N_DEV):
        px = 1 - x if kk & 4 else x
        py = 1 - y if kk & 2 else y
        pc = 1 - c if kk & 1 else c
        out.append(((px, py, pc), 4 * px + 2 * py + pc))
    return out


def _exchange(arrays, name, gather):
    n = len(arrays)

    def body(*refs):
        srcs, outs = refs[:n], refs[n:2 * n]
        send_sems, recv_sems, local_sems = refs[2 * n:]
        me = _my_index()
        peers = _peers()
        started = []
        for a, (src, out) in enumerate(zip(srcs, outs)):
            mine = pltpu.make_async_copy(src if gather else src.at[me], out.at[me], local_sems.at[a])
            mine.start()
            started.append(mine)
        sends = []
        for i, (pos, idx) in enumerate(peers):
            for a, (src, out) in enumerate(zip(srcs, outs)):
                cp = pltpu.make_async_remote_copy(
                    src_ref=src if gather else src.at[idx], dst_ref=out.at[me], send_sem=send_sems.at[a, i],
                    recv_sem=recv_sems.at[a, i], device_id=pos, device_id_type=pl.DeviceIdType.MESH)
                cp.start()
                sends.append(cp)
        for i, (pos, idx) in enumerate(peers):
            for a, (src, out) in enumerate(zip(srcs, outs)):
                pltpu.make_async_remote_copy(
                    src_ref=src if gather else src.at[idx], dst_ref=out.at[idx], send_sem=send_sems.at[a, i],
                    recv_sem=recv_sems.at[a, i], device_id=pos, device_id_type=pl.DeviceIdType.MESH).wait_recv()
        for cp in sends:
            cp.wait_send()
        for mine in started:
            mine.wait()

    any_space = pl.BlockSpec(memory_space=pl.ANY)
    return pl.pallas_call(
        body, name=name, in_specs=[any_space] * n, out_specs=[any_space] * n,
        out_shape=[jax.ShapeDtypeStruct(((N_DEV,) + a.shape) if gather else a.shape, a.dtype) for a in arrays],
        scratch_shapes=[pltpu.SemaphoreType.DMA((n, N_DEV - 1)), pltpu.SemaphoreType.DMA((n, N_DEV - 1)),
                        pltpu.SemaphoreType.DMA((n,))],
    )(*arrays)


def _adamw(w, g, m, v):
    m = ADAM_B1 * m + (1.0 - ADAM_B1) * g
    v = ADAM_B2 * v + (1.0 - ADAM_B2) * (g * g)
    m_hat = m / (1.0 - ADAM_B1 ** ADAM_STEP)
    v_hat = v / (1.0 - ADAM_B2 ** ADAM_STEP)
    delta = -ADAM_LR * (m_hat / (jnp.sqrt(v_hat) + ADAM_EPS) + ADAM_WD * w)
    return delta, m, v


def _reduce_adamw(parts, w, m, v, name):
    R, C = w.shape
    tr = R
    while N_DEV * tr * C * parts.dtype.itemsize > REDUCE_BLOCK_BYTES and tr % 32 == 0:
        tr //= 2

    def body(p_ref, w_ref, m_ref, v_ref, g_ref, d_ref, nm_ref, nv_ref):
        g = p_ref[0].astype(F32)
        for s in range(1, N_DEV):
            g = g + p_ref[s].astype(F32)
        g_ref[...] = g
        d_ref[...], nm_ref[...], nv_ref[...] = _adamw(w_ref[...], g, m_ref[...], v_ref[...])

    row = pl.BlockSpec((tr, C), lambda i: (i, 0))
    return pl.pallas_call(
        body, name=name, grid=(R // tr,),
        in_specs=[pl.BlockSpec((N_DEV, tr, C), lambda i: (0, i, 0)), row, row, row],
        out_specs=[row] * 4, out_shape=[jax.ShapeDtypeStruct((R, C), F32)] * 4,
        compiler_params=_cp(),
    )(parts, w, m, v)


BIG = ("w_in", "w_uq", "w_ukv", "w_out_a", "w_out_b", "w_out_c", "w_o")
SMALL = ("norm_g", "b_gate", "conv_w", "conv_b", "q_a_norm_g", "kv_a_norm_g", "mla_q_norm_g", "mla_k_norm_g",
         "dil_q_norm_g", "dil_k_norm_g")
PACK_ROWS = 128
REDUCE_BLOCK_BYTES = 6 * 1024 * 1024


def _pack_local(tensors):
    flat = jnp.concatenate([t.reshape(-1) for t in tensors])
    pad = (-flat.shape[0]) % (PACK_ROWS * LANE)
    return jnp.concatenate([flat, jnp.zeros((pad,), flat.dtype)]).reshape(-1, LANE)


def _unpack_local(rows, like):
    flat = rows.reshape(-1)
    out, off = [], 0
    for t in like:
        out.append(flat[off:off + t.size].reshape(t.shape))
        off += t.size
    return out


def _cols_to_slots(a):
    k = a.shape[0]
    return a.reshape(k, N_DEV, -1).transpose(1, 0, 2)


def _slots_to_cols(s):
    return s.transpose(1, 0, 2).reshape(s.shape[1], -1)


def _rope_tables(S):
    inv = ROPE_THETA ** (-jnp.arange(0, MLA_ROPE, 2, dtype=F32) / MLA_ROPE)
    ang = jnp.arange(S, dtype=F32)[:, None] * inv[None, :]
    cos, sin = jnp.cos(ang), jnp.sin(ang)
    one = jnp.ones((S, MLA_NOPE), F32)
    z16, z32, z64 = (jnp.zeros((S, n), F32) for n in (16, 32, 64))
    cosp = jnp.concatenate([one, cos, cos, jnp.ones((S, 32), F32)], axis=1)
    sa = jnp.concatenate([z64, -sin, z16, z32], axis=1)
    sb = jnp.concatenate([z64, z16, sin, z32], axis=1)
    return cosp, sa, sb


def _alibi_slopes():
    n = DIL_GROUPS * DIL_HEADS
    m = 2.0 ** (-8.0 * jnp.arange(1, n + 1, dtype=F32) / n)
    return m.reshape(DIL_GROUPS, NPAIR, 2)


def _pad_slots(s):
    n, k, c = s.shape
    return _slots_to_cols(jnp.concatenate([s, jnp.zeros((n, k, LANE - c), s.dtype)], axis=2))


def _layer_params(gw, small, l):
    p = {}
    p["wp"] = _pad_columns(gw["w_in"][:, l])
    p["norm_g"] = small["norm_g"][l][None]
    p["b_gate"] = small["b_gate"][l][None]
    p["conv_w"] = gw["conv_w"][:, l].transpose(1, 0, 2).reshape(CONV_K, CONV_WIDTH)
    p["conv_b"] = small["conv_b"][l][None]
    p["gq"] = small["q_a_norm_g"][l][None]
    p["gkv"] = small["kv_a_norm_g"][l][None]
    p["wuqp"] = _pad_slots(gw["w_uq"][:, l])
    kv = gw["w_ukv"][:, l]
    p["wkp"] = _pad_slots(kv[:, :, :MLA_NOPE])
    p["wv"] = kv[:, :, MLA_NOPE:].transpose(1, 0, 2).reshape(MLA_KV_LORA, MLA_HEADS * MLA_V)
    zpad = jnp.zeros((1, LANE - MLA_QK), F32)
    p["gmq"] = jnp.concatenate([small["mla_q_norm_g"][l][None], zpad], axis=1)
    p["gmk"] = jnp.concatenate([small["mla_k_norm_g"][l][None], zpad], axis=1)
    tile = lambda g: jnp.broadcast_to(g[:, None, :], (DIL_GROUPS, DIL_HEADS, DIL_HEAD_DIM)).reshape(1, DIL_QK)
    p["gdq"] = tile(small["dil_q_norm_g"][l])
    p["gdk"] = tile(small["dil_k_norm_g"][l])
    p["woa"], p["wob"], p["woc"] = (_slots_to_cols(gw[n][:, l]) for n in ("w_out_a", "w_out_b", "w_out_c"))
    p["wo"] = gw["w_o"][:, l].reshape(D_MODEL, D_MODEL)
    return p


def _layer_fwd(x, p, tabs, slopes, B, S):
    proj, h = _inproj_fwd(x, p["norm_g"], p["wp"])
    ya = _mixa_fwd(proj, p["conv_w"], p["conv_b"], B, S)
    q, k, v = _mla_prep_fwd(proj, p["gq"], p["gkv"], p["wuqp"], p["wkp"], p["wv"], p["gmq"], p["gmk"], *tabs, S)
    ob, lse_b = _mla_attn_fwd(q, k, v, B, S)
    qn, kn = _dil_prep_fwd(proj, p["gdq"], p["gdk"])
    ogs, lses = [], []
    for gi in range(DIL_GROUPS):
        o, lse = _dil_attn_fwd(gi, slopes[gi], qn, kn, proj, B, S)
        ogs.append(o)
        lses.append(lse)
    out = _merge_fwd(x, proj, p["b_gate"], ya, ob, ogs, lses, p["woa"], p["wob"], p["woc"], p["wo"])
    saved = dict(x=x, proj=proj, h=h, ya=ya, q=q, k=k, v=v, ob=ob, lse_b=lse_b, qn=qn, kn=kn, ogs=ogs, lses=lses)
    return out, saved


def _layer_bwd(dout, sv, p, tabs, slopes, B, S):
    proj = sv["proj"]
    (dproj, dya, dob, dlb, dg0, dg1, dg2, dl0, dl1, dl2, merged, dpa, dpb, dpc, yb, yc, dbg) = _merge_bwd(
        dout, proj, p["b_gate"], sv["ya"], sv["ob"], sv["ogs"], sv["lses"], p["woa"], p["wob"], p["woc"], p["wo"])
    g = {}
    g["w_o"] = _matmul_tn(merged, dout, "dw_o").reshape(N_DEV, D_MODEL // N_DEV, D_MODEL)
    g["w_out_a"] = _cols_to_slots(_matmul_tn(sv["ya"], dpa, "dw_out_a"))
    g["w_out_b"] = _cols_to_slots(_matmul_tn(yb, dpb, "dw_out_b"))
    g["w_out_c"] = _cols_to_slots(_matmul_tn(yc, dpc, "dw_out_c"))
    g["b_gate"] = dbg[0]
    dproj, st = _mixa_bwd(dproj, dya, proj, p["conv_w"], p["conv_b"], B, S)
    g["conv_w"] = st[0:CONV_K]
    g["conv_b"] = st[CONV_K]
    dq, dk, dv = _mla_attn_bwd(sv["q"], sv["k"], sv["v"], dob, sv["lse_b"], dlb, B, S)
    dproj, dwuqp, dwkp, dwv, dgq, dgkv, dgmq, dgmk = _mla_prep_bwd(
        dproj, dq, dk, dv, proj, p["gq"], p["gkv"], p["wuqp"], p["wkp"], p["wv"], p["gmq"], p["gmk"], *tabs, S)
    g["w_uq"] = _cols_to_slots(dwuqp)[:, :, :MLA_QK]
    g["w_ukv"] = jnp.concatenate([_cols_to_slots(dwkp)[:, :, :MLA_NOPE], _cols_to_slots(dwv)], axis=2)
    g["q_a_norm_g"], g["kv_a_norm_g"] = dgq[0], dgkv[0]
    g["mla_q_norm_g"], g["mla_k_norm_g"] = dgmq[0, :MLA_QK], dgmk[0, :MLA_QK]
    dqkv = None
    for gi, (dog, dlg) in enumerate(((dg0, dl0), (dg1, dl1), (dg2, dl2))):
        dqkv = _dil_attn_bwd(gi, slopes[gi], sv["qn"], sv["kn"], proj, dog, sv["lses"][gi], dlg, dqkv, B, S)
    dproj, dgdq, dgdk = _dil_prep_bwd(dproj, *dqkv, proj, p["gdq"], p["gdk"])
    g["dil_q_norm_g"] = dgdq.reshape(DIL_GROUPS, DIL_HEADS, DIL_HEAD_DIM).sum(axis=1)
    g["dil_k_norm_g"] = dgdk.reshape(DIL_GROUPS, DIL_HEADS, DIL_HEAD_DIM).sum(axis=1)
    g["w_in"] = _unpad_columns(_matmul_tn(sv["h"], dproj, "dw_in"))
    dx, dng = _inproj_bwd_x(dproj, p["wp"], sv["x"], p["norm_g"], dout)
    g["norm_g"] = dng[0]
    return dx, g


def _local_step(x, target, gw, small, B, S):
    tabs = _rope_tables(S)
    sl = _alibi_slopes()
    slopes = [sl[gi] * float(DIL_PATTERNS[gi][1]) for gi in range(DIL_GROUPS)]
    params = [_layer_params(gw, small, l) for l in range(DEPTH)]
    saved = []
    for l in range(DEPTH):
        x, sv = _layer_fwd(x, params[l], tabs, slopes, B, S)
        saved.append(sv)
    dout, lparts = _loss_head(x, target)
    sq = jnp.sum(lparts[:, 0, 0])
    grads = [None] * DEPTH
    for l in reversed(range(DEPTH)):
        dout, grads[l] = _layer_bwd(dout, saved[l], params[l], tabs, slopes, B, S)
    part = {n: jnp.stack([grads[l][n] for l in range(DEPTH)], axis=1) for n in BIG}
    part.update({n: jnp.stack([grads[l][n] for l in range(DEPTH)]) for n in SMALL})
    return sq, dout, part


def kernel(x, norm_g, w_in, b_gate, conv_w, conv_b, q_a_norm_g, w_uq, kv_a_norm_g, w_ukv, mla_q_norm_g, mla_k_norm_g, dil_q_norm_g, dil_k_norm_g, w_out_a, w_out_b, w_out_c, w_o, loss_target, m_norm_g, m_w_in, m_b_gate, m_conv_w, m_conv_b, m_q_a_norm_g, m_w_uq, m_kv_a_norm_g, m_w_ukv, m_mla_q_norm_g, m_mla_k_norm_g, m_dil_q_norm_g, m_dil_k_norm_g, m_w_out_a, m_w_out_b, m_w_out_c, m_w_o, v_norm_g, v_w_in, v_b_gate, v_conv_w, v_conv_b, v_q_a_norm_g, v_w_uq, v_kv_a_norm_g, v_w_ukv, v_mla_q_norm_g, v_mla_k_norm_g, v_dil_q_norm_g, v_dil_k_norm_g, v_w_out_a, v_w_out_b, v_w_out_c, v_w_o):
    names = ("norm_g", "w_in", "b_gate", "conv_w", "conv_b", "q_a_norm_g", "w_uq", "kv_a_norm_g", "w_ukv",
             "mla_q_norm_g", "mla_k_norm_g", "dil_q_norm_g", "dil_k_norm_g", "w_out_a", "w_out_b", "w_out_c", "w_o")
    w = dict(zip(names, (norm_g, w_in, b_gate, conv_w, conv_b, q_a_norm_g, w_uq, kv_a_norm_g, w_ukv, mla_q_norm_g,
                         mla_k_norm_g, dil_q_norm_g, dil_k_norm_g, w_out_a, w_out_b, w_out_c, w_o)))
    m = dict(zip(names, (m_norm_g, m_w_in, m_b_gate, m_conv_w, m_conv_b, m_q_a_norm_g, m_w_uq, m_kv_a_norm_g, m_w_ukv,
                         m_mla_q_norm_g, m_mla_k_norm_g, m_dil_q_norm_g, m_dil_k_norm_g, m_w_out_a, m_w_out_b,
                         m_w_out_c, m_w_o)))
    v = dict(zip(names, (v_norm_g, v_w_in, v_b_gate, v_conv_w, v_conv_b, v_q_a_norm_g, v_w_uq, v_kv_a_norm_g, v_w_ukv,
                         v_mla_q_norm_g, v_mla_k_norm_g, v_dil_q_norm_g, v_dil_k_norm_g, v_w_out_a, v_w_out_b,
                         v_w_out_c, v_w_o)))
    B, S, _ = x.shape
    me = _my_index()
    cshard = CONV_WIDTH // N_DEV

    gathered = _exchange([w[n].astype(BF16) for n in BIG] + [conv_w], "all_gather_weights", gather=True)
    gw = dict(zip(BIG + ("conv_w",), gathered))

    sq, grad_x, part = _local_step(x.reshape(B * S, D_MODEL), loss_target.reshape(B * S, D_MODEL), gw, w, B, S)
    loss = lax.psum(sq * (0.5 / D_MODEL), AXES)

    recv = _exchange([part[n].astype(BF16) for n in BIG], "exchange_weight_grads", gather=False)
    res = {}
    for n, parts in zip(BIG, recv):
        rows = lambda a: a.reshape(-1, a.shape[-1])
        outs = _reduce_adamw(parts.reshape(N_DEV, -1, parts.shape[-1]), rows(w[n]), rows(m[n]), rows(v[n]),
                             "reduce_adamw_" + n)
        res[n] = tuple(a.reshape(w[n].shape) for a in outs)

    def widen(t):
        return lax.dynamic_update_slice(jnp.zeros((DEPTH, CONV_K, CONV_WIDTH), F32), t, (0, 0, me * cshard))

    small_like = [part[n] for n in SMALL]
    pick = lambda d: [widen(d[n]) if n == "conv_w" else d[n] for n in SMALL]
    parts, = _exchange([_pack_local(small_like)], "all_gather_small_grads", gather=True)
    gs, ds, ms, vs = _reduce_adamw(parts, _pack_local(pick(w)), _pack_local(pick(m)), _pack_local(pick(v)),
                                   "reduce_adamw_small")
    for n, t in zip(SMALL, zip(*(_unpack_local(a, small_like) for a in (gs, ds, ms, vs)))):
        if n == "conv_w":
            t = tuple(lax.dynamic_slice(a, (0, 0, me * cshard), (DEPTH, CONV_K, cshard)) for a in t)
        res[n] = t

    out = [loss, grad_x.reshape(B, S, D_MODEL)]
    for i in range(4):
        out += [res[n][i] for n in names]
    return tuple(out)
```

```python
import jax
import jax.numpy as jnp
from jax import lax
from jax.experimental import pallas as pl
from jax.experimental.pallas import tpu as pltpu

F32 = jnp.float32
BF16 = jnp.bfloat16

D_MODEL = 1024
DEPTH = 2
CONV_WIDTH = 512
CONV_K = 3
MLA_HEADS = 8
MLA_Q_LORA = 256
MLA_KV_LORA = 128
MLA_NOPE = 64
MLA_ROPE = 32
MLA_V = 64
MLA_QK = MLA_NOPE + MLA_ROPE
ROPE_THETA = 10000.0
DIL_PATTERNS = ((128, 1), (512, 4), (2048, 16))
DIL_GROUPS = 3
DIL_HEADS = 8
DIL_HEAD_DIM = 64
DIL_WIDTH = DIL_HEADS * DIL_HEAD_DIM
DIL_QK = DIL_GROUPS * DIL_WIDTH
EPS = 1e-6
N_IN = 11168

ADAM_LR = 0.001
ADAM_B1 = 0.9
ADAM_B2 = 0.999
ADAM_EPS = 1e-08
ADAM_WD = 0.01
ADAM_STEP = 10

N_DEV = 8
AXES = ("x", "y", "c")
LANE = 128
HALF = 64
NPAIR = 4

CB_BZ, CB_CZ, CB_GATE = 0, 4, 8
CB_A = 32
CB_QKV = 48
CB_CQ, CB_CKV, CB_KPE = 84, 86, 87
NCB = 88
PP = NCB * LANE
SHARD_COLS = N_IN // N_DEV
NEG = -1e30
VMEM_LIMIT = 56 * 1024 * 1024


def _column_chunks():
    out = []
    col = 0

    def seg(nblocks, block_of):
        nonlocal col
        for i in range(nblocks):
            out.append((col, LANE, block_of(i)))
            col += LANE

    seg(4, lambda j: CB_A + 4 * j)
    seg(4, lambda j: CB_A + 4 * j + 1)
    seg(4, lambda j: CB_A + 4 * j + 2)
    seg(4, lambda j: CB_A + 4 * j + 3)
    seg(2, lambda i: CB_CQ + i)
    seg(1, lambda i: CB_CKV)
    out.append((col, MLA_ROPE, CB_KPE))
    col += MLA_ROPE
    seg(4, lambda j: CB_BZ + j)
    seg(12, lambda c: CB_QKV + 3 * c)
    seg(12, lambda c: CB_QKV + 3 * c + 1)
    seg(12, lambda c: CB_QKV + 3 * c + 2)
    seg(4, lambda j: CB_CZ + j)
    seg(24, lambda i: CB_GATE + i)
    assert col == N_IN and sorted(c[2] for c in out) == list(range(NCB))
    return out


COLUMN_CHUNKS = _column_chunks()


def _pad_columns(shards):
    parts = []
    for start, width, _ in sorted(COLUMN_CHUNKS, key=lambda c: c[2]):
        fill = LANE - width
        while width:
            p, off = divmod(start, SHARD_COLS)
            n = min(width, SHARD_COLS - off)
            parts.append(shards[p, :, off:off + n])
            start, width = start + n, width - n
        if fill:
            parts.append(jnp.zeros((shards.shape[1], fill), shards.dtype))
    return jnp.concatenate(parts, axis=1)


def _unpad_columns(wp):
    pieces = [[] for _ in range(N_DEV)]
    for start, width, b in COLUMN_CHUNKS:
        src = b * LANE
        while width:
            p, off = divmod(start, SHARD_COLS)
            n = min(width, SHARD_COLS - off)
            pieces[p].append(wp[:, src:src + n])
            start, width, src = start + n, width - n, src + n
    return jnp.stack([jnp.concatenate(ps, axis=1) for ps in pieces])


def _cp():
    return pltpu.CompilerParams(vmem_limit_bytes=VMEM_LIMIT)


def _rstd(x, n):
    return lax.rsqrt(jnp.sum(x * x, axis=-1, keepdims=True) * (1.0 / n) + EPS)


def _sigmoid(z):
    return 1.0 / (1.0 + jnp.exp(-z))


def _silu(z):
    return z * _sigmoid(z)


def _dsilu(z):
    s = _sigmoid(z)
    return s * (1.0 + z * (1.0 - s))


def _mm(a, b):
    return jnp.dot(a.astype(BF16), b.astype(BF16), preferred_element_type=F32)


def _mm_nt(a, b):
    return lax.dot_general(a.astype(BF16), b.astype(BF16), (((1,), (1,)), ((), ())), preferred_element_type=F32)


def _mm_tn(a, b):
    return lax.dot_general(a.astype(BF16), b.astype(BF16), (((0,), (0,)), ((), ())), preferred_element_type=F32)


def _lane_lo(shape):
    return lax.broadcasted_iota(jnp.int32, shape, len(shape) - 1) < HALF


def _head_bcast_sum(x):
    same = _lane_lo((LANE, LANE)) == (lax.broadcasted_iota(jnp.int32, (LANE, LANE), 0) < HALF)
    return jnp.dot(x, jnp.where(same, 1.0, 0.0), preferred_element_type=F32, precision=lax.Precision.HIGHEST)


def _rope(t, cos, sa, sb):
    return t * cos + pltpu.roll(t, LANE - 16, axis=1) * sa + pltpu.roll(t, 16, axis=1) * sb


def _rope_t(d, cos, sa, sb):
    return d * cos + pltpu.roll(d * sa, 16, axis=1) + pltpu.roll(d * sb, LANE - 16, axis=1)


def _shift_down(u, k):
    rows = lax.broadcasted_iota(jnp.int32, u.shape, 0)
    return jnp.where(rows >= k, pltpu.roll(u, k, axis=0), 0.0)


def _shift_up(u, k):
    n = u.shape[0]
    rows = lax.broadcasted_iota(jnp.int32, u.shape, 0)
    return jnp.where(rows < n - k, pltpu.roll(u, n - k, axis=0), 0.0)


def _tile(n, want):
    t = min(n, want)
    assert n % t == 0, (n, want)
    return t


def _inproj_fwd(x, g, wp):
    T = x.shape[0]
    tm, tn = _tile(T, 1024), 512

    def body(x_ref, g_ref, w_ref, proj_ref, h_ref):
        @pl.when(pl.program_id(1) == 0)
        def _():
            xv = x_ref[...]
            h_ref[...] = (xv * _rstd(xv, D_MODEL) * g_ref[...]).astype(BF16)

        proj_ref[...] = jnp.dot(h_ref[...], w_ref[...], preferred_element_type=F32)

    return pl.pallas_call(
        body, name="inproj_fwd", grid=(T // tm, PP // tn),
        in_specs=[pl.BlockSpec((tm, D_MODEL), lambda i, j: (i, 0)),
                  pl.BlockSpec((1, D_MODEL), lambda i, j: (0, 0)),
                  pl.BlockSpec((D_MODEL, tn), lambda i, j: (0, j))],
        out_specs=[pl.BlockSpec((tm, tn), lambda i, j: (i, j)),
                   pl.BlockSpec((tm, D_MODEL), lambda i, j: (i, 0))],
        out_shape=[jax.ShapeDtypeStruct((T, PP), F32), jax.ShapeDtypeStruct((T, D_MODEL), BF16)],
        compiler_params=_cp(),
    )(x, g, wp)


def _matmul_tn(a, b, name):
    T, K = a.shape
    N = b.shape[1]
    tt, tn = _tile(T, 512), _tile(N, 1024)

    def body(a_ref, b_ref, o_ref):
        @pl.when(pl.program_id(1) == 0)
        def _():
            o_ref[...] = jnp.zeros_like(o_ref)

        o_ref[...] += _mm_tn(a_ref[...], b_ref[...])

    return pl.pallas_call(
        body, name=name, grid=(N // tn, T // tt),
        in_specs=[pl.BlockSpec((tt, K), lambda j, k: (k, 0)),
                  pl.BlockSpec((tt, tn), lambda j, k: (k, j))],
        out_specs=pl.BlockSpec((K, tn), lambda j, k: (0, j)),
        out_shape=jax.ShapeDtypeStruct((K, N), F32),
        compiler_params=_cp(),
    )(a, b)


def _inproj_bwd_x(dproj, wp, x, g, dout):
    T = x.shape[0]
    tm, tk = _tile(T, 1024), 512
    nk = PP // tk

    def body(dp_ref, w_ref, x_ref, g_ref, do_ref, dx_ref, dg_ref, acc_ref):
        i, k = pl.program_id(0), pl.program_id(1)

        @pl.when(k == 0)
        def _():
            acc_ref[...] = jnp.zeros_like(acc_ref)

        @pl.when((k == 0) & (i == 0))
        def _():
            dg_ref[...] = jnp.zeros_like(dg_ref)

        acc_ref[...] += _mm_nt(dp_ref[...], w_ref[...])

        @pl.when(k == nk - 1)
        def _():
            dh = acc_ref[...]
            xv = x_ref[...]
            r = _rstd(xv, D_MODEL)
            gy = dh * g_ref[...]
            dot = jnp.sum(xv * gy, axis=-1, keepdims=True) * (1.0 / D_MODEL)
            dx_ref[...] = do_ref[...] + r * gy - xv * (r * r * r) * dot
            dg_ref[...] += jnp.sum(dh * xv * r, axis=0, keepdims=True)

    return pl.pallas_call(
        body, name="inproj_bwd_x", grid=(T // tm, nk),
        in_specs=[pl.BlockSpec((tm, tk), lambda i, k: (i, k)),
                  pl.BlockSpec((D_MODEL, tk), lambda i, k: (0, k)),
                  pl.BlockSpec((tm, D_MODEL), lambda i, k: (i, 0)),
                  pl.BlockSpec((1, D_MODEL), lambda i, k: (0, 0)),
                  pl.BlockSpec((tm, D_MODEL), lambda i, k: (i, 0))],
        out_specs=[pl.BlockSpec((tm, D_MODEL), lambda i, k: (i, 0)),
                   pl.BlockSpec((1, D_MODEL), lambda i, k: (0, 0))],
        out_shape=[jax.ShapeDtypeStruct((T, D_MODEL), F32), jax.ShapeDtypeStruct((1, D_MODEL), F32)],
        scratch_shapes=[pltpu.VMEM((tm, D_MODEL), F32)],
        compiler_params=_cp(),
    )(dproj, wp, x, g, dout)


def _mixa_fwd(proj, cw, cb, B, S):
    nc = CONV_WIDTH // LANE
    ca = CB_A // 4

    def body(p_ref, cw_ref, cb_ref, y_ref):
        ab, ac, ax, az = (p_ref[:, i * LANE:(i + 1) * LANE] for i in range(4))
        u = ac * ax
        conv = cb_ref[...] + cw_ref[0:1, :] * _shift_down(u, 2) + cw_ref[1:2, :] * _shift_down(u, 1) + cw_ref[2:3, :] * u
        y_ref[...] = (ab * conv * _silu(az)).astype(BF16)

    return pl.pallas_call(
        body, name="mixa_fwd", grid=(B, nc),
        in_specs=[pl.BlockSpec((S, 4 * LANE), lambda b, j: (b, ca + j)),
                  pl.BlockSpec((CONV_K, LANE), lambda b, j: (0, j)),
                  pl.BlockSpec((1, LANE), lambda b, j: (0, j))],
        out_specs=pl.BlockSpec((S, LANE), lambda b, j: (b, j)),
        out_shape=jax.ShapeDtypeStruct((B * S, CONV_WIDTH), BF16),
        compiler_params=_cp(),
    )(proj, cw, cb)


def _mixa_bwd(dproj, dy, proj, cw, cb, B, S):
    nc = CONV_WIDTH // LANE
    ca = CB_A // 4

    def body(dpin_ref, dy_ref, p_ref, cw_ref, cb_ref, dp_ref, st_ref):
        del dpin_ref
        ab, ac, ax, az = (p_ref[:, i * LANE:(i + 1) * LANE] for i in range(4))
        u = ac * ax
        u1, u2 = _shift_down(u, 1), _shift_down(u, 2)
        w0, w1, w2 = cw_ref[0:1, :], cw_ref[1:2, :], cw_ref[2:3, :]
        conv = cb_ref[...] + w0 * u2 + w1 * u1 + w2 * u
        s = _silu(az)
        d = dy_ref[...]
        dconv = d * ab * s
        du = w2 * dconv + w1 * _shift_up(dconv, 1) + w0 * _shift_up(dconv, 2)
        dp_ref[:, 0:LANE] = (d * conv * s).astype(BF16)
        dp_ref[:, LANE:2 * LANE] = (du * ax).astype(BF16)
        dp_ref[:, 2 * LANE:3 * LANE] = (du * ac).astype(BF16)
        dp_ref[:, 3 * LANE:4 * LANE] = (d * ab * conv * _dsilu(az)).astype(BF16)
        row = lax.broadcasted_iota(jnp.int32, (8, LANE), 0)
        st = jnp.zeros((8, LANE), F32)
        for r, v in enumerate((dconv * u2, dconv * u1, dconv * u, dconv)):
            st = st + jnp.where(row == r, jnp.sum(v, axis=0, keepdims=True), 0.0)

        @pl.when(pl.program_id(1) == 0)
        def _():
            st_ref[...] = st

        @pl.when(pl.program_id(1) != 0)
        def _():
            st_ref[...] += st

    return pl.pallas_call(
        body, name="mixa_bwd", grid=(nc, B),
        in_specs=[pl.BlockSpec(memory_space=pl.ANY),
                  pl.BlockSpec((S, LANE), lambda j, b: (b, j)),
                  pl.BlockSpec((S, 4 * LANE), lambda j, b: (b, ca + j)),
                  pl.BlockSpec((CONV_K, LANE), lambda j, b: (0, j)),
                  pl.BlockSpec((1, LANE), lambda j, b: (0, j))],
        out_specs=[pl.BlockSpec((S, 4 * LANE), lambda j, b: (b, ca + j)),
                   pl.BlockSpec((8, LANE), lambda j, b: (0, j))],
        out_shape=[jax.ShapeDtypeStruct(dproj.shape, BF16), jax.ShapeDtypeStruct((8, CONV_WIDTH), F32)],
        input_output_aliases={0: 0},
        compiler_params=_cp(),
    )(dproj, dy, proj, cw, cb)


def _mla_prep_fwd(proj, gq, gkv, wuqp, wkp, wv, gmq, gmk, cos, sa, sb, S):
    T = proj.shape[0]
    ts = _tile(S, 512)
    ns = S // ts
    W = MLA_HEADS * LANE

    def body(p_ref, gq_ref, gkv_ref, wuq_ref, wk_ref, wv_ref, gmq_ref, gmk_ref, cos_ref, sa_ref, sb_ref,
             q_ref, k_ref, v_ref):
        cq = p_ref[:, 0:2 * LANE]
        ckv = p_ref[:, 2 * LANE:3 * LANE]
        kpe = pltpu.roll(p_ref[:, 3 * LANE:4 * LANE], HALF, axis=1)
        cqn = cq * _rstd(cq, MLA_Q_LORA) * gq_ref[...]
        ckn = (ckv * _rstd(ckv, MLA_KV_LORA) * gkv_ref[...]).astype(BF16)
        q0 = _mm(cqn, wuq_ref[...])
        kn = _mm(ckn, wk_ref[...])
        v_ref[...] = _mm(ckn, wv_ref[...]).astype(BF16)
        c, a, b = cos_ref[...], sa_ref[...], sb_ref[...]
        for h in range(MLA_HEADS):
            q0h = q0[:, h * LANE:(h + 1) * LANE]
            q_ref[h] = _rope(q0h * _rstd(q0h, MLA_QK) * gmq_ref[...], c, a, b).astype(BF16)
            k0h = kn[:, h * LANE:(h + 1) * LANE] + kpe
            k_ref[h] = _rope(k0h * _rstd(k0h, MLA_QK) * gmk_ref[...], c, a, b).astype(BF16)

    def whole(r, c):
        return pl.BlockSpec((r, c), lambda i: (0, 0))

    tab = pl.BlockSpec((ts, LANE), lambda i: (i % ns, 0))
    return pl.pallas_call(
        body, name="mla_prep_fwd", grid=(T // ts,),
        in_specs=[pl.BlockSpec((ts, 4 * LANE), lambda i: (i, CB_CQ // 4)),
                  whole(1, MLA_Q_LORA), whole(1, MLA_KV_LORA), whole(MLA_Q_LORA, W), whole(MLA_KV_LORA, W),
                  whole(MLA_KV_LORA, MLA_HEADS * MLA_V), whole(1, LANE), whole(1, LANE), tab, tab, tab],
        out_specs=[pl.BlockSpec((MLA_HEADS, ts, LANE), lambda i: (0, i, 0)),
                   pl.BlockSpec((MLA_HEADS, ts, LANE), lambda i: (0, i, 0)),
                   pl.BlockSpec((ts, MLA_HEADS * MLA_V), lambda i: (i, 0))],
        out_shape=[jax.ShapeDtypeStruct((MLA_HEADS, T, LANE), BF16), jax.ShapeDtypeStruct((MLA_HEADS, T, LANE), BF16),
                   jax.ShapeDtypeStruct((T, MLA_HEADS * MLA_V), BF16)],
        compiler_params=_cp(),
    )(proj, gq, gkv, wuqp, wkp, wv, gmq, gmk, cos, sa, sb)


def _mla_prep_bwd(dproj, dq, dk, dv, proj, gq, gkv, wuqp, wkp, wv, gmq, gmk, cos, sa, sb, S):
    T = proj.shape[0]
    ts = _tile(S, 256)
    ns = S // ts
    W = MLA_HEADS * LANE

    def body(dpin_ref, dq_ref, dk_ref, dv_ref, p_ref, gq_ref, gkv_ref, wuq_ref, wk_ref, wv_ref, gmq_ref, gmk_ref,
             cos_ref, sa_ref, sb_ref,
             dp_ref, dwuq_ref, dwk_ref, dwv_ref, dgq_ref, dgkv_ref, dgmq_ref, dgmk_ref, dq0_ref, dkn_ref):
        del dpin_ref

        @pl.when(pl.program_id(0) == 0)
        def _():
            for r in (dwuq_ref, dwk_ref, dwv_ref, dgq_ref, dgkv_ref, dgmq_ref, dgmk_ref):
                r[...] = jnp.zeros_like(r)

        cq = p_ref[:, 0:2 * LANE]
        ckv = p_ref[:, 2 * LANE:3 * LANE]
        kpe = pltpu.roll(p_ref[:, 3 * LANE:4 * LANE], HALF, axis=1)
        rq = _rstd(cq, MLA_Q_LORA)
        rkv = _rstd(ckv, MLA_KV_LORA)
        gq, gkv, gmq, gmk = gq_ref[...], gkv_ref[...], gmq_ref[...], gmk_ref[...]
        cqn = (cq * rq * gq).astype(BF16)
        ckn = (ckv * rkv * gkv).astype(BF16)
        q0 = _mm(cqn, wuq_ref[...])
        kn = _mm(ckn, wk_ref[...])
        c, a, b = cos_ref[...], sa_ref[...], sb_ref[...]
        lane = lax.broadcasted_iota(jnp.int32, (ts, LANE), 1)
        dgmq = jnp.zeros((1, LANE), F32)
        dgmk = jnp.zeros((1, LANE), F32)
        dkpe = jnp.zeros((ts, LANE), F32)
        for h in range(MLA_HEADS):
            q0h = q0[:, h * LANE:(h + 1) * LANE]
            r = _rstd(q0h, MLA_QK)
            d1 = _rope_t(dq_ref[h], c, a, b)
            gy = d1 * gmq
            dq0_ref[:, h * LANE:(h + 1) * LANE] = (
                r * gy - q0h * (r * r * r) * (jnp.sum(q0h * gy, axis=-1, keepdims=True) * (1.0 / MLA_QK))).astype(BF16)
            dgmq = dgmq + jnp.sum(d1 * q0h * r, axis=0, keepdims=True)
            k0h = kn[:, h * LANE:(h + 1) * LANE] + kpe
            r = _rstd(k0h, MLA_QK)
            d1 = _rope_t(dk_ref[h], c, a, b)
            gy = d1 * gmk
            dk0 = r * gy - k0h * (r * r * r) * (jnp.sum(k0h * gy, axis=-1, keepdims=True) * (1.0 / MLA_QK))
            dgmk = dgmk + jnp.sum(d1 * k0h * r, axis=0, keepdims=True)
            dkn_ref[:, h * LANE:(h + 1) * LANE] = jnp.where(lane < MLA_NOPE, dk0, 0.0).astype(BF16)
            dkpe = dkpe + jnp.where((lane >= MLA_NOPE) & (lane < MLA_QK), dk0, 0.0)
        dq0 = dq0_ref[...]
        dkn = dkn_ref[...]
        dvv = dv_ref[...]
        dwuq_ref[...] += _mm_tn(cqn, dq0)
        dwk_ref[...] += _mm_tn(ckn, dkn)
        dwv_ref[...] += _mm_tn(ckn, dvv)
        dgmq_ref[...] += dgmq
        dgmk_ref[...] += dgmk
        dcqn = _mm_nt(dq0, wuq_ref[...])
        gy = dcqn * gq
        dp_ref[:, 0:2 * LANE] = (
            rq * gy - cq * (rq * rq * rq) * (jnp.sum(cq * gy, axis=-1, keepdims=True) * (1.0 / MLA_Q_LORA))).astype(BF16)
        dgq_ref[...] += jnp.sum(dcqn * cq * rq, axis=0, keepdims=True)
        dckn = _mm_nt(dkn, wk_ref[...]) + _mm_nt(dvv, wv_ref[...])
        gy = dckn * gkv
        dp_ref[:, 2 * LANE:3 * LANE] = (
            rkv * gy - ckv * (rkv * rkv * rkv) * (jnp.sum(ckv * gy, axis=-1, keepdims=True) * (1.0 / MLA_KV_LORA))).astype(BF16)
        dgkv_ref[...] += jnp.sum(dckn * ckv * rkv, axis=0, keepdims=True)
        dp_ref[:, 3 * LANE:4 * LANE] = pltpu.roll(dkpe, HALF, axis=1).astype(BF16)

    def whole(r, c):
        return pl.BlockSpec((r, c), lambda i: (0, 0))

    tab = pl.BlockSpec((ts, LANE), lambda i: (i % ns, 0))
    heads = pl.BlockSpec((MLA_HEADS, ts, LANE), lambda i: (0, i, 0))
    return pl.pallas_call(
        body, name="mla_prep_bwd", grid=(T // ts,),
        in_specs=[pl.BlockSpec(memory_space=pl.ANY), heads, heads,
                  pl.BlockSpec((ts, MLA_HEADS * MLA_V), lambda i: (i, 0)),
                  pl.BlockSpec((ts, 4 * LANE), lambda i: (i, CB_CQ // 4)),
                  whole(1, MLA_Q_LORA), whole(1, MLA_KV_LORA), whole(MLA_Q_LORA, W), whole(MLA_KV_LORA, W),
                  whole(MLA_KV_LORA, MLA_HEADS * MLA_V), whole(1, LANE), whole(1, LANE), tab, tab, tab],
        out_specs=[pl.BlockSpec((ts, 4 * LANE), lambda i: (i, CB_CQ // 4)),
                   whole(MLA_Q_LORA, W), whole(MLA_KV_LORA, W), whole(MLA_KV_LORA, MLA_HEADS * MLA_V),
                   whole(1, MLA_Q_LORA), whole(1, MLA_KV_LORA), whole(1, LANE), whole(1, LANE)],
        out_shape=[jax.ShapeDtypeStruct(dproj.shape, BF16),
                   jax.ShapeDtypeStruct((MLA_Q_LORA, W), F32), jax.ShapeDtypeStruct((MLA_KV_LORA, W), F32),
                   jax.ShapeDtypeStruct((MLA_KV_LORA, MLA_HEADS * MLA_V), F32),
                   jax.ShapeDtypeStruct((1, MLA_Q_LORA), F32), jax.ShapeDtypeStruct((1, MLA_KV_LORA), F32),
                   jax.ShapeDtypeStruct((1, LANE), F32), jax.ShapeDtypeStruct((1, LANE), F32)],
        scratch_shapes=[pltpu.VMEM((ts, W), BF16), pltpu.VMEM((ts, W), BF16)],
        input_output_aliases={0: 0},
        compiler_params=_cp(),
    )(dproj, dq, dk, dv, proj, gq, gkv, wuqp, wkp, wv, gmq, gmk, cos, sa, sb)


def _dil_prep_fwd(proj, gq, gk):
    T = proj.shape[0]
    ts = _tile(T, 512)
    nc = DIL_QK // LANE

    def body(p_ref, gq_ref, gk_ref, q_ref, k_ref):
        for i, (g_ref, o_ref) in enumerate(((gq_ref, q_ref), (gk_ref, k_ref))):
            t = p_ref[:, i * LANE:(i + 1) * LANE]
            r = lax.rsqrt(_head_bcast_sum(t * t) * (1.0 / DIL_HEAD_DIM) + EPS)
            o_ref[...] = t * r * g_ref[...]

    col = pl.BlockSpec((1, LANE), lambda i, c: (0, c))
    out = pl.BlockSpec((ts, LANE), lambda i, c: (i, c))
    return pl.pallas_call(
        body, name="dil_prep_fwd", grid=(T // ts, nc),
        in_specs=[pl.BlockSpec((ts, 3 * LANE), lambda i, c: (i, CB_QKV // 3 + c)), col, col],
        out_specs=[out, out],
        out_shape=[jax.ShapeDtypeStruct((T, DIL_QK), F32)] * 2,
        compiler_params=_cp(),
    )(proj, gq, gk)


def _dil_prep_bwd(dproj, ddq, ddk, ddv, proj, gq, gk):
    T = proj.shape[0]
    ts = _tile(T, 512)
    nc = DIL_QK // LANE

    def body(dpin_ref, ddq_ref, ddk_ref, ddv_ref, p_ref, gq_ref, gk_ref, dp_ref, dgq_ref, dgk_ref):
        del dpin_ref
        first = pl.program_id(1) == 0
        dp_ref[:, 2 * LANE:3 * LANE] = ddv_ref[...].astype(BF16)
        for i, (d_ref, g_ref, dg_ref) in enumerate(((ddq_ref, gq_ref, dgq_ref), (ddk_ref, gk_ref, dgk_ref))):
            t = p_ref[:, i * LANE:(i + 1) * LANE]
            r = lax.rsqrt(_head_bcast_sum(t * t) * (1.0 / DIL_HEAD_DIM) + EPS)
            d = d_ref[...]
            gy = d * g_ref[...]
            dp_ref[:, i * LANE:(i + 1) * LANE] = (
                r * gy - t * (r * r * r) * (_head_bcast_sum(t * gy) * (1.0 / DIL_HEAD_DIM))).astype(BF16)
            part = jnp.sum(d * t * r, axis=0, keepdims=True)

            @pl.when(first)
            def _():
                dg_ref[...] = part

            @pl.when(jnp.logical_not(first))
            def _():
                dg_ref[...] += part

    col = pl.BlockSpec((1, LANE), lambda c, i: (0, c))
    tok = pl.BlockSpec((ts, LANE), lambda c, i: (i, c))
    return pl.pallas_call(
        body, name="dil_prep_bwd", grid=(nc, T // ts),
        in_specs=[pl.BlockSpec(memory_space=pl.ANY), tok, tok, tok,
                  pl.BlockSpec((ts, 3 * LANE), lambda c, i: (i, CB_QKV // 3 + c)), col, col],
        out_specs=[pl.BlockSpec((ts, 3 * LANE), lambda c, i: (i, CB_QKV // 3 + c)), col, col],
        out_shape=[jax.ShapeDtypeStruct(dproj.shape, BF16), jax.ShapeDtypeStruct((1, DIL_QK), F32),
                   jax.ShapeDtypeStruct((1, DIL_QK), F32)],
        input_output_aliases={0: 0},
        compiler_params=_cp(),
    )(dproj, ddq, ddk, ddv, proj, gq, gk)


COPY_ROWS = 256


def _to_classes(src_ref, dst_ref, d, L, scale=None):
    n = min(L, COPY_ROWS)
    for r in range(d):
        for c0 in range(0, L, n):
            rows = pl.ds(r + c0 * d, n, stride=d) if d > 1 else pl.ds(c0, n)
            val = src_ref[rows, :]
            if scale is not None:
                val = val * scale
            dst_ref[r * L + c0:r * L + c0 + n, :] = val.astype(dst_ref.dtype)


def _from_classes(src_ref, dst_ref, d, L):
    n = min(L, COPY_ROWS)
    for r in range(d):
        for c0 in range(0, L, n):
            rows = pl.ds(r + c0 * d, n, stride=d) if d > 1 else pl.ds(c0, n)
            dst_ref[rows, :] = src_ref[r * L + c0:r * L + c0 + n, :].astype(dst_ref.dtype)


def _causal_bias(t):
    return jnp.where(lax.broadcasted_iota(jnp.int32, (t, t), 0) >= lax.broadcasted_iota(jnp.int32, (t, t), 1), 0.0, NEG)


def _mla_specs(S):
    heads = pl.BlockSpec((2, S, LANE), lambda b, j: (j, b, 0))
    pair = pl.BlockSpec((S, LANE), lambda b, j: (b, j))
    return heads, pair


def _mla_attn_fwd(q, k, v, B, S):
    t = _tile(S, 256)
    scale = MLA_QK ** -0.5
    heads, pair = _mla_specs(S)

    def body(q_ref, k_ref, v_ref, o_ref, lse_ref):
        lo = _lane_lo((t, LANE))
        tri = _causal_bias(t)

        def block(g, _):
            row0 = pl.multiple_of(g * t, t)
            rows = pl.ds(row0, t)
            res = []
            for hh in range(2):
                qh = q_ref[hh, rows, :]

                def step(off, carry, bias, hh=hh, qh=qh):
                    m, l, acc = carry
                    s = _mm_nt(qh, k_ref[hh, pl.ds(off, t), :]) * scale
                    if bias is not None:
                        s = s + bias
                    m_new = jnp.maximum(m, jnp.max(s, axis=-1, keepdims=True))
                    p = jnp.exp(s - m_new)
                    a = jnp.exp(m - m_new)
                    l = a * l + jnp.sum(p, axis=-1, keepdims=True)
                    acc = a * acc + _mm(p, v_ref[pl.ds(off, t), :])
                    return m_new, l, acc

                init = (jnp.full((t, 1), NEG, F32), jnp.zeros((t, 1), F32), jnp.zeros((t, LANE), F32))
                carry = lax.fori_loop(0, g, lambda i, c: step(pl.multiple_of(i * t, t), c, None), init)
                m, l, acc = step(row0, carry, tri)
                res.append((acc / l, m + jnp.log(l)))
            o_ref[rows, :] = jnp.where(lo, res[0][0], res[1][0])
            lse_ref[rows, :] = jnp.where(lo, res[0][1], res[1][1])
            return 0

        lax.fori_loop(0, S // t, block, 0)

    return pl.pallas_call(
        body, name="mla_attn_fwd", grid=(B, NPAIR), in_specs=[heads, heads, pair], out_specs=[pair, pair],
        out_shape=[jax.ShapeDtypeStruct((B * S, MLA_HEADS * MLA_V), F32)] * 2,
        compiler_params=_cp(),
    )(q, k, v)


def _dil_geometry(gi, S):
    span, d = DIL_PATTERNS[gi]
    L = S // d
    t = _tile(L, 128)
    window = span // d
    back = min(-(-window // t) * t, L - t)
    return d, L, t, window, back


def _dil_specs(gi, S):
    qk = pl.BlockSpec((S, LANE), lambda b, j: (b, NPAIR * gi + j))
    v = pl.BlockSpec((S, LANE), lambda b, j: (b, CB_QKV + 3 * (NPAIR * gi + j) + 2))
    pair = pl.BlockSpec((S, LANE), lambda b, j: (b, j))
    return qk, v, pair


def _dil_bias(bias_ref, sl_ref, j, t, kw, back, window):
    row = lax.broadcasted_iota(jnp.int32, (2 * t, kw), 0)
    col = lax.broadcasted_iota(jnp.int32, (2 * t, kw), 1)
    second = row >= t
    slope = jnp.where(second, sl_ref[j, 1], sl_ref[j, 0])
    for n in range(bias_ref.shape[0]):
        dist = jnp.where(second, row - t, row) + n * back - col
        bias_ref[n] = jnp.where((dist >= 0) & (dist <= window), -slope * dist.astype(F32), NEG)


def _stack_heads(x, lo):
    zero = jnp.zeros((), x.dtype)
    return jnp.concatenate([jnp.where(lo, x, zero), jnp.where(lo, zero, x)], axis=0)


def _dil_attn_fwd(gi, slopes, qn, kn, proj, B, S):
    d, L, t, window, back = _dil_geometry(gi, S)
    kw, nq = back + t, L // t
    nbias = 2 if back else 1
    qk, vspec, pair = _dil_specs(gi, S)

    def body(sl_ref, q_ref, k_ref, v_ref, o_ref, lse_ref, qs, ks, vs, os_, ls, bias_ref):
        _to_classes(q_ref, qs, d, L, DIL_HEAD_DIM ** -0.5)
        _to_classes(k_ref, ks, d, L)
        _to_classes(v_ref, vs, d, L)
        _dil_bias(bias_ref, sl_ref, pl.program_id(1), t, kw, back, window)
        lo = _lane_lo((t, LANE))

        def block(g, _):
            qb = g % nq if d > 1 else g
            row0 = pl.multiple_of(g * t, t)
            rows = pl.ds(row0, t)
            early = qb * t < back
            keys = pl.ds(pl.multiple_of(jnp.where(early, row0 - qb * t, row0 - back), t), kw)
            s = _mm_nt(_stack_heads(qs[rows, :], lo), ks[keys, :]) + bias_ref[jnp.where(early, 0, nbias - 1)]
            m = jnp.max(s, axis=-1, keepdims=True)
            p = jnp.exp(s - m)
            l = jnp.sum(p, axis=-1, keepdims=True)
            o2 = _mm(p, vs[keys, :]) / l
            lse2 = m + jnp.log(l)
            os_[rows, :] = jnp.where(lo, o2[:t], o2[t:])
            ls[rows, :] = jnp.where(lo, lse2[:t], lse2[t:])
            return 0

        lax.fori_loop(0, d * nq, block, 0)
        _from_classes(os_, o_ref, d, L)
        _from_classes(ls, lse_ref, d, L)

    return pl.pallas_call(
        body, name=f"dil_attn_fwd_{gi}", grid=(B, NPAIR),
        in_specs=[pl.BlockSpec(memory_space=pltpu.SMEM), qk, qk, vspec], out_specs=[pair, pair],
        out_shape=[jax.ShapeDtypeStruct((B * S, DIL_WIDTH), F32)] * 2,
        scratch_shapes=[pltpu.VMEM((S, LANE), BF16)] * 3 + [pltpu.VMEM((S, LANE), F32)] * 2
                       + [pltpu.VMEM((nbias, 2 * t, kw), F32)],
        compiler_params=_cp(),
    )(slopes, qn, kn, proj)


def _mla_attn_bwd(q, k, v, do, lse, delta, B, S):
    T = B * S
    t = _tile(S, 256)
    scale = MLA_QK ** -0.5
    heads, pair = _mla_specs(S)

    def body(q_ref, k_ref, v_ref, do_ref, lse_ref, dl_ref, dq_ref, dk_ref, dv_ref):
        dk_ref[...] = jnp.zeros_like(dk_ref)
        dv_ref[...] = jnp.zeros_like(dv_ref)
        lo = _lane_lo((t, LANE))
        tri = _causal_bias(t)

        def block(g, _):
            row0 = pl.multiple_of(g * t, t)
            rows = pl.ds(row0, t)
            for hh in range(2):
                sel = lo if hh == 0 else jnp.logical_not(lo)
                qh = q_ref[hh, rows, :]
                doh = jnp.where(sel, do_ref[rows, :], jnp.zeros((), BF16))
                lse_h = jnp.max(jnp.where(sel, lse_ref[rows, :], NEG), axis=-1, keepdims=True)
                dl_h = jnp.max(jnp.where(sel, dl_ref[rows, :], NEG), axis=-1, keepdims=True)

                def step(off, dq_acc, bias, hh=hh, qh=qh, doh=doh, lse_h=lse_h, dl_h=dl_h):
                    cols = pl.ds(off, t)
                    kh = k_ref[hh, cols, :]
                    s = _mm_nt(qh, kh) * scale
                    if bias is not None:
                        s = s + bias
                    p = jnp.exp(s - lse_h)
                    dp = _mm_nt(doh, v_ref[cols, :])
                    ds = (p * (dp - dl_h) * scale).astype(BF16)
                    dk_ref[hh, cols, :] += _mm_tn(ds, qh)
                    dv_ref[cols, :] += _mm_tn(p, doh)
                    return dq_acc + _mm(ds, kh)

                dq_acc = lax.fori_loop(0, g, lambda i, a: step(pl.multiple_of(i * t, t), a, None),
                                       jnp.zeros((t, LANE), F32))
                dq_ref[hh, rows, :] = step(row0, dq_acc, tri)
            return 0

        lax.fori_loop(0, S // t, block, 0)

    return pl.pallas_call(
        body, name="mla_attn_bwd", grid=(B, NPAIR), in_specs=[heads, heads, pair, pair, pair, pair],
        out_specs=[heads, heads, pair],
        out_shape=[jax.ShapeDtypeStruct((MLA_HEADS, T, LANE), F32), jax.ShapeDtypeStruct((MLA_HEADS, T, LANE), F32),
                   jax.ShapeDtypeStruct((T, MLA_HEADS * MLA_V), F32)],
        compiler_params=_cp(),
    )(q, k, v, do, lse, delta)


def _dil_attn_bwd(gi, slopes, qn, kn, proj, do, lse, delta, through, B, S):
    d, L, t, window, back = _dil_geometry(gi, S)
    kw, nq = back + t, L // t
    nbias = 2 if back else 1
    scale = DIL_HEAD_DIM ** -0.5
    qk, vspec, pair = _dil_specs(gi, S)

    def body(*refs):
        refs = list(refs)
        sl_ref, q_ref, k_ref, v_ref, do_ref, lse_ref, dl_ref = refs[:7]
        dq_ref, dk_ref, dv_ref, qs, ks, vs, dos, lss, dls, dqs, dks, dvs, bias_ref = refs[-13:]
        _to_classes(q_ref, qs, d, L, scale)
        for src, dst in ((k_ref, ks), (v_ref, vs), (do_ref, dos), (lse_ref, lss), (dl_ref, dls)):
            _to_classes(src, dst, d, L)
        _dil_bias(bias_ref, sl_ref, pl.program_id(1), t, kw, back, window)
        dks[...] = jnp.zeros_like(dks)
        dvs[...] = jnp.zeros_like(dvs)
        lo = _lane_lo((t, LANE))

        def stats(ref, rows):
            x = ref[rows, :]
            return jnp.concatenate([jnp.max(jnp.where(lo, x, NEG), axis=-1, keepdims=True),
                                    jnp.max(jnp.where(lo, NEG, x), axis=-1, keepdims=True)], axis=0)

        def block(g, _):
            qb = g % nq if d > 1 else g
            row0 = pl.multiple_of(g * t, t)
            rows = pl.ds(row0, t)
            early = qb * t < back
            keys = pl.ds(pl.multiple_of(jnp.where(early, row0 - qb * t, row0 - back), t), kw)
            q2 = _stack_heads(qs[rows, :], lo)
            do2 = _stack_heads(dos[rows, :], lo)
            kt = ks[keys, :]
            s = _mm_nt(q2, kt) + bias_ref[jnp.where(early, 0, nbias - 1)]
            p = jnp.exp(s - stats(lss, rows))
            ds = (p * (_mm_nt(do2, vs[keys, :]) - stats(dls, rows))).astype(BF16)
            dq2 = _mm(ds, kt) * scale
            dqs[rows, :] = jnp.where(lo, dq2[:t], dq2[t:])
            dks[keys, :] += _mm_tn(ds, q2)
            dvs[keys, :] += _mm_tn(p, do2)
            return 0

        lax.fori_loop(0, d * nq, block, 0)
        for src, dst in ((dqs, dq_ref), (dks, dk_ref), (dvs, dv_ref)):
            _from_classes(src, dst, d, L)

    in_specs = [pl.BlockSpec(memory_space=pltpu.SMEM), qk, qk, vspec, pair, pair, pair]
    args = [slopes, qn, kn, proj, do, lse, delta]
    aliases = {}
    if through is not None:
        aliases = {len(args) + i: i for i in range(3)}
        in_specs = in_specs + [pl.BlockSpec(memory_space=pl.ANY)] * 3
        args = args + list(through)
    return pl.pallas_call(
        body, name=f"dil_attn_bwd_{gi}", grid=(B, NPAIR), in_specs=in_specs, out_specs=[qk, qk, qk],
        out_shape=[jax.ShapeDtypeStruct((B * S, DIL_QK), F32)] * 3,
        scratch_shapes=[pltpu.VMEM((S, LANE), BF16)] * 4 + [pltpu.VMEM((S, LANE), F32)] * 5
                       + [pltpu.VMEM((nbias, 2 * t, kw), F32)],
        input_output_aliases=aliases,
        compiler_params=_cp(),
    )(*args)


def _merge_common(p_ref, bg_ref, ob_ref, og_refs, lse_refs):
    bz = p_ref[:, CB_BZ * LANE:(CB_BZ + 4) * LANE]
    cz = p_ref[:, CB_CZ * LANE:(CB_CZ + 4) * LANE]
    gates = [_sigmoid(p_ref[:, (CB_GATE + 8 * i) * LANE:(CB_GATE + 8 * i + 8) * LANE]
                      + bg_ref[:, i * D_MODEL:(i + 1) * D_MODEL]) for i in range(3)]
    ob = ob_ref[...]
    lses = [r[...] for r in lse_refs]
    mx = jnp.maximum(jnp.maximum(lses[0], lses[1]), lses[2])
    es = [jnp.exp(v - mx) for v in lses]
    inv = 1.0 / (es[0] + es[1] + es[2])
    alphas = [e * inv for e in es]
    oc = alphas[0] * og_refs[0][...] + alphas[1] * og_refs[1][...] + alphas[2] * og_refs[2][...]
    return bz, cz, gates, ob, alphas, oc


def _merge_fwd(x, proj, b_gate, ya, ob, ogs, lses, woa, wob, woc, wo):
    T = x.shape[0]
    ts = _tile(T, 256)
    MW = 32 * LANE

    def body(x_ref, p_ref, bg_ref, ya_ref, ob_ref, og0, og1, og2, l0, l1, l2, woa_ref, wob_ref, woc_ref, wo_ref, out_ref):
        bz, cz, gates, obv, alphas, oc = _merge_common(p_ref, bg_ref, ob_ref, (og0, og1, og2), (l0, l1, l2))
        yb = obv * _silu(bz)
        yc = oc * _silu(cz)
        merged = (gates[0] * _mm(ya_ref[...], woa_ref[...]) + gates[1] * _mm(yb, wob_ref[...])
                  + gates[2] * _mm(yc, woc_ref[...]))
        out_ref[...] = x_ref[...] + _mm(merged, wo_ref[...])

    def whole(r, c):
        return pl.BlockSpec((r, c), lambda i: (0, 0))

    tok = lambda w: pl.BlockSpec((ts, w), lambda i: (i, 0))
    return pl.pallas_call(
        body, name="merge_fwd", grid=(T // ts,),
        in_specs=[tok(D_MODEL), tok(MW), whole(1, 3 * D_MODEL), tok(CONV_WIDTH)] + [tok(DIL_WIDTH)] * 7
                 + [whole(CONV_WIDTH, D_MODEL)] * 3 + [whole(D_MODEL, D_MODEL)],
        out_specs=tok(D_MODEL),
        out_shape=jax.ShapeDtypeStruct((T, D_MODEL), F32),
        compiler_params=_cp(),
    )(x, proj, b_gate, ya, ob, *ogs, *lses, woa, wob, woc, wo)


def _merge_bwd(dout, proj, b_gate, ya, ob, ogs, lses, woa, wob, woc, wo):
    T = dout.shape[0]
    ts = _tile(T, 256)
    MW = 32 * LANE

    def body(do_ref, p_ref, bg_ref, ya_ref, ob_ref, og0, og1, og2, l0, l1, l2, woa_ref, wob_ref, woc_ref, wo_ref,
             dp_ref, dya_ref, dob_ref, dlb_ref, dg0, dg1, dg2, dl0, dl1, dl2,
             mg_ref, dpa_ref, dpb_ref, dpc_ref, yb_ref, yc_ref, dbg_ref):
        bz, cz, gates, obv, alphas, oc = _merge_common(p_ref, bg_ref, ob_ref, (og0, og1, og2), (l0, l1, l2))
        sb, sc = _silu(bz), _silu(cz)
        yb = obv * sb
        yc = oc * sc
        ps = [_mm(ya_ref[...], woa_ref[...]), _mm(yb, wob_ref[...]), _mm(yc, woc_ref[...])]
        mg_ref[...] = (gates[0] * ps[0] + gates[1] * ps[1] + gates[2] * ps[2]).astype(BF16)
        yb_ref[...] = yb.astype(BF16)
        yc_ref[...] = yc.astype(BF16)
        dm = _mm_nt(do_ref[...], wo_ref[...])
        dps = []
        first = pl.program_id(0) == 0
        for i, dref in enumerate((dpa_ref, dpb_ref, dpc_ref)):
            g = gates[i]
            dpi = (dm * g).astype(BF16)
            dref[...] = dpi
            dps.append(dpi)
            dgp = dm * ps[i] * g * (1.0 - g)
            dp_ref[:, (CB_GATE + 8 * i) * LANE:(CB_GATE + 8 * i + 8) * LANE] = dgp.astype(BF16)
            part = jnp.sum(dgp, axis=0, keepdims=True)

            @pl.when(first)
            def _():
                dbg_ref[:, i * D_MODEL:(i + 1) * D_MODEL] = part

            @pl.when(jnp.logical_not(first))
            def _():
                dbg_ref[:, i * D_MODEL:(i + 1) * D_MODEL] += part

        dya_ref[...] = _mm_nt(dps[0], woa_ref[...])
        dyb = _mm_nt(dps[1], wob_ref[...])
        dyc = _mm_nt(dps[2], woc_ref[...])
        dp_ref[:, CB_BZ * LANE:(CB_BZ + 4) * LANE] = (dyb * obv * _dsilu(bz)).astype(BF16)
        dp_ref[:, CB_CZ * LANE:(CB_CZ + 4) * LANE] = (dyc * oc * _dsilu(cz)).astype(BF16)
        dob = dyb * sb
        doc = dyc * sc
        dob_ref[...] = dob.astype(BF16)
        for c in range(NPAIR):
            cs = slice(c * LANE, (c + 1) * LANE)
            dlb_ref[:, cs] = _head_bcast_sum(dob[:, cs] * obv[:, cs])
            dd = _head_bcast_sum(doc[:, cs] * oc[:, cs])
            for a, dref, lref in zip(alphas, (dg0, dg1, dg2), (dl0, dl1, dl2)):
                dref[:, cs] = a[:, cs] * doc[:, cs]
                lref[:, cs] = a[:, cs] * dd

    def whole(r, c):
        return pl.BlockSpec((r, c), lambda i: (0, 0))

    tok = lambda w: pl.BlockSpec((ts, w), lambda i: (i, 0))
    sd = jax.ShapeDtypeStruct
    W = DIL_WIDTH
    return pl.pallas_call(
        body, name="merge_bwd", grid=(T // ts,),
        in_specs=[tok(D_MODEL), tok(MW), whole(1, 3 * D_MODEL), tok(CONV_WIDTH)] + [tok(W)] * 7
                 + [whole(CONV_WIDTH, D_MODEL)] * 3 + [whole(D_MODEL, D_MODEL)],
        out_specs=[tok(MW), tok(CONV_WIDTH), tok(W), tok(W)] + [tok(W)] * 6
                  + [tok(D_MODEL)] * 4 + [tok(W), tok(W), whole(1, 3 * D_MODEL)],
        out_shape=[sd((T, PP), BF16), sd((T, CONV_WIDTH), F32), sd((T, W), BF16), sd((T, W), F32)]
                  + [sd((T, W), F32)] * 6
                  + [sd((T, D_MODEL), BF16)] * 4 + [sd((T, W), BF16)] * 2 + [sd((1, 3 * D_MODEL), F32)],
        compiler_params=_cp(),
    )(dout, proj, b_gate, ya, ob, *ogs, *lses, woa, wob, woc, wo)


def _loss_head(y, target):
    T = y.shape[0]
    ts = _tile(T, 512)

    def body(y_ref, t_ref, d_ref, l_ref):
        e = y_ref[...] - t_ref[...]
        d_ref[...] = e * (1.0 / D_MODEL)
        l_ref[...] = jnp.zeros((1, 8, LANE), F32) + jnp.sum(e * e)

    tok = pl.BlockSpec((ts, D_MODEL), lambda i: (i, 0))
    return pl.pallas_call(
        body, name="loss_head", grid=(T // ts,), in_specs=[tok, tok],
        out_specs=[tok, pl.BlockSpec((1, 8, LANE), lambda i: (i, 0, 0))],
        out_shape=[jax.ShapeDtypeStruct((T, D_MODEL), F32), jax.ShapeDtypeStruct((T // ts, 8, LANE), F32)],
        compiler_params=_cp(),
    )(y, target)


def _my_index():
    return 4 * lax.axis_index("x") + 2 * lax.axis_index("y") + lax.axis_index("c")


def _peers():
    x, y, c = (lax.axis_index(a) for a in AXES)
    out = []
    for kk in range(1, N_DEV):
        px = 1 - x if kk & 4 else x
        py = 1 - y if kk & 2 else y
        pc = 1 - c if kk & 1 else c
        out.append(((px, py, pc), 4 * px + 2 * py + pc))
    return out


def _exchange(arrays, name, gather):
    n = len(arrays)

    def body(*refs):
        srcs, outs = refs[:n], refs[n:2 * n]
        send_sems, recv_sems, local_sems = refs[2 * n:]
        me = _my_index()
        peers = _peers()
        started = []
        for a, (src, out) in enumerate(zip(srcs, outs)):
            mine = pltpu.make_async_copy(src if gather else src.at[me], out.at[me], local_sems.at[a])
            mine.start()
            started.append(mine)
        sends = []
        for i, (pos, idx) in enumerate(peers):
            for a, (src, out) in enumerate(zip(srcs, outs)):
                cp = pltpu.make_async_remote_copy(
                    src_ref=src if gather else src.at[idx], dst_ref=out.at[me], send_sem=send_sems.at[a, i],
                    recv_sem=recv_sems.at[a, i], device_id=pos, device_id_type=pl.DeviceIdType.MESH)
                cp.start()
                sends.append(cp)
        for i, (pos, idx) in enumerate(peers):
            for a, (src, out) in enumerate(zip(srcs, outs)):
                pltpu.make_async_remote_copy(
                    src_ref=src if gather else src.at[idx], dst_ref=out.at[idx], send_sem=send_sems.at[a, i],
                    recv_sem=recv_sems.at[a, i], device_id=pos, device_id_type=pl.DeviceIdType.MESH).wait_recv()
        for cp in sends:
            cp.wait_send()
        for mine in started:
            mine.wait()

    any_space = pl.BlockSpec(memory_space=pl.ANY)
    return pl.pallas_call(
        body, name=name, in_specs=[any_space] * n, out_specs=[any_space] * n,
        out_shape=[jax.ShapeDtypeStruct(((N_DEV,) + a.shape) if gather else a.shape, a.dtype) for a in arrays],
        scratch_shapes=[pltpu.SemaphoreType.DMA((n, N_DEV - 1)), pltpu.SemaphoreType.DMA((n, N_DEV - 1)),
                        pltpu.SemaphoreType.DMA((n,))],
    )(*arrays)


def _adamw(w, g, m, v):
    m = ADAM_B1 * m + (1.0 - ADAM_B1) * g
    v = ADAM_B2 * v + (1.0 - ADAM_B2) * (g * g)
    m_hat = m / (1.0 - ADAM_B1 ** ADAM_STEP)
    v_hat = v / (1.0 - ADAM_B2 ** ADAM_STEP)
    delta = -ADAM_LR * (m_hat / (jnp.sqrt(v_hat) + ADAM_EPS) + ADAM_WD * w)
    return delta, m, v


def _reduce_adamw(parts, w, m, v, name):
    R, C = w.shape
    tr = R
    while N_DEV * tr * C * parts.dtype.itemsize > REDUCE_BLOCK_BYTES and tr % 32 == 0:
        tr //= 2

    def body(p_ref, w_ref, m_ref, v_ref, g_ref, d_ref, nm_ref, nv_ref):
        g = p_ref[0].astype(F32)
        for s in range(1, N_DEV):
            g = g + p_ref[s].astype(F32)
        g_ref[...] = g
        d_ref[...], nm_ref[...], nv_ref[...] = _adamw(w_ref[...], g, m_ref[...], v_ref[...])

    row = pl.BlockSpec((tr, C), lambda i: (i, 0))
    return pl.pallas_call(
        body, name=name, grid=(R // tr,),
        in_specs=[pl.BlockSpec((N_DEV, tr, C), lambda i: (0, i, 0)), row, row, row],
        out_specs=[row] * 4, out_shape=[jax.ShapeDtypeStruct((R, C), F32)] * 4,
        compiler_params=_cp(),
    )(parts, w, m, v)


BIG = ("w_in", "w_uq", "w_ukv", "w_out_a", "w_out_b", "w_out_c", "w_o")
SMALL = ("norm_g", "b_gate", "conv_w", "conv_b", "q_a_norm_g", "kv_a_norm_g", "mla_q_norm_g", "mla_k_norm_g",
         "dil_q_norm_g", "dil_k_norm_g")
PACK_ROWS = 128
REDUCE_BLOCK_BYTES = 6 * 1024 * 1024


def _pack_local(tensors):
    flat = jnp.concatenate([t.reshape(-1) for t in tensors])
    pad = (-flat.shape[0]) % (PACK_ROWS * LANE)
    return jnp.concatenate([flat, jnp.zeros((pad,), flat.dtype)]).reshape(-1, LANE)


def _unpack_local(rows, like):
    flat = rows.reshape(-1)
    out, off = [], 0
    for t in like:
        out.append(flat[off:off + t.size].reshape(t.shape))
        off += t.size
    return out


def _cols_to_slots(a):
    k = a.shape[0]
    return a.reshape(k, N_DEV, -1).transpose(1, 0, 2)


def _slots_to_cols(s):
    return s.transpose(1, 0, 2).reshape(s.shape[1], -1)


def _rope_tables(S):
    inv = ROPE_THETA ** (-jnp.arange(0, MLA_ROPE, 2, dtype=F32) / MLA_ROPE)
    ang = jnp.arange(S, dtype=F32)[:, None] * inv[None, :]
    cos, sin = jnp.cos(ang), jnp.sin(ang)
    one = jnp.ones((S, MLA_NOPE), F32)
    z16, z32, z64 = (jnp.zeros((S, n), F32) for n in (16, 32, 64))
    cosp = jnp.concatenate([one, cos, cos, jnp.ones((S, 32), F32)], axis=1)
    sa = jnp.concatenate([z64, -sin, z16, z32], axis=1)
    sb = jnp.concatenate([z64, z16, sin, z32], axis=1)
    return cosp, sa, sb


def _alibi_slopes():
    n = DIL_GROUPS * DIL_HEADS
    m = 2.0 ** (-8.0 * jnp.arange(1, n + 1, dtype=F32) / n)
    return m.reshape(DIL_GROUPS, NPAIR, 2)


def _pad_slots(s):
    n, k, c = s.shape
    return _slots_to_cols(jnp.concatenate([s, jnp.zeros((n, k, LANE - c), s.dtype)], axis=2))


def _layer_params(gw, small, l):
    p = {}
    p["wp"] = _pad_columns(gw["w_in"][:, l])
    p["norm_g"] = small["norm_g"][l][None]
    p["b_gate"] = small["b_gate"][l][None]
    p["conv_w"] = gw["conv_w"][:, l].transpose(1, 0, 2).reshape(CONV_K, CONV_WIDTH)
    p["conv_b"] = small["conv_b"][l][None]
    p["gq"] = small["q_a_norm_g"][l][None]
    p["gkv"] = small["kv_a_norm_g"][l][None]
    p["wuqp"] = _pad_slots(gw["w_uq"][:, l])
    kv = gw["w_ukv"][:, l]
    p["wkp"] = _pad_slots(kv[:, :, :MLA_NOPE])
    p["wv"] = kv[:, :, MLA_NOPE:].transpose(1, 0, 2).reshape(MLA_KV_LORA, MLA_HEADS * MLA_V)
    zpad = jnp.zeros((1, LANE - MLA_QK), F32)
    p["gmq"] = jnp.concatenate([small["mla_q_norm_g"][l][None], zpad], axis=1)
    p["gmk"] = jnp.concatenate([small["mla_k_norm_g"][l][None], zpad], axis=1)
    tile = lambda g: jnp.broadcast_to(g[:, None, :], (DIL_GROUPS, DIL_HEADS, DIL_HEAD_DIM)).reshape(1, DIL_QK)
    p["gdq"] = tile(small["dil_q_norm_g"][l])
    p["gdk"] = tile(small["dil_k_norm_g"][l])
    p["woa"], p["wob"], p["woc"] = (_slots_to_cols(gw[n][:, l]) for n in ("w_out_a", "w_out_b", "w_out_c"))
    p["wo"] = gw["w_o"][:, l].reshape(D_MODEL, D_MODEL)
    return p


def _layer_fwd(x, p, tabs, slopes, B, S):
    proj, h = _inproj_fwd(x, p["norm_g"], p["wp"])
    ya = _mixa_fwd(proj, p["conv_w"], p["conv_b"], B, S)
    q, k, v = _mla_prep_fwd(proj, p["gq"], p["gkv"], p["wuqp"], p["wkp"], p["wv"], p["gmq"], p["gmk"], *tabs, S)
    ob, lse_b = _mla_attn_fwd(q, k, v, B, S)
    qn, kn = _dil_prep_fwd(proj, p["gdq"], p["gdk"])
    ogs, lses = [], []
    for gi in range(DIL_GROUPS):
        o, lse = _dil_attn_fwd(gi, slopes[gi], qn, kn, proj, B, S)
        ogs.append(o)
        lses.append(lse)
    out = _merge_fwd(x, proj, p["b_gate"], ya, ob, ogs, lses, p["woa"], p["wob"], p["woc"], p["wo"])
    saved = dict(x=x, proj=proj, h=h, ya=ya, q=q, k=k, v=v, ob=ob, lse_b=lse_b, qn=qn, kn=kn, ogs=ogs, lses=lses)
    return out, saved


def _layer_bwd(dout, sv, p, tabs, slopes, B, S):
    proj = sv["proj"]
    (dproj, dya, dob, dlb, dg0, dg1, dg2, dl0, dl1, dl2, merged, dpa, dpb, dpc, yb, yc, dbg) = _merge_bwd(
        dout, proj, p["b_gate"], sv["ya"], sv["ob"], sv["ogs"], sv["lses"], p["woa"], p["wob"], p["woc"], p["wo"])
    g = {}
    g["w_o"] = _matmul_tn(merged, dout, "dw_o").reshape(N_DEV, D_MODEL // N_DEV, D_MODEL)
    g["w_out_a"] = _cols_to_slots(_matmul_tn(sv["ya"], dpa, "dw_out_a"))
    g["w_out_b"] = _cols_to_slots(_matmul_tn(yb, dpb, "dw_out_b"))
    g["w_out_c"] = _cols_to_slots(_matmul_tn(yc, dpc, "dw_out_c"))
    g["b_gate"] = dbg[0]
    dproj, st = _mixa_bwd(dproj, dya, proj, p["conv_w"], p["conv_b"], B, S)
    g["conv_w"] = st[0:CONV_K]
    g["conv_b"] = st[CONV_K]
    dq, dk, dv = _mla_attn_bwd(sv["q"], sv["k"], sv["v"], dob, sv["lse_b"], dlb, B, S)
    dproj, dwuqp, dwkp, dwv, dgq, dgkv, dgmq, dgmk = _mla_prep_bwd(
        dproj, dq, dk, dv, proj, p["gq"], p["gkv"], p["wuqp"], p["wkp"], p["wv"], p["gmq"], p["gmk"], *tabs, S)
    g["w_uq"] = _cols_to_slots(dwuqp)[:, :, :MLA_QK]
    g["w_ukv"] = jnp.concatenate([_cols_to_slots(dwkp)[:, :, :MLA_NOPE], _cols_to_slots(dwv)], axis=2)
    g["q_a_norm_g"], g["kv_a_norm_g"] = dgq[0], dgkv[0]
    g["mla_q_norm_g"], g["mla_k_norm_g"] = dgmq[0, :MLA_QK], dgmk[0, :MLA_QK]
    dqkv = None
    for gi, (dog, dlg) in enumerate(((dg0, dl0), (dg1, dl1), (dg2, dl2))):
        dqkv = _dil_attn_bwd(gi, slopes[gi], sv["qn"], sv["kn"], proj, dog, sv["lses"][gi], dlg, dqkv, B, S)
    dproj, dgdq, dgdk = _dil_prep_bwd(dproj, *dqkv, proj, p["gdq"], p["gdk"])
    g["dil_q_norm_g"] = dgdq.reshape(DIL_GROUPS, DIL_HEADS, DIL_HEAD_DIM).sum(axis=1)
    g["dil_k_norm_g"] = dgdk.reshape(DIL_GROUPS, DIL_HEADS, DIL_HEAD_DIM).sum(axis=1)
    g["w_in"] = _unpad_columns(_matmul_tn(sv["h"], dproj, "dw_in"))
    dx, dng = _inproj_bwd_x(dproj, p["wp"], sv["x"], p["norm_g"], dout)
    g["norm_g"] = dng[0]
    return dx, g


def _local_step(x, target, gw, small, B, S):
    tabs = _rope_tables(S)
    sl = _alibi_slopes()
    slopes = [sl[gi] * float(DIL_PATTERNS[gi][1]) for gi in range(DIL_GROUPS)]
    params = [_layer_params(gw, small, l) for l in range(DEPTH)]
    saved = []
    for l in range(DEPTH):
        x, sv = _layer_fwd(x, params[l], tabs, slopes, B, S)
        saved.append(sv)
    dout, lparts = _loss_head(x, target)
    sq = jnp.sum(lparts[:, 0, 0])
    grads = [None] * DEPTH
    for l in reversed(range(DEPTH)):
        dout, grads[l] = _layer_bwd(dout, saved[l], params[l], tabs, slopes, B, S)
    part = {n: jnp.stack([grads[l][n] for l in range(DEPTH)], axis=1) for n in BIG}
    part.update({n: jnp.stack([grads[l][n] for l in range(DEPTH)]) for n in SMALL})
    return sq, dout, part


def kernel(x, norm_g, w_in, b_gate, conv_w, conv_b, q_a_norm_g, w_uq, kv_a_norm_g, w_ukv, mla_q_norm_g, mla_k_norm_g, dil_q_norm_g, dil_k_norm_g, w_out_a, w_out_b, w_out_c, w_o, loss_target, m_norm_g, m_w_in, m_b_gate, m_conv_w, m_conv_b, m_q_a_norm_g, m_w_uq, m_kv_a_norm_g, m_w_ukv, m_mla_q_norm_g, m_mla_k_norm_g, m_dil_q_norm_g, m_dil_k_norm_g, m_w_out_a, m_w_out_b, m_w_out_c, m_w_o, v_norm_g, v_w_in, v_b_gate, v_conv_w, v_conv_b, v_q_a_norm_g, v_w_uq, v_kv_a_norm_g, v_w_ukv, v_mla_q_norm_g, v_mla_k_norm_g, v_dil_q_norm_g, v_dil_k_norm_g, v_w_out_a, v_w_out_b, v_w_out_c, v_w_o):
    names = ("norm_g", "w_in", "b_gate", "conv_w", "conv_b", "q_a_norm_g", "w_uq", "kv_a_norm_g", "w_ukv",
             "mla_q_norm_g", "mla_k_norm_g", "dil_q_norm_g", "dil_k_norm_g", "w_out_a", "w_out_b", "w_out_c", "w_o")
    w = dict(zip(names, (norm_g, w_in, b_gate, conv_w, conv_b, q_a_norm_g, w_uq, kv_a_norm_g, w_ukv, mla_q_norm_g,
                         mla_k_norm_g, dil_q_norm_g, dil_k_norm_g, w_out_a, w_out_b, w_out_c, w_o)))
    m = dict(zip(names, (m_norm_g, m_w_in, m_b_gate, m_conv_w, m_conv_b, m_q_a_norm_g, m_w_uq, m_kv_a_norm_g, m_w_ukv,
                         m_mla_q_norm_g, m_mla_k_norm_g, m_dil_q_norm_g, m_dil_k_norm_g, m_w_out_a, m_w_out_b,
                         m_w_out_c, m_w_o)))
    v = dict(zip(names, (v_norm_g, v_w_in, v_b_gate, v_conv_w, v_conv_b, v_q_a_norm_g, v_w_uq, v_kv_a_norm_g, v_w_ukv,
                         v_mla_q_norm_g, v_mla_k_norm_g, v_dil_q_norm_g, v_dil_k_norm_g, v_w_out_a, v_w_out_b,
                         v_w_out_c, v_w_o)))
    B, S, _ = x.shape
    me = _my_index()
    cshard = CONV_WIDTH // N_DEV

    gathered = _exchange([w[n].astype(BF16) for n in BIG] + [conv_w], "all_gather_weights", gather=True)
    gw = dict(zip(BIG + ("conv_w",), gathered))

    sq, grad_x, part = _local_step(x.reshape(B * S, D_MODEL), loss_target.reshape(B * S, D_MODEL), gw, w, B, S)
    loss = lax.psum(sq * (0.5 / D_MODEL), AXES)

    recv = _exchange([part[n].astype(BF16) for n in BIG], "exchange_weight_grads", gather=False)
    res = {}
    for n, parts in zip(BIG, recv):
        rows = lambda a: a.reshape(-1, a.shape[-1])
        outs = _reduce_adamw(parts.reshape(N_DEV, -1, parts.shape[-1]), rows(w[n]), rows(m[n]), rows(v[n]),
                             "reduce_adamw_" + n)
        res[n] = tuple(a.reshape(w[n].shape) for a in outs)

    def widen(t):
        return lax.dynamic_update_slice(jnp.zeros((DEPTH, CONV_K, CONV_WIDTH), F32), t, (0, 0, me * cshard))

    small_like = [part[n] for n in SMALL]
    pick = lambda d: [widen(d[n]) if n == "conv_w" else d[n] for n in SMALL]
    parts, = _exchange([_pack_local(small_like)], "all_gather_small_grads", gather=True)
    gs, ds, ms, vs = _reduce_adamw(parts, _pack_local(pick(w)), _pack_local(pick(m)), _pack_local(pick(v)),
                                   "reduce_adamw_small")
    for n, t in zip(SMALL, zip(*(_unpack_local(a, small_like) for a in (gs, ds, ms, vs)))):
        if n == "conv_w":
            t = tuple(lax.dynamic_slice(a, (0, 0, me * cshard), (DEPTH, CONV_K, cshard)) for a in t)
        res[n] = t

    out = [loss, grad_x.reshape(B, S, D_MODEL)]
    for i in range(4):
        out += [res[n][i] for n in names]
    return tuple(out)
```

```python
import jax
import jax.numpy as jnp
from jax import lax
from jax.experimental import pallas as pl
from jax.experimental.pallas import tpu as pltpu

F32 = jnp.float32
BF16 = jnp.bfloat16

D_MODEL = 1024
DEPTH = 2
CONV_WIDTH = 512
CONV_K = 3
MLA_HEADS = 8
MLA_Q_LORA = 256
MLA_KV_LORA = 128
MLA_NOPE = 64
MLA_ROPE = 32
MLA_V = 64
MLA_QK = MLA_NOPE + MLA_ROPE
ROPE_THETA = 10000.0
DIL_PATTERNS = ((128, 1), (512, 4), (2048, 16))
DIL_GROUPS = 3
DIL_HEADS = 8
DIL_HEAD_DIM = 64
DIL_WIDTH = DIL_HEADS * DIL_HEAD_DIM
DIL_QK = DIL_GROUPS * DIL_WIDTH
EPS = 1e-6
N_IN = 11168

ADAM_LR = 0.001
ADAM_B1 = 0.9
ADAM_B2 = 0.999
ADAM_EPS = 1e-08
ADAM_WD = 0.01
ADAM_STEP = 10

N_DEV = 8
AXES = ("x", "y", "c")
LANE = 128
HALF = 64
NPAIR = 4

CB_BZ, CB_CZ, CB_GATE = 0, 4, 8
CB_A = 32
CB_QKV = 48
CB_CQ, CB_CKV, CB_KPE = 84, 86, 87
NCB = 88
PP = NCB * LANE
SHARD_COLS = N_IN // N_DEV
NEG = -1e30
VMEM_LIMIT = 56 * 1024 * 1024


def _column_chunks():
    out = []
    col = 0

    def seg(nblocks, block_of):
        nonlocal col
        for i in range(nblocks):
            out.append((col, LANE, block_of(i)))
            col += LANE

    seg(4, lambda j: CB_A + 4 * j)
    seg(4, lambda j: CB_A + 4 * j + 1)
    seg(4, lambda j: CB_A + 4 * j + 2)
    seg(4, lambda j: CB_A + 4 * j + 3)
    seg(2, lambda i: CB_CQ + i)
    seg(1, lambda i: CB_CKV)
    out.append((col, MLA_ROPE, CB_KPE))
    col += MLA_ROPE
    seg(4, lambda j: CB_BZ + j)
    seg(12, lambda c: CB_QKV + 3 * c)
    seg(12, lambda c: CB_QKV + 3 * c + 1)
    seg(12, lambda c: CB_QKV + 3 * c + 2)
    seg(4, lambda j: CB_CZ + j)
    seg(24, lambda i: CB_GATE + i)
    assert col == N_IN and sorted(c[2] for c in out) == list(range(NCB))
    return out


COLUMN_CHUNKS = _column_chunks()


def _pad_columns(shards):
    parts = []
    for start, width, _ in sorted(COLUMN_CHUNKS, key=lambda c: c[2]):
        fill = LANE - width
        while width:
            p, off = divmod(start, SHARD_COLS)
            n = min(width, SHARD_COLS - off)
            parts.append(shards[p, :, off:off + n])
            start, width = start + n, width - n
        if fill:
            parts.append(jnp.zeros((shards.shape[1], fill), shards.dtype))
    return jnp.concatenate(parts, axis=1)


def _unpad_columns(wp):
    pieces = [[] for _ in range(N_DEV)]
    for start, width, b in COLUMN_CHUNKS:
        src = b * LANE
        while width:
            p, off = divmod(start, SHARD_COLS)
            n = min(width, SHARD_COLS - off)
            pieces[p].append(wp[:, src:src + n])
            start, width, src = start + n, width - n, src + n
    return jnp.stack([jnp.concatenate(ps, axis=1) for ps in pieces])


def _cp():
    return pltpu.CompilerParams(vmem_limit_bytes=VMEM_LIMIT)


def _rstd(x, n):
    return lax.rsqrt(jnp.sum(x * x, axis=-1, keepdims=True) * (1.0 / n) + EPS)


def _sigmoid(z):
    return 1.0 / (1.0 + jnp.exp(-z))


def _silu(z):
    return z * _sigmoid(z)


def _dsilu(z):
    s = _sigmoid(z)
    return s * (1.0 + z * (1.0 - s))


def _mm(a, b):
    return jnp.dot(a.astype(BF16), b.astype(BF16), preferred_element_type=F32)


def _mm_nt(a, b):
    return lax.dot_general(a.astype(BF16), b.astype(BF16), (((1,), (1,)), ((), ())), preferred_element_type=F32)


def _mm_tn(a, b):
    return lax.dot_general(a.astype(BF16), b.astype(BF16), (((0,), (0,)), ((), ())), preferred_element_type=F32)


def _lane_lo(shape):
    return lax.broadcasted_iota(jnp.int32, shape, len(shape) - 1) < HALF


def _head_bcast_sum(x):
    same = _lane_lo((LANE, LANE)) == (lax.broadcasted_iota(jnp.int32, (LANE, LANE), 0) < HALF)
    ones = jnp.where(same, 1.0, 0.0).astype(jnp.bfloat16)
    total = None
    for _ in range(3):
        term = x.astype(jnp.bfloat16)
        x = x - term.astype(F32)
        part = jnp.dot(term, ones, preferred_element_type=F32)
        total = part if total is None else total + part
    return total


def _rope(t, cos, sa, sb):
    return t * cos + pltpu.roll(t, LANE - 16, axis=1) * sa + pltpu.roll(t, 16, axis=1) * sb


def _rope_t(d, cos, sa, sb):
    return d * cos + pltpu.roll(d * sa, 16, axis=1) + pltpu.roll(d * sb, LANE - 16, axis=1)


def _shift_down(u, k):
    rows = lax.broadcasted_iota(jnp.int32, u.shape, 0)
    return jnp.where(rows >= k, pltpu.roll(u, k, axis=0), 0.0)


def _shift_up(u, k):
    n = u.shape[0]
    rows = lax.broadcasted_iota(jnp.int32, u.shape, 0)
    return jnp.where(rows < n - k, pltpu.roll(u, n - k, axis=0), 0.0)


def _tile(n, want):
    t = min(n, want)
    assert n % t == 0, (n, want)
    return t


def _inproj_fwd(x, g, wp):
    T = x.shape[0]
    tm, tn = _tile(T, 1024), 512

    def body(x_ref, g_ref, w_ref, proj_ref, h_ref):
        @pl.when(pl.program_id(1) == 0)
        def _():
            xv = x_ref[...]
            h_ref[...] = (xv * _rstd(xv, D_MODEL) * g_ref[...]).astype(BF16)

        proj_ref[...] = jnp.dot(h_ref[...], w_ref[...], preferred_element_type=F32)

    return pl.pallas_call(
        body, name="inproj_fwd", grid=(T // tm, PP // tn),
        in_specs=[pl.BlockSpec((tm, D_MODEL), lambda i, j: (i, 0)),
                  pl.BlockSpec((1, D_MODEL), lambda i, j: (0, 0)),
                  pl.BlockSpec((D_MODEL, tn), lambda i, j: (0, j))],
        out_specs=[pl.BlockSpec((tm, tn), lambda i, j: (i, j)),
                   pl.BlockSpec((tm, D_MODEL), lambda i, j: (i, 0))],
        out_shape=[jax.ShapeDtypeStruct((T, PP), F32), jax.ShapeDtypeStruct((T, D_MODEL), BF16)],
        compiler_params=_cp(),
    )(x, g, wp)


def _matmul_tn(a, b, name):
    T, K = a.shape
    N = b.shape[1]
    tt, tn = _tile(T, 512), _tile(N, 1024)

    def body(a_ref, b_ref, o_ref):
        @pl.when(pl.program_id(1) == 0)
        def _():
            o_ref[...] = jnp.zeros_like(o_ref)

        o_ref[...] += _mm_tn(a_ref[...], b_ref[...])

    return pl.pallas_call(
        body, name=name, grid=(N // tn, T // tt),
        in_specs=[pl.BlockSpec((tt, K), lambda j, k: (k, 0)),
                  pl.BlockSpec((tt, tn), lambda j, k: (k, j))],
        out_specs=pl.BlockSpec((K, tn), lambda j, k: (0, j)),
        out_shape=jax.ShapeDtypeStruct((K, N), F32),
        compiler_params=_cp(),
    )(a, b)


def _inproj_bwd_x(dproj, wp, x, g, dout):
    T = x.shape[0]
    tm, tk = _tile(T, 1024), 512
    nk = PP // tk

    def body(dp_ref, w_ref, x_ref, g_ref, do_ref, dx_ref, dg_ref, acc_ref):
        i, k = pl.program_id(0), pl.program_id(1)

        @pl.when(k == 0)
        def _():
            acc_ref[...] = jnp.zeros_like(acc_ref)

        @pl.when((k == 0) & (i == 0))
        def _():
            dg_ref[...] = jnp.zeros_like(dg_ref)

        acc_ref[...] += _mm_nt(dp_ref[...], w_ref[...])

        @pl.when(k == nk - 1)
        def _():
            dh = acc_ref[...]
            xv = x_ref[...]
            r = _rstd(xv, D_MODEL)
            gy = dh * g_ref[...]
            dot = jnp.sum(xv * gy, axis=-1, keepdims=True) * (1.0 / D_MODEL)
            dx_ref[...] = do_ref[...] + r * gy - xv * (r * r * r) * dot
            dg_ref[...] += jnp.sum(dh * xv * r, axis=0, keepdims=True)

    return pl.pallas_call(
        body, name="inproj_bwd_x", grid=(T // tm, nk),
        in_specs=[pl.BlockSpec((tm, tk), lambda i, k: (i, k)),
                  pl.BlockSpec((D_MODEL, tk), lambda i, k: (0, k)),
                  pl.BlockSpec((tm, D_MODEL), lambda i, k: (i, 0)),
                  pl.BlockSpec((1, D_MODEL), lambda i, k: (0, 0)),
                  pl.BlockSpec((tm, D_MODEL), lambda i, k: (i, 0))],
        out_specs=[pl.BlockSpec((tm, D_MODEL), lambda i, k: (i, 0)),
                   pl.BlockSpec((1, D_MODEL), lambda i, k: (0, 0))],
        out_shape=[jax.ShapeDtypeStruct((T, D_MODEL), F32), jax.ShapeDtypeStruct((1, D_MODEL), F32)],
        scratch_shapes=[pltpu.VMEM((tm, D_MODEL), F32)],
        compiler_params=_cp(),
    )(dproj, wp, x, g, dout)


def _mixa_fwd(proj, cw, cb, B, S):
    nc = CONV_WIDTH // LANE
    ca = CB_A // 4

    def body(p_ref, cw_ref, cb_ref, y_ref):
        ab, ac, ax, az = (p_ref[:, i * LANE:(i + 1) * LANE] for i in range(4))
        u = ac * ax
        conv = cb_ref[...] + cw_ref[0:1, :] * _shift_down(u, 2) + cw_ref[1:2, :] * _shift_down(u, 1) + cw_ref[2:3, :] * u
        y_ref[...] = (ab * conv * _silu(az)).astype(BF16)

    return pl.pallas_call(
        body, name="mixa_fwd", grid=(B, nc),
        in_specs=[pl.BlockSpec((S, 4 * LANE), lambda b, j: (b, ca + j)),
                  pl.BlockSpec((CONV_K, LANE), lambda b, j: (0, j)),
                  pl.BlockSpec((1, LANE), lambda b, j: (0, j))],
        out_specs=pl.BlockSpec((S, LANE), lambda b, j: (b, j)),
        out_shape=jax.ShapeDtypeStruct((B * S, CONV_WIDTH), BF16),
        compiler_params=_cp(),
    )(proj, cw, cb)


def _mixa_bwd(dproj, dy, proj, cw, cb, B, S):
    nc = CONV_WIDTH // LANE
    ca = CB_A // 4

    def body(dpin_ref, dy_ref, p_ref, cw_ref, cb_ref, dp_ref, st_ref):
        del dpin_ref
        ab, ac, ax, az = (p_ref[:, i * LANE:(i + 1) * LANE] for i in range(4))
        u = ac * ax
        u1, u2 = _shift_down(u, 1), _shift_down(u, 2)
        w0, w1, w2 = cw_ref[0:1, :], cw_ref[1:2, :], cw_ref[2:3, :]
        conv = cb_ref[...] + w0 * u2 + w1 * u1 + w2 * u
        s = _silu(az)
        d = dy_ref[...]
        dconv = d * ab * s
        du = w2 * dconv + w1 * _shift_up(dconv, 1) + w0 * _shift_up(dconv, 2)
        dp_ref[:, 0:LANE] = (d * conv * s).astype(BF16)
        dp_ref[:, LANE:2 * LANE] = (du * ax).astype(BF16)
        dp_ref[:, 2 * LANE:3 * LANE] = (du * ac).astype(BF16)
        dp_ref[:, 3 * LANE:4 * LANE] = (d * ab * conv * _dsilu(az)).astype(BF16)
        row = lax.broadcasted_iota(jnp.int32, (8, LANE), 0)
        st = jnp.zeros((8, LANE), F32)
        for r, v in enumerate((dconv * u2, dconv * u1, dconv * u, dconv)):
            st = st + jnp.where(row == r, jnp.sum(v, axis=0, keepdims=True), 0.0)

        @pl.when(pl.program_id(1) == 0)
        def _():
            st_ref[...] = st

        @pl.when(pl.program_id(1) != 0)
        def _():
            st_ref[...] += st

    return pl.pallas_call(
        body, name="mixa_bwd", grid=(nc, B),
        in_specs=[pl.BlockSpec(memory_space=pl.ANY),
                  pl.BlockSpec((S, LANE), lambda j, b: (b, j)),
                  pl.BlockSpec((S, 4 * LANE), lambda j, b: (b, ca + j)),
                  pl.BlockSpec((CONV_K, LANE), lambda j, b: (0, j)),
                  pl.BlockSpec((1, LANE), lambda j, b: (0, j))],
        out_specs=[pl.BlockSpec((S, 4 * LANE), lambda j, b: (b, ca + j)),
                   pl.BlockSpec((8, LANE), lambda j, b: (0, j))],
        out_shape=[jax.ShapeDtypeStruct(dproj.shape, BF16), jax.ShapeDtypeStruct((8, CONV_WIDTH), F32)],
        input_output_aliases={0: 0},
        compiler_params=_cp(),
    )(dproj, dy, proj, cw, cb)


def _mla_prep_fwd(proj, gq, gkv, wuqp, wkp, wv, gmq, gmk, cos, sa, sb, S):
    T = proj.shape[0]
    ts = _tile(S, 512)
    ns = S // ts
    W = MLA_HEADS * LANE

    def body(p_ref, gq_ref, gkv_ref, wuq_ref, wk_ref, wv_ref, gmq_ref, gmk_ref, cos_ref, sa_ref, sb_ref,
             q_ref, k_ref, v_ref):
        cq = p_ref[:, 0:2 * LANE]
        ckv = p_ref[:, 2 * LANE:3 * LANE]
        kpe = pltpu.roll(p_ref[:, 3 * LANE:4 * LANE], HALF, axis=1)
        cqn = cq * _rstd(cq, MLA_Q_LORA) * gq_ref[...]
        ckn = (ckv * _rstd(ckv, MLA_KV_LORA) * gkv_ref[...]).astype(BF16)
        q0 = _mm(cqn, wuq_ref[...])
        kn = _mm(ckn, wk_ref[...])
        v_ref[...] = _mm(ckn, wv_ref[...]).astype(BF16)
        c, a, b = cos_ref[...], sa_ref[...], sb_ref[...]
        for h in range(MLA_HEADS):
            q0h = q0[:, h * LANE:(h + 1) * LANE]
            q_ref[h] = _rope(q0h * _rstd(q0h, MLA_QK) * gmq_ref[...], c, a, b).astype(BF16)
            k0h = kn[:, h * LANE:(h + 1) * LANE] + kpe
            k_ref[h] = _rope(k0h * _rstd(k0h, MLA_QK) * gmk_ref[...], c, a, b).astype(BF16)

    def whole(r, c):
        return pl.BlockSpec((r, c), lambda i: (0, 0))

    tab = pl.BlockSpec((ts, LANE), lambda i: (i % ns, 0))
    return pl.pallas_call(
        body, name="mla_prep_fwd", grid=(T // ts,),
        in_specs=[pl.BlockSpec((ts, 4 * LANE), lambda i: (i, CB_CQ // 4)),
                  whole(1, MLA_Q_LORA), whole(1, MLA_KV_LORA), whole(MLA_Q_LORA, W), whole(MLA_KV_LORA, W),
                  whole(MLA_KV_LORA, MLA_HEADS * MLA_V), whole(1, LANE), whole(1, LANE), tab, tab, tab],
        out_specs=[pl.BlockSpec((MLA_HEADS, ts, LANE), lambda i: (0, i, 0)),
                   pl.BlockSpec((MLA_HEADS, ts, LANE), lambda i: (0, i, 0)),
                   pl.BlockSpec((ts, MLA_HEADS * MLA_V), lambda i: (i, 0))],
        out_shape=[jax.ShapeDtypeStruct((MLA_HEADS, T, LANE), BF16), jax.ShapeDtypeStruct((MLA_HEADS, T, LANE), BF16),
                   jax.ShapeDtypeStruct((T, MLA_HEADS * MLA_V), BF16)],
        compiler_params=_cp(),
    )(proj, gq, gkv, wuqp, wkp, wv, gmq, gmk, cos, sa, sb)


def _mla_prep_bwd(dproj, dq, dk, dv, proj, gq, gkv, wuqp, wkp, wv, gmq, gmk, cos, sa, sb, S):
    T = proj.shape[0]
    ts = _tile(S, 256)
    ns = S // ts
    W = MLA_HEADS * LANE

    def body(dpin_ref, dq_ref, dk_ref, dv_ref, p_ref, gq_ref, gkv_ref, wuq_ref, wk_ref, wv_ref, gmq_ref, gmk_ref,
             cos_ref, sa_ref, sb_ref,
             dp_ref, dwuq_ref, dwk_ref, dwv_ref, dgq_ref, dgkv_ref, dgmq_ref, dgmk_ref, dq0_ref, dkn_ref):
        del dpin_ref

        @pl.when(pl.program_id(0) == 0)
        def _():
            for r in (dwuq_ref, dwk_ref, dwv_ref, dgq_ref, dgkv_ref, dgmq_ref, dgmk_ref):
                r[...] = jnp.zeros_like(r)

        cq = p_ref[:, 0:2 * LANE]
        ckv = p_ref[:, 2 * LANE:3 * LANE]
        kpe = pltpu.roll(p_ref[:, 3 * LANE:4 * LANE], HALF, axis=1)
        rq = _rstd(cq, MLA_Q_LORA)
        rkv = _rstd(ckv, MLA_KV_LORA)
        gq, gkv, gmq, gmk = gq_ref[...], gkv_ref[...], gmq_ref[...], gmk_ref[...]
        cqn = (cq * rq * gq).astype(BF16)
        ckn = (ckv * rkv * gkv).astype(BF16)
        q0 = _mm(cqn, wuq_ref[...])
        kn = _mm(ckn, wk_ref[...])
        c, a, b = cos_ref[...], sa_ref[...], sb_ref[...]
        lane = lax.broadcasted_iota(jnp.int32, (ts, LANE), 1)
        dgmq = jnp.zeros((1, LANE), F32)
        dgmk = jnp.zeros((1, LANE), F32)
        dkpe = jnp.zeros((ts, LANE), F32)
        for h in range(MLA_HEADS):
            q0h = q0[:, h * LANE:(h + 1) * LANE]
            r = _rstd(q0h, MLA_QK)
            d1 = _rope_t(dq_ref[h], c, a, b)
            gy = d1 * gmq
            dq0_ref[:, h * LANE:(h + 1) * LANE] = (
                r * gy - q0h * (r * r * r) * (jnp.sum(q0h * gy, axis=-1, keepdims=True) * (1.0 / MLA_QK))).astype(BF16)
            dgmq = dgmq + jnp.sum(d1 * q0h * r, axis=0, keepdims=True)
            k0h = kn[:, h * LANE:(h + 1) * LANE] + kpe
            r = _rstd(k0h, MLA_QK)
            d1 = _rope_t(dk_ref[h], c, a, b)
            gy = d1 * gmk
            dk0 = r * gy - k0h * (r * r * r) * (jnp.sum(k0h * gy, axis=-1, keepdims=True) * (1.0 / MLA_QK))
            dgmk = dgmk + jnp.sum(d1 * k0h * r, axis=0, keepdims=True)
            dkn_ref[:, h * LANE:(h + 1) * LANE] = jnp.where(lane < MLA_NOPE, dk0, 0.0).astype(BF16)
            dkpe = dkpe + jnp.where((lane >= MLA_NOPE) & (lane < MLA_QK), dk0, 0.0)
        dq0 = dq0_ref[...]
        dkn = dkn_ref[...]
        dvv = dv_ref[...]
        dwuq_ref[...] += _mm_tn(cqn, dq0)
        dwk_ref[...] += _mm_tn(ckn, dkn)
        dwv_ref[...] += _mm_tn(ckn, dvv)
        dgmq_ref[...] += dgmq
        dgmk_ref[...] += dgmk
        dcqn = _mm_nt(dq0, wuq_ref[...])
        gy = dcqn * gq
        dp_ref[:, 0:2 * LANE] = (
            rq * gy - cq * (rq * rq * rq) * (jnp.sum(cq * gy, axis=-1, keepdims=True) * (1.0 / MLA_Q_LORA))).astype(BF16)
        dgq_ref[...] += jnp.sum(dcqn * cq * rq, axis=0, keepdims=True)
        dckn = _mm_nt(dkn, wk_ref[...]) + _mm_nt(dvv, wv_ref[...])
        gy = dckn * gkv
        dp_ref[:, 2 * LANE:3 * LANE] = (
            rkv * gy - ckv * (rkv * rkv * rkv) * (jnp.sum(ckv * gy, axis=-1, keepdims=True) * (1.0 / MLA_KV_LORA))).astype(BF16)
        dgkv_ref[...] += jnp.sum(dckn * ckv * rkv, axis=0, keepdims=True)
        dp_ref[:, 3 * LANE:4 * LANE] = pltpu.roll(dkpe, HALF, axis=1).astype(BF16)

    def whole(r, c):
        return pl.BlockSpec((r, c), lambda i: (0, 0))

    tab = pl.BlockSpec((ts, LANE), lambda i: (i % ns, 0))
    heads = pl.BlockSpec((MLA_HEADS, ts, LANE), lambda i: (0, i, 0))
    return pl.pallas_call(
        body, name="mla_prep_bwd", grid=(T // ts,),
        in_specs=[pl.BlockSpec(memory_space=pl.ANY), heads, heads,
                  pl.BlockSpec((ts, MLA_HEADS * MLA_V), lambda i: (i, 0)),
                  pl.BlockSpec((ts, 4 * LANE), lambda i: (i, CB_CQ // 4)),
                  whole(1, MLA_Q_LORA), whole(1, MLA_KV_LORA), whole(MLA_Q_LORA, W), whole(MLA_KV_LORA, W),
                  whole(MLA_KV_LORA, MLA_HEADS * MLA_V), whole(1, LANE), whole(1, LANE), tab, tab, tab],
        out_specs=[pl.BlockSpec((ts, 4 * LANE), lambda i: (i, CB_CQ // 4)),
                   whole(MLA_Q_LORA, W), whole(MLA_KV_LORA, W), whole(MLA_KV_LORA, MLA_HEADS * MLA_V),
                   whole(1, MLA_Q_LORA), whole(1, MLA_KV_LORA), whole(1, LANE), whole(1, LANE)],
        out_shape=[jax.ShapeDtypeStruct(dproj.shape, BF16),
                   jax.ShapeDtypeStruct((MLA_Q_LORA, W), F32), jax.ShapeDtypeStruct((MLA_KV_LORA, W), F32),
                   jax.ShapeDtypeStruct((MLA_KV_LORA, MLA_HEADS * MLA_V), F32),
                   jax.ShapeDtypeStruct((1, MLA_Q_LORA), F32), jax.ShapeDtypeStruct((1, MLA_KV_LORA), F32),
                   jax.ShapeDtypeStruct((1, LANE), F32), jax.ShapeDtypeStruct((1, LANE), F32)],
        scratch_shapes=[pltpu.VMEM((ts, W), BF16), pltpu.VMEM((ts, W), BF16)],
        input_output_aliases={0: 0},
        compiler_params=_cp(),
    )(dproj, dq, dk, dv, proj, gq, gkv, wuqp, wkp, wv, gmq, gmk, cos, sa, sb)


def _dil_prep_fwd(proj, gq, gk):
    T = proj.shape[0]
    ts = _tile(T, 512)
    nc = DIL_QK // LANE

    def body(p_ref, gq_ref, gk_ref, q_ref, k_ref):
        for i, (g_ref, o_ref) in enumerate(((gq_ref, q_ref), (gk_ref, k_ref))):
            t = p_ref[:, i * LANE:(i + 1) * LANE]
            r = lax.rsqrt(_head_bcast_sum(t * t) * (1.0 / DIL_HEAD_DIM) + EPS)
            o_ref[...] = t * r * g_ref[...]

    col = pl.BlockSpec((1, LANE), lambda i, c: (0, c))
    out = pl.BlockSpec((ts, LANE), lambda i, c: (i, c))
    return pl.pallas_call(
        body, name="dil_prep_fwd", grid=(T // ts, nc),
        in_specs=[pl.BlockSpec((ts, 3 * LANE), lambda i, c: (i, CB_QKV // 3 + c)), col, col],
        out_specs=[out, out],
        out_shape=[jax.ShapeDtypeStruct((T, DIL_QK), F32)] * 2,
        compiler_params=_cp(),
    )(proj, gq, gk)


def _dil_prep_bwd(dproj, ddq, ddk, ddv, proj, gq, gk):
    T = proj.shape[0]
    ts = _tile(T, 512)
    nc = DIL_QK // LANE

    def body(dpin_ref, ddq_ref, ddk_ref, ddv_ref, p_ref, gq_ref, gk_ref, dp_ref, dgq_ref, dgk_ref):
        del dpin_ref
        first = pl.program_id(1) == 0
        dp_ref[:, 2 * LANE:3 * LANE] = ddv_ref[...].astype(BF16)
        for i, (d_ref, g_ref, dg_ref) in enumerate(((ddq_ref, gq_ref, dgq_ref), (ddk_ref, gk_ref, dgk_ref))):
            t = p_ref[:, i * LANE:(i + 1) * LANE]
            r = lax.rsqrt(_head_bcast_sum(t * t) * (1.0 / DIL_HEAD_DIM) + EPS)
            d = d_ref[...]
            gy = d * g_ref[...]
            dp_ref[:, i * LANE:(i + 1) * LANE] = (
                r * gy - t * (r * r * r) * (_head_bcast_sum(t * gy) * (1.0 / DIL_HEAD_DIM))).astype(BF16)
            part = jnp.sum(d * t * r, axis=0, keepdims=True)

            @pl.when(first)
            def _():
                dg_ref[...] = part

            @pl.when(jnp.logical_not(first))
            def _():
                dg_ref[...] += part

    col = pl.BlockSpec((1, LANE), lambda c, i: (0, c))
    tok = pl.BlockSpec((ts, LANE), lambda c, i: (i, c))
    return pl.pallas_call(
        body, name="dil_prep_bwd", grid=(nc, T // ts),
        in_specs=[pl.BlockSpec(memory_space=pl.ANY), tok, tok, tok,
                  pl.BlockSpec((ts, 3 * LANE), lambda c, i: (i, CB_QKV // 3 + c)), col, col],
        out_specs=[pl.BlockSpec((ts, 3 * LANE), lambda c, i: (i, CB_QKV // 3 + c)), col, col],
        out_shape=[jax.ShapeDtypeStruct(dproj.shape, BF16), jax.ShapeDtypeStruct((1, DIL_QK), F32),
                   jax.ShapeDtypeStruct((1, DIL_QK), F32)],
        input_output_aliases={0: 0},
        compiler_params=_cp(),
    )(dproj, ddq, ddk, ddv, proj, gq, gk)


COPY_ROWS = 256


def _to_classes(src_ref, dst_ref, d, L, scale=None):
    n = min(L, COPY_ROWS)
    for r in range(d):
        for c0 in range(0, L, n):
            rows = pl.ds(r + c0 * d, n, stride=d) if d > 1 else pl.ds(c0, n)
            val = src_ref[rows, :]
            if scale is not None:
                val = val * scale
            dst_ref[r * L + c0:r * L + c0 + n, :] = val.astype(dst_ref.dtype)


def _from_classes(src_ref, dst_ref, d, L):
    n = min(L, COPY_ROWS)
    for r in range(d):
        for c0 in range(0, L, n):
            rows = pl.ds(r + c0 * d, n, stride=d) if d > 1 else pl.ds(c0, n)
            dst_ref[rows, :] = src_ref[r * L + c0:r * L + c0 + n, :].astype(dst_ref.dtype)


def _causal_bias(t):
    return jnp.where(lax.broadcasted_iota(jnp.int32, (t, t), 0) >= lax.broadcasted_iota(jnp.int32, (t, t), 1), 0.0, NEG)


def _mla_specs(S):
    heads = pl.BlockSpec((2, S, LANE), lambda b, j: (j, b, 0))
    pair = pl.BlockSpec((S, LANE), lambda b, j: (b, j))
    return heads, pair


def _mla_attn_fwd(q, k, v, B, S):
    t = _tile(S, 256)
    scale = MLA_QK ** -0.5
    heads, pair = _mla_specs(S)

    def body(q_ref, k_ref, v_ref, o_ref, lse_ref):
        lo = _lane_lo((t, LANE))
        tri = _causal_bias(t)

        def block(g, _):
            row0 = pl.multiple_of(g * t, t)
            rows = pl.ds(row0, t)
            res = []
            for hh in range(2):
                qh = q_ref[hh, rows, :]

                def step(off, carry, bias, hh=hh, qh=qh):
                    m, l, acc = carry
                    s = _mm_nt(qh, k_ref[hh, pl.ds(off, t), :]) * scale
                    if bias is not None:
                        s = s + bias
                    m_new = jnp.maximum(m, jnp.max(s, axis=-1, keepdims=True))
                    p = jnp.exp(s - m_new)
                    a = jnp.exp(m - m_new)
                    l = a * l + jnp.sum(p, axis=-1, keepdims=True)
                    acc = a * acc + _mm(p, v_ref[pl.ds(off, t), :])
                    return m_new, l, acc

                init = (jnp.full((t, 1), NEG, F32), jnp.zeros((t, 1), F32), jnp.zeros((t, LANE), F32))
                carry = lax.fori_loop(0, g, lambda i, c: step(pl.multiple_of(i * t, t), c, None), init)
                m, l, acc = step(row0, carry, tri)
                res.append((acc / l, m + jnp.log(l)))
            o_ref[rows, :] = jnp.where(lo, res[0][0], res[1][0])
            lse_ref[rows, :] = jnp.where(lo, res[0][1], res[1][1])
            return 0

        lax.fori_loop(0, S // t, block, 0)

    return pl.pallas_call(
        body, name="mla_attn_fwd", grid=(B, NPAIR), in_specs=[heads, heads, pair], out_specs=[pair, pair],
        out_shape=[jax.ShapeDtypeStruct((B * S, MLA_HEADS * MLA_V), F32)] * 2,
        compiler_params=_cp(),
    )(q, k, v)


def _dil_geometry(gi, S):
    span, d = DIL_PATTERNS[gi]
    L = S // d
    t = _tile(L, 128)
    window = span // d
    back = min(-(-window // t) * t, L - t)
    return d, L, t, window, back


def _dil_specs(gi, S):
    qk = pl.BlockSpec((S, LANE), lambda b, j: (b, NPAIR * gi + j))
    v = pl.BlockSpec((S, LANE), lambda b, j: (b, CB_QKV + 3 * (NPAIR * gi + j) + 2))
    pair = pl.BlockSpec((S, LANE), lambda b, j: (b, j))
    return qk, v, pair


def _dil_bias(bias_ref, sl_ref, j, t, kw, back, window):
    row = lax.broadcasted_iota(jnp.int32, (2 * t, kw), 0)
    col = lax.broadcasted_iota(jnp.int32, (2 * t, kw), 1)
    second = row >= t
    slope = jnp.where(second, sl_ref[j, 1], sl_ref[j, 0])
    for n in range(bias_ref.shape[0]):
        dist = jnp.where(second, row - t, row) + n * back - col
        bias_ref[n] = jnp.where((dist >= 0) & (dist <= window), -slope * dist.astype(F32), NEG)


def _stack_heads(x, lo):
    zero = jnp.zeros((), x.dtype)
    return jnp.concatenate([jnp.where(lo, x, zero), jnp.where(lo, zero, x)], axis=0)


def _dil_attn_fwd(gi, slopes, qn, kn, proj, B, S):
    d, L, t, window, back = _dil_geometry(gi, S)
    kw, nq = back + t, L // t
    nbias = 2 if back else 1
    qk, vspec, pair = _dil_specs(gi, S)

    def body(sl_ref, q_ref, k_ref, v_ref, o_ref, lse_ref, qs, ks, vs, os_, ls, bias_ref):
        _to_classes(q_ref, qs, d, L, DIL_HEAD_DIM ** -0.5)
        _to_classes(k_ref, ks, d, L)
        _to_classes(v_ref, vs, d, L)
        _dil_bias(bias_ref, sl_ref, pl.program_id(1), t, kw, back, window)
        lo = _lane_lo((t, LANE))

        def block(g, _):
            qb = g % nq if d > 1 else g
            row0 = pl.multiple_of(g * t, t)
            rows = pl.ds(row0, t)
            early = qb * t < back
            keys = pl.ds(pl.multiple_of(jnp.where(early, row0 - qb * t, row0 - back), t), kw)
            s = _mm_nt(_stack_heads(qs[rows, :], lo), ks[keys, :]) + bias_ref[jnp.where(early, 0, nbias - 1)]
            m = jnp.max(s, axis=-1, keepdims=True)
            p = jnp.exp(s - m)
            l = jnp.sum(p, axis=-1, keepdims=True)
            o2 = _mm(p, vs[keys, :]) / l
            lse2 = m + jnp.log(l)
            os_[rows, :] = jnp.where(lo, o2[:t], o2[t:])
            ls[rows, :] = jnp.where(lo, lse2[:t], lse2[t:])
            return 0

        lax.fori_loop(0, d * nq, block, 0)
        _from_classes(os_, o_ref, d, L)
        _from_classes(ls, lse_ref, d, L)

    return pl.pallas_call(
        body, name=f"dil_attn_fwd_{gi}", grid=(B, NPAIR),
        in_specs=[pl.BlockSpec(memory_space=pltpu.SMEM), qk, qk, vspec], out_specs=[pair, pair],
        out_shape=[jax.ShapeDtypeStruct((B * S, DIL_WIDTH), F32)] * 2,
        scratch_shapes=[pltpu.VMEM((S, LANE), BF16)] * 3 + [pltpu.VMEM((S, LANE), F32)] * 2
                       + [pltpu.VMEM((nbias, 2 * t, kw), F32)],
        compiler_params=_cp(),
    )(slopes, qn, kn, proj)


def _mla_attn_bwd(q, k, v, do, lse, delta, B, S):
    T = B * S
    t = _tile(S, 256)
    scale = MLA_QK ** -0.5
    heads, pair = _mla_specs(S)

    def body(q_ref, k_ref, v_ref, do_ref, lse_ref, dl_ref, dq_ref, dk_ref, dv_ref):
        dk_ref[...] = jnp.zeros_like(dk_ref)
        dv_ref[...] = jnp.zeros_like(dv_ref)
        lo = _lane_lo((t, LANE))
        tri = _causal_bias(t)

        def block(g, _):
            row0 = pl.multiple_of(g * t, t)
            rows = pl.ds(row0, t)
            for hh in range(2):
                sel = lo if hh == 0 else jnp.logical_not(lo)
                qh = q_ref[hh, rows, :]
                doh = jnp.where(sel, do_ref[rows, :], jnp.zeros((), BF16))
                lse_h = jnp.max(jnp.where(sel, lse_ref[rows, :], NEG), axis=-1, keepdims=True)
                dl_h = jnp.max(jnp.where(sel, dl_ref[rows, :], NEG), axis=-1, keepdims=True)

                def step(off, dq_acc, bias, hh=hh, qh=qh, doh=doh, lse_h=lse_h, dl_h=dl_h):
                    cols = pl.ds(off, t)
                    kh = k_ref[hh, cols, :]
                    s = _mm_nt(qh, kh) * scale
                    if bias is not None:
                        s = s + bias
                    p = jnp.exp(s - lse_h)
                    dp = _mm_nt(doh, v_ref[cols, :])
                    ds = (p * (dp - dl_h) * scale).astype(BF16)
                    dk_ref[hh, cols, :] += _mm_tn(ds, qh)
                    dv_ref[cols, :] += _mm_tn(p, doh)
                    return dq_acc + _mm(ds, kh)

                dq_acc = lax.fori_loop(0, g, lambda i, a: step(pl.multiple_of(i * t, t), a, None),
                                       jnp.zeros((t, LANE), F32))
                dq_ref[hh, rows, :] = step(row0, dq_acc, tri)
            return 0

        lax.fori_loop(0, S // t, block, 0)

    return pl.pallas_call(
        body, name="mla_attn_bwd", grid=(B, NPAIR), in_specs=[heads, heads, pair, pair, pair, pair],
        out_specs=[heads, heads, pair],
        out_shape=[jax.ShapeDtypeStruct((MLA_HEADS, T, LANE), F32), jax.ShapeDtypeStruct((MLA_HEADS, T, LANE), F32),
                   jax.ShapeDtypeStruct((T, MLA_HEADS * MLA_V), F32)],
        compiler_params=_cp(),
    )(q, k, v, do, lse, delta)


def _dil_attn_bwd(gi, slopes, qn, kn, proj, do, lse, delta, through, B, S):
    d, L, t, window, back = _dil_geometry(gi, S)
    kw, nq = back + t, L // t
    nbias = 2 if back else 1
    scale = DIL_HEAD_DIM ** -0.5
    qk, vspec, pair = _dil_specs(gi, S)

    def body(*refs):
        refs = list(refs)
        sl_ref, q_ref, k_ref, v_ref, do_ref, lse_ref, dl_ref = refs[:7]
        dq_ref, dk_ref, dv_ref, qs, ks, vs, dos, lss, dls, dqs, dks, dvs, bias_ref = refs[-13:]
        _to_classes(q_ref, qs, d, L, scale)
        for src, dst in ((k_ref, ks), (v_ref, vs), (do_ref, dos), (lse_ref, lss), (dl_ref, dls)):
            _to_classes(src, dst, d, L)
        _dil_bias(bias_ref, sl_ref, pl.program_id(1), t, kw, back, window)
        dks[...] = jnp.zeros_like(dks)
        dvs[...] = jnp.zeros_like(dvs)
        lo = _lane_lo((t, LANE))

        def stats(ref, rows):
            x = ref[rows, :]
            return jnp.concatenate([jnp.max(jnp.where(lo, x, NEG), axis=-1, keepdims=True),
                                    jnp.max(jnp.where(lo, NEG, x), axis=-1, keepdims=True)], axis=0)

        def block(g, _):
            qb = g % nq if d > 1 else g
            row0 = pl.multiple_of(g * t, t)
            rows = pl.ds(row0, t)
            early = qb * t < back
            keys = pl.ds(pl.multiple_of(jnp.where(early, row0 - qb * t, row0 - back), t), kw)
            q2 = _stack_heads(qs[rows, :], lo)
            do2 = _stack_heads(dos[rows, :], lo)
            kt = ks[keys, :]
            s = _mm_nt(q2, kt) + bias_ref[jnp.where(early, 0, nbias - 1)]
            p = jnp.exp(s - stats(lss, rows))
            ds = (p * (_mm_nt(do2, vs[keys, :]) - stats(dls, rows))).astype(BF16)
            dq2 = _mm(ds, kt) * scale
            dqs[rows, :] = jnp.where(lo, dq2[:t], dq2[t:])
            dks[keys, :] += _mm_tn(ds, q2)
            dvs[keys, :] += _mm_tn(p, do2)
            return 0

        lax.fori_loop(0, d * nq, block, 0)
        for src, dst in ((dqs, dq_ref), (dks, dk_ref), (dvs, dv_ref)):
            _from_classes(src, dst, d, L)

    in_specs = [pl.BlockSpec(memory_space=pltpu.SMEM), qk, qk, vspec, pair, pair, pair]
    args = [slopes, qn, kn, proj, do, lse, delta]
    aliases = {}
    if through is not None:
        aliases = {len(args) + i: i for i in range(3)}
        in_specs = in_specs + [pl.BlockSpec(memory_space=pl.ANY)] * 3
        args = args + list(through)
    return pl.pallas_call(
        body, name=f"dil_attn_bwd_{gi}", grid=(B, NPAIR), in_specs=in_specs, out_specs=[qk, qk, qk],
        out_shape=[jax.ShapeDtypeStruct((B * S, DIL_QK), F32)] * 3,
        scratch_shapes=[pltpu.VMEM((S, LANE), BF16)] * 4 + [pltpu.VMEM((S, LANE), F32)] * 5
                       + [pltpu.VMEM((nbias, 2 * t, kw), F32)],
        input_output_aliases=aliases,
        compiler_params=_cp(),
    )(*args)


def _merge_common(p_ref, bg_ref, ob_ref, og_refs, lse_refs):
    bz = p_ref[:, CB_BZ * LANE:(CB_BZ + 4) * LANE]
    cz = p_ref[:, CB_CZ * LANE:(CB_CZ + 4) * LANE]
    gates = [_sigmoid(p_ref[:, (CB_GATE + 8 * i) * LANE:(CB_GATE + 8 * i + 8) * LANE]
                      + bg_ref[:, i * D_MODEL:(i + 1) * D_MODEL]) for i in range(3)]
    ob = ob_ref[...]
    lses = [r[...] for r in lse_refs]
    mx = jnp.maximum(jnp.maximum(lses[0], lses[1]), lses[2])
    es = [jnp.exp(v - mx) for v in lses]
    inv = 1.0 / (es[0] + es[1] + es[2])
    alphas = [e * inv for e in es]
    oc = alphas[0] * og_refs[0][...] + alphas[1] * og_refs[1][...] + alphas[2] * og_refs[2][...]
    return bz, cz, gates, ob, alphas, oc


def _merge_fwd(x, proj, b_gate, ya, ob, ogs, lses, woa, wob, woc, wo):
    T = x.shape[0]
    ts = _tile(T, 256)
    MW = 32 * LANE

    def body(x_ref, p_ref, bg_ref, ya_ref, ob_ref, og0, og1, og2, l0, l1, l2, woa_ref, wob_ref, woc_ref, wo_ref, out_ref):
        bz, cz, gates, obv, alphas, oc = _merge_common(p_ref, bg_ref, ob_ref, (og0, og1, og2), (l0, l1, l2))
        yb = obv * _silu(bz)
        yc = oc * _silu(cz)
        merged = (gates[0] * _mm(ya_ref[...], woa_ref[...]) + gates[1] * _mm(yb, wob_ref[...])
                  + gates[2] * _mm(yc, woc_ref[...]))
        out_ref[...] = x_ref[...] + _mm(merged, wo_ref[...])

    def whole(r, c):
        return pl.BlockSpec((r, c), lambda i: (0, 0))

    tok = lambda w: pl.BlockSpec((ts, w), lambda i: (i, 0))
    return pl.pallas_call(
        body, name="merge_fwd", grid=(T // ts,),
        in_specs=[tok(D_MODEL), tok(MW), whole(1, 3 * D_MODEL), tok(CONV_WIDTH)] + [tok(DIL_WIDTH)] * 7
                 + [whole(CONV_WIDTH, D_MODEL)] * 3 + [whole(D_MODEL, D_MODEL)],
        out_specs=tok(D_MODEL),
        out_shape=jax.ShapeDtypeStruct((T, D_MODEL), F32),
        compiler_params=_cp(),
    )(x, proj, b_gate, ya, ob, *ogs, *lses, woa, wob, woc, wo)


def _merge_bwd(dout, proj, b_gate, ya, ob, ogs, lses, woa, wob, woc, wo):
    T = dout.shape[0]
    ts = _tile(T, 256)
    MW = 32 * LANE

    def body(do_ref, p_ref, bg_ref, ya_ref, ob_ref, og0, og1, og2, l0, l1, l2, woa_ref, wob_ref, woc_ref, wo_ref,
             dp_ref, dya_ref, dob_ref, dlb_ref, dg0, dg1, dg2, dl0, dl1, dl2,
             mg_ref, dpa_ref, dpb_ref, dpc_ref, yb_ref, yc_ref, dbg_ref):
        bz, cz, gates, obv, alphas, oc = _merge_common(p_ref, bg_ref, ob_ref, (og0, og1, og2), (l0, l1, l2))
        sb, sc = _silu(bz), _silu(cz)
        yb = obv * sb
        yc = oc * sc
        ps = [_mm(ya_ref[...], woa_ref[...]), _mm(yb, wob_ref[...]), _mm(yc, woc_ref[...])]
        mg_ref[...] = (gates[0] * ps[0] + gates[1] * ps[1] + gates[2] * ps[2]).astype(BF16)
        yb_ref[...] = yb.astype(BF16)
        yc_ref[...] = yc.astype(BF16)
        dm = _mm_nt(do_ref[...], wo_ref[...])
        dps = []
        first = pl.program_id(0) == 0
        for i, dref in enumerate((dpa_ref, dpb_ref, dpc_ref)):
            g = gates[i]
            dpi = (dm * g).astype(BF16)
            dref[...] = dpi
            dps.append(dpi)
            dgp = dm * ps[i] * g * (1.0 - g)
            dp_ref[:, (CB_GATE + 8 * i) * LANE:(CB_GATE + 8 * i + 8) * LANE] = dgp.astype(BF16)
            part = jnp.sum(dgp, axis=0, keepdims=True)

            @pl.when(first)
            def _():
                dbg_ref[:, i * D_MODEL:(i + 1) * D_MODEL] = part

            @pl.when(jnp.logical_not(first))
            def _():
                dbg_ref[:, i * D_MODEL:(i + 1) * D_MODEL] += part

        dya_ref[...] = _mm_nt(dps[0], woa_ref[...])
        dyb = _mm_nt(dps[1], wob_ref[...])
        dyc = _mm_nt(dps[2], woc_ref[...])
        dp_ref[:, CB_BZ * LANE:(CB_BZ + 4) * LANE] = (dyb * obv * _dsilu(bz)).astype(BF16)
        dp_ref[:, CB_CZ * LANE:(CB_CZ + 4) * LANE] = (dyc * oc * _dsilu(cz)).astype(BF16)
        dob = dyb * sb
        doc = dyc * sc
        dob_ref[...] = dob.astype(BF16)
        for c in range(NPAIR):
            cs = slice(c * LANE, (c + 1) * LANE)
            dlb_ref[:, cs] = _head_bcast_sum(dob[:, cs] * obv[:, cs])
            dd = _head_bcast_sum(doc[:, cs] * oc[:, cs])
            for a, dref, lref in zip(alphas, (dg0, dg1, dg2), (dl0, dl1, dl2)):
                dref[:, cs] = a[:, cs] * doc[:, cs]
                lref[:, cs] = a[:, cs] * dd

    def whole(r, c):
        return pl.BlockSpec((r, c), lambda i: (0, 0))

    tok = lambda w: pl.BlockSpec((ts, w), lambda i: (i, 0))
    sd = jax.ShapeDtypeStruct
    W = DIL_WIDTH
    return pl.pallas_call(
        body, name="merge_bwd", grid=(T // ts,),
        in_specs=[tok(D_MODEL), tok(MW), whole(1, 3 * D_MODEL), tok(CONV_WIDTH)] + [tok(W)] * 7
                 + [whole(CONV_WIDTH, D_MODEL)] * 3 + [whole(D_MODEL, D_MODEL)],
        out_specs=[tok(MW), tok(CONV_WIDTH), tok(W), tok(W)] + [tok(W)] * 6
                  + [tok(D_MODEL)] * 4 + [tok(W), tok(W), whole(1, 3 * D_MODEL)],
        out_shape=[sd((T, PP), BF16), sd((T, CONV_WIDTH), F32), sd((T, W), BF16), sd((T, W), F32)]
                  + [sd((T, W), F32)] * 6
                  + [sd((T, D_MODEL), BF16)] * 4 + [sd((T, W), BF16)] * 2 + [sd((1, 3 * D_MODEL), F32)],
        compiler_params=_cp(),
    )(dout, proj, b_gate, ya, ob, *ogs, *lses, woa, wob, woc, wo)


def _loss_head(y, target):
    T = y.shape[0]
    ts = _tile(T, 512)

    def body(y_ref, t_ref, d_ref, l_ref):
        e = y_ref[...] - t_ref[...]
        d_ref[...] = e * (1.0 / D_MODEL)
        l_ref[...] = jnp.zeros((1, 8, LANE), F32) + jnp.sum(e * e)

    tok = pl.BlockSpec((ts, D_MODEL), lambda i: (i, 0))
    return pl.pallas_call(
        body, name="loss_head", grid=(T // ts,), in_specs=[tok, tok],
        out_specs=[tok, pl.BlockSpec((1, 8, LANE), lambda i: (i, 0, 0))],
        out_shape=[jax.ShapeDtypeStruct((T, D_MODEL), F32), jax.ShapeDtypeStruct((T // ts, 8, LANE), F32)],
        compiler_params=_cp(),
    )(y, target)


def _my_index():
    return 4 * lax.axis_index("x") + 2 * lax.axis_index("y") + lax.axis_index("c")


def _peers():
    x, y, c = (lax.axis_index(a) for a in AXES)
    out = []
    for kk in range(1, N_DEV):
        px = 1 - x if kk & 4 else x
        py = 1 - y if kk & 2 else y
        pc = 1 - c if kk & 1 else c
        out.append(((px, py, pc), 4 * px + 2 * py + pc))
    return out


def _exchange(arrays, name, gather):
    n = len(arrays)

    def body(*refs):
        srcs, outs = refs[:n], refs[n:2 * n]
        send_sems, recv_sems, local_sems = refs[2 * n:]
        me = _my_index()
        peers = _peers()
        started = []
        for a, (src, out) in enumerate(zip(srcs, outs)):
            mine = pltpu.make_async_copy(src if gather else src.at[me], out.at[me], local_sems.at[a])
            mine.start()
            started.append(mine)
        sends = []
        for i, (pos, idx) in enumerate(peers):
            for a, (src, out) in enumerate(zip(srcs, outs)):
                cp = pltpu.make_async_remote_copy(
                    src_ref=src if gather else src.at[idx], dst_ref=out.at[me], send_sem=send_sems.at[a, i],
                    recv_sem=recv_sems.at[a, i], device_id=pos, device_id_type=pl.DeviceIdType.MESH)
                cp.start()
                sends.append(cp)
        for i, (pos, idx) in enumerate(peers):
            for a, (src, out) in enumerate(zip(srcs, outs)):
                pltpu.make_async_remote_copy(
                    src_ref=src if gather else src.at[idx], dst_ref=out.at[idx], send_sem=send_sems.at[a, i],
                    recv_sem=recv_sems.at[a, i], device_id=pos, device_id_type=pl.DeviceIdType.MESH).wait_recv()
        for cp in sends:
            cp.wait_send()
        for mine in started:
            mine.wait()

    any_space = pl.BlockSpec(memory_space=pl.ANY)
    return pl.pallas_call(
        body, name=name, in_specs=[any_space] * n, out_specs=[any_space] * n,
        out_shape=[jax.ShapeDtypeStruct(((N_DEV,) + a.shape) if gather else a.shape, a.dtype) for a in arrays],
        scratch_shapes=[pltpu.SemaphoreType.DMA((n, N_DEV - 1)), pltpu.SemaphoreType.DMA((n, N_DEV - 1)),
                        pltpu.SemaphoreType.DMA((n,))],
    )(*arrays)


def _remote_copies(srcs, lands, send_sems, recv_sems, gather):
    me = _my_index()
    out = []
    for i, (pos, idx) in enumerate(_peers()):
        for a, (src, land) in enumerate(zip(srcs, lands)):
            def copy(slot, a=a, src=src, land=land, i=i, pos=pos, idx=idx):
                return pltpu.make_async_remote_copy(
                    src_ref=src if gather else src.at[idx], dst_ref=land.at[slot],
                    send_sem=send_sems.at[a * (N_DEV - 1) + i], recv_sem=recv_sems.at[a * (N_DEV - 1) + i],
                    device_id=pos, device_id_type=pl.DeviceIdType.MESH)
            out.append((copy(me), copy(idx)))
    return out


def _exchange_start(arrays, name, gather):
    n = len(arrays)
    hbm = pl.BlockSpec(memory_space=pltpu.HBM)
    sem = pl.BlockSpec(memory_space=pltpu.SEMAPHORE)
    lands = [lax.empty(((N_DEV,) + a.shape) if gather else a.shape, a.dtype) for a in arrays]

    def body(*refs):
        srcs, lands_ = refs[:n], refs[n:2 * n]
        send_sems, recv_sems = refs[2 * n:2 * n + 2]
        for mine, _ in _remote_copies(srcs, lands_, send_sems, recv_sems, gather):
            mine.start()
        refs[-1][...] = jnp.zeros_like(refs[-1])

    sems = pltpu.SemaphoreType.DMA((n * (N_DEV - 1),))
    buffers = [pltpu.HBM(a.shape, a.dtype) for a in list(arrays) + lands]
    res = pl.pallas_call(
        body, name=name, in_specs=[hbm] * (2 * n), out_specs=[sem, sem] + [hbm] * (2 * n) + [pl.BlockSpec(memory_space=pltpu.VMEM)],
        out_shape=[sems, sems] + buffers + [jax.ShapeDtypeStruct((8, LANE), F32)],
        input_output_aliases={i: 2 + i for i in range(2 * n)},
        compiler_params=pltpu.CompilerParams(has_side_effects=pltpu.SideEffectType.DATAFLOW_SIDE_EFFECTING),
    )(*[pltpu.with_memory_space_constraint(a, pltpu.HBM) for a in list(arrays) + lands])
    return (res[0], res[1], res[2:2 + n], res[2 + n:2 + 2 * n]), res[-1]


def _exchange_wait(handle, after, name, gather):
    send_sems, recv_sems, srcs, lands = handle
    n = len(srcs)
    hbm = pl.BlockSpec(memory_space=pltpu.HBM)
    sem = pl.BlockSpec(memory_space=pltpu.SEMAPHORE)

    def body(*refs):
        for mine, arrival in _remote_copies(refs[:n], refs[n:2 * n], refs[2 * n], refs[2 * n + 1], gather):
            mine.wait_send()
            arrival.wait_recv()

    res = pl.pallas_call(
        body, name=name, in_specs=[hbm] * (2 * n) + [sem, sem, pl.BlockSpec(memory_space=pl.ANY)],
        out_specs=[hbm] * (2 * n), out_shape=[pltpu.HBM(a.shape, a.dtype) for a in list(srcs) + list(lands)],
        input_output_aliases={i: i for i in range(2 * n)},
        compiler_params=pltpu.CompilerParams(has_side_effects=pltpu.SideEffectType.DATAFLOW_SIDE_EFFECTING),
    )(*srcs, *lands, send_sems, recv_sems, after)
    return res[n:]


def _own_slot(land, mine):
    return lax.dynamic_update_slice(land, mine, (_my_index(),) + (0,) * (land.ndim - 1))


def _adamw(w, g, m, v):
    m = ADAM_B1 * m + (1.0 - ADAM_B1) * g
    v = ADAM_B2 * v + (1.0 - ADAM_B2) * (g * g)
    m_hat = m / (1.0 - ADAM_B1 ** ADAM_STEP)
    v_hat = v / (1.0 - ADAM_B2 ** ADAM_STEP)
    delta = -ADAM_LR * (m_hat / (jnp.sqrt(v_hat) + ADAM_EPS) + ADAM_WD * w)
    return delta, m, v


def _reduce_adamw(parts, w, m, v, name):
    nparts = len(parts)
    R, C = parts[0].shape[1:]
    tr = R
    while N_DEV * tr * C * parts[0].dtype.itemsize > REDUCE_BLOCK_BYTES and tr % 32 == 0:
        tr //= 2
    steps = R // tr

    def body(*refs):
        w_ref, m_ref, v_ref, g_ref, d_ref, nm_ref, nv_ref = refs[nparts:]
        for k, p_ref in enumerate(refs[:nparts]):
            @pl.when(pl.program_id(0) // steps == k)
            def _():
                g = p_ref[0].astype(F32)
                for s in range(1, N_DEV):
                    g = g + p_ref[s].astype(F32)
                g_ref[...] = g
                d_ref[...], nm_ref[...], nv_ref[...] = _adamw(w_ref[...], g, m_ref[...], v_ref[...])

    def part_spec(k):
        return pl.BlockSpec((N_DEV, tr, C), lambda i: (0, jnp.clip(i - k * steps, 0, steps - 1), 0))

    row = pl.BlockSpec((tr, C), lambda i: (i, 0))
    return pl.pallas_call(
        body, name=name, grid=(nparts * steps,),
        in_specs=[part_spec(k) for k in range(nparts)] + [row, row, row],
        out_specs=[row] * 4, out_shape=[jax.ShapeDtypeStruct((nparts * R, C), F32)] * 4,
        compiler_params=_cp(),
    )(*parts, w, m, v)


BIG = ("w_in", "w_uq", "w_ukv", "w_out_a", "w_out_b", "w_out_c", "w_o")
SMALL = ("norm_g", "b_gate", "conv_w", "conv_b", "q_a_norm_g", "kv_a_norm_g", "mla_q_norm_g", "mla_k_norm_g",
         "dil_q_norm_g", "dil_k_norm_g")
PACK_ROWS = 128
REDUCE_BLOCK_BYTES = 6 * 1024 * 1024


def _pack_local(tensors):
    flat = jnp.concatenate([t.reshape(-1) for t in tensors])
    pad = (-flat.shape[0]) % (PACK_ROWS * LANE)
    return jnp.concatenate([flat, jnp.zeros((pad,), flat.dtype)]).reshape(-1, LANE)


def _unpack_local(rows, like):
    flat = rows.reshape(-1)
    out, off = [], 0
    for t in like:
        out.append(flat[off:off + t.size].reshape(t.shape))
        off += t.size
    return out


def _cols_to_slots(a):
    k = a.shape[0]
    return a.reshape(k, N_DEV, -1).transpose(1, 0, 2)


def _slots_to_cols(s):
    return s.transpose(1, 0, 2).reshape(s.shape[1], -1)


def _rope_tables(S):
    inv = ROPE_THETA ** (-jnp.arange(0, MLA_ROPE, 2, dtype=F32) / MLA_ROPE)
    ang = jnp.arange(S, dtype=F32)[:, None] * inv[None, :]
    cos, sin = jnp.cos(ang), jnp.sin(ang)
    one = jnp.ones((S, MLA_NOPE), F32)
    z16, z32, z64 = (jnp.zeros((S, n), F32) for n in (16, 32, 64))
    cosp = jnp.concatenate([one, cos, cos, jnp.ones((S, 32), F32)], axis=1)
    sa = jnp.concatenate([z64, -sin, z16, z32], axis=1)
    sb = jnp.concatenate([z64, z16, sin, z32], axis=1)
    return cosp, sa, sb


def _alibi_slopes():
    n = DIL_GROUPS * DIL_HEADS
    m = 2.0 ** (-8.0 * jnp.arange(1, n + 1, dtype=F32) / n)
    return m.reshape(DIL_GROUPS, NPAIR, 2)


def _pad_slots(s):
    n, k, c = s.shape
    return _slots_to_cols(jnp.concatenate([s, jnp.zeros((n, k, LANE - c), s.dtype)], axis=2))


def _layer_params(gw, small, l):
    p = {}
    p["wp"] = _pad_columns(gw["w_in"])
    p["norm_g"] = small["norm_g"][l][None]
    p["b_gate"] = small["b_gate"][l][None]
    p["conv_w"] = gw["conv_w"].transpose(1, 0, 2).reshape(CONV_K, CONV_WIDTH)
    p["conv_b"] = small["conv_b"][l][None]
    p["gq"] = small["q_a_norm_g"][l][None]
    p["gkv"] = small["kv_a_norm_g"][l][None]
    p["wuqp"] = _pad_slots(gw["w_uq"])
    kv = gw["w_ukv"]
    p["wkp"] = _pad_slots(kv[:, :, :MLA_NOPE])
    p["wv"] = kv[:, :, MLA_NOPE:].transpose(1, 0, 2).reshape(MLA_KV_LORA, MLA_HEADS * MLA_V)
    zpad = jnp.zeros((1, LANE - MLA_QK), F32)
    p["gmq"] = jnp.concatenate([small["mla_q_norm_g"][l][None], zpad], axis=1)
    p["gmk"] = jnp.concatenate([small["mla_k_norm_g"][l][None], zpad], axis=1)
    tile = lambda g: jnp.broadcast_to(g[:, None, :], (DIL_GROUPS, DIL_HEADS, DIL_HEAD_DIM)).reshape(1, DIL_QK)
    p["gdq"] = tile(small["dil_q_norm_g"][l])
    p["gdk"] = tile(small["dil_k_norm_g"][l])
    p["woa"], p["wob"], p["woc"] = (_slots_to_cols(gw[n]) for n in ("w_out_a", "w_out_b", "w_out_c"))
    p["wo"] = gw["w_o"].reshape(D_MODEL, D_MODEL)
    return p


def _layer_fwd(x, p, tabs, slopes, B, S):
    proj, h = _inproj_fwd(x, p["norm_g"], p["wp"])
    ya = _mixa_fwd(proj, p["conv_w"], p["conv_b"], B, S)
    q, k, v = _mla_prep_fwd(proj, p["gq"], p["gkv"], p["wuqp"], p["wkp"], p["wv"], p["gmq"], p["gmk"], *tabs, S)
    ob, lse_b = _mla_attn_fwd(q, k, v, B, S)
    qn, kn = _dil_prep_fwd(proj, p["gdq"], p["gdk"])
    ogs, lses = [], []
    for gi in range(DIL_GROUPS):
        o, lse = _dil_attn_fwd(gi, slopes[gi], qn, kn, proj, B, S)
        ogs.append(o)
        lses.append(lse)
    out = _merge_fwd(x, proj, p["b_gate"], ya, ob, ogs, lses, p["woa"], p["wob"], p["woc"], p["wo"])
    saved = dict(x=x, proj=proj, h=h, ya=ya, q=q, k=k, v=v, ob=ob, lse_b=lse_b, qn=qn, kn=kn, ogs=ogs, lses=lses)
    return out, saved


def _layer_bwd(dout, sv, p, tabs, slopes, B, S):
    proj = sv["proj"]
    (dproj, dya, dob, dlb, dg0, dg1, dg2, dl0, dl1, dl2, merged, dpa, dpb, dpc, yb, yc, dbg) = _merge_bwd(
        dout, proj, p["b_gate"], sv["ya"], sv["ob"], sv["ogs"], sv["lses"], p["woa"], p["wob"], p["woc"], p["wo"])
    g = {}
    g["w_o"] = _matmul_tn(merged, dout, "dw_o").reshape(N_DEV, D_MODEL // N_DEV, D_MODEL)
    g["w_out_a"] = _cols_to_slots(_matmul_tn(sv["ya"], dpa, "dw_out_a"))
    g["w_out_b"] = _cols_to_slots(_matmul_tn(yb, dpb, "dw_out_b"))
    g["w_out_c"] = _cols_to_slots(_matmul_tn(yc, dpc, "dw_out_c"))
    g["b_gate"] = dbg[0]
    dproj, st = _mixa_bwd(dproj, dya, proj, p["conv_w"], p["conv_b"], B, S)
    g["conv_w"] = st[0:CONV_K]
    g["conv_b"] = st[CONV_K]
    dq, dk, dv = _mla_attn_bwd(sv["q"], sv["k"], sv["v"], dob, sv["lse_b"], dlb, B, S)
    dproj, dwuqp, dwkp, dwv, dgq, dgkv, dgmq, dgmk = _mla_prep_bwd(
        dproj, dq, dk, dv, proj, p["gq"], p["gkv"], p["wuqp"], p["wkp"], p["wv"], p["gmq"], p["gmk"], *tabs, S)
    g["w_uq"] = _cols_to_slots(dwuqp)[:, :, :MLA_QK]
    g["w_ukv"] = jnp.concatenate([_cols_to_slots(dwkp)[:, :, :MLA_NOPE], _cols_to_slots(dwv)], axis=2)
    g["q_a_norm_g"], g["kv_a_norm_g"] = dgq[0], dgkv[0]
    g["mla_q_norm_g"], g["mla_k_norm_g"] = dgmq[0, :MLA_QK], dgmk[0, :MLA_QK]
    dqkv = None
    for gi, (dog, dlg) in enumerate(((dg0, dl0), (dg1, dl1), (dg2, dl2))):
        dqkv = _dil_attn_bwd(gi, slopes[gi], sv["qn"], sv["kn"], proj, dog, sv["lses"][gi], dlg, dqkv, B, S)
    dproj, dgdq, dgdk = _dil_prep_bwd(dproj, *dqkv, proj, p["gdq"], p["gdk"])
    g["dil_q_norm_g"] = dgdq.reshape(DIL_GROUPS, DIL_HEADS, DIL_HEAD_DIM).sum(axis=1)
    g["dil_k_norm_g"] = dgdk.reshape(DIL_GROUPS, DIL_HEADS, DIL_HEAD_DIM).sum(axis=1)
    g["w_in"] = _unpad_columns(_matmul_tn(sv["h"], dproj, "dw_in"))
    dx, dng = _inproj_bwd_x(dproj, p["wp"], sv["x"], p["norm_g"], dout)
    g["norm_g"] = dng[0]
    return dx, g


def _after(token, a):
    return a if token is None else a + token[0:1, 0:1]


def _local_step(x, target, small, B, S, weights_of, grads_out):
    tabs = _rope_tables(S)
    sl = _alibi_slopes()
    slopes = [sl[gi] * float(DIL_PATTERNS[gi][1]) for gi in range(DIL_GROUPS)]
    params, saved = [], []
    for l in range(DEPTH):
        gw, token = weights_of(l, x)
        p = _layer_params(gw, small, l)
        p["norm_g"] = _after(token, p["norm_g"])
        x, sv = _layer_fwd(x, p, tabs, slopes, B, S)
        params.append(p)
        saved.append(sv)
    dout, lparts = _loss_head(x, target)
    sq = jnp.sum(lparts[:, 0, 0])
    token = None
    for l in reversed(range(DEPTH)):
        p = dict(params[l], b_gate=_after(token, params[l]["b_gate"]))
        dout, g = _layer_bwd(dout, saved[l], p, tabs, slopes, B, S)
        token = grads_out(l, g, dout)
    return sq, dout


def kernel(x, norm_g, w_in, b_gate, conv_w, conv_b, q_a_norm_g, w_uq, kv_a_norm_g, w_ukv, mla_q_norm_g, mla_k_norm_g, dil_q_norm_g, dil_k_norm_g, w_out_a, w_out_b, w_out_c, w_o, loss_target, m_norm_g, m_w_in, m_b_gate, m_conv_w, m_conv_b, m_q_a_norm_g, m_w_uq, m_kv_a_norm_g, m_w_ukv, m_mla_q_norm_g, m_mla_k_norm_g, m_dil_q_norm_g, m_dil_k_norm_g, m_w_out_a, m_w_out_b, m_w_out_c, m_w_o, v_norm_g, v_w_in, v_b_gate, v_conv_w, v_conv_b, v_q_a_norm_g, v_w_uq, v_kv_a_norm_g, v_w_ukv, v_mla_q_norm_g, v_mla_k_norm_g, v_dil_q_norm_g, v_dil_k_norm_g, v_w_out_a, v_w_out_b, v_w_out_c, v_w_o):
    names = ("norm_g", "w_in", "b_gate", "conv_w", "conv_b", "q_a_norm_g", "w_uq", "kv_a_norm_g", "w_ukv",
             "mla_q_norm_g", "mla_k_norm_g", "dil_q_norm_g", "dil_k_norm_g", "w_out_a", "w_out_b", "w_out_c", "w_o")
    w = dict(zip(names, (norm_g, w_in, b_gate, conv_w, conv_b, q_a_norm_g, w_uq, kv_a_norm_g, w_ukv, mla_q_norm_g,
                         mla_k_norm_g, dil_q_norm_g, dil_k_norm_g, w_out_a, w_out_b, w_out_c, w_o)))
    m = dict(zip(names, (m_norm_g, m_w_in, m_b_gate, m_conv_w, m_conv_b, m_q_a_norm_g, m_w_uq, m_kv_a_norm_g, m_w_ukv,
                         m_mla_q_norm_g, m_mla_k_norm_g, m_dil_q_norm_g, m_dil_k_norm_g, m_w_out_a, m_w_out_b,
                         m_w_out_c, m_w_o)))
    v = dict(zip(names, (v_norm_g, v_w_in, v_b_gate, v_conv_w, v_conv_b, v_q_a_norm_g, v_w_uq, v_kv_a_norm_g, v_w_ukv,
                         v_mla_q_norm_g, v_mla_k_norm_g, v_dil_q_norm_g, v_dil_k_norm_g, v_w_out_a, v_w_out_b,
                         v_w_out_c, v_w_o)))
    B, S, _ = x.shape
    me = _my_index()
    cshard = CONV_WIDTH // N_DEV

    shards = [[w[n][l].astype(BF16) for n in BIG] for l in range(DEPTH)]
    state = {}

    def weights_of(l, after):
        if l == 0:
            got = _exchange(shards[0] + [conv_w], "all_gather_weights_0", gather=True)
            state["gather"], token = _exchange_start(shards[1], "all_gather_weights_1_start", gather=True)
            state["conv_w"] = got[-1]
        else:
            landed = _exchange_wait(state["gather"], after, "all_gather_weights_1_wait", gather=True)
            got, token = [_own_slot(a, s[None]) for a, s in zip(landed, shards[1])], None
        gw = dict(zip(BIG, got))
        gw["conv_w"] = state["conv_w"][:, l]
        return gw, token

    recv, small_parts = {}, {}

    def grads_out(l, g, after):
        small_parts[l] = [g[n] for n in SMALL]
        send = [g[n].astype(BF16) for n in BIG]
        if l == DEPTH - 1:
            state["scatter"], token = _exchange_start(send, "exchange_weight_grads_1_start", gather=False)
            state["sent"] = send
            return token
        landed = _exchange_wait(state["scatter"], after, "exchange_weight_grads_1_wait", gather=False)
        mine = [lax.dynamic_slice_in_dim(s, me, 1, axis=0) for s in state["sent"]]
        recv[DEPTH - 1] = [_own_slot(a, s) for a, s in zip(landed, mine)]
        recv[l] = _exchange(send, "exchange_weight_grads_0", gather=False)
        return None

    sq, grad_x = _local_step(x.reshape(B * S, D_MODEL), loss_target.reshape(B * S, D_MODEL), w, B, S,
                             weights_of, grads_out)
    loss = lax.psum(sq * (0.5 / D_MODEL), AXES)

    res = {}
    for i, n in enumerate(BIG):
        rows = lambda a: a.reshape(-1, a.shape[-1])
        outs = _reduce_adamw([recv[l][i] for l in range(DEPTH)], rows(w[n]), rows(m[n]), rows(v[n]),
                             "reduce_adamw_" + n)
        res[n] = tuple(a.reshape(w[n].shape) for a in outs)
    part = {n: jnp.stack([small_parts[l][i] for l in range(DEPTH)]) for i, n in enumerate(SMALL)}

    def widen(t):
        return lax.dynamic_update_slice(jnp.zeros((DEPTH, CONV_K, CONV_WIDTH), F32), t, (0, 0, me * cshard))

    small_like = [part[n] for n in SMALL]
    pick = lambda d: [widen(d[n]) if n == "conv_w" else d[n] for n in SMALL]
    parts, = _exchange([_pack_local(small_like)], "all_gather_small_grads", gather=True)
    gs, ds, ms, vs = _reduce_adamw([parts], _pack_local(pick(w)), _pack_local(pick(m)), _pack_local(pick(v)),
                                   "reduce_adamw_small")
    for n, t in zip(SMALL, zip(*(_unpack_local(a, small_like) for a in (gs, ds, ms, vs)))):
        if n == "conv_w":
            t = tuple(lax.dynamic_slice(a, (0, 0, me * cshard), (DEPTH, CONV_K, cshard)) for a in t)
        res[n] = t

    out = [loss, grad_x.reshape(B, S, D_MODEL)]
    for i in range(4):
        out += [res[n][i] for n in names]
    return tuple(out)
```

```python
import jax
import jax.numpy as jnp
from jax import lax
from jax.experimental import pallas as pl
from jax.experimental.pallas import tpu as pltpu

F32 = jnp.float32
BF16 = jnp.bfloat16

D_MODEL = 1024
DEPTH = 2
CONV_WIDTH = 512
CONV_K = 3
MLA_HEADS = 8
MLA_Q_LORA = 256
MLA_KV_LORA = 128
MLA_NOPE = 64
MLA_ROPE = 32
MLA_V = 64
MLA_QK = MLA_NOPE + MLA_ROPE
ROPE_THETA = 10000.0
DIL_PATTERNS = ((128, 1), (512, 4), (2048, 16))
DIL_GROUPS = 3
DIL_HEADS = 8
DIL_HEAD_DIM = 64
DIL_WIDTH = DIL_HEADS * DIL_HEAD_DIM
DIL_QK = DIL_GROUPS * DIL_WIDTH
EPS = 1e-6
N_IN = 11168

ADAM_LR = 0.001
ADAM_B1 = 0.9
ADAM_B2 = 0.999
ADAM_EPS = 1e-08
ADAM_WD = 0.01
ADAM_STEP = 10

N_DEV = 8
AXES = ("x", "y", "c")
LANE = 128
HALF = 64
NPAIR = 4

CB_BZ, CB_CZ, CB_GATE = 0, 4, 8
CB_A = 32
CB_QKV = 48
CB_CQ, CB_CKV, CB_KPE = 84, 86, 87
NCB = 88
PP = NCB * LANE
SHARD_COLS = N_IN // N_DEV
NEG = -1e30
VMEM_LIMIT = 56 * 1024 * 1024


def _column_chunks():
    out = []
    col = 0

    def seg(nblocks, block_of):
        nonlocal col
        for i in range(nblocks):
            out.append((col, LANE, block_of(i)))
            col += LANE

    seg(4, lambda j: CB_A + 4 * j)
    seg(4, lambda j: CB_A + 4 * j + 1)
    seg(4, lambda j: CB_A + 4 * j + 2)
    seg(4, lambda j: CB_A + 4 * j + 3)
    seg(2, lambda i: CB_CQ + i)
    seg(1, lambda i: CB_CKV)
    out.append((col, MLA_ROPE, CB_KPE))
    col += MLA_ROPE
    seg(4, lambda j: CB_BZ + j)
    seg(12, lambda c: CB_QKV + 3 * c)
    seg(12, lambda c: CB_QKV + 3 * c + 1)
    seg(12, lambda c: CB_QKV + 3 * c + 2)
    seg(4, lambda j: CB_CZ + j)
    seg(24, lambda i: CB_GATE + i)
    assert col == N_IN and sorted(c[2] for c in out) == list(range(NCB))
    return out


COLUMN_CHUNKS = _column_chunks()


def _pad_columns(shards):
    parts = []
    for start, width, _ in sorted(COLUMN_CHUNKS, key=lambda c: c[2]):
        fill = LANE - width
        while width:
            p, off = divmod(start, SHARD_COLS)
            n = min(width, SHARD_COLS - off)
            parts.append(shards[p, :, off:off + n])
            start, width = start + n, width - n
        if fill:
            parts.append(jnp.zeros((shards.shape[1], fill), shards.dtype))
    return jnp.concatenate(parts, axis=1)


def _unpad_columns(wp):
    pieces = [[] for _ in range(N_DEV)]
    for start, width, b in COLUMN_CHUNKS:
        src = b * LANE
        while width:
            p, off = divmod(start, SHARD_COLS)
            n = min(width, SHARD_COLS - off)
            pieces[p].append(wp[:, src:src + n])
            start, width, src = start + n, width - n, src + n
    return jnp.stack([jnp.concatenate(ps, axis=1) for ps in pieces])


def _cp():
    return pltpu.CompilerParams(vmem_limit_bytes=VMEM_LIMIT)


def _rstd(x, n):
    return lax.rsqrt(jnp.sum(x * x, axis=-1, keepdims=True) * (1.0 / n) + EPS)


def _sigmoid(z):
    return 1.0 / (1.0 + jnp.exp(-z))


def _silu(z):
    return z * _sigmoid(z)


def _dsilu(z):
    s = _sigmoid(z)
    return s * (1.0 + z * (1.0 - s))


def _mm(a, b):
    return jnp.dot(a.astype(BF16), b.astype(BF16), preferred_element_type=F32)


def _mm_nt(a, b):
    return lax.dot_general(a.astype(BF16), b.astype(BF16), (((1,), (1,)), ((), ())), preferred_element_type=F32)


def _mm_tn(a, b):
    return lax.dot_general(a.astype(BF16), b.astype(BF16), (((0,), (0,)), ((), ())), preferred_element_type=F32)


def _lane_lo(shape):
    return lax.broadcasted_iota(jnp.int32, shape, len(shape) - 1) < HALF


def _head_bcast_sum(x):
    same = _lane_lo((LANE, LANE)) == (lax.broadcasted_iota(jnp.int32, (LANE, LANE), 0) < HALF)
    ones = jnp.where(same, 1.0, 0.0).astype(jnp.bfloat16)
    total = None
    for _ in range(3):
        term = x.astype(jnp.bfloat16)
        x = x - term.astype(F32)
        part = jnp.dot(term, ones, preferred_element_type=F32)
        total = part if total is None else total + part
    return total


def _rope(t, cos, sa, sb):
    return t * cos + pltpu.roll(t, LANE - 16, axis=1) * sa + pltpu.roll(t, 16, axis=1) * sb


def _rope_t(d, cos, sa, sb):
    return d * cos + pltpu.roll(d * sa, 16, axis=1) + pltpu.roll(d * sb, LANE - 16, axis=1)


def _shift_down(u, k):
    rows = lax.broadcasted_iota(jnp.int32, u.shape, 0)
    return jnp.where(rows >= k, pltpu.roll(u, k, axis=0), 0.0)


def _shift_up(u, k):
    n = u.shape[0]
    rows = lax.broadcasted_iota(jnp.int32, u.shape, 0)
    return jnp.where(rows < n - k, pltpu.roll(u, n - k, axis=0), 0.0)


def _tile(n, want):
    t = min(n, want)
    assert n % t == 0, (n, want)
    return t


def _inproj_fwd(x, g, wp):
    T = x.shape[0]
    tm, tn = _tile(T, 1024), 512

    def body(x_ref, g_ref, w_ref, proj_ref, h_ref):
        @pl.when(pl.program_id(1) == 0)
        def _():
            xv = x_ref[...]
            h_ref[...] = (xv * _rstd(xv, D_MODEL) * g_ref[...]).astype(BF16)

        proj_ref[...] = jnp.dot(h_ref[...], w_ref[...], preferred_element_type=F32)

    return pl.pallas_call(
        body, name="inproj_fwd", grid=(T // tm, PP // tn),
        in_specs=[pl.BlockSpec((tm, D_MODEL), lambda i, j: (i, 0)),
                  pl.BlockSpec((1, D_MODEL), lambda i, j: (0, 0)),
                  pl.BlockSpec((D_MODEL, tn), lambda i, j: (0, j))],
        out_specs=[pl.BlockSpec((tm, tn), lambda i, j: (i, j)),
                   pl.BlockSpec((tm, D_MODEL), lambda i, j: (i, 0))],
        out_shape=[jax.ShapeDtypeStruct((T, PP), F32), jax.ShapeDtypeStruct((T, D_MODEL), BF16)],
        compiler_params=_cp(),
    )(x, g, wp)


def _matmul_tn(a, b, name):
    T, K = a.shape
    N = b.shape[1]
    tt, tn = _tile(T, 512), _tile(N, 1024)

    def body(a_ref, b_ref, o_ref):
        @pl.when(pl.program_id(1) == 0)
        def _():
            o_ref[...] = jnp.zeros_like(o_ref)

        o_ref[...] += _mm_tn(a_ref[...], b_ref[...])

    return pl.pallas_call(
        body, name=name, grid=(N // tn, T // tt),
        in_specs=[pl.BlockSpec((tt, K), lambda j, k: (k, 0)),
                  pl.BlockSpec((tt, tn), lambda j, k: (k, j))],
        out_specs=pl.BlockSpec((K, tn), lambda j, k: (0, j)),
        out_shape=jax.ShapeDtypeStruct((K, N), F32),
        compiler_params=_cp(),
    )(a, b)


def _inproj_bwd_x(dproj, wp, x, g, dout):
    T = x.shape[0]
    tm, tk = _tile(T, 1024), 512
    nk = PP // tk

    def body(dp_ref, w_ref, x_ref, g_ref, do_ref, dx_ref, dg_ref, acc_ref):
        i, k = pl.program_id(0), pl.program_id(1)

        @pl.when(k == 0)
        def _():
            acc_ref[...] = jnp.zeros_like(acc_ref)

        @pl.when((k == 0) & (i == 0))
        def _():
            dg_ref[...] = jnp.zeros_like(dg_ref)

        acc_ref[...] += _mm_nt(dp_ref[...], w_ref[...])

        @pl.when(k == nk - 1)
        def _():
            dh = acc_ref[...]
            xv = x_ref[...]
            r = _rstd(xv, D_MODEL)
            gy = dh * g_ref[...]
            dot = jnp.sum(xv * gy, axis=-1, keepdims=True) * (1.0 / D_MODEL)
            dx_ref[...] = do_ref[...] + r * gy - xv * (r * r * r) * dot
            dg_ref[...] += jnp.sum(dh * xv * r, axis=0, keepdims=True)

    return pl.pallas_call(
        body, name="inproj_bwd_x", grid=(T // tm, nk),
        in_specs=[pl.BlockSpec((tm, tk), lambda i, k: (i, k)),
                  pl.BlockSpec((D_MODEL, tk), lambda i, k: (0, k)),
                  pl.BlockSpec((tm, D_MODEL), lambda i, k: (i, 0)),
                  pl.BlockSpec((1, D_MODEL), lambda i, k: (0, 0)),
                  pl.BlockSpec((tm, D_MODEL), lambda i, k: (i, 0))],
        out_specs=[pl.BlockSpec((tm, D_MODEL), lambda i, k: (i, 0)),
                   pl.BlockSpec((1, D_MODEL), lambda i, k: (0, 0))],
        out_shape=[jax.ShapeDtypeStruct((T, D_MODEL), F32), jax.ShapeDtypeStruct((1, D_MODEL), F32)],
        scratch_shapes=[pltpu.VMEM((tm, D_MODEL), F32)],
        compiler_params=_cp(),
    )(dproj, wp, x, g, dout)


def _mixa_fwd(proj, cw, cb, B, S):
    nc = CONV_WIDTH // LANE
    ca = CB_A // 4

    def body(p_ref, cw_ref, cb_ref, y_ref):
        ab, ac, ax, az = (p_ref[:, i * LANE:(i + 1) * LANE] for i in range(4))
        u = ac * ax
        conv = cb_ref[...] + cw_ref[0:1, :] * _shift_down(u, 2) + cw_ref[1:2, :] * _shift_down(u, 1) + cw_ref[2:3, :] * u
        y_ref[...] = (ab * conv * _silu(az)).astype(BF16)

    return pl.pallas_call(
        body, name="mixa_fwd", grid=(B, nc),
        in_specs=[pl.BlockSpec((S, 4 * LANE), lambda b, j: (b, ca + j)),
                  pl.BlockSpec((CONV_K, LANE), lambda b, j: (0, j)),
                  pl.BlockSpec((1, LANE), lambda b, j: (0, j))],
        out_specs=pl.BlockSpec((S, LANE), lambda b, j: (b, j)),
        out_shape=jax.ShapeDtypeStruct((B * S, CONV_WIDTH), BF16),
        compiler_params=_cp(),
    )(proj, cw, cb)


def _mixa_bwd(dproj, dy, proj, cw, cb, B, S):
    nc = CONV_WIDTH // LANE
    ca = CB_A // 4

    def body(dpin_ref, dy_ref, p_ref, cw_ref, cb_ref, dp_ref, st_ref):
        del dpin_ref
        ab, ac, ax, az = (p_ref[:, i * LANE:(i + 1) * LANE] for i in range(4))
        u = ac * ax
        u1, u2 = _shift_down(u, 1), _shift_down(u, 2)
        w0, w1, w2 = cw_ref[0:1, :], cw_ref[1:2, :], cw_ref[2:3, :]
        conv = cb_ref[...] + w0 * u2 + w1 * u1 + w2 * u
        s = _silu(az)
        d = dy_ref[...]
        dconv = d * ab * s
        du = w2 * dconv + w1 * _shift_up(dconv, 1) + w0 * _shift_up(dconv, 2)
        dp_ref[:, 0:LANE] = (d * conv * s).astype(BF16)
        dp_ref[:, LANE:2 * LANE] = (du * ax).astype(BF16)
        dp_ref[:, 2 * LANE:3 * LANE] = (du * ac).astype(BF16)
        dp_ref[:, 3 * LANE:4 * LANE] = (d * ab * conv * _dsilu(az)).astype(BF16)
        row = lax.broadcasted_iota(jnp.int32, (8, LANE), 0)
        st = jnp.zeros((8, LANE), F32)
        for r, v in enumerate((dconv * u2, dconv * u1, dconv * u, dconv)):
            st = st + jnp.where(row == r, jnp.sum(v, axis=0, keepdims=True), 0.0)

        @pl.when(pl.program_id(1) == 0)
        def _():
            st_ref[...] = st

        @pl.when(pl.program_id(1) != 0)
        def _():
            st_ref[...] += st

    return pl.pallas_call(
        body, name="mixa_bwd", grid=(nc, B),
        in_specs=[pl.BlockSpec(memory_space=pl.ANY),
                  pl.BlockSpec((S, LANE), lambda j, b: (b, j)),
                  pl.BlockSpec((S, 4 * LANE), lambda j, b: (b, ca + j)),
                  pl.BlockSpec((CONV_K, LANE), lambda j, b: (0, j)),
                  pl.BlockSpec((1, LANE), lambda j, b: (0, j))],
        out_specs=[pl.BlockSpec((S, 4 * LANE), lambda j, b: (b, ca + j)),
                   pl.BlockSpec((8, LANE), lambda j, b: (0, j))],
        out_shape=[jax.ShapeDtypeStruct(dproj.shape, BF16), jax.ShapeDtypeStruct((8, CONV_WIDTH), F32)],
        input_output_aliases={0: 0},
        compiler_params=_cp(),
    )(dproj, dy, proj, cw, cb)


def _mla_prep_fwd(proj, gq, gkv, wuqp, wkp, wv, gmq, gmk, cos, sa, sb, S):
    T = proj.shape[0]
    ts = _tile(S, 512)
    ns = S // ts
    W = MLA_HEADS * LANE

    def body(p_ref, gq_ref, gkv_ref, wuq_ref, wk_ref, wv_ref, gmq_ref, gmk_ref, cos_ref, sa_ref, sb_ref,
             q_ref, k_ref, v_ref):
        cq = p_ref[:, 0:2 * LANE]
        ckv = p_ref[:, 2 * LANE:3 * LANE]
        kpe = pltpu.roll(p_ref[:, 3 * LANE:4 * LANE], HALF, axis=1)
        cqn = cq * _rstd(cq, MLA_Q_LORA) * gq_ref[...]
        ckn = (ckv * _rstd(ckv, MLA_KV_LORA) * gkv_ref[...]).astype(BF16)
        q0 = _mm(cqn, wuq_ref[...])
        kn = _mm(ckn, wk_ref[...])
        v_ref[...] = _mm(ckn, wv_ref[...]).astype(BF16)
        c, a, b = cos_ref[...], sa_ref[...], sb_ref[...]
        for h in range(MLA_HEADS):
            q0h = q0[:, h * LANE:(h + 1) * LANE]
            q_ref[h] = _rope(q0h * _rstd(q0h, MLA_QK) * gmq_ref[...], c, a, b).astype(BF16)
            k0h = kn[:, h * LANE:(h + 1) * LANE] + kpe
            k_ref[h] = _rope(k0h * _rstd(k0h, MLA_QK) * gmk_ref[...], c, a, b).astype(BF16)

    def whole(r, c):
        return pl.BlockSpec((r, c), lambda i: (0, 0))

    tab = pl.BlockSpec((ts, LANE), lambda i: (i % ns, 0))
    return pl.pallas_call(
        body, name="mla_prep_fwd", grid=(T // ts,),
        in_specs=[pl.BlockSpec((ts, 4 * LANE), lambda i: (i, CB_CQ // 4)),
                  whole(1, MLA_Q_LORA), whole(1, MLA_KV_LORA), whole(MLA_Q_LORA, W), whole(MLA_KV_LORA, W),
                  whole(MLA_KV_LORA, MLA_HEADS * MLA_V), whole(1, LANE), whole(1, LANE), tab, tab, tab],
        out_specs=[pl.BlockSpec((MLA_HEADS, ts, LANE), lambda i: (0, i, 0)),
                   pl.BlockSpec((MLA_HEADS, ts, LANE), lambda i: (0, i, 0)),
                   pl.BlockSpec((ts, MLA_HEADS * MLA_V), lambda i: (i, 0))],
        out_shape=[jax.ShapeDtypeStruct((MLA_HEADS, T, LANE), BF16), jax.ShapeDtypeStruct((MLA_HEADS, T, LANE), BF16),
                   jax.ShapeDtypeStruct((T, MLA_HEADS * MLA_V), BF16)],
        compiler_params=_cp(),
    )(proj, gq, gkv, wuqp, wkp, wv, gmq, gmk, cos, sa, sb)


def _mla_prep_bwd(dproj, dq, dk, dv, proj, gq, gkv, wuqp, wkp, wv, gmq, gmk, cos, sa, sb, S):
    T = proj.shape[0]
    ts = _tile(S, 256)
    ns = S // ts
    W = MLA_HEADS * LANE

    def body(dpin_ref, dq_ref, dk_ref, dv_ref, p_ref, gq_ref, gkv_ref, wuq_ref, wk_ref, wv_ref, gmq_ref, gmk_ref,
             cos_ref, sa_ref, sb_ref,
             dp_ref, dwuq_ref, dwk_ref, dwv_ref, dgq_ref, dgkv_ref, dgmq_ref, dgmk_ref, dq0_ref, dkn_ref):
        del dpin_ref

        @pl.when(pl.program_id(0) == 0)
        def _():
            for r in (dwuq_ref, dwk_ref, dwv_ref, dgq_ref, dgkv_ref, dgmq_ref, dgmk_ref):
                r[...] = jnp.zeros_like(r)

        cq = p_ref[:, 0:2 * LANE]
        ckv = p_ref[:, 2 * LANE:3 * LANE]
        kpe = pltpu.roll(p_ref[:, 3 * LANE:4 * LANE], HALF, axis=1)
        rq = _rstd(cq, MLA_Q_LORA)
        rkv = _rstd(ckv, MLA_KV_LORA)
        gq, gkv, gmq, gmk = gq_ref[...], gkv_ref[...], gmq_ref[...], gmk_ref[...]
        cqn = (cq * rq * gq).astype(BF16)
        ckn = (ckv * rkv * gkv).astype(BF16)
        q0 = _mm(cqn, wuq_ref[...])
        kn = _mm(ckn, wk_ref[...])
        c, a, b = cos_ref[...], sa_ref[...], sb_ref[...]
        lane = lax.broadcasted_iota(jnp.int32, (ts, LANE), 1)
        dgmq = jnp.zeros((1, LANE), F32)
        dgmk = jnp.zeros((1, LANE), F32)
        dkpe = jnp.zeros((ts, LANE), F32)
        for h in range(MLA_HEADS):
            q0h = q0[:, h * LANE:(h + 1) * LANE]
            r = _rstd(q0h, MLA_QK)
            d1 = _rope_t(dq_ref[h], c, a, b)
            gy = d1 * gmq
            dq0_ref[:, h * LANE:(h + 1) * LANE] = (
                r * gy - q0h * (r * r * r) * (jnp.sum(q0h * gy, axis=-1, keepdims=True) * (1.0 / MLA_QK))).astype(BF16)
            dgmq = dgmq + jnp.sum(d1 * q0h * r, axis=0, keepdims=True)
            k0h = kn[:, h * LANE:(h + 1) * LANE] + kpe
            r = _rstd(k0h, MLA_QK)
            d1 = _rope_t(dk_ref[h], c, a, b)
            gy = d1 * gmk
            dk0 = r * gy - k0h * (r * r * r) * (jnp.sum(k0h * gy, axis=-1, keepdims=True) * (1.0 / MLA_QK))
            dgmk = dgmk + jnp.sum(d1 * k0h * r, axis=0, keepdims=True)
            dkn_ref[:, h * LANE:(h + 1) * LANE] = jnp.where(lane < MLA_NOPE, dk0, 0.0).astype(BF16)
            dkpe = dkpe + jnp.where((lane >= MLA_NOPE) & (lane < MLA_QK), dk0, 0.0)
        dq0 = dq0_ref[...]
        dkn = dkn_ref[...]
        dvv = dv_ref[...]
        dwuq_ref[...] += _mm_tn(cqn, dq0)
        dwk_ref[...] += _mm_tn(ckn, dkn)
        dwv_ref[...] += _mm_tn(ckn, dvv)
        dgmq_ref[...] += dgmq
        dgmk_ref[...] += dgmk
        dcqn = _mm_nt(dq0, wuq_ref[...])
        gy = dcqn * gq
        dp_ref[:, 0:2 * LANE] = (
            rq * gy - cq * (rq * rq * rq) * (jnp.sum(cq * gy, axis=-1, keepdims=True) * (1.0 / MLA_Q_LORA))).astype(BF16)
        dgq_ref[...] += jnp.sum(dcqn * cq * rq, axis=0, keepdims=True)
        dckn = _mm_nt(dkn, wk_ref[...]) + _mm_nt(dvv, wv_ref[...])
        gy = dckn * gkv
        dp_ref[:, 2 * LANE:3 * LANE] = (
            rkv * gy - ckv * (rkv * rkv * rkv) * (jnp.sum(ckv * gy, axis=-1, keepdims=True) * (1.0 / MLA_KV_LORA))).astype(BF16)
        dgkv_ref[...] += jnp.sum(dckn * ckv * rkv, axis=0, keepdims=True)
        dp_ref[:, 3 * LANE:4 * LANE] = pltpu.roll(dkpe, HALF, axis=1).astype(BF16)

    def whole(r, c):
        return pl.BlockSpec((r, c), lambda i: (0, 0))

    tab = pl.BlockSpec((ts, LANE), lambda i: (i % ns, 0))
    heads = pl.BlockSpec((MLA_HEADS, ts, LANE), lambda i: (0, i, 0))
    return pl.pallas_call(
        body, name="mla_prep_bwd", grid=(T // ts,),
        in_specs=[pl.BlockSpec(memory_space=pl.ANY), heads, heads,
                  pl.BlockSpec((ts, MLA_HEADS * MLA_V), lambda i: (i, 0)),
                  pl.BlockSpec((ts, 4 * LANE), lambda i: (i, CB_CQ // 4)),
                  whole(1, MLA_Q_LORA), whole(1, MLA_KV_LORA), whole(MLA_Q_LORA, W), whole(MLA_KV_LORA, W),
                  whole(MLA_KV_LORA, MLA_HEADS * MLA_V), whole(1, LANE), whole(1, LANE), tab, tab, tab],
        out_specs=[pl.BlockSpec((ts, 4 * LANE), lambda i: (i, CB_CQ // 4)),
                   whole(MLA_Q_LORA, W), whole(MLA_KV_LORA, W), whole(MLA_KV_LORA, MLA_HEADS * MLA_V),
                   whole(1, MLA_Q_LORA), whole(1, MLA_KV_LORA), whole(1, LANE), whole(1, LANE)],
        out_shape=[jax.ShapeDtypeStruct(dproj.shape, BF16),
                   jax.ShapeDtypeStruct((MLA_Q_LORA, W), F32), jax.ShapeDtypeStruct((MLA_KV_LORA, W), F32),
                   jax.ShapeDtypeStruct((MLA_KV_LORA, MLA_HEADS * MLA_V), F32),
                   jax.ShapeDtypeStruct((1, MLA_Q_LORA), F32), jax.ShapeDtypeStruct((1, MLA_KV_LORA), F32),
                   jax.ShapeDtypeStruct((1, LANE), F32), jax.ShapeDtypeStruct((1, LANE), F32)],
        scratch_shapes=[pltpu.VMEM((ts, W), BF16), pltpu.VMEM((ts, W), BF16)],
        input_output_aliases={0: 0},
        compiler_params=_cp(),
    )(dproj, dq, dk, dv, proj, gq, gkv, wuqp, wkp, wv, gmq, gmk, cos, sa, sb)


def _dil_prep_fwd(proj, gq, gk):
    T = proj.shape[0]
    ts = _tile(T, 512)
    nc = DIL_QK // LANE

    def body(p_ref, gq_ref, gk_ref, q_ref, k_ref):
        for i, (g_ref, o_ref) in enumerate(((gq_ref, q_ref), (gk_ref, k_ref))):
            t = p_ref[:, i * LANE:(i + 1) * LANE]
            r = lax.rsqrt(_head_bcast_sum(t * t) * (1.0 / DIL_HEAD_DIM) + EPS)
            o_ref[...] = t * r * g_ref[...]

    col = pl.BlockSpec((1, LANE), lambda i, c: (0, c))
    out = pl.BlockSpec((ts, LANE), lambda i, c: (i, c))
    return pl.pallas_call(
        body, name="dil_prep_fwd", grid=(T // ts, nc),
        in_specs=[pl.BlockSpec((ts, 3 * LANE), lambda i, c: (i, CB_QKV // 3 + c)), col, col],
        out_specs=[out, out],
        out_shape=[jax.ShapeDtypeStruct((T, DIL_QK), F32)] * 2,
        compiler_params=_cp(),
    )(proj, gq, gk)


def _dil_prep_bwd(dproj, ddq, ddk, ddv, proj, gq, gk):
    T = proj.shape[0]
    ts = _tile(T, 512)
    nc = DIL_QK // LANE

    def body(dpin_ref, ddq_ref, ddk_ref, ddv_ref, p_ref, gq_ref, gk_ref, dp_ref, dgq_ref, dgk_ref):
        del dpin_ref
        first = pl.program_id(1) == 0
        dp_ref[:, 2 * LANE:3 * LANE] = ddv_ref[...].astype(BF16)
        for i, (d_ref, g_ref, dg_ref) in enumerate(((ddq_ref, gq_ref, dgq_ref), (ddk_ref, gk_ref, dgk_ref))):
            t = p_ref[:, i * LANE:(i + 1) * LANE]
            r = lax.rsqrt(_head_bcast_sum(t * t) * (1.0 / DIL_HEAD_DIM) + EPS)
            d = d_ref[...]
            gy = d * g_ref[...]
            dp_ref[:, i * LANE:(i + 1) * LANE] = (
                r * gy - t * (r * r * r) * (_head_bcast_sum(t * gy) * (1.0 / DIL_HEAD_DIM))).astype(BF16)
            part = jnp.sum(d * t * r, axis=0, keepdims=True)

            @pl.when(first)
            def _():
                dg_ref[...] = part

            @pl.when(jnp.logical_not(first))
            def _():
                dg_ref[...] += part

    col = pl.BlockSpec((1, LANE), lambda c, i: (0, c))
    tok = pl.BlockSpec((ts, LANE), lambda c, i: (i, c))
    return pl.pallas_call(
        body, name="dil_prep_bwd", grid=(nc, T // ts),
        in_specs=[pl.BlockSpec(memory_space=pl.ANY), tok, tok, tok,
                  pl.BlockSpec((ts, 3 * LANE), lambda c, i: (i, CB_QKV // 3 + c)), col, col],
        out_specs=[pl.BlockSpec((ts, 3 * LANE), lambda c, i: (i, CB_QKV // 3 + c)), col, col],
        out_shape=[jax.ShapeDtypeStruct(dproj.shape, BF16), jax.ShapeDtypeStruct((1, DIL_QK), F32),
                   jax.ShapeDtypeStruct((1, DIL_QK), F32)],
        input_output_aliases={0: 0},
        compiler_params=_cp(),
    )(dproj, ddq, ddk, ddv, proj, gq, gk)


COPY_ROWS = 256


def _to_classes(src_ref, dst_ref, d, L, scale=None):
    n = min(L, COPY_ROWS)
    for r in range(d):
        for c0 in range(0, L, n):
            rows = pl.ds(r + c0 * d, n, stride=d) if d > 1 else pl.ds(c0, n)
            val = src_ref[rows, :]
            if scale is not None:
                val = val * scale
            dst_ref[r * L + c0:r * L + c0 + n, :] = val.astype(dst_ref.dtype)


def _from_classes(src_ref, dst_ref, d, L):
    n = min(L, COPY_ROWS)
    for r in range(d):
        for c0 in range(0, L, n):
            rows = pl.ds(r + c0 * d, n, stride=d) if d > 1 else pl.ds(c0, n)
            dst_ref[rows, :] = src_ref[r * L + c0:r * L + c0 + n, :].astype(dst_ref.dtype)


MLA_TQ, MLA_TK = 512, 512


def _causal_bias(tq, tk, shift):
    row = lax.broadcasted_iota(jnp.int32, (tq, tk), 0)
    col = lax.broadcasted_iota(jnp.int32, (tq, tk), 1)
    return jnp.where(row >= col + shift, 0.0, NEG)


def _mla_specs(S):
    heads = pl.BlockSpec((2, S, LANE), lambda b, j: (j, b, 0))
    pair = pl.BlockSpec((S, LANE), lambda b, j: (b, j))
    return heads, pair


def _mla_attn_fwd(q, k, v, B, S):
    tq = _tile(S, MLA_TQ)
    tk = _tile(tq, MLA_TK)
    nd = tq // tk
    scale = MLA_QK ** -0.5
    heads, pair = _mla_specs(S)

    def body(q_ref, k_ref, v_ref, o_ref, lse_ref):
        lo, lok = _lane_lo((tq, LANE)), _lane_lo((tk, LANE))
        diag = [_causal_bias(tq, tk, i * tk) for i in range(nd)]

        def block(g, _):
            row0 = pl.multiple_of(g * tq, tq)
            rows = pl.ds(row0, tq)
            qs = [q_ref[hh, rows, :] for hh in range(2)]

            one = jnp.ones((), BF16)

            def step(off, carries, bias):
                off = pl.multiple_of(off, tk)
                vt = v_ref[pl.ds(off, tk), :]
                vh = (jnp.where(lok, vt, one), jnp.where(lok, one, vt))
                out = []
                for hh, (m, acc) in enumerate(carries):
                    s = _mm_nt(qs[hh], k_ref[hh, pl.ds(off, tk), :]) * scale
                    if bias is not None:
                        s = s + bias
                    m_new = jnp.maximum(m, jnp.max(s, axis=-1, keepdims=True))
                    p = jnp.exp(s - m_new)
                    out.append((m_new, jnp.exp(m - m_new) * acc + _mm(p, vh[hh])))
                return tuple(out)

            init = (jnp.full((tq, 1), NEG, F32), jnp.zeros((tq, LANE), F32))
            carries = lax.fori_loop(0, g * nd, lambda i, c: step(i * tk, c, None), (init, init))
            for i in range(nd):
                carries = step(row0 + i * tk, carries, diag[i])
            (ma, acca), (mb, accb) = carries
            la, lb = pltpu.roll(acca, HALF, axis=1), pltpu.roll(accb, HALF, axis=1)
            o_ref[rows, :] = jnp.where(lo, acca / la, accb / lb)
            lse_ref[rows, :] = jnp.where(lo, ma + jnp.log(la), mb + jnp.log(lb))
            return 0

        lax.fori_loop(0, S // tq, block, 0)

    return pl.pallas_call(
        body, name="mla_attn_fwd", grid=(B, NPAIR), in_specs=[heads, heads, pair], out_specs=[pair, pair],
        out_shape=[jax.ShapeDtypeStruct((B * S, MLA_HEADS * MLA_V), F32)] * 2,
        compiler_params=_cp(),
    )(q, k, v)


DIL_UNROLL = 8


def _dil_geometry(gi, S):
    span, d = DIL_PATTERNS[gi]
    L = S // d
    t = _tile(L, 128)
    window = span // d
    back = min(-(-window // t) * t, L - t)
    return d, L, t, window, back


def _dil_specs(gi, S):
    qk = pl.BlockSpec((S, LANE), lambda b, j: (b, NPAIR * gi + j))
    v = pl.BlockSpec((S, LANE), lambda b, j: (b, CB_QKV + 3 * (NPAIR * gi + j) + 2))
    pair = pl.BlockSpec((S, LANE), lambda b, j: (b, j))
    return qk, v, pair


def _dil_bias(bias_ref, sl_ref, j, t, kw, back, window):
    row = lax.broadcasted_iota(jnp.int32, (2 * t, kw), 0)
    col = lax.broadcasted_iota(jnp.int32, (2 * t, kw), 1)
    second = row >= t
    slope = jnp.where(second, sl_ref[j, 1], sl_ref[j, 0])
    for n in range(bias_ref.shape[0]):
        dist = jnp.where(second, row - t, row) + n * back - col
        bias_ref[n] = jnp.where((dist >= 0) & (dist <= window), -slope * dist.astype(F32), NEG)


def _stack_heads(x, lo):
    zero = jnp.zeros((), x.dtype)
    return jnp.concatenate([jnp.where(lo, x, zero), jnp.where(lo, zero, x)], axis=0)


def _dil_attn_fwd(gi, slopes, qn, kn, proj, B, S):
    d, L, t, window, back = _dil_geometry(gi, S)
    kw, nq = back + t, L // t
    nbias = 2 if back else 1
    qk, vspec, pair = _dil_specs(gi, S)

    def body(sl_ref, q_ref, k_ref, v_ref, o_ref, lse_ref, qs, ks, vs, os_, ls, bias_ref):
        _to_classes(q_ref, qs, d, L, DIL_HEAD_DIM ** -0.5)
        _to_classes(k_ref, ks, d, L)
        _to_classes(v_ref, vs, d, L)
        _dil_bias(bias_ref, sl_ref, pl.program_id(1), t, kw, back, window)
        lo = _lane_lo((t, LANE))

        def block(g, _):
            qb = g % nq if d > 1 else g
            row0 = pl.multiple_of(g * t, t)
            rows = pl.ds(row0, t)
            early = qb * t < back
            keys = pl.ds(pl.multiple_of(jnp.where(early, row0 - qb * t, row0 - back), t), kw)
            s = _mm_nt(_stack_heads(qs[rows, :], lo), ks[keys, :]) + bias_ref[jnp.where(early, 0, nbias - 1)]
            m = jnp.max(s, axis=-1, keepdims=True)
            p = jnp.exp(s - m)
            l = jnp.sum(p, axis=-1, keepdims=True)
            o2 = _mm(p, vs[keys, :]) / l
            lse2 = m + jnp.log(l)
            os_[rows, :] = jnp.where(lo, o2[:t], o2[t:])
            ls[rows, :] = jnp.where(lo, lse2[:t], lse2[t:])
            return 0

        lax.fori_loop(0, d * nq, block, 0, unroll=DIL_UNROLL if d * nq % DIL_UNROLL == 0 else 1)
        _from_classes(os_, o_ref, d, L)
        _from_classes(ls, lse_ref, d, L)

    return pl.pallas_call(
        body, name=f"dil_attn_fwd_{gi}", grid=(B, NPAIR),
        in_specs=[pl.BlockSpec(memory_space=pltpu.SMEM), qk, qk, vspec], out_specs=[pair, pair],
        out_shape=[jax.ShapeDtypeStruct((B * S, DIL_WIDTH), F32)] * 2,
        scratch_shapes=[pltpu.VMEM((S, LANE), BF16)] * 3 + [pltpu.VMEM((S, LANE), F32)] * 2
                       + [pltpu.VMEM((nbias, 2 * t, kw), F32)],
        compiler_params=_cp(),
    )(slopes, qn, kn, proj)


def _mla_attn_bwd(q, k, v, do, lse, delta, B, S):
    T = B * S
    tq = _tile(S, MLA_TQ)
    tk = _tile(tq, MLA_TK)
    nd = tq // tk
    scale = MLA_QK ** -0.5
    heads, pair = _mla_specs(S)

    def body(q_ref, k_ref, v_ref, do_ref, lse_ref, dl_ref, dq_ref, dk_ref, dv_ref):
        dk_ref[...] = jnp.zeros_like(dk_ref)
        dv_ref[...] = jnp.zeros_like(dv_ref)
        lo = _lane_lo((tq, LANE))
        diag = [_causal_bias(tq, tk, i * tk) for i in range(nd)]

        def block(g, _):
            row0 = pl.multiple_of(g * tq, tq)
            rows = pl.ds(row0, tq)
            for hh in range(2):
                sel = lo if hh == 0 else jnp.logical_not(lo)
                qh = q_ref[hh, rows, :]
                doh = jnp.where(sel, do_ref[rows, :], jnp.zeros((), BF16))
                lse_h = jnp.max(jnp.where(sel, lse_ref[rows, :], NEG), axis=-1, keepdims=True)
                dl_h = jnp.max(jnp.where(sel, dl_ref[rows, :], NEG), axis=-1, keepdims=True)

                def step(off, dq_acc, bias, hh=hh, qh=qh, doh=doh, lse_h=lse_h, dl_h=dl_h):
                    cols = pl.ds(pl.multiple_of(off, tk), tk)
                    kh = k_ref[hh, cols, :]
                    s = _mm_nt(qh, kh) * scale
                    if bias is not None:
                        s = s + bias
                    p = jnp.exp(s - lse_h)
                    dp = _mm_nt(doh, v_ref[cols, :])
                    ds = (p * (dp - dl_h)).astype(BF16)
                    dk_ref[hh, cols, :] += _mm_tn(ds, qh) * scale
                    dv_ref[cols, :] += _mm_tn(p, doh)
                    return dq_acc + _mm(ds, kh)

                dq_acc = lax.fori_loop(0, g * nd, lambda i, a: step(i * tk, a, None), jnp.zeros((tq, LANE), F32))
                for i in range(nd):
                    dq_acc = step(row0 + i * tk, dq_acc, diag[i])
                dq_ref[hh, rows, :] = dq_acc * scale
            return 0

        lax.fori_loop(0, S // tq, block, 0)

    return pl.pallas_call(
        body, name="mla_attn_bwd", grid=(B, NPAIR), in_specs=[heads, heads, pair, pair, pair, pair],
        out_specs=[heads, heads, pair],
        out_shape=[jax.ShapeDtypeStruct((MLA_HEADS, T, LANE), F32), jax.ShapeDtypeStruct((MLA_HEADS, T, LANE), F32),
                   jax.ShapeDtypeStruct((T, MLA_HEADS * MLA_V), F32)],
        compiler_params=_cp(),
    )(q, k, v, do, lse, delta)


def _dil_attn_bwd(gi, slopes, qn, kn, proj, do, lse, delta, through, B, S):
    d, L, t, window, back = _dil_geometry(gi, S)
    kw, nq = back + t, L // t
    nbias = 2 if back else 1
    scale = DIL_HEAD_DIM ** -0.5
    qk, vspec, pair = _dil_specs(gi, S)

    def body(*refs):
        refs = list(refs)
        sl_ref, q_ref, k_ref, v_ref, do_ref, lse_ref, dl_ref = refs[:7]
        dq_ref, dk_ref, dv_ref, qs, ks, vs, dos, lss, dls, dqs, dks, dvs, bias_ref = refs[-13:]
        _to_classes(q_ref, qs, d, L, scale)
        for src, dst in ((k_ref, ks), (v_ref, vs), (do_ref, dos), (lse_ref, lss), (dl_ref, dls)):
            _to_classes(src, dst, d, L)
        _dil_bias(bias_ref, sl_ref, pl.program_id(1), t, kw, back, window)
        dks[...] = jnp.zeros_like(dks)
        dvs[...] = jnp.zeros_like(dvs)
        lo = _lane_lo((t, LANE))

        def stats(ref, rows):
            x = ref[rows, :]
            return jnp.concatenate([jnp.max(jnp.where(lo, x, NEG), axis=-1, keepdims=True),
                                    jnp.max(jnp.where(lo, NEG, x), axis=-1, keepdims=True)], axis=0)

        def block(g, _):
            qb = g % nq if d > 1 else g
            row0 = pl.multiple_of(g * t, t)
            rows = pl.ds(row0, t)
            early = qb * t < back
            keys = pl.ds(pl.multiple_of(jnp.where(early, row0 - qb * t, row0 - back), t), kw)
            q2 = _stack_heads(qs[rows, :], lo)
            do2 = _stack_heads(dos[rows, :], lo)
            kt = ks[keys, :]
            s = _mm_nt(q2, kt) + bias_ref[jnp.where(early, 0, nbias - 1)]
            p = jnp.exp(s - stats(lss, rows))
            ds = (p * (_mm_nt(do2, vs[keys, :]) - stats(dls, rows))).astype(BF16)
            dq2 = _mm(ds, kt) * scale
            dqs[rows, :] = jnp.where(lo, dq2[:t], dq2[t:])
            dks[keys, :] += _mm_tn(ds, q2)
            dvs[keys, :] += _mm_tn(p, do2)
            return 0

        lax.fori_loop(0, d * nq, block, 0, unroll=DIL_UNROLL if d * nq % DIL_UNROLL == 0 else 1)
        for src, dst in ((dqs, dq_ref), (dks, dk_ref), (dvs, dv_ref)):
            _from_classes(src, dst, d, L)

    in_specs = [pl.BlockSpec(memory_space=pltpu.SMEM), qk, qk, vspec, pair, pair, pair]
    args = [slopes, qn, kn, proj, do, lse, delta]
    aliases = {}
    if through is not None:
        aliases = {len(args) + i: i for i in range(3)}
        in_specs = in_specs + [pl.BlockSpec(memory_space=pl.ANY)] * 3
        args = args + list(through)
    return pl.pallas_call(
        body, name=f"dil_attn_bwd_{gi}", grid=(B, NPAIR), in_specs=in_specs, out_specs=[qk, qk, qk],
        out_shape=[jax.ShapeDtypeStruct((B * S, DIL_QK), F32)] * 3,
        scratch_shapes=[pltpu.VMEM((S, LANE), BF16)] * 4 + [pltpu.VMEM((S, LANE), F32)] * 5
                       + [pltpu.VMEM((nbias, 2 * t, kw), F32)],
        input_output_aliases=aliases,
        compiler_params=_cp(),
    )(*args)


def _merge_common(p_ref, bg_ref, ob_ref, og_refs, lse_refs):
    bz = p_ref[:, CB_BZ * LANE:(CB_BZ + 4) * LANE]
    cz = p_ref[:, CB_CZ * LANE:(CB_CZ + 4) * LANE]
    gates = [_sigmoid(p_ref[:, (CB_GATE + 8 * i) * LANE:(CB_GATE + 8 * i + 8) * LANE]
                      + bg_ref[:, i * D_MODEL:(i + 1) * D_MODEL]) for i in range(3)]
    ob = ob_ref[...]
    lses = [r[...] for r in lse_refs]
    mx = jnp.maximum(jnp.maximum(lses[0], lses[1]), lses[2])
    es = [jnp.exp(v - mx) for v in lses]
    inv = 1.0 / (es[0] + es[1] + es[2])
    alphas = [e * inv for e in es]
    oc = alphas[0] * og_refs[0][...] + alphas[1] * og_refs[1][...] + alphas[2] * og_refs[2][...]
    return bz, cz, gates, ob, alphas, oc


def _merge_fwd(x, proj, b_gate, ya, ob, ogs, lses, woa, wob, woc, wo):
    T = x.shape[0]
    ts = _tile(T, 256)
    MW = 32 * LANE

    def body(x_ref, p_ref, bg_ref, ya_ref, ob_ref, og0, og1, og2, l0, l1, l2, woa_ref, wob_ref, woc_ref, wo_ref, out_ref):
        bz, cz, gates, obv, alphas, oc = _merge_common(p_ref, bg_ref, ob_ref, (og0, og1, og2), (l0, l1, l2))
        yb = obv * _silu(bz)
        yc = oc * _silu(cz)
        merged = (gates[0] * _mm(ya_ref[...], woa_ref[...]) + gates[1] * _mm(yb, wob_ref[...])
                  + gates[2] * _mm(yc, woc_ref[...]))
        out_ref[...] = x_ref[...] + _mm(merged, wo_ref[...])

    def whole(r, c):
        return pl.BlockSpec((r, c), lambda i: (0, 0))

    tok = lambda w: pl.BlockSpec((ts, w), lambda i: (i, 0))
    return pl.pallas_call(
        body, name="merge_fwd", grid=(T // ts,),
        in_specs=[tok(D_MODEL), tok(MW), whole(1, 3 * D_MODEL), tok(CONV_WIDTH)] + [tok(DIL_WIDTH)] * 7
                 + [whole(CONV_WIDTH, D_MODEL)] * 3 + [whole(D_MODEL, D_MODEL)],
        out_specs=tok(D_MODEL),
        out_shape=jax.ShapeDtypeStruct((T, D_MODEL), F32),
        compiler_params=_cp(),
    )(x, proj, b_gate, ya, ob, *ogs, *lses, woa, wob, woc, wo)


def _merge_bwd(dout, proj, b_gate, ya, ob, ogs, lses, woa, wob, woc, wo):
    T = dout.shape[0]
    ts = _tile(T, 256)
    MW = 32 * LANE

    def body(do_ref, p_ref, bg_ref, ya_ref, ob_ref, og0, og1, og2, l0, l1, l2, woa_ref, wob_ref, woc_ref, wo_ref,
             dp_ref, dya_ref, dob_ref, dlb_ref, dg0, dg1, dg2, dl0, dl1, dl2,
             mg_ref, dpa_ref, dpb_ref, dpc_ref, yb_ref, yc_ref, dbg_ref):
        bz, cz, gates, obv, alphas, oc = _merge_common(p_ref, bg_ref, ob_ref, (og0, og1, og2), (l0, l1, l2))
        sb, sc = _silu(bz), _silu(cz)
        yb = obv * sb
        yc = oc * sc
        ps = [_mm(ya_ref[...], woa_ref[...]), _mm(yb, wob_ref[...]), _mm(yc, woc_ref[...])]
        mg_ref[...] = (gates[0] * ps[0] + gates[1] * ps[1] + gates[2] * ps[2]).astype(BF16)
        yb_ref[...] = yb.astype(BF16)
        yc_ref[...] = yc.astype(BF16)
        dm = _mm_nt(do_ref[...], wo_ref[...])
        dps = []
        first = pl.program_id(0) == 0
        for i, dref in enumerate((dpa_ref, dpb_ref, dpc_ref)):
            g = gates[i]
            dpi = (dm * g).astype(BF16)
            dref[...] = dpi
            dps.append(dpi)
            dgp = dm * ps[i] * g * (1.0 - g)
            dp_ref[:, (CB_GATE + 8 * i) * LANE:(CB_GATE + 8 * i + 8) * LANE] = dgp.astype(BF16)
            part = jnp.sum(dgp, axis=0, keepdims=True)

            @pl.when(first)
            def _():
                dbg_ref[:, i * D_MODEL:(i + 1) * D_MODEL] = part

            @pl.when(jnp.logical_not(first))
            def _():
                dbg_ref[:, i * D_MODEL:(i + 1) * D_MODEL] += part

        dya_ref[...] = _mm_nt(dps[0], woa_ref[...])
        dyb = _mm_nt(dps[1], wob_ref[...])
        dyc = _mm_nt(dps[2], woc_ref[...])
        dp_ref[:, CB_BZ * LANE:(CB_BZ + 4) * LANE] = (dyb * obv * _dsilu(bz)).astype(BF16)
        dp_ref[:, CB_CZ * LANE:(CB_CZ + 4) * LANE] = (dyc * oc * _dsilu(cz)).astype(BF16)
        dob = dyb * sb
        doc = dyc * sc
        dob_ref[...] = dob.astype(BF16)
        for c in range(NPAIR):
            cs = slice(c * LANE, (c + 1) * LANE)
            dlb_ref[:, cs] = _head_bcast_sum(dob[:, cs] * obv[:, cs])
            dd = _head_bcast_sum(doc[:, cs] * oc[:, cs])
            for a, dref, lref in zip(alphas, (dg0, dg1, dg2), (dl0, dl1, dl2)):
                dref[:, cs] = a[:, cs] * doc[:, cs]
                lref[:, cs] = a[:, cs] * dd

    def whole(r, c):
        return pl.BlockSpec((r, c), lambda i: (0, 0))

    tok = lambda w: pl.BlockSpec((ts, w), lambda i: (i, 0))
    sd = jax.ShapeDtypeStruct
    W = DIL_WIDTH
    return pl.pallas_call(
        body, name="merge_bwd", grid=(T // ts,),
        in_specs=[tok(D_MODEL), tok(MW), whole(1, 3 * D_MODEL), tok(CONV_WIDTH)] + [tok(W)] * 7
                 + [whole(CONV_WIDTH, D_MODEL)] * 3 + [whole(D_MODEL, D_MODEL)],
        out_specs=[tok(MW), tok(CONV_WIDTH), tok(W), tok(W)] + [tok(W)] * 6
                  + [tok(D_MODEL)] * 4 + [tok(W), tok(W), whole(1, 3 * D_MODEL)],
        out_shape=[sd((T, PP), BF16), sd((T, CONV_WIDTH), F32), sd((T, W), BF16), sd((T, W), F32)]
                  + [sd((T, W), F32)] * 6
                  + [sd((T, D_MODEL), BF16)] * 4 + [sd((T, W), BF16)] * 2 + [sd((1, 3 * D_MODEL), F32)],
        compiler_params=_cp(),
    )(dout, proj, b_gate, ya, ob, *ogs, *lses, woa, wob, woc, wo)


def _loss_head(y, target):
    T = y.shape[0]
    ts = _tile(T, 512)

    def body(y_ref, t_ref, d_ref, l_ref):
        e = y_ref[...] - t_ref[...]
        d_ref[...] = e * (1.0 / D_MODEL)
        l_ref[...] = jnp.zeros((1, 8, LANE), F32) + jnp.sum(e * e)

    tok = pl.BlockSpec((ts, D_MODEL), lambda i: (i, 0))
    return pl.pallas_call(
        body, name="loss_head", grid=(T // ts,), in_specs=[tok, tok],
        out_specs=[tok, pl.BlockSpec((1, 8, LANE), lambda i: (i, 0, 0))],
        out_shape=[jax.ShapeDtypeStruct((T, D_MODEL), F32), jax.ShapeDtypeStruct((T // ts, 8, LANE), F32)],
        compiler_params=_cp(),
    )(y, target)


def _my_index():
    return 4 * lax.axis_index("x") + 2 * lax.axis_index("y") + lax.axis_index("c")


def _peers():
    x, y, c = (lax.axis_index(a) for a in AXES)
    out = []
    for kk in range(1, N_DEV):
        px = 1 - x if kk & 4 else x
        py = 1 - y if kk & 2 else y
        pc = 1 - c if kk & 1 else c
        out.append(((px, py, pc), 4 * px + 2 * py + pc))
    return out


def _exchange(arrays, name, gather):
    n = len(arrays)

    def body(*refs):
        srcs, outs = refs[:n], refs[n:2 * n]
        send_sems, recv_sems, local_sems = refs[2 * n:]
        me = _my_index()
        peers = _peers()
        started = []
        for a, (src, out) in enumerate(zip(srcs, outs)):
            mine = pltpu.make_async_copy(src if gather else src.at[me], out.at[me], local_sems.at[a])
            mine.start()
            started.append(mine)
        sends = []
        for i, (pos, idx) in enumerate(peers):
            for a, (src, out) in enumerate(zip(srcs, outs)):
                cp = pltpu.make_async_remote_copy(
                    src_ref=src if gather else src.at[idx], dst_ref=out.at[me], send_sem=send_sems.at[a, i],
                    recv_sem=recv_sems.at[a, i], device_id=pos, device_id_type=pl.DeviceIdType.MESH)
                cp.start()
                sends.append(cp)
        for i, (pos, idx) in enumerate(peers):
            for a, (src, out) in enumerate(zip(srcs, outs)):
                pltpu.make_async_remote_copy(
                    src_ref=src if gather else src.at[idx], dst_ref=out.at[idx], send_sem=send_sems.at[a, i],
                    recv_sem=recv_sems.at[a, i], device_id=pos, device_id_type=pl.DeviceIdType.MESH).wait_recv()
        for cp in sends:
            cp.wait_send()
        for mine in started:
            mine.wait()

    any_space = pl.BlockSpec(memory_space=pl.ANY)
    return pl.pallas_call(
        body, name=name, in_specs=[any_space] * n, out_specs=[any_space] * n,
        out_shape=[jax.ShapeDtypeStruct(((N_DEV,) + a.shape) if gather else a.shape, a.dtype) for a in arrays],
        scratch_shapes=[pltpu.SemaphoreType.DMA((n, N_DEV - 1)), pltpu.SemaphoreType.DMA((n, N_DEV - 1)),
                        pltpu.SemaphoreType.DMA((n,))],
    )(*arrays)


def _remote_copies(srcs, lands, send_sems, recv_sems, gather):
    me = _my_index()
    out = []
    for i, (pos, idx) in enumerate(_peers()):
        for a, (src, land) in enumerate(zip(srcs, lands)):
            def copy(slot, a=a, src=src, land=land, i=i, pos=pos, idx=idx):
                return pltpu.make_async_remote_copy(
                    src_ref=src if gather else src.at[idx], dst_ref=land.at[slot],
                    send_sem=send_sems.at[a * (N_DEV - 1) + i], recv_sem=recv_sems.at[a * (N_DEV - 1) + i],
                    device_id=pos, device_id_type=pl.DeviceIdType.MESH)
            out.append((copy(me), copy(idx)))
    return out


def _exchange_start(arrays, name, gather):
    n = len(arrays)
    hbm = pl.BlockSpec(memory_space=pltpu.HBM)
    sem = pl.BlockSpec(memory_space=pltpu.SEMAPHORE)
    lands = [lax.empty(((N_DEV,) + a.shape) if gather else a.shape, a.dtype) for a in arrays]

    def body(*refs):
        srcs, lands_ = refs[:n], refs[n:2 * n]
        send_sems, recv_sems = refs[2 * n:2 * n + 2]
        for mine, _ in _remote_copies(srcs, lands_, send_sems, recv_sems, gather):
            mine.start()
        refs[-1][...] = jnp.zeros_like(refs[-1])

    sems = pltpu.SemaphoreType.DMA((n * (N_DEV - 1),))
    buffers = [pltpu.HBM(a.shape, a.dtype) for a in list(arrays) + lands]
    res = pl.pallas_call(
        body, name=name, in_specs=[hbm] * (2 * n), out_specs=[sem, sem] + [hbm] * (2 * n) + [pl.BlockSpec(memory_space=pltpu.VMEM)],
        out_shape=[sems, sems] + buffers + [jax.ShapeDtypeStruct((8, LANE), F32)],
        input_output_aliases={i: 2 + i for i in range(2 * n)},
        compiler_params=pltpu.CompilerParams(has_side_effects=pltpu.SideEffectType.DATAFLOW_SIDE_EFFECTING),
    )(*[pltpu.with_memory_space_constraint(a, pltpu.HBM) for a in list(arrays) + lands])
    return (res[0], res[1], res[2:2 + n], res[2 + n:2 + 2 * n]), res[-1]


def _exchange_wait(handle, after, name, gather):
    send_sems, recv_sems, srcs, lands = handle
    n = len(srcs)
    hbm = pl.BlockSpec(memory_space=pltpu.HBM)
    sem = pl.BlockSpec(memory_space=pltpu.SEMAPHORE)

    def body(*refs):
        for mine, arrival in _remote_copies(refs[:n], refs[n:2 * n], refs[2 * n], refs[2 * n + 1], gather):
            mine.wait_send()
            arrival.wait_recv()

    res = pl.pallas_call(
        body, name=name, in_specs=[hbm] * (2 * n) + [sem, sem, pl.BlockSpec(memory_space=pl.ANY)],
        out_specs=[hbm] * (2 * n), out_shape=[pltpu.HBM(a.shape, a.dtype) for a in list(srcs) + list(lands)],
        input_output_aliases={i: i for i in range(2 * n)},
        compiler_params=pltpu.CompilerParams(has_side_effects=pltpu.SideEffectType.DATAFLOW_SIDE_EFFECTING),
    )(*srcs, *lands, send_sems, recv_sems, after)
    return res[n:]


def _own_slot(land, mine):
    return lax.dynamic_update_slice(land, mine, (_my_index(),) + (0,) * (land.ndim - 1))


def _adamw(w, g, m, v):
    m = ADAM_B1 * m + (1.0 - ADAM_B1) * g
    v = ADAM_B2 * v + (1.0 - ADAM_B2) * (g * g)
    m_hat = m / (1.0 - ADAM_B1 ** ADAM_STEP)
    v_hat = v / (1.0 - ADAM_B2 ** ADAM_STEP)
    delta = -ADAM_LR * (m_hat / (jnp.sqrt(v_hat) + ADAM_EPS) + ADAM_WD * w)
    return delta, m, v


def _reduce_adamw(parts, w, m, v, name):
    nparts = len(parts)
    R, C = parts[0].shape[1:]
    tr = R
    while N_DEV * tr * C * parts[0].dtype.itemsize > REDUCE_BLOCK_BYTES and tr % 32 == 0:
        tr //= 2
    steps = R // tr

    def body(*refs):
        w_ref, m_ref, v_ref, g_ref, d_ref, nm_ref, nv_ref = refs[nparts:]
        for k, p_ref in enumerate(refs[:nparts]):
            @pl.when(pl.program_id(0) // steps == k)
            def _():
                g = p_ref[0].astype(F32)
                for s in range(1, N_DEV):
                    g = g + p_ref[s].astype(F32)
                g_ref[...] = g
                d_ref[...], nm_ref[...], nv_ref[...] = _adamw(w_ref[...], g, m_ref[...], v_ref[...])

    def part_spec(k):
        return pl.BlockSpec((N_DEV, tr, C), lambda i: (0, jnp.clip(i - k * steps, 0, steps - 1), 0))

    row = pl.BlockSpec((tr, C), lambda i: (i, 0))
    return pl.pallas_call(
        body, name=name, grid=(nparts * steps,),
        in_specs=[part_spec(k) for k in range(nparts)] + [row, row, row],
        out_specs=[row] * 4, out_shape=[jax.ShapeDtypeStruct((nparts * R, C), F32)] * 4,
        compiler_params=_cp(),
    )(*parts, w, m, v)


BIG = ("w_in", "w_uq", "w_ukv", "w_out_a", "w_out_b", "w_out_c", "w_o")
SMALL = ("norm_g", "b_gate", "conv_w", "conv_b", "q_a_norm_g", "kv_a_norm_g", "mla_q_norm_g", "mla_k_norm_g",
         "dil_q_norm_g", "dil_k_norm_g")
PACK_ROWS = 128
REDUCE_BLOCK_BYTES = 6 * 1024 * 1024


def _pack_local(tensors):
    flat = jnp.concatenate([t.reshape(-1) for t in tensors])
    pad = (-flat.shape[0]) % (PACK_ROWS * LANE)
    return jnp.concatenate([flat, jnp.zeros((pad,), flat.dtype)]).reshape(-1, LANE)


def _unpack_local(rows, like):
    flat = rows.reshape(-1)
    out, off = [], 0
    for t in like:
        out.append(flat[off:off + t.size].reshape(t.shape))
        off += t.size
    return out


def _cols_to_slots(a):
    k = a.shape[0]
    return a.reshape(k, N_DEV, -1).transpose(1, 0, 2)


def _slots_to_cols(s):
    return s.transpose(1, 0, 2).reshape(s.shape[1], -1)


def _rope_tables(S):
    inv = ROPE_THETA ** (-jnp.arange(0, MLA_ROPE, 2, dtype=F32) / MLA_ROPE)
    ang = jnp.arange(S, dtype=F32)[:, None] * inv[None, :]
    cos, sin = jnp.cos(ang), jnp.sin(ang)
    one = jnp.ones((S, MLA_NOPE), F32)
    z16, z32, z64 = (jnp.zeros((S, n), F32) for n in (16, 32, 64))
    cosp = jnp.concatenate([one, cos, cos, jnp.ones((S, 32), F32)], axis=1)
    sa = jnp.concatenate([z64, -sin, z16, z32], axis=1)
    sb = jnp.concatenate([z64, z16, sin, z32], axis=1)
    return cosp, sa, sb


def _alibi_slopes():
    n = DIL_GROUPS * DIL_HEADS
    m = 2.0 ** (-8.0 * jnp.arange(1, n + 1, dtype=F32) / n)
    return m.reshape(DIL_GROUPS, NPAIR, 2)


def _pad_slots(s):
    n, k, c = s.shape
    return _slots_to_cols(jnp.concatenate([s, jnp.zeros((n, k, LANE - c), s.dtype)], axis=2))


def _layer_params(gw, small, l):
    p = {}
    p["wp"] = _pad_columns(gw["w_in"])
    p["norm_g"] = small["norm_g"][l][None]
    p["b_gate"] = small["b_gate"][l][None]
    p["conv_w"] = gw["conv_w"].transpose(1, 0, 2).reshape(CONV_K, CONV_WIDTH)
    p["conv_b"] = small["conv_b"][l][None]
    p["gq"] = small["q_a_norm_g"][l][None]
    p["gkv"] = small["kv_a_norm_g"][l][None]
    p["wuqp"] = _pad_slots(gw["w_uq"])
    kv = gw["w_ukv"]
    p["wkp"] = _pad_slots(kv[:, :, :MLA_NOPE])
    p["wv"] = kv[:, :, MLA_NOPE:].transpose(1, 0, 2).reshape(MLA_KV_LORA, MLA_HEADS * MLA_V)
    zpad = jnp.zeros((1, LANE - MLA_QK), F32)
    p["gmq"] = jnp.concatenate([small["mla_q_norm_g"][l][None], zpad], axis=1)
    p["gmk"] = jnp.concatenate([small["mla_k_norm_g"][l][None], zpad], axis=1)
    tile = lambda g: jnp.broadcast_to(g[:, None, :], (DIL_GROUPS, DIL_HEADS, DIL_HEAD_DIM)).reshape(1, DIL_QK)
    p["gdq"] = tile(small["dil_q_norm_g"][l])
    p["gdk"] = tile(small["dil_k_norm_g"][l])
    p["woa"], p["wob"], p["woc"] = (_slots_to_cols(gw[n]) for n in ("w_out_a", "w_out_b", "w_out_c"))
    p["wo"] = gw["w_o"].reshape(D_MODEL, D_MODEL)
    return p


def _layer_fwd(x, p, tabs, slopes, B, S):
    proj, h = _inproj_fwd(x, p["norm_g"], p["wp"])
    ya = _mixa_fwd(proj, p["conv_w"], p["conv_b"], B, S)
    q, k, v = _mla_prep_fwd(proj, p["gq"], p["gkv"], p["wuqp"], p["wkp"], p["wv"], p["gmq"], p["gmk"], *tabs, S)
    ob, lse_b = _mla_attn_fwd(q, k, v, B, S)
    qn, kn = _dil_prep_fwd(proj, p["gdq"], p["gdk"])
    ogs, lses = [], []
    for gi in range(DIL_GROUPS):
        o, lse = _dil_attn_fwd(gi, slopes[gi], qn, kn, proj, B, S)
        ogs.append(o)
        lses.append(lse)
    out = _merge_fwd(x, proj, p["b_gate"], ya, ob, ogs, lses, p["woa"], p["wob"], p["woc"], p["wo"])
    saved = dict(x=x, proj=proj, h=h, ya=ya, q=q, k=k, v=v, ob=ob, lse_b=lse_b, qn=qn, kn=kn, ogs=ogs, lses=lses)
    return out, saved


def _layer_bwd(dout, sv, p, tabs, slopes, B, S):
    proj = sv["proj"]
    (dproj, dya, dob, dlb, dg0, dg1, dg2, dl0, dl1, dl2, merged, dpa, dpb, dpc, yb, yc, dbg) = _merge_bwd(
        dout, proj, p["b_gate"], sv["ya"], sv["ob"], sv["ogs"], sv["lses"], p["woa"], p["wob"], p["woc"], p["wo"])
    g = {}
    g["w_o"] = _matmul_tn(merged, dout, "dw_o").reshape(N_DEV, D_MODEL // N_DEV, D_MODEL)
    g["w_out_a"] = _cols_to_slots(_matmul_tn(sv["ya"], dpa, "dw_out_a"))
    g["w_out_b"] = _cols_to_slots(_matmul_tn(yb, dpb, "dw_out_b"))
    g["w_out_c"] = _cols_to_slots(_matmul_tn(yc, dpc, "dw_out_c"))
    g["b_gate"] = dbg[0]
    dproj, st = _mixa_bwd(dproj, dya, proj, p["conv_w"], p["conv_b"], B, S)
    g["conv_w"] = st[0:CONV_K]
    g["conv_b"] = st[CONV_K]
    dq, dk, dv = _mla_attn_bwd(sv["q"], sv["k"], sv["v"], dob, sv["lse_b"], dlb, B, S)
    dproj, dwuqp, dwkp, dwv, dgq, dgkv, dgmq, dgmk = _mla_prep_bwd(
        dproj, dq, dk, dv, proj, p["gq"], p["gkv"], p["wuqp"], p["wkp"], p["wv"], p["gmq"], p["gmk"], *tabs, S)
    g["w_uq"] = _cols_to_slots(dwuqp)[:, :, :MLA_QK]
    g["w_ukv"] = jnp.concatenate([_cols_to_slots(dwkp)[:, :, :MLA_NOPE], _cols_to_slots(dwv)], axis=2)
    g["q_a_norm_g"], g["kv_a_norm_g"] = dgq[0], dgkv[0]
    g["mla_q_norm_g"], g["mla_k_norm_g"] = dgmq[0, :MLA_QK], dgmk[0, :MLA_QK]
    dqkv = None
    for gi, (dog, dlg) in enumerate(((dg0, dl0), (dg1, dl1), (dg2, dl2))):
        dqkv = _dil_attn_bwd(gi, slopes[gi], sv["qn"], sv["kn"], proj, dog, sv["lses"][gi], dlg, dqkv, B, S)
    dproj, dgdq, dgdk = _dil_prep_bwd(dproj, *dqkv, proj, p["gdq"], p["gdk"])
    g["dil_q_norm_g"] = dgdq.reshape(DIL_GROUPS, DIL_HEADS, DIL_HEAD_DIM).sum(axis=1)
    g["dil_k_norm_g"] = dgdk.reshape(DIL_GROUPS, DIL_HEADS, DIL_HEAD_DIM).sum(axis=1)
    g["w_in"] = _unpad_columns(_matmul_tn(sv["h"], dproj, "dw_in"))
    dx, dng = _inproj_bwd_x(dproj, p["wp"], sv["x"], p["norm_g"], dout)
    g["norm_g"] = dng[0]
    return dx, g


def _after(token, a):
    return a if token is None else a + token[0:1, 0:1]


def _local_step(x, target, small, B, S, weights_of, grads_out):
    tabs = _rope_tables(S)
    sl = _alibi_slopes()
    slopes = [sl[gi] * float(DIL_PATTERNS[gi][1]) for gi in range(DIL_GROUPS)]
    params, saved = [], []
    for l in range(DEPTH):
        gw, token = weights_of(l, x)
        p = _layer_params(gw, small, l)
        p["norm_g"] = _after(token, p["norm_g"])
        x, sv = _layer_fwd(x, p, tabs, slopes, B, S)
        params.append(p)
        saved.append(sv)
    dout, lparts = _loss_head(x, target)
    sq = jnp.sum(lparts[:, 0, 0])
    token = None
    for l in reversed(range(DEPTH)):
        p = dict(params[l], b_gate=_after(token, params[l]["b_gate"]))
        dout, g = _layer_bwd(dout, saved[l], p, tabs, slopes, B, S)
        token = grads_out(l, g, dout)
    return sq, dout


def kernel(x, norm_g, w_in, b_gate, conv_w, conv_b, q_a_norm_g, w_uq, kv_a_norm_g, w_ukv, mla_q_norm_g, mla_k_norm_g, dil_q_norm_g, dil_k_norm_g, w_out_a, w_out_b, w_out_c, w_o, loss_target, m_norm_g, m_w_in, m_b_gate, m_conv_w, m_conv_b, m_q_a_norm_g, m_w_uq, m_kv_a_norm_g, m_w_ukv, m_mla_q_norm_g, m_mla_k_norm_g, m_dil_q_norm_g, m_dil_k_norm_g, m_w_out_a, m_w_out_b, m_w_out_c, m_w_o, v_norm_g, v_w_in, v_b_gate, v_conv_w, v_conv_b, v_q_a_norm_g, v_w_uq, v_kv_a_norm_g, v_w_ukv, v_mla_q_norm_g, v_mla_k_norm_g, v_dil_q_norm_g, v_dil_k_norm_g, v_w_out_a, v_w_out_b, v_w_out_c, v_w_o):
    names = ("norm_g", "w_in", "b_gate", "conv_w", "conv_b", "q_a_norm_g", "w_uq", "kv_a_norm_g", "w_ukv",
             "mla_q_norm_g", "mla_k_norm_g", "dil_q_norm_g", "dil_k_norm_g", "w_out_a", "w_out_b", "w_out_c", "w_o")
    w = dict(zip(names, (norm_g, w_in, b_gate, conv_w, conv_b, q_a_norm_g, w_uq, kv_a_norm_g, w_ukv, mla_q_norm_g,
                         mla_k_norm_g, dil_q_norm_g, dil_k_norm_g, w_out_a, w_out_b, w_out_c, w_o)))
    m = dict(zip(names, (m_norm_g, m_w_in, m_b_gate, m_conv_w, m_conv_b, m_q_a_norm_g, m_w_uq, m_kv_a_norm_g, m_w_ukv,
                         m_mla_q_norm_g, m_mla_k_norm_g, m_dil_q_norm_g, m_dil_k_norm_g, m_w_out_a, m_w_out_b,
                         m_w_out_c, m_w_o)))
    v = dict(zip(names, (v_norm_g, v_w_in, v_b_gate, v_conv_w, v_conv_b, v_q_a_norm_g, v_w_uq, v_kv_a_norm_g, v_w_ukv,
                         v_mla_q_norm_g, v_mla_k_norm_g, v_dil_q_norm_g, v_dil_k_norm_g, v_w_out_a, v_w_out_b,
                         v_w_out_c, v_w_o)))
    B, S, _ = x.shape
    me = _my_index()
    cshard = CONV_WIDTH // N_DEV

    shards = [[w[n][l].astype(BF16) for n in BIG] for l in range(DEPTH)]
    state = {}

    def weights_of(l, after):
        if l == 0:
            got = _exchange(shards[0] + [conv_w], "all_gather_weights_0", gather=True)
            state["gather"], token = _exchange_start(shards[1], "all_gather_weights_1_start", gather=True)
            state["conv_w"] = got[-1]
        else:
            landed = _exchange_wait(state["gather"], after, "all_gather_weights_1_wait", gather=True)
            got, token = [_own_slot(a, s[None]) for a, s in zip(landed, shards[1])], None
        gw = dict(zip(BIG, got))
        gw["conv_w"] = state["conv_w"][:, l]
        return gw, token

    recv, small_parts = {}, {}

    def grads_out(l, g, after):
        small_parts[l] = [g[n] for n in SMALL]
        send = [g[n].astype(BF16) for n in BIG]
        if l == DEPTH - 1:
            state["scatter"], token = _exchange_start(send, "exchange_weight_grads_1_start", gather=False)
            state["sent"] = send
            return token
        landed = _exchange_wait(state["scatter"], after, "exchange_weight_grads_1_wait", gather=False)
        mine = [lax.dynamic_slice_in_dim(s, me, 1, axis=0) for s in state["sent"]]
        recv[DEPTH - 1] = [_own_slot(a, s) for a, s in zip(landed, mine)]
        recv[l] = _exchange(send, "exchange_weight_grads_0", gather=False)
        return None

    sq, grad_x = _local_step(x.reshape(B * S, D_MODEL), loss_target.reshape(B * S, D_MODEL), w, B, S,
                             weights_of, grads_out)
    loss = lax.psum(sq * (0.5 / D_MODEL), AXES)

    res = {}
    for i, n in enumerate(BIG):
        rows = lambda a: a.reshape(-1, a.shape[-1])
        outs = _reduce_adamw([recv[l][i] for l in range(DEPTH)], rows(w[n]), rows(m[n]), rows(v[n]),
                             "reduce_adamw_" + n)
        res[n] = tuple(a.reshape(w[n].shape) for a in outs)
    part = {n: jnp.stack([small_parts[l][i] for l in range(DEPTH)]) for i, n in enumerate(SMALL)}

    def widen(t):
        return lax.dynamic_update_slice(jnp.zeros((DEPTH, CONV_K, CONV_WIDTH), F32), t, (0, 0, me * cshard))

    small_like = [part[n] for n in SMALL]
    pick = lambda d: [widen(d[n]) if n == "conv_w" else d[n] for n in SMALL]
    parts, = _exchange([_pack_local(small_like)], "all_gather_small_grads", gather=True)
    gs, ds, ms, vs = _reduce_adamw([parts], _pack_local(pick(w)), _pack_local(pick(m)), _pack_local(pick(v)),
                                   "reduce_adamw_small")
    for n, t in zip(SMALL, zip(*(_unpack_local(a, small_like) for a in (gs, ds, ms, vs)))):
        if n == "conv_w":
            t = tuple(lax.dynamic_slice(a, (0, 0, me * cshard), (DEPTH, CONV_K, cshard)) for a in t)
        res[n] = t

    out = [loss, grad_x.reshape(B, S, D_MODEL)]
    for i in range(4):
        out += [res[n][i] for n in names]
    return tuple(out)
```

```python
import jax
import jax.numpy as jnp
from jax import lax
from jax.experimental import pallas as pl
from jax.experimental.pallas import tpu as pltpu

F32 = jnp.float32
BF16 = jnp.bfloat16

D_MODEL = 1024
DEPTH = 2
CONV_WIDTH = 512
CONV_K = 3
MLA_HEADS = 8
MLA_Q_LORA = 256
MLA_KV_LORA = 128
MLA_NOPE = 64
MLA_ROPE = 32
MLA_V = 64
MLA_QK = MLA_NOPE + MLA_ROPE
ROPE_THETA = 10000.0
DIL_PATTERNS = ((128, 1), (512, 4), (2048, 16))
DIL_GROUPS = 3
DIL_HEADS = 8
DIL_HEAD_DIM = 64
DIL_WIDTH = DIL_HEADS * DIL_HEAD_DIM
DIL_QK = DIL_GROUPS * DIL_WIDTH
EPS = 1e-6
N_IN = 11168

ADAM_LR = 0.001
ADAM_B1 = 0.9
ADAM_B2 = 0.999
ADAM_EPS = 1e-08
ADAM_WD = 0.01
ADAM_STEP = 10

N_DEV = 8
AXES = ("x", "y", "c")
LANE = 128
HALF = 64
NPAIR = 4

CB_BZ, CB_CZ, CB_GATE = 0, 4, 8
CB_A = 32
CB_QKV = 48
CB_CQ, CB_CKV, CB_KPE = 84, 86, 87
NCB = 88
PP = NCB * LANE
SHARD_COLS = N_IN // N_DEV
NEG = -1e30
VMEM_LIMIT = 56 * 1024 * 1024


def _column_chunks():
    out = []
    col = 0

    def seg(nblocks, block_of):
        nonlocal col
        for i in range(nblocks):
            out.append((col, LANE, block_of(i)))
            col += LANE

    seg(4, lambda j: CB_A + 4 * j)
    seg(4, lambda j: CB_A + 4 * j + 1)
    seg(4, lambda j: CB_A + 4 * j + 2)
    seg(4, lambda j: CB_A + 4 * j + 3)
    seg(2, lambda i: CB_CQ + i)
    seg(1, lambda i: CB_CKV)
    out.append((col, MLA_ROPE, CB_KPE))
    col += MLA_ROPE
    seg(4, lambda j: CB_BZ + j)
    seg(12, lambda c: CB_QKV + 3 * c)
    seg(12, lambda c: CB_QKV + 3 * c + 1)
    seg(12, lambda c: CB_QKV + 3 * c + 2)
    seg(4, lambda j: CB_CZ + j)
    seg(24, lambda i: CB_GATE + i)
    assert col == N_IN and sorted(c[2] for c in out) == list(range(NCB))
    return out


COLUMN_CHUNKS = _column_chunks()


def _pad_columns(shards):
    parts = []
    for start, width, _ in sorted(COLUMN_CHUNKS, key=lambda c: c[2]):
        fill = LANE - width
        while width:
            p, off = divmod(start, SHARD_COLS)
            n = min(width, SHARD_COLS - off)
            parts.append(shards[p, :, off:off + n])
            start, width = start + n, width - n
        if fill:
            parts.append(jnp.zeros((shards.shape[1], fill), shards.dtype))
    return jnp.concatenate(parts, axis=1)


def _unpad_columns(wp):
    pieces = [[] for _ in range(N_DEV)]
    for start, width, b in COLUMN_CHUNKS:
        src = b * LANE
        while width:
            p, off = divmod(start, SHARD_COLS)
            n = min(width, SHARD_COLS - off)
            pieces[p].append(wp[:, src:src + n])
            start, width, src = start + n, width - n, src + n
    return jnp.stack([jnp.concatenate(ps, axis=1) for ps in pieces])


def _cp():
    return pltpu.CompilerParams(vmem_limit_bytes=VMEM_LIMIT)


def _rstd(x, n):
    return lax.rsqrt(jnp.sum(x * x, axis=-1, keepdims=True) * (1.0 / n) + EPS)


def _sigmoid(z):
    return 1.0 / (1.0 + jnp.exp(-z))


def _silu(z):
    return z * _sigmoid(z)


def _dsilu(z):
    s = _sigmoid(z)
    return s * (1.0 + z * (1.0 - s))


def _mm(a, b):
    return jnp.dot(a.astype(BF16), b.astype(BF16), preferred_element_type=F32)


def _mm_nt(a, b):
    return lax.dot_general(a.astype(BF16), b.astype(BF16), (((1,), (1,)), ((), ())), preferred_element_type=F32)


def _mm_tn(a, b):
    return lax.dot_general(a.astype(BF16), b.astype(BF16), (((0,), (0,)), ((), ())), preferred_element_type=F32)


def _lane_lo(shape):
    return lax.broadcasted_iota(jnp.int32, shape, len(shape) - 1) < HALF


def _head_bcast_sum(x, terms=3):
    w = x.shape[-1]
    same = (lax.broadcasted_iota(jnp.int32, (w, w), 0) // HALF) == (lax.broadcasted_iota(jnp.int32, (w, w), 1) // HALF)
    ones = jnp.where(same, 1.0, 0.0).astype(jnp.bfloat16)
    total = None
    for _ in range(terms):
        term = x.astype(jnp.bfloat16)
        x = x - term.astype(F32)
        part = jnp.dot(term, ones, preferred_element_type=F32)
        total = part if total is None else total + part
    return total


def _rope(t, cos, sa, sb):
    return t * cos + pltpu.roll(t, LANE - 16, axis=1) * sa + pltpu.roll(t, 16, axis=1) * sb


def _rope_t(d, cos, sa, sb):
    return d * cos + pltpu.roll(d * sa, 16, axis=1) + pltpu.roll(d * sb, LANE - 16, axis=1)


def _shift_down(u, k):
    rows = lax.broadcasted_iota(jnp.int32, u.shape, 0)
    return jnp.where(rows >= k, pltpu.roll(u, k, axis=0), 0.0)


def _shift_up(u, k):
    n = u.shape[0]
    rows = lax.broadcasted_iota(jnp.int32, u.shape, 0)
    return jnp.where(rows < n - k, pltpu.roll(u, n - k, axis=0), 0.0)


def _tile(n, want):
    t = min(n, want)
    assert n % t == 0, (n, want)
    return t


def _inproj_fwd(x, g, wp):
    T = x.shape[0]
    tm, tn = _tile(T, 2048), 512

    def body(x_ref, g_ref, w_ref, proj_ref, ht_ref, h_ref):
        @pl.when(pl.program_id(1) == 0)
        def _():
            n = min(tm, 512)
            for r0 in range(0, tm, n):
                xv = x_ref[r0:r0 + n, :]
                h = xv * _rstd(xv, D_MODEL) * g_ref[...]
                h_ref[r0:r0 + n, :] = h.astype(BF16)
                ht_ref[:, r0:r0 + n] = h.T.astype(BF16)

        proj_ref[...] = jnp.dot(h_ref[...], w_ref[...], preferred_element_type=F32)

    return pl.pallas_call(
        body, name="inproj_fwd", grid=(T // tm, PP // tn),
        in_specs=[pl.BlockSpec((tm, D_MODEL), lambda i, j: (i, 0)),
                  pl.BlockSpec((1, D_MODEL), lambda i, j: (0, 0)),
                  pl.BlockSpec((D_MODEL, tn), lambda i, j: (0, j))],
        out_specs=[pl.BlockSpec((tm, tn), lambda i, j: (i, j)),
                   pl.BlockSpec((D_MODEL, tm), lambda i, j: (0, i))],
        out_shape=[jax.ShapeDtypeStruct((T, PP), F32), jax.ShapeDtypeStruct((D_MODEL, T), BF16)],
        scratch_shapes=[pltpu.VMEM((tm, D_MODEL), BF16)],
        compiler_params=_cp(),
    )(x, g, wp)


def _matmul_nn(at, b, name):
    K, T = at.shape
    N = b.shape[1]
    tt, tn = _tile(T, 1024), _tile(N, 1024)

    def body(a_ref, b_ref, o_ref):
        @pl.when(pl.program_id(1) == 0)
        def _():
            o_ref[...] = jnp.zeros_like(o_ref)

        o_ref[...] += jnp.dot(a_ref[...], b_ref[...], preferred_element_type=F32)

    return pl.pallas_call(
        body, name=name, grid=(N // tn, T // tt),
        in_specs=[pl.BlockSpec((K, tt), lambda j, k: (0, k)),
                  pl.BlockSpec((tt, tn), lambda j, k: (k, j))],
        out_specs=pl.BlockSpec((K, tn), lambda j, k: (0, j)),
        out_shape=jax.ShapeDtypeStruct((K, N), F32),
        compiler_params=_cp(),
    )(at, b)


def _matmul_tn(a, b, name):
    T, K = a.shape
    N = b.shape[1]
    tt, tn = _tile(T, 512), _tile(N, 1024)

    def body(a_ref, b_ref, o_ref):
        @pl.when(pl.program_id(1) == 0)
        def _():
            o_ref[...] = jnp.zeros_like(o_ref)

        o_ref[...] += _mm_tn(a_ref[...], b_ref[...])

    return pl.pallas_call(
        body, name=name, grid=(N // tn, T // tt),
        in_specs=[pl.BlockSpec((tt, K), lambda j, k: (k, 0)),
                  pl.BlockSpec((tt, tn), lambda j, k: (k, j))],
        out_specs=pl.BlockSpec((K, tn), lambda j, k: (0, j)),
        out_shape=jax.ShapeDtypeStruct((K, N), F32),
        compiler_params=_cp(),
    )(a, b)


def _inproj_bwd_x(dproj, wp, x, g, dout):
    T = x.shape[0]
    tm, tk = _tile(T, 1024), 1024
    nk = PP // tk

    def body(dp_ref, w_ref, x_ref, g_ref, do_ref, dx_ref, dg_ref, acc_ref):
        i, k = pl.program_id(0), pl.program_id(1)

        @pl.when(k == 0)
        def _():
            acc_ref[...] = jnp.zeros_like(acc_ref)

        @pl.when((k == 0) & (i == 0))
        def _():
            dg_ref[...] = jnp.zeros_like(dg_ref)

        acc_ref[...] += _mm_nt(dp_ref[...], w_ref[...])

        @pl.when(k == nk - 1)
        def _():
            dh = acc_ref[...]
            xv = x_ref[...]
            r = _rstd(xv, D_MODEL)
            gy = dh * g_ref[...]
            dot = jnp.sum(xv * gy, axis=-1, keepdims=True) * (1.0 / D_MODEL)
            dx_ref[...] = do_ref[...] + r * gy - xv * (r * r * r) * dot
            dg_ref[...] += jnp.sum(dh * xv * r, axis=0, keepdims=True)

    return pl.pallas_call(
        body, name="inproj_bwd_x", grid=(T // tm, nk),
        in_specs=[pl.BlockSpec((tm, tk), lambda i, k: (i, k)),
                  pl.BlockSpec((D_MODEL, tk), lambda i, k: (0, k)),
                  pl.BlockSpec((tm, D_MODEL), lambda i, k: (i, 0)),
                  pl.BlockSpec((1, D_MODEL), lambda i, k: (0, 0)),
                  pl.BlockSpec((tm, D_MODEL), lambda i, k: (i, 0))],
        out_specs=[pl.BlockSpec((tm, D_MODEL), lambda i, k: (i, 0)),
                   pl.BlockSpec((1, D_MODEL), lambda i, k: (0, 0))],
        out_shape=[jax.ShapeDtypeStruct((T, D_MODEL), F32), jax.ShapeDtypeStruct((1, D_MODEL), F32)],
        scratch_shapes=[pltpu.VMEM((tm, D_MODEL), F32)],
        compiler_params=_cp(),
    )(dproj, wp, x, g, dout)


def _mixa_fwd(proj, cw, cb, B, S):
    nc = CONV_WIDTH // LANE
    ca = CB_A // 4

    def body(p_ref, cw_ref, cb_ref, y_ref):
        ab, ac, ax, az = (p_ref[:, i * LANE:(i + 1) * LANE] for i in range(4))
        u = ac * ax
        conv = cb_ref[...] + cw_ref[0:1, :] * _shift_down(u, 2) + cw_ref[1:2, :] * _shift_down(u, 1) + cw_ref[2:3, :] * u
        y_ref[...] = (ab * conv * _silu(az)).astype(BF16)

    return pl.pallas_call(
        body, name="mixa_fwd", grid=(B, nc),
        in_specs=[pl.BlockSpec((S, 4 * LANE), lambda b, j: (b, ca + j)),
                  pl.BlockSpec((CONV_K, LANE), lambda b, j: (0, j)),
                  pl.BlockSpec((1, LANE), lambda b, j: (0, j))],
        out_specs=pl.BlockSpec((S, LANE), lambda b, j: (b, j)),
        out_shape=jax.ShapeDtypeStruct((B * S, CONV_WIDTH), BF16),
        compiler_params=_cp(),
    )(proj, cw, cb)


def _mixa_bwd(dproj, dy, proj, cw, cb, B, S):
    nc = CONV_WIDTH // LANE
    ca = CB_A // 4

    def body(dpin_ref, dy_ref, p_ref, cw_ref, cb_ref, dp_ref, st_ref):
        del dpin_ref
        ab, ac, ax, az = (p_ref[:, i * LANE:(i + 1) * LANE] for i in range(4))
        u = ac * ax
        u1, u2 = _shift_down(u, 1), _shift_down(u, 2)
        w0, w1, w2 = cw_ref[0:1, :], cw_ref[1:2, :], cw_ref[2:3, :]
        conv = cb_ref[...] + w0 * u2 + w1 * u1 + w2 * u
        s = _silu(az)
        d = dy_ref[...]
        dconv = d * ab * s
        du = w2 * dconv + w1 * _shift_up(dconv, 1) + w0 * _shift_up(dconv, 2)
        dp_ref[:, 0:LANE] = (d * conv * s).astype(BF16)
        dp_ref[:, LANE:2 * LANE] = (du * ax).astype(BF16)
        dp_ref[:, 2 * LANE:3 * LANE] = (du * ac).astype(BF16)
        dp_ref[:, 3 * LANE:4 * LANE] = (d * ab * conv * _dsilu(az)).astype(BF16)
        row = lax.broadcasted_iota(jnp.int32, (8, LANE), 0)
        st = jnp.zeros((8, LANE), F32)
        for r, v in enumerate((dconv * u2, dconv * u1, dconv * u, dconv)):
            st = st + jnp.where(row == r, jnp.sum(v, axis=0, keepdims=True), 0.0)

        @pl.when(pl.program_id(1) == 0)
        def _():
            st_ref[...] = st

        @pl.when(pl.program_id(1) != 0)
        def _():
            st_ref[...] += st

    return pl.pallas_call(
        body, name="mixa_bwd", grid=(nc, B),
        in_specs=[pl.BlockSpec(memory_space=pl.ANY),
                  pl.BlockSpec((S, LANE), lambda j, b: (b, j)),
                  pl.BlockSpec((S, 4 * LANE), lambda j, b: (b, ca + j)),
                  pl.BlockSpec((CONV_K, LANE), lambda j, b: (0, j)),
                  pl.BlockSpec((1, LANE), lambda j, b: (0, j))],
        out_specs=[pl.BlockSpec((S, 4 * LANE), lambda j, b: (b, ca + j)),
                   pl.BlockSpec((8, LANE), lambda j, b: (0, j))],
        out_shape=[jax.ShapeDtypeStruct(dproj.shape, BF16), jax.ShapeDtypeStruct((8, CONV_WIDTH), F32)],
        input_output_aliases={0: 0},
        compiler_params=_cp(),
    )(dproj, dy, proj, cw, cb)


def _mla_prep_fwd(proj, gq, gkv, wuqp, wkp, wv, gmq, gmk, cos, sa, sb, S):
    T = proj.shape[0]
    ts = _tile(S, 512)
    ns = S // ts
    W = MLA_HEADS * LANE

    def body(p_ref, gq_ref, gkv_ref, wuq_ref, wk_ref, wv_ref, gmq_ref, gmk_ref, cos_ref, sa_ref, sb_ref,
             q_ref, k_ref, v_ref):
        cq = p_ref[:, 0:2 * LANE]
        ckv = p_ref[:, 2 * LANE:3 * LANE]
        kpe = pltpu.roll(p_ref[:, 3 * LANE:4 * LANE], HALF, axis=1)
        cqn = cq * _rstd(cq, MLA_Q_LORA) * gq_ref[...]
        ckn = (ckv * _rstd(ckv, MLA_KV_LORA) * gkv_ref[...]).astype(BF16)
        q0 = _mm(cqn, wuq_ref[...])
        kn = _mm(ckn, wk_ref[...])
        v_ref[...] = _mm(ckn, wv_ref[...]).astype(BF16)
        c, a, b = cos_ref[...], sa_ref[...], sb_ref[...]
        for h in range(MLA_HEADS):
            q0h = q0[:, h * LANE:(h + 1) * LANE]
            q_ref[h] = _rope(q0h * _rstd(q0h, MLA_QK) * gmq_ref[...], c, a, b).astype(BF16)
            k0h = kn[:, h * LANE:(h + 1) * LANE] + kpe
            k_ref[h] = _rope(k0h * _rstd(k0h, MLA_QK) * gmk_ref[...], c, a, b).astype(BF16)

    def whole(r, c):
        return pl.BlockSpec((r, c), lambda i: (0, 0))

    tab = pl.BlockSpec((ts, LANE), lambda i: (i % ns, 0))
    return pl.pallas_call(
        body, name="mla_prep_fwd", grid=(T // ts,),
        in_specs=[pl.BlockSpec((ts, 4 * LANE), lambda i: (i, CB_CQ // 4)),
                  whole(1, MLA_Q_LORA), whole(1, MLA_KV_LORA), whole(MLA_Q_LORA, W), whole(MLA_KV_LORA, W),
                  whole(MLA_KV_LORA, MLA_HEADS * MLA_V), whole(1, LANE), whole(1, LANE), tab, tab, tab],
        out_specs=[pl.BlockSpec((MLA_HEADS, ts, LANE), lambda i: (0, i, 0)),
                   pl.BlockSpec((MLA_HEADS, ts, LANE), lambda i: (0, i, 0)),
                   pl.BlockSpec((ts, MLA_HEADS * MLA_V), lambda i: (i, 0))],
        out_shape=[jax.ShapeDtypeStruct((MLA_HEADS, T, LANE), BF16), jax.ShapeDtypeStruct((MLA_HEADS, T, LANE), BF16),
                   jax.ShapeDtypeStruct((T, MLA_HEADS * MLA_V), BF16)],
        compiler_params=_cp(),
    )(proj, gq, gkv, wuqp, wkp, wv, gmq, gmk, cos, sa, sb)


def _mla_prep_bwd(dproj, dq, dk, dv, proj, gq, gkv, wuqp, wkp, wv, gmq, gmk, cos, sa, sb, S):
    T = proj.shape[0]
    ts = _tile(S, 256)
    ns = S // ts
    W = MLA_HEADS * LANE

    def body(dpin_ref, dq_ref, dk_ref, dv_ref, p_ref, gq_ref, gkv_ref, wuq_ref, wk_ref, wv_ref, gmq_ref, gmk_ref,
             cos_ref, sa_ref, sb_ref,
             dp_ref, dwuq_ref, dwk_ref, dwv_ref, dgq_ref, dgkv_ref, dgmq_ref, dgmk_ref, dq0_ref, dkn_ref):
        del dpin_ref

        @pl.when(pl.program_id(0) == 0)
        def _():
            for r in (dwuq_ref, dwk_ref, dwv_ref, dgq_ref, dgkv_ref, dgmq_ref, dgmk_ref):
                r[...] = jnp.zeros_like(r)

        cq = p_ref[:, 0:2 * LANE]
        ckv = p_ref[:, 2 * LANE:3 * LANE]
        kpe = pltpu.roll(p_ref[:, 3 * LANE:4 * LANE], HALF, axis=1)
        rq = _rstd(cq, MLA_Q_LORA)
        rkv = _rstd(ckv, MLA_KV_LORA)
        gq, gkv, gmq, gmk = gq_ref[...], gkv_ref[...], gmq_ref[...], gmk_ref[...]
        cqn = (cq * rq * gq).astype(BF16)
        ckn = (ckv * rkv * gkv).astype(BF16)
        q0 = _mm(cqn, wuq_ref[...])
        kn = _mm(ckn, wk_ref[...])
        c, a, b = cos_ref[...], sa_ref[...], sb_ref[...]
        lane = lax.broadcasted_iota(jnp.int32, (ts, LANE), 1)
        dgmq = jnp.zeros((1, LANE), F32)
        dgmk = jnp.zeros((1, LANE), F32)
        dkpe = jnp.zeros((ts, LANE), F32)
        for h in range(MLA_HEADS):
            q0h = q0[:, h * LANE:(h + 1) * LANE]
            r = _rstd(q0h, MLA_QK)
            d1 = _rope_t(dq_ref[h], c, a, b)
            gy = d1 * gmq
            dq0_ref[:, h * LANE:(h + 1) * LANE] = (
                r * gy - q0h * (r * r * r) * (jnp.sum(q0h * gy, axis=-1, keepdims=True) * (1.0 / MLA_QK))).astype(BF16)
            dgmq = dgmq + jnp.sum(d1 * q0h * r, axis=0, keepdims=True)
            k0h = kn[:, h * LANE:(h + 1) * LANE] + kpe
            r = _rstd(k0h, MLA_QK)
            d1 = _rope_t(dk_ref[h], c, a, b)
            gy = d1 * gmk
            dk0 = r * gy - k0h * (r * r * r) * (jnp.sum(k0h * gy, axis=-1, keepdims=True) * (1.0 / MLA_QK))
            dgmk = dgmk + jnp.sum(d1 * k0h * r, axis=0, keepdims=True)
            dkn_ref[:, h * LANE:(h + 1) * LANE] = jnp.where(lane < MLA_NOPE, dk0, 0.0).astype(BF16)
            dkpe = dkpe + jnp.where((lane >= MLA_NOPE) & (lane < MLA_QK), dk0, 0.0)
        dq0 = dq0_ref[...]
        dkn = dkn_ref[...]
        dvv = dv_ref[...]
        dwuq_ref[...] += _mm_tn(cqn, dq0)
        dwk_ref[...] += _mm_tn(ckn, dkn)
        dwv_ref[...] += _mm_tn(ckn, dvv)
        dgmq_ref[...] += dgmq
        dgmk_ref[...] += dgmk
        dcqn = _mm_nt(dq0, wuq_ref[...])
        gy = dcqn * gq
        dp_ref[:, 0:2 * LANE] = (
            rq * gy - cq * (rq * rq * rq) * (jnp.sum(cq * gy, axis=-1, keepdims=True) * (1.0 / MLA_Q_LORA))).astype(BF16)
        dgq_ref[...] += jnp.sum(dcqn * cq * rq, axis=0, keepdims=True)
        dckn = _mm_nt(dkn, wk_ref[...]) + _mm_nt(dvv, wv_ref[...])
        gy = dckn * gkv
        dp_ref[:, 2 * LANE:3 * LANE] = (
            rkv * gy - ckv * (rkv * rkv * rkv) * (jnp.sum(ckv * gy, axis=-1, keepdims=True) * (1.0 / MLA_KV_LORA))).astype(BF16)
        dgkv_ref[...] += jnp.sum(dckn * ckv * rkv, axis=0, keepdims=True)
        dp_ref[:, 3 * LANE:4 * LANE] = pltpu.roll(dkpe, HALF, axis=1).astype(BF16)

    def whole(r, c):
        return pl.BlockSpec((r, c), lambda i: (0, 0))

    tab = pl.BlockSpec((ts, LANE), lambda i: (i % ns, 0))
    heads = pl.BlockSpec((MLA_HEADS, ts, LANE), lambda i: (0, i, 0))
    return pl.pallas_call(
        body, name="mla_prep_bwd", grid=(T // ts,),
        in_specs=[pl.BlockSpec(memory_space=pl.ANY), heads, heads,
                  pl.BlockSpec((ts, MLA_HEADS * MLA_V), lambda i: (i, 0)),
                  pl.BlockSpec((ts, 4 * LANE), lambda i: (i, CB_CQ // 4)),
                  whole(1, MLA_Q_LORA), whole(1, MLA_KV_LORA), whole(MLA_Q_LORA, W), whole(MLA_KV_LORA, W),
                  whole(MLA_KV_LORA, MLA_HEADS * MLA_V), whole(1, LANE), whole(1, LANE), tab, tab, tab],
        out_specs=[pl.BlockSpec((ts, 4 * LANE), lambda i: (i, CB_CQ // 4)),
                   whole(MLA_Q_LORA, W), whole(MLA_KV_LORA, W), whole(MLA_KV_LORA, MLA_HEADS * MLA_V),
                   whole(1, MLA_Q_LORA), whole(1, MLA_KV_LORA), whole(1, LANE), whole(1, LANE)],
        out_shape=[jax.ShapeDtypeStruct(dproj.shape, BF16),
                   jax.ShapeDtypeStruct((MLA_Q_LORA, W), F32), jax.ShapeDtypeStruct((MLA_KV_LORA, W), F32),
                   jax.ShapeDtypeStruct((MLA_KV_LORA, MLA_HEADS * MLA_V), F32),
                   jax.ShapeDtypeStruct((1, MLA_Q_LORA), F32), jax.ShapeDtypeStruct((1, MLA_KV_LORA), F32),
                   jax.ShapeDtypeStruct((1, LANE), F32), jax.ShapeDtypeStruct((1, LANE), F32)],
        scratch_shapes=[pltpu.VMEM((ts, W), BF16), pltpu.VMEM((ts, W), BF16)],
        input_output_aliases={0: 0},
        compiler_params=_cp(),
    )(dproj, dq, dk, dv, proj, gq, gkv, wuqp, wkp, wv, gmq, gmk, cos, sa, sb)


def _dil_prep_fwd(proj, gq, gk):
    T = proj.shape[0]
    ts = _tile(T, 512)
    nc = DIL_QK // LANE

    def body(p_ref, gq_ref, gk_ref, q_ref, k_ref):
        t = p_ref[:, 0:2 * LANE]
        y = t * lax.rsqrt(_head_bcast_sum(t * t, terms=2) * (1.0 / DIL_HEAD_DIM) + EPS)
        q_ref[...] = y[:, 0:LANE] * gq_ref[...]
        k_ref[...] = y[:, LANE:2 * LANE] * gk_ref[...]

    col = pl.BlockSpec((1, LANE), lambda i, c: (0, c))
    out = pl.BlockSpec((ts, LANE), lambda i, c: (i, c))
    return pl.pallas_call(
        body, name="dil_prep_fwd", grid=(T // ts, nc),
        in_specs=[pl.BlockSpec((ts, 3 * LANE), lambda i, c: (i, CB_QKV // 3 + c)), col, col],
        out_specs=[out, out],
        out_shape=[jax.ShapeDtypeStruct((T, DIL_QK), F32)] * 2,
        compiler_params=_cp(),
    )(proj, gq, gk)


def _dil_prep_bwd(dproj, ddq, ddk, ddv, proj, gq, gk):
    T = proj.shape[0]
    ts = _tile(T, 512)
    nc = DIL_QK // LANE

    def body(dpin_ref, ddq_ref, ddk_ref, ddv_ref, p_ref, gq_ref, gk_ref, dp_ref, dgq_ref, dgk_ref):
        del dpin_ref
        first = pl.program_id(1) == 0
        dp_ref[:, 2 * LANE:3 * LANE] = ddv_ref[...].astype(BF16)
        t = p_ref[:, 0:2 * LANE]
        d = jnp.concatenate([ddq_ref[...], ddk_ref[...]], axis=1)
        gy = d * jnp.concatenate([gq_ref[...], gk_ref[...]], axis=1)
        r = lax.rsqrt(_head_bcast_sum(t * t, terms=2) * (1.0 / DIL_HEAD_DIM) + EPS)
        dot = _head_bcast_sum(t * gy, terms=2) * (1.0 / DIL_HEAD_DIM)
        dp_ref[:, 0:2 * LANE] = (r * gy - t * (r * r * r) * dot).astype(BF16)
        part = jnp.sum(d * t * r, axis=0, keepdims=True)
        for i, dg_ref in enumerate((dgq_ref, dgk_ref)):
            @pl.when(first)
            def _():
                dg_ref[...] = part[:, i * LANE:(i + 1) * LANE]

            @pl.when(jnp.logical_not(first))
            def _():
                dg_ref[...] += part[:, i * LANE:(i + 1) * LANE]

    col = pl.BlockSpec((1, LANE), lambda c, i: (0, c))
    tok = pl.BlockSpec((ts, LANE), lambda c, i: (i, c))
    return pl.pallas_call(
        body, name="dil_prep_bwd", grid=(nc, T // ts),
        in_specs=[pl.BlockSpec(memory_space=pl.ANY), tok, tok, tok,
                  pl.BlockSpec((ts, 3 * LANE), lambda c, i: (i, CB_QKV // 3 + c)), col, col],
        out_specs=[pl.BlockSpec((ts, 3 * LANE), lambda c, i: (i, CB_QKV // 3 + c)), col, col],
        out_shape=[jax.ShapeDtypeStruct(dproj.shape, BF16), jax.ShapeDtypeStruct((1, DIL_QK), F32),
                   jax.ShapeDtypeStruct((1, DIL_QK), F32)],
        input_output_aliases={0: 0},
        compiler_params=_cp(),
    )(dproj, ddq, ddk, ddv, proj, gq, gk)


COPY_ROWS = 256


def _to_classes(src_ref, dst_ref, d, L, scale=None):
    n = min(L, COPY_ROWS)
    for r in range(d):
        for c0 in range(0, L, n):
            rows = pl.ds(r + c0 * d, n, stride=d) if d > 1 else pl.ds(c0, n)
            val = src_ref[rows, :]
            if scale is not None:
                val = val * scale
            dst_ref[r * L + c0:r * L + c0 + n, :] = val.astype(dst_ref.dtype)


def _from_classes(src_ref, dst_ref, d, L):
    n = min(L, COPY_ROWS)
    for r in range(d):
        for c0 in range(0, L, n):
            rows = pl.ds(r + c0 * d, n, stride=d) if d > 1 else pl.ds(c0, n)
            dst_ref[rows, :] = src_ref[r * L + c0:r * L + c0 + n, :].astype(dst_ref.dtype)


MLA_TQ, MLA_TK = 512, 512


def _causal_bias(tq, tk, shift):
    row = lax.broadcasted_iota(jnp.int32, (tq, tk), 0)
    col = lax.broadcasted_iota(jnp.int32, (tq, tk), 1)
    return jnp.where(row >= col + shift, 0.0, NEG)


def _mla_specs(S):
    heads = pl.BlockSpec((2, S, LANE), lambda b, j: (j, b, 0))
    pair = pl.BlockSpec((S, LANE), lambda b, j: (b, j))
    return heads, pair


def _mla_attn_fwd(q, k, v, B, S):
    tq = _tile(S, MLA_TQ)
    tk = _tile(tq, MLA_TK)
    nd = tq // tk
    scale = MLA_QK ** -0.5
    heads, pair = _mla_specs(S)

    def body(q_ref, k_ref, v_ref, o_ref, lse_ref):
        lo, lok = _lane_lo((tq, LANE)), _lane_lo((tk, LANE))
        diag = [_causal_bias(tq, tk, i * tk) for i in range(nd)]

        def block(g, _):
            row0 = pl.multiple_of(g * tq, tq)
            rows = pl.ds(row0, tq)
            qs = [q_ref[hh, rows, :] for hh in range(2)]

            one = jnp.ones((), BF16)

            def step(off, carries, bias):
                off = pl.multiple_of(off, tk)
                vt = v_ref[pl.ds(off, tk), :]
                vh = (jnp.where(lok, vt, one), jnp.where(lok, one, vt))
                out = []
                for hh, (m, acc) in enumerate(carries):
                    s = _mm_nt(qs[hh], k_ref[hh, pl.ds(off, tk), :]) * scale
                    if bias is not None:
                        s = s + bias
                    m_new = jnp.maximum(m, jnp.max(s, axis=-1, keepdims=True))
                    p = jnp.exp(s - m_new)
                    out.append((m_new, jnp.exp(m - m_new) * acc + _mm(p, vh[hh])))
                return tuple(out)

            init = (jnp.full((tq, 1), NEG, F32), jnp.zeros((tq, LANE), F32))
            carries = lax.fori_loop(0, g * nd, lambda i, c: step(i * tk, c, None), (init, init))
            for i in range(nd):
                carries = step(row0 + i * tk, carries, diag[i])
            (ma, acca), (mb, accb) = carries
            la, lb = pltpu.roll(acca, HALF, axis=1), pltpu.roll(accb, HALF, axis=1)
            o_ref[rows, :] = jnp.where(lo, acca / la, accb / lb)
            lse_ref[rows, :] = jnp.where(lo, ma + jnp.log(la), mb + jnp.log(lb))
            return 0

        lax.fori_loop(0, S // tq, block, 0)

    return pl.pallas_call(
        body, name="mla_attn_fwd", grid=(B, NPAIR), in_specs=[heads, heads, pair], out_specs=[pair, pair],
        out_shape=[jax.ShapeDtypeStruct((B * S, MLA_HEADS * MLA_V), F32)] * 2,
        compiler_params=_cp(),
    )(q, k, v)


DIL_UNROLL = 8


def _dil_geometry(gi, S):
    span, d = DIL_PATTERNS[gi]
    L = S // d
    t = _tile(L, 128)
    window = span // d
    back = min(-(-window // t) * t, L - t)
    return d, L, t, window, back


def _dil_specs(gi, S):
    qk = pl.BlockSpec((S, LANE), lambda b, j: (b, NPAIR * gi + j))
    v = pl.BlockSpec((S, LANE), lambda b, j: (b, CB_QKV + 3 * (NPAIR * gi + j) + 2))
    pair = pl.BlockSpec((S, LANE), lambda b, j: (b, j))
    return qk, v, pair


def _dil_bias(bias_ref, sl_ref, j, t, kw, back, window):
    row = lax.broadcasted_iota(jnp.int32, (2 * t, kw), 0)
    col = lax.broadcasted_iota(jnp.int32, (2 * t, kw), 1)
    second = row >= t
    slope = jnp.where(second, sl_ref[j, 1], sl_ref[j, 0])
    for n in range(bias_ref.shape[0]):
        dist = jnp.where(second, row - t, row) + n * back - col
        bias_ref[n] = jnp.where((dist >= 0) & (dist <= window), -slope * dist.astype(F32), NEG)


def _stack_heads(x, lo):
    zero = jnp.zeros((), x.dtype)
    return jnp.concatenate([jnp.where(lo, x, zero), jnp.where(lo, zero, x)], axis=0)


def _dil_attn_fwd(gi, slopes, qn, kn, proj, B, S):
    d, L, t, window, back = _dil_geometry(gi, S)
    kw, nq = back + t, L // t
    nbias = 2 if back else 1
    qk, vspec, pair = _dil_specs(gi, S)

    def body(sl_ref, q_ref, k_ref, v_ref, o_ref, lse_ref, qs, ks, vs, os_, ls, bias_ref):
        _to_classes(q_ref, qs, d, L, DIL_HEAD_DIM ** -0.5)
        _to_classes(k_ref, ks, d, L)
        _to_classes(v_ref, vs, d, L)
        _dil_bias(bias_ref, sl_ref, pl.program_id(1), t, kw, back, window)
        lo = _lane_lo((t, LANE))

        def block(g, _):
            qb = g % nq if d > 1 else g
            row0 = pl.multiple_of(g * t, t)
            rows = pl.ds(row0, t)
            early = qb * t < back
            keys = pl.ds(pl.multiple_of(jnp.where(early, row0 - qb * t, row0 - back), t), kw)
            s = _mm_nt(_stack_heads(qs[rows, :], lo), ks[keys, :]) + bias_ref[jnp.where(early, 0, nbias - 1)]
            m = jnp.max(s, axis=-1, keepdims=True)
            p = jnp.exp(s - m)
            l = jnp.sum(p, axis=-1, keepdims=True)
            o2 = _mm(p, vs[keys, :]) / l
            lse2 = m + jnp.log(l)
            os_[rows, :] = jnp.where(lo, o2[:t], o2[t:])
            ls[rows, :] = jnp.where(lo, lse2[:t], lse2[t:])
            return 0

        lax.fori_loop(0, d * nq, block, 0, unroll=DIL_UNROLL if d * nq % DIL_UNROLL == 0 else 1)
        _from_classes(os_, o_ref, d, L)
        _from_classes(ls, lse_ref, d, L)

    return pl.pallas_call(
        body, name=f"dil_attn_fwd_{gi}", grid=(B, NPAIR),
        in_specs=[pl.BlockSpec(memory_space=pltpu.SMEM), qk, qk, vspec], out_specs=[pair, pair],
        out_shape=[jax.ShapeDtypeStruct((B * S, DIL_WIDTH), F32)] * 2,
        scratch_shapes=[pltpu.VMEM((S, LANE), BF16)] * 3 + [pltpu.VMEM((S, LANE), F32)] * 2
                       + [pltpu.VMEM((nbias, 2 * t, kw), F32)],
        compiler_params=_cp(),
    )(slopes, qn, kn, proj)


def _mla_attn_bwd(q, k, v, do, lse, delta, B, S):
    T = B * S
    tq = _tile(S, MLA_TQ)
    tk = _tile(tq, MLA_TK)
    nd = tq // tk
    scale = MLA_QK ** -0.5
    heads, pair = _mla_specs(S)

    def body(q_ref, k_ref, v_ref, do_ref, lse_ref, dl_ref, dq_ref, dk_ref, dv_ref):
        dk_ref[...] = jnp.zeros_like(dk_ref)
        dv_ref[...] = jnp.zeros_like(dv_ref)
        lo = _lane_lo((tq, LANE))
        diag = [_causal_bias(tq, tk, i * tk) for i in range(nd)]

        def block(g, _):
            row0 = pl.multiple_of(g * tq, tq)
            rows = pl.ds(row0, tq)
            for hh in range(2):
                sel = lo if hh == 0 else jnp.logical_not(lo)
                qh = q_ref[hh, rows, :]
                doh = jnp.where(sel, do_ref[rows, :], jnp.zeros((), BF16))
                lse_h = jnp.max(jnp.where(sel, lse_ref[rows, :], NEG), axis=-1, keepdims=True)
                dl_h = jnp.max(jnp.where(sel, dl_ref[rows, :], NEG), axis=-1, keepdims=True)

                def step(off, dq_acc, bias, hh=hh, qh=qh, doh=doh, lse_h=lse_h, dl_h=dl_h):
                    cols = pl.ds(pl.multiple_of(off, tk), tk)
                    kh = k_ref[hh, cols, :]
                    s = _mm_nt(qh, kh) * scale
                    if bias is not None:
                        s = s + bias
                    p = jnp.exp(s - lse_h)
                    dp = _mm_nt(doh, v_ref[cols, :])
                    ds = (p * (dp - dl_h)).astype(BF16)
                    dk_ref[hh, cols, :] += _mm_tn(ds, qh) * scale
                    dv_ref[cols, :] += _mm_tn(p, doh)
                    return dq_acc + _mm(ds, kh)

                dq_acc = lax.fori_loop(0, g * nd, lambda i, a: step(i * tk, a, None), jnp.zeros((tq, LANE), F32))
                for i in range(nd):
                    dq_acc = step(row0 + i * tk, dq_acc, diag[i])
                dq_ref[hh, rows, :] = dq_acc * scale
            return 0

        lax.fori_loop(0, S // tq, block, 0)

    return pl.pallas_call(
        body, name="mla_attn_bwd", grid=(B, NPAIR), in_specs=[heads, heads, pair, pair, pair, pair],
        out_specs=[heads, heads, pair],
        out_shape=[jax.ShapeDtypeStruct((MLA_HEADS, T, LANE), F32), jax.ShapeDtypeStruct((MLA_HEADS, T, LANE), F32),
                   jax.ShapeDtypeStruct((T, MLA_HEADS * MLA_V), F32)],
        compiler_params=_cp(),
    )(q, k, v, do, lse, delta)


def _dil_attn_bwd(gi, slopes, qn, kn, proj, do, lse, delta, through, B, S):
    d, L, t, window, back = _dil_geometry(gi, S)
    kw, nq = back + t, L // t
    nbias = 2 if back else 1
    scale = DIL_HEAD_DIM ** -0.5
    qk, vspec, pair = _dil_specs(gi, S)

    def body(*refs):
        refs = list(refs)
        sl_ref, q_ref, k_ref, v_ref, do_ref, lse_ref, dl_ref = refs[:7]
        dq_ref, dk_ref, dv_ref, qs, ks, vs, dos, lss, dls, dqs, dks, dvs, bias_ref = refs[-13:]
        _to_classes(q_ref, qs, d, L, scale)
        for src, dst in ((k_ref, ks), (v_ref, vs), (do_ref, dos), (lse_ref, lss), (dl_ref, dls)):
            _to_classes(src, dst, d, L)
        _dil_bias(bias_ref, sl_ref, pl.program_id(1), t, kw, back, window)
        dks[...] = jnp.zeros_like(dks)
        dvs[...] = jnp.zeros_like(dvs)
        lo = _lane_lo((t, LANE))

        def stats(ref, rows):
            x = ref[rows, :]
            return jnp.concatenate([jnp.max(jnp.where(lo, x, NEG), axis=-1, keepdims=True),
                                    jnp.max(jnp.where(lo, NEG, x), axis=-1, keepdims=True)], axis=0)

        def block(g, _):
            qb = g % nq if d > 1 else g
            row0 = pl.multiple_of(g * t, t)
            rows = pl.ds(row0, t)
            early = qb * t < back
            keys = pl.ds(pl.multiple_of(jnp.where(early, row0 - qb * t, row0 - back), t), kw)
            q2 = _stack_heads(qs[rows, :], lo)
            do2 = _stack_heads(dos[rows, :], lo)
            kt = ks[keys, :]
            s = _mm_nt(q2, kt) + bias_ref[jnp.where(early, 0, nbias - 1)]
            p = jnp.exp(s - stats(lss, rows))
            ds = (p * (_mm_nt(do2, vs[keys, :]) - stats(dls, rows))).astype(BF16)
            dq2 = _mm(ds, kt) * scale
            dqs[rows, :] = jnp.where(lo, dq2[:t], dq2[t:])
            dks[keys, :] += _mm_tn(ds, q2)
            dvs[keys, :] += _mm_tn(p, do2)
            return 0

        lax.fori_loop(0, d * nq, block, 0, unroll=DIL_UNROLL if d * nq % DIL_UNROLL == 0 else 1)
        for src, dst in ((dqs, dq_ref), (dks, dk_ref), (dvs, dv_ref)):
            _from_classes(src, dst, d, L)

    in_specs = [pl.BlockSpec(memory_space=pltpu.SMEM), qk, qk, vspec, pair, pair, pair]
    args = [slopes, qn, kn, proj, do, lse, delta]
    aliases = {}
    if through is not None:
        aliases = {len(args) + i: i for i in range(3)}
        in_specs = in_specs + [pl.BlockSpec(memory_space=pl.ANY)] * 3
        args = args + list(through)
    return pl.pallas_call(
        body, name=f"dil_attn_bwd_{gi}", grid=(B, NPAIR), in_specs=in_specs, out_specs=[qk, qk, qk],
        out_shape=[jax.ShapeDtypeStruct((B * S, DIL_QK), F32)] * 3,
        scratch_shapes=[pltpu.VMEM((S, LANE), BF16)] * 4 + [pltpu.VMEM((S, LANE), F32)] * 5
                       + [pltpu.VMEM((nbias, 2 * t, kw), F32)],
        input_output_aliases=aliases,
        compiler_params=_cp(),
    )(*args)


def _merge_common(p_ref, bg_ref, ob_ref, og_refs, lse_refs):
    bz = p_ref[:, CB_BZ * LANE:(CB_BZ + 4) * LANE]
    cz = p_ref[:, CB_CZ * LANE:(CB_CZ + 4) * LANE]
    gates = [_sigmoid(p_ref[:, (CB_GATE + 8 * i) * LANE:(CB_GATE + 8 * i + 8) * LANE]
                      + bg_ref[:, i * D_MODEL:(i + 1) * D_MODEL]) for i in range(3)]
    ob = ob_ref[...]
    lses = [r[...] for r in lse_refs]
    mx = jnp.maximum(jnp.maximum(lses[0], lses[1]), lses[2])
    es = [jnp.exp(v - mx) for v in lses]
    inv = 1.0 / (es[0] + es[1] + es[2])
    alphas = [e * inv for e in es]
    oc = alphas[0] * og_refs[0][...] + alphas[1] * og_refs[1][...] + alphas[2] * og_refs[2][...]
    return bz, cz, gates, ob, alphas, oc


def _merge_fwd(x, proj, b_gate, ya, ob, ogs, lses, woa, wob, woc, wo):
    T = x.shape[0]
    ts = _tile(T, 256)
    MW = 32 * LANE

    def body(x_ref, p_ref, bg_ref, ya_ref, ob_ref, og0, og1, og2, l0, l1, l2, woa_ref, wob_ref, woc_ref, wo_ref, out_ref):
        bz, cz, gates, obv, alphas, oc = _merge_common(p_ref, bg_ref, ob_ref, (og0, og1, og2), (l0, l1, l2))
        yb = obv * _silu(bz)
        yc = oc * _silu(cz)
        merged = (gates[0] * _mm(ya_ref[...], woa_ref[...]) + gates[1] * _mm(yb, wob_ref[...])
                  + gates[2] * _mm(yc, woc_ref[...]))
        out_ref[...] = x_ref[...] + _mm(merged, wo_ref[...])

    def whole(r, c):
        return pl.BlockSpec((r, c), lambda i: (0, 0))

    tok = lambda w: pl.BlockSpec((ts, w), lambda i: (i, 0))
    return pl.pallas_call(
        body, name="merge_fwd", grid=(T // ts,),
        in_specs=[tok(D_MODEL), tok(MW), whole(1, 3 * D_MODEL), tok(CONV_WIDTH)] + [tok(DIL_WIDTH)] * 7
                 + [whole(CONV_WIDTH, D_MODEL)] * 3 + [whole(D_MODEL, D_MODEL)],
        out_specs=tok(D_MODEL),
        out_shape=jax.ShapeDtypeStruct((T, D_MODEL), F32),
        compiler_params=_cp(),
    )(x, proj, b_gate, ya, ob, *ogs, *lses, woa, wob, woc, wo)


def _merge_bwd(dout, proj, b_gate, ya, ob, ogs, lses, woa, wob, woc, wo):
    T = dout.shape[0]
    ts = _tile(T, 256)
    MW = 32 * LANE

    def body(do_ref, p_ref, bg_ref, ya_ref, ob_ref, og0, og1, og2, l0, l1, l2, woa_ref, wob_ref, woc_ref, wo_ref,
             dp_ref, dya_ref, dob_ref, dlb_ref, dg0, dg1, dg2, dl0, dl1, dl2,
             mg_ref, dpa_ref, dpb_ref, dpc_ref, yb_ref, yc_ref, dbg_ref):
        bz, cz, gates, obv, alphas, oc = _merge_common(p_ref, bg_ref, ob_ref, (og0, og1, og2), (l0, l1, l2))
        sb, sc = _silu(bz), _silu(cz)
        yb = obv * sb
        yc = oc * sc
        ps = [_mm(ya_ref[...], woa_ref[...]), _mm(yb, wob_ref[...]), _mm(yc, woc_ref[...])]
        mg_ref[...] = (gates[0] * ps[0] + gates[1] * ps[1] + gates[2] * ps[2]).astype(BF16)
        yb_ref[...] = yb.astype(BF16)
        yc_ref[...] = yc.astype(BF16)
        dm = _mm_nt(do_ref[...], wo_ref[...])
        dps = []
        first = pl.program_id(0) == 0
        for i, dref in enumerate((dpa_ref, dpb_ref, dpc_ref)):
            g = gates[i]
            dpi = (dm * g).astype(BF16)
            dref[...] = dpi
            dps.append(dpi)
            dgp = dm * ps[i] * g * (1.0 - g)
            dp_ref[:, (CB_GATE + 8 * i) * LANE:(CB_GATE + 8 * i + 8) * LANE] = dgp.astype(BF16)
            part = jnp.sum(dgp, axis=0, keepdims=True)

            @pl.when(first)
            def _():
                dbg_ref[:, i * D_MODEL:(i + 1) * D_MODEL] = part

            @pl.when(jnp.logical_not(first))
            def _():
                dbg_ref[:, i * D_MODEL:(i + 1) * D_MODEL] += part

        dya_ref[...] = _mm_nt(dps[0], woa_ref[...])
        dyb = _mm_nt(dps[1], wob_ref[...])
        dyc = _mm_nt(dps[2], woc_ref[...])
        dp_ref[:, CB_BZ * LANE:(CB_BZ + 4) * LANE] = (dyb * obv * _dsilu(bz)).astype(BF16)
        dp_ref[:, CB_CZ * LANE:(CB_CZ + 4) * LANE] = (dyc * oc * _dsilu(cz)).astype(BF16)
        dob = dyb * sb
        doc = dyc * sc
        dob_ref[...] = dob.astype(BF16)
        for c in range(NPAIR):
            cs = slice(c * LANE, (c + 1) * LANE)
            dlb_ref[:, cs] = _head_bcast_sum(dob[:, cs] * obv[:, cs])
            dd = _head_bcast_sum(doc[:, cs] * oc[:, cs])
            for a, dref, lref in zip(alphas, (dg0, dg1, dg2), (dl0, dl1, dl2)):
                dref[:, cs] = a[:, cs] * doc[:, cs]
                lref[:, cs] = a[:, cs] * dd

    def whole(r, c):
        return pl.BlockSpec((r, c), lambda i: (0, 0))

    tok = lambda w: pl.BlockSpec((ts, w), lambda i: (i, 0))
    sd = jax.ShapeDtypeStruct
    W = DIL_WIDTH
    return pl.pallas_call(
        body, name="merge_bwd", grid=(T // ts,),
        in_specs=[tok(D_MODEL), tok(MW), whole(1, 3 * D_MODEL), tok(CONV_WIDTH)] + [tok(W)] * 7
                 + [whole(CONV_WIDTH, D_MODEL)] * 3 + [whole(D_MODEL, D_MODEL)],
        out_specs=[tok(MW), tok(CONV_WIDTH), tok(W), tok(W)] + [tok(W)] * 6
                  + [tok(D_MODEL)] * 4 + [tok(W), tok(W), whole(1, 3 * D_MODEL)],
        out_shape=[sd((T, PP), BF16), sd((T, CONV_WIDTH), F32), sd((T, W), BF16), sd((T, W), F32)]
                  + [sd((T, W), F32)] * 6
                  + [sd((T, D_MODEL), BF16)] * 4 + [sd((T, W), BF16)] * 2 + [sd((1, 3 * D_MODEL), F32)],
        compiler_params=_cp(),
    )(dout, proj, b_gate, ya, ob, *ogs, *lses, woa, wob, woc, wo)


def _loss_head(y, target):
    T = y.shape[0]
    ts = _tile(T, 512)

    def body(y_ref, t_ref, d_ref, l_ref):
        e = y_ref[...] - t_ref[...]
        d_ref[...] = e * (1.0 / D_MODEL)
        l_ref[...] = jnp.zeros((1, 8, LANE), F32) + jnp.sum(e * e)

    tok = pl.BlockSpec((ts, D_MODEL), lambda i: (i, 0))
    return pl.pallas_call(
        body, name="loss_head", grid=(T // ts,), in_specs=[tok, tok],
        out_specs=[tok, pl.BlockSpec((1, 8, LANE), lambda i: (i, 0, 0))],
        out_shape=[jax.ShapeDtypeStruct((T, D_MODEL), F32), jax.ShapeDtypeStruct((T // ts, 8, LANE), F32)],
        compiler_params=_cp(),
    )(y, target)


def _my_index():
    return 4 * lax.axis_index("x") + 2 * lax.axis_index("y") + lax.axis_index("c")


def _peers():
    x, y, c = (lax.axis_index(a) for a in AXES)
    out = []
    for kk in range(1, N_DEV):
        px = 1 - x if kk & 4 else x
        py = 1 - y if kk & 2 else y
        pc = 1 - c if kk & 1 else c
        out.append(((px, py, pc), 4 * px + 2 * py + pc))
    return out


def _exchange(arrays, name, gather):
    n = len(arrays)

    def body(*refs):
        srcs, outs = refs[:n], refs[n:2 * n]
        send_sems, recv_sems, local_sems = refs[2 * n:]
        me = _my_index()
        peers = _peers()
        started = []
        for a, (src, out) in enumerate(zip(srcs, outs)):
            mine = pltpu.make_async_copy(src if gather else src.at[me], out.at[me], local_sems.at[a])
            mine.start()
            started.append(mine)
        sends = []
        for i, (pos, idx) in enumerate(peers):
            for a, (src, out) in enumerate(zip(srcs, outs)):
                cp = pltpu.make_async_remote_copy(
                    src_ref=src if gather else src.at[idx], dst_ref=out.at[me], send_sem=send_sems.at[a, i],
                    recv_sem=recv_sems.at[a, i], device_id=pos, device_id_type=pl.DeviceIdType.MESH)
                cp.start()
                sends.append(cp)
        for i, (pos, idx) in enumerate(peers):
            for a, (src, out) in enumerate(zip(srcs, outs)):
                pltpu.make_async_remote_copy(
                    src_ref=src if gather else src.at[idx], dst_ref=out.at[idx], send_sem=send_sems.at[a, i],
                    recv_sem=recv_sems.at[a, i], device_id=pos, device_id_type=pl.DeviceIdType.MESH).wait_recv()
        for cp in sends:
            cp.wait_send()
        for mine in started:
            mine.wait()

    any_space = pl.BlockSpec(memory_space=pl.ANY)
    return pl.pallas_call(
        body, name=name, in_specs=[any_space] * n, out_specs=[any_space] * n,
        out_shape=[jax.ShapeDtypeStruct(((N_DEV,) + a.shape) if gather else a.shape, a.dtype) for a in arrays],
        scratch_shapes=[pltpu.SemaphoreType.DMA((n, N_DEV - 1)), pltpu.SemaphoreType.DMA((n, N_DEV - 1)),
                        pltpu.SemaphoreType.DMA((n,))],
    )(*arrays)


def _remote_copies(srcs, lands, send_sems, recv_sems, gather):
    me = _my_index()
    out = []
    for i, (pos, idx) in enumerate(_peers()):
        for a, (src, land) in enumerate(zip(srcs, lands)):
            def copy(slot, a=a, src=src, land=land, i=i, pos=pos, idx=idx):
                return pltpu.make_async_remote_copy(
                    src_ref=src if gather else src.at[idx], dst_ref=land.at[slot],
                    send_sem=send_sems.at[a * (N_DEV - 1) + i], recv_sem=recv_sems.at[a * (N_DEV - 1) + i],
                    device_id=pos, device_id_type=pl.DeviceIdType.MESH)
            out.append((copy(me), copy(idx)))
    return out


def _exchange_start(arrays, name, gather):
    n = len(arrays)
    hbm = pl.BlockSpec(memory_space=pltpu.HBM)
    sem = pl.BlockSpec(memory_space=pltpu.SEMAPHORE)
    lands = [lax.empty(((N_DEV,) + a.shape) if gather else a.shape, a.dtype) for a in arrays]

    def body(*refs):
        srcs, lands_ = refs[:n], refs[n:2 * n]
        send_sems, recv_sems = refs[2 * n:2 * n + 2]
        for mine, _ in _remote_copies(srcs, lands_, send_sems, recv_sems, gather):
            mine.start()
        refs[-1][...] = jnp.zeros_like(refs[-1])

    sems = pltpu.SemaphoreType.DMA((n * (N_DEV - 1),))
    buffers = [pltpu.HBM(a.shape, a.dtype) for a in list(arrays) + lands]
    res = pl.pallas_call(
        body, name=name, in_specs=[hbm] * (2 * n), out_specs=[sem, sem] + [hbm] * (2 * n) + [pl.BlockSpec(memory_space=pltpu.VMEM)],
        out_shape=[sems, sems] + buffers + [jax.ShapeDtypeStruct((8, LANE), F32)],
        input_output_aliases={i: 2 + i for i in range(2 * n)},
        compiler_params=pltpu.CompilerParams(has_side_effects=pltpu.SideEffectType.DATAFLOW_SIDE_EFFECTING),
    )(*[pltpu.with_memory_space_constraint(a, pltpu.HBM) for a in list(arrays) + lands])
    return (res[0], res[1], res[2:2 + n], res[2 + n:2 + 2 * n]), res[-1]


def _exchange_wait(handle, after, name, gather):
    send_sems, recv_sems, srcs, lands = handle
    n = len(srcs)
    hbm = pl.BlockSpec(memory_space=pltpu.HBM)
    sem = pl.BlockSpec(memory_space=pltpu.SEMAPHORE)

    def body(*refs):
        for mine, arrival in _remote_copies(refs[:n], refs[n:2 * n], refs[2 * n], refs[2 * n + 1], gather):
            mine.wait_send()
            arrival.wait_recv()

    res = pl.pallas_call(
        body, name=name, in_specs=[hbm] * (2 * n) + [sem, sem, pl.BlockSpec(memory_space=pl.ANY)],
        out_specs=[hbm] * (2 * n), out_shape=[pltpu.HBM(a.shape, a.dtype) for a in list(srcs) + list(lands)],
        input_output_aliases={i: i for i in range(2 * n)},
        compiler_params=pltpu.CompilerParams(has_side_effects=pltpu.SideEffectType.DATAFLOW_SIDE_EFFECTING),
    )(*srcs, *lands, send_sems, recv_sems, after)
    return res[n:]


def _own_slot(land, mine):
    return lax.dynamic_update_slice(land, mine, (_my_index(),) + (0,) * (land.ndim - 1))


def _adamw(w, g, m, v):
    m = ADAM_B1 * m + (1.0 - ADAM_B1) * g
    v = ADAM_B2 * v + (1.0 - ADAM_B2) * (g * g)
    m_hat = m / (1.0 - ADAM_B1 ** ADAM_STEP)
    v_hat = v / (1.0 - ADAM_B2 ** ADAM_STEP)
    delta = -ADAM_LR * (m_hat / (jnp.sqrt(v_hat) + ADAM_EPS) + ADAM_WD * w)
    return delta, m, v


def _reduce_adamw(parts, w, m, v, name):
    nparts = len(parts)
    R, C = parts[0].shape[1:]
    tr = R
    while N_DEV * tr * C * parts[0].dtype.itemsize > REDUCE_BLOCK_BYTES and tr % 32 == 0:
        tr //= 2
    steps = R // tr

    def body(*refs):
        w_ref, m_ref, v_ref, g_ref, d_ref, nm_ref, nv_ref = refs[nparts:]
        for k, p_ref in enumerate(refs[:nparts]):
            @pl.when(pl.program_id(0) // steps == k)
            def _():
                g = p_ref[0].astype(F32)
                for s in range(1, N_DEV):
                    g = g + p_ref[s].astype(F32)
                g_ref[...] = g
                d_ref[...], nm_ref[...], nv_ref[...] = _adamw(w_ref[...], g, m_ref[...], v_ref[...])

    def part_spec(k):
        return pl.BlockSpec((N_DEV, tr, C), lambda i: (0, jnp.clip(i - k * steps, 0, steps - 1), 0))

    row = pl.BlockSpec((tr, C), lambda i: (i, 0))
    return pl.pallas_call(
        body, name=name, grid=(nparts * steps,),
        in_specs=[part_spec(k) for k in range(nparts)] + [row, row, row],
        out_specs=[row] * 4, out_shape=[jax.ShapeDtypeStruct((nparts * R, C), F32)] * 4,
        compiler_params=_cp(),
    )(*parts, w, m, v)


BIG = ("w_in", "w_uq", "w_ukv", "w_out_a", "w_out_b", "w_out_c", "w_o")
SMALL = ("norm_g", "b_gate", "conv_w", "conv_b", "q_a_norm_g", "kv_a_norm_g", "mla_q_norm_g", "mla_k_norm_g",
         "dil_q_norm_g", "dil_k_norm_g")
PACK_ROWS = 128
REDUCE_BLOCK_BYTES = 6 * 1024 * 1024


def _pack_local(tensors):
    flat = jnp.concatenate([t.reshape(-1) for t in tensors])
    pad = (-flat.shape[0]) % (PACK_ROWS * LANE)
    return jnp.concatenate([flat, jnp.zeros((pad,), flat.dtype)]).reshape(-1, LANE)


def _unpack_local(rows, like):
    flat = rows.reshape(-1)
    out, off = [], 0
    for t in like:
        out.append(flat[off:off + t.size].reshape(t.shape))
        off += t.size
    return out


def _cols_to_slots(a):
    k = a.shape[0]
    return a.reshape(k, N_DEV, -1).transpose(1, 0, 2)


def _slots_to_cols(s):
    return s.transpose(1, 0, 2).reshape(s.shape[1], -1)


def _rope_tables(S):
    inv = ROPE_THETA ** (-jnp.arange(0, MLA_ROPE, 2, dtype=F32) / MLA_ROPE)
    ang = jnp.arange(S, dtype=F32)[:, None] * inv[None, :]
    cos, sin = jnp.cos(ang), jnp.sin(ang)
    one = jnp.ones((S, MLA_NOPE), F32)
    z16, z32, z64 = (jnp.zeros((S, n), F32) for n in (16, 32, 64))
    cosp = jnp.concatenate([one, cos, cos, jnp.ones((S, 32), F32)], axis=1)
    sa = jnp.concatenate([z64, -sin, z16, z32], axis=1)
    sb = jnp.concatenate([z64, z16, sin, z32], axis=1)
    return cosp, sa, sb


def _alibi_slopes():
    n = DIL_GROUPS * DIL_HEADS
    m = 2.0 ** (-8.0 * jnp.arange(1, n + 1, dtype=F32) / n)
    return m.reshape(DIL_GROUPS, NPAIR, 2)


def _pad_slots(s):
    n, k, c = s.shape
    return _slots_to_cols(jnp.concatenate([s, jnp.zeros((n, k, LANE - c), s.dtype)], axis=2))


def _layer_params(gw, small, l):
    p = {}
    p["wp"] = _pad_columns(gw["w_in"])
    p["norm_g"] = small["norm_g"][l][None]
    p["b_gate"] = small["b_gate"][l][None]
    p["conv_w"] = gw["conv_w"].transpose(1, 0, 2).reshape(CONV_K, CONV_WIDTH)
    p["conv_b"] = small["conv_b"][l][None]
    p["gq"] = small["q_a_norm_g"][l][None]
    p["gkv"] = small["kv_a_norm_g"][l][None]
    p["wuqp"] = _pad_slots(gw["w_uq"])
    kv = gw["w_ukv"]
    p["wkp"] = _pad_slots(kv[:, :, :MLA_NOPE])
    p["wv"] = kv[:, :, MLA_NOPE:].transpose(1, 0, 2).reshape(MLA_KV_LORA, MLA_HEADS * MLA_V)
    zpad = jnp.zeros((1, LANE - MLA_QK), F32)
    p["gmq"] = jnp.concatenate([small["mla_q_norm_g"][l][None], zpad], axis=1)
    p["gmk"] = jnp.concatenate([small["mla_k_norm_g"][l][None], zpad], axis=1)
    tile = lambda g: jnp.broadcast_to(g[:, None, :], (DIL_GROUPS, DIL_HEADS, DIL_HEAD_DIM)).reshape(1, DIL_QK)
    p["gdq"] = tile(small["dil_q_norm_g"][l])
    p["gdk"] = tile(small["dil_k_norm_g"][l])
    p["woa"], p["wob"], p["woc"] = (_slots_to_cols(gw[n]) for n in ("w_out_a", "w_out_b", "w_out_c"))
    p["wo"] = gw["w_o"].reshape(D_MODEL, D_MODEL)
    return p


def _layer_fwd(x, p, tabs, slopes, B, S):
    proj, ht = _inproj_fwd(x, p["norm_g"], p["wp"])
    ya = _mixa_fwd(proj, p["conv_w"], p["conv_b"], B, S)
    q, k, v = _mla_prep_fwd(proj, p["gq"], p["gkv"], p["wuqp"], p["wkp"], p["wv"], p["gmq"], p["gmk"], *tabs, S)
    ob, lse_b = _mla_attn_fwd(q, k, v, B, S)
    qn, kn = _dil_prep_fwd(proj, p["gdq"], p["gdk"])
    ogs, lses = [], []
    for gi in range(DIL_GROUPS):
        o, lse = _dil_attn_fwd(gi, slopes[gi], qn, kn, proj, B, S)
        ogs.append(o)
        lses.append(lse)
    out = _merge_fwd(x, proj, p["b_gate"], ya, ob, ogs, lses, p["woa"], p["wob"], p["woc"], p["wo"])
    saved = dict(x=x, proj=proj, ht=ht, ya=ya, q=q, k=k, v=v, ob=ob, lse_b=lse_b, qn=qn, kn=kn, ogs=ogs, lses=lses)
    return out, saved


def _layer_bwd(dout, sv, p, tabs, slopes, B, S):
    proj = sv["proj"]
    (dproj, dya, dob, dlb, dg0, dg1, dg2, dl0, dl1, dl2, merged, dpa, dpb, dpc, yb, yc, dbg) = _merge_bwd(
        dout, proj, p["b_gate"], sv["ya"], sv["ob"], sv["ogs"], sv["lses"], p["woa"], p["wob"], p["woc"], p["wo"])
    g = {}
    g["w_o"] = _matmul_tn(merged, dout, "dw_o").reshape(N_DEV, D_MODEL // N_DEV, D_MODEL)
    g["w_out_a"] = _cols_to_slots(_matmul_tn(sv["ya"], dpa, "dw_out_a"))
    g["w_out_b"] = _cols_to_slots(_matmul_tn(yb, dpb, "dw_out_b"))
    g["w_out_c"] = _cols_to_slots(_matmul_tn(yc, dpc, "dw_out_c"))
    g["b_gate"] = dbg[0]
    dproj, st = _mixa_bwd(dproj, dya, proj, p["conv_w"], p["conv_b"], B, S)
    g["conv_w"] = st[0:CONV_K]
    g["conv_b"] = st[CONV_K]
    dq, dk, dv = _mla_attn_bwd(sv["q"], sv["k"], sv["v"], dob, sv["lse_b"], dlb, B, S)
    dproj, dwuqp, dwkp, dwv, dgq, dgkv, dgmq, dgmk = _mla_prep_bwd(
        dproj, dq, dk, dv, proj, p["gq"], p["gkv"], p["wuqp"], p["wkp"], p["wv"], p["gmq"], p["gmk"], *tabs, S)
    g["w_uq"] = _cols_to_slots(dwuqp)[:, :, :MLA_QK]
    g["w_ukv"] = jnp.concatenate([_cols_to_slots(dwkp)[:, :, :MLA_NOPE], _cols_to_slots(dwv)], axis=2)
    g["q_a_norm_g"], g["kv_a_norm_g"] = dgq[0], dgkv[0]
    g["mla_q_norm_g"], g["mla_k_norm_g"] = dgmq[0, :MLA_QK], dgmk[0, :MLA_QK]
    dqkv = None
    for gi, (dog, dlg) in enumerate(((dg0, dl0), (dg1, dl1), (dg2, dl2))):
        dqkv = _dil_attn_bwd(gi, slopes[gi], sv["qn"], sv["kn"], proj, dog, sv["lses"][gi], dlg, dqkv, B, S)
    dproj, dgdq, dgdk = _dil_prep_bwd(dproj, *dqkv, proj, p["gdq"], p["gdk"])
    g["dil_q_norm_g"] = dgdq.reshape(DIL_GROUPS, DIL_HEADS, DIL_HEAD_DIM).sum(axis=1)
    g["dil_k_norm_g"] = dgdk.reshape(DIL_GROUPS, DIL_HEADS, DIL_HEAD_DIM).sum(axis=1)
    g["w_in"] = _unpad_columns(_matmul_nn(sv["ht"], dproj, "dw_in"))
    dx, dng = _inproj_bwd_x(dproj, p["wp"], sv["x"], p["norm_g"], dout)
    g["norm_g"] = dng[0]
    return dx, g


def _after(token, a):
    return a if token is None else a + token[0:1, 0:1]


def _local_step(x, target, small, B, S, weights_of, grads_out):
    tabs = _rope_tables(S)
    sl = _alibi_slopes()
    slopes = [sl[gi] * float(DIL_PATTERNS[gi][1]) for gi in range(DIL_GROUPS)]
    params, saved = [], []
    for l in range(DEPTH):
        gw, token = weights_of(l, x)
        p = _layer_params(gw, small, l)
        p["norm_g"] = _after(token, p["norm_g"])
        x, sv = _layer_fwd(x, p, tabs, slopes, B, S)
        params.append(p)
        saved.append(sv)
    dout, lparts = _loss_head(x, target)
    sq = jnp.sum(lparts[:, 0, 0])
    token = None
    for l in reversed(range(DEPTH)):
        p = dict(params[l], b_gate=_after(token, params[l]["b_gate"]))
        dout, g = _layer_bwd(dout, saved[l], p, tabs, slopes, B, S)
        token = grads_out(l, g, dout)
    return sq, dout


def kernel(x, norm_g, w_in, b_gate, conv_w, conv_b, q_a_norm_g, w_uq, kv_a_norm_g, w_ukv, mla_q_norm_g, mla_k_norm_g, dil_q_norm_g, dil_k_norm_g, w_out_a, w_out_b, w_out_c, w_o, loss_target, m_norm_g, m_w_in, m_b_gate, m_conv_w, m_conv_b, m_q_a_norm_g, m_w_uq, m_kv_a_norm_g, m_w_ukv, m_mla_q_norm_g, m_mla_k_norm_g, m_dil_q_norm_g, m_dil_k_norm_g, m_w_out_a, m_w_out_b, m_w_out_c, m_w_o, v_norm_g, v_w_in, v_b_gate, v_conv_w, v_conv_b, v_q_a_norm_g, v_w_uq, v_kv_a_norm_g, v_w_ukv, v_mla_q_norm_g, v_mla_k_norm_g, v_dil_q_norm_g, v_dil_k_norm_g, v_w_out_a, v_w_out_b, v_w_out_c, v_w_o):
    names = ("norm_g", "w_in", "b_gate", "conv_w", "conv_b", "q_a_norm_g", "w_uq", "kv_a_norm_g", "w_ukv",
             "mla_q_norm_g", "mla_k_norm_g", "dil_q_norm_g", "dil_k_norm_g", "w_out_a", "w_out_b", "w_out_c", "w_o")
    w = dict(zip(names, (norm_g, w_in, b_gate, conv_w, conv_b, q_a_norm_g, w_uq, kv_a_norm_g, w_ukv, mla_q_norm_g,
                         mla_k_norm_g, dil_q_norm_g, dil_k_norm_g, w_out_a, w_out_b, w_out_c, w_o)))
    m = dict(zip(names, (m_norm_g, m_w_in, m_b_gate, m_conv_w, m_conv_b, m_q_a_norm_g, m_w_uq, m_kv_a_norm_g, m_w_ukv,
                         m_mla_q_norm_g, m_mla_k_norm_g, m_dil_q_norm_g, m_dil_k_norm_g, m_w_out_a, m_w_out_b,
                         m_w_out_c, m_w_o)))
    v = dict(zip(names, (v_norm_g, v_w_in, v_b_gate, v_conv_w, v_conv_b, v_q_a_norm_g, v_w_uq, v_kv_a_norm_g, v_w_ukv,
                         v_mla_q_norm_g, v_mla_k_norm_g, v_dil_q_norm_g, v_dil_k_norm_g, v_w_out_a, v_w_out_b,
                         v_w_out_c, v_w_o)))
    B, S, _ = x.shape
    me = _my_index()
    cshard = CONV_WIDTH // N_DEV

    shards = [[w[n][l].astype(BF16) for n in BIG] for l in range(DEPTH)]
    state = {}

    def weights_of(l, after):
        if l == 0:
            got = _exchange(shards[0] + [conv_w], "all_gather_weights_0", gather=True)
            state["gather"], token = _exchange_start(shards[1], "all_gather_weights_1_start", gather=True)
            state["conv_w"] = got[-1]
        else:
            landed = _exchange_wait(state["gather"], after, "all_gather_weights_1_wait", gather=True)
            got, token = [_own_slot(a, s[None]) for a, s in zip(landed, shards[1])], None
        gw = dict(zip(BIG, got))
        gw["conv_w"] = state["conv_w"][:, l]
        return gw, token

    recv, small_parts = {}, {}

    def grads_out(l, g, after):
        small_parts[l] = [g[n] for n in SMALL]
        send = [g[n].astype(BF16) for n in BIG]
        if l == DEPTH - 1:
            state["scatter"], token = _exchange_start(send, "exchange_weight_grads_1_start", gather=False)
            state["sent"] = send
            return token
        landed = _exchange_wait(state["scatter"], after, "exchange_weight_grads_1_wait", gather=False)
        mine = [lax.dynamic_slice_in_dim(s, me, 1, axis=0) for s in state["sent"]]
        recv[DEPTH - 1] = [_own_slot(a, s) for a, s in zip(landed, mine)]
        recv[l] = _exchange(send, "exchange_weight_grads_0", gather=False)
        return None

    sq, grad_x = _local_step(x.reshape(B * S, D_MODEL), loss_target.reshape(B * S, D_MODEL), w, B, S,
                             weights_of, grads_out)
    loss = lax.psum(sq * (0.5 / D_MODEL), AXES)

    res = {}
    for i, n in enumerate(BIG):
        rows = lambda a: a.reshape(-1, a.shape[-1])
        outs = _reduce_adamw([recv[l][i] for l in range(DEPTH)], rows(w[n]), rows(m[n]), rows(v[n]),
                             "reduce_adamw_" + n)
        res[n] = tuple(a.reshape(w[n].shape) for a in outs)
    part = {n: jnp.stack([small_parts[l][i] for l in range(DEPTH)]) for i, n in enumerate(SMALL)}

    def widen(t):
        return lax.dynamic_update_slice(jnp.zeros((DEPTH, CONV_K, CONV_WIDTH), F32), t, (0, 0, me * cshard))

    small_like = [part[n] for n in SMALL]
    pick = lambda d: [widen(d[n]) if n == "conv_w" else d[n] for n in SMALL]
    parts, = _exchange([_pack_local(small_like)], "all_gather_small_grads", gather=True)
    gs, ds, ms, vs = _reduce_adamw([parts], _pack_local(pick(w)), _pack_local(pick(m)), _pack_local(pick(v)),
                                   "reduce_adamw_small")
    for n, t in zip(SMALL, zip(*(_unpack_local(a, small_like) for a in (gs, ds, ms, vs)))):
        if n == "conv_w":
            t = tuple(lax.dynamic_slice(a, (0, 0, me * cshard), (DEPTH, CONV_K, cshard)) for a in t)
        res[n] = t

    out = [loss, grad_x.reshape(B, S, D_MODEL)]
    for i in range(4):
        out += [res[n][i] for n in names]
    return tuple(out)
```

```python
import jax
import jax.numpy as jnp
from jax import lax
from jax.experimental import pallas as pl
from jax.experimental.pallas import tpu as pltpu

F32 = jnp.float32
BF16 = jnp.bfloat16

D_MODEL = 1024
DEPTH = 2
CONV_WIDTH = 512
CONV_K = 3
MLA_HEADS = 8
MLA_Q_LORA = 256
MLA_KV_LORA = 128
MLA_NOPE = 64
MLA_ROPE = 32
MLA_V = 64
MLA_QK = MLA_NOPE + MLA_ROPE
ROPE_THETA = 10000.0
DIL_PATTERNS = ((128, 1), (512, 4), (2048, 16))
DIL_GROUPS = 3
DIL_HEADS = 8
DIL_HEAD_DIM = 64
DIL_WIDTH = DIL_HEADS * DIL_HEAD_DIM
DIL_QK = DIL_GROUPS * DIL_WIDTH
EPS = 1e-6
N_IN = 11168

ADAM_LR = 0.001
ADAM_B1 = 0.9
ADAM_B2 = 0.999
ADAM_EPS = 1e-08
ADAM_WD = 0.01
ADAM_STEP = 10

N_DEV = 8
AXES = ("x", "y", "c")
LANE = 128
HALF = 64
NPAIR = 4

CB_BZ, CB_CZ, CB_GATE = 0, 4, 8
CB_A = 32
CB_QKV = 48
CB_CQ, CB_CKV, CB_KPE = 84, 86, 87
NCB = 88
PP = NCB * LANE
SHARD_COLS = N_IN // N_DEV
NEG = -1e30
VMEM_LIMIT = 56 * 1024 * 1024


def _column_chunks():
    out = []
    col = 0

    def seg(nblocks, block_of):
        nonlocal col
        for i in range(nblocks):
            out.append((col, LANE, block_of(i)))
            col += LANE

    seg(4, lambda j: CB_A + 4 * j)
    seg(4, lambda j: CB_A + 4 * j + 1)
    seg(4, lambda j: CB_A + 4 * j + 2)
    seg(4, lambda j: CB_A + 4 * j + 3)
    seg(2, lambda i: CB_CQ + i)
    seg(1, lambda i: CB_CKV)
    out.append((col, MLA_ROPE, CB_KPE))
    col += MLA_ROPE
    seg(4, lambda j: CB_BZ + j)
    seg(12, lambda c: CB_QKV + 3 * c)
    seg(12, lambda c: CB_QKV + 3 * c + 1)
    seg(12, lambda c: CB_QKV + 3 * c + 2)
    seg(4, lambda j: CB_CZ + j)
    seg(24, lambda i: CB_GATE + i)
    assert col == N_IN and sorted(c[2] for c in out) == list(range(NCB))
    return out


COLUMN_CHUNKS = _column_chunks()


def _pad_columns(shards):
    parts = []
    for start, width, _ in sorted(COLUMN_CHUNKS, key=lambda c: c[2]):
        fill = LANE - width
        while width:
            p, off = divmod(start, SHARD_COLS)
            n = min(width, SHARD_COLS - off)
            parts.append(shards[p, :, off:off + n])
            start, width = start + n, width - n
        if fill:
            parts.append(jnp.zeros((shards.shape[1], fill), shards.dtype))
    return jnp.concatenate(parts, axis=1)


def _unpad_columns(wp):
    pieces = [[] for _ in range(N_DEV)]
    for start, width, b in COLUMN_CHUNKS:
        src = b * LANE
        while width:
            p, off = divmod(start, SHARD_COLS)
            n = min(width, SHARD_COLS - off)
            pieces[p].append(wp[:, src:src + n])
            start, width, src = start + n, width - n, src + n
    return jnp.stack([jnp.concatenate(ps, axis=1) for ps in pieces])


def _cp():
    return pltpu.CompilerParams(vmem_limit_bytes=VMEM_LIMIT)


def _rstd(x, n):
    return lax.rsqrt(jnp.sum(x * x, axis=-1, keepdims=True) * (1.0 / n) + EPS)


def _sigmoid(z):
    return 1.0 / (1.0 + jnp.exp(-z))


def _silu(z):
    return z * _sigmoid(z)


def _dsilu(z):
    s = _sigmoid(z)
    return s * (1.0 + z * (1.0 - s))


def _mm(a, b):
    return jnp.dot(a.astype(BF16), b.astype(BF16), preferred_element_type=F32)


def _mm_nt(a, b):
    return lax.dot_general(a.astype(BF16), b.astype(BF16), (((1,), (1,)), ((), ())), preferred_element_type=F32)


def _mm_tn(a, b):
    return lax.dot_general(a.astype(BF16), b.astype(BF16), (((0,), (0,)), ((), ())), preferred_element_type=F32)


def _lane_lo(shape):
    return lax.broadcasted_iota(jnp.int32, shape, len(shape) - 1) < HALF


def _head_bcast_sum(x, terms=3):
    w = x.shape[-1]
    same = (lax.broadcasted_iota(jnp.int32, (w, w), 0) // HALF) == (lax.broadcasted_iota(jnp.int32, (w, w), 1) // HALF)
    ones = jnp.where(same, 1.0, 0.0).astype(jnp.bfloat16)
    total = None
    for _ in range(terms):
        term = x.astype(jnp.bfloat16)
        x = x - term.astype(F32)
        part = jnp.dot(term, ones, preferred_element_type=F32)
        total = part if total is None else total + part
    return total


def _rope(t, cos, sa, sb):
    return t * cos + pltpu.roll(t, LANE - 16, axis=1) * sa + pltpu.roll(t, 16, axis=1) * sb


def _rope_t(d, cos, sa, sb):
    return d * cos + pltpu.roll(d * sa, 16, axis=1) + pltpu.roll(d * sb, LANE - 16, axis=1)


def _shift_down(u, k):
    rows = lax.broadcasted_iota(jnp.int32, u.shape, 0)
    return jnp.where(rows >= k, pltpu.roll(u, k, axis=0), 0.0)


def _shift_up(u, k):
    n = u.shape[0]
    rows = lax.broadcasted_iota(jnp.int32, u.shape, 0)
    return jnp.where(rows < n - k, pltpu.roll(u, n - k, axis=0), 0.0)


def _tile(n, want):
    t = min(n, want)
    assert n % t == 0, (n, want)
    return t


def _inproj_fwd(x, g, wp):
    T = x.shape[0]
    tm, tn = _tile(T, 2048), 512

    def body(x_ref, g_ref, w_ref, proj_ref, ht_ref, h_ref):
        @pl.when(pl.program_id(1) == 0)
        def _():
            n = min(tm, 512)
            for r0 in range(0, tm, n):
                xv = x_ref[r0:r0 + n, :]
                h = xv * _rstd(xv, D_MODEL) * g_ref[...]
                h_ref[r0:r0 + n, :] = h.astype(BF16)
                ht_ref[:, r0:r0 + n] = h.T.astype(BF16)

        proj_ref[...] = jnp.dot(h_ref[...], w_ref[...], preferred_element_type=F32)

    return pl.pallas_call(
        body, name="inproj_fwd", grid=(T // tm, PP // tn),
        in_specs=[pl.BlockSpec((tm, D_MODEL), lambda i, j: (i, 0)),
                  pl.BlockSpec((1, D_MODEL), lambda i, j: (0, 0)),
                  pl.BlockSpec((D_MODEL, tn), lambda i, j: (0, j))],
        out_specs=[pl.BlockSpec((tm, tn), lambda i, j: (i, j)),
                   pl.BlockSpec((D_MODEL, tm), lambda i, j: (0, i))],
        out_shape=[jax.ShapeDtypeStruct((T, PP), F32), jax.ShapeDtypeStruct((D_MODEL, T), BF16)],
        scratch_shapes=[pltpu.VMEM((tm, D_MODEL), BF16)],
        compiler_params=_cp(),
    )(x, g, wp)


def _matmul_nn(at, b, name):
    K, T = at.shape
    N = b.shape[1]
    tt, tn = _tile(T, 1024), _tile(N, 1024)

    def body(a_ref, b_ref, o_ref):
        @pl.when(pl.program_id(1) == 0)
        def _():
            o_ref[...] = jnp.zeros_like(o_ref)

        o_ref[...] += jnp.dot(a_ref[...], b_ref[...], preferred_element_type=F32)

    return pl.pallas_call(
        body, name=name, grid=(N // tn, T // tt),
        in_specs=[pl.BlockSpec((K, tt), lambda j, k: (0, k)),
                  pl.BlockSpec((tt, tn), lambda j, k: (k, j))],
        out_specs=pl.BlockSpec((K, tn), lambda j, k: (0, j)),
        out_shape=jax.ShapeDtypeStruct((K, N), F32),
        compiler_params=_cp(),
    )(at, b)


def _matmul_tn(a, b, name):
    T, K = a.shape
    N = b.shape[1]
    tt, tn = _tile(T, 512), _tile(N, 1024)

    def body(a_ref, b_ref, o_ref):
        @pl.when(pl.program_id(1) == 0)
        def _():
            o_ref[...] = jnp.zeros_like(o_ref)

        o_ref[...] += _mm_tn(a_ref[...], b_ref[...])

    return pl.pallas_call(
        body, name=name, grid=(N // tn, T // tt),
        in_specs=[pl.BlockSpec((tt, K), lambda j, k: (k, 0)),
                  pl.BlockSpec((tt, tn), lambda j, k: (k, j))],
        out_specs=pl.BlockSpec((K, tn), lambda j, k: (0, j)),
        out_shape=jax.ShapeDtypeStruct((K, N), F32),
        compiler_params=_cp(),
    )(a, b)


def _inproj_bwd_x(dproj, wp, x, g, dout):
    T = x.shape[0]
    tm, tk = _tile(T, 1024), 1024
    nk = PP // tk

    def body(dp_ref, w_ref, x_ref, g_ref, do_ref, dx_ref, dg_ref, acc_ref):
        i, k = pl.program_id(0), pl.program_id(1)

        @pl.when(k == 0)
        def _():
            acc_ref[...] = jnp.zeros_like(acc_ref)

        @pl.when((k == 0) & (i == 0))
        def _():
            dg_ref[...] = jnp.zeros_like(dg_ref)

        acc_ref[...] += _mm_nt(dp_ref[...], w_ref[...])

        @pl.when(k == nk - 1)
        def _():
            dh = acc_ref[...]
            xv = x_ref[...]
            r = _rstd(xv, D_MODEL)
            gy = dh * g_ref[...]
            dot = jnp.sum(xv * gy, axis=-1, keepdims=True) * (1.0 / D_MODEL)
            dx_ref[...] = do_ref[...] + r * gy - xv * (r * r * r) * dot
            dg_ref[...] += jnp.sum(dh * xv * r, axis=0, keepdims=True)

    return pl.pallas_call(
        body, name="inproj_bwd_x", grid=(T // tm, nk),
        in_specs=[pl.BlockSpec((tm, tk), lambda i, k: (i, k)),
                  pl.BlockSpec((D_MODEL, tk), lambda i, k: (0, k)),
                  pl.BlockSpec((tm, D_MODEL), lambda i, k: (i, 0)),
                  pl.BlockSpec((1, D_MODEL), lambda i, k: (0, 0)),
                  pl.BlockSpec((tm, D_MODEL), lambda i, k: (i, 0))],
        out_specs=[pl.BlockSpec((tm, D_MODEL), lambda i, k: (i, 0)),
                   pl.BlockSpec((1, D_MODEL), lambda i, k: (0, 0))],
        out_shape=[jax.ShapeDtypeStruct((T, D_MODEL), F32), jax.ShapeDtypeStruct((1, D_MODEL), F32)],
        scratch_shapes=[pltpu.VMEM((tm, D_MODEL), F32)],
        compiler_params=_cp(),
    )(dproj, wp, x, g, dout)


def _mixa_fwd(proj, cw, cb, B, S):
    nc = CONV_WIDTH // LANE
    ca = CB_A // 4

    def body(p_ref, cw_ref, cb_ref, y_ref):
        ab, ac, ax, az = (p_ref[:, i * LANE:(i + 1) * LANE] for i in range(4))
        u = ac * ax
        conv = cb_ref[...] + cw_ref[0:1, :] * _shift_down(u, 2) + cw_ref[1:2, :] * _shift_down(u, 1) + cw_ref[2:3, :] * u
        y_ref[...] = (ab * conv * _silu(az)).astype(BF16)

    return pl.pallas_call(
        body, name="mixa_fwd", grid=(B, nc),
        in_specs=[pl.BlockSpec((S, 4 * LANE), lambda b, j: (b, ca + j)),
                  pl.BlockSpec((CONV_K, LANE), lambda b, j: (0, j)),
                  pl.BlockSpec((1, LANE), lambda b, j: (0, j))],
        out_specs=pl.BlockSpec((S, LANE), lambda b, j: (b, j)),
        out_shape=jax.ShapeDtypeStruct((B * S, CONV_WIDTH), BF16),
        compiler_params=_cp(),
    )(proj, cw, cb)


def _mixa_bwd(dproj, dy, proj, cw, cb, B, S):
    nc = CONV_WIDTH // LANE
    ca = CB_A // 4

    def body(dpin_ref, dy_ref, p_ref, cw_ref, cb_ref, dp_ref, st_ref):
        del dpin_ref
        ab, ac, ax, az = (p_ref[:, i * LANE:(i + 1) * LANE] for i in range(4))
        u = ac * ax
        u1, u2 = _shift_down(u, 1), _shift_down(u, 2)
        w0, w1, w2 = cw_ref[0:1, :], cw_ref[1:2, :], cw_ref[2:3, :]
        conv = cb_ref[...] + w0 * u2 + w1 * u1 + w2 * u
        s = _silu(az)
        d = dy_ref[...]
        dconv = d * ab * s
        du = w2 * dconv + w1 * _shift_up(dconv, 1) + w0 * _shift_up(dconv, 2)
        dp_ref[:, 0:LANE] = (d * conv * s).astype(BF16)
        dp_ref[:, LANE:2 * LANE] = (du * ax).astype(BF16)
        dp_ref[:, 2 * LANE:3 * LANE] = (du * ac).astype(BF16)
        dp_ref[:, 3 * LANE:4 * LANE] = (d * ab * conv * _dsilu(az)).astype(BF16)
        row = lax.broadcasted_iota(jnp.int32, (8, LANE), 0)
        st = jnp.zeros((8, LANE), F32)
        for r, v in enumerate((dconv * u2, dconv * u1, dconv * u, dconv)):
            st = st + jnp.where(row == r, jnp.sum(v, axis=0, keepdims=True), 0.0)

        @pl.when(pl.program_id(1) == 0)
        def _():
            st_ref[...] = st

        @pl.when(pl.program_id(1) != 0)
        def _():
            st_ref[...] += st

    return pl.pallas_call(
        body, name="mixa_bwd", grid=(nc, B),
        in_specs=[pl.BlockSpec(memory_space=pl.ANY),
                  pl.BlockSpec((S, LANE), lambda j, b: (b, j)),
                  pl.BlockSpec((S, 4 * LANE), lambda j, b: (b, ca + j)),
                  pl.BlockSpec((CONV_K, LANE), lambda j, b: (0, j)),
                  pl.BlockSpec((1, LANE), lambda j, b: (0, j))],
        out_specs=[pl.BlockSpec((S, 4 * LANE), lambda j, b: (b, ca + j)),
                   pl.BlockSpec((8, LANE), lambda j, b: (0, j))],
        out_shape=[jax.ShapeDtypeStruct(dproj.shape, BF16), jax.ShapeDtypeStruct((8, CONV_WIDTH), F32)],
        input_output_aliases={0: 0},
        compiler_params=_cp(),
    )(dproj, dy, proj, cw, cb)


def _mla_prep_fwd(proj, gq, gkv, wuqp, wkp, wv, gmq, gmk, cos, sa, sb, S):
    T = proj.shape[0]
    ts = _tile(S, 512)
    ns = S // ts
    W = MLA_HEADS * LANE

    def body(p_ref, gq_ref, gkv_ref, wuq_ref, wk_ref, wv_ref, gmq_ref, gmk_ref, cos_ref, sa_ref, sb_ref,
             q_ref, k_ref, v_ref):
        cq = p_ref[:, 0:2 * LANE]
        ckv = p_ref[:, 2 * LANE:3 * LANE]
        kpe = pltpu.roll(p_ref[:, 3 * LANE:4 * LANE], HALF, axis=1)
        cqn = cq * _rstd(cq, MLA_Q_LORA) * gq_ref[...]
        ckn = (ckv * _rstd(ckv, MLA_KV_LORA) * gkv_ref[...]).astype(BF16)
        q0 = _mm(cqn, wuq_ref[...])
        kn = _mm(ckn, wk_ref[...])
        v_ref[...] = _mm(ckn, wv_ref[...]).astype(BF16)
        c, a, b = cos_ref[...], sa_ref[...], sb_ref[...]
        for h in range(MLA_HEADS):
            q0h = q0[:, h * LANE:(h + 1) * LANE]
            q_ref[h] = _rope(q0h * _rstd(q0h, MLA_QK) * gmq_ref[...], c, a, b).astype(BF16)
            k0h = kn[:, h * LANE:(h + 1) * LANE] + kpe
            k_ref[h] = _rope(k0h * _rstd(k0h, MLA_QK) * gmk_ref[...], c, a, b).astype(BF16)

    def whole(r, c):
        return pl.BlockSpec((r, c), lambda i: (0, 0))

    tab = pl.BlockSpec((ts, LANE), lambda i: (i % ns, 0))
    return pl.pallas_call(
        body, name="mla_prep_fwd", grid=(T // ts,),
        in_specs=[pl.BlockSpec((ts, 4 * LANE), lambda i: (i, CB_CQ // 4)),
                  whole(1, MLA_Q_LORA), whole(1, MLA_KV_LORA), whole(MLA_Q_LORA, W), whole(MLA_KV_LORA, W),
                  whole(MLA_KV_LORA, MLA_HEADS * MLA_V), whole(1, LANE), whole(1, LANE), tab, tab, tab],
        out_specs=[pl.BlockSpec((MLA_HEADS, ts, LANE), lambda i: (0, i, 0)),
                   pl.BlockSpec((MLA_HEADS, ts, LANE), lambda i: (0, i, 0)),
                   pl.BlockSpec((ts, MLA_HEADS * MLA_V), lambda i: (i, 0))],
        out_shape=[jax.ShapeDtypeStruct((MLA_HEADS, T, LANE), BF16), jax.ShapeDtypeStruct((MLA_HEADS, T, LANE), BF16),
                   jax.ShapeDtypeStruct((T, MLA_HEADS * MLA_V), BF16)],
        compiler_params=_cp(),
    )(proj, gq, gkv, wuqp, wkp, wv, gmq, gmk, cos, sa, sb)


def _mla_prep_bwd(dproj, dq, dk, dv, proj, gq, gkv, wuqp, wkp, wv, gmq, gmk, cos, sa, sb, S):
    T = proj.shape[0]
    ts = _tile(S, 256)
    ns = S // ts
    W = MLA_HEADS * LANE

    def body(dpin_ref, dq_ref, dk_ref, dv_ref, p_ref, gq_ref, gkv_ref, wuq_ref, wk_ref, wv_ref, gmq_ref, gmk_ref,
             cos_ref, sa_ref, sb_ref,
             dp_ref, dwuq_ref, dwk_ref, dwv_ref, dgq_ref, dgkv_ref, dgmq_ref, dgmk_ref, dq0_ref, dkn_ref):
        del dpin_ref

        @pl.when(pl.program_id(0) == 0)
        def _():
            for r in (dwuq_ref, dwk_ref, dwv_ref, dgq_ref, dgkv_ref, dgmq_ref, dgmk_ref):
                r[...] = jnp.zeros_like(r)

        cq = p_ref[:, 0:2 * LANE]
        ckv = p_ref[:, 2 * LANE:3 * LANE]
        kpe = pltpu.roll(p_ref[:, 3 * LANE:4 * LANE], HALF, axis=1)
        rq = _rstd(cq, MLA_Q_LORA)
        rkv = _rstd(ckv, MLA_KV_LORA)
        gq, gkv, gmq, gmk = gq_ref[...], gkv_ref[...], gmq_ref[...], gmk_ref[...]
        cqn = (cq * rq * gq).astype(BF16)
        ckn = (ckv * rkv * gkv).astype(BF16)
        q0 = _mm(cqn, wuq_ref[...])
        kn = _mm(ckn, wk_ref[...])
        c, a, b = cos_ref[...], sa_ref[...], sb_ref[...]
        lane = lax.broadcasted_iota(jnp.int32, (ts, LANE), 1)
        dgmq = jnp.zeros((1, LANE), F32)
        dgmk = jnp.zeros((1, LANE), F32)
        dkpe = jnp.zeros((ts, LANE), F32)
        for h in range(MLA_HEADS):
            q0h = q0[:, h * LANE:(h + 1) * LANE]
            r = _rstd(q0h, MLA_QK)
            d1 = _rope_t(dq_ref[h], c, a, b)
            gy = d1 * gmq
            dq0_ref[:, h * LANE:(h + 1) * LANE] = (
                r * gy - q0h * (r * r * r) * (jnp.sum(q0h * gy, axis=-1, keepdims=True) * (1.0 / MLA_QK))).astype(BF16)
            dgmq = dgmq + jnp.sum(d1 * q0h * r, axis=0, keepdims=True)
            k0h = kn[:, h * LANE:(h + 1) * LANE] + kpe
            r = _rstd(k0h, MLA_QK)
            d1 = _rope_t(dk_ref[h], c, a, b)
            gy = d1 * gmk
            dk0 = r * gy - k0h * (r * r * r) * (jnp.sum(k0h * gy, axis=-1, keepdims=True) * (1.0 / MLA_QK))
            dgmk = dgmk + jnp.sum(d1 * k0h * r, axis=0, keepdims=True)
            dkn_ref[:, h * LANE:(h + 1) * LANE] = jnp.where(lane < MLA_NOPE, dk0, 0.0).astype(BF16)
            dkpe = dkpe + jnp.where((lane >= MLA_NOPE) & (lane < MLA_QK), dk0, 0.0)
        dq0 = dq0_ref[...]
        dkn = dkn_ref[...]
        dvv = dv_ref[...]
        dwuq_ref[...] += _mm_tn(cqn, dq0)
        dwk_ref[...] += _mm_tn(ckn, dkn)
        dwv_ref[...] += _mm_tn(ckn, dvv)
        dgmq_ref[...] += dgmq
        dgmk_ref[...] += dgmk
        dcqn = _mm_nt(dq0, wuq_ref[...])
        gy = dcqn * gq
        dp_ref[:, 0:2 * LANE] = (
            rq * gy - cq * (rq * rq * rq) * (jnp.sum(cq * gy, axis=-1, keepdims=True) * (1.0 / MLA_Q_LORA))).astype(BF16)
        dgq_ref[...] += jnp.sum(dcqn * cq * rq, axis=0, keepdims=True)
        dckn = _mm_nt(dkn, wk_ref[...]) + _mm_nt(dvv, wv_ref[...])
        gy = dckn * gkv
        dp_ref[:, 2 * LANE:3 * LANE] = (
            rkv * gy - ckv * (rkv * rkv * rkv) * (jnp.sum(ckv * gy, axis=-1, keepdims=True) * (1.0 / MLA_KV_LORA))).astype(BF16)
        dgkv_ref[...] += jnp.sum(dckn * ckv * rkv, axis=0, keepdims=True)
        dp_ref[:, 3 * LANE:4 * LANE] = pltpu.roll(dkpe, HALF, axis=1).astype(BF16)

    def whole(r, c):
        return pl.BlockSpec((r, c), lambda i: (0, 0))

    tab = pl.BlockSpec((ts, LANE), lambda i: (i % ns, 0))
    heads = pl.BlockSpec((MLA_HEADS, ts, LANE), lambda i: (0, i, 0))
    return pl.pallas_call(
        body, name="mla_prep_bwd", grid=(T // ts,),
        in_specs=[pl.BlockSpec(memory_space=pl.ANY), heads, heads,
                  pl.BlockSpec((ts, MLA_HEADS * MLA_V), lambda i: (i, 0)),
                  pl.BlockSpec((ts, 4 * LANE), lambda i: (i, CB_CQ // 4)),
                  whole(1, MLA_Q_LORA), whole(1, MLA_KV_LORA), whole(MLA_Q_LORA, W), whole(MLA_KV_LORA, W),
                  whole(MLA_KV_LORA, MLA_HEADS * MLA_V), whole(1, LANE), whole(1, LANE), tab, tab, tab],
        out_specs=[pl.BlockSpec((ts, 4 * LANE), lambda i: (i, CB_CQ // 4)),
                   whole(MLA_Q_LORA, W), whole(MLA_KV_LORA, W), whole(MLA_KV_LORA, MLA_HEADS * MLA_V),
                   whole(1, MLA_Q_LORA), whole(1, MLA_KV_LORA), whole(1, LANE), whole(1, LANE)],
        out_shape=[jax.ShapeDtypeStruct(dproj.shape, BF16),
                   jax.ShapeDtypeStruct((MLA_Q_LORA, W), F32), jax.ShapeDtypeStruct((MLA_KV_LORA, W), F32),
                   jax.ShapeDtypeStruct((MLA_KV_LORA, MLA_HEADS * MLA_V), F32),
                   jax.ShapeDtypeStruct((1, MLA_Q_LORA), F32), jax.ShapeDtypeStruct((1, MLA_KV_LORA), F32),
                   jax.ShapeDtypeStruct((1, LANE), F32), jax.ShapeDtypeStruct((1, LANE), F32)],
        scratch_shapes=[pltpu.VMEM((ts, W), BF16), pltpu.VMEM((ts, W), BF16)],
        input_output_aliases={0: 0},
        compiler_params=_cp(),
    )(dproj, dq, dk, dv, proj, gq, gkv, wuqp, wkp, wv, gmq, gmk, cos, sa, sb)


def _dil_prep_fwd(proj, gq, gk):
    T = proj.shape[0]
    ts = _tile(T, 512)
    gw = 3 * NPAIR * LANE

    def body(p_ref, gq_ref, gk_ref, q_ref, k_ref):
        for c in range(NPAIR):
            t = p_ref[:, 3 * c * LANE:(3 * c + 2) * LANE]
            y = t * lax.rsqrt(_head_bcast_sum(t * t, terms=2) * (1.0 / DIL_HEAD_DIM) + EPS)
            cs = slice(c * LANE, (c + 1) * LANE)
            q_ref[:, cs] = y[:, 0:LANE] * gq_ref[:, cs]
            k_ref[:, cs] = y[:, LANE:2 * LANE] * gk_ref[:, cs]

    col = pl.BlockSpec((1, DIL_WIDTH), lambda i, g: (0, g))
    out = pl.BlockSpec((ts, DIL_WIDTH), lambda i, g: (i, g))
    return pl.pallas_call(
        body, name="dil_prep_fwd", grid=(T // ts, DIL_GROUPS),
        in_specs=[pl.BlockSpec((ts, gw), lambda i, g: (i, CB_QKV * LANE // gw + g)), col, col],
        out_specs=[out, out],
        out_shape=[jax.ShapeDtypeStruct((T, DIL_QK), F32)] * 2,
        compiler_params=_cp(),
    )(proj, gq, gk)


def _dil_prep_bwd(dproj, ddq, ddk, ddv, proj, gq, gk):
    T = proj.shape[0]
    ts = _tile(T, 512)
    gw = 3 * NPAIR * LANE

    def body(dpin_ref, ddq_ref, ddk_ref, ddv_ref, p_ref, gq_ref, gk_ref, dp_ref, dgq_ref, dgk_ref):
        del dpin_ref

        @pl.when(pl.program_id(1) == 0)
        def _():
            dgq_ref[...] = jnp.zeros_like(dgq_ref)
            dgk_ref[...] = jnp.zeros_like(dgk_ref)

        for c in range(NPAIR):
            cs = slice(c * LANE, (c + 1) * LANE)
            dp_ref[:, (3 * c + 2) * LANE:(3 * c + 3) * LANE] = ddv_ref[:, cs].astype(BF16)
            t = p_ref[:, 3 * c * LANE:(3 * c + 2) * LANE]
            d = jnp.concatenate([ddq_ref[:, cs], ddk_ref[:, cs]], axis=1)
            gy = d * jnp.concatenate([gq_ref[:, cs], gk_ref[:, cs]], axis=1)
            r = lax.rsqrt(_head_bcast_sum(t * t, terms=2) * (1.0 / DIL_HEAD_DIM) + EPS)
            dot = _head_bcast_sum(t * gy, terms=2) * (1.0 / DIL_HEAD_DIM)
            dp_ref[:, 3 * c * LANE:(3 * c + 2) * LANE] = (r * gy - t * (r * r * r) * dot).astype(BF16)
            part = jnp.sum(d * t * r, axis=0, keepdims=True)
            dgq_ref[:, cs] += part[:, 0:LANE]
            dgk_ref[:, cs] += part[:, LANE:2 * LANE]

    col = pl.BlockSpec((1, DIL_WIDTH), lambda g, i: (0, g))
    tok = pl.BlockSpec((ts, DIL_WIDTH), lambda g, i: (i, g))
    return pl.pallas_call(
        body, name="dil_prep_bwd", grid=(DIL_GROUPS, T // ts),
        in_specs=[pl.BlockSpec(memory_space=pl.ANY), tok, tok, tok,
                  pl.BlockSpec((ts, gw), lambda g, i: (i, CB_QKV * LANE // gw + g)), col, col],
        out_specs=[pl.BlockSpec((ts, gw), lambda g, i: (i, CB_QKV * LANE // gw + g)), col, col],
        out_shape=[jax.ShapeDtypeStruct(dproj.shape, BF16), jax.ShapeDtypeStruct((1, DIL_QK), F32),
                   jax.ShapeDtypeStruct((1, DIL_QK), F32)],
        input_output_aliases={0: 0},
        compiler_params=_cp(),
    )(dproj, ddq, ddk, ddv, proj, gq, gk)


COPY_ROWS = 256


def _to_classes(src_ref, dst_ref, d, L, scale=None):
    n = min(L, COPY_ROWS)
    for r in range(d):
        for c0 in range(0, L, n):
            rows = pl.ds(r + c0 * d, n, stride=d) if d > 1 else pl.ds(c0, n)
            val = src_ref[rows, :]
            if scale is not None:
                val = val * scale
            dst_ref[r * L + c0:r * L + c0 + n, :] = val.astype(dst_ref.dtype)


def _from_classes(src_ref, dst_ref, d, L):
    n = min(L, COPY_ROWS)
    for r in range(d):
        for c0 in range(0, L, n):
            rows = pl.ds(r + c0 * d, n, stride=d) if d > 1 else pl.ds(c0, n)
            dst_ref[rows, :] = src_ref[r * L + c0:r * L + c0 + n, :].astype(dst_ref.dtype)


MLA_TQ, MLA_TK = 512, 512


def _causal_bias(tq, tk, shift):
    row = lax.broadcasted_iota(jnp.int32, (tq, tk), 0)
    col = lax.broadcasted_iota(jnp.int32, (tq, tk), 1)
    return jnp.where(row >= col + shift, 0.0, NEG)


def _mla_specs(S):
    heads = pl.BlockSpec((2, S, LANE), lambda b, j: (j, b, 0))
    pair = pl.BlockSpec((S, LANE), lambda b, j: (b, j))
    return heads, pair


def _mla_attn_fwd(q, k, v, B, S):
    tq = _tile(S, MLA_TQ)
    tk = _tile(tq, MLA_TK)
    nd = tq // tk
    scale = MLA_QK ** -0.5
    heads, pair = _mla_specs(S)

    def body(q_ref, k_ref, v_ref, o_ref, lse_ref):
        lo, lok = _lane_lo((tq, LANE)), _lane_lo((tk, LANE))
        diag = [_causal_bias(tq, tk, i * tk) for i in range(nd)]

        def block(g, _):
            row0 = pl.multiple_of(g * tq, tq)
            rows = pl.ds(row0, tq)
            qs = [q_ref[hh, rows, :] for hh in range(2)]

            one = jnp.ones((), BF16)

            def step(off, carries, bias):
                off = pl.multiple_of(off, tk)
                vt = v_ref[pl.ds(off, tk), :]
                vh = (jnp.where(lok, vt, one), jnp.where(lok, one, vt))
                out = []
                for hh, (m, acc) in enumerate(carries):
                    s = _mm_nt(qs[hh], k_ref[hh, pl.ds(off, tk), :]) * scale
                    if bias is not None:
                        s = s + bias
                    m_new = jnp.maximum(m, jnp.max(s, axis=-1, keepdims=True))
                    p = jnp.exp(s - m_new)
                    out.append((m_new, jnp.exp(m - m_new) * acc + _mm(p, vh[hh])))
                return tuple(out)

            init = (jnp.full((tq, 1), NEG, F32), jnp.zeros((tq, LANE), F32))
            carries = lax.fori_loop(0, g * nd, lambda i, c: step(i * tk, c, None), (init, init))
            for i in range(nd):
                carries = step(row0 + i * tk, carries, diag[i])
            (ma, acca), (mb, accb) = carries
            la, lb = pltpu.roll(acca, HALF, axis=1), pltpu.roll(accb, HALF, axis=1)
            o_ref[rows, :] = jnp.where(lo, acca / la, accb / lb)
            lse_ref[rows, :] = jnp.where(lo, ma + jnp.log(la), mb + jnp.log(lb))
            return 0

        lax.fori_loop(0, S // tq, block, 0)

    return pl.pallas_call(
        body, name="mla_attn_fwd", grid=(B, NPAIR), in_specs=[heads, heads, pair], out_specs=[pair, pair],
        out_shape=[jax.ShapeDtypeStruct((B * S, MLA_HEADS * MLA_V), F32)] * 2,
        compiler_params=_cp(),
    )(q, k, v)


DIL_UNROLL = 8


def _dil_geometry(gi, S):
    span, d = DIL_PATTERNS[gi]
    L = S // d
    t = _tile(L, 128)
    window = span // d
    back = min(-(-window // t) * t, L - t)
    return d, L, t, window, back


def _dil_specs(gi, S):
    qk = pl.BlockSpec((S, LANE), lambda b, j: (b, NPAIR * gi + j))
    v = pl.BlockSpec((S, LANE), lambda b, j: (b, CB_QKV + 3 * (NPAIR * gi + j) + 2))
    pair = pl.BlockSpec((S, LANE), lambda b, j: (b, j))
    return qk, v, pair


def _dil_bias(bias_ref, sl_ref, j, t, kw, back, window):
    row = lax.broadcasted_iota(jnp.int32, (2 * t, kw), 0)
    col = lax.broadcasted_iota(jnp.int32, (2 * t, kw), 1)
    second = row >= t
    slope = jnp.where(second, sl_ref[j, 1], sl_ref[j, 0])
    for n in range(bias_ref.shape[0]):
        dist = jnp.where(second, row - t, row) + n * back - col
        bias_ref[n] = jnp.where((dist >= 0) & (dist <= window), -slope * dist.astype(F32), NEG)


def _stack_heads(x, lo):
    zero = jnp.zeros((), x.dtype)
    return jnp.concatenate([jnp.where(lo, x, zero), jnp.where(lo, zero, x)], axis=0)


def _dil_attn_fwd(gi, slopes, qn, kn, proj, B, S):
    d, L, t, window, back = _dil_geometry(gi, S)
    kw, nq = back + t, L // t
    nbias = 2 if back else 1
    qk, vspec, pair = _dil_specs(gi, S)

    def body(sl_ref, q_ref, k_ref, v_ref, o_ref, lse_ref, qs, ks, vs, os_, ls, bias_ref):
        _to_classes(q_ref, qs, d, L, DIL_HEAD_DIM ** -0.5)
        _to_classes(k_ref, ks, d, L)
        _to_classes(v_ref, vs, d, L)
        _dil_bias(bias_ref, sl_ref, pl.program_id(1), t, kw, back, window)
        lo = _lane_lo((t, LANE))

        def block(g, _):
            qb = g % nq if d > 1 else g
            row0 = pl.multiple_of(g * t, t)
            rows = pl.ds(row0, t)
            early = qb * t < back
            keys = pl.ds(pl.multiple_of(jnp.where(early, row0 - qb * t, row0 - back), t), kw)
            s = _mm_nt(_stack_heads(qs[rows, :], lo), ks[keys, :]) + bias_ref[jnp.where(early, 0, nbias - 1)]
            m = jnp.max(s, axis=-1, keepdims=True)
            p = jnp.exp(s - m)
            l = jnp.sum(p, axis=-1, keepdims=True)
            o2 = _mm(p, vs[keys, :]) / l
            lse2 = m + jnp.log(l)
            os_[rows, :] = jnp.where(lo, o2[:t], o2[t:])
            ls[rows, :] = jnp.where(lo, lse2[:t], lse2[t:])
            return 0

        lax.fori_loop(0, d * nq, block, 0, unroll=DIL_UNROLL if d * nq % DIL_UNROLL == 0 else 1)
        _from_classes(os_, o_ref, d, L)
        _from_classes(ls, lse_ref, d, L)

    return pl.pallas_call(
        body, name=f"dil_attn_fwd_{gi}", grid=(B, NPAIR),
        in_specs=[pl.BlockSpec(memory_space=pltpu.SMEM), qk, qk, vspec], out_specs=[pair, pair],
        out_shape=[jax.ShapeDtypeStruct((B * S, DIL_WIDTH), F32)] * 2,
        scratch_shapes=[pltpu.VMEM((S, LANE), BF16)] * 3 + [pltpu.VMEM((S, LANE), F32)] * 2
                       + [pltpu.VMEM((nbias, 2 * t, kw), F32)],
        compiler_params=_cp(),
    )(slopes, qn, kn, proj)


def _mla_attn_bwd(q, k, v, do, lse, delta, B, S):
    T = B * S
    tq = _tile(S, MLA_TQ)
    tk = _tile(tq, MLA_TK)
    nd = tq // tk
    scale = MLA_QK ** -0.5
    heads, pair = _mla_specs(S)

    def body(q_ref, k_ref, v_ref, do_ref, lse_ref, dl_ref, dq_ref, dk_ref, dv_ref):
        dk_ref[...] = jnp.zeros_like(dk_ref)
        dv_ref[...] = jnp.zeros_like(dv_ref)
        lo = _lane_lo((tq, LANE))
        diag = [_causal_bias(tq, tk, i * tk) for i in range(nd)]

        def block(g, _):
            row0 = pl.multiple_of(g * tq, tq)
            rows = pl.ds(row0, tq)
            for hh in range(2):
                sel = lo if hh == 0 else jnp.logical_not(lo)
                qh = q_ref[hh, rows, :]
                doh = jnp.where(sel, do_ref[rows, :], jnp.zeros((), BF16))
                lse_h = jnp.max(jnp.where(sel, lse_ref[rows, :], NEG), axis=-1, keepdims=True)
                dl_h = jnp.max(jnp.where(sel, dl_ref[rows, :], NEG), axis=-1, keepdims=True)

                def step(off, dq_acc, bias, hh=hh, qh=qh, doh=doh, lse_h=lse_h, dl_h=dl_h):
                    cols = pl.ds(pl.multiple_of(off, tk), tk)
                    kh = k_ref[hh, cols, :]
                    s = _mm_nt(qh, kh) * scale
                    if bias is not None:
                        s = s + bias
                    p = jnp.exp(s - lse_h)
                    dp = _mm_nt(doh, v_ref[cols, :])
                    ds = (p * (dp - dl_h)).astype(BF16)
                    dk_ref[hh, cols, :] += _mm_tn(ds, qh) * scale
                    dv_ref[cols, :] += _mm_tn(p, doh)
                    return dq_acc + _mm(ds, kh)

                dq_acc = lax.fori_loop(0, g * nd, lambda i, a: step(i * tk, a, None), jnp.zeros((tq, LANE), F32))
                for i in range(nd):
                    dq_acc = step(row0 + i * tk, dq_acc, diag[i])
                dq_ref[hh, rows, :] = dq_acc * scale
            return 0

        lax.fori_loop(0, S // tq, block, 0)

    return pl.pallas_call(
        body, name="mla_attn_bwd", grid=(B, NPAIR), in_specs=[heads, heads, pair, pair, pair, pair],
        out_specs=[heads, heads, pair],
        out_shape=[jax.ShapeDtypeStruct((MLA_HEADS, T, LANE), F32), jax.ShapeDtypeStruct((MLA_HEADS, T, LANE), F32),
                   jax.ShapeDtypeStruct((T, MLA_HEADS * MLA_V), F32)],
        compiler_params=_cp(),
    )(q, k, v, do, lse, delta)


def _dil_attn_bwd(gi, slopes, qn, kn, proj, do, lse, delta, through, B, S):
    d, L, t, window, back = _dil_geometry(gi, S)
    kw, nq = back + t, L // t
    nbias = 2 if back else 1
    scale = DIL_HEAD_DIM ** -0.5
    qk, vspec, pair = _dil_specs(gi, S)

    def body(*refs):
        refs = list(refs)
        sl_ref, q_ref, k_ref, v_ref, do_ref, lse_ref, dl_ref = refs[:7]
        dq_ref, dk_ref, dv_ref, qs, ks, vs, dos, lss, dls, dqs, dks, dvs, bias_ref = refs[-13:]
        _to_classes(q_ref, qs, d, L, scale)
        for src, dst in ((k_ref, ks), (v_ref, vs), (do_ref, dos), (lse_ref, lss), (dl_ref, dls)):
            _to_classes(src, dst, d, L)
        _dil_bias(bias_ref, sl_ref, pl.program_id(1), t, kw, back, window)
        dks[...] = jnp.zeros_like(dks)
        dvs[...] = jnp.zeros_like(dvs)
        lo = _lane_lo((t, LANE))

        def stats(ref, rows):
            x = ref[rows, :]
            return jnp.concatenate([jnp.max(jnp.where(lo, x, NEG), axis=-1, keepdims=True),
                                    jnp.max(jnp.where(lo, NEG, x), axis=-1, keepdims=True)], axis=0)

        def block(g, _):
            qb = g % nq if d > 1 else g
            row0 = pl.multiple_of(g * t, t)
            rows = pl.ds(row0, t)
            early = qb * t < back
            keys = pl.ds(pl.multiple_of(jnp.where(early, row0 - qb * t, row0 - back), t), kw)
            q2 = _stack_heads(qs[rows, :], lo)
            do2 = _stack_heads(dos[rows, :], lo)
            kt = ks[keys, :]
            s = _mm_nt(q2, kt) + bias_ref[jnp.where(early, 0, nbias - 1)]
            p = jnp.exp(s - stats(lss, rows))
            ds = (p * (_mm_nt(do2, vs[keys, :]) - stats(dls, rows))).astype(BF16)
            dq2 = _mm(ds, kt) * scale
            dqs[rows, :] = jnp.where(lo, dq2[:t], dq2[t:])
            dks[keys, :] += _mm_tn(ds, q2)
            dvs[keys, :] += _mm_tn(p, do2)
            return 0

        lax.fori_loop(0, d * nq, block, 0, unroll=DIL_UNROLL if d * nq % DIL_UNROLL == 0 else 1)
        for src, dst in ((dqs, dq_ref), (dks, dk_ref), (dvs, dv_ref)):
            _from_classes(src, dst, d, L)

    in_specs = [pl.BlockSpec(memory_space=pltpu.SMEM), qk, qk, vspec, pair, pair, pair]
    args = [slopes, qn, kn, proj, do, lse, delta]
    aliases = {}
    if through is not None:
        aliases = {len(args) + i: i for i in range(3)}
        in_specs = in_specs + [pl.BlockSpec(memory_space=pl.ANY)] * 3
        args = args + list(through)
    return pl.pallas_call(
        body, name=f"dil_attn_bwd_{gi}", grid=(B, NPAIR), in_specs=in_specs, out_specs=[qk, qk, qk],
        out_shape=[jax.ShapeDtypeStruct((B * S, DIL_QK), F32)] * 3,
        scratch_shapes=[pltpu.VMEM((S, LANE), BF16)] * 4 + [pltpu.VMEM((S, LANE), F32)] * 5
                       + [pltpu.VMEM((nbias, 2 * t, kw), F32)],
        input_output_aliases=aliases,
        compiler_params=_cp(),
    )(*args)


def _merge_common(p_ref, bg_ref, ob_ref, og_refs, lse_refs):
    bz = p_ref[:, CB_BZ * LANE:(CB_BZ + 4) * LANE]
    cz = p_ref[:, CB_CZ * LANE:(CB_CZ + 4) * LANE]
    gates = [_sigmoid(p_ref[:, (CB_GATE + 8 * i) * LANE:(CB_GATE + 8 * i + 8) * LANE]
                      + bg_ref[:, i * D_MODEL:(i + 1) * D_MODEL]) for i in range(3)]
    ob = ob_ref[...]
    lses = [r[...] for r in lse_refs]
    mx = jnp.maximum(jnp.maximum(lses[0], lses[1]), lses[2])
    es = [jnp.exp(v - mx) for v in lses]
    inv = 1.0 / (es[0] + es[1] + es[2])
    alphas = [e * inv for e in es]
    oc = alphas[0] * og_refs[0][...] + alphas[1] * og_refs[1][...] + alphas[2] * og_refs[2][...]
    return bz, cz, gates, ob, alphas, oc


def _merge_fwd(x, proj, b_gate, ya, ob, ogs, lses, woa, wob, woc, wo):
    T = x.shape[0]
    ts = _tile(T, 256)
    MW = 32 * LANE

    def body(x_ref, p_ref, bg_ref, ya_ref, ob_ref, og0, og1, og2, l0, l1, l2, woa_ref, wob_ref, woc_ref, wo_ref, out_ref):
        bz, cz, gates, obv, alphas, oc = _merge_common(p_ref, bg_ref, ob_ref, (og0, og1, og2), (l0, l1, l2))
        yb = obv * _silu(bz)
        yc = oc * _silu(cz)
        merged = (gates[0] * _mm(ya_ref[...], woa_ref[...]) + gates[1] * _mm(yb, wob_ref[...])
                  + gates[2] * _mm(yc, woc_ref[...]))
        out_ref[...] = x_ref[...] + _mm(merged, wo_ref[...])

    def whole(r, c):
        return pl.BlockSpec((r, c), lambda i: (0, 0))

    tok = lambda w: pl.BlockSpec((ts, w), lambda i: (i, 0))
    return pl.pallas_call(
        body, name="merge_fwd", grid=(T // ts,),
        in_specs=[tok(D_MODEL), tok(MW), whole(1, 3 * D_MODEL), tok(CONV_WIDTH)] + [tok(DIL_WIDTH)] * 7
                 + [whole(CONV_WIDTH, D_MODEL)] * 3 + [whole(D_MODEL, D_MODEL)],
        out_specs=tok(D_MODEL),
        out_shape=jax.ShapeDtypeStruct((T, D_MODEL), F32),
        compiler_params=_cp(),
    )(x, proj, b_gate, ya, ob, *ogs, *lses, woa, wob, woc, wo)


def _merge_bwd(dout, proj, b_gate, ya, ob, ogs, lses, woa, wob, woc, wo):
    T = dout.shape[0]
    ts = _tile(T, 256)
    MW = 32 * LANE

    def body(do_ref, p_ref, bg_ref, ya_ref, ob_ref, og0, og1, og2, l0, l1, l2, woa_ref, wob_ref, woc_ref, wo_ref,
             dp_ref, dya_ref, dob_ref, dlb_ref, dg0, dg1, dg2, dl0, dl1, dl2,
             mg_ref, dpa_ref, dpb_ref, dpc_ref, yb_ref, yc_ref, dbg_ref):
        bz, cz, gates, obv, alphas, oc = _merge_common(p_ref, bg_ref, ob_ref, (og0, og1, og2), (l0, l1, l2))
        sb, sc = _silu(bz), _silu(cz)
        yb = obv * sb
        yc = oc * sc
        ps = [_mm(ya_ref[...], woa_ref[...]), _mm(yb, wob_ref[...]), _mm(yc, woc_ref[...])]
        mg_ref[...] = (gates[0] * ps[0] + gates[1] * ps[1] + gates[2] * ps[2]).astype(BF16)
        yb_ref[...] = yb.astype(BF16)
        yc_ref[...] = yc.astype(BF16)
        dm = _mm_nt(do_ref[...], wo_ref[...])
        dps = []
        first = pl.program_id(0) == 0
        for i, dref in enumerate((dpa_ref, dpb_ref, dpc_ref)):
            g = gates[i]
            dpi = (dm * g).astype(BF16)
            dref[...] = dpi
            dps.append(dpi)
            dgp = dm * ps[i] * g * (1.0 - g)
            dp_ref[:, (CB_GATE + 8 * i) * LANE:(CB_GATE + 8 * i + 8) * LANE] = dgp.astype(BF16)
            part = jnp.sum(dgp, axis=0, keepdims=True)

            @pl.when(first)
            def _():
                dbg_ref[:, i * D_MODEL:(i + 1) * D_MODEL] = part

            @pl.when(jnp.logical_not(first))
            def _():
                dbg_ref[:, i * D_MODEL:(i + 1) * D_MODEL] += part

        dya_ref[...] = _mm_nt(dps[0], woa_ref[...])
        dyb = _mm_nt(dps[1], wob_ref[...])
        dyc = _mm_nt(dps[2], woc_ref[...])
        dp_ref[:, CB_BZ * LANE:(CB_BZ + 4) * LANE] = (dyb * obv * _dsilu(bz)).astype(BF16)
        dp_ref[:, CB_CZ * LANE:(CB_CZ + 4) * LANE] = (dyc * oc * _dsilu(cz)).astype(BF16)
        dob = dyb * sb
        doc = dyc * sc
        dob_ref[...] = dob.astype(BF16)
        for c in range(NPAIR):
            cs = slice(c * LANE, (c + 1) * LANE)
            dlb_ref[:, cs] = _head_bcast_sum(dob[:, cs] * obv[:, cs])
            dd = _head_bcast_sum(doc[:, cs] * oc[:, cs])
            for a, dref, lref in zip(alphas, (dg0, dg1, dg2), (dl0, dl1, dl2)):
                dref[:, cs] = a[:, cs] * doc[:, cs]
                lref[:, cs] = a[:, cs] * dd

    def whole(r, c):
        return pl.BlockSpec((r, c), lambda i: (0, 0))

    tok = lambda w: pl.BlockSpec((ts, w), lambda i: (i, 0))
    sd = jax.ShapeDtypeStruct
    W = DIL_WIDTH
    return pl.pallas_call(
        body, name="merge_bwd", grid=(T // ts,),
        in_specs=[tok(D_MODEL), tok(MW), whole(1, 3 * D_MODEL), tok(CONV_WIDTH)] + [tok(W)] * 7
                 + [whole(CONV_WIDTH, D_MODEL)] * 3 + [whole(D_MODEL, D_MODEL)],
        out_specs=[tok(MW), tok(CONV_WIDTH), tok(W), tok(W)] + [tok(W)] * 6
                  + [tok(D_MODEL)] * 4 + [tok(W), tok(W), whole(1, 3 * D_MODEL)],
        out_shape=[sd((T, PP), BF16), sd((T, CONV_WIDTH), F32), sd((T, W), BF16), sd((T, W), F32)]
                  + [sd((T, W), F32)] * 6
                  + [sd((T, D_MODEL), BF16)] * 4 + [sd((T, W), BF16)] * 2 + [sd((1, 3 * D_MODEL), F32)],
        compiler_params=_cp(),
    )(dout, proj, b_gate, ya, ob, *ogs, *lses, woa, wob, woc, wo)


def _loss_head(y, target):
    T = y.shape[0]
    ts = _tile(T, 512)

    def body(y_ref, t_ref, d_ref, l_ref):
        e = y_ref[...] - t_ref[...]
        d_ref[...] = e * (1.0 / D_MODEL)
        l_ref[...] = jnp.zeros((1, 8, LANE), F32) + jnp.sum(e * e)

    tok = pl.BlockSpec((ts, D_MODEL), lambda i: (i, 0))
    return pl.pallas_call(
        body, name="loss_head", grid=(T // ts,), in_specs=[tok, tok],
        out_specs=[tok, pl.BlockSpec((1, 8, LANE), lambda i: (i, 0, 0))],
        out_shape=[jax.ShapeDtypeStruct((T, D_MODEL), F32), jax.ShapeDtypeStruct((T // ts, 8, LANE), F32)],
        compiler_params=_cp(),
    )(y, target)


def _my_index():
    return 4 * lax.axis_index("x") + 2 * lax.axis_index("y") + lax.axis_index("c")


def _peers():
    x, y, c = (lax.axis_index(a) for a in AXES)
    out = []
    for kk in range(1, N_DEV):
        px = 1 - x if kk & 4 else x
        py = 1 - y if kk & 2 else y
        pc = 1 - c if kk & 1 else c
        out.append(((px, py, pc), 4 * px + 2 * py + pc))
    return out


def _exchange(arrays, name, gather):
    n = len(arrays)

    def body(*refs):
        srcs, outs = refs[:n], refs[n:2 * n]
        send_sems, recv_sems, local_sems = refs[2 * n:]
        me = _my_index()
        peers = _peers()
        started = []
        for a, (src, out) in enumerate(zip(srcs, outs)):
            mine = pltpu.make_async_copy(src if gather else src.at[me], out.at[me], local_sems.at[a])
            mine.start()
            started.append(mine)
        sends = []
        for i, (pos, idx) in enumerate(peers):
            for a, (src, out) in enumerate(zip(srcs, outs)):
                cp = pltpu.make_async_remote_copy(
                    src_ref=src if gather else src.at[idx], dst_ref=out.at[me], send_sem=send_sems.at[a, i],
                    recv_sem=recv_sems.at[a, i], device_id=pos, device_id_type=pl.DeviceIdType.MESH)
                cp.start()
                sends.append(cp)
        for i, (pos, idx) in enumerate(peers):
            for a, (src, out) in enumerate(zip(srcs, outs)):
                pltpu.make_async_remote_copy(
                    src_ref=src if gather else src.at[idx], dst_ref=out.at[idx], send_sem=send_sems.at[a, i],
                    recv_sem=recv_sems.at[a, i], device_id=pos, device_id_type=pl.DeviceIdType.MESH).wait_recv()
        for cp in sends:
            cp.wait_send()
        for mine in started:
            mine.wait()

    any_space = pl.BlockSpec(memory_space=pl.ANY)
    return pl.pallas_call(
        body, name=name, in_specs=[any_space] * n, out_specs=[any_space] * n,
        out_shape=[jax.ShapeDtypeStruct(((N_DEV,) + a.shape) if gather else a.shape, a.dtype) for a in arrays],
        scratch_shapes=[pltpu.SemaphoreType.DMA((n, N_DEV - 1)), pltpu.SemaphoreType.DMA((n, N_DEV - 1)),
                        pltpu.SemaphoreType.DMA((n,))],
    )(*arrays)


N_CHIP = 4


def _chip_places():
    x, y, c = (lax.axis_index(a) for a in AXES)
    return (x, y, c), (x, y, 1 - c), [(1 - x, y, c), (x, 1 - y, c), (1 - x, 1 - y, c)]


def _index_of(pos):
    return 4 * pos[0] + 2 * pos[1] + pos[2]


def _gather_two_level(arrays, name):
    n = len(arrays)

    def body(*refs):
        srcs, outs = refs[:n], refs[n:2 * n]
        send_sems, recv_sems, local_sems = refs[2 * n:]
        me, sibling, others = _chip_places()

        def copy(a, k, block, to, src=None):
            slot = outs[a].at[_index_of(block)]
            return pltpu.make_async_remote_copy(
                src_ref=slot if src is None else src, dst_ref=slot, send_sem=send_sems.at[7 * a + k],
                recv_sem=recv_sems.at[7 * a + k], device_id=to, device_id_type=pl.DeviceIdType.MESH)

        started = []
        for a, src in enumerate(srcs):
            mine = pltpu.make_async_copy(src, outs[a].at[_index_of(me)], local_sems.at[a])
            mine.start()
            started.append(mine)
        sends = []
        for a, src in enumerate(srcs):
            sends.append(copy(a, 0, me, sibling, src))
            sends += [copy(a, 1 + j, me, chip, src) for j, chip in enumerate(others)]
        for cp in sends:
            cp.start()
        for j, chip in enumerate(others):
            for a in range(n):
                copy(a, 1 + j, chip, me).wait_recv()
                fwd = copy(a, 4 + j, chip, sibling)
                fwd.start()
                sends.append(fwd)
        for a in range(n):
            copy(a, 0, sibling, me).wait_recv()
            for j, chip in enumerate(others):
                copy(a, 4 + j, (chip[0], chip[1], sibling[2]), me).wait_recv()
        for cp in sends:
            cp.wait_send()
        for mine in started:
            mine.wait()

    any_space = pl.BlockSpec(memory_space=pl.ANY)
    return pl.pallas_call(
        body, name=name, in_specs=[any_space] * n, out_specs=[any_space] * n,
        out_shape=[jax.ShapeDtypeStruct((N_DEV,) + a.shape, a.dtype) for a in arrays],
        scratch_shapes=[pltpu.SemaphoreType.DMA((7 * n,)), pltpu.SemaphoreType.DMA((7 * n,)),
                        pltpu.SemaphoreType.DMA((n,))],
    )(*arrays)


def _sibling_swap(arrays, name):
    n = len(arrays)

    def body(*refs):
        srcs, outs = refs[:n], refs[n:2 * n]
        send_sems, recv_sems = refs[2 * n:]
        (x, y, c), sibling, _ = _chip_places()
        sends = []
        for a, (src, out) in enumerate(zip(srcs, outs)):
            for q in range(N_CHIP):
                def copy(core, a=a, q=q, src=src, out=out):
                    return pltpu.make_async_remote_copy(
                        src_ref=src.at[2 * q + core], dst_ref=out.at[q], send_sem=send_sems.at[N_CHIP * a + q],
                        recv_sem=recv_sems.at[N_CHIP * a + q], device_id=sibling, device_id_type=pl.DeviceIdType.MESH)
                mine = copy(1 - c)
                mine.start()
                sends.append((mine, copy(c)))
        for mine, arrival in sends:
            arrival.wait_recv()
            mine.wait_send()

    any_space = pl.BlockSpec(memory_space=pl.ANY)
    return pl.pallas_call(
        body, name=name, in_specs=[any_space] * n, out_specs=[any_space] * n,
        out_shape=[jax.ShapeDtypeStruct((N_CHIP,) + a.shape[1:], a.dtype) for a in arrays],
        scratch_shapes=[pltpu.SemaphoreType.DMA((N_CHIP * n,)), pltpu.SemaphoreType.DMA((N_CHIP * n,))],
    )(*arrays)


def _chip_pair_sum(part, got, name):
    R, C = part.shape[1:]
    tr = R
    while tr * C * part.dtype.itemsize > REDUCE_BLOCK_BYTES // 4 and tr % 32 == 0:
        tr //= 2
    c = lax.axis_index("c")

    def body(c_ref, p_ref, g_ref, o_ref):
        del c_ref
        o_ref[...] = (p_ref[...].astype(F32) + g_ref[...].astype(F32)).astype(o_ref.dtype)

    return pl.pallas_call(
        body, name=name, grid_spec=pltpu.PrefetchScalarGridSpec(
            num_scalar_prefetch=1, grid=(N_CHIP, R // tr),
            in_specs=[pl.BlockSpec((None, tr, C), lambda q, i, cr: (2 * q + cr[0], i, 0)),
                      pl.BlockSpec((None, tr, C), lambda q, i, cr: (q, i, 0))],
            out_specs=pl.BlockSpec((None, tr, C), lambda q, i, cr: (q, i, 0))),
        out_shape=jax.ShapeDtypeStruct((N_CHIP, R, C), part.dtype),
        compiler_params=_cp(),
    )(jnp.reshape(c, (1,)).astype(jnp.int32), part, got)


def _chip_exchange(arrays, name):
    n = len(arrays)

    def body(*refs):
        srcs, outs = refs[:n], refs[n:2 * n]
        send_sems, recv_sems, local_sems = refs[2 * n:]
        (x, y, c), _, others = _chip_places()
        mychip = 2 * x + y
        started, sends = [], []
        for a, (src, out) in enumerate(zip(srcs, outs)):
            mine = pltpu.make_async_copy(src.at[mychip], out.at[mychip], local_sems.at[a])
            mine.start()
            started.append(mine)
        for j, chip in enumerate(others):
            q = 2 * chip[0] + chip[1]
            for a, (src, out) in enumerate(zip(srcs, outs)):
                def copy(slot, a=a, j=j, q=q, chip=chip, src=src, out=out):
                    return pltpu.make_async_remote_copy(
                        src_ref=src.at[q], dst_ref=out.at[slot], send_sem=send_sems.at[3 * a + j],
                        recv_sem=recv_sems.at[3 * a + j], device_id=chip, device_id_type=pl.DeviceIdType.MESH)
                mine = copy(mychip)
                mine.start()
                sends.append((mine, copy(q)))
        for mine, arrival in sends:
            arrival.wait_recv()
        for mine, arrival in sends:
            mine.wait_send()
        for mine in started:
            mine.wait()

    any_space = pl.BlockSpec(memory_space=pl.ANY)
    return pl.pallas_call(
        body, name=name, in_specs=[any_space] * n, out_specs=[any_space] * n,
        out_shape=[jax.ShapeDtypeStruct(a.shape, a.dtype) for a in arrays],
        scratch_shapes=[pltpu.SemaphoreType.DMA((3 * n,)), pltpu.SemaphoreType.DMA((3 * n,)),
                        pltpu.SemaphoreType.DMA((n,))],
    )(*arrays)


def _remote_copies(srcs, lands, send_sems, recv_sems, gather):
    me = _my_index()
    out = []
    for i, (pos, idx) in enumerate(_peers()):
        for a, (src, land) in enumerate(zip(srcs, lands)):
            def copy(slot, a=a, src=src, land=land, i=i, pos=pos, idx=idx):
                return pltpu.make_async_remote_copy(
                    src_ref=src if gather else src.at[idx], dst_ref=land.at[slot],
                    send_sem=send_sems.at[a * (N_DEV - 1) + i], recv_sem=recv_sems.at[a * (N_DEV - 1) + i],
                    device_id=pos, device_id_type=pl.DeviceIdType.MESH)
            out.append((copy(me), copy(idx)))
    return out


def _exchange_start(arrays, name, gather):
    n = len(arrays)
    hbm = pl.BlockSpec(memory_space=pltpu.HBM)
    sem = pl.BlockSpec(memory_space=pltpu.SEMAPHORE)
    lands = [lax.empty(((N_DEV,) + a.shape) if gather else a.shape, a.dtype) for a in arrays]

    def body(*refs):
        srcs, lands_ = refs[:n], refs[n:2 * n]
        send_sems, recv_sems = refs[2 * n:2 * n + 2]
        for mine, _ in _remote_copies(srcs, lands_, send_sems, recv_sems, gather):
            mine.start()
        refs[-1][...] = jnp.zeros_like(refs[-1])

    sems = pltpu.SemaphoreType.DMA((n * (N_DEV - 1),))
    buffers = [pltpu.HBM(a.shape, a.dtype) for a in list(arrays) + lands]
    res = pl.pallas_call(
        body, name=name, in_specs=[hbm] * (2 * n), out_specs=[sem, sem] + [hbm] * (2 * n) + [pl.BlockSpec(memory_space=pltpu.VMEM)],
        out_shape=[sems, sems] + buffers + [jax.ShapeDtypeStruct((8, LANE), F32)],
        input_output_aliases={i: 2 + i for i in range(2 * n)},
        compiler_params=pltpu.CompilerParams(has_side_effects=pltpu.SideEffectType.DATAFLOW_SIDE_EFFECTING),
    )(*[pltpu.with_memory_space_constraint(a, pltpu.HBM) for a in list(arrays) + lands])
    return (res[0], res[1], res[2:2 + n], res[2 + n:2 + 2 * n]), res[-1]


def _exchange_wait(handle, after, name, gather):
    send_sems, recv_sems, srcs, lands = handle
    n = len(srcs)
    hbm = pl.BlockSpec(memory_space=pltpu.HBM)
    sem = pl.BlockSpec(memory_space=pltpu.SEMAPHORE)

    def body(*refs):
        for mine, arrival in _remote_copies(refs[:n], refs[n:2 * n], refs[2 * n], refs[2 * n + 1], gather):
            mine.wait_send()
            arrival.wait_recv()

    res = pl.pallas_call(
        body, name=name, in_specs=[hbm] * (2 * n) + [sem, sem, pl.BlockSpec(memory_space=pl.ANY)],
        out_specs=[hbm] * (2 * n), out_shape=[pltpu.HBM(a.shape, a.dtype) for a in list(srcs) + list(lands)],
        input_output_aliases={i: i for i in range(2 * n)},
        compiler_params=pltpu.CompilerParams(has_side_effects=pltpu.SideEffectType.DATAFLOW_SIDE_EFFECTING),
    )(*srcs, *lands, send_sems, recv_sems, after)
    return res[n:]


def _own_slot(land, mine):
    return lax.dynamic_update_slice(land, mine, (_my_index(),) + (0,) * (land.ndim - 1))


def _adamw(w, g, m, v):
    m = ADAM_B1 * m + (1.0 - ADAM_B1) * g
    v = ADAM_B2 * v + (1.0 - ADAM_B2) * (g * g)
    m_hat = m / (1.0 - ADAM_B1 ** ADAM_STEP)
    v_hat = v / (1.0 - ADAM_B2 ** ADAM_STEP)
    delta = -ADAM_LR * (m_hat / (jnp.sqrt(v_hat) + ADAM_EPS) + ADAM_WD * w)
    return delta, m, v


def _reduce_adamw(parts, w, m, v, name):
    nparts = len(parts)
    R, C = parts[0].shape[1:]
    tr = R
    while N_DEV * tr * C * parts[0].dtype.itemsize > REDUCE_BLOCK_BYTES and tr % 32 == 0:
        tr //= 2
    steps = R // tr

    def body(*refs):
        w_ref, m_ref, v_ref, g_ref, d_ref, nm_ref, nv_ref = refs[nparts:]
        for k, p_ref in enumerate(refs[:nparts]):
            @pl.when(pl.program_id(0) // steps == k)
            def _():
                g = p_ref[0].astype(F32)
                for s in range(1, p_ref.shape[0]):
                    g = g + p_ref[s].astype(F32)
                g_ref[...] = g
                d_ref[...], nm_ref[...], nv_ref[...] = _adamw(w_ref[...], g, m_ref[...], v_ref[...])

    def part_spec(k):
        return pl.BlockSpec((parts[k].shape[0], tr, C), lambda i: (0, jnp.clip(i - k * steps, 0, steps - 1), 0))

    row = pl.BlockSpec((tr, C), lambda i: (i, 0))
    return pl.pallas_call(
        body, name=name, grid=(nparts * steps,),
        in_specs=[part_spec(k) for k in range(nparts)] + [row, row, row],
        out_specs=[row] * 4, out_shape=[jax.ShapeDtypeStruct((nparts * R, C), F32)] * 4,
        compiler_params=_cp(),
    )(*parts, w, m, v)


BIG = ("w_in", "w_uq", "w_ukv", "w_out_a", "w_out_b", "w_out_c", "w_o")
SMALL = ("norm_g", "b_gate", "conv_w", "conv_b", "q_a_norm_g", "kv_a_norm_g", "mla_q_norm_g", "mla_k_norm_g",
         "dil_q_norm_g", "dil_k_norm_g")
PACK_ROWS = 128
REDUCE_BLOCK_BYTES = 6 * 1024 * 1024


def _pack_local(tensors):
    flat = jnp.concatenate([t.reshape(-1) for t in tensors])
    pad = (-flat.shape[0]) % (PACK_ROWS * LANE)
    return jnp.concatenate([flat, jnp.zeros((pad,), flat.dtype)]).reshape(-1, LANE)


def _unpack_local(rows, like):
    flat = rows.reshape(-1)
    out, off = [], 0
    for t in like:
        out.append(flat[off:off + t.size].reshape(t.shape))
        off += t.size
    return out


def _cols_to_slots(a):
    k = a.shape[0]
    return a.reshape(k, N_DEV, -1).transpose(1, 0, 2)


def _slots_to_cols(s):
    return s.transpose(1, 0, 2).reshape(s.shape[1], -1)


def _rope_tables(S):
    inv = ROPE_THETA ** (-jnp.arange(0, MLA_ROPE, 2, dtype=F32) / MLA_ROPE)
    ang = jnp.arange(S, dtype=F32)[:, None] * inv[None, :]
    cos, sin = jnp.cos(ang), jnp.sin(ang)
    one = jnp.ones((S, MLA_NOPE), F32)
    z16, z32, z64 = (jnp.zeros((S, n), F32) for n in (16, 32, 64))
    cosp = jnp.concatenate([one, cos, cos, jnp.ones((S, 32), F32)], axis=1)
    sa = jnp.concatenate([z64, -sin, z16, z32], axis=1)
    sb = jnp.concatenate([z64, z16, sin, z32], axis=1)
    return cosp, sa, sb


def _alibi_slopes():
    n = DIL_GROUPS * DIL_HEADS
    m = 2.0 ** (-8.0 * jnp.arange(1, n + 1, dtype=F32) / n)
    return m.reshape(DIL_GROUPS, NPAIR, 2)


def _pad_slots(s):
    n, k, c = s.shape
    return _slots_to_cols(jnp.concatenate([s, jnp.zeros((n, k, LANE - c), s.dtype)], axis=2))


def _layer_params(gw, small, l):
    p = {}
    p["wp"] = _pad_columns(gw["w_in"])
    p["norm_g"] = small["norm_g"][l][None]
    p["b_gate"] = small["b_gate"][l][None]
    p["conv_w"] = gw["conv_w"].transpose(1, 0, 2).reshape(CONV_K, CONV_WIDTH)
    p["conv_b"] = small["conv_b"][l][None]
    p["gq"] = small["q_a_norm_g"][l][None]
    p["gkv"] = small["kv_a_norm_g"][l][None]
    p["wuqp"] = _pad_slots(gw["w_uq"])
    kv = gw["w_ukv"]
    p["wkp"] = _pad_slots(kv[:, :, :MLA_NOPE])
    p["wv"] = kv[:, :, MLA_NOPE:].transpose(1, 0, 2).reshape(MLA_KV_LORA, MLA_HEADS * MLA_V)
    zpad = jnp.zeros((1, LANE - MLA_QK), F32)
    p["gmq"] = jnp.concatenate([small["mla_q_norm_g"][l][None], zpad], axis=1)
    p["gmk"] = jnp.concatenate([small["mla_k_norm_g"][l][None], zpad], axis=1)
    tile = lambda g: jnp.broadcast_to(g[:, None, :], (DIL_GROUPS, DIL_HEADS, DIL_HEAD_DIM)).reshape(1, DIL_QK)
    p["gdq"] = tile(small["dil_q_norm_g"][l])
    p["gdk"] = tile(small["dil_k_norm_g"][l])
    p["woa"], p["wob"], p["woc"] = (_slots_to_cols(gw[n]) for n in ("w_out_a", "w_out_b", "w_out_c"))
    p["wo"] = gw["w_o"].reshape(D_MODEL, D_MODEL)
    return p


def _layer_fwd(x, p, tabs, slopes, B, S):
    proj, ht = _inproj_fwd(x, p["norm_g"], p["wp"])
    ya = _mixa_fwd(proj, p["conv_w"], p["conv_b"], B, S)
    q, k, v = _mla_prep_fwd(proj, p["gq"], p["gkv"], p["wuqp"], p["wkp"], p["wv"], p["gmq"], p["gmk"], *tabs, S)
    ob, lse_b = _mla_attn_fwd(q, k, v, B, S)
    qn, kn = _dil_prep_fwd(proj, p["gdq"], p["gdk"])
    ogs, lses = [], []
    for gi in range(DIL_GROUPS):
        o, lse = _dil_attn_fwd(gi, slopes[gi], qn, kn, proj, B, S)
        ogs.append(o)
        lses.append(lse)
    out = _merge_fwd(x, proj, p["b_gate"], ya, ob, ogs, lses, p["woa"], p["wob"], p["woc"], p["wo"])
    saved = dict(x=x, proj=proj, ht=ht, ya=ya, q=q, k=k, v=v, ob=ob, lse_b=lse_b, qn=qn, kn=kn, ogs=ogs, lses=lses)
    return out, saved


def _layer_bwd(dout, sv, p, tabs, slopes, B, S):
    proj = sv["proj"]
    (dproj, dya, dob, dlb, dg0, dg1, dg2, dl0, dl1, dl2, merged, dpa, dpb, dpc, yb, yc, dbg) = _merge_bwd(
        dout, proj, p["b_gate"], sv["ya"], sv["ob"], sv["ogs"], sv["lses"], p["woa"], p["wob"], p["woc"], p["wo"])
    g = {}
    g["w_o"] = _matmul_tn(merged, dout, "dw_o").reshape(N_DEV, D_MODEL // N_DEV, D_MODEL)
    g["w_out_a"] = _cols_to_slots(_matmul_tn(sv["ya"], dpa, "dw_out_a"))
    g["w_out_b"] = _cols_to_slots(_matmul_tn(yb, dpb, "dw_out_b"))
    g["w_out_c"] = _cols_to_slots(_matmul_tn(yc, dpc, "dw_out_c"))
    g["b_gate"] = dbg[0]
    dproj, st = _mixa_bwd(dproj, dya, proj, p["conv_w"], p["conv_b"], B, S)
    g["conv_w"] = st[0:CONV_K]
    g["conv_b"] = st[CONV_K]
    dq, dk, dv = _mla_attn_bwd(sv["q"], sv["k"], sv["v"], dob, sv["lse_b"], dlb, B, S)
    dproj, dwuqp, dwkp, dwv, dgq, dgkv, dgmq, dgmk = _mla_prep_bwd(
        dproj, dq, dk, dv, proj, p["gq"], p["gkv"], p["wuqp"], p["wkp"], p["wv"], p["gmq"], p["gmk"], *tabs, S)
    g["w_uq"] = _cols_to_slots(dwuqp)[:, :, :MLA_QK]
    g["w_ukv"] = jnp.concatenate([_cols_to_slots(dwkp)[:, :, :MLA_NOPE], _cols_to_slots(dwv)], axis=2)
    g["q_a_norm_g"], g["kv_a_norm_g"] = dgq[0], dgkv[0]
    g["mla_q_norm_g"], g["mla_k_norm_g"] = dgmq[0, :MLA_QK], dgmk[0, :MLA_QK]
    dqkv = None
    for gi, (dog, dlg) in enumerate(((dg0, dl0), (dg1, dl1), (dg2, dl2))):
        dqkv = _dil_attn_bwd(gi, slopes[gi], sv["qn"], sv["kn"], proj, dog, sv["lses"][gi], dlg, dqkv, B, S)
    dproj, dgdq, dgdk = _dil_prep_bwd(dproj, *dqkv, proj, p["gdq"], p["gdk"])
    g["dil_q_norm_g"] = dgdq.reshape(DIL_GROUPS, DIL_HEADS, DIL_HEAD_DIM).sum(axis=1)
    g["dil_k_norm_g"] = dgdk.reshape(DIL_GROUPS, DIL_HEADS, DIL_HEAD_DIM).sum(axis=1)
    g["w_in"] = _unpad_columns(_matmul_nn(sv["ht"], dproj, "dw_in"))
    dx, dng = _inproj_bwd_x(dproj, p["wp"], sv["x"], p["norm_g"], dout)
    g["norm_g"] = dng[0]
    return dx, g


def _after(token, a):
    return a if token is None else a + token[0:1, 0:1]


def _local_step(x, target, small, B, S, weights_of, grads_out):
    tabs = _rope_tables(S)
    sl = _alibi_slopes()
    slopes = [sl[gi] * float(DIL_PATTERNS[gi][1]) for gi in range(DIL_GROUPS)]
    params, saved = [], []
    for l in range(DEPTH):
        gw, token = weights_of(l, x)
        p = _layer_params(gw, small, l)
        p["norm_g"] = _after(token, p["norm_g"])
        x, sv = _layer_fwd(x, p, tabs, slopes, B, S)
        params.append(p)
        saved.append(sv)
    dout, lparts = _loss_head(x, target)
    sq = jnp.sum(lparts[:, 0, 0])
    token = None
    for l in reversed(range(DEPTH)):
        p = dict(params[l], b_gate=_after(token, params[l]["b_gate"]))
        dout, g = _layer_bwd(dout, saved[l], p, tabs, slopes, B, S)
        token = grads_out(l, g, dout)
    return sq, dout


def kernel(x, norm_g, w_in, b_gate, conv_w, conv_b, q_a_norm_g, w_uq, kv_a_norm_g, w_ukv, mla_q_norm_g, mla_k_norm_g, dil_q_norm_g, dil_k_norm_g, w_out_a, w_out_b, w_out_c, w_o, loss_target, m_norm_g, m_w_in, m_b_gate, m_conv_w, m_conv_b, m_q_a_norm_g, m_w_uq, m_kv_a_norm_g, m_w_ukv, m_mla_q_norm_g, m_mla_k_norm_g, m_dil_q_norm_g, m_dil_k_norm_g, m_w_out_a, m_w_out_b, m_w_out_c, m_w_o, v_norm_g, v_w_in, v_b_gate, v_conv_w, v_conv_b, v_q_a_norm_g, v_w_uq, v_kv_a_norm_g, v_w_ukv, v_mla_q_norm_g, v_mla_k_norm_g, v_dil_q_norm_g, v_dil_k_norm_g, v_w_out_a, v_w_out_b, v_w_out_c, v_w_o):
    names = ("norm_g", "w_in", "b_gate", "conv_w", "conv_b", "q_a_norm_g", "w_uq", "kv_a_norm_g", "w_ukv",
             "mla_q_norm_g", "mla_k_norm_g", "dil_q_norm_g", "dil_k_norm_g", "w_out_a", "w_out_b", "w_out_c", "w_o")
    w = dict(zip(names, (norm_g, w_in, b_gate, conv_w, conv_b, q_a_norm_g, w_uq, kv_a_norm_g, w_ukv, mla_q_norm_g,
                         mla_k_norm_g, dil_q_norm_g, dil_k_norm_g, w_out_a, w_out_b, w_out_c, w_o)))
    m = dict(zip(names, (m_norm_g, m_w_in, m_b_gate, m_conv_w, m_conv_b, m_q_a_norm_g, m_w_uq, m_kv_a_norm_g, m_w_ukv,
                         m_mla_q_norm_g, m_mla_k_norm_g, m_dil_q_norm_g, m_dil_k_norm_g, m_w_out_a, m_w_out_b,
                         m_w_out_c, m_w_o)))
    v = dict(zip(names, (v_norm_g, v_w_in, v_b_gate, v_conv_w, v_conv_b, v_q_a_norm_g, v_w_uq, v_kv_a_norm_g, v_w_ukv,
                         v_mla_q_norm_g, v_mla_k_norm_g, v_dil_q_norm_g, v_dil_k_norm_g, v_w_out_a, v_w_out_b,
                         v_w_out_c, v_w_o)))
    B, S, _ = x.shape
    me = _my_index()
    cshard = CONV_WIDTH // N_DEV

    shards = [[w[n][l].astype(BF16) for n in BIG] for l in range(DEPTH)]
    state = {}

    def weights_of(l, after):
        if l == 0:
            got = _gather_two_level(shards[0] + [conv_w], "all_gather_weights_0")
            state["gather"], token = _exchange_start(shards[1], "all_gather_weights_1_start", gather=True)
            state["conv_w"] = got[-1]
        else:
            landed = _exchange_wait(state["gather"], after, "all_gather_weights_1_wait", gather=True)
            got, token = [_own_slot(a, s[None]) for a, s in zip(landed, shards[1])], None
        gw = dict(zip(BIG, got))
        gw["conv_w"] = state["conv_w"][:, l]
        return gw, token

    recv, small_parts = {}, {}

    def grads_out(l, g, after):
        small_parts[l] = [g[n] for n in SMALL]
        send = [g[n].astype(BF16) for n in BIG]
        if l == DEPTH - 1:
            state["scatter"], token = _exchange_start(send, "exchange_weight_grads_1_start", gather=False)
            state["sent"] = send
            return token
        landed = _exchange_wait(state["scatter"], after, "exchange_weight_grads_1_wait", gather=False)
        mine = [lax.dynamic_slice_in_dim(s, me, 1, axis=0) for s in state["sent"]]
        recv[DEPTH - 1] = [_own_slot(a, s) for a, s in zip(landed, mine)]
        swapped = _sibling_swap(send, "exchange_weight_grads_0_sibling")
        sums = [_chip_pair_sum(s, t, "chip_pair_sum_" + n) for n, s, t in zip(BIG, send, swapped)]
        recv[l] = _chip_exchange(sums, "exchange_weight_grads_0")
        return None

    sq, grad_x = _local_step(x.reshape(B * S, D_MODEL), loss_target.reshape(B * S, D_MODEL), w, B, S,
                             weights_of, grads_out)
    loss = lax.psum(sq * (0.5 / D_MODEL), AXES)

    res = {}
    for i, n in enumerate(BIG):
        rows = lambda a: a.reshape(-1, a.shape[-1])
        outs = _reduce_adamw([recv[l][i] for l in range(DEPTH)], rows(w[n]), rows(m[n]), rows(v[n]),
                             "reduce_adamw_" + n)
        res[n] = tuple(a.reshape(w[n].shape) for a in outs)
    part = {n: jnp.stack([small_parts[l][i] for l in range(DEPTH)]) for i, n in enumerate(SMALL)}

    def widen(t):
        return lax.dynamic_update_slice(jnp.zeros((DEPTH, CONV_K, CONV_WIDTH), F32), t, (0, 0, me * cshard))

    small_like = [part[n] for n in SMALL]
    pick = lambda d: [widen(d[n]) if n == "conv_w" else d[n] for n in SMALL]
    parts, = _exchange([_pack_local(small_like)], "all_gather_small_grads", gather=True)
    gs, ds, ms, vs = _reduce_adamw([parts], _pack_local(pick(w)), _pack_local(pick(m)), _pack_local(pick(v)),
                                   "reduce_adamw_small")
    for n, t in zip(SMALL, zip(*(_unpack_local(a, small_like) for a in (gs, ds, ms, vs)))):
        if n == "conv_w":
            t = tuple(lax.dynamic_slice(a, (0, 0, me * cshard), (DEPTH, CONV_K, cshard)) for a in t)
        res[n] = t

    out = [loss, grad_x.reshape(B, S, D_MODEL)]
    for i in range(4):
        out += [res[n][i] for n in names]
    return tuple(out)
```

```python
import jax
import jax.numpy as jnp
from jax import lax
from jax.experimental import pallas as pl
from jax.experimental.pallas import tpu as pltpu

F32 = jnp.float32
BF16 = jnp.bfloat16

D_MODEL = 1024
DEPTH = 2
CONV_WIDTH = 512
CONV_K = 3
MLA_HEADS = 8
MLA_Q_LORA = 256
MLA_KV_LORA = 128
MLA_NOPE = 64
MLA_ROPE = 32
MLA_V = 64
MLA_QK = MLA_NOPE + MLA_ROPE
ROPE_THETA = 10000.0
DIL_PATTERNS = ((128, 1), (512, 4), (2048, 16))
DIL_GROUPS = 3
DIL_HEADS = 8
DIL_HEAD_DIM = 64
DIL_WIDTH = DIL_HEADS * DIL_HEAD_DIM
DIL_QK = DIL_GROUPS * DIL_WIDTH
EPS = 1e-6
N_IN = 11168

ADAM_LR = 0.001
ADAM_B1 = 0.9
ADAM_B2 = 0.999
ADAM_EPS = 1e-08
ADAM_WD = 0.01
ADAM_STEP = 10

N_DEV = 8
AXES = ("x", "y", "c")
LANE = 128
HALF = 64
NPAIR = 4

CB_BZ, CB_CZ, CB_GATE = 0, 4, 8
CB_A = 32
CB_QKV = 48
CB_CQ, CB_CKV, CB_KPE = 84, 86, 87
NCB = 88
PP = NCB * LANE
SHARD_COLS = N_IN // N_DEV
NEG = -1e30
VMEM_LIMIT = 56 * 1024 * 1024


def _column_chunks():
    out = []
    col = 0

    def seg(nblocks, block_of):
        nonlocal col
        for i in range(nblocks):
            out.append((col, LANE, block_of(i)))
            col += LANE

    seg(4, lambda j: CB_A + 4 * j)
    seg(4, lambda j: CB_A + 4 * j + 1)
    seg(4, lambda j: CB_A + 4 * j + 2)
    seg(4, lambda j: CB_A + 4 * j + 3)
    seg(2, lambda i: CB_CQ + i)
    seg(1, lambda i: CB_CKV)
    out.append((col, MLA_ROPE, CB_KPE))
    col += MLA_ROPE
    seg(4, lambda j: CB_BZ + j)
    seg(12, lambda c: CB_QKV + 3 * c)
    seg(12, lambda c: CB_QKV + 3 * c + 1)
    seg(12, lambda c: CB_QKV + 3 * c + 2)
    seg(4, lambda j: CB_CZ + j)
    seg(24, lambda i: CB_GATE + i)
    assert col == N_IN and sorted(c[2] for c in out) == list(range(NCB))
    return out


COLUMN_CHUNKS = _column_chunks()


def _pad_columns(shards):
    parts = []
    for start, width, _ in sorted(COLUMN_CHUNKS, key=lambda c: c[2]):
        fill = LANE - width
        while width:
            p, off = divmod(start, SHARD_COLS)
            n = min(width, SHARD_COLS - off)
            parts.append(shards[p, :, off:off + n])
            start, width = start + n, width - n
        if fill:
            parts.append(jnp.zeros((shards.shape[1], fill), shards.dtype))
    return jnp.concatenate(parts, axis=1)


def _unpad_columns(wp):
    pieces = [[] for _ in range(N_DEV)]
    for start, width, b in COLUMN_CHUNKS:
        src = b * LANE
        while width:
            p, off = divmod(start, SHARD_COLS)
            n = min(width, SHARD_COLS - off)
            pieces[p].append(wp[:, src:src + n])
            start, width, src = start + n, width - n, src + n
    return jnp.stack([jnp.concatenate(ps, axis=1) for ps in pieces])


def _cp():
    return pltpu.CompilerParams(vmem_limit_bytes=VMEM_LIMIT)


def _rstd(x, n):
    return lax.rsqrt(jnp.sum(x * x, axis=-1, keepdims=True) * (1.0 / n) + EPS)


def _sigmoid(z):
    return 1.0 / (1.0 + jnp.exp(-z))


def _silu(z):
    return z * _sigmoid(z)


def _dsilu(z):
    s = _sigmoid(z)
    return s * (1.0 + z * (1.0 - s))


def _mm(a, b):
    return jnp.dot(a.astype(BF16), b.astype(BF16), preferred_element_type=F32)


def _mm_nt(a, b):
    return lax.dot_general(a.astype(BF16), b.astype(BF16), (((1,), (1,)), ((), ())), preferred_element_type=F32)


def _mm_tn(a, b):
    return lax.dot_general(a.astype(BF16), b.astype(BF16), (((0,), (0,)), ((), ())), preferred_element_type=F32)


def _lane_lo(shape):
    return lax.broadcasted_iota(jnp.int32, shape, len(shape) - 1) < HALF


def _head_bcast_sum(x, terms=3):
    w = x.shape[-1]
    same = (lax.broadcasted_iota(jnp.int32, (w, w), 0) // HALF) == (lax.broadcasted_iota(jnp.int32, (w, w), 1) // HALF)
    ones = jnp.where(same, 1.0, 0.0).astype(jnp.bfloat16)
    total = None
    for _ in range(terms):
        term = x.astype(jnp.bfloat16)
        x = x - term.astype(F32)
        part = jnp.dot(term, ones, preferred_element_type=F32)
        total = part if total is None else total + part
    return total


def _rope(t, cos, sa, sb):
    return t * cos + pltpu.roll(t, LANE - 16, axis=1) * sa + pltpu.roll(t, 16, axis=1) * sb


def _rope_t(d, cos, sa, sb):
    return d * cos + pltpu.roll(d * sa, 16, axis=1) + pltpu.roll(d * sb, LANE - 16, axis=1)


def _shift_down(u, k):
    rows = lax.broadcasted_iota(jnp.int32, u.shape, 0)
    return jnp.where(rows >= k, pltpu.roll(u, k, axis=0), 0.0)


def _shift_up(u, k):
    n = u.shape[0]
    rows = lax.broadcasted_iota(jnp.int32, u.shape, 0)
    return jnp.where(rows < n - k, pltpu.roll(u, n - k, axis=0), 0.0)


def _tile(n, want):
    t = min(n, want)
    assert n % t == 0, (n, want)
    return t


def _inproj_fwd(x, g, wp):
    T = x.shape[0]
    tm, tn = _tile(T, 2048), 512

    def body(x_ref, g_ref, w_ref, proj_ref, ht_ref, h_ref):
        @pl.when(pl.program_id(1) == 0)
        def _():
            n = min(tm, 512)
            for r0 in range(0, tm, n):
                xv = x_ref[r0:r0 + n, :]
                h = xv * _rstd(xv, D_MODEL) * g_ref[...]
                h_ref[r0:r0 + n, :] = h.astype(BF16)
                ht_ref[:, r0:r0 + n] = h.T.astype(BF16)

        proj_ref[...] = jnp.dot(h_ref[...], w_ref[...], preferred_element_type=F32).astype(BF16)

    return pl.pallas_call(
        body, name="inproj_fwd", grid=(T // tm, PP // tn),
        in_specs=[pl.BlockSpec((tm, D_MODEL), lambda i, j: (i, 0)),
                  pl.BlockSpec((1, D_MODEL), lambda i, j: (0, 0)),
                  pl.BlockSpec((D_MODEL, tn), lambda i, j: (0, j))],
        out_specs=[pl.BlockSpec((tm, tn), lambda i, j: (i, j)),
                   pl.BlockSpec((D_MODEL, tm), lambda i, j: (0, i))],
        out_shape=[jax.ShapeDtypeStruct((T, PP), BF16), jax.ShapeDtypeStruct((D_MODEL, T), BF16)],
        scratch_shapes=[pltpu.VMEM((tm, D_MODEL), BF16)],
        compiler_params=_cp(),
    )(x, g, wp)


def _matmul_nn(at, b, name):
    K, T = at.shape
    N = b.shape[1]
    tt, tn = _tile(T, 1024), _tile(N, 2816)
    nk = T // tt

    def body(a_ref, b_ref, o_ref, acc_ref):
        k = pl.program_id(1)

        @pl.when(k == 0)
        def _():
            acc_ref[...] = jnp.zeros_like(acc_ref)

        acc_ref[...] += jnp.dot(a_ref[...], b_ref[...], preferred_element_type=F32)

        @pl.when(k == nk - 1)
        def _():
            o_ref[...] = acc_ref[...].astype(BF16)

    return pl.pallas_call(
        body, name=name, grid=(N // tn, nk),
        in_specs=[pl.BlockSpec((K, tt), lambda j, k: (0, k)),
                  pl.BlockSpec((tt, tn), lambda j, k: (k, j))],
        out_specs=pl.BlockSpec((K, tn), lambda j, k: (0, j)),
        out_shape=jax.ShapeDtypeStruct((K, N), BF16),
        scratch_shapes=[pltpu.VMEM((K, tn), F32)],
        compiler_params=_cp(),
    )(at, b)


def _matmul_tn(a, b, name):
    T, K = a.shape
    N = b.shape[1]
    tt, tn = _tile(T, 512), _tile(N, 1024)

    def body(a_ref, b_ref, o_ref):
        @pl.when(pl.program_id(1) == 0)
        def _():
            o_ref[...] = jnp.zeros_like(o_ref)

        o_ref[...] += _mm_tn(a_ref[...], b_ref[...])

    return pl.pallas_call(
        body, name=name, grid=(N // tn, T // tt),
        in_specs=[pl.BlockSpec((tt, K), lambda j, k: (k, 0)),
                  pl.BlockSpec((tt, tn), lambda j, k: (k, j))],
        out_specs=pl.BlockSpec((K, tn), lambda j, k: (0, j)),
        out_shape=jax.ShapeDtypeStruct((K, N), F32),
        compiler_params=_cp(),
    )(a, b)


def _inproj_bwd_x(dproj, wp, x, g, dout):
    T = x.shape[0]
    tm, tk = _tile(T, 1024), 1024
    nk = PP // tk

    def body(dp_ref, w_ref, x_ref, g_ref, do_ref, dx_ref, dg_ref, acc_ref):
        i, k = pl.program_id(0), pl.program_id(1)

        @pl.when(k == 0)
        def _():
            acc_ref[...] = jnp.zeros_like(acc_ref)

        @pl.when((k == 0) & (i == 0))
        def _():
            dg_ref[...] = jnp.zeros_like(dg_ref)

        acc_ref[...] += _mm_nt(dp_ref[...], w_ref[...])

        @pl.when(k == nk - 1)
        def _():
            dh = acc_ref[...]
            xv = x_ref[...]
            r = _rstd(xv, D_MODEL)
            gy = dh * g_ref[...]
            dot = jnp.sum(xv * gy, axis=-1, keepdims=True) * (1.0 / D_MODEL)
            dx_ref[...] = do_ref[...] + r * gy - xv * (r * r * r) * dot
            dg_ref[...] += jnp.sum(dh * xv * r, axis=0, keepdims=True)

    return pl.pallas_call(
        body, name="inproj_bwd_x", grid=(T // tm, nk),
        in_specs=[pl.BlockSpec((tm, tk), lambda i, k: (i, k)),
                  pl.BlockSpec((D_MODEL, tk), lambda i, k: (0, k)),
                  pl.BlockSpec((tm, D_MODEL), lambda i, k: (i, 0)),
                  pl.BlockSpec((1, D_MODEL), lambda i, k: (0, 0)),
                  pl.BlockSpec((tm, D_MODEL), lambda i, k: (i, 0))],
        out_specs=[pl.BlockSpec((tm, D_MODEL), lambda i, k: (i, 0)),
                   pl.BlockSpec((1, D_MODEL), lambda i, k: (0, 0))],
        out_shape=[jax.ShapeDtypeStruct((T, D_MODEL), F32), jax.ShapeDtypeStruct((1, D_MODEL), F32)],
        scratch_shapes=[pltpu.VMEM((tm, D_MODEL), F32)],
        compiler_params=_cp(),
    )(dproj, wp, x, g, dout)


def _mixa_fwd(proj, cw, cb, B, S):
    nc = CONV_WIDTH // LANE
    ca = CB_A // 4

    def body(p_ref, cw_ref, cb_ref, y_ref):
        ab, ac, ax, az = (p_ref[:, i * LANE:(i + 1) * LANE].astype(F32) for i in range(4))
        u = ac * ax
        conv = cb_ref[...] + cw_ref[0:1, :] * _shift_down(u, 2) + cw_ref[1:2, :] * _shift_down(u, 1) + cw_ref[2:3, :] * u
        y_ref[...] = (ab * conv * _silu(az)).astype(BF16)

    return pl.pallas_call(
        body, name="mixa_fwd", grid=(B, nc),
        in_specs=[pl.BlockSpec((S, 4 * LANE), lambda b, j: (b, ca + j)),
                  pl.BlockSpec((CONV_K, LANE), lambda b, j: (0, j)),
                  pl.BlockSpec((1, LANE), lambda b, j: (0, j))],
        out_specs=pl.BlockSpec((S, LANE), lambda b, j: (b, j)),
        out_shape=jax.ShapeDtypeStruct((B * S, CONV_WIDTH), BF16),
        compiler_params=_cp(),
    )(proj, cw, cb)


def _mixa_bwd(dproj, dy, proj, cw, cb, B, S):
    nc = CONV_WIDTH // LANE
    ca = CB_A // 4

    def body(dpin_ref, dy_ref, p_ref, cw_ref, cb_ref, dp_ref, st_ref):
        del dpin_ref
        ab, ac, ax, az = (p_ref[:, i * LANE:(i + 1) * LANE].astype(F32) for i in range(4))
        u = ac * ax
        u1, u2 = _shift_down(u, 1), _shift_down(u, 2)
        w0, w1, w2 = cw_ref[0:1, :], cw_ref[1:2, :], cw_ref[2:3, :]
        conv = cb_ref[...] + w0 * u2 + w1 * u1 + w2 * u
        s = _silu(az)
        d = dy_ref[...]
        dconv = d * ab * s
        du = w2 * dconv + w1 * _shift_up(dconv, 1) + w0 * _shift_up(dconv, 2)
        dp_ref[:, 0:LANE] = (d * conv * s).astype(BF16)
        dp_ref[:, LANE:2 * LANE] = (du * ax).astype(BF16)
        dp_ref[:, 2 * LANE:3 * LANE] = (du * ac).astype(BF16)
        dp_ref[:, 3 * LANE:4 * LANE] = (d * ab * conv * _dsilu(az)).astype(BF16)
        row = lax.broadcasted_iota(jnp.int32, (8, LANE), 0)
        st = jnp.zeros((8, LANE), F32)
        for r, v in enumerate((dconv * u2, dconv * u1, dconv * u, dconv)):
            st = st + jnp.where(row == r, jnp.sum(v, axis=0, keepdims=True), 0.0)

        @pl.when(pl.program_id(1) == 0)
        def _():
            st_ref[...] = st

        @pl.when(pl.program_id(1) != 0)
        def _():
            st_ref[...] += st

    return pl.pallas_call(
        body, name="mixa_bwd", grid=(nc, B),
        in_specs=[pl.BlockSpec(memory_space=pl.ANY),
                  pl.BlockSpec((S, LANE), lambda j, b: (b, j)),
                  pl.BlockSpec((S, 4 * LANE), lambda j, b: (b, ca + j)),
                  pl.BlockSpec((CONV_K, LANE), lambda j, b: (0, j)),
                  pl.BlockSpec((1, LANE), lambda j, b: (0, j))],
        out_specs=[pl.BlockSpec((S, 4 * LANE), lambda j, b: (b, ca + j)),
                   pl.BlockSpec((8, LANE), lambda j, b: (0, j))],
        out_shape=[jax.ShapeDtypeStruct(dproj.shape, BF16), jax.ShapeDtypeStruct((8, CONV_WIDTH), F32)],
        input_output_aliases={0: 0},
        compiler_params=_cp(),
    )(dproj, dy, proj, cw, cb)


def _mla_prep_fwd(proj, gq, gkv, wuqp, wkp, wv, gmq, gmk, cos, sa, sb, S):
    T = proj.shape[0]
    ts = _tile(S, 512)
    ns = S // ts
    W = MLA_HEADS * LANE

    def body(p_ref, gq_ref, gkv_ref, wuq_ref, wk_ref, wv_ref, gmq_ref, gmk_ref, cos_ref, sa_ref, sb_ref,
             q_ref, k_ref, v_ref):
        cq = p_ref[:, 0:2 * LANE].astype(F32)
        ckv = p_ref[:, 2 * LANE:3 * LANE].astype(F32)
        kpe = pltpu.roll(p_ref[:, 3 * LANE:4 * LANE].astype(F32), HALF, axis=1)
        cqn = cq * _rstd(cq, MLA_Q_LORA) * gq_ref[...]
        ckn = (ckv * _rstd(ckv, MLA_KV_LORA) * gkv_ref[...]).astype(BF16)
        q0 = _mm(cqn, wuq_ref[...])
        kn = _mm(ckn, wk_ref[...])
        v_ref[...] = _mm(ckn, wv_ref[...]).astype(BF16)
        c, a, b = cos_ref[...], sa_ref[...], sb_ref[...]
        for h in range(MLA_HEADS):
            q0h = q0[:, h * LANE:(h + 1) * LANE]
            q_ref[h] = _rope(q0h * _rstd(q0h, MLA_QK) * gmq_ref[...], c, a, b).astype(BF16)
            k0h = kn[:, h * LANE:(h + 1) * LANE] + kpe
            k_ref[h] = _rope(k0h * _rstd(k0h, MLA_QK) * gmk_ref[...], c, a, b).astype(BF16)

    def whole(r, c):
        return pl.BlockSpec((r, c), lambda i: (0, 0))

    tab = pl.BlockSpec((ts, LANE), lambda i: (i % ns, 0))
    return pl.pallas_call(
        body, name="mla_prep_fwd", grid=(T // ts,),
        in_specs=[pl.BlockSpec((ts, 4 * LANE), lambda i: (i, CB_CQ // 4)),
                  whole(1, MLA_Q_LORA), whole(1, MLA_KV_LORA), whole(MLA_Q_LORA, W), whole(MLA_KV_LORA, W),
                  whole(MLA_KV_LORA, MLA_HEADS * MLA_V), whole(1, LANE), whole(1, LANE), tab, tab, tab],
        out_specs=[pl.BlockSpec((MLA_HEADS, ts, LANE), lambda i: (0, i, 0)),
                   pl.BlockSpec((MLA_HEADS, ts, LANE), lambda i: (0, i, 0)),
                   pl.BlockSpec((ts, MLA_HEADS * MLA_V), lambda i: (i, 0))],
        out_shape=[jax.ShapeDtypeStruct((MLA_HEADS, T, LANE), BF16), jax.ShapeDtypeStruct((MLA_HEADS, T, LANE), BF16),
                   jax.ShapeDtypeStruct((T, MLA_HEADS * MLA_V), BF16)],
        compiler_params=_cp(),
    )(proj, gq, gkv, wuqp, wkp, wv, gmq, gmk, cos, sa, sb)


def _mla_prep_bwd(dproj, dq, dk, dv, proj, gq, gkv, wuqp, wkp, wv, gmq, gmk, cos, sa, sb, S):
    T = proj.shape[0]
    ts = _tile(S, 256)
    ns = S // ts
    W = MLA_HEADS * LANE

    def body(dpin_ref, dq_ref, dk_ref, dv_ref, p_ref, gq_ref, gkv_ref, wuq_ref, wk_ref, wv_ref, gmq_ref, gmk_ref,
             cos_ref, sa_ref, sb_ref,
             dp_ref, dwuq_ref, dwk_ref, dwv_ref, dgq_ref, dgkv_ref, dgmq_ref, dgmk_ref, dq0_ref, dkn_ref):
        del dpin_ref

        @pl.when(pl.program_id(0) == 0)
        def _():
            for r in (dwuq_ref, dwk_ref, dwv_ref, dgq_ref, dgkv_ref, dgmq_ref, dgmk_ref):
                r[...] = jnp.zeros_like(r)

        cq = p_ref[:, 0:2 * LANE].astype(F32)
        ckv = p_ref[:, 2 * LANE:3 * LANE].astype(F32)
        kpe = pltpu.roll(p_ref[:, 3 * LANE:4 * LANE].astype(F32), HALF, axis=1)
        rq = _rstd(cq, MLA_Q_LORA)
        rkv = _rstd(ckv, MLA_KV_LORA)
        gq, gkv, gmq, gmk = gq_ref[...], gkv_ref[...], gmq_ref[...], gmk_ref[...]
        cqn = (cq * rq * gq).astype(BF16)
        ckn = (ckv * rkv * gkv).astype(BF16)
        q0 = _mm(cqn, wuq_ref[...])
        kn = _mm(ckn, wk_ref[...])
        c, a, b = cos_ref[...], sa_ref[...], sb_ref[...]
        lane = lax.broadcasted_iota(jnp.int32, (ts, LANE), 1)
        dgmq = jnp.zeros((1, LANE), F32)
        dgmk = jnp.zeros((1, LANE), F32)
        dkpe = jnp.zeros((ts, LANE), F32)
        for h in range(MLA_HEADS):
            q0h = q0[:, h * LANE:(h + 1) * LANE]
            r = _rstd(q0h, MLA_QK)
            d1 = _rope_t(dq_ref[h], c, a, b)
            gy = d1 * gmq
            dq0_ref[:, h * LANE:(h + 1) * LANE] = (
                r * gy - q0h * (r * r * r) * (jnp.sum(q0h * gy, axis=-1, keepdims=True) * (1.0 / MLA_QK))).astype(BF16)
            dgmq = dgmq + jnp.sum(d1 * q0h * r, axis=0, keepdims=True)
            k0h = kn[:, h * LANE:(h + 1) * LANE] + kpe
            r = _rstd(k0h, MLA_QK)
            d1 = _rope_t(dk_ref[h], c, a, b)
            gy = d1 * gmk
            dk0 = r * gy - k0h * (r * r * r) * (jnp.sum(k0h * gy, axis=-1, keepdims=True) * (1.0 / MLA_QK))
            dgmk = dgmk + jnp.sum(d1 * k0h * r, axis=0, keepdims=True)
            dkn_ref[:, h * LANE:(h + 1) * LANE] = jnp.where(lane < MLA_NOPE, dk0, 0.0).astype(BF16)
            dkpe = dkpe + jnp.where((lane >= MLA_NOPE) & (lane < MLA_QK), dk0, 0.0)
        dq0 = dq0_ref[...]
        dkn = dkn_ref[...]
        dvv = dv_ref[...]
        dwuq_ref[...] += _mm_tn(cqn, dq0)
        dwk_ref[...] += _mm_tn(ckn, dkn)
        dwv_ref[...] += _mm_tn(ckn, dvv)
        dgmq_ref[...] += dgmq
        dgmk_ref[...] += dgmk
        dcqn = _mm_nt(dq0, wuq_ref[...])
        gy = dcqn * gq
        dp_ref[:, 0:2 * LANE] = (
            rq * gy - cq * (rq * rq * rq) * (jnp.sum(cq * gy, axis=-1, keepdims=True) * (1.0 / MLA_Q_LORA))).astype(BF16)
        dgq_ref[...] += jnp.sum(dcqn * cq * rq, axis=0, keepdims=True)
        dckn = _mm_nt(dkn, wk_ref[...]) + _mm_nt(dvv, wv_ref[...])
        gy = dckn * gkv
        dp_ref[:, 2 * LANE:3 * LANE] = (
            rkv * gy - ckv * (rkv * rkv * rkv) * (jnp.sum(ckv * gy, axis=-1, keepdims=True) * (1.0 / MLA_KV_LORA))).astype(BF16)
        dgkv_ref[...] += jnp.sum(dckn * ckv * rkv, axis=0, keepdims=True)
        dp_ref[:, 3 * LANE:4 * LANE] = pltpu.roll(dkpe, HALF, axis=1).astype(BF16)

    def whole(r, c):
        return pl.BlockSpec((r, c), lambda i: (0, 0))

    tab = pl.BlockSpec((ts, LANE), lambda i: (i % ns, 0))
    heads = pl.BlockSpec((MLA_HEADS, ts, LANE), lambda i: (0, i, 0))
    return pl.pallas_call(
        body, name="mla_prep_bwd", grid=(T // ts,),
        in_specs=[pl.BlockSpec(memory_space=pl.ANY), heads, heads,
                  pl.BlockSpec((ts, MLA_HEADS * MLA_V), lambda i: (i, 0)),
                  pl.BlockSpec((ts, 4 * LANE), lambda i: (i, CB_CQ // 4)),
                  whole(1, MLA_Q_LORA), whole(1, MLA_KV_LORA), whole(MLA_Q_LORA, W), whole(MLA_KV_LORA, W),
                  whole(MLA_KV_LORA, MLA_HEADS * MLA_V), whole(1, LANE), whole(1, LANE), tab, tab, tab],
        out_specs=[pl.BlockSpec((ts, 4 * LANE), lambda i: (i, CB_CQ // 4)),
                   whole(MLA_Q_LORA, W), whole(MLA_KV_LORA, W), whole(MLA_KV_LORA, MLA_HEADS * MLA_V),
                   whole(1, MLA_Q_LORA), whole(1, MLA_KV_LORA), whole(1, LANE), whole(1, LANE)],
        out_shape=[jax.ShapeDtypeStruct(dproj.shape, BF16),
                   jax.ShapeDtypeStruct((MLA_Q_LORA, W), F32), jax.ShapeDtypeStruct((MLA_KV_LORA, W), F32),
                   jax.ShapeDtypeStruct((MLA_KV_LORA, MLA_HEADS * MLA_V), F32),
                   jax.ShapeDtypeStruct((1, MLA_Q_LORA), F32), jax.ShapeDtypeStruct((1, MLA_KV_LORA), F32),
                   jax.ShapeDtypeStruct((1, LANE), F32), jax.ShapeDtypeStruct((1, LANE), F32)],
        scratch_shapes=[pltpu.VMEM((ts, W), BF16), pltpu.VMEM((ts, W), BF16)],
        input_output_aliases={0: 0},
        compiler_params=_cp(),
    )(dproj, dq, dk, dv, proj, gq, gkv, wuqp, wkp, wv, gmq, gmk, cos, sa, sb)


def _dil_prep_fwd(proj, gq, gk):
    T = proj.shape[0]
    ts = _tile(T, 512)
    gw = 3 * NPAIR * LANE

    def body(p_ref, gq_ref, gk_ref, q_ref, k_ref, v_ref):
        for c in range(NPAIR):
            t = p_ref[:, 3 * c * LANE:(3 * c + 2) * LANE].astype(F32)
            y = t * lax.rsqrt(_head_bcast_sum(t * t, terms=2) * (1.0 / DIL_HEAD_DIM) + EPS)
            cs = slice(c * LANE, (c + 1) * LANE)
            q_ref[:, cs] = y[:, 0:LANE] * gq_ref[:, cs]
            k_ref[:, cs] = y[:, LANE:2 * LANE] * gk_ref[:, cs]
            v_ref[:, cs] = p_ref[:, (3 * c + 2) * LANE:(3 * c + 3) * LANE].astype(F32)

    col = pl.BlockSpec((1, DIL_WIDTH), lambda i, g: (0, g))
    out = pl.BlockSpec((ts, DIL_WIDTH), lambda i, g: (i, g))
    return pl.pallas_call(
        body, name="dil_prep_fwd", grid=(T // ts, DIL_GROUPS),
        in_specs=[pl.BlockSpec((ts, gw), lambda i, g: (i, CB_QKV * LANE // gw + g)), col, col],
        out_specs=[out, out, out],
        out_shape=[jax.ShapeDtypeStruct((T, DIL_QK), F32)] * 3,
        compiler_params=_cp(),
    )(proj, gq, gk)


def _dil_prep_bwd(dproj, ddq, ddk, ddv, proj, gq, gk):
    T = proj.shape[0]
    ts = _tile(T, 512)
    gw = 3 * NPAIR * LANE

    def body(dpin_ref, ddq_ref, ddk_ref, ddv_ref, p_ref, gq_ref, gk_ref, dp_ref, dgq_ref, dgk_ref):
        del dpin_ref

        @pl.when(pl.program_id(1) == 0)
        def _():
            dgq_ref[...] = jnp.zeros_like(dgq_ref)
            dgk_ref[...] = jnp.zeros_like(dgk_ref)

        for c in range(NPAIR):
            cs = slice(c * LANE, (c + 1) * LANE)
            dp_ref[:, (3 * c + 2) * LANE:(3 * c + 3) * LANE] = ddv_ref[:, cs].astype(BF16)
            t = p_ref[:, 3 * c * LANE:(3 * c + 2) * LANE].astype(F32)
            d = jnp.concatenate([ddq_ref[:, cs], ddk_ref[:, cs]], axis=1)
            gy = d * jnp.concatenate([gq_ref[:, cs], gk_ref[:, cs]], axis=1)
            r = lax.rsqrt(_head_bcast_sum(t * t, terms=2) * (1.0 / DIL_HEAD_DIM) + EPS)
            dot = _head_bcast_sum(t * gy, terms=2) * (1.0 / DIL_HEAD_DIM)
            dp_ref[:, 3 * c * LANE:(3 * c + 2) * LANE] = (r * gy - t * (r * r * r) * dot).astype(BF16)
            part = jnp.sum(d * t * r, axis=0, keepdims=True)
            dgq_ref[:, cs] += part[:, 0:LANE]
            dgk_ref[:, cs] += part[:, LANE:2 * LANE]

    col = pl.BlockSpec((1, DIL_WIDTH), lambda g, i: (0, g))
    tok = pl.BlockSpec((ts, DIL_WIDTH), lambda g, i: (i, g))
    return pl.pallas_call(
        body, name="dil_prep_bwd", grid=(DIL_GROUPS, T // ts),
        in_specs=[pl.BlockSpec(memory_space=pl.ANY), tok, tok, tok,
                  pl.BlockSpec((ts, gw), lambda g, i: (i, CB_QKV * LANE // gw + g)), col, col],
        out_specs=[pl.BlockSpec((ts, gw), lambda g, i: (i, CB_QKV * LANE // gw + g)), col, col],
        out_shape=[jax.ShapeDtypeStruct(dproj.shape, BF16), jax.ShapeDtypeStruct((1, DIL_QK), F32),
                   jax.ShapeDtypeStruct((1, DIL_QK), F32)],
        input_output_aliases={0: 0},
        compiler_params=_cp(),
    )(dproj, ddq, ddk, ddv, proj, gq, gk)


COPY_ROWS = 256


def _to_classes(src_ref, dst_ref, d, L, scale=None):
    n = min(L, COPY_ROWS)
    for r in range(d):
        for c0 in range(0, L, n):
            rows = pl.ds(r + c0 * d, n, stride=d) if d > 1 else pl.ds(c0, n)
            val = src_ref[rows, :]
            if scale is not None:
                val = val * scale
            dst_ref[r * L + c0:r * L + c0 + n, :] = val.astype(dst_ref.dtype)


def _from_classes(src_ref, dst_ref, d, L):
    n = min(L, COPY_ROWS)
    for r in range(d):
        for c0 in range(0, L, n):
            rows = pl.ds(r + c0 * d, n, stride=d) if d > 1 else pl.ds(c0, n)
            dst_ref[rows, :] = src_ref[r * L + c0:r * L + c0 + n, :].astype(dst_ref.dtype)


MLA_TQ, MLA_TK = 512, 512


def _causal_bias(tq, tk, shift):
    row = lax.broadcasted_iota(jnp.int32, (tq, tk), 0)
    col = lax.broadcasted_iota(jnp.int32, (tq, tk), 1)
    return jnp.where(row >= col + shift, 0.0, NEG)


def _mla_specs(S):
    heads = pl.BlockSpec((2, S, LANE), lambda b, j: (j, b, 0))
    pair = pl.BlockSpec((S, LANE), lambda b, j: (b, j))
    return heads, pair


def _mla_attn_fwd(q, k, v, B, S):
    tq = _tile(S, MLA_TQ)
    tk = _tile(tq, MLA_TK)
    nd = tq // tk
    scale = MLA_QK ** -0.5
    heads, pair = _mla_specs(S)

    def body(q_ref, k_ref, v_ref, o_ref, lse_ref):
        lo, lok = _lane_lo((tq, LANE)), _lane_lo((tk, LANE))
        diag = [_causal_bias(tq, tk, i * tk) for i in range(nd)]

        def block(g, _):
            row0 = pl.multiple_of(g * tq, tq)
            rows = pl.ds(row0, tq)
            qs = [q_ref[hh, rows, :] for hh in range(2)]

            one = jnp.ones((), BF16)

            def step(off, carries, bias):
                off = pl.multiple_of(off, tk)
                vt = v_ref[pl.ds(off, tk), :]
                vh = (jnp.where(lok, vt, one), jnp.where(lok, one, vt))
                out = []
                for hh, (m, acc) in enumerate(carries):
                    s = _mm_nt(qs[hh], k_ref[hh, pl.ds(off, tk), :]) * scale
                    if bias is not None:
                        s = s + bias
                    m_new = jnp.maximum(m, jnp.max(s, axis=-1, keepdims=True))
                    p = jnp.exp(s - m_new)
                    out.append((m_new, jnp.exp(m - m_new) * acc + _mm(p, vh[hh])))
                return tuple(out)

            init = (jnp.full((tq, 1), NEG, F32), jnp.zeros((tq, LANE), F32))
            carries = lax.fori_loop(0, g * nd, lambda i, c: step(i * tk, c, None), (init, init))
            for i in range(nd):
                carries = step(row0 + i * tk, carries, diag[i])
            (ma, acca), (mb, accb) = carries
            la, lb = pltpu.roll(acca, HALF, axis=1), pltpu.roll(accb, HALF, axis=1)
            o_ref[rows, :] = jnp.where(lo, acca / la, accb / lb)
            lse_ref[rows, :] = jnp.where(lo, ma + jnp.log(la), mb + jnp.log(lb))
            return 0

        lax.fori_loop(0, S // tq, block, 0)

    return pl.pallas_call(
        body, name="mla_attn_fwd", grid=(B, NPAIR), in_specs=[heads, heads, pair], out_specs=[pair, pair],
        out_shape=[jax.ShapeDtypeStruct((B * S, MLA_HEADS * MLA_V), F32)] * 2,
        compiler_params=_cp(),
    )(q, k, v)


DIL_UNROLL = 8


def _dil_geometry(gi, S):
    span, d = DIL_PATTERNS[gi]
    L = S // d
    t = _tile(L, 128)
    window = span // d
    back = min(-(-window // t) * t, L - t)
    return d, L, t, window, back


def _dil_specs(gi, S):
    qk = pl.BlockSpec((S, LANE), lambda b, j: (b, NPAIR * gi + j))
    pair = pl.BlockSpec((S, LANE), lambda b, j: (b, j))
    return qk, qk, pair


def _dil_bias(bias_ref, sl_ref, j, t, kw, back, window):
    row = lax.broadcasted_iota(jnp.int32, (2 * t, kw), 0)
    col = lax.broadcasted_iota(jnp.int32, (2 * t, kw), 1)
    second = row >= t
    slope = jnp.where(second, sl_ref[j, 1], sl_ref[j, 0])
    for n in range(bias_ref.shape[0]):
        dist = jnp.where(second, row - t, row) + n * back - col
        bias_ref[n] = jnp.where((dist >= 0) & (dist <= window), -slope * dist.astype(F32), NEG)


def _stack_heads(x, lo):
    zero = jnp.zeros((), x.dtype)
    return jnp.concatenate([jnp.where(lo, x, zero), jnp.where(lo, zero, x)], axis=0)


def _dil_attn_fwd(gi, slopes, qn, kn, proj, B, S):
    d, L, t, window, back = _dil_geometry(gi, S)
    kw, nq = back + t, L // t
    nbias = 2 if back else 1
    qk, vspec, pair = _dil_specs(gi, S)

    def body(sl_ref, q_ref, k_ref, v_ref, o_ref, lse_ref, qs, ks, vs, os_, ls, bias_ref):
        _to_classes(q_ref, qs, d, L, DIL_HEAD_DIM ** -0.5)
        _to_classes(k_ref, ks, d, L)
        _to_classes(v_ref, vs, d, L)
        _dil_bias(bias_ref, sl_ref, pl.program_id(1), t, kw, back, window)
        lo = _lane_lo((t, LANE))

        def block(g, _):
            qb = g % nq if d > 1 else g
            row0 = pl.multiple_of(g * t, t)
            rows = pl.ds(row0, t)
            early = qb * t < back
            keys = pl.ds(pl.multiple_of(jnp.where(early, row0 - qb * t, row0 - back), t), kw)
            s = _mm_nt(_stack_heads(qs[rows, :], lo), ks[keys, :]) + bias_ref[jnp.where(early, 0, nbias - 1)]
            m = jnp.max(s, axis=-1, keepdims=True)
            p = jnp.exp(s - m)
            l = jnp.sum(p, axis=-1, keepdims=True)
            o2 = _mm(p, vs[keys, :]) / l
            lse2 = m + jnp.log(l)
            os_[rows, :] = jnp.where(lo, o2[:t], o2[t:])
            ls[rows, :] = jnp.where(lo, lse2[:t], lse2[t:])
            return 0

        lax.fori_loop(0, d * nq, block, 0, unroll=DIL_UNROLL if d * nq % DIL_UNROLL == 0 else 1)
        _from_classes(os_, o_ref, d, L)
        _from_classes(ls, lse_ref, d, L)

    return pl.pallas_call(
        body, name=f"dil_attn_fwd_{gi}", grid=(B, NPAIR),
        in_specs=[pl.BlockSpec(memory_space=pltpu.SMEM), qk, qk, vspec], out_specs=[pair, pair],
        out_shape=[jax.ShapeDtypeStruct((B * S, DIL_WIDTH), F32)] * 2,
        scratch_shapes=[pltpu.VMEM((S, LANE), BF16)] * 3 + [pltpu.VMEM((S, LANE), F32)] * 2
                       + [pltpu.VMEM((nbias, 2 * t, kw), F32)],
        compiler_params=_cp(),
    )(slopes, qn, kn, proj)


def _mla_attn_bwd(q, k, v, do, lse, delta, B, S):
    T = B * S
    tq = _tile(S, MLA_TQ)
    tk = _tile(tq, MLA_TK)
    nd = tq // tk
    scale = MLA_QK ** -0.5
    heads, pair = _mla_specs(S)

    def body(q_ref, k_ref, v_ref, do_ref, lse_ref, dl_ref, dq_ref, dk_ref, dv_ref):
        dk_ref[...] = jnp.zeros_like(dk_ref)
        dv_ref[...] = jnp.zeros_like(dv_ref)
        lo = _lane_lo((tq, LANE))
        diag = [_causal_bias(tq, tk, i * tk) for i in range(nd)]

        def block(g, _):
            row0 = pl.multiple_of(g * tq, tq)
            rows = pl.ds(row0, tq)
            for hh in range(2):
                sel = lo if hh == 0 else jnp.logical_not(lo)
                qh = q_ref[hh, rows, :]
                doh = jnp.where(sel, do_ref[rows, :], jnp.zeros((), BF16))
                lse_h = jnp.max(jnp.where(sel, lse_ref[rows, :], NEG), axis=-1, keepdims=True)
                dl_h = jnp.max(jnp.where(sel, dl_ref[rows, :], NEG), axis=-1, keepdims=True)

                def step(off, dq_acc, bias, hh=hh, qh=qh, doh=doh, lse_h=lse_h, dl_h=dl_h):
                    cols = pl.ds(pl.multiple_of(off, tk), tk)
                    kh = k_ref[hh, cols, :]
                    s = _mm_nt(qh, kh) * scale
                    if bias is not None:
                        s = s + bias
                    p = jnp.exp(s - lse_h)
                    dp = _mm_nt(doh, v_ref[cols, :])
                    ds = (p * (dp - dl_h)).astype(BF16)
                    dk_ref[hh, cols, :] += _mm_tn(ds, qh) * scale
                    dv_ref[cols, :] += _mm_tn(p, doh)
                    return dq_acc + _mm(ds, kh)

                dq_acc = lax.fori_loop(0, g * nd, lambda i, a: step(i * tk, a, None), jnp.zeros((tq, LANE), F32))
                for i in range(nd):
                    dq_acc = step(row0 + i * tk, dq_acc, diag[i])
                dq_ref[hh, rows, :] = dq_acc * scale
            return 0

        lax.fori_loop(0, S // tq, block, 0)

    return pl.pallas_call(
        body, name="mla_attn_bwd", grid=(B, NPAIR), in_specs=[heads, heads, pair, pair, pair, pair],
        out_specs=[heads, heads, pair],
        out_shape=[jax.ShapeDtypeStruct((MLA_HEADS, T, LANE), F32), jax.ShapeDtypeStruct((MLA_HEADS, T, LANE), F32),
                   jax.ShapeDtypeStruct((T, MLA_HEADS * MLA_V), F32)],
        compiler_params=_cp(),
    )(q, k, v, do, lse, delta)


def _dil_attn_bwd(gi, slopes, qn, kn, proj, do, lse, delta, through, B, S):
    d, L, t, window, back = _dil_geometry(gi, S)
    kw, nq = back + t, L // t
    nbias = 2 if back else 1
    scale = DIL_HEAD_DIM ** -0.5
    qk, vspec, pair = _dil_specs(gi, S)

    def body(*refs):
        refs = list(refs)
        sl_ref, q_ref, k_ref, v_ref, do_ref, lse_ref, dl_ref = refs[:7]
        dq_ref, dk_ref, dv_ref, qs, ks, vs, dos, lss, dls, dqs, dks, dvs, bias_ref = refs[-13:]
        _to_classes(q_ref, qs, d, L, scale)
        for src, dst in ((k_ref, ks), (v_ref, vs), (do_ref, dos), (lse_ref, lss), (dl_ref, dls)):
            _to_classes(src, dst, d, L)
        _dil_bias(bias_ref, sl_ref, pl.program_id(1), t, kw, back, window)
        dks[...] = jnp.zeros_like(dks)
        dvs[...] = jnp.zeros_like(dvs)
        lo = _lane_lo((t, LANE))

        def stats(ref, rows):
            x = ref[rows, :]
            return jnp.concatenate([jnp.max(jnp.where(lo, x, NEG), axis=-1, keepdims=True),
                                    jnp.max(jnp.where(lo, NEG, x), axis=-1, keepdims=True)], axis=0)

        def block(g, _):
            qb = g % nq if d > 1 else g
            row0 = pl.multiple_of(g * t, t)
            rows = pl.ds(row0, t)
            early = qb * t < back
            keys = pl.ds(pl.multiple_of(jnp.where(early, row0 - qb * t, row0 - back), t), kw)
            q2 = _stack_heads(qs[rows, :], lo)
            do2 = _stack_heads(dos[rows, :], lo)
            kt = ks[keys, :]
            s = _mm_nt(q2, kt) + bias_ref[jnp.where(early, 0, nbias - 1)]
            p = jnp.exp(s - stats(lss, rows))
            ds = (p * (_mm_nt(do2, vs[keys, :]) - stats(dls, rows))).astype(BF16)
            dq2 = _mm(ds, kt) * scale
            dqs[rows, :] = jnp.where(lo, dq2[:t], dq2[t:])
            dks[keys, :] += _mm_tn(ds, q2)
            dvs[keys, :] += _mm_tn(p, do2)
            return 0

        lax.fori_loop(0, d * nq, block, 0, unroll=DIL_UNROLL if d * nq % DIL_UNROLL == 0 else 1)
        for src, dst in ((dqs, dq_ref), (dks, dk_ref), (dvs, dv_ref)):
            _from_classes(src, dst, d, L)

    in_specs = [pl.BlockSpec(memory_space=pltpu.SMEM), qk, qk, vspec, pair, pair, pair]
    args = [slopes, qn, kn, proj, do, lse, delta]
    aliases = {}
    if through is not None:
        aliases = {len(args) + i: i for i in range(3)}
        in_specs = in_specs + [pl.BlockSpec(memory_space=pl.ANY)] * 3
        args = args + list(through)
    return pl.pallas_call(
        body, name=f"dil_attn_bwd_{gi}", grid=(B, NPAIR), in_specs=in_specs, out_specs=[qk, qk, qk],
        out_shape=[jax.ShapeDtypeStruct((B * S, DIL_QK), F32)] * 3,
        scratch_shapes=[pltpu.VMEM((S, LANE), BF16)] * 4 + [pltpu.VMEM((S, LANE), F32)] * 5
                       + [pltpu.VMEM((nbias, 2 * t, kw), F32)],
        input_output_aliases=aliases,
        compiler_params=_cp(),
    )(*args)


def _merge_common(p_ref, bg_ref, ob_ref, og_refs, lse_refs):
    bz = p_ref[:, CB_BZ * LANE:(CB_BZ + 4) * LANE].astype(F32)
    cz = p_ref[:, CB_CZ * LANE:(CB_CZ + 4) * LANE].astype(F32)
    gates = [_sigmoid(p_ref[:, (CB_GATE + 8 * i) * LANE:(CB_GATE + 8 * i + 8) * LANE].astype(F32)
                      + bg_ref[:, i * D_MODEL:(i + 1) * D_MODEL]) for i in range(3)]
    ob = ob_ref[...]
    lses = [r[...] for r in lse_refs]
    mx = jnp.maximum(jnp.maximum(lses[0], lses[1]), lses[2])
    es = [jnp.exp(v - mx) for v in lses]
    inv = 1.0 / (es[0] + es[1] + es[2])
    alphas = [e * inv for e in es]
    oc = alphas[0] * og_refs[0][...] + alphas[1] * og_refs[1][...] + alphas[2] * og_refs[2][...]
    return bz, cz, gates, ob, alphas, oc


def _merge_fwd(x, proj, b_gate, ya, ob, ogs, lses, woa, wob, woc, wo):
    T = x.shape[0]
    ts = _tile(T, 256)
    MW = 32 * LANE

    def body(x_ref, p_ref, bg_ref, ya_ref, ob_ref, og0, og1, og2, l0, l1, l2, woa_ref, wob_ref, woc_ref, wo_ref, out_ref):
        bz, cz, gates, obv, alphas, oc = _merge_common(p_ref, bg_ref, ob_ref, (og0, og1, og2), (l0, l1, l2))
        yb = obv * _silu(bz)
        yc = oc * _silu(cz)
        merged = (gates[0] * _mm(ya_ref[...], woa_ref[...]) + gates[1] * _mm(yb, wob_ref[...])
                  + gates[2] * _mm(yc, woc_ref[...]))
        out_ref[...] = x_ref[...] + _mm(merged, wo_ref[...])

    def whole(r, c):
        return pl.BlockSpec((r, c), lambda i: (0, 0))

    tok = lambda w: pl.BlockSpec((ts, w), lambda i: (i, 0))
    return pl.pallas_call(
        body, name="merge_fwd", grid=(T // ts,),
        in_specs=[tok(D_MODEL), tok(MW), whole(1, 3 * D_MODEL), tok(CONV_WIDTH)] + [tok(DIL_WIDTH)] * 7
                 + [whole(CONV_WIDTH, D_MODEL)] * 3 + [whole(D_MODEL, D_MODEL)],
        out_specs=tok(D_MODEL),
        out_shape=jax.ShapeDtypeStruct((T, D_MODEL), F32),
        compiler_params=_cp(),
    )(x, proj, b_gate, ya, ob, *ogs, *lses, woa, wob, woc, wo)


def _merge_bwd(dout, proj, b_gate, ya, ob, ogs, lses, woa, wob, woc, wo):
    T = dout.shape[0]
    ts = _tile(T, 256)
    MW = 32 * LANE

    def body(do_ref, p_ref, bg_ref, ya_ref, ob_ref, og0, og1, og2, l0, l1, l2, woa_ref, wob_ref, woc_ref, wo_ref,
             dp_ref, dya_ref, dob_ref, dlb_ref, dg0, dg1, dg2, dl0, dl1, dl2,
             mg_ref, dpa_ref, dpb_ref, dpc_ref, yb_ref, yc_ref, dbg_ref):
        bz, cz, gates, obv, alphas, oc = _merge_common(p_ref, bg_ref, ob_ref, (og0, og1, og2), (l0, l1, l2))
        sb, sc = _silu(bz), _silu(cz)
        yb = obv * sb
        yc = oc * sc
        ps = [_mm(ya_ref[...], woa_ref[...]), _mm(yb, wob_ref[...]), _mm(yc, woc_ref[...])]
        mg_ref[...] = (gates[0] * ps[0] + gates[1] * ps[1] + gates[2] * ps[2]).astype(BF16)
        yb_ref[...] = yb.astype(BF16)
        yc_ref[...] = yc.astype(BF16)
        dm = _mm_nt(do_ref[...], wo_ref[...])
        dps = []
        first = pl.program_id(0) == 0
        for i, dref in enumerate((dpa_ref, dpb_ref, dpc_ref)):
            g = gates[i]
            dpi = (dm * g).astype(BF16)
            dref[...] = dpi
            dps.append(dpi)
            dgp = dm * ps[i] * g * (1.0 - g)
            dp_ref[:, (CB_GATE + 8 * i) * LANE:(CB_GATE + 8 * i + 8) * LANE] = dgp.astype(BF16)
            part = jnp.sum(dgp, axis=0, keepdims=True)

            @pl.when(first)
            def _():
                dbg_ref[:, i * D_MODEL:(i + 1) * D_MODEL] = part

            @pl.when(jnp.logical_not(first))
            def _():
                dbg_ref[:, i * D_MODEL:(i + 1) * D_MODEL] += part

        dya_ref[...] = _mm_nt(dps[0], woa_ref[...])
        dyb = _mm_nt(dps[1], wob_ref[...])
        dyc = _mm_nt(dps[2], woc_ref[...])
        dp_ref[:, CB_BZ * LANE:(CB_BZ + 4) * LANE] = (dyb * obv * _dsilu(bz)).astype(BF16)
        dp_ref[:, CB_CZ * LANE:(CB_CZ + 4) * LANE] = (dyc * oc * _dsilu(cz)).astype(BF16)
        dob = dyb * sb
        doc = dyc * sc
        dob_ref[...] = dob.astype(BF16)
        for c in range(NPAIR):
            cs = slice(c * LANE, (c + 1) * LANE)
            dlb_ref[:, cs] = _head_bcast_sum(dob[:, cs] * obv[:, cs])
            dd = _head_bcast_sum(doc[:, cs] * oc[:, cs])
            for a, dref, lref in zip(alphas, (dg0, dg1, dg2), (dl0, dl1, dl2)):
                dref[:, cs] = a[:, cs] * doc[:, cs]
                lref[:, cs] = a[:, cs] * dd

    def whole(r, c):
        return pl.BlockSpec((r, c), lambda i: (0, 0))

    tok = lambda w: pl.BlockSpec((ts, w), lambda i: (i, 0))
    sd = jax.ShapeDtypeStruct
    W = DIL_WIDTH
    return pl.pallas_call(
        body, name="merge_bwd", grid=(T // ts,),
        in_specs=[tok(D_MODEL), tok(MW), whole(1, 3 * D_MODEL), tok(CONV_WIDTH)] + [tok(W)] * 7
                 + [whole(CONV_WIDTH, D_MODEL)] * 3 + [whole(D_MODEL, D_MODEL)],
        out_specs=[tok(MW), tok(CONV_WIDTH), tok(W), tok(W)] + [tok(W)] * 6
                  + [tok(D_MODEL)] * 4 + [tok(W), tok(W), whole(1, 3 * D_MODEL)],
        out_shape=[sd((T, PP), BF16), sd((T, CONV_WIDTH), F32), sd((T, W), BF16), sd((T, W), F32)]
                  + [sd((T, W), F32)] * 6
                  + [sd((T, D_MODEL), BF16)] * 4 + [sd((T, W), BF16)] * 2 + [sd((1, 3 * D_MODEL), F32)],
        compiler_params=_cp(),
    )(dout, proj, b_gate, ya, ob, *ogs, *lses, woa, wob, woc, wo)


def _loss_head(y, target):
    T = y.shape[0]
    ts = _tile(T, 512)

    def body(y_ref, t_ref, d_ref, l_ref):
        e = y_ref[...] - t_ref[...]
        d_ref[...] = e * (1.0 / D_MODEL)
        l_ref[...] = jnp.zeros((1, 8, LANE), F32) + jnp.sum(e * e)

    tok = pl.BlockSpec((ts, D_MODEL), lambda i: (i, 0))
    return pl.pallas_call(
        body, name="loss_head", grid=(T // ts,), in_specs=[tok, tok],
        out_specs=[tok, pl.BlockSpec((1, 8, LANE), lambda i: (i, 0, 0))],
        out_shape=[jax.ShapeDtypeStruct((T, D_MODEL), F32), jax.ShapeDtypeStruct((T // ts, 8, LANE), F32)],
        compiler_params=_cp(),
    )(y, target)


def _my_index():
    return 4 * lax.axis_index("x") + 2 * lax.axis_index("y") + lax.axis_index("c")


def _peers():
    x, y, c = (lax.axis_index(a) for a in AXES)
    out = []
    for kk in range(1, N_DEV):
        px = 1 - x if kk & 4 else x
        py = 1 - y if kk & 2 else y
        pc = 1 - c if kk & 1 else c
        out.append(((px, py, pc), 4 * px + 2 * py + pc))
    return out


def _exchange(arrays, name, gather):
    n = len(arrays)

    def body(*refs):
        srcs, outs = refs[:n], refs[n:2 * n]
        send_sems, recv_sems, local_sems = refs[2 * n:]
        me = _my_index()
        peers = _peers()
        started = []
        for a, (src, out) in enumerate(zip(srcs, outs)):
            mine = pltpu.make_async_copy(src if gather else src.at[me], out.at[me], local_sems.at[a])
            mine.start()
            started.append(mine)
        sends = []
        for i, (pos, idx) in enumerate(peers):
            for a, (src, out) in enumerate(zip(srcs, outs)):
                cp = pltpu.make_async_remote_copy(
                    src_ref=src if gather else src.at[idx], dst_ref=out.at[me], send_sem=send_sems.at[a, i],
                    recv_sem=recv_sems.at[a, i], device_id=pos, device_id_type=pl.DeviceIdType.MESH)
                cp.start()
                sends.append(cp)
        for i, (pos, idx) in enumerate(peers):
            for a, (src, out) in enumerate(zip(srcs, outs)):
                pltpu.make_async_remote_copy(
                    src_ref=src if gather else src.at[idx], dst_ref=out.at[idx], send_sem=send_sems.at[a, i],
                    recv_sem=recv_sems.at[a, i], device_id=pos, device_id_type=pl.DeviceIdType.MESH).wait_recv()
        for cp in sends:
            cp.wait_send()
        for mine in started:
            mine.wait()

    any_space = pl.BlockSpec(memory_space=pl.ANY)
    return pl.pallas_call(
        body, name=name, in_specs=[any_space] * n, out_specs=[any_space] * n,
        out_shape=[jax.ShapeDtypeStruct(((N_DEV,) + a.shape) if gather else a.shape, a.dtype) for a in arrays],
        scratch_shapes=[pltpu.SemaphoreType.DMA((n, N_DEV - 1)), pltpu.SemaphoreType.DMA((n, N_DEV - 1)),
                        pltpu.SemaphoreType.DMA((n,))],
    )(*arrays)


N_CHIP = 4


def _chip_places():
    x, y, c = (lax.axis_index(a) for a in AXES)
    return (x, y, c), (x, y, 1 - c), [(1 - x, y, c), (x, 1 - y, c), (1 - x, 1 - y, c)]


def _index_of(pos):
    return 4 * pos[0] + 2 * pos[1] + pos[2]


def _gather_two_level(arrays, name):
    n = len(arrays)

    def body(*refs):
        srcs, outs = refs[:n], refs[n:2 * n]
        send_sems, recv_sems, local_sems = refs[2 * n:]
        me, sibling, others = _chip_places()

        def copy(a, k, block, to, src=None):
            slot = outs[a].at[_index_of(block)]
            return pltpu.make_async_remote_copy(
                src_ref=slot if src is None else src, dst_ref=slot, send_sem=send_sems.at[7 * a + k],
                recv_sem=recv_sems.at[7 * a + k], device_id=to, device_id_type=pl.DeviceIdType.MESH)

        started = []
        for a, src in enumerate(srcs):
            mine = pltpu.make_async_copy(src, outs[a].at[_index_of(me)], local_sems.at[a])
            mine.start()
            started.append(mine)
        sends = []
        for a, src in enumerate(srcs):
            sends.append(copy(a, 0, me, sibling, src))
            sends += [copy(a, 1 + j, me, chip, src) for j, chip in enumerate(others)]
        for cp in sends:
            cp.start()
        for j, chip in enumerate(others):
            for a in range(n):
                copy(a, 1 + j, chip, me).wait_recv()
                fwd = copy(a, 4 + j, chip, sibling)
                fwd.start()
                sends.append(fwd)
        for a in range(n):
            copy(a, 0, sibling, me).wait_recv()
            for j, chip in enumerate(others):
                copy(a, 4 + j, (chip[0], chip[1], sibling[2]), me).wait_recv()
        for cp in sends:
            cp.wait_send()
        for mine in started:
            mine.wait()

    any_space = pl.BlockSpec(memory_space=pl.ANY)
    return pl.pallas_call(
        body, name=name, in_specs=[any_space] * n, out_specs=[any_space] * n,
        out_shape=[jax.ShapeDtypeStruct((N_DEV,) + a.shape, a.dtype) for a in arrays],
        scratch_shapes=[pltpu.SemaphoreType.DMA((7 * n,)), pltpu.SemaphoreType.DMA((7 * n,)),
                        pltpu.SemaphoreType.DMA((n,))],
    )(*arrays)


def _sibling_swap(arrays, name):
    n = len(arrays)

    def body(*refs):
        srcs, outs = refs[:n], refs[n:2 * n]
        send_sems, recv_sems = refs[2 * n:]
        (x, y, c), sibling, _ = _chip_places()
        sends = []
        for a, (src, out) in enumerate(zip(srcs, outs)):
            for q in range(N_CHIP):
                def copy(core, a=a, q=q, src=src, out=out):
                    return pltpu.make_async_remote_copy(
                        src_ref=src.at[2 * q + core], dst_ref=out.at[q], send_sem=send_sems.at[N_CHIP * a + q],
                        recv_sem=recv_sems.at[N_CHIP * a + q], device_id=sibling, device_id_type=pl.DeviceIdType.MESH)
                mine = copy(1 - c)
                mine.start()
                sends.append((mine, copy(c)))
        for mine, arrival in sends:
            arrival.wait_recv()
            mine.wait_send()

    any_space = pl.BlockSpec(memory_space=pl.ANY)
    return pl.pallas_call(
        body, name=name, in_specs=[any_space] * n, out_specs=[any_space] * n,
        out_shape=[jax.ShapeDtypeStruct((N_CHIP,) + a.shape[1:], a.dtype) for a in arrays],
        scratch_shapes=[pltpu.SemaphoreType.DMA((N_CHIP * n,)), pltpu.SemaphoreType.DMA((N_CHIP * n,))],
    )(*arrays)


def _chip_pair_sum(part, got, name):
    R, C = part.shape[1:]
    tr = R
    while tr * C * part.dtype.itemsize > REDUCE_BLOCK_BYTES // 4 and tr % 32 == 0:
        tr //= 2
    c = lax.axis_index("c")

    def body(c_ref, p_ref, g_ref, o_ref):
        del c_ref
        o_ref[...] = (p_ref[...].astype(F32) + g_ref[...].astype(F32)).astype(o_ref.dtype)

    return pl.pallas_call(
        body, name=name, grid_spec=pltpu.PrefetchScalarGridSpec(
            num_scalar_prefetch=1, grid=(N_CHIP, R // tr),
            in_specs=[pl.BlockSpec((None, tr, C), lambda q, i, cr: (2 * q + cr[0], i, 0)),
                      pl.BlockSpec((None, tr, C), lambda q, i, cr: (q, i, 0))],
            out_specs=pl.BlockSpec((None, tr, C), lambda q, i, cr: (q, i, 0))),
        out_shape=jax.ShapeDtypeStruct((N_CHIP, R, C), part.dtype),
        compiler_params=_cp(),
    )(jnp.reshape(c, (1,)).astype(jnp.int32), part, got)


def _chip_exchange(arrays, name):
    n = len(arrays)

    def body(*refs):
        srcs, outs = refs[:n], refs[n:2 * n]
        send_sems, recv_sems, local_sems = refs[2 * n:]
        (x, y, c), _, others = _chip_places()
        mychip = 2 * x + y
        started, sends = [], []
        for a, (src, out) in enumerate(zip(srcs, outs)):
            mine = pltpu.make_async_copy(src.at[mychip], out.at[mychip], local_sems.at[a])
            mine.start()
            started.append(mine)
        for j, chip in enumerate(others):
            q = 2 * chip[0] + chip[1]
            for a, (src, out) in enumerate(zip(srcs, outs)):
                def copy(slot, a=a, j=j, q=q, chip=chip, src=src, out=out):
                    return pltpu.make_async_remote_copy(
                        src_ref=src.at[q], dst_ref=out.at[slot], send_sem=send_sems.at[3 * a + j],
                        recv_sem=recv_sems.at[3 * a + j], device_id=chip, device_id_type=pl.DeviceIdType.MESH)
                mine = copy(mychip)
                mine.start()
                sends.append((mine, copy(q)))
        for mine, arrival in sends:
            arrival.wait_recv()
        for mine, arrival in sends:
            mine.wait_send()
        for mine in started:
            mine.wait()

    any_space = pl.BlockSpec(memory_space=pl.ANY)
    return pl.pallas_call(
        body, name=name, in_specs=[any_space] * n, out_specs=[any_space] * n,
        out_shape=[jax.ShapeDtypeStruct(a.shape, a.dtype) for a in arrays],
        scratch_shapes=[pltpu.SemaphoreType.DMA((3 * n,)), pltpu.SemaphoreType.DMA((3 * n,)),
                        pltpu.SemaphoreType.DMA((n,))],
    )(*arrays)


def _remote_copies(srcs, lands, send_sems, recv_sems, gather):
    me = _my_index()
    out = []
    for i, (pos, idx) in enumerate(_peers()):
        for a, (src, land) in enumerate(zip(srcs, lands)):
            def copy(slot, a=a, src=src, land=land, i=i, pos=pos, idx=idx):
                return pltpu.make_async_remote_copy(
                    src_ref=src if gather else src.at[idx], dst_ref=land.at[slot],
                    send_sem=send_sems.at[a * (N_DEV - 1) + i], recv_sem=recv_sems.at[a * (N_DEV - 1) + i],
                    device_id=pos, device_id_type=pl.DeviceIdType.MESH)
            out.append((copy(me), copy(idx)))
    return out


def _exchange_start(arrays, name, gather):
    n = len(arrays)
    hbm = pl.BlockSpec(memory_space=pltpu.HBM)
    sem = pl.BlockSpec(memory_space=pltpu.SEMAPHORE)
    lands = [lax.empty(((N_DEV,) + a.shape) if gather else a.shape, a.dtype) for a in arrays]

    def body(*refs):
        srcs, lands_ = refs[:n], refs[n:2 * n]
        send_sems, recv_sems = refs[2 * n:2 * n + 2]
        for mine, _ in _remote_copies(srcs, lands_, send_sems, recv_sems, gather):
            mine.start()
        refs[-1][...] = jnp.zeros_like(refs[-1])

    sems = pltpu.SemaphoreType.DMA((n * (N_DEV - 1),))
    buffers = [pltpu.HBM(a.shape, a.dtype) for a in list(arrays) + lands]
    res = pl.pallas_call(
        body, name=name, in_specs=[hbm] * (2 * n), out_specs=[sem, sem] + [hbm] * (2 * n) + [pl.BlockSpec(memory_space=pltpu.VMEM)],
        out_shape=[sems, sems] + buffers + [jax.ShapeDtypeStruct((8, LANE), F32)],
        input_output_aliases={i: 2 + i for i in range(2 * n)},
        compiler_params=pltpu.CompilerParams(has_side_effects=pltpu.SideEffectType.DATAFLOW_SIDE_EFFECTING),
    )(*[pltpu.with_memory_space_constraint(a, pltpu.HBM) for a in list(arrays) + lands])
    return (res[0], res[1], res[2:2 + n], res[2 + n:2 + 2 * n]), res[-1]


def _exchange_wait(handle, after, name, gather):
    send_sems, recv_sems, srcs, lands = handle
    n = len(srcs)
    hbm = pl.BlockSpec(memory_space=pltpu.HBM)
    sem = pl.BlockSpec(memory_space=pltpu.SEMAPHORE)

    def body(*refs):
        for mine, arrival in _remote_copies(refs[:n], refs[n:2 * n], refs[2 * n], refs[2 * n + 1], gather):
            mine.wait_send()
            arrival.wait_recv()

    res = pl.pallas_call(
        body, name=name, in_specs=[hbm] * (2 * n) + [sem, sem, pl.BlockSpec(memory_space=pl.ANY)],
        out_specs=[hbm] * (2 * n), out_shape=[pltpu.HBM(a.shape, a.dtype) for a in list(srcs) + list(lands)],
        input_output_aliases={i: i for i in range(2 * n)},
        compiler_params=pltpu.CompilerParams(has_side_effects=pltpu.SideEffectType.DATAFLOW_SIDE_EFFECTING),
    )(*srcs, *lands, send_sems, recv_sems, after)
    return res[n:]


def _own_slot(land, mine):
    return lax.dynamic_update_slice(land, mine, (_my_index(),) + (0,) * (land.ndim - 1))


def _adamw(w, g, m, v):
    m = ADAM_B1 * m + (1.0 - ADAM_B1) * g
    v = ADAM_B2 * v + (1.0 - ADAM_B2) * (g * g)
    m_hat = m / (1.0 - ADAM_B1 ** ADAM_STEP)
    v_hat = v / (1.0 - ADAM_B2 ** ADAM_STEP)
    delta = -ADAM_LR * (m_hat / (jnp.sqrt(v_hat) + ADAM_EPS) + ADAM_WD * w)
    return delta, m, v


def _reduce_adamw(parts, w, m, v, name):
    nparts = len(parts)
    R, C = parts[0].shape[1:]
    tr = R
    while N_DEV * tr * C * parts[0].dtype.itemsize > REDUCE_BLOCK_BYTES and tr % 32 == 0:
        tr //= 2
    steps = R // tr

    def body(*refs):
        w_ref, m_ref, v_ref, g_ref, d_ref, nm_ref, nv_ref = refs[nparts:]
        for k, p_ref in enumerate(refs[:nparts]):
            @pl.when(pl.program_id(0) // steps == k)
            def _():
                g = p_ref[0].astype(F32)
                for s in range(1, p_ref.shape[0]):
                    g = g + p_ref[s].astype(F32)
                g_ref[...] = g
                d_ref[...], nm_ref[...], nv_ref[...] = _adamw(w_ref[...], g, m_ref[...], v_ref[...])

    def part_spec(k):
        return pl.BlockSpec((parts[k].shape[0], tr, C), lambda i: (0, jnp.clip(i - k * steps, 0, steps - 1), 0))

    row = pl.BlockSpec((tr, C), lambda i: (i, 0))
    return pl.pallas_call(
        body, name=name, grid=(nparts * steps,),
        in_specs=[part_spec(k) for k in range(nparts)] + [row, row, row],
        out_specs=[row] * 4, out_shape=[jax.ShapeDtypeStruct((nparts * R, C), F32)] * 4,
        compiler_params=_cp(),
    )(*parts, w, m, v)


BIG = ("w_in", "w_uq", "w_ukv", "w_out_a", "w_out_b", "w_out_c", "w_o")
SMALL = ("norm_g", "b_gate", "conv_w", "conv_b", "q_a_norm_g", "kv_a_norm_g", "mla_q_norm_g", "mla_k_norm_g",
         "dil_q_norm_g", "dil_k_norm_g")
PACK_ROWS = 128
REDUCE_BLOCK_BYTES = 6 * 1024 * 1024


def _pack_local(tensors):
    flat = jnp.concatenate([t.reshape(-1) for t in tensors])
    pad = (-flat.shape[0]) % (PACK_ROWS * LANE)
    return jnp.concatenate([flat, jnp.zeros((pad,), flat.dtype)]).reshape(-1, LANE)


def _unpack_local(rows, like):
    flat = rows.reshape(-1)
    out, off = [], 0
    for t in like:
        out.append(flat[off:off + t.size].reshape(t.shape))
        off += t.size
    return out


def _cols_to_slots(a):
    k = a.shape[0]
    return a.reshape(k, N_DEV, -1).transpose(1, 0, 2)


def _slots_to_cols(s):
    return s.transpose(1, 0, 2).reshape(s.shape[1], -1)


def _rope_tables(S):
    inv = ROPE_THETA ** (-jnp.arange(0, MLA_ROPE, 2, dtype=F32) / MLA_ROPE)
    ang = jnp.arange(S, dtype=F32)[:, None] * inv[None, :]
    cos, sin = jnp.cos(ang), jnp.sin(ang)
    one = jnp.ones((S, MLA_NOPE), F32)
    z16, z32, z64 = (jnp.zeros((S, n), F32) for n in (16, 32, 64))
    cosp = jnp.concatenate([one, cos, cos, jnp.ones((S, 32), F32)], axis=1)
    sa = jnp.concatenate([z64, -sin, z16, z32], axis=1)
    sb = jnp.concatenate([z64, z16, sin, z32], axis=1)
    return cosp, sa, sb


def _alibi_slopes():
    n = DIL_GROUPS * DIL_HEADS
    m = 2.0 ** (-8.0 * jnp.arange(1, n + 1, dtype=F32) / n)
    return m.reshape(DIL_GROUPS, NPAIR, 2)


def _pad_slots(s):
    n, k, c = s.shape
    return _slots_to_cols(jnp.concatenate([s, jnp.zeros((n, k, LANE - c), s.dtype)], axis=2))


def _layer_params(gw, small, l):
    p = {}
    p["wp"] = _pad_columns(gw["w_in"])
    p["norm_g"] = small["norm_g"][l][None]
    p["b_gate"] = small["b_gate"][l][None]
    p["conv_w"] = gw["conv_w"].transpose(1, 0, 2).reshape(CONV_K, CONV_WIDTH)
    p["conv_b"] = small["conv_b"][l][None]
    p["gq"] = small["q_a_norm_g"][l][None]
    p["gkv"] = small["kv_a_norm_g"][l][None]
    p["wuqp"] = _pad_slots(gw["w_uq"])
    kv = gw["w_ukv"]
    p["wkp"] = _pad_slots(kv[:, :, :MLA_NOPE])
    p["wv"] = kv[:, :, MLA_NOPE:].transpose(1, 0, 2).reshape(MLA_KV_LORA, MLA_HEADS * MLA_V)
    zpad = jnp.zeros((1, LANE - MLA_QK), F32)
    p["gmq"] = jnp.concatenate([small["mla_q_norm_g"][l][None], zpad], axis=1)
    p["gmk"] = jnp.concatenate([small["mla_k_norm_g"][l][None], zpad], axis=1)
    tile = lambda g: jnp.broadcast_to(g[:, None, :], (DIL_GROUPS, DIL_HEADS, DIL_HEAD_DIM)).reshape(1, DIL_QK)
    p["gdq"] = tile(small["dil_q_norm_g"][l])
    p["gdk"] = tile(small["dil_k_norm_g"][l])
    p["woa"], p["wob"], p["woc"] = (_slots_to_cols(gw[n]) for n in ("w_out_a", "w_out_b", "w_out_c"))
    p["wo"] = gw["w_o"].reshape(D_MODEL, D_MODEL)
    return p


def _layer_fwd(x, p, tabs, slopes, B, S):
    proj, ht = _inproj_fwd(x, p["norm_g"], p["wp"])
    ya = _mixa_fwd(proj, p["conv_w"], p["conv_b"], B, S)
    q, k, v = _mla_prep_fwd(proj, p["gq"], p["gkv"], p["wuqp"], p["wkp"], p["wv"], p["gmq"], p["gmk"], *tabs, S)
    ob, lse_b = _mla_attn_fwd(q, k, v, B, S)
    qn, kn, vn = _dil_prep_fwd(proj, p["gdq"], p["gdk"])
    ogs, lses = [], []
    for gi in range(DIL_GROUPS):
        o, lse = _dil_attn_fwd(gi, slopes[gi], qn, kn, vn, B, S)
        ogs.append(o)
        lses.append(lse)
    out = _merge_fwd(x, proj, p["b_gate"], ya, ob, ogs, lses, p["woa"], p["wob"], p["woc"], p["wo"])
    saved = dict(x=x, proj=proj, ht=ht, ya=ya, q=q, k=k, v=v, ob=ob, lse_b=lse_b, qn=qn, kn=kn, vn=vn, ogs=ogs, lses=lses)
    return out, saved


def _layer_bwd(dout, sv, p, tabs, slopes, B, S):
    proj = sv["proj"]
    (dproj, dya, dob, dlb, dg0, dg1, dg2, dl0, dl1, dl2, merged, dpa, dpb, dpc, yb, yc, dbg) = _merge_bwd(
        dout, proj, p["b_gate"], sv["ya"], sv["ob"], sv["ogs"], sv["lses"], p["woa"], p["wob"], p["woc"], p["wo"])
    g = {}
    g["w_o"] = _matmul_tn(merged, dout, "dw_o").reshape(N_DEV, D_MODEL // N_DEV, D_MODEL)
    g["w_out_a"] = _cols_to_slots(_matmul_tn(sv["ya"], dpa, "dw_out_a"))
    g["w_out_b"] = _cols_to_slots(_matmul_tn(yb, dpb, "dw_out_b"))
    g["w_out_c"] = _cols_to_slots(_matmul_tn(yc, dpc, "dw_out_c"))
    g["b_gate"] = dbg[0]
    dproj, st = _mixa_bwd(dproj, dya, proj, p["conv_w"], p["conv_b"], B, S)
    g["conv_w"] = st[0:CONV_K]
    g["conv_b"] = st[CONV_K]
    dq, dk, dv = _mla_attn_bwd(sv["q"], sv["k"], sv["v"], dob, sv["lse_b"], dlb, B, S)
    dproj, dwuqp, dwkp, dwv, dgq, dgkv, dgmq, dgmk = _mla_prep_bwd(
        dproj, dq, dk, dv, proj, p["gq"], p["gkv"], p["wuqp"], p["wkp"], p["wv"], p["gmq"], p["gmk"], *tabs, S)
    g["w_uq"] = _cols_to_slots(dwuqp)[:, :, :MLA_QK]
    g["w_ukv"] = jnp.concatenate([_cols_to_slots(dwkp)[:, :, :MLA_NOPE], _cols_to_slots(dwv)], axis=2)
    g["q_a_norm_g"], g["kv_a_norm_g"] = dgq[0], dgkv[0]
    g["mla_q_norm_g"], g["mla_k_norm_g"] = dgmq[0, :MLA_QK], dgmk[0, :MLA_QK]
    dqkv = None
    for gi, (dog, dlg) in enumerate(((dg0, dl0), (dg1, dl1), (dg2, dl2))):
        dqkv = _dil_attn_bwd(gi, slopes[gi], sv["qn"], sv["kn"], sv["vn"], dog, sv["lses"][gi], dlg, dqkv, B, S)
    dproj, dgdq, dgdk = _dil_prep_bwd(dproj, *dqkv, proj, p["gdq"], p["gdk"])
    g["dil_q_norm_g"] = dgdq.reshape(DIL_GROUPS, DIL_HEADS, DIL_HEAD_DIM).sum(axis=1)
    g["dil_k_norm_g"] = dgdk.reshape(DIL_GROUPS, DIL_HEADS, DIL_HEAD_DIM).sum(axis=1)
    g["w_in"] = _unpad_columns(_matmul_nn(sv["ht"], dproj, "dw_in"))
    dx, dng = _inproj_bwd_x(dproj, p["wp"], sv["x"], p["norm_g"], dout)
    g["norm_g"] = dng[0]
    return dx, g


def _after(token, a):
    return a if token is None else a + token[0:1, 0:1]


def _local_step(x, target, small, B, S, weights_of, grads_out):
    tabs = _rope_tables(S)
    sl = _alibi_slopes()
    slopes = [sl[gi] * float(DIL_PATTERNS[gi][1]) for gi in range(DIL_GROUPS)]
    params, saved = [], []
    for l in range(DEPTH):
        gw, token = weights_of(l, x)
        p = _layer_params(gw, small, l)
        p["norm_g"] = _after(token, p["norm_g"])
        x, sv = _layer_fwd(x, p, tabs, slopes, B, S)
        params.append(p)
        saved.append(sv)
    dout, lparts = _loss_head(x, target)
    sq = jnp.sum(lparts[:, 0, 0])
    token = None
    for l in reversed(range(DEPTH)):
        p = dict(params[l], b_gate=_after(token, params[l]["b_gate"]))
        dout, g = _layer_bwd(dout, saved[l], p, tabs, slopes, B, S)
        token = grads_out(l, g, dout)
    return sq, dout


def kernel(x, norm_g, w_in, b_gate, conv_w, conv_b, q_a_norm_g, w_uq, kv_a_norm_g, w_ukv, mla_q_norm_g, mla_k_norm_g, dil_q_norm_g, dil_k_norm_g, w_out_a, w_out_b, w_out_c, w_o, loss_target, m_norm_g, m_w_in, m_b_gate, m_conv_w, m_conv_b, m_q_a_norm_g, m_w_uq, m_kv_a_norm_g, m_w_ukv, m_mla_q_norm_g, m_mla_k_norm_g, m_dil_q_norm_g, m_dil_k_norm_g, m_w_out_a, m_w_out_b, m_w_out_c, m_w_o, v_norm_g, v_w_in, v_b_gate, v_conv_w, v_conv_b, v_q_a_norm_g, v_w_uq, v_kv_a_norm_g, v_w_ukv, v_mla_q_norm_g, v_mla_k_norm_g, v_dil_q_norm_g, v_dil_k_norm_g, v_w_out_a, v_w_out_b, v_w_out_c, v_w_o):
    names = ("norm_g", "w_in", "b_gate", "conv_w", "conv_b", "q_a_norm_g", "w_uq", "kv_a_norm_g", "w_ukv",
             "mla_q_norm_g", "mla_k_norm_g", "dil_q_norm_g", "dil_k_norm_g", "w_out_a", "w_out_b", "w_out_c", "w_o")
    w = dict(zip(names, (norm_g, w_in, b_gate, conv_w, conv_b, q_a_norm_g, w_uq, kv_a_norm_g, w_ukv, mla_q_norm_g,
                         mla_k_norm_g, dil_q_norm_g, dil_k_norm_g, w_out_a, w_out_b, w_out_c, w_o)))
    m = dict(zip(names, (m_norm_g, m_w_in, m_b_gate, m_conv_w, m_conv_b, m_q_a_norm_g, m_w_uq, m_kv_a_norm_g, m_w_ukv,
                         m_mla_q_norm_g, m_mla_k_norm_g, m_dil_q_norm_g, m_dil_k_norm_g, m_w_out_a, m_w_out_b,
                         m_w_out_c, m_w_o)))
    v = dict(zip(names, (v_norm_g, v_w_in, v_b_gate, v_conv_w, v_conv_b, v_q_a_norm_g, v_w_uq, v_kv_a_norm_g, v_w_ukv,
                         v_mla_q_norm_g, v_mla_k_norm_g, v_dil_q_norm_g, v_dil_k_norm_g, v_w_out_a, v_w_out_b,
                         v_w_out_c, v_w_o)))
    B, S, _ = x.shape
    me = _my_index()
    cshard = CONV_WIDTH // N_DEV

    shards = [[w[n][l].astype(BF16) for n in BIG] for l in range(DEPTH)]
    state = {}

    def weights_of(l, after):
        if l == 0:
            got = _gather_two_level(shards[0] + [conv_w], "all_gather_weights_0")
            state["gather"], token = _exchange_start(shards[1], "all_gather_weights_1_start", gather=True)
            state["conv_w"] = got[-1]
        else:
            landed = _exchange_wait(state["gather"], after, "all_gather_weights_1_wait", gather=True)
            got, token = [_own_slot(a, s[None]) for a, s in zip(landed, shards[1])], None
        gw = dict(zip(BIG, got))
        gw["conv_w"] = state["conv_w"][:, l]
        return gw, token

    recv, small_parts = {}, {}

    def grads_out(l, g, after):
        small_parts[l] = [g[n] for n in SMALL]
        send = [g[n].astype(BF16) for n in BIG]
        if l == DEPTH - 1:
            state["scatter"], token = _exchange_start(send, "exchange_weight_grads_1_start", gather=False)
            state["sent"] = send
            return token
        landed = _exchange_wait(state["scatter"], after, "exchange_weight_grads_1_wait", gather=False)
        mine = [lax.dynamic_slice_in_dim(s, me, 1, axis=0) for s in state["sent"]]
        recv[DEPTH - 1] = [_own_slot(a, s) for a, s in zip(landed, mine)]
        swapped = _sibling_swap(send, "exchange_weight_grads_0_sibling")
        sums = [_chip_pair_sum(s, t, "chip_pair_sum_" + n) for n, s, t in zip(BIG, send, swapped)]
        recv[l] = _chip_exchange(sums, "exchange_weight_grads_0")
        return None

    sq, grad_x = _local_step(x.reshape(B * S, D_MODEL), loss_target.reshape(B * S, D_MODEL), w, B, S,
                             weights_of, grads_out)
    loss = lax.psum(sq * (0.5 / D_MODEL), AXES)

    res = {}
    for i, n in enumerate(BIG):
        rows = lambda a: a.reshape(-1, a.shape[-1])
        outs = _reduce_adamw([recv[l][i] for l in range(DEPTH)], rows(w[n]), rows(m[n]), rows(v[n]),
                             "reduce_adamw_" + n)
        res[n] = tuple(a.reshape(w[n].shape) for a in outs)
    part = {n: jnp.stack([small_parts[l][i] for l in range(DEPTH)]) for i, n in enumerate(SMALL)}

    def widen(t):
        return lax.dynamic_update_slice(jnp.zeros((DEPTH, CONV_K, CONV_WIDTH), F32), t, (0, 0, me * cshard))

    small_like = [part[n] for n in SMALL]
    pick = lambda d: [widen(d[n]) if n == "conv_w" else d[n] for n in SMALL]
    parts, = _exchange([_pack_local(small_like)], "all_gather_small_grads", gather=True)
    gs, ds, ms, vs = _reduce_adamw([parts], _pack_local(pick(w)), _pack_local(pick(m)), _pack_local(pick(v)),
                                   "reduce_adamw_small")
    for n, t in zip(SMALL, zip(*(_unpack_local(a, small_like) for a in (gs, ds, ms, vs)))):
        if n == "conv_w":
            t = tuple(lax.dynamic_slice(a, (0, 0, me * cshard), (DEPTH, CONV_K, cshard)) for a in t)
        res[n] = t

    out = [loss, grad_x.reshape(B, S, D_MODEL)]
    for i in range(4):
        out += [res[n][i] for n in names]
    return tuple(out)
```

```python
import jax
import jax.numpy as jnp
from jax import lax
from jax.experimental import pallas as pl
from jax.experimental.pallas import tpu as pltpu

F32 = jnp.float32
BF16 = jnp.bfloat16

D_MODEL = 1024
DEPTH = 2
CONV_WIDTH = 512
CONV_K = 3
MLA_HEADS = 8
MLA_Q_LORA = 256
MLA_KV_LORA = 128
MLA_NOPE = 64
MLA_ROPE = 32
MLA_V = 64
MLA_QK = MLA_NOPE + MLA_ROPE
ROPE_THETA = 10000.0
DIL_PATTERNS = ((128, 1), (512, 4), (2048, 16))
DIL_GROUPS = 3
DIL_HEADS = 8
DIL_HEAD_DIM = 64
DIL_WIDTH = DIL_HEADS * DIL_HEAD_DIM
DIL_QK = DIL_GROUPS * DIL_WIDTH
EPS = 1e-6
N_IN = 11168

ADAM_LR = 0.001
ADAM_B1 = 0.9
ADAM_B2 = 0.999
ADAM_EPS = 1e-08
ADAM_WD = 0.01
ADAM_STEP = 10

N_DEV = 8
AXES = ("x", "y", "c")
LANE = 128
HALF = 64
NPAIR = 4

CB_BZ, CB_CZ, CB_GATE = 0, 4, 8
CB_A = 32
CB_QKV = 48
CB_CQ, CB_CKV, CB_KPE = 84, 86, 87
NCB = 88
PP = NCB * LANE
SHARD_COLS = N_IN // N_DEV
NEG = -1e30
VMEM_LIMIT = 56 * 1024 * 1024


def _column_chunks():
    out = []
    col = 0

    def seg(nblocks, block_of):
        nonlocal col
        for i in range(nblocks):
            out.append((col, LANE, block_of(i)))
            col += LANE

    seg(4, lambda j: CB_A + 4 * j)
    seg(4, lambda j: CB_A + 4 * j + 1)
    seg(4, lambda j: CB_A + 4 * j + 2)
    seg(4, lambda j: CB_A + 4 * j + 3)
    seg(2, lambda i: CB_CQ + i)
    seg(1, lambda i: CB_CKV)
    out.append((col, MLA_ROPE, CB_KPE))
    col += MLA_ROPE
    seg(4, lambda j: CB_BZ + j)
    seg(12, lambda c: CB_QKV + 3 * c)
    seg(12, lambda c: CB_QKV + 3 * c + 1)
    seg(12, lambda c: CB_QKV + 3 * c + 2)
    seg(4, lambda j: CB_CZ + j)
    seg(24, lambda i: CB_GATE + i)
    assert col == N_IN and sorted(c[2] for c in out) == list(range(NCB))
    return out


COLUMN_CHUNKS = _column_chunks()


def _pad_columns(shards):
    w = jnp.concatenate([shards[p] for p in range(N_DEV)], axis=1)
    cut = next(i for i, c in enumerate(COLUMN_CHUNKS) if c[1] < LANE) * LANE + MLA_ROPE
    u = jnp.concatenate([w[:, :cut], jnp.zeros((w.shape[0], LANE - MLA_ROPE), w.dtype), w[:, cut:]], axis=1)
    at = {b: i for i, (_, _, b) in enumerate(COLUMN_CHUNKS)}
    return jnp.concatenate([u[:, at[b] * LANE:(at[b] + 1) * LANE] for b in range(NCB)], axis=1)


def _unpad_columns(wp):
    u = jnp.concatenate([wp[:, b * LANE:(b + 1) * LANE] for _, _, b in COLUMN_CHUNKS], axis=1)
    cut = next(i for i, c in enumerate(COLUMN_CHUNKS) if c[1] < LANE) * LANE + MLA_ROPE
    w = jnp.concatenate([u[:, :cut], u[:, cut + LANE - MLA_ROPE:]], axis=1)
    return jnp.stack([w[:, p * SHARD_COLS:(p + 1) * SHARD_COLS] for p in range(N_DEV)])


def _cp():
    return pltpu.CompilerParams(vmem_limit_bytes=VMEM_LIMIT)


def _rstd(x, n):
    return lax.rsqrt(jnp.sum(x * x, axis=-1, keepdims=True) * (1.0 / n) + EPS)


def _sigmoid(z):
    return 1.0 / (1.0 + jnp.exp(-z))


def _silu(z):
    return z * _sigmoid(z)


def _dsilu(z):
    s = _sigmoid(z)
    return s * (1.0 + z * (1.0 - s))


def _mm(a, b):
    return jnp.dot(a.astype(BF16), b.astype(BF16), preferred_element_type=F32)


def _mm_nt(a, b):
    return lax.dot_general(a.astype(BF16), b.astype(BF16), (((1,), (1,)), ((), ())), preferred_element_type=F32)


def _mm_tn(a, b):
    return lax.dot_general(a.astype(BF16), b.astype(BF16), (((0,), (0,)), ((), ())), preferred_element_type=F32)


def _lane_lo(shape):
    return lax.broadcasted_iota(jnp.int32, shape, len(shape) - 1) < HALF


def _head_bcast_sum(x, terms=3):
    w = x.shape[-1]
    same = (lax.broadcasted_iota(jnp.int32, (w, w), 0) // HALF) == (lax.broadcasted_iota(jnp.int32, (w, w), 1) // HALF)
    ones = jnp.where(same, 1.0, 0.0).astype(jnp.bfloat16)
    total = None
    for _ in range(terms):
        term = x.astype(jnp.bfloat16)
        x = x - term.astype(F32)
        part = jnp.dot(term, ones, preferred_element_type=F32)
        total = part if total is None else total + part
    return total


def _rope(t, cos, sa, sb):
    return t * cos + pltpu.roll(t, LANE - 16, axis=1) * sa + pltpu.roll(t, 16, axis=1) * sb


def _rope_t(d, cos, sa, sb):
    return d * cos + pltpu.roll(d * sa, 16, axis=1) + pltpu.roll(d * sb, LANE - 16, axis=1)


def _shift_down(u, k):
    rows = lax.broadcasted_iota(jnp.int32, u.shape, 0)
    return jnp.where(rows >= k, pltpu.roll(u, k, axis=0), 0.0)


def _shift_up(u, k):
    n = u.shape[0]
    rows = lax.broadcasted_iota(jnp.int32, u.shape, 0)
    return jnp.where(rows < n - k, pltpu.roll(u, n - k, axis=0), 0.0)


def _tile(n, want):
    t = min(n, want)
    assert n % t == 0, (n, want)
    return t


def _inproj_fwd(x, g, wp):
    T = x.shape[0]
    tm, tn = _tile(T, 2048), 512

    def body(x_ref, g_ref, w_ref, proj_ref, ht_ref, h_ref):
        @pl.when(pl.program_id(1) == 0)
        def _():
            n = min(tm, 512)
            for r0 in range(0, tm, n):
                xv = x_ref[r0:r0 + n, :]
                h = xv * _rstd(xv, D_MODEL) * g_ref[...]
                h_ref[r0:r0 + n, :] = h.astype(BF16)
                ht_ref[:, r0:r0 + n] = h.T.astype(BF16)

        proj_ref[...] = jnp.dot(h_ref[...], w_ref[...], preferred_element_type=F32).astype(BF16)

    return pl.pallas_call(
        body, name="inproj_fwd", grid=(T // tm, PP // tn),
        in_specs=[pl.BlockSpec((tm, D_MODEL), lambda i, j: (i, 0)),
                  pl.BlockSpec((1, D_MODEL), lambda i, j: (0, 0)),
                  pl.BlockSpec((D_MODEL, tn), lambda i, j: (0, j))],
        out_specs=[pl.BlockSpec((tm, tn), lambda i, j: (i, j)),
                   pl.BlockSpec((D_MODEL, tm), lambda i, j: (0, i))],
        out_shape=[jax.ShapeDtypeStruct((T, PP), BF16), jax.ShapeDtypeStruct((D_MODEL, T), BF16)],
        scratch_shapes=[pltpu.VMEM((tm, D_MODEL), BF16)],
        compiler_params=_cp(),
    )(x, g, wp)


def _matmul_nn(at, b, name):
    K, T = at.shape
    N = b.shape[1]
    tt, tn = _tile(T, 1024), _tile(N, 2816)
    nk = T // tt

    def body(a_ref, b_ref, o_ref, acc_ref):
        k = pl.program_id(1)

        @pl.when(k == 0)
        def _():
            acc_ref[...] = jnp.zeros_like(acc_ref)

        acc_ref[...] += jnp.dot(a_ref[...], b_ref[...], preferred_element_type=F32)

        @pl.when(k == nk - 1)
        def _():
            o_ref[...] = acc_ref[...].astype(BF16)

    return pl.pallas_call(
        body, name=name, grid=(N // tn, nk),
        in_specs=[pl.BlockSpec((K, tt), lambda j, k: (0, k)),
                  pl.BlockSpec((tt, tn), lambda j, k: (k, j))],
        out_specs=pl.BlockSpec((K, tn), lambda j, k: (0, j)),
        out_shape=jax.ShapeDtypeStruct((K, N), BF16),
        scratch_shapes=[pltpu.VMEM((K, tn), F32)],
        compiler_params=_cp(),
    )(at, b)


def _matmul_tn(a, b, name):
    T, K = a.shape
    N = b.shape[1]
    tt, tn = _tile(T, 512), _tile(N, 1024)

    def body(a_ref, b_ref, o_ref):
        @pl.when(pl.program_id(1) == 0)
        def _():
            o_ref[...] = jnp.zeros_like(o_ref)

        o_ref[...] += _mm_tn(a_ref[...], b_ref[...])

    return pl.pallas_call(
        body, name=name, grid=(N // tn, T // tt),
        in_specs=[pl.BlockSpec((tt, K), lambda j, k: (k, 0)),
                  pl.BlockSpec((tt, tn), lambda j, k: (k, j))],
        out_specs=pl.BlockSpec((K, tn), lambda j, k: (0, j)),
        out_shape=jax.ShapeDtypeStruct((K, N), F32),
        compiler_params=_cp(),
    )(a, b)


def _inproj_bwd_x(dproj, wp, x, g, dout):
    T = x.shape[0]
    tm, tk = _tile(T, 1024), 1024
    nk = PP // tk

    def body(dp_ref, w_ref, x_ref, g_ref, do_ref, dx_ref, dg_ref, acc_ref):
        i, k = pl.program_id(0), pl.program_id(1)

        @pl.when(k == 0)
        def _():
            acc_ref[...] = jnp.zeros_like(acc_ref)

        @pl.when((k == 0) & (i == 0))
        def _():
            dg_ref[...] = jnp.zeros_like(dg_ref)

        acc_ref[...] += _mm_nt(dp_ref[...], w_ref[...])

        @pl.when(k == nk - 1)
        def _():
            dh = acc_ref[...]
            xv = x_ref[...]
            r = _rstd(xv, D_MODEL)
            gy = dh * g_ref[...]
            dot = jnp.sum(xv * gy, axis=-1, keepdims=True) * (1.0 / D_MODEL)
            dx_ref[...] = do_ref[...] + r * gy - xv * (r * r * r) * dot
            dg_ref[...] += jnp.sum(dh * xv * r, axis=0, keepdims=True)

    return pl.pallas_call(
        body, name="inproj_bwd_x", grid=(T // tm, nk),
        in_specs=[pl.BlockSpec((tm, tk), lambda i, k: (i, k)),
                  pl.BlockSpec((D_MODEL, tk), lambda i, k: (0, k)),
                  pl.BlockSpec((tm, D_MODEL), lambda i, k: (i, 0)),
                  pl.BlockSpec((1, D_MODEL), lambda i, k: (0, 0)),
                  pl.BlockSpec((tm, D_MODEL), lambda i, k: (i, 0))],
        out_specs=[pl.BlockSpec((tm, D_MODEL), lambda i, k: (i, 0)),
                   pl.BlockSpec((1, D_MODEL), lambda i, k: (0, 0))],
        out_shape=[jax.ShapeDtypeStruct((T, D_MODEL), F32), jax.ShapeDtypeStruct((1, D_MODEL), F32)],
        scratch_shapes=[pltpu.VMEM((tm, D_MODEL), F32)],
        compiler_params=_cp(),
    )(dproj, wp, x, g, dout)


def _mixa_fwd(proj, cw, cb, B, S):
    nc = CONV_WIDTH // LANE
    ca = CB_A // 4

    def body(p_ref, cw_ref, cb_ref, y_ref):
        ab, ac, ax, az = (p_ref[:, i * LANE:(i + 1) * LANE].astype(F32) for i in range(4))
        u = ac * ax
        conv = cb_ref[...] + cw_ref[0:1, :] * _shift_down(u, 2) + cw_ref[1:2, :] * _shift_down(u, 1) + cw_ref[2:3, :] * u
        y_ref[...] = (ab * conv * _silu(az)).astype(BF16)

    return pl.pallas_call(
        body, name="mixa_fwd", grid=(B, nc),
        in_specs=[pl.BlockSpec((S, 4 * LANE), lambda b, j: (b, ca + j)),
                  pl.BlockSpec((CONV_K, LANE), lambda b, j: (0, j)),
                  pl.BlockSpec((1, LANE), lambda b, j: (0, j))],
        out_specs=pl.BlockSpec((S, LANE), lambda b, j: (b, j)),
        out_shape=jax.ShapeDtypeStruct((B * S, CONV_WIDTH), BF16),
        compiler_params=_cp(),
    )(proj, cw, cb)


def _mixa_bwd(dproj, dy, proj, cw, cb, B, S):
    nc = CONV_WIDTH // LANE
    ca = CB_A // 4

    def body(dpin_ref, dy_ref, p_ref, cw_ref, cb_ref, dp_ref, st_ref):
        del dpin_ref
        ab, ac, ax, az = (p_ref[:, i * LANE:(i + 1) * LANE].astype(F32) for i in range(4))
        u = ac * ax
        u1, u2 = _shift_down(u, 1), _shift_down(u, 2)
        w0, w1, w2 = cw_ref[0:1, :], cw_ref[1:2, :], cw_ref[2:3, :]
        conv = cb_ref[...] + w0 * u2 + w1 * u1 + w2 * u
        s = _silu(az)
        d = dy_ref[...]
        dconv = d * ab * s
        du = w2 * dconv + w1 * _shift_up(dconv, 1) + w0 * _shift_up(dconv, 2)
        dp_ref[:, 0:LANE] = (d * conv * s).astype(BF16)
        dp_ref[:, LANE:2 * LANE] = (du * ax).astype(BF16)
        dp_ref[:, 2 * LANE:3 * LANE] = (du * ac).astype(BF16)
        dp_ref[:, 3 * LANE:4 * LANE] = (d * ab * conv * _dsilu(az)).astype(BF16)
        row = lax.broadcasted_iota(jnp.int32, (8, LANE), 0)
        st = jnp.zeros((8, LANE), F32)
        for r, v in enumerate((dconv * u2, dconv * u1, dconv * u, dconv)):
            st = st + jnp.where(row == r, jnp.sum(v, axis=0, keepdims=True), 0.0)

        @pl.when(pl.program_id(1) == 0)
        def _():
            st_ref[...] = st

        @pl.when(pl.program_id(1) != 0)
        def _():
            st_ref[...] += st

    return pl.pallas_call(
        body, name="mixa_bwd", grid=(nc, B),
        in_specs=[pl.BlockSpec(memory_space=pl.ANY),
                  pl.BlockSpec((S, LANE), lambda j, b: (b, j)),
                  pl.BlockSpec((S, 4 * LANE), lambda j, b: (b, ca + j)),
                  pl.BlockSpec((CONV_K, LANE), lambda j, b: (0, j)),
                  pl.BlockSpec((1, LANE), lambda j, b: (0, j))],
        out_specs=[pl.BlockSpec((S, 4 * LANE), lambda j, b: (b, ca + j)),
                   pl.BlockSpec((8, LANE), lambda j, b: (0, j))],
        out_shape=[jax.ShapeDtypeStruct(dproj.shape, BF16), jax.ShapeDtypeStruct((8, CONV_WIDTH), F32)],
        input_output_aliases={0: 0},
        compiler_params=_cp(),
    )(dproj, dy, proj, cw, cb)


def _mla_prep_fwd(proj, gq, gkv, wuqp, wkp, wv, gmq, gmk, cos, sa, sb, S):
    T = proj.shape[0]
    ts = _tile(S, 512)
    ns = S // ts
    W = MLA_HEADS * LANE

    def body(p_ref, gq_ref, gkv_ref, wuq_ref, wk_ref, wv_ref, gmq_ref, gmk_ref, cos_ref, sa_ref, sb_ref,
             q_ref, k_ref, v_ref):
        cq = p_ref[:, 0:2 * LANE].astype(F32)
        ckv = p_ref[:, 2 * LANE:3 * LANE].astype(F32)
        kpe = pltpu.roll(p_ref[:, 3 * LANE:4 * LANE].astype(F32), HALF, axis=1)
        cqn = cq * _rstd(cq, MLA_Q_LORA) * gq_ref[...]
        ckn = (ckv * _rstd(ckv, MLA_KV_LORA) * gkv_ref[...]).astype(BF16)
        q0 = _mm(cqn, wuq_ref[...])
        kn = _mm(ckn, wk_ref[...])
        v_ref[...] = _mm(ckn, wv_ref[...]).astype(BF16)
        c, a, b = cos_ref[...], sa_ref[...], sb_ref[...]
        for h in range(MLA_HEADS):
            q0h = q0[:, h * LANE:(h + 1) * LANE]
            q_ref[h] = _rope(q0h * _rstd(q0h, MLA_QK) * gmq_ref[...], c, a, b).astype(BF16)
            k0h = kn[:, h * LANE:(h + 1) * LANE] + kpe
            k_ref[h] = _rope(k0h * _rstd(k0h, MLA_QK) * gmk_ref[...], c, a, b).astype(BF16)

    def whole(r, c):
        return pl.BlockSpec((r, c), lambda i: (0, 0))

    tab = pl.BlockSpec((ts, LANE), lambda i: (i % ns, 0))
    return pl.pallas_call(
        body, name="mla_prep_fwd", grid=(T // ts,),
        in_specs=[pl.BlockSpec((ts, 4 * LANE), lambda i: (i, CB_CQ // 4)),
                  whole(1, MLA_Q_LORA), whole(1, MLA_KV_LORA), whole(MLA_Q_LORA, W), whole(MLA_KV_LORA, W),
                  whole(MLA_KV_LORA, MLA_HEADS * MLA_V), whole(1, LANE), whole(1, LANE), tab, tab, tab],
        out_specs=[pl.BlockSpec((MLA_HEADS, ts, LANE), lambda i: (0, i, 0)),
                   pl.BlockSpec((MLA_HEADS, ts, LANE), lambda i: (0, i, 0)),
                   pl.BlockSpec((ts, MLA_HEADS * MLA_V), lambda i: (i, 0))],
        out_shape=[jax.ShapeDtypeStruct((MLA_HEADS, T, LANE), BF16), jax.ShapeDtypeStruct((MLA_HEADS, T, LANE), BF16),
                   jax.ShapeDtypeStruct((T, MLA_HEADS * MLA_V), BF16)],
        compiler_params=_cp(),
    )(proj, gq, gkv, wuqp, wkp, wv, gmq, gmk, cos, sa, sb)


def _mla_prep_bwd(dproj, dq, dk, dv, proj, gq, gkv, wuqp, wkp, wv, gmq, gmk, cos, sa, sb, S):
    T = proj.shape[0]
    ts = _tile(S, 256)
    ns = S // ts
    W = MLA_HEADS * LANE

    def body(dpin_ref, dq_ref, dk_ref, dv_ref, p_ref, gq_ref, gkv_ref, wuq_ref, wk_ref, wv_ref, gmq_ref, gmk_ref,
             cos_ref, sa_ref, sb_ref,
             dp_ref, dwuq_ref, dwk_ref, dwv_ref, dgq_ref, dgkv_ref, dgmq_ref, dgmk_ref, dq0_ref, dkn_ref):
        del dpin_ref

        @pl.when(pl.program_id(0) == 0)
        def _():
            for r in (dwuq_ref, dwk_ref, dwv_ref, dgq_ref, dgkv_ref, dgmq_ref, dgmk_ref):
                r[...] = jnp.zeros_like(r)

        cq = p_ref[:, 0:2 * LANE].astype(F32)
        ckv = p_ref[:, 2 * LANE:3 * LANE].astype(F32)
        kpe = pltpu.roll(p_ref[:, 3 * LANE:4 * LANE].astype(F32), HALF, axis=1)
        rq = _rstd(cq, MLA_Q_LORA)
        rkv = _rstd(ckv, MLA_KV_LORA)
        gq, gkv, gmq, gmk = gq_ref[...], gkv_ref[...], gmq_ref[...], gmk_ref[...]
        cqn = (cq * rq * gq).astype(BF16)
        ckn = (ckv * rkv * gkv).astype(BF16)
        q0 = _mm(cqn, wuq_ref[...])
        kn = _mm(ckn, wk_ref[...])
        c, a, b = cos_ref[...], sa_ref[...], sb_ref[...]
        lane = lax.broadcasted_iota(jnp.int32, (ts, LANE), 1)
        dgmq = jnp.zeros((1, LANE), F32)
        dgmk = jnp.zeros((1, LANE), F32)
        dkpe = jnp.zeros((ts, LANE), F32)
        for h in range(MLA_HEADS):
            q0h = q0[:, h * LANE:(h + 1) * LANE]
            r = _rstd(q0h, MLA_QK)
            d1 = _rope_t(dq_ref[h], c, a, b)
            gy = d1 * gmq
            dq0_ref[:, h * LANE:(h + 1) * LANE] = (
                r * gy - q0h * (r * r * r) * (jnp.sum(q0h * gy, axis=-1, keepdims=True) * (1.0 / MLA_QK))).astype(BF16)
            dgmq = dgmq + jnp.sum(d1 * q0h * r, axis=0, keepdims=True)
            k0h = kn[:, h * LANE:(h + 1) * LANE] + kpe
            r = _rstd(k0h, MLA_QK)
            d1 = _rope_t(dk_ref[h], c, a, b)
            gy = d1 * gmk
            dk0 = r * gy - k0h * (r * r * r) * (jnp.sum(k0h * gy, axis=-1, keepdims=True) * (1.0 / MLA_QK))
            dgmk = dgmk + jnp.sum(d1 * k0h * r, axis=0, keepdims=True)
            dkn_ref[:, h * LANE:(h + 1) * LANE] = jnp.where(lane < MLA_NOPE, dk0, 0.0).astype(BF16)
            dkpe = dkpe + jnp.where((lane >= MLA_NOPE) & (lane < MLA_QK), dk0, 0.0)
        dq0 = dq0_ref[...]
        dkn = dkn_ref[...]
        dvv = dv_ref[...]
        dwuq_ref[...] += _mm_tn(cqn, dq0)
        dwk_ref[...] += _mm_tn(ckn, dkn)
        dwv_ref[...] += _mm_tn(ckn, dvv)
        dgmq_ref[...] += dgmq
        dgmk_ref[...] += dgmk
        dcqn = _mm_nt(dq0, wuq_ref[...])
        gy = dcqn * gq
        dp_ref[:, 0:2 * LANE] = (
            rq * gy - cq * (rq * rq * rq) * (jnp.sum(cq * gy, axis=-1, keepdims=True) * (1.0 / MLA_Q_LORA))).astype(BF16)
        dgq_ref[...] += jnp.sum(dcqn * cq * rq, axis=0, keepdims=True)
        dckn = _mm_nt(dkn, wk_ref[...]) + _mm_nt(dvv, wv_ref[...])
        gy = dckn * gkv
        dp_ref[:, 2 * LANE:3 * LANE] = (
            rkv * gy - ckv * (rkv * rkv * rkv) * (jnp.sum(ckv * gy, axis=-1, keepdims=True) * (1.0 / MLA_KV_LORA))).astype(BF16)
        dgkv_ref[...] += jnp.sum(dckn * ckv * rkv, axis=0, keepdims=True)
        dp_ref[:, 3 * LANE:4 * LANE] = pltpu.roll(dkpe, HALF, axis=1).astype(BF16)

    def whole(r, c):
        return pl.BlockSpec((r, c), lambda i: (0, 0))

    tab = pl.BlockSpec((ts, LANE), lambda i: (i % ns, 0))
    heads = pl.BlockSpec((MLA_HEADS, ts, LANE), lambda i: (0, i, 0))
    return pl.pallas_call(
        body, name="mla_prep_bwd", grid=(T // ts,),
        in_specs=[pl.BlockSpec(memory_space=pl.ANY), heads, heads,
                  pl.BlockSpec((ts, MLA_HEADS * MLA_V), lambda i: (i, 0)),
                  pl.BlockSpec((ts, 4 * LANE), lambda i: (i, CB_CQ // 4)),
                  whole(1, MLA_Q_LORA), whole(1, MLA_KV_LORA), whole(MLA_Q_LORA, W), whole(MLA_KV_LORA, W),
                  whole(MLA_KV_LORA, MLA_HEADS * MLA_V), whole(1, LANE), whole(1, LANE), tab, tab, tab],
        out_specs=[pl.BlockSpec((ts, 4 * LANE), lambda i: (i, CB_CQ // 4)),
                   whole(MLA_Q_LORA, W), whole(MLA_KV_LORA, W), whole(MLA_KV_LORA, MLA_HEADS * MLA_V),
                   whole(1, MLA_Q_LORA), whole(1, MLA_KV_LORA), whole(1, LANE), whole(1, LANE)],
        out_shape=[jax.ShapeDtypeStruct(dproj.shape, BF16),
                   jax.ShapeDtypeStruct((MLA_Q_LORA, W), F32), jax.ShapeDtypeStruct((MLA_KV_LORA, W), F32),
                   jax.ShapeDtypeStruct((MLA_KV_LORA, MLA_HEADS * MLA_V), F32),
                   jax.ShapeDtypeStruct((1, MLA_Q_LORA), F32), jax.ShapeDtypeStruct((1, MLA_KV_LORA), F32),
                   jax.ShapeDtypeStruct((1, LANE), F32), jax.ShapeDtypeStruct((1, LANE), F32)],
        scratch_shapes=[pltpu.VMEM((ts, W), BF16), pltpu.VMEM((ts, W), BF16)],
        input_output_aliases={0: 0},
        compiler_params=_cp(),
    )(dproj, dq, dk, dv, proj, gq, gkv, wuqp, wkp, wv, gmq, gmk, cos, sa, sb)


def _dil_prep_fwd(proj, gq, gk):
    T = proj.shape[0]
    ts = _tile(T, 512)
    gw = 3 * NPAIR * LANE

    def body(p_ref, gq_ref, gk_ref, q_ref, k_ref, v_ref):
        for c in range(NPAIR):
            t = p_ref[:, 3 * c * LANE:(3 * c + 2) * LANE].astype(F32)
            y = t * lax.rsqrt(_head_bcast_sum(t * t, terms=2) * (1.0 / DIL_HEAD_DIM) + EPS)
            cs = slice(c * LANE, (c + 1) * LANE)
            q_ref[:, cs] = y[:, 0:LANE] * gq_ref[:, cs]
            k_ref[:, cs] = y[:, LANE:2 * LANE] * gk_ref[:, cs]
            v_ref[:, cs] = p_ref[:, (3 * c + 2) * LANE:(3 * c + 3) * LANE].astype(F32)

    col = pl.BlockSpec((1, DIL_WIDTH), lambda i, g: (0, g))
    out = pl.BlockSpec((ts, DIL_WIDTH), lambda i, g: (i, g))
    return pl.pallas_call(
        body, name="dil_prep_fwd", grid=(T // ts, DIL_GROUPS),
        in_specs=[pl.BlockSpec((ts, gw), lambda i, g: (i, CB_QKV * LANE // gw + g)), col, col],
        out_specs=[out, out, out],
        out_shape=[jax.ShapeDtypeStruct((T, DIL_QK), F32)] * 3,
        compiler_params=_cp(),
    )(proj, gq, gk)


def _dil_prep_bwd(dproj, ddq, ddk, ddv, proj, gq, gk):
    T = proj.shape[0]
    ts = _tile(T, 512)
    gw = 3 * NPAIR * LANE

    def body(dpin_ref, ddq_ref, ddk_ref, ddv_ref, p_ref, gq_ref, gk_ref, dp_ref, dgq_ref, dgk_ref):
        del dpin_ref

        @pl.when(pl.program_id(1) == 0)
        def _():
            dgq_ref[...] = jnp.zeros_like(dgq_ref)
            dgk_ref[...] = jnp.zeros_like(dgk_ref)

        for c in range(NPAIR):
            cs = slice(c * LANE, (c + 1) * LANE)
            dp_ref[:, (3 * c + 2) * LANE:(3 * c + 3) * LANE] = ddv_ref[:, cs].astype(BF16)
            t = p_ref[:, 3 * c * LANE:(3 * c + 2) * LANE].astype(F32)
            d = jnp.concatenate([ddq_ref[:, cs], ddk_ref[:, cs]], axis=1)
            gy = d * jnp.concatenate([gq_ref[:, cs], gk_ref[:, cs]], axis=1)
            r = lax.rsqrt(_head_bcast_sum(t * t, terms=2) * (1.0 / DIL_HEAD_DIM) + EPS)
            dot = _head_bcast_sum(t * gy, terms=2) * (1.0 / DIL_HEAD_DIM)
            dp_ref[:, 3 * c * LANE:(3 * c + 2) * LANE] = (r * gy - t * (r * r * r) * dot).astype(BF16)
            part = jnp.sum(d * t * r, axis=0, keepdims=True)
            dgq_ref[:, cs] += part[:, 0:LANE]
            dgk_ref[:, cs] += part[:, LANE:2 * LANE]

    col = pl.BlockSpec((1, DIL_WIDTH), lambda g, i: (0, g))
    tok = pl.BlockSpec((ts, DIL_WIDTH), lambda g, i: (i, g))
    return pl.pallas_call(
        body, name="dil_prep_bwd", grid=(DIL_GROUPS, T // ts),
        in_specs=[pl.BlockSpec(memory_space=pl.ANY), tok, tok, tok,
                  pl.BlockSpec((ts, gw), lambda g, i: (i, CB_QKV * LANE // gw + g)), col, col],
        out_specs=[pl.BlockSpec((ts, gw), lambda g, i: (i, CB_QKV * LANE // gw + g)), col, col],
        out_shape=[jax.ShapeDtypeStruct(dproj.shape, BF16), jax.ShapeDtypeStruct((1, DIL_QK), F32),
                   jax.ShapeDtypeStruct((1, DIL_QK), F32)],
        input_output_aliases={0: 0},
        compiler_params=_cp(),
    )(dproj, ddq, ddk, ddv, proj, gq, gk)


COPY_ROWS = 256


def _to_classes(src_ref, dst_ref, d, L, scale=None):
    n = min(L, COPY_ROWS)
    for r in range(d):
        for c0 in range(0, L, n):
            rows = pl.ds(r + c0 * d, n, stride=d) if d > 1 else pl.ds(c0, n)
            val = src_ref[rows, :]
            if scale is not None:
                val = val * scale
            dst_ref[r * L + c0:r * L + c0 + n, :] = val.astype(dst_ref.dtype)


def _from_classes(src_ref, dst_ref, d, L):
    n = min(L, COPY_ROWS)
    for r in range(d):
        for c0 in range(0, L, n):
            rows = pl.ds(r + c0 * d, n, stride=d) if d > 1 else pl.ds(c0, n)
            dst_ref[rows, :] = src_ref[r * L + c0:r * L + c0 + n, :].astype(dst_ref.dtype)


MLA_TQ, MLA_TK = 512, 512


def _causal_bias(tq, tk, shift):
    row = lax.broadcasted_iota(jnp.int32, (tq, tk), 0)
    col = lax.broadcasted_iota(jnp.int32, (tq, tk), 1)
    return jnp.where(row >= col + shift, 0.0, NEG)


def _mla_specs(S):
    heads = pl.BlockSpec((2, S, LANE), lambda b, j: (j, b, 0))
    pair = pl.BlockSpec((S, LANE), lambda b, j: (b, j))
    return heads, pair


def _mla_attn_fwd(q, k, v, B, S):
    tq = _tile(S, MLA_TQ)
    tk = _tile(tq, MLA_TK)
    nd = tq // tk
    scale = MLA_QK ** -0.5
    heads, pair = _mla_specs(S)

    def body(q_ref, k_ref, v_ref, o_ref, lse_ref):
        lo, lok = _lane_lo((tq, LANE)), _lane_lo((tk, LANE))
        diag = [_causal_bias(tq, tk, i * tk) for i in range(nd)]

        def block(g, _):
            row0 = pl.multiple_of(g * tq, tq)
            rows = pl.ds(row0, tq)
            qs = [q_ref[hh, rows, :] for hh in range(2)]

            one = jnp.ones((), BF16)

            def step(off, carries, bias):
                off = pl.multiple_of(off, tk)
                vt = v_ref[pl.ds(off, tk), :]
                vh = (jnp.where(lok, vt, one), jnp.where(lok, one, vt))
                out = []
                for hh, (m, acc) in enumerate(carries):
                    s = _mm_nt(qs[hh], k_ref[hh, pl.ds(off, tk), :]) * scale
                    if bias is not None:
                        s = s + bias
                    m_new = jnp.maximum(m, jnp.max(s, axis=-1, keepdims=True))
                    p = jnp.exp(s - m_new)
                    out.append((m_new, jnp.exp(m - m_new) * acc + _mm(p, vh[hh])))
                return tuple(out)

            init = (jnp.full((tq, 1), NEG, F32), jnp.zeros((tq, LANE), F32))
            carries = lax.fori_loop(0, g * nd, lambda i, c: step(i * tk, c, None), (init, init))
            for i in range(nd):
                carries = step(row0 + i * tk, carries, diag[i])
            (ma, acca), (mb, accb) = carries
            la, lb = pltpu.roll(acca, HALF, axis=1), pltpu.roll(accb, HALF, axis=1)
            o_ref[rows, :] = jnp.where(lo, acca / la, accb / lb)
            lse_ref[rows, :] = jnp.where(lo, ma + jnp.log(la), mb + jnp.log(lb))
            return 0

        lax.fori_loop(0, S // tq, block, 0)

    return pl.pallas_call(
        body, name="mla_attn_fwd", grid=(B, NPAIR), in_specs=[heads, heads, pair], out_specs=[pair, pair],
        out_shape=[jax.ShapeDtypeStruct((B * S, MLA_HEADS * MLA_V), F32)] * 2,
        compiler_params=_cp(),
    )(q, k, v)


DIL_UNROLL = 8


def _dil_geometry(gi, S):
    span, d = DIL_PATTERNS[gi]
    L = S // d
    t = _tile(L, 128)
    window = span // d
    back = min(-(-window // t) * t, L - t)
    return d, L, t, window, back


def _dil_specs(gi, S):
    qk = pl.BlockSpec((S, LANE), lambda b, j: (b, NPAIR * gi + j))
    pair = pl.BlockSpec((S, LANE), lambda b, j: (b, j))
    return qk, qk, pair


def _dil_bias(bias_ref, sl_ref, j, t, kw, back, window):
    row = lax.broadcasted_iota(jnp.int32, (2 * t, kw), 0)
    col = lax.broadcasted_iota(jnp.int32, (2 * t, kw), 1)
    second = row >= t
    slope = jnp.where(second, sl_ref[j, 1], sl_ref[j, 0])
    for n in range(bias_ref.shape[0]):
        dist = jnp.where(second, row - t, row) + n * back - col
        bias_ref[n] = jnp.where((dist >= 0) & (dist <= window), -slope * dist.astype(F32), NEG)


def _stack_heads(x, lo):
    zero = jnp.zeros((), x.dtype)
    return jnp.concatenate([jnp.where(lo, x, zero), jnp.where(lo, zero, x)], axis=0)


def _dil_attn_fwd(gi, slopes, qn, kn, proj, B, S):
    d, L, t, window, back = _dil_geometry(gi, S)
    kw, nq = back + t, L // t
    nbias = 2 if back else 1
    qk, vspec, pair = _dil_specs(gi, S)

    def body(sl_ref, q_ref, k_ref, v_ref, o_ref, lse_ref, qs, ks, vs, os_, ls, bias_ref):
        _to_classes(q_ref, qs, d, L, DIL_HEAD_DIM ** -0.5)
        _to_classes(k_ref, ks, d, L)
        _to_classes(v_ref, vs, d, L)
        _dil_bias(bias_ref, sl_ref, pl.program_id(1), t, kw, back, window)
        lo = _lane_lo((t, LANE))

        def block(g, _):
            qb = g % nq if d > 1 else g
            row0 = pl.multiple_of(g * t, t)
            rows = pl.ds(row0, t)
            early = qb * t < back
            keys = pl.ds(pl.multiple_of(jnp.where(early, row0 - qb * t, row0 - back), t), kw)
            s = _mm_nt(_stack_heads(qs[rows, :], lo), ks[keys, :]) + bias_ref[jnp.where(early, 0, nbias - 1)]
            m = jnp.max(s, axis=-1, keepdims=True)
            p = jnp.exp(s - m)
            l = jnp.sum(p, axis=-1, keepdims=True)
            o2 = _mm(p, vs[keys, :]) / l
            lse2 = m + jnp.log(l)
            os_[rows, :] = jnp.where(lo, o2[:t], o2[t:])
            ls[rows, :] = jnp.where(lo, lse2[:t], lse2[t:])
            return 0

        lax.fori_loop(0, d * nq, block, 0, unroll=DIL_UNROLL if d * nq % DIL_UNROLL == 0 else 1)
        _from_classes(os_, o_ref, d, L)
        _from_classes(ls, lse_ref, d, L)

    return pl.pallas_call(
        body, name=f"dil_attn_fwd_{gi}", grid=(B, NPAIR),
        in_specs=[pl.BlockSpec(memory_space=pltpu.SMEM), qk, qk, vspec], out_specs=[pair, pair],
        out_shape=[jax.ShapeDtypeStruct((B * S, DIL_WIDTH), F32)] * 2,
        scratch_shapes=[pltpu.VMEM((S, LANE), BF16)] * 3 + [pltpu.VMEM((S, LANE), F32)] * 2
                       + [pltpu.VMEM((nbias, 2 * t, kw), F32)],
        compiler_params=_cp(),
    )(slopes, qn, kn, proj)


def _mla_attn_bwd(q, k, v, do, lse, delta, B, S):
    T = B * S
    tq = _tile(S, MLA_TQ)
    tk = _tile(tq, MLA_TK)
    nd = tq // tk
    scale = MLA_QK ** -0.5
    heads, pair = _mla_specs(S)

    def body(q_ref, k_ref, v_ref, do_ref, lse_ref, dl_ref, dq_ref, dk_ref, dv_ref):
        dk_ref[...] = jnp.zeros_like(dk_ref)
        dv_ref[...] = jnp.zeros_like(dv_ref)
        lo = _lane_lo((tq, LANE))
        diag = [_causal_bias(tq, tk, i * tk) for i in range(nd)]

        def block(g, _):
            row0 = pl.multiple_of(g * tq, tq)
            rows = pl.ds(row0, tq)
            for hh in range(2):
                sel = lo if hh == 0 else jnp.logical_not(lo)
                qh = q_ref[hh, rows, :]
                doh = jnp.where(sel, do_ref[rows, :], jnp.zeros((), BF16))
                lse_h = jnp.max(jnp.where(sel, lse_ref[rows, :], NEG), axis=-1, keepdims=True)
                dl_h = jnp.max(jnp.where(sel, dl_ref[rows, :], NEG), axis=-1, keepdims=True)

                def step(off, dq_acc, bias, hh=hh, qh=qh, doh=doh, lse_h=lse_h, dl_h=dl_h):
                    cols = pl.ds(pl.multiple_of(off, tk), tk)
                    kh = k_ref[hh, cols, :]
                    s = _mm_nt(qh, kh) * scale
                    if bias is not None:
                        s = s + bias
                    p = jnp.exp(s - lse_h)
                    dp = _mm_nt(doh, v_ref[cols, :])
                    ds = (p * (dp - dl_h)).astype(BF16)
                    dk_ref[hh, cols, :] += _mm_tn(ds, qh) * scale
                    dv_ref[cols, :] += _mm_tn(p, doh)
                    return dq_acc + _mm(ds, kh)

                dq_acc = lax.fori_loop(0, g * nd, lambda i, a: step(i * tk, a, None), jnp.zeros((tq, LANE), F32))
                for i in range(nd):
                    dq_acc = step(row0 + i * tk, dq_acc, diag[i])
                dq_ref[hh, rows, :] = dq_acc * scale
            return 0

        lax.fori_loop(0, S // tq, block, 0)

    return pl.pallas_call(
        body, name="mla_attn_bwd", grid=(B, NPAIR), in_specs=[heads, heads, pair, pair, pair, pair],
        out_specs=[heads, heads, pair],
        out_shape=[jax.ShapeDtypeStruct((MLA_HEADS, T, LANE), F32), jax.ShapeDtypeStruct((MLA_HEADS, T, LANE), F32),
                   jax.ShapeDtypeStruct((T, MLA_HEADS * MLA_V), F32)],
        compiler_params=_cp(),
    )(q, k, v, do, lse, delta)


def _dil_attn_bwd(gi, slopes, qn, kn, proj, do, lse, delta, through, B, S):
    d, L, t, window, back = _dil_geometry(gi, S)
    kw, nq = back + t, L // t
    nbias = 2 if back else 1
    scale = DIL_HEAD_DIM ** -0.5
    qk, vspec, pair = _dil_specs(gi, S)

    def body(*refs):
        refs = list(refs)
        sl_ref, q_ref, k_ref, v_ref, do_ref, lse_ref, dl_ref = refs[:7]
        dq_ref, dk_ref, dv_ref, qs, ks, vs, dos, lss, dls, dqs, dks, dvs, bias_ref = refs[-13:]
        _to_classes(q_ref, qs, d, L, scale)
        for src, dst in ((k_ref, ks), (v_ref, vs), (do_ref, dos), (lse_ref, lss), (dl_ref, dls)):
            _to_classes(src, dst, d, L)
        _dil_bias(bias_ref, sl_ref, pl.program_id(1), t, kw, back, window)
        dks[...] = jnp.zeros_like(dks)
        dvs[...] = jnp.zeros_like(dvs)
        lo = _lane_lo((t, LANE))

        def stats(ref, rows):
            x = ref[rows, :]
            return jnp.concatenate([jnp.max(jnp.where(lo, x, NEG), axis=-1, keepdims=True),
                                    jnp.max(jnp.where(lo, NEG, x), axis=-1, keepdims=True)], axis=0)

        def block(g, _):
            qb = g % nq if d > 1 else g
            row0 = pl.multiple_of(g * t, t)
            rows = pl.ds(row0, t)
            early = qb * t < back
            keys = pl.ds(pl.multiple_of(jnp.where(early, row0 - qb * t, row0 - back), t), kw)
            q2 = _stack_heads(qs[rows, :], lo)
            do2 = _stack_heads(dos[rows, :], lo)
            kt = ks[keys, :]
            s = _mm_nt(q2, kt) + bias_ref[jnp.where(early, 0, nbias - 1)]
            p = jnp.exp(s - stats(lss, rows))
            ds = (p * (_mm_nt(do2, vs[keys, :]) - stats(dls, rows))).astype(BF16)
            dq2 = _mm(ds, kt) * scale
            dqs[rows, :] = jnp.where(lo, dq2[:t], dq2[t:])
            dks[keys, :] += _mm_tn(ds, q2)
            dvs[keys, :] += _mm_tn(p, do2)
            return 0

        lax.fori_loop(0, d * nq, block, 0, unroll=DIL_UNROLL if d * nq % DIL_UNROLL == 0 else 1)
        for src, dst in ((dqs, dq_ref), (dks, dk_ref), (dvs, dv_ref)):
            _from_classes(src, dst, d, L)

    in_specs = [pl.BlockSpec(memory_space=pltpu.SMEM), qk, qk, vspec, pair, pair, pair]
    args = [slopes, qn, kn, proj, do, lse, delta]
    aliases = {}
    if through is not None:
        aliases = {len(args) + i: i for i in range(3)}
        in_specs = in_specs + [pl.BlockSpec(memory_space=pl.ANY)] * 3
        args = args + list(through)
    return pl.pallas_call(
        body, name=f"dil_attn_bwd_{gi}", grid=(B, NPAIR), in_specs=in_specs, out_specs=[qk, qk, qk],
        out_shape=[jax.ShapeDtypeStruct((B * S, DIL_QK), F32)] * 3,
        scratch_shapes=[pltpu.VMEM((S, LANE), BF16)] * 4 + [pltpu.VMEM((S, LANE), F32)] * 5
                       + [pltpu.VMEM((nbias, 2 * t, kw), F32)],
        input_output_aliases=aliases,
        compiler_params=_cp(),
    )(*args)


def _merge_common(p_ref, bg_ref, ob_ref, og_refs, lse_refs):
    bz = p_ref[:, CB_BZ * LANE:(CB_BZ + 4) * LANE].astype(F32)
    cz = p_ref[:, CB_CZ * LANE:(CB_CZ + 4) * LANE].astype(F32)
    gates = [_sigmoid(p_ref[:, (CB_GATE + 8 * i) * LANE:(CB_GATE + 8 * i + 8) * LANE].astype(F32)
                      + bg_ref[:, i * D_MODEL:(i + 1) * D_MODEL]) for i in range(3)]
    ob = ob_ref[...]
    lses = [r[...] for r in lse_refs]
    mx = jnp.maximum(jnp.maximum(lses[0], lses[1]), lses[2])
    es = [jnp.exp(v - mx) for v in lses]
    inv = 1.0 / (es[0] + es[1] + es[2])
    alphas = [e * inv for e in es]
    oc = alphas[0] * og_refs[0][...] + alphas[1] * og_refs[1][...] + alphas[2] * og_refs[2][...]
    return bz, cz, gates, ob, alphas, oc


def _merge_fwd(x, proj, b_gate, ya, ob, ogs, lses, woa, wob, woc, wo):
    T = x.shape[0]
    ts = _tile(T, 256)
    MW = 32 * LANE

    def body(x_ref, p_ref, bg_ref, ya_ref, ob_ref, og0, og1, og2, l0, l1, l2, woa_ref, wob_ref, woc_ref, wo_ref, out_ref):
        bz, cz, gates, obv, alphas, oc = _merge_common(p_ref, bg_ref, ob_ref, (og0, og1, og2), (l0, l1, l2))
        yb = obv * _silu(bz)
        yc = oc * _silu(cz)
        merged = (gates[0] * _mm(ya_ref[...], woa_ref[...]) + gates[1] * _mm(yb, wob_ref[...])
                  + gates[2] * _mm(yc, woc_ref[...]))
        out_ref[...] = x_ref[...] + _mm(merged, wo_ref[...])

    def whole(r, c):
        return pl.BlockSpec((r, c), lambda i: (0, 0))

    tok = lambda w: pl.BlockSpec((ts, w), lambda i: (i, 0))
    return pl.pallas_call(
        body, name="merge_fwd", grid=(T // ts,),
        in_specs=[tok(D_MODEL), tok(MW), whole(1, 3 * D_MODEL), tok(CONV_WIDTH)] + [tok(DIL_WIDTH)] * 7
                 + [whole(CONV_WIDTH, D_MODEL)] * 3 + [whole(D_MODEL, D_MODEL)],
        out_specs=tok(D_MODEL),
        out_shape=jax.ShapeDtypeStruct((T, D_MODEL), F32),
        compiler_params=_cp(),
    )(x, proj, b_gate, ya, ob, *ogs, *lses, woa, wob, woc, wo)


def _merge_bwd(dout, proj, b_gate, ya, ob, ogs, lses, woa, wob, woc, wo):
    T = dout.shape[0]
    ts = _tile(T, 256)
    MW = 32 * LANE

    def body(do_ref, p_ref, bg_ref, ya_ref, ob_ref, og0, og1, og2, l0, l1, l2, woa_ref, wob_ref, woc_ref, wo_ref,
             dp_ref, dya_ref, dob_ref, dlb_ref, dg0, dg1, dg2, dl0, dl1, dl2,
             mg_ref, dpa_ref, dpb_ref, dpc_ref, yb_ref, yc_ref, dbg_ref):
        bz, cz, gates, obv, alphas, oc = _merge_common(p_ref, bg_ref, ob_ref, (og0, og1, og2), (l0, l1, l2))
        sb, sc = _silu(bz), _silu(cz)
        yb = obv * sb
        yc = oc * sc
        ps = [_mm(ya_ref[...], woa_ref[...]), _mm(yb, wob_ref[...]), _mm(yc, woc_ref[...])]
        mg_ref[...] = (gates[0] * ps[0] + gates[1] * ps[1] + gates[2] * ps[2]).astype(BF16)
        yb_ref[...] = yb.astype(BF16)
        yc_ref[...] = yc.astype(BF16)
        dm = _mm_nt(do_ref[...], wo_ref[...])
        dps = []
        first = pl.program_id(0) == 0
        for i, dref in enumerate((dpa_ref, dpb_ref, dpc_ref)):
            g = gates[i]
            dpi = (dm * g).astype(BF16)
            dref[...] = dpi
            dps.append(dpi)
            dgp = dm * ps[i] * g * (1.0 - g)
            dp_ref[:, (CB_GATE + 8 * i) * LANE:(CB_GATE + 8 * i + 8) * LANE] = dgp.astype(BF16)
            part = jnp.sum(dgp, axis=0, keepdims=True)

            @pl.when(first)
            def _():
                dbg_ref[:, i * D_MODEL:(i + 1) * D_MODEL] = part

            @pl.when(jnp.logical_not(first))
            def _():
                dbg_ref[:, i * D_MODEL:(i + 1) * D_MODEL] += part

        dya_ref[...] = _mm_nt(dps[0], woa_ref[...])
        dyb = _mm_nt(dps[1], wob_ref[...])
        dyc = _mm_nt(dps[2], woc_ref[...])
        dp_ref[:, CB_BZ * LANE:(CB_BZ + 4) * LANE] = (dyb * obv * _dsilu(bz)).astype(BF16)
        dp_ref[:, CB_CZ * LANE:(CB_CZ + 4) * LANE] = (dyc * oc * _dsilu(cz)).astype(BF16)
        dob = dyb * sb
        doc = dyc * sc
        dob_ref[...] = dob.astype(BF16)
        for c in range(NPAIR):
            cs = slice(c * LANE, (c + 1) * LANE)
            dlb_ref[:, cs] = _head_bcast_sum(dob[:, cs] * obv[:, cs])
            dd = _head_bcast_sum(doc[:, cs] * oc[:, cs])
            for a, dref, lref in zip(alphas, (dg0, dg1, dg2), (dl0, dl1, dl2)):
                dref[:, cs] = a[:, cs] * doc[:, cs]
                lref[:, cs] = a[:, cs] * dd

    def whole(r, c):
        return pl.BlockSpec((r, c), lambda i: (0, 0))

    tok = lambda w: pl.BlockSpec((ts, w), lambda i: (i, 0))
    sd = jax.ShapeDtypeStruct
    W = DIL_WIDTH
    return pl.pallas_call(
        body, name="merge_bwd", grid=(T // ts,),
        in_specs=[tok(D_MODEL), tok(MW), whole(1, 3 * D_MODEL), tok(CONV_WIDTH)] + [tok(W)] * 7
                 + [whole(CONV_WIDTH, D_MODEL)] * 3 + [whole(D_MODEL, D_MODEL)],
        out_specs=[tok(MW), tok(CONV_WIDTH), tok(W), tok(W)] + [tok(W)] * 6
                  + [tok(D_MODEL)] * 4 + [tok(W), tok(W), whole(1, 3 * D_MODEL)],
        out_shape=[sd((T, PP), BF16), sd((T, CONV_WIDTH), F32), sd((T, W), BF16), sd((T, W), F32)]
                  + [sd((T, W), F32)] * 6
                  + [sd((T, D_MODEL), BF16)] * 4 + [sd((T, W), BF16)] * 2 + [sd((1, 3 * D_MODEL), F32)],
        compiler_params=_cp(),
    )(dout, proj, b_gate, ya, ob, *ogs, *lses, woa, wob, woc, wo)


def _loss_head(y, target):
    T = y.shape[0]
    ts = _tile(T, 512)

    def body(y_ref, t_ref, d_ref, l_ref):
        e = y_ref[...] - t_ref[...]
        d_ref[...] = e * (1.0 / D_MODEL)
        l_ref[...] = jnp.zeros((1, 8, LANE), F32) + jnp.sum(e * e)

    tok = pl.BlockSpec((ts, D_MODEL), lambda i: (i, 0))
    return pl.pallas_call(
        body, name="loss_head", grid=(T // ts,), in_specs=[tok, tok],
        out_specs=[tok, pl.BlockSpec((1, 8, LANE), lambda i: (i, 0, 0))],
        out_shape=[jax.ShapeDtypeStruct((T, D_MODEL), F32), jax.ShapeDtypeStruct((T // ts, 8, LANE), F32)],
        compiler_params=_cp(),
    )(y, target)


def _my_index():
    return 4 * lax.axis_index("x") + 2 * lax.axis_index("y") + lax.axis_index("c")


def _peers():
    x, y, c = (lax.axis_index(a) for a in AXES)
    out = []
    for kk in range(1, N_DEV):
        px = 1 - x if kk & 4 else x
        py = 1 - y if kk & 2 else y
        pc = 1 - c if kk & 1 else c
        out.append(((px, py, pc), 4 * px + 2 * py + pc))
    return out


def _exchange(arrays, name, gather):
    n = len(arrays)

    def body(*refs):
        srcs, outs = refs[:n], refs[n:2 * n]
        send_sems, recv_sems, local_sems = refs[2 * n:]
        me = _my_index()
        peers = _peers()
        started = []
        for a, (src, out) in enumerate(zip(srcs, outs)):
            mine = pltpu.make_async_copy(src if gather else src.at[me], out.at[me], local_sems.at[a])
            mine.start()
            started.append(mine)
        sends = []
        for i, (pos, idx) in enumerate(peers):
            for a, (src, out) in enumerate(zip(srcs, outs)):
                cp = pltpu.make_async_remote_copy(
                    src_ref=src if gather else src.at[idx], dst_ref=out.at[me], send_sem=send_sems.at[a, i],
                    recv_sem=recv_sems.at[a, i], device_id=pos, device_id_type=pl.DeviceIdType.MESH)
                cp.start()
                sends.append(cp)
        for i, (pos, idx) in enumerate(peers):
            for a, (src, out) in enumerate(zip(srcs, outs)):
                pltpu.make_async_remote_copy(
                    src_ref=src if gather else src.at[idx], dst_ref=out.at[idx], send_sem=send_sems.at[a, i],
                    recv_sem=recv_sems.at[a, i], device_id=pos, device_id_type=pl.DeviceIdType.MESH).wait_recv()
        for cp in sends:
            cp.wait_send()
        for mine in started:
            mine.wait()

    any_space = pl.BlockSpec(memory_space=pl.ANY)
    return pl.pallas_call(
        body, name=name, in_specs=[any_space] * n, out_specs=[any_space] * n,
        out_shape=[jax.ShapeDtypeStruct(((N_DEV,) + a.shape) if gather else a.shape, a.dtype) for a in arrays],
        scratch_shapes=[pltpu.SemaphoreType.DMA((n, N_DEV - 1)), pltpu.SemaphoreType.DMA((n, N_DEV - 1)),
                        pltpu.SemaphoreType.DMA((n,))],
    )(*arrays)


N_CHIP = 4


def _chip_places():
    x, y, c = (lax.axis_index(a) for a in AXES)
    return (x, y, c), (x, y, 1 - c), [(1 - x, y, c), (x, 1 - y, c), (1 - x, 1 - y, c)]


def _index_of(pos):
    return 4 * pos[0] + 2 * pos[1] + pos[2]


def _gather_two_level(arrays, name):
    n = len(arrays)

    def body(*refs):
        srcs, outs = refs[:n], refs[n:2 * n]
        send_sems, recv_sems, local_sems = refs[2 * n:]
        me, sibling, others = _chip_places()

        def copy(a, k, block, to, src=None):
            slot = outs[a].at[_index_of(block)]
            return pltpu.make_async_remote_copy(
                src_ref=slot if src is None else src, dst_ref=slot, send_sem=send_sems.at[7 * a + k],
                recv_sem=recv_sems.at[7 * a + k], device_id=to, device_id_type=pl.DeviceIdType.MESH)

        started = []
        for a, src in enumerate(srcs):
            mine = pltpu.make_async_copy(src, outs[a].at[_index_of(me)], local_sems.at[a])
            mine.start()
            started.append(mine)
        sends = []
        for a, src in enumerate(srcs):
            sends.append(copy(a, 0, me, sibling, src))
            sends += [copy(a, 1 + j, me, chip, src) for j, chip in enumerate(others)]
        for cp in sends:
            cp.start()
        for j, chip in enumerate(others):
            for a in range(n):
                copy(a, 1 + j, chip, me).wait_recv()
                fwd = copy(a, 4 + j, chip, sibling)
                fwd.start()
                sends.append(fwd)
        for a in range(n):
            copy(a, 0, sibling, me).wait_recv()
            for j, chip in enumerate(others):
                copy(a, 4 + j, (chip[0], chip[1], sibling[2]), me).wait_recv()
        for cp in sends:
            cp.wait_send()
        for mine in started:
            mine.wait()

    any_space = pl.BlockSpec(memory_space=pl.ANY)
    return pl.pallas_call(
        body, name=name, in_specs=[any_space] * n, out_specs=[any_space] * n,
        out_shape=[jax.ShapeDtypeStruct((N_DEV,) + a.shape, a.dtype) for a in arrays],
        scratch_shapes=[pltpu.SemaphoreType.DMA((7 * n,)), pltpu.SemaphoreType.DMA((7 * n,)),
                        pltpu.SemaphoreType.DMA((n,))],
    )(*arrays)


def _sibling_swap(arrays, name):
    n = len(arrays)

    def body(*refs):
        srcs, outs = refs[:n], refs[n:2 * n]
        send_sems, recv_sems = refs[2 * n:]
        (x, y, c), sibling, _ = _chip_places()
        sends = []
        for a, (src, out) in enumerate(zip(srcs, outs)):
            for q in range(N_CHIP):
                def copy(core, a=a, q=q, src=src, out=out):
                    return pltpu.make_async_remote_copy(
                        src_ref=src.at[2 * q + core], dst_ref=out.at[q], send_sem=send_sems.at[N_CHIP * a + q],
                        recv_sem=recv_sems.at[N_CHIP * a + q], device_id=sibling, device_id_type=pl.DeviceIdType.MESH)
                mine = copy(1 - c)
                mine.start()
                sends.append((mine, copy(c)))
        for mine, arrival in sends:
            arrival.wait_recv()
            mine.wait_send()

    any_space = pl.BlockSpec(memory_space=pl.ANY)
    return pl.pallas_call(
        body, name=name, in_specs=[any_space] * n, out_specs=[any_space] * n,
        out_shape=[jax.ShapeDtypeStruct((N_CHIP,) + a.shape[1:], a.dtype) for a in arrays],
        scratch_shapes=[pltpu.SemaphoreType.DMA((N_CHIP * n,)), pltpu.SemaphoreType.DMA((N_CHIP * n,))],
    )(*arrays)


def _chip_pair_sum(part, got, name):
    R, C = part.shape[1:]
    tr = R
    while tr * C * part.dtype.itemsize > REDUCE_BLOCK_BYTES // 4 and tr % 32 == 0:
        tr //= 2
    c = lax.axis_index("c")

    def body(c_ref, p_ref, g_ref, o_ref):
        del c_ref
        o_ref[...] = (p_ref[...].astype(F32) + g_ref[...].astype(F32)).astype(o_ref.dtype)

    return pl.pallas_call(
        body, name=name, grid_spec=pltpu.PrefetchScalarGridSpec(
            num_scalar_prefetch=1, grid=(N_CHIP, R // tr),
            in_specs=[pl.BlockSpec((None, tr, C), lambda q, i, cr: (2 * q + cr[0], i, 0)),
                      pl.BlockSpec((None, tr, C), lambda q, i, cr: (q, i, 0))],
            out_specs=pl.BlockSpec((None, tr, C), lambda q, i, cr: (q, i, 0))),
        out_shape=jax.ShapeDtypeStruct((N_CHIP, R, C), part.dtype),
        compiler_params=_cp(),
    )(jnp.reshape(c, (1,)).astype(jnp.int32), part, got)


def _chip_exchange(arrays, name):
    n = len(arrays)

    def body(*refs):
        srcs, outs = refs[:n], refs[n:2 * n]
        send_sems, recv_sems, local_sems = refs[2 * n:]
        (x, y, c), _, others = _chip_places()
        mychip = 2 * x + y
        started, sends = [], []
        for a, (src, out) in enumerate(zip(srcs, outs)):
            mine = pltpu.make_async_copy(src.at[mychip], out.at[mychip], local_sems.at[a])
            mine.start()
            started.append(mine)
        for j, chip in enumerate(others):
            q = 2 * chip[0] + chip[1]
            for a, (src, out) in enumerate(zip(srcs, outs)):
                def copy(slot, a=a, j=j, q=q, chip=chip, src=src, out=out):
                    return pltpu.make_async_remote_copy(
                        src_ref=src.at[q], dst_ref=out.at[slot], send_sem=send_sems.at[3 * a + j],
                        recv_sem=recv_sems.at[3 * a + j], device_id=chip, device_id_type=pl.DeviceIdType.MESH)
                mine = copy(mychip)
                mine.start()
                sends.append((mine, copy(q)))
        for mine, arrival in sends:
            arrival.wait_recv()
        for mine, arrival in sends:
            mine.wait_send()
        for mine in started:
            mine.wait()

    any_space = pl.BlockSpec(memory_space=pl.ANY)
    return pl.pallas_call(
        body, name=name, in_specs=[any_space] * n, out_specs=[any_space] * n,
        out_shape=[jax.ShapeDtypeStruct(a.shape, a.dtype) for a in arrays],
        scratch_shapes=[pltpu.SemaphoreType.DMA((3 * n,)), pltpu.SemaphoreType.DMA((3 * n,)),
                        pltpu.SemaphoreType.DMA((n,))],
    )(*arrays)


def _remote_copies(srcs, lands, send_sems, recv_sems, gather):
    me = _my_index()
    out = []
    for i, (pos, idx) in enumerate(_peers()):
        for a, (src, land) in enumerate(zip(srcs, lands)):
            def copy(slot, a=a, src=src, land=land, i=i, pos=pos, idx=idx):
                return pltpu.make_async_remote_copy(
                    src_ref=src if gather else src.at[idx], dst_ref=land.at[slot],
                    send_sem=send_sems.at[a * (N_DEV - 1) + i], recv_sem=recv_sems.at[a * (N_DEV - 1) + i],
                    device_id=pos, device_id_type=pl.DeviceIdType.MESH)
            out.append((copy(me), copy(idx)))
    return out


def _exchange_start(arrays, name, gather):
    n = len(arrays)
    hbm = pl.BlockSpec(memory_space=pltpu.HBM)
    sem = pl.BlockSpec(memory_space=pltpu.SEMAPHORE)
    lands = [lax.empty(((N_DEV,) + a.shape) if gather else a.shape, a.dtype) for a in arrays]

    def body(*refs):
        srcs, lands_ = refs[:n], refs[n:2 * n]
        send_sems, recv_sems = refs[2 * n:2 * n + 2]
        for mine, _ in _remote_copies(srcs, lands_, send_sems, recv_sems, gather):
            mine.start()
        refs[-1][...] = jnp.zeros_like(refs[-1])

    sems = pltpu.SemaphoreType.DMA((n * (N_DEV - 1),))
    buffers = [pltpu.HBM(a.shape, a.dtype) for a in list(arrays) + lands]
    res = pl.pallas_call(
        body, name=name, in_specs=[hbm] * (2 * n), out_specs=[sem, sem] + [hbm] * (2 * n) + [pl.BlockSpec(memory_space=pltpu.VMEM)],
        out_shape=[sems, sems] + buffers + [jax.ShapeDtypeStruct((8, LANE), F32)],
        input_output_aliases={i: 2 + i for i in range(2 * n)},
        compiler_params=pltpu.CompilerParams(has_side_effects=pltpu.SideEffectType.DATAFLOW_SIDE_EFFECTING),
    )(*[pltpu.with_memory_space_constraint(a, pltpu.HBM) for a in list(arrays) + lands])
    return (res[0], res[1], res[2:2 + n], res[2 + n:2 + 2 * n]), res[-1]


def _exchange_wait(handle, after, name, gather):
    send_sems, recv_sems, srcs, lands = handle
    n = len(srcs)
    hbm = pl.BlockSpec(memory_space=pltpu.HBM)
    sem = pl.BlockSpec(memory_space=pltpu.SEMAPHORE)

    def body(*refs):
        for mine, arrival in _remote_copies(refs[:n], refs[n:2 * n], refs[2 * n], refs[2 * n + 1], gather):
            mine.wait_send()
            arrival.wait_recv()

    res = pl.pallas_call(
        body, name=name, in_specs=[hbm] * (2 * n) + [sem, sem, pl.BlockSpec(memory_space=pl.ANY)],
        out_specs=[hbm] * (2 * n), out_shape=[pltpu.HBM(a.shape, a.dtype) for a in list(srcs) + list(lands)],
        input_output_aliases={i: i for i in range(2 * n)},
        compiler_params=pltpu.CompilerParams(has_side_effects=pltpu.SideEffectType.DATAFLOW_SIDE_EFFECTING),
    )(*srcs, *lands, send_sems, recv_sems, after)
    return res[n:]


def _own_slot(land, mine):
    return lax.dynamic_update_slice(land, mine, (_my_index(),) + (0,) * (land.ndim - 1))


def _adamw(w, g, m, v):
    m = ADAM_B1 * m + (1.0 - ADAM_B1) * g
    v = ADAM_B2 * v + (1.0 - ADAM_B2) * (g * g)
    m_hat = m / (1.0 - ADAM_B1 ** ADAM_STEP)
    v_hat = v / (1.0 - ADAM_B2 ** ADAM_STEP)
    delta = -ADAM_LR * (m_hat / (jnp.sqrt(v_hat) + ADAM_EPS) + ADAM_WD * w)
    return delta, m, v


def _reduce_adamw(parts, w, m, v, name):
    nparts = len(parts)
    R, C = parts[0].shape[1:]
    tr = R
    while N_DEV * tr * C * parts[0].dtype.itemsize > REDUCE_BLOCK_BYTES and tr % 32 == 0:
        tr //= 2
    steps = R // tr

    def body(*refs):
        w_ref, m_ref, v_ref, g_ref, d_ref, nm_ref, nv_ref = refs[nparts:]
        for k, p_ref in enumerate(refs[:nparts]):
            @pl.when(pl.program_id(0) // steps == k)
            def _():
                g = p_ref[0].astype(F32)
                for s in range(1, p_ref.shape[0]):
                    g = g + p_ref[s].astype(F32)
                g_ref[...] = g
                d_ref[...], nm_ref[...], nv_ref[...] = _adamw(w_ref[...], g, m_ref[...], v_ref[...])

    def part_spec(k):
        return pl.BlockSpec((parts[k].shape[0], tr, C), lambda i: (0, jnp.clip(i - k * steps, 0, steps - 1), 0))

    row = pl.BlockSpec((tr, C), lambda i: (i, 0))
    return pl.pallas_call(
        body, name=name, grid=(nparts * steps,),
        in_specs=[part_spec(k) for k in range(nparts)] + [row, row, row],
        out_specs=[row] * 4, out_shape=[jax.ShapeDtypeStruct((nparts * R, C), F32)] * 4,
        compiler_params=_cp(),
    )(*parts, w, m, v)


BIG = ("w_in", "w_uq", "w_ukv", "w_out_a", "w_out_b", "w_out_c", "w_o")
SMALL = ("norm_g", "b_gate", "conv_w", "conv_b", "q_a_norm_g", "kv_a_norm_g", "mla_q_norm_g", "mla_k_norm_g",
         "dil_q_norm_g", "dil_k_norm_g")
PACK_ROWS = 128
REDUCE_BLOCK_BYTES = 6 * 1024 * 1024


def _pack_local(tensors):
    flat = jnp.concatenate([t.reshape(-1) for t in tensors])
    pad = (-flat.shape[0]) % (PACK_ROWS * LANE)
    return jnp.concatenate([flat, jnp.zeros((pad,), flat.dtype)]).reshape(-1, LANE)


def _unpack_local(rows, like):
    flat = rows.reshape(-1)
    out, off = [], 0
    for t in like:
        out.append(flat[off:off + t.size].reshape(t.shape))
        off += t.size
    return out


def _cols_to_slots(a):
    k = a.shape[0]
    return a.reshape(k, N_DEV, -1).transpose(1, 0, 2)


def _slots_to_cols(s):
    return s.transpose(1, 0, 2).reshape(s.shape[1], -1)


def _rope_tables(S):
    inv = ROPE_THETA ** (-jnp.arange(0, MLA_ROPE, 2, dtype=F32) / MLA_ROPE)
    ang = jnp.arange(S, dtype=F32)[:, None] * inv[None, :]
    cos, sin = jnp.cos(ang), jnp.sin(ang)
    one = jnp.ones((S, MLA_NOPE), F32)
    z16, z32, z64 = (jnp.zeros((S, n), F32) for n in (16, 32, 64))
    cosp = jnp.concatenate([one, cos, cos, jnp.ones((S, 32), F32)], axis=1)
    sa = jnp.concatenate([z64, -sin, z16, z32], axis=1)
    sb = jnp.concatenate([z64, z16, sin, z32], axis=1)
    return cosp, sa, sb


def _alibi_slopes():
    n = DIL_GROUPS * DIL_HEADS
    m = 2.0 ** (-8.0 * jnp.arange(1, n + 1, dtype=F32) / n)
    return m.reshape(DIL_GROUPS, NPAIR, 2)


def _pad_slots(s):
    n, k, c = s.shape
    return _slots_to_cols(jnp.concatenate([s, jnp.zeros((n, k, LANE - c), s.dtype)], axis=2))


def _layer_params(gw, small, l):
    p = {}
    p["wp"] = _pad_columns(gw["w_in"])
    p["norm_g"] = small["norm_g"][l][None]
    p["b_gate"] = small["b_gate"][l][None]
    p["conv_w"] = gw["conv_w"].transpose(1, 0, 2).reshape(CONV_K, CONV_WIDTH)
    p["conv_b"] = small["conv_b"][l][None]
    p["gq"] = small["q_a_norm_g"][l][None]
    p["gkv"] = small["kv_a_norm_g"][l][None]
    p["wuqp"] = _pad_slots(gw["w_uq"])
    kv = gw["w_ukv"]
    p["wkp"] = _pad_slots(kv[:, :, :MLA_NOPE])
    p["wv"] = kv[:, :, MLA_NOPE:].transpose(1, 0, 2).reshape(MLA_KV_LORA, MLA_HEADS * MLA_V)
    zpad = jnp.zeros((1, LANE - MLA_QK), F32)
    p["gmq"] = jnp.concatenate([small["mla_q_norm_g"][l][None], zpad], axis=1)
    p["gmk"] = jnp.concatenate([small["mla_k_norm_g"][l][None], zpad], axis=1)
    tile = lambda g: jnp.broadcast_to(g[:, None, :], (DIL_GROUPS, DIL_HEADS, DIL_HEAD_DIM)).reshape(1, DIL_QK)
    p["gdq"] = tile(small["dil_q_norm_g"][l])
    p["gdk"] = tile(small["dil_k_norm_g"][l])
    p["woa"], p["wob"], p["woc"] = (_slots_to_cols(gw[n]) for n in ("w_out_a", "w_out_b", "w_out_c"))
    p["wo"] = gw["w_o"].reshape(D_MODEL, D_MODEL)
    return p


def _layer_fwd(x, p, tabs, slopes, B, S):
    proj, ht = _inproj_fwd(x, p["norm_g"], p["wp"])
    ya = _mixa_fwd(proj, p["conv_w"], p["conv_b"], B, S)
    q, k, v = _mla_prep_fwd(proj, p["gq"], p["gkv"], p["wuqp"], p["wkp"], p["wv"], p["gmq"], p["gmk"], *tabs, S)
    ob, lse_b = _mla_attn_fwd(q, k, v, B, S)
    qn, kn, vn = _dil_prep_fwd(proj, p["gdq"], p["gdk"])
    ogs, lses = [], []
    for gi in range(DIL_GROUPS):
        o, lse = _dil_attn_fwd(gi, slopes[gi], qn, kn, vn, B, S)
        ogs.append(o)
        lses.append(lse)
    out = _merge_fwd(x, proj, p["b_gate"], ya, ob, ogs, lses, p["woa"], p["wob"], p["woc"], p["wo"])
    saved = dict(x=x, proj=proj, ht=ht, ya=ya, q=q, k=k, v=v, ob=ob, lse_b=lse_b, qn=qn, kn=kn, vn=vn, ogs=ogs, lses=lses)
    return out, saved


def _layer_bwd(dout, sv, p, tabs, slopes, B, S):
    proj = sv["proj"]
    (dproj, dya, dob, dlb, dg0, dg1, dg2, dl0, dl1, dl2, merged, dpa, dpb, dpc, yb, yc, dbg) = _merge_bwd(
        dout, proj, p["b_gate"], sv["ya"], sv["ob"], sv["ogs"], sv["lses"], p["woa"], p["wob"], p["woc"], p["wo"])
    g = {}
    g["w_o"] = _matmul_tn(merged, dout, "dw_o").reshape(N_DEV, D_MODEL // N_DEV, D_MODEL)
    g["w_out_a"] = _cols_to_slots(_matmul_tn(sv["ya"], dpa, "dw_out_a"))
    g["w_out_b"] = _cols_to_slots(_matmul_tn(yb, dpb, "dw_out_b"))
    g["w_out_c"] = _cols_to_slots(_matmul_tn(yc, dpc, "dw_out_c"))
    g["b_gate"] = dbg[0]
    dproj, st = _mixa_bwd(dproj, dya, proj, p["conv_w"], p["conv_b"], B, S)
    g["conv_w"] = st[0:CONV_K]
    g["conv_b"] = st[CONV_K]
    dq, dk, dv = _mla_attn_bwd(sv["q"], sv["k"], sv["v"], dob, sv["lse_b"], dlb, B, S)
    dproj, dwuqp, dwkp, dwv, dgq, dgkv, dgmq, dgmk = _mla_prep_bwd(
        dproj, dq, dk, dv, proj, p["gq"], p["gkv"], p["wuqp"], p["wkp"], p["wv"], p["gmq"], p["gmk"], *tabs, S)
    g["w_uq"] = _cols_to_slots(dwuqp)[:, :, :MLA_QK]
    g["w_ukv"] = jnp.concatenate([_cols_to_slots(dwkp)[:, :, :MLA_NOPE], _cols_to_slots(dwv)], axis=2)
    g["q_a_norm_g"], g["kv_a_norm_g"] = dgq[0], dgkv[0]
    g["mla_q_norm_g"], g["mla_k_norm_g"] = dgmq[0, :MLA_QK], dgmk[0, :MLA_QK]
    dqkv = None
    for gi, (dog, dlg) in enumerate(((dg0, dl0), (dg1, dl1), (dg2, dl2))):
        dqkv = _dil_attn_bwd(gi, slopes[gi], sv["qn"], sv["kn"], sv["vn"], dog, sv["lses"][gi], dlg, dqkv, B, S)
    dproj, dgdq, dgdk = _dil_prep_bwd(dproj, *dqkv, proj, p["gdq"], p["gdk"])
    g["dil_q_norm_g"] = dgdq.reshape(DIL_GROUPS, DIL_HEADS, DIL_HEAD_DIM).sum(axis=1)
    g["dil_k_norm_g"] = dgdk.reshape(DIL_GROUPS, DIL_HEADS, DIL_HEAD_DIM).sum(axis=1)
    g["w_in"] = _unpad_columns(_matmul_nn(sv["ht"], dproj, "dw_in"))
    dx, dng = _inproj_bwd_x(dproj, p["wp"], sv["x"], p["norm_g"], dout)
    g["norm_g"] = dng[0]
    return dx, g


def _after(token, a):
    return a if token is None else a + token[0:1, 0:1]


def _local_step(x, target, small, B, S, weights_of, grads_out):
    tabs = _rope_tables(S)
    sl = _alibi_slopes()
    slopes = [sl[gi] * float(DIL_PATTERNS[gi][1]) for gi in range(DIL_GROUPS)]
    params, saved = [], []
    for l in range(DEPTH):
        gw, token = weights_of(l, x)
        p = _layer_params(gw, small, l)
        p["norm_g"] = _after(token, p["norm_g"])
        x, sv = _layer_fwd(x, p, tabs, slopes, B, S)
        params.append(p)
        saved.append(sv)
    dout, lparts = _loss_head(x, target)
    sq = jnp.sum(lparts[:, 0, 0])
    token = None
    for l in reversed(range(DEPTH)):
        p = dict(params[l], b_gate=_after(token, params[l]["b_gate"]))
        dout, g = _layer_bwd(dout, saved[l], p, tabs, slopes, B, S)
        token = grads_out(l, g, dout)
    return sq, dout


def kernel(x, norm_g, w_in, b_gate, conv_w, conv_b, q_a_norm_g, w_uq, kv_a_norm_g, w_ukv, mla_q_norm_g, mla_k_norm_g, dil_q_norm_g, dil_k_norm_g, w_out_a, w_out_b, w_out_c, w_o, loss_target, m_norm_g, m_w_in, m_b_gate, m_conv_w, m_conv_b, m_q_a_norm_g, m_w_uq, m_kv_a_norm_g, m_w_ukv, m_mla_q_norm_g, m_mla_k_norm_g, m_dil_q_norm_g, m_dil_k_norm_g, m_w_out_a, m_w_out_b, m_w_out_c, m_w_o, v_norm_g, v_w_in, v_b_gate, v_conv_w, v_conv_b, v_q_a_norm_g, v_w_uq, v_kv_a_norm_g, v_w_ukv, v_mla_q_norm_g, v_mla_k_norm_g, v_dil_q_norm_g, v_dil_k_norm_g, v_w_out_a, v_w_out_b, v_w_out_c, v_w_o):
    names = ("norm_g", "w_in", "b_gate", "conv_w", "conv_b", "q_a_norm_g", "w_uq", "kv_a_norm_g", "w_ukv",
             "mla_q_norm_g", "mla_k_norm_g", "dil_q_norm_g", "dil_k_norm_g", "w_out_a", "w_out_b", "w_out_c", "w_o")
    w = dict(zip(names, (norm_g, w_in, b_gate, conv_w, conv_b, q_a_norm_g, w_uq, kv_a_norm_g, w_ukv, mla_q_norm_g,
                         mla_k_norm_g, dil_q_norm_g, dil_k_norm_g, w_out_a, w_out_b, w_out_c, w_o)))
    m = dict(zip(names, (m_norm_g, m_w_in, m_b_gate, m_conv_w, m_conv_b, m_q_a_norm_g, m_w_uq, m_kv_a_norm_g, m_w_ukv,
                         m_mla_q_norm_g, m_mla_k_norm_g, m_dil_q_norm_g, m_dil_k_norm_g, m_w_out_a, m_w_out_b,
                         m_w_out_c, m_w_o)))
    v = dict(zip(names, (v_norm_g, v_w_in, v_b_gate, v_conv_w, v_conv_b, v_q_a_norm_g, v_w_uq, v_kv_a_norm_g, v_w_ukv,
                         v_mla_q_norm_g, v_mla_k_norm_g, v_dil_q_norm_g, v_dil_k_norm_g, v_w_out_a, v_w_out_b,
                         v_w_out_c, v_w_o)))
    B, S, _ = x.shape
    me = _my_index()
    cshard = CONV_WIDTH // N_DEV

    shards = [[w[n][l].astype(BF16) for n in BIG] for l in range(DEPTH)]
    state = {}

    def weights_of(l, after):
        if l == 0:
            got = _gather_two_level(shards[0] + [conv_w], "all_gather_weights_0")
            state["gather"], token = _exchange_start(shards[1], "all_gather_weights_1_start", gather=True)
            state["conv_w"] = got[-1]
        else:
            landed = _exchange_wait(state["gather"], after, "all_gather_weights_1_wait", gather=True)
            got, token = [_own_slot(a, s[None]) for a, s in zip(landed, shards[1])], None
        gw = dict(zip(BIG, got))
        gw["conv_w"] = state["conv_w"][:, l]
        return gw, token

    recv, small_parts = {}, {}

    def grads_out(l, g, after):
        small_parts[l] = [g[n] for n in SMALL]
        send = [g[n].astype(BF16) for n in BIG]
        if l == DEPTH - 1:
            state["scatter"], token = _exchange_start(send, "exchange_weight_grads_1_start", gather=False)
            state["sent"] = send
            return token
        landed = _exchange_wait(state["scatter"], after, "exchange_weight_grads_1_wait", gather=False)
        mine = [lax.dynamic_slice_in_dim(s, me, 1, axis=0) for s in state["sent"]]
        recv[DEPTH - 1] = [_own_slot(a, s) for a, s in zip(landed, mine)]
        swapped = _sibling_swap(send, "exchange_weight_grads_0_sibling")
        sums = [_chip_pair_sum(s, t, "chip_pair_sum_" + n) for n, s, t in zip(BIG, send, swapped)]
        recv[l] = _chip_exchange(sums, "exchange_weight_grads_0")
        return None

    sq, grad_x = _local_step(x.reshape(B * S, D_MODEL), loss_target.reshape(B * S, D_MODEL), w, B, S,
                             weights_of, grads_out)
    loss = lax.psum(sq * (0.5 / D_MODEL), AXES)

    res = {}
    for i, n in enumerate(BIG):
        rows = lambda a: a.reshape(-1, a.shape[-1])
        outs = _reduce_adamw([recv[l][i] for l in range(DEPTH)], rows(w[n]), rows(m[n]), rows(v[n]),
                             "reduce_adamw_" + n)
        res[n] = tuple(a.reshape(w[n].shape) for a in outs)
    part = {n: jnp.stack([small_parts[l][i] for l in range(DEPTH)]) for i, n in enumerate(SMALL)}

    def widen(t):
        return lax.dynamic_update_slice(jnp.zeros((DEPTH, CONV_K, CONV_WIDTH), F32), t, (0, 0, me * cshard))

    small_like = [part[n] for n in SMALL]
    pick = lambda d: [widen(d[n]) if n == "conv_w" else d[n] for n in SMALL]
    parts, = _exchange([_pack_local(small_like)], "all_gather_small_grads", gather=True)
    gs, ds, ms, vs = _reduce_adamw([parts], _pack_local(pick(w)), _pack_local(pick(m)), _pack_local(pick(v)),
                                   "reduce_adamw_small")
    for n, t in zip(SMALL, zip(*(_unpack_local(a, small_like) for a in (gs, ds, ms, vs)))):
        if n == "conv_w":
            t = tuple(lax.dynamic_slice(a, (0, 0, me * cshard), (DEPTH, CONV_K, cshard)) for a in t)
        res[n] = t

    out = [loss, grad_x.reshape(B, S, D_MODEL)]
    for i in range(4):
        out += [res[n][i] for n in names]
    return tuple(out)
```

```python
import jax
import jax.numpy as jnp
from jax import lax
from jax.experimental import pallas as pl
from jax.experimental.pallas import tpu as pltpu

F32 = jnp.float32
BF16 = jnp.bfloat16

D_MODEL = 1024
DEPTH = 2
CONV_WIDTH = 512
CONV_K = 3
MLA_HEADS = 8
MLA_Q_LORA = 256
MLA_KV_LORA = 128
MLA_NOPE = 64
MLA_ROPE = 32
MLA_V = 64
MLA_QK = MLA_NOPE + MLA_ROPE
ROPE_THETA = 10000.0
DIL_PATTERNS = ((128, 1), (512, 4), (2048, 16))
DIL_GROUPS = 3
DIL_HEADS = 8
DIL_HEAD_DIM = 64
DIL_WIDTH = DIL_HEADS * DIL_HEAD_DIM
DIL_QK = DIL_GROUPS * DIL_WIDTH
EPS = 1e-6
N_IN = 11168

ADAM_LR = 0.001
ADAM_B1 = 0.9
ADAM_B2 = 0.999
ADAM_EPS = 1e-08
ADAM_WD = 0.01
ADAM_STEP = 10

N_DEV = 8
AXES = ("x", "y", "c")
LANE = 128
HALF = 64
NPAIR = 4

CB_BZ, CB_CZ, CB_GATE = 0, 4, 8
CB_A = 32
CB_QKV = 48
CB_CQ, CB_CKV, CB_KPE = 84, 86, 87
NCB = 88
PP = NCB * LANE
SHARD_COLS = N_IN // N_DEV
NEG = -1e30
VMEM_LIMIT = 56 * 1024 * 1024


def _column_chunks():
    out = []
    col = 0

    def seg(nblocks, block_of):
        nonlocal col
        for i in range(nblocks):
            out.append((col, LANE, block_of(i)))
            col += LANE

    seg(4, lambda j: CB_A + 4 * j)
    seg(4, lambda j: CB_A + 4 * j + 1)
    seg(4, lambda j: CB_A + 4 * j + 2)
    seg(4, lambda j: CB_A + 4 * j + 3)
    seg(2, lambda i: CB_CQ + i)
    seg(1, lambda i: CB_CKV)
    out.append((col, MLA_ROPE, CB_KPE))
    col += MLA_ROPE
    seg(4, lambda j: CB_BZ + j)
    seg(12, lambda c: CB_QKV + 3 * c)
    seg(12, lambda c: CB_QKV + 3 * c + 1)
    seg(12, lambda c: CB_QKV + 3 * c + 2)
    seg(4, lambda j: CB_CZ + j)
    seg(24, lambda i: CB_GATE + i)
    assert col == N_IN and sorted(c[2] for c in out) == list(range(NCB))
    return out


COLUMN_CHUNKS = _column_chunks()


KPE_CUT = next(i for i, c in enumerate(COLUMN_CHUNKS) if c[1] < LANE) * LANE + MLA_ROPE


def _permute_blocks(u, source, name):
    def body(u_ref, o_ref, sems):
        copies = [pltpu.make_async_copy(u_ref.at[:, pl.ds(s * LANE, LANE)], o_ref.at[:, pl.ds(j * LANE, LANE)],
                                        sems.at[j]) for j, s in enumerate(source)]
        for cp in copies:
            cp.start()
        for cp in copies:
            cp.wait()

    return pl.pallas_call(
        body, name=name, in_specs=[pl.BlockSpec(memory_space=pl.ANY)], out_specs=pl.BlockSpec(memory_space=pl.ANY),
        out_shape=jax.ShapeDtypeStruct(u.shape, u.dtype), scratch_shapes=[pltpu.SemaphoreType.DMA((NCB,))],
    )(u)


def _pad_columns(shards, name):
    w = jnp.concatenate([shards[p] for p in range(N_DEV)], axis=1)
    u = jnp.concatenate([w[:, :KPE_CUT], jnp.zeros((w.shape[0], LANE - MLA_ROPE), w.dtype), w[:, KPE_CUT:]], axis=1)
    at = {b: i for i, (_, _, b) in enumerate(COLUMN_CHUNKS)}
    return _permute_blocks(u, [at[b] for b in range(NCB)], name)


def _unpad_columns(wp, name):
    u = _permute_blocks(wp, [b for _, _, b in COLUMN_CHUNKS], name)
    w = jnp.concatenate([u[:, :KPE_CUT], u[:, KPE_CUT + LANE - MLA_ROPE:]], axis=1)
    return jnp.stack([w[:, p * SHARD_COLS:(p + 1) * SHARD_COLS] for p in range(N_DEV)])


def _cp():
    return pltpu.CompilerParams(vmem_limit_bytes=VMEM_LIMIT)


def _rstd(x, n):
    return lax.rsqrt(jnp.sum(x * x, axis=-1, keepdims=True) * (1.0 / n) + EPS)


def _sigmoid(z):
    return 1.0 / (1.0 + jnp.exp(-z))


def _silu(z):
    return z * _sigmoid(z)


def _dsilu(z):
    s = _sigmoid(z)
    return s * (1.0 + z * (1.0 - s))


def _mm(a, b):
    return jnp.dot(a.astype(BF16), b.astype(BF16), preferred_element_type=F32)


def _mm_nt(a, b):
    return lax.dot_general(a.astype(BF16), b.astype(BF16), (((1,), (1,)), ((), ())), preferred_element_type=F32)


def _mm_tn(a, b):
    return lax.dot_general(a.astype(BF16), b.astype(BF16), (((0,), (0,)), ((), ())), preferred_element_type=F32)


def _lane_lo(shape):
    return lax.broadcasted_iota(jnp.int32, shape, len(shape) - 1) < HALF


def _head_bcast_sum(x, terms=3):
    w = x.shape[-1]
    same = (lax.broadcasted_iota(jnp.int32, (w, w), 0) // HALF) == (lax.broadcasted_iota(jnp.int32, (w, w), 1) // HALF)
    ones = jnp.where(same, 1.0, 0.0).astype(jnp.bfloat16)
    total = None
    for _ in range(terms):
        term = x.astype(jnp.bfloat16)
        x = x - term.astype(F32)
        part = jnp.dot(term, ones, preferred_element_type=F32)
        total = part if total is None else total + part
    return total


def _rope(t, cos, sa, sb):
    return t * cos + pltpu.roll(t, LANE - 16, axis=1) * sa + pltpu.roll(t, 16, axis=1) * sb


def _rope_t(d, cos, sa, sb):
    return d * cos + pltpu.roll(d * sa, 16, axis=1) + pltpu.roll(d * sb, LANE - 16, axis=1)


def _shift_down(u, k):
    rows = lax.broadcasted_iota(jnp.int32, u.shape, 0)
    return jnp.where(rows >= k, pltpu.roll(u, k, axis=0), 0.0)


def _shift_up(u, k):
    n = u.shape[0]
    rows = lax.broadcasted_iota(jnp.int32, u.shape, 0)
    return jnp.where(rows < n - k, pltpu.roll(u, n - k, axis=0), 0.0)


def _tile(n, want):
    t = min(n, want)
    assert n % t == 0, (n, want)
    return t


def _inproj_fwd(x, g, wp):
    T = x.shape[0]
    tm, tn = _tile(T, 2048), 512

    def body(x_ref, g_ref, w_ref, proj_ref, ht_ref, h_ref):
        @pl.when(pl.program_id(1) == 0)
        def _():
            n = min(tm, 512)
            for r0 in range(0, tm, n):
                xv = x_ref[r0:r0 + n, :]
                h = xv * _rstd(xv, D_MODEL) * g_ref[...]
                h_ref[r0:r0 + n, :] = h.astype(BF16)
                ht_ref[:, r0:r0 + n] = h.T.astype(BF16)

        proj_ref[...] = jnp.dot(h_ref[...], w_ref[...], preferred_element_type=F32).astype(BF16)

    return pl.pallas_call(
        body, name="inproj_fwd", grid=(T // tm, PP // tn),
        in_specs=[pl.BlockSpec((tm, D_MODEL), lambda i, j: (i, 0)),
                  pl.BlockSpec((1, D_MODEL), lambda i, j: (0, 0)),
                  pl.BlockSpec((D_MODEL, tn), lambda i, j: (0, j))],
        out_specs=[pl.BlockSpec((tm, tn), lambda i, j: (i, j)),
                   pl.BlockSpec((D_MODEL, tm), lambda i, j: (0, i))],
        out_shape=[jax.ShapeDtypeStruct((T, PP), BF16), jax.ShapeDtypeStruct((D_MODEL, T), BF16)],
        scratch_shapes=[pltpu.VMEM((tm, D_MODEL), BF16)],
        compiler_params=_cp(),
    )(x, g, wp)


def _matmul_nn(at, b, name):
    K, T = at.shape
    N = b.shape[1]
    tt, tn = _tile(T, 1024), _tile(N, 2816)
    nk = T // tt

    def body(a_ref, b_ref, o_ref, acc_ref):
        k = pl.program_id(1)

        @pl.when(k == 0)
        def _():
            acc_ref[...] = jnp.zeros_like(acc_ref)

        acc_ref[...] += jnp.dot(a_ref[...], b_ref[...], preferred_element_type=F32)

        @pl.when(k == nk - 1)
        def _():
            o_ref[...] = acc_ref[...].astype(BF16)

    return pl.pallas_call(
        body, name=name, grid=(N // tn, nk),
        in_specs=[pl.BlockSpec((K, tt), lambda j, k: (0, k)),
                  pl.BlockSpec((tt, tn), lambda j, k: (k, j))],
        out_specs=pl.BlockSpec((K, tn), lambda j, k: (0, j)),
        out_shape=jax.ShapeDtypeStruct((K, N), BF16),
        scratch_shapes=[pltpu.VMEM((K, tn), F32)],
        compiler_params=_cp(),
    )(at, b)


def _matmul_tn(a, b, name):
    T, K = a.shape
    N = b.shape[1]
    tt, tn = _tile(T, 512), _tile(N, 1024)

    def body(a_ref, b_ref, o_ref):
        @pl.when(pl.program_id(1) == 0)
        def _():
            o_ref[...] = jnp.zeros_like(o_ref)

        o_ref[...] += _mm_tn(a_ref[...], b_ref[...])

    return pl.pallas_call(
        body, name=name, grid=(N // tn, T // tt),
        in_specs=[pl.BlockSpec((tt, K), lambda j, k: (k, 0)),
                  pl.BlockSpec((tt, tn), lambda j, k: (k, j))],
        out_specs=pl.BlockSpec((K, tn), lambda j, k: (0, j)),
        out_shape=jax.ShapeDtypeStruct((K, N), F32),
        compiler_params=_cp(),
    )(a, b)


def _inproj_bwd_x(dproj, wp, x, g, dout):
    T = x.shape[0]
    tm, tk = _tile(T, 1024), 1024
    nk = PP // tk

    def body(dp_ref, w_ref, x_ref, g_ref, do_ref, dx_ref, dg_ref, acc_ref):
        i, k = pl.program_id(0), pl.program_id(1)

        @pl.when(k == 0)
        def _():
            acc_ref[...] = jnp.zeros_like(acc_ref)

        @pl.when((k == 0) & (i == 0))
        def _():
            dg_ref[...] = jnp.zeros_like(dg_ref)

        acc_ref[...] += _mm_nt(dp_ref[...], w_ref[...])

        @pl.when(k == nk - 1)
        def _():
            dh = acc_ref[...]
            xv = x_ref[...]
            r = _rstd(xv, D_MODEL)
            gy = dh * g_ref[...]
            dot = jnp.sum(xv * gy, axis=-1, keepdims=True) * (1.0 / D_MODEL)
            dx_ref[...] = do_ref[...] + r * gy - xv * (r * r * r) * dot
            dg_ref[...] += jnp.sum(dh * xv * r, axis=0, keepdims=True)

    return pl.pallas_call(
        body, name="inproj_bwd_x", grid=(T // tm, nk),
        in_specs=[pl.BlockSpec((tm, tk), lambda i, k: (i, k)),
                  pl.BlockSpec((D_MODEL, tk), lambda i, k: (0, k)),
                  pl.BlockSpec((tm, D_MODEL), lambda i, k: (i, 0)),
                  pl.BlockSpec((1, D_MODEL), lambda i, k: (0, 0)),
                  pl.BlockSpec((tm, D_MODEL), lambda i, k: (i, 0))],
        out_specs=[pl.BlockSpec((tm, D_MODEL), lambda i, k: (i, 0)),
                   pl.BlockSpec((1, D_MODEL), lambda i, k: (0, 0))],
        out_shape=[jax.ShapeDtypeStruct((T, D_MODEL), F32), jax.ShapeDtypeStruct((1, D_MODEL), F32)],
        scratch_shapes=[pltpu.VMEM((tm, D_MODEL), F32)],
        compiler_params=_cp(),
    )(dproj, wp, x, g, dout)


def _mixa_fwd(proj, cw, cb, B, S):
    nc = CONV_WIDTH // LANE
    ca = CB_A // 4

    def body(p_ref, cw_ref, cb_ref, y_ref):
        ab, ac, ax, az = (p_ref[:, i * LANE:(i + 1) * LANE].astype(F32) for i in range(4))
        u = ac * ax
        conv = cb_ref[...] + cw_ref[0:1, :] * _shift_down(u, 2) + cw_ref[1:2, :] * _shift_down(u, 1) + cw_ref[2:3, :] * u
        y_ref[...] = (ab * conv * _silu(az)).astype(BF16)

    return pl.pallas_call(
        body, name="mixa_fwd", grid=(B, nc),
        in_specs=[pl.BlockSpec((S, 4 * LANE), lambda b, j: (b, ca + j)),
                  pl.BlockSpec((CONV_K, LANE), lambda b, j: (0, j)),
                  pl.BlockSpec((1, LANE), lambda b, j: (0, j))],
        out_specs=pl.BlockSpec((S, LANE), lambda b, j: (b, j)),
        out_shape=jax.ShapeDtypeStruct((B * S, CONV_WIDTH), BF16),
        compiler_params=_cp(),
    )(proj, cw, cb)


def _mixa_bwd(dproj, dy, proj, cw, cb, B, S):
    nc = CONV_WIDTH // LANE
    ca = CB_A // 4

    def body(dpin_ref, dy_ref, p_ref, cw_ref, cb_ref, dp_ref, st_ref):
        del dpin_ref
        ab, ac, ax, az = (p_ref[:, i * LANE:(i + 1) * LANE].astype(F32) for i in range(4))
        u = ac * ax
        u1, u2 = _shift_down(u, 1), _shift_down(u, 2)
        w0, w1, w2 = cw_ref[0:1, :], cw_ref[1:2, :], cw_ref[2:3, :]
        conv = cb_ref[...] + w0 * u2 + w1 * u1 + w2 * u
        s = _silu(az)
        d = dy_ref[...]
        dconv = d * ab * s
        du = w2 * dconv + w1 * _shift_up(dconv, 1) + w0 * _shift_up(dconv, 2)
        dp_ref[:, 0:LANE] = (d * conv * s).astype(BF16)
        dp_ref[:, LANE:2 * LANE] = (du * ax).astype(BF16)
        dp_ref[:, 2 * LANE:3 * LANE] = (du * ac).astype(BF16)
        dp_ref[:, 3 * LANE:4 * LANE] = (d * ab * conv * _dsilu(az)).astype(BF16)
        row = lax.broadcasted_iota(jnp.int32, (8, LANE), 0)
        st = jnp.zeros((8, LANE), F32)
        for r, v in enumerate((dconv * u2, dconv * u1, dconv * u, dconv)):
            st = st + jnp.where(row == r, jnp.sum(v, axis=0, keepdims=True), 0.0)

        @pl.when(pl.program_id(1) == 0)
        def _():
            st_ref[...] = st

        @pl.when(pl.program_id(1) != 0)
        def _():
            st_ref[...] += st

    return pl.pallas_call(
        body, name="mixa_bwd", grid=(nc, B),
        in_specs=[pl.BlockSpec(memory_space=pl.ANY),
                  pl.BlockSpec((S, LANE), lambda j, b: (b, j)),
                  pl.BlockSpec((S, 4 * LANE), lambda j, b: (b, ca + j)),
                  pl.BlockSpec((CONV_K, LANE), lambda j, b: (0, j)),
                  pl.BlockSpec((1, LANE), lambda j, b: (0, j))],
        out_specs=[pl.BlockSpec((S, 4 * LANE), lambda j, b: (b, ca + j)),
                   pl.BlockSpec((8, LANE), lambda j, b: (0, j))],
        out_shape=[jax.ShapeDtypeStruct(dproj.shape, BF16), jax.ShapeDtypeStruct((8, CONV_WIDTH), F32)],
        input_output_aliases={0: 0},
        compiler_params=_cp(),
    )(dproj, dy, proj, cw, cb)


def _mla_prep_fwd(proj, gq, gkv, wuqp, wkp, wv, gmq, gmk, cos, sa, sb, S):
    T = proj.shape[0]
    ts = _tile(S, 512)
    ns = S // ts
    W = MLA_HEADS * LANE

    def body(p_ref, gq_ref, gkv_ref, wuq_ref, wk_ref, wv_ref, gmq_ref, gmk_ref, cos_ref, sa_ref, sb_ref,
             q_ref, k_ref, v_ref):
        cq = p_ref[:, 0:2 * LANE].astype(F32)
        ckv = p_ref[:, 2 * LANE:3 * LANE].astype(F32)
        kpe = pltpu.roll(p_ref[:, 3 * LANE:4 * LANE].astype(F32), HALF, axis=1)
        cqn = cq * _rstd(cq, MLA_Q_LORA) * gq_ref[...]
        ckn = (ckv * _rstd(ckv, MLA_KV_LORA) * gkv_ref[...]).astype(BF16)
        q0 = _mm(cqn, wuq_ref[...])
        kn = _mm(ckn, wk_ref[...])
        v_ref[...] = _mm(ckn, wv_ref[...]).astype(BF16)
        c, a, b = cos_ref[...], sa_ref[...], sb_ref[...]
        for h in range(MLA_HEADS):
            q0h = q0[:, h * LANE:(h + 1) * LANE]
            q_ref[h] = _rope(q0h * _rstd(q0h, MLA_QK) * gmq_ref[...], c, a, b).astype(BF16)
            k0h = kn[:, h * LANE:(h + 1) * LANE] + kpe
            k_ref[h] = _rope(k0h * _rstd(k0h, MLA_QK) * gmk_ref[...], c, a, b).astype(BF16)

    def whole(r, c):
        return pl.BlockSpec((r, c), lambda i: (0, 0))

    tab = pl.BlockSpec((ts, LANE), lambda i: (i % ns, 0))
    return pl.pallas_call(
        body, name="mla_prep_fwd", grid=(T // ts,),
        in_specs=[pl.BlockSpec((ts, 4 * LANE), lambda i: (i, CB_CQ // 4)),
                  whole(1, MLA_Q_LORA), whole(1, MLA_KV_LORA), whole(MLA_Q_LORA, W), whole(MLA_KV_LORA, W),
                  whole(MLA_KV_LORA, MLA_HEADS * MLA_V), whole(1, LANE), whole(1, LANE), tab, tab, tab],
        out_specs=[pl.BlockSpec((MLA_HEADS, ts, LANE), lambda i: (0, i, 0)),
                   pl.BlockSpec((MLA_HEADS, ts, LANE), lambda i: (0, i, 0)),
                   pl.BlockSpec((ts, MLA_HEADS * MLA_V), lambda i: (i, 0))],
        out_shape=[jax.ShapeDtypeStruct((MLA_HEADS, T, LANE), BF16), jax.ShapeDtypeStruct((MLA_HEADS, T, LANE), BF16),
                   jax.ShapeDtypeStruct((T, MLA_HEADS * MLA_V), BF16)],
        compiler_params=_cp(),
    )(proj, gq, gkv, wuqp, wkp, wv, gmq, gmk, cos, sa, sb)


def _mla_prep_bwd(dproj, dq, dk, dv, proj, gq, gkv, wuqp, wkp, wv, gmq, gmk, cos, sa, sb, S):
    T = proj.shape[0]
    ts = _tile(S, 256)
    ns = S // ts
    W = MLA_HEADS * LANE

    def body(dpin_ref, dq_ref, dk_ref, dv_ref, p_ref, gq_ref, gkv_ref, wuq_ref, wk_ref, wv_ref, gmq_ref, gmk_ref,
             cos_ref, sa_ref, sb_ref,
             dp_ref, dwuq_ref, dwk_ref, dwv_ref, dgq_ref, dgkv_ref, dgmq_ref, dgmk_ref, dq0_ref, dkn_ref):
        del dpin_ref

        @pl.when(pl.program_id(0) == 0)
        def _():
            for r in (dwuq_ref, dwk_ref, dwv_ref, dgq_ref, dgkv_ref, dgmq_ref, dgmk_ref):
                r[...] = jnp.zeros_like(r)

        cq = p_ref[:, 0:2 * LANE].astype(F32)
        ckv = p_ref[:, 2 * LANE:3 * LANE].astype(F32)
        kpe = pltpu.roll(p_ref[:, 3 * LANE:4 * LANE].astype(F32), HALF, axis=1)
        rq = _rstd(cq, MLA_Q_LORA)
        rkv = _rstd(ckv, MLA_KV_LORA)
        gq, gkv, gmq, gmk = gq_ref[...], gkv_ref[...], gmq_ref[...], gmk_ref[...]
        cqn = (cq * rq * gq).astype(BF16)
        ckn = (ckv * rkv * gkv).astype(BF16)
        q0 = _mm(cqn, wuq_ref[...])
        kn = _mm(ckn, wk_ref[...])
        c, a, b = cos_ref[...], sa_ref[...], sb_ref[...]
        lane = lax.broadcasted_iota(jnp.int32, (ts, LANE), 1)
        dgmq = jnp.zeros((1, LANE), F32)
        dgmk = jnp.zeros((1, LANE), F32)
        dkpe = jnp.zeros((ts, LANE), F32)
        for h in range(MLA_HEADS):
            q0h = q0[:, h * LANE:(h + 1) * LANE]
            r = _rstd(q0h, MLA_QK)
            d1 = _rope_t(dq_ref[h], c, a, b)
            gy = d1 * gmq
            dq0_ref[:, h * LANE:(h + 1) * LANE] = (
                r * gy - q0h * (r * r * r) * (jnp.sum(q0h * gy, axis=-1, keepdims=True) * (1.0 / MLA_QK))).astype(BF16)
            dgmq = dgmq + jnp.sum(d1 * q0h * r, axis=0, keepdims=True)
            k0h = kn[:, h * LANE:(h + 1) * LANE] + kpe
            r = _rstd(k0h, MLA_QK)
            d1 = _rope_t(dk_ref[h], c, a, b)
            gy = d1 * gmk
            dk0 = r * gy - k0h * (r * r * r) * (jnp.sum(k0h * gy, axis=-1, keepdims=True) * (1.0 / MLA_QK))
            dgmk = dgmk + jnp.sum(d1 * k0h * r, axis=0, keepdims=True)
            dkn_ref[:, h * LANE:(h + 1) * LANE] = jnp.where(lane < MLA_NOPE, dk0, 0.0).astype(BF16)
            dkpe = dkpe + jnp.where((lane >= MLA_NOPE) & (lane < MLA_QK), dk0, 0.0)
        dq0 = dq0_ref[...]
        dkn = dkn_ref[...]
        dvv = dv_ref[...]
        dwuq_ref[...] += _mm_tn(cqn, dq0)
        dwk_ref[...] += _mm_tn(ckn, dkn)
        dwv_ref[...] += _mm_tn(ckn, dvv)
        dgmq_ref[...] += dgmq
        dgmk_ref[...] += dgmk
        dcqn = _mm_nt(dq0, wuq_ref[...])
        gy = dcqn * gq
        dp_ref[:, 0:2 * LANE] = (
            rq * gy - cq * (rq * rq * rq) * (jnp.sum(cq * gy, axis=-1, keepdims=True) * (1.0 / MLA_Q_LORA))).astype(BF16)
        dgq_ref[...] += jnp.sum(dcqn * cq * rq, axis=0, keepdims=True)
        dckn = _mm_nt(dkn, wk_ref[...]) + _mm_nt(dvv, wv_ref[...])
        gy = dckn * gkv
        dp_ref[:, 2 * LANE:3 * LANE] = (
            rkv * gy - ckv * (rkv * rkv * rkv) * (jnp.sum(ckv * gy, axis=-1, keepdims=True) * (1.0 / MLA_KV_LORA))).astype(BF16)
        dgkv_ref[...] += jnp.sum(dckn * ckv * rkv, axis=0, keepdims=True)
        dp_ref[:, 3 * LANE:4 * LANE] = pltpu.roll(dkpe, HALF, axis=1).astype(BF16)

    def whole(r, c):
        return pl.BlockSpec((r, c), lambda i: (0, 0))

    tab = pl.BlockSpec((ts, LANE), lambda i: (i % ns, 0))
    heads = pl.BlockSpec((MLA_HEADS, ts, LANE), lambda i: (0, i, 0))
    return pl.pallas_call(
        body, name="mla_prep_bwd", grid=(T // ts,),
        in_specs=[pl.BlockSpec(memory_space=pl.ANY), heads, heads,
                  pl.BlockSpec((ts, MLA_HEADS * MLA_V), lambda i: (i, 0)),
                  pl.BlockSpec((ts, 4 * LANE), lambda i: (i, CB_CQ // 4)),
                  whole(1, MLA_Q_LORA), whole(1, MLA_KV_LORA), whole(MLA_Q_LORA, W), whole(MLA_KV_LORA, W),
                  whole(MLA_KV_LORA, MLA_HEADS * MLA_V), whole(1, LANE), whole(1, LANE), tab, tab, tab],
        out_specs=[pl.BlockSpec((ts, 4 * LANE), lambda i: (i, CB_CQ // 4)),
                   whole(MLA_Q_LORA, W), whole(MLA_KV_LORA, W), whole(MLA_KV_LORA, MLA_HEADS * MLA_V),
                   whole(1, MLA_Q_LORA), whole(1, MLA_KV_LORA), whole(1, LANE), whole(1, LANE)],
        out_shape=[jax.ShapeDtypeStruct(dproj.shape, BF16),
                   jax.ShapeDtypeStruct((MLA_Q_LORA, W), F32), jax.ShapeDtypeStruct((MLA_KV_LORA, W), F32),
                   jax.ShapeDtypeStruct((MLA_KV_LORA, MLA_HEADS * MLA_V), F32),
                   jax.ShapeDtypeStruct((1, MLA_Q_LORA), F32), jax.ShapeDtypeStruct((1, MLA_KV_LORA), F32),
                   jax.ShapeDtypeStruct((1, LANE), F32), jax.ShapeDtypeStruct((1, LANE), F32)],
        scratch_shapes=[pltpu.VMEM((ts, W), BF16), pltpu.VMEM((ts, W), BF16)],
        input_output_aliases={0: 0},
        compiler_params=_cp(),
    )(dproj, dq, dk, dv, proj, gq, gkv, wuqp, wkp, wv, gmq, gmk, cos, sa, sb)


def _dil_prep_fwd(proj, gq, gk):
    T = proj.shape[0]
    ts = _tile(T, 512)
    gw = 3 * NPAIR * LANE

    def body(p_ref, gq_ref, gk_ref, q_ref, k_ref, v_ref):
        for c in range(NPAIR):
            t = p_ref[:, 3 * c * LANE:(3 * c + 2) * LANE].astype(F32)
            y = t * lax.rsqrt(_head_bcast_sum(t * t, terms=2) * (1.0 / DIL_HEAD_DIM) + EPS)
            cs = slice(c * LANE, (c + 1) * LANE)
            q_ref[:, cs] = y[:, 0:LANE] * gq_ref[:, cs]
            k_ref[:, cs] = y[:, LANE:2 * LANE] * gk_ref[:, cs]
            v_ref[:, cs] = p_ref[:, (3 * c + 2) * LANE:(3 * c + 3) * LANE].astype(F32)

    col = pl.BlockSpec((1, DIL_WIDTH), lambda i, g: (0, g))
    out = pl.BlockSpec((ts, DIL_WIDTH), lambda i, g: (i, g))
    return pl.pallas_call(
        body, name="dil_prep_fwd", grid=(T // ts, DIL_GROUPS),
        in_specs=[pl.BlockSpec((ts, gw), lambda i, g: (i, CB_QKV * LANE // gw + g)), col, col],
        out_specs=[out, out, out],
        out_shape=[jax.ShapeDtypeStruct((T, DIL_QK), F32)] * 3,
        compiler_params=_cp(),
    )(proj, gq, gk)


def _dil_prep_bwd(dproj, ddq, ddk, ddv, proj, gq, gk):
    T = proj.shape[0]
    ts = _tile(T, 512)
    gw = 3 * NPAIR * LANE

    def body(dpin_ref, ddq_ref, ddk_ref, ddv_ref, p_ref, gq_ref, gk_ref, dp_ref, dgq_ref, dgk_ref):
        del dpin_ref

        @pl.when(pl.program_id(1) == 0)
        def _():
            dgq_ref[...] = jnp.zeros_like(dgq_ref)
            dgk_ref[...] = jnp.zeros_like(dgk_ref)

        for c in range(NPAIR):
            cs = slice(c * LANE, (c + 1) * LANE)
            dp_ref[:, (3 * c + 2) * LANE:(3 * c + 3) * LANE] = ddv_ref[:, cs].astype(BF16)
            t = p_ref[:, 3 * c * LANE:(3 * c + 2) * LANE].astype(F32)
            d = jnp.concatenate([ddq_ref[:, cs], ddk_ref[:, cs]], axis=1)
            gy = d * jnp.concatenate([gq_ref[:, cs], gk_ref[:, cs]], axis=1)
            r = lax.rsqrt(_head_bcast_sum(t * t, terms=2) * (1.0 / DIL_HEAD_DIM) + EPS)
            dot = _head_bcast_sum(t * gy, terms=2) * (1.0 / DIL_HEAD_DIM)
            dp_ref[:, 3 * c * LANE:(3 * c + 2) * LANE] = (r * gy - t * (r * r * r) * dot).astype(BF16)
            part = jnp.sum(d * t * r, axis=0, keepdims=True)
            dgq_ref[:, cs] += part[:, 0:LANE]
            dgk_ref[:, cs] += part[:, LANE:2 * LANE]

    col = pl.BlockSpec((1, DIL_WIDTH), lambda g, i: (0, g))
    tok = pl.BlockSpec((ts, DIL_WIDTH), lambda g, i: (i, g))
    return pl.pallas_call(
        body, name="dil_prep_bwd", grid=(DIL_GROUPS, T // ts),
        in_specs=[pl.BlockSpec(memory_space=pl.ANY), tok, tok, tok,
                  pl.BlockSpec((ts, gw), lambda g, i: (i, CB_QKV * LANE // gw + g)), col, col],
        out_specs=[pl.BlockSpec((ts, gw), lambda g, i: (i, CB_QKV * LANE // gw + g)), col, col],
        out_shape=[jax.ShapeDtypeStruct(dproj.shape, BF16), jax.ShapeDtypeStruct((1, DIL_QK), F32),
                   jax.ShapeDtypeStruct((1, DIL_QK), F32)],
        input_output_aliases={0: 0},
        compiler_params=_cp(),
    )(dproj, ddq, ddk, ddv, proj, gq, gk)


COPY_ROWS = 256


def _to_classes(src_ref, dst_ref, d, L, scale=None):
    n = min(L, COPY_ROWS)
    for r in range(d):
        for c0 in range(0, L, n):
            rows = pl.ds(r + c0 * d, n, stride=d) if d > 1 else pl.ds(c0, n)
            val = src_ref[rows, :]
            if scale is not None:
                val = val * scale
            dst_ref[r * L + c0:r * L + c0 + n, :] = val.astype(dst_ref.dtype)


def _from_classes(src_ref, dst_ref, d, L):
    n = min(L, COPY_ROWS)
    for r in range(d):
        for c0 in range(0, L, n):
            rows = pl.ds(r + c0 * d, n, stride=d) if d > 1 else pl.ds(c0, n)
            dst_ref[rows, :] = src_ref[r * L + c0:r * L + c0 + n, :].astype(dst_ref.dtype)


MLA_TQ, MLA_TK = 512, 512


def _causal_bias(tq, tk, shift):
    row = lax.broadcasted_iota(jnp.int32, (tq, tk), 0)
    col = lax.broadcasted_iota(jnp.int32, (tq, tk), 1)
    return jnp.where(row >= col + shift, 0.0, NEG)


def _mla_specs(S):
    heads = pl.BlockSpec((2, S, LANE), lambda b, j: (j, b, 0))
    pair = pl.BlockSpec((S, LANE), lambda b, j: (b, j))
    return heads, pair


def _mla_attn_fwd(q, k, v, B, S):
    tq = _tile(S, MLA_TQ)
    tk = _tile(tq, MLA_TK)
    nd = tq // tk
    scale = MLA_QK ** -0.5
    heads, pair = _mla_specs(S)

    def body(q_ref, k_ref, v_ref, o_ref, lse_ref):
        lo, lok = _lane_lo((tq, LANE)), _lane_lo((tk, LANE))
        diag = [_causal_bias(tq, tk, i * tk) for i in range(nd)]

        def block(g, _):
            row0 = pl.multiple_of(g * tq, tq)
            rows = pl.ds(row0, tq)
            qs = [q_ref[hh, rows, :] for hh in range(2)]

            one = jnp.ones((), BF16)

            def step(off, carries, bias):
                off = pl.multiple_of(off, tk)
                vt = v_ref[pl.ds(off, tk), :]
                vh = (jnp.where(lok, vt, one), jnp.where(lok, one, vt))
                out = []
                for hh, (m, acc) in enumerate(carries):
                    s = _mm_nt(qs[hh], k_ref[hh, pl.ds(off, tk), :]) * scale
                    if bias is not None:
                        s = s + bias
                    m_new = jnp.maximum(m, jnp.max(s, axis=-1, keepdims=True))
                    p = jnp.exp(s - m_new)
                    out.append((m_new, jnp.exp(m - m_new) * acc + _mm(p, vh[hh])))
                return tuple(out)

            init = (jnp.full((tq, 1), NEG, F32), jnp.zeros((tq, LANE), F32))
            carries = lax.fori_loop(0, g * nd, lambda i, c: step(i * tk, c, None), (init, init))
            for i in range(nd):
                carries = step(row0 + i * tk, carries, diag[i])
            (ma, acca), (mb, accb) = carries
            la, lb = pltpu.roll(acca, HALF, axis=1), pltpu.roll(accb, HALF, axis=1)
            o_ref[rows, :] = jnp.where(lo, acca / la, accb / lb)
            lse_ref[rows, :] = jnp.where(lo, ma + jnp.log(la), mb + jnp.log(lb))
            return 0

        lax.fori_loop(0, S // tq, block, 0)

    return pl.pallas_call(
        body, name="mla_attn_fwd", grid=(B, NPAIR), in_specs=[heads, heads, pair], out_specs=[pair, pair],
        out_shape=[jax.ShapeDtypeStruct((B * S, MLA_HEADS * MLA_V), F32)] * 2,
        compiler_params=_cp(),
    )(q, k, v)


DIL_UNROLL = 8


def _dil_geometry(gi, S):
    span, d = DIL_PATTERNS[gi]
    L = S // d
    t = _tile(L, 128)
    window = span // d
    back = min(-(-window // t) * t, L - t)
    return d, L, t, window, back


def _dil_specs(gi, S):
    qk = pl.BlockSpec((S, LANE), lambda b, j: (b, NPAIR * gi + j))
    pair = pl.BlockSpec((S, LANE), lambda b, j: (b, j))
    return qk, qk, pair


def _dil_bias(bias_ref, sl_ref, j, t, kw, back, window):
    row = lax.broadcasted_iota(jnp.int32, (2 * t, kw), 0)
    col = lax.broadcasted_iota(jnp.int32, (2 * t, kw), 1)
    second = row >= t
    slope = jnp.where(second, sl_ref[j, 1], sl_ref[j, 0])
    for n in range(bias_ref.shape[0]):
        dist = jnp.where(second, row - t, row) + n * back - col
        bias_ref[n] = jnp.where((dist >= 0) & (dist <= window), -slope * dist.astype(F32), NEG)


def _stack_heads(x, lo):
    zero = jnp.zeros((), x.dtype)
    return jnp.concatenate([jnp.where(lo, x, zero), jnp.where(lo, zero, x)], axis=0)


def _dil_attn_fwd(gi, slopes, qn, kn, proj, B, S):
    d, L, t, window, back = _dil_geometry(gi, S)
    kw, nq = back + t, L // t
    nbias = 2 if back else 1
    qk, vspec, pair = _dil_specs(gi, S)

    def body(sl_ref, q_ref, k_ref, v_ref, o_ref, lse_ref, qs, ks, vs, os_, ls, bias_ref):
        _to_classes(q_ref, qs, d, L, DIL_HEAD_DIM ** -0.5)
        _to_classes(k_ref, ks, d, L)
        _to_classes(v_ref, vs, d, L)
        _dil_bias(bias_ref, sl_ref, pl.program_id(1), t, kw, back, window)
        lo = _lane_lo((t, LANE))

        def block(g, _):
            qb = g % nq if d > 1 else g
            row0 = pl.multiple_of(g * t, t)
            rows = pl.ds(row0, t)
            early = qb * t < back
            keys = pl.ds(pl.multiple_of(jnp.where(early, row0 - qb * t, row0 - back), t), kw)
            s = _mm_nt(_stack_heads(qs[rows, :], lo), ks[keys, :]) + bias_ref[jnp.where(early, 0, nbias - 1)]
            m = jnp.max(s, axis=-1, keepdims=True)
            p = jnp.exp(s - m)
            l = jnp.sum(p, axis=-1, keepdims=True)
            o2 = _mm(p, vs[keys, :]) / l
            lse2 = m + jnp.log(l)
            os_[rows, :] = jnp.where(lo, o2[:t], o2[t:])
            ls[rows, :] = jnp.where(lo, lse2[:t], lse2[t:])
            return 0

        lax.fori_loop(0, d * nq, block, 0, unroll=DIL_UNROLL if d * nq % DIL_UNROLL == 0 else 1)
        _from_classes(os_, o_ref, d, L)
        _from_classes(ls, lse_ref, d, L)

    return pl.pallas_call(
        body, name=f"dil_attn_fwd_{gi}", grid=(B, NPAIR),
        in_specs=[pl.BlockSpec(memory_space=pltpu.SMEM), qk, qk, vspec], out_specs=[pair, pair],
        out_shape=[jax.ShapeDtypeStruct((B * S, DIL_WIDTH), F32)] * 2,
        scratch_shapes=[pltpu.VMEM((S, LANE), BF16)] * 3 + [pltpu.VMEM((S, LANE), F32)] * 2
                       + [pltpu.VMEM((nbias, 2 * t, kw), F32)],
        compiler_params=_cp(),
    )(slopes, qn, kn, proj)


def _mla_attn_bwd(q, k, v, do, lse, delta, B, S):
    T = B * S
    tq = _tile(S, MLA_TQ)
    tk = _tile(tq, MLA_TK)
    nd = tq // tk
    scale = MLA_QK ** -0.5
    heads, pair = _mla_specs(S)

    def body(q_ref, k_ref, v_ref, do_ref, lse_ref, dl_ref, dq_ref, dk_ref, dv_ref):
        dk_ref[...] = jnp.zeros_like(dk_ref)
        dv_ref[...] = jnp.zeros_like(dv_ref)
        lo = _lane_lo((tq, LANE))
        diag = [_causal_bias(tq, tk, i * tk) for i in range(nd)]

        def block(g, _):
            row0 = pl.multiple_of(g * tq, tq)
            rows = pl.ds(row0, tq)
            for hh in range(2):
                sel = lo if hh == 0 else jnp.logical_not(lo)
                qh = q_ref[hh, rows, :]
                doh = jnp.where(sel, do_ref[rows, :], jnp.zeros((), BF16))
                lse_h = jnp.max(jnp.where(sel, lse_ref[rows, :], NEG), axis=-1, keepdims=True)
                dl_h = jnp.max(jnp.where(sel, dl_ref[rows, :], NEG), axis=-1, keepdims=True)

                def step(off, dq_acc, bias, hh=hh, qh=qh, doh=doh, lse_h=lse_h, dl_h=dl_h):
                    cols = pl.ds(pl.multiple_of(off, tk), tk)
                    kh = k_ref[hh, cols, :]
                    s = _mm_nt(qh, kh) * scale
                    if bias is not None:
                        s = s + bias
                    p = jnp.exp(s - lse_h)
                    dp = _mm_nt(doh, v_ref[cols, :])
                    ds = (p * (dp - dl_h)).astype(BF16)
                    dk_ref[hh, cols, :] += _mm_tn(ds, qh) * scale
                    dv_ref[cols, :] += _mm_tn(p, doh)
                    return dq_acc + _mm(ds, kh)

                dq_acc = lax.fori_loop(0, g * nd, lambda i, a: step(i * tk, a, None), jnp.zeros((tq, LANE), F32))
                for i in range(nd):
                    dq_acc = step(row0 + i * tk, dq_acc, diag[i])
                dq_ref[hh, rows, :] = dq_acc * scale
            return 0

        lax.fori_loop(0, S // tq, block, 0)

    return pl.pallas_call(
        body, name="mla_attn_bwd", grid=(B, NPAIR), in_specs=[heads, heads, pair, pair, pair, pair],
        out_specs=[heads, heads, pair],
        out_shape=[jax.ShapeDtypeStruct((MLA_HEADS, T, LANE), F32), jax.ShapeDtypeStruct((MLA_HEADS, T, LANE), F32),
                   jax.ShapeDtypeStruct((T, MLA_HEADS * MLA_V), F32)],
        compiler_params=_cp(),
    )(q, k, v, do, lse, delta)


def _dil_attn_bwd(gi, slopes, qn, kn, proj, do, lse, delta, through, B, S):
    d, L, t, window, back = _dil_geometry(gi, S)
    kw, nq = back + t, L // t
    nbias = 2 if back else 1
    scale = DIL_HEAD_DIM ** -0.5
    qk, vspec, pair = _dil_specs(gi, S)

    def body(*refs):
        refs = list(refs)
        sl_ref, q_ref, k_ref, v_ref, do_ref, lse_ref, dl_ref = refs[:7]
        dq_ref, dk_ref, dv_ref, qs, ks, vs, dos, lss, dls, dqs, dks, dvs, bias_ref = refs[-13:]
        _to_classes(q_ref, qs, d, L, scale)
        for src, dst in ((k_ref, ks), (v_ref, vs), (do_ref, dos), (lse_ref, lss), (dl_ref, dls)):
            _to_classes(src, dst, d, L)
        _dil_bias(bias_ref, sl_ref, pl.program_id(1), t, kw, back, window)
        dks[...] = jnp.zeros_like(dks)
        dvs[...] = jnp.zeros_like(dvs)
        lo = _lane_lo((t, LANE))

        def stats(ref, rows):
            x = ref[rows, :]
            return jnp.concatenate([jnp.max(jnp.where(lo, x, NEG), axis=-1, keepdims=True),
                                    jnp.max(jnp.where(lo, NEG, x), axis=-1, keepdims=True)], axis=0)

        def block(g, _):
            qb = g % nq if d > 1 else g
            row0 = pl.multiple_of(g * t, t)
            rows = pl.ds(row0, t)
            early = qb * t < back
            keys = pl.ds(pl.multiple_of(jnp.where(early, row0 - qb * t, row0 - back), t), kw)
            q2 = _stack_heads(qs[rows, :], lo)
            do2 = _stack_heads(dos[rows, :], lo)
            kt = ks[keys, :]
            s = _mm_nt(q2, kt) + bias_ref[jnp.where(early, 0, nbias - 1)]
            p = jnp.exp(s - stats(lss, rows))
            ds = (p * (_mm_nt(do2, vs[keys, :]) - stats(dls, rows))).astype(BF16)
            dq2 = _mm(ds, kt) * scale
            dqs[rows, :] = jnp.where(lo, dq2[:t], dq2[t:])
            dks[keys, :] += _mm_tn(ds, q2)
            dvs[keys, :] += _mm_tn(p, do2)
            return 0

        lax.fori_loop(0, d * nq, block, 0, unroll=DIL_UNROLL if d * nq % DIL_UNROLL == 0 else 1)
        for src, dst in ((dqs, dq_ref), (dks, dk_ref), (dvs, dv_ref)):
            _from_classes(src, dst, d, L)

    in_specs = [pl.BlockSpec(memory_space=pltpu.SMEM), qk, qk, vspec, pair, pair, pair]
    args = [slopes, qn, kn, proj, do, lse, delta]
    aliases = {}
    if through is not None:
        aliases = {len(args) + i: i for i in range(3)}
        in_specs = in_specs + [pl.BlockSpec(memory_space=pl.ANY)] * 3
        args = args + list(through)
    return pl.pallas_call(
        body, name=f"dil_attn_bwd_{gi}", grid=(B, NPAIR), in_specs=in_specs, out_specs=[qk, qk, qk],
        out_shape=[jax.ShapeDtypeStruct((B * S, DIL_QK), F32)] * 3,
        scratch_shapes=[pltpu.VMEM((S, LANE), BF16)] * 4 + [pltpu.VMEM((S, LANE), F32)] * 5
                       + [pltpu.VMEM((nbias, 2 * t, kw), F32)],
        input_output_aliases=aliases,
        compiler_params=_cp(),
    )(*args)


def _merge_common(p_ref, bg_ref, ob_ref, og_refs, lse_refs):
    bz = p_ref[:, CB_BZ * LANE:(CB_BZ + 4) * LANE].astype(F32)
    cz = p_ref[:, CB_CZ * LANE:(CB_CZ + 4) * LANE].astype(F32)
    gates = [_sigmoid(p_ref[:, (CB_GATE + 8 * i) * LANE:(CB_GATE + 8 * i + 8) * LANE].astype(F32)
                      + bg_ref[:, i * D_MODEL:(i + 1) * D_MODEL]) for i in range(3)]
    ob = ob_ref[...]
    lses = [r[...] for r in lse_refs]
    mx = jnp.maximum(jnp.maximum(lses[0], lses[1]), lses[2])
    es = [jnp.exp(v - mx) for v in lses]
    inv = 1.0 / (es[0] + es[1] + es[2])
    alphas = [e * inv for e in es]
    oc = alphas[0] * og_refs[0][...] + alphas[1] * og_refs[1][...] + alphas[2] * og_refs[2][...]
    return bz, cz, gates, ob, alphas, oc


def _merge_fwd(x, proj, b_gate, ya, ob, ogs, lses, woa, wob, woc, wo):
    T = x.shape[0]
    ts = _tile(T, 256)
    MW = 32 * LANE

    def body(x_ref, p_ref, bg_ref, ya_ref, ob_ref, og0, og1, og2, l0, l1, l2, woa_ref, wob_ref, woc_ref, wo_ref, out_ref):
        bz, cz, gates, obv, alphas, oc = _merge_common(p_ref, bg_ref, ob_ref, (og0, og1, og2), (l0, l1, l2))
        yb = obv * _silu(bz)
        yc = oc * _silu(cz)
        merged = (gates[0] * _mm(ya_ref[...], woa_ref[...]) + gates[1] * _mm(yb, wob_ref[...])
                  + gates[2] * _mm(yc, woc_ref[...]))
        out_ref[...] = x_ref[...] + _mm(merged, wo_ref[...])

    def whole(r, c):
        return pl.BlockSpec((r, c), lambda i: (0, 0))

    tok = lambda w: pl.BlockSpec((ts, w), lambda i: (i, 0))
    return pl.pallas_call(
        body, name="merge_fwd", grid=(T // ts,),
        in_specs=[tok(D_MODEL), tok(MW), whole(1, 3 * D_MODEL), tok(CONV_WIDTH)] + [tok(DIL_WIDTH)] * 7
                 + [whole(CONV_WIDTH, D_MODEL)] * 3 + [whole(D_MODEL, D_MODEL)],
        out_specs=tok(D_MODEL),
        out_shape=jax.ShapeDtypeStruct((T, D_MODEL), F32),
        compiler_params=_cp(),
    )(x, proj, b_gate, ya, ob, *ogs, *lses, woa, wob, woc, wo)


def _merge_bwd(dout, proj, b_gate, ya, ob, ogs, lses, woa, wob, woc, wo):
    T = dout.shape[0]
    ts = _tile(T, 256)
    MW = 32 * LANE

    def body(do_ref, p_ref, bg_ref, ya_ref, ob_ref, og0, og1, og2, l0, l1, l2, woa_ref, wob_ref, woc_ref, wo_ref,
             dp_ref, dya_ref, dob_ref, dlb_ref, dg0, dg1, dg2, dl0, dl1, dl2,
             mg_ref, dpa_ref, dpb_ref, dpc_ref, yb_ref, yc_ref, dbg_ref):
        bz, cz, gates, obv, alphas, oc = _merge_common(p_ref, bg_ref, ob_ref, (og0, og1, og2), (l0, l1, l2))
        sb, sc = _silu(bz), _silu(cz)
        yb = obv * sb
        yc = oc * sc
        ps = [_mm(ya_ref[...], woa_ref[...]), _mm(yb, wob_ref[...]), _mm(yc, woc_ref[...])]
        mg_ref[...] = (gates[0] * ps[0] + gates[1] * ps[1] + gates[2] * ps[2]).astype(BF16)
        yb_ref[...] = yb.astype(BF16)
        yc_ref[...] = yc.astype(BF16)
        dm = _mm_nt(do_ref[...], wo_ref[...])
        dps = []
        first = pl.program_id(0) == 0
        for i, dref in enumerate((dpa_ref, dpb_ref, dpc_ref)):
            g = gates[i]
            dpi = (dm * g).astype(BF16)
            dref[...] = dpi
            dps.append(dpi)
            dgp = dm * ps[i] * g * (1.0 - g)
            dp_ref[:, (CB_GATE + 8 * i) * LANE:(CB_GATE + 8 * i + 8) * LANE] = dgp.astype(BF16)
            part = jnp.sum(dgp, axis=0, keepdims=True)

            @pl.when(first)
            def _():
                dbg_ref[:, i * D_MODEL:(i + 1) * D_MODEL] = part

            @pl.when(jnp.logical_not(first))
            def _():
                dbg_ref[:, i * D_MODEL:(i + 1) * D_MODEL] += part

        dya_ref[...] = _mm_nt(dps[0], woa_ref[...])
        dyb = _mm_nt(dps[1], wob_ref[...])
        dyc = _mm_nt(dps[2], woc_ref[...])
        dp_ref[:, CB_BZ * LANE:(CB_BZ + 4) * LANE] = (dyb * obv * _dsilu(bz)).astype(BF16)
        dp_ref[:, CB_CZ * LANE:(CB_CZ + 4) * LANE] = (dyc * oc * _dsilu(cz)).astype(BF16)
        dob = dyb * sb
        doc = dyc * sc
        dob_ref[...] = dob.astype(BF16)
        for c in range(NPAIR):
            cs = slice(c * LANE, (c + 1) * LANE)
            dlb_ref[:, cs] = _head_bcast_sum(dob[:, cs] * obv[:, cs])
            dd = _head_bcast_sum(doc[:, cs] * oc[:, cs])
            for a, dref, lref in zip(alphas, (dg0, dg1, dg2), (dl0, dl1, dl2)):
                dref[:, cs] = a[:, cs] * doc[:, cs]
                lref[:, cs] = a[:, cs] * dd

    def whole(r, c):
        return pl.BlockSpec((r, c), lambda i: (0, 0))

    tok = lambda w: pl.BlockSpec((ts, w), lambda i: (i, 0))
    sd = jax.ShapeDtypeStruct
    W = DIL_WIDTH
    return pl.pallas_call(
        body, name="merge_bwd", grid=(T // ts,),
        in_specs=[tok(D_MODEL), tok(MW), whole(1, 3 * D_MODEL), tok(CONV_WIDTH)] + [tok(W)] * 7
                 + [whole(CONV_WIDTH, D_MODEL)] * 3 + [whole(D_MODEL, D_MODEL)],
        out_specs=[tok(MW), tok(CONV_WIDTH), tok(W), tok(W)] + [tok(W)] * 6
                  + [tok(D_MODEL)] * 4 + [tok(W), tok(W), whole(1, 3 * D_MODEL)],
        out_shape=[sd((T, PP), BF16), sd((T, CONV_WIDTH), F32), sd((T, W), BF16), sd((T, W), F32)]
                  + [sd((T, W), F32)] * 6
                  + [sd((T, D_MODEL), BF16)] * 4 + [sd((T, W), BF16)] * 2 + [sd((1, 3 * D_MODEL), F32)],
        compiler_params=_cp(),
    )(dout, proj, b_gate, ya, ob, *ogs, *lses, woa, wob, woc, wo)


def _loss_head(y, target):
    T = y.shape[0]
    ts = _tile(T, 512)

    def body(y_ref, t_ref, d_ref, l_ref):
        e = y_ref[...] - t_ref[...]
        d_ref[...] = e * (1.0 / D_MODEL)
        l_ref[...] = jnp.zeros((1, 8, LANE), F32) + jnp.sum(e * e)

    tok = pl.BlockSpec((ts, D_MODEL), lambda i: (i, 0))
    return pl.pallas_call(
        body, name="loss_head", grid=(T // ts,), in_specs=[tok, tok],
        out_specs=[tok, pl.BlockSpec((1, 8, LANE), lambda i: (i, 0, 0))],
        out_shape=[jax.ShapeDtypeStruct((T, D_MODEL), F32), jax.ShapeDtypeStruct((T // ts, 8, LANE), F32)],
        compiler_params=_cp(),
    )(y, target)


def _my_index():
    return 4 * lax.axis_index("x") + 2 * lax.axis_index("y") + lax.axis_index("c")


def _peers():
    x, y, c = (lax.axis_index(a) for a in AXES)
    out = []
    for kk in range(1, N_DEV):
        px = 1 - x if kk & 4 else x
        py = 1 - y if kk & 2 else y
        pc = 1 - c if kk & 1 else c
        out.append(((px, py, pc), 4 * px + 2 * py + pc))
    return out


def _exchange(arrays, name, gather):
    n = len(arrays)

    def body(*refs):
        srcs, outs = refs[:n], refs[n:2 * n]
        send_sems, recv_sems, local_sems = refs[2 * n:]
        me = _my_index()
        peers = _peers()
        started = []
        for a, (src, out) in enumerate(zip(srcs, outs)):
            mine = pltpu.make_async_copy(src if gather else src.at[me], out.at[me], local_sems.at[a])
            mine.start()
            started.append(mine)
        sends = []
        for i, (pos, idx) in enumerate(peers):
            for a, (src, out) in enumerate(zip(srcs, outs)):
                cp = pltpu.make_async_remote_copy(
                    src_ref=src if gather else src.at[idx], dst_ref=out.at[me], send_sem=send_sems.at[a, i],
                    recv_sem=recv_sems.at[a, i], device_id=pos, device_id_type=pl.DeviceIdType.MESH)
                cp.start()
                sends.append(cp)
        for i, (pos, idx) in enumerate(peers):
            for a, (src, out) in enumerate(zip(srcs, outs)):
                pltpu.make_async_remote_copy(
                    src_ref=src if gather else src.at[idx], dst_ref=out.at[idx], send_sem=send_sems.at[a, i],
                    recv_sem=recv_sems.at[a, i], device_id=pos, device_id_type=pl.DeviceIdType.MESH).wait_recv()
        for cp in sends:
            cp.wait_send()
        for mine in started:
            mine.wait()

    any_space = pl.BlockSpec(memory_space=pl.ANY)
    return pl.pallas_call(
        body, name=name, in_specs=[any_space] * n, out_specs=[any_space] * n,
        out_shape=[jax.ShapeDtypeStruct(((N_DEV,) + a.shape) if gather else a.shape, a.dtype) for a in arrays],
        scratch_shapes=[pltpu.SemaphoreType.DMA((n, N_DEV - 1)), pltpu.SemaphoreType.DMA((n, N_DEV - 1)),
                        pltpu.SemaphoreType.DMA((n,))],
    )(*arrays)


N_CHIP = 4


def _chip_places():
    x, y, c = (lax.axis_index(a) for a in AXES)
    return (x, y, c), (x, y, 1 - c), [(1 - x, y, c), (x, 1 - y, c), (1 - x, 1 - y, c)]


def _index_of(pos):
    return 4 * pos[0] + 2 * pos[1] + pos[2]


def _gather_two_level(arrays, name):
    n = len(arrays)

    def body(*refs):
        srcs, outs = refs[:n], refs[n:2 * n]
        send_sems, recv_sems, local_sems = refs[2 * n:]
        me, sibling, others = _chip_places()

        def copy(a, k, block, to, src=None):
            slot = outs[a].at[_index_of(block)]
            return pltpu.make_async_remote_copy(
                src_ref=slot if src is None else src, dst_ref=slot, send_sem=send_sems.at[7 * a + k],
                recv_sem=recv_sems.at[7 * a + k], device_id=to, device_id_type=pl.DeviceIdType.MESH)

        started = []
        for a, src in enumerate(srcs):
            mine = pltpu.make_async_copy(src, outs[a].at[_index_of(me)], local_sems.at[a])
            mine.start()
            started.append(mine)
        sends = []
        for a, src in enumerate(srcs):
            sends.append(copy(a, 0, me, sibling, src))
            sends += [copy(a, 1 + j, me, chip, src) for j, chip in enumerate(others)]
        for cp in sends:
            cp.start()
        for j, chip in enumerate(others):
            for a in range(n):
                copy(a, 1 + j, chip, me).wait_recv()
                fwd = copy(a, 4 + j, chip, sibling)
                fwd.start()
                sends.append(fwd)
        for a in range(n):
            copy(a, 0, sibling, me).wait_recv()
            for j, chip in enumerate(others):
                copy(a, 4 + j, (chip[0], chip[1], sibling[2]), me).wait_recv()
        for cp in sends:
            cp.wait_send()
        for mine in started:
            mine.wait()

    any_space = pl.BlockSpec(memory_space=pl.ANY)
    return pl.pallas_call(
        body, name=name, in_specs=[any_space] * n, out_specs=[any_space] * n,
        out_shape=[jax.ShapeDtypeStruct((N_DEV,) + a.shape, a.dtype) for a in arrays],
        scratch_shapes=[pltpu.SemaphoreType.DMA((7 * n,)), pltpu.SemaphoreType.DMA((7 * n,)),
                        pltpu.SemaphoreType.DMA((n,))],
    )(*arrays)


def _sibling_swap(arrays, name):
    n = len(arrays)

    def body(*refs):
        srcs, outs = refs[:n], refs[n:2 * n]
        send_sems, recv_sems = refs[2 * n:]
        (x, y, c), sibling, _ = _chip_places()
        sends = []
        for a, (src, out) in enumerate(zip(srcs, outs)):
            for q in range(N_CHIP):
                def copy(core, a=a, q=q, src=src, out=out):
                    return pltpu.make_async_remote_copy(
                        src_ref=src.at[2 * q + core], dst_ref=out.at[q], send_sem=send_sems.at[N_CHIP * a + q],
                        recv_sem=recv_sems.at[N_CHIP * a + q], device_id=sibling, device_id_type=pl.DeviceIdType.MESH)
                mine = copy(1 - c)
                mine.start()
                sends.append((mine, copy(c)))
        for mine, arrival in sends:
            arrival.wait_recv()
            mine.wait_send()

    any_space = pl.BlockSpec(memory_space=pl.ANY)
    return pl.pallas_call(
        body, name=name, in_specs=[any_space] * n, out_specs=[any_space] * n,
        out_shape=[jax.ShapeDtypeStruct((N_CHIP,) + a.shape[1:], a.dtype) for a in arrays],
        scratch_shapes=[pltpu.SemaphoreType.DMA((N_CHIP * n,)), pltpu.SemaphoreType.DMA((N_CHIP * n,))],
    )(*arrays)


def _chip_pair_sum(part, got, name):
    R, C = part.shape[1:]
    tr = R
    while tr * C * part.dtype.itemsize > REDUCE_BLOCK_BYTES // 4 and tr % 32 == 0:
        tr //= 2
    c = lax.axis_index("c")

    def body(c_ref, p_ref, g_ref, o_ref):
        del c_ref
        o_ref[...] = (p_ref[...].astype(F32) + g_ref[...].astype(F32)).astype(o_ref.dtype)

    return pl.pallas_call(
        body, name=name, grid_spec=pltpu.PrefetchScalarGridSpec(
            num_scalar_prefetch=1, grid=(N_CHIP, R // tr),
            in_specs=[pl.BlockSpec((None, tr, C), lambda q, i, cr: (2 * q + cr[0], i, 0)),
                      pl.BlockSpec((None, tr, C), lambda q, i, cr: (q, i, 0))],
            out_specs=pl.BlockSpec((None, tr, C), lambda q, i, cr: (q, i, 0))),
        out_shape=jax.ShapeDtypeStruct((N_CHIP, R, C), part.dtype),
        compiler_params=_cp(),
    )(jnp.reshape(c, (1,)).astype(jnp.int32), part, got)


def _chip_exchange(arrays, name):
    n = len(arrays)

    def body(*refs):
        srcs, outs = refs[:n], refs[n:2 * n]
        send_sems, recv_sems, local_sems = refs[2 * n:]
        (x, y, c), _, others = _chip_places()
        mychip = 2 * x + y
        started, sends = [], []
        for a, (src, out) in enumerate(zip(srcs, outs)):
            mine = pltpu.make_async_copy(src.at[mychip], out.at[mychip], local_sems.at[a])
            mine.start()
            started.append(mine)
        for j, chip in enumerate(others):
            q = 2 * chip[0] + chip[1]
            for a, (src, out) in enumerate(zip(srcs, outs)):
                def copy(slot, a=a, j=j, q=q, chip=chip, src=src, out=out):
                    return pltpu.make_async_remote_copy(
                        src_ref=src.at[q], dst_ref=out.at[slot], send_sem=send_sems.at[3 * a + j],
                        recv_sem=recv_sems.at[3 * a + j], device_id=chip, device_id_type=pl.DeviceIdType.MESH)
                mine = copy(mychip)
                mine.start()
                sends.append((mine, copy(q)))
        for mine, arrival in sends:
            arrival.wait_recv()
        for mine, arrival in sends:
            mine.wait_send()
        for mine in started:
            mine.wait()

    any_space = pl.BlockSpec(memory_space=pl.ANY)
    return pl.pallas_call(
        body, name=name, in_specs=[any_space] * n, out_specs=[any_space] * n,
        out_shape=[jax.ShapeDtypeStruct(a.shape, a.dtype) for a in arrays],
        scratch_shapes=[pltpu.SemaphoreType.DMA((3 * n,)), pltpu.SemaphoreType.DMA((3 * n,)),
                        pltpu.SemaphoreType.DMA((n,))],
    )(*arrays)


def _remote_copies(srcs, lands, send_sems, recv_sems, gather):
    me = _my_index()
    out = []
    for i, (pos, idx) in enumerate(_peers()):
        for a, (src, land) in enumerate(zip(srcs, lands)):
            def copy(slot, a=a, src=src, land=land, i=i, pos=pos, idx=idx):
                return pltpu.make_async_remote_copy(
                    src_ref=src if gather else src.at[idx], dst_ref=land.at[slot],
                    send_sem=send_sems.at[a * (N_DEV - 1) + i], recv_sem=recv_sems.at[a * (N_DEV - 1) + i],
                    device_id=pos, device_id_type=pl.DeviceIdType.MESH)
            out.append((copy(me), copy(idx)))
    return out


def _exchange_start(arrays, name, gather):
    n = len(arrays)
    hbm = pl.BlockSpec(memory_space=pltpu.HBM)
    sem = pl.BlockSpec(memory_space=pltpu.SEMAPHORE)
    lands = [lax.empty(((N_DEV,) + a.shape) if gather else a.shape, a.dtype) for a in arrays]

    def body(*refs):
        srcs, lands_ = refs[:n], refs[n:2 * n]
        send_sems, recv_sems = refs[2 * n:2 * n + 2]
        for mine, _ in _remote_copies(srcs, lands_, send_sems, recv_sems, gather):
            mine.start()
        refs[-1][...] = jnp.zeros_like(refs[-1])

    sems = pltpu.SemaphoreType.DMA((n * (N_DEV - 1),))
    buffers = [pltpu.HBM(a.shape, a.dtype) for a in list(arrays) + lands]
    res = pl.pallas_call(
        body, name=name, in_specs=[hbm] * (2 * n), out_specs=[sem, sem] + [hbm] * (2 * n) + [pl.BlockSpec(memory_space=pltpu.VMEM)],
        out_shape=[sems, sems] + buffers + [jax.ShapeDtypeStruct((8, LANE), F32)],
        input_output_aliases={i: 2 + i for i in range(2 * n)},
        compiler_params=pltpu.CompilerParams(has_side_effects=pltpu.SideEffectType.DATAFLOW_SIDE_EFFECTING),
    )(*[pltpu.with_memory_space_constraint(a, pltpu.HBM) for a in list(arrays) + lands])
    return (res[0], res[1], res[2:2 + n], res[2 + n:2 + 2 * n]), res[-1]


def _exchange_wait(handle, after, name, gather):
    send_sems, recv_sems, srcs, lands = handle
    n = len(srcs)
    hbm = pl.BlockSpec(memory_space=pltpu.HBM)
    sem = pl.BlockSpec(memory_space=pltpu.SEMAPHORE)

    def body(*refs):
        for mine, arrival in _remote_copies(refs[:n], refs[n:2 * n], refs[2 * n], refs[2 * n + 1], gather):
            mine.wait_send()
            arrival.wait_recv()

    res = pl.pallas_call(
        body, name=name, in_specs=[hbm] * (2 * n) + [sem, sem, pl.BlockSpec(memory_space=pl.ANY)],
        out_specs=[hbm] * (2 * n), out_shape=[pltpu.HBM(a.shape, a.dtype) for a in list(srcs) + list(lands)],
        input_output_aliases={i: i for i in range(2 * n)},
        compiler_params=pltpu.CompilerParams(has_side_effects=pltpu.SideEffectType.DATAFLOW_SIDE_EFFECTING),
    )(*srcs, *lands, send_sems, recv_sems, after)
    return res[n:]


def _own_slot(land, mine):
    return lax.dynamic_update_slice(land, mine, (_my_index(),) + (0,) * (land.ndim - 1))


def _adamw(w, g, m, v):
    m = ADAM_B1 * m + (1.0 - ADAM_B1) * g
    v = ADAM_B2 * v + (1.0 - ADAM_B2) * (g * g)
    m_hat = m / (1.0 - ADAM_B1 ** ADAM_STEP)
    v_hat = v / (1.0 - ADAM_B2 ** ADAM_STEP)
    delta = -ADAM_LR * (m_hat / (jnp.sqrt(v_hat) + ADAM_EPS) + ADAM_WD * w)
    return delta, m, v


def _reduce_adamw(parts, w, m, v, name):
    nparts = len(parts)
    R, C = parts[0].shape[1:]
    tr = R
    while N_DEV * tr * C * parts[0].dtype.itemsize > REDUCE_BLOCK_BYTES and tr % 32 == 0:
        tr //= 2
    steps = R // tr

    def body(*refs):
        w_ref, m_ref, v_ref, g_ref, d_ref, nm_ref, nv_ref = refs[nparts:]
        for k, p_ref in enumerate(refs[:nparts]):
            @pl.when(pl.program_id(0) // steps == k)
            def _():
                g = p_ref[0].astype(F32)
                for s in range(1, p_ref.shape[0]):
                    g = g + p_ref[s].astype(F32)
                g_ref[...] = g
                d_ref[...], nm_ref[...], nv_ref[...] = _adamw(w_ref[...], g, m_ref[...], v_ref[...])

    def part_spec(k):
        return pl.BlockSpec((parts[k].shape[0], tr, C), lambda i: (0, jnp.clip(i - k * steps, 0, steps - 1), 0))

    row = pl.BlockSpec((tr, C), lambda i: (i, 0))
    return pl.pallas_call(
        body, name=name, grid=(nparts * steps,),
        in_specs=[part_spec(k) for k in range(nparts)] + [row, row, row],
        out_specs=[row] * 4, out_shape=[jax.ShapeDtypeStruct((nparts * R, C), F32)] * 4,
        compiler_params=_cp(),
    )(*parts, w, m, v)


BIG = ("w_in", "w_uq", "w_ukv", "w_out_a", "w_out_b", "w_out_c", "w_o")
SMALL = ("norm_g", "b_gate", "conv_w", "conv_b", "q_a_norm_g", "kv_a_norm_g", "mla_q_norm_g", "mla_k_norm_g",
         "dil_q_norm_g", "dil_k_norm_g")
PACK_ROWS = 128
REDUCE_BLOCK_BYTES = 6 * 1024 * 1024


def _pack_local(tensors):
    flat = jnp.concatenate([t.reshape(-1) for t in tensors])
    pad = (-flat.shape[0]) % (PACK_ROWS * LANE)
    return jnp.concatenate([flat, jnp.zeros((pad,), flat.dtype)]).reshape(-1, LANE)


def _unpack_local(rows, like):
    flat = rows.reshape(-1)
    out, off = [], 0
    for t in like:
        out.append(flat[off:off + t.size].reshape(t.shape))
        off += t.size
    return out


def _cols_to_slots(a):
    k = a.shape[0]
    return a.reshape(k, N_DEV, -1).transpose(1, 0, 2)


def _slots_to_cols(s):
    return s.transpose(1, 0, 2).reshape(s.shape[1], -1)


def _rope_tables(S):
    inv = ROPE_THETA ** (-jnp.arange(0, MLA_ROPE, 2, dtype=F32) / MLA_ROPE)
    ang = jnp.arange(S, dtype=F32)[:, None] * inv[None, :]
    cos, sin = jnp.cos(ang), jnp.sin(ang)
    one = jnp.ones((S, MLA_NOPE), F32)
    z16, z32, z64 = (jnp.zeros((S, n), F32) for n in (16, 32, 64))
    cosp = jnp.concatenate([one, cos, cos, jnp.ones((S, 32), F32)], axis=1)
    sa = jnp.concatenate([z64, -sin, z16, z32], axis=1)
    sb = jnp.concatenate([z64, z16, sin, z32], axis=1)
    return cosp, sa, sb


def _alibi_slopes():
    n = DIL_GROUPS * DIL_HEADS
    m = 2.0 ** (-8.0 * jnp.arange(1, n + 1, dtype=F32) / n)
    return m.reshape(DIL_GROUPS, NPAIR, 2)


def _pad_slots(s):
    n, k, c = s.shape
    return _slots_to_cols(jnp.concatenate([s, jnp.zeros((n, k, LANE - c), s.dtype)], axis=2))


def _layer_params(gw, small, l):
    p = {}
    p["wp"] = _pad_columns(gw["w_in"], "pad_w_in")
    p["norm_g"] = small["norm_g"][l][None]
    p["b_gate"] = small["b_gate"][l][None]
    p["conv_w"] = gw["conv_w"].transpose(1, 0, 2).reshape(CONV_K, CONV_WIDTH)
    p["conv_b"] = small["conv_b"][l][None]
    p["gq"] = small["q_a_norm_g"][l][None]
    p["gkv"] = small["kv_a_norm_g"][l][None]
    p["wuqp"] = _pad_slots(gw["w_uq"])
    kv = gw["w_ukv"]
    p["wkp"] = _pad_slots(kv[:, :, :MLA_NOPE])
    p["wv"] = kv[:, :, MLA_NOPE:].transpose(1, 0, 2).reshape(MLA_KV_LORA, MLA_HEADS * MLA_V)
    zpad = jnp.zeros((1, LANE - MLA_QK), F32)
    p["gmq"] = jnp.concatenate([small["mla_q_norm_g"][l][None], zpad], axis=1)
    p["gmk"] = jnp.concatenate([small["mla_k_norm_g"][l][None], zpad], axis=1)
    tile = lambda g: jnp.broadcast_to(g[:, None, :], (DIL_GROUPS, DIL_HEADS, DIL_HEAD_DIM)).reshape(1, DIL_QK)
    p["gdq"] = tile(small["dil_q_norm_g"][l])
    p["gdk"] = tile(small["dil_k_norm_g"][l])
    p["woa"], p["wob"], p["woc"] = (_slots_to_cols(gw[n]) for n in ("w_out_a", "w_out_b", "w_out_c"))
    p["wo"] = gw["w_o"].reshape(D_MODEL, D_MODEL)
    return p


def _layer_fwd(x, p, tabs, slopes, B, S):
    proj, ht = _inproj_fwd(x, p["norm_g"], p["wp"])
    ya = _mixa_fwd(proj, p["conv_w"], p["conv_b"], B, S)
    q, k, v = _mla_prep_fwd(proj, p["gq"], p["gkv"], p["wuqp"], p["wkp"], p["wv"], p["gmq"], p["gmk"], *tabs, S)
    ob, lse_b = _mla_attn_fwd(q, k, v, B, S)
    qn, kn, vn = _dil_prep_fwd(proj, p["gdq"], p["gdk"])
    ogs, lses = [], []
    for gi in range(DIL_GROUPS):
        o, lse = _dil_attn_fwd(gi, slopes[gi], qn, kn, vn, B, S)
        ogs.append(o)
        lses.append(lse)
    out = _merge_fwd(x, proj, p["b_gate"], ya, ob, ogs, lses, p["woa"], p["wob"], p["woc"], p["wo"])
    saved = dict(x=x, proj=proj, ht=ht, ya=ya, q=q, k=k, v=v, ob=ob, lse_b=lse_b, qn=qn, kn=kn, vn=vn, ogs=ogs, lses=lses)
    return out, saved


def _layer_bwd(dout, sv, p, tabs, slopes, B, S):
    proj = sv["proj"]
    (dproj, dya, dob, dlb, dg0, dg1, dg2, dl0, dl1, dl2, merged, dpa, dpb, dpc, yb, yc, dbg) = _merge_bwd(
        dout, proj, p["b_gate"], sv["ya"], sv["ob"], sv["ogs"], sv["lses"], p["woa"], p["wob"], p["woc"], p["wo"])
    g = {}
    g["w_o"] = _matmul_tn(merged, dout, "dw_o").reshape(N_DEV, D_MODEL // N_DEV, D_MODEL)
    g["w_out_a"] = _cols_to_slots(_matmul_tn(sv["ya"], dpa, "dw_out_a"))
    g["w_out_b"] = _cols_to_slots(_matmul_tn(yb, dpb, "dw_out_b"))
    g["w_out_c"] = _cols_to_slots(_matmul_tn(yc, dpc, "dw_out_c"))
    g["b_gate"] = dbg[0]
    dproj, st = _mixa_bwd(dproj, dya, proj, p["conv_w"], p["conv_b"], B, S)
    g["conv_w"] = st[0:CONV_K]
    g["conv_b"] = st[CONV_K]
    dq, dk, dv = _mla_attn_bwd(sv["q"], sv["k"], sv["v"], dob, sv["lse_b"], dlb, B, S)
    dproj, dwuqp, dwkp, dwv, dgq, dgkv, dgmq, dgmk = _mla_prep_bwd(
        dproj, dq, dk, dv, proj, p["gq"], p["gkv"], p["wuqp"], p["wkp"], p["wv"], p["gmq"], p["gmk"], *tabs, S)
    g["w_uq"] = _cols_to_slots(dwuqp)[:, :, :MLA_QK]
    g["w_ukv"] = jnp.concatenate([_cols_to_slots(dwkp)[:, :, :MLA_NOPE], _cols_to_slots(dwv)], axis=2)
    g["q_a_norm_g"], g["kv_a_norm_g"] = dgq[0], dgkv[0]
    g["mla_q_norm_g"], g["mla_k_norm_g"] = dgmq[0, :MLA_QK], dgmk[0, :MLA_QK]
    dqkv = None
    for gi, (dog, dlg) in enumerate(((dg0, dl0), (dg1, dl1), (dg2, dl2))):
        dqkv = _dil_attn_bwd(gi, slopes[gi], sv["qn"], sv["kn"], sv["vn"], dog, sv["lses"][gi], dlg, dqkv, B, S)
    dproj, dgdq, dgdk = _dil_prep_bwd(dproj, *dqkv, proj, p["gdq"], p["gdk"])
    g["dil_q_norm_g"] = dgdq.reshape(DIL_GROUPS, DIL_HEADS, DIL_HEAD_DIM).sum(axis=1)
    g["dil_k_norm_g"] = dgdk.reshape(DIL_GROUPS, DIL_HEADS, DIL_HEAD_DIM).sum(axis=1)
    g["w_in"] = _unpad_columns(_matmul_nn(sv["ht"], dproj, "dw_in"), "unpad_dw_in")
    dx, dng = _inproj_bwd_x(dproj, p["wp"], sv["x"], p["norm_g"], dout)
    g["norm_g"] = dng[0]
    return dx, g


def _after(token, a):
    return a if token is None else a + token[0:1, 0:1]


def _local_step(x, target, small, B, S, weights_of, grads_out):
    tabs = _rope_tables(S)
    sl = _alibi_slopes()
    slopes = [sl[gi] * float(DIL_PATTERNS[gi][1]) for gi in range(DIL_GROUPS)]
    params, saved = [], []
    for l in range(DEPTH):
        gw, token = weights_of(l, x)
        p = _layer_params(gw, small, l)
        p["norm_g"] = _after(token, p["norm_g"])
        x, sv = _layer_fwd(x, p, tabs, slopes, B, S)
        params.append(p)
        saved.append(sv)
    dout, lparts = _loss_head(x, target)
    sq = jnp.sum(lparts[:, 0, 0])
    token = None
    for l in reversed(range(DEPTH)):
        p = dict(params[l], b_gate=_after(token, params[l]["b_gate"]))
        dout, g = _layer_bwd(dout, saved[l], p, tabs, slopes, B, S)
        token = grads_out(l, g, dout)
    return sq, dout


def kernel(x, norm_g, w_in, b_gate, conv_w, conv_b, q_a_norm_g, w_uq, kv_a_norm_g, w_ukv, mla_q_norm_g, mla_k_norm_g, dil_q_norm_g, dil_k_norm_g, w_out_a, w_out_b, w_out_c, w_o, loss_target, m_norm_g, m_w_in, m_b_gate, m_conv_w, m_conv_b, m_q_a_norm_g, m_w_uq, m_kv_a_norm_g, m_w_ukv, m_mla_q_norm_g, m_mla_k_norm_g, m_dil_q_norm_g, m_dil_k_norm_g, m_w_out_a, m_w_out_b, m_w_out_c, m_w_o, v_norm_g, v_w_in, v_b_gate, v_conv_w, v_conv_b, v_q_a_norm_g, v_w_uq, v_kv_a_norm_g, v_w_ukv, v_mla_q_norm_g, v_mla_k_norm_g, v_dil_q_norm_g, v_dil_k_norm_g, v_w_out_a, v_w_out_b, v_w_out_c, v_w_o):
    names = ("norm_g", "w_in", "b_gate", "conv_w", "conv_b", "q_a_norm_g", "w_uq", "kv_a_norm_g", "w_ukv",
             "mla_q_norm_g", "mla_k_norm_g", "dil_q_norm_g", "dil_k_norm_g", "w_out_a", "w_out_b", "w_out_c", "w_o")
    w = dict(zip(names, (norm_g, w_in, b_gate, conv_w, conv_b, q_a_norm_g, w_uq, kv_a_norm_g, w_ukv, mla_q_norm_g,
                         mla_k_norm_g, dil_q_norm_g, dil_k_norm_g, w_out_a, w_out_b, w_out_c, w_o)))
    m = dict(zip(names, (m_norm_g, m_w_in, m_b_gate, m_conv_w, m_conv_b, m_q_a_norm_g, m_w_uq, m_kv_a_norm_g, m_w_ukv,
                         m_mla_q_norm_g, m_mla_k_norm_g, m_dil_q_norm_g, m_dil_k_norm_g, m_w_out_a, m_w_out_b,
                         m_w_out_c, m_w_o)))
    v = dict(zip(names, (v_norm_g, v_w_in, v_b_gate, v_conv_w, v_conv_b, v_q_a_norm_g, v_w_uq, v_kv_a_norm_g, v_w_ukv,
                         v_mla_q_norm_g, v_mla_k_norm_g, v_dil_q_norm_g, v_dil_k_norm_g, v_w_out_a, v_w_out_b,
                         v_w_out_c, v_w_o)))
    B, S, _ = x.shape
    me = _my_index()
    cshard = CONV_WIDTH // N_DEV

    shards = [[w[n][l].astype(BF16) for n in BIG] for l in range(DEPTH)]
    state = {}

    def weights_of(l, after):
        if l == 0:
            got = _gather_two_level(shards[0] + [conv_w], "all_gather_weights_0")
            state["gather"], token = _exchange_start(shards[1], "all_gather_weights_1_start", gather=True)
            state["conv_w"] = got[-1]
        else:
            landed = _exchange_wait(state["gather"], after, "all_gather_weights_1_wait", gather=True)
            got, token = [_own_slot(a, s[None]) for a, s in zip(landed, shards[1])], None
        gw = dict(zip(BIG, got))
        gw["conv_w"] = state["conv_w"][:, l]
        return gw, token

    recv, small_parts = {}, {}

    def grads_out(l, g, after):
        small_parts[l] = [g[n] for n in SMALL]
        send = [g[n].astype(BF16) for n in BIG]
        if l == DEPTH - 1:
            state["scatter"], token = _exchange_start(send, "exchange_weight_grads_1_start", gather=False)
            state["sent"] = send
            return token
        landed = _exchange_wait(state["scatter"], after, "exchange_weight_grads_1_wait", gather=False)
        mine = [lax.dynamic_slice_in_dim(s, me, 1, axis=0) for s in state["sent"]]
        recv[DEPTH - 1] = [_own_slot(a, s) for a, s in zip(landed, mine)]
        swapped = _sibling_swap(send, "exchange_weight_grads_0_sibling")
        sums = [_chip_pair_sum(s, t, "chip_pair_sum_" + n) for n, s, t in zip(BIG, send, swapped)]
        recv[l] = _chip_exchange(sums, "exchange_weight_grads_0")
        return None

    sq, grad_x = _local_step(x.reshape(B * S, D_MODEL), loss_target.reshape(B * S, D_MODEL), w, B, S,
                             weights_of, grads_out)
    loss = lax.psum(sq * (0.5 / D_MODEL), AXES)

    res = {}
    for i, n in enumerate(BIG):
        rows = lambda a: a.reshape(-1, a.shape[-1])
        outs = _reduce_adamw([recv[l][i] for l in range(DEPTH)], rows(w[n]), rows(m[n]), rows(v[n]),
                             "reduce_adamw_" + n)
        res[n] = tuple(a.reshape(w[n].shape) for a in outs)
    part = {n: jnp.stack([small_parts[l][i] for l in range(DEPTH)]) for i, n in enumerate(SMALL)}

    def widen(t):
        return lax.dynamic_update_slice(jnp.zeros((DEPTH, CONV_K, CONV_WIDTH), F32), t, (0, 0, me * cshard))

    small_like = [part[n] for n in SMALL]
    pick = lambda d: [widen(d[n]) if n == "conv_w" else d[n] for n in SMALL]
    parts, = _exchange([_pack_local(small_like)], "all_gather_small_grads", gather=True)
    gs, ds, ms, vs = _reduce_adamw([parts], _pack_local(pick(w)), _pack_local(pick(m)), _pack_local(pick(v)),
                                   "reduce_adamw_small")
    for n, t in zip(SMALL, zip(*(_unpack_local(a, small_like) for a in (gs, ds, ms, vs)))):
        if n == "conv_w":
            t = tuple(lax.dynamic_slice(a, (0, 0, me * cshard), (DEPTH, CONV_K, cshard)) for a in t)
        res[n] = t

    out = [loss, grad_x.reshape(B, S, D_MODEL)]
    for i in range(4):
        out += [res[n][i] for n in names]
    return tuple(out)
```

```python
import jax
import jax.numpy as jnp
from jax import lax
from jax.experimental import pallas as pl
from jax.experimental.pallas import tpu as pltpu

F32 = jnp.float32
BF16 = jnp.bfloat16

D_MODEL = 1024
DEPTH = 2
CONV_WIDTH = 512
CONV_K = 3
MLA_HEADS = 8
MLA_Q_LORA = 256
MLA_KV_LORA = 128
MLA_NOPE = 64
MLA_ROPE = 32
MLA_V = 64
MLA_QK = MLA_NOPE + MLA_ROPE
ROPE_THETA = 10000.0
DIL_PATTERNS = ((128, 1), (512, 4), (2048, 16))
DIL_GROUPS = 3
DIL_HEADS = 8
DIL_HEAD_DIM = 64
DIL_WIDTH = DIL_HEADS * DIL_HEAD_DIM
DIL_QK = DIL_GROUPS * DIL_WIDTH
EPS = 1e-6
N_IN = 11168

ADAM_LR = 0.001
ADAM_B1 = 0.9
ADAM_B2 = 0.999
ADAM_EPS = 1e-08
ADAM_WD = 0.01
ADAM_STEP = 10

N_DEV = 8
AXES = ("x", "y", "c")
LANE = 128
HALF = 64
NPAIR = 4

CB_AB, CB_AC, CB_AX, CB_AZ = 0, 4, 8, 12
CB_CQ, CB_CKV, CB_KPE = 16, 18, 19
CB_BZ = 20
CB_DQ, CB_DK, CB_DV = 24, 36, 48
CB_CZ, CB_GATE = 60, 64
NCB = 88
PP = NCB * LANE
KPE_END = CB_KPE * LANE + MLA_ROPE
SHARD_COLS = N_IN // N_DEV
NEG = -1e30
VMEM_LIMIT = 56 * 1024 * 1024


def _pad_columns(shards):
    w = jnp.concatenate([shards[p] for p in range(N_DEV)], axis=1)
    return jnp.concatenate([w[:, :KPE_END], jnp.zeros((w.shape[0], LANE - MLA_ROPE), w.dtype), w[:, KPE_END:]], axis=1)


def _unpad_columns(wp):
    w = jnp.concatenate([wp[:, :KPE_END], wp[:, KPE_END + LANE - MLA_ROPE:]], axis=1)
    return jnp.stack([w[:, p * SHARD_COLS:(p + 1) * SHARD_COLS] for p in range(N_DEV)])


def _put_copies(stages, dst_ref, sems, slot, rows, cols):
    return [pltpu.make_async_copy(st.at[slot], dst_ref.at[rows, pl.ds(c0, st.shape[-1])], sems.at[slot, k])
            for k, (st, c0) in enumerate(zip(stages, cols))]


def _put_pipeline(step, nsteps, copies_of, fill):
    @pl.when(step >= 2)
    def _():
        for cp in copies_of(step - 2):
            cp.wait()

    fill(step % 2)
    for cp in copies_of(step):
        cp.start()

    @pl.when(step == nsteps - 1)
    def _():
        if nsteps >= 2:
            for cp in copies_of(step - 1):
                cp.wait()
        for cp in copies_of(step):
            cp.wait()


def _cp():
    return pltpu.CompilerParams(vmem_limit_bytes=VMEM_LIMIT)


def _rstd(x, n):
    return lax.rsqrt(jnp.sum(x * x, axis=-1, keepdims=True) * (1.0 / n) + EPS)


def _sigmoid(z):
    return 1.0 / (1.0 + jnp.exp(-z))


def _silu(z):
    return z * _sigmoid(z)


def _dsilu(z):
    s = _sigmoid(z)
    return s * (1.0 + z * (1.0 - s))


def _mm(a, b):
    return jnp.dot(a.astype(BF16), b.astype(BF16), preferred_element_type=F32)


def _mm_nt(a, b):
    return lax.dot_general(a.astype(BF16), b.astype(BF16), (((1,), (1,)), ((), ())), preferred_element_type=F32)


def _mm_tn(a, b):
    return lax.dot_general(a.astype(BF16), b.astype(BF16), (((0,), (0,)), ((), ())), preferred_element_type=F32)


def _lane_lo(shape):
    return lax.broadcasted_iota(jnp.int32, shape, len(shape) - 1) < HALF


def _head_bcast_sum(x, terms=3):
    w = x.shape[-1]
    same = (lax.broadcasted_iota(jnp.int32, (w, w), 0) // HALF) == (lax.broadcasted_iota(jnp.int32, (w, w), 1) // HALF)
    ones = jnp.where(same, 1.0, 0.0).astype(jnp.bfloat16)
    total = None
    for _ in range(terms):
        term = x.astype(jnp.bfloat16)
        x = x - term.astype(F32)
        part = jnp.dot(term, ones, preferred_element_type=F32)
        total = part if total is None else total + part
    return total


def _rope(t, cos, sa, sb):
    return t * cos + pltpu.roll(t, LANE - 16, axis=1) * sa + pltpu.roll(t, 16, axis=1) * sb


def _rope_t(d, cos, sa, sb):
    return d * cos + pltpu.roll(d * sa, 16, axis=1) + pltpu.roll(d * sb, LANE - 16, axis=1)


def _shift_down(u, k):
    rows = lax.broadcasted_iota(jnp.int32, u.shape, 0)
    return jnp.where(rows >= k, pltpu.roll(u, k, axis=0), 0.0)


def _shift_up(u, k):
    n = u.shape[0]
    rows = lax.broadcasted_iota(jnp.int32, u.shape, 0)
    return jnp.where(rows < n - k, pltpu.roll(u, n - k, axis=0), 0.0)


def _tile(n, want):
    t = min(n, want)
    assert n % t == 0, (n, want)
    return t


def _inproj_fwd(x, g, wp):
    T = x.shape[0]
    tm, tn = _tile(T, 2048), 512

    def body(x_ref, g_ref, w_ref, proj_ref, ht_ref, h_ref):
        @pl.when(pl.program_id(1) == 0)
        def _():
            n = min(tm, 512)
            for r0 in range(0, tm, n):
                xv = x_ref[r0:r0 + n, :]
                h = xv * _rstd(xv, D_MODEL) * g_ref[...]
                h_ref[r0:r0 + n, :] = h.astype(BF16)
                ht_ref[:, r0:r0 + n] = h.T.astype(BF16)

        proj_ref[...] = jnp.dot(h_ref[...], w_ref[...], preferred_element_type=F32).astype(BF16)

    return pl.pallas_call(
        body, name="inproj_fwd", grid=(T // tm, PP // tn),
        in_specs=[pl.BlockSpec((tm, D_MODEL), lambda i, j: (i, 0)),
                  pl.BlockSpec((1, D_MODEL), lambda i, j: (0, 0)),
                  pl.BlockSpec((D_MODEL, tn), lambda i, j: (0, j))],
        out_specs=[pl.BlockSpec((tm, tn), lambda i, j: (i, j)),
                   pl.BlockSpec((D_MODEL, tm), lambda i, j: (0, i))],
        out_shape=[jax.ShapeDtypeStruct((T, PP), BF16), jax.ShapeDtypeStruct((D_MODEL, T), BF16)],
        scratch_shapes=[pltpu.VMEM((tm, D_MODEL), BF16)],
        compiler_params=_cp(),
    )(x, g, wp)


def _matmul_nn(at, b, name):
    K, T = at.shape
    N = b.shape[1]
    tt, tn = _tile(T, 1024), _tile(N, 2816)
    nk = T // tt

    def body(a_ref, b_ref, o_ref, acc_ref):
        k = pl.program_id(1)

        @pl.when(k == 0)
        def _():
            acc_ref[...] = jnp.zeros_like(acc_ref)

        acc_ref[...] += jnp.dot(a_ref[...], b_ref[...], preferred_element_type=F32)

        @pl.when(k == nk - 1)
        def _():
            o_ref[...] = acc_ref[...].astype(BF16)

    return pl.pallas_call(
        body, name=name, grid=(N // tn, nk),
        in_specs=[pl.BlockSpec((K, tt), lambda j, k: (0, k)),
                  pl.BlockSpec((tt, tn), lambda j, k: (k, j))],
        out_specs=pl.BlockSpec((K, tn), lambda j, k: (0, j)),
        out_shape=jax.ShapeDtypeStruct((K, N), BF16),
        scratch_shapes=[pltpu.VMEM((K, tn), F32)],
        compiler_params=_cp(),
    )(at, b)


def _matmul_tn(a, b, name):
    T, K = a.shape
    N = b.shape[1]
    tt, tn = _tile(T, 512), _tile(N, 1024)

    def body(a_ref, b_ref, o_ref):
        @pl.when(pl.program_id(1) == 0)
        def _():
            o_ref[...] = jnp.zeros_like(o_ref)

        o_ref[...] += _mm_tn(a_ref[...], b_ref[...])

    return pl.pallas_call(
        body, name=name, grid=(N // tn, T // tt),
        in_specs=[pl.BlockSpec((tt, K), lambda j, k: (k, 0)),
                  pl.BlockSpec((tt, tn), lambda j, k: (k, j))],
        out_specs=pl.BlockSpec((K, tn), lambda j, k: (0, j)),
        out_shape=jax.ShapeDtypeStruct((K, N), F32),
        compiler_params=_cp(),
    )(a, b)


def _inproj_bwd_x(dproj, wp, x, g, dout):
    T = x.shape[0]
    tm, tk = _tile(T, 1024), 1024
    nk = PP // tk

    def body(dp_ref, w_ref, x_ref, g_ref, do_ref, dx_ref, dg_ref, acc_ref):
        i, k = pl.program_id(0), pl.program_id(1)

        @pl.when(k == 0)
        def _():
            acc_ref[...] = jnp.zeros_like(acc_ref)

        @pl.when((k == 0) & (i == 0))
        def _():
            dg_ref[...] = jnp.zeros_like(dg_ref)

        acc_ref[...] += _mm_nt(dp_ref[...], w_ref[...])

        @pl.when(k == nk - 1)
        def _():
            dh = acc_ref[...]
            xv = x_ref[...]
            r = _rstd(xv, D_MODEL)
            gy = dh * g_ref[...]
            dot = jnp.sum(xv * gy, axis=-1, keepdims=True) * (1.0 / D_MODEL)
            dx_ref[...] = do_ref[...] + r * gy - xv * (r * r * r) * dot
            dg_ref[...] += jnp.sum(dh * xv * r, axis=0, keepdims=True)

    return pl.pallas_call(
        body, name="inproj_bwd_x", grid=(T // tm, nk),
        in_specs=[pl.BlockSpec((tm, tk), lambda i, k: (i, k)),
                  pl.BlockSpec((D_MODEL, tk), lambda i, k: (0, k)),
                  pl.BlockSpec((tm, D_MODEL), lambda i, k: (i, 0)),
                  pl.BlockSpec((1, D_MODEL), lambda i, k: (0, 0)),
                  pl.BlockSpec((tm, D_MODEL), lambda i, k: (i, 0))],
        out_specs=[pl.BlockSpec((tm, D_MODEL), lambda i, k: (i, 0)),
                   pl.BlockSpec((1, D_MODEL), lambda i, k: (0, 0))],
        out_shape=[jax.ShapeDtypeStruct((T, D_MODEL), F32), jax.ShapeDtypeStruct((1, D_MODEL), F32)],
        scratch_shapes=[pltpu.VMEM((tm, D_MODEL), F32)],
        compiler_params=_cp(),
    )(dproj, wp, x, g, dout)


A_SEGS = (CB_AB, CB_AC, CB_AX, CB_AZ)


def _mixa_fwd(proj, cw, cb, B, S):
    nc = CONV_WIDTH // LANE

    def body(ab_ref, ac_ref, ax_ref, az_ref, cw_ref, cb_ref, y_ref):
        ab, ac, ax, az = (r[...].astype(F32) for r in (ab_ref, ac_ref, ax_ref, az_ref))
        u = ac * ax
        conv = cb_ref[...] + cw_ref[0:1, :] * _shift_down(u, 2) + cw_ref[1:2, :] * _shift_down(u, 1) + cw_ref[2:3, :] * u
        y_ref[...] = (ab * conv * _silu(az)).astype(BF16)

    return pl.pallas_call(
        body, name="mixa_fwd", grid=(B, nc),
        in_specs=[pl.BlockSpec((S, LANE), lambda b, j, c0=c0: (b, c0 + j)) for c0 in A_SEGS]
                 + [pl.BlockSpec((CONV_K, LANE), lambda b, j: (0, j)),
                    pl.BlockSpec((1, LANE), lambda b, j: (0, j))],
        out_specs=pl.BlockSpec((S, LANE), lambda b, j: (b, j)),
        out_shape=jax.ShapeDtypeStruct((B * S, CONV_WIDTH), BF16),
        compiler_params=_cp(),
    )(proj, proj, proj, proj, cw, cb)


def _mixa_bwd(dproj, dy, proj, cw, cb, B, S):
    nc = CONV_WIDTH // LANE

    def body(dpin_ref, dy_ref, ab_ref, ac_ref, ax_ref, az_ref, cw_ref, cb_ref, dp_ref, st_ref, stage, sems):
        del dpin_ref
        j, b = pl.program_id(0), pl.program_id(1)
        ab, ac, ax, az = (r[...].astype(F32) for r in (ab_ref, ac_ref, ax_ref, az_ref))
        u = ac * ax
        u1, u2 = _shift_down(u, 1), _shift_down(u, 2)
        w0, w1, w2 = cw_ref[0:1, :], cw_ref[1:2, :], cw_ref[2:3, :]
        conv = cb_ref[...] + w0 * u2 + w1 * u1 + w2 * u
        s = _silu(az)
        d = dy_ref[...]
        dconv = d * ab * s
        du = w2 * dconv + w1 * _shift_up(dconv, 1) + w0 * _shift_up(dconv, 2)
        grads = (d * conv * s, du * ax, du * ac, d * ab * conv * _dsilu(az))

        def fill(slot):
            for k, v in enumerate(grads):
                stage[slot, k] = v.astype(BF16)

        def copies_of(step):
            sj, sb = step // B, step % B
            return _put_copies([stage.at[:, k] for k in range(4)], dp_ref, sems, step % 2,
                               pl.ds(pl.multiple_of(sb * S, S), S),
                               [pl.multiple_of((c0 + sj) * LANE, LANE) for c0 in A_SEGS])

        _put_pipeline(j * B + b, nc * B, copies_of, fill)
        row = lax.broadcasted_iota(jnp.int32, (8, LANE), 0)
        st = jnp.zeros((8, LANE), F32)
        for r, v in enumerate((dconv * u2, dconv * u1, dconv * u, dconv)):
            st = st + jnp.where(row == r, jnp.sum(v, axis=0, keepdims=True), 0.0)

        @pl.when(pl.program_id(1) == 0)
        def _():
            st_ref[...] = st

        @pl.when(pl.program_id(1) != 0)
        def _():
            st_ref[...] += st

    return pl.pallas_call(
        body, name="mixa_bwd", grid=(nc, B),
        in_specs=[pl.BlockSpec(memory_space=pl.ANY),
                  pl.BlockSpec((S, LANE), lambda j, b: (b, j))]
                 + [pl.BlockSpec((S, LANE), lambda j, b, c0=c0: (b, c0 + j)) for c0 in A_SEGS]
                 + [pl.BlockSpec((CONV_K, LANE), lambda j, b: (0, j)),
                    pl.BlockSpec((1, LANE), lambda j, b: (0, j))],
        out_specs=[pl.BlockSpec(memory_space=pl.ANY),
                   pl.BlockSpec((8, LANE), lambda j, b: (0, j))],
        out_shape=[jax.ShapeDtypeStruct(dproj.shape, BF16), jax.ShapeDtypeStruct((8, CONV_WIDTH), F32)],
        scratch_shapes=[pltpu.VMEM((2, 4, S, LANE), BF16), pltpu.SemaphoreType.DMA((2, 4))],
        input_output_aliases={0: 0},
        compiler_params=_cp(),
    )(dproj, dy, proj, proj, proj, proj, cw, cb)


def _mla_prep_fwd(proj, gq, gkv, wuqp, wkp, wv, gmq, gmk, cos, sa, sb, S):
    T = proj.shape[0]
    ts = _tile(S, 512)
    ns = S // ts
    W = MLA_HEADS * LANE

    def body(p_ref, gq_ref, gkv_ref, wuq_ref, wk_ref, wv_ref, gmq_ref, gmk_ref, cos_ref, sa_ref, sb_ref,
             q_ref, k_ref, v_ref):
        cq = p_ref[:, 0:2 * LANE].astype(F32)
        ckv = p_ref[:, 2 * LANE:3 * LANE].astype(F32)
        kpe = pltpu.roll(p_ref[:, 3 * LANE:4 * LANE].astype(F32), HALF, axis=1)
        cqn = cq * _rstd(cq, MLA_Q_LORA) * gq_ref[...]
        ckn = (ckv * _rstd(ckv, MLA_KV_LORA) * gkv_ref[...]).astype(BF16)
        q0 = _mm(cqn, wuq_ref[...])
        kn = _mm(ckn, wk_ref[...])
        v_ref[...] = _mm(ckn, wv_ref[...]).astype(BF16)
        c, a, b = cos_ref[...], sa_ref[...], sb_ref[...]
        for h in range(MLA_HEADS):
            q0h = q0[:, h * LANE:(h + 1) * LANE]
            q_ref[h] = _rope(q0h * _rstd(q0h, MLA_QK) * gmq_ref[...], c, a, b).astype(BF16)
            k0h = kn[:, h * LANE:(h + 1) * LANE] + kpe
            k_ref[h] = _rope(k0h * _rstd(k0h, MLA_QK) * gmk_ref[...], c, a, b).astype(BF16)

    def whole(r, c):
        return pl.BlockSpec((r, c), lambda i: (0, 0))

    tab = pl.BlockSpec((ts, LANE), lambda i: (i % ns, 0))
    return pl.pallas_call(
        body, name="mla_prep_fwd", grid=(T // ts,),
        in_specs=[pl.BlockSpec((ts, 4 * LANE), lambda i: (i, CB_CQ // 4)),
                  whole(1, MLA_Q_LORA), whole(1, MLA_KV_LORA), whole(MLA_Q_LORA, W), whole(MLA_KV_LORA, W),
                  whole(MLA_KV_LORA, MLA_HEADS * MLA_V), whole(1, LANE), whole(1, LANE), tab, tab, tab],
        out_specs=[pl.BlockSpec((MLA_HEADS, ts, LANE), lambda i: (0, i, 0)),
                   pl.BlockSpec((MLA_HEADS, ts, LANE), lambda i: (0, i, 0)),
                   pl.BlockSpec((ts, MLA_HEADS * MLA_V), lambda i: (i, 0))],
        out_shape=[jax.ShapeDtypeStruct((MLA_HEADS, T, LANE), BF16), jax.ShapeDtypeStruct((MLA_HEADS, T, LANE), BF16),
                   jax.ShapeDtypeStruct((T, MLA_HEADS * MLA_V), BF16)],
        compiler_params=_cp(),
    )(proj, gq, gkv, wuqp, wkp, wv, gmq, gmk, cos, sa, sb)


def _mla_prep_bwd(dproj, dq, dk, dv, proj, gq, gkv, wuqp, wkp, wv, gmq, gmk, cos, sa, sb, S):
    T = proj.shape[0]
    ts = _tile(S, 256)
    ns = S // ts
    W = MLA_HEADS * LANE

    def body(dpin_ref, dq_ref, dk_ref, dv_ref, p_ref, gq_ref, gkv_ref, wuq_ref, wk_ref, wv_ref, gmq_ref, gmk_ref,
             cos_ref, sa_ref, sb_ref,
             dp_ref, dwuq_ref, dwk_ref, dwv_ref, dgq_ref, dgkv_ref, dgmq_ref, dgmk_ref, dq0_ref, dkn_ref):
        del dpin_ref

        @pl.when(pl.program_id(0) == 0)
        def _():
            for r in (dwuq_ref, dwk_ref, dwv_ref, dgq_ref, dgkv_ref, dgmq_ref, dgmk_ref):
                r[...] = jnp.zeros_like(r)

        cq = p_ref[:, 0:2 * LANE].astype(F32)
        ckv = p_ref[:, 2 * LANE:3 * LANE].astype(F32)
        kpe = pltpu.roll(p_ref[:, 3 * LANE:4 * LANE].astype(F32), HALF, axis=1)
        rq = _rstd(cq, MLA_Q_LORA)
        rkv = _rstd(ckv, MLA_KV_LORA)
        gq, gkv, gmq, gmk = gq_ref[...], gkv_ref[...], gmq_ref[...], gmk_ref[...]
        cqn = (cq * rq * gq).astype(BF16)
        ckn = (ckv * rkv * gkv).astype(BF16)
        q0 = _mm(cqn, wuq_ref[...])
        kn = _mm(ckn, wk_ref[...])
        c, a, b = cos_ref[...], sa_ref[...], sb_ref[...]
        lane = lax.broadcasted_iota(jnp.int32, (ts, LANE), 1)
        dgmq = jnp.zeros((1, LANE), F32)
        dgmk = jnp.zeros((1, LANE), F32)
        dkpe = jnp.zeros((ts, LANE), F32)
        for h in range(MLA_HEADS):
            q0h = q0[:, h * LANE:(h + 1) * LANE]
            r = _rstd(q0h, MLA_QK)
            d1 = _rope_t(dq_ref[h], c, a, b)
            gy = d1 * gmq
            dq0_ref[:, h * LANE:(h + 1) * LANE] = (
                r * gy - q0h * (r * r * r) * (jnp.sum(q0h * gy, axis=-1, keepdims=True) * (1.0 / MLA_QK))).astype(BF16)
            dgmq = dgmq + jnp.sum(d1 * q0h * r, axis=0, keepdims=True)
            k0h = kn[:, h * LANE:(h + 1) * LANE] + kpe
            r = _rstd(k0h, MLA_QK)
            d1 = _rope_t(dk_ref[h], c, a, b)
            gy = d1 * gmk
            dk0 = r * gy - k0h * (r * r * r) * (jnp.sum(k0h * gy, axis=-1, keepdims=True) * (1.0 / MLA_QK))
            dgmk = dgmk + jnp.sum(d1 * k0h * r, axis=0, keepdims=True)
            dkn_ref[:, h * LANE:(h + 1) * LANE] = jnp.where(lane < MLA_NOPE, dk0, 0.0).astype(BF16)
            dkpe = dkpe + jnp.where((lane >= MLA_NOPE) & (lane < MLA_QK), dk0, 0.0)
        dq0 = dq0_ref[...]
        dkn = dkn_ref[...]
        dvv = dv_ref[...]
        dwuq_ref[...] += _mm_tn(cqn, dq0)
        dwk_ref[...] += _mm_tn(ckn, dkn)
        dwv_ref[...] += _mm_tn(ckn, dvv)
        dgmq_ref[...] += dgmq
        dgmk_ref[...] += dgmk
        dcqn = _mm_nt(dq0, wuq_ref[...])
        gy = dcqn * gq
        dp_ref[:, 0:2 * LANE] = (
            rq * gy - cq * (rq * rq * rq) * (jnp.sum(cq * gy, axis=-1, keepdims=True) * (1.0 / MLA_Q_LORA))).astype(BF16)
        dgq_ref[...] += jnp.sum(dcqn * cq * rq, axis=0, keepdims=True)
        dckn = _mm_nt(dkn, wk_ref[...]) + _mm_nt(dvv, wv_ref[...])
        gy = dckn * gkv
        dp_ref[:, 2 * LANE:3 * LANE] = (
            rkv * gy - ckv * (rkv * rkv * rkv) * (jnp.sum(ckv * gy, axis=-1, keepdims=True) * (1.0 / MLA_KV_LORA))).astype(BF16)
        dgkv_ref[...] += jnp.sum(dckn * ckv * rkv, axis=0, keepdims=True)
        dp_ref[:, 3 * LANE:4 * LANE] = pltpu.roll(dkpe, HALF, axis=1).astype(BF16)

    def whole(r, c):
        return pl.BlockSpec((r, c), lambda i: (0, 0))

    tab = pl.BlockSpec((ts, LANE), lambda i: (i % ns, 0))
    heads = pl.BlockSpec((MLA_HEADS, ts, LANE), lambda i: (0, i, 0))
    return pl.pallas_call(
        body, name="mla_prep_bwd", grid=(T // ts,),
        in_specs=[pl.BlockSpec(memory_space=pl.ANY), heads, heads,
                  pl.BlockSpec((ts, MLA_HEADS * MLA_V), lambda i: (i, 0)),
                  pl.BlockSpec((ts, 4 * LANE), lambda i: (i, CB_CQ // 4)),
                  whole(1, MLA_Q_LORA), whole(1, MLA_KV_LORA), whole(MLA_Q_LORA, W), whole(MLA_KV_LORA, W),
                  whole(MLA_KV_LORA, MLA_HEADS * MLA_V), whole(1, LANE), whole(1, LANE), tab, tab, tab],
        out_specs=[pl.BlockSpec((ts, 4 * LANE), lambda i: (i, CB_CQ // 4)),
                   whole(MLA_Q_LORA, W), whole(MLA_KV_LORA, W), whole(MLA_KV_LORA, MLA_HEADS * MLA_V),
                   whole(1, MLA_Q_LORA), whole(1, MLA_KV_LORA), whole(1, LANE), whole(1, LANE)],
        out_shape=[jax.ShapeDtypeStruct(dproj.shape, BF16),
                   jax.ShapeDtypeStruct((MLA_Q_LORA, W), F32), jax.ShapeDtypeStruct((MLA_KV_LORA, W), F32),
                   jax.ShapeDtypeStruct((MLA_KV_LORA, MLA_HEADS * MLA_V), F32),
                   jax.ShapeDtypeStruct((1, MLA_Q_LORA), F32), jax.ShapeDtypeStruct((1, MLA_KV_LORA), F32),
                   jax.ShapeDtypeStruct((1, LANE), F32), jax.ShapeDtypeStruct((1, LANE), F32)],
        scratch_shapes=[pltpu.VMEM((ts, W), BF16), pltpu.VMEM((ts, W), BF16)],
        input_output_aliases={0: 0},
        compiler_params=_cp(),
    )(dproj, dq, dk, dv, proj, gq, gkv, wuqp, wkp, wv, gmq, gmk, cos, sa, sb)


def _dil_prep_fwd(proj, gq, gk):
    T = proj.shape[0]
    ts = _tile(T, 512)

    def body(pq_ref, pk_ref, pv_ref, gq_ref, gk_ref, q_ref, k_ref, v_ref):
        v_ref[...] = pv_ref[...].astype(F32)
        for c in range(NPAIR):
            cs = slice(c * LANE, (c + 1) * LANE)
            t = jnp.concatenate([pq_ref[:, cs], pk_ref[:, cs]], axis=1).astype(F32)
            y = t * lax.rsqrt(_head_bcast_sum(t * t, terms=2) * (1.0 / DIL_HEAD_DIM) + EPS)
            q_ref[:, cs] = y[:, 0:LANE] * gq_ref[:, cs]
            k_ref[:, cs] = y[:, LANE:2 * LANE] * gk_ref[:, cs]

    col = pl.BlockSpec((1, DIL_WIDTH), lambda i, g: (0, g))
    out = pl.BlockSpec((ts, DIL_WIDTH), lambda i, g: (i, g))
    seg = lambda c0: pl.BlockSpec((ts, DIL_WIDTH), lambda i, g: (i, c0 // NPAIR + g))
    return pl.pallas_call(
        body, name="dil_prep_fwd", grid=(T // ts, DIL_GROUPS),
        in_specs=[seg(CB_DQ), seg(CB_DK), seg(CB_DV), col, col],
        out_specs=[out, out, out],
        out_shape=[jax.ShapeDtypeStruct((T, DIL_QK), F32)] * 3,
        compiler_params=_cp(),
    )(proj, proj, proj, gq, gk)


def _dil_prep_bwd(dproj, ddq, ddk, ddv, proj, gq, gk):
    T = proj.shape[0]
    ts = _tile(T, 512)
    nt = T // ts

    def body(dpin_ref, ddq_ref, ddk_ref, ddv_ref, pq_ref, pk_ref, gq_ref, gk_ref, dp_ref, dgq_ref, dgk_ref,
             stage, sems):
        del dpin_ref
        g, i = pl.program_id(0), pl.program_id(1)

        @pl.when(i == 0)
        def _():
            dgq_ref[...] = jnp.zeros_like(dgq_ref)
            dgk_ref[...] = jnp.zeros_like(dgk_ref)

        def fill(slot):
            stage[slot, 2] = ddv_ref[...].astype(BF16)
            for c in range(NPAIR):
                cs = slice(c * LANE, (c + 1) * LANE)
                t = jnp.concatenate([pq_ref[:, cs], pk_ref[:, cs]], axis=1).astype(F32)
                d = jnp.concatenate([ddq_ref[:, cs], ddk_ref[:, cs]], axis=1)
                gy = d * jnp.concatenate([gq_ref[:, cs], gk_ref[:, cs]], axis=1)
                r = lax.rsqrt(_head_bcast_sum(t * t, terms=2) * (1.0 / DIL_HEAD_DIM) + EPS)
                dot = _head_bcast_sum(t * gy, terms=2) * (1.0 / DIL_HEAD_DIM)
                dx = (r * gy - t * (r * r * r) * dot).astype(BF16)
                stage[slot, 0, :, cs] = dx[:, 0:LANE]
                stage[slot, 1, :, cs] = dx[:, LANE:2 * LANE]
                part = jnp.sum(d * t * r, axis=0, keepdims=True)
                dgq_ref[:, cs] += part[:, 0:LANE]
                dgk_ref[:, cs] += part[:, LANE:2 * LANE]

        def copies_of(step):
            sg, si = step // nt, step % nt
            return _put_copies([stage.at[:, k] for k in range(3)], dp_ref, sems, step % 2,
                               pl.ds(pl.multiple_of(si * ts, ts), ts),
                               [pl.multiple_of((c0 + NPAIR * sg) * LANE, LANE) for c0 in (CB_DQ, CB_DK, CB_DV)])

        _put_pipeline(g * nt + i, DIL_GROUPS * nt, copies_of, fill)

    col = pl.BlockSpec((1, DIL_WIDTH), lambda g, i: (0, g))
    tok = pl.BlockSpec((ts, DIL_WIDTH), lambda g, i: (i, g))
    seg = lambda c0: pl.BlockSpec((ts, DIL_WIDTH), lambda g, i: (i, c0 // NPAIR + g))
    return pl.pallas_call(
        body, name="dil_prep_bwd", grid=(DIL_GROUPS, nt),
        in_specs=[pl.BlockSpec(memory_space=pl.ANY), tok, tok, tok, seg(CB_DQ), seg(CB_DK), col, col],
        out_specs=[pl.BlockSpec(memory_space=pl.ANY), col, col],
        out_shape=[jax.ShapeDtypeStruct(dproj.shape, BF16), jax.ShapeDtypeStruct((1, DIL_QK), F32),
                   jax.ShapeDtypeStruct((1, DIL_QK), F32)],
        scratch_shapes=[pltpu.VMEM((2, 3, ts, DIL_WIDTH), BF16), pltpu.SemaphoreType.DMA((2, 3))],
        input_output_aliases={0: 0},
        compiler_params=_cp(),
    )(dproj, ddq, ddk, ddv, proj, proj, gq, gk)


COPY_ROWS = 256


def _to_classes(src_ref, dst_ref, d, L, scale=None):
    n = min(L, COPY_ROWS)
    for r in range(d):
        for c0 in range(0, L, n):
            rows = pl.ds(r + c0 * d, n, stride=d) if d > 1 else pl.ds(c0, n)
            val = src_ref[rows, :]
            if scale is not None:
                val = val * scale
            dst_ref[r * L + c0:r * L + c0 + n, :] = val.astype(dst_ref.dtype)


def _from_classes(src_ref, dst_ref, d, L):
    n = min(L, COPY_ROWS)
    for r in range(d):
        for c0 in range(0, L, n):
            rows = pl.ds(r + c0 * d, n, stride=d) if d > 1 else pl.ds(c0, n)
            dst_ref[rows, :] = src_ref[r * L + c0:r * L + c0 + n, :].astype(dst_ref.dtype)


MLA_TQ, MLA_TK = 512, 512


def _causal_bias(tq, tk, shift):
    row = lax.broadcasted_iota(jnp.int32, (tq, tk), 0)
    col = lax.broadcasted_iota(jnp.int32, (tq, tk), 1)
    return jnp.where(row >= col + shift, 0.0, NEG)


def _mla_specs(S):
    heads = pl.BlockSpec((2, S, LANE), lambda b, j: (j, b, 0))
    pair = pl.BlockSpec((S, LANE), lambda b, j: (b, j))
    return heads, pair


def _mla_attn_fwd(q, k, v, B, S):
    tq = _tile(S, MLA_TQ)
    tk = _tile(tq, MLA_TK)
    nd = tq // tk
    scale = MLA_QK ** -0.5
    heads, pair = _mla_specs(S)

    def body(q_ref, k_ref, v_ref, o_ref, lse_ref):
        lo, lok = _lane_lo((tq, LANE)), _lane_lo((tk, LANE))
        diag = [_causal_bias(tq, tk, i * tk) for i in range(nd)]

        def block(g, _):
            row0 = pl.multiple_of(g * tq, tq)
            rows = pl.ds(row0, tq)
            qs = [q_ref[hh, rows, :] for hh in range(2)]

            one = jnp.ones((), BF16)

            def step(off, carries, bias):
                off = pl.multiple_of(off, tk)
                vt = v_ref[pl.ds(off, tk), :]
                vh = (jnp.where(lok, vt, one), jnp.where(lok, one, vt))
                out = []
                for hh, (m, acc) in enumerate(carries):
                    s = _mm_nt(qs[hh], k_ref[hh, pl.ds(off, tk), :]) * scale
                    if bias is not None:
                        s = s + bias
                    m_new = jnp.maximum(m, jnp.max(s, axis=-1, keepdims=True))
                    p = jnp.exp(s - m_new)
                    out.append((m_new, jnp.exp(m - m_new) * acc + _mm(p, vh[hh])))
                return tuple(out)

            init = (jnp.full((tq, 1), NEG, F32), jnp.zeros((tq, LANE), F32))
            carries = lax.fori_loop(0, g * nd, lambda i, c: step(i * tk, c, None), (init, init))
            for i in range(nd):
                carries = step(row0 + i * tk, carries, diag[i])
            (ma, acca), (mb, accb) = carries
            la, lb = pltpu.roll(acca, HALF, axis=1), pltpu.roll(accb, HALF, axis=1)
            o_ref[rows, :] = jnp.where(lo, acca / la, accb / lb)
            lse_ref[rows, :] = jnp.where(lo, ma + jnp.log(la), mb + jnp.log(lb))
            return 0

        lax.fori_loop(0, S // tq, block, 0)

    return pl.pallas_call(
        body, name="mla_attn_fwd", grid=(B, NPAIR), in_specs=[heads, heads, pair], out_specs=[pair, pair],
        out_shape=[jax.ShapeDtypeStruct((B * S, MLA_HEADS * MLA_V), F32)] * 2,
        compiler_params=_cp(),
    )(q, k, v)


DIL_UNROLL = 8


def _dil_geometry(gi, S):
    span, d = DIL_PATTERNS[gi]
    L = S // d
    t = _tile(L, 128)
    window = span // d
    back = min(-(-window // t) * t, L - t)
    return d, L, t, window, back


def _dil_specs(gi, S):
    qk = pl.BlockSpec((S, LANE), lambda b, j: (b, NPAIR * gi + j))
    pair = pl.BlockSpec((S, LANE), lambda b, j: (b, j))
    return qk, qk, pair


def _dil_bias(bias_ref, sl_ref, j, t, kw, back, window):
    row = lax.broadcasted_iota(jnp.int32, (2 * t, kw), 0)
    col = lax.broadcasted_iota(jnp.int32, (2 * t, kw), 1)
    second = row >= t
    slope = jnp.where(second, sl_ref[j, 1], sl_ref[j, 0])
    for n in range(bias_ref.shape[0]):
        dist = jnp.where(second, row - t, row) + n * back - col
        bias_ref[n] = jnp.where((dist >= 0) & (dist <= window), -slope * dist.astype(F32), NEG)


def _stack_heads(x, lo):
    zero = jnp.zeros((), x.dtype)
    return jnp.concatenate([jnp.where(lo, x, zero), jnp.where(lo, zero, x)], axis=0)


def _dil_attn_fwd(gi, slopes, qn, kn, proj, B, S):
    d, L, t, window, back = _dil_geometry(gi, S)
    kw, nq = back + t, L // t
    nbias = 2 if back else 1
    qk, vspec, pair = _dil_specs(gi, S)

    def body(sl_ref, q_ref, k_ref, v_ref, o_ref, lse_ref, qs, ks, vs, os_, ls, bias_ref):
        _to_classes(q_ref, qs, d, L, DIL_HEAD_DIM ** -0.5)
        _to_classes(k_ref, ks, d, L)
        _to_classes(v_ref, vs, d, L)
        _dil_bias(bias_ref, sl_ref, pl.program_id(1), t, kw, back, window)
        lo = _lane_lo((t, LANE))

        def block(g, _):
            qb = g % nq if d > 1 else g
            row0 = pl.multiple_of(g * t, t)
            rows = pl.ds(row0, t)
            early = qb * t < back
            keys = pl.ds(pl.multiple_of(jnp.where(early, row0 - qb * t, row0 - back), t), kw)
            s = _mm_nt(_stack_heads(qs[rows, :], lo), ks[keys, :]) + bias_ref[jnp.where(early, 0, nbias - 1)]
            m = jnp.max(s, axis=-1, keepdims=True)
            p = jnp.exp(s - m)
            l = jnp.sum(p, axis=-1, keepdims=True)
            o2 = _mm(p, vs[keys, :]) / l
            lse2 = m + jnp.log(l)
            os_[rows, :] = jnp.where(lo, o2[:t], o2[t:])
            ls[rows, :] = jnp.where(lo, lse2[:t], lse2[t:])
            return 0

        lax.fori_loop(0, d * nq, block, 0, unroll=DIL_UNROLL if d * nq % DIL_UNROLL == 0 else 1)
        _from_classes(os_, o_ref, d, L)
        _from_classes(ls, lse_ref, d, L)

    return pl.pallas_call(
        body, name=f"dil_attn_fwd_{gi}", grid=(B, NPAIR),
        in_specs=[pl.BlockSpec(memory_space=pltpu.SMEM), qk, qk, vspec], out_specs=[pair, pair],
        out_shape=[jax.ShapeDtypeStruct((B * S, DIL_WIDTH), F32)] * 2,
        scratch_shapes=[pltpu.VMEM((S, LANE), BF16)] * 3 + [pltpu.VMEM((S, LANE), F32)] * 2
                       + [pltpu.VMEM((nbias, 2 * t, kw), F32)],
        compiler_params=_cp(),
    )(slopes, qn, kn, proj)


def _mla_attn_bwd(q, k, v, do, lse, delta, B, S):
    T = B * S
    tq = _tile(S, MLA_TQ)
    tk = _tile(tq, MLA_TK)
    nd = tq // tk
    scale = MLA_QK ** -0.5
    heads, pair = _mla_specs(S)

    def body(q_ref, k_ref, v_ref, do_ref, lse_ref, dl_ref, dq_ref, dk_ref, dv_ref):
        dk_ref[...] = jnp.zeros_like(dk_ref)
        dv_ref[...] = jnp.zeros_like(dv_ref)
        lo = _lane_lo((tq, LANE))
        diag = [_causal_bias(tq, tk, i * tk) for i in range(nd)]

        def block(g, _):
            row0 = pl.multiple_of(g * tq, tq)
            rows = pl.ds(row0, tq)
            for hh in range(2):
                sel = lo if hh == 0 else jnp.logical_not(lo)
                qh = q_ref[hh, rows, :]
                doh = jnp.where(sel, do_ref[rows, :], jnp.zeros((), BF16))
                lse_h = jnp.max(jnp.where(sel, lse_ref[rows, :], NEG), axis=-1, keepdims=True)
                dl_h = jnp.max(jnp.where(sel, dl_ref[rows, :], NEG), axis=-1, keepdims=True)

                def step(off, dq_acc, bias, hh=hh, qh=qh, doh=doh, lse_h=lse_h, dl_h=dl_h):
                    cols = pl.ds(pl.multiple_of(off, tk), tk)
                    kh = k_ref[hh, cols, :]
                    s = _mm_nt(qh, kh) * scale
                    if bias is not None:
                        s = s + bias
                    p = jnp.exp(s - lse_h)
                    dp = _mm_nt(doh, v_ref[cols, :])
                    ds = (p * (dp - dl_h)).astype(BF16)
                    dk_ref[hh, cols, :] += _mm_tn(ds, qh) * scale
                    dv_ref[cols, :] += _mm_tn(p, doh)
                    return dq_acc + _mm(ds, kh)

                dq_acc = lax.fori_loop(0, g * nd, lambda i, a: step(i * tk, a, None), jnp.zeros((tq, LANE), F32))
                for i in range(nd):
                    dq_acc = step(row0 + i * tk, dq_acc, diag[i])
                dq_ref[hh, rows, :] = dq_acc * scale
            return 0

        lax.fori_loop(0, S // tq, block, 0)

    return pl.pallas_call(
        body, name="mla_attn_bwd", grid=(B, NPAIR), in_specs=[heads, heads, pair, pair, pair, pair],
        out_specs=[heads, heads, pair],
        out_shape=[jax.ShapeDtypeStruct((MLA_HEADS, T, LANE), F32), jax.ShapeDtypeStruct((MLA_HEADS, T, LANE), F32),
                   jax.ShapeDtypeStruct((T, MLA_HEADS * MLA_V), F32)],
        compiler_params=_cp(),
    )(q, k, v, do, lse, delta)


def _dil_attn_bwd(gi, slopes, qn, kn, proj, do, lse, delta, through, B, S):
    d, L, t, window, back = _dil_geometry(gi, S)
    kw, nq = back + t, L // t
    nbias = 2 if back else 1
    scale = DIL_HEAD_DIM ** -0.5
    qk, vspec, pair = _dil_specs(gi, S)

    def body(*refs):
        refs = list(refs)
        sl_ref, q_ref, k_ref, v_ref, do_ref, lse_ref, dl_ref = refs[:7]
        dq_ref, dk_ref, dv_ref, qs, ks, vs, dos, lss, dls, dqs, dks, dvs, bias_ref = refs[-13:]
        _to_classes(q_ref, qs, d, L, scale)
        for src, dst in ((k_ref, ks), (v_ref, vs), (do_ref, dos), (lse_ref, lss), (dl_ref, dls)):
            _to_classes(src, dst, d, L)
        _dil_bias(bias_ref, sl_ref, pl.program_id(1), t, kw, back, window)
        dks[...] = jnp.zeros_like(dks)
        dvs[...] = jnp.zeros_like(dvs)
        lo = _lane_lo((t, LANE))

        def stats(ref, rows):
            x = ref[rows, :]
            return jnp.concatenate([jnp.max(jnp.where(lo, x, NEG), axis=-1, keepdims=True),
                                    jnp.max(jnp.where(lo, NEG, x), axis=-1, keepdims=True)], axis=0)

        def block(g, _):
            qb = g % nq if d > 1 else g
            row0 = pl.multiple_of(g * t, t)
            rows = pl.ds(row0, t)
            early = qb * t < back
            keys = pl.ds(pl.multiple_of(jnp.where(early, row0 - qb * t, row0 - back), t), kw)
            q2 = _stack_heads(qs[rows, :], lo)
            do2 = _stack_heads(dos[rows, :], lo)
            kt = ks[keys, :]
            s = _mm_nt(q2, kt) + bias_ref[jnp.where(early, 0, nbias - 1)]
            p = jnp.exp(s - stats(lss, rows))
            ds = (p * (_mm_nt(do2, vs[keys, :]) - stats(dls, rows))).astype(BF16)
            dq2 = _mm(ds, kt) * scale
            dqs[rows, :] = jnp.where(lo, dq2[:t], dq2[t:])
            dks[keys, :] += _mm_tn(ds, q2)
            dvs[keys, :] += _mm_tn(p, do2)
            return 0

        lax.fori_loop(0, d * nq, block, 0, unroll=DIL_UNROLL if d * nq % DIL_UNROLL == 0 else 1)
        for src, dst in ((dqs, dq_ref), (dks, dk_ref), (dvs, dv_ref)):
            _from_classes(src, dst, d, L)

    in_specs = [pl.BlockSpec(memory_space=pltpu.SMEM), qk, qk, vspec, pair, pair, pair]
    args = [slopes, qn, kn, proj, do, lse, delta]
    aliases = {}
    if through is not None:
        aliases = {len(args) + i: i for i in range(3)}
        in_specs = in_specs + [pl.BlockSpec(memory_space=pl.ANY)] * 3
        args = args + list(through)
    return pl.pallas_call(
        body, name=f"dil_attn_bwd_{gi}", grid=(B, NPAIR), in_specs=in_specs, out_specs=[qk, qk, qk],
        out_shape=[jax.ShapeDtypeStruct((B * S, DIL_QK), F32)] * 3,
        scratch_shapes=[pltpu.VMEM((S, LANE), BF16)] * 4 + [pltpu.VMEM((S, LANE), F32)] * 5
                       + [pltpu.VMEM((nbias, 2 * t, kw), F32)],
        input_output_aliases=aliases,
        compiler_params=_cp(),
    )(*args)


def _merge_proj_specs(ts):
    wide = lambda c0, w: pl.BlockSpec((ts, w), lambda i: (i, c0 * LANE // w))
    return [wide(CB_BZ, DIL_WIDTH), wide(CB_CZ, DIL_WIDTH)] + [wide(CB_GATE + 8 * i, D_MODEL) for i in range(3)]


def _merge_common(p_refs, bg_ref, ob_ref, og_refs, lse_refs):
    bz = p_refs[0][...].astype(F32)
    cz = p_refs[1][...].astype(F32)
    gates = [_sigmoid(p_refs[2 + i][...].astype(F32) + bg_ref[:, i * D_MODEL:(i + 1) * D_MODEL]) for i in range(3)]
    ob = ob_ref[...]
    lses = [r[...] for r in lse_refs]
    mx = jnp.maximum(jnp.maximum(lses[0], lses[1]), lses[2])
    es = [jnp.exp(v - mx) for v in lses]
    inv = 1.0 / (es[0] + es[1] + es[2])
    alphas = [e * inv for e in es]
    oc = alphas[0] * og_refs[0][...] + alphas[1] * og_refs[1][...] + alphas[2] * og_refs[2][...]
    return bz, cz, gates, ob, alphas, oc


def _merge_fwd(x, proj, b_gate, ya, ob, ogs, lses, woa, wob, woc, wo):
    T = x.shape[0]
    ts = _tile(T, 256)

    def body(x_ref, p0, p1, p2, p3, p4, bg_ref, ya_ref, ob_ref, og0, og1, og2, l0, l1, l2,
             woa_ref, wob_ref, woc_ref, wo_ref, out_ref):
        bz, cz, gates, obv, alphas, oc = _merge_common((p0, p1, p2, p3, p4), bg_ref, ob_ref, (og0, og1, og2),
                                                       (l0, l1, l2))
        yb = obv * _silu(bz)
        yc = oc * _silu(cz)
        merged = (gates[0] * _mm(ya_ref[...], woa_ref[...]) + gates[1] * _mm(yb, wob_ref[...])
                  + gates[2] * _mm(yc, woc_ref[...]))
        out_ref[...] = x_ref[...] + _mm(merged, wo_ref[...])

    def whole(r, c):
        return pl.BlockSpec((r, c), lambda i: (0, 0))

    tok = lambda w: pl.BlockSpec((ts, w), lambda i: (i, 0))
    return pl.pallas_call(
        body, name="merge_fwd", grid=(T // ts,),
        in_specs=[tok(D_MODEL)] + _merge_proj_specs(ts) + [whole(1, 3 * D_MODEL), tok(CONV_WIDTH)]
                 + [tok(DIL_WIDTH)] * 7 + [whole(CONV_WIDTH, D_MODEL)] * 3 + [whole(D_MODEL, D_MODEL)],
        out_specs=tok(D_MODEL),
        out_shape=jax.ShapeDtypeStruct((T, D_MODEL), F32),
        compiler_params=_cp(),
    )(x, *[proj] * 5, b_gate, ya, ob, *ogs, *lses, woa, wob, woc, wo)


def _merge_bwd(dout, proj, b_gate, ya, ob, ogs, lses, woa, wob, woc, wo):
    T = dout.shape[0]
    ts = _tile(T, 256)
    nt = T // ts

    def body(do_ref, p0, p1, p2, p3, p4, bg_ref, ya_ref, ob_ref, og0, og1, og2, l0, l1, l2,
             woa_ref, wob_ref, woc_ref, wo_ref,
             dp_ref, dya_ref, dob_ref, dlb_ref, dg0, dg1, dg2, dl0, dl1, dl2,
             mg_ref, dpa_ref, dpb_ref, dpc_ref, yb_ref, yc_ref, dbg_ref, st_bz, st_cz, st_gate, sems):
        step = pl.program_id(0)
        slot = step % 2

        def copies_of(s):
            return _put_copies([st_bz, st_cz, st_gate], dp_ref, sems, s % 2, pl.ds(pl.multiple_of(s * ts, ts), ts),
                               [CB_BZ * LANE, CB_CZ * LANE, CB_GATE * LANE])

        @pl.when(step >= 2)
        def _():
            for cp in copies_of(step - 2):
                cp.wait()

        bz, cz, gates, obv, alphas, oc = _merge_common((p0, p1, p2, p3, p4), bg_ref, ob_ref, (og0, og1, og2),
                                                       (l0, l1, l2))
        sb, sc = _silu(bz), _silu(cz)
        yb = obv * sb
        yc = oc * sc
        ps = [_mm(ya_ref[...], woa_ref[...]), _mm(yb, wob_ref[...]), _mm(yc, woc_ref[...])]
        mg_ref[...] = (gates[0] * ps[0] + gates[1] * ps[1] + gates[2] * ps[2]).astype(BF16)
        yb_ref[...] = yb.astype(BF16)
        yc_ref[...] = yc.astype(BF16)
        dm = _mm_nt(do_ref[...], wo_ref[...])
        dps = []
        first = pl.program_id(0) == 0
        for i, dref in enumerate((dpa_ref, dpb_ref, dpc_ref)):
            g = gates[i]
            dpi = (dm * g).astype(BF16)
            dref[...] = dpi
            dps.append(dpi)
            dgp = dm * ps[i] * g * (1.0 - g)
            st_gate[slot, :, i * D_MODEL:(i + 1) * D_MODEL] = dgp.astype(BF16)
            part = jnp.sum(dgp, axis=0, keepdims=True)

            @pl.when(first)
            def _():
                dbg_ref[:, i * D_MODEL:(i + 1) * D_MODEL] = part

            @pl.when(jnp.logical_not(first))
            def _():
                dbg_ref[:, i * D_MODEL:(i + 1) * D_MODEL] += part

        dya_ref[...] = _mm_nt(dps[0], woa_ref[...])
        dyb = _mm_nt(dps[1], wob_ref[...])
        dyc = _mm_nt(dps[2], woc_ref[...])
        st_bz[slot] = (dyb * obv * _dsilu(bz)).astype(BF16)
        st_cz[slot] = (dyc * oc * _dsilu(cz)).astype(BF16)
        for cp in copies_of(step):
            cp.start()
        dob = dyb * sb
        doc = dyc * sc
        dob_ref[...] = dob.astype(BF16)
        for c in range(NPAIR):
            cs = slice(c * LANE, (c + 1) * LANE)
            dlb_ref[:, cs] = _head_bcast_sum(dob[:, cs] * obv[:, cs])
            dd = _head_bcast_sum(doc[:, cs] * oc[:, cs])
            for a, dref, lref in zip(alphas, (dg0, dg1, dg2), (dl0, dl1, dl2)):
                dref[:, cs] = a[:, cs] * doc[:, cs]
                lref[:, cs] = a[:, cs] * dd

        @pl.when(step == nt - 1)
        def _():
            if nt >= 2:
                for cp in copies_of(step - 1):
                    cp.wait()
            for cp in copies_of(step):
                cp.wait()

    def whole(r, c):
        return pl.BlockSpec((r, c), lambda i: (0, 0))

    tok = lambda w: pl.BlockSpec((ts, w), lambda i: (i, 0))
    sd = jax.ShapeDtypeStruct
    W = DIL_WIDTH
    return pl.pallas_call(
        body, name="merge_bwd", grid=(nt,),
        in_specs=[tok(D_MODEL)] + _merge_proj_specs(ts) + [whole(1, 3 * D_MODEL), tok(CONV_WIDTH)] + [tok(W)] * 7
                 + [whole(CONV_WIDTH, D_MODEL)] * 3 + [whole(D_MODEL, D_MODEL)],
        out_specs=[pl.BlockSpec(memory_space=pl.ANY), tok(CONV_WIDTH), tok(W), tok(W)] + [tok(W)] * 6
                  + [tok(D_MODEL)] * 4 + [tok(W), tok(W), whole(1, 3 * D_MODEL)],
        out_shape=[sd((T, PP), BF16), sd((T, CONV_WIDTH), F32), sd((T, W), BF16), sd((T, W), F32)]
                  + [sd((T, W), F32)] * 6
                  + [sd((T, D_MODEL), BF16)] * 4 + [sd((T, W), BF16)] * 2 + [sd((1, 3 * D_MODEL), F32)],
        scratch_shapes=[pltpu.VMEM((2, ts, W), BF16), pltpu.VMEM((2, ts, W), BF16),
                        pltpu.VMEM((2, ts, 3 * D_MODEL), BF16), pltpu.SemaphoreType.DMA((2, 3))],
        compiler_params=_cp(),
    )(dout, *[proj] * 5, b_gate, ya, ob, *ogs, *lses, woa, wob, woc, wo)


def _loss_head(y, target):
    T = y.shape[0]
    ts = _tile(T, 512)

    def body(y_ref, t_ref, d_ref, l_ref):
        e = y_ref[...] - t_ref[...]
        d_ref[...] = e * (1.0 / D_MODEL)
        l_ref[...] = jnp.zeros((1, 8, LANE), F32) + jnp.sum(e * e)

    tok = pl.BlockSpec((ts, D_MODEL), lambda i: (i, 0))
    return pl.pallas_call(
        body, name="loss_head", grid=(T // ts,), in_specs=[tok, tok],
        out_specs=[tok, pl.BlockSpec((1, 8, LANE), lambda i: (i, 0, 0))],
        out_shape=[jax.ShapeDtypeStruct((T, D_MODEL), F32), jax.ShapeDtypeStruct((T // ts, 8, LANE), F32)],
        compiler_params=_cp(),
    )(y, target)


def _my_index():
    return 4 * lax.axis_index("x") + 2 * lax.axis_index("y") + lax.axis_index("c")


def _peers():
    x, y, c = (lax.axis_index(a) for a in AXES)
    out = []
    for kk in range(1, N_DEV):
        px = 1 - x if kk & 4 else x
        py = 1 - y if kk & 2 else y
        pc = 1 - c if kk & 1 else c
        out.append(((px, py, pc), 4 * px + 2 * py + pc))
    return out


def _exchange(arrays, name, gather):
    n = len(arrays)

    def body(*refs):
        srcs, outs = refs[:n], refs[n:2 * n]
        send_sems, recv_sems, local_sems = refs[2 * n:]
        me = _my_index()
        peers = _peers()
        started = []
        for a, (src, out) in enumerate(zip(srcs, outs)):
            mine = pltpu.make_async_copy(src if gather else src.at[me], out.at[me], local_sems.at[a])
            mine.start()
            started.append(mine)
        sends = []
        for i, (pos, idx) in enumerate(peers):
            for a, (src, out) in enumerate(zip(srcs, outs)):
                cp = pltpu.make_async_remote_copy(
                    src_ref=src if gather else src.at[idx], dst_ref=out.at[me], send_sem=send_sems.at[a, i],
                    recv_sem=recv_sems.at[a, i], device_id=pos, device_id_type=pl.DeviceIdType.MESH)
                cp.start()
                sends.append(cp)
        for i, (pos, idx) in enumerate(peers):
            for a, (src, out) in enumerate(zip(srcs, outs)):
                pltpu.make_async_remote_copy(
                    src_ref=src if gather else src.at[idx], dst_ref=out.at[idx], send_sem=send_sems.at[a, i],
                    recv_sem=recv_sems.at[a, i], device_id=pos, device_id_type=pl.DeviceIdType.MESH).wait_recv()
        for cp in sends:
            cp.wait_send()
        for mine in started:
            mine.wait()

    any_space = pl.BlockSpec(memory_space=pl.ANY)
    return pl.pallas_call(
        body, name=name, in_specs=[any_space] * n, out_specs=[any_space] * n,
        out_shape=[jax.ShapeDtypeStruct(((N_DEV,) + a.shape) if gather else a.shape, a.dtype) for a in arrays],
        scratch_shapes=[pltpu.SemaphoreType.DMA((n, N_DEV - 1)), pltpu.SemaphoreType.DMA((n, N_DEV - 1)),
                        pltpu.SemaphoreType.DMA((n,))],
    )(*arrays)


N_CHIP = 4


def _chip_places():
    x, y, c = (lax.axis_index(a) for a in AXES)
    return (x, y, c), (x, y, 1 - c), [(1 - x, y, c), (x, 1 - y, c), (1 - x, 1 - y, c)]


def _index_of(pos):
    return 4 * pos[0] + 2 * pos[1] + pos[2]


def _gather_two_level(arrays, name):
    n = len(arrays)

    def body(*refs):
        srcs, outs = refs[:n], refs[n:2 * n]
        send_sems, recv_sems, local_sems = refs[2 * n:]
        me, sibling, others = _chip_places()

        def copy(a, k, block, to, src=None):
            slot = outs[a].at[_index_of(block)]
            return pltpu.make_async_remote_copy(
                src_ref=slot if src is None else src, dst_ref=slot, send_sem=send_sems.at[7 * a + k],
                recv_sem=recv_sems.at[7 * a + k], device_id=to, device_id_type=pl.DeviceIdType.MESH)

        started = []
        for a, src in enumerate(srcs):
            mine = pltpu.make_async_copy(src, outs[a].at[_index_of(me)], local_sems.at[a])
            mine.start()
            started.append(mine)
        sends = []
        for a, src in enumerate(srcs):
            sends.append(copy(a, 0, me, sibling, src))
            sends += [copy(a, 1 + j, me, chip, src) for j, chip in enumerate(others)]
        for cp in sends:
            cp.start()
        for j, chip in enumerate(others):
            for a in range(n):
                copy(a, 1 + j, chip, me).wait_recv()
                fwd = copy(a, 4 + j, chip, sibling)
                fwd.start()
                sends.append(fwd)
        for a in range(n):
            copy(a, 0, sibling, me).wait_recv()
            for j, chip in enumerate(others):
                copy(a, 4 + j, (chip[0], chip[1], sibling[2]), me).wait_recv()
        for cp in sends:
            cp.wait_send()
        for mine in started:
            mine.wait()

    any_space = pl.BlockSpec(memory_space=pl.ANY)
    return pl.pallas_call(
        body, name=name, in_specs=[any_space] * n, out_specs=[any_space] * n,
        out_shape=[jax.ShapeDtypeStruct((N_DEV,) + a.shape, a.dtype) for a in arrays],
        scratch_shapes=[pltpu.SemaphoreType.DMA((7 * n,)), pltpu.SemaphoreType.DMA((7 * n,)),
                        pltpu.SemaphoreType.DMA((n,))],
    )(*arrays)


def _sibling_swap(arrays, name):
    n = len(arrays)

    def body(*refs):
        srcs, outs = refs[:n], refs[n:2 * n]
        send_sems, recv_sems = refs[2 * n:]
        (x, y, c), sibling, _ = _chip_places()
        sends = []
        for a, (src, out) in enumerate(zip(srcs, outs)):
            for q in range(N_CHIP):
                def copy(core, a=a, q=q, src=src, out=out):
                    return pltpu.make_async_remote_copy(
                        src_ref=src.at[2 * q + core], dst_ref=out.at[q], send_sem=send_sems.at[N_CHIP * a + q],
                        recv_sem=recv_sems.at[N_CHIP * a + q], device_id=sibling, device_id_type=pl.DeviceIdType.MESH)
                mine = copy(1 - c)
                mine.start()
                sends.append((mine, copy(c)))
        for mine, arrival in sends:
            arrival.wait_recv()
            mine.wait_send()

    any_space = pl.BlockSpec(memory_space=pl.ANY)
    return pl.pallas_call(
        body, name=name, in_specs=[any_space] * n, out_specs=[any_space] * n,
        out_shape=[jax.ShapeDtypeStruct((N_CHIP,) + a.shape[1:], a.dtype) for a in arrays],
        scratch_shapes=[pltpu.SemaphoreType.DMA((N_CHIP * n,)), pltpu.SemaphoreType.DMA((N_CHIP * n,))],
    )(*arrays)


def _chip_pair_sum(part, got, name):
    R, C = part.shape[1:]
    tr = R
    while tr * C * part.dtype.itemsize > REDUCE_BLOCK_BYTES // 4 and tr % 32 == 0:
        tr //= 2
    c = lax.axis_index("c")

    def body(c_ref, p_ref, g_ref, o_ref):
        del c_ref
        o_ref[...] = (p_ref[...].astype(F32) + g_ref[...].astype(F32)).astype(o_ref.dtype)

    return pl.pallas_call(
        body, name=name, grid_spec=pltpu.PrefetchScalarGridSpec(
            num_scalar_prefetch=1, grid=(N_CHIP, R // tr),
            in_specs=[pl.BlockSpec((None, tr, C), lambda q, i, cr: (2 * q + cr[0], i, 0)),
                      pl.BlockSpec((None, tr, C), lambda q, i, cr: (q, i, 0))],
            out_specs=pl.BlockSpec((None, tr, C), lambda q, i, cr: (q, i, 0))),
        out_shape=jax.ShapeDtypeStruct((N_CHIP, R, C), part.dtype),
        compiler_params=_cp(),
    )(jnp.reshape(c, (1,)).astype(jnp.int32), part, got)


def _chip_exchange(arrays, name):
    n = len(arrays)

    def body(*refs):
        srcs, outs = refs[:n], refs[n:2 * n]
        send_sems, recv_sems, local_sems = refs[2 * n:]
        (x, y, c), _, others = _chip_places()
        mychip = 2 * x + y
        started, sends = [], []
        for a, (src, out) in enumerate(zip(srcs, outs)):
            mine = pltpu.make_async_copy(src.at[mychip], out.at[mychip], local_sems.at[a])
            mine.start()
            started.append(mine)
        for j, chip in enumerate(others):
            q = 2 * chip[0] + chip[1]
            for a, (src, out) in enumerate(zip(srcs, outs)):
                def copy(slot, a=a, j=j, q=q, chip=chip, src=src, out=out):
                    return pltpu.make_async_remote_copy(
                        src_ref=src.at[q], dst_ref=out.at[slot], send_sem=send_sems.at[3 * a + j],
                        recv_sem=recv_sems.at[3 * a + j], device_id=chip, device_id_type=pl.DeviceIdType.MESH)
                mine = copy(mychip)
                mine.start()
                sends.append((mine, copy(q)))
        for mine, arrival in sends:
            arrival.wait_recv()
        for mine, arrival in sends:
            mine.wait_send()
        for mine in started:
            mine.wait()

    any_space = pl.BlockSpec(memory_space=pl.ANY)
    return pl.pallas_call(
        body, name=name, in_specs=[any_space] * n, out_specs=[any_space] * n,
        out_shape=[jax.ShapeDtypeStruct(a.shape, a.dtype) for a in arrays],
        scratch_shapes=[pltpu.SemaphoreType.DMA((3 * n,)), pltpu.SemaphoreType.DMA((3 * n,)),
                        pltpu.SemaphoreType.DMA((n,))],
    )(*arrays)


def _remote_copies(srcs, lands, send_sems, recv_sems, gather):
    me = _my_index()
    out = []
    for i, (pos, idx) in enumerate(_peers()):
        for a, (src, land) in enumerate(zip(srcs, lands)):
            def copy(slot, a=a, src=src, land=land, i=i, pos=pos, idx=idx):
                return pltpu.make_async_remote_copy(
                    src_ref=src if gather else src.at[idx], dst_ref=land.at[slot],
                    send_sem=send_sems.at[a * (N_DEV - 1) + i], recv_sem=recv_sems.at[a * (N_DEV - 1) + i],
                    device_id=pos, device_id_type=pl.DeviceIdType.MESH)
            out.append((copy(me), copy(idx)))
    return out


def _exchange_start(arrays, name, gather):
    n = len(arrays)
    hbm = pl.BlockSpec(memory_space=pltpu.HBM)
    sem = pl.BlockSpec(memory_space=pltpu.SEMAPHORE)
    lands = [lax.empty(((N_DEV,) + a.shape) if gather else a.shape, a.dtype) for a in arrays]

    def body(*refs):
        srcs, lands_ = refs[:n], refs[n:2 * n]
        send_sems, recv_sems = refs[2 * n:2 * n + 2]
        for mine, _ in _remote_copies(srcs, lands_, send_sems, recv_sems, gather):
            mine.start()
        refs[-1][...] = jnp.zeros_like(refs[-1])

    sems = pltpu.SemaphoreType.DMA((n * (N_DEV - 1),))
    buffers = [pltpu.HBM(a.shape, a.dtype) for a in list(arrays) + lands]
    res = pl.pallas_call(
        body, name=name, in_specs=[hbm] * (2 * n), out_specs=[sem, sem] + [hbm] * (2 * n) + [pl.BlockSpec(memory_space=pltpu.VMEM)],
        out_shape=[sems, sems] + buffers + [jax.ShapeDtypeStruct((8, LANE), F32)],
        input_output_aliases={i: 2 + i for i in range(2 * n)},
        compiler_params=pltpu.CompilerParams(has_side_effects=pltpu.SideEffectType.DATAFLOW_SIDE_EFFECTING),
    )(*[pltpu.with_memory_space_constraint(a, pltpu.HBM) for a in list(arrays) + lands])
    return (res[0], res[1], res[2:2 + n], res[2 + n:2 + 2 * n]), res[-1]


def _exchange_wait(handle, after, name, gather):
    send_sems, recv_sems, srcs, lands = handle
    n = len(srcs)
    hbm = pl.BlockSpec(memory_space=pltpu.HBM)
    sem = pl.BlockSpec(memory_space=pltpu.SEMAPHORE)

    def body(*refs):
        for mine, arrival in _remote_copies(refs[:n], refs[n:2 * n], refs[2 * n], refs[2 * n + 1], gather):
            mine.wait_send()
            arrival.wait_recv()

    res = pl.pallas_call(
        body, name=name, in_specs=[hbm] * (2 * n) + [sem, sem, pl.BlockSpec(memory_space=pl.ANY)],
        out_specs=[hbm] * (2 * n), out_shape=[pltpu.HBM(a.shape, a.dtype) for a in list(srcs) + list(lands)],
        input_output_aliases={i: i for i in range(2 * n)},
        compiler_params=pltpu.CompilerParams(has_side_effects=pltpu.SideEffectType.DATAFLOW_SIDE_EFFECTING),
    )(*srcs, *lands, send_sems, recv_sems, after)
    return res[n:]


def _own_slot(land, mine):
    return lax.dynamic_update_slice(land, mine, (_my_index(),) + (0,) * (land.ndim - 1))


def _adamw(w, g, m, v):
    m = ADAM_B1 * m + (1.0 - ADAM_B1) * g
    v = ADAM_B2 * v + (1.0 - ADAM_B2) * (g * g)
    m_hat = m / (1.0 - ADAM_B1 ** ADAM_STEP)
    v_hat = v / (1.0 - ADAM_B2 ** ADAM_STEP)
    delta = -ADAM_LR * (m_hat / (jnp.sqrt(v_hat) + ADAM_EPS) + ADAM_WD * w)
    return delta, m, v


def _reduce_adamw(parts, w, m, v, name):
    nparts = len(parts)
    R, C = parts[0].shape[1:]
    tr = R
    while N_DEV * tr * C * parts[0].dtype.itemsize > REDUCE_BLOCK_BYTES and tr % 32 == 0:
        tr //= 2
    steps = R // tr

    def body(*refs):
        w_ref, m_ref, v_ref, g_ref, d_ref, nm_ref, nv_ref = refs[nparts:]
        for k, p_ref in enumerate(refs[:nparts]):
            @pl.when(pl.program_id(0) // steps == k)
            def _():
                g = p_ref[0].astype(F32)
                for s in range(1, p_ref.shape[0]):
                    g = g + p_ref[s].astype(F32)
                g_ref[...] = g
                d_ref[...], nm_ref[...], nv_ref[...] = _adamw(w_ref[...], g, m_ref[...], v_ref[...])

    def part_spec(k):
        return pl.BlockSpec((parts[k].shape[0], tr, C), lambda i: (0, jnp.clip(i - k * steps, 0, steps - 1), 0))

    row = pl.BlockSpec((tr, C), lambda i: (i, 0))
    return pl.pallas_call(
        body, name=name, grid=(nparts * steps,),
        in_specs=[part_spec(k) for k in range(nparts)] + [row, row, row],
        out_specs=[row] * 4, out_shape=[jax.ShapeDtypeStruct((nparts * R, C), F32)] * 4,
        compiler_params=_cp(),
    )(*parts, w, m, v)


BIG = ("w_in", "w_uq", "w_ukv", "w_out_a", "w_out_b", "w_out_c", "w_o")
SMALL = ("norm_g", "b_gate", "conv_w", "conv_b", "q_a_norm_g", "kv_a_norm_g", "mla_q_norm_g", "mla_k_norm_g",
         "dil_q_norm_g", "dil_k_norm_g")
PACK_ROWS = 128
REDUCE_BLOCK_BYTES = 6 * 1024 * 1024


def _pack_local(tensors):
    flat = jnp.concatenate([t.reshape(-1) for t in tensors])
    pad = (-flat.shape[0]) % (PACK_ROWS * LANE)
    return jnp.concatenate([flat, jnp.zeros((pad,), flat.dtype)]).reshape(-1, LANE)


def _unpack_local(rows, like):
    flat = rows.reshape(-1)
    out, off = [], 0
    for t in like:
        out.append(flat[off:off + t.size].reshape(t.shape))
        off += t.size
    return out


def _cols_to_slots(a):
    k = a.shape[0]
    return a.reshape(k, N_DEV, -1).transpose(1, 0, 2)


def _slots_to_cols(s):
    return s.transpose(1, 0, 2).reshape(s.shape[1], -1)


def _rope_tables(S):
    inv = ROPE_THETA ** (-jnp.arange(0, MLA_ROPE, 2, dtype=F32) / MLA_ROPE)
    ang = jnp.arange(S, dtype=F32)[:, None] * inv[None, :]
    cos, sin = jnp.cos(ang), jnp.sin(ang)
    one = jnp.ones((S, MLA_NOPE), F32)
    z16, z32, z64 = (jnp.zeros((S, n), F32) for n in (16, 32, 64))
    cosp = jnp.concatenate([one, cos, cos, jnp.ones((S, 32), F32)], axis=1)
    sa = jnp.concatenate([z64, -sin, z16, z32], axis=1)
    sb = jnp.concatenate([z64, z16, sin, z32], axis=1)
    return cosp, sa, sb


def _alibi_slopes():
    n = DIL_GROUPS * DIL_HEADS
    m = 2.0 ** (-8.0 * jnp.arange(1, n + 1, dtype=F32) / n)
    return m.reshape(DIL_GROUPS, NPAIR, 2)


def _pad_slots(s):
    n, k, c = s.shape
    return _slots_to_cols(jnp.concatenate([s, jnp.zeros((n, k, LANE - c), s.dtype)], axis=2))


def _layer_params(gw, small, l):
    p = {}
    p["wp"] = _pad_columns(gw["w_in"])
    p["norm_g"] = small["norm_g"][l][None]
    p["b_gate"] = small["b_gate"][l][None]
    p["conv_w"] = gw["conv_w"].transpose(1, 0, 2).reshape(CONV_K, CONV_WIDTH)
    p["conv_b"] = small["conv_b"][l][None]
    p["gq"] = small["q_a_norm_g"][l][None]
    p["gkv"] = small["kv_a_norm_g"][l][None]
    p["wuqp"] = _pad_slots(gw["w_uq"])
    kv = gw["w_ukv"]
    p["wkp"] = _pad_slots(kv[:, :, :MLA_NOPE])
    p["wv"] = kv[:, :, MLA_NOPE:].transpose(1, 0, 2).reshape(MLA_KV_LORA, MLA_HEADS * MLA_V)
    zpad = jnp.zeros((1, LANE - MLA_QK), F32)
    p["gmq"] = jnp.concatenate([small["mla_q_norm_g"][l][None], zpad], axis=1)
    p["gmk"] = jnp.concatenate([small["mla_k_norm_g"][l][None], zpad], axis=1)
    tile = lambda g: jnp.broadcast_to(g[:, None, :], (DIL_GROUPS, DIL_HEADS, DIL_HEAD_DIM)).reshape(1, DIL_QK)
    p["gdq"] = tile(small["dil_q_norm_g"][l])
    p["gdk"] = tile(small["dil_k_norm_g"][l])
    p["woa"], p["wob"], p["woc"] = (_slots_to_cols(gw[n]) for n in ("w_out_a", "w_out_b", "w_out_c"))
    p["wo"] = gw["w_o"].reshape(D_MODEL, D_MODEL)
    return p


def _layer_fwd(x, p, tabs, slopes, B, S):
    proj, ht = _inproj_fwd(x, p["norm_g"], p["wp"])
    ya = _mixa_fwd(proj, p["conv_w"], p["conv_b"], B, S)
    q, k, v = _mla_prep_fwd(proj, p["gq"], p["gkv"], p["wuqp"], p["wkp"], p["wv"], p["gmq"], p["gmk"], *tabs, S)
    ob, lse_b = _mla_attn_fwd(q, k, v, B, S)
    qn, kn, vn = _dil_prep_fwd(proj, p["gdq"], p["gdk"])
    ogs, lses = [], []
    for gi in range(DIL_GROUPS):
        o, lse = _dil_attn_fwd(gi, slopes[gi], qn, kn, vn, B, S)
        ogs.append(o)
        lses.append(lse)
    out = _merge_fwd(x, proj, p["b_gate"], ya, ob, ogs, lses, p["woa"], p["wob"], p["woc"], p["wo"])
    saved = dict(x=x, proj=proj, ht=ht, ya=ya, q=q, k=k, v=v, ob=ob, lse_b=lse_b, qn=qn, kn=kn, vn=vn, ogs=ogs, lses=lses)
    return out, saved


def _layer_bwd(dout, sv, p, tabs, slopes, B, S):
    proj = sv["proj"]
    (dproj, dya, dob, dlb, dg0, dg1, dg2, dl0, dl1, dl2, merged, dpa, dpb, dpc, yb, yc, dbg) = _merge_bwd(
        dout, proj, p["b_gate"], sv["ya"], sv["ob"], sv["ogs"], sv["lses"], p["woa"], p["wob"], p["woc"], p["wo"])
    g = {}
    g["w_o"] = _matmul_tn(merged, dout, "dw_o").reshape(N_DEV, D_MODEL // N_DEV, D_MODEL)
    g["w_out_a"] = _cols_to_slots(_matmul_tn(sv["ya"], dpa, "dw_out_a"))
    g["w_out_b"] = _cols_to_slots(_matmul_tn(yb, dpb, "dw_out_b"))
    g["w_out_c"] = _cols_to_slots(_matmul_tn(yc, dpc, "dw_out_c"))
    g["b_gate"] = dbg[0]
    dproj, st = _mixa_bwd(dproj, dya, proj, p["conv_w"], p["conv_b"], B, S)
    g["conv_w"] = st[0:CONV_K]
    g["conv_b"] = st[CONV_K]
    dq, dk, dv = _mla_attn_bwd(sv["q"], sv["k"], sv["v"], dob, sv["lse_b"], dlb, B, S)
    dproj, dwuqp, dwkp, dwv, dgq, dgkv, dgmq, dgmk = _mla_prep_bwd(
        dproj, dq, dk, dv, proj, p["gq"], p["gkv"], p["wuqp"], p["wkp"], p["wv"], p["gmq"], p["gmk"], *tabs, S)
    g["w_uq"] = _cols_to_slots(dwuqp)[:, :, :MLA_QK]
    g["w_ukv"] = jnp.concatenate([_cols_to_slots(dwkp)[:, :, :MLA_NOPE], _cols_to_slots(dwv)], axis=2)
    g["q_a_norm_g"], g["kv_a_norm_g"] = dgq[0], dgkv[0]
    g["mla_q_norm_g"], g["mla_k_norm_g"] = dgmq[0, :MLA_QK], dgmk[0, :MLA_QK]
    dqkv = None
    for gi, (dog, dlg) in enumerate(((dg0, dl0), (dg1, dl1), (dg2, dl2))):
        dqkv = _dil_attn_bwd(gi, slopes[gi], sv["qn"], sv["kn"], sv["vn"], dog, sv["lses"][gi], dlg, dqkv, B, S)
    dproj, dgdq, dgdk = _dil_prep_bwd(dproj, *dqkv, proj, p["gdq"], p["gdk"])
    g["dil_q_norm_g"] = dgdq.reshape(DIL_GROUPS, DIL_HEADS, DIL_HEAD_DIM).sum(axis=1)
    g["dil_k_norm_g"] = dgdk.reshape(DIL_GROUPS, DIL_HEADS, DIL_HEAD_DIM).sum(axis=1)
    g["w_in"] = _unpad_columns(_matmul_nn(sv["ht"], dproj, "dw_in"))
    dx, dng = _inproj_bwd_x(dproj, p["wp"], sv["x"], p["norm_g"], dout)
    g["norm_g"] = dng[0]
    return dx, g


def _after(token, a):
    return a if token is None else a + token[0:1, 0:1]


def _local_step(x, target, small, B, S, weights_of, grads_out):
    tabs = _rope_tables(S)
    sl = _alibi_slopes()
    slopes = [sl[gi] * float(DIL_PATTERNS[gi][1]) for gi in range(DIL_GROUPS)]
    params, saved = [], []
    for l in range(DEPTH):
        gw, token = weights_of(l, x)
        p = _layer_params(gw, small, l)
        p["norm_g"] = _after(token, p["norm_g"])
        x, sv = _layer_fwd(x, p, tabs, slopes, B, S)
        params.append(p)
        saved.append(sv)
    dout, lparts = _loss_head(x, target)
    sq = jnp.sum(lparts[:, 0, 0])
    token = None
    for l in reversed(range(DEPTH)):
        p = dict(params[l], b_gate=_after(token, params[l]["b_gate"]))
        dout, g = _layer_bwd(dout, saved[l], p, tabs, slopes, B, S)
        token = grads_out(l, g, dout)
    return sq, dout


def kernel(x, norm_g, w_in, b_gate, conv_w, conv_b, q_a_norm_g, w_uq, kv_a_norm_g, w_ukv, mla_q_norm_g, mla_k_norm_g, dil_q_norm_g, dil_k_norm_g, w_out_a, w_out_b, w_out_c, w_o, loss_target, m_norm_g, m_w_in, m_b_gate, m_conv_w, m_conv_b, m_q_a_norm_g, m_w_uq, m_kv_a_norm_g, m_w_ukv, m_mla_q_norm_g, m_mla_k_norm_g, m_dil_q_norm_g, m_dil_k_norm_g, m_w_out_a, m_w_out_b, m_w_out_c, m_w_o, v_norm_g, v_w_in, v_b_gate, v_conv_w, v_conv_b, v_q_a_norm_g, v_w_uq, v_kv_a_norm_g, v_w_ukv, v_mla_q_norm_g, v_mla_k_norm_g, v_dil_q_norm_g, v_dil_k_norm_g, v_w_out_a, v_w_out_b, v_w_out_c, v_w_o):
    names = ("norm_g", "w_in", "b_gate", "conv_w", "conv_b", "q_a_norm_g", "w_uq", "kv_a_norm_g", "w_ukv",
             "mla_q_norm_g", "mla_k_norm_g", "dil_q_norm_g", "dil_k_norm_g", "w_out_a", "w_out_b", "w_out_c", "w_o")
    w = dict(zip(names, (norm_g, w_in, b_gate, conv_w, conv_b, q_a_norm_g, w_uq, kv_a_norm_g, w_ukv, mla_q_norm_g,
                         mla_k_norm_g, dil_q_norm_g, dil_k_norm_g, w_out_a, w_out_b, w_out_c, w_o)))
    m = dict(zip(names, (m_norm_g, m_w_in, m_b_gate, m_conv_w, m_conv_b, m_q_a_norm_g, m_w_uq, m_kv_a_norm_g, m_w_ukv,
                         m_mla_q_norm_g, m_mla_k_norm_g, m_dil_q_norm_g, m_dil_k_norm_g, m_w_out_a, m_w_out_b,
                         m_w_out_c, m_w_o)))
    v = dict(zip(names, (v_norm_g, v_w_in, v_b_gate, v_conv_w, v_conv_b, v_q_a_norm_g, v_w_uq, v_kv_a_norm_g, v_w_ukv,
                         v_mla_q_norm_g, v_mla_k_norm_g, v_dil_q_norm_g, v_dil_k_norm_g, v_w_out_a, v_w_out_b,
                         v_w_out_c, v_w_o)))
    B, S, _ = x.shape
    me = _my_index()
    cshard = CONV_WIDTH // N_DEV

    shards = [[w[n][l].astype(BF16) for n in BIG] for l in range(DEPTH)]
    state = {}

    def weights_of(l, after):
        if l == 0:
            got = _gather_two_level(shards[0] + [conv_w], "all_gather_weights_0")
            state["gather"], token = _exchange_start(shards[1], "all_gather_weights_1_start", gather=True)
            state["conv_w"] = got[-1]
        else:
            landed = _exchange_wait(state["gather"], after, "all_gather_weights_1_wait", gather=True)
            got, token = [_own_slot(a, s[None]) for a, s in zip(landed, shards[1])], None
        gw = dict(zip(BIG, got))
        gw["conv_w"] = state["conv_w"][:, l]
        return gw, token

    recv, small_parts = {}, {}

    def grads_out(l, g, after):
        small_parts[l] = [g[n] for n in SMALL]
        send = [g[n].astype(BF16) for n in BIG]
        if l == DEPTH - 1:
            state["scatter"], token = _exchange_start(send, "exchange_weight_grads_1_start", gather=False)
            state["sent"] = send
            return token
        landed = _exchange_wait(state["scatter"], after, "exchange_weight_grads_1_wait", gather=False)
        mine = [lax.dynamic_slice_in_dim(s, me, 1, axis=0) for s in state["sent"]]
        recv[DEPTH - 1] = [_own_slot(a, s) for a, s in zip(landed, mine)]
        swapped = _sibling_swap(send, "exchange_weight_grads_0_sibling")
        sums = [_chip_pair_sum(s, t, "chip_pair_sum_" + n) for n, s, t in zip(BIG, send, swapped)]
        recv[l] = _chip_exchange(sums, "exchange_weight_grads_0")
        return None

    sq, grad_x = _local_step(x.reshape(B * S, D_MODEL), loss_target.reshape(B * S, D_MODEL), w, B, S,
                             weights_of, grads_out)
    loss = lax.psum(sq * (0.5 / D_MODEL), AXES)

    res = {}
    for i, n in enumerate(BIG):
        rows = lambda a: a.reshape(-1, a.shape[-1])
        outs = _reduce_adamw([recv[l][i] for l in range(DEPTH)], rows(w[n]), rows(m[n]), rows(v[n]),
                             "reduce_adamw_" + n)
        res[n] = tuple(a.reshape(w[n].shape) for a in outs)
    part = {n: jnp.stack([small_parts[l][i] for l in range(DEPTH)]) for i, n in enumerate(SMALL)}

    def widen(t):
        return lax.dynamic_update_slice(jnp.zeros((DEPTH, CONV_K, CONV_WIDTH), F32), t, (0, 0, me * cshard))

    small_like = [part[n] for n in SMALL]
    pick = lambda d: [widen(d[n]) if n == "conv_w" else d[n] for n in SMALL]
    parts, = _exchange([_pack_local(small_like)], "all_gather_small_grads", gather=True)
    gs, ds, ms, vs = _reduce_adamw([parts], _pack_local(pick(w)), _pack_local(pick(m)), _pack_local(pick(v)),
                                   "reduce_adamw_small")
    for n, t in zip(SMALL, zip(*(_unpack_local(a, small_like) for a in (gs, ds, ms, vs)))):
        if n == "conv_w":
            t = tuple(lax.dynamic_slice(a, (0, 0, me * cshard), (DEPTH, CONV_K, cshard)) for a in t)
        res[n] = t

    out = [loss, grad_x.reshape(B, S, D_MODEL)]
    for i in range(4):
        out += [res[n][i] for n in names]
    return tuple(out)
```

```python
import jax
import jax.numpy as jnp
from jax import lax
from jax.experimental import pallas as pl
from jax.experimental.pallas import tpu as pltpu

F32 = jnp.float32
BF16 = jnp.bfloat16

D_MODEL = 1024
DEPTH = 2
CONV_WIDTH = 512
CONV_K = 3
MLA_HEADS = 8
MLA_Q_LORA = 256
MLA_KV_LORA = 128
MLA_NOPE = 64
MLA_ROPE = 32
MLA_V = 64
MLA_QK = MLA_NOPE + MLA_ROPE
ROPE_THETA = 10000.0
DIL_PATTERNS = ((128, 1), (512, 4), (2048, 16))
DIL_GROUPS = 3
DIL_HEADS = 8
DIL_HEAD_DIM = 64
DIL_WIDTH = DIL_HEADS * DIL_HEAD_DIM
DIL_QK = DIL_GROUPS * DIL_WIDTH
EPS = 1e-6
N_IN = 11168

ADAM_LR = 0.001
ADAM_B1 = 0.9
ADAM_B2 = 0.999
ADAM_EPS = 1e-08
ADAM_WD = 0.01
ADAM_STEP = 10

N_DEV = 8
AXES = ("x", "y", "c")
LANE = 128
HALF = 64
NPAIR = 4

CB_AB, CB_AC, CB_AX, CB_AZ = 0, 4, 8, 12
CB_CQ, CB_CKV, CB_KPE = 16, 18, 19
CB_BZ = 20
CB_DQ, CB_DK, CB_DV = 24, 36, 48
CB_CZ, CB_GATE = 60, 64
NCB = 88
PP = NCB * LANE
KPE_END = CB_KPE * LANE + MLA_ROPE
SHARD_COLS = N_IN // N_DEV
NEG = -1e30
VMEM_LIMIT = 56 * 1024 * 1024


def _pad_columns(shards):
    parts = []
    for p in range(N_DEV):
        cut = min(max(KPE_END - p * SHARD_COLS, 0), SHARD_COLS)
        if 0 < cut < SHARD_COLS:
            parts += [shards[p, :, :cut], jnp.zeros((shards.shape[1], LANE - MLA_ROPE), shards.dtype), shards[p, :, cut:]]
        else:
            parts.append(shards[p])
    return jnp.concatenate(parts, axis=1)


def _unpad_columns(wp):
    def columns(a, b):
        gap = LANE - MLA_ROPE
        if b <= KPE_END:
            return wp[:, a:b]
        if a >= KPE_END:
            return wp[:, a + gap:b + gap]
        return jnp.concatenate([wp[:, a:KPE_END], wp[:, KPE_END + gap:b + gap]], axis=1)

    return jnp.stack([columns(p * SHARD_COLS, (p + 1) * SHARD_COLS) for p in range(N_DEV)])


def _put_copies(stages, dst_ref, sems, slot, rows, cols):
    return [pltpu.make_async_copy(st.at[slot], dst_ref.at[rows, pl.ds(c0, st.shape[-1])], sems.at[slot, k])
            for k, (st, c0) in enumerate(zip(stages, cols))]


def _put_pipeline(step, nsteps, copies_of, fill):
    @pl.when(step >= 2)
    def _():
        for cp in copies_of(step - 2):
            cp.wait()

    fill(step % 2)
    for cp in copies_of(step):
        cp.start()

    @pl.when(step == nsteps - 1)
    def _():
        if nsteps >= 2:
            for cp in copies_of(step - 1):
                cp.wait()
        for cp in copies_of(step):
            cp.wait()


def _cp():
    return pltpu.CompilerParams(vmem_limit_bytes=VMEM_LIMIT)


def _rstd(x, n):
    return lax.rsqrt(jnp.sum(x * x, axis=-1, keepdims=True) * (1.0 / n) + EPS)


def _sigmoid(z):
    return 1.0 / (1.0 + jnp.exp(-z))


def _silu(z):
    return z * _sigmoid(z)


def _dsilu(z):
    s = _sigmoid(z)
    return s * (1.0 + z * (1.0 - s))


def _mm(a, b):
    return jnp.dot(a.astype(BF16), b.astype(BF16), preferred_element_type=F32)


def _mm_nt(a, b):
    return lax.dot_general(a.astype(BF16), b.astype(BF16), (((1,), (1,)), ((), ())), preferred_element_type=F32)


def _mm_tn(a, b):
    return lax.dot_general(a.astype(BF16), b.astype(BF16), (((0,), (0,)), ((), ())), preferred_element_type=F32)


def _lane_lo(shape):
    return lax.broadcasted_iota(jnp.int32, shape, len(shape) - 1) < HALF


def _head_bcast_sum(x, terms=3):
    w = x.shape[-1]
    same = (lax.broadcasted_iota(jnp.int32, (w, w), 0) // HALF) == (lax.broadcasted_iota(jnp.int32, (w, w), 1) // HALF)
    ones = jnp.where(same, 1.0, 0.0).astype(jnp.bfloat16)
    total = None
    for _ in range(terms):
        term = x.astype(jnp.bfloat16)
        x = x - term.astype(F32)
        part = jnp.dot(term, ones, preferred_element_type=F32)
        total = part if total is None else total + part
    return total


def _rope(t, cos, sa, sb):
    return t * cos + pltpu.roll(t, LANE - 16, axis=1) * sa + pltpu.roll(t, 16, axis=1) * sb


def _rope_t(d, cos, sa, sb):
    return d * cos + pltpu.roll(d * sa, 16, axis=1) + pltpu.roll(d * sb, LANE - 16, axis=1)


def _shift_down(u, k):
    rows = lax.broadcasted_iota(jnp.int32, u.shape, 0)
    return jnp.where(rows >= k, pltpu.roll(u, k, axis=0), 0.0)


def _shift_up(u, k):
    n = u.shape[0]
    rows = lax.broadcasted_iota(jnp.int32, u.shape, 0)
    return jnp.where(rows < n - k, pltpu.roll(u, n - k, axis=0), 0.0)


def _tile(n, want):
    t = min(n, want)
    assert n % t == 0, (n, want)
    return t


def _inproj_fwd(x, g, wp):
    T = x.shape[0]
    tm, tn = _tile(T, 2048), 512

    def body(x_ref, g_ref, w_ref, proj_ref, ht_ref, h_ref):
        @pl.when(pl.program_id(1) == 0)
        def _():
            n = min(tm, 512)
            for r0 in range(0, tm, n):
                xv = x_ref[r0:r0 + n, :]
                h = xv * _rstd(xv, D_MODEL) * g_ref[...]
                h_ref[r0:r0 + n, :] = h.astype(BF16)
                ht_ref[:, r0:r0 + n] = h.T.astype(BF16)

        proj_ref[...] = jnp.dot(h_ref[...], w_ref[...], preferred_element_type=F32).astype(BF16)

    return pl.pallas_call(
        body, name="inproj_fwd", grid=(T // tm, PP // tn),
        in_specs=[pl.BlockSpec((tm, D_MODEL), lambda i, j: (i, 0)),
                  pl.BlockSpec((1, D_MODEL), lambda i, j: (0, 0)),
                  pl.BlockSpec((D_MODEL, tn), lambda i, j: (0, j))],
        out_specs=[pl.BlockSpec((tm, tn), lambda i, j: (i, j)),
                   pl.BlockSpec((D_MODEL, tm), lambda i, j: (0, i))],
        out_shape=[jax.ShapeDtypeStruct((T, PP), BF16), jax.ShapeDtypeStruct((D_MODEL, T), BF16)],
        scratch_shapes=[pltpu.VMEM((tm, D_MODEL), BF16)],
        compiler_params=_cp(),
    )(x, g, wp)


def _matmul_nn(at, b, name):
    K, T = at.shape
    N = b.shape[1]
    tt, tn = _tile(T, 1024), _tile(N, 2816)
    nk = T // tt

    def body(a_ref, b_ref, o_ref, acc_ref):
        k = pl.program_id(1)

        @pl.when(k == 0)
        def _():
            acc_ref[...] = jnp.zeros_like(acc_ref)

        acc_ref[...] += jnp.dot(a_ref[...], b_ref[...], preferred_element_type=F32)

        @pl.when(k == nk - 1)
        def _():
            o_ref[...] = acc_ref[...].astype(BF16)

    return pl.pallas_call(
        body, name=name, grid=(N // tn, nk),
        in_specs=[pl.BlockSpec((K, tt), lambda j, k: (0, k)),
                  pl.BlockSpec((tt, tn), lambda j, k: (k, j))],
        out_specs=pl.BlockSpec((K, tn), lambda j, k: (0, j)),
        out_shape=jax.ShapeDtypeStruct((K, N), BF16),
        scratch_shapes=[pltpu.VMEM((K, tn), F32)],
        compiler_params=_cp(),
    )(at, b)


def _matmul_tn(a, b, name):
    T, K = a.shape
    N = b.shape[1]
    tt, tn = _tile(T, 512), _tile(N, 1024)

    def body(a_ref, b_ref, o_ref):
        @pl.when(pl.program_id(1) == 0)
        def _():
            o_ref[...] = jnp.zeros_like(o_ref)

        o_ref[...] += _mm_tn(a_ref[...], b_ref[...])

    return pl.pallas_call(
        body, name=name, grid=(N // tn, T // tt),
        in_specs=[pl.BlockSpec((tt, K), lambda j, k: (k, 0)),
                  pl.BlockSpec((tt, tn), lambda j, k: (k, j))],
        out_specs=pl.BlockSpec((K, tn), lambda j, k: (0, j)),
        out_shape=jax.ShapeDtypeStruct((K, N), F32),
        compiler_params=_cp(),
    )(a, b)


def _inproj_bwd_x(dproj, wp, x, g, dout):
    T = x.shape[0]
    tm, tk = _tile(T, 1024), 1024
    nk = PP // tk

    def body(dp_ref, w_ref, x_ref, g_ref, do_ref, dx_ref, dg_ref, acc_ref):
        i, k = pl.program_id(0), pl.program_id(1)

        @pl.when(k == 0)
        def _():
            acc_ref[...] = jnp.zeros_like(acc_ref)

        @pl.when((k == 0) & (i == 0))
        def _():
            dg_ref[...] = jnp.zeros_like(dg_ref)

        acc_ref[...] += _mm_nt(dp_ref[...], w_ref[...])

        @pl.when(k == nk - 1)
        def _():
            dh = acc_ref[...]
            xv = x_ref[...]
            r = _rstd(xv, D_MODEL)
            gy = dh * g_ref[...]
            dot = jnp.sum(xv * gy, axis=-1, keepdims=True) * (1.0 / D_MODEL)
            dx_ref[...] = do_ref[...] + r * gy - xv * (r * r * r) * dot
            dg_ref[...] += jnp.sum(dh * xv * r, axis=0, keepdims=True)

    return pl.pallas_call(
        body, name="inproj_bwd_x", grid=(T // tm, nk),
        in_specs=[pl.BlockSpec((tm, tk), lambda i, k: (i, k)),
                  pl.BlockSpec((D_MODEL, tk), lambda i, k: (0, k)),
                  pl.BlockSpec((tm, D_MODEL), lambda i, k: (i, 0)),
                  pl.BlockSpec((1, D_MODEL), lambda i, k: (0, 0)),
                  pl.BlockSpec((tm, D_MODEL), lambda i, k: (i, 0))],
        out_specs=[pl.BlockSpec((tm, D_MODEL), lambda i, k: (i, 0)),
                   pl.BlockSpec((1, D_MODEL), lambda i, k: (0, 0))],
        out_shape=[jax.ShapeDtypeStruct((T, D_MODEL), F32), jax.ShapeDtypeStruct((1, D_MODEL), F32)],
        scratch_shapes=[pltpu.VMEM((tm, D_MODEL), F32)],
        compiler_params=_cp(),
    )(dproj, wp, x, g, dout)


A_SEGS = (CB_AB, CB_AC, CB_AX, CB_AZ)


def _mixa_fwd(proj, cw, cb, B, S):
    nc = CONV_WIDTH // LANE

    def body(ab_ref, ac_ref, ax_ref, az_ref, cw_ref, cb_ref, y_ref):
        ab, ac, ax, az = (r[...].astype(F32) for r in (ab_ref, ac_ref, ax_ref, az_ref))
        u = ac * ax
        conv = cb_ref[...] + cw_ref[0:1, :] * _shift_down(u, 2) + cw_ref[1:2, :] * _shift_down(u, 1) + cw_ref[2:3, :] * u
        y_ref[...] = (ab * conv * _silu(az)).astype(BF16)

    return pl.pallas_call(
        body, name="mixa_fwd", grid=(B, nc),
        in_specs=[pl.BlockSpec((S, LANE), lambda b, j, c0=c0: (b, c0 + j)) for c0 in A_SEGS]
                 + [pl.BlockSpec((CONV_K, LANE), lambda b, j: (0, j)),
                    pl.BlockSpec((1, LANE), lambda b, j: (0, j))],
        out_specs=pl.BlockSpec((S, LANE), lambda b, j: (b, j)),
        out_shape=jax.ShapeDtypeStruct((B * S, CONV_WIDTH), BF16),
        compiler_params=_cp(),
    )(proj, proj, proj, proj, cw, cb)


def _mixa_bwd(dproj, dy, proj, cw, cb, B, S):
    nc = CONV_WIDTH // LANE

    def body(dpin_ref, dy_ref, ab_ref, ac_ref, ax_ref, az_ref, cw_ref, cb_ref, dp_ref, st_ref, stage, sems):
        del dpin_ref
        j, b = pl.program_id(0), pl.program_id(1)
        ab, ac, ax, az = (r[...].astype(F32) for r in (ab_ref, ac_ref, ax_ref, az_ref))
        u = ac * ax
        u1, u2 = _shift_down(u, 1), _shift_down(u, 2)
        w0, w1, w2 = cw_ref[0:1, :], cw_ref[1:2, :], cw_ref[2:3, :]
        conv = cb_ref[...] + w0 * u2 + w1 * u1 + w2 * u
        s = _silu(az)
        d = dy_ref[...]
        dconv = d * ab * s
        du = w2 * dconv + w1 * _shift_up(dconv, 1) + w0 * _shift_up(dconv, 2)
        grads = (d * conv * s, du * ax, du * ac, d * ab * conv * _dsilu(az))

        def fill(slot):
            for k, v in enumerate(grads):
                stage[slot, k] = v.astype(BF16)

        def copies_of(step):
            sj, sb = step // B, step % B
            return _put_copies([stage.at[:, k] for k in range(4)], dp_ref, sems, step % 2,
                               pl.ds(pl.multiple_of(sb * S, S), S),
                               [pl.multiple_of((c0 + sj) * LANE, LANE) for c0 in A_SEGS])

        _put_pipeline(j * B + b, nc * B, copies_of, fill)
        row = lax.broadcasted_iota(jnp.int32, (8, LANE), 0)
        st = jnp.zeros((8, LANE), F32)
        for r, v in enumerate((dconv * u2, dconv * u1, dconv * u, dconv)):
            st = st + jnp.where(row == r, jnp.sum(v, axis=0, keepdims=True), 0.0)

        @pl.when(pl.program_id(1) == 0)
        def _():
            st_ref[...] = st

        @pl.when(pl.program_id(1) != 0)
        def _():
            st_ref[...] += st

    return pl.pallas_call(
        body, name="mixa_bwd", grid=(nc, B),
        in_specs=[pl.BlockSpec(memory_space=pl.ANY),
                  pl.BlockSpec((S, LANE), lambda j, b: (b, j))]
                 + [pl.BlockSpec((S, LANE), lambda j, b, c0=c0: (b, c0 + j)) for c0 in A_SEGS]
                 + [pl.BlockSpec((CONV_K, LANE), lambda j, b: (0, j)),
                    pl.BlockSpec((1, LANE), lambda j, b: (0, j))],
        out_specs=[pl.BlockSpec(memory_space=pl.ANY),
                   pl.BlockSpec((8, LANE), lambda j, b: (0, j))],
        out_shape=[jax.ShapeDtypeStruct(dproj.shape, BF16), jax.ShapeDtypeStruct((8, CONV_WIDTH), F32)],
        scratch_shapes=[pltpu.VMEM((2, 4, S, LANE), BF16), pltpu.SemaphoreType.DMA((2, 4))],
        input_output_aliases={0: 0},
        compiler_params=_cp(),
    )(dproj, dy, proj, proj, proj, proj, cw, cb)


def _mla_prep_fwd(proj, gq, gkv, wuqp, wkp, wv, gmq, gmk, cos, sa, sb, S):
    T = proj.shape[0]
    ts = _tile(S, 512)
    ns = S // ts
    W = MLA_HEADS * LANE

    def body(p_ref, gq_ref, gkv_ref, wuq_ref, wk_ref, wv_ref, gmq_ref, gmk_ref, cos_ref, sa_ref, sb_ref,
             q_ref, k_ref, v_ref):
        cq = p_ref[:, 0:2 * LANE].astype(F32)
        ckv = p_ref[:, 2 * LANE:3 * LANE].astype(F32)
        kpe = pltpu.roll(p_ref[:, 3 * LANE:4 * LANE].astype(F32), HALF, axis=1)
        cqn = cq * _rstd(cq, MLA_Q_LORA) * gq_ref[...]
        ckn = (ckv * _rstd(ckv, MLA_KV_LORA) * gkv_ref[...]).astype(BF16)
        q0 = _mm(cqn, wuq_ref[...])
        kn = _mm(ckn, wk_ref[...])
        v_ref[...] = _mm(ckn, wv_ref[...]).astype(BF16)
        c, a, b = cos_ref[...], sa_ref[...], sb_ref[...]
        for h in range(MLA_HEADS):
            q0h = q0[:, h * LANE:(h + 1) * LANE]
            q_ref[h] = _rope(q0h * _rstd(q0h, MLA_QK) * gmq_ref[...], c, a, b).astype(BF16)
            k0h = kn[:, h * LANE:(h + 1) * LANE] + kpe
            k_ref[h] = _rope(k0h * _rstd(k0h, MLA_QK) * gmk_ref[...], c, a, b).astype(BF16)

    def whole(r, c):
        return pl.BlockSpec((r, c), lambda i: (0, 0))

    tab = pl.BlockSpec((ts, LANE), lambda i: (i % ns, 0))
    return pl.pallas_call(
        body, name="mla_prep_fwd", grid=(T // ts,),
        in_specs=[pl.BlockSpec((ts, 4 * LANE), lambda i: (i, CB_CQ // 4)),
                  whole(1, MLA_Q_LORA), whole(1, MLA_KV_LORA), whole(MLA_Q_LORA, W), whole(MLA_KV_LORA, W),
                  whole(MLA_KV_LORA, MLA_HEADS * MLA_V), whole(1, LANE), whole(1, LANE), tab, tab, tab],
        out_specs=[pl.BlockSpec((MLA_HEADS, ts, LANE), lambda i: (0, i, 0)),
                   pl.BlockSpec((MLA_HEADS, ts, LANE), lambda i: (0, i, 0)),
                   pl.BlockSpec((ts, MLA_HEADS * MLA_V), lambda i: (i, 0))],
        out_shape=[jax.ShapeDtypeStruct((MLA_HEADS, T, LANE), BF16), jax.ShapeDtypeStruct((MLA_HEADS, T, LANE), BF16),
                   jax.ShapeDtypeStruct((T, MLA_HEADS * MLA_V), BF16)],
        compiler_params=_cp(),
    )(proj, gq, gkv, wuqp, wkp, wv, gmq, gmk, cos, sa, sb)


def _mla_prep_bwd(dproj, dq, dk, dv, proj, gq, gkv, wuqp, wkp, wv, gmq, gmk, cos, sa, sb, S):
    T = proj.shape[0]
    ts = _tile(S, 256)
    ns = S // ts
    W = MLA_HEADS * LANE

    def body(dpin_ref, dq_ref, dk_ref, dv_ref, p_ref, gq_ref, gkv_ref, wuq_ref, wk_ref, wv_ref, gmq_ref, gmk_ref,
             cos_ref, sa_ref, sb_ref,
             dp_ref, dwuq_ref, dwk_ref, dwv_ref, dgq_ref, dgkv_ref, dgmq_ref, dgmk_ref, dq0_ref, dkn_ref):
        del dpin_ref

        @pl.when(pl.program_id(0) == 0)
        def _():
            for r in (dwuq_ref, dwk_ref, dwv_ref, dgq_ref, dgkv_ref, dgmq_ref, dgmk_ref):
                r[...] = jnp.zeros_like(r)

        cq = p_ref[:, 0:2 * LANE].astype(F32)
        ckv = p_ref[:, 2 * LANE:3 * LANE].astype(F32)
        kpe = pltpu.roll(p_ref[:, 3 * LANE:4 * LANE].astype(F32), HALF, axis=1)
        rq = _rstd(cq, MLA_Q_LORA)
        rkv = _rstd(ckv, MLA_KV_LORA)
        gq, gkv, gmq, gmk = gq_ref[...], gkv_ref[...], gmq_ref[...], gmk_ref[...]
        cqn = (cq * rq * gq).astype(BF16)
        ckn = (ckv * rkv * gkv).astype(BF16)
        q0 = _mm(cqn, wuq_ref[...])
        kn = _mm(ckn, wk_ref[...])
        c, a, b = cos_ref[...], sa_ref[...], sb_ref[...]
        lane = lax.broadcasted_iota(jnp.int32, (ts, LANE), 1)
        dgmq = jnp.zeros((1, LANE), F32)
        dgmk = jnp.zeros((1, LANE), F32)
        dkpe = jnp.zeros((ts, LANE), F32)
        for h in range(MLA_HEADS):
            q0h = q0[:, h * LANE:(h + 1) * LANE]
            r = _rstd(q0h, MLA_QK)
            d1 = _rope_t(dq_ref[h], c, a, b)
            gy = d1 * gmq
            dq0_ref[:, h * LANE:(h + 1) * LANE] = (
                r * gy - q0h * (r * r * r) * (jnp.sum(q0h * gy, axis=-1, keepdims=True) * (1.0 / MLA_QK))).astype(BF16)
            dgmq = dgmq + jnp.sum(d1 * q0h * r, axis=0, keepdims=True)
            k0h = kn[:, h * LANE:(h + 1) * LANE] + kpe
            r = _rstd(k0h, MLA_QK)
            d1 = _rope_t(dk_ref[h], c, a, b)
            gy = d1 * gmk
            dk0 = r * gy - k0h * (r * r * r) * (jnp.sum(k0h * gy, axis=-1, keepdims=True) * (1.0 / MLA_QK))
            dgmk = dgmk + jnp.sum(d1 * k0h * r, axis=0, keepdims=True)
            dkn_ref[:, h * LANE:(h + 1) * LANE] = jnp.where(lane < MLA_NOPE, dk0, 0.0).astype(BF16)
            dkpe = dkpe + jnp.where((lane >= MLA_NOPE) & (lane < MLA_QK), dk0, 0.0)
        dq0 = dq0_ref[...]
        dkn = dkn_ref[...]
        dvv = dv_ref[...]
        dwuq_ref[...] += _mm_tn(cqn, dq0)
        dwk_ref[...] += _mm_tn(ckn, dkn)
        dwv_ref[...] += _mm_tn(ckn, dvv)
        dgmq_ref[...] += dgmq
        dgmk_ref[...] += dgmk
        dcqn = _mm_nt(dq0, wuq_ref[...])
        gy = dcqn * gq
        dp_ref[:, 0:2 * LANE] = (
            rq * gy - cq * (rq * rq * rq) * (jnp.sum(cq * gy, axis=-1, keepdims=True) * (1.0 / MLA_Q_LORA))).astype(BF16)
        dgq_ref[...] += jnp.sum(dcqn * cq * rq, axis=0, keepdims=True)
        dckn = _mm_nt(dkn, wk_ref[...]) + _mm_nt(dvv, wv_ref[...])
        gy = dckn * gkv
        dp_ref[:, 2 * LANE:3 * LANE] = (
            rkv * gy - ckv * (rkv * rkv * rkv) * (jnp.sum(ckv * gy, axis=-1, keepdims=True) * (1.0 / MLA_KV_LORA))).astype(BF16)
        dgkv_ref[...] += jnp.sum(dckn * ckv * rkv, axis=0, keepdims=True)
        dp_ref[:, 3 * LANE:4 * LANE] = pltpu.roll(dkpe, HALF, axis=1).astype(BF16)

    def whole(r, c):
        return pl.BlockSpec((r, c), lambda i: (0, 0))

    tab = pl.BlockSpec((ts, LANE), lambda i: (i % ns, 0))
    heads = pl.BlockSpec((MLA_HEADS, ts, LANE), lambda i: (0, i, 0))
    return pl.pallas_call(
        body, name="mla_prep_bwd", grid=(T // ts,),
        in_specs=[pl.BlockSpec(memory_space=pl.ANY), heads, heads,
                  pl.BlockSpec((ts, MLA_HEADS * MLA_V), lambda i: (i, 0)),
                  pl.BlockSpec((ts, 4 * LANE), lambda i: (i, CB_CQ // 4)),
                  whole(1, MLA_Q_LORA), whole(1, MLA_KV_LORA), whole(MLA_Q_LORA, W), whole(MLA_KV_LORA, W),
                  whole(MLA_KV_LORA, MLA_HEADS * MLA_V), whole(1, LANE), whole(1, LANE), tab, tab, tab],
        out_specs=[pl.BlockSpec((ts, 4 * LANE), lambda i: (i, CB_CQ // 4)),
                   whole(MLA_Q_LORA, W), whole(MLA_KV_LORA, W), whole(MLA_KV_LORA, MLA_HEADS * MLA_V),
                   whole(1, MLA_Q_LORA), whole(1, MLA_KV_LORA), whole(1, LANE), whole(1, LANE)],
        out_shape=[jax.ShapeDtypeStruct(dproj.shape, BF16),
                   jax.ShapeDtypeStruct((MLA_Q_LORA, W), F32), jax.ShapeDtypeStruct((MLA_KV_LORA, W), F32),
                   jax.ShapeDtypeStruct((MLA_KV_LORA, MLA_HEADS * MLA_V), F32),
                   jax.ShapeDtypeStruct((1, MLA_Q_LORA), F32), jax.ShapeDtypeStruct((1, MLA_KV_LORA), F32),
                   jax.ShapeDtypeStruct((1, LANE), F32), jax.ShapeDtypeStruct((1, LANE), F32)],
        scratch_shapes=[pltpu.VMEM((ts, W), BF16), pltpu.VMEM((ts, W), BF16)],
        input_output_aliases={0: 0},
        compiler_params=_cp(),
    )(dproj, dq, dk, dv, proj, gq, gkv, wuqp, wkp, wv, gmq, gmk, cos, sa, sb)


def _dil_prep_fwd(proj, gq, gk):
    T = proj.shape[0]
    ts = _tile(T, 512)

    def body(pq_ref, pk_ref, pv_ref, gq_ref, gk_ref, q_ref, k_ref, v_ref):
        v_ref[...] = pv_ref[...].astype(F32)
        for c in range(NPAIR):
            cs = slice(c * LANE, (c + 1) * LANE)
            t = jnp.concatenate([pq_ref[:, cs], pk_ref[:, cs]], axis=1).astype(F32)
            y = t * lax.rsqrt(_head_bcast_sum(t * t, terms=2) * (1.0 / DIL_HEAD_DIM) + EPS)
            q_ref[:, cs] = y[:, 0:LANE] * gq_ref[:, cs]
            k_ref[:, cs] = y[:, LANE:2 * LANE] * gk_ref[:, cs]

    col = pl.BlockSpec((1, DIL_WIDTH), lambda i, g: (0, g))
    out = pl.BlockSpec((ts, DIL_WIDTH), lambda i, g: (i, g))
    seg = lambda c0: pl.BlockSpec((ts, DIL_WIDTH), lambda i, g: (i, c0 // NPAIR + g))
    return pl.pallas_call(
        body, name="dil_prep_fwd", grid=(T // ts, DIL_GROUPS),
        in_specs=[seg(CB_DQ), seg(CB_DK), seg(CB_DV), col, col],
        out_specs=[out, out, out],
        out_shape=[jax.ShapeDtypeStruct((T, DIL_QK), F32)] * 3,
        compiler_params=_cp(),
    )(proj, proj, proj, gq, gk)


def _dil_prep_bwd(dproj, ddq, ddk, ddv, proj, gq, gk):
    T = proj.shape[0]
    ts = _tile(T, 512)
    nt = T // ts

    def body(dpin_ref, ddq_ref, ddk_ref, ddv_ref, pq_ref, pk_ref, gq_ref, gk_ref, dp_ref, dgq_ref, dgk_ref,
             stage, sems):
        del dpin_ref
        g, i = pl.program_id(0), pl.program_id(1)

        @pl.when(i == 0)
        def _():
            dgq_ref[...] = jnp.zeros_like(dgq_ref)
            dgk_ref[...] = jnp.zeros_like(dgk_ref)

        def fill(slot):
            stage[slot, 2] = ddv_ref[...].astype(BF16)
            for c in range(NPAIR):
                cs = slice(c * LANE, (c + 1) * LANE)
                t = jnp.concatenate([pq_ref[:, cs], pk_ref[:, cs]], axis=1).astype(F32)
                d = jnp.concatenate([ddq_ref[:, cs], ddk_ref[:, cs]], axis=1)
                gy = d * jnp.concatenate([gq_ref[:, cs], gk_ref[:, cs]], axis=1)
                r = lax.rsqrt(_head_bcast_sum(t * t, terms=2) * (1.0 / DIL_HEAD_DIM) + EPS)
                dot = _head_bcast_sum(t * gy, terms=2) * (1.0 / DIL_HEAD_DIM)
                dx = (r * gy - t * (r * r * r) * dot).astype(BF16)
                stage[slot, 0, :, cs] = dx[:, 0:LANE]
                stage[slot, 1, :, cs] = dx[:, LANE:2 * LANE]
                part = jnp.sum(d * t * r, axis=0, keepdims=True)
                dgq_ref[:, cs] += part[:, 0:LANE]
                dgk_ref[:, cs] += part[:, LANE:2 * LANE]

        def copies_of(step):
            sg, si = step // nt, step % nt
            return _put_copies([stage.at[:, k] for k in range(3)], dp_ref, sems, step % 2,
                               pl.ds(pl.multiple_of(si * ts, ts), ts),
                               [pl.multiple_of((c0 + NPAIR * sg) * LANE, LANE) for c0 in (CB_DQ, CB_DK, CB_DV)])

        _put_pipeline(g * nt + i, DIL_GROUPS * nt, copies_of, fill)

    col = pl.BlockSpec((1, DIL_WIDTH), lambda g, i: (0, g))
    tok = pl.BlockSpec((ts, DIL_WIDTH), lambda g, i: (i, g))
    seg = lambda c0: pl.BlockSpec((ts, DIL_WIDTH), lambda g, i: (i, c0 // NPAIR + g))
    return pl.pallas_call(
        body, name="dil_prep_bwd", grid=(DIL_GROUPS, nt),
        in_specs=[pl.BlockSpec(memory_space=pl.ANY), tok, tok, tok, seg(CB_DQ), seg(CB_DK), col, col],
        out_specs=[pl.BlockSpec(memory_space=pl.ANY), col, col],
        out_shape=[jax.ShapeDtypeStruct(dproj.shape, BF16), jax.ShapeDtypeStruct((1, DIL_QK), F32),
                   jax.ShapeDtypeStruct((1, DIL_QK), F32)],
        scratch_shapes=[pltpu.VMEM((2, 3, ts, DIL_WIDTH), BF16), pltpu.SemaphoreType.DMA((2, 3))],
        input_output_aliases={0: 0},
        compiler_params=_cp(),
    )(dproj, ddq, ddk, ddv, proj, proj, gq, gk)


COPY_ROWS = 256


def _to_classes(src_ref, dst_ref, d, L, scale=None):
    n = min(L, COPY_ROWS)
    for r in range(d):
        for c0 in range(0, L, n):
            rows = pl.ds(r + c0 * d, n, stride=d) if d > 1 else pl.ds(c0, n)
            val = src_ref[rows, :]
            if scale is not None:
                val = val * scale
            dst_ref[r * L + c0:r * L + c0 + n, :] = val.astype(dst_ref.dtype)


def _from_classes(src_ref, dst_ref, d, L):
    n = min(L, COPY_ROWS)
    for r in range(d):
        for c0 in range(0, L, n):
            rows = pl.ds(r + c0 * d, n, stride=d) if d > 1 else pl.ds(c0, n)
            dst_ref[rows, :] = src_ref[r * L + c0:r * L + c0 + n, :].astype(dst_ref.dtype)


MLA_TQ, MLA_TK = 512, 512


def _causal_bias(tq, tk, shift):
    row = lax.broadcasted_iota(jnp.int32, (tq, tk), 0)
    col = lax.broadcasted_iota(jnp.int32, (tq, tk), 1)
    return jnp.where(row >= col + shift, 0.0, NEG)


def _mla_specs(S):
    heads = pl.BlockSpec((2, S, LANE), lambda b, j: (j, b, 0))
    pair = pl.BlockSpec((S, LANE), lambda b, j: (b, j))
    return heads, pair


def _mla_attn_fwd(q, k, v, B, S):
    tq = _tile(S, MLA_TQ)
    tk = _tile(tq, MLA_TK)
    nd = tq // tk
    scale = MLA_QK ** -0.5
    heads, pair = _mla_specs(S)

    def body(q_ref, k_ref, v_ref, o_ref, lse_ref):
        lo, lok = _lane_lo((tq, LANE)), _lane_lo((tk, LANE))
        diag = [_causal_bias(tq, tk, i * tk) for i in range(nd)]

        def block(g, _):
            row0 = pl.multiple_of(g * tq, tq)
            rows = pl.ds(row0, tq)
            qs = [q_ref[hh, rows, :] for hh in range(2)]

            one = jnp.ones((), BF16)

            def step(off, carries, bias):
                off = pl.multiple_of(off, tk)
                vt = v_ref[pl.ds(off, tk), :]
                vh = (jnp.where(lok, vt, one), jnp.where(lok, one, vt))
                out = []
                for hh, (m, acc) in enumerate(carries):
                    s = _mm_nt(qs[hh], k_ref[hh, pl.ds(off, tk), :]) * scale
                    if bias is not None:
                        s = s + bias
                    m_new = jnp.maximum(m, jnp.max(s, axis=-1, keepdims=True))
                    p = jnp.exp(s - m_new)
                    out.append((m_new, jnp.exp(m - m_new) * acc + _mm(p, vh[hh])))
                return tuple(out)

            init = (jnp.full((tq, 1), NEG, F32), jnp.zeros((tq, LANE), F32))
            carries = lax.fori_loop(0, g * nd, lambda i, c: step(i * tk, c, None), (init, init))
            for i in range(nd):
                carries = step(row0 + i * tk, carries, diag[i])
            (ma, acca), (mb, accb) = carries
            la, lb = pltpu.roll(acca, HALF, axis=1), pltpu.roll(accb, HALF, axis=1)
            o_ref[rows, :] = jnp.where(lo, acca / la, accb / lb)
            lse_ref[rows, :] = jnp.where(lo, ma + jnp.log(la), mb + jnp.log(lb))
            return 0

        lax.fori_loop(0, S // tq, block, 0)

    return pl.pallas_call(
        body, name="mla_attn_fwd", grid=(B, NPAIR), in_specs=[heads, heads, pair], out_specs=[pair, pair],
        out_shape=[jax.ShapeDtypeStruct((B * S, MLA_HEADS * MLA_V), F32)] * 2,
        compiler_params=_cp(),
    )(q, k, v)


DIL_UNROLL = 8


def _dil_geometry(gi, S):
    span, d = DIL_PATTERNS[gi]
    L = S // d
    t = _tile(L, 128)
    window = span // d
    back = min(-(-window // t) * t, L - t)
    return d, L, t, window, back


def _dil_specs(gi, S):
    qk = pl.BlockSpec((S, LANE), lambda b, j: (b, NPAIR * gi + j))
    pair = pl.BlockSpec((S, LANE), lambda b, j: (b, j))
    return qk, qk, pair


def _dil_bias(bias_ref, sl_ref, j, t, kw, back, window):
    row = lax.broadcasted_iota(jnp.int32, (2 * t, kw), 0)
    col = lax.broadcasted_iota(jnp.int32, (2 * t, kw), 1)
    second = row >= t
    slope = jnp.where(second, sl_ref[j, 1], sl_ref[j, 0])
    for n in range(bias_ref.shape[0]):
        dist = jnp.where(second, row - t, row) + n * back - col
        bias_ref[n] = jnp.where((dist >= 0) & (dist <= window), -slope * dist.astype(F32), NEG)


def _stack_heads(x, lo):
    zero = jnp.zeros((), x.dtype)
    return jnp.concatenate([jnp.where(lo, x, zero), jnp.where(lo, zero, x)], axis=0)


def _dil_attn_fwd(gi, slopes, qn, kn, proj, B, S):
    d, L, t, window, back = _dil_geometry(gi, S)
    kw, nq = back + t, L // t
    nbias = 2 if back else 1
    qk, vspec, pair = _dil_specs(gi, S)

    def body(sl_ref, q_ref, k_ref, v_ref, o_ref, lse_ref, qs, ks, vs, os_, ls, bias_ref):
        _to_classes(q_ref, qs, d, L, DIL_HEAD_DIM ** -0.5)
        _to_classes(k_ref, ks, d, L)
        _to_classes(v_ref, vs, d, L)
        _dil_bias(bias_ref, sl_ref, pl.program_id(1), t, kw, back, window)
        lo = _lane_lo((t, LANE))

        def block(g, _):
            qb = g % nq if d > 1 else g
            row0 = pl.multiple_of(g * t, t)
            rows = pl.ds(row0, t)
            early = qb * t < back
            keys = pl.ds(pl.multiple_of(jnp.where(early, row0 - qb * t, row0 - back), t), kw)
            s = _mm_nt(_stack_heads(qs[rows, :], lo), ks[keys, :]) + bias_ref[jnp.where(early, 0, nbias - 1)]
            m = jnp.max(s, axis=-1, keepdims=True)
            p = jnp.exp(s - m)
            l = jnp.sum(p, axis=-1, keepdims=True)
            o2 = _mm(p, vs[keys, :]) / l
            lse2 = m + jnp.log(l)
            os_[rows, :] = jnp.where(lo, o2[:t], o2[t:])
            ls[rows, :] = jnp.where(lo, lse2[:t], lse2[t:])
            return 0

        lax.fori_loop(0, d * nq, block, 0, unroll=DIL_UNROLL if d * nq % DIL_UNROLL == 0 else 1)
        _from_classes(os_, o_ref, d, L)
        _from_classes(ls, lse_ref, d, L)

    return pl.pallas_call(
        body, name=f"dil_attn_fwd_{gi}", grid=(B, NPAIR),
        in_specs=[pl.BlockSpec(memory_space=pltpu.SMEM), qk, qk, vspec], out_specs=[pair, pair],
        out_shape=[jax.ShapeDtypeStruct((B * S, DIL_WIDTH), F32)] * 2,
        scratch_shapes=[pltpu.VMEM((S, LANE), BF16)] * 3 + [pltpu.VMEM((S, LANE), F32)] * 2
                       + [pltpu.VMEM((nbias, 2 * t, kw), F32)],
        compiler_params=_cp(),
    )(slopes, qn, kn, proj)


def _mla_attn_bwd(q, k, v, do, lse, delta, B, S):
    T = B * S
    tq = _tile(S, MLA_TQ)
    tk = _tile(tq, MLA_TK)
    nd = tq // tk
    scale = MLA_QK ** -0.5
    heads, pair = _mla_specs(S)

    def body(q_ref, k_ref, v_ref, do_ref, lse_ref, dl_ref, dq_ref, dk_ref, dv_ref):
        dk_ref[...] = jnp.zeros_like(dk_ref)
        dv_ref[...] = jnp.zeros_like(dv_ref)
        lo = _lane_lo((tq, LANE))
        diag = [_causal_bias(tq, tk, i * tk) for i in range(nd)]

        def block(g, _):
            row0 = pl.multiple_of(g * tq, tq)
            rows = pl.ds(row0, tq)
            for hh in range(2):
                sel = lo if hh == 0 else jnp.logical_not(lo)
                qh = q_ref[hh, rows, :]
                doh = jnp.where(sel, do_ref[rows, :], jnp.zeros((), BF16))
                lse_h = jnp.max(jnp.where(sel, lse_ref[rows, :], NEG), axis=-1, keepdims=True)
                dl_h = jnp.max(jnp.where(sel, dl_ref[rows, :], NEG), axis=-1, keepdims=True)

                def step(off, dq_acc, bias, hh=hh, qh=qh, doh=doh, lse_h=lse_h, dl_h=dl_h):
                    cols = pl.ds(pl.multiple_of(off, tk), tk)
                    kh = k_ref[hh, cols, :]
                    s = _mm_nt(qh, kh) * scale
                    if bias is not None:
                        s = s + bias
                    p = jnp.exp(s - lse_h)
                    dp = _mm_nt(doh, v_ref[cols, :])
                    ds = (p * (dp - dl_h)).astype(BF16)
                    dk_ref[hh, cols, :] += _mm_tn(ds, qh) * scale
                    dv_ref[cols, :] += _mm_tn(p, doh)
                    return dq_acc + _mm(ds, kh)

                dq_acc = lax.fori_loop(0, g * nd, lambda i, a: step(i * tk, a, None), jnp.zeros((tq, LANE), F32))
                for i in range(nd):
                    dq_acc = step(row0 + i * tk, dq_acc, diag[i])
                dq_ref[hh, rows, :] = dq_acc * scale
            return 0

        lax.fori_loop(0, S // tq, block, 0)

    return pl.pallas_call(
        body, name="mla_attn_bwd", grid=(B, NPAIR), in_specs=[heads, heads, pair, pair, pair, pair],
        out_specs=[heads, heads, pair],
        out_shape=[jax.ShapeDtypeStruct((MLA_HEADS, T, LANE), F32), jax.ShapeDtypeStruct((MLA_HEADS, T, LANE), F32),
                   jax.ShapeDtypeStruct((T, MLA_HEADS * MLA_V), F32)],
        compiler_params=_cp(),
    )(q, k, v, do, lse, delta)


def _dil_attn_bwd(gi, slopes, qn, kn, proj, do, lse, delta, through, B, S):
    d, L, t, window, back = _dil_geometry(gi, S)
    kw, nq = back + t, L // t
    nbias = 2 if back else 1
    scale = DIL_HEAD_DIM ** -0.5
    qk, vspec, pair = _dil_specs(gi, S)

    def body(*refs):
        refs = list(refs)
        sl_ref, q_ref, k_ref, v_ref, do_ref, lse_ref, dl_ref = refs[:7]
        dq_ref, dk_ref, dv_ref, qs, ks, vs, dos, lss, dls, dqs, dks, dvs, bias_ref = refs[-13:]
        _to_classes(q_ref, qs, d, L, scale)
        for src, dst in ((k_ref, ks), (v_ref, vs), (do_ref, dos), (lse_ref, lss), (dl_ref, dls)):
            _to_classes(src, dst, d, L)
        _dil_bias(bias_ref, sl_ref, pl.program_id(1), t, kw, back, window)
        dks[...] = jnp.zeros_like(dks)
        dvs[...] = jnp.zeros_like(dvs)
        lo = _lane_lo((t, LANE))

        def stats(ref, rows):
            x = ref[rows, :]
            return jnp.concatenate([jnp.max(jnp.where(lo, x, NEG), axis=-1, keepdims=True),
                                    jnp.max(jnp.where(lo, NEG, x), axis=-1, keepdims=True)], axis=0)

        def block(g, _):
            qb = g % nq if d > 1 else g
            row0 = pl.multiple_of(g * t, t)
            rows = pl.ds(row0, t)
            early = qb * t < back
            keys = pl.ds(pl.multiple_of(jnp.where(early, row0 - qb * t, row0 - back), t), kw)
            q2 = _stack_heads(qs[rows, :], lo)
            do2 = _stack_heads(dos[rows, :], lo)
            kt = ks[keys, :]
            s = _mm_nt(q2, kt) + bias_ref[jnp.where(early, 0, nbias - 1)]
            p = jnp.exp(s - stats(lss, rows))
            ds = (p * (_mm_nt(do2, vs[keys, :]) - stats(dls, rows))).astype(BF16)
            dq2 = _mm(ds, kt) * scale
            dqs[rows, :] = jnp.where(lo, dq2[:t], dq2[t:])
            dks[keys, :] += _mm_tn(ds, q2)
            dvs[keys, :] += _mm_tn(p, do2)
            return 0

        lax.fori_loop(0, d * nq, block, 0, unroll=DIL_UNROLL if d * nq % DIL_UNROLL == 0 else 1)
        for src, dst in ((dqs, dq_ref), (dks, dk_ref), (dvs, dv_ref)):
            _from_classes(src, dst, d, L)

    in_specs = [pl.BlockSpec(memory_space=pltpu.SMEM), qk, qk, vspec, pair, pair, pair]
    args = [slopes, qn, kn, proj, do, lse, delta]
    aliases = {}
    if through is not None:
        aliases = {len(args) + i: i for i in range(3)}
        in_specs = in_specs + [pl.BlockSpec(memory_space=pl.ANY)] * 3
        args = args + list(through)
    return pl.pallas_call(
        body, name=f"dil_attn_bwd_{gi}", grid=(B, NPAIR), in_specs=in_specs, out_specs=[qk, qk, qk],
        out_shape=[jax.ShapeDtypeStruct((B * S, DIL_QK), F32)] * 3,
        scratch_shapes=[pltpu.VMEM((S, LANE), BF16)] * 4 + [pltpu.VMEM((S, LANE), F32)] * 5
                       + [pltpu.VMEM((nbias, 2 * t, kw), F32)],
        input_output_aliases=aliases,
        compiler_params=_cp(),
    )(*args)


def _merge_proj_specs(ts):
    wide = lambda c0, w: pl.BlockSpec((ts, w), lambda i: (i, c0 * LANE // w))
    return [wide(CB_BZ, DIL_WIDTH), wide(CB_CZ, DIL_WIDTH)] + [wide(CB_GATE + 8 * i, D_MODEL) for i in range(3)]


def _merge_common(p_refs, bg_ref, ob_ref, og_refs, lse_refs):
    bz = p_refs[0][...].astype(F32)
    cz = p_refs[1][...].astype(F32)
    gates = [_sigmoid(p_refs[2 + i][...].astype(F32) + bg_ref[:, i * D_MODEL:(i + 1) * D_MODEL]) for i in range(3)]
    ob = ob_ref[...]
    lses = [r[...] for r in lse_refs]
    mx = jnp.maximum(jnp.maximum(lses[0], lses[1]), lses[2])
    es = [jnp.exp(v - mx) for v in lses]
    inv = 1.0 / (es[0] + es[1] + es[2])
    alphas = [e * inv for e in es]
    oc = alphas[0] * og_refs[0][...] + alphas[1] * og_refs[1][...] + alphas[2] * og_refs[2][...]
    return bz, cz, gates, ob, alphas, oc


def _merge_fwd(x, proj, b_gate, ya, ob, ogs, lses, woa, wob, woc, wo):
    T = x.shape[0]
    ts = _tile(T, 256)

    def body(x_ref, p0, p1, p2, p3, p4, bg_ref, ya_ref, ob_ref, og0, og1, og2, l0, l1, l2,
             woa_ref, wob_ref, woc_ref, wo_ref, out_ref):
        bz, cz, gates, obv, alphas, oc = _merge_common((p0, p1, p2, p3, p4), bg_ref, ob_ref, (og0, og1, og2),
                                                       (l0, l1, l2))
        yb = obv * _silu(bz)
        yc = oc * _silu(cz)
        merged = (gates[0] * _mm(ya_ref[...], woa_ref[...]) + gates[1] * _mm(yb, wob_ref[...])
                  + gates[2] * _mm(yc, woc_ref[...]))
        out_ref[...] = x_ref[...] + _mm(merged, wo_ref[...])

    def whole(r, c):
        return pl.BlockSpec((r, c), lambda i: (0, 0))

    tok = lambda w: pl.BlockSpec((ts, w), lambda i: (i, 0))
    return pl.pallas_call(
        body, name="merge_fwd", grid=(T // ts,),
        in_specs=[tok(D_MODEL)] + _merge_proj_specs(ts) + [whole(1, 3 * D_MODEL), tok(CONV_WIDTH)]
                 + [tok(DIL_WIDTH)] * 7 + [whole(CONV_WIDTH, D_MODEL)] * 3 + [whole(D_MODEL, D_MODEL)],
        out_specs=tok(D_MODEL),
        out_shape=jax.ShapeDtypeStruct((T, D_MODEL), F32),
        compiler_params=_cp(),
    )(x, *[proj] * 5, b_gate, ya, ob, *ogs, *lses, woa, wob, woc, wo)


def _merge_bwd(dout, proj, b_gate, ya, ob, ogs, lses, woa, wob, woc, wo):
    T = dout.shape[0]
    ts = _tile(T, 256)
    nt = T // ts

    def body(do_ref, p0, p1, p2, p3, p4, bg_ref, ya_ref, ob_ref, og0, og1, og2, l0, l1, l2,
             woa_ref, wob_ref, woc_ref, wo_ref,
             dp_ref, dya_ref, dob_ref, dlb_ref, dg0, dg1, dg2, dl0, dl1, dl2,
             mg_ref, dpa_ref, dpb_ref, dpc_ref, yb_ref, yc_ref, dbg_ref, st_bz, st_cz, st_gate, sems):
        step = pl.program_id(0)
        slot = step % 2

        def copies_of(s):
            return _put_copies([st_bz, st_cz, st_gate], dp_ref, sems, s % 2, pl.ds(pl.multiple_of(s * ts, ts), ts),
                               [CB_BZ * LANE, CB_CZ * LANE, CB_GATE * LANE])

        @pl.when(step >= 2)
        def _():
            for cp in copies_of(step - 2):
                cp.wait()

        bz, cz, gates, obv, alphas, oc = _merge_common((p0, p1, p2, p3, p4), bg_ref, ob_ref, (og0, og1, og2),
                                                       (l0, l1, l2))
        sb, sc = _silu(bz), _silu(cz)
        yb = obv * sb
        yc = oc * sc
        ps = [_mm(ya_ref[...], woa_ref[...]), _mm(yb, wob_ref[...]), _mm(yc, woc_ref[...])]
        mg_ref[...] = (gates[0] * ps[0] + gates[1] * ps[1] + gates[2] * ps[2]).astype(BF16)
        yb_ref[...] = yb.astype(BF16)
        yc_ref[...] = yc.astype(BF16)
        dm = _mm_nt(do_ref[...], wo_ref[...])
        dps = []
        first = pl.program_id(0) == 0
        for i, dref in enumerate((dpa_ref, dpb_ref, dpc_ref)):
            g = gates[i]
            dpi = (dm * g).astype(BF16)
            dref[...] = dpi
            dps.append(dpi)
            dgp = dm * ps[i] * g * (1.0 - g)
            st_gate[slot, :, i * D_MODEL:(i + 1) * D_MODEL] = dgp.astype(BF16)
            part = jnp.sum(dgp, axis=0, keepdims=True)

            @pl.when(first)
            def _():
                dbg_ref[:, i * D_MODEL:(i + 1) * D_MODEL] = part

            @pl.when(jnp.logical_not(first))
            def _():
                dbg_ref[:, i * D_MODEL:(i + 1) * D_MODEL] += part

        dya_ref[...] = _mm_nt(dps[0], woa_ref[...])
        dyb = _mm_nt(dps[1], wob_ref[...])
        dyc = _mm_nt(dps[2], woc_ref[...])
        st_bz[slot] = (dyb * obv * _dsilu(bz)).astype(BF16)
        st_cz[slot] = (dyc * oc * _dsilu(cz)).astype(BF16)
        for cp in copies_of(step):
            cp.start()
        dob = dyb * sb
        doc = dyc * sc
        dob_ref[...] = dob.astype(BF16)
        for c in range(NPAIR):
            cs = slice(c * LANE, (c + 1) * LANE)
            dlb_ref[:, cs] = _head_bcast_sum(dob[:, cs] * obv[:, cs])
            dd = _head_bcast_sum(doc[:, cs] * oc[:, cs])
            for a, dref, lref in zip(alphas, (dg0, dg1, dg2), (dl0, dl1, dl2)):
                dref[:, cs] = a[:, cs] * doc[:, cs]
                lref[:, cs] = a[:, cs] * dd

        @pl.when(step == nt - 1)
        def _():
            if nt >= 2:
                for cp in copies_of(step - 1):
                    cp.wait()
            for cp in copies_of(step):
                cp.wait()

    def whole(r, c):
        return pl.BlockSpec((r, c), lambda i: (0, 0))

    tok = lambda w: pl.BlockSpec((ts, w), lambda i: (i, 0))
    sd = jax.ShapeDtypeStruct
    W = DIL_WIDTH
    return pl.pallas_call(
        body, name="merge_bwd", grid=(nt,),
        in_specs=[tok(D_MODEL)] + _merge_proj_specs(ts) + [whole(1, 3 * D_MODEL), tok(CONV_WIDTH)] + [tok(W)] * 7
                 + [whole(CONV_WIDTH, D_MODEL)] * 3 + [whole(D_MODEL, D_MODEL)],
        out_specs=[pl.BlockSpec(memory_space=pl.ANY), tok(CONV_WIDTH), tok(W), tok(W)] + [tok(W)] * 6
                  + [tok(D_MODEL)] * 4 + [tok(W), tok(W), whole(1, 3 * D_MODEL)],
        out_shape=[sd((T, PP), BF16), sd((T, CONV_WIDTH), F32), sd((T, W), BF16), sd((T, W), F32)]
                  + [sd((T, W), F32)] * 6
                  + [sd((T, D_MODEL), BF16)] * 4 + [sd((T, W), BF16)] * 2 + [sd((1, 3 * D_MODEL), F32)],
        scratch_shapes=[pltpu.VMEM((2, ts, W), BF16), pltpu.VMEM((2, ts, W), BF16),
                        pltpu.VMEM((2, ts, 3 * D_MODEL), BF16), pltpu.SemaphoreType.DMA((2, 3))],
        compiler_params=_cp(),
    )(dout, *[proj] * 5, b_gate, ya, ob, *ogs, *lses, woa, wob, woc, wo)


def _loss_head(y, target):
    T = y.shape[0]
    ts = _tile(T, 512)

    def body(y_ref, t_ref, d_ref, l_ref):
        e = y_ref[...] - t_ref[...]
        d_ref[...] = e * (1.0 / D_MODEL)
        l_ref[...] = jnp.zeros((1, 8, LANE), F32) + jnp.sum(e * e)

    tok = pl.BlockSpec((ts, D_MODEL), lambda i: (i, 0))
    return pl.pallas_call(
        body, name="loss_head", grid=(T // ts,), in_specs=[tok, tok],
        out_specs=[tok, pl.BlockSpec((1, 8, LANE), lambda i: (i, 0, 0))],
        out_shape=[jax.ShapeDtypeStruct((T, D_MODEL), F32), jax.ShapeDtypeStruct((T // ts, 8, LANE), F32)],
        compiler_params=_cp(),
    )(y, target)


def _my_index():
    return 4 * lax.axis_index("x") + 2 * lax.axis_index("y") + lax.axis_index("c")


def _peers():
    x, y, c = (lax.axis_index(a) for a in AXES)
    out = []
    for kk in range(1, N_DEV):
        px = 1 - x if kk & 4 else x
        py = 1 - y if kk & 2 else y
        pc = 1 - c if kk & 1 else c
        out.append(((px, py, pc), 4 * px + 2 * py + pc))
    return out


def _exchange(arrays, name, gather):
    n = len(arrays)

    def body(*refs):
        srcs, outs = refs[:n], refs[n:2 * n]
        send_sems, recv_sems, local_sems = refs[2 * n:]
        me = _my_index()
        peers = _peers()
        started = []
        for a, (src, out) in enumerate(zip(srcs, outs)):
            mine = pltpu.make_async_copy(src if gather else src.at[me], out.at[me], local_sems.at[a])
            mine.start()
            started.append(mine)
        sends = []
        for i, (pos, idx) in enumerate(peers):
            for a, (src, out) in enumerate(zip(srcs, outs)):
                cp = pltpu.make_async_remote_copy(
                    src_ref=src if gather else src.at[idx], dst_ref=out.at[me], send_sem=send_sems.at[a, i],
                    recv_sem=recv_sems.at[a, i], device_id=pos, device_id_type=pl.DeviceIdType.MESH)
                cp.start()
                sends.append(cp)
        for i, (pos, idx) in enumerate(peers):
            for a, (src, out) in enumerate(zip(srcs, outs)):
                pltpu.make_async_remote_copy(
                    src_ref=src if gather else src.at[idx], dst_ref=out.at[idx], send_sem=send_sems.at[a, i],
                    recv_sem=recv_sems.at[a, i], device_id=pos, device_id_type=pl.DeviceIdType.MESH).wait_recv()
        for cp in sends:
            cp.wait_send()
        for mine in started:
            mine.wait()

    any_space = pl.BlockSpec(memory_space=pl.ANY)
    return pl.pallas_call(
        body, name=name, in_specs=[any_space] * n, out_specs=[any_space] * n,
        out_shape=[jax.ShapeDtypeStruct(((N_DEV,) + a.shape) if gather else a.shape, a.dtype) for a in arrays],
        scratch_shapes=[pltpu.SemaphoreType.DMA((n, N_DEV - 1)), pltpu.SemaphoreType.DMA((n, N_DEV - 1)),
                        pltpu.SemaphoreType.DMA((n,))],
    )(*arrays)


N_CHIP = 4


def _chip_places():
    x, y, c = (lax.axis_index(a) for a in AXES)
    return (x, y, c), (x, y, 1 - c), [(1 - x, y, c), (x, 1 - y, c), (1 - x, 1 - y, c)]


def _index_of(pos):
    return 4 * pos[0] + 2 * pos[1] + pos[2]


def _gather_two_level(arrays, name):
    n = len(arrays)

    def body(*refs):
        srcs, outs = refs[:n], refs[n:2 * n]
        send_sems, recv_sems, local_sems = refs[2 * n:]
        me, sibling, others = _chip_places()

        def copy(a, k, block, to, src=None):
            slot = outs[a].at[_index_of(block)]
            return pltpu.make_async_remote_copy(
                src_ref=slot if src is None else src, dst_ref=slot, send_sem=send_sems.at[7 * a + k],
                recv_sem=recv_sems.at[7 * a + k], device_id=to, device_id_type=pl.DeviceIdType.MESH)

        started = []
        for a, src in enumerate(srcs):
            mine = pltpu.make_async_copy(src, outs[a].at[_index_of(me)], local_sems.at[a])
            mine.start()
            started.append(mine)
        sends = []
        for a, src in enumerate(srcs):
            sends.append(copy(a, 0, me, sibling, src))
            sends += [copy(a, 1 + j, me, chip, src) for j, chip in enumerate(others)]
        for cp in sends:
            cp.start()
        for j, chip in enumerate(others):
            for a in range(n):
                copy(a, 1 + j, chip, me).wait_recv()
                fwd = copy(a, 4 + j, chip, sibling)
                fwd.start()
                sends.append(fwd)
        for a in range(n):
            copy(a, 0, sibling, me).wait_recv()
            for j, chip in enumerate(others):
                copy(a, 4 + j, (chip[0], chip[1], sibling[2]), me).wait_recv()
        for cp in sends:
            cp.wait_send()
        for mine in started:
            mine.wait()

    any_space = pl.BlockSpec(memory_space=pl.ANY)
    return pl.pallas_call(
        body, name=name, in_specs=[any_space] * n, out_specs=[any_space] * n,
        out_shape=[jax.ShapeDtypeStruct((N_DEV,) + a.shape, a.dtype) for a in arrays],
        scratch_shapes=[pltpu.SemaphoreType.DMA((7 * n,)), pltpu.SemaphoreType.DMA((7 * n,)),
                        pltpu.SemaphoreType.DMA((n,))],
    )(*arrays)


def _sibling_swap(arrays, name):
    n = len(arrays)

    def body(*refs):
        srcs, outs = refs[:n], refs[n:2 * n]
        send_sems, recv_sems = refs[2 * n:]
        (x, y, c), sibling, _ = _chip_places()
        sends = []
        for a, (src, out) in enumerate(zip(srcs, outs)):
            for q in range(N_CHIP):
                def copy(core, a=a, q=q, src=src, out=out):
                    return pltpu.make_async_remote_copy(
                        src_ref=src.at[2 * q + core], dst_ref=out.at[q], send_sem=send_sems.at[N_CHIP * a + q],
                        recv_sem=recv_sems.at[N_CHIP * a + q], device_id=sibling, device_id_type=pl.DeviceIdType.MESH)
                mine = copy(1 - c)
                mine.start()
                sends.append((mine, copy(c)))
        for mine, arrival in sends:
            arrival.wait_recv()
            mine.wait_send()

    any_space = pl.BlockSpec(memory_space=pl.ANY)
    return pl.pallas_call(
        body, name=name, in_specs=[any_space] * n, out_specs=[any_space] * n,
        out_shape=[jax.ShapeDtypeStruct((N_CHIP,) + a.shape[1:], a.dtype) for a in arrays],
        scratch_shapes=[pltpu.SemaphoreType.DMA((N_CHIP * n,)), pltpu.SemaphoreType.DMA((N_CHIP * n,))],
    )(*arrays)


def _chip_pair_sum(part, got, name):
    R, C = part.shape[1:]
    tr = R
    while tr * C * part.dtype.itemsize > REDUCE_BLOCK_BYTES // 4 and tr % 32 == 0:
        tr //= 2
    c = lax.axis_index("c")

    def body(c_ref, p_ref, g_ref, o_ref):
        del c_ref
        o_ref[...] = (p_ref[...].astype(F32) + g_ref[...].astype(F32)).astype(o_ref.dtype)

    return pl.pallas_call(
        body, name=name, grid_spec=pltpu.PrefetchScalarGridSpec(
            num_scalar_prefetch=1, grid=(N_CHIP, R // tr),
            in_specs=[pl.BlockSpec((None, tr, C), lambda q, i, cr: (2 * q + cr[0], i, 0)),
                      pl.BlockSpec((None, tr, C), lambda q, i, cr: (q, i, 0))],
            out_specs=pl.BlockSpec((None, tr, C), lambda q, i, cr: (q, i, 0))),
        out_shape=jax.ShapeDtypeStruct((N_CHIP, R, C), part.dtype),
        compiler_params=_cp(),
    )(jnp.reshape(c, (1,)).astype(jnp.int32), part, got)


def _chip_exchange(arrays, name):
    n = len(arrays)

    def body(*refs):
        srcs, outs = refs[:n], refs[n:2 * n]
        send_sems, recv_sems, local_sems = refs[2 * n:]
        (x, y, c), _, others = _chip_places()
        mychip = 2 * x + y
        started, sends = [], []
        for a, (src, out) in enumerate(zip(srcs, outs)):
            mine = pltpu.make_async_copy(src.at[mychip], out.at[mychip], local_sems.at[a])
            mine.start()
            started.append(mine)
        for j, chip in enumerate(others):
            q = 2 * chip[0] + chip[1]
            for a, (src, out) in enumerate(zip(srcs, outs)):
                def copy(slot, a=a, j=j, q=q, chip=chip, src=src, out=out):
                    return pltpu.make_async_remote_copy(
                        src_ref=src.at[q], dst_ref=out.at[slot], send_sem=send_sems.at[3 * a + j],
                        recv_sem=recv_sems.at[3 * a + j], device_id=chip, device_id_type=pl.DeviceIdType.MESH)
                mine = copy(mychip)
                mine.start()
                sends.append((mine, copy(q)))
        for mine, arrival in sends:
            arrival.wait_recv()
        for mine, arrival in sends:
            mine.wait_send()
        for mine in started:
            mine.wait()

    any_space = pl.BlockSpec(memory_space=pl.ANY)
    return pl.pallas_call(
        body, name=name, in_specs=[any_space] * n, out_specs=[any_space] * n,
        out_shape=[jax.ShapeDtypeStruct(a.shape, a.dtype) for a in arrays],
        scratch_shapes=[pltpu.SemaphoreType.DMA((3 * n,)), pltpu.SemaphoreType.DMA((3 * n,)),
                        pltpu.SemaphoreType.DMA((n,))],
    )(*arrays)


def _remote_copies(srcs, lands, send_sems, recv_sems, gather):
    me = _my_index()
    out = []
    for i, (pos, idx) in enumerate(_peers()):
        for a, (src, land) in enumerate(zip(srcs, lands)):
            def copy(slot, a=a, src=src, land=land, i=i, pos=pos, idx=idx):
                return pltpu.make_async_remote_copy(
                    src_ref=src if gather else src.at[idx], dst_ref=land.at[slot],
                    send_sem=send_sems.at[a * (N_DEV - 1) + i], recv_sem=recv_sems.at[a * (N_DEV - 1) + i],
                    device_id=pos, device_id_type=pl.DeviceIdType.MESH)
            out.append((copy(me), copy(idx)))
    return out


def _exchange_start(arrays, name, gather):
    n = len(arrays)
    hbm = pl.BlockSpec(memory_space=pltpu.HBM)
    sem = pl.BlockSpec(memory_space=pltpu.SEMAPHORE)
    lands = [lax.empty(((N_DEV,) + a.shape) if gather else a.shape, a.dtype) for a in arrays]

    def body(*refs):
        srcs, lands_ = refs[:n], refs[n:2 * n]
        send_sems, recv_sems = refs[2 * n:2 * n + 2]
        for mine, _ in _remote_copies(srcs, lands_, send_sems, recv_sems, gather):
            mine.start()
        refs[-1][...] = jnp.zeros_like(refs[-1])

    sems = pltpu.SemaphoreType.DMA((n * (N_DEV - 1),))
    buffers = [pltpu.HBM(a.shape, a.dtype) for a in list(arrays) + lands]
    res = pl.pallas_call(
        body, name=name, in_specs=[hbm] * (2 * n), out_specs=[sem, sem] + [hbm] * (2 * n) + [pl.BlockSpec(memory_space=pltpu.VMEM)],
        out_shape=[sems, sems] + buffers + [jax.ShapeDtypeStruct((8, LANE), F32)],
        input_output_aliases={i: 2 + i for i in range(2 * n)},
        compiler_params=pltpu.CompilerParams(has_side_effects=pltpu.SideEffectType.DATAFLOW_SIDE_EFFECTING),
    )(*[pltpu.with_memory_space_constraint(a, pltpu.HBM) for a in list(arrays) + lands])
    return (res[0], res[1], res[2:2 + n], res[2 + n:2 + 2 * n]), res[-1]


def _exchange_wait(handle, after, name, gather):
    send_sems, recv_sems, srcs, lands = handle
    n = len(srcs)
    hbm = pl.BlockSpec(memory_space=pltpu.HBM)
    sem = pl.BlockSpec(memory_space=pltpu.SEMAPHORE)

    def body(*refs):
        for mine, arrival in _remote_copies(refs[:n], refs[n:2 * n], refs[2 * n], refs[2 * n + 1], gather):
            mine.wait_send()
            arrival.wait_recv()

    res = pl.pallas_call(
        body, name=name, in_specs=[hbm] * (2 * n) + [sem, sem, pl.BlockSpec(memory_space=pl.ANY)],
        out_specs=[hbm] * (2 * n), out_shape=[pltpu.HBM(a.shape, a.dtype) for a in list(srcs) + list(lands)],
        input_output_aliases={i: i for i in range(2 * n)},
        compiler_params=pltpu.CompilerParams(has_side_effects=pltpu.SideEffectType.DATAFLOW_SIDE_EFFECTING),
    )(*srcs, *lands, send_sems, recv_sems, after)
    return res[n:]


def _own_slot(land, mine):
    return lax.dynamic_update_slice(land, mine, (_my_index(),) + (0,) * (land.ndim - 1))


def _adamw(w, g, m, v):
    m = ADAM_B1 * m + (1.0 - ADAM_B1) * g
    v = ADAM_B2 * v + (1.0 - ADAM_B2) * (g * g)
    m_hat = m / (1.0 - ADAM_B1 ** ADAM_STEP)
    v_hat = v / (1.0 - ADAM_B2 ** ADAM_STEP)
    delta = -ADAM_LR * (m_hat / (jnp.sqrt(v_hat) + ADAM_EPS) + ADAM_WD * w)
    return delta, m, v


def _reduce_adamw(parts, w, m, v, name):
    nparts = len(parts)
    R, C = parts[0].shape[1:]
    tr = R
    while N_DEV * tr * C * parts[0].dtype.itemsize > REDUCE_BLOCK_BYTES and tr % 32 == 0:
        tr //= 2
    steps = R // tr

    def body(*refs):
        w_ref, m_ref, v_ref, g_ref, d_ref, nm_ref, nv_ref = refs[nparts:]
        for k, p_ref in enumerate(refs[:nparts]):
            @pl.when(pl.program_id(0) // steps == k)
            def _():
                g = p_ref[0].astype(F32)
                for s in range(1, p_ref.shape[0]):
                    g = g + p_ref[s].astype(F32)
                g_ref[...] = g
                d_ref[...], nm_ref[...], nv_ref[...] = _adamw(w_ref[...], g, m_ref[...], v_ref[...])

    def part_spec(k):
        return pl.BlockSpec((parts[k].shape[0], tr, C), lambda i: (0, jnp.clip(i - k * steps, 0, steps - 1), 0))

    row = pl.BlockSpec((tr, C), lambda i: (i, 0))
    return pl.pallas_call(
        body, name=name, grid=(nparts * steps,),
        in_specs=[part_spec(k) for k in range(nparts)] + [row, row, row],
        out_specs=[row] * 4, out_shape=[jax.ShapeDtypeStruct((nparts * R, C), F32)] * 4,
        compiler_params=_cp(),
    )(*parts, w, m, v)


BIG = ("w_in", "w_uq", "w_ukv", "w_out_a", "w_out_b", "w_out_c", "w_o")
SMALL = ("norm_g", "b_gate", "conv_w", "conv_b", "q_a_norm_g", "kv_a_norm_g", "mla_q_norm_g", "mla_k_norm_g",
         "dil_q_norm_g", "dil_k_norm_g")
PACK_ROWS = 128
REDUCE_BLOCK_BYTES = 6 * 1024 * 1024


def _pack_local(tensors):
    flat = jnp.concatenate([t.reshape(-1) for t in tensors])
    pad = (-flat.shape[0]) % (PACK_ROWS * LANE)
    return jnp.concatenate([flat, jnp.zeros((pad,), flat.dtype)]).reshape(-1, LANE)


def _unpack_local(rows, like):
    flat = rows.reshape(-1)
    out, off = [], 0
    for t in like:
        out.append(flat[off:off + t.size].reshape(t.shape))
        off += t.size
    return out


def _cols_to_slots(a):
    k = a.shape[0]
    return a.reshape(k, N_DEV, -1).transpose(1, 0, 2)


def _slots_to_cols(s):
    return s.transpose(1, 0, 2).reshape(s.shape[1], -1)


def _rope_tables(S):
    inv = ROPE_THETA ** (-jnp.arange(0, MLA_ROPE, 2, dtype=F32) / MLA_ROPE)
    ang = jnp.arange(S, dtype=F32)[:, None] * inv[None, :]
    cos, sin = jnp.cos(ang), jnp.sin(ang)
    one = jnp.ones((S, MLA_NOPE), F32)
    z16, z32, z64 = (jnp.zeros((S, n), F32) for n in (16, 32, 64))
    cosp = jnp.concatenate([one, cos, cos, jnp.ones((S, 32), F32)], axis=1)
    sa = jnp.concatenate([z64, -sin, z16, z32], axis=1)
    sb = jnp.concatenate([z64, z16, sin, z32], axis=1)
    return cosp, sa, sb


def _alibi_slopes():
    n = DIL_GROUPS * DIL_HEADS
    m = 2.0 ** (-8.0 * jnp.arange(1, n + 1, dtype=F32) / n)
    return m.reshape(DIL_GROUPS, NPAIR, 2)


def _pad_slots(s):
    n, k, c = s.shape
    return _slots_to_cols(jnp.concatenate([s, jnp.zeros((n, k, LANE - c), s.dtype)], axis=2))


def _layer_params(gw, small, l):
    p = {}
    p["wp"] = _pad_columns(gw["w_in"])
    p["norm_g"] = small["norm_g"][l][None]
    p["b_gate"] = small["b_gate"][l][None]
    p["conv_w"] = gw["conv_w"].transpose(1, 0, 2).reshape(CONV_K, CONV_WIDTH)
    p["conv_b"] = small["conv_b"][l][None]
    p["gq"] = small["q_a_norm_g"][l][None]
    p["gkv"] = small["kv_a_norm_g"][l][None]
    p["wuqp"] = _pad_slots(gw["w_uq"])
    kv = gw["w_ukv"]
    p["wkp"] = _pad_slots(kv[:, :, :MLA_NOPE])
    p["wv"] = kv[:, :, MLA_NOPE:].transpose(1, 0, 2).reshape(MLA_KV_LORA, MLA_HEADS * MLA_V)
    zpad = jnp.zeros((1, LANE - MLA_QK), F32)
    p["gmq"] = jnp.concatenate([small["mla_q_norm_g"][l][None], zpad], axis=1)
    p["gmk"] = jnp.concatenate([small["mla_k_norm_g"][l][None], zpad], axis=1)
    tile = lambda g: jnp.broadcast_to(g[:, None, :], (DIL_GROUPS, DIL_HEADS, DIL_HEAD_DIM)).reshape(1, DIL_QK)
    p["gdq"] = tile(small["dil_q_norm_g"][l])
    p["gdk"] = tile(small["dil_k_norm_g"][l])
    p["woa"], p["wob"], p["woc"] = (_slots_to_cols(gw[n]) for n in ("w_out_a", "w_out_b", "w_out_c"))
    p["wo"] = gw["w_o"].reshape(D_MODEL, D_MODEL)
    return p


def _layer_fwd(x, p, tabs, slopes, B, S):
    proj, ht = _inproj_fwd(x, p["norm_g"], p["wp"])
    ya = _mixa_fwd(proj, p["conv_w"], p["conv_b"], B, S)
    q, k, v = _mla_prep_fwd(proj, p["gq"], p["gkv"], p["wuqp"], p["wkp"], p["wv"], p["gmq"], p["gmk"], *tabs, S)
    ob, lse_b = _mla_attn_fwd(q, k, v, B, S)
    qn, kn, vn = _dil_prep_fwd(proj, p["gdq"], p["gdk"])
    ogs, lses = [], []
    for gi in range(DIL_GROUPS):
        o, lse = _dil_attn_fwd(gi, slopes[gi], qn, kn, vn, B, S)
        ogs.append(o)
        lses.append(lse)
    out = _merge_fwd(x, proj, p["b_gate"], ya, ob, ogs, lses, p["woa"], p["wob"], p["woc"], p["wo"])
    saved = dict(x=x, proj=proj, ht=ht, ya=ya, q=q, k=k, v=v, ob=ob, lse_b=lse_b, qn=qn, kn=kn, vn=vn, ogs=ogs, lses=lses)
    return out, saved


def _layer_bwd(dout, sv, p, tabs, slopes, B, S):
    proj = sv["proj"]
    (dproj, dya, dob, dlb, dg0, dg1, dg2, dl0, dl1, dl2, merged, dpa, dpb, dpc, yb, yc, dbg) = _merge_bwd(
        dout, proj, p["b_gate"], sv["ya"], sv["ob"], sv["ogs"], sv["lses"], p["woa"], p["wob"], p["woc"], p["wo"])
    g = {}
    g["w_o"] = _matmul_tn(merged, dout, "dw_o").reshape(N_DEV, D_MODEL // N_DEV, D_MODEL)
    g["w_out_a"] = _cols_to_slots(_matmul_tn(sv["ya"], dpa, "dw_out_a"))
    g["w_out_b"] = _cols_to_slots(_matmul_tn(yb, dpb, "dw_out_b"))
    g["w_out_c"] = _cols_to_slots(_matmul_tn(yc, dpc, "dw_out_c"))
    g["b_gate"] = dbg[0]
    dproj, st = _mixa_bwd(dproj, dya, proj, p["conv_w"], p["conv_b"], B, S)
    g["conv_w"] = st[0:CONV_K]
    g["conv_b"] = st[CONV_K]
    dq, dk, dv = _mla_attn_bwd(sv["q"], sv["k"], sv["v"], dob, sv["lse_b"], dlb, B, S)
    dproj, dwuqp, dwkp, dwv, dgq, dgkv, dgmq, dgmk = _mla_prep_bwd(
        dproj, dq, dk, dv, proj, p["gq"], p["gkv"], p["wuqp"], p["wkp"], p["wv"], p["gmq"], p["gmk"], *tabs, S)
    g["w_uq"] = _cols_to_slots(dwuqp)[:, :, :MLA_QK]
    g["w_ukv"] = jnp.concatenate([_cols_to_slots(dwkp)[:, :, :MLA_NOPE], _cols_to_slots(dwv)], axis=2)
    g["q_a_norm_g"], g["kv_a_norm_g"] = dgq[0], dgkv[0]
    g["mla_q_norm_g"], g["mla_k_norm_g"] = dgmq[0, :MLA_QK], dgmk[0, :MLA_QK]
    dqkv = None
    for gi, (dog, dlg) in enumerate(((dg0, dl0), (dg1, dl1), (dg2, dl2))):
        dqkv = _dil_attn_bwd(gi, slopes[gi], sv["qn"], sv["kn"], sv["vn"], dog, sv["lses"][gi], dlg, dqkv, B, S)
    dproj, dgdq, dgdk = _dil_prep_bwd(dproj, *dqkv, proj, p["gdq"], p["gdk"])
    g["dil_q_norm_g"] = dgdq.reshape(DIL_GROUPS, DIL_HEADS, DIL_HEAD_DIM).sum(axis=1)
    g["dil_k_norm_g"] = dgdk.reshape(DIL_GROUPS, DIL_HEADS, DIL_HEAD_DIM).sum(axis=1)
    g["w_in"] = _unpad_columns(_matmul_nn(sv["ht"], dproj, "dw_in"))
    dx, dng = _inproj_bwd_x(dproj, p["wp"], sv["x"], p["norm_g"], dout)
    g["norm_g"] = dng[0]
    return dx, g


def _after(token, a):
    return a if token is None else a + token[0:1, 0:1]


def _local_step(x, target, small, B, S, weights_of, grads_out):
    tabs = _rope_tables(S)
    sl = _alibi_slopes()
    slopes = [sl[gi] * float(DIL_PATTERNS[gi][1]) for gi in range(DIL_GROUPS)]
    params, saved = [], []
    for l in range(DEPTH):
        gw, token = weights_of(l, x)
        p = _layer_params(gw, small, l)
        p["norm_g"] = _after(token, p["norm_g"])
        x, sv = _layer_fwd(x, p, tabs, slopes, B, S)
        params.append(p)
        saved.append(sv)
    dout, lparts = _loss_head(x, target)
    sq = jnp.sum(lparts[:, 0, 0])
    token = None
    for l in reversed(range(DEPTH)):
        p = dict(params[l], b_gate=_after(token, params[l]["b_gate"]))
        dout, g = _layer_bwd(dout, saved[l], p, tabs, slopes, B, S)
        token = grads_out(l, g, dout)
    return sq, dout


def kernel(x, norm_g, w_in, b_gate, conv_w, conv_b, q_a_norm_g, w_uq, kv_a_norm_g, w_ukv, mla_q_norm_g, mla_k_norm_g, dil_q_norm_g, dil_k_norm_g, w_out_a, w_out_b, w_out_c, w_o, loss_target, m_norm_g, m_w_in, m_b_gate, m_conv_w, m_conv_b, m_q_a_norm_g, m_w_uq, m_kv_a_norm_g, m_w_ukv, m_mla_q_norm_g, m_mla_k_norm_g, m_dil_q_norm_g, m_dil_k_norm_g, m_w_out_a, m_w_out_b, m_w_out_c, m_w_o, v_norm_g, v_w_in, v_b_gate, v_conv_w, v_conv_b, v_q_a_norm_g, v_w_uq, v_kv_a_norm_g, v_w_ukv, v_mla_q_norm_g, v_mla_k_norm_g, v_dil_q_norm_g, v_dil_k_norm_g, v_w_out_a, v_w_out_b, v_w_out_c, v_w_o):
    names = ("norm_g", "w_in", "b_gate", "conv_w", "conv_b", "q_a_norm_g", "w_uq", "kv_a_norm_g", "w_ukv",
             "mla_q_norm_g", "mla_k_norm_g", "dil_q_norm_g", "dil_k_norm_g", "w_out_a", "w_out_b", "w_out_c", "w_o")
    w = dict(zip(names, (norm_g, w_in, b_gate, conv_w, conv_b, q_a_norm_g, w_uq, kv_a_norm_g, w_ukv, mla_q_norm_g,
                         mla_k_norm_g, dil_q_norm_g, dil_k_norm_g, w_out_a, w_out_b, w_out_c, w_o)))
    m = dict(zip(names, (m_norm_g, m_w_in, m_b_gate, m_conv_w, m_conv_b, m_q_a_norm_g, m_w_uq, m_kv_a_norm_g, m_w_ukv,
                         m_mla_q_norm_g, m_mla_k_norm_g, m_dil_q_norm_g, m_dil_k_norm_g, m_w_out_a, m_w_out_b,
                         m_w_out_c, m_w_o)))
    v = dict(zip(names, (v_norm_g, v_w_in, v_b_gate, v_conv_w, v_conv_b, v_q_a_norm_g, v_w_uq, v_kv_a_norm_g, v_w_ukv,
                         v_mla_q_norm_g, v_mla_k_norm_g, v_dil_q_norm_g, v_dil_k_norm_g, v_w_out_a, v_w_out_b,
                         v_w_out_c, v_w_o)))
    B, S, _ = x.shape
    me = _my_index()
    cshard = CONV_WIDTH // N_DEV

    shards = [[w[n][l].astype(BF16) for n in BIG] for l in range(DEPTH)]
    state = {}

    def weights_of(l, after):
        if l == 0:
            got = _gather_two_level(shards[0] + [conv_w], "all_gather_weights_0")
            state["gather"], token = _exchange_start(shards[1], "all_gather_weights_1_start", gather=True)
            state["conv_w"] = got[-1]
        else:
            landed = _exchange_wait(state["gather"], after, "all_gather_weights_1_wait", gather=True)
            got, token = [_own_slot(a, s[None]) for a, s in zip(landed, shards[1])], None
        gw = dict(zip(BIG, got))
        gw["conv_w"] = state["conv_w"][:, l]
        return gw, token

    recv, small_parts = {}, {}

    def grads_out(l, g, after):
        small_parts[l] = [g[n] for n in SMALL]
        send = [g[n].astype(BF16) for n in BIG]
        if l == DEPTH - 1:
            state["scatter"], token = _exchange_start(send, "exchange_weight_grads_1_start", gather=False)
            state["sent"] = send
            return token
        landed = _exchange_wait(state["scatter"], after, "exchange_weight_grads_1_wait", gather=False)
        mine = [lax.dynamic_slice_in_dim(s, me, 1, axis=0) for s in state["sent"]]
        recv[DEPTH - 1] = [_own_slot(a, s) for a, s in zip(landed, mine)]
        swapped = _sibling_swap(send, "exchange_weight_grads_0_sibling")
        sums = [_chip_pair_sum(s, t, "chip_pair_sum_" + n) for n, s, t in zip(BIG, send, swapped)]
        recv[l] = _chip_exchange(sums, "exchange_weight_grads_0")
        return None

    sq, grad_x = _local_step(x.reshape(B * S, D_MODEL), loss_target.reshape(B * S, D_MODEL), w, B, S,
                             weights_of, grads_out)
    loss = lax.psum(sq * (0.5 / D_MODEL), AXES)

    res = {}
    for i, n in enumerate(BIG):
        rows = lambda a: a.reshape(-1, a.shape[-1])
        outs = _reduce_adamw([recv[l][i] for l in range(DEPTH)], rows(w[n]), rows(m[n]), rows(v[n]),
                             "reduce_adamw_" + n)
        res[n] = tuple(a.reshape(w[n].shape) for a in outs)
    part = {n: jnp.stack([small_parts[l][i] for l in range(DEPTH)]) for i, n in enumerate(SMALL)}

    def widen(t):
        return lax.dynamic_update_slice(jnp.zeros((DEPTH, CONV_K, CONV_WIDTH), F32), t, (0, 0, me * cshard))

    small_like = [part[n] for n in SMALL]
    pick = lambda d: [widen(d[n]) if n == "conv_w" else d[n] for n in SMALL]
    parts, = _exchange([_pack_local(small_like)], "all_gather_small_grads", gather=True)
    gs, ds, ms, vs = _reduce_adamw([parts], _pack_local(pick(w)), _pack_local(pick(m)), _pack_local(pick(v)),
                                   "reduce_adamw_small")
    for n, t in zip(SMALL, zip(*(_unpack_local(a, small_like) for a in (gs, ds, ms, vs)))):
        if n == "conv_w":
            t = tuple(lax.dynamic_slice(a, (0, 0, me * cshard), (DEPTH, CONV_K, cshard)) for a in t)
        res[n] = t

    out = [loss, grad_x.reshape(B, S, D_MODEL)]
    for i in range(4):
        out += [res[n][i] for n in names]
    return tuple(out)
```

```python
import jax
import jax.numpy as jnp
from jax import lax
from jax.experimental import pallas as pl
from jax.experimental.pallas import tpu as pltpu

F32 = jnp.float32
BF16 = jnp.bfloat16

D_MODEL = 1024
DEPTH = 2
CONV_WIDTH = 512
CONV_K = 3
MLA_HEADS = 8
MLA_Q_LORA = 256
MLA_KV_LORA = 128
MLA_NOPE = 64
MLA_ROPE = 32
MLA_V = 64
MLA_QK = MLA_NOPE + MLA_ROPE
ROPE_THETA = 10000.0
DIL_PATTERNS = ((128, 1), (512, 4), (2048, 16))
DIL_GROUPS = 3
DIL_HEADS = 8
DIL_HEAD_DIM = 64
DIL_WIDTH = DIL_HEADS * DIL_HEAD_DIM
DIL_QK = DIL_GROUPS * DIL_WIDTH
EPS = 1e-6
N_IN = 11168

ADAM_LR = 0.001
ADAM_B1 = 0.9
ADAM_B2 = 0.999
ADAM_EPS = 1e-08
ADAM_WD = 0.01
ADAM_STEP = 10

N_DEV = 8
AXES = ("x", "y", "c")
LANE = 128
HALF = 64
NPAIR = 4

CB_AB, CB_AC, CB_AX, CB_AZ = 0, 4, 8, 12
CB_CQ, CB_CKV, CB_KPE = 16, 18, 19
CB_BZ = 20
CB_DQ, CB_DK, CB_DV = 24, 36, 48
CB_CZ, CB_GATE = 60, 64
NCB = 88
PP = NCB * LANE
KPE_END = CB_KPE * LANE + MLA_ROPE
SHARD_COLS = N_IN // N_DEV
NEG = -1e30
VMEM_LIMIT = 56 * 1024 * 1024


def _pad_columns(shards):
    parts = []
    for p in range(N_DEV):
        cut = min(max(KPE_END - p * SHARD_COLS, 0), SHARD_COLS)
        if 0 < cut < SHARD_COLS:
            parts += [shards[p, :, :cut], jnp.zeros((shards.shape[1], LANE - MLA_ROPE), shards.dtype), shards[p, :, cut:]]
        else:
            parts.append(shards[p])
    return jnp.concatenate(parts, axis=1)


def _unpad_columns(wp):
    def columns(a, b):
        gap = LANE - MLA_ROPE
        if b <= KPE_END:
            return wp[:, a:b]
        if a >= KPE_END:
            return wp[:, a + gap:b + gap]
        return jnp.concatenate([wp[:, a:KPE_END], wp[:, KPE_END + gap:b + gap]], axis=1)

    return jnp.stack([columns(p * SHARD_COLS, (p + 1) * SHARD_COLS) for p in range(N_DEV)])


def _put_copies(stages, dst_ref, sems, slot, rows, cols):
    return [pltpu.make_async_copy(st.at[slot], dst_ref.at[rows, pl.ds(c0, st.shape[-1])], sems.at[slot, k])
            for k, (st, c0) in enumerate(zip(stages, cols))]


def _put_pipeline(step, nsteps, copies_of, fill):
    @pl.when(step >= 2)
    def _():
        for cp in copies_of(step - 2):
            cp.wait()

    fill(step % 2)
    for cp in copies_of(step):
        cp.start()

    @pl.when(step == nsteps - 1)
    def _():
        if nsteps >= 2:
            for cp in copies_of(step - 1):
                cp.wait()
        for cp in copies_of(step):
            cp.wait()


def _cp():
    return pltpu.CompilerParams(vmem_limit_bytes=VMEM_LIMIT)


def _rstd(x, n):
    return lax.rsqrt(jnp.sum(x * x, axis=-1, keepdims=True) * (1.0 / n) + EPS)


def _sigmoid(z):
    return 1.0 / (1.0 + jnp.exp(-z))


def _silu(z):
    return z * _sigmoid(z)


def _dsilu(z):
    s = _sigmoid(z)
    return s * (1.0 + z * (1.0 - s))


def _mm(a, b):
    return jnp.dot(a.astype(BF16), b.astype(BF16), preferred_element_type=F32)


def _mm_nt(a, b):
    return lax.dot_general(a.astype(BF16), b.astype(BF16), (((1,), (1,)), ((), ())), preferred_element_type=F32)


def _mm_tn(a, b):
    return lax.dot_general(a.astype(BF16), b.astype(BF16), (((0,), (0,)), ((), ())), preferred_element_type=F32)


def _lane_lo(shape):
    return lax.broadcasted_iota(jnp.int32, shape, len(shape) - 1) < HALF


def _head_bcast_sum(x, terms=3):
    w = x.shape[-1]
    same = (lax.broadcasted_iota(jnp.int32, (w, w), 0) // HALF) == (lax.broadcasted_iota(jnp.int32, (w, w), 1) // HALF)
    ones = jnp.where(same, 1.0, 0.0).astype(jnp.bfloat16)
    total = None
    for _ in range(terms):
        term = x.astype(jnp.bfloat16)
        x = x - term.astype(F32)
        part = jnp.dot(term, ones, preferred_element_type=F32)
        total = part if total is None else total + part
    return total


def _rope(t, cos, sa, sb):
    return t * cos + pltpu.roll(t, LANE - 16, axis=1) * sa + pltpu.roll(t, 16, axis=1) * sb


def _rope_t(d, cos, sa, sb):
    return d * cos + pltpu.roll(d * sa, 16, axis=1) + pltpu.roll(d * sb, LANE - 16, axis=1)


def _shift_down(u, k):
    rows = lax.broadcasted_iota(jnp.int32, u.shape, 0)
    return jnp.where(rows >= k, pltpu.roll(u, k, axis=0), 0.0)


def _shift_up(u, k):
    n = u.shape[0]
    rows = lax.broadcasted_iota(jnp.int32, u.shape, 0)
    return jnp.where(rows < n - k, pltpu.roll(u, n - k, axis=0), 0.0)


def _tile(n, want):
    t = min(n, want)
    assert n % t == 0, (n, want)
    return t


def _inproj_fwd(x, g, wp):
    T = x.shape[0]
    tm, tn = _tile(T, 2048), 512

    def body(x_ref, g_ref, w_ref, proj_ref, ht_ref, h_ref):
        @pl.when(pl.program_id(1) == 0)
        def _():
            n = min(tm, 512)
            for r0 in range(0, tm, n):
                xv = x_ref[r0:r0 + n, :]
                h = xv * _rstd(xv, D_MODEL) * g_ref[...]
                h_ref[r0:r0 + n, :] = h.astype(BF16)
                ht_ref[:, r0:r0 + n] = h.T.astype(BF16)

        proj_ref[...] = jnp.dot(h_ref[...], w_ref[...], preferred_element_type=F32).astype(BF16)

    return pl.pallas_call(
        body, name="inproj_fwd", grid=(T // tm, PP // tn),
        in_specs=[pl.BlockSpec((tm, D_MODEL), lambda i, j: (i, 0)),
                  pl.BlockSpec((1, D_MODEL), lambda i, j: (0, 0)),
                  pl.BlockSpec((D_MODEL, tn), lambda i, j: (0, j))],
        out_specs=[pl.BlockSpec((tm, tn), lambda i, j: (i, j)),
                   pl.BlockSpec((D_MODEL, tm), lambda i, j: (0, i))],
        out_shape=[jax.ShapeDtypeStruct((T, PP), BF16), jax.ShapeDtypeStruct((D_MODEL, T), BF16)],
        scratch_shapes=[pltpu.VMEM((tm, D_MODEL), BF16)],
        compiler_params=_cp(),
    )(x, g, wp)


def _matmul_nn(at, b, name):
    K, T = at.shape
    N = b.shape[1]
    tt, tn = _tile(T, 1024), _tile(N, 2816)
    nk = T // tt

    def body(a_ref, b_ref, o_ref, acc_ref):
        k = pl.program_id(1)

        @pl.when(k == 0)
        def _():
            acc_ref[...] = jnp.zeros_like(acc_ref)

        acc_ref[...] += jnp.dot(a_ref[...], b_ref[...], preferred_element_type=F32)

        @pl.when(k == nk - 1)
        def _():
            o_ref[...] = acc_ref[...].astype(BF16)

    return pl.pallas_call(
        body, name=name, grid=(N // tn, nk),
        in_specs=[pl.BlockSpec((K, tt), lambda j, k: (0, k)),
                  pl.BlockSpec((tt, tn), lambda j, k: (k, j))],
        out_specs=pl.BlockSpec((K, tn), lambda j, k: (0, j)),
        out_shape=jax.ShapeDtypeStruct((K, N), BF16),
        scratch_shapes=[pltpu.VMEM((K, tn), F32)],
        compiler_params=_cp(),
    )(at, b)


def _matmul_tn(a, b, name):
    T, K = a.shape
    N = b.shape[1]
    tt, tn = _tile(T, 512), _tile(N, 1024)

    def body(a_ref, b_ref, o_ref):
        @pl.when(pl.program_id(1) == 0)
        def _():
            o_ref[...] = jnp.zeros_like(o_ref)

        o_ref[...] += _mm_tn(a_ref[...], b_ref[...])

    return pl.pallas_call(
        body, name=name, grid=(N // tn, T // tt),
        in_specs=[pl.BlockSpec((tt, K), lambda j, k: (k, 0)),
                  pl.BlockSpec((tt, tn), lambda j, k: (k, j))],
        out_specs=pl.BlockSpec((K, tn), lambda j, k: (0, j)),
        out_shape=jax.ShapeDtypeStruct((K, N), F32),
        compiler_params=_cp(),
    )(a, b)


def _inproj_bwd_x(dproj, wp, x, g, dout):
    T = x.shape[0]
    tm, tk = _tile(T, 1024), 1024
    nk = PP // tk

    def body(dp_ref, w_ref, x_ref, g_ref, do_ref, dx_ref, dg_ref, acc_ref):
        i, k = pl.program_id(0), pl.program_id(1)

        @pl.when(k == 0)
        def _():
            acc_ref[...] = jnp.zeros_like(acc_ref)

        @pl.when((k == 0) & (i == 0))
        def _():
            dg_ref[...] = jnp.zeros_like(dg_ref)

        acc_ref[...] += _mm_nt(dp_ref[...], w_ref[...])

        @pl.when(k == nk - 1)
        def _():
            dh = acc_ref[...]
            xv = x_ref[...]
            r = _rstd(xv, D_MODEL)
            gy = dh * g_ref[...]
            dot = jnp.sum(xv * gy, axis=-1, keepdims=True) * (1.0 / D_MODEL)
            dx_ref[...] = do_ref[...] + r * gy - xv * (r * r * r) * dot
            dg_ref[...] += jnp.sum(dh * xv * r, axis=0, keepdims=True)

    return pl.pallas_call(
        body, name="inproj_bwd_x", grid=(T // tm, nk),
        in_specs=[pl.BlockSpec((tm, tk), lambda i, k: (i, k)),
                  pl.BlockSpec((D_MODEL, tk), lambda i, k: (0, k)),
                  pl.BlockSpec((tm, D_MODEL), lambda i, k: (i, 0)),
                  pl.BlockSpec((1, D_MODEL), lambda i, k: (0, 0)),
                  pl.BlockSpec((tm, D_MODEL), lambda i, k: (i, 0))],
        out_specs=[pl.BlockSpec((tm, D_MODEL), lambda i, k: (i, 0)),
                   pl.BlockSpec((1, D_MODEL), lambda i, k: (0, 0))],
        out_shape=[jax.ShapeDtypeStruct((T, D_MODEL), F32), jax.ShapeDtypeStruct((1, D_MODEL), F32)],
        scratch_shapes=[pltpu.VMEM((tm, D_MODEL), F32)],
        compiler_params=_cp(),
    )(dproj, wp, x, g, dout)


A_SEGS = (CB_AB, CB_AC, CB_AX, CB_AZ)


def _mixa_fwd(proj, cw, cb, B, S):
    nc = CONV_WIDTH // LANE

    def body(ab_ref, ac_ref, ax_ref, az_ref, cw_ref, cb_ref, y_ref):
        ab, ac, ax, az = (r[...].astype(F32) for r in (ab_ref, ac_ref, ax_ref, az_ref))
        u = ac * ax
        conv = cb_ref[...] + cw_ref[0:1, :] * _shift_down(u, 2) + cw_ref[1:2, :] * _shift_down(u, 1) + cw_ref[2:3, :] * u
        y_ref[...] = (ab * conv * _silu(az)).astype(BF16)

    return pl.pallas_call(
        body, name="mixa_fwd", grid=(B, nc),
        in_specs=[pl.BlockSpec((S, LANE), lambda b, j, c0=c0: (b, c0 + j)) for c0 in A_SEGS]
                 + [pl.BlockSpec((CONV_K, LANE), lambda b, j: (0, j)),
                    pl.BlockSpec((1, LANE), lambda b, j: (0, j))],
        out_specs=pl.BlockSpec((S, LANE), lambda b, j: (b, j)),
        out_shape=jax.ShapeDtypeStruct((B * S, CONV_WIDTH), BF16),
        compiler_params=_cp(),
    )(proj, proj, proj, proj, cw, cb)


def _mixa_bwd(dproj, dy, proj, cw, cb, B, S):
    nc = CONV_WIDTH // LANE

    def body(dpin_ref, dy_ref, ab_ref, ac_ref, ax_ref, az_ref, cw_ref, cb_ref, dp_ref, st_ref, stage, sems):
        del dpin_ref
        j, b = pl.program_id(0), pl.program_id(1)
        ab, ac, ax, az = (r[...].astype(F32) for r in (ab_ref, ac_ref, ax_ref, az_ref))
        u = ac * ax
        u1, u2 = _shift_down(u, 1), _shift_down(u, 2)
        w0, w1, w2 = cw_ref[0:1, :], cw_ref[1:2, :], cw_ref[2:3, :]
        conv = cb_ref[...] + w0 * u2 + w1 * u1 + w2 * u
        s = _silu(az)
        d = dy_ref[...]
        dconv = d * ab * s
        du = w2 * dconv + w1 * _shift_up(dconv, 1) + w0 * _shift_up(dconv, 2)
        grads = (d * conv * s, du * ax, du * ac, d * ab * conv * _dsilu(az))

        def fill(slot):
            for k, v in enumerate(grads):
                stage[slot, k] = v.astype(BF16)

        def copies_of(step):
            sj, sb = step // B, step % B
            return _put_copies([stage.at[:, k] for k in range(4)], dp_ref, sems, step % 2,
                               pl.ds(pl.multiple_of(sb * S, S), S),
                               [pl.multiple_of((c0 + sj) * LANE, LANE) for c0 in A_SEGS])

        _put_pipeline(j * B + b, nc * B, copies_of, fill)
        row = lax.broadcasted_iota(jnp.int32, (8, LANE), 0)
        st = jnp.zeros((8, LANE), F32)
        for r, v in enumerate((dconv * u2, dconv * u1, dconv * u, dconv)):
            st = st + jnp.where(row == r, jnp.sum(v, axis=0, keepdims=True), 0.0)

        @pl.when(pl.program_id(1) == 0)
        def _():
            st_ref[...] = st

        @pl.when(pl.program_id(1) != 0)
        def _():
            st_ref[...] += st

    return pl.pallas_call(
        body, name="mixa_bwd", grid=(nc, B),
        in_specs=[pl.BlockSpec(memory_space=pl.ANY),
                  pl.BlockSpec((S, LANE), lambda j, b: (b, j))]
                 + [pl.BlockSpec((S, LANE), lambda j, b, c0=c0: (b, c0 + j)) for c0 in A_SEGS]
                 + [pl.BlockSpec((CONV_K, LANE), lambda j, b: (0, j)),
                    pl.BlockSpec((1, LANE), lambda j, b: (0, j))],
        out_specs=[pl.BlockSpec(memory_space=pl.ANY),
                   pl.BlockSpec((8, LANE), lambda j, b: (0, j))],
        out_shape=[jax.ShapeDtypeStruct(dproj.shape, BF16), jax.ShapeDtypeStruct((8, CONV_WIDTH), F32)],
        scratch_shapes=[pltpu.VMEM((2, 4, S, LANE), BF16), pltpu.SemaphoreType.DMA((2, 4))],
        input_output_aliases={0: 0},
        compiler_params=_cp(),
    )(dproj, dy, proj, proj, proj, proj, cw, cb)


def _mla_prep_fwd(proj, gq, gkv, wuqp, wkp, wv, gmq, gmk, cos, sa, sb, S):
    T = proj.shape[0]
    ts = _tile(S, 512)
    ns = S // ts
    W = MLA_HEADS * LANE

    def body(p_ref, gq_ref, gkv_ref, wuq_ref, wk_ref, wv_ref, gmq_ref, gmk_ref, cos_ref, sa_ref, sb_ref,
             q_ref, k_ref, v_ref):
        cq = p_ref[:, 0:2 * LANE].astype(F32)
        ckv = p_ref[:, 2 * LANE:3 * LANE].astype(F32)
        kpe = pltpu.roll(p_ref[:, 3 * LANE:4 * LANE].astype(F32), HALF, axis=1)
        cqn = cq * _rstd(cq, MLA_Q_LORA) * gq_ref[...]
        ckn = (ckv * _rstd(ckv, MLA_KV_LORA) * gkv_ref[...]).astype(BF16)
        q0 = _mm(cqn, wuq_ref[...])
        kn = _mm(ckn, wk_ref[...])
        v_ref[...] = _mm(ckn, wv_ref[...]).astype(BF16)
        c, a, b = cos_ref[...], sa_ref[...], sb_ref[...]
        for h in range(MLA_HEADS):
            q0h = q0[:, h * LANE:(h + 1) * LANE]
            q_ref[h] = _rope(q0h * _rstd(q0h, MLA_QK) * gmq_ref[...], c, a, b).astype(BF16)
            k0h = kn[:, h * LANE:(h + 1) * LANE] + kpe
            k_ref[h] = _rope(k0h * _rstd(k0h, MLA_QK) * gmk_ref[...], c, a, b).astype(BF16)

    def whole(r, c):
        return pl.BlockSpec((r, c), lambda i: (0, 0))

    tab = pl.BlockSpec((ts, LANE), lambda i: (i % ns, 0))
    return pl.pallas_call(
        body, name="mla_prep_fwd", grid=(T // ts,),
        in_specs=[pl.BlockSpec((ts, 4 * LANE), lambda i: (i, CB_CQ // 4)),
                  whole(1, MLA_Q_LORA), whole(1, MLA_KV_LORA), whole(MLA_Q_LORA, W), whole(MLA_KV_LORA, W),
                  whole(MLA_KV_LORA, MLA_HEADS * MLA_V), whole(1, LANE), whole(1, LANE), tab, tab, tab],
        out_specs=[pl.BlockSpec((MLA_HEADS, ts, LANE), lambda i: (0, i, 0)),
                   pl.BlockSpec((MLA_HEADS, ts, LANE), lambda i: (0, i, 0)),
                   pl.BlockSpec((ts, MLA_HEADS * MLA_V), lambda i: (i, 0))],
        out_shape=[jax.ShapeDtypeStruct((MLA_HEADS, T, LANE), BF16), jax.ShapeDtypeStruct((MLA_HEADS, T, LANE), BF16),
                   jax.ShapeDtypeStruct((T, MLA_HEADS * MLA_V), BF16)],
        compiler_params=_cp(),
    )(proj, gq, gkv, wuqp, wkp, wv, gmq, gmk, cos, sa, sb)


def _mla_prep_bwd(dproj, dq, dk, dv, proj, gq, gkv, wuqp, wkp, wv, gmq, gmk, cos, sa, sb, S):
    T = proj.shape[0]
    ts = _tile(S, 256)
    ns = S // ts
    W = MLA_HEADS * LANE

    def body(dpin_ref, dq_ref, dk_ref, dv_ref, p_ref, gq_ref, gkv_ref, wuq_ref, wk_ref, wv_ref, gmq_ref, gmk_ref,
             cos_ref, sa_ref, sb_ref,
             dp_ref, dwuq_ref, dwk_ref, dwv_ref, dgq_ref, dgkv_ref, dgmq_ref, dgmk_ref, dq0_ref, dkn_ref):
        del dpin_ref

        @pl.when(pl.program_id(0) == 0)
        def _():
            for r in (dwuq_ref, dwk_ref, dwv_ref, dgq_ref, dgkv_ref, dgmq_ref, dgmk_ref):
                r[...] = jnp.zeros_like(r)

        cq = p_ref[:, 0:2 * LANE].astype(F32)
        ckv = p_ref[:, 2 * LANE:3 * LANE].astype(F32)
        kpe = pltpu.roll(p_ref[:, 3 * LANE:4 * LANE].astype(F32), HALF, axis=1)
        rq = _rstd(cq, MLA_Q_LORA)
        rkv = _rstd(ckv, MLA_KV_LORA)
        gq, gkv, gmq, gmk = gq_ref[...], gkv_ref[...], gmq_ref[...], gmk_ref[...]
        cqn = (cq * rq * gq).astype(BF16)
        ckn = (ckv * rkv * gkv).astype(BF16)
        q0 = _mm(cqn, wuq_ref[...])
        kn = _mm(ckn, wk_ref[...])
        c, a, b = cos_ref[...], sa_ref[...], sb_ref[...]
        lane = lax.broadcasted_iota(jnp.int32, (ts, LANE), 1)
        dgmq = jnp.zeros((1, LANE), F32)
        dgmk = jnp.zeros((1, LANE), F32)
        dkpe = jnp.zeros((ts, LANE), F32)
        for h in range(MLA_HEADS):
            q0h = q0[:, h * LANE:(h + 1) * LANE]
            r = _rstd(q0h, MLA_QK)
            d1 = _rope_t(dq_ref[h], c, a, b)
            gy = d1 * gmq
            dq0_ref[:, h * LANE:(h + 1) * LANE] = (
                r * gy - q0h * (r * r * r) * (jnp.sum(q0h * gy, axis=-1, keepdims=True) * (1.0 / MLA_QK))).astype(BF16)
            dgmq = dgmq + jnp.sum(d1 * q0h * r, axis=0, keepdims=True)
            k0h = kn[:, h * LANE:(h + 1) * LANE] + kpe
            r = _rstd(k0h, MLA_QK)
            d1 = _rope_t(dk_ref[h], c, a, b)
            gy = d1 * gmk
            dk0 = r * gy - k0h * (r * r * r) * (jnp.sum(k0h * gy, axis=-1, keepdims=True) * (1.0 / MLA_QK))
            dgmk = dgmk + jnp.sum(d1 * k0h * r, axis=0, keepdims=True)
            dkn_ref[:, h * LANE:(h + 1) * LANE] = jnp.where(lane < MLA_NOPE, dk0, 0.0).astype(BF16)
            dkpe = dkpe + jnp.where((lane >= MLA_NOPE) & (lane < MLA_QK), dk0, 0.0)
        dq0 = dq0_ref[...]
        dkn = dkn_ref[...]
        dvv = dv_ref[...]
        dwuq_ref[...] += _mm_tn(cqn, dq0)
        dwk_ref[...] += _mm_tn(ckn, dkn)
        dwv_ref[...] += _mm_tn(ckn, dvv)
        dgmq_ref[...] += dgmq
        dgmk_ref[...] += dgmk
        dcqn = _mm_nt(dq0, wuq_ref[...])
        gy = dcqn * gq
        dp_ref[:, 0:2 * LANE] = (
            rq * gy - cq * (rq * rq * rq) * (jnp.sum(cq * gy, axis=-1, keepdims=True) * (1.0 / MLA_Q_LORA))).astype(BF16)
        dgq_ref[...] += jnp.sum(dcqn * cq * rq, axis=0, keepdims=True)
        dckn = _mm_nt(dkn, wk_ref[...]) + _mm_nt(dvv, wv_ref[...])
        gy = dckn * gkv
        dp_ref[:, 2 * LANE:3 * LANE] = (
            rkv * gy - ckv * (rkv * rkv * rkv) * (jnp.sum(ckv * gy, axis=-1, keepdims=True) * (1.0 / MLA_KV_LORA))).astype(BF16)
        dgkv_ref[...] += jnp.sum(dckn * ckv * rkv, axis=0, keepdims=True)
        dp_ref[:, 3 * LANE:4 * LANE] = pltpu.roll(dkpe, HALF, axis=1).astype(BF16)

    def whole(r, c):
        return pl.BlockSpec((r, c), lambda i: (0, 0))

    tab = pl.BlockSpec((ts, LANE), lambda i: (i % ns, 0))
    heads = pl.BlockSpec((MLA_HEADS, ts, LANE), lambda i: (0, i, 0))
    return pl.pallas_call(
        body, name="mla_prep_bwd", grid=(T // ts,),
        in_specs=[pl.BlockSpec(memory_space=pl.ANY), heads, heads,
                  pl.BlockSpec((ts, MLA_HEADS * MLA_V), lambda i: (i, 0)),
                  pl.BlockSpec((ts, 4 * LANE), lambda i: (i, CB_CQ // 4)),
                  whole(1, MLA_Q_LORA), whole(1, MLA_KV_LORA), whole(MLA_Q_LORA, W), whole(MLA_KV_LORA, W),
                  whole(MLA_KV_LORA, MLA_HEADS * MLA_V), whole(1, LANE), whole(1, LANE), tab, tab, tab],
        out_specs=[pl.BlockSpec((ts, 4 * LANE), lambda i: (i, CB_CQ // 4)),
                   whole(MLA_Q_LORA, W), whole(MLA_KV_LORA, W), whole(MLA_KV_LORA, MLA_HEADS * MLA_V),
                   whole(1, MLA_Q_LORA), whole(1, MLA_KV_LORA), whole(1, LANE), whole(1, LANE)],
        out_shape=[jax.ShapeDtypeStruct(dproj.shape, BF16),
                   jax.ShapeDtypeStruct((MLA_Q_LORA, W), F32), jax.ShapeDtypeStruct((MLA_KV_LORA, W), F32),
                   jax.ShapeDtypeStruct((MLA_KV_LORA, MLA_HEADS * MLA_V), F32),
                   jax.ShapeDtypeStruct((1, MLA_Q_LORA), F32), jax.ShapeDtypeStruct((1, MLA_KV_LORA), F32),
                   jax.ShapeDtypeStruct((1, LANE), F32), jax.ShapeDtypeStruct((1, LANE), F32)],
        scratch_shapes=[pltpu.VMEM((ts, W), BF16), pltpu.VMEM((ts, W), BF16)],
        input_output_aliases={0: 0},
        compiler_params=_cp(),
    )(dproj, dq, dk, dv, proj, gq, gkv, wuqp, wkp, wv, gmq, gmk, cos, sa, sb)


def _dil_prep_fwd(proj, gq, gk):
    T = proj.shape[0]
    ts = _tile(T, 512)

    def body(pq_ref, pk_ref, pv_ref, gq_ref, gk_ref, q_ref, k_ref, v_ref):
        v_ref[...] = pv_ref[...].astype(F32)
        for c in range(NPAIR):
            cs = slice(c * LANE, (c + 1) * LANE)
            t = jnp.concatenate([pq_ref[:, cs], pk_ref[:, cs]], axis=1).astype(F32)
            y = t * lax.rsqrt(_head_bcast_sum(t * t, terms=2) * (1.0 / DIL_HEAD_DIM) + EPS)
            q_ref[:, cs] = y[:, 0:LANE] * gq_ref[:, cs]
            k_ref[:, cs] = y[:, LANE:2 * LANE] * gk_ref[:, cs]

    col = pl.BlockSpec((1, DIL_WIDTH), lambda i, g: (0, g))
    out = pl.BlockSpec((ts, DIL_WIDTH), lambda i, g: (i, g))
    seg = lambda c0: pl.BlockSpec((ts, DIL_WIDTH), lambda i, g: (i, c0 // NPAIR + g))
    return pl.pallas_call(
        body, name="dil_prep_fwd", grid=(T // ts, DIL_GROUPS),
        in_specs=[seg(CB_DQ), seg(CB_DK), seg(CB_DV), col, col],
        out_specs=[out, out, out],
        out_shape=[jax.ShapeDtypeStruct((T, DIL_QK), F32)] * 3,
        compiler_params=_cp(),
    )(proj, proj, proj, gq, gk)


def _dil_prep_bwd(dproj, ddq, ddk, ddv, proj, gq, gk):
    T = proj.shape[0]
    ts = _tile(T, 512)
    nt = T // ts

    def body(dpin_ref, ddq_ref, ddk_ref, ddv_ref, pq_ref, pk_ref, gq_ref, gk_ref, dp_ref, dgq_ref, dgk_ref,
             stage, sems):
        del dpin_ref
        g, i = pl.program_id(0), pl.program_id(1)

        @pl.when(i == 0)
        def _():
            dgq_ref[...] = jnp.zeros_like(dgq_ref)
            dgk_ref[...] = jnp.zeros_like(dgk_ref)

        def fill(slot):
            stage[slot, 2] = ddv_ref[...].astype(BF16)
            for c in range(NPAIR):
                cs = slice(c * LANE, (c + 1) * LANE)
                t = jnp.concatenate([pq_ref[:, cs], pk_ref[:, cs]], axis=1).astype(F32)
                d = jnp.concatenate([ddq_ref[:, cs], ddk_ref[:, cs]], axis=1)
                gy = d * jnp.concatenate([gq_ref[:, cs], gk_ref[:, cs]], axis=1)
                r = lax.rsqrt(_head_bcast_sum(t * t, terms=2) * (1.0 / DIL_HEAD_DIM) + EPS)
                dot = _head_bcast_sum(t * gy, terms=2) * (1.0 / DIL_HEAD_DIM)
                dx = (r * gy - t * (r * r * r) * dot).astype(BF16)
                stage[slot, 0, :, cs] = dx[:, 0:LANE]
                stage[slot, 1, :, cs] = dx[:, LANE:2 * LANE]
                part = jnp.sum(d * t * r, axis=0, keepdims=True)
                dgq_ref[:, cs] += part[:, 0:LANE]
                dgk_ref[:, cs] += part[:, LANE:2 * LANE]

        def copies_of(step):
            sg, si = step // nt, step % nt
            return _put_copies([stage.at[:, k] for k in range(3)], dp_ref, sems, step % 2,
                               pl.ds(pl.multiple_of(si * ts, ts), ts),
                               [pl.multiple_of((c0 + NPAIR * sg) * LANE, LANE) for c0 in (CB_DQ, CB_DK, CB_DV)])

        _put_pipeline(g * nt + i, DIL_GROUPS * nt, copies_of, fill)

    col = pl.BlockSpec((1, DIL_WIDTH), lambda g, i: (0, g))
    tok = pl.BlockSpec((ts, DIL_WIDTH), lambda g, i: (i, g))
    seg = lambda c0: pl.BlockSpec((ts, DIL_WIDTH), lambda g, i: (i, c0 // NPAIR + g))
    return pl.pallas_call(
        body, name="dil_prep_bwd", grid=(DIL_GROUPS, nt),
        in_specs=[pl.BlockSpec(memory_space=pl.ANY), tok, tok, tok, seg(CB_DQ), seg(CB_DK), col, col],
        out_specs=[pl.BlockSpec(memory_space=pl.ANY), col, col],
        out_shape=[jax.ShapeDtypeStruct(dproj.shape, BF16), jax.ShapeDtypeStruct((1, DIL_QK), F32),
                   jax.ShapeDtypeStruct((1, DIL_QK), F32)],
        scratch_shapes=[pltpu.VMEM((2, 3, ts, DIL_WIDTH), BF16), pltpu.SemaphoreType.DMA((2, 3))],
        input_output_aliases={0: 0},
        compiler_params=_cp(),
    )(dproj, ddq, ddk, ddv, proj, proj, gq, gk)


COPY_ROWS = 256


def _to_classes(src_ref, dst_ref, d, L, scale=None):
    n = min(L, COPY_ROWS)
    for r in range(d):
        for c0 in range(0, L, n):
            rows = pl.ds(r + c0 * d, n, stride=d) if d > 1 else pl.ds(c0, n)
            val = src_ref[rows, :]
            if scale is not None:
                val = val * scale
            dst_ref[r * L + c0:r * L + c0 + n, :] = val.astype(dst_ref.dtype)


def _from_classes(src_ref, dst_ref, d, L):
    n = min(L, COPY_ROWS)
    for r in range(d):
        for c0 in range(0, L, n):
            rows = pl.ds(r + c0 * d, n, stride=d) if d > 1 else pl.ds(c0, n)
            dst_ref[rows, :] = src_ref[r * L + c0:r * L + c0 + n, :].astype(dst_ref.dtype)


MLA_TQ, MLA_TK = 512, 512


def _causal_bias(tq, tk, shift):
    row = lax.broadcasted_iota(jnp.int32, (tq, tk), 0)
    col = lax.broadcasted_iota(jnp.int32, (tq, tk), 1)
    return jnp.where(row >= col + shift, 0.0, NEG)


def _mla_specs(S):
    heads = pl.BlockSpec((2, S, LANE), lambda b, j: (j, b, 0))
    pair = pl.BlockSpec((S, LANE), lambda b, j: (b, j))
    return heads, pair


def _mla_attn_fwd(q, k, v, B, S):
    tq = _tile(S, MLA_TQ)
    tk = _tile(tq, MLA_TK)
    nd = tq // tk
    scale = MLA_QK ** -0.5
    heads, pair = _mla_specs(S)

    def body(q_ref, k_ref, v_ref, o_ref, lse_ref):
        lo, lok = _lane_lo((tq, LANE)), _lane_lo((tk, LANE))
        diag = [_causal_bias(tq, tk, i * tk) for i in range(nd)]

        def block(g, _):
            row0 = pl.multiple_of(g * tq, tq)
            rows = pl.ds(row0, tq)
            qs = [q_ref[hh, rows, :] for hh in range(2)]

            one = jnp.ones((), BF16)

            def step(off, carries, bias):
                off = pl.multiple_of(off, tk)
                vt = v_ref[pl.ds(off, tk), :]
                vh = (jnp.where(lok, vt, one), jnp.where(lok, one, vt))
                out = []
                for hh, (m, acc) in enumerate(carries):
                    s = _mm_nt(qs[hh], k_ref[hh, pl.ds(off, tk), :]) * scale
                    if bias is not None:
                        s = s + bias
                    m_new = jnp.maximum(m, jnp.max(s, axis=-1, keepdims=True))
                    p = jnp.exp(s - m_new)
                    out.append((m_new, jnp.exp(m - m_new) * acc + _mm(p, vh[hh])))
                return tuple(out)

            init = (jnp.full((tq, 1), NEG, F32), jnp.zeros((tq, LANE), F32))
            carries = lax.fori_loop(0, g * nd, lambda i, c: step(i * tk, c, None), (init, init))
            for i in range(nd):
                carries = step(row0 + i * tk, carries, diag[i])
            (ma, acca), (mb, accb) = carries
            la, lb = pltpu.roll(acca, HALF, axis=1), pltpu.roll(accb, HALF, axis=1)
            o_ref[rows, :] = jnp.where(lo, acca / la, accb / lb)
            lse_ref[rows, :] = jnp.where(lo, ma + jnp.log(la), mb + jnp.log(lb))
            return 0

        lax.fori_loop(0, S // tq, block, 0)

    return pl.pallas_call(
        body, name="mla_attn_fwd", grid=(B, NPAIR), in_specs=[heads, heads, pair], out_specs=[pair, pair],
        out_shape=[jax.ShapeDtypeStruct((B * S, MLA_HEADS * MLA_V), F32)] * 2,
        compiler_params=_cp(),
    )(q, k, v)


DIL_UNROLL = 16


def _dil_geometry(gi, S):
    span, d = DIL_PATTERNS[gi]
    L = S // d
    t = _tile(L, 128)
    window = span // d
    back = min(-(-window // t) * t, L - t)
    return d, L, t, window, back


def _dil_specs(gi, S):
    qk = pl.BlockSpec((S, LANE), lambda b, j: (b, NPAIR * gi + j))
    pair = pl.BlockSpec((S, LANE), lambda b, j: (b, j))
    return qk, qk, pair


def _dil_bias(bias_ref, sl_ref, j, t, kw, back, window):
    row = lax.broadcasted_iota(jnp.int32, (2 * t, kw), 0)
    col = lax.broadcasted_iota(jnp.int32, (2 * t, kw), 1)
    second = row >= t
    slope = jnp.where(second, sl_ref[j, 1], sl_ref[j, 0])
    for n in range(bias_ref.shape[0]):
        dist = jnp.where(second, row - t, row) + n * back - col
        bias_ref[n] = jnp.where((dist >= 0) & (dist <= window), -slope * dist.astype(F32), NEG)


def _stack_heads(x, lo):
    zero = jnp.zeros((), x.dtype)
    return jnp.concatenate([jnp.where(lo, x, zero), jnp.where(lo, zero, x)], axis=0)


def _dil_attn_fwd(gi, slopes, qn, kn, proj, B, S):
    d, L, t, window, back = _dil_geometry(gi, S)
    kw, nq = back + t, L // t
    nbias = 2 if back else 1
    qk, vspec, pair = _dil_specs(gi, S)

    def body(sl_ref, q_ref, k_ref, v_ref, o_ref, lse_ref, qs, ks, vs, os_, ls, bias_ref):
        _to_classes(q_ref, qs, d, L, DIL_HEAD_DIM ** -0.5)
        _to_classes(k_ref, ks, d, L)
        _to_classes(v_ref, vs, d, L)
        _dil_bias(bias_ref, sl_ref, pl.program_id(1), t, kw, back, window)
        lo = _lane_lo((t, LANE))

        def block(g, _):
            qb = g % nq if d > 1 else g
            row0 = pl.multiple_of(g * t, t)
            rows = pl.ds(row0, t)
            early = qb * t < back
            keys = pl.ds(pl.multiple_of(jnp.where(early, row0 - qb * t, row0 - back), t), kw)
            s = _mm_nt(_stack_heads(qs[rows, :], lo), ks[keys, :]) + bias_ref[jnp.where(early, 0, nbias - 1)]
            m = jnp.max(s, axis=-1, keepdims=True)
            p = jnp.exp(s - m)
            l = jnp.sum(p, axis=-1, keepdims=True)
            o2 = _mm(p, vs[keys, :]) / l
            lse2 = m + jnp.log(l)
            os_[rows, :] = jnp.where(lo, o2[:t], o2[t:])
            ls[rows, :] = jnp.where(lo, lse2[:t], lse2[t:])
            return 0

        lax.fori_loop(0, d * nq, block, 0, unroll=DIL_UNROLL if d * nq % DIL_UNROLL == 0 else 1)
        _from_classes(os_, o_ref, d, L)
        _from_classes(ls, lse_ref, d, L)

    return pl.pallas_call(
        body, name=f"dil_attn_fwd_{gi}", grid=(B, NPAIR),
        in_specs=[pl.BlockSpec(memory_space=pltpu.SMEM), qk, qk, vspec], out_specs=[pair, pair],
        out_shape=[jax.ShapeDtypeStruct((B * S, DIL_WIDTH), F32)] * 2,
        scratch_shapes=[pltpu.VMEM((S, LANE), BF16)] * 3 + [pltpu.VMEM((S, LANE), F32)] * 2
                       + [pltpu.VMEM((nbias, 2 * t, kw), F32)],
        compiler_params=_cp(),
    )(slopes, qn, kn, proj)


def _mla_attn_bwd(q, k, v, do, lse, delta, B, S):
    T = B * S
    tq = _tile(S, MLA_TQ)
    tk = _tile(tq, MLA_TK)
    nd = tq // tk
    scale = MLA_QK ** -0.5
    heads, pair = _mla_specs(S)

    def body(q_ref, k_ref, v_ref, do_ref, lse_ref, dl_ref, dq_ref, dk_ref, dv_ref):
        dk_ref[...] = jnp.zeros_like(dk_ref)
        dv_ref[...] = jnp.zeros_like(dv_ref)
        lo = _lane_lo((tq, LANE))
        diag = [_causal_bias(tq, tk, i * tk) for i in range(nd)]

        def block(g, _):
            row0 = pl.multiple_of(g * tq, tq)
            rows = pl.ds(row0, tq)
            per_head = []
            for hh in range(2):
                sel = lo if hh == 0 else jnp.logical_not(lo)
                per_head.append((q_ref[hh, rows, :], jnp.where(sel, do_ref[rows, :], jnp.zeros((), BF16)),
                                 jnp.max(jnp.where(sel, lse_ref[rows, :], NEG), axis=-1, keepdims=True),
                                 jnp.max(jnp.where(sel, dl_ref[rows, :], NEG), axis=-1, keepdims=True)))

            def step(off, dq_accs, bias):
                cols = pl.ds(pl.multiple_of(off, tk), tk)
                vt = v_ref[cols, :]
                out, dv = [], None
                for hh, (qh, doh, lse_h, dl_h) in enumerate(per_head):
                    kh = k_ref[hh, cols, :]
                    s = _mm_nt(qh, kh) * scale
                    if bias is not None:
                        s = s + bias
                    p = jnp.exp(s - lse_h)
                    ds = (p * (_mm_nt(doh, vt) - dl_h)).astype(BF16)
                    dk_ref[hh, cols, :] += _mm_tn(ds, qh) * scale
                    part = _mm_tn(p, doh)
                    dv = part if dv is None else dv + part
                    out.append(dq_accs[hh] + _mm(ds, kh))
                dv_ref[cols, :] += dv
                return tuple(out)

            zero = jnp.zeros((tq, LANE), F32)
            dq_accs = lax.fori_loop(0, g * nd, lambda i, a: step(i * tk, a, None), (zero, zero))
            for i in range(nd):
                dq_accs = step(row0 + i * tk, dq_accs, diag[i])
            for hh in range(2):
                dq_ref[hh, rows, :] = dq_accs[hh] * scale
            return 0

        lax.fori_loop(0, S // tq, block, 0)

    return pl.pallas_call(
        body, name="mla_attn_bwd", grid=(B, NPAIR), in_specs=[heads, heads, pair, pair, pair, pair],
        out_specs=[heads, heads, pair],
        out_shape=[jax.ShapeDtypeStruct((MLA_HEADS, T, LANE), F32), jax.ShapeDtypeStruct((MLA_HEADS, T, LANE), F32),
                   jax.ShapeDtypeStruct((T, MLA_HEADS * MLA_V), F32)],
        compiler_params=_cp(),
    )(q, k, v, do, lse, delta)


def _dil_attn_bwd(gi, slopes, qn, kn, proj, do, lse, delta, through, B, S):
    d, L, t, window, back = _dil_geometry(gi, S)
    kw, nq = back + t, L // t
    nbias = 2 if back else 1
    scale = DIL_HEAD_DIM ** -0.5
    qk, vspec, pair = _dil_specs(gi, S)

    def body(*refs):
        refs = list(refs)
        sl_ref, q_ref, k_ref, v_ref, do_ref, lse_ref, dl_ref = refs[:7]
        dq_ref, dk_ref, dv_ref, qs, ks, vs, dos, lss, dls, dqs, dks, dvs, bias_ref = refs[-13:]
        _to_classes(q_ref, qs, d, L, scale)
        for src, dst in ((k_ref, ks), (v_ref, vs), (do_ref, dos), (lse_ref, lss), (dl_ref, dls)):
            _to_classes(src, dst, d, L)
        _dil_bias(bias_ref, sl_ref, pl.program_id(1), t, kw, back, window)
        dks[...] = jnp.zeros_like(dks)
        dvs[...] = jnp.zeros_like(dvs)
        lo = _lane_lo((t, LANE))

        def stats(ref, rows):
            x = ref[rows, :]
            return jnp.concatenate([jnp.max(jnp.where(lo, x, NEG), axis=-1, keepdims=True),
                                    jnp.max(jnp.where(lo, NEG, x), axis=-1, keepdims=True)], axis=0)

        def block(g, _):
            qb = g % nq if d > 1 else g
            row0 = pl.multiple_of(g * t, t)
            rows = pl.ds(row0, t)
            early = qb * t < back
            keys = pl.ds(pl.multiple_of(jnp.where(early, row0 - qb * t, row0 - back), t), kw)
            q2 = _stack_heads(qs[rows, :], lo)
            do2 = _stack_heads(dos[rows, :], lo)
            kt = ks[keys, :]
            s = _mm_nt(q2, kt) + bias_ref[jnp.where(early, 0, nbias - 1)]
            p = jnp.exp(s - stats(lss, rows))
            ds = (p * (_mm_nt(do2, vs[keys, :]) - stats(dls, rows))).astype(BF16)
            dq2 = _mm(ds, kt) * scale
            dqs[rows, :] = jnp.where(lo, dq2[:t], dq2[t:])
            dks[keys, :] += _mm_tn(ds, q2)
            dvs[keys, :] += _mm_tn(p, do2)
            return 0

        lax.fori_loop(0, d * nq, block, 0, unroll=DIL_UNROLL if d * nq % DIL_UNROLL == 0 else 1)
        for src, dst in ((dqs, dq_ref), (dks, dk_ref), (dvs, dv_ref)):
            _from_classes(src, dst, d, L)

    in_specs = [pl.BlockSpec(memory_space=pltpu.SMEM), qk, qk, vspec, pair, pair, pair]
    args = [slopes, qn, kn, proj, do, lse, delta]
    aliases = {}
    if through is not None:
        aliases = {len(args) + i: i for i in range(3)}
        in_specs = in_specs + [pl.BlockSpec(memory_space=pl.ANY)] * 3
        args = args + list(through)
    return pl.pallas_call(
        body, name=f"dil_attn_bwd_{gi}", grid=(B, NPAIR), in_specs=in_specs, out_specs=[qk, qk, qk],
        out_shape=[jax.ShapeDtypeStruct((B * S, DIL_QK), F32)] * 3,
        scratch_shapes=[pltpu.VMEM((S, LANE), BF16)] * 4 + [pltpu.VMEM((S, LANE), F32)] * 5
                       + [pltpu.VMEM((nbias, 2 * t, kw), F32)],
        input_output_aliases=aliases,
        compiler_params=_cp(),
    )(*args)


def _merge_proj_specs(ts):
    wide = lambda c0, w: pl.BlockSpec((ts, w), lambda i: (i, c0 * LANE // w))
    return [wide(CB_BZ, DIL_WIDTH), wide(CB_CZ, DIL_WIDTH)] + [wide(CB_GATE + 8 * i, D_MODEL) for i in range(3)]


def _merge_common(p_refs, bg_ref, ob_ref, og_refs, lse_refs):
    bz = p_refs[0][...].astype(F32)
    cz = p_refs[1][...].astype(F32)
    gates = [_sigmoid(p_refs[2 + i][...].astype(F32) + bg_ref[:, i * D_MODEL:(i + 1) * D_MODEL]) for i in range(3)]
    ob = ob_ref[...]
    lses = [r[...] for r in lse_refs]
    mx = jnp.maximum(jnp.maximum(lses[0], lses[1]), lses[2])
    es = [jnp.exp(v - mx) for v in lses]
    inv = 1.0 / (es[0] + es[1] + es[2])
    alphas = [e * inv for e in es]
    oc = alphas[0] * og_refs[0][...] + alphas[1] * og_refs[1][...] + alphas[2] * og_refs[2][...]
    return bz, cz, gates, ob, alphas, oc


def _merge_fwd(x, proj, b_gate, ya, ob, ogs, lses, woa, wob, woc, wo):
    T = x.shape[0]
    ts = _tile(T, 256)

    def body(x_ref, p0, p1, p2, p3, p4, bg_ref, ya_ref, ob_ref, og0, og1, og2, l0, l1, l2,
             woa_ref, wob_ref, woc_ref, wo_ref, out_ref):
        bz, cz, gates, obv, alphas, oc = _merge_common((p0, p1, p2, p3, p4), bg_ref, ob_ref, (og0, og1, og2),
                                                       (l0, l1, l2))
        yb = obv * _silu(bz)
        yc = oc * _silu(cz)
        merged = (gates[0] * _mm(ya_ref[...], woa_ref[...]) + gates[1] * _mm(yb, wob_ref[...])
                  + gates[2] * _mm(yc, woc_ref[...]))
        out_ref[...] = x_ref[...] + _mm(merged, wo_ref[...])

    def whole(r, c):
        return pl.BlockSpec((r, c), lambda i: (0, 0))

    tok = lambda w: pl.BlockSpec((ts, w), lambda i: (i, 0))
    return pl.pallas_call(
        body, name="merge_fwd", grid=(T // ts,),
        in_specs=[tok(D_MODEL)] + _merge_proj_specs(ts) + [whole(1, 3 * D_MODEL), tok(CONV_WIDTH)]
                 + [tok(DIL_WIDTH)] * 7 + [whole(CONV_WIDTH, D_MODEL)] * 3 + [whole(D_MODEL, D_MODEL)],
        out_specs=tok(D_MODEL),
        out_shape=jax.ShapeDtypeStruct((T, D_MODEL), F32),
        compiler_params=_cp(),
    )(x, *[proj] * 5, b_gate, ya, ob, *ogs, *lses, woa, wob, woc, wo)


def _merge_bwd(dout, proj, b_gate, ya, ob, ogs, lses, woa, wob, woc, wo):
    T = dout.shape[0]
    ts = _tile(T, 256)
    nt = T // ts

    def body(do_ref, p0, p1, p2, p3, p4, bg_ref, ya_ref, ob_ref, og0, og1, og2, l0, l1, l2,
             woa_ref, wob_ref, woc_ref, wo_ref,
             dp_ref, dya_ref, dob_ref, dlb_ref, dg0, dg1, dg2, dl0, dl1, dl2,
             mg_ref, dpa_ref, dpb_ref, dpc_ref, yb_ref, yc_ref, dbg_ref, st_bz, st_cz, st_gate, sems):
        step = pl.program_id(0)
        slot = step % 2

        def copies_of(s):
            return _put_copies([st_bz, st_cz, st_gate], dp_ref, sems, s % 2, pl.ds(pl.multiple_of(s * ts, ts), ts),
                               [CB_BZ * LANE, CB_CZ * LANE, CB_GATE * LANE])

        @pl.when(step >= 2)
        def _():
            for cp in copies_of(step - 2):
                cp.wait()

        bz, cz, gates, obv, alphas, oc = _merge_common((p0, p1, p2, p3, p4), bg_ref, ob_ref, (og0, og1, og2),
                                                       (l0, l1, l2))
        sb, sc = _silu(bz), _silu(cz)
        yb = obv * sb
        yc = oc * sc
        ps = [_mm(ya_ref[...], woa_ref[...]), _mm(yb, wob_ref[...]), _mm(yc, woc_ref[...])]
        mg_ref[...] = (gates[0] * ps[0] + gates[1] * ps[1] + gates[2] * ps[2]).astype(BF16)
        yb_ref[...] = yb.astype(BF16)
        yc_ref[...] = yc.astype(BF16)
        dm = _mm_nt(do_ref[...], wo_ref[...])
        dps = []
        first = pl.program_id(0) == 0
        for i, dref in enumerate((dpa_ref, dpb_ref, dpc_ref)):
            g = gates[i]
            dpi = (dm * g).astype(BF16)
            dref[...] = dpi
            dps.append(dpi)
            dgp = dm * ps[i] * g * (1.0 - g)
            st_gate[slot, :, i * D_MODEL:(i + 1) * D_MODEL] = dgp.astype(BF16)
            part = jnp.sum(dgp, axis=0, keepdims=True)

            @pl.when(first)
            def _():
                dbg_ref[:, i * D_MODEL:(i + 1) * D_MODEL] = part

            @pl.when(jnp.logical_not(first))
            def _():
                dbg_ref[:, i * D_MODEL:(i + 1) * D_MODEL] += part

        dya_ref[...] = _mm_nt(dps[0], woa_ref[...])
        dyb = _mm_nt(dps[1], wob_ref[...])
        dyc = _mm_nt(dps[2], woc_ref[...])
        st_bz[slot] = (dyb * obv * _dsilu(bz)).astype(BF16)
        st_cz[slot] = (dyc * oc * _dsilu(cz)).astype(BF16)
        for cp in copies_of(step):
            cp.start()
        dob = dyb * sb
        doc = dyc * sc
        dob_ref[...] = dob.astype(BF16)
        for c in range(NPAIR):
            cs = slice(c * LANE, (c + 1) * LANE)
            dlb_ref[:, cs] = _head_bcast_sum(dob[:, cs] * obv[:, cs])
            dd = _head_bcast_sum(doc[:, cs] * oc[:, cs])
            for a, dref, lref in zip(alphas, (dg0, dg1, dg2), (dl0, dl1, dl2)):
                dref[:, cs] = a[:, cs] * doc[:, cs]
                lref[:, cs] = a[:, cs] * dd

        @pl.when(step == nt - 1)
        def _():
            if nt >= 2:
                for cp in copies_of(step - 1):
                    cp.wait()
            for cp in copies_of(step):
                cp.wait()

    def whole(r, c):
        return pl.BlockSpec((r, c), lambda i: (0, 0))

    tok = lambda w: pl.BlockSpec((ts, w), lambda i: (i, 0))
    sd = jax.ShapeDtypeStruct
    W = DIL_WIDTH
    return pl.pallas_call(
        body, name="merge_bwd", grid=(nt,),
        in_specs=[tok(D_MODEL)] + _merge_proj_specs(ts) + [whole(1, 3 * D_MODEL), tok(CONV_WIDTH)] + [tok(W)] * 7
                 + [whole(CONV_WIDTH, D_MODEL)] * 3 + [whole(D_MODEL, D_MODEL)],
        out_specs=[pl.BlockSpec(memory_space=pl.ANY), tok(CONV_WIDTH), tok(W), tok(W)] + [tok(W)] * 6
                  + [tok(D_MODEL)] * 4 + [tok(W), tok(W), whole(1, 3 * D_MODEL)],
        out_shape=[sd((T, PP), BF16), sd((T, CONV_WIDTH), F32), sd((T, W), BF16), sd((T, W), F32)]
                  + [sd((T, W), F32)] * 6
                  + [sd((T, D_MODEL), BF16)] * 4 + [sd((T, W), BF16)] * 2 + [sd((1, 3 * D_MODEL), F32)],
        scratch_shapes=[pltpu.VMEM((2, ts, W), BF16), pltpu.VMEM((2, ts, W), BF16),
                        pltpu.VMEM((2, ts, 3 * D_MODEL), BF16), pltpu.SemaphoreType.DMA((2, 3))],
        compiler_params=_cp(),
    )(dout, *[proj] * 5, b_gate, ya, ob, *ogs, *lses, woa, wob, woc, wo)


def _loss_head(y, target):
    T = y.shape[0]
    ts = _tile(T, 512)

    def body(y_ref, t_ref, d_ref, l_ref):
        e = y_ref[...] - t_ref[...]
        d_ref[...] = e * (1.0 / D_MODEL)
        l_ref[...] = jnp.zeros((1, 8, LANE), F32) + jnp.sum(e * e)

    tok = pl.BlockSpec((ts, D_MODEL), lambda i: (i, 0))
    return pl.pallas_call(
        body, name="loss_head", grid=(T // ts,), in_specs=[tok, tok],
        out_specs=[tok, pl.BlockSpec((1, 8, LANE), lambda i: (i, 0, 0))],
        out_shape=[jax.ShapeDtypeStruct((T, D_MODEL), F32), jax.ShapeDtypeStruct((T // ts, 8, LANE), F32)],
        compiler_params=_cp(),
    )(y, target)


def _my_index():
    return 4 * lax.axis_index("x") + 2 * lax.axis_index("y") + lax.axis_index("c")


def _peers():
    x, y, c = (lax.axis_index(a) for a in AXES)
    out = []
    for kk in range(1, N_DEV):
        px = 1 - x if kk & 4 else x
        py = 1 - y if kk & 2 else y
        pc = 1 - c if kk & 1 else c
        out.append(((px, py, pc), 4 * px + 2 * py + pc))
    return out


def _exchange(arrays, name, gather):
    n = len(arrays)

    def body(*refs):
        srcs, outs = refs[:n], refs[n:2 * n]
        send_sems, recv_sems, local_sems = refs[2 * n:]
        me = _my_index()
        peers = _peers()
        started = []
        for a, (src, out) in enumerate(zip(srcs, outs)):
            mine = pltpu.make_async_copy(src if gather else src.at[me], out.at[me], local_sems.at[a])
            mine.start()
            started.append(mine)
        sends = []
        for i, (pos, idx) in enumerate(peers):
            for a, (src, out) in enumerate(zip(srcs, outs)):
                cp = pltpu.make_async_remote_copy(
                    src_ref=src if gather else src.at[idx], dst_ref=out.at[me], send_sem=send_sems.at[a, i],
                    recv_sem=recv_sems.at[a, i], device_id=pos, device_id_type=pl.DeviceIdType.MESH)
                cp.start()
                sends.append(cp)
        for i, (pos, idx) in enumerate(peers):
            for a, (src, out) in enumerate(zip(srcs, outs)):
                pltpu.make_async_remote_copy(
                    src_ref=src if gather else src.at[idx], dst_ref=out.at[idx], send_sem=send_sems.at[a, i],
                    recv_sem=recv_sems.at[a, i], device_id=pos, device_id_type=pl.DeviceIdType.MESH).wait_recv()
        for cp in sends:
            cp.wait_send()
        for mine in started:
            mine.wait()

    any_space = pl.BlockSpec(memory_space=pl.ANY)
    return pl.pallas_call(
        body, name=name, in_specs=[any_space] * n, out_specs=[any_space] * n,
        out_shape=[jax.ShapeDtypeStruct(((N_DEV,) + a.shape) if gather else a.shape, a.dtype) for a in arrays],
        scratch_shapes=[pltpu.SemaphoreType.DMA((n, N_DEV - 1)), pltpu.SemaphoreType.DMA((n, N_DEV - 1)),
                        pltpu.SemaphoreType.DMA((n,))],
    )(*arrays)


N_CHIP = 4


def _chip_places():
    x, y, c = (lax.axis_index(a) for a in AXES)
    return (x, y, c), (x, y, 1 - c), [(1 - x, y, c), (x, 1 - y, c), (1 - x, 1 - y, c)]


def _index_of(pos):
    return 4 * pos[0] + 2 * pos[1] + pos[2]


def _gather_two_level(arrays, name):
    n = len(arrays)

    def body(*refs):
        srcs, outs = refs[:n], refs[n:2 * n]
        send_sems, recv_sems, local_sems = refs[2 * n:]
        me, sibling, others = _chip_places()

        def copy(a, k, block, to, src=None):
            slot = outs[a].at[_index_of(block)]
            return pltpu.make_async_remote_copy(
                src_ref=slot if src is None else src, dst_ref=slot, send_sem=send_sems.at[7 * a + k],
                recv_sem=recv_sems.at[7 * a + k], device_id=to, device_id_type=pl.DeviceIdType.MESH)

        started = []
        for a, src in enumerate(srcs):
            mine = pltpu.make_async_copy(src, outs[a].at[_index_of(me)], local_sems.at[a])
            mine.start()
            started.append(mine)
        sends = []
        for a, src in enumerate(srcs):
            sends.append(copy(a, 0, me, sibling, src))
            sends += [copy(a, 1 + j, me, chip, src) for j, chip in enumerate(others)]
        for cp in sends:
            cp.start()
        for j, chip in enumerate(others):
            for a in range(n):
                copy(a, 1 + j, chip, me).wait_recv()
                fwd = copy(a, 4 + j, chip, sibling)
                fwd.start()
                sends.append(fwd)
        for a in range(n):
            copy(a, 0, sibling, me).wait_recv()
            for j, chip in enumerate(others):
                copy(a, 4 + j, (chip[0], chip[1], sibling[2]), me).wait_recv()
        for cp in sends:
            cp.wait_send()
        for mine in started:
            mine.wait()

    any_space = pl.BlockSpec(memory_space=pl.ANY)
    return pl.pallas_call(
        body, name=name, in_specs=[any_space] * n, out_specs=[any_space] * n,
        out_shape=[jax.ShapeDtypeStruct((N_DEV,) + a.shape, a.dtype) for a in arrays],
        scratch_shapes=[pltpu.SemaphoreType.DMA((7 * n,)), pltpu.SemaphoreType.DMA((7 * n,)),
                        pltpu.SemaphoreType.DMA((n,))],
    )(*arrays)


def _sibling_swap(arrays, name):
    n = len(arrays)

    def body(*refs):
        srcs, outs = refs[:n], refs[n:2 * n]
        send_sems, recv_sems = refs[2 * n:]
        (x, y, c), sibling, _ = _chip_places()
        sends = []
        for a, (src, out) in enumerate(zip(srcs, outs)):
            for q in range(N_CHIP):
                def copy(core, a=a, q=q, src=src, out=out):
                    return pltpu.make_async_remote_copy(
                        src_ref=src.at[2 * q + core], dst_ref=out.at[q], send_sem=send_sems.at[N_CHIP * a + q],
                        recv_sem=recv_sems.at[N_CHIP * a + q], device_id=sibling, device_id_type=pl.DeviceIdType.MESH)
                mine = copy(1 - c)
                mine.start()
                sends.append((mine, copy(c)))
        for mine, arrival in sends:
            arrival.wait_recv()
            mine.wait_send()

    any_space = pl.BlockSpec(memory_space=pl.ANY)
    return pl.pallas_call(
        body, name=name, in_specs=[any_space] * n, out_specs=[any_space] * n,
        out_shape=[jax.ShapeDtypeStruct((N_CHIP,) + a.shape[1:], a.dtype) for a in arrays],
        scratch_shapes=[pltpu.SemaphoreType.DMA((N_CHIP * n,)), pltpu.SemaphoreType.DMA((N_CHIP * n,))],
    )(*arrays)


def _chip_pair_sum(part, got, name):
    R, C = part.shape[1:]
    tr = R
    while tr * C * part.dtype.itemsize > REDUCE_BLOCK_BYTES // 4 and tr % 32 == 0:
        tr //= 2
    c = lax.axis_index("c")

    def body(c_ref, p_ref, g_ref, o_ref):
        del c_ref
        o_ref[...] = (p_ref[...].astype(F32) + g_ref[...].astype(F32)).astype(o_ref.dtype)

    return pl.pallas_call(
        body, name=name, grid_spec=pltpu.PrefetchScalarGridSpec(
            num_scalar_prefetch=1, grid=(N_CHIP, R // tr),
            in_specs=[pl.BlockSpec((None, tr, C), lambda q, i, cr: (2 * q + cr[0], i, 0)),
                      pl.BlockSpec((None, tr, C), lambda q, i, cr: (q, i, 0))],
            out_specs=pl.BlockSpec((None, tr, C), lambda q, i, cr: (q, i, 0))),
        out_shape=jax.ShapeDtypeStruct((N_CHIP, R, C), part.dtype),
        compiler_params=_cp(),
    )(jnp.reshape(c, (1,)).astype(jnp.int32), part, got)


def _chip_exchange(arrays, name):
    n = len(arrays)

    def body(*refs):
        srcs, outs = refs[:n], refs[n:2 * n]
        send_sems, recv_sems, local_sems = refs[2 * n:]
        (x, y, c), _, others = _chip_places()
        mychip = 2 * x + y
        started, sends = [], []
        for a, (src, out) in enumerate(zip(srcs, outs)):
            mine = pltpu.make_async_copy(src.at[mychip], out.at[mychip], local_sems.at[a])
            mine.start()
            started.append(mine)
        for j, chip in enumerate(others):
            q = 2 * chip[0] + chip[1]
            for a, (src, out) in enumerate(zip(srcs, outs)):
                def copy(slot, a=a, j=j, q=q, chip=chip, src=src, out=out):
                    return pltpu.make_async_remote_copy(
                        src_ref=src.at[q], dst_ref=out.at[slot], send_sem=send_sems.at[3 * a + j],
                        recv_sem=recv_sems.at[3 * a + j], device_id=chip, device_id_type=pl.DeviceIdType.MESH)
                mine = copy(mychip)
                mine.start()
                sends.append((mine, copy(q)))
        for mine, arrival in sends:
            arrival.wait_recv()
        for mine, arrival in sends:
            mine.wait_send()
        for mine in started:
            mine.wait()

    any_space = pl.BlockSpec(memory_space=pl.ANY)
    return pl.pallas_call(
        body, name=name, in_specs=[any_space] * n, out_specs=[any_space] * n,
        out_shape=[jax.ShapeDtypeStruct(a.shape, a.dtype) for a in arrays],
        scratch_shapes=[pltpu.SemaphoreType.DMA((3 * n,)), pltpu.SemaphoreType.DMA((3 * n,)),
                        pltpu.SemaphoreType.DMA((n,))],
    )(*arrays)


def _remote_copies(srcs, lands, send_sems, recv_sems, gather):
    me = _my_index()
    out = []
    for i, (pos, idx) in enumerate(_peers()):
        for a, (src, land) in enumerate(zip(srcs, lands)):
            def copy(slot, a=a, src=src, land=land, i=i, pos=pos, idx=idx):
                return pltpu.make_async_remote_copy(
                    src_ref=src if gather else src.at[idx], dst_ref=land.at[slot],
                    send_sem=send_sems.at[a * (N_DEV - 1) + i], recv_sem=recv_sems.at[a * (N_DEV - 1) + i],
                    device_id=pos, device_id_type=pl.DeviceIdType.MESH)
            out.append((copy(me), copy(idx)))
    return out


def _exchange_start(arrays, name, gather):
    n = len(arrays)
    hbm = pl.BlockSpec(memory_space=pltpu.HBM)
    sem = pl.BlockSpec(memory_space=pltpu.SEMAPHORE)
    lands = [lax.empty(((N_DEV,) + a.shape) if gather else a.shape, a.dtype) for a in arrays]

    def body(*refs):
        srcs, lands_ = refs[:n], refs[n:2 * n]
        send_sems, recv_sems = refs[2 * n:2 * n + 2]
        for mine, _ in _remote_copies(srcs, lands_, send_sems, recv_sems, gather):
            mine.start()
        refs[-1][...] = jnp.zeros_like(refs[-1])

    sems = pltpu.SemaphoreType.DMA((n * (N_DEV - 1),))
    buffers = [pltpu.HBM(a.shape, a.dtype) for a in list(arrays) + lands]
    res = pl.pallas_call(
        body, name=name, in_specs=[hbm] * (2 * n), out_specs=[sem, sem] + [hbm] * (2 * n) + [pl.BlockSpec(memory_space=pltpu.VMEM)],
        out_shape=[sems, sems] + buffers + [jax.ShapeDtypeStruct((8, LANE), F32)],
        input_output_aliases={i: 2 + i for i in range(2 * n)},
        compiler_params=pltpu.CompilerParams(has_side_effects=pltpu.SideEffectType.DATAFLOW_SIDE_EFFECTING),
    )(*[pltpu.with_memory_space_constraint(a, pltpu.HBM) for a in list(arrays) + lands])
    return (res[0], res[1], res[2:2 + n], res[2 + n:2 + 2 * n]), res[-1]


def _exchange_wait(handle, after, name, gather):
    send_sems, recv_sems, srcs, lands = handle
    n = len(srcs)
    hbm = pl.BlockSpec(memory_space=pltpu.HBM)
    sem = pl.BlockSpec(memory_space=pltpu.SEMAPHORE)

    def body(*refs):
        for mine, arrival in _remote_copies(refs[:n], refs[n:2 * n], refs[2 * n], refs[2 * n + 1], gather):
            mine.wait_send()
            arrival.wait_recv()

    res = pl.pallas_call(
        body, name=name, in_specs=[hbm] * (2 * n) + [sem, sem, pl.BlockSpec(memory_space=pl.ANY)],
        out_specs=[hbm] * (2 * n), out_shape=[pltpu.HBM(a.shape, a.dtype) for a in list(srcs) + list(lands)],
        input_output_aliases={i: i for i in range(2 * n)},
        compiler_params=pltpu.CompilerParams(has_side_effects=pltpu.SideEffectType.DATAFLOW_SIDE_EFFECTING),
    )(*srcs, *lands, send_sems, recv_sems, after)
    return res[n:]


def _own_slot(land, mine):
    return lax.dynamic_update_slice(land, mine, (_my_index(),) + (0,) * (land.ndim - 1))


def _adamw(w, g, m, v):
    m = ADAM_B1 * m + (1.0 - ADAM_B1) * g
    v = ADAM_B2 * v + (1.0 - ADAM_B2) * (g * g)
    m_hat = m / (1.0 - ADAM_B1 ** ADAM_STEP)
    v_hat = v / (1.0 - ADAM_B2 ** ADAM_STEP)
    delta = -ADAM_LR * (m_hat / (jnp.sqrt(v_hat) + ADAM_EPS) + ADAM_WD * w)
    return delta, m, v


def _reduce_adamw(parts, w, m, v, name):
    nparts = len(parts)
    R, C = parts[0].shape[1:]
    tr = R
    while N_DEV * tr * C * parts[0].dtype.itemsize > REDUCE_BLOCK_BYTES and tr % 32 == 0:
        tr //= 2
    steps = R // tr

    def body(*refs):
        w_ref, m_ref, v_ref, g_ref, d_ref, nm_ref, nv_ref = refs[nparts:]
        for k, p_ref in enumerate(refs[:nparts]):
            @pl.when(pl.program_id(0) // steps == k)
            def _():
                g = p_ref[0].astype(F32)
                for s in range(1, p_ref.shape[0]):
                    g = g + p_ref[s].astype(F32)
                g_ref[...] = g
                d_ref[...], nm_ref[...], nv_ref[...] = _adamw(w_ref[...], g, m_ref[...], v_ref[...])

    def part_spec(k):
        return pl.BlockSpec((parts[k].shape[0], tr, C), lambda i: (0, jnp.clip(i - k * steps, 0, steps - 1), 0))

    row = pl.BlockSpec((tr, C), lambda i: (i, 0))
    return pl.pallas_call(
        body, name=name, grid=(nparts * steps,),
        in_specs=[part_spec(k) for k in range(nparts)] + [row, row, row],
        out_specs=[row] * 4, out_shape=[jax.ShapeDtypeStruct((nparts * R, C), F32)] * 4,
        compiler_params=_cp(),
    )(*parts, w, m, v)


BIG = ("w_in", "w_uq", "w_ukv", "w_out_a", "w_out_b", "w_out_c", "w_o")
SMALL = ("norm_g", "b_gate", "conv_w", "conv_b", "q_a_norm_g", "kv_a_norm_g", "mla_q_norm_g", "mla_k_norm_g",
         "dil_q_norm_g", "dil_k_norm_g")
PACK_ROWS = 128
REDUCE_BLOCK_BYTES = 6 * 1024 * 1024


def _pack_local(tensors):
    flat = jnp.concatenate([t.reshape(-1) for t in tensors])
    pad = (-flat.shape[0]) % (PACK_ROWS * LANE)
    return jnp.concatenate([flat, jnp.zeros((pad,), flat.dtype)]).reshape(-1, LANE)


def _unpack_local(rows, like):
    flat = rows.reshape(-1)
    out, off = [], 0
    for t in like:
        out.append(flat[off:off + t.size].reshape(t.shape))
        off += t.size
    return out


def _cols_to_slots(a):
    k = a.shape[0]
    return a.reshape(k, N_DEV, -1).transpose(1, 0, 2)


def _slots_to_cols(s):
    return s.transpose(1, 0, 2).reshape(s.shape[1], -1)


def _rope_tables(S):
    inv = ROPE_THETA ** (-jnp.arange(0, MLA_ROPE, 2, dtype=F32) / MLA_ROPE)
    ang = jnp.arange(S, dtype=F32)[:, None] * inv[None, :]
    cos, sin = jnp.cos(ang), jnp.sin(ang)
    one = jnp.ones((S, MLA_NOPE), F32)
    z16, z32, z64 = (jnp.zeros((S, n), F32) for n in (16, 32, 64))
    cosp = jnp.concatenate([one, cos, cos, jnp.ones((S, 32), F32)], axis=1)
    sa = jnp.concatenate([z64, -sin, z16, z32], axis=1)
    sb = jnp.concatenate([z64, z16, sin, z32], axis=1)
    return cosp, sa, sb


def _alibi_slopes():
    n = DIL_GROUPS * DIL_HEADS
    m = 2.0 ** (-8.0 * jnp.arange(1, n + 1, dtype=F32) / n)
    return m.reshape(DIL_GROUPS, NPAIR, 2)


def _pad_slots(s):
    n, k, c = s.shape
    return _slots_to_cols(jnp.concatenate([s, jnp.zeros((n, k, LANE - c), s.dtype)], axis=2))


def _layer_params(gw, small, l):
    p = {}
    p["wp"] = _pad_columns(gw["w_in"])
    p["norm_g"] = small["norm_g"][l][None]
    p["b_gate"] = small["b_gate"][l][None]
    p["conv_w"] = gw["conv_w"].transpose(1, 0, 2).reshape(CONV_K, CONV_WIDTH)
    p["conv_b"] = small["conv_b"][l][None]
    p["gq"] = small["q_a_norm_g"][l][None]
    p["gkv"] = small["kv_a_norm_g"][l][None]
    p["wuqp"] = _pad_slots(gw["w_uq"])
    kv = gw["w_ukv"]
    p["wkp"] = _pad_slots(kv[:, :, :MLA_NOPE])
    p["wv"] = kv[:, :, MLA_NOPE:].transpose(1, 0, 2).reshape(MLA_KV_LORA, MLA_HEADS * MLA_V)
    zpad = jnp.zeros((1, LANE - MLA_QK), F32)
    p["gmq"] = jnp.concatenate([small["mla_q_norm_g"][l][None], zpad], axis=1)
    p["gmk"] = jnp.concatenate([small["mla_k_norm_g"][l][None], zpad], axis=1)
    tile = lambda g: jnp.broadcast_to(g[:, None, :], (DIL_GROUPS, DIL_HEADS, DIL_HEAD_DIM)).reshape(1, DIL_QK)
    p["gdq"] = tile(small["dil_q_norm_g"][l])
    p["gdk"] = tile(small["dil_k_norm_g"][l])
    p["woa"], p["wob"], p["woc"] = (_slots_to_cols(gw[n]) for n in ("w_out_a", "w_out_b", "w_out_c"))
    p["wo"] = gw["w_o"].reshape(D_MODEL, D_MODEL)
    return p


def _layer_fwd(x, p, tabs, slopes, B, S):
    proj, ht = _inproj_fwd(x, p["norm_g"], p["wp"])
    ya = _mixa_fwd(proj, p["conv_w"], p["conv_b"], B, S)
    q, k, v = _mla_prep_fwd(proj, p["gq"], p["gkv"], p["wuqp"], p["wkp"], p["wv"], p["gmq"], p["gmk"], *tabs, S)
    ob, lse_b = _mla_attn_fwd(q, k, v, B, S)
    qn, kn, vn = _dil_prep_fwd(proj, p["gdq"], p["gdk"])
    ogs, lses = [], []
    for gi in range(DIL_GROUPS):
        o, lse = _dil_attn_fwd(gi, slopes[gi], qn, kn, vn, B, S)
        ogs.append(o)
        lses.append(lse)
    out = _merge_fwd(x, proj, p["b_gate"], ya, ob, ogs, lses, p["woa"], p["wob"], p["woc"], p["wo"])
    saved = dict(x=x, proj=proj, ht=ht, ya=ya, q=q, k=k, v=v, ob=ob, lse_b=lse_b, qn=qn, kn=kn, vn=vn, ogs=ogs, lses=lses)
    return out, saved


def _layer_bwd(dout, sv, p, tabs, slopes, B, S):
    proj = sv["proj"]
    (dproj, dya, dob, dlb, dg0, dg1, dg2, dl0, dl1, dl2, merged, dpa, dpb, dpc, yb, yc, dbg) = _merge_bwd(
        dout, proj, p["b_gate"], sv["ya"], sv["ob"], sv["ogs"], sv["lses"], p["woa"], p["wob"], p["woc"], p["wo"])
    g = {}
    g["w_o"] = _matmul_tn(merged, dout, "dw_o").reshape(N_DEV, D_MODEL // N_DEV, D_MODEL)
    g["w_out_a"] = _cols_to_slots(_matmul_tn(sv["ya"], dpa, "dw_out_a"))
    g["w_out_b"] = _cols_to_slots(_matmul_tn(yb, dpb, "dw_out_b"))
    g["w_out_c"] = _cols_to_slots(_matmul_tn(yc, dpc, "dw_out_c"))
    g["b_gate"] = dbg[0]
    dproj, st = _mixa_bwd(dproj, dya, proj, p["conv_w"], p["conv_b"], B, S)
    g["conv_w"] = st[0:CONV_K]
    g["conv_b"] = st[CONV_K]
    dq, dk, dv = _mla_attn_bwd(sv["q"], sv["k"], sv["v"], dob, sv["lse_b"], dlb, B, S)
    dproj, dwuqp, dwkp, dwv, dgq, dgkv, dgmq, dgmk = _mla_prep_bwd(
        dproj, dq, dk, dv, proj, p["gq"], p["gkv"], p["wuqp"], p["wkp"], p["wv"], p["gmq"], p["gmk"], *tabs, S)
    g["w_uq"] = _cols_to_slots(dwuqp)[:, :, :MLA_QK]
    g["w_ukv"] = jnp.concatenate([_cols_to_slots(dwkp)[:, :, :MLA_NOPE], _cols_to_slots(dwv)], axis=2)
    g["q_a_norm_g"], g["kv_a_norm_g"] = dgq[0], dgkv[0]
    g["mla_q_norm_g"], g["mla_k_norm_g"] = dgmq[0, :MLA_QK], dgmk[0, :MLA_QK]
    dqkv = None
    for gi, (dog, dlg) in enumerate(((dg0, dl0), (dg1, dl1), (dg2, dl2))):
        dqkv = _dil_attn_bwd(gi, slopes[gi], sv["qn"], sv["kn"], sv["vn"], dog, sv["lses"][gi], dlg, dqkv, B, S)
    dproj, dgdq, dgdk = _dil_prep_bwd(dproj, *dqkv, proj, p["gdq"], p["gdk"])
    g["dil_q_norm_g"] = dgdq.reshape(DIL_GROUPS, DIL_HEADS, DIL_HEAD_DIM).sum(axis=1)
    g["dil_k_norm_g"] = dgdk.reshape(DIL_GROUPS, DIL_HEADS, DIL_HEAD_DIM).sum(axis=1)
    g["w_in"] = _unpad_columns(_matmul_nn(sv["ht"], dproj, "dw_in"))
    dx, dng = _inproj_bwd_x(dproj, p["wp"], sv["x"], p["norm_g"], dout)
    g["norm_g"] = dng[0]
    return dx, g


def _after(token, a):
    return a if token is None else a + token[0:1, 0:1]


def _local_step(x, target, small, B, S, weights_of, grads_out):
    tabs = _rope_tables(S)
    sl = _alibi_slopes()
    slopes = [sl[gi] * float(DIL_PATTERNS[gi][1]) for gi in range(DIL_GROUPS)]
    params, saved = [], []
    for l in range(DEPTH):
        gw, token = weights_of(l, x)
        p = _layer_params(gw, small, l)
        p["norm_g"] = _after(token, p["norm_g"])
        x, sv = _layer_fwd(x, p, tabs, slopes, B, S)
        params.append(p)
        saved.append(sv)
    dout, lparts = _loss_head(x, target)
    sq = jnp.sum(lparts[:, 0, 0])
    token = None
    for l in reversed(range(DEPTH)):
        p = dict(params[l], b_gate=_after(token, params[l]["b_gate"]))
        dout, g = _layer_bwd(dout, saved[l], p, tabs, slopes, B, S)
        token = grads_out(l, g, dout)
    return sq, dout


def kernel(x, norm_g, w_in, b_gate, conv_w, conv_b, q_a_norm_g, w_uq, kv_a_norm_g, w_ukv, mla_q_norm_g, mla_k_norm_g, dil_q_norm_g, dil_k_norm_g, w_out_a, w_out_b, w_out_c, w_o, loss_target, m_norm_g, m_w_in, m_b_gate, m_conv_w, m_conv_b, m_q_a_norm_g, m_w_uq, m_kv_a_norm_g, m_w_ukv, m_mla_q_norm_g, m_mla_k_norm_g, m_dil_q_norm_g, m_dil_k_norm_g, m_w_out_a, m_w_out_b, m_w_out_c, m_w_o, v_norm_g, v_w_in, v_b_gate, v_conv_w, v_conv_b, v_q_a_norm_g, v_w_uq, v_kv_a_norm_g, v_w_ukv, v_mla_q_norm_g, v_mla_k_norm_g, v_dil_q_norm_g, v_dil_k_norm_g, v_w_out_a, v_w_out_b, v_w_out_c, v_w_o):
    names = ("norm_g", "w_in", "b_gate", "conv_w", "conv_b", "q_a_norm_g", "w_uq", "kv_a_norm_g", "w_ukv",
             "mla_q_norm_g", "mla_k_norm_g", "dil_q_norm_g", "dil_k_norm_g", "w_out_a", "w_out_b", "w_out_c", "w_o")
    w = dict(zip(names, (norm_g, w_in, b_gate, conv_w, conv_b, q_a_norm_g, w_uq, kv_a_norm_g, w_ukv, mla_q_norm_g,
                         mla_k_norm_g, dil_q_norm_g, dil_k_norm_g, w_out_a, w_out_b, w_out_c, w_o)))
    m = dict(zip(names, (m_norm_g, m_w_in, m_b_gate, m_conv_w, m_conv_b, m_q_a_norm_g, m_w_uq, m_kv_a_norm_g, m_w_ukv,
                         m_mla_q_norm_g, m_mla_k_norm_g, m_dil_q_norm_g, m_dil_k_norm_g, m_w_out_a, m_w_out_b,
                         m_w_out_c, m_w_o)))
    v = dict(zip(names, (v_norm_g, v_w_in, v_b_gate, v_conv_w, v_conv_b, v_q_a_norm_g, v_w_uq, v_kv_a_norm_g, v_w_ukv,
                         v_mla_q_norm_g, v_mla_k_norm_g, v_dil_q_norm_g, v_dil_k_norm_g, v_w_out_a, v_w_out_b,
                         v_w_out_c, v_w_o)))
    B, S, _ = x.shape
    me = _my_index()
    cshard = CONV_WIDTH // N_DEV

    shards = [[w[n][l].astype(BF16) for n in BIG] for l in range(DEPTH)]
    state = {}

    def weights_of(l, after):
        if l == 0:
            got = _gather_two_level(shards[0] + [conv_w], "all_gather_weights_0")
            state["gather"], token = _exchange_start(shards[1], "all_gather_weights_1_start", gather=True)
            state["conv_w"] = got[-1]
        else:
            landed = _exchange_wait(state["gather"], after, "all_gather_weights_1_wait", gather=True)
            got, token = [_own_slot(a, s[None]) for a, s in zip(landed, shards[1])], None
        gw = dict(zip(BIG, got))
        gw["conv_w"] = state["conv_w"][:, l]
        return gw, token

    recv, small_parts = {}, {}

    def grads_out(l, g, after):
        small_parts[l] = [g[n] for n in SMALL]
        send = [g[n].astype(BF16) for n in BIG]
        if l == DEPTH - 1:
            state["scatter"], token = _exchange_start(send, "exchange_weight_grads_1_start", gather=False)
            state["sent"] = send
            return token
        landed = _exchange_wait(state["scatter"], after, "exchange_weight_grads_1_wait", gather=False)
        mine = [lax.dynamic_slice_in_dim(s, me, 1, axis=0) for s in state["sent"]]
        recv[DEPTH - 1] = [_own_slot(a, s) for a, s in zip(landed, mine)]
        swapped = _sibling_swap(send, "exchange_weight_grads_0_sibling")
        sums = [_chip_pair_sum(s, t, "chip_pair_sum_" + n) for n, s, t in zip(BIG, send, swapped)]
        recv[l] = _chip_exchange(sums, "exchange_weight_grads_0")
        return None

    sq, grad_x = _local_step(x.reshape(B * S, D_MODEL), loss_target.reshape(B * S, D_MODEL), w, B, S,
                             weights_of, grads_out)
    loss = lax.psum(sq * (0.5 / D_MODEL), AXES)

    res = {}
    for i, n in enumerate(BIG):
        rows = lambda a: a.reshape(-1, a.shape[-1])
        outs = _reduce_adamw([recv[l][i] for l in range(DEPTH)], rows(w[n]), rows(m[n]), rows(v[n]),
                             "reduce_adamw_" + n)
        res[n] = tuple(a.reshape(w[n].shape) for a in outs)
    part = {n: jnp.stack([small_parts[l][i] for l in range(DEPTH)]) for i, n in enumerate(SMALL)}

    def widen(t):
        return lax.dynamic_update_slice(jnp.zeros((DEPTH, CONV_K, CONV_WIDTH), F32), t, (0, 0, me * cshard))

    small_like = [part[n] for n in SMALL]
    pick = lambda d: [widen(d[n]) if n == "conv_w" else d[n] for n in SMALL]
    parts, = _exchange([_pack_local(small_like)], "all_gather_small_grads", gather=True)
    gs, ds, ms, vs = _reduce_adamw([parts], _pack_local(pick(w)), _pack_local(pick(m)), _pack_local(pick(v)),
                                   "reduce_adamw_small")
    for n, t in zip(SMALL, zip(*(_unpack_local(a, small_like) for a in (gs, ds, ms, vs)))):
        if n == "conv_w":
            t = tuple(lax.dynamic_slice(a, (0, 0, me * cshard), (DEPTH, CONV_K, cshard)) for a in t)
        res[n] = t

    out = [loss, grad_x.reshape(B, S, D_MODEL)]
    for i in range(4):
        out += [res[n][i] for n in names]
    return tuple(out)
```

```python
import jax
import jax.numpy as jnp
from jax import lax
from jax.experimental import pallas as pl
from jax.experimental.pallas import tpu as pltpu

F32 = jnp.float32
BF16 = jnp.bfloat16

D_MODEL = 1024
DEPTH = 2
CONV_WIDTH = 512
CONV_K = 3
MLA_HEADS = 8
MLA_Q_LORA = 256
MLA_KV_LORA = 128
MLA_NOPE = 64
MLA_ROPE = 32
MLA_V = 64
MLA_QK = MLA_NOPE + MLA_ROPE
ROPE_THETA = 10000.0
DIL_PATTERNS = ((128, 1), (512, 4), (2048, 16))
DIL_GROUPS = 3
DIL_HEADS = 8
DIL_HEAD_DIM = 64
DIL_WIDTH = DIL_HEADS * DIL_HEAD_DIM
DIL_QK = DIL_GROUPS * DIL_WIDTH
EPS = 1e-6
N_IN = 11168

ADAM_LR = 0.001
ADAM_B1 = 0.9
ADAM_B2 = 0.999
ADAM_EPS = 1e-08
ADAM_WD = 0.01
ADAM_STEP = 10

N_DEV = 8
AXES = ("x", "y", "c")
LANE = 128
HALF = 64
NPAIR = 4

CB_AB, CB_AC, CB_AX, CB_AZ = 0, 4, 8, 12
CB_CQ, CB_CKV, CB_KPE = 16, 18, 19
CB_BZ = 20
CB_DQ, CB_DK, CB_DV = 24, 36, 48
CB_CZ, CB_GATE = 60, 64
NCB = 88
PP = NCB * LANE
KPE_END = CB_KPE * LANE + MLA_ROPE
SHARD_COLS = N_IN // N_DEV
NEG = -1e30
VMEM_LIMIT = 56 * 1024 * 1024


def _pad_columns(shards):
    parts = []
    for p in range(N_DEV):
        cut = min(max(KPE_END - p * SHARD_COLS, 0), SHARD_COLS)
        if 0 < cut < SHARD_COLS:
            parts += [shards[p, :, :cut], jnp.zeros((shards.shape[1], LANE - MLA_ROPE), shards.dtype), shards[p, :, cut:]]
        else:
            parts.append(shards[p])
    return jnp.concatenate(parts, axis=1)


def _unpad_columns(wp):
    def columns(a, b):
        gap = LANE - MLA_ROPE
        if b <= KPE_END:
            return wp[:, a:b]
        if a >= KPE_END:
            return wp[:, a + gap:b + gap]
        return jnp.concatenate([wp[:, a:KPE_END], wp[:, KPE_END + gap:b + gap]], axis=1)

    return jnp.stack([columns(p * SHARD_COLS, (p + 1) * SHARD_COLS) for p in range(N_DEV)])


def _put_copies(stages, dst_ref, sems, slot, rows, cols):
    return [pltpu.make_async_copy(st.at[slot], dst_ref.at[rows, pl.ds(c0, st.shape[-1])], sems.at[slot, k])
            for k, (st, c0) in enumerate(zip(stages, cols))]


def _put_pipeline(step, nsteps, copies_of, fill):
    @pl.when(step >= 2)
    def _():
        for cp in copies_of(step - 2):
            cp.wait()

    fill(step % 2)
    for cp in copies_of(step):
        cp.start()

    @pl.when(step == nsteps - 1)
    def _():
        if nsteps >= 2:
            for cp in copies_of(step - 1):
                cp.wait()
        for cp in copies_of(step):
            cp.wait()


def _cp():
    return pltpu.CompilerParams(vmem_limit_bytes=VMEM_LIMIT)


def _rstd(x, n):
    return lax.rsqrt(jnp.sum(x * x, axis=-1, keepdims=True) * (1.0 / n) + EPS)


def _sigmoid(z):
    return 1.0 / (1.0 + jnp.exp(-z))


def _silu(z):
    return z * _sigmoid(z)


def _dsilu(z):
    s = _sigmoid(z)
    return s * (1.0 + z * (1.0 - s))


def _mm(a, b):
    return jnp.dot(a.astype(BF16), b.astype(BF16), preferred_element_type=F32)


def _mm_nt(a, b):
    return lax.dot_general(a.astype(BF16), b.astype(BF16), (((1,), (1,)), ((), ())), preferred_element_type=F32)


def _mm_tn(a, b):
    return lax.dot_general(a.astype(BF16), b.astype(BF16), (((0,), (0,)), ((), ())), preferred_element_type=F32)


def _lane_lo(shape):
    return lax.broadcasted_iota(jnp.int32, shape, len(shape) - 1) < HALF


def _head_bcast_sum(x, terms=3):
    w = x.shape[-1]
    same = (lax.broadcasted_iota(jnp.int32, (w, w), 0) // HALF) == (lax.broadcasted_iota(jnp.int32, (w, w), 1) // HALF)
    ones = jnp.where(same, 1.0, 0.0).astype(jnp.bfloat16)
    total = None
    for _ in range(terms):
        term = x.astype(jnp.bfloat16)
        x = x - term.astype(F32)
        part = jnp.dot(term, ones, preferred_element_type=F32)
        total = part if total is None else total + part
    return total


def _rope(t, cos, sa, sb):
    return t * cos + pltpu.roll(t, LANE - 16, axis=1) * sa + pltpu.roll(t, 16, axis=1) * sb


def _rope_t(d, cos, sa, sb):
    return d * cos + pltpu.roll(d * sa, 16, axis=1) + pltpu.roll(d * sb, LANE - 16, axis=1)


def _shift_down(u, k):
    rows = lax.broadcasted_iota(jnp.int32, u.shape, 0)
    return jnp.where(rows >= k, pltpu.roll(u, k, axis=0), 0.0)


def _shift_up(u, k):
    n = u.shape[0]
    rows = lax.broadcasted_iota(jnp.int32, u.shape, 0)
    return jnp.where(rows < n - k, pltpu.roll(u, n - k, axis=0), 0.0)


def _tile(n, want):
    t = min(n, want)
    assert n % t == 0, (n, want)
    return t


def _inproj_fwd(x, g, wp):
    T = x.shape[0]
    tm, tn = _tile(T, 2048), 512

    def body(x_ref, g_ref, w_ref, proj_ref, ht_ref, h_ref):
        @pl.when(pl.program_id(1) == 0)
        def _():
            n = min(tm, 512)
            for r0 in range(0, tm, n):
                xv = x_ref[r0:r0 + n, :]
                h = xv * _rstd(xv, D_MODEL) * g_ref[...]
                h_ref[r0:r0 + n, :] = h.astype(BF16)
                ht_ref[:, r0:r0 + n] = h.T.astype(BF16)

        proj_ref[...] = jnp.dot(h_ref[...], w_ref[...], preferred_element_type=F32).astype(BF16)

    return pl.pallas_call(
        body, name="inproj_fwd", grid=(T // tm, PP // tn),
        in_specs=[pl.BlockSpec((tm, D_MODEL), lambda i, j: (i, 0)),
                  pl.BlockSpec((1, D_MODEL), lambda i, j: (0, 0)),
                  pl.BlockSpec((D_MODEL, tn), lambda i, j: (0, j))],
        out_specs=[pl.BlockSpec((tm, tn), lambda i, j: (i, j)),
                   pl.BlockSpec((D_MODEL, tm), lambda i, j: (0, i))],
        out_shape=[jax.ShapeDtypeStruct((T, PP), BF16), jax.ShapeDtypeStruct((D_MODEL, T), BF16)],
        scratch_shapes=[pltpu.VMEM((tm, D_MODEL), BF16)],
        compiler_params=_cp(),
    )(x, g, wp)


def _matmul_nn(at, b, name):
    K, T = at.shape
    N = b.shape[1]
    tt, tn = _tile(T, 1024), _tile(N, 2816)
    nk = T // tt

    def body(a_ref, b_ref, o_ref, acc_ref):
        k = pl.program_id(1)

        @pl.when(k == 0)
        def _():
            acc_ref[...] = jnp.zeros_like(acc_ref)

        acc_ref[...] += jnp.dot(a_ref[...], b_ref[...], preferred_element_type=F32)

        @pl.when(k == nk - 1)
        def _():
            o_ref[...] = acc_ref[...].astype(BF16)

    return pl.pallas_call(
        body, name=name, grid=(N // tn, nk),
        in_specs=[pl.BlockSpec((K, tt), lambda j, k: (0, k)),
                  pl.BlockSpec((tt, tn), lambda j, k: (k, j))],
        out_specs=pl.BlockSpec((K, tn), lambda j, k: (0, j)),
        out_shape=jax.ShapeDtypeStruct((K, N), BF16),
        scratch_shapes=[pltpu.VMEM((K, tn), F32)],
        compiler_params=_cp(),
    )(at, b)


def _matmul_tn(a, b, name):
    T, K = a.shape
    N = b.shape[1]
    tt, tn = _tile(T, 512), _tile(N, 1024)

    def body(a_ref, b_ref, o_ref):
        @pl.when(pl.program_id(1) == 0)
        def _():
            o_ref[...] = jnp.zeros_like(o_ref)

        o_ref[...] += _mm_tn(a_ref[...], b_ref[...])

    return pl.pallas_call(
        body, name=name, grid=(N // tn, T // tt),
        in_specs=[pl.BlockSpec((tt, K), lambda j, k: (k, 0)),
                  pl.BlockSpec((tt, tn), lambda j, k: (k, j))],
        out_specs=pl.BlockSpec((K, tn), lambda j, k: (0, j)),
        out_shape=jax.ShapeDtypeStruct((K, N), F32),
        compiler_params=_cp(),
    )(a, b)


def _inproj_bwd_x(dproj, wp, x, g, dout):
    T = x.shape[0]
    tm, tk = _tile(T, 1024), 1024
    nk = PP // tk

    def body(dp_ref, w_ref, x_ref, g_ref, do_ref, dx_ref, dg_ref, acc_ref):
        i, k = pl.program_id(0), pl.program_id(1)

        @pl.when(k == 0)
        def _():
            acc_ref[...] = jnp.zeros_like(acc_ref)

        @pl.when((k == 0) & (i == 0))
        def _():
            dg_ref[...] = jnp.zeros_like(dg_ref)

        acc_ref[...] += _mm_nt(dp_ref[...], w_ref[...])

        @pl.when(k == nk - 1)
        def _():
            dh = acc_ref[...]
            xv = x_ref[...]
            r = _rstd(xv, D_MODEL)
            gy = dh * g_ref[...]
            dot = jnp.sum(xv * gy, axis=-1, keepdims=True) * (1.0 / D_MODEL)
            dx_ref[...] = do_ref[...] + r * gy - xv * (r * r * r) * dot
            dg_ref[...] += jnp.sum(dh * xv * r, axis=0, keepdims=True)

    return pl.pallas_call(
        body, name="inproj_bwd_x", grid=(T // tm, nk),
        in_specs=[pl.BlockSpec((tm, tk), lambda i, k: (i, k)),
                  pl.BlockSpec((D_MODEL, tk), lambda i, k: (0, k)),
                  pl.BlockSpec((tm, D_MODEL), lambda i, k: (i, 0)),
                  pl.BlockSpec((1, D_MODEL), lambda i, k: (0, 0)),
                  pl.BlockSpec((tm, D_MODEL), lambda i, k: (i, 0))],
        out_specs=[pl.BlockSpec((tm, D_MODEL), lambda i, k: (i, 0)),
                   pl.BlockSpec((1, D_MODEL), lambda i, k: (0, 0))],
        out_shape=[jax.ShapeDtypeStruct((T, D_MODEL), F32), jax.ShapeDtypeStruct((1, D_MODEL), F32)],
        scratch_shapes=[pltpu.VMEM((tm, D_MODEL), F32)],
        compiler_params=_cp(),
    )(dproj, wp, x, g, dout)


A_SEGS = (CB_AB, CB_AC, CB_AX, CB_AZ)


def _mixa_fwd(proj, cw, cb, B, S):
    nc = CONV_WIDTH // LANE

    def body(ab_ref, ac_ref, ax_ref, az_ref, cw_ref, cb_ref, y_ref):
        ab, ac, ax, az = (r[...].astype(F32) for r in (ab_ref, ac_ref, ax_ref, az_ref))
        u = ac * ax
        conv = cb_ref[...] + cw_ref[0:1, :] * _shift_down(u, 2) + cw_ref[1:2, :] * _shift_down(u, 1) + cw_ref[2:3, :] * u
        y_ref[...] = (ab * conv * _silu(az)).astype(BF16)

    return pl.pallas_call(
        body, name="mixa_fwd", grid=(B, nc),
        in_specs=[pl.BlockSpec((S, LANE), lambda b, j, c0=c0: (b, c0 + j)) for c0 in A_SEGS]
                 + [pl.BlockSpec((CONV_K, LANE), lambda b, j: (0, j)),
                    pl.BlockSpec((1, LANE), lambda b, j: (0, j))],
        out_specs=pl.BlockSpec((S, LANE), lambda b, j: (b, j)),
        out_shape=jax.ShapeDtypeStruct((B * S, CONV_WIDTH), BF16),
        compiler_params=_cp(),
    )(proj, proj, proj, proj, cw, cb)


def _mixa_bwd(dproj, dy, proj, cw, cb, B, S):
    nc = CONV_WIDTH // LANE

    def body(dpin_ref, dy_ref, ab_ref, ac_ref, ax_ref, az_ref, cw_ref, cb_ref, dp_ref, st_ref, stage, sems):
        del dpin_ref
        j, b = pl.program_id(0), pl.program_id(1)
        ab, ac, ax, az = (r[...].astype(F32) for r in (ab_ref, ac_ref, ax_ref, az_ref))
        u = ac * ax
        u1, u2 = _shift_down(u, 1), _shift_down(u, 2)
        w0, w1, w2 = cw_ref[0:1, :], cw_ref[1:2, :], cw_ref[2:3, :]
        conv = cb_ref[...] + w0 * u2 + w1 * u1 + w2 * u
        s = _silu(az)
        d = dy_ref[...]
        dconv = d * ab * s
        du = w2 * dconv + w1 * _shift_up(dconv, 1) + w0 * _shift_up(dconv, 2)
        grads = (d * conv * s, du * ax, du * ac, d * ab * conv * _dsilu(az))

        def fill(slot):
            for k, v in enumerate(grads):
                stage[slot, k] = v.astype(BF16)

        def copies_of(step):
            sj, sb = step // B, step % B
            return _put_copies([stage.at[:, k] for k in range(4)], dp_ref, sems, step % 2,
                               pl.ds(pl.multiple_of(sb * S, S), S),
                               [pl.multiple_of((c0 + sj) * LANE, LANE) for c0 in A_SEGS])

        _put_pipeline(j * B + b, nc * B, copies_of, fill)
        row = lax.broadcasted_iota(jnp.int32, (8, LANE), 0)
        st = jnp.zeros((8, LANE), F32)
        for r, v in enumerate((dconv * u2, dconv * u1, dconv * u, dconv)):
            st = st + jnp.where(row == r, jnp.sum(v, axis=0, keepdims=True), 0.0)

        @pl.when(pl.program_id(1) == 0)
        def _():
            st_ref[...] = st

        @pl.when(pl.program_id(1) != 0)
        def _():
            st_ref[...] += st

    return pl.pallas_call(
        body, name="mixa_bwd", grid=(nc, B),
        in_specs=[pl.BlockSpec(memory_space=pl.ANY),
                  pl.BlockSpec((S, LANE), lambda j, b: (b, j))]
                 + [pl.BlockSpec((S, LANE), lambda j, b, c0=c0: (b, c0 + j)) for c0 in A_SEGS]
                 + [pl.BlockSpec((CONV_K, LANE), lambda j, b: (0, j)),
                    pl.BlockSpec((1, LANE), lambda j, b: (0, j))],
        out_specs=[pl.BlockSpec(memory_space=pl.ANY),
                   pl.BlockSpec((8, LANE), lambda j, b: (0, j))],
        out_shape=[jax.ShapeDtypeStruct(dproj.shape, BF16), jax.ShapeDtypeStruct((8, CONV_WIDTH), F32)],
        scratch_shapes=[pltpu.VMEM((2, 4, S, LANE), BF16), pltpu.SemaphoreType.DMA((2, 4))],
        input_output_aliases={0: 0},
        compiler_params=_cp(),
    )(dproj, dy, proj, proj, proj, proj, cw, cb)


def _mla_prep_fwd(proj, gq, gkv, wuqp, wkp, wv, gmq, gmk, cos, sa, sb, S):
    T = proj.shape[0]
    ts = _tile(S, 512)
    ns = S // ts
    W = MLA_HEADS * LANE

    def body(p_ref, gq_ref, gkv_ref, wuq_ref, wk_ref, wv_ref, gmq_ref, gmk_ref, cos_ref, sa_ref, sb_ref,
             q_ref, k_ref, v_ref):
        cq = p_ref[:, 0:2 * LANE].astype(F32)
        ckv = p_ref[:, 2 * LANE:3 * LANE].astype(F32)
        kpe = pltpu.roll(p_ref[:, 3 * LANE:4 * LANE].astype(F32), HALF, axis=1)
        cqn = cq * _rstd(cq, MLA_Q_LORA) * gq_ref[...]
        ckn = (ckv * _rstd(ckv, MLA_KV_LORA) * gkv_ref[...]).astype(BF16)
        q0 = _mm(cqn, wuq_ref[...])
        kn = _mm(ckn, wk_ref[...])
        v_ref[...] = _mm(ckn, wv_ref[...]).astype(BF16)
        c, a, b = cos_ref[...], sa_ref[...], sb_ref[...]
        for h in range(MLA_HEADS):
            q0h = q0[:, h * LANE:(h + 1) * LANE]
            q_ref[h] = _rope(q0h * _rstd(q0h, MLA_QK) * gmq_ref[...], c, a, b).astype(BF16)
            k0h = kn[:, h * LANE:(h + 1) * LANE] + kpe
            k_ref[h] = _rope(k0h * _rstd(k0h, MLA_QK) * gmk_ref[...], c, a, b).astype(BF16)

    def whole(r, c):
        return pl.BlockSpec((r, c), lambda i: (0, 0))

    tab = pl.BlockSpec((ts, LANE), lambda i: (i % ns, 0))
    return pl.pallas_call(
        body, name="mla_prep_fwd", grid=(T // ts,),
        in_specs=[pl.BlockSpec((ts, 4 * LANE), lambda i: (i, CB_CQ // 4)),
                  whole(1, MLA_Q_LORA), whole(1, MLA_KV_LORA), whole(MLA_Q_LORA, W), whole(MLA_KV_LORA, W),
                  whole(MLA_KV_LORA, MLA_HEADS * MLA_V), whole(1, LANE), whole(1, LANE), tab, tab, tab],
        out_specs=[pl.BlockSpec((MLA_HEADS, ts, LANE), lambda i: (0, i, 0)),
                   pl.BlockSpec((MLA_HEADS, ts, LANE), lambda i: (0, i, 0)),
                   pl.BlockSpec((ts, MLA_HEADS * MLA_V), lambda i: (i, 0))],
        out_shape=[jax.ShapeDtypeStruct((MLA_HEADS, T, LANE), BF16), jax.ShapeDtypeStruct((MLA_HEADS, T, LANE), BF16),
                   jax.ShapeDtypeStruct((T, MLA_HEADS * MLA_V), BF16)],
        compiler_params=_cp(),
    )(proj, gq, gkv, wuqp, wkp, wv, gmq, gmk, cos, sa, sb)


def _mla_prep_bwd(dproj, dq, dk, dv, proj, gq, gkv, wuqp, wkp, wv, gmq, gmk, cos, sa, sb, S):
    T = proj.shape[0]
    ts = _tile(S, 256)
    ns = S // ts
    W = MLA_HEADS * LANE

    def body(dpin_ref, dq_ref, dk_ref, dv_ref, p_ref, gq_ref, gkv_ref, wuq_ref, wk_ref, wv_ref, gmq_ref, gmk_ref,
             cos_ref, sa_ref, sb_ref,
             dp_ref, dwuq_ref, dwk_ref, dwv_ref, dgq_ref, dgkv_ref, dgmq_ref, dgmk_ref, dq0_ref, dkn_ref):
        del dpin_ref

        @pl.when(pl.program_id(0) == 0)
        def _():
            for r in (dwuq_ref, dwk_ref, dwv_ref, dgq_ref, dgkv_ref, dgmq_ref, dgmk_ref):
                r[...] = jnp.zeros_like(r)

        cq = p_ref[:, 0:2 * LANE].astype(F32)
        ckv = p_ref[:, 2 * LANE:3 * LANE].astype(F32)
        kpe = pltpu.roll(p_ref[:, 3 * LANE:4 * LANE].astype(F32), HALF, axis=1)
        rq = _rstd(cq, MLA_Q_LORA)
        rkv = _rstd(ckv, MLA_KV_LORA)
        gq, gkv, gmq, gmk = gq_ref[...], gkv_ref[...], gmq_ref[...], gmk_ref[...]
        cqn = (cq * rq * gq).astype(BF16)
        ckn = (ckv * rkv * gkv).astype(BF16)
        q0 = _mm(cqn, wuq_ref[...])
        kn = _mm(ckn, wk_ref[...])
        c, a, b = cos_ref[...], sa_ref[...], sb_ref[...]
        lane = lax.broadcasted_iota(jnp.int32, (ts, LANE), 1)
        dgmq = jnp.zeros((1, LANE), F32)
        dgmk = jnp.zeros((1, LANE), F32)
        dkpe = jnp.zeros((ts, LANE), F32)
        for h in range(MLA_HEADS):
            q0h = q0[:, h * LANE:(h + 1) * LANE]
            r = _rstd(q0h, MLA_QK)
            d1 = _rope_t(dq_ref[h], c, a, b)
            gy = d1 * gmq
            dq0_ref[:, h * LANE:(h + 1) * LANE] = (
                r * gy - q0h * (r * r * r) * (jnp.sum(q0h * gy, axis=-1, keepdims=True) * (1.0 / MLA_QK))).astype(BF16)
            dgmq = dgmq + jnp.sum(d1 * q0h * r, axis=0, keepdims=True)
            k0h = kn[:, h * LANE:(h + 1) * LANE] + kpe
            r = _rstd(k0h, MLA_QK)
            d1 = _rope_t(dk_ref[h], c, a, b)
            gy = d1 * gmk
            dk0 = r * gy - k0h * (r * r * r) * (jnp.sum(k0h * gy, axis=-1, keepdims=True) * (1.0 / MLA_QK))
            dgmk = dgmk + jnp.sum(d1 * k0h * r, axis=0, keepdims=True)
            dkn_ref[:, h * LANE:(h + 1) * LANE] = jnp.where(lane < MLA_NOPE, dk0, 0.0).astype(BF16)
            dkpe = dkpe + jnp.where((lane >= MLA_NOPE) & (lane < MLA_QK), dk0, 0.0)
        dq0 = dq0_ref[...]
        dkn = dkn_ref[...]
        dvv = dv_ref[...]
        dwuq_ref[...] += _mm_tn(cqn, dq0)
        dwk_ref[...] += _mm_tn(ckn, dkn)
        dwv_ref[...] += _mm_tn(ckn, dvv)
        dgmq_ref[...] += dgmq
        dgmk_ref[...] += dgmk
        dcqn = _mm_nt(dq0, wuq_ref[...])
        gy = dcqn * gq
        dp_ref[:, 0:2 * LANE] = (
            rq * gy - cq * (rq * rq * rq) * (jnp.sum(cq * gy, axis=-1, keepdims=True) * (1.0 / MLA_Q_LORA))).astype(BF16)
        dgq_ref[...] += jnp.sum(dcqn * cq * rq, axis=0, keepdims=True)
        dckn = _mm_nt(dkn, wk_ref[...]) + _mm_nt(dvv, wv_ref[...])
        gy = dckn * gkv
        dp_ref[:, 2 * LANE:3 * LANE] = (
            rkv * gy - ckv * (rkv * rkv * rkv) * (jnp.sum(ckv * gy, axis=-1, keepdims=True) * (1.0 / MLA_KV_LORA))).astype(BF16)
        dgkv_ref[...] += jnp.sum(dckn * ckv * rkv, axis=0, keepdims=True)
        dp_ref[:, 3 * LANE:4 * LANE] = pltpu.roll(dkpe, HALF, axis=1).astype(BF16)

    def whole(r, c):
        return pl.BlockSpec((r, c), lambda i: (0, 0))

    tab = pl.BlockSpec((ts, LANE), lambda i: (i % ns, 0))
    heads = pl.BlockSpec((MLA_HEADS, ts, LANE), lambda i: (0, i, 0))
    return pl.pallas_call(
        body, name="mla_prep_bwd", grid=(T // ts,),
        in_specs=[pl.BlockSpec(memory_space=pl.ANY), heads, heads,
                  pl.BlockSpec((ts, MLA_HEADS * MLA_V), lambda i: (i, 0)),
                  pl.BlockSpec((ts, 4 * LANE), lambda i: (i, CB_CQ // 4)),
                  whole(1, MLA_Q_LORA), whole(1, MLA_KV_LORA), whole(MLA_Q_LORA, W), whole(MLA_KV_LORA, W),
                  whole(MLA_KV_LORA, MLA_HEADS * MLA_V), whole(1, LANE), whole(1, LANE), tab, tab, tab],
        out_specs=[pl.BlockSpec((ts, 4 * LANE), lambda i: (i, CB_CQ // 4)),
                   whole(MLA_Q_LORA, W), whole(MLA_KV_LORA, W), whole(MLA_KV_LORA, MLA_HEADS * MLA_V),
                   whole(1, MLA_Q_LORA), whole(1, MLA_KV_LORA), whole(1, LANE), whole(1, LANE)],
        out_shape=[jax.ShapeDtypeStruct(dproj.shape, BF16),
                   jax.ShapeDtypeStruct((MLA_Q_LORA, W), F32), jax.ShapeDtypeStruct((MLA_KV_LORA, W), F32),
                   jax.ShapeDtypeStruct((MLA_KV_LORA, MLA_HEADS * MLA_V), F32),
                   jax.ShapeDtypeStruct((1, MLA_Q_LORA), F32), jax.ShapeDtypeStruct((1, MLA_KV_LORA), F32),
                   jax.ShapeDtypeStruct((1, LANE), F32), jax.ShapeDtypeStruct((1, LANE), F32)],
        scratch_shapes=[pltpu.VMEM((ts, W), BF16), pltpu.VMEM((ts, W), BF16)],
        input_output_aliases={0: 0},
        compiler_params=_cp(),
    )(dproj, dq, dk, dv, proj, gq, gkv, wuqp, wkp, wv, gmq, gmk, cos, sa, sb)


def _dil_prep_fwd(proj, gq, gk):
    T = proj.shape[0]
    ts = _tile(T, 512)

    def body(pq_ref, pk_ref, pv_ref, gq_ref, gk_ref, q_ref, k_ref, v_ref):
        v_ref[...] = pv_ref[...].astype(F32)
        for c in range(NPAIR):
            cs = slice(c * LANE, (c + 1) * LANE)
            t = jnp.concatenate([pq_ref[:, cs], pk_ref[:, cs]], axis=1).astype(F32)
            y = t * lax.rsqrt(_head_bcast_sum(t * t, terms=2) * (1.0 / DIL_HEAD_DIM) + EPS)
            q_ref[:, cs] = y[:, 0:LANE] * gq_ref[:, cs]
            k_ref[:, cs] = y[:, LANE:2 * LANE] * gk_ref[:, cs]

    col = pl.BlockSpec((1, DIL_WIDTH), lambda i, g: (0, g))
    out = pl.BlockSpec((ts, DIL_WIDTH), lambda i, g: (i, g))
    seg = lambda c0: pl.BlockSpec((ts, DIL_WIDTH), lambda i, g: (i, c0 // NPAIR + g))
    return pl.pallas_call(
        body, name="dil_prep_fwd", grid=(T // ts, DIL_GROUPS),
        in_specs=[seg(CB_DQ), seg(CB_DK), seg(CB_DV), col, col],
        out_specs=[out, out, out],
        out_shape=[jax.ShapeDtypeStruct((T, DIL_QK), F32)] * 3,
        compiler_params=_cp(),
    )(proj, proj, proj, gq, gk)


def _dil_prep_bwd(dproj, ddq, ddk, ddv, proj, gq, gk):
    T = proj.shape[0]
    ts = _tile(T, 512)
    nt = T // ts

    def body(dpin_ref, ddq_ref, ddk_ref, ddv_ref, pq_ref, pk_ref, gq_ref, gk_ref, dp_ref, dgq_ref, dgk_ref,
             stage, sems):
        del dpin_ref
        g, i = pl.program_id(0), pl.program_id(1)

        @pl.when(i == 0)
        def _():
            dgq_ref[...] = jnp.zeros_like(dgq_ref)
            dgk_ref[...] = jnp.zeros_like(dgk_ref)

        def fill(slot):
            stage[slot, 2] = ddv_ref[...].astype(BF16)
            for c in range(NPAIR):
                cs = slice(c * LANE, (c + 1) * LANE)
                t = jnp.concatenate([pq_ref[:, cs], pk_ref[:, cs]], axis=1).astype(F32)
                d = jnp.concatenate([ddq_ref[:, cs], ddk_ref[:, cs]], axis=1)
                gy = d * jnp.concatenate([gq_ref[:, cs], gk_ref[:, cs]], axis=1)
                r = lax.rsqrt(_head_bcast_sum(t * t, terms=2) * (1.0 / DIL_HEAD_DIM) + EPS)
                dot = _head_bcast_sum(t * gy, terms=2) * (1.0 / DIL_HEAD_DIM)
                dx = (r * gy - t * (r * r * r) * dot).astype(BF16)
                stage[slot, 0, :, cs] = dx[:, 0:LANE]
                stage[slot, 1, :, cs] = dx[:, LANE:2 * LANE]
                part = jnp.sum(d * t * r, axis=0, keepdims=True)
                dgq_ref[:, cs] += part[:, 0:LANE]
                dgk_ref[:, cs] += part[:, LANE:2 * LANE]

        def copies_of(step):
            sg, si = step // nt, step % nt
            return _put_copies([stage.at[:, k] for k in range(3)], dp_ref, sems, step % 2,
                               pl.ds(pl.multiple_of(si * ts, ts), ts),
                               [pl.multiple_of((c0 + NPAIR * sg) * LANE, LANE) for c0 in (CB_DQ, CB_DK, CB_DV)])

        _put_pipeline(g * nt + i, DIL_GROUPS * nt, copies_of, fill)

    col = pl.BlockSpec((1, DIL_WIDTH), lambda g, i: (0, g))
    tok = pl.BlockSpec((ts, DIL_WIDTH), lambda g, i: (i, g))
    seg = lambda c0: pl.BlockSpec((ts, DIL_WIDTH), lambda g, i: (i, c0 // NPAIR + g))
    return pl.pallas_call(
        body, name="dil_prep_bwd", grid=(DIL_GROUPS, nt),
        in_specs=[pl.BlockSpec(memory_space=pl.ANY), tok, tok, tok, seg(CB_DQ), seg(CB_DK), col, col],
        out_specs=[pl.BlockSpec(memory_space=pl.ANY), col, col],
        out_shape=[jax.ShapeDtypeStruct(dproj.shape, BF16), jax.ShapeDtypeStruct((1, DIL_QK), F32),
                   jax.ShapeDtypeStruct((1, DIL_QK), F32)],
        scratch_shapes=[pltpu.VMEM((2, 3, ts, DIL_WIDTH), BF16), pltpu.SemaphoreType.DMA((2, 3))],
        input_output_aliases={0: 0},
        compiler_params=_cp(),
    )(dproj, ddq, ddk, ddv, proj, proj, gq, gk)


COPY_ROWS = 256


def _to_classes(src_ref, dst_ref, d, L, scale=None):
    n = min(L, COPY_ROWS)
    for r in range(d):
        for c0 in range(0, L, n):
            rows = pl.ds(r + c0 * d, n, stride=d) if d > 1 else pl.ds(c0, n)
            val = src_ref[rows, :]
            if scale is not None:
                val = val * scale
            dst_ref[r * L + c0:r * L + c0 + n, :] = val.astype(dst_ref.dtype)


def _from_classes(src_ref, dst_ref, d, L):
    n = min(L, COPY_ROWS)
    for r in range(d):
        for c0 in range(0, L, n):
            rows = pl.ds(r + c0 * d, n, stride=d) if d > 1 else pl.ds(c0, n)
            dst_ref[rows, :] = src_ref[r * L + c0:r * L + c0 + n, :].astype(dst_ref.dtype)


MLA_TQ, MLA_TK = 512, 512


def _causal_bias(tq, tk, shift):
    row = lax.broadcasted_iota(jnp.int32, (tq, tk), 0)
    col = lax.broadcasted_iota(jnp.int32, (tq, tk), 1)
    return jnp.where(row >= col + shift, 0.0, NEG)


def _mla_specs(S):
    heads = pl.BlockSpec((2, S, LANE), lambda b, j: (j, b, 0))
    pair = pl.BlockSpec((S, LANE), lambda b, j: (b, j))
    return heads, pair


def _mla_attn_fwd(q, k, v, B, S):
    tq = _tile(S, MLA_TQ)
    tk = _tile(tq, MLA_TK)
    nd = tq // tk
    scale = MLA_QK ** -0.5
    heads, pair = _mla_specs(S)

    def body(q_ref, k_ref, v_ref, o_ref, lse_ref):
        lo, lok = _lane_lo((tq, LANE)), _lane_lo((tk, LANE))
        diag = [_causal_bias(tq, tk, i * tk) for i in range(nd)]

        def block(g, _):
            row0 = pl.multiple_of(g * tq, tq)
            rows = pl.ds(row0, tq)
            qs = [q_ref[hh, rows, :] for hh in range(2)]

            one = jnp.ones((), BF16)

            def step(off, carries, bias):
                off = pl.multiple_of(off, tk)
                vt = v_ref[pl.ds(off, tk), :]
                vh = (jnp.where(lok, vt, one), jnp.where(lok, one, vt))
                out = []
                for hh, (m, acc) in enumerate(carries):
                    s = _mm_nt(qs[hh], k_ref[hh, pl.ds(off, tk), :]) * scale
                    if bias is not None:
                        s = s + bias
                    m_new = jnp.maximum(m, jnp.max(s, axis=-1, keepdims=True))
                    p = jnp.exp(s - m_new)
                    out.append((m_new, jnp.exp(m - m_new) * acc + _mm(p, vh[hh])))
                return tuple(out)

            init = (jnp.full((tq, 1), NEG, F32), jnp.zeros((tq, LANE), F32))
            carries = lax.fori_loop(0, g * nd, lambda i, c: step(i * tk, c, None), (init, init))
            for i in range(nd):
                carries = step(row0 + i * tk, carries, diag[i])
            (ma, acca), (mb, accb) = carries
            la, lb = pltpu.roll(acca, HALF, axis=1), pltpu.roll(accb, HALF, axis=1)
            o_ref[rows, :] = jnp.where(lo, acca / la, accb / lb)
            lse_ref[rows, :] = jnp.where(lo, ma + jnp.log(la), mb + jnp.log(lb))
            return 0

        lax.fori_loop(0, S // tq, block, 0)

    return pl.pallas_call(
        body, name="mla_attn_fwd", grid=(B, NPAIR), in_specs=[heads, heads, pair], out_specs=[pair, pair],
        out_shape=[jax.ShapeDtypeStruct((B * S, MLA_HEADS * MLA_V), F32)] * 2,
        compiler_params=_cp(),
    )(q, k, v)


DIL_UNROLL = 16


def _dil_geometry(gi, S):
    span, d = DIL_PATTERNS[gi]
    L = S // d
    t = _tile(L, 128)
    window = span // d
    back = min(-(-window // t) * t, L - t)
    return d, L, t, window, back


def _dil_specs(gi, S):
    qk = pl.BlockSpec((S, LANE), lambda b, j: (b, NPAIR * gi + j))
    pair = pl.BlockSpec((S, LANE), lambda b, j: (b, j))
    return qk, qk, pair


def _dil_bias(bias_ref, sl_ref, j, t, kw, back, window):
    row = lax.broadcasted_iota(jnp.int32, (2 * t, kw), 0)
    col = lax.broadcasted_iota(jnp.int32, (2 * t, kw), 1)
    second = row >= t
    slope = jnp.where(second, sl_ref[j, 1], sl_ref[j, 0])
    for n in range(bias_ref.shape[0]):
        dist = jnp.where(second, row - t, row) + n * back - col
        bias_ref[n] = jnp.where((dist >= 0) & (dist <= window), -slope * dist.astype(F32), NEG)


def _stack_heads(x, lo):
    zero = jnp.zeros((), x.dtype)
    return jnp.concatenate([jnp.where(lo, x, zero), jnp.where(lo, zero, x)], axis=0)


def _dil_attn_fwd(gi, slopes, qn, kn, proj, B, S):
    d, L, t, window, back = _dil_geometry(gi, S)
    kw, nq = back + t, L // t
    nbias = 2 if back else 1
    qk, vspec, pair = _dil_specs(gi, S)

    def body(sl_ref, q_ref, k_ref, v_ref, o_ref, lse_ref, qs, ks, vs, os_, ls, bias_ref):
        _to_classes(q_ref, qs, d, L, DIL_HEAD_DIM ** -0.5)
        _to_classes(k_ref, ks, d, L)
        _to_classes(v_ref, vs, d, L)
        _dil_bias(bias_ref, sl_ref, pl.program_id(1), t, kw, back, window)
        lo = _lane_lo((t, LANE))

        def block(g, _):
            qb = g % nq if d > 1 else g
            row0 = pl.multiple_of(g * t, t)
            rows = pl.ds(row0, t)
            early = qb * t < back
            keys = pl.ds(pl.multiple_of(jnp.where(early, row0 - qb * t, row0 - back), t), kw)
            s = _mm_nt(_stack_heads(qs[rows, :], lo), ks[keys, :]) + bias_ref[jnp.where(early, 0, nbias - 1)]
            m = jnp.max(s, axis=-1, keepdims=True)
            p = jnp.exp(s - m)
            l = jnp.sum(p, axis=-1, keepdims=True)
            o2 = _mm(p, vs[keys, :]) / l
            lse2 = m + jnp.log(l)
            os_[rows, :] = jnp.where(lo, o2[:t], o2[t:])
            ls[rows, :] = jnp.where(lo, lse2[:t], lse2[t:])
            return 0

        lax.fori_loop(0, d * nq, block, 0, unroll=DIL_UNROLL if d * nq % DIL_UNROLL == 0 else 1)
        _from_classes(os_, o_ref, d, L)
        _from_classes(ls, lse_ref, d, L)

    return pl.pallas_call(
        body, name=f"dil_attn_fwd_{gi}", grid=(B, NPAIR),
        in_specs=[pl.BlockSpec(memory_space=pltpu.SMEM), qk, qk, vspec], out_specs=[pair, pair],
        out_shape=[jax.ShapeDtypeStruct((B * S, DIL_WIDTH), F32)] * 2,
        scratch_shapes=[pltpu.VMEM((S, LANE), BF16)] * 3 + [pltpu.VMEM((S, LANE), F32)] * 2
                       + [pltpu.VMEM((nbias, 2 * t, kw), F32)],
        compiler_params=_cp(),
    )(slopes, qn, kn, proj)


def _mla_attn_bwd(q, k, v, do, lse, delta, B, S):
    T = B * S
    tq = _tile(S, MLA_TQ)
    tk = _tile(tq, MLA_TK)
    nd = tq // tk
    scale = MLA_QK ** -0.5
    heads, pair = _mla_specs(S)

    def body(q_ref, k_ref, v_ref, do_ref, lse_ref, dl_ref, dq_ref, dk_ref, dv_ref):
        dk_ref[...] = jnp.zeros_like(dk_ref)
        dv_ref[...] = jnp.zeros_like(dv_ref)
        lo = _lane_lo((tq, LANE))
        diag = [_causal_bias(tq, tk, i * tk) for i in range(nd)]

        def block(g, _):
            row0 = pl.multiple_of(g * tq, tq)
            rows = pl.ds(row0, tq)
            per_head = []
            for hh in range(2):
                sel = lo if hh == 0 else jnp.logical_not(lo)
                per_head.append((q_ref[hh, rows, :], jnp.where(sel, do_ref[rows, :], jnp.zeros((), BF16)),
                                 jnp.max(jnp.where(sel, lse_ref[rows, :], NEG), axis=-1, keepdims=True),
                                 jnp.max(jnp.where(sel, dl_ref[rows, :], NEG), axis=-1, keepdims=True)))

            def step(off, dq_accs, bias):
                cols = pl.ds(pl.multiple_of(off, tk), tk)
                vt = v_ref[cols, :]
                out, dv = [], None
                for hh, (qh, doh, lse_h, dl_h) in enumerate(per_head):
                    kh = k_ref[hh, cols, :]
                    s = _mm_nt(qh, kh) * scale
                    if bias is not None:
                        s = s + bias
                    p = jnp.exp(s - lse_h)
                    ds = (p * (_mm_nt(doh, vt) - dl_h)).astype(BF16)
                    dk_ref[hh, cols, :] += _mm_tn(ds, qh) * scale
                    part = _mm_tn(p, doh)
                    dv = part if dv is None else dv + part
                    out.append(dq_accs[hh] + _mm(ds, kh))
                dv_ref[cols, :] += dv
                return tuple(out)

            zero = jnp.zeros((tq, LANE), F32)
            dq_accs = lax.fori_loop(0, g * nd, lambda i, a: step(i * tk, a, None), (zero, zero))
            for i in range(nd):
                dq_accs = step(row0 + i * tk, dq_accs, diag[i])
            for hh in range(2):
                dq_ref[hh, rows, :] = dq_accs[hh] * scale
            return 0

        lax.fori_loop(0, S // tq, block, 0)

    return pl.pallas_call(
        body, name="mla_attn_bwd", grid=(B, NPAIR), in_specs=[heads, heads, pair, pair, pair, pair],
        out_specs=[heads, heads, pair],
        out_shape=[jax.ShapeDtypeStruct((MLA_HEADS, T, LANE), F32), jax.ShapeDtypeStruct((MLA_HEADS, T, LANE), F32),
                   jax.ShapeDtypeStruct((T, MLA_HEADS * MLA_V), F32)],
        compiler_params=_cp(),
    )(q, k, v, do, lse, delta)


def _dil_attn_bwd(gi, slopes, qn, kn, proj, do, lse, delta, through, B, S):
    d, L, t, window, back = _dil_geometry(gi, S)
    kw, nq = back + t, L // t
    nbias = 2 if back else 1
    scale = DIL_HEAD_DIM ** -0.5
    qk, vspec, pair = _dil_specs(gi, S)

    def body(*refs):
        refs = list(refs)
        sl_ref, q_ref, k_ref, v_ref, do_ref, lse_ref, dl_ref = refs[:7]
        dq_ref, dk_ref, dv_ref, qs, ks, vs, dos, lss, dls, dqs, dks, dvs, bias_ref = refs[-13:]
        _to_classes(q_ref, qs, d, L, scale)
        for src, dst in ((k_ref, ks), (v_ref, vs), (do_ref, dos), (lse_ref, lss), (dl_ref, dls)):
            _to_classes(src, dst, d, L)
        _dil_bias(bias_ref, sl_ref, pl.program_id(1), t, kw, back, window)
        dks[...] = jnp.zeros_like(dks)
        dvs[...] = jnp.zeros_like(dvs)
        lo = _lane_lo((t, LANE))

        def stats(ref, rows):
            x = ref[rows, :]
            return jnp.concatenate([jnp.max(jnp.where(lo, x, NEG), axis=-1, keepdims=True),
                                    jnp.max(jnp.where(lo, NEG, x), axis=-1, keepdims=True)], axis=0)

        def block(g, _):
            qb = g % nq if d > 1 else g
            row0 = pl.multiple_of(g * t, t)
            rows = pl.ds(row0, t)
            early = qb * t < back
            keys = pl.ds(pl.multiple_of(jnp.where(early, row0 - qb * t, row0 - back), t), kw)
            q2 = _stack_heads(qs[rows, :], lo)
            do2 = _stack_heads(dos[rows, :], lo)
            kt = ks[keys, :]
            s = _mm_nt(q2, kt) + bias_ref[jnp.where(early, 0, nbias - 1)]
            p = jnp.exp(s - stats(lss, rows))
            ds = (p * (_mm_nt(do2, vs[keys, :]) - stats(dls, rows))).astype(BF16)
            dq2 = _mm(ds, kt) * scale
            dqs[rows, :] = jnp.where(lo, dq2[:t], dq2[t:])
            dks[keys, :] += _mm_tn(ds, q2)
            dvs[keys, :] += _mm_tn(p, do2)
            return 0

        lax.fori_loop(0, d * nq, block, 0, unroll=DIL_UNROLL if d * nq % DIL_UNROLL == 0 else 1)
        for src, dst in ((dqs, dq_ref), (dks, dk_ref), (dvs, dv_ref)):
            _from_classes(src, dst, d, L)

    in_specs = [pl.BlockSpec(memory_space=pltpu.SMEM), qk, qk, vspec, pair, pair, pair]
    args = [slopes, qn, kn, proj, do, lse, delta]
    aliases = {}
    if through is not None:
        aliases = {len(args) + i: i for i in range(3)}
        in_specs = in_specs + [pl.BlockSpec(memory_space=pl.ANY)] * 3
        args = args + list(through)
    return pl.pallas_call(
        body, name=f"dil_attn_bwd_{gi}", grid=(B, NPAIR), in_specs=in_specs, out_specs=[qk, qk, qk],
        out_shape=[jax.ShapeDtypeStruct((B * S, DIL_QK), F32)] * 3,
        scratch_shapes=[pltpu.VMEM((S, LANE), BF16)] * 4 + [pltpu.VMEM((S, LANE), F32)] * 5
                       + [pltpu.VMEM((nbias, 2 * t, kw), F32)],
        input_output_aliases=aliases,
        compiler_params=_cp(),
    )(*args)


def _merge_proj_specs(ts):
    wide = lambda c0, w: pl.BlockSpec((ts, w), lambda i: (i, c0 * LANE // w))
    return [wide(CB_BZ, DIL_WIDTH), wide(CB_CZ, DIL_WIDTH)] + [wide(CB_GATE + 8 * i, D_MODEL) for i in range(3)]


def _merge_common(p_refs, bg_ref, ob_ref, og_refs, lse_refs):
    bz = p_refs[0][...].astype(F32)
    cz = p_refs[1][...].astype(F32)
    gates = [_sigmoid(p_refs[2 + i][...].astype(F32) + bg_ref[:, i * D_MODEL:(i + 1) * D_MODEL]) for i in range(3)]
    ob = ob_ref[...]
    lses = [r[...] for r in lse_refs]
    mx = jnp.maximum(jnp.maximum(lses[0], lses[1]), lses[2])
    es = [jnp.exp(v - mx) for v in lses]
    inv = 1.0 / (es[0] + es[1] + es[2])
    alphas = [e * inv for e in es]
    oc = alphas[0] * og_refs[0][...] + alphas[1] * og_refs[1][...] + alphas[2] * og_refs[2][...]
    return bz, cz, gates, ob, alphas, oc


def _merge_fwd(x, proj, b_gate, ya, ob, ogs, lses, woa, wob, woc, wo):
    T = x.shape[0]
    ts = _tile(T, 256)

    def body(x_ref, p0, p1, p2, p3, p4, bg_ref, ya_ref, ob_ref, og0, og1, og2, l0, l1, l2,
             woa_ref, wob_ref, woc_ref, wo_ref, out_ref):
        bz, cz, gates, obv, alphas, oc = _merge_common((p0, p1, p2, p3, p4), bg_ref, ob_ref, (og0, og1, og2),
                                                       (l0, l1, l2))
        yb = obv * _silu(bz)
        yc = oc * _silu(cz)
        merged = (gates[0] * _mm(ya_ref[...], woa_ref[...]) + gates[1] * _mm(yb, wob_ref[...])
                  + gates[2] * _mm(yc, woc_ref[...]))
        out_ref[...] = x_ref[...] + _mm(merged, wo_ref[...])

    def whole(r, c):
        return pl.BlockSpec((r, c), lambda i: (0, 0))

    tok = lambda w: pl.BlockSpec((ts, w), lambda i: (i, 0))
    return pl.pallas_call(
        body, name="merge_fwd", grid=(T // ts,),
        in_specs=[tok(D_MODEL)] + _merge_proj_specs(ts) + [whole(1, 3 * D_MODEL), tok(CONV_WIDTH)]
                 + [tok(DIL_WIDTH)] * 7 + [whole(CONV_WIDTH, D_MODEL)] * 3 + [whole(D_MODEL, D_MODEL)],
        out_specs=tok(D_MODEL),
        out_shape=jax.ShapeDtypeStruct((T, D_MODEL), F32),
        compiler_params=_cp(),
    )(x, *[proj] * 5, b_gate, ya, ob, *ogs, *lses, woa, wob, woc, wo)


def _merge_bwd(dout, proj, b_gate, ya, ob, ogs, lses, woa, wob, woc, wo):
    T = dout.shape[0]
    ts = _tile(T, 256)
    nt = T // ts

    def body(do_ref, p0, p1, p2, p3, p4, bg_ref, ya_ref, ob_ref, og0, og1, og2, l0, l1, l2,
             woa_ref, wob_ref, woc_ref, wo_ref,
             dp_ref, dya_ref, dob_ref, dlb_ref, dg0, dg1, dg2, dl0, dl1, dl2,
             mg_ref, dpa_ref, dpb_ref, dpc_ref, yb_ref, yc_ref, dbg_ref, st_bz, st_cz, st_gate, sems):
        step = pl.program_id(0)
        slot = step % 2

        def copies_of(s):
            return _put_copies([st_bz, st_cz, st_gate], dp_ref, sems, s % 2, pl.ds(pl.multiple_of(s * ts, ts), ts),
                               [CB_BZ * LANE, CB_CZ * LANE, CB_GATE * LANE])

        @pl.when(step >= 2)
        def _():
            for cp in copies_of(step - 2):
                cp.wait()

        bz, cz, gates, obv, alphas, oc = _merge_common((p0, p1, p2, p3, p4), bg_ref, ob_ref, (og0, og1, og2),
                                                       (l0, l1, l2))
        sb, sc = _silu(bz), _silu(cz)
        yb = obv * sb
        yc = oc * sc
        ps = [_mm(ya_ref[...], woa_ref[...]), _mm(yb, wob_ref[...]), _mm(yc, woc_ref[...])]
        mg_ref[...] = (gates[0] * ps[0] + gates[1] * ps[1] + gates[2] * ps[2]).astype(BF16)
        yb_ref[...] = yb.astype(BF16)
        yc_ref[...] = yc.astype(BF16)
        dm = _mm_nt(do_ref[...], wo_ref[...])
        dps = []
        first = pl.program_id(0) == 0
        for i, dref in enumerate((dpa_ref, dpb_ref, dpc_ref)):
            g = gates[i]
            dpi = (dm * g).astype(BF16)
            dref[...] = dpi
            dps.append(dpi)
            dgp = dm * ps[i] * g * (1.0 - g)
            st_gate[slot, :, i * D_MODEL:(i + 1) * D_MODEL] = dgp.astype(BF16)
            part = jnp.sum(dgp, axis=0, keepdims=True)

            @pl.when(first)
            def _():
                dbg_ref[:, i * D_MODEL:(i + 1) * D_MODEL] = part

            @pl.when(jnp.logical_not(first))
            def _():
                dbg_ref[:, i * D_MODEL:(i + 1) * D_MODEL] += part

        dya_ref[...] = _mm_nt(dps[0], woa_ref[...])
        dyb = _mm_nt(dps[1], wob_ref[...])
        dyc = _mm_nt(dps[2], woc_ref[...])
        st_bz[slot] = (dyb * obv * _dsilu(bz)).astype(BF16)
        st_cz[slot] = (dyc * oc * _dsilu(cz)).astype(BF16)
        for cp in copies_of(step):
            cp.start()
        dob = dyb * sb
        doc = dyc * sc
        dob_ref[...] = dob.astype(BF16)
        for c in range(NPAIR):
            cs = slice(c * LANE, (c + 1) * LANE)
            dlb_ref[:, cs] = _head_bcast_sum(dob[:, cs] * obv[:, cs])
            dd = _head_bcast_sum(doc[:, cs] * oc[:, cs])
            for a, dref, lref in zip(alphas, (dg0, dg1, dg2), (dl0, dl1, dl2)):
                dref[:, cs] = a[:, cs] * doc[:, cs]
                lref[:, cs] = a[:, cs] * dd

        @pl.when(step == nt - 1)
        def _():
            if nt >= 2:
                for cp in copies_of(step - 1):
                    cp.wait()
            for cp in copies_of(step):
                cp.wait()

    def whole(r, c):
        return pl.BlockSpec((r, c), lambda i: (0, 0))

    tok = lambda w: pl.BlockSpec((ts, w), lambda i: (i, 0))
    sd = jax.ShapeDtypeStruct
    W = DIL_WIDTH
    return pl.pallas_call(
        body, name="merge_bwd", grid=(nt,),
        in_specs=[tok(D_MODEL)] + _merge_proj_specs(ts) + [whole(1, 3 * D_MODEL), tok(CONV_WIDTH)] + [tok(W)] * 7
                 + [whole(CONV_WIDTH, D_MODEL)] * 3 + [whole(D_MODEL, D_MODEL)],
        out_specs=[pl.BlockSpec(memory_space=pl.ANY), tok(CONV_WIDTH), tok(W), tok(W)] + [tok(W)] * 6
                  + [tok(D_MODEL)] * 4 + [tok(W), tok(W), whole(1, 3 * D_MODEL)],
        out_shape=[sd((T, PP), BF16), sd((T, CONV_WIDTH), F32), sd((T, W), BF16), sd((T, W), F32)]
                  + [sd((T, W), F32)] * 6
                  + [sd((T, D_MODEL), BF16)] * 4 + [sd((T, W), BF16)] * 2 + [sd((1, 3 * D_MODEL), F32)],
        scratch_shapes=[pltpu.VMEM((2, ts, W), BF16), pltpu.VMEM((2, ts, W), BF16),
                        pltpu.VMEM((2, ts, 3 * D_MODEL), BF16), pltpu.SemaphoreType.DMA((2, 3))],
        compiler_params=_cp(),
    )(dout, *[proj] * 5, b_gate, ya, ob, *ogs, *lses, woa, wob, woc, wo)


def _loss_head(y, target):
    T = y.shape[0]
    ts = _tile(T, 512)

    def body(y_ref, t_ref, d_ref, l_ref):
        e = y_ref[...] - t_ref[...]
        d_ref[...] = e * (1.0 / D_MODEL)
        l_ref[...] = jnp.zeros((1, 8, LANE), F32) + jnp.sum(e * e)

    tok = pl.BlockSpec((ts, D_MODEL), lambda i: (i, 0))
    return pl.pallas_call(
        body, name="loss_head", grid=(T // ts,), in_specs=[tok, tok],
        out_specs=[tok, pl.BlockSpec((1, 8, LANE), lambda i: (i, 0, 0))],
        out_shape=[jax.ShapeDtypeStruct((T, D_MODEL), F32), jax.ShapeDtypeStruct((T // ts, 8, LANE), F32)],
        compiler_params=_cp(),
    )(y, target)


def _my_index():
    return 4 * lax.axis_index("x") + 2 * lax.axis_index("y") + lax.axis_index("c")


def _peers():
    x, y, c = (lax.axis_index(a) for a in AXES)
    out = []
    for kk in range(1, N_DEV):
        px = 1 - x if kk & 4 else x
        py = 1 - y if kk & 2 else y
        pc = 1 - c if kk & 1 else c
        out.append(((px, py, pc), 4 * px + 2 * py + pc))
    return out


def _exchange(arrays, name, gather):
    n = len(arrays)

    def body(*refs):
        srcs, outs = refs[:n], refs[n:2 * n]
        send_sems, recv_sems, local_sems = refs[2 * n:]
        me = _my_index()
        peers = _peers()
        started = []
        for a, (src, out) in enumerate(zip(srcs, outs)):
            mine = pltpu.make_async_copy(src if gather else src.at[me], out.at[me], local_sems.at[a])
            mine.start()
            started.append(mine)
        sends = []
        for i, (pos, idx) in enumerate(peers):
            for a, (src, out) in enumerate(zip(srcs, outs)):
                cp = pltpu.make_async_remote_copy(
                    src_ref=src if gather else src.at[idx], dst_ref=out.at[me], send_sem=send_sems.at[a, i],
                    recv_sem=recv_sems.at[a, i], device_id=pos, device_id_type=pl.DeviceIdType.MESH)
                cp.start()
                sends.append(cp)
        for i, (pos, idx) in enumerate(peers):
            for a, (src, out) in enumerate(zip(srcs, outs)):
                pltpu.make_async_remote_copy(
                    src_ref=src if gather else src.at[idx], dst_ref=out.at[idx], send_sem=send_sems.at[a, i],
                    recv_sem=recv_sems.at[a, i], device_id=pos, device_id_type=pl.DeviceIdType.MESH).wait_recv()
        for cp in sends:
            cp.wait_send()
        for mine in started:
            mine.wait()

    any_space = pl.BlockSpec(memory_space=pl.ANY)
    return pl.pallas_call(
        body, name=name, in_specs=[any_space] * n, out_specs=[any_space] * n,
        out_shape=[jax.ShapeDtypeStruct(((N_DEV,) + a.shape) if gather else a.shape, a.dtype) for a in arrays],
        scratch_shapes=[pltpu.SemaphoreType.DMA((n, N_DEV - 1)), pltpu.SemaphoreType.DMA((n, N_DEV - 1)),
                        pltpu.SemaphoreType.DMA((n,))],
    )(*arrays)


N_CHIP = 4


def _chip_places():
    x, y, c = (lax.axis_index(a) for a in AXES)
    return (x, y, c), (x, y, 1 - c), [(1 - x, y, c), (x, 1 - y, c), (1 - x, 1 - y, c)]


def _index_of(pos):
    return 4 * pos[0] + 2 * pos[1] + pos[2]


def _gather_two_level(arrays, name):
    n = len(arrays)

    def body(*refs):
        srcs, outs = refs[:n], refs[n:2 * n]
        send_sems, recv_sems, local_sems = refs[2 * n:]
        me, sibling, others = _chip_places()

        def copy(a, k, block, to, src=None):
            slot = outs[a].at[_index_of(block)]
            return pltpu.make_async_remote_copy(
                src_ref=slot if src is None else src, dst_ref=slot, send_sem=send_sems.at[7 * a + k],
                recv_sem=recv_sems.at[7 * a + k], device_id=to, device_id_type=pl.DeviceIdType.MESH)

        started = []
        for a, src in enumerate(srcs):
            mine = pltpu.make_async_copy(src, outs[a].at[_index_of(me)], local_sems.at[a])
            mine.start()
            started.append(mine)
        sends = []
        for a, src in enumerate(srcs):
            sends.append(copy(a, 0, me, sibling, src))
            sends += [copy(a, 1 + j, me, chip, src) for j, chip in enumerate(others)]
        for cp in sends:
            cp.start()
        for j, chip in enumerate(others):
            for a in range(n):
                copy(a, 1 + j, chip, me).wait_recv()
                fwd = copy(a, 4 + j, chip, sibling)
                fwd.start()
                sends.append(fwd)
        for a in range(n):
            copy(a, 0, sibling, me).wait_recv()
            for j, chip in enumerate(others):
                copy(a, 4 + j, (chip[0], chip[1], sibling[2]), me).wait_recv()
        for cp in sends:
            cp.wait_send()
        for mine in started:
            mine.wait()

    any_space = pl.BlockSpec(memory_space=pl.ANY)
    return pl.pallas_call(
        body, name=name, in_specs=[any_space] * n, out_specs=[any_space] * n,
        out_shape=[jax.ShapeDtypeStruct((N_DEV,) + a.shape, a.dtype) for a in arrays],
        scratch_shapes=[pltpu.SemaphoreType.DMA((7 * n,)), pltpu.SemaphoreType.DMA((7 * n,)),
                        pltpu.SemaphoreType.DMA((n,))],
    )(*arrays)


def _sibling_swap(arrays, name):
    n = len(arrays)

    def body(*refs):
        srcs, outs = refs[:n], refs[n:2 * n]
        send_sems, recv_sems = refs[2 * n:]
        (x, y, c), sibling, _ = _chip_places()
        sends = []
        for a, (src, out) in enumerate(zip(srcs, outs)):
            for q in range(N_CHIP):
                def copy(core, a=a, q=q, src=src, out=out):
                    return pltpu.make_async_remote_copy(
                        src_ref=src.at[2 * q + core], dst_ref=out.at[q], send_sem=send_sems.at[N_CHIP * a + q],
                        recv_sem=recv_sems.at[N_CHIP * a + q], device_id=sibling, device_id_type=pl.DeviceIdType.MESH)
                mine = copy(1 - c)
                mine.start()
                sends.append((mine, copy(c)))
        for mine, arrival in sends:
            arrival.wait_recv()
            mine.wait_send()

    any_space = pl.BlockSpec(memory_space=pl.ANY)
    return pl.pallas_call(
        body, name=name, in_specs=[any_space] * n, out_specs=[any_space] * n,
        out_shape=[jax.ShapeDtypeStruct((N_CHIP,) + a.shape[1:], a.dtype) for a in arrays],
        scratch_shapes=[pltpu.SemaphoreType.DMA((N_CHIP * n,)), pltpu.SemaphoreType.DMA((N_CHIP * n,))],
    )(*arrays)


def _chip_pair_sum(part, got, name):
    R, C = part.shape[1:]
    tr = R
    while tr * C * part.dtype.itemsize > REDUCE_BLOCK_BYTES // 4 and tr % 32 == 0:
        tr //= 2
    c = lax.axis_index("c")

    def body(c_ref, p_ref, g_ref, o_ref):
        del c_ref
        o_ref[...] = (p_ref[...].astype(F32) + g_ref[...].astype(F32)).astype(o_ref.dtype)

    return pl.pallas_call(
        body, name=name, grid_spec=pltpu.PrefetchScalarGridSpec(
            num_scalar_prefetch=1, grid=(N_CHIP, R // tr),
            in_specs=[pl.BlockSpec((None, tr, C), lambda q, i, cr: (2 * q + cr[0], i, 0)),
                      pl.BlockSpec((None, tr, C), lambda q, i, cr: (q, i, 0))],
            out_specs=pl.BlockSpec((None, tr, C), lambda q, i, cr: (q, i, 0))),
        out_shape=jax.ShapeDtypeStruct((N_CHIP, R, C), part.dtype),
        compiler_params=_cp(),
    )(jnp.reshape(c, (1,)).astype(jnp.int32), part, got)


def _peer_count(mode):
    return N_CHIP - 1 if mode == "chips" else N_DEV - 1


def _remote_copies(srcs, lands, send_sems, recv_sems, mode):
    if mode == "chips":
        (x, y, _), _, others = _chip_places()
        my_slot, peers = 2 * x + y, [(chip, 2 * chip[0] + chip[1]) for chip in others]
    else:
        my_slot, peers = _my_index(), _peers()
    out = []
    for i, (pos, idx) in enumerate(peers):
        for a, (src, land) in enumerate(zip(srcs, lands)):
            def copy(slot, a=a, src=src, land=land, i=i, pos=pos, idx=idx):
                return pltpu.make_async_remote_copy(
                    src_ref=src if mode == "gather" else src.at[idx], dst_ref=land.at[slot],
                    send_sem=send_sems.at[a * len(peers) + i], recv_sem=recv_sems.at[a * len(peers) + i],
                    device_id=pos, device_id_type=pl.DeviceIdType.MESH)
            out.append((copy(my_slot), copy(idx)))
    return out


def _exchange_start(arrays, name, mode):
    n = len(arrays)
    hbm = pl.BlockSpec(memory_space=pltpu.HBM)
    sem = pl.BlockSpec(memory_space=pltpu.SEMAPHORE)
    lands = [lax.empty(((N_DEV,) + a.shape) if mode == "gather" else a.shape, a.dtype) for a in arrays]

    def body(*refs):
        srcs, lands_ = refs[:n], refs[n:2 * n]
        send_sems, recv_sems = refs[2 * n:2 * n + 2]
        for mine, _ in _remote_copies(srcs, lands_, send_sems, recv_sems, mode):
            mine.start()
        refs[-1][...] = jnp.zeros_like(refs[-1])

    sems = pltpu.SemaphoreType.DMA((n * _peer_count(mode),))
    buffers = [pltpu.HBM(a.shape, a.dtype) for a in list(arrays) + lands]
    res = pl.pallas_call(
        body, name=name, in_specs=[hbm] * (2 * n), out_specs=[sem, sem] + [hbm] * (2 * n) + [pl.BlockSpec(memory_space=pltpu.VMEM)],
        out_shape=[sems, sems] + buffers + [jax.ShapeDtypeStruct((8, LANE), F32)],
        input_output_aliases={i: 2 + i for i in range(2 * n)},
        compiler_params=pltpu.CompilerParams(has_side_effects=pltpu.SideEffectType.DATAFLOW_SIDE_EFFECTING),
    )(*[pltpu.with_memory_space_constraint(a, pltpu.HBM) for a in list(arrays) + lands])
    return (res[0], res[1], res[2:2 + n], res[2 + n:2 + 2 * n]), res[-1]


def _exchange_wait(handle, after, name, mode):
    send_sems, recv_sems, srcs, lands = handle
    n = len(srcs)
    hbm = pl.BlockSpec(memory_space=pltpu.HBM)
    sem = pl.BlockSpec(memory_space=pltpu.SEMAPHORE)

    def body(*refs):
        for mine, arrival in _remote_copies(refs[:n], refs[n:2 * n], refs[2 * n], refs[2 * n + 1], mode):
            mine.wait_send()
            arrival.wait_recv()

    res = pl.pallas_call(
        body, name=name, in_specs=[hbm] * (2 * n) + [sem, sem, pl.BlockSpec(memory_space=pl.ANY)],
        out_specs=[hbm] * (2 * n), out_shape=[pltpu.HBM(a.shape, a.dtype) for a in list(srcs) + list(lands)],
        input_output_aliases={i: i for i in range(2 * n)},
        compiler_params=pltpu.CompilerParams(has_side_effects=pltpu.SideEffectType.DATAFLOW_SIDE_EFFECTING),
    )(*srcs, *lands, send_sems, recv_sems, after)
    return res[n:]


def _own_slot(land, mine, slot=None):
    slot = _my_index() if slot is None else slot
    return lax.dynamic_update_slice(land, mine, (slot,) + (0,) * (land.ndim - 1))


def _adamw(w, g, m, v):
    m = ADAM_B1 * m + (1.0 - ADAM_B1) * g
    v = ADAM_B2 * v + (1.0 - ADAM_B2) * (g * g)
    m_hat = m / (1.0 - ADAM_B1 ** ADAM_STEP)
    v_hat = v / (1.0 - ADAM_B2 ** ADAM_STEP)
    delta = -ADAM_LR * (m_hat / (jnp.sqrt(v_hat) + ADAM_EPS) + ADAM_WD * w)
    return delta, m, v


def _reduce_adamw(parts, w, m, v, name):
    nparts = len(parts)
    R, C = parts[0].shape[1:]
    tr = R
    while N_DEV * tr * C * parts[0].dtype.itemsize > REDUCE_BLOCK_BYTES and tr % 32 == 0:
        tr //= 2
    steps = R // tr

    def body(*refs):
        w_ref, m_ref, v_ref, g_ref, d_ref, nm_ref, nv_ref = refs[nparts:]
        for k, p_ref in enumerate(refs[:nparts]):
            @pl.when(pl.program_id(0) // steps == k)
            def _():
                g = p_ref[0].astype(F32)
                for s in range(1, p_ref.shape[0]):
                    g = g + p_ref[s].astype(F32)
                g_ref[...] = g
                d_ref[...], nm_ref[...], nv_ref[...] = _adamw(w_ref[...], g, m_ref[...], v_ref[...])

    def part_spec(k):
        return pl.BlockSpec((parts[k].shape[0], tr, C), lambda i: (0, jnp.clip(i - k * steps, 0, steps - 1), 0))

    row = pl.BlockSpec((tr, C), lambda i: (i, 0))
    return pl.pallas_call(
        body, name=name, grid=(nparts * steps,),
        in_specs=[part_spec(k) for k in range(nparts)] + [row, row, row],
        out_specs=[row] * 4, out_shape=[jax.ShapeDtypeStruct((nparts * R, C), F32)] * 4,
        compiler_params=_cp(),
    )(*parts, w, m, v)


BIG = ("w_in", "w_uq", "w_ukv", "w_out_a", "w_out_b", "w_out_c", "w_o")
SMALL = ("norm_g", "b_gate", "conv_w", "conv_b", "q_a_norm_g", "kv_a_norm_g", "mla_q_norm_g", "mla_k_norm_g",
         "dil_q_norm_g", "dil_k_norm_g")
PACK_ROWS = 128
REDUCE_BLOCK_BYTES = 6 * 1024 * 1024


def _pack_local(tensors):
    flat = jnp.concatenate([t.reshape(-1) for t in tensors])
    pad = (-flat.shape[0]) % (PACK_ROWS * LANE)
    return jnp.concatenate([flat, jnp.zeros((pad,), flat.dtype)]).reshape(-1, LANE)


def _unpack_local(rows, like):
    flat = rows.reshape(-1)
    out, off = [], 0
    for t in like:
        out.append(flat[off:off + t.size].reshape(t.shape))
        off += t.size
    return out


def _cols_to_slots(a):
    k = a.shape[0]
    return a.reshape(k, N_DEV, -1).transpose(1, 0, 2)


def _slots_to_cols(s):
    return s.transpose(1, 0, 2).reshape(s.shape[1], -1)


def _rope_tables(S):
    inv = ROPE_THETA ** (-jnp.arange(0, MLA_ROPE, 2, dtype=F32) / MLA_ROPE)
    ang = jnp.arange(S, dtype=F32)[:, None] * inv[None, :]
    cos, sin = jnp.cos(ang), jnp.sin(ang)
    one = jnp.ones((S, MLA_NOPE), F32)
    z16, z32, z64 = (jnp.zeros((S, n), F32) for n in (16, 32, 64))
    cosp = jnp.concatenate([one, cos, cos, jnp.ones((S, 32), F32)], axis=1)
    sa = jnp.concatenate([z64, -sin, z16, z32], axis=1)
    sb = jnp.concatenate([z64, z16, sin, z32], axis=1)
    return cosp, sa, sb


def _alibi_slopes():
    n = DIL_GROUPS * DIL_HEADS
    m = 2.0 ** (-8.0 * jnp.arange(1, n + 1, dtype=F32) / n)
    return m.reshape(DIL_GROUPS, NPAIR, 2)


def _pad_slots(s):
    n, k, c = s.shape
    return _slots_to_cols(jnp.concatenate([s, jnp.zeros((n, k, LANE - c), s.dtype)], axis=2))


def _layer_params(gw, small, l):
    p = {}
    p["wp"] = _pad_columns(gw["w_in"])
    p["norm_g"] = small["norm_g"][l][None]
    p["b_gate"] = small["b_gate"][l][None]
    p["conv_w"] = gw["conv_w"].transpose(1, 0, 2).reshape(CONV_K, CONV_WIDTH)
    p["conv_b"] = small["conv_b"][l][None]
    p["gq"] = small["q_a_norm_g"][l][None]
    p["gkv"] = small["kv_a_norm_g"][l][None]
    p["wuqp"] = _pad_slots(gw["w_uq"])
    kv = gw["w_ukv"]
    p["wkp"] = _pad_slots(kv[:, :, :MLA_NOPE])
    p["wv"] = kv[:, :, MLA_NOPE:].transpose(1, 0, 2).reshape(MLA_KV_LORA, MLA_HEADS * MLA_V)
    zpad = jnp.zeros((1, LANE - MLA_QK), F32)
    p["gmq"] = jnp.concatenate([small["mla_q_norm_g"][l][None], zpad], axis=1)
    p["gmk"] = jnp.concatenate([small["mla_k_norm_g"][l][None], zpad], axis=1)
    tile = lambda g: jnp.broadcast_to(g[:, None, :], (DIL_GROUPS, DIL_HEADS, DIL_HEAD_DIM)).reshape(1, DIL_QK)
    p["gdq"] = tile(small["dil_q_norm_g"][l])
    p["gdk"] = tile(small["dil_k_norm_g"][l])
    p["woa"], p["wob"], p["woc"] = (_slots_to_cols(gw[n]) for n in ("w_out_a", "w_out_b", "w_out_c"))
    p["wo"] = gw["w_o"].reshape(D_MODEL, D_MODEL)
    return p


def _layer_fwd(x, p, tabs, slopes, B, S):
    proj, ht = _inproj_fwd(x, p["norm_g"], p["wp"])
    ya = _mixa_fwd(proj, p["conv_w"], p["conv_b"], B, S)
    q, k, v = _mla_prep_fwd(proj, p["gq"], p["gkv"], p["wuqp"], p["wkp"], p["wv"], p["gmq"], p["gmk"], *tabs, S)
    ob, lse_b = _mla_attn_fwd(q, k, v, B, S)
    qn, kn, vn = _dil_prep_fwd(proj, p["gdq"], p["gdk"])
    ogs, lses = [], []
    for gi in range(DIL_GROUPS):
        o, lse = _dil_attn_fwd(gi, slopes[gi], qn, kn, vn, B, S)
        ogs.append(o)
        lses.append(lse)
    out = _merge_fwd(x, proj, p["b_gate"], ya, ob, ogs, lses, p["woa"], p["wob"], p["woc"], p["wo"])
    saved = dict(x=x, proj=proj, ht=ht, ya=ya, q=q, k=k, v=v, ob=ob, lse_b=lse_b, qn=qn, kn=kn, vn=vn, ogs=ogs, lses=lses)
    return out, saved


def _layer_bwd(dout, sv, p, tabs, slopes, B, S, big_ready=None):
    proj = sv["proj"]
    (dproj, dya, dob, dlb, dg0, dg1, dg2, dl0, dl1, dl2, merged, dpa, dpb, dpc, yb, yc, dbg) = _merge_bwd(
        dout, proj, p["b_gate"], sv["ya"], sv["ob"], sv["ogs"], sv["lses"], p["woa"], p["wob"], p["woc"], p["wo"])
    g = {}
    g["w_o"] = _matmul_tn(merged, dout, "dw_o").reshape(N_DEV, D_MODEL // N_DEV, D_MODEL)
    g["w_out_a"] = _cols_to_slots(_matmul_tn(sv["ya"], dpa, "dw_out_a"))
    g["w_out_b"] = _cols_to_slots(_matmul_tn(yb, dpb, "dw_out_b"))
    g["w_out_c"] = _cols_to_slots(_matmul_tn(yc, dpc, "dw_out_c"))
    g["b_gate"] = dbg[0]
    dproj, st = _mixa_bwd(dproj, dya, proj, p["conv_w"], p["conv_b"], B, S)
    g["conv_w"] = st[0:CONV_K]
    g["conv_b"] = st[CONV_K]
    dq, dk, dv = _mla_attn_bwd(sv["q"], sv["k"], sv["v"], dob, sv["lse_b"], dlb, B, S)
    dproj, dwuqp, dwkp, dwv, dgq, dgkv, dgmq, dgmk = _mla_prep_bwd(
        dproj, dq, dk, dv, proj, p["gq"], p["gkv"], p["wuqp"], p["wkp"], p["wv"], p["gmq"], p["gmk"], *tabs, S)
    g["w_uq"] = _cols_to_slots(dwuqp)[:, :, :MLA_QK]
    g["w_ukv"] = jnp.concatenate([_cols_to_slots(dwkp)[:, :, :MLA_NOPE], _cols_to_slots(dwv)], axis=2)
    g["q_a_norm_g"], g["kv_a_norm_g"] = dgq[0], dgkv[0]
    g["mla_q_norm_g"], g["mla_k_norm_g"] = dgmq[0, :MLA_QK], dgmk[0, :MLA_QK]
    dqkv = None
    for gi, (dog, dlg) in enumerate(((dg0, dl0), (dg1, dl1), (dg2, dl2))):
        dqkv = _dil_attn_bwd(gi, slopes[gi], sv["qn"], sv["kn"], sv["vn"], dog, sv["lses"][gi], dlg, dqkv, B, S)
    dproj, dgdq, dgdk = _dil_prep_bwd(dproj, *dqkv, proj, p["gdq"], p["gdk"])
    g["dil_q_norm_g"] = dgdq.reshape(DIL_GROUPS, DIL_HEADS, DIL_HEAD_DIM).sum(axis=1)
    g["dil_k_norm_g"] = dgdk.reshape(DIL_GROUPS, DIL_HEADS, DIL_HEAD_DIM).sum(axis=1)
    g["w_in"] = _unpad_columns(_matmul_nn(sv["ht"], dproj, "dw_in"))
    token = None if big_ready is None else big_ready(g)
    dx, dng = _inproj_bwd_x(dproj, p["wp"], sv["x"], _after(token, p["norm_g"]), dout)
    g["norm_g"] = dng[0]
    return dx, g


def _after(token, a):
    return a if token is None else a + token[0:1, 0:1]


def _local_step(x, target, small, B, S, weights_of, grads_out, big_ready=None):
    tabs = _rope_tables(S)
    sl = _alibi_slopes()
    slopes = [sl[gi] * float(DIL_PATTERNS[gi][1]) for gi in range(DIL_GROUPS)]
    params, saved = [], []
    for l in range(DEPTH):
        gw, token = weights_of(l, x)
        p = _layer_params(gw, small, l)
        p["norm_g"] = _after(token, p["norm_g"])
        x, sv = _layer_fwd(x, p, tabs, slopes, B, S)
        params.append(p)
        saved.append(sv)
    dout, lparts = _loss_head(x, target)
    sq = jnp.sum(lparts[:, 0, 0])
    token = None
    for l in reversed(range(DEPTH)):
        p = dict(params[l], b_gate=_after(token, params[l]["b_gate"]))
        ready = None if big_ready is None else (lambda g, l=l: big_ready(l, g))
        dout, g = _layer_bwd(dout, saved[l], p, tabs, slopes, B, S, ready)
        token = grads_out(l, g, dout)
    return sq, dout


def kernel(x, norm_g, w_in, b_gate, conv_w, conv_b, q_a_norm_g, w_uq, kv_a_norm_g, w_ukv, mla_q_norm_g, mla_k_norm_g, dil_q_norm_g, dil_k_norm_g, w_out_a, w_out_b, w_out_c, w_o, loss_target, m_norm_g, m_w_in, m_b_gate, m_conv_w, m_conv_b, m_q_a_norm_g, m_w_uq, m_kv_a_norm_g, m_w_ukv, m_mla_q_norm_g, m_mla_k_norm_g, m_dil_q_norm_g, m_dil_k_norm_g, m_w_out_a, m_w_out_b, m_w_out_c, m_w_o, v_norm_g, v_w_in, v_b_gate, v_conv_w, v_conv_b, v_q_a_norm_g, v_w_uq, v_kv_a_norm_g, v_w_ukv, v_mla_q_norm_g, v_mla_k_norm_g, v_dil_q_norm_g, v_dil_k_norm_g, v_w_out_a, v_w_out_b, v_w_out_c, v_w_o):
    names = ("norm_g", "w_in", "b_gate", "conv_w", "conv_b", "q_a_norm_g", "w_uq", "kv_a_norm_g", "w_ukv",
             "mla_q_norm_g", "mla_k_norm_g", "dil_q_norm_g", "dil_k_norm_g", "w_out_a", "w_out_b", "w_out_c", "w_o")
    w = dict(zip(names, (norm_g, w_in, b_gate, conv_w, conv_b, q_a_norm_g, w_uq, kv_a_norm_g, w_ukv, mla_q_norm_g,
                         mla_k_norm_g, dil_q_norm_g, dil_k_norm_g, w_out_a, w_out_b, w_out_c, w_o)))
    m = dict(zip(names, (m_norm_g, m_w_in, m_b_gate, m_conv_w, m_conv_b, m_q_a_norm_g, m_w_uq, m_kv_a_norm_g, m_w_ukv,
                         m_mla_q_norm_g, m_mla_k_norm_g, m_dil_q_norm_g, m_dil_k_norm_g, m_w_out_a, m_w_out_b,
                         m_w_out_c, m_w_o)))
    v = dict(zip(names, (v_norm_g, v_w_in, v_b_gate, v_conv_w, v_conv_b, v_q_a_norm_g, v_w_uq, v_kv_a_norm_g, v_w_ukv,
                         v_mla_q_norm_g, v_mla_k_norm_g, v_dil_q_norm_g, v_dil_k_norm_g, v_w_out_a, v_w_out_b,
                         v_w_out_c, v_w_o)))
    B, S, _ = x.shape
    me = _my_index()
    cshard = CONV_WIDTH // N_DEV

    shards = [[w[n][l].astype(BF16) for n in BIG] for l in range(DEPTH)]
    state = {}

    def weights_of(l, after):
        if l == 0:
            got = _gather_two_level(shards[0] + [conv_w], "all_gather_weights_0")
            state["gather"], token = _exchange_start(shards[1], "all_gather_weights_1_start", "gather")
            state["conv_w"] = got[-1]
        else:
            landed = _exchange_wait(state["gather"], after, "all_gather_weights_1_wait", "gather")
            got, token = [_own_slot(a, s[None]) for a, s in zip(landed, shards[1])], None
        gw = dict(zip(BIG, got))
        gw["conv_w"] = state["conv_w"][:, l]
        return gw, token

    recv, small_parts = {}, {}
    my_chip = 2 * lax.axis_index("x") + lax.axis_index("y")

    def big_ready(l, g):
        send = [g[n].astype(BF16) for n in BIG]
        if l == DEPTH - 1:
            state["scatter"], token = _exchange_start(send, "exchange_weight_grads_1_start", "scatter")
        else:
            swapped = _sibling_swap(send, "exchange_weight_grads_0_sibling")
            send = [_chip_pair_sum(s, t, "chip_pair_sum_" + n) for n, s, t in zip(BIG, send, swapped)]
            state["chips"], token = _exchange_start(send, "exchange_weight_grads_0_start", "chips")
        state["sent", l] = send
        return token

    def grads_out(l, g, after):
        small_parts[l] = [g[n] for n in SMALL]
        if l == DEPTH - 1:
            return None
        for k, key, mode, slot in ((DEPTH - 1, "scatter", "scatter", me), (0, "chips", "chips", my_chip)):
            landed = _exchange_wait(state[key], after, f"exchange_weight_grads_{k}_wait", mode)
            mine = [lax.dynamic_slice_in_dim(s, slot, 1, axis=0) for s in state["sent", k]]
            recv[k] = [_own_slot(a, s, slot) for a, s in zip(landed, mine)]
        return None

    sq, grad_x = _local_step(x.reshape(B * S, D_MODEL), loss_target.reshape(B * S, D_MODEL), w, B, S,
                             weights_of, grads_out, big_ready)
    loss = lax.psum(sq * (0.5 / D_MODEL), AXES)

    res = {}
    for i, n in enumerate(BIG):
        rows = lambda a: a.reshape(-1, a.shape[-1])
        outs = _reduce_adamw([recv[l][i] for l in range(DEPTH)], rows(w[n]), rows(m[n]), rows(v[n]),
                             "reduce_adamw_" + n)
        res[n] = tuple(a.reshape(w[n].shape) for a in outs)
    part = {n: jnp.stack([small_parts[l][i] for l in range(DEPTH)]) for i, n in enumerate(SMALL)}

    def widen(t):
        return lax.dynamic_update_slice(jnp.zeros((DEPTH, CONV_K, CONV_WIDTH), F32), t, (0, 0, me * cshard))

    small_like = [part[n] for n in SMALL]
    pick = lambda d: [widen(d[n]) if n == "conv_w" else d[n] for n in SMALL]
    parts, = _exchange([_pack_local(small_like)], "all_gather_small_grads", gather=True)
    gs, ds, ms, vs = _reduce_adamw([parts], _pack_local(pick(w)), _pack_local(pick(m)), _pack_local(pick(v)),
                                   "reduce_adamw_small")
    for n, t in zip(SMALL, zip(*(_unpack_local(a, small_like) for a in (gs, ds, ms, vs)))):
        if n == "conv_w":
            t = tuple(lax.dynamic_slice(a, (0, 0, me * cshard), (DEPTH, CONV_K, cshard)) for a in t)
        res[n] = t

    out = [loss, grad_x.reshape(B, S, D_MODEL)]
    for i in range(4):
        out += [res[n][i] for n in names]
    return tuple(out)
```

```python
import jax
import jax.numpy as jnp
from jax import lax
from jax.experimental import pallas as pl
from jax.experimental.pallas import tpu as pltpu

F32 = jnp.float32
BF16 = jnp.bfloat16

D_MODEL = 1024
DEPTH = 2
CONV_WIDTH = 512
CONV_K = 3
MLA_HEADS = 8
MLA_Q_LORA = 256
MLA_KV_LORA = 128
MLA_NOPE = 64
MLA_ROPE = 32
MLA_V = 64
MLA_QK = MLA_NOPE + MLA_ROPE
ROPE_THETA = 10000.0
DIL_PATTERNS = ((128, 1), (512, 4), (2048, 16))
DIL_GROUPS = 3
DIL_HEADS = 8
DIL_HEAD_DIM = 64
DIL_WIDTH = DIL_HEADS * DIL_HEAD_DIM
DIL_QK = DIL_GROUPS * DIL_WIDTH
EPS = 1e-6
N_IN = 11168

ADAM_LR = 0.001
ADAM_B1 = 0.9
ADAM_B2 = 0.999
ADAM_EPS = 1e-08
ADAM_WD = 0.01
ADAM_STEP = 10

N_DEV = 8
AXES = ("x", "y", "c")
LANE = 128
HALF = 64
NPAIR = 4

CB_AB, CB_AC, CB_AX, CB_AZ = 0, 4, 8, 12
CB_CQ, CB_CKV, CB_KPE = 16, 18, 19
CB_BZ = 20
CB_DQ, CB_DK, CB_DV = 24, 36, 48
CB_CZ, CB_GATE = 60, 64
NCB = 88
PP = NCB * LANE
KPE_END = CB_KPE * LANE + MLA_ROPE
SHARD_COLS = N_IN // N_DEV
NEG = -1e30
VMEM_LIMIT = 56 * 1024 * 1024


def _pad_columns(shards):
    parts = []
    for p in range(N_DEV):
        cut = min(max(KPE_END - p * SHARD_COLS, 0), SHARD_COLS)
        if 0 < cut < SHARD_COLS:
            parts += [shards[p, :, :cut], jnp.zeros((shards.shape[1], LANE - MLA_ROPE), shards.dtype), shards[p, :, cut:]]
        else:
            parts.append(shards[p])
    return jnp.concatenate(parts, axis=1)


def _unpad_columns(wp):
    def columns(a, b):
        gap = LANE - MLA_ROPE
        if b <= KPE_END:
            return wp[:, a:b]
        if a >= KPE_END:
            return wp[:, a + gap:b + gap]
        return jnp.concatenate([wp[:, a:KPE_END], wp[:, KPE_END + gap:b + gap]], axis=1)

    return jnp.stack([columns(p * SHARD_COLS, (p + 1) * SHARD_COLS) for p in range(N_DEV)])


def _put_copies(stages, dst_ref, sems, slot, rows, cols):
    return [pltpu.make_async_copy(st.at[slot], dst_ref.at[rows, pl.ds(c0, st.shape[-1])], sems.at[slot, k])
            for k, (st, c0) in enumerate(zip(stages, cols))]


def _put_pipeline(step, nsteps, copies_of, fill):
    @pl.when(step >= 2)
    def _():
        for cp in copies_of(step - 2):
            cp.wait()

    fill(step % 2)
    for cp in copies_of(step):
        cp.start()

    @pl.when(step == nsteps - 1)
    def _():
        if nsteps >= 2:
            for cp in copies_of(step - 1):
                cp.wait()
        for cp in copies_of(step):
            cp.wait()


def _cp():
    return pltpu.CompilerParams(vmem_limit_bytes=VMEM_LIMIT)


def _rstd(x, n):
    return lax.rsqrt(jnp.sum(x * x, axis=-1, keepdims=True) * (1.0 / n) + EPS)


def _sigmoid(z):
    return 1.0 / (1.0 + jnp.exp(-z))


def _silu(z):
    return z * _sigmoid(z)


def _dsilu(z):
    s = _sigmoid(z)
    return s * (1.0 + z * (1.0 - s))


def _mm(a, b):
    return jnp.dot(a.astype(BF16), b.astype(BF16), preferred_element_type=F32)


def _mm_nt(a, b):
    return lax.dot_general(a.astype(BF16), b.astype(BF16), (((1,), (1,)), ((), ())), preferred_element_type=F32)


def _mm_tn(a, b):
    return lax.dot_general(a.astype(BF16), b.astype(BF16), (((0,), (0,)), ((), ())), preferred_element_type=F32)


def _lane_lo(shape):
    return lax.broadcasted_iota(jnp.int32, shape, len(shape) - 1) < HALF


def _head_bcast_sum(x, terms=3):
    w = x.shape[-1]
    same = (lax.broadcasted_iota(jnp.int32, (w, w), 0) // HALF) == (lax.broadcasted_iota(jnp.int32, (w, w), 1) // HALF)
    ones = jnp.where(same, 1.0, 0.0).astype(jnp.bfloat16)
    total = None
    for _ in range(terms):
        term = x.astype(jnp.bfloat16)
        x = x - term.astype(F32)
        part = jnp.dot(term, ones, preferred_element_type=F32)
        total = part if total is None else total + part
    return total


def _rope(t, cos, sa, sb):
    return t * cos + pltpu.roll(t, LANE - 16, axis=1) * sa + pltpu.roll(t, 16, axis=1) * sb


def _rope_t(d, cos, sa, sb):
    return d * cos + pltpu.roll(d * sa, 16, axis=1) + pltpu.roll(d * sb, LANE - 16, axis=1)


def _shift_down(u, k):
    rows = lax.broadcasted_iota(jnp.int32, u.shape, 0)
    return jnp.where(rows >= k, pltpu.roll(u, k, axis=0), 0.0)


def _shift_up(u, k):
    n = u.shape[0]
    rows = lax.broadcasted_iota(jnp.int32, u.shape, 0)
    return jnp.where(rows < n - k, pltpu.roll(u, n - k, axis=0), 0.0)


def _tile(n, want):
    t = min(n, want)
    assert n % t == 0, (n, want)
    return t


def _inproj_fwd(x, g, wp):
    T = x.shape[0]
    tm, tn = _tile(T, 2048), 512

    def body(x_ref, g_ref, w_ref, proj_ref, ht_ref, h_ref):
        @pl.when(pl.program_id(1) == 0)
        def _():
            n = min(tm, 512)
            for r0 in range(0, tm, n):
                xv = x_ref[r0:r0 + n, :]
                h = xv * _rstd(xv, D_MODEL) * g_ref[...]
                h_ref[r0:r0 + n, :] = h.astype(BF16)
                ht_ref[:, r0:r0 + n] = h.T.astype(BF16)

        proj_ref[...] = jnp.dot(h_ref[...], w_ref[...], preferred_element_type=F32).astype(BF16)

    return pl.pallas_call(
        body, name="inproj_fwd", grid=(T // tm, PP // tn),
        in_specs=[pl.BlockSpec((tm, D_MODEL), lambda i, j: (i, 0)),
                  pl.BlockSpec((1, D_MODEL), lambda i, j: (0, 0)),
                  pl.BlockSpec((D_MODEL, tn), lambda i, j: (0, j))],
        out_specs=[pl.BlockSpec((tm, tn), lambda i, j: (i, j)),
                   pl.BlockSpec((D_MODEL, tm), lambda i, j: (0, i))],
        out_shape=[jax.ShapeDtypeStruct((T, PP), BF16), jax.ShapeDtypeStruct((D_MODEL, T), BF16)],
        scratch_shapes=[pltpu.VMEM((tm, D_MODEL), BF16)],
        compiler_params=_cp(),
    )(x, g, wp)


def _matmul_nn(at, b, name):
    K, T = at.shape
    N = b.shape[1]
    tt, tn = _tile(T, 1024), _tile(N, 2816)
    nk = T // tt

    def body(a_ref, b_ref, o_ref, acc_ref):
        k = pl.program_id(1)

        @pl.when(k == 0)
        def _():
            acc_ref[...] = jnp.zeros_like(acc_ref)

        acc_ref[...] += jnp.dot(a_ref[...], b_ref[...], preferred_element_type=F32)

        @pl.when(k == nk - 1)
        def _():
            o_ref[...] = acc_ref[...].astype(BF16)

    return pl.pallas_call(
        body, name=name, grid=(N // tn, nk),
        in_specs=[pl.BlockSpec((K, tt), lambda j, k: (0, k)),
                  pl.BlockSpec((tt, tn), lambda j, k: (k, j))],
        out_specs=pl.BlockSpec((K, tn), lambda j, k: (0, j)),
        out_shape=jax.ShapeDtypeStruct((K, N), BF16),
        scratch_shapes=[pltpu.VMEM((K, tn), F32)],
        compiler_params=_cp(),
    )(at, b)


def _matmul_tn(a, b, name):
    T, K = a.shape
    N = b.shape[1]
    tt, tn = _tile(T, 512), _tile(N, 1024)

    def body(a_ref, b_ref, o_ref):
        @pl.when(pl.program_id(1) == 0)
        def _():
            o_ref[...] = jnp.zeros_like(o_ref)

        o_ref[...] += _mm_tn(a_ref[...], b_ref[...])

    return pl.pallas_call(
        body, name=name, grid=(N // tn, T // tt),
        in_specs=[pl.BlockSpec((tt, K), lambda j, k: (k, 0)),
                  pl.BlockSpec((tt, tn), lambda j, k: (k, j))],
        out_specs=pl.BlockSpec((K, tn), lambda j, k: (0, j)),
        out_shape=jax.ShapeDtypeStruct((K, N), F32),
        compiler_params=_cp(),
    )(a, b)


def _inproj_bwd_x(dproj, wp, x, g, dout):
    T = x.shape[0]
    tm, tk = _tile(T, 1024), 1024
    nk = PP // tk

    def body(dp_ref, w_ref, x_ref, g_ref, do_ref, dx_ref, dg_ref, acc_ref):
        i, k = pl.program_id(0), pl.program_id(1)

        @pl.when(k == 0)
        def _():
            acc_ref[...] = jnp.zeros_like(acc_ref)

        @pl.when((k == 0) & (i == 0))
        def _():
            dg_ref[...] = jnp.zeros_like(dg_ref)

        acc_ref[...] += _mm_nt(dp_ref[...], w_ref[...])

        @pl.when(k == nk - 1)
        def _():
            dh = acc_ref[...]
            xv = x_ref[...]
            r = _rstd(xv, D_MODEL)
            gy = dh * g_ref[...]
            dot = jnp.sum(xv * gy, axis=-1, keepdims=True) * (1.0 / D_MODEL)
            dx_ref[...] = do_ref[...] + r * gy - xv * (r * r * r) * dot
            dg_ref[...] += jnp.sum(dh * xv * r, axis=0, keepdims=True)

    return pl.pallas_call(
        body, name="inproj_bwd_x", grid=(T // tm, nk),
        in_specs=[pl.BlockSpec((tm, tk), lambda i, k: (i, k)),
                  pl.BlockSpec((D_MODEL, tk), lambda i, k: (0, k)),
                  pl.BlockSpec((tm, D_MODEL), lambda i, k: (i, 0)),
                  pl.BlockSpec((1, D_MODEL), lambda i, k: (0, 0)),
                  pl.BlockSpec((tm, D_MODEL), lambda i, k: (i, 0))],
        out_specs=[pl.BlockSpec((tm, D_MODEL), lambda i, k: (i, 0)),
                   pl.BlockSpec((1, D_MODEL), lambda i, k: (0, 0))],
        out_shape=[jax.ShapeDtypeStruct((T, D_MODEL), F32), jax.ShapeDtypeStruct((1, D_MODEL), F32)],
        scratch_shapes=[pltpu.VMEM((tm, D_MODEL), F32)],
        compiler_params=_cp(),
    )(dproj, wp, x, g, dout)


A_SEGS = (CB_AB, CB_AC, CB_AX, CB_AZ)


def _mixa_fwd(proj, cw, cb, B, S):
    nc = CONV_WIDTH // LANE

    def body(ab_ref, ac_ref, ax_ref, az_ref, cw_ref, cb_ref, y_ref):
        ab, ac, ax, az = (r[...].astype(F32) for r in (ab_ref, ac_ref, ax_ref, az_ref))
        u = ac * ax
        conv = cb_ref[...] + cw_ref[0:1, :] * _shift_down(u, 2) + cw_ref[1:2, :] * _shift_down(u, 1) + cw_ref[2:3, :] * u
        y_ref[...] = (ab * conv * _silu(az)).astype(BF16)

    return pl.pallas_call(
        body, name="mixa_fwd", grid=(B, nc),
        in_specs=[pl.BlockSpec((S, LANE), lambda b, j, c0=c0: (b, c0 + j)) for c0 in A_SEGS]
                 + [pl.BlockSpec((CONV_K, LANE), lambda b, j: (0, j)),
                    pl.BlockSpec((1, LANE), lambda b, j: (0, j))],
        out_specs=pl.BlockSpec((S, LANE), lambda b, j: (b, j)),
        out_shape=jax.ShapeDtypeStruct((B * S, CONV_WIDTH), BF16),
        compiler_params=_cp(),
    )(proj, proj, proj, proj, cw, cb)


def _mixa_bwd(dproj, dy, proj, cw, cb, B, S):
    nc = CONV_WIDTH // LANE

    def body(dpin_ref, dy_ref, ab_ref, ac_ref, ax_ref, az_ref, cw_ref, cb_ref, dp_ref, st_ref, stage, sems):
        del dpin_ref
        j, b = pl.program_id(0), pl.program_id(1)
        ab, ac, ax, az = (r[...].astype(F32) for r in (ab_ref, ac_ref, ax_ref, az_ref))
        u = ac * ax
        u1, u2 = _shift_down(u, 1), _shift_down(u, 2)
        w0, w1, w2 = cw_ref[0:1, :], cw_ref[1:2, :], cw_ref[2:3, :]
        conv = cb_ref[...] + w0 * u2 + w1 * u1 + w2 * u
        s = _silu(az)
        d = dy_ref[...]
        dconv = d * ab * s
        du = w2 * dconv + w1 * _shift_up(dconv, 1) + w0 * _shift_up(dconv, 2)
        grads = (d * conv * s, du * ax, du * ac, d * ab * conv * _dsilu(az))

        def fill(slot):
            for k, v in enumerate(grads):
                stage[slot, k] = v.astype(BF16)

        def copies_of(step):
            sj, sb = step // B, step % B
            return _put_copies([stage.at[:, k] for k in range(4)], dp_ref, sems, step % 2,
                               pl.ds(pl.multiple_of(sb * S, S), S),
                               [pl.multiple_of((c0 + sj) * LANE, LANE) for c0 in A_SEGS])

        _put_pipeline(j * B + b, nc * B, copies_of, fill)
        row = lax.broadcasted_iota(jnp.int32, (8, LANE), 0)
        st = jnp.zeros((8, LANE), F32)
        for r, v in enumerate((dconv * u2, dconv * u1, dconv * u, dconv)):
            st = st + jnp.where(row == r, jnp.sum(v, axis=0, keepdims=True), 0.0)

        @pl.when(pl.program_id(1) == 0)
        def _():
            st_ref[...] = st

        @pl.when(pl.program_id(1) != 0)
        def _():
            st_ref[...] += st

    return pl.pallas_call(
        body, name="mixa_bwd", grid=(nc, B),
        in_specs=[pl.BlockSpec(memory_space=pl.ANY),
                  pl.BlockSpec((S, LANE), lambda j, b: (b, j))]
                 + [pl.BlockSpec((S, LANE), lambda j, b, c0=c0: (b, c0 + j)) for c0 in A_SEGS]
                 + [pl.BlockSpec((CONV_K, LANE), lambda j, b: (0, j)),
                    pl.BlockSpec((1, LANE), lambda j, b: (0, j))],
        out_specs=[pl.BlockSpec(memory_space=pl.ANY),
                   pl.BlockSpec((8, LANE), lambda j, b: (0, j))],
        out_shape=[jax.ShapeDtypeStruct(dproj.shape, BF16), jax.ShapeDtypeStruct((8, CONV_WIDTH), F32)],
        scratch_shapes=[pltpu.VMEM((2, 4, S, LANE), BF16), pltpu.SemaphoreType.DMA((2, 4))],
        input_output_aliases={0: 0},
        compiler_params=_cp(),
    )(dproj, dy, proj, proj, proj, proj, cw, cb)


def _mla_prep_fwd(proj, gq, gkv, wuqp, wkp, wv, gmq, gmk, cos, sa, sb, S):
    T = proj.shape[0]
    ts = _tile(S, 512)
    ns = S // ts
    W = MLA_HEADS * LANE

    def body(p_ref, gq_ref, gkv_ref, wuq_ref, wk_ref, wv_ref, gmq_ref, gmk_ref, cos_ref, sa_ref, sb_ref,
             q_ref, k_ref, v_ref):
        cq = p_ref[:, 0:2 * LANE].astype(F32)
        ckv = p_ref[:, 2 * LANE:3 * LANE].astype(F32)
        kpe = pltpu.roll(p_ref[:, 3 * LANE:4 * LANE].astype(F32), HALF, axis=1)
        cqn = cq * _rstd(cq, MLA_Q_LORA) * gq_ref[...]
        ckn = (ckv * _rstd(ckv, MLA_KV_LORA) * gkv_ref[...]).astype(BF16)
        q0 = _mm(cqn, wuq_ref[...])
        kn = _mm(ckn, wk_ref[...])
        v_ref[...] = _mm(ckn, wv_ref[...]).astype(BF16)
        c, a, b = cos_ref[...], sa_ref[...], sb_ref[...]
        kpe_rot = _rope(kpe * gmk_ref[...], c, a, b)
        for h in range(MLA_HEADS):
            q0h = q0[:, h * LANE:(h + 1) * LANE]
            q_ref[h] = _rope(q0h * _rstd(q0h, MLA_QK) * gmq_ref[...], c, a, b).astype(BF16)
            knh = kn[:, h * LANE:(h + 1) * LANE]
            k_ref[h] = (_rstd(knh + kpe, MLA_QK) * (knh * gmk_ref[...] + kpe_rot)).astype(BF16)

    def whole(r, c):
        return pl.BlockSpec((r, c), lambda i: (0, 0))

    tab = pl.BlockSpec((ts, LANE), lambda i: (i % ns, 0))
    return pl.pallas_call(
        body, name="mla_prep_fwd", grid=(T // ts,),
        in_specs=[pl.BlockSpec((ts, 4 * LANE), lambda i: (i, CB_CQ // 4)),
                  whole(1, MLA_Q_LORA), whole(1, MLA_KV_LORA), whole(MLA_Q_LORA, W), whole(MLA_KV_LORA, W),
                  whole(MLA_KV_LORA, MLA_HEADS * MLA_V), whole(1, LANE), whole(1, LANE), tab, tab, tab],
        out_specs=[pl.BlockSpec((MLA_HEADS, ts, LANE), lambda i: (0, i, 0)),
                   pl.BlockSpec((MLA_HEADS, ts, LANE), lambda i: (0, i, 0)),
                   pl.BlockSpec((ts, MLA_HEADS * MLA_V), lambda i: (i, 0))],
        out_shape=[jax.ShapeDtypeStruct((MLA_HEADS, T, LANE), BF16), jax.ShapeDtypeStruct((MLA_HEADS, T, LANE), BF16),
                   jax.ShapeDtypeStruct((T, MLA_HEADS * MLA_V), BF16)],
        compiler_params=_cp(),
    )(proj, gq, gkv, wuqp, wkp, wv, gmq, gmk, cos, sa, sb)


def _mla_prep_bwd(dproj, dq, dk, dv, proj, gq, gkv, wuqp, wkp, wv, gmq, gmk, cos, sa, sb, S):
    T = proj.shape[0]
    ts = _tile(S, 256)
    ns = S // ts
    W = MLA_HEADS * LANE

    def body(dpin_ref, dq_ref, dk_ref, dv_ref, p_ref, gq_ref, gkv_ref, wuq_ref, wk_ref, wv_ref, gmq_ref, gmk_ref,
             cos_ref, sa_ref, sb_ref,
             dp_ref, dwuq_ref, dwk_ref, dwv_ref, dgq_ref, dgkv_ref, dgmq_ref, dgmk_ref, dq0_ref, dkn_ref):
        del dpin_ref

        @pl.when(pl.program_id(0) == 0)
        def _():
            for r in (dwuq_ref, dwk_ref, dwv_ref, dgq_ref, dgkv_ref, dgmq_ref, dgmk_ref):
                r[...] = jnp.zeros_like(r)

        cq = p_ref[:, 0:2 * LANE].astype(F32)
        ckv = p_ref[:, 2 * LANE:3 * LANE].astype(F32)
        kpe = pltpu.roll(p_ref[:, 3 * LANE:4 * LANE].astype(F32), HALF, axis=1)
        rq = _rstd(cq, MLA_Q_LORA)
        rkv = _rstd(ckv, MLA_KV_LORA)
        gq, gkv, gmq, gmk = gq_ref[...], gkv_ref[...], gmq_ref[...], gmk_ref[...]
        cqn = (cq * rq * gq).astype(BF16)
        ckn = (ckv * rkv * gkv).astype(BF16)
        q0 = _mm(cqn, wuq_ref[...])
        kn = _mm(ckn, wk_ref[...])
        c, a, b = cos_ref[...], sa_ref[...], sb_ref[...]
        lane = lax.broadcasted_iota(jnp.int32, (ts, LANE), 1)
        dgmq = jnp.zeros((1, LANE), F32)
        dgmk = jnp.zeros((1, LANE), F32)
        dkpe = jnp.zeros((ts, LANE), F32)
        for h in range(MLA_HEADS):
            q0h = q0[:, h * LANE:(h + 1) * LANE]
            r = _rstd(q0h, MLA_QK)
            d1 = _rope_t(dq_ref[h], c, a, b)
            gy = d1 * gmq
            dq0_ref[:, h * LANE:(h + 1) * LANE] = (
                r * gy - q0h * (r * r * r) * (jnp.sum(q0h * gy, axis=-1, keepdims=True) * (1.0 / MLA_QK))).astype(BF16)
            dgmq = dgmq + jnp.sum(d1 * q0h * r, axis=0, keepdims=True)
            k0h = kn[:, h * LANE:(h + 1) * LANE] + kpe
            r = _rstd(k0h, MLA_QK)
            d1 = _rope_t(dk_ref[h], c, a, b)
            gy = d1 * gmk
            dk0 = r * gy - k0h * (r * r * r) * (jnp.sum(k0h * gy, axis=-1, keepdims=True) * (1.0 / MLA_QK))
            dgmk = dgmk + jnp.sum(d1 * k0h * r, axis=0, keepdims=True)
            dkn_ref[:, h * LANE:(h + 1) * LANE] = jnp.where(lane < MLA_NOPE, dk0, 0.0).astype(BF16)
            dkpe = dkpe + jnp.where((lane >= MLA_NOPE) & (lane < MLA_QK), dk0, 0.0)
        dq0 = dq0_ref[...]
        dkn = dkn_ref[...]
        dvv = dv_ref[...]
        dwuq_ref[...] += _mm_tn(cqn, dq0)
        dwk_ref[...] += _mm_tn(ckn, dkn)
        dwv_ref[...] += _mm_tn(ckn, dvv)
        dgmq_ref[...] += dgmq
        dgmk_ref[...] += dgmk
        dcqn = _mm_nt(dq0, wuq_ref[...])
        gy = dcqn * gq
        dp_ref[:, 0:2 * LANE] = (
            rq * gy - cq * (rq * rq * rq) * (jnp.sum(cq * gy, axis=-1, keepdims=True) * (1.0 / MLA_Q_LORA))).astype(BF16)
        dgq_ref[...] += jnp.sum(dcqn * cq * rq, axis=0, keepdims=True)
        dckn = _mm_nt(dkn, wk_ref[...]) + _mm_nt(dvv, wv_ref[...])
        gy = dckn * gkv
        dp_ref[:, 2 * LANE:3 * LANE] = (
            rkv * gy - ckv * (rkv * rkv * rkv) * (jnp.sum(ckv * gy, axis=-1, keepdims=True) * (1.0 / MLA_KV_LORA))).astype(BF16)
        dgkv_ref[...] += jnp.sum(dckn * ckv * rkv, axis=0, keepdims=True)
        dp_ref[:, 3 * LANE:4 * LANE] = pltpu.roll(dkpe, HALF, axis=1).astype(BF16)

    def whole(r, c):
        return pl.BlockSpec((r, c), lambda i: (0, 0))

    tab = pl.BlockSpec((ts, LANE), lambda i: (i % ns, 0))
    heads = pl.BlockSpec((MLA_HEADS, ts, LANE), lambda i: (0, i, 0))
    return pl.pallas_call(
        body, name="mla_prep_bwd", grid=(T // ts,),
        in_specs=[pl.BlockSpec(memory_space=pl.ANY), heads, heads,
                  pl.BlockSpec((ts, MLA_HEADS * MLA_V), lambda i: (i, 0)),
                  pl.BlockSpec((ts, 4 * LANE), lambda i: (i, CB_CQ // 4)),
                  whole(1, MLA_Q_LORA), whole(1, MLA_KV_LORA), whole(MLA_Q_LORA, W), whole(MLA_KV_LORA, W),
                  whole(MLA_KV_LORA, MLA_HEADS * MLA_V), whole(1, LANE), whole(1, LANE), tab, tab, tab],
        out_specs=[pl.BlockSpec((ts, 4 * LANE), lambda i: (i, CB_CQ // 4)),
                   whole(MLA_Q_LORA, W), whole(MLA_KV_LORA, W), whole(MLA_KV_LORA, MLA_HEADS * MLA_V),
                   whole(1, MLA_Q_LORA), whole(1, MLA_KV_LORA), whole(1, LANE), whole(1, LANE)],
        out_shape=[jax.ShapeDtypeStruct(dproj.shape, BF16),
                   jax.ShapeDtypeStruct((MLA_Q_LORA, W), F32), jax.ShapeDtypeStruct((MLA_KV_LORA, W), F32),
                   jax.ShapeDtypeStruct((MLA_KV_LORA, MLA_HEADS * MLA_V), F32),
                   jax.ShapeDtypeStruct((1, MLA_Q_LORA), F32), jax.ShapeDtypeStruct((1, MLA_KV_LORA), F32),
                   jax.ShapeDtypeStruct((1, LANE), F32), jax.ShapeDtypeStruct((1, LANE), F32)],
        scratch_shapes=[pltpu.VMEM((ts, W), BF16), pltpu.VMEM((ts, W), BF16)],
        input_output_aliases={0: 0},
        compiler_params=_cp(),
    )(dproj, dq, dk, dv, proj, gq, gkv, wuqp, wkp, wv, gmq, gmk, cos, sa, sb)


def _dil_prep_fwd(proj, gq, gk):
    T = proj.shape[0]
    ts = _tile(T, 512)

    def body(pq_ref, pk_ref, pv_ref, gq_ref, gk_ref, q_ref, k_ref, v_ref):
        v_ref[...] = pv_ref[...].astype(F32)
        for c in range(NPAIR):
            cs = slice(c * LANE, (c + 1) * LANE)
            t = jnp.concatenate([pq_ref[:, cs], pk_ref[:, cs]], axis=1).astype(F32)
            y = t * lax.rsqrt(_head_bcast_sum(t * t, terms=2) * (1.0 / DIL_HEAD_DIM) + EPS)
            q_ref[:, cs] = y[:, 0:LANE] * gq_ref[:, cs]
            k_ref[:, cs] = y[:, LANE:2 * LANE] * gk_ref[:, cs]

    col = pl.BlockSpec((1, DIL_WIDTH), lambda i, g: (0, g))
    out = pl.BlockSpec((ts, DIL_WIDTH), lambda i, g: (i, g))
    seg = lambda c0: pl.BlockSpec((ts, DIL_WIDTH), lambda i, g: (i, c0 // NPAIR + g))
    return pl.pallas_call(
        body, name="dil_prep_fwd", grid=(T // ts, DIL_GROUPS),
        in_specs=[seg(CB_DQ), seg(CB_DK), seg(CB_DV), col, col],
        out_specs=[out, out, out],
        out_shape=[jax.ShapeDtypeStruct((T, DIL_QK), F32)] * 3,
        compiler_params=_cp(),
    )(proj, proj, proj, gq, gk)


def _dil_prep_bwd(dproj, ddq, ddk, ddv, proj, gq, gk):
    T = proj.shape[0]
    ts = _tile(T, 512)
    nt = T // ts

    def body(dpin_ref, ddq_ref, ddk_ref, ddv_ref, pq_ref, pk_ref, gq_ref, gk_ref, dp_ref, dgq_ref, dgk_ref,
             stage, sems):
        del dpin_ref
        g, i = pl.program_id(0), pl.program_id(1)

        @pl.when(i == 0)
        def _():
            dgq_ref[...] = jnp.zeros_like(dgq_ref)
            dgk_ref[...] = jnp.zeros_like(dgk_ref)

        def fill(slot):
            stage[slot, 2] = ddv_ref[...].astype(BF16)
            for c in range(NPAIR):
                cs = slice(c * LANE, (c + 1) * LANE)
                t = jnp.concatenate([pq_ref[:, cs], pk_ref[:, cs]], axis=1).astype(F32)
                d = jnp.concatenate([ddq_ref[:, cs], ddk_ref[:, cs]], axis=1)
                gy = d * jnp.concatenate([gq_ref[:, cs], gk_ref[:, cs]], axis=1)
                r = lax.rsqrt(_head_bcast_sum(t * t, terms=2) * (1.0 / DIL_HEAD_DIM) + EPS)
                dot = _head_bcast_sum(t * gy, terms=2) * (1.0 / DIL_HEAD_DIM)
                dx = (r * gy - t * (r * r * r) * dot).astype(BF16)
                stage[slot, 0, :, cs] = dx[:, 0:LANE]
                stage[slot, 1, :, cs] = dx[:, LANE:2 * LANE]
                part = jnp.sum(d * t * r, axis=0, keepdims=True)
                dgq_ref[:, cs] += part[:, 0:LANE]
                dgk_ref[:, cs] += part[:, LANE:2 * LANE]

        def copies_of(step):
            sg, si = step // nt, step % nt
            return _put_copies([stage.at[:, k] for k in range(3)], dp_ref, sems, step % 2,
                               pl.ds(pl.multiple_of(si * ts, ts), ts),
                               [pl.multiple_of((c0 + NPAIR * sg) * LANE, LANE) for c0 in (CB_DQ, CB_DK, CB_DV)])

        _put_pipeline(g * nt + i, DIL_GROUPS * nt, copies_of, fill)

    col = pl.BlockSpec((1, DIL_WIDTH), lambda g, i: (0, g))
    tok = pl.BlockSpec((ts, DIL_WIDTH), lambda g, i: (i, g))
    seg = lambda c0: pl.BlockSpec((ts, DIL_WIDTH), lambda g, i: (i, c0 // NPAIR + g))
    return pl.pallas_call(
        body, name="dil_prep_bwd", grid=(DIL_GROUPS, nt),
        in_specs=[pl.BlockSpec(memory_space=pl.ANY), tok, tok, tok, seg(CB_DQ), seg(CB_DK), col, col],
        out_specs=[pl.BlockSpec(memory_space=pl.ANY), col, col],
        out_shape=[jax.ShapeDtypeStruct(dproj.shape, BF16), jax.ShapeDtypeStruct((1, DIL_QK), F32),
                   jax.ShapeDtypeStruct((1, DIL_QK), F32)],
        scratch_shapes=[pltpu.VMEM((2, 3, ts, DIL_WIDTH), BF16), pltpu.SemaphoreType.DMA((2, 3))],
        input_output_aliases={0: 0},
        compiler_params=_cp(),
    )(dproj, ddq, ddk, ddv, proj, proj, gq, gk)


COPY_ROWS = 256


def _to_classes(src_ref, dst_ref, d, L, scale=None):
    n = min(L, COPY_ROWS)
    for r in range(d):
        for c0 in range(0, L, n):
            rows = pl.ds(r + c0 * d, n, stride=d) if d > 1 else pl.ds(c0, n)
            val = src_ref[rows, :]
            if scale is not None:
                val = val * scale
            dst_ref[r * L + c0:r * L + c0 + n, :] = val.astype(dst_ref.dtype)


def _from_classes(src_ref, dst_ref, d, L):
    n = min(L, COPY_ROWS)
    for r in range(d):
        for c0 in range(0, L, n):
            rows = pl.ds(r + c0 * d, n, stride=d) if d > 1 else pl.ds(c0, n)
            dst_ref[rows, :] = src_ref[r * L + c0:r * L + c0 + n, :].astype(dst_ref.dtype)


MLA_TQ, MLA_TK = 512, 512


def _causal_bias(tq, tk, shift):
    row = lax.broadcasted_iota(jnp.int32, (tq, tk), 0)
    col = lax.broadcasted_iota(jnp.int32, (tq, tk), 1)
    return jnp.where(row >= col + shift, 0.0, NEG)


def _mla_specs(S):
    heads = pl.BlockSpec((2, S, LANE), lambda b, j: (j, b, 0))
    pair = pl.BlockSpec((S, LANE), lambda b, j: (b, j))
    return heads, pair


def _mla_attn_fwd(q, k, v, B, S):
    tq = _tile(S, MLA_TQ)
    tk = _tile(tq, MLA_TK)
    nd = tq // tk
    scale = MLA_QK ** -0.5
    heads, pair = _mla_specs(S)

    def body(q_ref, k_ref, v_ref, o_ref, lse_ref):
        lo, lok = _lane_lo((tq, LANE)), _lane_lo((tk, LANE))
        diag = [_causal_bias(tq, tk, i * tk) for i in range(nd)]

        def block(g, _):
            row0 = pl.multiple_of(g * tq, tq)
            rows = pl.ds(row0, tq)
            qs = [q_ref[hh, rows, :] for hh in range(2)]

            one = jnp.ones((), BF16)

            def step(off, carries, bias):
                off = pl.multiple_of(off, tk)
                vt = v_ref[pl.ds(off, tk), :]
                vh = (jnp.where(lok, vt, one), jnp.where(lok, one, vt))
                out = []
                for hh, (m, acc) in enumerate(carries):
                    s = _mm_nt(qs[hh], k_ref[hh, pl.ds(off, tk), :]) * scale
                    if bias is not None:
                        s = s + bias
                    m_new = jnp.maximum(m, jnp.max(s, axis=-1, keepdims=True))
                    p = jnp.exp(s - m_new)
                    out.append((m_new, jnp.exp(m - m_new) * acc + _mm(p, vh[hh])))
                return tuple(out)

            init = (jnp.full((tq, 1), NEG, F32), jnp.zeros((tq, LANE), F32))
            carries = lax.fori_loop(0, g * nd, lambda i, c: step(i * tk, c, None), (init, init))
            for i in range(nd):
                carries = step(row0 + i * tk, carries, diag[i])
            (ma, acca), (mb, accb) = carries
            la, lb = pltpu.roll(acca, HALF, axis=1), pltpu.roll(accb, HALF, axis=1)
            o_ref[rows, :] = jnp.where(lo, acca / la, accb / lb)
            lse_ref[rows, :] = jnp.where(lo, ma + jnp.log(la), mb + jnp.log(lb))
            return 0

        lax.fori_loop(0, S // tq, block, 0)

    return pl.pallas_call(
        body, name="mla_attn_fwd", grid=(B, NPAIR), in_specs=[heads, heads, pair], out_specs=[pair, pair],
        out_shape=[jax.ShapeDtypeStruct((B * S, MLA_HEADS * MLA_V), F32)] * 2,
        compiler_params=_cp(),
    )(q, k, v)


DIL_UNROLL = 16


def _dil_geometry(gi, S):
    span, d = DIL_PATTERNS[gi]
    L = S // d
    t = _tile(L, 128)
    window = span // d
    back = min(-(-window // t) * t, L - t)
    return d, L, t, window, back


def _dil_specs(gi, S):
    qk = pl.BlockSpec((S, LANE), lambda b, j: (b, NPAIR * gi + j))
    pair = pl.BlockSpec((S, LANE), lambda b, j: (b, j))
    return qk, qk, pair


def _dil_bias(bias_ref, sl_ref, j, t, kw, back, window):
    row = lax.broadcasted_iota(jnp.int32, (2 * t, kw), 0)
    col = lax.broadcasted_iota(jnp.int32, (2 * t, kw), 1)
    second = row >= t
    slope = jnp.where(second, sl_ref[j, 1], sl_ref[j, 0])
    for n in range(bias_ref.shape[0]):
        dist = jnp.where(second, row - t, row) + n * back - col
        bias_ref[n] = jnp.where((dist >= 0) & (dist <= window), -slope * dist.astype(F32), NEG)


def _stack_heads(x, lo):
    zero = jnp.zeros((), x.dtype)
    return jnp.concatenate([jnp.where(lo, x, zero), jnp.where(lo, zero, x)], axis=0)


def _dil_attn_fwd(gi, slopes, qn, kn, proj, B, S):
    d, L, t, window, back = _dil_geometry(gi, S)
    kw, nq = back + t, L // t
    nbias = 2 if back else 1
    qk, vspec, pair = _dil_specs(gi, S)

    def body(sl_ref, q_ref, k_ref, v_ref, o_ref, lse_ref, qs, ks, vs, os_, ls, bias_ref):
        _to_classes(q_ref, qs, d, L, DIL_HEAD_DIM ** -0.5)
        _to_classes(k_ref, ks, d, L)
        _to_classes(v_ref, vs, d, L)
        _dil_bias(bias_ref, sl_ref, pl.program_id(1), t, kw, back, window)
        lo = _lane_lo((t, LANE))

        def block(g, _):
            qb = g % nq if d > 1 else g
            row0 = pl.multiple_of(g * t, t)
            rows = pl.ds(row0, t)
            early = qb * t < back
            keys = pl.ds(pl.multiple_of(jnp.where(early, row0 - qb * t, row0 - back), t), kw)
            s = _mm_nt(_stack_heads(qs[rows, :], lo), ks[keys, :]) + bias_ref[jnp.where(early, 0, nbias - 1)]
            m = jnp.max(s, axis=-1, keepdims=True)
            p = jnp.exp(s - m)
            l = jnp.sum(p, axis=-1, keepdims=True)
            o2 = _mm(p, vs[keys, :]) / l
            lse2 = m + jnp.log(l)
            os_[rows, :] = jnp.where(lo, o2[:t], o2[t:])
            ls[rows, :] = jnp.where(lo, lse2[:t], lse2[t:])
            return 0

        lax.fori_loop(0, d * nq, block, 0, unroll=DIL_UNROLL if d * nq % DIL_UNROLL == 0 else 1)
        _from_classes(os_, o_ref, d, L)
        _from_classes(ls, lse_ref, d, L)

    return pl.pallas_call(
        body, name=f"dil_attn_fwd_{gi}", grid=(B, NPAIR),
        in_specs=[pl.BlockSpec(memory_space=pltpu.SMEM), qk, qk, vspec], out_specs=[pair, pair],
        out_shape=[jax.ShapeDtypeStruct((B * S, DIL_WIDTH), F32)] * 2,
        scratch_shapes=[pltpu.VMEM((S, LANE), BF16)] * 3 + [pltpu.VMEM((S, LANE), F32)] * 2
                       + [pltpu.VMEM((nbias, 2 * t, kw), F32)],
        compiler_params=_cp(),
    )(slopes, qn, kn, proj)


def _mla_attn_bwd(q, k, v, do, lse, delta, B, S):
    T = B * S
    tq = _tile(S, MLA_TQ)
    tk = _tile(tq, MLA_TK)
    nd = tq // tk
    scale = MLA_QK ** -0.5
    heads, pair = _mla_specs(S)

    def body(q_ref, k_ref, v_ref, do_ref, lse_ref, dl_ref, dq_ref, dk_ref, dv_ref):
        dk_ref[...] = jnp.zeros_like(dk_ref)
        dv_ref[...] = jnp.zeros_like(dv_ref)
        lo = _lane_lo((tq, LANE))
        diag = [_causal_bias(tq, tk, i * tk) for i in range(nd)]

        def block(g, _):
            row0 = pl.multiple_of(g * tq, tq)
            rows = pl.ds(row0, tq)
            per_head = []
            for hh in range(2):
                sel = lo if hh == 0 else jnp.logical_not(lo)
                per_head.append((q_ref[hh, rows, :], jnp.where(sel, do_ref[rows, :], jnp.zeros((), BF16)),
                                 jnp.max(jnp.where(sel, lse_ref[rows, :], NEG), axis=-1, keepdims=True),
                                 jnp.max(jnp.where(sel, dl_ref[rows, :], NEG), axis=-1, keepdims=True)))

            def step(off, dq_accs, bias):
                cols = pl.ds(pl.multiple_of(off, tk), tk)
                vt = v_ref[cols, :]
                out, dv = [], None
                for hh, (qh, doh, lse_h, dl_h) in enumerate(per_head):
                    kh = k_ref[hh, cols, :]
                    s = _mm_nt(qh, kh) * scale
                    if bias is not None:
                        s = s + bias
                    p = jnp.exp(s - lse_h)
                    ds = (p * (_mm_nt(doh, vt) - dl_h)).astype(BF16)
                    dk_ref[hh, cols, :] += _mm_tn(ds, qh) * scale
                    part = _mm_tn(p, doh)
                    dv = part if dv is None else dv + part
                    out.append(dq_accs[hh] + _mm(ds, kh))
                dv_ref[cols, :] += dv
                return tuple(out)

            zero = jnp.zeros((tq, LANE), F32)
            dq_accs = lax.fori_loop(0, g * nd, lambda i, a: step(i * tk, a, None), (zero, zero))
            for i in range(nd):
                dq_accs = step(row0 + i * tk, dq_accs, diag[i])
            for hh in range(2):
                dq_ref[hh, rows, :] = dq_accs[hh] * scale
            return 0

        lax.fori_loop(0, S // tq, block, 0)

    return pl.pallas_call(
        body, name="mla_attn_bwd", grid=(B, NPAIR), in_specs=[heads, heads, pair, pair, pair, pair],
        out_specs=[heads, heads, pair],
        out_shape=[jax.ShapeDtypeStruct((MLA_HEADS, T, LANE), F32), jax.ShapeDtypeStruct((MLA_HEADS, T, LANE), F32),
                   jax.ShapeDtypeStruct((T, MLA_HEADS * MLA_V), F32)],
        compiler_params=_cp(),
    )(q, k, v, do, lse, delta)


def _dil_attn_bwd(gi, slopes, qn, kn, proj, do, lse, delta, through, B, S):
    d, L, t, window, back = _dil_geometry(gi, S)
    kw, nq = back + t, L // t
    nbias = 2 if back else 1
    scale = DIL_HEAD_DIM ** -0.5
    qk, vspec, pair = _dil_specs(gi, S)

    def body(*refs):
        refs = list(refs)
        sl_ref, q_ref, k_ref, v_ref, do_ref, lse_ref, dl_ref = refs[:7]
        dq_ref, dk_ref, dv_ref, qs, ks, vs, dos, lss, dls, dqs, dks, dvs, bias_ref = refs[-13:]
        _to_classes(q_ref, qs, d, L, scale)
        for src, dst in ((k_ref, ks), (v_ref, vs), (do_ref, dos), (lse_ref, lss), (dl_ref, dls)):
            _to_classes(src, dst, d, L)
        _dil_bias(bias_ref, sl_ref, pl.program_id(1), t, kw, back, window)
        dks[...] = jnp.zeros_like(dks)
        dvs[...] = jnp.zeros_like(dvs)
        lo = _lane_lo((t, LANE))

        def stats(ref, rows):
            x = ref[rows, :]
            return jnp.concatenate([jnp.max(jnp.where(lo, x, NEG), axis=-1, keepdims=True),
                                    jnp.max(jnp.where(lo, NEG, x), axis=-1, keepdims=True)], axis=0)

        def block(g, _):
            qb = g % nq if d > 1 else g
            row0 = pl.multiple_of(g * t, t)
            rows = pl.ds(row0, t)
            early = qb * t < back
            keys = pl.ds(pl.multiple_of(jnp.where(early, row0 - qb * t, row0 - back), t), kw)
            q2 = _stack_heads(qs[rows, :], lo)
            do2 = _stack_heads(dos[rows, :], lo)
            kt = ks[keys, :]
            s = _mm_nt(q2, kt) + bias_ref[jnp.where(early, 0, nbias - 1)]
            p = jnp.exp(s - stats(lss, rows))
            ds = (p * (_mm_nt(do2, vs[keys, :]) - stats(dls, rows))).astype(BF16)
            dq2 = _mm(ds, kt) * scale
            dqs[rows, :] = jnp.where(lo, dq2[:t], dq2[t:])
            dks[keys, :] += _mm_tn(ds, q2)
            dvs[keys, :] += _mm_tn(p, do2)
            return 0

        lax.fori_loop(0, d * nq, block, 0, unroll=DIL_UNROLL if d * nq % DIL_UNROLL == 0 else 1)
        for src, dst in ((dqs, dq_ref), (dks, dk_ref), (dvs, dv_ref)):
            _from_classes(src, dst, d, L)

    in_specs = [pl.BlockSpec(memory_space=pltpu.SMEM), qk, qk, vspec, pair, pair, pair]
    args = [slopes, qn, kn, proj, do, lse, delta]
    aliases = {}
    if through is not None:
        aliases = {len(args) + i: i for i in range(3)}
        in_specs = in_specs + [pl.BlockSpec(memory_space=pl.ANY)] * 3
        args = args + list(through)
    return pl.pallas_call(
        body, name=f"dil_attn_bwd_{gi}", grid=(B, NPAIR), in_specs=in_specs, out_specs=[qk, qk, qk],
        out_shape=[jax.ShapeDtypeStruct((B * S, DIL_QK), F32)] * 3,
        scratch_shapes=[pltpu.VMEM((S, LANE), BF16)] * 4 + [pltpu.VMEM((S, LANE), F32)] * 5
                       + [pltpu.VMEM((nbias, 2 * t, kw), F32)],
        input_output_aliases=aliases,
        compiler_params=_cp(),
    )(*args)


def _merge_proj_specs(ts):
    wide = lambda c0, w: pl.BlockSpec((ts, w), lambda i: (i, c0 * LANE // w))
    return [wide(CB_BZ, DIL_WIDTH), wide(CB_CZ, DIL_WIDTH)] + [wide(CB_GATE + 8 * i, D_MODEL) for i in range(3)]


def _merge_common(p_refs, bg_ref, ob_ref, og_refs, lse_refs):
    bz = p_refs[0][...].astype(F32)
    cz = p_refs[1][...].astype(F32)
    gates = [_sigmoid(p_refs[2 + i][...].astype(F32) + bg_ref[:, i * D_MODEL:(i + 1) * D_MODEL]) for i in range(3)]
    ob = ob_ref[...]
    lses = [r[...] for r in lse_refs]
    mx = jnp.maximum(jnp.maximum(lses[0], lses[1]), lses[2])
    es = [jnp.exp(v - mx) for v in lses]
    inv = 1.0 / (es[0] + es[1] + es[2])
    alphas = [e * inv for e in es]
    oc = alphas[0] * og_refs[0][...] + alphas[1] * og_refs[1][...] + alphas[2] * og_refs[2][...]
    return bz, cz, gates, ob, alphas, oc


def _merge_fwd(x, proj, b_gate, ya, ob, ogs, lses, woa, wob, woc, wo):
    T = x.shape[0]
    ts = _tile(T, 256)

    def body(x_ref, p0, p1, p2, p3, p4, bg_ref, ya_ref, ob_ref, og0, og1, og2, l0, l1, l2,
             woa_ref, wob_ref, woc_ref, wo_ref, out_ref):
        bz, cz, gates, obv, alphas, oc = _merge_common((p0, p1, p2, p3, p4), bg_ref, ob_ref, (og0, og1, og2),
                                                       (l0, l1, l2))
        yb = obv * _silu(bz)
        yc = oc * _silu(cz)
        merged = (gates[0] * _mm(ya_ref[...], woa_ref[...]) + gates[1] * _mm(yb, wob_ref[...])
                  + gates[2] * _mm(yc, woc_ref[...]))
        out_ref[...] = x_ref[...] + _mm(merged, wo_ref[...])

    def whole(r, c):
        return pl.BlockSpec((r, c), lambda i: (0, 0))

    tok = lambda w: pl.BlockSpec((ts, w), lambda i: (i, 0))
    return pl.pallas_call(
        body, name="merge_fwd", grid=(T // ts,),
        in_specs=[tok(D_MODEL)] + _merge_proj_specs(ts) + [whole(1, 3 * D_MODEL), tok(CONV_WIDTH)]
                 + [tok(DIL_WIDTH)] * 7 + [whole(CONV_WIDTH, D_MODEL)] * 3 + [whole(D_MODEL, D_MODEL)],
        out_specs=tok(D_MODEL),
        out_shape=jax.ShapeDtypeStruct((T, D_MODEL), F32),
        compiler_params=_cp(),
    )(x, *[proj] * 5, b_gate, ya, ob, *ogs, *lses, woa, wob, woc, wo)


def _merge_bwd(dout, proj, b_gate, ya, ob, ogs, lses, woa, wob, woc, wo):
    T = dout.shape[0]
    ts = _tile(T, 256)
    nt = T // ts

    def body(do_ref, p0, p1, p2, p3, p4, bg_ref, ya_ref, ob_ref, og0, og1, og2, l0, l1, l2,
             woa_ref, wob_ref, woc_ref, wo_ref,
             dp_ref, dya_ref, dob_ref, dlb_ref, dg0, dg1, dg2, dl0, dl1, dl2,
             mg_ref, dpa_ref, dpb_ref, dpc_ref, yb_ref, yc_ref, dbg_ref, st_bz, st_cz, st_gate, sems):
        step = pl.program_id(0)
        slot = step % 2

        def copies_of(s):
            return _put_copies([st_bz, st_cz, st_gate], dp_ref, sems, s % 2, pl.ds(pl.multiple_of(s * ts, ts), ts),
                               [CB_BZ * LANE, CB_CZ * LANE, CB_GATE * LANE])

        @pl.when(step >= 2)
        def _():
            for cp in copies_of(step - 2):
                cp.wait()

        bz, cz, gates, obv, alphas, oc = _merge_common((p0, p1, p2, p3, p4), bg_ref, ob_ref, (og0, og1, og2),
                                                       (l0, l1, l2))
        sb, sc = _silu(bz), _silu(cz)
        yb = obv * sb
        yc = oc * sc
        ps = [_mm(ya_ref[...], woa_ref[...]), _mm(yb, wob_ref[...]), _mm(yc, woc_ref[...])]
        mg_ref[...] = (gates[0] * ps[0] + gates[1] * ps[1] + gates[2] * ps[2]).astype(BF16)
        yb_ref[...] = yb.astype(BF16)
        yc_ref[...] = yc.astype(BF16)
        dm = _mm_nt(do_ref[...], wo_ref[...])
        dps = []
        first = pl.program_id(0) == 0
        for i, dref in enumerate((dpa_ref, dpb_ref, dpc_ref)):
            g = gates[i]
            dpi = (dm * g).astype(BF16)
            dref[...] = dpi
            dps.append(dpi)
            dgp = dm * ps[i] * g * (1.0 - g)
            st_gate[slot, :, i * D_MODEL:(i + 1) * D_MODEL] = dgp.astype(BF16)
            part = jnp.sum(dgp, axis=0, keepdims=True)

            @pl.when(first)
            def _():
                dbg_ref[:, i * D_MODEL:(i + 1) * D_MODEL] = part

            @pl.when(jnp.logical_not(first))
            def _():
                dbg_ref[:, i * D_MODEL:(i + 1) * D_MODEL] += part

        dya_ref[...] = _mm_nt(dps[0], woa_ref[...])
        dyb = _mm_nt(dps[1], wob_ref[...])
        dyc = _mm_nt(dps[2], woc_ref[...])
        st_bz[slot] = (dyb * obv * _dsilu(bz)).astype(BF16)
        st_cz[slot] = (dyc * oc * _dsilu(cz)).astype(BF16)
        for cp in copies_of(step):
            cp.start()
        dob = dyb * sb
        doc = dyc * sc
        dob_ref[...] = dob.astype(BF16)
        for c in range(NPAIR):
            cs = slice(c * LANE, (c + 1) * LANE)
            dlb_ref[:, cs] = _head_bcast_sum(dob[:, cs] * obv[:, cs])
            dd = _head_bcast_sum(doc[:, cs] * oc[:, cs])
            for a, dref, lref in zip(alphas, (dg0, dg1, dg2), (dl0, dl1, dl2)):
                dref[:, cs] = a[:, cs] * doc[:, cs]
                lref[:, cs] = a[:, cs] * dd

        @pl.when(step == nt - 1)
        def _():
            if nt >= 2:
                for cp in copies_of(step - 1):
                    cp.wait()
            for cp in copies_of(step):
                cp.wait()

    def whole(r, c):
        return pl.BlockSpec((r, c), lambda i: (0, 0))

    tok = lambda w: pl.BlockSpec((ts, w), lambda i: (i, 0))
    sd = jax.ShapeDtypeStruct
    W = DIL_WIDTH
    return pl.pallas_call(
        body, name="merge_bwd", grid=(nt,),
        in_specs=[tok(D_MODEL)] + _merge_proj_specs(ts) + [whole(1, 3 * D_MODEL), tok(CONV_WIDTH)] + [tok(W)] * 7
                 + [whole(CONV_WIDTH, D_MODEL)] * 3 + [whole(D_MODEL, D_MODEL)],
        out_specs=[pl.BlockSpec(memory_space=pl.ANY), tok(CONV_WIDTH), tok(W), tok(W)] + [tok(W)] * 6
                  + [tok(D_MODEL)] * 4 + [tok(W), tok(W), whole(1, 3 * D_MODEL)],
        out_shape=[sd((T, PP), BF16), sd((T, CONV_WIDTH), F32), sd((T, W), BF16), sd((T, W), F32)]
                  + [sd((T, W), F32)] * 6
                  + [sd((T, D_MODEL), BF16)] * 4 + [sd((T, W), BF16)] * 2 + [sd((1, 3 * D_MODEL), F32)],
        scratch_shapes=[pltpu.VMEM((2, ts, W), BF16), pltpu.VMEM((2, ts, W), BF16),
                        pltpu.VMEM((2, ts, 3 * D_MODEL), BF16), pltpu.SemaphoreType.DMA((2, 3))],
        compiler_params=_cp(),
    )(dout, *[proj] * 5, b_gate, ya, ob, *ogs, *lses, woa, wob, woc, wo)


def _loss_head(y, target):
    T = y.shape[0]
    ts = _tile(T, 512)

    def body(y_ref, t_ref, d_ref, l_ref):
        e = y_ref[...] - t_ref[...]
        d_ref[...] = e * (1.0 / D_MODEL)
        l_ref[...] = jnp.zeros((1, 8, LANE), F32) + jnp.sum(e * e)

    tok = pl.BlockSpec((ts, D_MODEL), lambda i: (i, 0))
    return pl.pallas_call(
        body, name="loss_head", grid=(T // ts,), in_specs=[tok, tok],
        out_specs=[tok, pl.BlockSpec((1, 8, LANE), lambda i: (i, 0, 0))],
        out_shape=[jax.ShapeDtypeStruct((T, D_MODEL), F32), jax.ShapeDtypeStruct((T // ts, 8, LANE), F32)],
        compiler_params=_cp(),
    )(y, target)


def _my_index():
    return 4 * lax.axis_index("x") + 2 * lax.axis_index("y") + lax.axis_index("c")


def _peers():
    x, y, c = (lax.axis_index(a) for a in AXES)
    out = []
    for kk in range(1, N_DEV):
        px = 1 - x if kk & 4 else x
        py = 1 - y if kk & 2 else y
        pc = 1 - c if kk & 1 else c
        out.append(((px, py, pc), 4 * px + 2 * py + pc))
    return out


def _exchange(arrays, name, gather):
    n = len(arrays)

    def body(*refs):
        srcs, outs = refs[:n], refs[n:2 * n]
        send_sems, recv_sems, local_sems = refs[2 * n:]
        me = _my_index()
        peers = _peers()
        started = []
        for a, (src, out) in enumerate(zip(srcs, outs)):
            mine = pltpu.make_async_copy(src if gather else src.at[me], out.at[me], local_sems.at[a])
            mine.start()
            started.append(mine)
        sends = []
        for i, (pos, idx) in enumerate(peers):
            for a, (src, out) in enumerate(zip(srcs, outs)):
                cp = pltpu.make_async_remote_copy(
                    src_ref=src if gather else src.at[idx], dst_ref=out.at[me], send_sem=send_sems.at[a, i],
                    recv_sem=recv_sems.at[a, i], device_id=pos, device_id_type=pl.DeviceIdType.MESH)
                cp.start()
                sends.append(cp)
        for i, (pos, idx) in enumerate(peers):
            for a, (src, out) in enumerate(zip(srcs, outs)):
                pltpu.make_async_remote_copy(
                    src_ref=src if gather else src.at[idx], dst_ref=out.at[idx], send_sem=send_sems.at[a, i],
                    recv_sem=recv_sems.at[a, i], device_id=pos, device_id_type=pl.DeviceIdType.MESH).wait_recv()
        for cp in sends:
            cp.wait_send()
        for mine in started:
            mine.wait()

    any_space = pl.BlockSpec(memory_space=pl.ANY)
    return pl.pallas_call(
        body, name=name, in_specs=[any_space] * n, out_specs=[any_space] * n,
        out_shape=[jax.ShapeDtypeStruct(((N_DEV,) + a.shape) if gather else a.shape, a.dtype) for a in arrays],
        scratch_shapes=[pltpu.SemaphoreType.DMA((n, N_DEV - 1)), pltpu.SemaphoreType.DMA((n, N_DEV - 1)),
                        pltpu.SemaphoreType.DMA((n,))],
    )(*arrays)


N_CHIP = 4


def _chip_places():
    x, y, c = (lax.axis_index(a) for a in AXES)
    return (x, y, c), (x, y, 1 - c), [(1 - x, y, c), (x, 1 - y, c), (1 - x, 1 - y, c)]


def _index_of(pos):
    return 4 * pos[0] + 2 * pos[1] + pos[2]


def _gather_two_level(arrays, name):
    n = len(arrays)

    def body(*refs):
        srcs, outs = refs[:n], refs[n:2 * n]
        send_sems, recv_sems, local_sems = refs[2 * n:]
        me, sibling, others = _chip_places()

        def copy(a, k, block, to, src=None):
            slot = outs[a].at[_index_of(block)]
            return pltpu.make_async_remote_copy(
                src_ref=slot if src is None else src, dst_ref=slot, send_sem=send_sems.at[7 * a + k],
                recv_sem=recv_sems.at[7 * a + k], device_id=to, device_id_type=pl.DeviceIdType.MESH)

        started = []
        for a, src in enumerate(srcs):
            mine = pltpu.make_async_copy(src, outs[a].at[_index_of(me)], local_sems.at[a])
            mine.start()
            started.append(mine)
        sends = []
        for a, src in enumerate(srcs):
            sends.append(copy(a, 0, me, sibling, src))
            sends += [copy(a, 1 + j, me, chip, src) for j, chip in enumerate(others)]
        for cp in sends:
            cp.start()
        for j, chip in enumerate(others):
            for a in range(n):
                copy(a, 1 + j, chip, me).wait_recv()
                fwd = copy(a, 4 + j, chip, sibling)
                fwd.start()
                sends.append(fwd)
        for a in range(n):
            copy(a, 0, sibling, me).wait_recv()
            for j, chip in enumerate(others):
                copy(a, 4 + j, (chip[0], chip[1], sibling[2]), me).wait_recv()
        for cp in sends:
            cp.wait_send()
        for mine in started:
            mine.wait()

    any_space = pl.BlockSpec(memory_space=pl.ANY)
    return pl.pallas_call(
        body, name=name, in_specs=[any_space] * n, out_specs=[any_space] * n,
        out_shape=[jax.ShapeDtypeStruct((N_DEV,) + a.shape, a.dtype) for a in arrays],
        scratch_shapes=[pltpu.SemaphoreType.DMA((7 * n,)), pltpu.SemaphoreType.DMA((7 * n,)),
                        pltpu.SemaphoreType.DMA((n,))],
    )(*arrays)


def _sibling_swap(arrays, name):
    n = len(arrays)

    def body(*refs):
        srcs, outs = refs[:n], refs[n:2 * n]
        send_sems, recv_sems = refs[2 * n:]
        (x, y, c), sibling, _ = _chip_places()
        sends = []
        for a, (src, out) in enumerate(zip(srcs, outs)):
            for q in range(N_CHIP):
                def copy(core, a=a, q=q, src=src, out=out):
                    return pltpu.make_async_remote_copy(
                        src_ref=src.at[2 * q + core], dst_ref=out.at[q], send_sem=send_sems.at[N_CHIP * a + q],
                        recv_sem=recv_sems.at[N_CHIP * a + q], device_id=sibling, device_id_type=pl.DeviceIdType.MESH)
                mine = copy(1 - c)
                mine.start()
                sends.append((mine, copy(c)))
        for mine, arrival in sends:
            arrival.wait_recv()
            mine.wait_send()

    any_space = pl.BlockSpec(memory_space=pl.ANY)
    return pl.pallas_call(
        body, name=name, in_specs=[any_space] * n, out_specs=[any_space] * n,
        out_shape=[jax.ShapeDtypeStruct((N_CHIP,) + a.shape[1:], a.dtype) for a in arrays],
        scratch_shapes=[pltpu.SemaphoreType.DMA((N_CHIP * n,)), pltpu.SemaphoreType.DMA((N_CHIP * n,))],
    )(*arrays)


def _chip_pair_sum(part, got, name):
    R, C = part.shape[1:]
    tr = R
    while tr * C * part.dtype.itemsize > REDUCE_BLOCK_BYTES // 4 and tr % 32 == 0:
        tr //= 2
    c = lax.axis_index("c")

    def body(c_ref, p_ref, g_ref, o_ref):
        del c_ref
        o_ref[...] = (p_ref[...].astype(F32) + g_ref[...].astype(F32)).astype(o_ref.dtype)

    return pl.pallas_call(
        body, name=name, grid_spec=pltpu.PrefetchScalarGridSpec(
            num_scalar_prefetch=1, grid=(N_CHIP, R // tr),
            in_specs=[pl.BlockSpec((None, tr, C), lambda q, i, cr: (2 * q + cr[0], i, 0)),
                      pl.BlockSpec((None, tr, C), lambda q, i, cr: (q, i, 0))],
            out_specs=pl.BlockSpec((None, tr, C), lambda q, i, cr: (q, i, 0))),
        out_shape=jax.ShapeDtypeStruct((N_CHIP, R, C), part.dtype),
        compiler_params=_cp(),
    )(jnp.reshape(c, (1,)).astype(jnp.int32), part, got)


def _peer_count(mode):
    return N_CHIP - 1 if mode == "chips" else N_DEV - 1


def _remote_copies(srcs, lands, send_sems, recv_sems, mode):
    if mode == "chips":
        (x, y, _), _, others = _chip_places()
        my_slot, peers = 2 * x + y, [(chip, 2 * chip[0] + chip[1]) for chip in others]
    else:
        my_slot, peers = _my_index(), _peers()
    out = []
    for i, (pos, idx) in enumerate(peers):
        for a, (src, land) in enumerate(zip(srcs, lands)):
            def copy(slot, a=a, src=src, land=land, i=i, pos=pos, idx=idx):
                return pltpu.make_async_remote_copy(
                    src_ref=src if mode == "gather" else src.at[idx], dst_ref=land.at[slot],
                    send_sem=send_sems.at[a * len(peers) + i], recv_sem=recv_sems.at[a * len(peers) + i],
                    device_id=pos, device_id_type=pl.DeviceIdType.MESH)
            out.append((copy(my_slot), copy(idx)))
    return out


def _exchange_start(arrays, name, mode):
    n = len(arrays)
    hbm = pl.BlockSpec(memory_space=pltpu.HBM)
    sem = pl.BlockSpec(memory_space=pltpu.SEMAPHORE)
    lands = [lax.empty(((N_DEV,) + a.shape) if mode == "gather" else a.shape, a.dtype) for a in arrays]

    def body(*refs):
        srcs, lands_ = refs[:n], refs[n:2 * n]
        send_sems, recv_sems = refs[2 * n:2 * n + 2]
        for mine, _ in _remote_copies(srcs, lands_, send_sems, recv_sems, mode):
            mine.start()
        refs[-1][...] = jnp.zeros_like(refs[-1])

    sems = pltpu.SemaphoreType.DMA((n * _peer_count(mode),))
    buffers = [pltpu.HBM(a.shape, a.dtype) for a in list(arrays) + lands]
    res = pl.pallas_call(
        body, name=name, in_specs=[hbm] * (2 * n), out_specs=[sem, sem] + [hbm] * (2 * n) + [pl.BlockSpec(memory_space=pltpu.VMEM)],
        out_shape=[sems, sems] + buffers + [jax.ShapeDtypeStruct((8, LANE), F32)],
        input_output_aliases={i: 2 + i for i in range(2 * n)},
        compiler_params=pltpu.CompilerParams(has_side_effects=pltpu.SideEffectType.DATAFLOW_SIDE_EFFECTING),
    )(*[pltpu.with_memory_space_constraint(a, pltpu.HBM) for a in list(arrays) + lands])
    return (res[0], res[1], res[2:2 + n], res[2 + n:2 + 2 * n]), res[-1]


def _exchange_wait(handle, after, name, mode):
    send_sems, recv_sems, srcs, lands = handle
    n = len(srcs)
    hbm = pl.BlockSpec(memory_space=pltpu.HBM)
    sem = pl.BlockSpec(memory_space=pltpu.SEMAPHORE)

    def body(*refs):
        for mine, arrival in _remote_copies(refs[:n], refs[n:2 * n], refs[2 * n], refs[2 * n + 1], mode):
            mine.wait_send()
            arrival.wait_recv()

    res = pl.pallas_call(
        body, name=name, in_specs=[hbm] * (2 * n) + [sem, sem, pl.BlockSpec(memory_space=pl.ANY)],
        out_specs=[hbm] * (2 * n), out_shape=[pltpu.HBM(a.shape, a.dtype) for a in list(srcs) + list(lands)],
        input_output_aliases={i: i for i in range(2 * n)},
        compiler_params=pltpu.CompilerParams(has_side_effects=pltpu.SideEffectType.DATAFLOW_SIDE_EFFECTING),
    )(*srcs, *lands, send_sems, recv_sems, after)
    return res[n:]


def _own_slot(land, mine, slot=None):
    slot = _my_index() if slot is None else slot
    return lax.dynamic_update_slice(land, mine, (slot,) + (0,) * (land.ndim - 1))


def _adamw(w, g, m, v):
    m = ADAM_B1 * m + (1.0 - ADAM_B1) * g
    v = ADAM_B2 * v + (1.0 - ADAM_B2) * (g * g)
    m_hat = m / (1.0 - ADAM_B1 ** ADAM_STEP)
    v_hat = v / (1.0 - ADAM_B2 ** ADAM_STEP)
    delta = -ADAM_LR * (m_hat / (jnp.sqrt(v_hat) + ADAM_EPS) + ADAM_WD * w)
    return delta, m, v


def _reduce_adamw(parts, w, m, v, name):
    nparts = len(parts)
    R, C = parts[0].shape[1:]
    tr = R
    while N_DEV * tr * C * parts[0].dtype.itemsize > REDUCE_BLOCK_BYTES and tr % 32 == 0:
        tr //= 2
    steps = R // tr

    def body(*refs):
        w_ref, m_ref, v_ref, g_ref, d_ref, nm_ref, nv_ref = refs[nparts:]
        for k, p_ref in enumerate(refs[:nparts]):
            @pl.when(pl.program_id(0) // steps == k)
            def _():
                g = p_ref[0].astype(F32)
                for s in range(1, p_ref.shape[0]):
                    g = g + p_ref[s].astype(F32)
                g_ref[...] = g
                d_ref[...], nm_ref[...], nv_ref[...] = _adamw(w_ref[...], g, m_ref[...], v_ref[...])

    def part_spec(k):
        return pl.BlockSpec((parts[k].shape[0], tr, C), lambda i: (0, jnp.clip(i - k * steps, 0, steps - 1), 0))

    row = pl.BlockSpec((tr, C), lambda i: (i, 0))
    return pl.pallas_call(
        body, name=name, grid=(nparts * steps,),
        in_specs=[part_spec(k) for k in range(nparts)] + [row, row, row],
        out_specs=[row] * 4, out_shape=[jax.ShapeDtypeStruct((nparts * R, C), F32)] * 4,
        compiler_params=_cp(),
    )(*parts, w, m, v)


BIG = ("w_in", "w_uq", "w_ukv", "w_out_a", "w_out_b", "w_out_c", "w_o")
SMALL = ("norm_g", "b_gate", "conv_w", "conv_b", "q_a_norm_g", "kv_a_norm_g", "mla_q_norm_g", "mla_k_norm_g",
         "dil_q_norm_g", "dil_k_norm_g")
PACK_ROWS = 128
REDUCE_BLOCK_BYTES = 6 * 1024 * 1024


def _pack_local(tensors):
    flat = jnp.concatenate([t.reshape(-1) for t in tensors])
    pad = (-flat.shape[0]) % (PACK_ROWS * LANE)
    return jnp.concatenate([flat, jnp.zeros((pad,), flat.dtype)]).reshape(-1, LANE)


def _unpack_local(rows, like):
    flat = rows.reshape(-1)
    out, off = [], 0
    for t in like:
        out.append(flat[off:off + t.size].reshape(t.shape))
        off += t.size
    return out


def _cols_to_slots(a):
    k = a.shape[0]
    return a.reshape(k, N_DEV, -1).transpose(1, 0, 2)


def _slots_to_cols(s):
    return s.transpose(1, 0, 2).reshape(s.shape[1], -1)


def _rope_tables(S):
    inv = ROPE_THETA ** (-jnp.arange(0, MLA_ROPE, 2, dtype=F32) / MLA_ROPE)
    ang = jnp.arange(S, dtype=F32)[:, None] * inv[None, :]
    cos, sin = jnp.cos(ang), jnp.sin(ang)
    one = jnp.ones((S, MLA_NOPE), F32)
    z16, z32, z64 = (jnp.zeros((S, n), F32) for n in (16, 32, 64))
    cosp = jnp.concatenate([one, cos, cos, jnp.ones((S, 32), F32)], axis=1)
    sa = jnp.concatenate([z64, -sin, z16, z32], axis=1)
    sb = jnp.concatenate([z64, z16, sin, z32], axis=1)
    return cosp, sa, sb


def _alibi_slopes():
    n = DIL_GROUPS * DIL_HEADS
    m = 2.0 ** (-8.0 * jnp.arange(1, n + 1, dtype=F32) / n)
    return m.reshape(DIL_GROUPS, NPAIR, 2)


def _pad_slots(s):
    n, k, c = s.shape
    return _slots_to_cols(jnp.concatenate([s, jnp.zeros((n, k, LANE - c), s.dtype)], axis=2))


def _layer_params(gw, small, l):
    p = {}
    p["wp"] = _pad_columns(gw["w_in"])
    p["norm_g"] = small["norm_g"][l][None]
    p["b_gate"] = small["b_gate"][l][None]
    p["conv_w"] = gw["conv_w"].transpose(1, 0, 2).reshape(CONV_K, CONV_WIDTH)
    p["conv_b"] = small["conv_b"][l][None]
    p["gq"] = small["q_a_norm_g"][l][None]
    p["gkv"] = small["kv_a_norm_g"][l][None]
    p["wuqp"] = _pad_slots(gw["w_uq"])
    kv = gw["w_ukv"]
    p["wkp"] = _pad_slots(kv[:, :, :MLA_NOPE])
    p["wv"] = kv[:, :, MLA_NOPE:].transpose(1, 0, 2).reshape(MLA_KV_LORA, MLA_HEADS * MLA_V)
    zpad = jnp.zeros((1, LANE - MLA_QK), F32)
    p["gmq"] = jnp.concatenate([small["mla_q_norm_g"][l][None], zpad], axis=1)
    p["gmk"] = jnp.concatenate([small["mla_k_norm_g"][l][None], zpad], axis=1)
    tile = lambda g: jnp.broadcast_to(g[:, None, :], (DIL_GROUPS, DIL_HEADS, DIL_HEAD_DIM)).reshape(1, DIL_QK)
    p["gdq"] = tile(small["dil_q_norm_g"][l])
    p["gdk"] = tile(small["dil_k_norm_g"][l])
    p["woa"], p["wob"], p["woc"] = (_slots_to_cols(gw[n]) for n in ("w_out_a", "w_out_b", "w_out_c"))
    p["wo"] = gw["w_o"].reshape(D_MODEL, D_MODEL)
    return p


def _layer_fwd(x, p, tabs, slopes, B, S):
    proj, ht = _inproj_fwd(x, p["norm_g"], p["wp"])
    ya = _mixa_fwd(proj, p["conv_w"], p["conv_b"], B, S)
    q, k, v = _mla_prep_fwd(proj, p["gq"], p["gkv"], p["wuqp"], p["wkp"], p["wv"], p["gmq"], p["gmk"], *tabs, S)
    ob, lse_b = _mla_attn_fwd(q, k, v, B, S)
    qn, kn, vn = _dil_prep_fwd(proj, p["gdq"], p["gdk"])
    ogs, lses = [], []
    for gi in range(DIL_GROUPS):
        o, lse = _dil_attn_fwd(gi, slopes[gi], qn, kn, vn, B, S)
        ogs.append(o)
        lses.append(lse)
    out = _merge_fwd(x, proj, p["b_gate"], ya, ob, ogs, lses, p["woa"], p["wob"], p["woc"], p["wo"])
    saved = dict(x=x, proj=proj, ht=ht, ya=ya, q=q, k=k, v=v, ob=ob, lse_b=lse_b, qn=qn, kn=kn, vn=vn, ogs=ogs, lses=lses)
    return out, saved


def _layer_bwd(dout, sv, p, tabs, slopes, B, S, big_ready=None):
    proj = sv["proj"]
    (dproj, dya, dob, dlb, dg0, dg1, dg2, dl0, dl1, dl2, merged, dpa, dpb, dpc, yb, yc, dbg) = _merge_bwd(
        dout, proj, p["b_gate"], sv["ya"], sv["ob"], sv["ogs"], sv["lses"], p["woa"], p["wob"], p["woc"], p["wo"])
    g = {}
    g["w_o"] = _matmul_tn(merged, dout, "dw_o").reshape(N_DEV, D_MODEL // N_DEV, D_MODEL)
    g["w_out_a"] = _cols_to_slots(_matmul_tn(sv["ya"], dpa, "dw_out_a"))
    g["w_out_b"] = _cols_to_slots(_matmul_tn(yb, dpb, "dw_out_b"))
    g["w_out_c"] = _cols_to_slots(_matmul_tn(yc, dpc, "dw_out_c"))
    g["b_gate"] = dbg[0]
    dproj, st = _mixa_bwd(dproj, dya, proj, p["conv_w"], p["conv_b"], B, S)
    g["conv_w"] = st[0:CONV_K]
    g["conv_b"] = st[CONV_K]
    dq, dk, dv = _mla_attn_bwd(sv["q"], sv["k"], sv["v"], dob, sv["lse_b"], dlb, B, S)
    dproj, dwuqp, dwkp, dwv, dgq, dgkv, dgmq, dgmk = _mla_prep_bwd(
        dproj, dq, dk, dv, proj, p["gq"], p["gkv"], p["wuqp"], p["wkp"], p["wv"], p["gmq"], p["gmk"], *tabs, S)
    g["w_uq"] = _cols_to_slots(dwuqp)[:, :, :MLA_QK]
    g["w_ukv"] = jnp.concatenate([_cols_to_slots(dwkp)[:, :, :MLA_NOPE], _cols_to_slots(dwv)], axis=2)
    g["q_a_norm_g"], g["kv_a_norm_g"] = dgq[0], dgkv[0]
    g["mla_q_norm_g"], g["mla_k_norm_g"] = dgmq[0, :MLA_QK], dgmk[0, :MLA_QK]
    dqkv = None
    for gi, (dog, dlg) in enumerate(((dg0, dl0), (dg1, dl1), (dg2, dl2))):
        dqkv = _dil_attn_bwd(gi, slopes[gi], sv["qn"], sv["kn"], sv["vn"], dog, sv["lses"][gi], dlg, dqkv, B, S)
    dproj, dgdq, dgdk = _dil_prep_bwd(dproj, *dqkv, proj, p["gdq"], p["gdk"])
    g["dil_q_norm_g"] = dgdq.reshape(DIL_GROUPS, DIL_HEADS, DIL_HEAD_DIM).sum(axis=1)
    g["dil_k_norm_g"] = dgdk.reshape(DIL_GROUPS, DIL_HEADS, DIL_HEAD_DIM).sum(axis=1)
    g["w_in"] = _unpad_columns(_matmul_nn(sv["ht"], dproj, "dw_in"))
    token = None if big_ready is None else big_ready(g)
    dx, dng = _inproj_bwd_x(dproj, p["wp"], sv["x"], _after(token, p["norm_g"]), dout)
    g["norm_g"] = dng[0]
    return dx, g


def _after(token, a):
    return a if token is None else a + token[0:1, 0:1]


def _local_step(x, target, small, B, S, weights_of, grads_out, big_ready=None):
    tabs = _rope_tables(S)
    sl = _alibi_slopes()
    slopes = [sl[gi] * float(DIL_PATTERNS[gi][1]) for gi in range(DIL_GROUPS)]
    params, saved = [], []
    for l in range(DEPTH):
        gw, token = weights_of(l, x)
        p = _layer_params(gw, small, l)
        p["norm_g"] = _after(token, p["norm_g"])
        x, sv = _layer_fwd(x, p, tabs, slopes, B, S)
        params.append(p)
        saved.append(sv)
    dout, lparts = _loss_head(x, target)
    sq = jnp.sum(lparts[:, 0, 0])
    token = None
    for l in reversed(range(DEPTH)):
        p = dict(params[l], b_gate=_after(token, params[l]["b_gate"]))
        ready = None if big_ready is None else (lambda g, l=l: big_ready(l, g))
        dout, g = _layer_bwd(dout, saved[l], p, tabs, slopes, B, S, ready)
        token = grads_out(l, g, dout)
    return sq, dout


def kernel(x, norm_g, w_in, b_gate, conv_w, conv_b, q_a_norm_g, w_uq, kv_a_norm_g, w_ukv, mla_q_norm_g, mla_k_norm_g, dil_q_norm_g, dil_k_norm_g, w_out_a, w_out_b, w_out_c, w_o, loss_target, m_norm_g, m_w_in, m_b_gate, m_conv_w, m_conv_b, m_q_a_norm_g, m_w_uq, m_kv_a_norm_g, m_w_ukv, m_mla_q_norm_g, m_mla_k_norm_g, m_dil_q_norm_g, m_dil_k_norm_g, m_w_out_a, m_w_out_b, m_w_out_c, m_w_o, v_norm_g, v_w_in, v_b_gate, v_conv_w, v_conv_b, v_q_a_norm_g, v_w_uq, v_kv_a_norm_g, v_w_ukv, v_mla_q_norm_g, v_mla_k_norm_g, v_dil_q_norm_g, v_dil_k_norm_g, v_w_out_a, v_w_out_b, v_w_out_c, v_w_o):
    names = ("norm_g", "w_in", "b_gate", "conv_w", "conv_b", "q_a_norm_g", "w_uq", "kv_a_norm_g", "w_ukv",
             "mla_q_norm_g", "mla_k_norm_g", "dil_q_norm_g", "dil_k_norm_g", "w_out_a", "w_out_b", "w_out_c", "w_o")
    w = dict(zip(names, (norm_g, w_in, b_gate, conv_w, conv_b, q_a_norm_g, w_uq, kv_a_norm_g, w_ukv, mla_q_norm_g,
                         mla_k_norm_g, dil_q_norm_g, dil_k_norm_g, w_out_a, w_out_b, w_out_c, w_o)))
    m = dict(zip(names, (m_norm_g, m_w_in, m_b_gate, m_conv_w, m_conv_b, m_q_a_norm_g, m_w_uq, m_kv_a_norm_g, m_w_ukv,
                         m_mla_q_norm_g, m_mla_k_norm_g, m_dil_q_norm_g, m_dil_k_norm_g, m_w_out_a, m_w_out_b,
                         m_w_out_c, m_w_o)))
    v = dict(zip(names, (v_norm_g, v_w_in, v_b_gate, v_conv_w, v_conv_b, v_q_a_norm_g, v_w_uq, v_kv_a_norm_g, v_w_ukv,
                         v_mla_q_norm_g, v_mla_k_norm_g, v_dil_q_norm_g, v_dil_k_norm_g, v_w_out_a, v_w_out_b,
                         v_w_out_c, v_w_o)))
    B, S, _ = x.shape
    me = _my_index()
    cshard = CONV_WIDTH // N_DEV

    shards = [[w[n][l].astype(BF16) for n in BIG] for l in range(DEPTH)]
    state = {}

    def weights_of(l, after):
        if l == 0:
            got = _gather_two_level(shards[0] + [conv_w], "all_gather_weights_0")
            state["gather"], token = _exchange_start(shards[1], "all_gather_weights_1_start", "gather")
            state["conv_w"] = got[-1]
        else:
            landed = _exchange_wait(state["gather"], after, "all_gather_weights_1_wait", "gather")
            got, token = [_own_slot(a, s[None]) for a, s in zip(landed, shards[1])], None
        gw = dict(zip(BIG, got))
        gw["conv_w"] = state["conv_w"][:, l]
        return gw, token

    recv, small_parts = {}, {}
    my_chip = 2 * lax.axis_index("x") + lax.axis_index("y")

    def big_ready(l, g):
        send = [g[n].astype(BF16) for n in BIG]
        if l == DEPTH - 1:
            state["scatter"], token = _exchange_start(send, "exchange_weight_grads_1_start", "scatter")
        else:
            swapped = _sibling_swap(send, "exchange_weight_grads_0_sibling")
            send = [_chip_pair_sum(s, t, "chip_pair_sum_" + n) for n, s, t in zip(BIG, send, swapped)]
            state["chips"], token = _exchange_start(send, "exchange_weight_grads_0_start", "chips")
        state["sent", l] = send
        return token

    def grads_out(l, g, after):
        small_parts[l] = [g[n] for n in SMALL]
        if l == DEPTH - 1:
            return None
        for k, key, mode, slot in ((DEPTH - 1, "scatter", "scatter", me), (0, "chips", "chips", my_chip)):
            landed = _exchange_wait(state[key], after, f"exchange_weight_grads_{k}_wait", mode)
            mine = [lax.dynamic_slice_in_dim(s, slot, 1, axis=0) for s in state["sent", k]]
            recv[k] = [_own_slot(a, s, slot) for a, s in zip(landed, mine)]
        return None

    sq, grad_x = _local_step(x.reshape(B * S, D_MODEL), loss_target.reshape(B * S, D_MODEL), w, B, S,
                             weights_of, grads_out, big_ready)
    loss = lax.psum(sq * (0.5 / D_MODEL), AXES)

    res = {}
    for i, n in enumerate(BIG):
        rows = lambda a: a.reshape(-1, a.shape[-1])
        outs = _reduce_adamw([recv[l][i] for l in range(DEPTH)], rows(w[n]), rows(m[n]), rows(v[n]),
                             "reduce_adamw_" + n)
        res[n] = tuple(a.reshape(w[n].shape) for a in outs)
    part = {n: jnp.stack([small_parts[l][i] for l in range(DEPTH)]) for i, n in enumerate(SMALL)}

    def widen(t):
        return lax.dynamic_update_slice(jnp.zeros((DEPTH, CONV_K, CONV_WIDTH), F32), t, (0, 0, me * cshard))

    small_like = [part[n] for n in SMALL]
    pick = lambda d: [widen(d[n]) if n == "conv_w" else d[n] for n in SMALL]
    parts, = _exchange([_pack_local(small_like)], "all_gather_small_grads", gather=True)
    gs, ds, ms, vs = _reduce_adamw([parts], _pack_local(pick(w)), _pack_local(pick(m)), _pack_local(pick(v)),
                                   "reduce_adamw_small")
    for n, t in zip(SMALL, zip(*(_unpack_local(a, small_like) for a in (gs, ds, ms, vs)))):
        if n == "conv_w":
            t = tuple(lax.dynamic_slice(a, (0, 0, me * cshard), (DEPTH, CONV_K, cshard)) for a in t)
        res[n] = t

    out = [loss, grad_x.reshape(B, S, D_MODEL)]
    for i in range(4):
        out += [res[n][i] for n in names]
    return tuple(out)
```

```python
import jax
import jax.numpy as jnp
from jax import lax
from jax.experimental import pallas as pl
from jax.experimental.pallas import tpu as pltpu

F32 = jnp.float32
BF16 = jnp.bfloat16

D_MODEL = 1024
DEPTH = 2
CONV_WIDTH = 512
CONV_K = 3
MLA_HEADS = 8
MLA_Q_LORA = 256
MLA_KV_LORA = 128
MLA_NOPE = 64
MLA_ROPE = 32
MLA_V = 64
MLA_QK = MLA_NOPE + MLA_ROPE
MLA_SCALE = MLA_QK ** -0.5
ROPE_THETA = 10000.0
DIL_PATTERNS = ((128, 1), (512, 4), (2048, 16))
DIL_GROUPS = 3
DIL_HEADS = 8
DIL_HEAD_DIM = 64
DIL_WIDTH = DIL_HEADS * DIL_HEAD_DIM
DIL_QK = DIL_GROUPS * DIL_WIDTH
EPS = 1e-6
N_IN = 11168

ADAM_LR = 0.001
ADAM_B1 = 0.9
ADAM_B2 = 0.999
ADAM_EPS = 1e-08
ADAM_WD = 0.01
ADAM_STEP = 10

N_DEV = 8
AXES = ("x", "y", "c")
LANE = 128
HALF = 64
NPAIR = 4

CB_AB, CB_AC, CB_AX, CB_AZ = 0, 4, 8, 12
CB_CQ, CB_CKV, CB_KPE = 16, 18, 19
CB_BZ = 20
CB_DQ, CB_DK, CB_DV = 24, 36, 48
CB_CZ, CB_GATE = 60, 64
NCB = 88
PP = NCB * LANE
KPE_END = CB_KPE * LANE + MLA_ROPE
SHARD_COLS = N_IN // N_DEV
NEG = -1e30
VMEM_LIMIT = 56 * 1024 * 1024


def _pad_columns(shards):
    parts = []
    for p in range(N_DEV):
        cut = min(max(KPE_END - p * SHARD_COLS, 0), SHARD_COLS)
        if 0 < cut < SHARD_COLS:
            parts += [shards[p, :, :cut], jnp.zeros((shards.shape[1], LANE - MLA_ROPE), shards.dtype), shards[p, :, cut:]]
        else:
            parts.append(shards[p])
    return jnp.concatenate(parts, axis=1)


def _unpad_columns(wp):
    def columns(a, b):
        gap = LANE - MLA_ROPE
        if b <= KPE_END:
            return wp[:, a:b]
        if a >= KPE_END:
            return wp[:, a + gap:b + gap]
        return jnp.concatenate([wp[:, a:KPE_END], wp[:, KPE_END + gap:b + gap]], axis=1)

    return jnp.stack([columns(p * SHARD_COLS, (p + 1) * SHARD_COLS) for p in range(N_DEV)])


def _put_copies(stages, dst_ref, sems, slot, rows, cols):
    return [pltpu.make_async_copy(st.at[slot], dst_ref.at[rows, pl.ds(c0, st.shape[-1])], sems.at[slot, k])
            for k, (st, c0) in enumerate(zip(stages, cols))]


def _put_pipeline(step, nsteps, copies_of, fill):
    @pl.when(step >= 2)
    def _():
        for cp in copies_of(step - 2):
            cp.wait()

    fill(step % 2)
    for cp in copies_of(step):
        cp.start()

    @pl.when(step == nsteps - 1)
    def _():
        if nsteps >= 2:
            for cp in copies_of(step - 1):
                cp.wait()
        for cp in copies_of(step):
            cp.wait()


def _cp():
    return pltpu.CompilerParams(vmem_limit_bytes=VMEM_LIMIT)


def _rstd(x, n):
    return lax.rsqrt(jnp.sum(x * x, axis=-1, keepdims=True) * (1.0 / n) + EPS)


def _sigmoid(z):
    return 1.0 / (1.0 + jnp.exp(-z))


def _silu(z):
    return z * _sigmoid(z)


def _dsilu(z):
    s = _sigmoid(z)
    return s * (1.0 + z * (1.0 - s))


def _mm(a, b):
    return jnp.dot(a.astype(BF16), b.astype(BF16), preferred_element_type=F32)


def _mm_nt(a, b):
    return lax.dot_general(a.astype(BF16), b.astype(BF16), (((1,), (1,)), ((), ())), preferred_element_type=F32)


def _mm_tn(a, b):
    return lax.dot_general(a.astype(BF16), b.astype(BF16), (((0,), (0,)), ((), ())), preferred_element_type=F32)


def _lane_lo(shape):
    return lax.broadcasted_iota(jnp.int32, shape, len(shape) - 1) < HALF


def _head_bcast_sum(x, terms=3):
    w = x.shape[-1]
    same = (lax.broadcasted_iota(jnp.int32, (w, w), 0) // HALF) == (lax.broadcasted_iota(jnp.int32, (w, w), 1) // HALF)
    ones = jnp.where(same, 1.0, 0.0).astype(jnp.bfloat16)
    total = None
    for _ in range(terms):
        term = x.astype(jnp.bfloat16)
        x = x - term.astype(F32)
        part = jnp.dot(term, ones, preferred_element_type=F32)
        total = part if total is None else total + part
    return total


def _rope(t, cos, sa, sb):
    return t * cos + pltpu.roll(t, LANE - 16, axis=1) * sa + pltpu.roll(t, 16, axis=1) * sb


def _rope_t(d, cos, sa, sb):
    return d * cos + pltpu.roll(d * sa, 16, axis=1) + pltpu.roll(d * sb, LANE - 16, axis=1)


def _shift_down(u, k):
    rows = lax.broadcasted_iota(jnp.int32, u.shape, 0)
    return jnp.where(rows >= k, pltpu.roll(u, k, axis=0), 0.0)


def _shift_up(u, k):
    n = u.shape[0]
    rows = lax.broadcasted_iota(jnp.int32, u.shape, 0)
    return jnp.where(rows < n - k, pltpu.roll(u, n - k, axis=0), 0.0)


def _tile(n, want):
    t = min(n, want)
    assert n % t == 0, (n, want)
    return t


def _inproj_fwd(x, g, wp):
    T = x.shape[0]
    tm, tn = _tile(T, 2048), 512

    def body(x_ref, g_ref, w_ref, proj_ref, ht_ref, h_ref):
        @pl.when(pl.program_id(1) == 0)
        def _():
            n = min(tm, 512)
            for r0 in range(0, tm, n):
                xv = x_ref[r0:r0 + n, :]
                h = xv * _rstd(xv, D_MODEL) * g_ref[...]
                h_ref[r0:r0 + n, :] = h.astype(BF16)
                ht_ref[:, r0:r0 + n] = h.T.astype(BF16)

        proj_ref[...] = jnp.dot(h_ref[...], w_ref[...], preferred_element_type=F32).astype(BF16)

    return pl.pallas_call(
        body, name="inproj_fwd", grid=(T // tm, PP // tn),
        in_specs=[pl.BlockSpec((tm, D_MODEL), lambda i, j: (i, 0)),
                  pl.BlockSpec((1, D_MODEL), lambda i, j: (0, 0)),
                  pl.BlockSpec((D_MODEL, tn), lambda i, j: (0, j))],
        out_specs=[pl.BlockSpec((tm, tn), lambda i, j: (i, j)),
                   pl.BlockSpec((D_MODEL, tm), lambda i, j: (0, i))],
        out_shape=[jax.ShapeDtypeStruct((T, PP), BF16), jax.ShapeDtypeStruct((D_MODEL, T), BF16)],
        scratch_shapes=[pltpu.VMEM((tm, D_MODEL), BF16)],
        compiler_params=_cp(),
    )(x, g, wp)


def _matmul_nn(at, b, name):
    K, T = at.shape
    N = b.shape[1]
    tt, tn = _tile(T, 1024), _tile(N, 2816)
    nk = T // tt

    def body(a_ref, b_ref, o_ref, acc_ref):
        k = pl.program_id(1)

        @pl.when(k == 0)
        def _():
            acc_ref[...] = jnp.zeros_like(acc_ref)

        acc_ref[...] += jnp.dot(a_ref[...], b_ref[...], preferred_element_type=F32)

        @pl.when(k == nk - 1)
        def _():
            o_ref[...] = acc_ref[...].astype(BF16)

    return pl.pallas_call(
        body, name=name, grid=(N // tn, nk),
        in_specs=[pl.BlockSpec((K, tt), lambda j, k: (0, k)),
                  pl.BlockSpec((tt, tn), lambda j, k: (k, j))],
        out_specs=pl.BlockSpec((K, tn), lambda j, k: (0, j)),
        out_shape=jax.ShapeDtypeStruct((K, N), BF16),
        scratch_shapes=[pltpu.VMEM((K, tn), F32)],
        compiler_params=_cp(),
    )(at, b)


def _matmul_tn(a, b, name):
    T, K = a.shape
    N = b.shape[1]
    tt, tn = _tile(T, 512), _tile(N, 1024)

    def body(a_ref, b_ref, o_ref):
        @pl.when(pl.program_id(1) == 0)
        def _():
            o_ref[...] = jnp.zeros_like(o_ref)

        o_ref[...] += _mm_tn(a_ref[...], b_ref[...])

    return pl.pallas_call(
        body, name=name, grid=(N // tn, T // tt),
        in_specs=[pl.BlockSpec((tt, K), lambda j, k: (k, 0)),
                  pl.BlockSpec((tt, tn), lambda j, k: (k, j))],
        out_specs=pl.BlockSpec((K, tn), lambda j, k: (0, j)),
        out_shape=jax.ShapeDtypeStruct((K, N), F32),
        compiler_params=_cp(),
    )(a, b)


def _inproj_bwd_x(dproj, wp, x, g, dout):
    T = x.shape[0]
    tm, tk = _tile(T, 1024), 1024
    nk = PP // tk

    def body(dp_ref, w_ref, x_ref, g_ref, do_ref, dx_ref, dg_ref, acc_ref):
        i, k = pl.program_id(0), pl.program_id(1)

        @pl.when(k == 0)
        def _():
            acc_ref[...] = jnp.zeros_like(acc_ref)

        @pl.when((k == 0) & (i == 0))
        def _():
            dg_ref[...] = jnp.zeros_like(dg_ref)

        acc_ref[...] += _mm_nt(dp_ref[...], w_ref[...])

        @pl.when(k == nk - 1)
        def _():
            dh = acc_ref[...]
            xv = x_ref[...]
            r = _rstd(xv, D_MODEL)
            gy = dh * g_ref[...]
            dot = jnp.sum(xv * gy, axis=-1, keepdims=True) * (1.0 / D_MODEL)
            dx_ref[...] = do_ref[...] + r * gy - xv * (r * r * r) * dot
            dg_ref[...] += jnp.sum(dh * xv * r, axis=0, keepdims=True)

    return pl.pallas_call(
        body, name="inproj_bwd_x", grid=(T // tm, nk),
        in_specs=[pl.BlockSpec((tm, tk), lambda i, k: (i, k)),
                  pl.BlockSpec((D_MODEL, tk), lambda i, k: (0, k)),
                  pl.BlockSpec((tm, D_MODEL), lambda i, k: (i, 0)),
                  pl.BlockSpec((1, D_MODEL), lambda i, k: (0, 0)),
                  pl.BlockSpec((tm, D_MODEL), lambda i, k: (i, 0))],
        out_specs=[pl.BlockSpec((tm, D_MODEL), lambda i, k: (i, 0)),
                   pl.BlockSpec((1, D_MODEL), lambda i, k: (0, 0))],
        out_shape=[jax.ShapeDtypeStruct((T, D_MODEL), F32), jax.ShapeDtypeStruct((1, D_MODEL), F32)],
        scratch_shapes=[pltpu.VMEM((tm, D_MODEL), F32)],
        compiler_params=_cp(),
    )(dproj, wp, x, g, dout)


A_SEGS = (CB_AB, CB_AC, CB_AX, CB_AZ)


def _mixa_fwd(proj, cw, cb, B, S):
    nc = CONV_WIDTH // LANE

    def body(ab_ref, ac_ref, ax_ref, az_ref, cw_ref, cb_ref, y_ref):
        ab, ac, ax, az = (r[...].astype(F32) for r in (ab_ref, ac_ref, ax_ref, az_ref))
        u = ac * ax
        conv = cb_ref[...] + cw_ref[0:1, :] * _shift_down(u, 2) + cw_ref[1:2, :] * _shift_down(u, 1) + cw_ref[2:3, :] * u
        y_ref[...] = (ab * conv * _silu(az)).astype(BF16)

    return pl.pallas_call(
        body, name="mixa_fwd", grid=(B, nc),
        in_specs=[pl.BlockSpec((S, LANE), lambda b, j, c0=c0: (b, c0 + j)) for c0 in A_SEGS]
                 + [pl.BlockSpec((CONV_K, LANE), lambda b, j: (0, j)),
                    pl.BlockSpec((1, LANE), lambda b, j: (0, j))],
        out_specs=pl.BlockSpec((S, LANE), lambda b, j: (b, j)),
        out_shape=jax.ShapeDtypeStruct((B * S, CONV_WIDTH), BF16),
        compiler_params=_cp(),
    )(proj, proj, proj, proj, cw, cb)


def _mixa_bwd(dproj, dy, proj, cw, cb, B, S):
    nc = CONV_WIDTH // LANE

    def body(dpin_ref, dy_ref, ab_ref, ac_ref, ax_ref, az_ref, cw_ref, cb_ref, dp_ref, st_ref, stage, sems):
        del dpin_ref
        j, b = pl.program_id(0), pl.program_id(1)
        ab, ac, ax, az = (r[...].astype(F32) for r in (ab_ref, ac_ref, ax_ref, az_ref))
        u = ac * ax
        u1, u2 = _shift_down(u, 1), _shift_down(u, 2)
        w0, w1, w2 = cw_ref[0:1, :], cw_ref[1:2, :], cw_ref[2:3, :]
        conv = cb_ref[...] + w0 * u2 + w1 * u1 + w2 * u
        s = _silu(az)
        d = dy_ref[...]
        dconv = d * ab * s
        du = w2 * dconv + w1 * _shift_up(dconv, 1) + w0 * _shift_up(dconv, 2)
        grads = (d * conv * s, du * ax, du * ac, d * ab * conv * _dsilu(az))

        def fill(slot):
            for k, v in enumerate(grads):
                stage[slot, k] = v.astype(BF16)

        def copies_of(step):
            sj, sb = step // B, step % B
            return _put_copies([stage.at[:, k] for k in range(4)], dp_ref, sems, step % 2,
                               pl.ds(pl.multiple_of(sb * S, S), S),
                               [pl.multiple_of((c0 + sj) * LANE, LANE) for c0 in A_SEGS])

        _put_pipeline(j * B + b, nc * B, copies_of, fill)
        row = lax.broadcasted_iota(jnp.int32, (8, LANE), 0)
        st = jnp.zeros((8, LANE), F32)
        for r, v in enumerate((dconv * u2, dconv * u1, dconv * u, dconv)):
            st = st + jnp.where(row == r, jnp.sum(v, axis=0, keepdims=True), 0.0)

        @pl.when(pl.program_id(1) == 0)
        def _():
            st_ref[...] = st

        @pl.when(pl.program_id(1) != 0)
        def _():
            st_ref[...] += st

    return pl.pallas_call(
        body, name="mixa_bwd", grid=(nc, B),
        in_specs=[pl.BlockSpec(memory_space=pl.ANY),
                  pl.BlockSpec((S, LANE), lambda j, b: (b, j))]
                 + [pl.BlockSpec((S, LANE), lambda j, b, c0=c0: (b, c0 + j)) for c0 in A_SEGS]
                 + [pl.BlockSpec((CONV_K, LANE), lambda j, b: (0, j)),
                    pl.BlockSpec((1, LANE), lambda j, b: (0, j))],
        out_specs=[pl.BlockSpec(memory_space=pl.ANY),
                   pl.BlockSpec((8, LANE), lambda j, b: (0, j))],
        out_shape=[jax.ShapeDtypeStruct(dproj.shape, BF16), jax.ShapeDtypeStruct((8, CONV_WIDTH), F32)],
        scratch_shapes=[pltpu.VMEM((2, 4, S, LANE), BF16), pltpu.SemaphoreType.DMA((2, 4))],
        input_output_aliases={0: 0},
        compiler_params=_cp(),
    )(dproj, dy, proj, proj, proj, proj, cw, cb)


def _mla_prep_fwd(proj, gq, gkv, wuqp, wkp, wv, gmq, gmk, cos, sa, sb, S):
    T = proj.shape[0]
    ts = _tile(S, 512)
    ns = S // ts
    W = MLA_HEADS * LANE

    def body(p_ref, gq_ref, gkv_ref, wuq_ref, wk_ref, wv_ref, gmq_ref, gmk_ref, cos_ref, sa_ref, sb_ref,
             q_ref, k_ref, v_ref):
        cq = p_ref[:, 0:2 * LANE].astype(F32)
        ckv = p_ref[:, 2 * LANE:3 * LANE].astype(F32)
        kpe = pltpu.roll(p_ref[:, 3 * LANE:4 * LANE].astype(F32), HALF, axis=1)
        cqn = cq * _rstd(cq, MLA_Q_LORA) * gq_ref[...]
        ckn = (ckv * _rstd(ckv, MLA_KV_LORA) * gkv_ref[...]).astype(BF16)
        q0 = _mm(cqn, wuq_ref[...])
        kn = _mm(ckn, wk_ref[...])
        v_ref[...] = _mm(ckn, wv_ref[...]).astype(BF16)
        c, a, b = cos_ref[...], sa_ref[...], sb_ref[...]
        kpe_rot = _rope(kpe * gmk_ref[...], c, a, b)
        for h in range(MLA_HEADS):
            q0h = q0[:, h * LANE:(h + 1) * LANE]
            q_ref[h] = (_rope(q0h * _rstd(q0h, MLA_QK) * gmq_ref[...], c, a, b) * MLA_SCALE).astype(BF16)
            knh = kn[:, h * LANE:(h + 1) * LANE]
            k_ref[h] = (_rstd(knh + kpe, MLA_QK) * (knh * gmk_ref[...] + kpe_rot)).astype(BF16)

    def whole(r, c):
        return pl.BlockSpec((r, c), lambda i: (0, 0))

    tab = pl.BlockSpec((ts, LANE), lambda i: (i % ns, 0))
    return pl.pallas_call(
        body, name="mla_prep_fwd", grid=(T // ts,),
        in_specs=[pl.BlockSpec((ts, 4 * LANE), lambda i: (i, CB_CQ // 4)),
                  whole(1, MLA_Q_LORA), whole(1, MLA_KV_LORA), whole(MLA_Q_LORA, W), whole(MLA_KV_LORA, W),
                  whole(MLA_KV_LORA, MLA_HEADS * MLA_V), whole(1, LANE), whole(1, LANE), tab, tab, tab],
        out_specs=[pl.BlockSpec((MLA_HEADS, ts, LANE), lambda i: (0, i, 0)),
                   pl.BlockSpec((MLA_HEADS, ts, LANE), lambda i: (0, i, 0)),
                   pl.BlockSpec((ts, MLA_HEADS * MLA_V), lambda i: (i, 0))],
        out_shape=[jax.ShapeDtypeStruct((MLA_HEADS, T, LANE), BF16), jax.ShapeDtypeStruct((MLA_HEADS, T, LANE), BF16),
                   jax.ShapeDtypeStruct((T, MLA_HEADS * MLA_V), BF16)],
        compiler_params=_cp(),
    )(proj, gq, gkv, wuqp, wkp, wv, gmq, gmk, cos, sa, sb)


def _mla_prep_bwd(dproj, dq, dk, dv, proj, gq, gkv, wuqp, wkp, wv, gmq, gmk, cos, sa, sb, S):
    T = proj.shape[0]
    ts = _tile(S, 256)
    ns = S // ts
    W = MLA_HEADS * LANE

    def body(dpin_ref, dq_ref, dk_ref, dv_ref, p_ref, gq_ref, gkv_ref, wuq_ref, wk_ref, wv_ref, gmq_ref, gmk_ref,
             cos_ref, sa_ref, sb_ref,
             dp_ref, dwuq_ref, dwk_ref, dwv_ref, dgq_ref, dgkv_ref, dgmq_ref, dgmk_ref, dq0_ref, dkn_ref):
        del dpin_ref

        @pl.when(pl.program_id(0) == 0)
        def _():
            for r in (dwuq_ref, dwk_ref, dwv_ref, dgq_ref, dgkv_ref, dgmq_ref, dgmk_ref):
                r[...] = jnp.zeros_like(r)

        cq = p_ref[:, 0:2 * LANE].astype(F32)
        ckv = p_ref[:, 2 * LANE:3 * LANE].astype(F32)
        kpe = pltpu.roll(p_ref[:, 3 * LANE:4 * LANE].astype(F32), HALF, axis=1)
        rq = _rstd(cq, MLA_Q_LORA)
        rkv = _rstd(ckv, MLA_KV_LORA)
        gq, gkv, gmq, gmk = gq_ref[...], gkv_ref[...], gmq_ref[...], gmk_ref[...]
        cqn = (cq * rq * gq).astype(BF16)
        ckn = (ckv * rkv * gkv).astype(BF16)
        q0 = _mm(cqn, wuq_ref[...])
        kn = _mm(ckn, wk_ref[...])
        c, a, b = cos_ref[...], sa_ref[...], sb_ref[...]
        lane = lax.broadcasted_iota(jnp.int32, (ts, LANE), 1)
        dgmq = jnp.zeros((1, LANE), F32)
        dgmk = jnp.zeros((1, LANE), F32)
        dkpe = jnp.zeros((ts, LANE), F32)
        for h in range(MLA_HEADS):
            q0h = q0[:, h * LANE:(h + 1) * LANE]
            r = _rstd(q0h, MLA_QK)
            d1 = _rope_t(dq_ref[h], c, a, b)
            gy = d1 * gmq
            dq0_ref[:, h * LANE:(h + 1) * LANE] = (
                r * gy - q0h * (r * r * r) * (jnp.sum(q0h * gy, axis=-1, keepdims=True) * (1.0 / MLA_QK))).astype(BF16)
            dgmq = dgmq + jnp.sum(d1 * q0h * r, axis=0, keepdims=True)
            k0h = kn[:, h * LANE:(h + 1) * LANE] + kpe
            r = _rstd(k0h, MLA_QK)
            d1 = _rope_t(dk_ref[h], c, a, b)
            gy = d1 * gmk
            dk0 = r * gy - k0h * (r * r * r) * (jnp.sum(k0h * gy, axis=-1, keepdims=True) * (1.0 / MLA_QK))
            dgmk = dgmk + jnp.sum(d1 * k0h * r, axis=0, keepdims=True)
            dkn_ref[:, h * LANE:(h + 1) * LANE] = jnp.where(lane < MLA_NOPE, dk0, 0.0).astype(BF16)
            dkpe = dkpe + jnp.where((lane >= MLA_NOPE) & (lane < MLA_QK), dk0, 0.0)
        dq0 = dq0_ref[...]
        dkn = dkn_ref[...]
        dvv = dv_ref[...]
        dwuq_ref[...] += _mm_tn(cqn, dq0)
        dwk_ref[...] += _mm_tn(ckn, dkn)
        dwv_ref[...] += _mm_tn(ckn, dvv)
        dgmq_ref[...] += dgmq
        dgmk_ref[...] += dgmk
        dcqn = _mm_nt(dq0, wuq_ref[...])
        gy = dcqn * gq
        dp_ref[:, 0:2 * LANE] = (
            rq * gy - cq * (rq * rq * rq) * (jnp.sum(cq * gy, axis=-1, keepdims=True) * (1.0 / MLA_Q_LORA))).astype(BF16)
        dgq_ref[...] += jnp.sum(dcqn * cq * rq, axis=0, keepdims=True)
        dckn = _mm_nt(dkn, wk_ref[...]) + _mm_nt(dvv, wv_ref[...])
        gy = dckn * gkv
        dp_ref[:, 2 * LANE:3 * LANE] = (
            rkv * gy - ckv * (rkv * rkv * rkv) * (jnp.sum(ckv * gy, axis=-1, keepdims=True) * (1.0 / MLA_KV_LORA))).astype(BF16)
        dgkv_ref[...] += jnp.sum(dckn * ckv * rkv, axis=0, keepdims=True)
        dp_ref[:, 3 * LANE:4 * LANE] = pltpu.roll(dkpe, HALF, axis=1).astype(BF16)

    def whole(r, c):
        return pl.BlockSpec((r, c), lambda i: (0, 0))

    tab = pl.BlockSpec((ts, LANE), lambda i: (i % ns, 0))
    heads = pl.BlockSpec((MLA_HEADS, ts, LANE), lambda i: (0, i, 0))
    return pl.pallas_call(
        body, name="mla_prep_bwd", grid=(T // ts,),
        in_specs=[pl.BlockSpec(memory_space=pl.ANY), heads, heads,
                  pl.BlockSpec((ts, MLA_HEADS * MLA_V), lambda i: (i, 0)),
                  pl.BlockSpec((ts, 4 * LANE), lambda i: (i, CB_CQ // 4)),
                  whole(1, MLA_Q_LORA), whole(1, MLA_KV_LORA), whole(MLA_Q_LORA, W), whole(MLA_KV_LORA, W),
                  whole(MLA_KV_LORA, MLA_HEADS * MLA_V), whole(1, LANE), whole(1, LANE), tab, tab, tab],
        out_specs=[pl.BlockSpec((ts, 4 * LANE), lambda i: (i, CB_CQ // 4)),
                   whole(MLA_Q_LORA, W), whole(MLA_KV_LORA, W), whole(MLA_KV_LORA, MLA_HEADS * MLA_V),
                   whole(1, MLA_Q_LORA), whole(1, MLA_KV_LORA), whole(1, LANE), whole(1, LANE)],
        out_shape=[jax.ShapeDtypeStruct(dproj.shape, BF16),
                   jax.ShapeDtypeStruct((MLA_Q_LORA, W), F32), jax.ShapeDtypeStruct((MLA_KV_LORA, W), F32),
                   jax.ShapeDtypeStruct((MLA_KV_LORA, MLA_HEADS * MLA_V), F32),
                   jax.ShapeDtypeStruct((1, MLA_Q_LORA), F32), jax.ShapeDtypeStruct((1, MLA_KV_LORA), F32),
                   jax.ShapeDtypeStruct((1, LANE), F32), jax.ShapeDtypeStruct((1, LANE), F32)],
        scratch_shapes=[pltpu.VMEM((ts, W), BF16), pltpu.VMEM((ts, W), BF16)],
        input_output_aliases={0: 0},
        compiler_params=_cp(),
    )(dproj, dq, dk, dv, proj, gq, gkv, wuqp, wkp, wv, gmq, gmk, cos, sa, sb)


def _dil_prep_fwd(proj, gq, gk):
    T = proj.shape[0]
    ts = _tile(T, 512)

    def body(pq_ref, pk_ref, pv_ref, gq_ref, gk_ref, q_ref, k_ref, v_ref):
        v_ref[...] = pv_ref[...].astype(F32)
        for c in range(NPAIR):
            cs = slice(c * LANE, (c + 1) * LANE)
            t = jnp.concatenate([pq_ref[:, cs], pk_ref[:, cs]], axis=1).astype(F32)
            y = t * lax.rsqrt(_head_bcast_sum(t * t, terms=2) * (1.0 / DIL_HEAD_DIM) + EPS)
            q_ref[:, cs] = y[:, 0:LANE] * gq_ref[:, cs]
            k_ref[:, cs] = y[:, LANE:2 * LANE] * gk_ref[:, cs]

    col = pl.BlockSpec((1, DIL_WIDTH), lambda i, g: (0, g))
    out = pl.BlockSpec((ts, DIL_WIDTH), lambda i, g: (i, g))
    seg = lambda c0: pl.BlockSpec((ts, DIL_WIDTH), lambda i, g: (i, c0 // NPAIR + g))
    return pl.pallas_call(
        body, name="dil_prep_fwd", grid=(T // ts, DIL_GROUPS),
        in_specs=[seg(CB_DQ), seg(CB_DK), seg(CB_DV), col, col],
        out_specs=[out, out, out],
        out_shape=[jax.ShapeDtypeStruct((T, DIL_QK), F32)] * 3,
        compiler_params=_cp(),
    )(proj, proj, proj, gq, gk)


def _dil_prep_bwd(dproj, ddq, ddk, ddv, proj, gq, gk):
    T = proj.shape[0]
    ts = _tile(T, 512)
    nt = T // ts

    def body(dpin_ref, ddq_ref, ddk_ref, ddv_ref, pq_ref, pk_ref, gq_ref, gk_ref, dp_ref, dgq_ref, dgk_ref,
             stage, sems):
        del dpin_ref
        g, i = pl.program_id(0), pl.program_id(1)

        @pl.when(i == 0)
        def _():
            dgq_ref[...] = jnp.zeros_like(dgq_ref)
            dgk_ref[...] = jnp.zeros_like(dgk_ref)

        def fill(slot):
            stage[slot, 2] = ddv_ref[...].astype(BF16)
            for c in range(NPAIR):
                cs = slice(c * LANE, (c + 1) * LANE)
                t = jnp.concatenate([pq_ref[:, cs], pk_ref[:, cs]], axis=1).astype(F32)
                d = jnp.concatenate([ddq_ref[:, cs], ddk_ref[:, cs]], axis=1)
                gy = d * jnp.concatenate([gq_ref[:, cs], gk_ref[:, cs]], axis=1)
                r = lax.rsqrt(_head_bcast_sum(t * t, terms=2) * (1.0 / DIL_HEAD_DIM) + EPS)
                dot = _head_bcast_sum(t * gy, terms=2) * (1.0 / DIL_HEAD_DIM)
                dx = (r * gy - t * (r * r * r) * dot).astype(BF16)
                stage[slot, 0, :, cs] = dx[:, 0:LANE]
                stage[slot, 1, :, cs] = dx[:, LANE:2 * LANE]
                part = jnp.sum(d * t * r, axis=0, keepdims=True)
                dgq_ref[:, cs] += part[:, 0:LANE]
                dgk_ref[:, cs] += part[:, LANE:2 * LANE]

        def copies_of(step):
            sg, si = step // nt, step % nt
            return _put_copies([stage.at[:, k] for k in range(3)], dp_ref, sems, step % 2,
                               pl.ds(pl.multiple_of(si * ts, ts), ts),
                               [pl.multiple_of((c0 + NPAIR * sg) * LANE, LANE) for c0 in (CB_DQ, CB_DK, CB_DV)])

        _put_pipeline(g * nt + i, DIL_GROUPS * nt, copies_of, fill)

    col = pl.BlockSpec((1, DIL_WIDTH), lambda g, i: (0, g))
    tok = pl.BlockSpec((ts, DIL_WIDTH), lambda g, i: (i, g))
    seg = lambda c0: pl.BlockSpec((ts, DIL_WIDTH), lambda g, i: (i, c0 // NPAIR + g))
    return pl.pallas_call(
        body, name="dil_prep_bwd", grid=(DIL_GROUPS, nt),
        in_specs=[pl.BlockSpec(memory_space=pl.ANY), tok, tok, tok, seg(CB_DQ), seg(CB_DK), col, col],
        out_specs=[pl.BlockSpec(memory_space=pl.ANY), col, col],
        out_shape=[jax.ShapeDtypeStruct(dproj.shape, BF16), jax.ShapeDtypeStruct((1, DIL_QK), F32),
                   jax.ShapeDtypeStruct((1, DIL_QK), F32)],
        scratch_shapes=[pltpu.VMEM((2, 3, ts, DIL_WIDTH), BF16), pltpu.SemaphoreType.DMA((2, 3))],
        input_output_aliases={0: 0},
        compiler_params=_cp(),
    )(dproj, ddq, ddk, ddv, proj, proj, gq, gk)


COPY_ROWS = 256


def _to_classes(src_ref, dst_ref, d, L, scale=None):
    n = min(L, COPY_ROWS)
    for r in range(d):
        for c0 in range(0, L, n):
            rows = pl.ds(r + c0 * d, n, stride=d) if d > 1 else pl.ds(c0, n)
            val = src_ref[rows, :]
            if scale is not None:
                val = val * scale
            dst_ref[r * L + c0:r * L + c0 + n, :] = val.astype(dst_ref.dtype)


def _from_classes(src_ref, dst_ref, d, L):
    n = min(L, COPY_ROWS)
    for r in range(d):
        for c0 in range(0, L, n):
            rows = pl.ds(r + c0 * d, n, stride=d) if d > 1 else pl.ds(c0, n)
            dst_ref[rows, :] = src_ref[r * L + c0:r * L + c0 + n, :].astype(dst_ref.dtype)


MLA_TQ, MLA_TK = 512, 512


def _causal_bias(tq, tk, shift):
    row = lax.broadcasted_iota(jnp.int32, (tq, tk), 0)
    col = lax.broadcasted_iota(jnp.int32, (tq, tk), 1)
    return jnp.where(row >= col + shift, 0.0, NEG)


def _mla_specs(S):
    heads = pl.BlockSpec((2, S, LANE), lambda b, j: (j, b, 0))
    pair = pl.BlockSpec((S, LANE), lambda b, j: (b, j))
    return heads, pair


def _mla_attn_fwd(q, k, v, B, S):
    tq = _tile(S, MLA_TQ)
    tk = _tile(tq, MLA_TK)
    nd = tq // tk
    heads, pair = _mla_specs(S)

    def body(q_ref, k_ref, v_ref, o_ref, lse_ref):
        lo, lok = _lane_lo((tq, LANE)), _lane_lo((tk, LANE))
        diag = [_causal_bias(tq, tk, i * tk) for i in range(nd)]

        def block(g, _):
            row0 = pl.multiple_of(g * tq, tq)
            rows = pl.ds(row0, tq)
            qs = [q_ref[hh, rows, :] for hh in range(2)]

            one = jnp.ones((), BF16)

            def step(off, carries, bias):
                off = pl.multiple_of(off, tk)
                vt = v_ref[pl.ds(off, tk), :]
                vh = (jnp.where(lok, vt, one), jnp.where(lok, one, vt))
                out = []
                for hh, (m, acc) in enumerate(carries):
                    s = _mm_nt(qs[hh], k_ref[hh, pl.ds(off, tk), :])
                    if bias is not None:
                        s = s + bias
                    m_new = jnp.maximum(m, jnp.max(s, axis=-1, keepdims=True))
                    p = jnp.exp(s - m_new)
                    out.append((m_new, jnp.exp(m - m_new) * acc + _mm(p, vh[hh])))
                return tuple(out)

            init = (jnp.full((tq, 1), NEG, F32), jnp.zeros((tq, LANE), F32))
            carries = lax.fori_loop(0, g * nd, lambda i, c: step(i * tk, c, None), (init, init))
            for i in range(nd):
                carries = step(row0 + i * tk, carries, diag[i])
            (ma, acca), (mb, accb) = carries
            la, lb = pltpu.roll(acca, HALF, axis=1), pltpu.roll(accb, HALF, axis=1)
            o_ref[rows, :] = jnp.where(lo, acca / la, accb / lb)
            lse_ref[rows, :] = jnp.where(lo, ma + jnp.log(la), mb + jnp.log(lb))
            return 0

        lax.fori_loop(0, S // tq, block, 0)

    return pl.pallas_call(
        body, name="mla_attn_fwd", grid=(B, NPAIR), in_specs=[heads, heads, pair], out_specs=[pair, pair],
        out_shape=[jax.ShapeDtypeStruct((B * S, MLA_HEADS * MLA_V), F32)] * 2,
        compiler_params=_cp(),
    )(q, k, v)


DIL_UNROLL = 16


def _dil_geometry(gi, S):
    span, d = DIL_PATTERNS[gi]
    L = S // d
    t = _tile(L, 128)
    window = span // d
    back = min(-(-window // t) * t, L - t)
    return d, L, t, window, back


def _dil_specs(gi, S):
    qk = pl.BlockSpec((S, LANE), lambda b, j: (b, NPAIR * gi + j))
    pair = pl.BlockSpec((S, LANE), lambda b, j: (b, j))
    return qk, qk, pair


def _dil_bias(bias_ref, sl_ref, j, t, kw, back, window):
    row = lax.broadcasted_iota(jnp.int32, (2 * t, kw), 0)
    col = lax.broadcasted_iota(jnp.int32, (2 * t, kw), 1)
    second = row >= t
    slope = jnp.where(second, sl_ref[j, 1], sl_ref[j, 0])
    for n in range(bias_ref.shape[0]):
        dist = jnp.where(second, row - t, row) + n * back - col
        bias_ref[n] = jnp.where((dist >= 0) & (dist <= window), -slope * dist.astype(F32), NEG)


def _stack_heads(x, lo):
    zero = jnp.zeros((), x.dtype)
    return jnp.concatenate([jnp.where(lo, x, zero), jnp.where(lo, zero, x)], axis=0)


def _dil_attn_fwd(gi, slopes, qn, kn, proj, B, S):
    d, L, t, window, back = _dil_geometry(gi, S)
    kw, nq = back + t, L // t
    nbias = 2 if back else 1
    qk, vspec, pair = _dil_specs(gi, S)

    def body(sl_ref, q_ref, k_ref, v_ref, o_ref, lse_ref, qs, ks, vs, os_, ls, bias_ref):
        _to_classes(q_ref, qs, d, L, DIL_HEAD_DIM ** -0.5)
        _to_classes(k_ref, ks, d, L)
        _to_classes(v_ref, vs, d, L)
        _dil_bias(bias_ref, sl_ref, pl.program_id(1), t, kw, back, window)
        lo = _lane_lo((t, LANE))

        def block(g, _):
            qb = g % nq if d > 1 else g
            row0 = pl.multiple_of(g * t, t)
            rows = pl.ds(row0, t)
            early = qb * t < back
            keys = pl.ds(pl.multiple_of(jnp.where(early, row0 - qb * t, row0 - back), t), kw)
            s = _mm_nt(_stack_heads(qs[rows, :], lo), ks[keys, :]) + bias_ref[jnp.where(early, 0, nbias - 1)]
            m = jnp.max(s, axis=-1, keepdims=True)
            p = jnp.exp(s - m)
            l = jnp.sum(p, axis=-1, keepdims=True)
            o2 = _mm(p, vs[keys, :]) / l
            lse2 = m + jnp.log(l)
            os_[rows, :] = jnp.where(lo, o2[:t], o2[t:])
            ls[rows, :] = jnp.where(lo, lse2[:t], lse2[t:])
            return 0

        lax.fori_loop(0, d * nq, block, 0, unroll=DIL_UNROLL if d * nq % DIL_UNROLL == 0 else 1)
        _from_classes(os_, o_ref, d, L)
        _from_classes(ls, lse_ref, d, L)

    return pl.pallas_call(
        body, name=f"dil_attn_fwd_{gi}", grid=(B, NPAIR),
        in_specs=[pl.BlockSpec(memory_space=pltpu.SMEM), qk, qk, vspec], out_specs=[pair, pair],
        out_shape=[jax.ShapeDtypeStruct((B * S, DIL_WIDTH), F32)] * 2,
        scratch_shapes=[pltpu.VMEM((S, LANE), BF16)] * 3 + [pltpu.VMEM((S, LANE), F32)] * 2
                       + [pltpu.VMEM((nbias, 2 * t, kw), F32)],
        compiler_params=_cp(),
    )(slopes, qn, kn, proj)


def _mla_attn_bwd(q, k, v, do, lse, delta, B, S):
    T = B * S
    tq = _tile(S, MLA_TQ)
    tk = _tile(tq, MLA_TK)
    nd = tq // tk
    scale = MLA_SCALE
    heads, pair = _mla_specs(S)

    def body(q_ref, k_ref, v_ref, do_ref, lse_ref, dl_ref, dq_ref, dk_ref, dv_ref):
        dk_ref[...] = jnp.zeros_like(dk_ref)
        dv_ref[...] = jnp.zeros_like(dv_ref)
        lo = _lane_lo((tq, LANE))
        diag = [_causal_bias(tq, tk, i * tk) for i in range(nd)]

        def block(g, _):
            row0 = pl.multiple_of(g * tq, tq)
            rows = pl.ds(row0, tq)
            per_head = []
            for hh in range(2):
                sel = lo if hh == 0 else jnp.logical_not(lo)
                per_head.append((q_ref[hh, rows, :], jnp.where(sel, do_ref[rows, :], jnp.zeros((), BF16)),
                                 jnp.max(jnp.where(sel, lse_ref[rows, :], NEG), axis=-1, keepdims=True),
                                 jnp.max(jnp.where(sel, dl_ref[rows, :], NEG), axis=-1, keepdims=True)))

            def step(off, dq_accs, bias):
                cols = pl.ds(pl.multiple_of(off, tk), tk)
                vt = v_ref[cols, :]
                out, dv = [], None
                for hh, (qh, doh, lse_h, dl_h) in enumerate(per_head):
                    kh = k_ref[hh, cols, :]
                    s = _mm_nt(qh, kh)
                    if bias is not None:
                        s = s + bias
                    p = jnp.exp(s - lse_h)
                    ds = (p * (_mm_nt(doh, vt) - dl_h)).astype(BF16)
                    dk_ref[hh, cols, :] += _mm_tn(ds, qh)
                    part = _mm_tn(p, doh)
                    dv = part if dv is None else dv + part
                    out.append(dq_accs[hh] + _mm(ds, kh))
                dv_ref[cols, :] += dv
                return tuple(out)

            zero = jnp.zeros((tq, LANE), F32)
            dq_accs = lax.fori_loop(0, g * nd, lambda i, a: step(i * tk, a, None), (zero, zero))
            for i in range(nd):
                dq_accs = step(row0 + i * tk, dq_accs, diag[i])
            for hh in range(2):
                dq_ref[hh, rows, :] = dq_accs[hh] * scale
            return 0

        lax.fori_loop(0, S // tq, block, 0)

    return pl.pallas_call(
        body, name="mla_attn_bwd", grid=(B, NPAIR), in_specs=[heads, heads, pair, pair, pair, pair],
        out_specs=[heads, heads, pair],
        out_shape=[jax.ShapeDtypeStruct((MLA_HEADS, T, LANE), F32), jax.ShapeDtypeStruct((MLA_HEADS, T, LANE), F32),
                   jax.ShapeDtypeStruct((T, MLA_HEADS * MLA_V), F32)],
        compiler_params=_cp(),
    )(q, k, v, do, lse, delta)


def _dil_attn_bwd(gi, slopes, qn, kn, proj, do, lse, delta, through, B, S):
    d, L, t, window, back = _dil_geometry(gi, S)
    kw, nq = back + t, L // t
    nbias = 2 if back else 1
    scale = DIL_HEAD_DIM ** -0.5
    qk, vspec, pair = _dil_specs(gi, S)

    def body(*refs):
        refs = list(refs)
        sl_ref, q_ref, k_ref, v_ref, do_ref, lse_ref, dl_ref = refs[:7]
        dq_ref, dk_ref, dv_ref, qs, ks, vs, dos, lss, dls, dqs, dks, dvs, bias_ref = refs[-13:]
        _to_classes(q_ref, qs, d, L, scale)
        for src, dst in ((k_ref, ks), (v_ref, vs), (do_ref, dos), (lse_ref, lss), (dl_ref, dls)):
            _to_classes(src, dst, d, L)
        _dil_bias(bias_ref, sl_ref, pl.program_id(1), t, kw, back, window)
        dks[...] = jnp.zeros_like(dks)
        dvs[...] = jnp.zeros_like(dvs)
        lo = _lane_lo((t, LANE))

        def stats(ref, rows):
            x = ref[rows, :]
            return jnp.concatenate([jnp.max(jnp.where(lo, x, NEG), axis=-1, keepdims=True),
                                    jnp.max(jnp.where(lo, NEG, x), axis=-1, keepdims=True)], axis=0)

        def block(g, _):
            qb = g % nq if d > 1 else g
            row0 = pl.multiple_of(g * t, t)
            rows = pl.ds(row0, t)
            early = qb * t < back
            keys = pl.ds(pl.multiple_of(jnp.where(early, row0 - qb * t, row0 - back), t), kw)
            q2 = _stack_heads(qs[rows, :], lo)
            do2 = _stack_heads(dos[rows, :], lo)
            kt = ks[keys, :]
            s = _mm_nt(q2, kt) + bias_ref[jnp.where(early, 0, nbias - 1)]
            p = jnp.exp(s - stats(lss, rows))
            ds = (p * (_mm_nt(do2, vs[keys, :]) - stats(dls, rows))).astype(BF16)
            dq2 = _mm(ds, kt) * scale
            dqs[rows, :] = jnp.where(lo, dq2[:t], dq2[t:])
            dks[keys, :] += _mm_tn(ds, q2)
            dvs[keys, :] += _mm_tn(p, do2)
            return 0

        lax.fori_loop(0, d * nq, block, 0, unroll=DIL_UNROLL if d * nq % DIL_UNROLL == 0 else 1)
        for src, dst in ((dqs, dq_ref), (dks, dk_ref), (dvs, dv_ref)):
            _from_classes(src, dst, d, L)

    in_specs = [pl.BlockSpec(memory_space=pltpu.SMEM), qk, qk, vspec, pair, pair, pair]
    args = [slopes, qn, kn, proj, do, lse, delta]
    aliases = {}
    if through is not None:
        aliases = {len(args) + i: i for i in range(3)}
        in_specs = in_specs + [pl.BlockSpec(memory_space=pl.ANY)] * 3
        args = args + list(through)
    return pl.pallas_call(
        body, name=f"dil_attn_bwd_{gi}", grid=(B, NPAIR), in_specs=in_specs, out_specs=[qk, qk, qk],
        out_shape=[jax.ShapeDtypeStruct((B * S, DIL_QK), F32)] * 3,
        scratch_shapes=[pltpu.VMEM((S, LANE), BF16)] * 4 + [pltpu.VMEM((S, LANE), F32)] * 5
                       + [pltpu.VMEM((nbias, 2 * t, kw), F32)],
        input_output_aliases=aliases,
        compiler_params=_cp(),
    )(*args)


def _merge_proj_specs(ts):
    wide = lambda c0, w: pl.BlockSpec((ts, w), lambda i: (i, c0 * LANE // w))
    return [wide(CB_BZ, DIL_WIDTH), wide(CB_CZ, DIL_WIDTH)] + [wide(CB_GATE + 8 * i, D_MODEL) for i in range(3)]


def _merge_common(p_refs, bg_ref, ob_ref, og_refs, lse_refs):
    bz = p_refs[0][...].astype(F32)
    cz = p_refs[1][...].astype(F32)
    gates = [_sigmoid(p_refs[2 + i][...].astype(F32) + bg_ref[:, i * D_MODEL:(i + 1) * D_MODEL]) for i in range(3)]
    ob = ob_ref[...]
    lses = [r[...] for r in lse_refs]
    mx = jnp.maximum(jnp.maximum(lses[0], lses[1]), lses[2])
    es = [jnp.exp(v - mx) for v in lses]
    inv = 1.0 / (es[0] + es[1] + es[2])
    alphas = [e * inv for e in es]
    oc = alphas[0] * og_refs[0][...] + alphas[1] * og_refs[1][...] + alphas[2] * og_refs[2][...]
    return bz, cz, gates, ob, alphas, oc


def _merge_fwd(x, proj, b_gate, ya, ob, ogs, lses, woa, wob, woc, wo):
    T = x.shape[0]
    ts = _tile(T, 256)

    def body(x_ref, p0, p1, p2, p3, p4, bg_ref, ya_ref, ob_ref, og0, og1, og2, l0, l1, l2,
             woa_ref, wob_ref, woc_ref, wo_ref, out_ref):
        bz, cz, gates, obv, alphas, oc = _merge_common((p0, p1, p2, p3, p4), bg_ref, ob_ref, (og0, og1, og2),
                                                       (l0, l1, l2))
        yb = obv * _silu(bz)
        yc = oc * _silu(cz)
        merged = (gates[0] * _mm(ya_ref[...], woa_ref[...]) + gates[1] * _mm(yb, wob_ref[...])
                  + gates[2] * _mm(yc, woc_ref[...]))
        out_ref[...] = x_ref[...] + _mm(merged, wo_ref[...])

    def whole(r, c):
        return pl.BlockSpec((r, c), lambda i: (0, 0))

    tok = lambda w: pl.BlockSpec((ts, w), lambda i: (i, 0))
    return pl.pallas_call(
        body, name="merge_fwd", grid=(T // ts,),
        in_specs=[tok(D_MODEL)] + _merge_proj_specs(ts) + [whole(1, 3 * D_MODEL), tok(CONV_WIDTH)]
                 + [tok(DIL_WIDTH)] * 7 + [whole(CONV_WIDTH, D_MODEL)] * 3 + [whole(D_MODEL, D_MODEL)],
        out_specs=tok(D_MODEL),
        out_shape=jax.ShapeDtypeStruct((T, D_MODEL), F32),
        compiler_params=_cp(),
    )(x, *[proj] * 5, b_gate, ya, ob, *ogs, *lses, woa, wob, woc, wo)


def _merge_bwd(dout, proj, b_gate, ya, ob, ogs, lses, woa, wob, woc, wo):
    T = dout.shape[0]
    ts = _tile(T, 256)
    nt = T // ts

    def body(do_ref, p0, p1, p2, p3, p4, bg_ref, ya_ref, ob_ref, og0, og1, og2, l0, l1, l2,
             woa_ref, wob_ref, woc_ref, wo_ref,
             dp_ref, dya_ref, dob_ref, dlb_ref, dg0, dg1, dg2, dl0, dl1, dl2,
             mg_ref, dpa_ref, dpb_ref, dpc_ref, yb_ref, yc_ref, dbg_ref, st_bz, st_cz, st_gate, sems):
        step = pl.program_id(0)
        slot = step % 2

        def copies_of(s):
            return _put_copies([st_bz, st_cz, st_gate], dp_ref, sems, s % 2, pl.ds(pl.multiple_of(s * ts, ts), ts),
                               [CB_BZ * LANE, CB_CZ * LANE, CB_GATE * LANE])

        @pl.when(step >= 2)
        def _():
            for cp in copies_of(step - 2):
                cp.wait()

        bz, cz, gates, obv, alphas, oc = _merge_common((p0, p1, p2, p3, p4), bg_ref, ob_ref, (og0, og1, og2),
                                                       (l0, l1, l2))
        sb, sc = _silu(bz), _silu(cz)
        yb = obv * sb
        yc = oc * sc
        ps = [_mm(ya_ref[...], woa_ref[...]), _mm(yb, wob_ref[...]), _mm(yc, woc_ref[...])]
        mg_ref[...] = (gates[0] * ps[0] + gates[1] * ps[1] + gates[2] * ps[2]).astype(BF16)
        yb_ref[...] = yb.astype(BF16)
        yc_ref[...] = yc.astype(BF16)
        dm = _mm_nt(do_ref[...], wo_ref[...])
        dps = []
        first = pl.program_id(0) == 0
        for i, dref in enumerate((dpa_ref, dpb_ref, dpc_ref)):
            g = gates[i]
            dpi = (dm * g).astype(BF16)
            dref[...] = dpi
            dps.append(dpi)
            dgp = dm * ps[i] * g * (1.0 - g)
            st_gate[slot, :, i * D_MODEL:(i + 1) * D_MODEL] = dgp.astype(BF16)
            part = jnp.sum(dgp, axis=0, keepdims=True)

            @pl.when(first)
            def _():
                dbg_ref[:, i * D_MODEL:(i + 1) * D_MODEL] = part

            @pl.when(jnp.logical_not(first))
            def _():
                dbg_ref[:, i * D_MODEL:(i + 1) * D_MODEL] += part

        dya_ref[...] = _mm_nt(dps[0], woa_ref[...])
        dyb = _mm_nt(dps[1], wob_ref[...])
        dyc = _mm_nt(dps[2], woc_ref[...])
        st_bz[slot] = (dyb * obv * _dsilu(bz)).astype(BF16)
        st_cz[slot] = (dyc * oc * _dsilu(cz)).astype(BF16)
        for cp in copies_of(step):
            cp.start()
        dob = dyb * sb
        doc = dyc * sc
        dob_ref[...] = dob.astype(BF16)
        for c in range(NPAIR):
            cs = slice(c * LANE, (c + 1) * LANE)
            dlb_ref[:, cs] = _head_bcast_sum(dob[:, cs] * obv[:, cs])
            dd = _head_bcast_sum(doc[:, cs] * oc[:, cs])
            for a, dref, lref in zip(alphas, (dg0, dg1, dg2), (dl0, dl1, dl2)):
                dref[:, cs] = a[:, cs] * doc[:, cs]
                lref[:, cs] = a[:, cs] * dd

        @pl.when(step == nt - 1)
        def _():
            if nt >= 2:
                for cp in copies_of(step - 1):
                    cp.wait()
            for cp in copies_of(step):
                cp.wait()

    def whole(r, c):
        return pl.BlockSpec((r, c), lambda i: (0, 0))

    tok = lambda w: pl.BlockSpec((ts, w), lambda i: (i, 0))
    sd = jax.ShapeDtypeStruct
    W = DIL_WIDTH
    return pl.pallas_call(
        body, name="merge_bwd", grid=(nt,),
        in_specs=[tok(D_MODEL)] + _merge_proj_specs(ts) + [whole(1, 3 * D_MODEL), tok(CONV_WIDTH)] + [tok(W)] * 7
                 + [whole(CONV_WIDTH, D_MODEL)] * 3 + [whole(D_MODEL, D_MODEL)],
        out_specs=[pl.BlockSpec(memory_space=pl.ANY), tok(CONV_WIDTH), tok(W), tok(W)] + [tok(W)] * 6
                  + [tok(D_MODEL)] * 4 + [tok(W), tok(W), whole(1, 3 * D_MODEL)],
        out_shape=[sd((T, PP), BF16), sd((T, CONV_WIDTH), F32), sd((T, W), BF16), sd((T, W), F32)]
                  + [sd((T, W), F32)] * 6
                  + [sd((T, D_MODEL), BF16)] * 4 + [sd((T, W), BF16)] * 2 + [sd((1, 3 * D_MODEL), F32)],
        scratch_shapes=[pltpu.VMEM((2, ts, W), BF16), pltpu.VMEM((2, ts, W), BF16),
                        pltpu.VMEM((2, ts, 3 * D_MODEL), BF16), pltpu.SemaphoreType.DMA((2, 3))],
        compiler_params=_cp(),
    )(dout, *[proj] * 5, b_gate, ya, ob, *ogs, *lses, woa, wob, woc, wo)


def _loss_head(y, target):
    T = y.shape[0]
    ts = _tile(T, 512)

    def body(y_ref, t_ref, d_ref, l_ref):
        e = y_ref[...] - t_ref[...]
        d_ref[...] = e * (1.0 / D_MODEL)
        l_ref[...] = jnp.zeros((1, 8, LANE), F32) + jnp.sum(e * e)

    tok = pl.BlockSpec((ts, D_MODEL), lambda i: (i, 0))
    return pl.pallas_call(
        body, name="loss_head", grid=(T // ts,), in_specs=[tok, tok],
        out_specs=[tok, pl.BlockSpec((1, 8, LANE), lambda i: (i, 0, 0))],
        out_shape=[jax.ShapeDtypeStruct((T, D_MODEL), F32), jax.ShapeDtypeStruct((T // ts, 8, LANE), F32)],
        compiler_params=_cp(),
    )(y, target)


def _my_index():
    return 4 * lax.axis_index("x") + 2 * lax.axis_index("y") + lax.axis_index("c")


def _peers():
    x, y, c = (lax.axis_index(a) for a in AXES)
    out = []
    for kk in range(1, N_DEV):
        px = 1 - x if kk & 4 else x
        py = 1 - y if kk & 2 else y
        pc = 1 - c if kk & 1 else c
        out.append(((px, py, pc), 4 * px + 2 * py + pc))
    return out


def _exchange(arrays, name, gather):
    n = len(arrays)

    def body(*refs):
        srcs, outs = refs[:n], refs[n:2 * n]
        send_sems, recv_sems, local_sems = refs[2 * n:]
        me = _my_index()
        peers = _peers()
        started = []
        for a, (src, out) in enumerate(zip(srcs, outs)):
            mine = pltpu.make_async_copy(src if gather else src.at[me], out.at[me], local_sems.at[a])
            mine.start()
            started.append(mine)
        sends = []
        for i, (pos, idx) in enumerate(peers):
            for a, (src, out) in enumerate(zip(srcs, outs)):
                cp = pltpu.make_async_remote_copy(
                    src_ref=src if gather else src.at[idx], dst_ref=out.at[me], send_sem=send_sems.at[a, i],
                    recv_sem=recv_sems.at[a, i], device_id=pos, device_id_type=pl.DeviceIdType.MESH)
                cp.start()
                sends.append(cp)
        for i, (pos, idx) in enumerate(peers):
            for a, (src, out) in enumerate(zip(srcs, outs)):
                pltpu.make_async_remote_copy(
                    src_ref=src if gather else src.at[idx], dst_ref=out.at[idx], send_sem=send_sems.at[a, i],
                    recv_sem=recv_sems.at[a, i], device_id=pos, device_id_type=pl.DeviceIdType.MESH).wait_recv()
        for cp in sends:
            cp.wait_send()
        for mine in started:
            mine.wait()

    any_space = pl.BlockSpec(memory_space=pl.ANY)
    return pl.pallas_call(
        body, name=name, in_specs=[any_space] * n, out_specs=[any_space] * n,
        out_shape=[jax.ShapeDtypeStruct(((N_DEV,) + a.shape) if gather else a.shape, a.dtype) for a in arrays],
        scratch_shapes=[pltpu.SemaphoreType.DMA((n, N_DEV - 1)), pltpu.SemaphoreType.DMA((n, N_DEV - 1)),
                        pltpu.SemaphoreType.DMA((n,))],
    )(*arrays)


N_CHIP = 4


def _chip_places():
    x, y, c = (lax.axis_index(a) for a in AXES)
    return (x, y, c), (x, y, 1 - c), [(1 - x, y, c), (x, 1 - y, c), (1 - x, 1 - y, c)]


def _index_of(pos):
    return 4 * pos[0] + 2 * pos[1] + pos[2]


def _gather_two_level(arrays, name):
    n = len(arrays)

    def body(*refs):
        srcs, outs = refs[:n], refs[n:2 * n]
        send_sems, recv_sems, local_sems = refs[2 * n:]
        me, sibling, others = _chip_places()

        def copy(a, k, block, to, src=None):
            slot = outs[a].at[_index_of(block)]
            return pltpu.make_async_remote_copy(
                src_ref=slot if src is None else src, dst_ref=slot, send_sem=send_sems.at[7 * a + k],
                recv_sem=recv_sems.at[7 * a + k], device_id=to, device_id_type=pl.DeviceIdType.MESH)

        started = []
        for a, src in enumerate(srcs):
            mine = pltpu.make_async_copy(src, outs[a].at[_index_of(me)], local_sems.at[a])
            mine.start()
            started.append(mine)
        sends = []
        for a, src in enumerate(srcs):
            sends.append(copy(a, 0, me, sibling, src))
            sends += [copy(a, 1 + j, me, chip, src) for j, chip in enumerate(others)]
        for cp in sends:
            cp.start()
        for j, chip in enumerate(others):
            for a in range(n):
                copy(a, 1 + j, chip, me).wait_recv()
                fwd = copy(a, 4 + j, chip, sibling)
                fwd.start()
                sends.append(fwd)
        for a in range(n):
            copy(a, 0, sibling, me).wait_recv()
            for j, chip in enumerate(others):
                copy(a, 4 + j, (chip[0], chip[1], sibling[2]), me).wait_recv()
        for cp in sends:
            cp.wait_send()
        for mine in started:
            mine.wait()

    any_space = pl.BlockSpec(memory_space=pl.ANY)
    return pl.pallas_call(
        body, name=name, in_specs=[any_space] * n, out_specs=[any_space] * n,
        out_shape=[jax.ShapeDtypeStruct((N_DEV,) + a.shape, a.dtype) for a in arrays],
        scratch_shapes=[pltpu.SemaphoreType.DMA((7 * n,)), pltpu.SemaphoreType.DMA((7 * n,)),
                        pltpu.SemaphoreType.DMA((n,))],
    )(*arrays)


def _sibling_swap(arrays, name):
    n = len(arrays)

    def body(*refs):
        srcs, outs = refs[:n], refs[n:2 * n]
        send_sems, recv_sems = refs[2 * n:]
        (x, y, c), sibling, _ = _chip_places()
        sends = []
        for a, (src, out) in enumerate(zip(srcs, outs)):
            for q in range(N_CHIP):
                def copy(core, a=a, q=q, src=src, out=out):
                    return pltpu.make_async_remote_copy(
                        src_ref=src.at[2 * q + core], dst_ref=out.at[q], send_sem=send_sems.at[N_CHIP * a + q],
                        recv_sem=recv_sems.at[N_CHIP * a + q], device_id=sibling, device_id_type=pl.DeviceIdType.MESH)
                mine = copy(1 - c)
                mine.start()
                sends.append((mine, copy(c)))
        for mine, arrival in sends:
            arrival.wait_recv()
            mine.wait_send()

    any_space = pl.BlockSpec(memory_space=pl.ANY)
    return pl.pallas_call(
        body, name=name, in_specs=[any_space] * n, out_specs=[any_space] * n,
        out_shape=[jax.ShapeDtypeStruct((N_CHIP,) + a.shape[1:], a.dtype) for a in arrays],
        scratch_shapes=[pltpu.SemaphoreType.DMA((N_CHIP * n,)), pltpu.SemaphoreType.DMA((N_CHIP * n,))],
    )(*arrays)


def _chip_pair_sum(part, got, name):
    R, C = part.shape[1:]
    tr = R
    while tr * C * part.dtype.itemsize > REDUCE_BLOCK_BYTES // 4 and tr % 32 == 0:
        tr //= 2
    c = lax.axis_index("c")

    def body(c_ref, p_ref, g_ref, o_ref):
        del c_ref
        o_ref[...] = (p_ref[...].astype(F32) + g_ref[...].astype(F32)).astype(o_ref.dtype)

    return pl.pallas_call(
        body, name=name, grid_spec=pltpu.PrefetchScalarGridSpec(
            num_scalar_prefetch=1, grid=(N_CHIP, R // tr),
            in_specs=[pl.BlockSpec((None, tr, C), lambda q, i, cr: (2 * q + cr[0], i, 0)),
                      pl.BlockSpec((None, tr, C), lambda q, i, cr: (q, i, 0))],
            out_specs=pl.BlockSpec((None, tr, C), lambda q, i, cr: (q, i, 0))),
        out_shape=jax.ShapeDtypeStruct((N_CHIP, R, C), part.dtype),
        compiler_params=_cp(),
    )(jnp.reshape(c, (1,)).astype(jnp.int32), part, got)


def _peer_count(mode):
    return N_CHIP - 1 if mode == "chips" else N_DEV - 1


def _remote_copies(srcs, lands, send_sems, recv_sems, mode):
    if mode == "chips":
        (x, y, _), _, others = _chip_places()
        my_slot, peers = 2 * x + y, [(chip, 2 * chip[0] + chip[1]) for chip in others]
    else:
        my_slot, peers = _my_index(), _peers()
    out = []
    for i, (pos, idx) in enumerate(peers):
        for a, (src, land) in enumerate(zip(srcs, lands)):
            def copy(slot, a=a, src=src, land=land, i=i, pos=pos, idx=idx):
                return pltpu.make_async_remote_copy(
                    src_ref=src if mode == "gather" else src.at[idx], dst_ref=land.at[slot],
                    send_sem=send_sems.at[a * len(peers) + i], recv_sem=recv_sems.at[a * len(peers) + i],
                    device_id=pos, device_id_type=pl.DeviceIdType.MESH)
            out.append((copy(my_slot), copy(idx)))
    return out


def _exchange_start(arrays, name, mode):
    n = len(arrays)
    hbm = pl.BlockSpec(memory_space=pltpu.HBM)
    sem = pl.BlockSpec(memory_space=pltpu.SEMAPHORE)
    lands = [lax.empty(((N_DEV,) + a.shape) if mode == "gather" else a.shape, a.dtype) for a in arrays]

    def body(*refs):
        srcs, lands_ = refs[:n], refs[n:2 * n]
        send_sems, recv_sems = refs[2 * n:2 * n + 2]
        for mine, _ in _remote_copies(srcs, lands_, send_sems, recv_sems, mode):
            mine.start()
        refs[-1][...] = jnp.zeros_like(refs[-1])

    sems = pltpu.SemaphoreType.DMA((n * _peer_count(mode),))
    buffers = [pltpu.HBM(a.shape, a.dtype) for a in list(arrays) + lands]
    res = pl.pallas_call(
        body, name=name, in_specs=[hbm] * (2 * n), out_specs=[sem, sem] + [hbm] * (2 * n) + [pl.BlockSpec(memory_space=pltpu.VMEM)],
        out_shape=[sems, sems] + buffers + [jax.ShapeDtypeStruct((8, LANE), F32)],
        input_output_aliases={i: 2 + i for i in range(2 * n)},
        compiler_params=pltpu.CompilerParams(has_side_effects=pltpu.SideEffectType.DATAFLOW_SIDE_EFFECTING),
    )(*[pltpu.with_memory_space_constraint(a, pltpu.HBM) for a in list(arrays) + lands])
    return (res[0], res[1], res[2:2 + n], res[2 + n:2 + 2 * n]), res[-1]


def _exchange_wait(handle, after, name, mode):
    send_sems, recv_sems, srcs, lands = handle
    n = len(srcs)
    hbm = pl.BlockSpec(memory_space=pltpu.HBM)
    sem = pl.BlockSpec(memory_space=pltpu.SEMAPHORE)

    def body(*refs):
        for mine, arrival in _remote_copies(refs[:n], refs[n:2 * n], refs[2 * n], refs[2 * n + 1], mode):
            mine.wait_send()
            arrival.wait_recv()

    res = pl.pallas_call(
        body, name=name, in_specs=[hbm] * (2 * n) + [sem, sem, pl.BlockSpec(memory_space=pl.ANY)],
        out_specs=[hbm] * (2 * n), out_shape=[pltpu.HBM(a.shape, a.dtype) for a in list(srcs) + list(lands)],
        input_output_aliases={i: i for i in range(2 * n)},
        compiler_params=pltpu.CompilerParams(has_side_effects=pltpu.SideEffectType.DATAFLOW_SIDE_EFFECTING),
    )(*srcs, *lands, send_sems, recv_sems, after)
    return res[n:]


def _own_slot(land, mine, slot=None):
    slot = _my_index() if slot is None else slot
    return lax.dynamic_update_slice(land, mine, (slot,) + (0,) * (land.ndim - 1))


def _adamw(w, g, m, v):
    m = ADAM_B1 * m + (1.0 - ADAM_B1) * g
    v = ADAM_B2 * v + (1.0 - ADAM_B2) * (g * g)
    m_hat = m / (1.0 - ADAM_B1 ** ADAM_STEP)
    v_hat = v / (1.0 - ADAM_B2 ** ADAM_STEP)
    delta = -ADAM_LR * (m_hat / (jnp.sqrt(v_hat) + ADAM_EPS) + ADAM_WD * w)
    return delta, m, v


def _reduce_adamw(parts, w, m, v, name):
    nparts = len(parts)
    R, C = parts[0].shape[1:]
    tr = R
    while N_DEV * tr * C * parts[0].dtype.itemsize > REDUCE_BLOCK_BYTES and tr % 32 == 0:
        tr //= 2
    steps = R // tr

    def body(*refs):
        w_ref, m_ref, v_ref, g_ref, d_ref, nm_ref, nv_ref = refs[nparts:]
        for k, p_ref in enumerate(refs[:nparts]):
            @pl.when(pl.program_id(0) // steps == k)
            def _():
                g = p_ref[0].astype(F32)
                for s in range(1, p_ref.shape[0]):
                    g = g + p_ref[s].astype(F32)
                g_ref[...] = g
                d_ref[...], nm_ref[...], nv_ref[...] = _adamw(w_ref[...], g, m_ref[...], v_ref[...])

    def part_spec(k):
        return pl.BlockSpec((parts[k].shape[0], tr, C), lambda i: (0, jnp.clip(i - k * steps, 0, steps - 1), 0))

    row = pl.BlockSpec((tr, C), lambda i: (i, 0))
    return pl.pallas_call(
        body, name=name, grid=(nparts * steps,),
        in_specs=[part_spec(k) for k in range(nparts)] + [row, row, row],
        out_specs=[row] * 4, out_shape=[jax.ShapeDtypeStruct((nparts * R, C), F32)] * 4,
        compiler_params=_cp(),
    )(*parts, w, m, v)


BIG = ("w_in", "w_uq", "w_ukv", "w_out_a", "w_out_b", "w_out_c", "w_o")
SMALL = ("norm_g", "b_gate", "conv_w", "conv_b", "q_a_norm_g", "kv_a_norm_g", "mla_q_norm_g", "mla_k_norm_g",
         "dil_q_norm_g", "dil_k_norm_g")
PACK_ROWS = 128
REDUCE_BLOCK_BYTES = 6 * 1024 * 1024


def _pack_local(tensors):
    flat = jnp.concatenate([t.reshape(-1) for t in tensors])
    pad = (-flat.shape[0]) % (PACK_ROWS * LANE)
    return jnp.concatenate([flat, jnp.zeros((pad,), flat.dtype)]).reshape(-1, LANE)


def _unpack_local(rows, like):
    flat = rows.reshape(-1)
    out, off = [], 0
    for t in like:
        out.append(flat[off:off + t.size].reshape(t.shape))
        off += t.size
    return out


def _cols_to_slots(a):
    k = a.shape[0]
    return a.reshape(k, N_DEV, -1).transpose(1, 0, 2)


def _slots_to_cols(s):
    return s.transpose(1, 0, 2).reshape(s.shape[1], -1)


def _rope_tables(S):
    inv = ROPE_THETA ** (-jnp.arange(0, MLA_ROPE, 2, dtype=F32) / MLA_ROPE)
    ang = jnp.arange(S, dtype=F32)[:, None] * inv[None, :]
    cos, sin = jnp.cos(ang), jnp.sin(ang)
    one = jnp.ones((S, MLA_NOPE), F32)
    z16, z32, z64 = (jnp.zeros((S, n), F32) for n in (16, 32, 64))
    cosp = jnp.concatenate([one, cos, cos, jnp.ones((S, 32), F32)], axis=1)
    sa = jnp.concatenate([z64, -sin, z16, z32], axis=1)
    sb = jnp.concatenate([z64, z16, sin, z32], axis=1)
    return cosp, sa, sb


def _alibi_slopes():
    n = DIL_GROUPS * DIL_HEADS
    m = 2.0 ** (-8.0 * jnp.arange(1, n + 1, dtype=F32) / n)
    return m.reshape(DIL_GROUPS, NPAIR, 2)


def _pad_slots(s):
    n, k, c = s.shape
    return _slots_to_cols(jnp.concatenate([s, jnp.zeros((n, k, LANE - c), s.dtype)], axis=2))


def _layer_params(gw, small, l):
    p = {}
    p["wp"] = _pad_columns(gw["w_in"])
    p["norm_g"] = small["norm_g"][l][None]
    p["b_gate"] = small["b_gate"][l][None]
    p["conv_w"] = gw["conv_w"].transpose(1, 0, 2).reshape(CONV_K, CONV_WIDTH)
    p["conv_b"] = small["conv_b"][l][None]
    p["gq"] = small["q_a_norm_g"][l][None]
    p["gkv"] = small["kv_a_norm_g"][l][None]
    p["wuqp"] = _pad_slots(gw["w_uq"])
    kv = gw["w_ukv"]
    p["wkp"] = _pad_slots(kv[:, :, :MLA_NOPE])
    p["wv"] = kv[:, :, MLA_NOPE:].transpose(1, 0, 2).reshape(MLA_KV_LORA, MLA_HEADS * MLA_V)
    zpad = jnp.zeros((1, LANE - MLA_QK), F32)
    p["gmq"] = jnp.concatenate([small["mla_q_norm_g"][l][None], zpad], axis=1)
    p["gmk"] = jnp.concatenate([small["mla_k_norm_g"][l][None], zpad], axis=1)
    tile = lambda g: jnp.broadcast_to(g[:, None, :], (DIL_GROUPS, DIL_HEADS, DIL_HEAD_DIM)).reshape(1, DIL_QK)
    p["gdq"] = tile(small["dil_q_norm_g"][l])
    p["gdk"] = tile(small["dil_k_norm_g"][l])
    p["woa"], p["wob"], p["woc"] = (_slots_to_cols(gw[n]) for n in ("w_out_a", "w_out_b", "w_out_c"))
    p["wo"] = gw["w_o"].reshape(D_MODEL, D_MODEL)
    return p


def _layer_fwd(x, p, tabs, slopes, B, S):
    proj, ht = _inproj_fwd(x, p["norm_g"], p["wp"])
    ya = _mixa_fwd(proj, p["conv_w"], p["conv_b"], B, S)
    q, k, v = _mla_prep_fwd(proj, p["gq"], p["gkv"], p["wuqp"], p["wkp"], p["wv"], p["gmq"], p["gmk"], *tabs, S)
    ob, lse_b = _mla_attn_fwd(q, k, v, B, S)
    qn, kn, vn = _dil_prep_fwd(proj, p["gdq"], p["gdk"])
    ogs, lses = [], []
    for gi in range(DIL_GROUPS):
        o, lse = _dil_attn_fwd(gi, slopes[gi], qn, kn, vn, B, S)
        ogs.append(o)
        lses.append(lse)
    out = _merge_fwd(x, proj, p["b_gate"], ya, ob, ogs, lses, p["woa"], p["wob"], p["woc"], p["wo"])
    saved = dict(x=x, proj=proj, ht=ht, ya=ya, q=q, k=k, v=v, ob=ob, lse_b=lse_b, qn=qn, kn=kn, vn=vn, ogs=ogs, lses=lses)
    return out, saved


def _layer_bwd(dout, sv, p, tabs, slopes, B, S, big_ready=None):
    proj = sv["proj"]
    (dproj, dya, dob, dlb, dg0, dg1, dg2, dl0, dl1, dl2, merged, dpa, dpb, dpc, yb, yc, dbg) = _merge_bwd(
        dout, proj, p["b_gate"], sv["ya"], sv["ob"], sv["ogs"], sv["lses"], p["woa"], p["wob"], p["woc"], p["wo"])
    g = {}
    g["w_o"] = _matmul_tn(merged, dout, "dw_o").reshape(N_DEV, D_MODEL // N_DEV, D_MODEL)
    g["w_out_a"] = _cols_to_slots(_matmul_tn(sv["ya"], dpa, "dw_out_a"))
    g["w_out_b"] = _cols_to_slots(_matmul_tn(yb, dpb, "dw_out_b"))
    g["w_out_c"] = _cols_to_slots(_matmul_tn(yc, dpc, "dw_out_c"))
    g["b_gate"] = dbg[0]
    dproj, st = _mixa_bwd(dproj, dya, proj, p["conv_w"], p["conv_b"], B, S)
    g["conv_w"] = st[0:CONV_K]
    g["conv_b"] = st[CONV_K]
    dq, dk, dv = _mla_attn_bwd(sv["q"], sv["k"], sv["v"], dob, sv["lse_b"], dlb, B, S)
    dproj, dwuqp, dwkp, dwv, dgq, dgkv, dgmq, dgmk = _mla_prep_bwd(
        dproj, dq, dk, dv, proj, p["gq"], p["gkv"], p["wuqp"], p["wkp"], p["wv"], p["gmq"], p["gmk"], *tabs, S)
    g["w_uq"] = _cols_to_slots(dwuqp)[:, :, :MLA_QK]
    g["w_ukv"] = jnp.concatenate([_cols_to_slots(dwkp)[:, :, :MLA_NOPE], _cols_to_slots(dwv)], axis=2)
    g["q_a_norm_g"], g["kv_a_norm_g"] = dgq[0], dgkv[0]
    g["mla_q_norm_g"], g["mla_k_norm_g"] = dgmq[0, :MLA_QK], dgmk[0, :MLA_QK]
    dqkv = None
    for gi, (dog, dlg) in enumerate(((dg0, dl0), (dg1, dl1), (dg2, dl2))):
        dqkv = _dil_attn_bwd(gi, slopes[gi], sv["qn"], sv["kn"], sv["vn"], dog, sv["lses"][gi], dlg, dqkv, B, S)
    dproj, dgdq, dgdk = _dil_prep_bwd(dproj, *dqkv, proj, p["gdq"], p["gdk"])
    g["dil_q_norm_g"] = dgdq.reshape(DIL_GROUPS, DIL_HEADS, DIL_HEAD_DIM).sum(axis=1)
    g["dil_k_norm_g"] = dgdk.reshape(DIL_GROUPS, DIL_HEADS, DIL_HEAD_DIM).sum(axis=1)
    g["w_in"] = _unpad_columns(_matmul_nn(sv["ht"], dproj, "dw_in"))
    token = None if big_ready is None else big_ready(g)
    dx, dng = _inproj_bwd_x(dproj, p["wp"], sv["x"], _after(token, p["norm_g"]), dout)
    g["norm_g"] = dng[0]
    return dx, g


def _after(token, a):
    return a if token is None else a + token[0:1, 0:1]


def _local_step(x, target, small, B, S, weights_of, grads_out, big_ready=None):
    tabs = _rope_tables(S)
    sl = _alibi_slopes()
    slopes = [sl[gi] * float(DIL_PATTERNS[gi][1]) for gi in range(DIL_GROUPS)]
    params, saved = [], []
    for l in range(DEPTH):
        gw, token = weights_of(l, x)
        p = _layer_params(gw, small, l)
        p["norm_g"] = _after(token, p["norm_g"])
        x, sv = _layer_fwd(x, p, tabs, slopes, B, S)
        params.append(p)
        saved.append(sv)
    dout, lparts = _loss_head(x, target)
    sq = jnp.sum(lparts[:, 0, 0])
    token = None
    for l in reversed(range(DEPTH)):
        p = dict(params[l], b_gate=_after(token, params[l]["b_gate"]))
        ready = None if big_ready is None else (lambda g, l=l: big_ready(l, g))
        dout, g = _layer_bwd(dout, saved[l], p, tabs, slopes, B, S, ready)
        token = grads_out(l, g, dout)
    return sq, dout


def kernel(x, norm_g, w_in, b_gate, conv_w, conv_b, q_a_norm_g, w_uq, kv_a_norm_g, w_ukv, mla_q_norm_g, mla_k_norm_g, dil_q_norm_g, dil_k_norm_g, w_out_a, w_out_b, w_out_c, w_o, loss_target, m_norm_g, m_w_in, m_b_gate, m_conv_w, m_conv_b, m_q_a_norm_g, m_w_uq, m_kv_a_norm_g, m_w_ukv, m_mla_q_norm_g, m_mla_k_norm_g, m_dil_q_norm_g, m_dil_k_norm_g, m_w_out_a, m_w_out_b, m_w_out_c, m_w_o, v_norm_g, v_w_in, v_b_gate, v_conv_w, v_conv_b, v_q_a_norm_g, v_w_uq, v_kv_a_norm_g, v_w_ukv, v_mla_q_norm_g, v_mla_k_norm_g, v_dil_q_norm_g, v_dil_k_norm_g, v_w_out_a, v_w_out_b, v_w_out_c, v_w_o):
    names = ("norm_g", "w_in", "b_gate", "conv_w", "conv_b", "q_a_norm_g", "w_uq", "kv_a_norm_g", "w_ukv",
             "mla_q_norm_g", "mla_k_norm_g", "dil_q_norm_g", "dil_k_norm_g", "w_out_a", "w_out_b", "w_out_c", "w_o")
    w = dict(zip(names, (norm_g, w_in, b_gate, conv_w, conv_b, q_a_norm_g, w_uq, kv_a_norm_g, w_ukv, mla_q_norm_g,
                         mla_k_norm_g, dil_q_norm_g, dil_k_norm_g, w_out_a, w_out_b, w_out_c, w_o)))
    m = dict(zip(names, (m_norm_g, m_w_in, m_b_gate, m_conv_w, m_conv_b, m_q_a_norm_g, m_w_uq, m_kv_a_norm_g, m_w_ukv,
                         m_mla_q_norm_g, m_mla_k_norm_g, m_dil_q_norm_g, m_dil_k_norm_g, m_w_out_a, m_w_out_b,
                         m_w_out_c, m_w_o)))
    v = dict(zip(names, (v_norm_g, v_w_in, v_b_gate, v_conv_w, v_conv_b, v_q_a_norm_g, v_w_uq, v_kv_a_norm_g, v_w_ukv,
                         v_mla_q_norm_g, v_mla_k_norm_g, v_dil_q_norm_g, v_dil_k_norm_g, v_w_out_a, v_w_out_b,
                         v_w_out_c, v_w_o)))
    B, S, _ = x.shape
    me = _my_index()
    cshard = CONV_WIDTH // N_DEV

    shards = [[w[n][l].astype(BF16) for n in BIG] for l in range(DEPTH)]
    state = {}

    def weights_of(l, after):
        if l == 0:
            got = _gather_two_level(shards[0] + [conv_w], "all_gather_weights_0")
            state["gather"], token = _exchange_start(shards[1], "all_gather_weights_1_start", "gather")
            state["conv_w"] = got[-1]
        else:
            landed = _exchange_wait(state["gather"], after, "all_gather_weights_1_wait", "gather")
            got, token = [_own_slot(a, s[None]) for a, s in zip(landed, shards[1])], None
        gw = dict(zip(BIG, got))
        gw["conv_w"] = state["conv_w"][:, l]
        return gw, token

    recv, small_parts = {}, {}
    my_chip = 2 * lax.axis_index("x") + lax.axis_index("y")

    def big_ready(l, g):
        send = [g[n].astype(BF16) for n in BIG]
        if l == DEPTH - 1:
            state["scatter"], token = _exchange_start(send, "exchange_weight_grads_1_start", "scatter")
        else:
            swapped = _sibling_swap(send, "exchange_weight_grads_0_sibling")
            send = [_chip_pair_sum(s, t, "chip_pair_sum_" + n) for n, s, t in zip(BIG, send, swapped)]
            state["chips"], token = _exchange_start(send, "exchange_weight_grads_0_start", "chips")
        state["sent", l] = send
        return token

    def grads_out(l, g, after):
        small_parts[l] = [g[n] for n in SMALL]
        if l == DEPTH - 1:
            return None
        for k, key, mode, slot in ((DEPTH - 1, "scatter", "scatter", me), (0, "chips", "chips", my_chip)):
            landed = _exchange_wait(state[key], after, f"exchange_weight_grads_{k}_wait", mode)
            mine = [lax.dynamic_slice_in_dim(s, slot, 1, axis=0) for s in state["sent", k]]
            recv[k] = [_own_slot(a, s, slot) for a, s in zip(landed, mine)]
        return None

    sq, grad_x = _local_step(x.reshape(B * S, D_MODEL), loss_target.reshape(B * S, D_MODEL), w, B, S,
                             weights_of, grads_out, big_ready)
    loss = lax.psum(sq * (0.5 / D_MODEL), AXES)

    res = {}
    for i, n in enumerate(BIG):
        rows = lambda a: a.reshape(-1, a.shape[-1])
        outs = _reduce_adamw([recv[l][i] for l in range(DEPTH)], rows(w[n]), rows(m[n]), rows(v[n]),
                             "reduce_adamw_" + n)
        res[n] = tuple(a.reshape(w[n].shape) for a in outs)
    part = {n: jnp.stack([small_parts[l][i] for l in range(DEPTH)]) for i, n in enumerate(SMALL)}

    def widen(t):
        return lax.dynamic_update_slice(jnp.zeros((DEPTH, CONV_K, CONV_WIDTH), F32), t, (0, 0, me * cshard))

    small_like = [part[n] for n in SMALL]
    pick = lambda d: [widen(d[n]) if n == "conv_w" else d[n] for n in SMALL]
    parts, = _exchange([_pack_local(small_like)], "all_gather_small_grads", gather=True)
    gs, ds, ms, vs = _reduce_adamw([parts], _pack_local(pick(w)), _pack_local(pick(m)), _pack_local(pick(v)),
                                   "reduce_adamw_small")
    for n, t in zip(SMALL, zip(*(_unpack_local(a, small_like) for a in (gs, ds, ms, vs)))):
        if n == "conv_w":
            t = tuple(lax.dynamic_slice(a, (0, 0, me * cshard), (DEPTH, CONV_K, cshard)) for a in t)
        res[n] = t

    out = [loss, grad_x.reshape(B, S, D_MODEL)]
    for i in range(4):
        out += [res[n][i] for n in names]
    return tuple(out)
```

```python
import jax
import jax.numpy as jnp
from jax import lax
from jax.experimental import pallas as pl
from jax.experimental.pallas import tpu as pltpu

F32 = jnp.float32
BF16 = jnp.bfloat16

D_MODEL = 1024
DEPTH = 2
CONV_WIDTH = 512
CONV_K = 3
MLA_HEADS = 8
MLA_Q_LORA = 256
MLA_KV_LORA = 128
MLA_NOPE = 64
MLA_ROPE = 32
MLA_V = 64
MLA_QK = MLA_NOPE + MLA_ROPE
ROPE_THETA = 10000.0
DIL_PATTERNS = ((128, 1), (512, 4), (2048, 16))
DIL_GROUPS = 3
DIL_HEADS = 8
DIL_HEAD_DIM = 64
DIL_WIDTH = DIL_HEADS * DIL_HEAD_DIM
DIL_QK = DIL_GROUPS * DIL_WIDTH
EPS = 1e-6
N_IN = 11168

ADAM_LR = 0.001
ADAM_B1 = 0.9
ADAM_B2 = 0.999
ADAM_EPS = 1e-08
ADAM_WD = 0.01
ADAM_STEP = 10

N_DEV = 8
AXES = ("x", "y", "c")
LANE = 128
HALF = 64
NPAIR = 4

CB_AB, CB_AC, CB_AX, CB_AZ = 0, 4, 8, 12
CB_CQ, CB_CKV, CB_KPE = 16, 18, 19
CB_BZ = 20
CB_DQ, CB_DK, CB_DV = 24, 36, 48
CB_CZ, CB_GATE = 60, 64
NCB = 88
PP = NCB * LANE
KPE_END = CB_KPE * LANE + MLA_ROPE
SHARD_COLS = N_IN // N_DEV
NEG = -1e30
VMEM_LIMIT = 56 * 1024 * 1024


def _pad_columns(shards):
    parts = []
    for p in range(N_DEV):
        cut = min(max(KPE_END - p * SHARD_COLS, 0), SHARD_COLS)
        if 0 < cut < SHARD_COLS:
            parts += [shards[p, :, :cut], jnp.zeros((shards.shape[1], LANE - MLA_ROPE), shards.dtype), shards[p, :, cut:]]
        else:
            parts.append(shards[p])
    return jnp.concatenate(parts, axis=1)


def _unpad_columns(wp):
    def columns(a, b):
        gap = LANE - MLA_ROPE
        if b <= KPE_END:
            return wp[:, a:b]
        if a >= KPE_END:
            return wp[:, a + gap:b + gap]
        return jnp.concatenate([wp[:, a:KPE_END], wp[:, KPE_END + gap:b + gap]], axis=1)

    return jnp.stack([columns(p * SHARD_COLS, (p + 1) * SHARD_COLS) for p in range(N_DEV)])


def _put_copies(stages, dst_ref, sems, slot, rows, cols):
    return [pltpu.make_async_copy(st.at[slot], dst_ref.at[rows, pl.ds(c0, st.shape[-1])], sems.at[slot, k])
            for k, (st, c0) in enumerate(zip(stages, cols))]


def _put_pipeline(step, nsteps, copies_of, fill):
    @pl.when(step >= 2)
    def _():
        for cp in copies_of(step - 2):
            cp.wait()

    fill(step % 2)
    for cp in copies_of(step):
        cp.start()

    @pl.when(step == nsteps - 1)
    def _():
        if nsteps >= 2:
            for cp in copies_of(step - 1):
                cp.wait()
        for cp in copies_of(step):
            cp.wait()


def _cp():
    return pltpu.CompilerParams(vmem_limit_bytes=VMEM_LIMIT)


def _rstd(x, n):
    return lax.rsqrt(jnp.sum(x * x, axis=-1, keepdims=True) * (1.0 / n) + EPS)


def _sigmoid(z):
    return 1.0 / (1.0 + jnp.exp(-z))


def _silu(z):
    return z * _sigmoid(z)


def _dsilu(z):
    s = _sigmoid(z)
    return s * (1.0 + z * (1.0 - s))


def _mm(a, b):
    return jnp.dot(a.astype(BF16), b.astype(BF16), preferred_element_type=F32)


def _mm_nt(a, b):
    return lax.dot_general(a.astype(BF16), b.astype(BF16), (((1,), (1,)), ((), ())), preferred_element_type=F32)


def _mm_tn(a, b):
    return lax.dot_general(a.astype(BF16), b.astype(BF16), (((0,), (0,)), ((), ())), preferred_element_type=F32)


def _lane_lo(shape):
    return lax.broadcasted_iota(jnp.int32, shape, len(shape) - 1) < HALF


def _head_bcast_sum(x, terms=3):
    w = x.shape[-1]
    same = (lax.broadcasted_iota(jnp.int32, (w, w), 0) // HALF) == (lax.broadcasted_iota(jnp.int32, (w, w), 1) // HALF)
    ones = jnp.where(same, 1.0, 0.0).astype(jnp.bfloat16)
    total = None
    for _ in range(terms):
        term = x.astype(jnp.bfloat16)
        x = x - term.astype(F32)
        part = jnp.dot(term, ones, preferred_element_type=F32)
        total = part if total is None else total + part
    return total


def _rope(t, cos, sa, sb):
    return t * cos + pltpu.roll(t, LANE - 16, axis=1) * sa + pltpu.roll(t, 16, axis=1) * sb


def _rope_t(d, cos, sa, sb):
    return d * cos + pltpu.roll(d * sa, 16, axis=1) + pltpu.roll(d * sb, LANE - 16, axis=1)


def _shift_down(u, k):
    rows = lax.broadcasted_iota(jnp.int32, u.shape, 0)
    return jnp.where(rows >= k, pltpu.roll(u, k, axis=0), 0.0)


def _shift_up(u, k):
    n = u.shape[0]
    rows = lax.broadcasted_iota(jnp.int32, u.shape, 0)
    return jnp.where(rows < n - k, pltpu.roll(u, n - k, axis=0), 0.0)


def _tile(n, want):
    t = min(n, want)
    assert n % t == 0, (n, want)
    return t


def _inproj_fwd(x, g, wp):
    T = x.shape[0]
    tm, tn = _tile(T, 2048), 512

    def body(x_ref, g_ref, w_ref, proj_ref, ht_ref, h_ref):
        @pl.when(pl.program_id(1) == 0)
        def _():
            n = min(tm, 512)
            for r0 in range(0, tm, n):
                xv = x_ref[r0:r0 + n, :]
                h = xv * _rstd(xv, D_MODEL) * g_ref[...]
                h_ref[r0:r0 + n, :] = h.astype(BF16)
                ht_ref[:, r0:r0 + n] = h.T.astype(BF16)

        proj_ref[...] = jnp.dot(h_ref[...], w_ref[...], preferred_element_type=F32).astype(BF16)

    return pl.pallas_call(
        body, name="inproj_fwd", grid=(T // tm, PP // tn),
        in_specs=[pl.BlockSpec((tm, D_MODEL), lambda i, j: (i, 0)),
                  pl.BlockSpec((1, D_MODEL), lambda i, j: (0, 0)),
                  pl.BlockSpec((D_MODEL, tn), lambda i, j: (0, j))],
        out_specs=[pl.BlockSpec((tm, tn), lambda i, j: (i, j)),
                   pl.BlockSpec((D_MODEL, tm), lambda i, j: (0, i))],
        out_shape=[jax.ShapeDtypeStruct((T, PP), BF16), jax.ShapeDtypeStruct((D_MODEL, T), BF16)],
        scratch_shapes=[pltpu.VMEM((tm, D_MODEL), BF16)],
        compiler_params=_cp(),
    )(x, g, wp)


def _matmul_nn(at, b, name):
    K, T = at.shape
    N = b.shape[1]
    tt, tn = _tile(T, 1024), _tile(N, 2816)
    nk = T // tt

    def body(a_ref, b_ref, o_ref, acc_ref):
        k = pl.program_id(1)

        @pl.when(k == 0)
        def _():
            acc_ref[...] = jnp.zeros_like(acc_ref)

        acc_ref[...] += jnp.dot(a_ref[...], b_ref[...], preferred_element_type=F32)

        @pl.when(k == nk - 1)
        def _():
            o_ref[...] = acc_ref[...].astype(BF16)

    return pl.pallas_call(
        body, name=name, grid=(N // tn, nk),
        in_specs=[pl.BlockSpec((K, tt), lambda j, k: (0, k)),
                  pl.BlockSpec((tt, tn), lambda j, k: (k, j))],
        out_specs=pl.BlockSpec((K, tn), lambda j, k: (0, j)),
        out_shape=jax.ShapeDtypeStruct((K, N), BF16),
        scratch_shapes=[pltpu.VMEM((K, tn), F32)],
        compiler_params=_cp(),
    )(at, b)


def _matmul_tn(a, b, name):
    T, K = a.shape
    N = b.shape[1]
    tt, tn = _tile(T, 512), _tile(N, 1024)

    def body(a_ref, b_ref, o_ref):
        @pl.when(pl.program_id(1) == 0)
        def _():
            o_ref[...] = jnp.zeros_like(o_ref)

        o_ref[...] += _mm_tn(a_ref[...], b_ref[...])

    return pl.pallas_call(
        body, name=name, grid=(N // tn, T // tt),
        in_specs=[pl.BlockSpec((tt, K), lambda j, k: (k, 0)),
                  pl.BlockSpec((tt, tn), lambda j, k: (k, j))],
        out_specs=pl.BlockSpec((K, tn), lambda j, k: (0, j)),
        out_shape=jax.ShapeDtypeStruct((K, N), F32),
        compiler_params=_cp(),
    )(a, b)


def _inproj_bwd_x(dproj, wp, x, g, dout):
    T = x.shape[0]
    tm, tk = _tile(T, 1024), 1024
    nk = PP // tk

    def body(dp_ref, w_ref, x_ref, g_ref, do_ref, dx_ref, dg_ref, acc_ref):
        i, k = pl.program_id(0), pl.program_id(1)

        @pl.when(k == 0)
        def _():
            acc_ref[...] = jnp.zeros_like(acc_ref)

        @pl.when((k == 0) & (i == 0))
        def _():
            dg_ref[...] = jnp.zeros_like(dg_ref)

        acc_ref[...] += _mm_nt(dp_ref[...], w_ref[...])

        @pl.when(k == nk - 1)
        def _():
            dh = acc_ref[...]
            xv = x_ref[...]
            r = _rstd(xv, D_MODEL)
            gy = dh * g_ref[...]
            dot = jnp.sum(xv * gy, axis=-1, keepdims=True) * (1.0 / D_MODEL)
            dx_ref[...] = do_ref[...] + r * gy - xv * (r * r * r) * dot
            dg_ref[...] += jnp.sum(dh * xv * r, axis=0, keepdims=True)

    return pl.pallas_call(
        body, name="inproj_bwd_x", grid=(T // tm, nk),
        in_specs=[pl.BlockSpec((tm, tk), lambda i, k: (i, k)),
                  pl.BlockSpec((D_MODEL, tk), lambda i, k: (0, k)),
                  pl.BlockSpec((tm, D_MODEL), lambda i, k: (i, 0)),
                  pl.BlockSpec((1, D_MODEL), lambda i, k: (0, 0)),
                  pl.BlockSpec((tm, D_MODEL), lambda i, k: (i, 0))],
        out_specs=[pl.BlockSpec((tm, D_MODEL), lambda i, k: (i, 0)),
                   pl.BlockSpec((1, D_MODEL), lambda i, k: (0, 0))],
        out_shape=[jax.ShapeDtypeStruct((T, D_MODEL), F32), jax.ShapeDtypeStruct((1, D_MODEL), F32)],
        scratch_shapes=[pltpu.VMEM((tm, D_MODEL), F32)],
        compiler_params=_cp(),
    )(dproj, wp, x, g, dout)


A_SEGS = (CB_AB, CB_AC, CB_AX, CB_AZ)


def _mixa_fwd(proj, cw, cb, B, S):
    nc = CONV_WIDTH // LANE

    def body(ab_ref, ac_ref, ax_ref, az_ref, cw_ref, cb_ref, y_ref):
        ab, ac, ax, az = (r[...].astype(F32) for r in (ab_ref, ac_ref, ax_ref, az_ref))
        u = ac * ax
        conv = cb_ref[...] + cw_ref[0:1, :] * _shift_down(u, 2) + cw_ref[1:2, :] * _shift_down(u, 1) + cw_ref[2:3, :] * u
        y_ref[...] = (ab * conv * _silu(az)).astype(BF16)

    return pl.pallas_call(
        body, name="mixa_fwd", grid=(B, nc),
        in_specs=[pl.BlockSpec((S, LANE), lambda b, j, c0=c0: (b, c0 + j)) for c0 in A_SEGS]
                 + [pl.BlockSpec((CONV_K, LANE), lambda b, j: (0, j)),
                    pl.BlockSpec((1, LANE), lambda b, j: (0, j))],
        out_specs=pl.BlockSpec((S, LANE), lambda b, j: (b, j)),
        out_shape=jax.ShapeDtypeStruct((B * S, CONV_WIDTH), BF16),
        compiler_params=_cp(),
    )(proj, proj, proj, proj, cw, cb)


def _mixa_bwd(dproj, dy, proj, cw, cb, B, S):
    nc = CONV_WIDTH // LANE

    def body(dpin_ref, dy_ref, ab_ref, ac_ref, ax_ref, az_ref, cw_ref, cb_ref, dp_ref, st_ref, stage, sems):
        del dpin_ref
        j, b = pl.program_id(0), pl.program_id(1)
        ab, ac, ax, az = (r[...].astype(F32) for r in (ab_ref, ac_ref, ax_ref, az_ref))
        u = ac * ax
        u1, u2 = _shift_down(u, 1), _shift_down(u, 2)
        w0, w1, w2 = cw_ref[0:1, :], cw_ref[1:2, :], cw_ref[2:3, :]
        conv = cb_ref[...] + w0 * u2 + w1 * u1 + w2 * u
        s = _silu(az)
        d = dy_ref[...]
        dconv = d * ab * s
        du = w2 * dconv + w1 * _shift_up(dconv, 1) + w0 * _shift_up(dconv, 2)
        grads = (d * conv * s, du * ax, du * ac, d * ab * conv * _dsilu(az))

        def fill(slot):
            for k, v in enumerate(grads):
                stage[slot, k] = v.astype(BF16)

        def copies_of(step):
            sj, sb = step // B, step % B
            return _put_copies([stage.at[:, k] for k in range(4)], dp_ref, sems, step % 2,
                               pl.ds(pl.multiple_of(sb * S, S), S),
                               [pl.multiple_of((c0 + sj) * LANE, LANE) for c0 in A_SEGS])

        _put_pipeline(j * B + b, nc * B, copies_of, fill)
        row = lax.broadcasted_iota(jnp.int32, (8, LANE), 0)
        st = jnp.zeros((8, LANE), F32)
        for r, v in enumerate((dconv * u2, dconv * u1, dconv * u, dconv)):
            st = st + jnp.where(row == r, jnp.sum(v, axis=0, keepdims=True), 0.0)

        @pl.when(pl.program_id(1) == 0)
        def _():
            st_ref[...] = st

        @pl.when(pl.program_id(1) != 0)
        def _():
            st_ref[...] += st

    return pl.pallas_call(
        body, name="mixa_bwd", grid=(nc, B),
        in_specs=[pl.BlockSpec(memory_space=pl.ANY),
                  pl.BlockSpec((S, LANE), lambda j, b: (b, j))]
                 + [pl.BlockSpec((S, LANE), lambda j, b, c0=c0: (b, c0 + j)) for c0 in A_SEGS]
                 + [pl.BlockSpec((CONV_K, LANE), lambda j, b: (0, j)),
                    pl.BlockSpec((1, LANE), lambda j, b: (0, j))],
        out_specs=[pl.BlockSpec(memory_space=pl.ANY),
                   pl.BlockSpec((8, LANE), lambda j, b: (0, j))],
        out_shape=[jax.ShapeDtypeStruct(dproj.shape, BF16), jax.ShapeDtypeStruct((8, CONV_WIDTH), F32)],
        scratch_shapes=[pltpu.VMEM((2, 4, S, LANE), BF16), pltpu.SemaphoreType.DMA((2, 4))],
        input_output_aliases={0: 0},
        compiler_params=_cp(),
    )(dproj, dy, proj, proj, proj, proj, cw, cb)


def _mla_prep_fwd(proj, gq, gkv, wuqp, wkp, wv, gmq, gmk, cos, sa, sb, S):
    T = proj.shape[0]
    ts = _tile(S, 512)
    ns = S // ts
    W = MLA_HEADS * LANE

    def body(p_ref, gq_ref, gkv_ref, wuq_ref, wk_ref, wv_ref, gmq_ref, gmk_ref, cos_ref, sa_ref, sb_ref,
             q_ref, k_ref, v_ref):
        cq = p_ref[:, 0:2 * LANE].astype(F32)
        ckv = p_ref[:, 2 * LANE:3 * LANE].astype(F32)
        kpe = pltpu.roll(p_ref[:, 3 * LANE:4 * LANE].astype(F32), HALF, axis=1)
        cqn = cq * _rstd(cq, MLA_Q_LORA) * gq_ref[...]
        ckn = (ckv * _rstd(ckv, MLA_KV_LORA) * gkv_ref[...]).astype(BF16)
        q0 = _mm(cqn, wuq_ref[...])
        kn = _mm(ckn, wk_ref[...])
        v_ref[...] = _mm(ckn, wv_ref[...]).astype(BF16)
        c, a, b = cos_ref[...], sa_ref[...], sb_ref[...]
        kpe_rot = _rope(kpe * gmk_ref[...], c, a, b)
        for h in range(MLA_HEADS):
            q0h = q0[:, h * LANE:(h + 1) * LANE]
            q_ref[h] = _rope(q0h * _rstd(q0h, MLA_QK) * gmq_ref[...], c, a, b).astype(BF16)
            knh = kn[:, h * LANE:(h + 1) * LANE]
            k_ref[h] = (_rstd(knh + kpe, MLA_QK) * (knh * gmk_ref[...] + kpe_rot)).astype(BF16)

    def whole(r, c):
        return pl.BlockSpec((r, c), lambda i: (0, 0))

    tab = pl.BlockSpec((ts, LANE), lambda i: (i % ns, 0))
    return pl.pallas_call(
        body, name="mla_prep_fwd", grid=(T // ts,),
        in_specs=[pl.BlockSpec((ts, 4 * LANE), lambda i: (i, CB_CQ // 4)),
                  whole(1, MLA_Q_LORA), whole(1, MLA_KV_LORA), whole(MLA_Q_LORA, W), whole(MLA_KV_LORA, W),
                  whole(MLA_KV_LORA, MLA_HEADS * MLA_V), whole(1, LANE), whole(1, LANE), tab, tab, tab],
        out_specs=[pl.BlockSpec((MLA_HEADS, ts, LANE), lambda i: (0, i, 0)),
                   pl.BlockSpec((MLA_HEADS, ts, LANE), lambda i: (0, i, 0)),
                   pl.BlockSpec((ts, MLA_HEADS * MLA_V), lambda i: (i, 0))],
        out_shape=[jax.ShapeDtypeStruct((MLA_HEADS, T, LANE), BF16), jax.ShapeDtypeStruct((MLA_HEADS, T, LANE), BF16),
                   jax.ShapeDtypeStruct((T, MLA_HEADS * MLA_V), BF16)],
        compiler_params=_cp(),
    )(proj, gq, gkv, wuqp, wkp, wv, gmq, gmk, cos, sa, sb)


def _mla_prep_bwd(dproj, dq, dk, dv, proj, gq, gkv, wuqp, wkp, wv, gmq, gmk, cos, sa, sb, S):
    T = proj.shape[0]
    ts = _tile(S, 256)
    ns = S // ts
    W = MLA_HEADS * LANE

    def body(dpin_ref, dq_ref, dk_ref, dv_ref, p_ref, gq_ref, gkv_ref, wuq_ref, wk_ref, wv_ref, gmq_ref, gmk_ref,
             cos_ref, sa_ref, sb_ref,
             dp_ref, dwuq_ref, dwk_ref, dwv_ref, dgq_ref, dgkv_ref, dgmq_ref, dgmk_ref, dq0_ref, dkn_ref):
        del dpin_ref

        @pl.when(pl.program_id(0) == 0)
        def _():
            for r in (dwuq_ref, dwk_ref, dwv_ref, dgq_ref, dgkv_ref, dgmq_ref, dgmk_ref):
                r[...] = jnp.zeros_like(r)

        cq = p_ref[:, 0:2 * LANE].astype(F32)
        ckv = p_ref[:, 2 * LANE:3 * LANE].astype(F32)
        kpe = pltpu.roll(p_ref[:, 3 * LANE:4 * LANE].astype(F32), HALF, axis=1)
        rq = _rstd(cq, MLA_Q_LORA)
        rkv = _rstd(ckv, MLA_KV_LORA)
        gq, gkv, gmq, gmk = gq_ref[...], gkv_ref[...], gmq_ref[...], gmk_ref[...]
        cqn = (cq * rq * gq).astype(BF16)
        ckn = (ckv * rkv * gkv).astype(BF16)
        q0 = _mm(cqn, wuq_ref[...])
        kn = _mm(ckn, wk_ref[...])
        c, a, b = cos_ref[...], sa_ref[...], sb_ref[...]
        lane = lax.broadcasted_iota(jnp.int32, (ts, LANE), 1)
        dgmq = jnp.zeros((1, LANE), F32)
        dgmk = jnp.zeros((1, LANE), F32)
        dkpe = jnp.zeros((ts, LANE), F32)
        for h in range(MLA_HEADS):
            q0h = q0[:, h * LANE:(h + 1) * LANE]
            r = _rstd(q0h, MLA_QK)
            d1 = _rope_t(dq_ref[h], c, a, b)
            gy = d1 * gmq
            dq0_ref[:, h * LANE:(h + 1) * LANE] = (
                r * gy - q0h * (r * r * r) * (jnp.sum(q0h * gy, axis=-1, keepdims=True) * (1.0 / MLA_QK))).astype(BF16)
            dgmq = dgmq + jnp.sum(d1 * q0h * r, axis=0, keepdims=True)
            k0h = kn[:, h * LANE:(h + 1) * LANE] + kpe
            r = _rstd(k0h, MLA_QK)
            d1 = _rope_t(dk_ref[h], c, a, b)
            gy = d1 * gmk
            dk0 = r * gy - k0h * (r * r * r) * (jnp.sum(k0h * gy, axis=-1, keepdims=True) * (1.0 / MLA_QK))
            dgmk = dgmk + jnp.sum(d1 * k0h * r, axis=0, keepdims=True)
            dkn_ref[:, h * LANE:(h + 1) * LANE] = jnp.where(lane < MLA_NOPE, dk0, 0.0).astype(BF16)
            dkpe = dkpe + jnp.where((lane >= MLA_NOPE) & (lane < MLA_QK), dk0, 0.0)
        dq0 = dq0_ref[...]
        dkn = dkn_ref[...]
        dvv = dv_ref[...]
        dwuq_ref[...] += _mm_tn(cqn, dq0)
        dwk_ref[...] += _mm_tn(ckn, dkn)
        dwv_ref[...] += _mm_tn(ckn, dvv)
        dgmq_ref[...] += dgmq
        dgmk_ref[...] += dgmk
        dcqn = _mm_nt(dq0, wuq_ref[...])
        gy = dcqn * gq
        dp_ref[:, 0:2 * LANE] = (
            rq * gy - cq * (rq * rq * rq) * (jnp.sum(cq * gy, axis=-1, keepdims=True) * (1.0 / MLA_Q_LORA))).astype(BF16)
        dgq_ref[...] += jnp.sum(dcqn * cq * rq, axis=0, keepdims=True)
        dckn = _mm_nt(dkn, wk_ref[...]) + _mm_nt(dvv, wv_ref[...])
        gy = dckn * gkv
        dp_ref[:, 2 * LANE:3 * LANE] = (
            rkv * gy - ckv * (rkv * rkv * rkv) * (jnp.sum(ckv * gy, axis=-1, keepdims=True) * (1.0 / MLA_KV_LORA))).astype(BF16)
        dgkv_ref[...] += jnp.sum(dckn * ckv * rkv, axis=0, keepdims=True)
        dp_ref[:, 3 * LANE:4 * LANE] = pltpu.roll(dkpe, HALF, axis=1).astype(BF16)

    def whole(r, c):
        return pl.BlockSpec((r, c), lambda i: (0, 0))

    tab = pl.BlockSpec((ts, LANE), lambda i: (i % ns, 0))
    heads = pl.BlockSpec((MLA_HEADS, ts, LANE), lambda i: (0, i, 0))
    return pl.pallas_call(
        body, name="mla_prep_bwd", grid=(T // ts,),
        in_specs=[pl.BlockSpec(memory_space=pl.ANY), heads, heads,
                  pl.BlockSpec((ts, MLA_HEADS * MLA_V), lambda i: (i, 0)),
                  pl.BlockSpec((ts, 4 * LANE), lambda i: (i, CB_CQ // 4)),
                  whole(1, MLA_Q_LORA), whole(1, MLA_KV_LORA), whole(MLA_Q_LORA, W), whole(MLA_KV_LORA, W),
                  whole(MLA_KV_LORA, MLA_HEADS * MLA_V), whole(1, LANE), whole(1, LANE), tab, tab, tab],
        out_specs=[pl.BlockSpec((ts, 4 * LANE), lambda i: (i, CB_CQ // 4)),
                   whole(MLA_Q_LORA, W), whole(MLA_KV_LORA, W), whole(MLA_KV_LORA, MLA_HEADS * MLA_V),
                   whole(1, MLA_Q_LORA), whole(1, MLA_KV_LORA), whole(1, LANE), whole(1, LANE)],
        out_shape=[jax.ShapeDtypeStruct(dproj.shape, BF16),
                   jax.ShapeDtypeStruct((MLA_Q_LORA, W), F32), jax.ShapeDtypeStruct((MLA_KV_LORA, W), F32),
                   jax.ShapeDtypeStruct((MLA_KV_LORA, MLA_HEADS * MLA_V), F32),
                   jax.ShapeDtypeStruct((1, MLA_Q_LORA), F32), jax.ShapeDtypeStruct((1, MLA_KV_LORA), F32),
                   jax.ShapeDtypeStruct((1, LANE), F32), jax.ShapeDtypeStruct((1, LANE), F32)],
        scratch_shapes=[pltpu.VMEM((ts, W), BF16), pltpu.VMEM((ts, W), BF16)],
        input_output_aliases={0: 0},
        compiler_params=_cp(),
    )(dproj, dq, dk, dv, proj, gq, gkv, wuqp, wkp, wv, gmq, gmk, cos, sa, sb)


def _dil_prep_fwd(proj, gq, gk):
    T = proj.shape[0]
    ts = _tile(T, 512)

    def body(pq_ref, pk_ref, gq_ref, gk_ref, q_ref, k_ref):
        for c in range(NPAIR):
            cs = slice(c * LANE, (c + 1) * LANE)
            t = jnp.concatenate([pq_ref[:, cs], pk_ref[:, cs]], axis=1).astype(F32)
            y = t * lax.rsqrt(_head_bcast_sum(t * t, terms=2) * (1.0 / DIL_HEAD_DIM) + EPS)
            q_ref[:, cs] = (y[:, 0:LANE] * gq_ref[:, cs]).astype(BF16)
            k_ref[:, cs] = (y[:, LANE:2 * LANE] * gk_ref[:, cs]).astype(BF16)

    col = pl.BlockSpec((1, DIL_WIDTH), lambda i, g: (0, g))
    out = pl.BlockSpec((ts, DIL_WIDTH), lambda i, g: (i, g))
    seg = lambda c0: pl.BlockSpec((ts, DIL_WIDTH), lambda i, g: (i, c0 // NPAIR + g))
    return pl.pallas_call(
        body, name="dil_prep_fwd", grid=(T // ts, DIL_GROUPS),
        in_specs=[seg(CB_DQ), seg(CB_DK), col, col],
        out_specs=[out, out],
        out_shape=[jax.ShapeDtypeStruct((T, DIL_QK), BF16)] * 2,
        compiler_params=_cp(),
    )(proj, proj, gq, gk)


def _dil_prep_bwd(dproj, ddq, ddk, ddv, proj, gq, gk):
    T = proj.shape[0]
    ts = _tile(T, 512)
    nt = T // ts

    def body(dpin_ref, ddq_ref, ddk_ref, ddv_ref, pq_ref, pk_ref, gq_ref, gk_ref, dp_ref, dgq_ref, dgk_ref,
             stage, sems):
        del dpin_ref
        g, i = pl.program_id(0), pl.program_id(1)

        @pl.when(i == 0)
        def _():
            dgq_ref[...] = jnp.zeros_like(dgq_ref)
            dgk_ref[...] = jnp.zeros_like(dgk_ref)

        def fill(slot):
            stage[slot, 2] = ddv_ref[...].astype(BF16)
            for c in range(NPAIR):
                cs = slice(c * LANE, (c + 1) * LANE)
                t = jnp.concatenate([pq_ref[:, cs], pk_ref[:, cs]], axis=1).astype(F32)
                d = jnp.concatenate([ddq_ref[:, cs], ddk_ref[:, cs]], axis=1)
                gy = d * jnp.concatenate([gq_ref[:, cs], gk_ref[:, cs]], axis=1)
                r = lax.rsqrt(_head_bcast_sum(t * t, terms=2) * (1.0 / DIL_HEAD_DIM) + EPS)
                dot = _head_bcast_sum(t * gy, terms=2) * (1.0 / DIL_HEAD_DIM)
                dx = (r * gy - t * (r * r * r) * dot).astype(BF16)
                stage[slot, 0, :, cs] = dx[:, 0:LANE]
                stage[slot, 1, :, cs] = dx[:, LANE:2 * LANE]
                part = jnp.sum(d * t * r, axis=0, keepdims=True)
                dgq_ref[:, cs] += part[:, 0:LANE]
                dgk_ref[:, cs] += part[:, LANE:2 * LANE]

        def copies_of(step):
            sg, si = step // nt, step % nt
            return _put_copies([stage.at[:, k] for k in range(3)], dp_ref, sems, step % 2,
                               pl.ds(pl.multiple_of(si * ts, ts), ts),
                               [pl.multiple_of((c0 + NPAIR * sg) * LANE, LANE) for c0 in (CB_DQ, CB_DK, CB_DV)])

        _put_pipeline(g * nt + i, DIL_GROUPS * nt, copies_of, fill)

    col = pl.BlockSpec((1, DIL_WIDTH), lambda g, i: (0, g))
    tok = pl.BlockSpec((ts, DIL_WIDTH), lambda g, i: (i, g))
    seg = lambda c0: pl.BlockSpec((ts, DIL_WIDTH), lambda g, i: (i, c0 // NPAIR + g))
    return pl.pallas_call(
        body, name="dil_prep_bwd", grid=(DIL_GROUPS, nt),
        in_specs=[pl.BlockSpec(memory_space=pl.ANY), tok, tok, tok, seg(CB_DQ), seg(CB_DK), col, col],
        out_specs=[pl.BlockSpec(memory_space=pl.ANY), col, col],
        out_shape=[jax.ShapeDtypeStruct(dproj.shape, BF16), jax.ShapeDtypeStruct((1, DIL_QK), F32),
                   jax.ShapeDtypeStruct((1, DIL_QK), F32)],
        scratch_shapes=[pltpu.VMEM((2, 3, ts, DIL_WIDTH), BF16), pltpu.SemaphoreType.DMA((2, 3))],
        input_output_aliases={0: 0},
        compiler_params=_cp(),
    )(dproj, ddq, ddk, ddv, proj, proj, gq, gk)


COPY_ROWS = 256


def _to_classes(src_ref, dst_ref, d, L, scale=None, wide_ref=None):
    n = min(L, COPY_ROWS)
    if d > 1 and src_ref.dtype != F32:
        step = min(COPY_ROWS, d * L)
        for c0 in range(0, d * L, step):
            wide_ref[c0:c0 + step, :] = src_ref[c0:c0 + step, :].astype(F32)
        src_ref = wide_ref
    for r in range(d):
        for c0 in range(0, L, n):
            rows = pl.ds(r + c0 * d, n, stride=d) if d > 1 else pl.ds(c0, n)
            val = src_ref[rows, :]
            if scale is not None:
                val = val * scale
            dst_ref[r * L + c0:r * L + c0 + n, :] = val.astype(dst_ref.dtype)


def _from_classes(src_ref, dst_ref, d, L):
    n = min(L, COPY_ROWS)
    for r in range(d):
        for c0 in range(0, L, n):
            rows = pl.ds(r + c0 * d, n, stride=d) if d > 1 else pl.ds(c0, n)
            dst_ref[rows, :] = src_ref[r * L + c0:r * L + c0 + n, :].astype(dst_ref.dtype)


MLA_TQ, MLA_TK = 512, 512


def _causal_bias(tq, tk, shift):
    row = lax.broadcasted_iota(jnp.int32, (tq, tk), 0)
    col = lax.broadcasted_iota(jnp.int32, (tq, tk), 1)
    return jnp.where(row >= col + shift, 0.0, NEG)


def _mla_specs(S):
    heads = pl.BlockSpec((2, S, LANE), lambda b, j: (j, b, 0))
    pair = pl.BlockSpec((S, LANE), lambda b, j: (b, j))
    return heads, pair


def _mla_attn_fwd(q, k, v, B, S):
    tq = _tile(S, MLA_TQ)
    tk = _tile(tq, MLA_TK)
    nd = tq // tk
    scale = MLA_QK ** -0.5
    heads, pair = _mla_specs(S)

    def body(q_ref, k_ref, v_ref, o_ref, lse_ref):
        lo, lok = _lane_lo((tq, LANE)), _lane_lo((tk, LANE))
        diag = [_causal_bias(tq, tk, i * tk) for i in range(nd)]

        def block(g, _):
            row0 = pl.multiple_of(g * tq, tq)
            rows = pl.ds(row0, tq)
            qs = [q_ref[hh, rows, :] for hh in range(2)]

            one = jnp.ones((), BF16)

            def step(off, carries, bias):
                off = pl.multiple_of(off, tk)
                vt = v_ref[pl.ds(off, tk), :]
                vh = (jnp.where(lok, vt, one), jnp.where(lok, one, vt))
                out = []
                for hh, (m, acc) in enumerate(carries):
                    s = _mm_nt(qs[hh], k_ref[hh, pl.ds(off, tk), :]) * scale
                    if bias is not None:
                        s = s + bias
                    m_new = jnp.maximum(m, jnp.max(s, axis=-1, keepdims=True))
                    p = jnp.exp(s - m_new)
                    out.append((m_new, jnp.exp(m - m_new) * acc + _mm(p, vh[hh])))
                return tuple(out)

            init = (jnp.full((tq, 1), NEG, F32), jnp.zeros((tq, LANE), F32))
            carries = lax.fori_loop(0, g * nd, lambda i, c: step(i * tk, c, None), (init, init))
            for i in range(nd):
                carries = step(row0 + i * tk, carries, diag[i])
            (ma, acca), (mb, accb) = carries
            la, lb = pltpu.roll(acca, HALF, axis=1), pltpu.roll(accb, HALF, axis=1)
            o_ref[rows, :] = jnp.where(lo, acca / la, accb / lb)
            lse_ref[rows, :] = jnp.where(lo, ma + jnp.log(la), mb + jnp.log(lb))
            return 0

        lax.fori_loop(0, S // tq, block, 0)

    return pl.pallas_call(
        body, name="mla_attn_fwd", grid=(B, NPAIR), in_specs=[heads, heads, pair], out_specs=[pair, pair],
        out_shape=[jax.ShapeDtypeStruct((B * S, MLA_HEADS * MLA_V), F32)] * 2,
        compiler_params=_cp(),
    )(q, k, v)


DIL_UNROLL = 16


def _dil_geometry(gi, S):
    span, d = DIL_PATTERNS[gi]
    L = S // d
    t = _tile(L, 128)
    window = span // d
    back = min(-(-window // t) * t, L - t)
    return d, L, t, window, back


def _dil_specs(gi, S):
    qk = pl.BlockSpec((S, LANE), lambda b, j: (b, NPAIR * gi + j))
    v = pl.BlockSpec((S, LANE), lambda b, j: (b, CB_DV + NPAIR * gi + j))
    pair = pl.BlockSpec((S, LANE), lambda b, j: (b, j))
    return qk, v, pair


def _dil_bias(bias_ref, sl_ref, j, t, kw, back, window):
    row = lax.broadcasted_iota(jnp.int32, (2 * t, kw), 0)
    col = lax.broadcasted_iota(jnp.int32, (2 * t, kw), 1)
    second = row >= t
    slope = jnp.where(second, sl_ref[j, 1], sl_ref[j, 0])
    for n in range(bias_ref.shape[0]):
        dist = jnp.where(second, row - t, row) + n * back - col
        bias_ref[n] = jnp.where((dist >= 0) & (dist <= window), -slope * dist.astype(F32), NEG)


def _stack_heads(x, lo):
    zero = jnp.zeros((), x.dtype)
    return jnp.concatenate([jnp.where(lo, x, zero), jnp.where(lo, zero, x)], axis=0)


def _dil_attn_fwd(gi, slopes, qn, kn, proj, B, S):
    d, L, t, window, back = _dil_geometry(gi, S)
    kw, nq = back + t, L // t
    nbias = 2 if back else 1
    qk, vspec, pair = _dil_specs(gi, S)

    def body(sl_ref, q_ref, k_ref, v_ref, o_ref, lse_ref, qs, ks, vs, os_, ls, bias_ref):
        _to_classes(q_ref, qs, d, L, DIL_HEAD_DIM ** -0.5, wide_ref=os_)
        _to_classes(k_ref, ks, d, L, wide_ref=os_)
        _to_classes(v_ref, vs, d, L, wide_ref=os_)
        _dil_bias(bias_ref, sl_ref, pl.program_id(1), t, kw, back, window)
        lo = _lane_lo((t, LANE))

        def block(g, _):
            qb = g % nq if d > 1 else g
            row0 = pl.multiple_of(g * t, t)
            rows = pl.ds(row0, t)
            early = qb * t < back
            keys = pl.ds(pl.multiple_of(jnp.where(early, row0 - qb * t, row0 - back), t), kw)
            s = _mm_nt(_stack_heads(qs[rows, :], lo), ks[keys, :]) + bias_ref[jnp.where(early, 0, nbias - 1)]
            m = jnp.max(s, axis=-1, keepdims=True)
            p = jnp.exp(s - m)
            l = jnp.sum(p, axis=-1, keepdims=True)
            o2 = _mm(p, vs[keys, :]) / l
            lse2 = m + jnp.log(l)
            os_[rows, :] = jnp.where(lo, o2[:t], o2[t:])
            ls[rows, :] = jnp.where(lo, lse2[:t], lse2[t:])
            return 0

        lax.fori_loop(0, d * nq, block, 0, unroll=DIL_UNROLL if d * nq % DIL_UNROLL == 0 else 1)
        _from_classes(os_, o_ref, d, L)
        _from_classes(ls, lse_ref, d, L)

    return pl.pallas_call(
        body, name=f"dil_attn_fwd_{gi}", grid=(B, NPAIR),
        in_specs=[pl.BlockSpec(memory_space=pltpu.SMEM), qk, qk, vspec], out_specs=[pair, pair],
        out_shape=[jax.ShapeDtypeStruct((B * S, DIL_WIDTH), F32)] * 2,
        scratch_shapes=[pltpu.VMEM((S, LANE), BF16)] * 3 + [pltpu.VMEM((S, LANE), F32)] * 2
                       + [pltpu.VMEM((nbias, 2 * t, kw), F32)],
        compiler_params=_cp(),
    )(slopes, qn, kn, proj)


def _mla_attn_bwd(q, k, v, do, lse, delta, B, S):
    T = B * S
    tq = _tile(S, MLA_TQ)
    tk = _tile(tq, MLA_TK)
    nd = tq // tk
    scale = MLA_QK ** -0.5
    heads, pair = _mla_specs(S)

    def body(q_ref, k_ref, v_ref, do_ref, lse_ref, dl_ref, dq_ref, dk_ref, dv_ref):
        dk_ref[...] = jnp.zeros_like(dk_ref)
        dv_ref[...] = jnp.zeros_like(dv_ref)
        lo = _lane_lo((tq, LANE))
        diag = [_causal_bias(tq, tk, i * tk) for i in range(nd)]

        def block(g, _):
            row0 = pl.multiple_of(g * tq, tq)
            rows = pl.ds(row0, tq)
            per_head = []
            for hh in range(2):
                sel = lo if hh == 0 else jnp.logical_not(lo)
                per_head.append((q_ref[hh, rows, :], jnp.where(sel, do_ref[rows, :], jnp.zeros((), BF16)),
                                 jnp.max(jnp.where(sel, lse_ref[rows, :], NEG), axis=-1, keepdims=True),
                                 jnp.max(jnp.where(sel, dl_ref[rows, :], NEG), axis=-1, keepdims=True)))

            def step(off, dq_accs, bias):
                cols = pl.ds(pl.multiple_of(off, tk), tk)
                vt = v_ref[cols, :]
                out, dv = [], None
                for hh, (qh, doh, lse_h, dl_h) in enumerate(per_head):
                    kh = k_ref[hh, cols, :]
                    s = _mm_nt(qh, kh) * scale
                    if bias is not None:
                        s = s + bias
                    p = jnp.exp(s - lse_h)
                    ds = (p * (_mm_nt(doh, vt) - dl_h)).astype(BF16)
                    dk_ref[hh, cols, :] += _mm_tn(ds, qh) * scale
                    part = _mm_tn(p, doh)
                    dv = part if dv is None else dv + part
                    out.append(dq_accs[hh] + _mm(ds, kh))
                dv_ref[cols, :] += dv
                return tuple(out)

            zero = jnp.zeros((tq, LANE), F32)
            dq_accs = lax.fori_loop(0, g * nd, lambda i, a: step(i * tk, a, None), (zero, zero))
            for i in range(nd):
                dq_accs = step(row0 + i * tk, dq_accs, diag[i])
            for hh in range(2):
                dq_ref[hh, rows, :] = dq_accs[hh] * scale
            return 0

        lax.fori_loop(0, S // tq, block, 0)

    return pl.pallas_call(
        body, name="mla_attn_bwd", grid=(B, NPAIR), in_specs=[heads, heads, pair, pair, pair, pair],
        out_specs=[heads, heads, pair],
        out_shape=[jax.ShapeDtypeStruct((MLA_HEADS, T, LANE), F32), jax.ShapeDtypeStruct((MLA_HEADS, T, LANE), F32),
                   jax.ShapeDtypeStruct((T, MLA_HEADS * MLA_V), F32)],
        compiler_params=_cp(),
    )(q, k, v, do, lse, delta)


def _dil_attn_bwd(gi, slopes, qn, kn, proj, do, lse, delta, through, B, S):
    d, L, t, window, back = _dil_geometry(gi, S)
    kw, nq = back + t, L // t
    nbias = 2 if back else 1
    scale = DIL_HEAD_DIM ** -0.5
    qk, vspec, pair = _dil_specs(gi, S)

    def body(*refs):
        refs = list(refs)
        sl_ref, q_ref, k_ref, v_ref, do_ref, lse_ref, dl_ref = refs[:7]
        dq_ref, dk_ref, dv_ref, qs, ks, vs, dos, lss, dls, dqs, dks, dvs, bias_ref = refs[-13:]
        _to_classes(q_ref, qs, d, L, scale, wide_ref=dqs)
        for src, dst in ((k_ref, ks), (v_ref, vs), (do_ref, dos), (lse_ref, lss), (dl_ref, dls)):
            _to_classes(src, dst, d, L, wide_ref=dqs)
        _dil_bias(bias_ref, sl_ref, pl.program_id(1), t, kw, back, window)
        dks[...] = jnp.zeros_like(dks)
        dvs[...] = jnp.zeros_like(dvs)
        lo = _lane_lo((t, LANE))

        def stats(ref, rows):
            x = ref[rows, :]
            return jnp.concatenate([jnp.max(jnp.where(lo, x, NEG), axis=-1, keepdims=True),
                                    jnp.max(jnp.where(lo, NEG, x), axis=-1, keepdims=True)], axis=0)

        def block(g, _):
            qb = g % nq if d > 1 else g
            row0 = pl.multiple_of(g * t, t)
            rows = pl.ds(row0, t)
            early = qb * t < back
            keys = pl.ds(pl.multiple_of(jnp.where(early, row0 - qb * t, row0 - back), t), kw)
            q2 = _stack_heads(qs[rows, :], lo)
            do2 = _stack_heads(dos[rows, :], lo)
            kt = ks[keys, :]
            s = _mm_nt(q2, kt) + bias_ref[jnp.where(early, 0, nbias - 1)]
            p = jnp.exp(s - stats(lss, rows))
            ds = (p * (_mm_nt(do2, vs[keys, :]) - stats(dls, rows))).astype(BF16)
            dq2 = _mm(ds, kt) * scale
            dqs[rows, :] = jnp.where(lo, dq2[:t], dq2[t:])
            dks[keys, :] += _mm_tn(ds, q2)
            dvs[keys, :] += _mm_tn(p, do2)
            return 0

        lax.fori_loop(0, d * nq, block, 0, unroll=DIL_UNROLL if d * nq % DIL_UNROLL == 0 else 1)
        for src, dst in ((dqs, dq_ref), (dks, dk_ref), (dvs, dv_ref)):
            _from_classes(src, dst, d, L)

    in_specs = [pl.BlockSpec(memory_space=pltpu.SMEM), qk, qk, vspec, pair, pair, pair]
    args = [slopes, qn, kn, proj, do, lse, delta]
    aliases = {}
    if through is not None:
        aliases = {len(args) + i: i for i in range(3)}
        in_specs = in_specs + [pl.BlockSpec(memory_space=pl.ANY)] * 3
        args = args + list(through)
    return pl.pallas_call(
        body, name=f"dil_attn_bwd_{gi}", grid=(B, NPAIR), in_specs=in_specs, out_specs=[qk, qk, qk],
        out_shape=[jax.ShapeDtypeStruct((B * S, DIL_QK), F32)] * 3,
        scratch_shapes=[pltpu.VMEM((S, LANE), BF16)] * 4 + [pltpu.VMEM((S, LANE), F32)] * 5
                       + [pltpu.VMEM((nbias, 2 * t, kw), F32)],
        input_output_aliases=aliases,
        compiler_params=_cp(),
    )(*args)


def _merge_proj_specs(ts):
    wide = lambda c0, w: pl.BlockSpec((ts, w), lambda i: (i, c0 * LANE // w))
    return [wide(CB_BZ, DIL_WIDTH), wide(CB_CZ, DIL_WIDTH)] + [wide(CB_GATE + 8 * i, D_MODEL) for i in range(3)]


def _merge_common(p_refs, bg_ref, ob_ref, og_refs, lse_refs):
    bz = p_refs[0][...].astype(F32)
    cz = p_refs[1][...].astype(F32)
    gates = [_sigmoid(p_refs[2 + i][...].astype(F32) + bg_ref[:, i * D_MODEL:(i + 1) * D_MODEL]) for i in range(3)]
    ob = ob_ref[...]
    lses = [r[...] for r in lse_refs]
    mx = jnp.maximum(jnp.maximum(lses[0], lses[1]), lses[2])
    es = [jnp.exp(v - mx) for v in lses]
    inv = 1.0 / (es[0] + es[1] + es[2])
    alphas = [e * inv for e in es]
    oc = alphas[0] * og_refs[0][...] + alphas[1] * og_refs[1][...] + alphas[2] * og_refs[2][...]
    return bz, cz, gates, ob, alphas, oc


def _merge_fwd(x, proj, b_gate, ya, ob, ogs, lses, woa, wob, woc, wo):
    T = x.shape[0]
    ts = _tile(T, 256)

    def body(x_ref, p0, p1, p2, p3, p4, bg_ref, ya_ref, ob_ref, og0, og1, og2, l0, l1, l2,
             woa_ref, wob_ref, woc_ref, wo_ref, out_ref):
        bz, cz, gates, obv, alphas, oc = _merge_common((p0, p1, p2, p3, p4), bg_ref, ob_ref, (og0, og1, og2),
                                                       (l0, l1, l2))
        yb = obv * _silu(bz)
        yc = oc * _silu(cz)
        merged = (gates[0] * _mm(ya_ref[...], woa_ref[...]) + gates[1] * _mm(yb, wob_ref[...])
                  + gates[2] * _mm(yc, woc_ref[...]))
        out_ref[...] = x_ref[...] + _mm(merged, wo_ref[...])

    def whole(r, c):
        return pl.BlockSpec((r, c), lambda i: (0, 0))

    tok = lambda w: pl.BlockSpec((ts, w), lambda i: (i, 0))
    return pl.pallas_call(
        body, name="merge_fwd", grid=(T // ts,),
        in_specs=[tok(D_MODEL)] + _merge_proj_specs(ts) + [whole(1, 3 * D_MODEL), tok(CONV_WIDTH)]
                 + [tok(DIL_WIDTH)] * 7 + [whole(CONV_WIDTH, D_MODEL)] * 3 + [whole(D_MODEL, D_MODEL)],
        out_specs=tok(D_MODEL),
        out_shape=jax.ShapeDtypeStruct((T, D_MODEL), F32),
        compiler_params=_cp(),
    )(x, *[proj] * 5, b_gate, ya, ob, *ogs, *lses, woa, wob, woc, wo)


def _merge_bwd(dout, proj, b_gate, ya, ob, ogs, lses, woa, wob, woc, wo):
    T = dout.shape[0]
    ts = _tile(T, 256)
    nt = T // ts

    def body(do_ref, p0, p1, p2, p3, p4, bg_ref, ya_ref, ob_ref, og0, og1, og2, l0, l1, l2,
             woa_ref, wob_ref, woc_ref, wo_ref,
             dp_ref, dya_ref, dob_ref, dlb_ref, dg0, dg1, dg2, dl0, dl1, dl2,
             mg_ref, dpa_ref, dpb_ref, dpc_ref, yb_ref, yc_ref, dbg_ref, st_bz, st_cz, st_gate, sems):
        step = pl.program_id(0)
        slot = step % 2

        def copies_of(s):
            return _put_copies([st_bz, st_cz, st_gate], dp_ref, sems, s % 2, pl.ds(pl.multiple_of(s * ts, ts), ts),
                               [CB_BZ * LANE, CB_CZ * LANE, CB_GATE * LANE])

        @pl.when(step >= 2)
        def _():
            for cp in copies_of(step - 2):
                cp.wait()

        bz, cz, gates, obv, alphas, oc = _merge_common((p0, p1, p2, p3, p4), bg_ref, ob_ref, (og0, og1, og2),
                                                       (l0, l1, l2))
        sb, sc = _silu(bz), _silu(cz)
        yb = obv * sb
        yc = oc * sc
        ps = [_mm(ya_ref[...], woa_ref[...]), _mm(yb, wob_ref[...]), _mm(yc, woc_ref[...])]
        mg_ref[...] = (gates[0] * ps[0] + gates[1] * ps[1] + gates[2] * ps[2]).astype(BF16)
        yb_ref[...] = yb.astype(BF16)
        yc_ref[...] = yc.astype(BF16)
        dm = _mm_nt(do_ref[...], wo_ref[...])
        dps = []
        first = pl.program_id(0) == 0
        for i, dref in enumerate((dpa_ref, dpb_ref, dpc_ref)):
            g = gates[i]
            dpi = (dm * g).astype(BF16)
            dref[...] = dpi
            dps.append(dpi)
            dgp = dm * ps[i] * g * (1.0 - g)
            st_gate[slot, :, i * D_MODEL:(i + 1) * D_MODEL] = dgp.astype(BF16)
            part = jnp.sum(dgp, axis=0, keepdims=True)

            @pl.when(first)
            def _():
                dbg_ref[:, i * D_MODEL:(i + 1) * D_MODEL] = part

            @pl.when(jnp.logical_not(first))
            def _():
                dbg_ref[:, i * D_MODEL:(i + 1) * D_MODEL] += part

        dya_ref[...] = _mm_nt(dps[0], woa_ref[...])
        dyb = _mm_nt(dps[1], wob_ref[...])
        dyc = _mm_nt(dps[2], woc_ref[...])
        st_bz[slot] = (dyb * obv * _dsilu(bz)).astype(BF16)
        st_cz[slot] = (dyc * oc * _dsilu(cz)).astype(BF16)
        for cp in copies_of(step):
            cp.start()
        dob = dyb * sb
        doc = dyc * sc
        dob_ref[...] = dob.astype(BF16)
        for c in range(NPAIR):
            cs = slice(c * LANE, (c + 1) * LANE)
            dlb_ref[:, cs] = _head_bcast_sum(dob[:, cs] * obv[:, cs])
            dd = _head_bcast_sum(doc[:, cs] * oc[:, cs])
            for a, dref, lref in zip(alphas, (dg0, dg1, dg2), (dl0, dl1, dl2)):
                dref[:, cs] = (a[:, cs] * doc[:, cs]).astype(BF16)
                lref[:, cs] = a[:, cs] * dd

        @pl.when(step == nt - 1)
        def _():
            if nt >= 2:
                for cp in copies_of(step - 1):
                    cp.wait()
            for cp in copies_of(step):
                cp.wait()

    def whole(r, c):
        return pl.BlockSpec((r, c), lambda i: (0, 0))

    tok = lambda w: pl.BlockSpec((ts, w), lambda i: (i, 0))
    sd = jax.ShapeDtypeStruct
    W = DIL_WIDTH
    return pl.pallas_call(
        body, name="merge_bwd", grid=(nt,),
        in_specs=[tok(D_MODEL)] + _merge_proj_specs(ts) + [whole(1, 3 * D_MODEL), tok(CONV_WIDTH)] + [tok(W)] * 7
                 + [whole(CONV_WIDTH, D_MODEL)] * 3 + [whole(D_MODEL, D_MODEL)],
        out_specs=[pl.BlockSpec(memory_space=pl.ANY), tok(CONV_WIDTH), tok(W), tok(W)] + [tok(W)] * 6
                  + [tok(D_MODEL)] * 4 + [tok(W), tok(W), whole(1, 3 * D_MODEL)],
        out_shape=[sd((T, PP), BF16), sd((T, CONV_WIDTH), F32), sd((T, W), BF16), sd((T, W), F32)]
                  + [sd((T, W), BF16)] * 3 + [sd((T, W), F32)] * 3
                  + [sd((T, D_MODEL), BF16)] * 4 + [sd((T, W), BF16)] * 2 + [sd((1, 3 * D_MODEL), F32)],
        scratch_shapes=[pltpu.VMEM((2, ts, W), BF16), pltpu.VMEM((2, ts, W), BF16),
                        pltpu.VMEM((2, ts, 3 * D_MODEL), BF16), pltpu.SemaphoreType.DMA((2, 3))],
        compiler_params=_cp(),
    )(dout, *[proj] * 5, b_gate, ya, ob, *ogs, *lses, woa, wob, woc, wo)


def _loss_head(y, target):
    T = y.shape[0]
    ts = _tile(T, 512)

    def body(y_ref, t_ref, d_ref, l_ref):
        e = y_ref[...] - t_ref[...]
        d_ref[...] = e * (1.0 / D_MODEL)
        l_ref[...] = jnp.zeros((1, 8, LANE), F32) + jnp.sum(e * e)

    tok = pl.BlockSpec((ts, D_MODEL), lambda i: (i, 0))
    return pl.pallas_call(
        body, name="loss_head", grid=(T // ts,), in_specs=[tok, tok],
        out_specs=[tok, pl.BlockSpec((1, 8, LANE), lambda i: (i, 0, 0))],
        out_shape=[jax.ShapeDtypeStruct((T, D_MODEL), F32), jax.ShapeDtypeStruct((T // ts, 8, LANE), F32)],
        compiler_params=_cp(),
    )(y, target)


def _my_index():
    return 4 * lax.axis_index("x") + 2 * lax.axis_index("y") + lax.axis_index("c")


def _peers():
    x, y, c = (lax.axis_index(a) for a in AXES)
    out = []
    for kk in range(1, N_DEV):
        px = 1 - x if kk & 4 else x
        py = 1 - y if kk & 2 else y
        pc = 1 - c if kk & 1 else c
        out.append(((px, py, pc), 4 * px + 2 * py + pc))
    return out


def _exchange(arrays, name, gather):
    n = len(arrays)

    def body(*refs):
        srcs, outs = refs[:n], refs[n:2 * n]
        send_sems, recv_sems, local_sems = refs[2 * n:]
        me = _my_index()
        peers = _peers()
        started = []
        for a, (src, out) in enumerate(zip(srcs, outs)):
            mine = pltpu.make_async_copy(src if gather else src.at[me], out.at[me], local_sems.at[a])
            mine.start()
            started.append(mine)
        sends = []
        for i, (pos, idx) in enumerate(peers):
            for a, (src, out) in enumerate(zip(srcs, outs)):
                cp = pltpu.make_async_remote_copy(
                    src_ref=src if gather else src.at[idx], dst_ref=out.at[me], send_sem=send_sems.at[a, i],
                    recv_sem=recv_sems.at[a, i], device_id=pos, device_id_type=pl.DeviceIdType.MESH)
                cp.start()
                sends.append(cp)
        for i, (pos, idx) in enumerate(peers):
            for a, (src, out) in enumerate(zip(srcs, outs)):
                pltpu.make_async_remote_copy(
                    src_ref=src if gather else src.at[idx], dst_ref=out.at[idx], send_sem=send_sems.at[a, i],
                    recv_sem=recv_sems.at[a, i], device_id=pos, device_id_type=pl.DeviceIdType.MESH).wait_recv()
        for cp in sends:
            cp.wait_send()
        for mine in started:
            mine.wait()

    any_space = pl.BlockSpec(memory_space=pl.ANY)
    return pl.pallas_call(
        body, name=name, in_specs=[any_space] * n, out_specs=[any_space] * n,
        out_shape=[jax.ShapeDtypeStruct(((N_DEV,) + a.shape) if gather else a.shape, a.dtype) for a in arrays],
        scratch_shapes=[pltpu.SemaphoreType.DMA((n, N_DEV - 1)), pltpu.SemaphoreType.DMA((n, N_DEV - 1)),
                        pltpu.SemaphoreType.DMA((n,))],
    )(*arrays)


N_CHIP = 4


def _chip_places():
    x, y, c = (lax.axis_index(a) for a in AXES)
    return (x, y, c), (x, y, 1 - c), [(1 - x, y, c), (x, 1 - y, c), (1 - x, 1 - y, c)]


def _index_of(pos):
    return 4 * pos[0] + 2 * pos[1] + pos[2]


def _gather_two_level(arrays, name):
    n = len(arrays)

    def body(*refs):
        srcs, outs = refs[:n], refs[n:2 * n]
        send_sems, recv_sems, local_sems = refs[2 * n:]
        me, sibling, others = _chip_places()

        def copy(a, k, block, to, src=None):
            slot = outs[a].at[_index_of(block)]
            return pltpu.make_async_remote_copy(
                src_ref=slot if src is None else src, dst_ref=slot, send_sem=send_sems.at[7 * a + k],
                recv_sem=recv_sems.at[7 * a + k], device_id=to, device_id_type=pl.DeviceIdType.MESH)

        started = []
        for a, src in enumerate(srcs):
            mine = pltpu.make_async_copy(src, outs[a].at[_index_of(me)], local_sems.at[a])
            mine.start()
            started.append(mine)
        sends = []
        for a, src in enumerate(srcs):
            sends.append(copy(a, 0, me, sibling, src))
            sends += [copy(a, 1 + j, me, chip, src) for j, chip in enumerate(others)]
        for cp in sends:
            cp.start()
        for j, chip in enumerate(others):
            for a in range(n):
                copy(a, 1 + j, chip, me).wait_recv()
                fwd = copy(a, 4 + j, chip, sibling)
                fwd.start()
                sends.append(fwd)
        for a in range(n):
            copy(a, 0, sibling, me).wait_recv()
            for j, chip in enumerate(others):
                copy(a, 4 + j, (chip[0], chip[1], sibling[2]), me).wait_recv()
        for cp in sends:
            cp.wait_send()
        for mine in started:
            mine.wait()

    any_space = pl.BlockSpec(memory_space=pl.ANY)
    return pl.pallas_call(
        body, name=name, in_specs=[any_space] * n, out_specs=[any_space] * n,
        out_shape=[jax.ShapeDtypeStruct((N_DEV,) + a.shape, a.dtype) for a in arrays],
        scratch_shapes=[pltpu.SemaphoreType.DMA((7 * n,)), pltpu.SemaphoreType.DMA((7 * n,)),
                        pltpu.SemaphoreType.DMA((n,))],
    )(*arrays)


def _sibling_swap(arrays, name):
    n = len(arrays)

    def body(*refs):
        srcs, outs = refs[:n], refs[n:2 * n]
        send_sems, recv_sems = refs[2 * n:]
        (x, y, c), sibling, _ = _chip_places()
        sends = []
        for a, (src, out) in enumerate(zip(srcs, outs)):
            for q in range(N_CHIP):
                def copy(core, a=a, q=q, src=src, out=out):
                    return pltpu.make_async_remote_copy(
                        src_ref=src.at[2 * q + core], dst_ref=out.at[q], send_sem=send_sems.at[N_CHIP * a + q],
                        recv_sem=recv_sems.at[N_CHIP * a + q], device_id=sibling, device_id_type=pl.DeviceIdType.MESH)
                mine = copy(1 - c)
                mine.start()
                sends.append((mine, copy(c)))
        for mine, arrival in sends:
            arrival.wait_recv()
            mine.wait_send()

    any_space = pl.BlockSpec(memory_space=pl.ANY)
    return pl.pallas_call(
        body, name=name, in_specs=[any_space] * n, out_specs=[any_space] * n,
        out_shape=[jax.ShapeDtypeStruct((N_CHIP,) + a.shape[1:], a.dtype) for a in arrays],
        scratch_shapes=[pltpu.SemaphoreType.DMA((N_CHIP * n,)), pltpu.SemaphoreType.DMA((N_CHIP * n,))],
    )(*arrays)


def _chip_pair_sum(part, got, name):
    R, C = part.shape[1:]
    tr = R
    while tr * C * part.dtype.itemsize > REDUCE_BLOCK_BYTES // 4 and tr % 32 == 0:
        tr //= 2
    c = lax.axis_index("c")

    def body(c_ref, p_ref, g_ref, o_ref):
        del c_ref
        o_ref[...] = (p_ref[...].astype(F32) + g_ref[...].astype(F32)).astype(o_ref.dtype)

    return pl.pallas_call(
        body, name=name, grid_spec=pltpu.PrefetchScalarGridSpec(
            num_scalar_prefetch=1, grid=(N_CHIP, R // tr),
            in_specs=[pl.BlockSpec((None, tr, C), lambda q, i, cr: (2 * q + cr[0], i, 0)),
                      pl.BlockSpec((None, tr, C), lambda q, i, cr: (q, i, 0))],
            out_specs=pl.BlockSpec((None, tr, C), lambda q, i, cr: (q, i, 0))),
        out_shape=jax.ShapeDtypeStruct((N_CHIP, R, C), part.dtype),
        compiler_params=_cp(),
    )(jnp.reshape(c, (1,)).astype(jnp.int32), part, got)


def _peer_count(mode):
    return N_CHIP - 1 if mode == "chips" else N_DEV - 1


def _remote_copies(srcs, lands, send_sems, recv_sems, mode):
    if mode == "chips":
        (x, y, _), _, others = _chip_places()
        my_slot, peers = 2 * x + y, [(chip, 2 * chip[0] + chip[1]) for chip in others]
    else:
        my_slot, peers = _my_index(), _peers()
    out = []
    for i, (pos, idx) in enumerate(peers):
        for a, (src, land) in enumerate(zip(srcs, lands)):
            def copy(slot, a=a, src=src, land=land, i=i, pos=pos, idx=idx):
                return pltpu.make_async_remote_copy(
                    src_ref=src if mode == "gather" else src.at[idx], dst_ref=land.at[slot],
                    send_sem=send_sems.at[a * len(peers) + i], recv_sem=recv_sems.at[a * len(peers) + i],
                    device_id=pos, device_id_type=pl.DeviceIdType.MESH)
            out.append((copy(my_slot), copy(idx)))
    return out


def _exchange_start(arrays, name, mode):
    n = len(arrays)
    hbm = pl.BlockSpec(memory_space=pltpu.HBM)
    sem = pl.BlockSpec(memory_space=pltpu.SEMAPHORE)
    lands = [lax.empty(((N_DEV,) + a.shape) if mode == "gather" else a.shape, a.dtype) for a in arrays]

    def body(*refs):
        srcs, lands_ = refs[:n], refs[n:2 * n]
        send_sems, recv_sems = refs[2 * n:2 * n + 2]
        for mine, _ in _remote_copies(srcs, lands_, send_sems, recv_sems, mode):
            mine.start()
        refs[-1][...] = jnp.zeros_like(refs[-1])

    sems = pltpu.SemaphoreType.DMA((n * _peer_count(mode),))
    buffers = [pltpu.HBM(a.shape, a.dtype) for a in list(arrays) + lands]
    res = pl.pallas_call(
        body, name=name, in_specs=[hbm] * (2 * n), out_specs=[sem, sem] + [hbm] * (2 * n) + [pl.BlockSpec(memory_space=pltpu.VMEM)],
        out_shape=[sems, sems] + buffers + [jax.ShapeDtypeStruct((8, LANE), F32)],
        input_output_aliases={i: 2 + i for i in range(2 * n)},
        compiler_params=pltpu.CompilerParams(has_side_effects=pltpu.SideEffectType.DATAFLOW_SIDE_EFFECTING),
    )(*[pltpu.with_memory_space_constraint(a, pltpu.HBM) for a in list(arrays) + lands])
    return (res[0], res[1], res[2:2 + n], res[2 + n:2 + 2 * n]), res[-1]


def _exchange_wait(handle, after, name, mode):
    send_sems, recv_sems, srcs, lands = handle
    n = len(srcs)
    hbm = pl.BlockSpec(memory_space=pltpu.HBM)
    sem = pl.BlockSpec(memory_space=pltpu.SEMAPHORE)

    def body(*refs):
        for mine, arrival in _remote_copies(refs[:n], refs[n:2 * n], refs[2 * n], refs[2 * n + 1], mode):
            mine.wait_send()
            arrival.wait_recv()

    res = pl.pallas_call(
        body, name=name, in_specs=[hbm] * (2 * n) + [sem, sem, pl.BlockSpec(memory_space=pl.ANY)],
        out_specs=[hbm] * (2 * n), out_shape=[pltpu.HBM(a.shape, a.dtype) for a in list(srcs) + list(lands)],
        input_output_aliases={i: i for i in range(2 * n)},
        compiler_params=pltpu.CompilerParams(has_side_effects=pltpu.SideEffectType.DATAFLOW_SIDE_EFFECTING),
    )(*srcs, *lands, send_sems, recv_sems, after)
    return res[n:]


def _own_slot(land, mine, slot=None):
    slot = _my_index() if slot is None else slot
    return lax.dynamic_update_slice(land, mine, (slot,) + (0,) * (land.ndim - 1))


def _adamw(w, g, m, v):
    m = ADAM_B1 * m + (1.0 - ADAM_B1) * g
    v = ADAM_B2 * v + (1.0 - ADAM_B2) * (g * g)
    m_hat = m / (1.0 - ADAM_B1 ** ADAM_STEP)
    v_hat = v / (1.0 - ADAM_B2 ** ADAM_STEP)
    delta = -ADAM_LR * (m_hat / (jnp.sqrt(v_hat) + ADAM_EPS) + ADAM_WD * w)
    return delta, m, v


def _reduce_adamw(parts, w, m, v, name):
    nparts = len(parts)
    R, C = parts[0].shape[1:]
    tr = R
    while N_DEV * tr * C * parts[0].dtype.itemsize > REDUCE_BLOCK_BYTES and tr % 32 == 0:
        tr //= 2
    steps = R // tr

    def body(*refs):
        w_ref, m_ref, v_ref, g_ref, d_ref, nm_ref, nv_ref = refs[nparts:]
        for k, p_ref in enumerate(refs[:nparts]):
            @pl.when(pl.program_id(0) // steps == k)
            def _():
                g = p_ref[0].astype(F32)
                for s in range(1, p_ref.shape[0]):
                    g = g + p_ref[s].astype(F32)
                g_ref[...] = g
                d_ref[...], nm_ref[...], nv_ref[...] = _adamw(w_ref[...], g, m_ref[...], v_ref[...])

    def part_spec(k):
        return pl.BlockSpec((parts[k].shape[0], tr, C), lambda i: (0, jnp.clip(i - k * steps, 0, steps - 1), 0))

    row = pl.BlockSpec((tr, C), lambda i: (i, 0))
    return pl.pallas_call(
        body, name=name, grid=(nparts * steps,),
        in_specs=[part_spec(k) for k in range(nparts)] + [row, row, row],
        out_specs=[row] * 4, out_shape=[jax.ShapeDtypeStruct((nparts * R, C), F32)] * 4,
        compiler_params=_cp(),
    )(*parts, w, m, v)


BIG = ("w_in", "w_uq", "w_ukv", "w_out_a", "w_out_b", "w_out_c", "w_o")
SMALL = ("norm_g", "b_gate", "conv_w", "conv_b", "q_a_norm_g", "kv_a_norm_g", "mla_q_norm_g", "mla_k_norm_g",
         "dil_q_norm_g", "dil_k_norm_g")
PACK_ROWS = 128
REDUCE_BLOCK_BYTES = 6 * 1024 * 1024


def _pack_local(tensors):
    flat = jnp.concatenate([t.reshape(-1) for t in tensors])
    pad = (-flat.shape[0]) % (PACK_ROWS * LANE)
    return jnp.concatenate([flat, jnp.zeros((pad,), flat.dtype)]).reshape(-1, LANE)


def _unpack_local(rows, like):
    flat = rows.reshape(-1)
    out, off = [], 0
    for t in like:
        out.append(flat[off:off + t.size].reshape(t.shape))
        off += t.size
    return out


def _cols_to_slots(a):
    k = a.shape[0]
    return a.reshape(k, N_DEV, -1).transpose(1, 0, 2)


def _slots_to_cols(s):
    return s.transpose(1, 0, 2).reshape(s.shape[1], -1)


def _rope_tables(S):
    inv = ROPE_THETA ** (-jnp.arange(0, MLA_ROPE, 2, dtype=F32) / MLA_ROPE)
    ang = jnp.arange(S, dtype=F32)[:, None] * inv[None, :]
    cos, sin = jnp.cos(ang), jnp.sin(ang)
    one = jnp.ones((S, MLA_NOPE), F32)
    z16, z32, z64 = (jnp.zeros((S, n), F32) for n in (16, 32, 64))
    cosp = jnp.concatenate([one, cos, cos, jnp.ones((S, 32), F32)], axis=1)
    sa = jnp.concatenate([z64, -sin, z16, z32], axis=1)
    sb = jnp.concatenate([z64, z16, sin, z32], axis=1)
    return cosp, sa, sb


def _alibi_slopes():
    n = DIL_GROUPS * DIL_HEADS
    m = 2.0 ** (-8.0 * jnp.arange(1, n + 1, dtype=F32) / n)
    return m.reshape(DIL_GROUPS, NPAIR, 2)


def _pad_slots(s):
    n, k, c = s.shape
    return _slots_to_cols(jnp.concatenate([s, jnp.zeros((n, k, LANE - c), s.dtype)], axis=2))


def _layer_params(gw, small, l):
    p = {}
    p["wp"] = _pad_columns(gw["w_in"])
    p["norm_g"] = small["norm_g"][l][None]
    p["b_gate"] = small["b_gate"][l][None]
    p["conv_w"] = gw["conv_w"].transpose(1, 0, 2).reshape(CONV_K, CONV_WIDTH)
    p["conv_b"] = small["conv_b"][l][None]
    p["gq"] = small["q_a_norm_g"][l][None]
    p["gkv"] = small["kv_a_norm_g"][l][None]
    p["wuqp"] = _pad_slots(gw["w_uq"])
    kv = gw["w_ukv"]
    p["wkp"] = _pad_slots(kv[:, :, :MLA_NOPE])
    p["wv"] = kv[:, :, MLA_NOPE:].transpose(1, 0, 2).reshape(MLA_KV_LORA, MLA_HEADS * MLA_V)
    zpad = jnp.zeros((1, LANE - MLA_QK), F32)
    p["gmq"] = jnp.concatenate([small["mla_q_norm_g"][l][None], zpad], axis=1)
    p["gmk"] = jnp.concatenate([small["mla_k_norm_g"][l][None], zpad], axis=1)
    tile = lambda g: jnp.broadcast_to(g[:, None, :], (DIL_GROUPS, DIL_HEADS, DIL_HEAD_DIM)).reshape(1, DIL_QK)
    p["gdq"] = tile(small["dil_q_norm_g"][l])
    p["gdk"] = tile(small["dil_k_norm_g"][l])
    p["woa"], p["wob"], p["woc"] = (_slots_to_cols(gw[n]) for n in ("w_out_a", "w_out_b", "w_out_c"))
    p["wo"] = gw["w_o"].reshape(D_MODEL, D_MODEL)
    return p


def _layer_fwd(x, p, tabs, slopes, B, S):
    proj, ht = _inproj_fwd(x, p["norm_g"], p["wp"])
    ya = _mixa_fwd(proj, p["conv_w"], p["conv_b"], B, S)
    q, k, v = _mla_prep_fwd(proj, p["gq"], p["gkv"], p["wuqp"], p["wkp"], p["wv"], p["gmq"], p["gmk"], *tabs, S)
    ob, lse_b = _mla_attn_fwd(q, k, v, B, S)
    qn, kn = _dil_prep_fwd(proj, p["gdq"], p["gdk"])
    ogs, lses = [], []
    for gi in range(DIL_GROUPS):
        o, lse = _dil_attn_fwd(gi, slopes[gi], qn, kn, proj, B, S)
        ogs.append(o)
        lses.append(lse)
    out = _merge_fwd(x, proj, p["b_gate"], ya, ob, ogs, lses, p["woa"], p["wob"], p["woc"], p["wo"])
    saved = dict(x=x, proj=proj, ht=ht, ya=ya, q=q, k=k, v=v, ob=ob, lse_b=lse_b, qn=qn, kn=kn, ogs=ogs, lses=lses)
    return out, saved


def _layer_bwd(dout, sv, p, tabs, slopes, B, S, big_ready=None):
    proj = sv["proj"]
    (dproj, dya, dob, dlb, dg0, dg1, dg2, dl0, dl1, dl2, merged, dpa, dpb, dpc, yb, yc, dbg) = _merge_bwd(
        dout, proj, p["b_gate"], sv["ya"], sv["ob"], sv["ogs"], sv["lses"], p["woa"], p["wob"], p["woc"], p["wo"])
    g = {}
    g["w_o"] = _matmul_tn(merged, dout, "dw_o").reshape(N_DEV, D_MODEL // N_DEV, D_MODEL)
    g["w_out_a"] = _cols_to_slots(_matmul_tn(sv["ya"], dpa, "dw_out_a"))
    g["w_out_b"] = _cols_to_slots(_matmul_tn(yb, dpb, "dw_out_b"))
    g["w_out_c"] = _cols_to_slots(_matmul_tn(yc, dpc, "dw_out_c"))
    g["b_gate"] = dbg[0]
    dproj, st = _mixa_bwd(dproj, dya, proj, p["conv_w"], p["conv_b"], B, S)
    g["conv_w"] = st[0:CONV_K]
    g["conv_b"] = st[CONV_K]
    dq, dk, dv = _mla_attn_bwd(sv["q"], sv["k"], sv["v"], dob, sv["lse_b"], dlb, B, S)
    dproj, dwuqp, dwkp, dwv, dgq, dgkv, dgmq, dgmk = _mla_prep_bwd(
        dproj, dq, dk, dv, proj, p["gq"], p["gkv"], p["wuqp"], p["wkp"], p["wv"], p["gmq"], p["gmk"], *tabs, S)
    g["w_uq"] = _cols_to_slots(dwuqp)[:, :, :MLA_QK]
    g["w_ukv"] = jnp.concatenate([_cols_to_slots(dwkp)[:, :, :MLA_NOPE], _cols_to_slots(dwv)], axis=2)
    g["q_a_norm_g"], g["kv_a_norm_g"] = dgq[0], dgkv[0]
    g["mla_q_norm_g"], g["mla_k_norm_g"] = dgmq[0, :MLA_QK], dgmk[0, :MLA_QK]
    dqkv = None
    for gi, (dog, dlg) in enumerate(((dg0, dl0), (dg1, dl1), (dg2, dl2))):
        dqkv = _dil_attn_bwd(gi, slopes[gi], sv["qn"], sv["kn"], proj, dog, sv["lses"][gi], dlg, dqkv, B, S)
    dproj, dgdq, dgdk = _dil_prep_bwd(dproj, *dqkv, proj, p["gdq"], p["gdk"])
    g["dil_q_norm_g"] = dgdq.reshape(DIL_GROUPS, DIL_HEADS, DIL_HEAD_DIM).sum(axis=1)
    g["dil_k_norm_g"] = dgdk.reshape(DIL_GROUPS, DIL_HEADS, DIL_HEAD_DIM).sum(axis=1)
    g["w_in"] = _unpad_columns(_matmul_nn(sv["ht"], dproj, "dw_in"))
    token = None if big_ready is None else big_ready(g)
    dx, dng = _inproj_bwd_x(dproj, p["wp"], sv["x"], _after(token, p["norm_g"]), dout)
    g["norm_g"] = dng[0]
    return dx, g


def _after(token, a):
    return a if token is None else a + token[0:1, 0:1]


def _local_step(x, target, small, B, S, weights_of, grads_out, big_ready=None):
    tabs = _rope_tables(S)
    sl = _alibi_slopes()
    slopes = [sl[gi] * float(DIL_PATTERNS[gi][1]) for gi in range(DIL_GROUPS)]
    params, saved = [], []
    for l in range(DEPTH):
        gw, token = weights_of(l, x)
        p = _layer_params(gw, small, l)
        p["norm_g"] = _after(token, p["norm_g"])
        x, sv = _layer_fwd(x, p, tabs, slopes, B, S)
        params.append(p)
        saved.append(sv)
    dout, lparts = _loss_head(x, target)
    sq = jnp.sum(lparts[:, 0, 0])
    token = None
    for l in reversed(range(DEPTH)):
        p = dict(params[l], b_gate=_after(token, params[l]["b_gate"]))
        ready = None if big_ready is None else (lambda g, l=l: big_ready(l, g))
        dout, g = _layer_bwd(dout, saved[l], p, tabs, slopes, B, S, ready)
        token = grads_out(l, g, dout)
    return sq, dout


def kernel(x, norm_g, w_in, b_gate, conv_w, conv_b, q_a_norm_g, w_uq, kv_a_norm_g, w_ukv, mla_q_norm_g, mla_k_norm_g, dil_q_norm_g, dil_k_norm_g, w_out_a, w_out_b, w_out_c, w_o, loss_target, m_norm_g, m_w_in, m_b_gate, m_conv_w, m_conv_b, m_q_a_norm_g, m_w_uq, m_kv_a_norm_g, m_w_ukv, m_mla_q_norm_g, m_mla_k_norm_g, m_dil_q_norm_g, m_dil_k_norm_g, m_w_out_a, m_w_out_b, m_w_out_c, m_w_o, v_norm_g, v_w_in, v_b_gate, v_conv_w, v_conv_b, v_q_a_norm_g, v_w_uq, v_kv_a_norm_g, v_w_ukv, v_mla_q_norm_g, v_mla_k_norm_g, v_dil_q_norm_g, v_dil_k_norm_g, v_w_out_a, v_w_out_b, v_w_out_c, v_w_o):
    names = ("norm_g", "w_in", "b_gate", "conv_w", "conv_b", "q_a_norm_g", "w_uq", "kv_a_norm_g", "w_ukv",
             "mla_q_norm_g", "mla_k_norm_g", "dil_q_norm_g", "dil_k_norm_g", "w_out_a", "w_out_b", "w_out_c", "w_o")
    w = dict(zip(names, (norm_g, w_in, b_gate, conv_w, conv_b, q_a_norm_g, w_uq, kv_a_norm_g, w_ukv, mla_q_norm_g,
                         mla_k_norm_g, dil_q_norm_g, dil_k_norm_g, w_out_a, w_out_b, w_out_c, w_o)))
    m = dict(zip(names, (m_norm_g, m_w_in, m_b_gate, m_conv_w, m_conv_b, m_q_a_norm_g, m_w_uq, m_kv_a_norm_g, m_w_ukv,
                         m_mla_q_norm_g, m_mla_k_norm_g, m_dil_q_norm_g, m_dil_k_norm_g, m_w_out_a, m_w_out_b,
                         m_w_out_c, m_w_o)))
    v = dict(zip(names, (v_norm_g, v_w_in, v_b_gate, v_conv_w, v_conv_b, v_q_a_norm_g, v_w_uq, v_kv_a_norm_g, v_w_ukv,
                         v_mla_q_norm_g, v_mla_k_norm_g, v_dil_q_norm_g, v_dil_k_norm_g, v_w_out_a, v_w_out_b,
                         v_w_out_c, v_w_o)))
    B, S, _ = x.shape
    me = _my_index()
    cshard = CONV_WIDTH // N_DEV

    shards = [[w[n][l].astype(BF16) for n in BIG] for l in range(DEPTH)]
    state = {}

    def weights_of(l, after):
        if l == 0:
            got = _gather_two_level(shards[0] + [conv_w], "all_gather_weights_0")
            state["gather"], token = _exchange_start(shards[1], "all_gather_weights_1_start", "gather")
            state["conv_w"] = got[-1]
        else:
            landed = _exchange_wait(state["gather"], after, "all_gather_weights_1_wait", "gather")
            got, token = [_own_slot(a, s[None]) for a, s in zip(landed, shards[1])], None
        gw = dict(zip(BIG, got))
        gw["conv_w"] = state["conv_w"][:, l]
        return gw, token

    recv, small_parts = {}, {}
    my_chip = 2 * lax.axis_index("x") + lax.axis_index("y")

    def big_ready(l, g):
        send = [g[n].astype(BF16) for n in BIG]
        if l == DEPTH - 1:
            state["scatter"], token = _exchange_start(send, "exchange_weight_grads_1_start", "scatter")
        else:
            swapped = _sibling_swap(send, "exchange_weight_grads_0_sibling")
            send = [_chip_pair_sum(s, t, "chip_pair_sum_" + n) for n, s, t in zip(BIG, send, swapped)]
            state["chips"], token = _exchange_start(send, "exchange_weight_grads_0_start", "chips")
        state["sent", l] = send
        return token

    def grads_out(l, g, after):
        small_parts[l] = [g[n] for n in SMALL]
        if l == DEPTH - 1:
            return None
        for k, key, mode, slot in ((DEPTH - 1, "scatter", "scatter", me), (0, "chips", "chips", my_chip)):
            landed = _exchange_wait(state[key], after, f"exchange_weight_grads_{k}_wait", mode)
            mine = [lax.dynamic_slice_in_dim(s, slot, 1, axis=0) for s in state["sent", k]]
            recv[k] = [_own_slot(a, s, slot) for a, s in zip(landed, mine)]
        return None

    sq, grad_x = _local_step(x.reshape(B * S, D_MODEL), loss_target.reshape(B * S, D_MODEL), w, B, S,
                             weights_of, grads_out, big_ready)
    loss = lax.psum(sq * (0.5 / D_MODEL), AXES)

    res = {}
    for i, n in enumerate(BIG):
        rows = lambda a: a.reshape(-1, a.shape[-1])
        outs = _reduce_adamw([recv[l][i] for l in range(DEPTH)], rows(w[n]), rows(m[n]), rows(v[n]),
                             "reduce_adamw_" + n)
        res[n] = tuple(a.reshape(w[n].shape) for a in outs)
    part = {n: jnp.stack([small_parts[l][i] for l in range(DEPTH)]) for i, n in enumerate(SMALL)}

    def widen(t):
        return lax.dynamic_update_slice(jnp.zeros((DEPTH, CONV_K, CONV_WIDTH), F32), t, (0, 0, me * cshard))

    small_like = [part[n] for n in SMALL]
    pick = lambda d: [widen(d[n]) if n == "conv_w" else d[n] for n in SMALL]
    parts, = _exchange([_pack_local(small_like)], "all_gather_small_grads", gather=True)
    gs, ds, ms, vs = _reduce_adamw([parts], _pack_local(pick(w)), _pack_local(pick(m)), _pack_local(pick(v)),
                                   "reduce_adamw_small")
    for n, t in zip(SMALL, zip(*(_unpack_local(a, small_like) for a in (gs, ds, ms, vs)))):
        if n == "conv_w":
            t = tuple(lax.dynamic_slice(a, (0, 0, me * cshard), (DEPTH, CONV_K, cshard)) for a in t)
        res[n] = t

    out = [loss, grad_x.reshape(B, S, D_MODEL)]
    for i in range(4):
        out += [res[n][i] for n in names]
    return tuple(out)
```

```python
import jax
import jax.numpy as jnp
from jax import lax
from jax.experimental import pallas as pl
from jax.experimental.pallas import tpu as pltpu

F32 = jnp.float32
BF16 = jnp.bfloat16

D_MODEL = 1024
DEPTH = 2
CONV_WIDTH = 512
CONV_K = 3
MLA_HEADS = 8
MLA_Q_LORA = 256
MLA_KV_LORA = 128
MLA_NOPE = 64
MLA_ROPE = 32
MLA_V = 64
MLA_QK = MLA_NOPE + MLA_ROPE
ROPE_THETA = 10000.0
DIL_PATTERNS = ((128, 1), (512, 4), (2048, 16))
DIL_GROUPS = 3
DIL_HEADS = 8
DIL_HEAD_DIM = 64
DIL_WIDTH = DIL_HEADS * DIL_HEAD_DIM
DIL_QK = DIL_GROUPS * DIL_WIDTH
EPS = 1e-6
N_IN = 11168

ADAM_LR = 0.001
ADAM_B1 = 0.9
ADAM_B2 = 0.999
ADAM_EPS = 1e-08
ADAM_WD = 0.01
ADAM_STEP = 10

N_DEV = 8
AXES = ("x", "y", "c")
LANE = 128
HALF = 64
NPAIR = 4

CB_AB, CB_AC, CB_AX, CB_AZ = 0, 4, 8, 12
CB_CQ, CB_CKV, CB_KPE = 16, 18, 19
CB_BZ = 20
CB_DQ, CB_DK, CB_DV = 24, 36, 48
CB_CZ, CB_GATE = 60, 64
NCB = 88
PP = NCB * LANE
KPE_END = CB_KPE * LANE + MLA_ROPE
SHARD_COLS = N_IN // N_DEV
NEG = -1e30
VMEM_LIMIT = 56 * 1024 * 1024


def _pad_columns(shards):
    parts = []
    for p in range(N_DEV):
        cut = min(max(KPE_END - p * SHARD_COLS, 0), SHARD_COLS)
        if 0 < cut < SHARD_COLS:
            parts += [shards[p, :, :cut], jnp.zeros((shards.shape[1], LANE - MLA_ROPE), shards.dtype), shards[p, :, cut:]]
        else:
            parts.append(shards[p])
    return jnp.concatenate(parts, axis=1)


def _unpad_columns(wp):
    def columns(a, b):
        gap = LANE - MLA_ROPE
        if b <= KPE_END:
            return wp[:, a:b]
        if a >= KPE_END:
            return wp[:, a + gap:b + gap]
        return jnp.concatenate([wp[:, a:KPE_END], wp[:, KPE_END + gap:b + gap]], axis=1)

    return jnp.stack([columns(p * SHARD_COLS, (p + 1) * SHARD_COLS) for p in range(N_DEV)])


def _put_copies(stages, dst_ref, sems, slot, rows, cols):
    return [pltpu.make_async_copy(st.at[slot], dst_ref.at[rows, pl.ds(c0, st.shape[-1])], sems.at[slot, k])
            for k, (st, c0) in enumerate(zip(stages, cols))]


def _put_pipeline(step, nsteps, copies_of, fill):
    @pl.when(step >= 2)
    def _():
        for cp in copies_of(step - 2):
            cp.wait()

    fill(step % 2)
    for cp in copies_of(step):
        cp.start()

    @pl.when(step == nsteps - 1)
    def _():
        if nsteps >= 2:
            for cp in copies_of(step - 1):
                cp.wait()
        for cp in copies_of(step):
            cp.wait()


def _cp():
    return pltpu.CompilerParams(vmem_limit_bytes=VMEM_LIMIT)


def _rstd(x, n):
    return lax.rsqrt(jnp.sum(x * x, axis=-1, keepdims=True) * (1.0 / n) + EPS)


def _sigmoid(z):
    return 1.0 / (1.0 + jnp.exp(-z))


def _silu(z):
    return z * _sigmoid(z)


def _dsilu(z):
    s = _sigmoid(z)
    return s * (1.0 + z * (1.0 - s))


def _mm(a, b):
    return jnp.dot(a.astype(BF16), b.astype(BF16), preferred_element_type=F32)


def _mm_nt(a, b):
    return lax.dot_general(a.astype(BF16), b.astype(BF16), (((1,), (1,)), ((), ())), preferred_element_type=F32)


def _mm_tn(a, b):
    return lax.dot_general(a.astype(BF16), b.astype(BF16), (((0,), (0,)), ((), ())), preferred_element_type=F32)


def _lane_lo(shape):
    return lax.broadcasted_iota(jnp.int32, shape, len(shape) - 1) < HALF


def _head_bcast_sum(x, terms=3):
    w = x.shape[-1]
    same = (lax.broadcasted_iota(jnp.int32, (w, w), 0) // HALF) == (lax.broadcasted_iota(jnp.int32, (w, w), 1) // HALF)
    ones = jnp.where(same, 1.0, 0.0).astype(jnp.bfloat16)
    total = None
    for _ in range(terms):
        term = x.astype(jnp.bfloat16)
        x = x - term.astype(F32)
        part = jnp.dot(term, ones, preferred_element_type=F32)
        total = part if total is None else total + part
    return total


def _rope(t, cos, sa, sb):
    return t * cos + pltpu.roll(t, LANE - 16, axis=1) * sa + pltpu.roll(t, 16, axis=1) * sb


def _rope_t(d, cos, sa, sb):
    return d * cos + pltpu.roll(d * sa, 16, axis=1) + pltpu.roll(d * sb, LANE - 16, axis=1)


def _shift_down(u, k):
    rows = lax.broadcasted_iota(jnp.int32, u.shape, 0)
    return jnp.where(rows >= k, pltpu.roll(u, k, axis=0), 0.0)


def _shift_up(u, k):
    n = u.shape[0]
    rows = lax.broadcasted_iota(jnp.int32, u.shape, 0)
    return jnp.where(rows < n - k, pltpu.roll(u, n - k, axis=0), 0.0)


def _tile(n, want):
    t = min(n, want)
    assert n % t == 0, (n, want)
    return t


def _inproj_fwd(x, g, wp):
    T = x.shape[0]
    tm, tn = _tile(T, 2048), 512

    def body(x_ref, g_ref, w_ref, proj_ref, ht_ref, h_ref):
        @pl.when(pl.program_id(1) == 0)
        def _():
            n = min(tm, 512)
            for r0 in range(0, tm, n):
                xv = x_ref[r0:r0 + n, :]
                h = xv * _rstd(xv, D_MODEL) * g_ref[...]
                h_ref[r0:r0 + n, :] = h.astype(BF16)
                ht_ref[:, r0:r0 + n] = h.T.astype(BF16)

        proj_ref[...] = jnp.dot(h_ref[...], w_ref[...], preferred_element_type=F32).astype(BF16)

    return pl.pallas_call(
        body, name="inproj_fwd", grid=(T // tm, PP // tn),
        in_specs=[pl.BlockSpec((tm, D_MODEL), lambda i, j: (i, 0)),
                  pl.BlockSpec((1, D_MODEL), lambda i, j: (0, 0)),
                  pl.BlockSpec((D_MODEL, tn), lambda i, j: (0, j))],
        out_specs=[pl.BlockSpec((tm, tn), lambda i, j: (i, j)),
                   pl.BlockSpec((D_MODEL, tm), lambda i, j: (0, i))],
        out_shape=[jax.ShapeDtypeStruct((T, PP), BF16), jax.ShapeDtypeStruct((D_MODEL, T), BF16)],
        scratch_shapes=[pltpu.VMEM((tm, D_MODEL), BF16)],
        compiler_params=_cp(),
    )(x, g, wp)


def _matmul_nn(at, b, name):
    K, T = at.shape
    N = b.shape[1]
    tt, tn = _tile(T, 1024), _tile(N, 2816)
    nk = T // tt

    def body(a_ref, b_ref, o_ref, acc_ref):
        k = pl.program_id(1)

        @pl.when(k == 0)
        def _():
            acc_ref[...] = jnp.zeros_like(acc_ref)

        acc_ref[...] += jnp.dot(a_ref[...], b_ref[...], preferred_element_type=F32)

        @pl.when(k == nk - 1)
        def _():
            o_ref[...] = acc_ref[...].astype(BF16)

    return pl.pallas_call(
        body, name=name, grid=(N // tn, nk),
        in_specs=[pl.BlockSpec((K, tt), lambda j, k: (0, k)),
                  pl.BlockSpec((tt, tn), lambda j, k: (k, j))],
        out_specs=pl.BlockSpec((K, tn), lambda j, k: (0, j)),
        out_shape=jax.ShapeDtypeStruct((K, N), BF16),
        scratch_shapes=[pltpu.VMEM((K, tn), F32)],
        compiler_params=_cp(),
    )(at, b)


def _matmul_tn(a, b, name):
    T, K = a.shape
    N = b.shape[1]
    tt, tn = _tile(T, 512), _tile(N, 1024)

    def body(a_ref, b_ref, o_ref):
        @pl.when(pl.program_id(1) == 0)
        def _():
            o_ref[...] = jnp.zeros_like(o_ref)

        o_ref[...] += _mm_tn(a_ref[...], b_ref[...])

    return pl.pallas_call(
        body, name=name, grid=(N // tn, T // tt),
        in_specs=[pl.BlockSpec((tt, K), lambda j, k: (k, 0)),
                  pl.BlockSpec((tt, tn), lambda j, k: (k, j))],
        out_specs=pl.BlockSpec((K, tn), lambda j, k: (0, j)),
        out_shape=jax.ShapeDtypeStruct((K, N), F32),
        compiler_params=_cp(),
    )(a, b)


def _inproj_bwd_x(dproj, wp, x, g, dout):
    T = x.shape[0]
    tm, tk = _tile(T, 1024), 1024
    nk = PP // tk

    def body(dp_ref, w_ref, x_ref, g_ref, do_ref, dx_ref, dg_ref, acc_ref):
        i, k = pl.program_id(0), pl.program_id(1)

        @pl.when(k == 0)
        def _():
            acc_ref[...] = jnp.zeros_like(acc_ref)

        @pl.when((k == 0) & (i == 0))
        def _():
            dg_ref[...] = jnp.zeros_like(dg_ref)

        acc_ref[...] += _mm_nt(dp_ref[...], w_ref[...])

        @pl.when(k == nk - 1)
        def _():
            dh = acc_ref[...]
            xv = x_ref[...]
            r = _rstd(xv, D_MODEL)
            gy = dh * g_ref[...]
            dot = jnp.sum(xv * gy, axis=-1, keepdims=True) * (1.0 / D_MODEL)
            dx_ref[...] = do_ref[...] + r * gy - xv * (r * r * r) * dot
            dg_ref[...] += jnp.sum(dh * xv * r, axis=0, keepdims=True)

    return pl.pallas_call(
        body, name="inproj_bwd_x", grid=(T // tm, nk),
        in_specs=[pl.BlockSpec((tm, tk), lambda i, k: (i, k)),
                  pl.BlockSpec((D_MODEL, tk), lambda i, k: (0, k)),
                  pl.BlockSpec((tm, D_MODEL), lambda i, k: (i, 0)),
                  pl.BlockSpec((1, D_MODEL), lambda i, k: (0, 0)),
                  pl.BlockSpec((tm, D_MODEL), lambda i, k: (i, 0))],
        out_specs=[pl.BlockSpec((tm, D_MODEL), lambda i, k: (i, 0)),
                   pl.BlockSpec((1, D_MODEL), lambda i, k: (0, 0))],
        out_shape=[jax.ShapeDtypeStruct((T, D_MODEL), F32), jax.ShapeDtypeStruct((1, D_MODEL), F32)],
        scratch_shapes=[pltpu.VMEM((tm, D_MODEL), F32)],
        compiler_params=_cp(),
    )(dproj, wp, x, g, dout)


A_SEGS = (CB_AB, CB_AC, CB_AX, CB_AZ)


def _mixa_fwd(proj, cw, cb, B, S):
    nc = CONV_WIDTH // LANE

    def body(ab_ref, ac_ref, ax_ref, az_ref, cw_ref, cb_ref, y_ref):
        ab, ac, ax, az = (r[...].astype(F32) for r in (ab_ref, ac_ref, ax_ref, az_ref))
        u = ac * ax
        conv = cb_ref[...] + cw_ref[0:1, :] * _shift_down(u, 2) + cw_ref[1:2, :] * _shift_down(u, 1) + cw_ref[2:3, :] * u
        y_ref[...] = (ab * conv * _silu(az)).astype(BF16)

    return pl.pallas_call(
        body, name="mixa_fwd", grid=(B, nc),
        in_specs=[pl.BlockSpec((S, LANE), lambda b, j, c0=c0: (b, c0 + j)) for c0 in A_SEGS]
                 + [pl.BlockSpec((CONV_K, LANE), lambda b, j: (0, j)),
                    pl.BlockSpec((1, LANE), lambda b, j: (0, j))],
        out_specs=pl.BlockSpec((S, LANE), lambda b, j: (b, j)),
        out_shape=jax.ShapeDtypeStruct((B * S, CONV_WIDTH), BF16),
        compiler_params=_cp(),
    )(proj, proj, proj, proj, cw, cb)


def _mixa_bwd(dproj, dy, proj, cw, cb, B, S):
    nc = CONV_WIDTH // LANE

    def body(dpin_ref, dy_ref, ab_ref, ac_ref, ax_ref, az_ref, cw_ref, cb_ref, dp_ref, st_ref, stage, sems):
        del dpin_ref
        j, b = pl.program_id(0), pl.program_id(1)
        ab, ac, ax, az = (r[...].astype(F32) for r in (ab_ref, ac_ref, ax_ref, az_ref))
        u = ac * ax
        u1, u2 = _shift_down(u, 1), _shift_down(u, 2)
        w0, w1, w2 = cw_ref[0:1, :], cw_ref[1:2, :], cw_ref[2:3, :]
        conv = cb_ref[...] + w0 * u2 + w1 * u1 + w2 * u
        s = _silu(az)
        d = dy_ref[...]
        dconv = d * ab * s
        du = w2 * dconv + w1 * _shift_up(dconv, 1) + w0 * _shift_up(dconv, 2)
        grads = (d * conv * s, du * ax, du * ac, d * ab * conv * _dsilu(az))

        def fill(slot):
            for k, v in enumerate(grads):
                stage[slot, k] = v.astype(BF16)

        def copies_of(step):
            sj, sb = step // B, step % B
            return _put_copies([stage.at[:, k] for k in range(4)], dp_ref, sems, step % 2,
                               pl.ds(pl.multiple_of(sb * S, S), S),
                               [pl.multiple_of((c0 + sj) * LANE, LANE) for c0 in A_SEGS])

        _put_pipeline(j * B + b, nc * B, copies_of, fill)
        row = lax.broadcasted_iota(jnp.int32, (8, LANE), 0)
        st = jnp.zeros((8, LANE), F32)
        for r, v in enumerate((dconv * u2, dconv * u1, dconv * u, dconv)):
            st = st + jnp.where(row == r, jnp.sum(v, axis=0, keepdims=True), 0.0)

        @pl.when(pl.program_id(1) == 0)
        def _():
            st_ref[...] = st

        @pl.when(pl.program_id(1) != 0)
        def _():
            st_ref[...] += st

    return pl.pallas_call(
        body, name="mixa_bwd", grid=(nc, B),
        in_specs=[pl.BlockSpec(memory_space=pl.ANY),
                  pl.BlockSpec((S, LANE), lambda j, b: (b, j))]
                 + [pl.BlockSpec((S, LANE), lambda j, b, c0=c0: (b, c0 + j)) for c0 in A_SEGS]
                 + [pl.BlockSpec((CONV_K, LANE), lambda j, b: (0, j)),
                    pl.BlockSpec((1, LANE), lambda j, b: (0, j))],
        out_specs=[pl.BlockSpec(memory_space=pl.ANY),
                   pl.BlockSpec((8, LANE), lambda j, b: (0, j))],
        out_shape=[jax.ShapeDtypeStruct(dproj.shape, BF16), jax.ShapeDtypeStruct((8, CONV_WIDTH), F32)],
        scratch_shapes=[pltpu.VMEM((2, 4, S, LANE), BF16), pltpu.SemaphoreType.DMA((2, 4))],
        input_output_aliases={0: 0},
        compiler_params=_cp(),
    )(dproj, dy, proj, proj, proj, proj, cw, cb)


def _mla_prep_fwd(proj, gq, gkv, wuqp, wkp, wv, gmq, gmk, cos, sa, sb, S):
    T = proj.shape[0]
    ts = _tile(S, 512)
    ns = S // ts
    W = MLA_HEADS * LANE

    def body(p_ref, gq_ref, gkv_ref, wuq_ref, wk_ref, wv_ref, gmq_ref, gmk_ref, cos_ref, sa_ref, sb_ref,
             q_ref, k_ref, v_ref):
        cq = p_ref[:, 0:2 * LANE].astype(F32)
        ckv = p_ref[:, 2 * LANE:3 * LANE].astype(F32)
        kpe = pltpu.roll(p_ref[:, 3 * LANE:4 * LANE].astype(F32), HALF, axis=1)
        cqn = cq * _rstd(cq, MLA_Q_LORA) * gq_ref[...]
        ckn = (ckv * _rstd(ckv, MLA_KV_LORA) * gkv_ref[...]).astype(BF16)
        q0 = _mm(cqn, wuq_ref[...])
        kn = _mm(ckn, wk_ref[...])
        v_ref[...] = _mm(ckn, wv_ref[...]).astype(BF16)
        c, a, b = cos_ref[...], sa_ref[...], sb_ref[...]
        kpe_rot = _rope(kpe * gmk_ref[...], c, a, b)
        for h in range(MLA_HEADS):
            q0h = q0[:, h * LANE:(h + 1) * LANE]
            q_ref[h] = _rope(q0h * _rstd(q0h, MLA_QK) * gmq_ref[...], c, a, b).astype(BF16)
            knh = kn[:, h * LANE:(h + 1) * LANE]
            k_ref[h] = (_rstd(knh + kpe, MLA_QK) * (knh * gmk_ref[...] + kpe_rot)).astype(BF16)

    def whole(r, c):
        return pl.BlockSpec((r, c), lambda i: (0, 0))

    tab = pl.BlockSpec((ts, LANE), lambda i: (i % ns, 0))
    return pl.pallas_call(
        body, name="mla_prep_fwd", grid=(T // ts,),
        in_specs=[pl.BlockSpec((ts, 4 * LANE), lambda i: (i, CB_CQ // 4)),
                  whole(1, MLA_Q_LORA), whole(1, MLA_KV_LORA), whole(MLA_Q_LORA, W), whole(MLA_KV_LORA, W),
                  whole(MLA_KV_LORA, MLA_HEADS * MLA_V), whole(1, LANE), whole(1, LANE), tab, tab, tab],
        out_specs=[pl.BlockSpec((MLA_HEADS, ts, LANE), lambda i: (0, i, 0)),
                   pl.BlockSpec((MLA_HEADS, ts, LANE), lambda i: (0, i, 0)),
                   pl.BlockSpec((ts, MLA_HEADS * MLA_V), lambda i: (i, 0))],
        out_shape=[jax.ShapeDtypeStruct((MLA_HEADS, T, LANE), BF16), jax.ShapeDtypeStruct((MLA_HEADS, T, LANE), BF16),
                   jax.ShapeDtypeStruct((T, MLA_HEADS * MLA_V), BF16)],
        compiler_params=_cp(),
    )(proj, gq, gkv, wuqp, wkp, wv, gmq, gmk, cos, sa, sb)


def _mla_prep_bwd(dproj, dq, dk, dv, proj, gq, gkv, wuqp, wkp, wv, gmq, gmk, cos, sa, sb, S):
    T = proj.shape[0]
    ts = _tile(S, 256)
    ns = S // ts
    W = MLA_HEADS * LANE

    def body(dpin_ref, dq_ref, dk_ref, dv_ref, p_ref, gq_ref, gkv_ref, wuq_ref, wk_ref, wv_ref, gmq_ref, gmk_ref,
             cos_ref, sa_ref, sb_ref,
             dp_ref, dwuq_ref, dwk_ref, dwv_ref, dgq_ref, dgkv_ref, dgmq_ref, dgmk_ref, dq0_ref, dkn_ref):
        del dpin_ref

        @pl.when(pl.program_id(0) == 0)
        def _():
            for r in (dwuq_ref, dwk_ref, dwv_ref, dgq_ref, dgkv_ref, dgmq_ref, dgmk_ref):
                r[...] = jnp.zeros_like(r)

        cq = p_ref[:, 0:2 * LANE].astype(F32)
        ckv = p_ref[:, 2 * LANE:3 * LANE].astype(F32)
        kpe = pltpu.roll(p_ref[:, 3 * LANE:4 * LANE].astype(F32), HALF, axis=1)
        rq = _rstd(cq, MLA_Q_LORA)
        rkv = _rstd(ckv, MLA_KV_LORA)
        gq, gkv, gmq, gmk = gq_ref[...], gkv_ref[...], gmq_ref[...], gmk_ref[...]
        cqn = (cq * rq * gq).astype(BF16)
        ckn = (ckv * rkv * gkv).astype(BF16)
        q0 = _mm(cqn, wuq_ref[...])
        kn = _mm(ckn, wk_ref[...])
        c, a, b = cos_ref[...], sa_ref[...], sb_ref[...]
        lane = lax.broadcasted_iota(jnp.int32, (ts, LANE), 1)
        dgmq = jnp.zeros((1, LANE), F32)
        dgmk = jnp.zeros((1, LANE), F32)
        dkpe = jnp.zeros((ts, LANE), F32)
        for h in range(MLA_HEADS):
            q0h = q0[:, h * LANE:(h + 1) * LANE]
            r = _rstd(q0h, MLA_QK)
            d1 = _rope_t(dq_ref[h], c, a, b)
            gy = d1 * gmq
            dq0_ref[:, h * LANE:(h + 1) * LANE] = (
                r * gy - q0h * (r * r * r) * (jnp.sum(q0h * gy, axis=-1, keepdims=True) * (1.0 / MLA_QK))).astype(BF16)
            dgmq = dgmq + jnp.sum(d1 * q0h * r, axis=0, keepdims=True)
            k0h = kn[:, h * LANE:(h + 1) * LANE] + kpe
            r = _rstd(k0h, MLA_QK)
            d1 = _rope_t(dk_ref[h], c, a, b)
            gy = d1 * gmk
            dk0 = r * gy - k0h * (r * r * r) * (jnp.sum(k0h * gy, axis=-1, keepdims=True) * (1.0 / MLA_QK))
            dgmk = dgmk + jnp.sum(d1 * k0h * r, axis=0, keepdims=True)
            dkn_ref[:, h * LANE:(h + 1) * LANE] = jnp.where(lane < MLA_NOPE, dk0, 0.0).astype(BF16)
            dkpe = dkpe + jnp.where((lane >= MLA_NOPE) & (lane < MLA_QK), dk0, 0.0)
        dq0 = dq0_ref[...]
        dkn = dkn_ref[...]
        dvv = dv_ref[...]
        dwuq_ref[...] += _mm_tn(cqn, dq0)
        dwk_ref[...] += _mm_tn(ckn, dkn)
        dwv_ref[...] += _mm_tn(ckn, dvv)
        dgmq_ref[...] += dgmq
        dgmk_ref[...] += dgmk
        dcqn = _mm_nt(dq0, wuq_ref[...])
        gy = dcqn * gq
        dp_ref[:, 0:2 * LANE] = (
            rq * gy - cq * (rq * rq * rq) * (jnp.sum(cq * gy, axis=-1, keepdims=True) * (1.0 / MLA_Q_LORA))).astype(BF16)
        dgq_ref[...] += jnp.sum(dcqn * cq * rq, axis=0, keepdims=True)
        dckn = _mm_nt(dkn, wk_ref[...]) + _mm_nt(dvv, wv_ref[...])
        gy = dckn * gkv
        dp_ref[:, 2 * LANE:3 * LANE] = (
            rkv * gy - ckv * (rkv * rkv * rkv) * (jnp.sum(ckv * gy, axis=-1, keepdims=True) * (1.0 / MLA_KV_LORA))).astype(BF16)
        dgkv_ref[...] += jnp.sum(dckn * ckv * rkv, axis=0, keepdims=True)
        dp_ref[:, 3 * LANE:4 * LANE] = pltpu.roll(dkpe, HALF, axis=1).astype(BF16)

    def whole(r, c):
        return pl.BlockSpec((r, c), lambda i: (0, 0))

    tab = pl.BlockSpec((ts, LANE), lambda i: (i % ns, 0))
    heads = pl.BlockSpec((MLA_HEADS, ts, LANE), lambda i: (0, i, 0))
    return pl.pallas_call(
        body, name="mla_prep_bwd", grid=(T // ts,),
        in_specs=[pl.BlockSpec(memory_space=pl.ANY), heads, heads,
                  pl.BlockSpec((ts, MLA_HEADS * MLA_V), lambda i: (i, 0)),
                  pl.BlockSpec((ts, 4 * LANE), lambda i: (i, CB_CQ // 4)),
                  whole(1, MLA_Q_LORA), whole(1, MLA_KV_LORA), whole(MLA_Q_LORA, W), whole(MLA_KV_LORA, W),
                  whole(MLA_KV_LORA, MLA_HEADS * MLA_V), whole(1, LANE), whole(1, LANE), tab, tab, tab],
        out_specs=[pl.BlockSpec((ts, 4 * LANE), lambda i: (i, CB_CQ // 4)),
                   whole(MLA_Q_LORA, W), whole(MLA_KV_LORA, W), whole(MLA_KV_LORA, MLA_HEADS * MLA_V),
                   whole(1, MLA_Q_LORA), whole(1, MLA_KV_LORA), whole(1, LANE), whole(1, LANE)],
        out_shape=[jax.ShapeDtypeStruct(dproj.shape, BF16),
                   jax.ShapeDtypeStruct((MLA_Q_LORA, W), F32), jax.ShapeDtypeStruct((MLA_KV_LORA, W), F32),
                   jax.ShapeDtypeStruct((MLA_KV_LORA, MLA_HEADS * MLA_V), F32),
                   jax.ShapeDtypeStruct((1, MLA_Q_LORA), F32), jax.ShapeDtypeStruct((1, MLA_KV_LORA), F32),
                   jax.ShapeDtypeStruct((1, LANE), F32), jax.ShapeDtypeStruct((1, LANE), F32)],
        scratch_shapes=[pltpu.VMEM((ts, W), BF16), pltpu.VMEM((ts, W), BF16)],
        input_output_aliases={0: 0},
        compiler_params=_cp(),
    )(dproj, dq, dk, dv, proj, gq, gkv, wuqp, wkp, wv, gmq, gmk, cos, sa, sb)


def _dil_prep_fwd(proj, gq, gk):
    T = proj.shape[0]
    ts = _tile(T, 512)

    def body(pq_ref, pk_ref, gq_ref, gk_ref, q_ref, k_ref):
        for c in range(NPAIR):
            cs = slice(c * LANE, (c + 1) * LANE)
            t = jnp.concatenate([pq_ref[:, cs], pk_ref[:, cs]], axis=1).astype(F32)
            y = t * lax.rsqrt(_head_bcast_sum(t * t, terms=2) * (1.0 / DIL_HEAD_DIM) + EPS)
            q_ref[:, cs] = (y[:, 0:LANE] * gq_ref[:, cs]).astype(BF16)
            k_ref[:, cs] = (y[:, LANE:2 * LANE] * gk_ref[:, cs]).astype(BF16)

    col = pl.BlockSpec((1, DIL_WIDTH), lambda i, g: (0, g))
    out = pl.BlockSpec((ts, DIL_WIDTH), lambda i, g: (i, g))
    seg = lambda c0: pl.BlockSpec((ts, DIL_WIDTH), lambda i, g: (i, c0 // NPAIR + g))
    return pl.pallas_call(
        body, name="dil_prep_fwd", grid=(T // ts, DIL_GROUPS),
        in_specs=[seg(CB_DQ), seg(CB_DK), col, col],
        out_specs=[out, out],
        out_shape=[jax.ShapeDtypeStruct((T, DIL_QK), BF16)] * 2,
        compiler_params=_cp(),
    )(proj, proj, gq, gk)


def _dil_prep_bwd(dproj, ddq, ddk, ddv, proj, gq, gk):
    T = proj.shape[0]
    ts = _tile(T, 512)
    nt = T // ts

    def body(dpin_ref, ddq_ref, ddk_ref, ddv_ref, pq_ref, pk_ref, gq_ref, gk_ref, dp_ref, dgq_ref, dgk_ref,
             stage, sems):
        del dpin_ref
        g, i = pl.program_id(0), pl.program_id(1)

        @pl.when(i == 0)
        def _():
            dgq_ref[...] = jnp.zeros_like(dgq_ref)
            dgk_ref[...] = jnp.zeros_like(dgk_ref)

        def fill(slot):
            stage[slot, 2] = ddv_ref[...].astype(BF16)
            for c in range(NPAIR):
                cs = slice(c * LANE, (c + 1) * LANE)
                t = jnp.concatenate([pq_ref[:, cs], pk_ref[:, cs]], axis=1).astype(F32)
                d = jnp.concatenate([ddq_ref[:, cs], ddk_ref[:, cs]], axis=1)
                gy = d * jnp.concatenate([gq_ref[:, cs], gk_ref[:, cs]], axis=1)
                r = lax.rsqrt(_head_bcast_sum(t * t, terms=2) * (1.0 / DIL_HEAD_DIM) + EPS)
                dot = _head_bcast_sum(t * gy, terms=2) * (1.0 / DIL_HEAD_DIM)
                dx = (r * gy - t * (r * r * r) * dot).astype(BF16)
                stage[slot, 0, :, cs] = dx[:, 0:LANE]
                stage[slot, 1, :, cs] = dx[:, LANE:2 * LANE]
                part = jnp.sum(d * t * r, axis=0, keepdims=True)
                dgq_ref[:, cs] += part[:, 0:LANE]
                dgk_ref[:, cs] += part[:, LANE:2 * LANE]

        def copies_of(step):
            sg, si = step // nt, step % nt
            return _put_copies([stage.at[:, k] for k in range(3)], dp_ref, sems, step % 2,
                               pl.ds(pl.multiple_of(si * ts, ts), ts),
                               [pl.multiple_of((c0 + NPAIR * sg) * LANE, LANE) for c0 in (CB_DQ, CB_DK, CB_DV)])

        _put_pipeline(g * nt + i, DIL_GROUPS * nt, copies_of, fill)

    col = pl.BlockSpec((1, DIL_WIDTH), lambda g, i: (0, g))
    tok = pl.BlockSpec((ts, DIL_WIDTH), lambda g, i: (i, g))
    seg = lambda c0: pl.BlockSpec((ts, DIL_WIDTH), lambda g, i: (i, c0 // NPAIR + g))
    return pl.pallas_call(
        body, name="dil_prep_bwd", grid=(DIL_GROUPS, nt),
        in_specs=[pl.BlockSpec(memory_space=pl.ANY), tok, tok, tok, seg(CB_DQ), seg(CB_DK), col, col],
        out_specs=[pl.BlockSpec(memory_space=pl.ANY), col, col],
        out_shape=[jax.ShapeDtypeStruct(dproj.shape, BF16), jax.ShapeDtypeStruct((1, DIL_QK), F32),
                   jax.ShapeDtypeStruct((1, DIL_QK), F32)],
        scratch_shapes=[pltpu.VMEM((2, 3, ts, DIL_WIDTH), BF16), pltpu.SemaphoreType.DMA((2, 3))],
        input_output_aliases={0: 0},
        compiler_params=_cp(),
    )(dproj, ddq, ddk, ddv, proj, proj, gq, gk)


COPY_ROWS = 256


def _to_classes(src_ref, dst_ref, d, L, scale=None):
    m = min(L, max(8, COPY_ROWS // d))
    for c0 in range(0, L, m):
        x = src_ref[c0 * d:(c0 + m) * d, :].astype(F32)
        if scale is not None:
            x = x * scale
        if d > 1:
            x = jnp.swapaxes(x.reshape(m, d, LANE), 0, 1)
        for r in range(d):
            dst_ref[r * L + c0:r * L + c0 + m, :] = (x[r] if d > 1 else x).astype(dst_ref.dtype)


def _from_classes(src_ref, dst_ref, d, L):
    n = min(L, COPY_ROWS)
    for r in range(d):
        for c0 in range(0, L, n):
            rows = pl.ds(r + c0 * d, n, stride=d) if d > 1 else pl.ds(c0, n)
            dst_ref[rows, :] = src_ref[r * L + c0:r * L + c0 + n, :].astype(dst_ref.dtype)


MLA_TQ, MLA_TK = 512, 512


def _causal_bias(tq, tk, shift):
    row = lax.broadcasted_iota(jnp.int32, (tq, tk), 0)
    col = lax.broadcasted_iota(jnp.int32, (tq, tk), 1)
    return jnp.where(row >= col + shift, 0.0, NEG)


def _mla_specs(S):
    heads = pl.BlockSpec((2, S, LANE), lambda b, j: (j, b, 0))
    pair = pl.BlockSpec((S, LANE), lambda b, j: (b, j))
    return heads, pair


def _mla_attn_fwd(q, k, v, B, S):
    tq = _tile(S, MLA_TQ)
    tk = _tile(tq, MLA_TK)
    nd = tq // tk
    scale = MLA_QK ** -0.5
    heads, pair = _mla_specs(S)

    def body(q_ref, k_ref, v_ref, o_ref, lse_ref):
        lo, lok = _lane_lo((tq, LANE)), _lane_lo((tk, LANE))
        diag = [_causal_bias(tq, tk, i * tk) for i in range(nd)]

        def block(g, _):
            row0 = pl.multiple_of(g * tq, tq)
            rows = pl.ds(row0, tq)
            qs = [q_ref[hh, rows, :] for hh in range(2)]

            one = jnp.ones((), BF16)

            def step(off, carries, bias):
                off = pl.multiple_of(off, tk)
                vt = v_ref[pl.ds(off, tk), :]
                vh = (jnp.where(lok, vt, one), jnp.where(lok, one, vt))
                out = []
                for hh, (m, acc) in enumerate(carries):
                    s = _mm_nt(qs[hh], k_ref[hh, pl.ds(off, tk), :]) * scale
                    if bias is not None:
                        s = s + bias
                    m_new = jnp.maximum(m, jnp.max(s, axis=-1, keepdims=True))
                    p = jnp.exp(s - m_new)
                    out.append((m_new, jnp.exp(m - m_new) * acc + _mm(p, vh[hh])))
                return tuple(out)

            init = (jnp.full((tq, 1), NEG, F32), jnp.zeros((tq, LANE), F32))
            carries = lax.fori_loop(0, g * nd, lambda i, c: step(i * tk, c, None), (init, init))
            for i in range(nd):
                carries = step(row0 + i * tk, carries, diag[i])
            (ma, acca), (mb, accb) = carries
            la, lb = pltpu.roll(acca, HALF, axis=1), pltpu.roll(accb, HALF, axis=1)
            o_ref[rows, :] = jnp.where(lo, acca / la, accb / lb)
            lse_ref[rows, :] = jnp.where(lo, ma + jnp.log(la), mb + jnp.log(lb))
            return 0

        lax.fori_loop(0, S // tq, block, 0)

    return pl.pallas_call(
        body, name="mla_attn_fwd", grid=(B, NPAIR), in_specs=[heads, heads, pair], out_specs=[pair, pair],
        out_shape=[jax.ShapeDtypeStruct((B * S, MLA_HEADS * MLA_V), F32)] * 2,
        compiler_params=_cp(),
    )(q, k, v)


DIL_UNROLL = 16


def _dil_geometry(gi, S):
    span, d = DIL_PATTERNS[gi]
    L = S // d
    t = _tile(L, 128)
    window = span // d
    back = min(-(-window // t) * t, L - t)
    return d, L, t, window, back


def _dil_specs(gi, S):
    qk = pl.BlockSpec((S, LANE), lambda b, j: (b, NPAIR * gi + j))
    v = pl.BlockSpec((S, LANE), lambda b, j: (b, CB_DV + NPAIR * gi + j))
    pair = pl.BlockSpec((S, LANE), lambda b, j: (b, j))
    return qk, v, pair


def _dil_bias(bias_ref, sl_ref, j, t, kw, back, window):
    row = lax.broadcasted_iota(jnp.int32, (2 * t, kw), 0)
    col = lax.broadcasted_iota(jnp.int32, (2 * t, kw), 1)
    second = row >= t
    slope = jnp.where(second, sl_ref[j, 1], sl_ref[j, 0])
    for n in range(bias_ref.shape[0]):
        dist = jnp.where(second, row - t, row) + n * back - col
        bias_ref[n] = jnp.where((dist >= 0) & (dist <= window), -slope * dist.astype(F32), NEG)


def _stack_heads(x, lo):
    zero = jnp.zeros((), x.dtype)
    return jnp.concatenate([jnp.where(lo, x, zero), jnp.where(lo, zero, x)], axis=0)


def _dil_attn_fwd(gi, slopes, qn, kn, proj, B, S):
    d, L, t, window, back = _dil_geometry(gi, S)
    kw, nq = back + t, L // t
    nbias = 2 if back else 1
    qk, vspec, pair = _dil_specs(gi, S)

    def body(sl_ref, q_ref, k_ref, v_ref, o_ref, lse_ref, qs, ks, vs, os_, ls, bias_ref):
        _to_classes(q_ref, qs, d, L, DIL_HEAD_DIM ** -0.5)
        _to_classes(k_ref, ks, d, L)
        _to_classes(v_ref, vs, d, L)
        _dil_bias(bias_ref, sl_ref, pl.program_id(1), t, kw, back, window)
        lo = _lane_lo((t, LANE))

        def block(g, _):
            qb = g % nq if d > 1 else g
            row0 = pl.multiple_of(g * t, t)
            rows = pl.ds(row0, t)
            early = qb * t < back
            keys = pl.ds(pl.multiple_of(jnp.where(early, row0 - qb * t, row0 - back), t), kw)
            s = _mm_nt(_stack_heads(qs[rows, :], lo), ks[keys, :]) + bias_ref[jnp.where(early, 0, nbias - 1)]
            m = jnp.max(s, axis=-1, keepdims=True)
            p = jnp.exp(s - m)
            l = jnp.sum(p, axis=-1, keepdims=True)
            o2 = _mm(p, vs[keys, :]) / l
            lse2 = m + jnp.log(l)
            os_[rows, :] = jnp.where(lo, o2[:t], o2[t:])
            ls[rows, :] = jnp.where(lo, lse2[:t], lse2[t:])
            return 0

        lax.fori_loop(0, d * nq, block, 0, unroll=DIL_UNROLL if d * nq % DIL_UNROLL == 0 else 1)
        _from_classes(os_, o_ref, d, L)
        _from_classes(ls, lse_ref, d, L)

    return pl.pallas_call(
        body, name=f"dil_attn_fwd_{gi}", grid=(B, NPAIR),
        in_specs=[pl.BlockSpec(memory_space=pltpu.SMEM), qk, qk, vspec], out_specs=[pair, pair],
        out_shape=[jax.ShapeDtypeStruct((B * S, DIL_WIDTH), F32)] * 2,
        scratch_shapes=[pltpu.VMEM((S, LANE), BF16)] * 3 + [pltpu.VMEM((S, LANE), F32)] * 2
                       + [pltpu.VMEM((nbias, 2 * t, kw), F32)],
        compiler_params=_cp(),
    )(slopes, qn, kn, proj)


def _mla_attn_bwd(q, k, v, do, lse, delta, B, S):
    T = B * S
    tq = _tile(S, MLA_TQ)
    tk = _tile(tq, MLA_TK)
    nd = tq // tk
    scale = MLA_QK ** -0.5
    heads, pair = _mla_specs(S)

    def body(q_ref, k_ref, v_ref, do_ref, lse_ref, dl_ref, dq_ref, dk_ref, dv_ref):
        dk_ref[...] = jnp.zeros_like(dk_ref)
        dv_ref[...] = jnp.zeros_like(dv_ref)
        lo = _lane_lo((tq, LANE))
        diag = [_causal_bias(tq, tk, i * tk) for i in range(nd)]

        def block(g, _):
            row0 = pl.multiple_of(g * tq, tq)
            rows = pl.ds(row0, tq)
            per_head = []
            for hh in range(2):
                sel = lo if hh == 0 else jnp.logical_not(lo)
                per_head.append((q_ref[hh, rows, :], jnp.where(sel, do_ref[rows, :], jnp.zeros((), BF16)),
                                 jnp.max(jnp.where(sel, lse_ref[rows, :], NEG), axis=-1, keepdims=True),
                                 jnp.max(jnp.where(sel, dl_ref[rows, :], NEG), axis=-1, keepdims=True)))

            def step(off, dq_accs, bias):
                cols = pl.ds(pl.multiple_of(off, tk), tk)
                vt = v_ref[cols, :]
                out, dv = [], None
                for hh, (qh, doh, lse_h, dl_h) in enumerate(per_head):
                    kh = k_ref[hh, cols, :]
                    s = _mm_nt(qh, kh) * scale
                    if bias is not None:
                        s = s + bias
                    p = jnp.exp(s - lse_h)
                    ds = (p * (_mm_nt(doh, vt) - dl_h)).astype(BF16)
                    dk_ref[hh, cols, :] += _mm_tn(ds, qh) * scale
                    part = _mm_tn(p, doh)
                    dv = part if dv is None else dv + part
                    out.append(dq_accs[hh] + _mm(ds, kh))
                dv_ref[cols, :] += dv
                return tuple(out)

            zero = jnp.zeros((tq, LANE), F32)
            dq_accs = lax.fori_loop(0, g * nd, lambda i, a: step(i * tk, a, None), (zero, zero))
            for i in range(nd):
                dq_accs = step(row0 + i * tk, dq_accs, diag[i])
            for hh in range(2):
                dq_ref[hh, rows, :] = dq_accs[hh] * scale
            return 0

        lax.fori_loop(0, S // tq, block, 0)

    return pl.pallas_call(
        body, name="mla_attn_bwd", grid=(B, NPAIR), in_specs=[heads, heads, pair, pair, pair, pair],
        out_specs=[heads, heads, pair],
        out_shape=[jax.ShapeDtypeStruct((MLA_HEADS, T, LANE), F32), jax.ShapeDtypeStruct((MLA_HEADS, T, LANE), F32),
                   jax.ShapeDtypeStruct((T, MLA_HEADS * MLA_V), F32)],
        compiler_params=_cp(),
    )(q, k, v, do, lse, delta)


def _dil_attn_bwd(gi, slopes, qn, kn, proj, do, lse, delta, through, B, S):
    d, L, t, window, back = _dil_geometry(gi, S)
    kw, nq = back + t, L // t
    nbias = 2 if back else 1
    scale = DIL_HEAD_DIM ** -0.5
    qk, vspec, pair = _dil_specs(gi, S)

    def body(*refs):
        refs = list(refs)
        sl_ref, q_ref, k_ref, v_ref, do_ref, lse_ref, dl_ref = refs[:7]
        dq_ref, dk_ref, dv_ref, qs, ks, vs, dos, lss, dls, dqs, dks, dvs, bias_ref = refs[-13:]
        _to_classes(q_ref, qs, d, L, scale)
        for src, dst in ((k_ref, ks), (v_ref, vs), (do_ref, dos), (lse_ref, lss), (dl_ref, dls)):
            _to_classes(src, dst, d, L)
        _dil_bias(bias_ref, sl_ref, pl.program_id(1), t, kw, back, window)
        dks[...] = jnp.zeros_like(dks)
        dvs[...] = jnp.zeros_like(dvs)
        lo = _lane_lo((t, LANE))

        def stats(ref, rows):
            x = ref[rows, :]
            return jnp.concatenate([jnp.max(jnp.where(lo, x, NEG), axis=-1, keepdims=True),
                                    jnp.max(jnp.where(lo, NEG, x), axis=-1, keepdims=True)], axis=0)

        def block(g, _):
            qb = g % nq if d > 1 else g
            row0 = pl.multiple_of(g * t, t)
            rows = pl.ds(row0, t)
            early = qb * t < back
            keys = pl.ds(pl.multiple_of(jnp.where(early, row0 - qb * t, row0 - back), t), kw)
            q2 = _stack_heads(qs[rows, :], lo)
            do2 = _stack_heads(dos[rows, :], lo)
            kt = ks[keys, :]
            s = _mm_nt(q2, kt) + bias_ref[jnp.where(early, 0, nbias - 1)]
            p = jnp.exp(s - stats(lss, rows))
            ds = (p * (_mm_nt(do2, vs[keys, :]) - stats(dls, rows))).astype(BF16)
            dq2 = _mm(ds, kt) * scale
            dqs[rows, :] = jnp.where(lo, dq2[:t], dq2[t:])
            dks[keys, :] += _mm_tn(ds, q2)
            dvs[keys, :] += _mm_tn(p, do2)
            return 0

        lax.fori_loop(0, d * nq, block, 0, unroll=DIL_UNROLL if d * nq % DIL_UNROLL == 0 else 1)
        for src, dst in ((dqs, dq_ref), (dks, dk_ref), (dvs, dv_ref)):
            _from_classes(src, dst, d, L)

    in_specs = [pl.BlockSpec(memory_space=pltpu.SMEM), qk, qk, vspec, pair, pair, pair]
    args = [slopes, qn, kn, proj, do, lse, delta]
    aliases = {}
    if through is not None:
        aliases = {len(args) + i: i for i in range(3)}
        in_specs = in_specs + [pl.BlockSpec(memory_space=pl.ANY)] * 3
        args = args + list(through)
    return pl.pallas_call(
        body, name=f"dil_attn_bwd_{gi}", grid=(B, NPAIR), in_specs=in_specs, out_specs=[qk, qk, qk],
        out_shape=[jax.ShapeDtypeStruct((B * S, DIL_QK), F32)] * 3,
        scratch_shapes=[pltpu.VMEM((S, LANE), BF16)] * 4 + [pltpu.VMEM((S, LANE), F32)] * 5
                       + [pltpu.VMEM((nbias, 2 * t, kw), F32)],
        input_output_aliases=aliases,
        compiler_params=_cp(),
    )(*args)


def _merge_proj_specs(ts):
    wide = lambda c0, w: pl.BlockSpec((ts, w), lambda i: (i, c0 * LANE // w))
    return [wide(CB_BZ, DIL_WIDTH), wide(CB_CZ, DIL_WIDTH)] + [wide(CB_GATE + 8 * i, D_MODEL) for i in range(3)]


def _merge_common(p_refs, bg_ref, ob_ref, og_refs, lse_refs):
    bz = p_refs[0][...].astype(F32)
    cz = p_refs[1][...].astype(F32)
    gates = [_sigmoid(p_refs[2 + i][...].astype(F32) + bg_ref[:, i * D_MODEL:(i + 1) * D_MODEL]) for i in range(3)]
    ob = ob_ref[...]
    lses = [r[...] for r in lse_refs]
    mx = jnp.maximum(jnp.maximum(lses[0], lses[1]), lses[2])
    es = [jnp.exp(v - mx) for v in lses]
    inv = 1.0 / (es[0] + es[1] + es[2])
    alphas = [e * inv for e in es]
    oc = alphas[0] * og_refs[0][...] + alphas[1] * og_refs[1][...] + alphas[2] * og_refs[2][...]
    return bz, cz, gates, ob, alphas, oc


def _merge_fwd(x, proj, b_gate, ya, ob, ogs, lses, woa, wob, woc, wo):
    T = x.shape[0]
    ts = _tile(T, 256)

    def body(x_ref, p0, p1, p2, p3, p4, bg_ref, ya_ref, ob_ref, og0, og1, og2, l0, l1, l2,
             woa_ref, wob_ref, woc_ref, wo_ref, out_ref):
        bz, cz, gates, obv, alphas, oc = _merge_common((p0, p1, p2, p3, p4), bg_ref, ob_ref, (og0, og1, og2),
                                                       (l0, l1, l2))
        yb = obv * _silu(bz)
        yc = oc * _silu(cz)
        merged = (gates[0] * _mm(ya_ref[...], woa_ref[...]) + gates[1] * _mm(yb, wob_ref[...])
                  + gates[2] * _mm(yc, woc_ref[...]))
        out_ref[...] = x_ref[...] + _mm(merged, wo_ref[...])

    def whole(r, c):
        return pl.BlockSpec((r, c), lambda i: (0, 0))

    tok = lambda w: pl.BlockSpec((ts, w), lambda i: (i, 0))
    return pl.pallas_call(
        body, name="merge_fwd", grid=(T // ts,),
        in_specs=[tok(D_MODEL)] + _merge_proj_specs(ts) + [whole(1, 3 * D_MODEL), tok(CONV_WIDTH)]
                 + [tok(DIL_WIDTH)] * 7 + [whole(CONV_WIDTH, D_MODEL)] * 3 + [whole(D_MODEL, D_MODEL)],
        out_specs=tok(D_MODEL),
        out_shape=jax.ShapeDtypeStruct((T, D_MODEL), F32),
        compiler_params=_cp(),
    )(x, *[proj] * 5, b_gate, ya, ob, *ogs, *lses, woa, wob, woc, wo)


def _merge_bwd(dout, proj, b_gate, ya, ob, ogs, lses, woa, wob, woc, wo):
    T = dout.shape[0]
    ts = _tile(T, 256)
    nt = T // ts

    def body(do_ref, p0, p1, p2, p3, p4, bg_ref, ya_ref, ob_ref, og0, og1, og2, l0, l1, l2,
             woa_ref, wob_ref, woc_ref, wo_ref,
             dp_ref, dya_ref, dob_ref, dlb_ref, dg0, dg1, dg2, dl0, dl1, dl2,
             mg_ref, dpa_ref, dpb_ref, dpc_ref, yb_ref, yc_ref, dbg_ref, st_bz, st_cz, st_gate, sems):
        step = pl.program_id(0)
        slot = step % 2

        def copies_of(s):
            return _put_copies([st_bz, st_cz, st_gate], dp_ref, sems, s % 2, pl.ds(pl.multiple_of(s * ts, ts), ts),
                               [CB_BZ * LANE, CB_CZ * LANE, CB_GATE * LANE])

        @pl.when(step >= 2)
        def _():
            for cp in copies_of(step - 2):
                cp.wait()

        bz, cz, gates, obv, alphas, oc = _merge_common((p0, p1, p2, p3, p4), bg_ref, ob_ref, (og0, og1, og2),
                                                       (l0, l1, l2))
        sb, sc = _silu(bz), _silu(cz)
        yb = obv * sb
        yc = oc * sc
        ps = [_mm(ya_ref[...], woa_ref[...]), _mm(yb, wob_ref[...]), _mm(yc, woc_ref[...])]
        mg_ref[...] = (gates[0] * ps[0] + gates[1] * ps[1] + gates[2] * ps[2]).astype(BF16)
        yb_ref[...] = yb.astype(BF16)
        yc_ref[...] = yc.astype(BF16)
        dm = _mm_nt(do_ref[...], wo_ref[...])
        dps = []
        first = pl.program_id(0) == 0
        for i, dref in enumerate((dpa_ref, dpb_ref, dpc_ref)):
            g = gates[i]
            dpi = (dm * g).astype(BF16)
            dref[...] = dpi
            dps.append(dpi)
            dgp = dm * ps[i] * g * (1.0 - g)
            st_gate[slot, :, i * D_MODEL:(i + 1) * D_MODEL] = dgp.astype(BF16)
            part = jnp.sum(dgp, axis=0, keepdims=True)

            @pl.when(first)
            def _():
                dbg_ref[:, i * D_MODEL:(i + 1) * D_MODEL] = part

            @pl.when(jnp.logical_not(first))
            def _():
                dbg_ref[:, i * D_MODEL:(i + 1) * D_MODEL] += part

        dya_ref[...] = _mm_nt(dps[0], woa_ref[...])
        dyb = _mm_nt(dps[1], wob_ref[...])
        dyc = _mm_nt(dps[2], woc_ref[...])
        st_bz[slot] = (dyb * obv * _dsilu(bz)).astype(BF16)
        st_cz[slot] = (dyc * oc * _dsilu(cz)).astype(BF16)
        for cp in copies_of(step):
            cp.start()
        dob = dyb * sb
        doc = dyc * sc
        dob_ref[...] = dob.astype(BF16)
        for c in range(NPAIR):
            cs = slice(c * LANE, (c + 1) * LANE)
            dlb_ref[:, cs] = _head_bcast_sum(dob[:, cs] * obv[:, cs])
            dd = _head_bcast_sum(doc[:, cs] * oc[:, cs])
            for a, dref, lref in zip(alphas, (dg0, dg1, dg2), (dl0, dl1, dl2)):
                dref[:, cs] = (a[:, cs] * doc[:, cs]).astype(BF16)
                lref[:, cs] = a[:, cs] * dd

        @pl.when(step == nt - 1)
        def _():
            if nt >= 2:
                for cp in copies_of(step - 1):
                    cp.wait()
            for cp in copies_of(step):
                cp.wait()

    def whole(r, c):
        return pl.BlockSpec((r, c), lambda i: (0, 0))

    tok = lambda w: pl.BlockSpec((ts, w), lambda i: (i, 0))
    sd = jax.ShapeDtypeStruct
    W = DIL_WIDTH
    return pl.pallas_call(
        body, name="merge_bwd", grid=(nt,),
        in_specs=[tok(D_MODEL)] + _merge_proj_specs(ts) + [whole(1, 3 * D_MODEL), tok(CONV_WIDTH)] + [tok(W)] * 7
                 + [whole(CONV_WIDTH, D_MODEL)] * 3 + [whole(D_MODEL, D_MODEL)],
        out_specs=[pl.BlockSpec(memory_space=pl.ANY), tok(CONV_WIDTH), tok(W), tok(W)] + [tok(W)] * 6
                  + [tok(D_MODEL)] * 4 + [tok(W), tok(W), whole(1, 3 * D_MODEL)],
        out_shape=[sd((T, PP), BF16), sd((T, CONV_WIDTH), F32), sd((T, W), BF16), sd((T, W), F32)]
                  + [sd((T, W), BF16)] * 3 + [sd((T, W), F32)] * 3
                  + [sd((T, D_MODEL), BF16)] * 4 + [sd((T, W), BF16)] * 2 + [sd((1, 3 * D_MODEL), F32)],
        scratch_shapes=[pltpu.VMEM((2, ts, W), BF16), pltpu.VMEM((2, ts, W), BF16),
                        pltpu.VMEM((2, ts, 3 * D_MODEL), BF16), pltpu.SemaphoreType.DMA((2, 3))],
        compiler_params=_cp(),
    )(dout, *[proj] * 5, b_gate, ya, ob, *ogs, *lses, woa, wob, woc, wo)


def _loss_head(y, target):
    T = y.shape[0]
    ts = _tile(T, 512)

    def body(y_ref, t_ref, d_ref, l_ref):
        e = y_ref[...] - t_ref[...]
        d_ref[...] = e * (1.0 / D_MODEL)
        l_ref[...] = jnp.zeros((1, 8, LANE), F32) + jnp.sum(e * e)

    tok = pl.BlockSpec((ts, D_MODEL), lambda i: (i, 0))
    return pl.pallas_call(
        body, name="loss_head", grid=(T // ts,), in_specs=[tok, tok],
        out_specs=[tok, pl.BlockSpec((1, 8, LANE), lambda i: (i, 0, 0))],
        out_shape=[jax.ShapeDtypeStruct((T, D_MODEL), F32), jax.ShapeDtypeStruct((T // ts, 8, LANE), F32)],
        compiler_params=_cp(),
    )(y, target)


def _my_index():
    return 4 * lax.axis_index("x") + 2 * lax.axis_index("y") + lax.axis_index("c")


def _peers():
    x, y, c = (lax.axis_index(a) for a in AXES)
    out = []
    for kk in range(1, N_DEV):
        px = 1 - x if kk & 4 else x
        py = 1 - y if kk & 2 else y
        pc = 1 - c if kk & 1 else c
        out.append(((px, py, pc), 4 * px + 2 * py + pc))
    return out


def _exchange(arrays, name, gather):
    n = len(arrays)

    def body(*refs):
        srcs, outs = refs[:n], refs[n:2 * n]
        send_sems, recv_sems, local_sems = refs[2 * n:]
        me = _my_index()
        peers = _peers()
        started = []
        for a, (src, out) in enumerate(zip(srcs, outs)):
            mine = pltpu.make_async_copy(src if gather else src.at[me], out.at[me], local_sems.at[a])
            mine.start()
            started.append(mine)
        sends = []
        for i, (pos, idx) in enumerate(peers):
            for a, (src, out) in enumerate(zip(srcs, outs)):
                cp = pltpu.make_async_remote_copy(
                    src_ref=src if gather else src.at[idx], dst_ref=out.at[me], send_sem=send_sems.at[a, i],
                    recv_sem=recv_sems.at[a, i], device_id=pos, device_id_type=pl.DeviceIdType.MESH)
                cp.start()
                sends.append(cp)
        for i, (pos, idx) in enumerate(peers):
            for a, (src, out) in enumerate(zip(srcs, outs)):
                pltpu.make_async_remote_copy(
                    src_ref=src if gather else src.at[idx], dst_ref=out.at[idx], send_sem=send_sems.at[a, i],
                    recv_sem=recv_sems.at[a, i], device_id=pos, device_id_type=pl.DeviceIdType.MESH).wait_recv()
        for cp in sends:
            cp.wait_send()
        for mine in started:
            mine.wait()

    any_space = pl.BlockSpec(memory_space=pl.ANY)
    return pl.pallas_call(
        body, name=name, in_specs=[any_space] * n, out_specs=[any_space] * n,
        out_shape=[jax.ShapeDtypeStruct(((N_DEV,) + a.shape) if gather else a.shape, a.dtype) for a in arrays],
        scratch_shapes=[pltpu.SemaphoreType.DMA((n, N_DEV - 1)), pltpu.SemaphoreType.DMA((n, N_DEV - 1)),
                        pltpu.SemaphoreType.DMA((n,))],
    )(*arrays)


N_CHIP = 4


def _chip_places():
    x, y, c = (lax.axis_index(a) for a in AXES)
    return (x, y, c), (x, y, 1 - c), [(1 - x, y, c), (x, 1 - y, c), (1 - x, 1 - y, c)]


def _index_of(pos):
    return 4 * pos[0] + 2 * pos[1] + pos[2]


def _gather_two_level(arrays, name):
    n = len(arrays)

    def body(*refs):
        srcs, outs = refs[:n], refs[n:2 * n]
        send_sems, recv_sems, local_sems = refs[2 * n:]
        me, sibling, others = _chip_places()

        def copy(a, k, block, to, src=None):
            slot = outs[a].at[_index_of(block)]
            return pltpu.make_async_remote_copy(
                src_ref=slot if src is None else src, dst_ref=slot, send_sem=send_sems.at[7 * a + k],
                recv_sem=recv_sems.at[7 * a + k], device_id=to, device_id_type=pl.DeviceIdType.MESH)

        started = []
        for a, src in enumerate(srcs):
            mine = pltpu.make_async_copy(src, outs[a].at[_index_of(me)], local_sems.at[a])
            mine.start()
            started.append(mine)
        sends = []
        for a, src in enumerate(srcs):
            sends.append(copy(a, 0, me, sibling, src))
            sends += [copy(a, 1 + j, me, chip, src) for j, chip in enumerate(others)]
        for cp in sends:
            cp.start()
        for j, chip in enumerate(others):
            for a in range(n):
                copy(a, 1 + j, chip, me).wait_recv()
                fwd = copy(a, 4 + j, chip, sibling)
                fwd.start()
                sends.append(fwd)
        for a in range(n):
            copy(a, 0, sibling, me).wait_recv()
            for j, chip in enumerate(others):
                copy(a, 4 + j, (chip[0], chip[1], sibling[2]), me).wait_recv()
        for cp in sends:
            cp.wait_send()
        for mine in started:
            mine.wait()

    any_space = pl.BlockSpec(memory_space=pl.ANY)
    return pl.pallas_call(
        body, name=name, in_specs=[any_space] * n, out_specs=[any_space] * n,
        out_shape=[jax.ShapeDtypeStruct((N_DEV,) + a.shape, a.dtype) for a in arrays],
        scratch_shapes=[pltpu.SemaphoreType.DMA((7 * n,)), pltpu.SemaphoreType.DMA((7 * n,)),
                        pltpu.SemaphoreType.DMA((n,))],
    )(*arrays)


def _sibling_swap(arrays, name):
    n = len(arrays)

    def body(*refs):
        srcs, outs = refs[:n], refs[n:2 * n]
        send_sems, recv_sems = refs[2 * n:]
        (x, y, c), sibling, _ = _chip_places()
        sends = []
        for a, (src, out) in enumerate(zip(srcs, outs)):
            for q in range(N_CHIP):
                def copy(core, a=a, q=q, src=src, out=out):
                    return pltpu.make_async_remote_copy(
                        src_ref=src.at[2 * q + core], dst_ref=out.at[q], send_sem=send_sems.at[N_CHIP * a + q],
                        recv_sem=recv_sems.at[N_CHIP * a + q], device_id=sibling, device_id_type=pl.DeviceIdType.MESH)
                mine = copy(1 - c)
                mine.start()
                sends.append((mine, copy(c)))
        for mine, arrival in sends:
            arrival.wait_recv()
            mine.wait_send()

    any_space = pl.BlockSpec(memory_space=pl.ANY)
    return pl.pallas_call(
        body, name=name, in_specs=[any_space] * n, out_specs=[any_space] * n,
        out_shape=[jax.ShapeDtypeStruct((N_CHIP,) + a.shape[1:], a.dtype) for a in arrays],
        scratch_shapes=[pltpu.SemaphoreType.DMA((N_CHIP * n,)), pltpu.SemaphoreType.DMA((N_CHIP * n,))],
    )(*arrays)


def _chip_pair_sum(part, got, name):
    R, C = part.shape[1:]
    tr = R
    while tr * C * part.dtype.itemsize > REDUCE_BLOCK_BYTES // 4 and tr % 32 == 0:
        tr //= 2
    c = lax.axis_index("c")

    def body(c_ref, p_ref, g_ref, o_ref):
        del c_ref
        o_ref[...] = (p_ref[...].astype(F32) + g_ref[...].astype(F32)).astype(o_ref.dtype)

    return pl.pallas_call(
        body, name=name, grid_spec=pltpu.PrefetchScalarGridSpec(
            num_scalar_prefetch=1, grid=(N_CHIP, R // tr),
            in_specs=[pl.BlockSpec((None, tr, C), lambda q, i, cr: (2 * q + cr[0], i, 0)),
                      pl.BlockSpec((None, tr, C), lambda q, i, cr: (q, i, 0))],
            out_specs=pl.BlockSpec((None, tr, C), lambda q, i, cr: (q, i, 0))),
        out_shape=jax.ShapeDtypeStruct((N_CHIP, R, C), part.dtype),
        compiler_params=_cp(),
    )(jnp.reshape(c, (1,)).astype(jnp.int32), part, got)


def _peer_count(mode):
    return N_CHIP - 1 if mode == "chips" else N_DEV - 1


def _remote_copies(srcs, lands, send_sems, recv_sems, mode):
    if mode == "chips":
        (x, y, _), _, others = _chip_places()
        my_slot, peers = 2 * x + y, [(chip, 2 * chip[0] + chip[1]) for chip in others]
    else:
        my_slot, peers = _my_index(), _peers()
    out = []
    for i, (pos, idx) in enumerate(peers):
        for a, (src, land) in enumerate(zip(srcs, lands)):
            def copy(slot, a=a, src=src, land=land, i=i, pos=pos, idx=idx):
                return pltpu.make_async_remote_copy(
                    src_ref=src if mode == "gather" else src.at[idx], dst_ref=land.at[slot],
                    send_sem=send_sems.at[a * len(peers) + i], recv_sem=recv_sems.at[a * len(peers) + i],
                    device_id=pos, device_id_type=pl.DeviceIdType.MESH)
            out.append((copy(my_slot), copy(idx)))
    return out


def _exchange_start(arrays, name, mode):
    n = len(arrays)
    hbm = pl.BlockSpec(memory_space=pltpu.HBM)
    sem = pl.BlockSpec(memory_space=pltpu.SEMAPHORE)
    lands = [lax.empty(((N_DEV,) + a.shape) if mode == "gather" else a.shape, a.dtype) for a in arrays]

    def body(*refs):
        srcs, lands_ = refs[:n], refs[n:2 * n]
        send_sems, recv_sems = refs[2 * n:2 * n + 2]
        for mine, _ in _remote_copies(srcs, lands_, send_sems, recv_sems, mode):
            mine.start()
        refs[-1][...] = jnp.zeros_like(refs[-1])

    sems = pltpu.SemaphoreType.DMA((n * _peer_count(mode),))
    buffers = [pltpu.HBM(a.shape, a.dtype) for a in list(arrays) + lands]
    res = pl.pallas_call(
        body, name=name, in_specs=[hbm] * (2 * n), out_specs=[sem, sem] + [hbm] * (2 * n) + [pl.BlockSpec(memory_space=pltpu.VMEM)],
        out_shape=[sems, sems] + buffers + [jax.ShapeDtypeStruct((8, LANE), F32)],
        input_output_aliases={i: 2 + i for i in range(2 * n)},
        compiler_params=pltpu.CompilerParams(has_side_effects=pltpu.SideEffectType.DATAFLOW_SIDE_EFFECTING),
    )(*[pltpu.with_memory_space_constraint(a, pltpu.HBM) for a in list(arrays) + lands])
    return (res[0], res[1], res[2:2 + n], res[2 + n:2 + 2 * n]), res[-1]


def _exchange_wait(handle, after, name, mode):
    send_sems, recv_sems, srcs, lands = handle
    n = len(srcs)
    hbm = pl.BlockSpec(memory_space=pltpu.HBM)
    sem = pl.BlockSpec(memory_space=pltpu.SEMAPHORE)

    def body(*refs):
        for mine, arrival in _remote_copies(refs[:n], refs[n:2 * n], refs[2 * n], refs[2 * n + 1], mode):
            mine.wait_send()
            arrival.wait_recv()

    res = pl.pallas_call(
        body, name=name, in_specs=[hbm] * (2 * n) + [sem, sem, pl.BlockSpec(memory_space=pl.ANY)],
        out_specs=[hbm] * (2 * n), out_shape=[pltpu.HBM(a.shape, a.dtype) for a in list(srcs) + list(lands)],
        input_output_aliases={i: i for i in range(2 * n)},
        compiler_params=pltpu.CompilerParams(has_side_effects=pltpu.SideEffectType.DATAFLOW_SIDE_EFFECTING),
    )(*srcs, *lands, send_sems, recv_sems, after)
    return res[n:]


def _own_slot(land, mine, slot=None):
    slot = _my_index() if slot is None else slot
    return lax.dynamic_update_slice(land, mine, (slot,) + (0,) * (land.ndim - 1))


def _adamw(w, g, m, v):
    m = ADAM_B1 * m + (1.0 - ADAM_B1) * g
    v = ADAM_B2 * v + (1.0 - ADAM_B2) * (g * g)
    m_hat = m / (1.0 - ADAM_B1 ** ADAM_STEP)
    v_hat = v / (1.0 - ADAM_B2 ** ADAM_STEP)
    delta = -ADAM_LR * (m_hat / (jnp.sqrt(v_hat) + ADAM_EPS) + ADAM_WD * w)
    return delta, m, v


def _reduce_adamw(parts, w, m, v, name):
    nparts = len(parts)
    R, C = parts[0].shape[1:]
    tr = R
    while N_DEV * tr * C * parts[0].dtype.itemsize > REDUCE_BLOCK_BYTES and tr % 32 == 0:
        tr //= 2
    steps = R // tr

    def body(*refs):
        w_ref, m_ref, v_ref, g_ref, d_ref, nm_ref, nv_ref = refs[nparts:]
        for k, p_ref in enumerate(refs[:nparts]):
            @pl.when(pl.program_id(0) // steps == k)
            def _():
                g = p_ref[0].astype(F32)
                for s in range(1, p_ref.shape[0]):
                    g = g + p_ref[s].astype(F32)
                g_ref[...] = g
                d_ref[...], nm_ref[...], nv_ref[...] = _adamw(w_ref[...], g, m_ref[...], v_ref[...])

    def part_spec(k):
        return pl.BlockSpec((parts[k].shape[0], tr, C), lambda i: (0, jnp.clip(i - k * steps, 0, steps - 1), 0))

    row = pl.BlockSpec((tr, C), lambda i: (i, 0))
    return pl.pallas_call(
        body, name=name, grid=(nparts * steps,),
        in_specs=[part_spec(k) for k in range(nparts)] + [row, row, row],
        out_specs=[row] * 4, out_shape=[jax.ShapeDtypeStruct((nparts * R, C), F32)] * 4,
        compiler_params=_cp(),
    )(*parts, w, m, v)


BIG = ("w_in", "w_uq", "w_ukv", "w_out_a", "w_out_b", "w_out_c", "w_o")
SMALL = ("norm_g", "b_gate", "conv_w", "conv_b", "q_a_norm_g", "kv_a_norm_g", "mla_q_norm_g", "mla_k_norm_g",
         "dil_q_norm_g", "dil_k_norm_g")
PACK_ROWS = 128
REDUCE_BLOCK_BYTES = 6 * 1024 * 1024


def _pack_local(tensors):
    flat = jnp.concatenate([t.reshape(-1) for t in tensors])
    pad = (-flat.shape[0]) % (PACK_ROWS * LANE)
    return jnp.concatenate([flat, jnp.zeros((pad,), flat.dtype)]).reshape(-1, LANE)


def _unpack_local(rows, like):
    flat = rows.reshape(-1)
    out, off = [], 0
    for t in like:
        out.append(flat[off:off + t.size].reshape(t.shape))
        off += t.size
    return out


def _cols_to_slots(a):
    k = a.shape[0]
    return a.reshape(k, N_DEV, -1).transpose(1, 0, 2)


def _slots_to_cols(s):
    return s.transpose(1, 0, 2).reshape(s.shape[1], -1)


def _rope_tables(S):
    inv = ROPE_THETA ** (-jnp.arange(0, MLA_ROPE, 2, dtype=F32) / MLA_ROPE)
    ang = jnp.arange(S, dtype=F32)[:, None] * inv[None, :]
    cos, sin = jnp.cos(ang), jnp.sin(ang)
    one = jnp.ones((S, MLA_NOPE), F32)
    z16, z32, z64 = (jnp.zeros((S, n), F32) for n in (16, 32, 64))
    cosp = jnp.concatenate([one, cos, cos, jnp.ones((S, 32), F32)], axis=1)
    sa = jnp.concatenate([z64, -sin, z16, z32], axis=1)
    sb = jnp.concatenate([z64, z16, sin, z32], axis=1)
    return cosp, sa, sb


def _alibi_slopes():
    n = DIL_GROUPS * DIL_HEADS
    m = 2.0 ** (-8.0 * jnp.arange(1, n + 1, dtype=F32) / n)
    return m.reshape(DIL_GROUPS, NPAIR, 2)


def _pad_slots(s):
    n, k, c = s.shape
    return _slots_to_cols(jnp.concatenate([s, jnp.zeros((n, k, LANE - c), s.dtype)], axis=2))


def _layer_params(gw, small, l):
    p = {}
    p["wp"] = _pad_columns(gw["w_in"])
    p["norm_g"] = small["norm_g"][l][None]
    p["b_gate"] = small["b_gate"][l][None]
    p["conv_w"] = gw["conv_w"].transpose(1, 0, 2).reshape(CONV_K, CONV_WIDTH)
    p["conv_b"] = small["conv_b"][l][None]
    p["gq"] = small["q_a_norm_g"][l][None]
    p["gkv"] = small["kv_a_norm_g"][l][None]
    p["wuqp"] = _pad_slots(gw["w_uq"])
    kv = gw["w_ukv"]
    p["wkp"] = _pad_slots(kv[:, :, :MLA_NOPE])
    p["wv"] = kv[:, :, MLA_NOPE:].transpose(1, 0, 2).reshape(MLA_KV_LORA, MLA_HEADS * MLA_V)
    zpad = jnp.zeros((1, LANE - MLA_QK), F32)
    p["gmq"] = jnp.concatenate([small["mla_q_norm_g"][l][None], zpad], axis=1)
    p["gmk"] = jnp.concatenate([small["mla_k_norm_g"][l][None], zpad], axis=1)
    tile = lambda g: jnp.broadcast_to(g[:, None, :], (DIL_GROUPS, DIL_HEADS, DIL_HEAD_DIM)).reshape(1, DIL_QK)
    p["gdq"] = tile(small["dil_q_norm_g"][l])
    p["gdk"] = tile(small["dil_k_norm_g"][l])
    p["woa"], p["wob"], p["woc"] = (_slots_to_cols(gw[n]) for n in ("w_out_a", "w_out_b", "w_out_c"))
    p["wo"] = gw["w_o"].reshape(D_MODEL, D_MODEL)
    return p


def _layer_fwd(x, p, tabs, slopes, B, S):
    proj, ht = _inproj_fwd(x, p["norm_g"], p["wp"])
    ya = _mixa_fwd(proj, p["conv_w"], p["conv_b"], B, S)
    q, k, v = _mla_prep_fwd(proj, p["gq"], p["gkv"], p["wuqp"], p["wkp"], p["wv"], p["gmq"], p["gmk"], *tabs, S)
    ob, lse_b = _mla_attn_fwd(q, k, v, B, S)
    qn, kn = _dil_prep_fwd(proj, p["gdq"], p["gdk"])
    ogs, lses = [], []
    for gi in range(DIL_GROUPS):
        o, lse = _dil_attn_fwd(gi, slopes[gi], qn, kn, proj, B, S)
        ogs.append(o)
        lses.append(lse)
    out = _merge_fwd(x, proj, p["b_gate"], ya, ob, ogs, lses, p["woa"], p["wob"], p["woc"], p["wo"])
    saved = dict(x=x, proj=proj, ht=ht, ya=ya, q=q, k=k, v=v, ob=ob, lse_b=lse_b, qn=qn, kn=kn, ogs=ogs, lses=lses)
    return out, saved


def _layer_bwd(dout, sv, p, tabs, slopes, B, S, big_ready=None):
    proj = sv["proj"]
    (dproj, dya, dob, dlb, dg0, dg1, dg2, dl0, dl1, dl2, merged, dpa, dpb, dpc, yb, yc, dbg) = _merge_bwd(
        dout, proj, p["b_gate"], sv["ya"], sv["ob"], sv["ogs"], sv["lses"], p["woa"], p["wob"], p["woc"], p["wo"])
    g = {}
    g["w_o"] = _matmul_tn(merged, dout, "dw_o").reshape(N_DEV, D_MODEL // N_DEV, D_MODEL)
    g["w_out_a"] = _cols_to_slots(_matmul_tn(sv["ya"], dpa, "dw_out_a"))
    g["w_out_b"] = _cols_to_slots(_matmul_tn(yb, dpb, "dw_out_b"))
    g["w_out_c"] = _cols_to_slots(_matmul_tn(yc, dpc, "dw_out_c"))
    g["b_gate"] = dbg[0]
    dproj, st = _mixa_bwd(dproj, dya, proj, p["conv_w"], p["conv_b"], B, S)
    g["conv_w"] = st[0:CONV_K]
    g["conv_b"] = st[CONV_K]
    dq, dk, dv = _mla_attn_bwd(sv["q"], sv["k"], sv["v"], dob, sv["lse_b"], dlb, B, S)
    dproj, dwuqp, dwkp, dwv, dgq, dgkv, dgmq, dgmk = _mla_prep_bwd(
        dproj, dq, dk, dv, proj, p["gq"], p["gkv"], p["wuqp"], p["wkp"], p["wv"], p["gmq"], p["gmk"], *tabs, S)
    g["w_uq"] = _cols_to_slots(dwuqp)[:, :, :MLA_QK]
    g["w_ukv"] = jnp.concatenate([_cols_to_slots(dwkp)[:, :, :MLA_NOPE], _cols_to_slots(dwv)], axis=2)
    g["q_a_norm_g"], g["kv_a_norm_g"] = dgq[0], dgkv[0]
    g["mla_q_norm_g"], g["mla_k_norm_g"] = dgmq[0, :MLA_QK], dgmk[0, :MLA_QK]
    dqkv = None
    for gi, (dog, dlg) in enumerate(((dg0, dl0), (dg1, dl1), (dg2, dl2))):
        dqkv = _dil_attn_bwd(gi, slopes[gi], sv["qn"], sv["kn"], proj, dog, sv["lses"][gi], dlg, dqkv, B, S)
    dproj, dgdq, dgdk = _dil_prep_bwd(dproj, *dqkv, proj, p["gdq"], p["gdk"])
    g["dil_q_norm_g"] = dgdq.reshape(DIL_GROUPS, DIL_HEADS, DIL_HEAD_DIM).sum(axis=1)
    g["dil_k_norm_g"] = dgdk.reshape(DIL_GROUPS, DIL_HEADS, DIL_HEAD_DIM).sum(axis=1)
    g["w_in"] = _unpad_columns(_matmul_nn(sv["ht"], dproj, "dw_in"))
    token = None if big_ready is None else big_ready(g)
    dx, dng = _inproj_bwd_x(dproj, p["wp"], sv["x"], _after(token, p["norm_g"]), dout)
    g["norm_g"] = dng[0]
    return dx, g


def _after(token, a):
    return a if token is None else a + token[0:1, 0:1]


def _local_step(x, target, small, B, S, weights_of, grads_out, big_ready=None):
    tabs = _rope_tables(S)
    sl = _alibi_slopes()
    slopes = [sl[gi] * float(DIL_PATTERNS[gi][1]) for gi in range(DIL_GROUPS)]
    params, saved = [], []
    for l in range(DEPTH):
        gw, token = weights_of(l, x)
        p = _layer_params(gw, small, l)
        p["norm_g"] = _after(token, p["norm_g"])
        x, sv = _layer_fwd(x, p, tabs, slopes, B, S)
        params.append(p)
        saved.append(sv)
    dout, lparts = _loss_head(x, target)
    sq = jnp.sum(lparts[:, 0, 0])
    token = None
    for l in reversed(range(DEPTH)):
        p = dict(params[l], b_gate=_after(token, params[l]["b_gate"]))
        ready = None if big_ready is None else (lambda g, l=l: big_ready(l, g))
        dout, g = _layer_bwd(dout, saved[l], p, tabs, slopes, B, S, ready)
        token = grads_out(l, g, dout)
    return sq, dout


def kernel(x, norm_g, w_in, b_gate, conv_w, conv_b, q_a_norm_g, w_uq, kv_a_norm_g, w_ukv, mla_q_norm_g, mla_k_norm_g, dil_q_norm_g, dil_k_norm_g, w_out_a, w_out_b, w_out_c, w_o, loss_target, m_norm_g, m_w_in, m_b_gate, m_conv_w, m_conv_b, m_q_a_norm_g, m_w_uq, m_kv_a_norm_g, m_w_ukv, m_mla_q_norm_g, m_mla_k_norm_g, m_dil_q_norm_g, m_dil_k_norm_g, m_w_out_a, m_w_out_b, m_w_out_c, m_w_o, v_norm_g, v_w_in, v_b_gate, v_conv_w, v_conv_b, v_q_a_norm_g, v_w_uq, v_kv_a_norm_g, v_w_ukv, v_mla_q_norm_g, v_mla_k_norm_g, v_dil_q_norm_g, v_dil_k_norm_g, v_w_out_a, v_w_out_b, v_w_out_c, v_w_o):
    names = ("norm_g", "w_in", "b_gate", "conv_w", "conv_b", "q_a_norm_g", "w_uq", "kv_a_norm_g", "w_ukv",
             "mla_q_norm_g", "mla_k_norm_g", "dil_q_norm_g", "dil_k_norm_g", "w_out_a", "w_out_b", "w_out_c", "w_o")
    w = dict(zip(names, (norm_g, w_in, b_gate, conv_w, conv_b, q_a_norm_g, w_uq, kv_a_norm_g, w_ukv, mla_q_norm_g,
                         mla_k_norm_g, dil_q_norm_g, dil_k_norm_g, w_out_a, w_out_b, w_out_c, w_o)))
    m = dict(zip(names, (m_norm_g, m_w_in, m_b_gate, m_conv_w, m_conv_b, m_q_a_norm_g, m_w_uq, m_kv_a_norm_g, m_w_ukv,
                         m_mla_q_norm_g, m_mla_k_norm_g, m_dil_q_norm_g, m_dil_k_norm_g, m_w_out_a, m_w_out_b,
                         m_w_out_c, m_w_o)))
    v = dict(zip(names, (v_norm_g, v_w_in, v_b_gate, v_conv_w, v_conv_b, v_q_a_norm_g, v_w_uq, v_kv_a_norm_g, v_w_ukv,
                         v_mla_q_norm_g, v_mla_k_norm_g, v_dil_q_norm_g, v_dil_k_norm_g, v_w_out_a, v_w_out_b,
                         v_w_out_c, v_w_o)))
    B, S, _ = x.shape
    me = _my_index()
    cshard = CONV_WIDTH // N_DEV

    shards = [[w[n][l].astype(BF16) for n in BIG] for l in range(DEPTH)]
    state = {}

    def weights_of(l, after):
        if l == 0:
            got = _gather_two_level(shards[0] + [conv_w], "all_gather_weights_0")
            state["gather"], token = _exchange_start(shards[1], "all_gather_weights_1_start", "gather")
            state["conv_w"] = got[-1]
        else:
            landed = _exchange_wait(state["gather"], after, "all_gather_weights_1_wait", "gather")
            got, token = [_own_slot(a, s[None]) for a, s in zip(landed, shards[1])], None
        gw = dict(zip(BIG, got))
        gw["conv_w"] = state["conv_w"][:, l]
        return gw, token

    recv, small_parts = {}, {}
    my_chip = 2 * lax.axis_index("x") + lax.axis_index("y")

    def big_ready(l, g):
        send = [g[n].astype(BF16) for n in BIG]
        if l == DEPTH - 1:
            state["scatter"], token = _exchange_start(send, "exchange_weight_grads_1_start", "scatter")
        else:
            swapped = _sibling_swap(send, "exchange_weight_grads_0_sibling")
            send = [_chip_pair_sum(s, t, "chip_pair_sum_" + n) for n, s, t in zip(BIG, send, swapped)]
            state["chips"], token = _exchange_start(send, "exchange_weight_grads_0_start", "chips")
        state["sent", l] = send
        return token

    def grads_out(l, g, after):
        small_parts[l] = [g[n] for n in SMALL]
        if l == DEPTH - 1:
            return None
        for k, key, mode, slot in ((DEPTH - 1, "scatter", "scatter", me), (0, "chips", "chips", my_chip)):
            landed = _exchange_wait(state[key], after, f"exchange_weight_grads_{k}_wait", mode)
            mine = [lax.dynamic_slice_in_dim(s, slot, 1, axis=0) for s in state["sent", k]]
            recv[k] = [_own_slot(a, s, slot) for a, s in zip(landed, mine)]
        return None

    sq, grad_x = _local_step(x.reshape(B * S, D_MODEL), loss_target.reshape(B * S, D_MODEL), w, B, S,
                             weights_of, grads_out, big_ready)
    loss = lax.psum(sq * (0.5 / D_MODEL), AXES)

    res = {}
    for i, n in enumerate(BIG):
        rows = lambda a: a.reshape(-1, a.shape[-1])
        outs = _reduce_adamw([recv[l][i] for l in range(DEPTH)], rows(w[n]), rows(m[n]), rows(v[n]),
                             "reduce_adamw_" + n)
        res[n] = tuple(a.reshape(w[n].shape) for a in outs)
    part = {n: jnp.stack([small_parts[l][i] for l in range(DEPTH)]) for i, n in enumerate(SMALL)}

    def widen(t):
        return lax.dynamic_update_slice(jnp.zeros((DEPTH, CONV_K, CONV_WIDTH), F32), t, (0, 0, me * cshard))

    small_like = [part[n] for n in SMALL]
    pick = lambda d: [widen(d[n]) if n == "conv_w" else d[n] for n in SMALL]
    parts, = _exchange([_pack_local(small_like)], "all_gather_small_grads", gather=True)
    gs, ds, ms, vs = _reduce_adamw([parts], _pack_local(pick(w)), _pack_local(pick(m)), _pack_local(pick(v)),
                                   "reduce_adamw_small")
    for n, t in zip(SMALL, zip(*(_unpack_local(a, small_like) for a in (gs, ds, ms, vs)))):
        if n == "conv_w":
            t = tuple(lax.dynamic_slice(a, (0, 0, me * cshard), (DEPTH, CONV_K, cshard)) for a in t)
        res[n] = t

    out = [loss, grad_x.reshape(B, S, D_MODEL)]
    for i in range(4):
        out += [res[n][i] for n in names]
    return tuple(out)
```

```python
import jax
import jax.numpy as jnp
from jax import lax
from jax.experimental import pallas as pl
from jax.experimental.pallas import tpu as pltpu

F32 = jnp.float32
BF16 = jnp.bfloat16

D_MODEL = 1024
DEPTH = 2
CONV_WIDTH = 512
CONV_K = 3
MLA_HEADS = 8
MLA_Q_LORA = 256
MLA_KV_LORA = 128
MLA_NOPE = 64
MLA_ROPE = 32
MLA_V = 64
MLA_QK = MLA_NOPE + MLA_ROPE
ROPE_THETA = 10000.0
DIL_PATTERNS = ((128, 1), (512, 4), (2048, 16))
DIL_GROUPS = 3
DIL_HEADS = 8
DIL_HEAD_DIM = 64
DIL_WIDTH = DIL_HEADS * DIL_HEAD_DIM
DIL_QK = DIL_GROUPS * DIL_WIDTH
EPS = 1e-6
N_IN = 11168

ADAM_LR = 0.001
ADAM_B1 = 0.9
ADAM_B2 = 0.999
ADAM_EPS = 1e-08
ADAM_WD = 0.01
ADAM_STEP = 10

N_DEV = 8
AXES = ("x", "y", "c")
LANE = 128
HALF = 64
NPAIR = 4

CB_AB, CB_AC, CB_AX, CB_AZ = 0, 4, 8, 12
CB_CQ, CB_CKV, CB_KPE = 16, 18, 19
CB_BZ = 20
CB_DQ, CB_DK, CB_DV = 24, 36, 48
CB_CZ, CB_GATE = 60, 64
NCB = 88
PP = NCB * LANE
KPE_END = CB_KPE * LANE + MLA_ROPE
SHARD_COLS = N_IN // N_DEV
NEG = -1e30
VMEM_LIMIT = 56 * 1024 * 1024


def _pad_columns(shards):
    parts = []
    for p in range(N_DEV):
        cut = min(max(KPE_END - p * SHARD_COLS, 0), SHARD_COLS)
        if 0 < cut < SHARD_COLS:
            parts += [shards[p, :, :cut], jnp.zeros((shards.shape[1], LANE - MLA_ROPE), shards.dtype), shards[p, :, cut:]]
        else:
            parts.append(shards[p])
    return jnp.concatenate(parts, axis=1)


def _unpad_columns(wp):
    def columns(a, b):
        gap = LANE - MLA_ROPE
        if b <= KPE_END:
            return wp[:, a:b]
        if a >= KPE_END:
            return wp[:, a + gap:b + gap]
        return jnp.concatenate([wp[:, a:KPE_END], wp[:, KPE_END + gap:b + gap]], axis=1)

    return jnp.stack([columns(p * SHARD_COLS, (p + 1) * SHARD_COLS) for p in range(N_DEV)])


def _put_copies(stages, dst_ref, sems, slot, rows, cols):
    return [pltpu.make_async_copy(st.at[slot], dst_ref.at[rows, pl.ds(c0, st.shape[-1])], sems.at[slot, k])
            for k, (st, c0) in enumerate(zip(stages, cols))]


def _put_pipeline(step, nsteps, copies_of, fill):
    @pl.when(step >= 2)
    def _():
        for cp in copies_of(step - 2):
            cp.wait()

    fill(step % 2)
    for cp in copies_of(step):
        cp.start()

    @pl.when(step == nsteps - 1)
    def _():
        if nsteps >= 2:
            for cp in copies_of(step - 1):
                cp.wait()
        for cp in copies_of(step):
            cp.wait()


def _cp():
    return pltpu.CompilerParams(vmem_limit_bytes=VMEM_LIMIT)


def _rstd(x, n):
    return lax.rsqrt(jnp.sum(x * x, axis=-1, keepdims=True) * (1.0 / n) + EPS)


def _sigmoid(z):
    return 1.0 / (1.0 + jnp.exp(-z))


def _silu(z):
    return z * _sigmoid(z)


def _silu_and_grad(z):
    s = _sigmoid(z)
    return z * s, s * (1.0 + z * (1.0 - s))


def _mm(a, b):
    return jnp.dot(a.astype(BF16), b.astype(BF16), preferred_element_type=F32)


def _mm_nt(a, b):
    return lax.dot_general(a.astype(BF16), b.astype(BF16), (((1,), (1,)), ((), ())), preferred_element_type=F32)


def _mm_tn(a, b):
    return lax.dot_general(a.astype(BF16), b.astype(BF16), (((0,), (0,)), ((), ())), preferred_element_type=F32)


def _lane_lo(shape):
    return lax.broadcasted_iota(jnp.int32, shape, len(shape) - 1) < HALF


def _head_bcast_sum(x, terms=3):
    w = x.shape[-1]
    same = (lax.broadcasted_iota(jnp.int32, (w, w), 0) // HALF) == (lax.broadcasted_iota(jnp.int32, (w, w), 1) // HALF)
    ones = jnp.where(same, 1.0, 0.0).astype(jnp.bfloat16)
    total = None
    for _ in range(terms):
        term = x.astype(jnp.bfloat16)
        x = x - term.astype(F32)
        part = jnp.dot(term, ones, preferred_element_type=F32)
        total = part if total is None else total + part
    return total


def _rope(t, cos, sa, sb):
    return t * cos + pltpu.roll(t, LANE - 16, axis=1) * sa + pltpu.roll(t, 16, axis=1) * sb


def _rope_t(d, cos, sa, sb):
    return d * cos + pltpu.roll(d * sa, 16, axis=1) + pltpu.roll(d * sb, LANE - 16, axis=1)


def _shift_down(u, k):
    rows = lax.broadcasted_iota(jnp.int32, u.shape, 0)
    return jnp.where(rows >= k, pltpu.roll(u, k, axis=0), 0.0)


def _shift_up(u, k):
    n = u.shape[0]
    rows = lax.broadcasted_iota(jnp.int32, u.shape, 0)
    return jnp.where(rows < n - k, pltpu.roll(u, n - k, axis=0), 0.0)


def _tile(n, want):
    t = min(n, want)
    assert n % t == 0, (n, want)
    return t


def _inproj_fwd(x, g, wp):
    T = x.shape[0]
    tm, tn = _tile(T, 2048), 512

    def body(x_ref, g_ref, w_ref, proj_ref, ht_ref, h_ref):
        @pl.when(pl.program_id(1) == 0)
        def _():
            n = min(tm, 512)
            for r0 in range(0, tm, n):
                xv = x_ref[r0:r0 + n, :]
                h = xv * _rstd(xv, D_MODEL) * g_ref[...]
                h_ref[r0:r0 + n, :] = h.astype(BF16)
                ht_ref[:, r0:r0 + n] = h.T.astype(BF16)

        proj_ref[...] = jnp.dot(h_ref[...], w_ref[...], preferred_element_type=F32).astype(BF16)

    return pl.pallas_call(
        body, name="inproj_fwd", grid=(T // tm, PP // tn),
        in_specs=[pl.BlockSpec((tm, D_MODEL), lambda i, j: (i, 0)),
                  pl.BlockSpec((1, D_MODEL), lambda i, j: (0, 0)),
                  pl.BlockSpec((D_MODEL, tn), lambda i, j: (0, j))],
        out_specs=[pl.BlockSpec((tm, tn), lambda i, j: (i, j)),
                   pl.BlockSpec((D_MODEL, tm), lambda i, j: (0, i))],
        out_shape=[jax.ShapeDtypeStruct((T, PP), BF16), jax.ShapeDtypeStruct((D_MODEL, T), BF16)],
        scratch_shapes=[pltpu.VMEM((tm, D_MODEL), BF16)],
        compiler_params=_cp(),
    )(x, g, wp)


def _matmul_nn(at, b, name):
    K, T = at.shape
    N = b.shape[1]
    tt, tn = _tile(T, 1024), _tile(N, 2816)
    nk = T // tt

    def body(a_ref, b_ref, o_ref, acc_ref):
        k = pl.program_id(1)

        @pl.when(k == 0)
        def _():
            acc_ref[...] = jnp.zeros_like(acc_ref)

        acc_ref[...] += jnp.dot(a_ref[...], b_ref[...], preferred_element_type=F32)

        @pl.when(k == nk - 1)
        def _():
            o_ref[...] = acc_ref[...].astype(BF16)

    return pl.pallas_call(
        body, name=name, grid=(N // tn, nk),
        in_specs=[pl.BlockSpec((K, tt), lambda j, k: (0, k)),
                  pl.BlockSpec((tt, tn), lambda j, k: (k, j))],
        out_specs=pl.BlockSpec((K, tn), lambda j, k: (0, j)),
        out_shape=jax.ShapeDtypeStruct((K, N), BF16),
        scratch_shapes=[pltpu.VMEM((K, tn), F32)],
        compiler_params=_cp(),
    )(at, b)


def _matmul_tn(a, b, name):
    T, K = a.shape
    N = b.shape[1]
    tt, tn = _tile(T, 512), _tile(N, 1024)

    def body(a_ref, b_ref, o_ref):
        @pl.when(pl.program_id(1) == 0)
        def _():
            o_ref[...] = jnp.zeros_like(o_ref)

        o_ref[...] += _mm_tn(a_ref[...], b_ref[...])

    return pl.pallas_call(
        body, name=name, grid=(N // tn, T // tt),
        in_specs=[pl.BlockSpec((tt, K), lambda j, k: (k, 0)),
                  pl.BlockSpec((tt, tn), lambda j, k: (k, j))],
        out_specs=pl.BlockSpec((K, tn), lambda j, k: (0, j)),
        out_shape=jax.ShapeDtypeStruct((K, N), F32),
        compiler_params=_cp(),
    )(a, b)


def _inproj_bwd_x(dproj, wp, x, g, dout):
    T = x.shape[0]
    tm, tk = _tile(T, 1024), 1024
    nk = PP // tk

    def body(dp_ref, w_ref, x_ref, g_ref, do_ref, dx_ref, dg_ref, acc_ref):
        i, k = pl.program_id(0), pl.program_id(1)

        @pl.when(k == 0)
        def _():
            acc_ref[...] = jnp.zeros_like(acc_ref)

        @pl.when((k == 0) & (i == 0))
        def _():
            dg_ref[...] = jnp.zeros_like(dg_ref)

        acc_ref[...] += _mm_nt(dp_ref[...], w_ref[...])

        @pl.when(k == nk - 1)
        def _():
            dh = acc_ref[...]
            xv = x_ref[...]
            r = _rstd(xv, D_MODEL)
            gy = dh * g_ref[...]
            dot = jnp.sum(xv * gy, axis=-1, keepdims=True) * (1.0 / D_MODEL)
            dx_ref[...] = do_ref[...] + r * gy - xv * (r * r * r) * dot
            dg_ref[...] += jnp.sum(dh * xv * r, axis=0, keepdims=True)

    return pl.pallas_call(
        body, name="inproj_bwd_x", grid=(T // tm, nk),
        in_specs=[pl.BlockSpec((tm, tk), lambda i, k: (i, k)),
                  pl.BlockSpec((D_MODEL, tk), lambda i, k: (0, k)),
                  pl.BlockSpec((tm, D_MODEL), lambda i, k: (i, 0)),
                  pl.BlockSpec((1, D_MODEL), lambda i, k: (0, 0)),
                  pl.BlockSpec((tm, D_MODEL), lambda i, k: (i, 0))],
        out_specs=[pl.BlockSpec((tm, D_MODEL), lambda i, k: (i, 0)),
                   pl.BlockSpec((1, D_MODEL), lambda i, k: (0, 0))],
        out_shape=[jax.ShapeDtypeStruct((T, D_MODEL), F32), jax.ShapeDtypeStruct((1, D_MODEL), F32)],
        scratch_shapes=[pltpu.VMEM((tm, D_MODEL), F32)],
        compiler_params=_cp(),
    )(dproj, wp, x, g, dout)


A_SEGS = (CB_AB, CB_AC, CB_AX, CB_AZ)


def _mixa_fwd(proj, cw, cb, B, S):
    nc = CONV_WIDTH // LANE

    def body(ab_ref, ac_ref, ax_ref, az_ref, cw_ref, cb_ref, y_ref):
        ab, ac, ax, az = (r[...].astype(F32) for r in (ab_ref, ac_ref, ax_ref, az_ref))
        u = ac * ax
        conv = cb_ref[...] + cw_ref[0:1, :] * _shift_down(u, 2) + cw_ref[1:2, :] * _shift_down(u, 1) + cw_ref[2:3, :] * u
        y_ref[...] = (ab * conv * _silu(az)).astype(BF16)

    return pl.pallas_call(
        body, name="mixa_fwd", grid=(B, nc),
        in_specs=[pl.BlockSpec((S, LANE), lambda b, j, c0=c0: (b, c0 + j)) for c0 in A_SEGS]
                 + [pl.BlockSpec((CONV_K, LANE), lambda b, j: (0, j)),
                    pl.BlockSpec((1, LANE), lambda b, j: (0, j))],
        out_specs=pl.BlockSpec((S, LANE), lambda b, j: (b, j)),
        out_shape=jax.ShapeDtypeStruct((B * S, CONV_WIDTH), BF16),
        compiler_params=_cp(),
    )(proj, proj, proj, proj, cw, cb)


def _mixa_bwd(dproj, dy, proj, cw, cb, B, S):
    nc = CONV_WIDTH // LANE

    def body(dpin_ref, dy_ref, ab_ref, ac_ref, ax_ref, az_ref, cw_ref, cb_ref, dp_ref, st_ref, stage, sems):
        del dpin_ref
        j, b = pl.program_id(0), pl.program_id(1)
        ab, ac, ax, az = (r[...].astype(F32) for r in (ab_ref, ac_ref, ax_ref, az_ref))
        u = ac * ax
        u1, u2 = _shift_down(u, 1), _shift_down(u, 2)
        w0, w1, w2 = cw_ref[0:1, :], cw_ref[1:2, :], cw_ref[2:3, :]
        conv = cb_ref[...] + w0 * u2 + w1 * u1 + w2 * u
        s, ds_az = _silu_and_grad(az)
        d = dy_ref[...]
        dconv = d * ab * s
        du = w2 * dconv + w1 * _shift_up(dconv, 1) + w0 * _shift_up(dconv, 2)
        grads = (d * conv * s, du * ax, du * ac, d * ab * conv * ds_az)

        def fill(slot):
            for k, v in enumerate(grads):
                stage[slot, k] = v.astype(BF16)

        def copies_of(step):
            sj, sb = step // B, step % B
            return _put_copies([stage.at[:, k] for k in range(4)], dp_ref, sems, step % 2,
                               pl.ds(pl.multiple_of(sb * S, S), S),
                               [pl.multiple_of((c0 + sj) * LANE, LANE) for c0 in A_SEGS])

        _put_pipeline(j * B + b, nc * B, copies_of, fill)
        row = lax.broadcasted_iota(jnp.int32, (8, LANE), 0)
        st = jnp.zeros((8, LANE), F32)
        for r, v in enumerate((dconv * u2, dconv * u1, dconv * u, dconv)):
            st = st + jnp.where(row == r, jnp.sum(v, axis=0, keepdims=True), 0.0)

        @pl.when(pl.program_id(1) == 0)
        def _():
            st_ref[...] = st

        @pl.when(pl.program_id(1) != 0)
        def _():
            st_ref[...] += st

    return pl.pallas_call(
        body, name="mixa_bwd", grid=(nc, B),
        in_specs=[pl.BlockSpec(memory_space=pl.ANY),
                  pl.BlockSpec((S, LANE), lambda j, b: (b, j))]
                 + [pl.BlockSpec((S, LANE), lambda j, b, c0=c0: (b, c0 + j)) for c0 in A_SEGS]
                 + [pl.BlockSpec((CONV_K, LANE), lambda j, b: (0, j)),
                    pl.BlockSpec((1, LANE), lambda j, b: (0, j))],
        out_specs=[pl.BlockSpec(memory_space=pl.ANY),
                   pl.BlockSpec((8, LANE), lambda j, b: (0, j))],
        out_shape=[jax.ShapeDtypeStruct(dproj.shape, BF16), jax.ShapeDtypeStruct((8, CONV_WIDTH), F32)],
        scratch_shapes=[pltpu.VMEM((2, 4, S, LANE), BF16), pltpu.SemaphoreType.DMA((2, 4))],
        input_output_aliases={0: 0},
        compiler_params=_cp(),
    )(dproj, dy, proj, proj, proj, proj, cw, cb)


def _mla_prep_fwd(proj, gq, gkv, wuqp, wkp, wv, gmq, gmk, cos, sa, sb, S):
    T = proj.shape[0]
    ts = _tile(S, 512)
    ns = S // ts
    W = MLA_HEADS * LANE

    def body(p_ref, gq_ref, gkv_ref, wuq_ref, wk_ref, wv_ref, gmq_ref, gmk_ref, cos_ref, sa_ref, sb_ref,
             q_ref, k_ref, v_ref):
        cq = p_ref[:, 0:2 * LANE].astype(F32)
        ckv = p_ref[:, 2 * LANE:3 * LANE].astype(F32)
        kpe = pltpu.roll(p_ref[:, 3 * LANE:4 * LANE].astype(F32), HALF, axis=1)
        cqn = cq * _rstd(cq, MLA_Q_LORA) * gq_ref[...]
        ckn = (ckv * _rstd(ckv, MLA_KV_LORA) * gkv_ref[...]).astype(BF16)
        q0 = _mm(cqn, wuq_ref[...])
        kn = _mm(ckn, wk_ref[...])
        v_ref[...] = _mm(ckn, wv_ref[...]).astype(BF16)
        c, a, b = cos_ref[...], sa_ref[...], sb_ref[...]
        kpe_rot = _rope(kpe * gmk_ref[...], c, a, b)
        for h in range(MLA_HEADS):
            q0h = q0[:, h * LANE:(h + 1) * LANE]
            q_ref[h] = _rope(q0h * _rstd(q0h, MLA_QK) * gmq_ref[...], c, a, b).astype(BF16)
            knh = kn[:, h * LANE:(h + 1) * LANE]
            k_ref[h] = (_rstd(knh + kpe, MLA_QK) * (knh * gmk_ref[...] + kpe_rot)).astype(BF16)

    def whole(r, c):
        return pl.BlockSpec((r, c), lambda i: (0, 0))

    tab = pl.BlockSpec((ts, LANE), lambda i: (i % ns, 0))
    return pl.pallas_call(
        body, name="mla_prep_fwd", grid=(T // ts,),
        in_specs=[pl.BlockSpec((ts, 4 * LANE), lambda i: (i, CB_CQ // 4)),
                  whole(1, MLA_Q_LORA), whole(1, MLA_KV_LORA), whole(MLA_Q_LORA, W), whole(MLA_KV_LORA, W),
                  whole(MLA_KV_LORA, MLA_HEADS * MLA_V), whole(1, LANE), whole(1, LANE), tab, tab, tab],
        out_specs=[pl.BlockSpec((MLA_HEADS, ts, LANE), lambda i: (0, i, 0)),
                   pl.BlockSpec((MLA_HEADS, ts, LANE), lambda i: (0, i, 0)),
                   pl.BlockSpec((ts, MLA_HEADS * MLA_V), lambda i: (i, 0))],
        out_shape=[jax.ShapeDtypeStruct((MLA_HEADS, T, LANE), BF16), jax.ShapeDtypeStruct((MLA_HEADS, T, LANE), BF16),
                   jax.ShapeDtypeStruct((T, MLA_HEADS * MLA_V), BF16)],
        compiler_params=_cp(),
    )(proj, gq, gkv, wuqp, wkp, wv, gmq, gmk, cos, sa, sb)


def _mla_prep_bwd(dproj, dq, dk, dv, proj, gq, gkv, wuqp, wkp, wv, gmq, gmk, cos, sa, sb, S):
    T = proj.shape[0]
    ts = _tile(S, 256)
    ns = S // ts
    W = MLA_HEADS * LANE

    def body(dpin_ref, dq_ref, dk_ref, dv_ref, p_ref, gq_ref, gkv_ref, wuq_ref, wk_ref, wv_ref, gmq_ref, gmk_ref,
             cos_ref, sa_ref, sb_ref,
             dp_ref, dwuq_ref, dwk_ref, dwv_ref, dgq_ref, dgkv_ref, dgmq_ref, dgmk_ref, dq0_ref, dkn_ref):
        del dpin_ref

        @pl.when(pl.program_id(0) == 0)
        def _():
            for r in (dwuq_ref, dwk_ref, dwv_ref, dgq_ref, dgkv_ref, dgmq_ref, dgmk_ref):
                r[...] = jnp.zeros_like(r)

        cq = p_ref[:, 0:2 * LANE].astype(F32)
        ckv = p_ref[:, 2 * LANE:3 * LANE].astype(F32)
        kpe = pltpu.roll(p_ref[:, 3 * LANE:4 * LANE].astype(F32), HALF, axis=1)
        rq = _rstd(cq, MLA_Q_LORA)
        rkv = _rstd(ckv, MLA_KV_LORA)
        gq, gkv, gmq, gmk = gq_ref[...], gkv_ref[...], gmq_ref[...], gmk_ref[...]
        cqn = (cq * rq * gq).astype(BF16)
        ckn = (ckv * rkv * gkv).astype(BF16)
        q0 = _mm(cqn, wuq_ref[...])
        kn = _mm(ckn, wk_ref[...])
        c, a, b = cos_ref[...], sa_ref[...], sb_ref[...]
        lane = lax.broadcasted_iota(jnp.int32, (ts, LANE), 1)
        dgmq = jnp.zeros((1, LANE), F32)
        dgmk = jnp.zeros((1, LANE), F32)
        nope = lane < MLA_NOPE
        kpe_rot = _rope(kpe * gmk, c, a, b)
        dk_sum = jnp.zeros((ts, LANE), F32)
        back = jnp.zeros((ts, 1), F32)
        for h in range(MLA_HEADS):
            q0h = q0[:, h * LANE:(h + 1) * LANE]
            r = _rstd(q0h, MLA_QK)
            d1 = _rope_t(dq_ref[h], c, a, b)
            gy = d1 * gmq
            dq0_ref[:, h * LANE:(h + 1) * LANE] = (
                r * gy - q0h * (r * r * r) * (jnp.sum(q0h * gy, axis=-1, keepdims=True) * (1.0 / MLA_QK))).astype(BF16)
            dgmq = dgmq + jnp.sum(d1 * q0h * r, axis=0, keepdims=True)
            knh = kn[:, h * LANE:(h + 1) * LANE]
            dkh = dk_ref[h]
            r = _rstd(knh + kpe, MLA_QK)
            r3dot = (r * r * r) * (jnp.sum((knh * gmk + kpe_rot) * dkh, axis=-1, keepdims=True) * (1.0 / MLA_QK))
            dkn_ref[:, h * LANE:(h + 1) * LANE] = jnp.where(nope, r * gmk * dkh - knh * r3dot, 0.0).astype(BF16)
            dgmk = dgmk + jnp.sum(jnp.where(nope, dkh * knh * r, 0.0), axis=0, keepdims=True)
            dk_sum = dk_sum + r * dkh
            back = back + r3dot
        rot = jnp.where(nope | (lane >= MLA_QK), 0.0, _rope_t(dk_sum, c, a, b))
        dkpe = gmk * rot - kpe * back
        dgmk = dgmk + jnp.sum(kpe * rot, axis=0, keepdims=True)
        dq0 = dq0_ref[...]
        dkn = dkn_ref[...]
        dvv = dv_ref[...]
        dwuq_ref[...] += _mm_tn(cqn, dq0)
        dwk_ref[...] += _mm_tn(ckn, dkn)
        dwv_ref[...] += _mm_tn(ckn, dvv)
        dgmq_ref[...] += dgmq
        dgmk_ref[...] += dgmk
        dcqn = _mm_nt(dq0, wuq_ref[...])
        gy = dcqn * gq
        dp_ref[:, 0:2 * LANE] = (
            rq * gy - cq * (rq * rq * rq) * (jnp.sum(cq * gy, axis=-1, keepdims=True) * (1.0 / MLA_Q_LORA))).astype(BF16)
        dgq_ref[...] += jnp.sum(dcqn * cq * rq, axis=0, keepdims=True)
        dckn = _mm_nt(dkn, wk_ref[...]) + _mm_nt(dvv, wv_ref[...])
        gy = dckn * gkv
        dp_ref[:, 2 * LANE:3 * LANE] = (
            rkv * gy - ckv * (rkv * rkv * rkv) * (jnp.sum(ckv * gy, axis=-1, keepdims=True) * (1.0 / MLA_KV_LORA))).astype(BF16)
        dgkv_ref[...] += jnp.sum(dckn * ckv * rkv, axis=0, keepdims=True)
        dp_ref[:, 3 * LANE:4 * LANE] = pltpu.roll(dkpe, HALF, axis=1).astype(BF16)

    def whole(r, c):
        return pl.BlockSpec((r, c), lambda i: (0, 0))

    tab = pl.BlockSpec((ts, LANE), lambda i: (i % ns, 0))
    heads = pl.BlockSpec((MLA_HEADS, ts, LANE), lambda i: (0, i, 0))
    return pl.pallas_call(
        body, name="mla_prep_bwd", grid=(T // ts,),
        in_specs=[pl.BlockSpec(memory_space=pl.ANY), heads, heads,
                  pl.BlockSpec((ts, MLA_HEADS * MLA_V), lambda i: (i, 0)),
                  pl.BlockSpec((ts, 4 * LANE), lambda i: (i, CB_CQ // 4)),
                  whole(1, MLA_Q_LORA), whole(1, MLA_KV_LORA), whole(MLA_Q_LORA, W), whole(MLA_KV_LORA, W),
                  whole(MLA_KV_LORA, MLA_HEADS * MLA_V), whole(1, LANE), whole(1, LANE), tab, tab, tab],
        out_specs=[pl.BlockSpec((ts, 4 * LANE), lambda i: (i, CB_CQ // 4)),
                   whole(MLA_Q_LORA, W), whole(MLA_KV_LORA, W), whole(MLA_KV_LORA, MLA_HEADS * MLA_V),
                   whole(1, MLA_Q_LORA), whole(1, MLA_KV_LORA), whole(1, LANE), whole(1, LANE)],
        out_shape=[jax.ShapeDtypeStruct(dproj.shape, BF16),
                   jax.ShapeDtypeStruct((MLA_Q_LORA, W), F32), jax.ShapeDtypeStruct((MLA_KV_LORA, W), F32),
                   jax.ShapeDtypeStruct((MLA_KV_LORA, MLA_HEADS * MLA_V), F32),
                   jax.ShapeDtypeStruct((1, MLA_Q_LORA), F32), jax.ShapeDtypeStruct((1, MLA_KV_LORA), F32),
                   jax.ShapeDtypeStruct((1, LANE), F32), jax.ShapeDtypeStruct((1, LANE), F32)],
        scratch_shapes=[pltpu.VMEM((ts, W), BF16), pltpu.VMEM((ts, W), BF16)],
        input_output_aliases={0: 0},
        compiler_params=_cp(),
    )(dproj, dq, dk, dv, proj, gq, gkv, wuqp, wkp, wv, gmq, gmk, cos, sa, sb)


def _dil_prep_fwd(proj, gq, gk):
    T = proj.shape[0]
    ts = _tile(T, 512)

    def body(pq_ref, pk_ref, gq_ref, gk_ref, q_ref, k_ref):
        for c in range(NPAIR):
            cs = slice(c * LANE, (c + 1) * LANE)
            t = jnp.concatenate([pq_ref[:, cs], pk_ref[:, cs]], axis=1).astype(F32)
            y = t * lax.rsqrt(_head_bcast_sum(t * t, terms=2) * (1.0 / DIL_HEAD_DIM) + EPS)
            q_ref[:, cs] = (y[:, 0:LANE] * gq_ref[:, cs]).astype(BF16)
            k_ref[:, cs] = (y[:, LANE:2 * LANE] * gk_ref[:, cs]).astype(BF16)

    col = pl.BlockSpec((1, DIL_WIDTH), lambda i, g: (0, g))
    out = pl.BlockSpec((ts, DIL_WIDTH), lambda i, g: (i, g))
    seg = lambda c0: pl.BlockSpec((ts, DIL_WIDTH), lambda i, g: (i, c0 // NPAIR + g))
    return pl.pallas_call(
        body, name="dil_prep_fwd", grid=(T // ts, DIL_GROUPS),
        in_specs=[seg(CB_DQ), seg(CB_DK), col, col],
        out_specs=[out, out],
        out_shape=[jax.ShapeDtypeStruct((T, DIL_QK), BF16)] * 2,
        compiler_params=_cp(),
    )(proj, proj, gq, gk)


def _dil_prep_bwd(dproj, ddq, ddk, ddv, proj, gq, gk):
    T = proj.shape[0]
    ts = _tile(T, 512)
    nt = T // ts

    def body(dpin_ref, ddq_ref, ddk_ref, ddv_ref, pq_ref, pk_ref, gq_ref, gk_ref, dp_ref, dgq_ref, dgk_ref,
             stage, sems):
        del dpin_ref
        g, i = pl.program_id(0), pl.program_id(1)

        @pl.when(i == 0)
        def _():
            dgq_ref[...] = jnp.zeros_like(dgq_ref)
            dgk_ref[...] = jnp.zeros_like(dgk_ref)

        def fill(slot):
            stage[slot, 2] = ddv_ref[...].astype(BF16)
            for c in range(NPAIR):
                cs = slice(c * LANE, (c + 1) * LANE)
                t = jnp.concatenate([pq_ref[:, cs], pk_ref[:, cs]], axis=1).astype(F32)
                d = jnp.concatenate([ddq_ref[:, cs], ddk_ref[:, cs]], axis=1)
                gy = d * jnp.concatenate([gq_ref[:, cs], gk_ref[:, cs]], axis=1)
                r = lax.rsqrt(_head_bcast_sum(t * t, terms=2) * (1.0 / DIL_HEAD_DIM) + EPS)
                dot = _head_bcast_sum(t * gy, terms=2) * (1.0 / DIL_HEAD_DIM)
                dx = (r * gy - t * (r * r * r) * dot).astype(BF16)
                stage[slot, 0, :, cs] = dx[:, 0:LANE]
                stage[slot, 1, :, cs] = dx[:, LANE:2 * LANE]
                part = jnp.sum(d * t * r, axis=0, keepdims=True)
                dgq_ref[:, cs] += part[:, 0:LANE]
                dgk_ref[:, cs] += part[:, LANE:2 * LANE]

        def copies_of(step):
            sg, si = step // nt, step % nt
            return _put_copies([stage.at[:, k] for k in range(3)], dp_ref, sems, step % 2,
                               pl.ds(pl.multiple_of(si * ts, ts), ts),
                               [pl.multiple_of((c0 + NPAIR * sg) * LANE, LANE) for c0 in (CB_DQ, CB_DK, CB_DV)])

        _put_pipeline(g * nt + i, DIL_GROUPS * nt, copies_of, fill)

    col = pl.BlockSpec((1, DIL_WIDTH), lambda g, i: (0, g))
    tok = pl.BlockSpec((ts, DIL_WIDTH), lambda g, i: (i, g))
    seg = lambda c0: pl.BlockSpec((ts, DIL_WIDTH), lambda g, i: (i, c0 // NPAIR + g))
    return pl.pallas_call(
        body, name="dil_prep_bwd", grid=(DIL_GROUPS, nt),
        in_specs=[pl.BlockSpec(memory_space=pl.ANY), tok, tok, tok, seg(CB_DQ), seg(CB_DK), col, col],
        out_specs=[pl.BlockSpec(memory_space=pl.ANY), col, col],
        out_shape=[jax.ShapeDtypeStruct(dproj.shape, BF16), jax.ShapeDtypeStruct((1, DIL_QK), F32),
                   jax.ShapeDtypeStruct((1, DIL_QK), F32)],
        scratch_shapes=[pltpu.VMEM((2, 3, ts, DIL_WIDTH), BF16), pltpu.SemaphoreType.DMA((2, 3))],
        input_output_aliases={0: 0},
        compiler_params=_cp(),
    )(dproj, ddq, ddk, ddv, proj, proj, gq, gk)


COPY_ROWS = 256


def _to_classes(src_ref, dst_ref, d, L, scale=None):
    m = min(L, max(8, COPY_ROWS // d))
    for c0 in range(0, L, m):
        x = src_ref[c0 * d:(c0 + m) * d, :].astype(F32)
        if scale is not None:
            x = x * scale
        if d > 1:
            x = jnp.swapaxes(x.reshape(m, d, LANE), 0, 1)
        for r in range(d):
            dst_ref[r * L + c0:r * L + c0 + m, :] = (x[r] if d > 1 else x).astype(dst_ref.dtype)


def _from_classes(src_ref, dst_ref, d, L):
    n = min(L, COPY_ROWS)
    for r in range(d):
        for c0 in range(0, L, n):
            rows = pl.ds(r + c0 * d, n, stride=d) if d > 1 else pl.ds(c0, n)
            dst_ref[rows, :] = src_ref[r * L + c0:r * L + c0 + n, :].astype(dst_ref.dtype)


MLA_TQ, MLA_TK = 512, 512


def _causal_bias(tq, tk, shift):
    row = lax.broadcasted_iota(jnp.int32, (tq, tk), 0)
    col = lax.broadcasted_iota(jnp.int32, (tq, tk), 1)
    return jnp.where(row >= col + shift, 0.0, NEG)


def _mla_specs(S):
    heads = pl.BlockSpec((2, S, LANE), lambda b, j: (j, b, 0))
    pair = pl.BlockSpec((S, LANE), lambda b, j: (b, j))
    return heads, pair


def _mla_attn_fwd(q, k, v, B, S):
    tq = _tile(S, MLA_TQ)
    tk = _tile(tq, MLA_TK)
    nd = tq // tk
    scale = MLA_QK ** -0.5
    heads, pair = _mla_specs(S)

    def body(q_ref, k_ref, v_ref, o_ref, lse_ref):
        lo, lok = _lane_lo((tq, LANE)), _lane_lo((tk, LANE))
        diag = [_causal_bias(tq, tk, i * tk) for i in range(nd)]

        def block(g, _):
            row0 = pl.multiple_of(g * tq, tq)
            rows = pl.ds(row0, tq)
            qs = [q_ref[hh, rows, :] for hh in range(2)]

            one = jnp.ones((), BF16)

            def step(off, carries, bias):
                off = pl.multiple_of(off, tk)
                vt = v_ref[pl.ds(off, tk), :]
                vh = (jnp.where(lok, vt, one), jnp.where(lok, one, vt))
                out = []
                for hh, (m, acc) in enumerate(carries):
                    s = _mm_nt(qs[hh], k_ref[hh, pl.ds(off, tk), :]) * scale
                    if bias is not None:
                        s = s + bias
                    m_new = jnp.maximum(m, jnp.max(s, axis=-1, keepdims=True))
                    p = jnp.exp(s - m_new)
                    out.append((m_new, jnp.exp(m - m_new) * acc + _mm(p, vh[hh])))
                return tuple(out)

            init = (jnp.full((tq, 1), NEG, F32), jnp.zeros((tq, LANE), F32))
            carries = lax.fori_loop(0, g * nd, lambda i, c: step(i * tk, c, None), (init, init))
            for i in range(nd):
                carries = step(row0 + i * tk, carries, diag[i])
            (ma, acca), (mb, accb) = carries
            la, lb = pltpu.roll(acca, HALF, axis=1), pltpu.roll(accb, HALF, axis=1)
            o_ref[rows, :] = jnp.where(lo, acca / la, accb / lb)
            lse_ref[rows, :] = jnp.where(lo, ma + jnp.log(la), mb + jnp.log(lb))
            return 0

        lax.fori_loop(0, S // tq, block, 0)

    return pl.pallas_call(
        body, name="mla_attn_fwd", grid=(B, NPAIR), in_specs=[heads, heads, pair], out_specs=[pair, pair],
        out_shape=[jax.ShapeDtypeStruct((B * S, MLA_HEADS * MLA_V), F32)] * 2,
        compiler_params=_cp(),
    )(q, k, v)


DIL_UNROLL = 16


def _dil_geometry(gi, S):
    span, d = DIL_PATTERNS[gi]
    L = S // d
    t = _tile(L, 128)
    window = span // d
    back = min(-(-window // t) * t, L - t)
    return d, L, t, window, back


def _dil_specs(gi, S):
    qk = pl.BlockSpec((S, LANE), lambda b, j: (b, NPAIR * gi + j))
    v = pl.BlockSpec((S, LANE), lambda b, j: (b, CB_DV + NPAIR * gi + j))
    pair = pl.BlockSpec((S, LANE), lambda b, j: (b, j))
    return qk, v, pair


def _dil_bias(bias_ref, sl_ref, j, t, kw, back, window):
    row = lax.broadcasted_iota(jnp.int32, (2 * t, kw), 0)
    col = lax.broadcasted_iota(jnp.int32, (2 * t, kw), 1)
    second = row >= t
    slope = jnp.where(second, sl_ref[j, 1], sl_ref[j, 0])
    for n in range(bias_ref.shape[0]):
        dist = jnp.where(second, row - t, row) + n * back - col
        bias_ref[n] = jnp.where((dist >= 0) & (dist <= window), -slope * dist.astype(F32), NEG)


def _stack_heads(x, lo):
    zero = jnp.zeros((), x.dtype)
    return jnp.concatenate([jnp.where(lo, x, zero), jnp.where(lo, zero, x)], axis=0)


def _dil_attn_fwd(gi, slopes, qn, kn, proj, B, S):
    d, L, t, window, back = _dil_geometry(gi, S)
    kw, nq = back + t, L // t
    nbias = 2 if back else 1
    qk, vspec, pair = _dil_specs(gi, S)

    def body(sl_ref, q_ref, k_ref, v_ref, o_ref, lse_ref, qs, ks, vs, os_, ls, bias_ref):
        _to_classes(q_ref, qs, d, L, DIL_HEAD_DIM ** -0.5)
        _to_classes(k_ref, ks, d, L)
        _to_classes(v_ref, vs, d, L)
        _dil_bias(bias_ref, sl_ref, pl.program_id(1), t, kw, back, window)
        lo = _lane_lo((t, LANE))

        def block(g, _):
            qb = g % nq if d > 1 else g
            row0 = pl.multiple_of(g * t, t)
            rows = pl.ds(row0, t)
            early = qb * t < back
            keys = pl.ds(pl.multiple_of(jnp.where(early, row0 - qb * t, row0 - back), t), kw)
            s = _mm_nt(_stack_heads(qs[rows, :], lo), ks[keys, :]) + bias_ref[jnp.where(early, 0, nbias - 1)]
            m = jnp.max(s, axis=-1, keepdims=True)
            p = jnp.exp(s - m)
            l = jnp.sum(p, axis=-1, keepdims=True)
            o2 = _mm(p, vs[keys, :]) / l
            lse2 = m + jnp.log(l)
            os_[rows, :] = jnp.where(lo, o2[:t], o2[t:])
            ls[rows, :] = jnp.where(lo, lse2[:t], lse2[t:])
            return 0

        lax.fori_loop(0, d * nq, block, 0, unroll=DIL_UNROLL if d * nq % DIL_UNROLL == 0 else 1)
        _from_classes(os_, o_ref, d, L)
        _from_classes(ls, lse_ref, d, L)

    return pl.pallas_call(
        body, name=f"dil_attn_fwd_{gi}", grid=(B, NPAIR),
        in_specs=[pl.BlockSpec(memory_space=pltpu.SMEM), qk, qk, vspec], out_specs=[pair, pair],
        out_shape=[jax.ShapeDtypeStruct((B * S, DIL_WIDTH), F32)] * 2,
        scratch_shapes=[pltpu.VMEM((S, LANE), BF16)] * 3 + [pltpu.VMEM((S, LANE), F32)] * 2
                       + [pltpu.VMEM((nbias, 2 * t, kw), F32)],
        compiler_params=_cp(),
    )(slopes, qn, kn, proj)


def _mla_attn_bwd(q, k, v, do, lse, delta, B, S):
    T = B * S
    tq = _tile(S, MLA_TQ)
    tk = _tile(tq, MLA_TK)
    nd = tq // tk
    scale = MLA_QK ** -0.5
    heads, pair = _mla_specs(S)

    def body(q_ref, k_ref, v_ref, do_ref, lse_ref, dl_ref, dq_ref, dk_ref, dv_ref):
        dk_ref[...] = jnp.zeros_like(dk_ref)
        dv_ref[...] = jnp.zeros_like(dv_ref)
        lo = _lane_lo((tq, LANE))
        diag = [_causal_bias(tq, tk, i * tk) for i in range(nd)]

        def block(g, _):
            row0 = pl.multiple_of(g * tq, tq)
            rows = pl.ds(row0, tq)
            per_head = []
            for hh in range(2):
                sel = lo if hh == 0 else jnp.logical_not(lo)
                per_head.append((q_ref[hh, rows, :], jnp.where(sel, do_ref[rows, :], jnp.zeros((), BF16)),
                                 jnp.max(jnp.where(sel, lse_ref[rows, :], NEG), axis=-1, keepdims=True),
                                 jnp.max(jnp.where(sel, dl_ref[rows, :], NEG), axis=-1, keepdims=True)))

            def step(off, dq_accs, bias):
                cols = pl.ds(pl.multiple_of(off, tk), tk)
                vt = v_ref[cols, :]
                out, dv = [], None
                for hh, (qh, doh, lse_h, dl_h) in enumerate(per_head):
                    kh = k_ref[hh, cols, :]
                    s = _mm_nt(qh, kh) * scale
                    if bias is not None:
                        s = s + bias
                    p = jnp.exp(s - lse_h)
                    ds = (p * (_mm_nt(doh, vt) - dl_h)).astype(BF16)
                    dk_ref[hh, cols, :] += _mm_tn(ds, qh) * scale
                    part = _mm_tn(p, doh)
                    dv = part if dv is None else dv + part
                    out.append(dq_accs[hh] + _mm(ds, kh))
                dv_ref[cols, :] += dv
                return tuple(out)

            zero = jnp.zeros((tq, LANE), F32)
            dq_accs = lax.fori_loop(0, g * nd, lambda i, a: step(i * tk, a, None), (zero, zero))
            for i in range(nd):
                dq_accs = step(row0 + i * tk, dq_accs, diag[i])
            for hh in range(2):
                dq_ref[hh, rows, :] = dq_accs[hh] * scale
            return 0

        lax.fori_loop(0, S // tq, block, 0)

    return pl.pallas_call(
        body, name="mla_attn_bwd", grid=(B, NPAIR), in_specs=[heads, heads, pair, pair, pair, pair],
        out_specs=[heads, heads, pair],
        out_shape=[jax.ShapeDtypeStruct((MLA_HEADS, T, LANE), F32), jax.ShapeDtypeStruct((MLA_HEADS, T, LANE), F32),
                   jax.ShapeDtypeStruct((T, MLA_HEADS * MLA_V), F32)],
        compiler_params=_cp(),
    )(q, k, v, do, lse, delta)


def _dil_attn_bwd(gi, slopes, qn, kn, proj, do, lse, delta, through, B, S):
    d, L, t, window, back = _dil_geometry(gi, S)
    kw, nq = back + t, L // t
    nbias = 2 if back else 1
    scale = DIL_HEAD_DIM ** -0.5
    qk, vspec, pair = _dil_specs(gi, S)

    def body(*refs):
        refs = list(refs)
        sl_ref, q_ref, k_ref, v_ref, do_ref, lse_ref, dl_ref = refs[:7]
        dq_ref, dk_ref, dv_ref, qs, ks, vs, dos, lss, dls, dqs, dks, dvs, bias_ref = refs[-13:]
        _to_classes(q_ref, qs, d, L, scale)
        for src, dst in ((k_ref, ks), (v_ref, vs), (do_ref, dos), (lse_ref, lss), (dl_ref, dls)):
            _to_classes(src, dst, d, L)
        _dil_bias(bias_ref, sl_ref, pl.program_id(1), t, kw, back, window)
        dks[...] = jnp.zeros_like(dks)
        dvs[...] = jnp.zeros_like(dvs)
        lo = _lane_lo((t, LANE))

        def stats(ref, rows):
            x = ref[rows, :]
            return jnp.concatenate([jnp.max(jnp.where(lo, x, NEG), axis=-1, keepdims=True),
                                    jnp.max(jnp.where(lo, NEG, x), axis=-1, keepdims=True)], axis=0)

        def block(g, _):
            qb = g % nq if d > 1 else g
            row0 = pl.multiple_of(g * t, t)
            rows = pl.ds(row0, t)
            early = qb * t < back
            keys = pl.ds(pl.multiple_of(jnp.where(early, row0 - qb * t, row0 - back), t), kw)
            q2 = _stack_heads(qs[rows, :], lo)
            do2 = _stack_heads(dos[rows, :], lo)
            kt = ks[keys, :]
            s = _mm_nt(q2, kt) + bias_ref[jnp.where(early, 0, nbias - 1)]
            p = jnp.exp(s - stats(lss, rows))
            ds = (p * (_mm_nt(do2, vs[keys, :]) - stats(dls, rows))).astype(BF16)
            dq2 = _mm(ds, kt) * scale
            dqs[rows, :] = jnp.where(lo, dq2[:t], dq2[t:])
            dks[keys, :] += _mm_tn(ds, q2)
            dvs[keys, :] += _mm_tn(p, do2)
            return 0

        lax.fori_loop(0, d * nq, block, 0, unroll=DIL_UNROLL if d * nq % DIL_UNROLL == 0 else 1)
        for src, dst in ((dqs, dq_ref), (dks, dk_ref), (dvs, dv_ref)):
            _from_classes(src, dst, d, L)

    in_specs = [pl.BlockSpec(memory_space=pltpu.SMEM), qk, qk, vspec, pair, pair, pair]
    args = [slopes, qn, kn, proj, do, lse, delta]
    aliases = {}
    if through is not None:
        aliases = {len(args) + i: i for i in range(3)}
        in_specs = in_specs + [pl.BlockSpec(memory_space=pl.ANY)] * 3
        args = args + list(through)
    return pl.pallas_call(
        body, name=f"dil_attn_bwd_{gi}", grid=(B, NPAIR), in_specs=in_specs, out_specs=[qk, qk, qk],
        out_shape=[jax.ShapeDtypeStruct((B * S, DIL_QK), F32)] * 3,
        scratch_shapes=[pltpu.VMEM((S, LANE), BF16)] * 4 + [pltpu.VMEM((S, LANE), F32)] * 5
                       + [pltpu.VMEM((nbias, 2 * t, kw), F32)],
        input_output_aliases=aliases,
        compiler_params=_cp(),
    )(*args)


def _merge_proj_specs(ts):
    wide = lambda c0, w: pl.BlockSpec((ts, w), lambda i: (i, c0 * LANE // w))
    return [wide(CB_BZ, DIL_WIDTH), wide(CB_CZ, DIL_WIDTH)] + [wide(CB_GATE + 8 * i, D_MODEL) for i in range(3)]


def _merge_common(p_refs, bg_ref, ob_ref, og_refs, lse_refs):
    bz = p_refs[0][...].astype(F32)
    cz = p_refs[1][...].astype(F32)
    gates = [_sigmoid(p_refs[2 + i][...].astype(F32) + bg_ref[:, i * D_MODEL:(i + 1) * D_MODEL]) for i in range(3)]
    ob = ob_ref[...]
    lses = [r[...] for r in lse_refs]
    mx = jnp.maximum(jnp.maximum(lses[0], lses[1]), lses[2])
    es = [jnp.exp(v - mx) for v in lses]
    inv = 1.0 / (es[0] + es[1] + es[2])
    alphas = [e * inv for e in es]
    oc = alphas[0] * og_refs[0][...] + alphas[1] * og_refs[1][...] + alphas[2] * og_refs[2][...]
    return bz, cz, gates, ob, alphas, oc


def _merge_fwd(x, proj, b_gate, ya, ob, ogs, lses, woa, wob, woc, wo):
    T = x.shape[0]
    ts = _tile(T, 256)

    def body(x_ref, p0, p1, p2, p3, p4, bg_ref, ya_ref, ob_ref, og0, og1, og2, l0, l1, l2,
             woa_ref, wob_ref, woc_ref, wo_ref, out_ref):
        bz, cz, gates, obv, alphas, oc = _merge_common((p0, p1, p2, p3, p4), bg_ref, ob_ref, (og0, og1, og2),
                                                       (l0, l1, l2))
        yb = obv * _silu(bz)
        yc = oc * _silu(cz)
        merged = (gates[0] * _mm(ya_ref[...], woa_ref[...]) + gates[1] * _mm(yb, wob_ref[...])
                  + gates[2] * _mm(yc, woc_ref[...]))
        out_ref[...] = x_ref[...] + _mm(merged, wo_ref[...])

    def whole(r, c):
        return pl.BlockSpec((r, c), lambda i: (0, 0))

    tok = lambda w: pl.BlockSpec((ts, w), lambda i: (i, 0))
    return pl.pallas_call(
        body, name="merge_fwd", grid=(T // ts,),
        in_specs=[tok(D_MODEL)] + _merge_proj_specs(ts) + [whole(1, 3 * D_MODEL), tok(CONV_WIDTH)]
                 + [tok(DIL_WIDTH)] * 7 + [whole(CONV_WIDTH, D_MODEL)] * 3 + [whole(D_MODEL, D_MODEL)],
        out_specs=tok(D_MODEL),
        out_shape=jax.ShapeDtypeStruct((T, D_MODEL), F32),
        compiler_params=_cp(),
    )(x, *[proj] * 5, b_gate, ya, ob, *ogs, *lses, woa, wob, woc, wo)


def _merge_bwd(dout, proj, b_gate, ya, ob, ogs, lses, woa, wob, woc, wo):
    T = dout.shape[0]
    ts = _tile(T, 256)
    nt = T // ts

    def body(do_ref, p0, p1, p2, p3, p4, bg_ref, ya_ref, ob_ref, og0, og1, og2, l0, l1, l2,
             woa_ref, wob_ref, woc_ref, wo_ref,
             dp_ref, dya_ref, dob_ref, dlb_ref, dg0, dg1, dg2, dl0, dl1, dl2,
             mg_ref, dpa_ref, dpb_ref, dpc_ref, yb_ref, yc_ref, dbg_ref, st_bz, st_cz, st_gate, sems):
        step = pl.program_id(0)
        slot = step % 2

        def copies_of(s):
            return _put_copies([st_bz, st_cz, st_gate], dp_ref, sems, s % 2, pl.ds(pl.multiple_of(s * ts, ts), ts),
                               [CB_BZ * LANE, CB_CZ * LANE, CB_GATE * LANE])

        @pl.when(step >= 2)
        def _():
            for cp in copies_of(step - 2):
                cp.wait()

        bz, cz, gates, obv, alphas, oc = _merge_common((p0, p1, p2, p3, p4), bg_ref, ob_ref, (og0, og1, og2),
                                                       (l0, l1, l2))
        (sb, dsb), (sc, dsc) = _silu_and_grad(bz), _silu_and_grad(cz)
        yb = obv * sb
        yc = oc * sc
        ps = [_mm(ya_ref[...], woa_ref[...]), _mm(yb, wob_ref[...]), _mm(yc, woc_ref[...])]
        mg_ref[...] = (gates[0] * ps[0] + gates[1] * ps[1] + gates[2] * ps[2]).astype(BF16)
        yb_ref[...] = yb.astype(BF16)
        yc_ref[...] = yc.astype(BF16)
        dm = _mm_nt(do_ref[...], wo_ref[...])
        dps = []
        first = pl.program_id(0) == 0
        for i, dref in enumerate((dpa_ref, dpb_ref, dpc_ref)):
            g = gates[i]
            dpi = (dm * g).astype(BF16)
            dref[...] = dpi
            dps.append(dpi)
            dgp = dm * ps[i] * g * (1.0 - g)
            st_gate[slot, :, i * D_MODEL:(i + 1) * D_MODEL] = dgp.astype(BF16)
            part = jnp.sum(dgp, axis=0, keepdims=True)

            @pl.when(first)
            def _():
                dbg_ref[:, i * D_MODEL:(i + 1) * D_MODEL] = part

            @pl.when(jnp.logical_not(first))
            def _():
                dbg_ref[:, i * D_MODEL:(i + 1) * D_MODEL] += part

        dya_ref[...] = _mm_nt(dps[0], woa_ref[...])
        dyb = _mm_nt(dps[1], wob_ref[...])
        dyc = _mm_nt(dps[2], woc_ref[...])
        st_bz[slot] = (dyb * obv * dsb).astype(BF16)
        st_cz[slot] = (dyc * oc * dsc).astype(BF16)
        for cp in copies_of(step):
            cp.start()
        dob = dyb * sb
        doc = dyc * sc
        dob_ref[...] = dob.astype(BF16)
        for c in range(NPAIR):
            cs = slice(c * LANE, (c + 1) * LANE)
            dlb_ref[:, cs] = _head_bcast_sum(dob[:, cs] * obv[:, cs])
            dd = _head_bcast_sum(doc[:, cs] * oc[:, cs])
            for a, dref, lref in zip(alphas, (dg0, dg1, dg2), (dl0, dl1, dl2)):
                dref[:, cs] = (a[:, cs] * doc[:, cs]).astype(BF16)
                lref[:, cs] = a[:, cs] * dd

        @pl.when(step == nt - 1)
        def _():
            if nt >= 2:
                for cp in copies_of(step - 1):
                    cp.wait()
            for cp in copies_of(step):
                cp.wait()

    def whole(r, c):
        return pl.BlockSpec((r, c), lambda i: (0, 0))

    tok = lambda w: pl.BlockSpec((ts, w), lambda i: (i, 0))
    sd = jax.ShapeDtypeStruct
    W = DIL_WIDTH
    return pl.pallas_call(
        body, name="merge_bwd", grid=(nt,),
        in_specs=[tok(D_MODEL)] + _merge_proj_specs(ts) + [whole(1, 3 * D_MODEL), tok(CONV_WIDTH)] + [tok(W)] * 7
                 + [whole(CONV_WIDTH, D_MODEL)] * 3 + [whole(D_MODEL, D_MODEL)],
        out_specs=[pl.BlockSpec(memory_space=pl.ANY), tok(CONV_WIDTH), tok(W), tok(W)] + [tok(W)] * 6
                  + [tok(D_MODEL)] * 4 + [tok(W), tok(W), whole(1, 3 * D_MODEL)],
        out_shape=[sd((T, PP), BF16), sd((T, CONV_WIDTH), F32), sd((T, W), BF16), sd((T, W), F32)]
                  + [sd((T, W), BF16)] * 3 + [sd((T, W), F32)] * 3
                  + [sd((T, D_MODEL), BF16)] * 4 + [sd((T, W), BF16)] * 2 + [sd((1, 3 * D_MODEL), F32)],
        scratch_shapes=[pltpu.VMEM((2, ts, W), BF16), pltpu.VMEM((2, ts, W), BF16),
                        pltpu.VMEM((2, ts, 3 * D_MODEL), BF16), pltpu.SemaphoreType.DMA((2, 3))],
        compiler_params=_cp(),
    )(dout, *[proj] * 5, b_gate, ya, ob, *ogs, *lses, woa, wob, woc, wo)


def _loss_head(y, target):
    T = y.shape[0]
    ts = _tile(T, 512)

    def body(y_ref, t_ref, d_ref, l_ref):
        e = y_ref[...] - t_ref[...]
        d_ref[...] = e * (1.0 / D_MODEL)
        l_ref[...] = jnp.zeros((1, 8, LANE), F32) + jnp.sum(e * e)

    tok = pl.BlockSpec((ts, D_MODEL), lambda i: (i, 0))
    return pl.pallas_call(
        body, name="loss_head", grid=(T // ts,), in_specs=[tok, tok],
        out_specs=[tok, pl.BlockSpec((1, 8, LANE), lambda i: (i, 0, 0))],
        out_shape=[jax.ShapeDtypeStruct((T, D_MODEL), F32), jax.ShapeDtypeStruct((T // ts, 8, LANE), F32)],
        compiler_params=_cp(),
    )(y, target)


def _my_index():
    return 4 * lax.axis_index("x") + 2 * lax.axis_index("y") + lax.axis_index("c")


def _peers():
    x, y, c = (lax.axis_index(a) for a in AXES)
    out = []
    for kk in range(1, N_DEV):
        px = 1 - x if kk & 4 else x
        py = 1 - y if kk & 2 else y
        pc = 1 - c if kk & 1 else c
        out.append(((px, py, pc), 4 * px + 2 * py + pc))
    return out


def _exchange(arrays, name, gather):
    n = len(arrays)

    def body(*refs):
        srcs, outs = refs[:n], refs[n:2 * n]
        send_sems, recv_sems, local_sems = refs[2 * n:]
        me = _my_index()
        peers = _peers()
        started = []
        for a, (src, out) in enumerate(zip(srcs, outs)):
            mine = pltpu.make_async_copy(src if gather else src.at[me], out.at[me], local_sems.at[a])
            mine.start()
            started.append(mine)
        sends = []
        for i, (pos, idx) in enumerate(peers):
            for a, (src, out) in enumerate(zip(srcs, outs)):
                cp = pltpu.make_async_remote_copy(
                    src_ref=src if gather else src.at[idx], dst_ref=out.at[me], send_sem=send_sems.at[a, i],
                    recv_sem=recv_sems.at[a, i], device_id=pos, device_id_type=pl.DeviceIdType.MESH)
                cp.start()
                sends.append(cp)
        for i, (pos, idx) in enumerate(peers):
            for a, (src, out) in enumerate(zip(srcs, outs)):
                pltpu.make_async_remote_copy(
                    src_ref=src if gather else src.at[idx], dst_ref=out.at[idx], send_sem=send_sems.at[a, i],
                    recv_sem=recv_sems.at[a, i], device_id=pos, device_id_type=pl.DeviceIdType.MESH).wait_recv()
        for cp in sends:
            cp.wait_send()
        for mine in started:
            mine.wait()

    any_space = pl.BlockSpec(memory_space=pl.ANY)
    return pl.pallas_call(
        body, name=name, in_specs=[any_space] * n, out_specs=[any_space] * n,
        out_shape=[jax.ShapeDtypeStruct(((N_DEV,) + a.shape) if gather else a.shape, a.dtype) for a in arrays],
        scratch_shapes=[pltpu.SemaphoreType.DMA((n, N_DEV - 1)), pltpu.SemaphoreType.DMA((n, N_DEV - 1)),
                        pltpu.SemaphoreType.DMA((n,))],
    )(*arrays)


N_CHIP = 4


def _chip_places():
    x, y, c = (lax.axis_index(a) for a in AXES)
    return (x, y, c), (x, y, 1 - c), [(1 - x, y, c), (x, 1 - y, c), (1 - x, 1 - y, c)]


def _index_of(pos):
    return 4 * pos[0] + 2 * pos[1] + pos[2]


def _gather_two_level(arrays, name):
    n = len(arrays)

    def body(*refs):
        srcs, outs = refs[:n], refs[n:2 * n]
        send_sems, recv_sems, local_sems = refs[2 * n:]
        me, sibling, others = _chip_places()

        def copy(a, k, block, to, src=None):
            slot = outs[a].at[_index_of(block)]
            return pltpu.make_async_remote_copy(
                src_ref=slot if src is None else src, dst_ref=slot, send_sem=send_sems.at[7 * a + k],
                recv_sem=recv_sems.at[7 * a + k], device_id=to, device_id_type=pl.DeviceIdType.MESH)

        started = []
        for a, src in enumerate(srcs):
            mine = pltpu.make_async_copy(src, outs[a].at[_index_of(me)], local_sems.at[a])
            mine.start()
            started.append(mine)
        sends = []
        for a, src in enumerate(srcs):
            sends.append(copy(a, 0, me, sibling, src))
            sends += [copy(a, 1 + j, me, chip, src) for j, chip in enumerate(others)]
        for cp in sends:
            cp.start()
        for j, chip in enumerate(others):
            for a in range(n):
                copy(a, 1 + j, chip, me).wait_recv()
                fwd = copy(a, 4 + j, chip, sibling)
                fwd.start()
                sends.append(fwd)
        for a in range(n):
            copy(a, 0, sibling, me).wait_recv()
            for j, chip in enumerate(others):
                copy(a, 4 + j, (chip[0], chip[1], sibling[2]), me).wait_recv()
        for cp in sends:
            cp.wait_send()
        for mine in started:
            mine.wait()

    any_space = pl.BlockSpec(memory_space=pl.ANY)
    return pl.pallas_call(
        body, name=name, in_specs=[any_space] * n, out_specs=[any_space] * n,
        out_shape=[jax.ShapeDtypeStruct((N_DEV,) + a.shape, a.dtype) for a in arrays],
        scratch_shapes=[pltpu.SemaphoreType.DMA((7 * n,)), pltpu.SemaphoreType.DMA((7 * n,)),
                        pltpu.SemaphoreType.DMA((n,))],
    )(*arrays)


def _sibling_swap(arrays, name):
    n = len(arrays)

    def body(*refs):
        srcs, outs = refs[:n], refs[n:2 * n]
        send_sems, recv_sems = refs[2 * n:]
        (x, y, c), sibling, _ = _chip_places()
        sends = []
        for a, (src, out) in enumerate(zip(srcs, outs)):
            for q in range(N_CHIP):
                def copy(core, a=a, q=q, src=src, out=out):
                    return pltpu.make_async_remote_copy(
                        src_ref=src.at[2 * q + core], dst_ref=out.at[q], send_sem=send_sems.at[N_CHIP * a + q],
                        recv_sem=recv_sems.at[N_CHIP * a + q], device_id=sibling, device_id_type=pl.DeviceIdType.MESH)
                mine = copy(1 - c)
                mine.start()
                sends.append((mine, copy(c)))
        for mine, arrival in sends:
            arrival.wait_recv()
            mine.wait_send()

    any_space = pl.BlockSpec(memory_space=pl.ANY)
    return pl.pallas_call(
        body, name=name, in_specs=[any_space] * n, out_specs=[any_space] * n,
        out_shape=[jax.ShapeDtypeStruct((N_CHIP,) + a.shape[1:], a.dtype) for a in arrays],
        scratch_shapes=[pltpu.SemaphoreType.DMA((N_CHIP * n,)), pltpu.SemaphoreType.DMA((N_CHIP * n,))],
    )(*arrays)


def _chip_pair_sum(part, got, name):
    R, C = part.shape[1:]
    tr = R
    while tr * C * part.dtype.itemsize > REDUCE_BLOCK_BYTES // 4 and tr % 32 == 0:
        tr //= 2
    c = lax.axis_index("c")

    def body(c_ref, p_ref, g_ref, o_ref):
        del c_ref
        o_ref[...] = (p_ref[...].astype(F32) + g_ref[...].astype(F32)).astype(o_ref.dtype)

    return pl.pallas_call(
        body, name=name, grid_spec=pltpu.PrefetchScalarGridSpec(
            num_scalar_prefetch=1, grid=(N_CHIP, R // tr),
            in_specs=[pl.BlockSpec((None, tr, C), lambda q, i, cr: (2 * q + cr[0], i, 0)),
                      pl.BlockSpec((None, tr, C), lambda q, i, cr: (q, i, 0))],
            out_specs=pl.BlockSpec((None, tr, C), lambda q, i, cr: (q, i, 0))),
        out_shape=jax.ShapeDtypeStruct((N_CHIP, R, C), part.dtype),
        compiler_params=_cp(),
    )(jnp.reshape(c, (1,)).astype(jnp.int32), part, got)


def _peer_count(mode):
    return N_CHIP - 1 if mode == "chips" else N_DEV - 1


def _remote_copies(srcs, lands, send_sems, recv_sems, mode):
    if mode == "chips":
        (x, y, _), _, others = _chip_places()
        my_slot, peers = 2 * x + y, [(chip, 2 * chip[0] + chip[1]) for chip in others]
    else:
        my_slot, peers = _my_index(), _peers()
    out = []
    for i, (pos, idx) in enumerate(peers):
        for a, (src, land) in enumerate(zip(srcs, lands)):
            def copy(slot, a=a, src=src, land=land, i=i, pos=pos, idx=idx):
                return pltpu.make_async_remote_copy(
                    src_ref=src if mode == "gather" else src.at[idx], dst_ref=land.at[slot],
                    send_sem=send_sems.at[a * len(peers) + i], recv_sem=recv_sems.at[a * len(peers) + i],
                    device_id=pos, device_id_type=pl.DeviceIdType.MESH)
            out.append((copy(my_slot), copy(idx)))
    return out


def _exchange_start(arrays, name, mode):
    n = len(arrays)
    hbm = pl.BlockSpec(memory_space=pltpu.HBM)
    sem = pl.BlockSpec(memory_space=pltpu.SEMAPHORE)
    lands = [lax.empty(((N_DEV,) + a.shape) if mode == "gather" else a.shape, a.dtype) for a in arrays]

    def body(*refs):
        srcs, lands_ = refs[:n], refs[n:2 * n]
        send_sems, recv_sems = refs[2 * n:2 * n + 2]
        for mine, _ in _remote_copies(srcs, lands_, send_sems, recv_sems, mode):
            mine.start()
        refs[-1][...] = jnp.zeros_like(refs[-1])

    sems = pltpu.SemaphoreType.DMA((n * _peer_count(mode),))
    buffers = [pltpu.HBM(a.shape, a.dtype) for a in list(arrays) + lands]
    res = pl.pallas_call(
        body, name=name, in_specs=[hbm] * (2 * n), out_specs=[sem, sem] + [hbm] * (2 * n) + [pl.BlockSpec(memory_space=pltpu.VMEM)],
        out_shape=[sems, sems] + buffers + [jax.ShapeDtypeStruct((8, LANE), F32)],
        input_output_aliases={i: 2 + i for i in range(2 * n)},
        compiler_params=pltpu.CompilerParams(has_side_effects=pltpu.SideEffectType.DATAFLOW_SIDE_EFFECTING),
    )(*[pltpu.with_memory_space_constraint(a, pltpu.HBM) for a in list(arrays) + lands])
    return (res[0], res[1], res[2:2 + n], res[2 + n:2 + 2 * n]), res[-1]


def _exchange_wait(handle, after, name, mode):
    send_sems, recv_sems, srcs, lands = handle
    n = len(srcs)
    hbm = pl.BlockSpec(memory_space=pltpu.HBM)
    sem = pl.BlockSpec(memory_space=pltpu.SEMAPHORE)

    def body(*refs):
        for mine, arrival in _remote_copies(refs[:n], refs[n:2 * n], refs[2 * n], refs[2 * n + 1], mode):
            mine.wait_send()
            arrival.wait_recv()

    res = pl.pallas_call(
        body, name=name, in_specs=[hbm] * (2 * n) + [sem, sem, pl.BlockSpec(memory_space=pl.ANY)],
        out_specs=[hbm] * (2 * n), out_shape=[pltpu.HBM(a.shape, a.dtype) for a in list(srcs) + list(lands)],
        input_output_aliases={i: i for i in range(2 * n)},
        compiler_params=pltpu.CompilerParams(has_side_effects=pltpu.SideEffectType.DATAFLOW_SIDE_EFFECTING),
    )(*srcs, *lands, send_sems, recv_sems, after)
    return res[n:]


def _own_slot(land, mine, slot=None):
    slot = _my_index() if slot is None else slot
    return lax.dynamic_update_slice(land, mine, (slot,) + (0,) * (land.ndim - 1))


def _adamw(w, g, m, v):
    m = ADAM_B1 * m + (1.0 - ADAM_B1) * g
    v = ADAM_B2 * v + (1.0 - ADAM_B2) * (g * g)
    m_hat = m / (1.0 - ADAM_B1 ** ADAM_STEP)
    v_hat = v / (1.0 - ADAM_B2 ** ADAM_STEP)
    delta = -ADAM_LR * (m_hat / (jnp.sqrt(v_hat) + ADAM_EPS) + ADAM_WD * w)
    return delta, m, v


def _reduce_adamw(parts, w, m, v, name):
    nparts = len(parts)
    R, C = parts[0].shape[1:]
    tr = R
    while N_DEV * tr * C * parts[0].dtype.itemsize > REDUCE_BLOCK_BYTES and tr % 32 == 0:
        tr //= 2
    steps = R // tr

    def body(*refs):
        w_ref, m_ref, v_ref, g_ref, d_ref, nm_ref, nv_ref = refs[nparts:]
        for k, p_ref in enumerate(refs[:nparts]):
            @pl.when(pl.program_id(0) // steps == k)
            def _():
                g = p_ref[0].astype(F32)
                for s in range(1, p_ref.shape[0]):
                    g = g + p_ref[s].astype(F32)
                g_ref[...] = g
                d_ref[...], nm_ref[...], nv_ref[...] = _adamw(w_ref[...], g, m_ref[...], v_ref[...])

    def part_spec(k):
        return pl.BlockSpec((parts[k].shape[0], tr, C), lambda i: (0, jnp.clip(i - k * steps, 0, steps - 1), 0))

    row = pl.BlockSpec((tr, C), lambda i: (i, 0))
    return pl.pallas_call(
        body, name=name, grid=(nparts * steps,),
        in_specs=[part_spec(k) for k in range(nparts)] + [row, row, row],
        out_specs=[row] * 4, out_shape=[jax.ShapeDtypeStruct((nparts * R, C), F32)] * 4,
        compiler_params=_cp(),
    )(*parts, w, m, v)


BIG = ("w_in", "w_uq", "w_ukv", "w_out_a", "w_out_b", "w_out_c", "w_o")
SMALL = ("norm_g", "b_gate", "conv_w", "conv_b", "q_a_norm_g", "kv_a_norm_g", "mla_q_norm_g", "mla_k_norm_g",
         "dil_q_norm_g", "dil_k_norm_g")
PACK_ROWS = 128
REDUCE_BLOCK_BYTES = 6 * 1024 * 1024


def _pack_local(tensors):
    flat = jnp.concatenate([t.reshape(-1) for t in tensors])
    pad = (-flat.shape[0]) % (PACK_ROWS * LANE)
    return jnp.concatenate([flat, jnp.zeros((pad,), flat.dtype)]).reshape(-1, LANE)


def _unpack_local(rows, like):
    flat = rows.reshape(-1)
    out, off = [], 0
    for t in like:
        out.append(flat[off:off + t.size].reshape(t.shape))
        off += t.size
    return out


def _cols_to_slots(a):
    k = a.shape[0]
    return a.reshape(k, N_DEV, -1).transpose(1, 0, 2)


def _slots_to_cols(s):
    return s.transpose(1, 0, 2).reshape(s.shape[1], -1)


def _rope_tables(S):
    inv = ROPE_THETA ** (-jnp.arange(0, MLA_ROPE, 2, dtype=F32) / MLA_ROPE)
    ang = jnp.arange(S, dtype=F32)[:, None] * inv[None, :]
    cos, sin = jnp.cos(ang), jnp.sin(ang)
    one = jnp.ones((S, MLA_NOPE), F32)
    z16, z32, z64 = (jnp.zeros((S, n), F32) for n in (16, 32, 64))
    cosp = jnp.concatenate([one, cos, cos, jnp.ones((S, 32), F32)], axis=1)
    sa = jnp.concatenate([z64, -sin, z16, z32], axis=1)
    sb = jnp.concatenate([z64, z16, sin, z32], axis=1)
    return cosp, sa, sb


def _alibi_slopes():
    n = DIL_GROUPS * DIL_HEADS
    m = 2.0 ** (-8.0 * jnp.arange(1, n + 1, dtype=F32) / n)
    return m.reshape(DIL_GROUPS, NPAIR, 2)


def _pad_slots(s):
    n, k, c = s.shape
    return _slots_to_cols(jnp.concatenate([s, jnp.zeros((n, k, LANE - c), s.dtype)], axis=2))


def _layer_params(gw, small, l):
    p = {}
    p["wp"] = _pad_columns(gw["w_in"])
    p["norm_g"] = small["norm_g"][l][None]
    p["b_gate"] = small["b_gate"][l][None]
    p["conv_w"] = gw["conv_w"].transpose(1, 0, 2).reshape(CONV_K, CONV_WIDTH)
    p["conv_b"] = small["conv_b"][l][None]
    p["gq"] = small["q_a_norm_g"][l][None]
    p["gkv"] = small["kv_a_norm_g"][l][None]
    p["wuqp"] = _pad_slots(gw["w_uq"])
    kv = gw["w_ukv"]
    p["wkp"] = _pad_slots(kv[:, :, :MLA_NOPE])
    p["wv"] = kv[:, :, MLA_NOPE:].transpose(1, 0, 2).reshape(MLA_KV_LORA, MLA_HEADS * MLA_V)
    zpad = jnp.zeros((1, LANE - MLA_QK), F32)
    p["gmq"] = jnp.concatenate([small["mla_q_norm_g"][l][None], zpad], axis=1)
    p["gmk"] = jnp.concatenate([small["mla_k_norm_g"][l][None], zpad], axis=1)
    tile = lambda g: jnp.broadcast_to(g[:, None, :], (DIL_GROUPS, DIL_HEADS, DIL_HEAD_DIM)).reshape(1, DIL_QK)
    p["gdq"] = tile(small["dil_q_norm_g"][l])
    p["gdk"] = tile(small["dil_k_norm_g"][l])
    p["woa"], p["wob"], p["woc"] = (_slots_to_cols(gw[n]) for n in ("w_out_a", "w_out_b", "w_out_c"))
    p["wo"] = gw["w_o"].reshape(D_MODEL, D_MODEL)
    return p


def _layer_fwd(x, p, tabs, slopes, B, S):
    proj, ht = _inproj_fwd(x, p["norm_g"], p["wp"])
    ya = _mixa_fwd(proj, p["conv_w"], p["conv_b"], B, S)
    q, k, v = _mla_prep_fwd(proj, p["gq"], p["gkv"], p["wuqp"], p["wkp"], p["wv"], p["gmq"], p["gmk"], *tabs, S)
    ob, lse_b = _mla_attn_fwd(q, k, v, B, S)
    qn, kn = _dil_prep_fwd(proj, p["gdq"], p["gdk"])
    ogs, lses = [], []
    for gi in range(DIL_GROUPS):
        o, lse = _dil_attn_fwd(gi, slopes[gi], qn, kn, proj, B, S)
        ogs.append(o)
        lses.append(lse)
    out = _merge_fwd(x, proj, p["b_gate"], ya, ob, ogs, lses, p["woa"], p["wob"], p["woc"], p["wo"])
    saved = dict(x=x, proj=proj, ht=ht, ya=ya, q=q, k=k, v=v, ob=ob, lse_b=lse_b, qn=qn, kn=kn, ogs=ogs, lses=lses)
    return out, saved


def _layer_bwd(dout, sv, p, tabs, slopes, B, S, big_ready=None):
    proj = sv["proj"]
    (dproj, dya, dob, dlb, dg0, dg1, dg2, dl0, dl1, dl2, merged, dpa, dpb, dpc, yb, yc, dbg) = _merge_bwd(
        dout, proj, p["b_gate"], sv["ya"], sv["ob"], sv["ogs"], sv["lses"], p["woa"], p["wob"], p["woc"], p["wo"])
    g = {}
    g["w_o"] = _matmul_tn(merged, dout, "dw_o").reshape(N_DEV, D_MODEL // N_DEV, D_MODEL)
    g["w_out_a"] = _cols_to_slots(_matmul_tn(sv["ya"], dpa, "dw_out_a"))
    g["w_out_b"] = _cols_to_slots(_matmul_tn(yb, dpb, "dw_out_b"))
    g["w_out_c"] = _cols_to_slots(_matmul_tn(yc, dpc, "dw_out_c"))
    g["b_gate"] = dbg[0]
    dproj, st = _mixa_bwd(dproj, dya, proj, p["conv_w"], p["conv_b"], B, S)
    g["conv_w"] = st[0:CONV_K]
    g["conv_b"] = st[CONV_K]
    dq, dk, dv = _mla_attn_bwd(sv["q"], sv["k"], sv["v"], dob, sv["lse_b"], dlb, B, S)
    dproj, dwuqp, dwkp, dwv, dgq, dgkv, dgmq, dgmk = _mla_prep_bwd(
        dproj, dq, dk, dv, proj, p["gq"], p["gkv"], p["wuqp"], p["wkp"], p["wv"], p["gmq"], p["gmk"], *tabs, S)
    g["w_uq"] = _cols_to_slots(dwuqp)[:, :, :MLA_QK]
    g["w_ukv"] = jnp.concatenate([_cols_to_slots(dwkp)[:, :, :MLA_NOPE], _cols_to_slots(dwv)], axis=2)
    g["q_a_norm_g"], g["kv_a_norm_g"] = dgq[0], dgkv[0]
    g["mla_q_norm_g"], g["mla_k_norm_g"] = dgmq[0, :MLA_QK], dgmk[0, :MLA_QK]
    dqkv = None
    for gi, (dog, dlg) in enumerate(((dg0, dl0), (dg1, dl1), (dg2, dl2))):
        dqkv = _dil_attn_bwd(gi, slopes[gi], sv["qn"], sv["kn"], proj, dog, sv["lses"][gi], dlg, dqkv, B, S)
    dproj, dgdq, dgdk = _dil_prep_bwd(dproj, *dqkv, proj, p["gdq"], p["gdk"])
    g["dil_q_norm_g"] = dgdq.reshape(DIL_GROUPS, DIL_HEADS, DIL_HEAD_DIM).sum(axis=1)
    g["dil_k_norm_g"] = dgdk.reshape(DIL_GROUPS, DIL_HEADS, DIL_HEAD_DIM).sum(axis=1)
    g["w_in"] = _unpad_columns(_matmul_nn(sv["ht"], dproj, "dw_in"))
    token = None if big_ready is None else big_ready(g)
    dx, dng = _inproj_bwd_x(dproj, p["wp"], sv["x"], _after(token, p["norm_g"]), dout)
    g["norm_g"] = dng[0]
    return dx, g


def _after(token, a):
    return a if token is None else a + token[0:1, 0:1]


def _local_step(x, target, small, B, S, weights_of, grads_out, big_ready=None):
    tabs = _rope_tables(S)
    sl = _alibi_slopes()
    slopes = [sl[gi] * float(DIL_PATTERNS[gi][1]) for gi in range(DIL_GROUPS)]
    params, saved = [], []
    for l in range(DEPTH):
        gw, token = weights_of(l, x)
        p = _layer_params(gw, small, l)
        p["norm_g"] = _after(token, p["norm_g"])
        x, sv = _layer_fwd(x, p, tabs, slopes, B, S)
        params.append(p)
        saved.append(sv)
    dout, lparts = _loss_head(x, target)
    sq = jnp.sum(lparts[:, 0, 0])
    token = None
    for l in reversed(range(DEPTH)):
        p = dict(params[l], b_gate=_after(token, params[l]["b_gate"]))
        ready = None if big_ready is None else (lambda g, l=l: big_ready(l, g))
        dout, g = _layer_bwd(dout, saved[l], p, tabs, slopes, B, S, ready)
        token = grads_out(l, g, dout)
    return sq, dout


def kernel(x, norm_g, w_in, b_gate, conv_w, conv_b, q_a_norm_g, w_uq, kv_a_norm_g, w_ukv, mla_q_norm_g, mla_k_norm_g, dil_q_norm_g, dil_k_norm_g, w_out_a, w_out_b, w_out_c, w_o, loss_target, m_norm_g, m_w_in, m_b_gate, m_conv_w, m_conv_b, m_q_a_norm_g, m_w_uq, m_kv_a_norm_g, m_w_ukv, m_mla_q_norm_g, m_mla_k_norm_g, m_dil_q_norm_g, m_dil_k_norm_g, m_w_out_a, m_w_out_b, m_w_out_c, m_w_o, v_norm_g, v_w_in, v_b_gate, v_conv_w, v_conv_b, v_q_a_norm_g, v_w_uq, v_kv_a_norm_g, v_w_ukv, v_mla_q_norm_g, v_mla_k_norm_g, v_dil_q_norm_g, v_dil_k_norm_g, v_w_out_a, v_w_out_b, v_w_out_c, v_w_o):
    names = ("norm_g", "w_in", "b_gate", "conv_w", "conv_b", "q_a_norm_g", "w_uq", "kv_a_norm_g", "w_ukv",
             "mla_q_norm_g", "mla_k_norm_g", "dil_q_norm_g", "dil_k_norm_g", "w_out_a", "w_out_b", "w_out_c", "w_o")
    w = dict(zip(names, (norm_g, w_in, b_gate, conv_w, conv_b, q_a_norm_g, w_uq, kv_a_norm_g, w_ukv, mla_q_norm_g,
                         mla_k_norm_g, dil_q_norm_g, dil_k_norm_g, w_out_a, w_out_b, w_out_c, w_o)))
    m = dict(zip(names, (m_norm_g, m_w_in, m_b_gate, m_conv_w, m_conv_b, m_q_a_norm_g, m_w_uq, m_kv_a_norm_g, m_w_ukv,
                         m_mla_q_norm_g, m_mla_k_norm_g, m_dil_q_norm_g, m_dil_k_norm_g, m_w_out_a, m_w_out_b,
                         m_w_out_c, m_w_o)))
    v = dict(zip(names, (v_norm_g, v_w_in, v_b_gate, v_conv_w, v_conv_b, v_q_a_norm_g, v_w_uq, v_kv_a_norm_g, v_w_ukv,
                         v_mla_q_norm_g, v_mla_k_norm_g, v_dil_q_norm_g, v_dil_k_norm_g, v_w_out_a, v_w_out_b,
                         v_w_out_c, v_w_o)))
    B, S, _ = x.shape
    me = _my_index()
    cshard = CONV_WIDTH // N_DEV

    shards = [[w[n][l].astype(BF16) for n in BIG] for l in range(DEPTH)]
    state = {}

    def weights_of(l, after):
        if l == 0:
            got = _gather_two_level(shards[0] + [conv_w], "all_gather_weights_0")
            state["gather"], token = _exchange_start(shards[1], "all_gather_weights_1_start", "gather")
            state["conv_w"] = got[-1]
        else:
            landed = _exchange_wait(state["gather"], after, "all_gather_weights_1_wait", "gather")
            got, token = [_own_slot(a, s[None]) for a, s in zip(landed, shards[1])], None
        gw = dict(zip(BIG, got))
        gw["conv_w"] = state["conv_w"][:, l]
        return gw, token

    recv, small_parts = {}, {}
    my_chip = 2 * lax.axis_index("x") + lax.axis_index("y")

    def big_ready(l, g):
        send = [g[n].astype(BF16) for n in BIG]
        if l == DEPTH - 1:
            state["scatter"], token = _exchange_start(send, "exchange_weight_grads_1_start", "scatter")
        else:
            swapped = _sibling_swap(send, "exchange_weight_grads_0_sibling")
            send = [_chip_pair_sum(s, t, "chip_pair_sum_" + n) for n, s, t in zip(BIG, send, swapped)]
            state["chips"], token = _exchange_start(send, "exchange_weight_grads_0_start", "chips")
        state["sent", l] = send
        return token

    def grads_out(l, g, after):
        small_parts[l] = [g[n] for n in SMALL]
        if l == DEPTH - 1:
            return None
        for k, key, mode, slot in ((DEPTH - 1, "scatter", "scatter", me), (0, "chips", "chips", my_chip)):
            landed = _exchange_wait(state[key], after, f"exchange_weight_grads_{k}_wait", mode)
            mine = [lax.dynamic_slice_in_dim(s, slot, 1, axis=0) for s in state["sent", k]]
            recv[k] = [_own_slot(a, s, slot) for a, s in zip(landed, mine)]
        return None

    sq, grad_x = _local_step(x.reshape(B * S, D_MODEL), loss_target.reshape(B * S, D_MODEL), w, B, S,
                             weights_of, grads_out, big_ready)
    loss = lax.psum(sq * (0.5 / D_MODEL), AXES)

    res = {}
    for i, n in enumerate(BIG):
        rows = lambda a: a.reshape(-1, a.shape[-1])
        outs = _reduce_adamw([recv[l][i] for l in range(DEPTH)], rows(w[n]), rows(m[n]), rows(v[n]),
                             "reduce_adamw_" + n)
        res[n] = tuple(a.reshape(w[n].shape) for a in outs)
    part = {n: jnp.stack([small_parts[l][i] for l in range(DEPTH)]) for i, n in enumerate(SMALL)}

    def widen(t):
        return lax.dynamic_update_slice(jnp.zeros((DEPTH, CONV_K, CONV_WIDTH), F32), t, (0, 0, me * cshard))

    small_like = [part[n] for n in SMALL]
    pick = lambda d: [widen(d[n]) if n == "conv_w" else d[n] for n in SMALL]
    parts, = _exchange([_pack_local(small_like)], "all_gather_small_grads", gather=True)
    gs, ds, ms, vs = _reduce_adamw([parts], _pack_local(pick(w)), _pack_local(pick(m)), _pack_local(pick(v)),
                                   "reduce_adamw_small")
    for n, t in zip(SMALL, zip(*(_unpack_local(a, small_like) for a in (gs, ds, ms, vs)))):
        if n == "conv_w":
            t = tuple(lax.dynamic_slice(a, (0, 0, me * cshard), (DEPTH, CONV_K, cshard)) for a in t)
        res[n] = t

    out = [loss, grad_x.reshape(B, S, D_MODEL)]
    for i in range(4):
        out += [res[n][i] for n in names]
    return tuple(out)
```

```python
import jax
import jax.numpy as jnp
from jax import lax
from jax.experimental import pallas as pl
from jax.experimental.pallas import tpu as pltpu

F32 = jnp.float32
BF16 = jnp.bfloat16

D_MODEL = 1024
DEPTH = 2
CONV_WIDTH = 512
CONV_K = 3
MLA_HEADS = 8
MLA_Q_LORA = 256
MLA_KV_LORA = 128
MLA_NOPE = 64
MLA_ROPE = 32
MLA_V = 64
MLA_QK = MLA_NOPE + MLA_ROPE
ROPE_THETA = 10000.0
DIL_PATTERNS = ((128, 1), (512, 4), (2048, 16))
DIL_GROUPS = 3
DIL_HEADS = 8
DIL_HEAD_DIM = 64
DIL_WIDTH = DIL_HEADS * DIL_HEAD_DIM
DIL_QK = DIL_GROUPS * DIL_WIDTH
EPS = 1e-6
N_IN = 11168

ADAM_LR = 0.001
ADAM_B1 = 0.9
ADAM_B2 = 0.999
ADAM_EPS = 1e-08
ADAM_WD = 0.01
ADAM_STEP = 10

N_DEV = 8
AXES = ("x", "y", "c")
LANE = 128
HALF = 64
NPAIR = 4

CB_AB, CB_AC, CB_AX, CB_AZ = 0, 4, 8, 12
CB_CQ, CB_CKV, CB_KPE = 16, 18, 19
CB_BZ = 20
CB_DQ, CB_DK, CB_DV = 24, 36, 48
CB_CZ, CB_GATE = 60, 64
NCB = 88
PP = NCB * LANE
KPE_END = CB_KPE * LANE + MLA_ROPE
SHARD_COLS = N_IN // N_DEV
NEG = -1e30
VMEM_LIMIT = 56 * 1024 * 1024


def _pad_columns(shards):
    parts = []
    for p in range(N_DEV):
        cut = min(max(KPE_END - p * SHARD_COLS, 0), SHARD_COLS)
        if 0 < cut < SHARD_COLS:
            parts += [shards[p, :, :cut], jnp.zeros((shards.shape[1], LANE - MLA_ROPE), shards.dtype), shards[p, :, cut:]]
        else:
            parts.append(shards[p])
    return jnp.concatenate(parts, axis=1)


def _unpad_columns(wp):
    def columns(a, b):
        gap = LANE - MLA_ROPE
        if b <= KPE_END:
            return wp[:, a:b]
        if a >= KPE_END:
            return wp[:, a + gap:b + gap]
        return jnp.concatenate([wp[:, a:KPE_END], wp[:, KPE_END + gap:b + gap]], axis=1)

    return jnp.stack([columns(p * SHARD_COLS, (p + 1) * SHARD_COLS) for p in range(N_DEV)])


def _put_copies(stages, dst_ref, sems, slot, rows, cols):
    return [pltpu.make_async_copy(st.at[slot], dst_ref.at[rows, pl.ds(c0, st.shape[-1])], sems.at[slot, k])
            for k, (st, c0) in enumerate(zip(stages, cols))]


def _put_pipeline(step, nsteps, copies_of, fill):
    @pl.when(step >= 2)
    def _():
        for cp in copies_of(step - 2):
            cp.wait()

    fill(step % 2)
    for cp in copies_of(step):
        cp.start()

    @pl.when(step == nsteps - 1)
    def _():
        if nsteps >= 2:
            for cp in copies_of(step - 1):
                cp.wait()
        for cp in copies_of(step):
            cp.wait()


def _cp():
    return pltpu.CompilerParams(vmem_limit_bytes=VMEM_LIMIT)


def _rstd(x, n):
    return lax.rsqrt(jnp.sum(x * x, axis=-1, keepdims=True) * (1.0 / n) + EPS)


def _sigmoid(z):
    return 1.0 / (1.0 + jnp.exp(-z))


def _silu(z):
    return z * _sigmoid(z)


def _silu_and_grad(z):
    s = _sigmoid(z)
    return z * s, s * (1.0 + z * (1.0 - s))


def _mm(a, b):
    return jnp.dot(a.astype(BF16), b.astype(BF16), preferred_element_type=F32)


def _mm_nt(a, b):
    return lax.dot_general(a.astype(BF16), b.astype(BF16), (((1,), (1,)), ((), ())), preferred_element_type=F32)


def _mm_tn(a, b):
    return lax.dot_general(a.astype(BF16), b.astype(BF16), (((0,), (0,)), ((), ())), preferred_element_type=F32)


def _lane_lo(shape):
    return lax.broadcasted_iota(jnp.int32, shape, len(shape) - 1) < HALF


def _head_bcast_sum(x, terms=3):
    w = x.shape[-1]
    same = (lax.broadcasted_iota(jnp.int32, (w, w), 0) // HALF) == (lax.broadcasted_iota(jnp.int32, (w, w), 1) // HALF)
    ones = jnp.where(same, 1.0, 0.0).astype(jnp.bfloat16)
    total = None
    for _ in range(terms):
        term = x.astype(jnp.bfloat16)
        x = x - term.astype(F32)
        part = jnp.dot(term, ones, preferred_element_type=F32)
        total = part if total is None else total + part
    return total


def _rope(t, cos, sa, sb):
    return t * cos + pltpu.roll(t, LANE - 16, axis=1) * sa + pltpu.roll(t, 16, axis=1) * sb


def _rope_t(d, cos, sa, sb):
    return d * cos + pltpu.roll(d * sa, 16, axis=1) + pltpu.roll(d * sb, LANE - 16, axis=1)


def _shift_down(u, k):
    rows = lax.broadcasted_iota(jnp.int32, u.shape, 0)
    return jnp.where(rows >= k, pltpu.roll(u, k, axis=0), 0.0)


def _shift_up(u, k):
    n = u.shape[0]
    rows = lax.broadcasted_iota(jnp.int32, u.shape, 0)
    return jnp.where(rows < n - k, pltpu.roll(u, n - k, axis=0), 0.0)


def _tile(n, want):
    t = min(n, want)
    assert n % t == 0, (n, want)
    return t


def _inproj_fwd(x, g, wp):
    T = x.shape[0]
    tm, tn = _tile(T, 2048), 512

    def body(x_ref, g_ref, w_ref, proj_ref, ht_ref, h_ref):
        @pl.when(pl.program_id(1) == 0)
        def _():
            n = min(tm, 512)
            for r0 in range(0, tm, n):
                xv = x_ref[r0:r0 + n, :]
                h = xv * _rstd(xv, D_MODEL) * g_ref[...]
                h_ref[r0:r0 + n, :] = h.astype(BF16)
                ht_ref[:, r0:r0 + n] = h.T.astype(BF16)

        proj_ref[...] = jnp.dot(h_ref[...], w_ref[...], preferred_element_type=F32).astype(BF16)

    return pl.pallas_call(
        body, name="inproj_fwd", grid=(T // tm, PP // tn),
        in_specs=[pl.BlockSpec((tm, D_MODEL), lambda i, j: (i, 0)),
                  pl.BlockSpec((1, D_MODEL), lambda i, j: (0, 0)),
                  pl.BlockSpec((D_MODEL, tn), lambda i, j: (0, j))],
        out_specs=[pl.BlockSpec((tm, tn), lambda i, j: (i, j)),
                   pl.BlockSpec((D_MODEL, tm), lambda i, j: (0, i))],
        out_shape=[jax.ShapeDtypeStruct((T, PP), BF16), jax.ShapeDtypeStruct((D_MODEL, T), BF16)],
        scratch_shapes=[pltpu.VMEM((tm, D_MODEL), BF16)],
        compiler_params=_cp(),
    )(x, g, wp)


def _matmul_nn(at, b, name):
    K, T = at.shape
    N = b.shape[1]
    tt, tn = _tile(T, 1024), _tile(N, 2816)
    nk = T // tt

    def body(a_ref, b_ref, o_ref, acc_ref):
        k = pl.program_id(1)

        @pl.when(k == 0)
        def _():
            acc_ref[...] = jnp.zeros_like(acc_ref)

        acc_ref[...] += jnp.dot(a_ref[...], b_ref[...], preferred_element_type=F32)

        @pl.when(k == nk - 1)
        def _():
            o_ref[...] = acc_ref[...].astype(BF16)

    return pl.pallas_call(
        body, name=name, grid=(N // tn, nk),
        in_specs=[pl.BlockSpec((K, tt), lambda j, k: (0, k)),
                  pl.BlockSpec((tt, tn), lambda j, k: (k, j))],
        out_specs=pl.BlockSpec((K, tn), lambda j, k: (0, j)),
        out_shape=jax.ShapeDtypeStruct((K, N), BF16),
        scratch_shapes=[pltpu.VMEM((K, tn), F32)],
        compiler_params=_cp(),
    )(at, b)


def _matmul_tn(a, b, name):
    T, K = a.shape
    N = b.shape[1]
    tt, tn = _tile(T, 512), _tile(N, 1024)

    def body(a_ref, b_ref, o_ref):
        @pl.when(pl.program_id(1) == 0)
        def _():
            o_ref[...] = jnp.zeros_like(o_ref)

        o_ref[...] += _mm_tn(a_ref[...], b_ref[...])

    return pl.pallas_call(
        body, name=name, grid=(N // tn, T // tt),
        in_specs=[pl.BlockSpec((tt, K), lambda j, k: (k, 0)),
                  pl.BlockSpec((tt, tn), lambda j, k: (k, j))],
        out_specs=pl.BlockSpec((K, tn), lambda j, k: (0, j)),
        out_shape=jax.ShapeDtypeStruct((K, N), F32),
        compiler_params=_cp(),
    )(a, b)


def _inproj_bwd_x(dproj, wp, x, g, dout):
    T = x.shape[0]
    tm, tk = _tile(T, 1024), 1024
    nk = PP // tk

    def body(dp_ref, w_ref, x_ref, g_ref, do_ref, dx_ref, dg_ref, acc_ref):
        i, k = pl.program_id(0), pl.program_id(1)

        @pl.when(k == 0)
        def _():
            acc_ref[...] = jnp.zeros_like(acc_ref)

        @pl.when((k == 0) & (i == 0))
        def _():
            dg_ref[...] = jnp.zeros_like(dg_ref)

        acc_ref[...] += _mm_nt(dp_ref[...], w_ref[...])

        @pl.when(k == nk - 1)
        def _():
            dh = acc_ref[...]
            xv = x_ref[...]
            r = _rstd(xv, D_MODEL)
            gy = dh * g_ref[...]
            dot = jnp.sum(xv * gy, axis=-1, keepdims=True) * (1.0 / D_MODEL)
            dx_ref[...] = do_ref[...] + r * gy - xv * (r * r * r) * dot
            dg_ref[...] += jnp.sum(dh * xv * r, axis=0, keepdims=True)

    return pl.pallas_call(
        body, name="inproj_bwd_x", grid=(T // tm, nk),
        in_specs=[pl.BlockSpec((tm, tk), lambda i, k: (i, k)),
                  pl.BlockSpec((D_MODEL, tk), lambda i, k: (0, k)),
                  pl.BlockSpec((tm, D_MODEL), lambda i, k: (i, 0)),
                  pl.BlockSpec((1, D_MODEL), lambda i, k: (0, 0)),
                  pl.BlockSpec((tm, D_MODEL), lambda i, k: (i, 0))],
        out_specs=[pl.BlockSpec((tm, D_MODEL), lambda i, k: (i, 0)),
                   pl.BlockSpec((1, D_MODEL), lambda i, k: (0, 0))],
        out_shape=[jax.ShapeDtypeStruct((T, D_MODEL), F32), jax.ShapeDtypeStruct((1, D_MODEL), F32)],
        scratch_shapes=[pltpu.VMEM((tm, D_MODEL), F32)],
        compiler_params=_cp(),
    )(dproj, wp, x, g, dout)


A_SEGS = (CB_AB, CB_AC, CB_AX, CB_AZ)


def _mixa_fwd(proj, cw, cb, B, S):
    nc = CONV_WIDTH // LANE

    def body(ab_ref, ac_ref, ax_ref, az_ref, cw_ref, cb_ref, y_ref):
        ab, ac, ax, az = (r[...].astype(F32) for r in (ab_ref, ac_ref, ax_ref, az_ref))
        u = ac * ax
        conv = cb_ref[...] + cw_ref[0:1, :] * _shift_down(u, 2) + cw_ref[1:2, :] * _shift_down(u, 1) + cw_ref[2:3, :] * u
        y_ref[...] = (ab * conv * _silu(az)).astype(BF16)

    return pl.pallas_call(
        body, name="mixa_fwd", grid=(B, nc),
        in_specs=[pl.BlockSpec((S, LANE), lambda b, j, c0=c0: (b, c0 + j)) for c0 in A_SEGS]
                 + [pl.BlockSpec((CONV_K, LANE), lambda b, j: (0, j)),
                    pl.BlockSpec((1, LANE), lambda b, j: (0, j))],
        out_specs=pl.BlockSpec((S, LANE), lambda b, j: (b, j)),
        out_shape=jax.ShapeDtypeStruct((B * S, CONV_WIDTH), BF16),
        compiler_params=_cp(),
    )(proj, proj, proj, proj, cw, cb)


def _mixa_bwd(dproj, dy, proj, cw, cb, B, S):
    nc = CONV_WIDTH // LANE

    def body(dpin_ref, dy_ref, ab_ref, ac_ref, ax_ref, az_ref, cw_ref, cb_ref, dp_ref, st_ref, stage, sems):
        del dpin_ref
        j, b = pl.program_id(0), pl.program_id(1)
        ab, ac, ax, az = (r[...].astype(F32) for r in (ab_ref, ac_ref, ax_ref, az_ref))
        u = ac * ax
        u1, u2 = _shift_down(u, 1), _shift_down(u, 2)
        w0, w1, w2 = cw_ref[0:1, :], cw_ref[1:2, :], cw_ref[2:3, :]
        conv = cb_ref[...] + w0 * u2 + w1 * u1 + w2 * u
        s, ds_az = _silu_and_grad(az)
        d = dy_ref[...]
        dconv = d * ab * s
        du = w2 * dconv + w1 * _shift_up(dconv, 1) + w0 * _shift_up(dconv, 2)
        grads = (d * conv * s, du * ax, du * ac, d * ab * conv * ds_az)

        def fill(slot):
            for k, v in enumerate(grads):
                stage[slot, k] = v.astype(BF16)

        def copies_of(step):
            sj, sb = step // B, step % B
            return _put_copies([stage.at[:, k] for k in range(4)], dp_ref, sems, step % 2,
                               pl.ds(pl.multiple_of(sb * S, S), S),
                               [pl.multiple_of((c0 + sj) * LANE, LANE) for c0 in A_SEGS])

        _put_pipeline(j * B + b, nc * B, copies_of, fill)
        row = lax.broadcasted_iota(jnp.int32, (8, LANE), 0)
        st = jnp.zeros((8, LANE), F32)
        for r, v in enumerate((dconv * u2, dconv * u1, dconv * u, dconv)):
            st = st + jnp.where(row == r, jnp.sum(v, axis=0, keepdims=True), 0.0)

        @pl.when(pl.program_id(1) == 0)
        def _():
            st_ref[...] = st

        @pl.when(pl.program_id(1) != 0)
        def _():
            st_ref[...] += st

    return pl.pallas_call(
        body, name="mixa_bwd", grid=(nc, B),
        in_specs=[pl.BlockSpec(memory_space=pl.ANY),
                  pl.BlockSpec((S, LANE), lambda j, b: (b, j))]
                 + [pl.BlockSpec((S, LANE), lambda j, b, c0=c0: (b, c0 + j)) for c0 in A_SEGS]
                 + [pl.BlockSpec((CONV_K, LANE), lambda j, b: (0, j)),
                    pl.BlockSpec((1, LANE), lambda j, b: (0, j))],
        out_specs=[pl.BlockSpec(memory_space=pl.ANY),
                   pl.BlockSpec((8, LANE), lambda j, b: (0, j))],
        out_shape=[jax.ShapeDtypeStruct(dproj.shape, BF16), jax.ShapeDtypeStruct((8, CONV_WIDTH), F32)],
        scratch_shapes=[pltpu.VMEM((2, 4, S, LANE), BF16), pltpu.SemaphoreType.DMA((2, 4))],
        input_output_aliases={0: 0},
        compiler_params=_cp(),
    )(dproj, dy, proj, proj, proj, proj, cw, cb)


def _mla_prep_fwd(proj, gq, gkv, wuqp, wkp, wv, gmq, gmk, cos, sa, sb, S):
    T = proj.shape[0]
    ts = _tile(S, 512)
    ns = S // ts
    W = MLA_HEADS * LANE

    def body(p_ref, gq_ref, gkv_ref, wuq_ref, wk_ref, wv_ref, gmq_ref, gmk_ref, cos_ref, sa_ref, sb_ref,
             q_ref, k_ref, v_ref):
        cq = p_ref[:, 0:2 * LANE].astype(F32)
        ckv = p_ref[:, 2 * LANE:3 * LANE].astype(F32)
        kpe = pltpu.roll(p_ref[:, 3 * LANE:4 * LANE].astype(F32), HALF, axis=1)
        cqn = cq * _rstd(cq, MLA_Q_LORA) * gq_ref[...]
        ckn = (ckv * _rstd(ckv, MLA_KV_LORA) * gkv_ref[...]).astype(BF16)
        q0 = _mm(cqn, wuq_ref[...])
        kn = _mm(ckn, wk_ref[...])
        v_ref[...] = _mm(ckn, wv_ref[...]).astype(BF16)
        c, a, b = cos_ref[...], sa_ref[...], sb_ref[...]
        kpe_rot = _rope(kpe * gmk_ref[...], c, a, b)
        for h in range(MLA_HEADS):
            q0h = q0[:, h * LANE:(h + 1) * LANE]
            q_ref[h] = _rope(q0h * _rstd(q0h, MLA_QK) * gmq_ref[...], c, a, b).astype(BF16)
            knh = kn[:, h * LANE:(h + 1) * LANE]
            k_ref[h] = (_rstd(knh + kpe, MLA_QK) * (knh * gmk_ref[...] + kpe_rot)).astype(BF16)

    def whole(r, c):
        return pl.BlockSpec((r, c), lambda i: (0, 0))

    tab = pl.BlockSpec((ts, LANE), lambda i: (i % ns, 0))
    return pl.pallas_call(
        body, name="mla_prep_fwd", grid=(T // ts,),
        in_specs=[pl.BlockSpec((ts, 4 * LANE), lambda i: (i, CB_CQ // 4)),
                  whole(1, MLA_Q_LORA), whole(1, MLA_KV_LORA), whole(MLA_Q_LORA, W), whole(MLA_KV_LORA, W),
                  whole(MLA_KV_LORA, MLA_HEADS * MLA_V), whole(1, LANE), whole(1, LANE), tab, tab, tab],
        out_specs=[pl.BlockSpec((MLA_HEADS, ts, LANE), lambda i: (0, i, 0)),
                   pl.BlockSpec((MLA_HEADS, ts, LANE), lambda i: (0, i, 0)),
                   pl.BlockSpec((ts, MLA_HEADS * MLA_V), lambda i: (i, 0))],
        out_shape=[jax.ShapeDtypeStruct((MLA_HEADS, T, LANE), BF16), jax.ShapeDtypeStruct((MLA_HEADS, T, LANE), BF16),
                   jax.ShapeDtypeStruct((T, MLA_HEADS * MLA_V), BF16)],
        compiler_params=_cp(),
    )(proj, gq, gkv, wuqp, wkp, wv, gmq, gmk, cos, sa, sb)


def _mla_prep_bwd(dproj, dq, dk, dv, proj, gq, gkv, wuqp, wkp, wv, gmq, gmk, cos, sa, sb, S):
    T = proj.shape[0]
    ts = _tile(S, 256)
    ns = S // ts
    W = MLA_HEADS * LANE

    def body(dpin_ref, dq_ref, dk_ref, dv_ref, p_ref, gq_ref, gkv_ref, wuq_ref, wk_ref, wv_ref, gmq_ref, gmk_ref,
             cos_ref, sa_ref, sb_ref,
             dp_ref, dwuq_ref, dwk_ref, dwv_ref, dgq_ref, dgkv_ref, dgmq_ref, dgmk_ref, dq0_ref, dkn_ref):
        del dpin_ref

        @pl.when(pl.program_id(0) == 0)
        def _():
            for r in (dwuq_ref, dwk_ref, dwv_ref, dgq_ref, dgkv_ref, dgmq_ref, dgmk_ref):
                r[...] = jnp.zeros_like(r)

        cq = p_ref[:, 0:2 * LANE].astype(F32)
        ckv = p_ref[:, 2 * LANE:3 * LANE].astype(F32)
        kpe = pltpu.roll(p_ref[:, 3 * LANE:4 * LANE].astype(F32), HALF, axis=1)
        rq = _rstd(cq, MLA_Q_LORA)
        rkv = _rstd(ckv, MLA_KV_LORA)
        gq, gkv, gmq, gmk = gq_ref[...], gkv_ref[...], gmq_ref[...], gmk_ref[...]
        cqn = (cq * rq * gq).astype(BF16)
        ckn = (ckv * rkv * gkv).astype(BF16)
        q0 = _mm(cqn, wuq_ref[...])
        kn = _mm(ckn, wk_ref[...])
        c, a, b = cos_ref[...], sa_ref[...], sb_ref[...]
        lane = lax.broadcasted_iota(jnp.int32, (ts, LANE), 1)
        dgmq = jnp.zeros((1, LANE), F32)
        dgmk = jnp.zeros((1, LANE), F32)
        nope = lane < MLA_NOPE
        kpe_rot = _rope(kpe * gmk, c, a, b)
        dk_sum = jnp.zeros((ts, LANE), F32)
        back = jnp.zeros((ts, 1), F32)
        for h in range(MLA_HEADS):
            q0h = q0[:, h * LANE:(h + 1) * LANE]
            r = _rstd(q0h, MLA_QK)
            d1 = _rope_t(dq_ref[h], c, a, b)
            gy = d1 * gmq
            dq0_ref[:, h * LANE:(h + 1) * LANE] = (
                r * gy - q0h * (r * r * r) * (jnp.sum(q0h * gy, axis=-1, keepdims=True) * (1.0 / MLA_QK))).astype(BF16)
            dgmq = dgmq + jnp.sum(d1 * q0h * r, axis=0, keepdims=True)
            knh = kn[:, h * LANE:(h + 1) * LANE]
            dkh = dk_ref[h]
            r = _rstd(knh + kpe, MLA_QK)
            r3dot = (r * r * r) * (jnp.sum((knh * gmk + kpe_rot) * dkh, axis=-1, keepdims=True) * (1.0 / MLA_QK))
            dkn_ref[:, h * LANE:(h + 1) * LANE] = jnp.where(nope, r * gmk * dkh - knh * r3dot, 0.0).astype(BF16)
            dgmk = dgmk + jnp.sum(jnp.where(nope, dkh * knh * r, 0.0), axis=0, keepdims=True)
            dk_sum = dk_sum + r * dkh
            back = back + r3dot
        rot = jnp.where(nope | (lane >= MLA_QK), 0.0, _rope_t(dk_sum, c, a, b))
        dkpe = gmk * rot - kpe * back
        dgmk = dgmk + jnp.sum(kpe * rot, axis=0, keepdims=True)
        dq0 = dq0_ref[...]
        dkn = dkn_ref[...]
        dvv = dv_ref[...]
        dwuq_ref[...] += _mm_tn(cqn, dq0)
        dwk_ref[...] += _mm_tn(ckn, dkn)
        dwv_ref[...] += _mm_tn(ckn, dvv)
        dgmq_ref[...] += dgmq
        dgmk_ref[...] += dgmk
        dcqn = _mm_nt(dq0, wuq_ref[...])
        gy = dcqn * gq
        dp_ref[:, 0:2 * LANE] = (
            rq * gy - cq * (rq * rq * rq) * (jnp.sum(cq * gy, axis=-1, keepdims=True) * (1.0 / MLA_Q_LORA))).astype(BF16)
        dgq_ref[...] += jnp.sum(dcqn * cq * rq, axis=0, keepdims=True)
        dckn = _mm_nt(dkn, wk_ref[...]) + _mm_nt(dvv, wv_ref[...])
        gy = dckn * gkv
        dp_ref[:, 2 * LANE:3 * LANE] = (
            rkv * gy - ckv * (rkv * rkv * rkv) * (jnp.sum(ckv * gy, axis=-1, keepdims=True) * (1.0 / MLA_KV_LORA))).astype(BF16)
        dgkv_ref[...] += jnp.sum(dckn * ckv * rkv, axis=0, keepdims=True)
        dp_ref[:, 3 * LANE:4 * LANE] = pltpu.roll(dkpe, HALF, axis=1).astype(BF16)

    def whole(r, c):
        return pl.BlockSpec((r, c), lambda i: (0, 0))

    tab = pl.BlockSpec((ts, LANE), lambda i: (i % ns, 0))
    heads = pl.BlockSpec((MLA_HEADS, ts, LANE), lambda i: (0, i, 0))
    return pl.pallas_call(
        body, name="mla_prep_bwd", grid=(T // ts,),
        in_specs=[pl.BlockSpec(memory_space=pl.ANY), heads, heads,
                  pl.BlockSpec((ts, MLA_HEADS * MLA_V), lambda i: (i, 0)),
                  pl.BlockSpec((ts, 4 * LANE), lambda i: (i, CB_CQ // 4)),
                  whole(1, MLA_Q_LORA), whole(1, MLA_KV_LORA), whole(MLA_Q_LORA, W), whole(MLA_KV_LORA, W),
                  whole(MLA_KV_LORA, MLA_HEADS * MLA_V), whole(1, LANE), whole(1, LANE), tab, tab, tab],
        out_specs=[pl.BlockSpec((ts, 4 * LANE), lambda i: (i, CB_CQ // 4)),
                   whole(MLA_Q_LORA, W), whole(MLA_KV_LORA, W), whole(MLA_KV_LORA, MLA_HEADS * MLA_V),
                   whole(1, MLA_Q_LORA), whole(1, MLA_KV_LORA), whole(1, LANE), whole(1, LANE)],
        out_shape=[jax.ShapeDtypeStruct(dproj.shape, BF16),
                   jax.ShapeDtypeStruct((MLA_Q_LORA, W), F32), jax.ShapeDtypeStruct((MLA_KV_LORA, W), F32),
                   jax.ShapeDtypeStruct((MLA_KV_LORA, MLA_HEADS * MLA_V), F32),
                   jax.ShapeDtypeStruct((1, MLA_Q_LORA), F32), jax.ShapeDtypeStruct((1, MLA_KV_LORA), F32),
                   jax.ShapeDtypeStruct((1, LANE), F32), jax.ShapeDtypeStruct((1, LANE), F32)],
        scratch_shapes=[pltpu.VMEM((ts, W), BF16), pltpu.VMEM((ts, W), BF16)],
        input_output_aliases={0: 0},
        compiler_params=_cp(),
    )(dproj, dq, dk, dv, proj, gq, gkv, wuqp, wkp, wv, gmq, gmk, cos, sa, sb)


def _dil_prep_fwd(proj, gq, gk):
    T = proj.shape[0]
    ts = _tile(T, 512)

    def body(pq_ref, pk_ref, gq_ref, gk_ref, q_ref, k_ref):
        for c in range(NPAIR):
            cs = slice(c * LANE, (c + 1) * LANE)
            t = jnp.concatenate([pq_ref[:, cs], pk_ref[:, cs]], axis=1).astype(F32)
            y = t * lax.rsqrt(_head_bcast_sum(t * t, terms=2) * (1.0 / DIL_HEAD_DIM) + EPS)
            q_ref[:, cs] = (y[:, 0:LANE] * gq_ref[:, cs]).astype(BF16)
            k_ref[:, cs] = (y[:, LANE:2 * LANE] * gk_ref[:, cs]).astype(BF16)

    col = pl.BlockSpec((1, DIL_WIDTH), lambda i, g: (0, g))
    out = pl.BlockSpec((ts, DIL_WIDTH), lambda i, g: (i, g))
    seg = lambda c0: pl.BlockSpec((ts, DIL_WIDTH), lambda i, g: (i, c0 // NPAIR + g))
    return pl.pallas_call(
        body, name="dil_prep_fwd", grid=(T // ts, DIL_GROUPS),
        in_specs=[seg(CB_DQ), seg(CB_DK), col, col],
        out_specs=[out, out],
        out_shape=[jax.ShapeDtypeStruct((T, DIL_QK), BF16)] * 2,
        compiler_params=_cp(),
    )(proj, proj, gq, gk)


def _dil_prep_bwd(dproj, ddq, ddk, ddv, proj, gq, gk):
    T = proj.shape[0]
    ts = _tile(T, 512)
    nt = T // ts

    def body(dpin_ref, ddq_ref, ddk_ref, ddv_ref, pq_ref, pk_ref, gq_ref, gk_ref, dp_ref, dgq_ref, dgk_ref,
             stage, sems):
        del dpin_ref
        g, i = pl.program_id(0), pl.program_id(1)

        @pl.when(i == 0)
        def _():
            dgq_ref[...] = jnp.zeros_like(dgq_ref)
            dgk_ref[...] = jnp.zeros_like(dgk_ref)

        def fill(slot):
            stage[slot, 2] = ddv_ref[...].astype(BF16)
            for c in range(NPAIR):
                cs = slice(c * LANE, (c + 1) * LANE)
                t = jnp.concatenate([pq_ref[:, cs], pk_ref[:, cs]], axis=1).astype(F32)
                d = jnp.concatenate([ddq_ref[:, cs], ddk_ref[:, cs]], axis=1)
                gy = d * jnp.concatenate([gq_ref[:, cs], gk_ref[:, cs]], axis=1)
                r = lax.rsqrt(_head_bcast_sum(t * t, terms=2) * (1.0 / DIL_HEAD_DIM) + EPS)
                dot = _head_bcast_sum(t * gy, terms=2) * (1.0 / DIL_HEAD_DIM)
                dx = (r * gy - t * (r * r * r) * dot).astype(BF16)
                stage[slot, 0, :, cs] = dx[:, 0:LANE]
                stage[slot, 1, :, cs] = dx[:, LANE:2 * LANE]
                part = jnp.sum(d * t * r, axis=0, keepdims=True)
                dgq_ref[:, cs] += part[:, 0:LANE]
                dgk_ref[:, cs] += part[:, LANE:2 * LANE]

        def copies_of(step):
            sg, si = step // nt, step % nt
            return _put_copies([stage.at[:, k] for k in range(3)], dp_ref, sems, step % 2,
                               pl.ds(pl.multiple_of(si * ts, ts), ts),
                               [pl.multiple_of((c0 + NPAIR * sg) * LANE, LANE) for c0 in (CB_DQ, CB_DK, CB_DV)])

        _put_pipeline(g * nt + i, DIL_GROUPS * nt, copies_of, fill)

    col = pl.BlockSpec((1, DIL_WIDTH), lambda g, i: (0, g))
    tok = pl.BlockSpec((ts, DIL_WIDTH), lambda g, i: (i, g))
    seg = lambda c0: pl.BlockSpec((ts, DIL_WIDTH), lambda g, i: (i, c0 // NPAIR + g))
    return pl.pallas_call(
        body, name="dil_prep_bwd", grid=(DIL_GROUPS, nt),
        in_specs=[pl.BlockSpec(memory_space=pl.ANY), tok, tok, tok, seg(CB_DQ), seg(CB_DK), col, col],
        out_specs=[pl.BlockSpec(memory_space=pl.ANY), col, col],
        out_shape=[jax.ShapeDtypeStruct(dproj.shape, BF16), jax.ShapeDtypeStruct((1, DIL_QK), F32),
                   jax.ShapeDtypeStruct((1, DIL_QK), F32)],
        scratch_shapes=[pltpu.VMEM((2, 3, ts, DIL_WIDTH), BF16), pltpu.SemaphoreType.DMA((2, 3))],
        input_output_aliases={0: 0},
        compiler_params=_cp(),
    )(dproj, ddq, ddk, ddv, proj, proj, gq, gk)


COPY_ROWS = 256


def _to_classes(src_ref, dst_ref, d, L, scale=None):
    m = min(L, max(8, COPY_ROWS // d))
    for c0 in range(0, L, m):
        x = src_ref[c0 * d:(c0 + m) * d, :].astype(F32)
        if scale is not None:
            x = x * scale
        if d > 1:
            x = jnp.swapaxes(x.reshape(m, d, LANE), 0, 1)
        for r in range(d):
            dst_ref[r * L + c0:r * L + c0 + m, :] = (x[r] if d > 1 else x).astype(dst_ref.dtype)


def _from_classes(src_ref, dst_ref, d, L):
    n = min(L, COPY_ROWS)
    for r in range(d):
        for c0 in range(0, L, n):
            rows = pl.ds(r + c0 * d, n, stride=d) if d > 1 else pl.ds(c0, n)
            dst_ref[rows, :] = src_ref[r * L + c0:r * L + c0 + n, :].astype(dst_ref.dtype)


MLA_TQ, MLA_TK = 512, 512


def _causal_bias(tq, tk, shift):
    row = lax.broadcasted_iota(jnp.int32, (tq, tk), 0)
    col = lax.broadcasted_iota(jnp.int32, (tq, tk), 1)
    return jnp.where(row >= col + shift, 0.0, NEG)


def _mla_specs(S):
    heads = pl.BlockSpec((2, S, LANE), lambda b, j: (j, b, 0))
    pair = pl.BlockSpec((S, LANE), lambda b, j: (b, j))
    return heads, pair


def _mla_attn_fwd(q, k, v, B, S):
    tq = _tile(S, MLA_TQ)
    tk = _tile(tq, MLA_TK)
    nd = tq // tk
    scale = MLA_QK ** -0.5
    heads, pair = _mla_specs(S)

    def body(q_ref, k_ref, v_ref, o_ref, lse_ref):
        lo, lok = _lane_lo((tq, LANE)), _lane_lo((tk, LANE))
        diag = [_causal_bias(tq, tk, i * tk) for i in range(nd)]

        def block(g, _):
            row0 = pl.multiple_of(g * tq, tq)
            rows = pl.ds(row0, tq)
            qs = [q_ref[hh, rows, :] for hh in range(2)]

            one = jnp.ones((), BF16)

            def step(off, carries, bias):
                off = pl.multiple_of(off, tk)
                vt = v_ref[pl.ds(off, tk), :]
                vh = (jnp.where(lok, vt, one), jnp.where(lok, one, vt))
                out = []
                for hh, (m, acc) in enumerate(carries):
                    s = _mm_nt(qs[hh], k_ref[hh, pl.ds(off, tk), :]) * scale
                    if bias is not None:
                        s = s + bias
                    m_new = jnp.maximum(m, jnp.max(s, axis=-1, keepdims=True))
                    p = jnp.exp(s - m_new)
                    out.append((m_new, jnp.exp(m - m_new) * acc + _mm(p, vh[hh])))
                return tuple(out)

            init = (jnp.full((tq, 1), NEG, F32), jnp.zeros((tq, LANE), F32))
            carries = lax.fori_loop(0, g * nd, lambda i, c: step(i * tk, c, None), (init, init))
            for i in range(nd):
                carries = step(row0 + i * tk, carries, diag[i])
            (ma, acca), (mb, accb) = carries
            la, lb = pltpu.roll(acca, HALF, axis=1), pltpu.roll(accb, HALF, axis=1)
            o_ref[rows, :] = jnp.where(lo, acca / la, accb / lb)
            lse_ref[rows, :] = jnp.where(lo, ma + jnp.log(la), mb + jnp.log(lb))
            return 0

        lax.fori_loop(0, S // tq, block, 0)

    return pl.pallas_call(
        body, name="mla_attn_fwd", grid=(B, NPAIR), in_specs=[heads, heads, pair], out_specs=[pair, pair],
        out_shape=[jax.ShapeDtypeStruct((B * S, MLA_HEADS * MLA_V), F32)] * 2,
        compiler_params=_cp(),
    )(q, k, v)


DIL_UNROLL = 16


def _dil_geometry(gi, S):
    span, d = DIL_PATTERNS[gi]
    L = S // d
    t = _tile(L, 128)
    window = span // d
    back = min(-(-window // t) * t, L - t)
    return d, L, t, window, back


def _dil_specs(gi, S):
    qk = pl.BlockSpec((S, LANE), lambda b, j: (b, NPAIR * gi + j))
    v = pl.BlockSpec((S, LANE), lambda b, j: (b, CB_DV + NPAIR * gi + j))
    pair = pl.BlockSpec((S, LANE), lambda b, j: (b, j))
    return qk, v, pair


def _dil_bias(bias_ref, sl_ref, j, t, kw, back, window):
    row = lax.broadcasted_iota(jnp.int32, (2 * t, kw), 0)
    col = lax.broadcasted_iota(jnp.int32, (2 * t, kw), 1)
    second = row >= t
    slope = jnp.where(second, sl_ref[j, 1], sl_ref[j, 0])
    for n in range(bias_ref.shape[0]):
        dist = jnp.where(second, row - t, row) + n * back - col
        bias_ref[n] = jnp.where((dist >= 0) & (dist <= window), -slope * dist.astype(F32), NEG)


def _stack_heads(x, lo):
    zero = jnp.zeros((), x.dtype)
    return jnp.concatenate([jnp.where(lo, x, zero), jnp.where(lo, zero, x)], axis=0)


def _dil_attn_fwd(gi, slopes, qn, kn, proj, B, S):
    d, L, t, window, back = _dil_geometry(gi, S)
    kw, nq = back + t, L // t
    nbias = 2 if back else 1
    qk, vspec, pair = _dil_specs(gi, S)

    def body(sl_ref, q_ref, k_ref, v_ref, o_ref, lse_ref, qs, ks, vs, os_, ls, bias_ref):
        _to_classes(q_ref, qs, d, L, DIL_HEAD_DIM ** -0.5)
        _to_classes(k_ref, ks, d, L)
        _to_classes(v_ref, vs, d, L)
        _dil_bias(bias_ref, sl_ref, pl.program_id(1), t, kw, back, window)
        lo = _lane_lo((t, LANE))

        def block(g, _):
            qb = g % nq if d > 1 else g
            row0 = pl.multiple_of(g * t, t)
            rows = pl.ds(row0, t)
            early = qb * t < back
            keys = pl.ds(pl.multiple_of(jnp.where(early, row0 - qb * t, row0 - back), t), kw)
            s = _mm_nt(_stack_heads(qs[rows, :], lo), ks[keys, :]) + bias_ref[jnp.where(early, 0, nbias - 1)]
            m = jnp.max(s, axis=-1, keepdims=True)
            p = jnp.exp(s - m)
            l = jnp.sum(p, axis=-1, keepdims=True)
            o2 = _mm(p, vs[keys, :]) / l
            lse2 = m + jnp.log(l)
            os_[rows, :] = jnp.where(lo, o2[:t], o2[t:])
            ls[rows, :] = jnp.where(lo, lse2[:t], lse2[t:])
            return 0

        lax.fori_loop(0, d * nq, block, 0, unroll=DIL_UNROLL if d * nq % DIL_UNROLL == 0 else 1)
        _from_classes(os_, o_ref, d, L)
        _from_classes(ls, lse_ref, d, L)

    return pl.pallas_call(
        body, name=f"dil_attn_fwd_{gi}", grid=(B, NPAIR),
        in_specs=[pl.BlockSpec(memory_space=pltpu.SMEM), qk, qk, vspec], out_specs=[pair, pair],
        out_shape=[jax.ShapeDtypeStruct((B * S, DIL_WIDTH), F32)] * 2,
        scratch_shapes=[pltpu.VMEM((S, LANE), BF16)] * 3 + [pltpu.VMEM((S, LANE), F32)] * 2
                       + [pltpu.VMEM((nbias, 2 * t, kw), F32)],
        compiler_params=_cp(),
    )(slopes, qn, kn, proj)


def _mla_attn_bwd(q, k, v, do, lse, delta, B, S):
    T = B * S
    tq = _tile(S, MLA_TQ)
    tk = _tile(tq, MLA_TK)
    nd = tq // tk
    scale = MLA_QK ** -0.5
    heads, pair = _mla_specs(S)

    def body(q_ref, k_ref, v_ref, do_ref, lse_ref, dl_ref, dq_ref, dk_ref, dv_ref):
        dk_ref[...] = jnp.zeros_like(dk_ref)
        dv_ref[...] = jnp.zeros_like(dv_ref)
        lo = _lane_lo((tq, LANE))
        diag = [_causal_bias(tq, tk, i * tk) for i in range(nd)]

        def block(g, _):
            row0 = pl.multiple_of(g * tq, tq)
            rows = pl.ds(row0, tq)
            per_head = []
            for hh in range(2):
                sel = lo if hh == 0 else jnp.logical_not(lo)
                per_head.append((q_ref[hh, rows, :], jnp.where(sel, do_ref[rows, :], jnp.zeros((), BF16)),
                                 jnp.max(jnp.where(sel, lse_ref[rows, :], NEG), axis=-1, keepdims=True),
                                 jnp.max(jnp.where(sel, dl_ref[rows, :], NEG), axis=-1, keepdims=True)))

            def step(off, dq_accs, bias):
                cols = pl.ds(pl.multiple_of(off, tk), tk)
                vt = v_ref[cols, :]
                out, dv = [], None
                for hh, (qh, doh, lse_h, dl_h) in enumerate(per_head):
                    kh = k_ref[hh, cols, :]
                    s = _mm_nt(qh, kh) * scale
                    if bias is not None:
                        s = s + bias
                    p = jnp.exp(s - lse_h)
                    ds = (p * (_mm_nt(doh, vt) - dl_h)).astype(BF16)
                    dk_ref[hh, cols, :] += _mm_tn(ds, qh) * scale
                    part = _mm_tn(p, doh)
                    dv = part if dv is None else dv + part
                    out.append(dq_accs[hh] + _mm(ds, kh))
                dv_ref[cols, :] += dv
                return tuple(out)

            zero = jnp.zeros((tq, LANE), F32)
            dq_accs = lax.fori_loop(0, g * nd, lambda i, a: step(i * tk, a, None), (zero, zero))
            for i in range(nd):
                dq_accs = step(row0 + i * tk, dq_accs, diag[i])
            for hh in range(2):
                dq_ref[hh, rows, :] = dq_accs[hh] * scale
            return 0

        lax.fori_loop(0, S // tq, block, 0)

    return pl.pallas_call(
        body, name="mla_attn_bwd", grid=(B, NPAIR), in_specs=[heads, heads, pair, pair, pair, pair],
        out_specs=[heads, heads, pair],
        out_shape=[jax.ShapeDtypeStruct((MLA_HEADS, T, LANE), F32), jax.ShapeDtypeStruct((MLA_HEADS, T, LANE), F32),
                   jax.ShapeDtypeStruct((T, MLA_HEADS * MLA_V), F32)],
        compiler_params=_cp(),
    )(q, k, v, do, lse, delta)


def _dil_attn_bwd(gi, slopes, qn, kn, proj, do, lse, delta, through, B, S):
    d, L, t, window, back = _dil_geometry(gi, S)
    kw, nq = back + t, L // t
    nbias = 2 if back else 1
    scale = DIL_HEAD_DIM ** -0.5
    qk, vspec, pair = _dil_specs(gi, S)

    def body(*refs):
        refs = list(refs)
        sl_ref, q_ref, k_ref, v_ref, do_ref, lse_ref, dl_ref = refs[:7]
        dq_ref, dk_ref, dv_ref, qs, ks, vs, dos, lss, dls, dqs, dks, dvs, bias_ref = refs[-13:]
        _to_classes(q_ref, qs, d, L, scale)
        for src, dst in ((k_ref, ks), (v_ref, vs), (do_ref, dos), (lse_ref, lss), (dl_ref, dls)):
            _to_classes(src, dst, d, L)
        _dil_bias(bias_ref, sl_ref, pl.program_id(1), t, kw, back, window)
        dks[...] = jnp.zeros_like(dks)
        dvs[...] = jnp.zeros_like(dvs)
        lo = _lane_lo((t, LANE))

        def stats(ref, rows):
            x = ref[rows, :]
            return jnp.concatenate([jnp.max(jnp.where(lo, x, NEG), axis=-1, keepdims=True),
                                    jnp.max(jnp.where(lo, NEG, x), axis=-1, keepdims=True)], axis=0)

        def block(g, _):
            qb = g % nq if d > 1 else g
            row0 = pl.multiple_of(g * t, t)
            rows = pl.ds(row0, t)
            early = qb * t < back
            keys = pl.ds(pl.multiple_of(jnp.where(early, row0 - qb * t, row0 - back), t), kw)
            q2 = _stack_heads(qs[rows, :], lo)
            do2 = _stack_heads(dos[rows, :], lo)
            kt = ks[keys, :]
            s = _mm_nt(q2, kt) + bias_ref[jnp.where(early, 0, nbias - 1)]
            p = jnp.exp(s - stats(lss, rows))
            ds = (p * (_mm_nt(do2, vs[keys, :]) - stats(dls, rows))).astype(BF16)
            dq2 = _mm(ds, kt) * scale
            dqs[rows, :] = jnp.where(lo, dq2[:t], dq2[t:])
            dks[keys, :] += _mm_tn(ds, q2)
            dvs[keys, :] += _mm_tn(p, do2)
            return 0

        lax.fori_loop(0, d * nq, block, 0, unroll=DIL_UNROLL if d * nq % DIL_UNROLL == 0 else 1)
        for src, dst in ((dqs, dq_ref), (dks, dk_ref), (dvs, dv_ref)):
            _from_classes(src, dst, d, L)

    in_specs = [pl.BlockSpec(memory_space=pltpu.SMEM), qk, qk, vspec, pair, pair, pair]
    args = [slopes, qn, kn, proj, do, lse, delta]
    aliases = {}
    if through is not None:
        aliases = {len(args) + i: i for i in range(3)}
        in_specs = in_specs + [pl.BlockSpec(memory_space=pl.ANY)] * 3
        args = args + list(through)
    return pl.pallas_call(
        body, name=f"dil_attn_bwd_{gi}", grid=(B, NPAIR), in_specs=in_specs, out_specs=[qk, qk, qk],
        out_shape=[jax.ShapeDtypeStruct((B * S, DIL_QK), F32)] * 3,
        scratch_shapes=[pltpu.VMEM((S, LANE), BF16)] * 4 + [pltpu.VMEM((S, LANE), F32)] * 5
                       + [pltpu.VMEM((nbias, 2 * t, kw), F32)],
        input_output_aliases=aliases,
        compiler_params=_cp(),
    )(*args)


def _merge_proj_specs(ts):
    wide = lambda c0, w: pl.BlockSpec((ts, w), lambda i: (i, c0 * LANE // w))
    return [wide(CB_BZ, DIL_WIDTH), wide(CB_CZ, DIL_WIDTH)] + [wide(CB_GATE + 8 * i, D_MODEL) for i in range(3)]


def _merge_common(p_refs, bg_ref, ob_ref, og_refs, lse_refs):
    bz = p_refs[0][...].astype(F32)
    cz = p_refs[1][...].astype(F32)
    gates = [_sigmoid(p_refs[2 + i][...].astype(F32) + bg_ref[:, i * D_MODEL:(i + 1) * D_MODEL]) for i in range(3)]
    ob = ob_ref[...]
    lses = [r[...] for r in lse_refs]
    mx = jnp.maximum(jnp.maximum(lses[0], lses[1]), lses[2])
    es = [jnp.exp(v - mx) for v in lses]
    inv = 1.0 / (es[0] + es[1] + es[2])
    alphas = [e * inv for e in es]
    oc = alphas[0] * og_refs[0][...] + alphas[1] * og_refs[1][...] + alphas[2] * og_refs[2][...]
    return bz, cz, gates, ob, alphas, oc


def _merge_fwd(x, proj, b_gate, ya, ob, ogs, lses, woa, wob, woc, wo):
    T = x.shape[0]
    ts = _tile(T, 256)

    def body(x_ref, p0, p1, p2, p3, p4, bg_ref, ya_ref, ob_ref, og0, og1, og2, l0, l1, l2,
             woa_ref, wob_ref, woc_ref, wo_ref, out_ref):
        bz, cz, gates, obv, alphas, oc = _merge_common((p0, p1, p2, p3, p4), bg_ref, ob_ref, (og0, og1, og2),
                                                       (l0, l1, l2))
        yb = obv * _silu(bz)
        yc = oc * _silu(cz)
        merged = (gates[0] * _mm(ya_ref[...], woa_ref[...]) + gates[1] * _mm(yb, wob_ref[...])
                  + gates[2] * _mm(yc, woc_ref[...]))
        out_ref[...] = x_ref[...] + _mm(merged, wo_ref[...])

    def whole(r, c):
        return pl.BlockSpec((r, c), lambda i: (0, 0))

    tok = lambda w: pl.BlockSpec((ts, w), lambda i: (i, 0))
    return pl.pallas_call(
        body, name="merge_fwd", grid=(T // ts,),
        in_specs=[tok(D_MODEL)] + _merge_proj_specs(ts) + [whole(1, 3 * D_MODEL), tok(CONV_WIDTH)]
                 + [tok(DIL_WIDTH)] * 7 + [whole(CONV_WIDTH, D_MODEL)] * 3 + [whole(D_MODEL, D_MODEL)],
        out_specs=tok(D_MODEL),
        out_shape=jax.ShapeDtypeStruct((T, D_MODEL), F32),
        compiler_params=_cp(),
    )(x, *[proj] * 5, b_gate, ya, ob, *ogs, *lses, woa, wob, woc, wo)


def _merge_bwd(dout, proj, b_gate, ya, ob, ogs, lses, woa, wob, woc, wo):
    T = dout.shape[0]
    ts = _tile(T, 256)
    nt = T // ts

    def body(do_ref, p0, p1, p2, p3, p4, bg_ref, ya_ref, ob_ref, og0, og1, og2, l0, l1, l2,
             woa_ref, wob_ref, woc_ref, wo_ref,
             dp_ref, dya_ref, dob_ref, dlb_ref, dg0, dg1, dg2, dl0, dl1, dl2,
             mg_ref, dpa_ref, dpb_ref, dpc_ref, yb_ref, yc_ref, dbg_ref, st_bz, st_cz, st_gate, sems):
        step = pl.program_id(0)
        slot = step % 2

        def copies_of(s):
            return _put_copies([st_bz, st_cz, st_gate], dp_ref, sems, s % 2, pl.ds(pl.multiple_of(s * ts, ts), ts),
                               [CB_BZ * LANE, CB_CZ * LANE, CB_GATE * LANE])

        @pl.when(step >= 2)
        def _():
            for cp in copies_of(step - 2):
                cp.wait()

        bz, cz, gates, obv, alphas, oc = _merge_common((p0, p1, p2, p3, p4), bg_ref, ob_ref, (og0, og1, og2),
                                                       (l0, l1, l2))
        (sb, dsb), (sc, dsc) = _silu_and_grad(bz), _silu_and_grad(cz)
        yb = obv * sb
        yc = oc * sc
        ps = [_mm(ya_ref[...], woa_ref[...]), _mm(yb, wob_ref[...]), _mm(yc, woc_ref[...])]
        mg_ref[...] = (gates[0] * ps[0] + gates[1] * ps[1] + gates[2] * ps[2]).astype(BF16)
        yb_ref[...] = yb.astype(BF16)
        yc_ref[...] = yc.astype(BF16)
        dm = _mm_nt(do_ref[...], wo_ref[...])
        dps = []
        first = pl.program_id(0) == 0
        for i, dref in enumerate((dpa_ref, dpb_ref, dpc_ref)):
            g = gates[i]
            dpi = (dm * g).astype(BF16)
            dref[...] = dpi
            dps.append(dpi)
            dgp = dm * ps[i] * g * (1.0 - g)
            st_gate[slot, :, i * D_MODEL:(i + 1) * D_MODEL] = dgp.astype(BF16)
            part = jnp.sum(dgp, axis=0, keepdims=True)

            @pl.when(first)
            def _():
                dbg_ref[:, i * D_MODEL:(i + 1) * D_MODEL] = part

            @pl.when(jnp.logical_not(first))
            def _():
                dbg_ref[:, i * D_MODEL:(i + 1) * D_MODEL] += part

        dya_ref[...] = _mm_nt(dps[0], woa_ref[...])
        dyb = _mm_nt(dps[1], wob_ref[...])
        dyc = _mm_nt(dps[2], woc_ref[...])
        st_bz[slot] = (dyb * obv * dsb).astype(BF16)
        st_cz[slot] = (dyc * oc * dsc).astype(BF16)
        for cp in copies_of(step):
            cp.start()
        dob = dyb * sb
        doc = dyc * sc
        dob_ref[...] = dob.astype(BF16)
        for c in range(NPAIR):
            cs = slice(c * LANE, (c + 1) * LANE)
            dlb_ref[:, cs] = _head_bcast_sum(dob[:, cs] * obv[:, cs])
            dd = _head_bcast_sum(doc[:, cs] * oc[:, cs])
            for a, dref, lref in zip(alphas, (dg0, dg1, dg2), (dl0, dl1, dl2)):
                dref[:, cs] = (a[:, cs] * doc[:, cs]).astype(BF16)
                lref[:, cs] = a[:, cs] * dd

        @pl.when(step == nt - 1)
        def _():
            if nt >= 2:
                for cp in copies_of(step - 1):
                    cp.wait()
            for cp in copies_of(step):
                cp.wait()

    def whole(r, c):
        return pl.BlockSpec((r, c), lambda i: (0, 0))

    tok = lambda w: pl.BlockSpec((ts, w), lambda i: (i, 0))
    sd = jax.ShapeDtypeStruct
    W = DIL_WIDTH
    return pl.pallas_call(
        body, name="merge_bwd", grid=(nt,),
        in_specs=[tok(D_MODEL)] + _merge_proj_specs(ts) + [whole(1, 3 * D_MODEL), tok(CONV_WIDTH)] + [tok(W)] * 7
                 + [whole(CONV_WIDTH, D_MODEL)] * 3 + [whole(D_MODEL, D_MODEL)],
        out_specs=[pl.BlockSpec(memory_space=pl.ANY), tok(CONV_WIDTH), tok(W), tok(W)] + [tok(W)] * 6
                  + [tok(D_MODEL)] * 4 + [tok(W), tok(W), whole(1, 3 * D_MODEL)],
        out_shape=[sd((T, PP), BF16), sd((T, CONV_WIDTH), F32), sd((T, W), BF16), sd((T, W), F32)]
                  + [sd((T, W), BF16)] * 3 + [sd((T, W), F32)] * 3
                  + [sd((T, D_MODEL), BF16)] * 4 + [sd((T, W), BF16)] * 2 + [sd((1, 3 * D_MODEL), F32)],
        scratch_shapes=[pltpu.VMEM((2, ts, W), BF16), pltpu.VMEM((2, ts, W), BF16),
                        pltpu.VMEM((2, ts, 3 * D_MODEL), BF16), pltpu.SemaphoreType.DMA((2, 3))],
        compiler_params=_cp(),
    )(dout, *[proj] * 5, b_gate, ya, ob, *ogs, *lses, woa, wob, woc, wo)


def _loss_head(y, target):
    T = y.shape[0]
    ts = _tile(T, 512)

    def body(y_ref, t_ref, d_ref, l_ref):
        e = y_ref[...] - t_ref[...]
        d_ref[...] = e * (1.0 / D_MODEL)
        l_ref[...] = jnp.zeros((1, 8, LANE), F32) + jnp.sum(e * e)

    tok = pl.BlockSpec((ts, D_MODEL), lambda i: (i, 0))
    return pl.pallas_call(
        body, name="loss_head", grid=(T // ts,), in_specs=[tok, tok],
        out_specs=[tok, pl.BlockSpec((1, 8, LANE), lambda i: (i, 0, 0))],
        out_shape=[jax.ShapeDtypeStruct((T, D_MODEL), F32), jax.ShapeDtypeStruct((T // ts, 8, LANE), F32)],
        compiler_params=_cp(),
    )(y, target)


def _my_index():
    return 4 * lax.axis_index("x") + 2 * lax.axis_index("y") + lax.axis_index("c")


def _peers():
    x, y, c = (lax.axis_index(a) for a in AXES)
    out = []
    for kk in range(1, N_DEV):
        px = 1 - x if kk & 4 else x
        py = 1 - y if kk & 2 else y
        pc = 1 - c if kk & 1 else c
        out.append(((px, py, pc), 4 * px + 2 * py + pc))
    return out


def _exchange(arrays, name, gather):
    n = len(arrays)

    def body(*refs):
        srcs, outs = refs[:n], refs[n:2 * n]
        send_sems, recv_sems, local_sems = refs[2 * n:]
        me = _my_index()
        peers = _peers()
        started = []
        for a, (src, out) in enumerate(zip(srcs, outs)):
            mine = pltpu.make_async_copy(src if gather else src.at[me], out.at[me], local_sems.at[a])
            mine.start()
            started.append(mine)
        sends = []
        for i, (pos, idx) in enumerate(peers):
            for a, (src, out) in enumerate(zip(srcs, outs)):
                cp = pltpu.make_async_remote_copy(
                    src_ref=src if gather else src.at[idx], dst_ref=out.at[me], send_sem=send_sems.at[a, i],
                    recv_sem=recv_sems.at[a, i], device_id=pos, device_id_type=pl.DeviceIdType.MESH)
                cp.start()
                sends.append(cp)
        for i, (pos, idx) in enumerate(peers):
            for a, (src, out) in enumerate(zip(srcs, outs)):
                pltpu.make_async_remote_copy(
                    src_ref=src if gather else src.at[idx], dst_ref=out.at[idx], send_sem=send_sems.at[a, i],
                    recv_sem=recv_sems.at[a, i], device_id=pos, device_id_type=pl.DeviceIdType.MESH).wait_recv()
        for cp in sends:
            cp.wait_send()
        for mine in started:
            mine.wait()

    any_space = pl.BlockSpec(memory_space=pl.ANY)
    return pl.pallas_call(
        body, name=name, in_specs=[any_space] * n, out_specs=[any_space] * n,
        out_shape=[jax.ShapeDtypeStruct(((N_DEV,) + a.shape) if gather else a.shape, a.dtype) for a in arrays],
        scratch_shapes=[pltpu.SemaphoreType.DMA((n, N_DEV - 1)), pltpu.SemaphoreType.DMA((n, N_DEV - 1)),
                        pltpu.SemaphoreType.DMA((n,))],
    )(*arrays)


N_CHIP = 4


def _chip_places():
    x, y, c = (lax.axis_index(a) for a in AXES)
    return (x, y, c), (x, y, 1 - c), [(1 - x, y, c), (x, 1 - y, c), (1 - x, 1 - y, c)]


def _index_of(pos):
    return 4 * pos[0] + 2 * pos[1] + pos[2]


def _gather_two_level(arrays, name):
    n = len(arrays)

    def body(*refs):
        srcs, outs = refs[:n], refs[n:2 * n]
        send_sems, recv_sems, local_sems = refs[2 * n:]
        me, sibling, others = _chip_places()

        def copy(a, k, block, to, src=None):
            slot = outs[a].at[_index_of(block)]
            return pltpu.make_async_remote_copy(
                src_ref=slot if src is None else src, dst_ref=slot, send_sem=send_sems.at[7 * a + k],
                recv_sem=recv_sems.at[7 * a + k], device_id=to, device_id_type=pl.DeviceIdType.MESH)

        started = []
        for a, src in enumerate(srcs):
            mine = pltpu.make_async_copy(src, outs[a].at[_index_of(me)], local_sems.at[a])
            mine.start()
            started.append(mine)
        sends = []
        for a, src in enumerate(srcs):
            sends.append(copy(a, 0, me, sibling, src))
            sends += [copy(a, 1 + j, me, chip, src) for j, chip in enumerate(others)]
        for cp in sends:
            cp.start()
        for j, chip in enumerate(others):
            for a in range(n):
                copy(a, 1 + j, chip, me).wait_recv()
                fwd = copy(a, 4 + j, chip, sibling)
                fwd.start()
                sends.append(fwd)
        for a in range(n):
            copy(a, 0, sibling, me).wait_recv()
            for j, chip in enumerate(others):
                copy(a, 4 + j, (chip[0], chip[1], sibling[2]), me).wait_recv()
        for cp in sends:
            cp.wait_send()
        for mine in started:
            mine.wait()

    any_space = pl.BlockSpec(memory_space=pl.ANY)
    return pl.pallas_call(
        body, name=name, in_specs=[any_space] * n, out_specs=[any_space] * n,
        out_shape=[jax.ShapeDtypeStruct((N_DEV,) + a.shape, a.dtype) for a in arrays],
        scratch_shapes=[pltpu.SemaphoreType.DMA((7 * n,)), pltpu.SemaphoreType.DMA((7 * n,)),
                        pltpu.SemaphoreType.DMA((n,))],
    )(*arrays)


def _sibling_swap(arrays, name):
    n = len(arrays)

    def body(*refs):
        srcs, outs = refs[:n], refs[n:2 * n]
        send_sems, recv_sems = refs[2 * n:]
        (x, y, c), sibling, _ = _chip_places()
        sends = []
        for a, (src, out) in enumerate(zip(srcs, outs)):
            for q in range(N_CHIP):
                def copy(core, a=a, q=q, src=src, out=out):
                    return pltpu.make_async_remote_copy(
                        src_ref=src.at[2 * q + core], dst_ref=out.at[q], send_sem=send_sems.at[N_CHIP * a + q],
                        recv_sem=recv_sems.at[N_CHIP * a + q], device_id=sibling, device_id_type=pl.DeviceIdType.MESH)
                mine = copy(1 - c)
                mine.start()
                sends.append((mine, copy(c)))
        for mine, arrival in sends:
            arrival.wait_recv()
            mine.wait_send()

    any_space = pl.BlockSpec(memory_space=pl.ANY)
    return pl.pallas_call(
        body, name=name, in_specs=[any_space] * n, out_specs=[any_space] * n,
        out_shape=[jax.ShapeDtypeStruct((N_CHIP,) + a.shape[1:], a.dtype) for a in arrays],
        scratch_shapes=[pltpu.SemaphoreType.DMA((N_CHIP * n,)), pltpu.SemaphoreType.DMA((N_CHIP * n,))],
    )(*arrays)


def _chip_pair_sum(part, got, name):
    R, C = part.shape[1:]
    tr = R
    while tr * C * part.dtype.itemsize > REDUCE_BLOCK_BYTES // 4 and tr % 32 == 0:
        tr //= 2
    c = lax.axis_index("c")

    def body(c_ref, p_ref, g_ref, o_ref):
        del c_ref
        o_ref[...] = (p_ref[...].astype(F32) + g_ref[...].astype(F32)).astype(o_ref.dtype)

    return pl.pallas_call(
        body, name=name, grid_spec=pltpu.PrefetchScalarGridSpec(
            num_scalar_prefetch=1, grid=(N_CHIP, R // tr),
            in_specs=[pl.BlockSpec((None, tr, C), lambda q, i, cr: (2 * q + cr[0], i, 0)),
                      pl.BlockSpec((None, tr, C), lambda q, i, cr: (q, i, 0))],
            out_specs=pl.BlockSpec((None, tr, C), lambda q, i, cr: (q, i, 0))),
        out_shape=jax.ShapeDtypeStruct((N_CHIP, R, C), part.dtype),
        compiler_params=_cp(),
    )(jnp.reshape(c, (1,)).astype(jnp.int32), part, got)


def _peer_count(mode):
    return {"chips": N_CHIP - 1, "near": N_CHIP}.get(mode, N_DEV - 1)


def _remote_copies(srcs, lands, send_sems, recv_sems, mode):
    if mode == "chips":
        (x, y, _), _, others = _chip_places()
        my_slot, peers = 2 * x + y, [(chip, 2 * chip[0] + chip[1]) for chip in others]
    elif mode == "near":
        me, sibling, others = _chip_places()
        my_slot, peers = _index_of(me), [(pos, _index_of(pos)) for pos in [sibling] + others]
    else:
        my_slot, peers = _my_index(), _peers()
    whole = mode in ("gather", "near")
    out = []
    for i, (pos, idx) in enumerate(peers):
        for a, (src, land) in enumerate(zip(srcs, lands)):
            def copy(slot, a=a, src=src, land=land, i=i, pos=pos, idx=idx):
                return pltpu.make_async_remote_copy(
                    src_ref=src if whole else src.at[idx], dst_ref=land.at[slot],
                    send_sem=send_sems.at[a * len(peers) + i], recv_sem=recv_sems.at[a * len(peers) + i],
                    device_id=pos, device_id_type=pl.DeviceIdType.MESH)
            out.append((copy(my_slot), copy(idx)))
    return out


def _exchange_start(arrays, name, mode):
    n = len(arrays)
    hbm = pl.BlockSpec(memory_space=pltpu.HBM)
    sem = pl.BlockSpec(memory_space=pltpu.SEMAPHORE)
    lands = [lax.empty(((N_DEV,) + a.shape) if mode in ("gather", "near") else a.shape, a.dtype) for a in arrays]

    def body(*refs):
        srcs, lands_ = refs[:n], refs[n:2 * n]
        send_sems, recv_sems = refs[2 * n:2 * n + 2]
        for mine, _ in _remote_copies(srcs, lands_, send_sems, recv_sems, mode):
            mine.start()
        refs[-1][...] = jnp.zeros_like(refs[-1])

    sems = pltpu.SemaphoreType.DMA((n * _peer_count(mode),))
    buffers = [pltpu.HBM(a.shape, a.dtype) for a in list(arrays) + lands]
    res = pl.pallas_call(
        body, name=name, in_specs=[hbm] * (2 * n), out_specs=[sem, sem] + [hbm] * (2 * n) + [pl.BlockSpec(memory_space=pltpu.VMEM)],
        out_shape=[sems, sems] + buffers + [jax.ShapeDtypeStruct((8, LANE), F32)],
        input_output_aliases={i: 2 + i for i in range(2 * n)},
        compiler_params=pltpu.CompilerParams(has_side_effects=pltpu.SideEffectType.DATAFLOW_SIDE_EFFECTING),
    )(*[pltpu.with_memory_space_constraint(a, pltpu.HBM) for a in list(arrays) + lands])
    return (res[0], res[1], res[2:2 + n], res[2 + n:2 + 2 * n]), res[-1]


def _exchange_wait(handle, after, name, mode):
    send_sems, recv_sems, srcs, lands = handle
    n = len(srcs)
    after = list(after) if isinstance(after, (list, tuple)) else [after]
    hbm = pl.BlockSpec(memory_space=pltpu.HBM)
    sem = pl.BlockSpec(memory_space=pltpu.SEMAPHORE)

    def body(*refs):
        for mine, arrival in _remote_copies(refs[:n], refs[n:2 * n], refs[2 * n], refs[2 * n + 1], mode):
            mine.wait_send()
            arrival.wait_recv()

    res = pl.pallas_call(
        body, name=name, in_specs=[hbm] * (2 * n) + [sem, sem] + [pl.BlockSpec(memory_space=pl.ANY)] * len(after),
        out_specs=[hbm] * (2 * n), out_shape=[pltpu.HBM(a.shape, a.dtype) for a in list(srcs) + list(lands)],
        input_output_aliases={i: i for i in range(2 * n)},
        compiler_params=pltpu.CompilerParams(has_side_effects=pltpu.SideEffectType.DATAFLOW_SIDE_EFFECTING),
    )(*srcs, *lands, send_sems, recv_sems, *after)
    return res[n:]


def _sibling_forward(lands, name):
    n = len(lands)

    def body(*refs):
        ins, outs, send_sems, recv_sems = refs[:n], refs[n:2 * n], refs[2 * n], refs[2 * n + 1]
        (x, y, c), sibling, others = _chip_places()
        copies = []
        for a, (src, out) in enumerate(zip(ins, outs)):
            for j, chip in enumerate(others):
                def copy(core, a=a, j=j, chip=chip, src=src, out=out):
                    slot = _index_of((chip[0], chip[1], core))
                    return pltpu.make_async_remote_copy(
                        src_ref=src.at[slot], dst_ref=out.at[slot], send_sem=send_sems.at[3 * a + j],
                        recv_sem=recv_sems.at[3 * a + j], device_id=sibling, device_id_type=pl.DeviceIdType.MESH)
                mine = copy(c)
                mine.start()
                copies.append((mine, copy(1 - c)))
        for mine, arrival in copies:
            arrival.wait_recv()
        for mine, arrival in copies:
            mine.wait_send()

    any_space = pl.BlockSpec(memory_space=pl.ANY)
    return pl.pallas_call(
        body, name=name, in_specs=[any_space] * n, out_specs=[any_space] * n,
        out_shape=[jax.ShapeDtypeStruct(a.shape, a.dtype) for a in lands],
        scratch_shapes=[pltpu.SemaphoreType.DMA((3 * n,)), pltpu.SemaphoreType.DMA((3 * n,))],
        input_output_aliases={i: i for i in range(n)},
    )(*lands)


def _own_slot(land, mine, slot=None):
    slot = _my_index() if slot is None else slot
    return lax.dynamic_update_slice(land, mine, (slot,) + (0,) * (land.ndim - 1))


def _adamw(w, g, m, v):
    m = ADAM_B1 * m + (1.0 - ADAM_B1) * g
    v = ADAM_B2 * v + (1.0 - ADAM_B2) * (g * g)
    m_hat = m / (1.0 - ADAM_B1 ** ADAM_STEP)
    v_hat = v / (1.0 - ADAM_B2 ** ADAM_STEP)
    delta = -ADAM_LR * (m_hat / (jnp.sqrt(v_hat) + ADAM_EPS) + ADAM_WD * w)
    return delta, m, v


def _reduce_adamw(parts, w, m, v, name):
    nparts = len(parts)
    R, C = parts[0].shape[1:]
    tr = R
    while N_DEV * tr * C * parts[0].dtype.itemsize > REDUCE_BLOCK_BYTES and tr % 32 == 0:
        tr //= 2
    steps = R // tr

    def body(*refs):
        w_ref, m_ref, v_ref, g_ref, d_ref, nm_ref, nv_ref = refs[nparts:]
        for k, p_ref in enumerate(refs[:nparts]):
            @pl.when(pl.program_id(0) // steps == k)
            def _():
                g = p_ref[0].astype(F32)
                for s in range(1, p_ref.shape[0]):
                    g = g + p_ref[s].astype(F32)
                g_ref[...] = g
                d_ref[...], nm_ref[...], nv_ref[...] = _adamw(w_ref[...], g, m_ref[...], v_ref[...])

    def part_spec(k):
        return pl.BlockSpec((parts[k].shape[0], tr, C), lambda i: (0, jnp.clip(i - k * steps, 0, steps - 1), 0))

    row = pl.BlockSpec((tr, C), lambda i: (i, 0))
    return pl.pallas_call(
        body, name=name, grid=(nparts * steps,),
        in_specs=[part_spec(k) for k in range(nparts)] + [row, row, row],
        out_specs=[row] * 4, out_shape=[jax.ShapeDtypeStruct((nparts * R, C), F32)] * 4,
        compiler_params=_cp(),
    )(*parts, w, m, v)


BIG = ("w_in", "w_uq", "w_ukv", "w_out_a", "w_out_b", "w_out_c", "w_o")
SMALL = ("norm_g", "b_gate", "conv_w", "conv_b", "q_a_norm_g", "kv_a_norm_g", "mla_q_norm_g", "mla_k_norm_g",
         "dil_q_norm_g", "dil_k_norm_g")
PACK_ROWS = 128
REDUCE_BLOCK_BYTES = 6 * 1024 * 1024


def _pack_local(tensors):
    flat = jnp.concatenate([t.reshape(-1) for t in tensors])
    pad = (-flat.shape[0]) % (PACK_ROWS * LANE)
    return jnp.concatenate([flat, jnp.zeros((pad,), flat.dtype)]).reshape(-1, LANE)


def _unpack_local(rows, like):
    flat = rows.reshape(-1)
    out, off = [], 0
    for t in like:
        out.append(flat[off:off + t.size].reshape(t.shape))
        off += t.size
    return out


def _cols_to_slots(a):
    k = a.shape[0]
    return a.reshape(k, N_DEV, -1).transpose(1, 0, 2)


def _slots_to_cols(s):
    return s.transpose(1, 0, 2).reshape(s.shape[1], -1)


def _rope_tables(S):
    inv = ROPE_THETA ** (-jnp.arange(0, MLA_ROPE, 2, dtype=F32) / MLA_ROPE)
    ang = jnp.arange(S, dtype=F32)[:, None] * inv[None, :]
    cos, sin = jnp.cos(ang), jnp.sin(ang)
    one = jnp.ones((S, MLA_NOPE), F32)
    z16, z32, z64 = (jnp.zeros((S, n), F32) for n in (16, 32, 64))
    cosp = jnp.concatenate([one, cos, cos, jnp.ones((S, 32), F32)], axis=1)
    sa = jnp.concatenate([z64, -sin, z16, z32], axis=1)
    sb = jnp.concatenate([z64, z16, sin, z32], axis=1)
    return cosp, sa, sb


def _alibi_slopes():
    n = DIL_GROUPS * DIL_HEADS
    m = 2.0 ** (-8.0 * jnp.arange(1, n + 1, dtype=F32) / n)
    return m.reshape(DIL_GROUPS, NPAIR, 2)


def _pad_slots(s):
    n, k, c = s.shape
    return _slots_to_cols(jnp.concatenate([s, jnp.zeros((n, k, LANE - c), s.dtype)], axis=2))


def _layer_params(gw, small, l):
    p = {}
    p["wp"] = _pad_columns(gw["w_in"])
    p["norm_g"] = small["norm_g"][l][None]
    p["b_gate"] = small["b_gate"][l][None]
    p["conv_w"] = gw["conv_w"].transpose(1, 0, 2).reshape(CONV_K, CONV_WIDTH)
    p["conv_b"] = small["conv_b"][l][None]
    p["gq"] = small["q_a_norm_g"][l][None]
    p["gkv"] = small["kv_a_norm_g"][l][None]
    p["wuqp"] = _pad_slots(gw["w_uq"])
    kv = gw["w_ukv"]
    p["wkp"] = _pad_slots(kv[:, :, :MLA_NOPE])
    p["wv"] = kv[:, :, MLA_NOPE:].transpose(1, 0, 2).reshape(MLA_KV_LORA, MLA_HEADS * MLA_V)
    zpad = jnp.zeros((1, LANE - MLA_QK), F32)
    p["gmq"] = jnp.concatenate([small["mla_q_norm_g"][l][None], zpad], axis=1)
    p["gmk"] = jnp.concatenate([small["mla_k_norm_g"][l][None], zpad], axis=1)
    tile = lambda g: jnp.broadcast_to(g[:, None, :], (DIL_GROUPS, DIL_HEADS, DIL_HEAD_DIM)).reshape(1, DIL_QK)
    p["gdq"] = tile(small["dil_q_norm_g"][l])
    p["gdk"] = tile(small["dil_k_norm_g"][l])
    p["woa"], p["wob"], p["woc"] = (_slots_to_cols(gw[n]) for n in ("w_out_a", "w_out_b", "w_out_c"))
    p["wo"] = gw["w_o"].reshape(D_MODEL, D_MODEL)
    return p


def _layer_fwd(x, p, tabs, slopes, B, S):
    proj, ht = _inproj_fwd(x, p["norm_g"], p["wp"])
    ya = _mixa_fwd(proj, p["conv_w"], p["conv_b"], B, S)
    q, k, v = _mla_prep_fwd(proj, p["gq"], p["gkv"], p["wuqp"], p["wkp"], p["wv"], p["gmq"], p["gmk"], *tabs, S)
    ob, lse_b = _mla_attn_fwd(q, k, v, B, S)
    qn, kn = _dil_prep_fwd(proj, p["gdq"], p["gdk"])
    ogs, lses = [], []
    for gi in range(DIL_GROUPS):
        o, lse = _dil_attn_fwd(gi, slopes[gi], qn, kn, proj, B, S)
        ogs.append(o)
        lses.append(lse)
    out = _merge_fwd(x, proj, p["b_gate"], ya, ob, ogs, lses, p["woa"], p["wob"], p["woc"], p["wo"])
    saved = dict(x=x, proj=proj, ht=ht, ya=ya, q=q, k=k, v=v, ob=ob, lse_b=lse_b, qn=qn, kn=kn, ogs=ogs, lses=lses)
    return out, saved


def _layer_bwd(dout, sv, p, tabs, slopes, B, S, big_ready=None):
    proj = sv["proj"]
    (dproj, dya, dob, dlb, dg0, dg1, dg2, dl0, dl1, dl2, merged, dpa, dpb, dpc, yb, yc, dbg) = _merge_bwd(
        dout, proj, p["b_gate"], sv["ya"], sv["ob"], sv["ogs"], sv["lses"], p["woa"], p["wob"], p["woc"], p["wo"])
    g = {}
    g["w_o"] = _matmul_tn(merged, dout, "dw_o").reshape(N_DEV, D_MODEL // N_DEV, D_MODEL)
    g["w_out_a"] = _cols_to_slots(_matmul_tn(sv["ya"], dpa, "dw_out_a"))
    g["w_out_b"] = _cols_to_slots(_matmul_tn(yb, dpb, "dw_out_b"))
    g["w_out_c"] = _cols_to_slots(_matmul_tn(yc, dpc, "dw_out_c"))
    g["b_gate"] = dbg[0]
    dproj, st = _mixa_bwd(dproj, dya, proj, p["conv_w"], p["conv_b"], B, S)
    g["conv_w"] = st[0:CONV_K]
    g["conv_b"] = st[CONV_K]
    dq, dk, dv = _mla_attn_bwd(sv["q"], sv["k"], sv["v"], dob, sv["lse_b"], dlb, B, S)
    dproj, dwuqp, dwkp, dwv, dgq, dgkv, dgmq, dgmk = _mla_prep_bwd(
        dproj, dq, dk, dv, proj, p["gq"], p["gkv"], p["wuqp"], p["wkp"], p["wv"], p["gmq"], p["gmk"], *tabs, S)
    g["w_uq"] = _cols_to_slots(dwuqp)[:, :, :MLA_QK]
    g["w_ukv"] = jnp.concatenate([_cols_to_slots(dwkp)[:, :, :MLA_NOPE], _cols_to_slots(dwv)], axis=2)
    g["q_a_norm_g"], g["kv_a_norm_g"] = dgq[0], dgkv[0]
    g["mla_q_norm_g"], g["mla_k_norm_g"] = dgmq[0, :MLA_QK], dgmk[0, :MLA_QK]
    dqkv = None
    for gi, (dog, dlg) in enumerate(((dg0, dl0), (dg1, dl1), (dg2, dl2))):
        dqkv = _dil_attn_bwd(gi, slopes[gi], sv["qn"], sv["kn"], proj, dog, sv["lses"][gi], dlg, dqkv, B, S)
    dproj, dgdq, dgdk = _dil_prep_bwd(dproj, *dqkv, proj, p["gdq"], p["gdk"])
    g["dil_q_norm_g"] = dgdq.reshape(DIL_GROUPS, DIL_HEADS, DIL_HEAD_DIM).sum(axis=1)
    g["dil_k_norm_g"] = dgdk.reshape(DIL_GROUPS, DIL_HEADS, DIL_HEAD_DIM).sum(axis=1)
    g["w_in"] = _unpad_columns(_matmul_nn(sv["ht"], dproj, "dw_in"))
    token = None if big_ready is None else big_ready(g)
    dx, dng = _inproj_bwd_x(dproj, p["wp"], sv["x"], _after(token, p["norm_g"]), dout)
    g["norm_g"] = dng[0]
    return dx, g


def _after(token, a):
    return a if token is None else a + token[0:1, 0:1]


def _local_step(x, target, small, B, S, weights_of, grads_out, big_ready=None):
    tabs = _rope_tables(S)
    sl = _alibi_slopes()
    slopes = [sl[gi] * float(DIL_PATTERNS[gi][1]) for gi in range(DIL_GROUPS)]
    params, saved = [], []
    for l in range(DEPTH):
        gw, token = weights_of(l, x)
        p = _layer_params(gw, small, l)
        p["norm_g"] = _after(token, p["norm_g"])
        x, sv = _layer_fwd(x, p, tabs, slopes, B, S)
        params.append(p)
        saved.append(sv)
    dout, lparts = _loss_head(x, target)
    sq = jnp.sum(lparts[:, 0, 0])
    token = None
    for l in reversed(range(DEPTH)):
        p = dict(params[l], b_gate=_after(token, params[l]["b_gate"]))
        ready = None if big_ready is None else (lambda g, l=l: big_ready(l, g))
        dout, g = _layer_bwd(dout, saved[l], p, tabs, slopes, B, S, ready)
        token = grads_out(l, g, dout)
    return sq, dout


def kernel(x, norm_g, w_in, b_gate, conv_w, conv_b, q_a_norm_g, w_uq, kv_a_norm_g, w_ukv, mla_q_norm_g, mla_k_norm_g, dil_q_norm_g, dil_k_norm_g, w_out_a, w_out_b, w_out_c, w_o, loss_target, m_norm_g, m_w_in, m_b_gate, m_conv_w, m_conv_b, m_q_a_norm_g, m_w_uq, m_kv_a_norm_g, m_w_ukv, m_mla_q_norm_g, m_mla_k_norm_g, m_dil_q_norm_g, m_dil_k_norm_g, m_w_out_a, m_w_out_b, m_w_out_c, m_w_o, v_norm_g, v_w_in, v_b_gate, v_conv_w, v_conv_b, v_q_a_norm_g, v_w_uq, v_kv_a_norm_g, v_w_ukv, v_mla_q_norm_g, v_mla_k_norm_g, v_dil_q_norm_g, v_dil_k_norm_g, v_w_out_a, v_w_out_b, v_w_out_c, v_w_o):
    names = ("norm_g", "w_in", "b_gate", "conv_w", "conv_b", "q_a_norm_g", "w_uq", "kv_a_norm_g", "w_ukv",
             "mla_q_norm_g", "mla_k_norm_g", "dil_q_norm_g", "dil_k_norm_g", "w_out_a", "w_out_b", "w_out_c", "w_o")
    w = dict(zip(names, (norm_g, w_in, b_gate, conv_w, conv_b, q_a_norm_g, w_uq, kv_a_norm_g, w_ukv, mla_q_norm_g,
                         mla_k_norm_g, dil_q_norm_g, dil_k_norm_g, w_out_a, w_out_b, w_out_c, w_o)))
    m = dict(zip(names, (m_norm_g, m_w_in, m_b_gate, m_conv_w, m_conv_b, m_q_a_norm_g, m_w_uq, m_kv_a_norm_g, m_w_ukv,
                         m_mla_q_norm_g, m_mla_k_norm_g, m_dil_q_norm_g, m_dil_k_norm_g, m_w_out_a, m_w_out_b,
                         m_w_out_c, m_w_o)))
    v = dict(zip(names, (v_norm_g, v_w_in, v_b_gate, v_conv_w, v_conv_b, v_q_a_norm_g, v_w_uq, v_kv_a_norm_g, v_w_ukv,
                         v_mla_q_norm_g, v_mla_k_norm_g, v_dil_q_norm_g, v_dil_k_norm_g, v_w_out_a, v_w_out_b,
                         v_w_out_c, v_w_o)))
    B, S, _ = x.shape
    me = _my_index()
    cshard = CONV_WIDTH // N_DEV

    shards = [[w[n][0].astype(BF16) for n in BIG]]
    state = {}

    def weights_of(l, after):
        if l == 0:
            first = shards[0] + [conv_w]
            handle, token = _exchange_start(first, "all_gather_weights_0_start", "near")
            zero = token[0:1, 0:1]
            state["shards1"] = [(w[n][1] + zero).astype(BF16) for n in BIG]
            state["rows_w_in"] = [a.reshape(-1, a.shape[-1]) + zero for a in (w["w_in"], m["w_in"], v["w_in"])]
            landed = _exchange_wait(handle, state["shards1"] + state["rows_w_in"], "all_gather_weights_0_wait", "near")
            landed = _sibling_forward(landed, "all_gather_weights_0_forward")
            got = [_own_slot(a, s[None]) for a, s in zip(landed, first)]
            state["gather"], token = _exchange_start(state["shards1"], "all_gather_weights_1_start", "gather")
            state["conv_w"] = got[-1]
        else:
            landed = _exchange_wait(state["gather"], after, "all_gather_weights_1_wait", "gather")
            got, token = [_own_slot(a, s[None]) for a, s in zip(landed, state["shards1"])], None
        gw = dict(zip(BIG, got))
        gw["conv_w"] = state["conv_w"][:, l]
        return gw, token

    recv, small_parts = {}, {}
    my_chip = 2 * lax.axis_index("x") + lax.axis_index("y")

    def big_ready(l, g):
        send = [g[n].astype(BF16) for n in BIG]
        if l == DEPTH - 1:
            state["scatter"], token = _exchange_start(send, "exchange_weight_grads_1_start", "scatter")
        else:
            swapped = _sibling_swap(send, "exchange_weight_grads_0_sibling")
            send = [_chip_pair_sum(s, t, "chip_pair_sum_" + n) for n, s, t in zip(BIG, send, swapped)]
            state["chips"], token = _exchange_start(send, "exchange_weight_grads_0_start", "chips")
        state["sent", l] = send
        return token

    def grads_out(l, g, after):
        small_parts[l] = [g[n] for n in SMALL]
        if l == DEPTH - 1:
            return None
        for k, key, mode, slot in ((DEPTH - 1, "scatter", "scatter", me), (0, "chips", "chips", my_chip)):
            landed = _exchange_wait(state[key], after, f"exchange_weight_grads_{k}_wait", mode)
            mine = [lax.dynamic_slice_in_dim(s, slot, 1, axis=0) for s in state["sent", k]]
            recv[k] = [_own_slot(a, s, slot) for a, s in zip(landed, mine)]
        return None

    sq, grad_x = _local_step(x.reshape(B * S, D_MODEL), loss_target.reshape(B * S, D_MODEL), w, B, S,
                             weights_of, grads_out, big_ready)
    loss = lax.psum(sq * (0.5 / D_MODEL), AXES)

    res = {}
    for i, n in enumerate(BIG):
        rows = lambda a: a.reshape(-1, a.shape[-1])
        wmv = state["rows_w_in"] if n == "w_in" else [rows(w[n]), rows(m[n]), rows(v[n])]
        outs = _reduce_adamw([recv[l][i] for l in range(DEPTH)], *wmv, "reduce_adamw_" + n)
        res[n] = tuple(a.reshape(w[n].shape) for a in outs)
    part = {n: jnp.stack([small_parts[l][i] for l in range(DEPTH)]) for i, n in enumerate(SMALL)}

    def widen(t):
        return lax.dynamic_update_slice(jnp.zeros((DEPTH, CONV_K, CONV_WIDTH), F32), t, (0, 0, me * cshard))

    small_like = [part[n] for n in SMALL]
    pick = lambda d: [widen(d[n]) if n == "conv_w" else d[n] for n in SMALL]
    parts, = _exchange([_pack_local(small_like)], "all_gather_small_grads", gather=True)
    gs, ds, ms, vs = _reduce_adamw([parts], _pack_local(pick(w)), _pack_local(pick(m)), _pack_local(pick(v)),
                                   "reduce_adamw_small")
    for n, t in zip(SMALL, zip(*(_unpack_local(a, small_like) for a in (gs, ds, ms, vs)))):
        if n == "conv_w":
            t = tuple(lax.dynamic_slice(a, (0, 0, me * cshard), (DEPTH, CONV_K, cshard)) for a in t)
        res[n] = t

    out = [loss, grad_x.reshape(B, S, D_MODEL)]
    for i in range(4):
        out += [res[n][i] for n in names]
    return tuple(out)
```

```python
import jax
import jax.numpy as jnp
from jax import lax
from jax.experimental import pallas as pl
from jax.experimental.pallas import tpu as pltpu

F32 = jnp.float32
BF16 = jnp.bfloat16

D_MODEL = 1024
DEPTH = 2
CONV_WIDTH = 512
CONV_K = 3
MLA_HEADS = 8
MLA_Q_LORA = 256
MLA_KV_LORA = 128
MLA_NOPE = 64
MLA_ROPE = 32
MLA_V = 64
MLA_QK = MLA_NOPE + MLA_ROPE
ROPE_THETA = 10000.0
DIL_PATTERNS = ((128, 1), (512, 4), (2048, 16))
DIL_GROUPS = 3
DIL_HEADS = 8
DIL_HEAD_DIM = 64
DIL_WIDTH = DIL_HEADS * DIL_HEAD_DIM
DIL_QK = DIL_GROUPS * DIL_WIDTH
EPS = 1e-6
N_IN = 11168

ADAM_LR = 0.001
ADAM_B1 = 0.9
ADAM_B2 = 0.999
ADAM_EPS = 1e-08
ADAM_WD = 0.01
ADAM_STEP = 10

N_DEV = 8
AXES = ("x", "y", "c")
LANE = 128
HALF = 64
NPAIR = 4

CB_AB, CB_AC, CB_AX, CB_AZ = 0, 4, 8, 12
CB_CQ, CB_CKV, CB_KPE = 16, 18, 19
CB_BZ = 20
CB_DQ, CB_DK, CB_DV = 24, 36, 48
CB_CZ, CB_GATE = 60, 64
NCB = 88
PP = NCB * LANE
KPE_END = CB_KPE * LANE + MLA_ROPE
SHARD_COLS = N_IN // N_DEV
NEG = -1e30
VMEM_LIMIT = 56 * 1024 * 1024


def _pad_columns(shards):
    parts = []
    for p in range(N_DEV):
        cut = min(max(KPE_END - p * SHARD_COLS, 0), SHARD_COLS)
        if 0 < cut < SHARD_COLS:
            parts += [shards[p, :, :cut], jnp.zeros((shards.shape[1], LANE - MLA_ROPE), shards.dtype), shards[p, :, cut:]]
        else:
            parts.append(shards[p])
    return jnp.concatenate(parts, axis=1)


def _unpad_columns(wp):
    def columns(a, b):
        gap = LANE - MLA_ROPE
        if b <= KPE_END:
            return wp[:, a:b]
        if a >= KPE_END:
            return wp[:, a + gap:b + gap]
        return jnp.concatenate([wp[:, a:KPE_END], wp[:, KPE_END + gap:b + gap]], axis=1)

    return jnp.stack([columns(p * SHARD_COLS, (p + 1) * SHARD_COLS) for p in range(N_DEV)])


def _put_copies(stages, dst_ref, sems, slot, rows, cols):
    return [pltpu.make_async_copy(st.at[slot], dst_ref.at[rows, pl.ds(c0, st.shape[-1])], sems.at[slot, k])
            for k, (st, c0) in enumerate(zip(stages, cols))]


def _put_pipeline(step, nsteps, copies_of, fill):
    @pl.when(step >= 2)
    def _():
        for cp in copies_of(step - 2):
            cp.wait()

    fill(step % 2)
    for cp in copies_of(step):
        cp.start()

    @pl.when(step == nsteps - 1)
    def _():
        if nsteps >= 2:
            for cp in copies_of(step - 1):
                cp.wait()
        for cp in copies_of(step):
            cp.wait()


def _cp():
    return pltpu.CompilerParams(vmem_limit_bytes=VMEM_LIMIT)


def _rstd(x, n):
    return lax.rsqrt(jnp.sum(x * x, axis=-1, keepdims=True) * (1.0 / n) + EPS)


def _sigmoid(z):
    return 1.0 / (1.0 + jnp.exp(-z))


def _silu(z):
    return z * _sigmoid(z)


def _silu_and_grad(z):
    s = _sigmoid(z)
    return z * s, s * (1.0 + z * (1.0 - s))


def _mm(a, b):
    return jnp.dot(a.astype(BF16), b.astype(BF16), preferred_element_type=F32)


def _mm_nt(a, b):
    return lax.dot_general(a.astype(BF16), b.astype(BF16), (((1,), (1,)), ((), ())), preferred_element_type=F32)


def _mm_tn(a, b):
    return lax.dot_general(a.astype(BF16), b.astype(BF16), (((0,), (0,)), ((), ())), preferred_element_type=F32)


def _lane_lo(shape):
    return lax.broadcasted_iota(jnp.int32, shape, len(shape) - 1) < HALF


def _head_bcast_sum(x, terms=3):
    w = x.shape[-1]
    same = (lax.broadcasted_iota(jnp.int32, (w, w), 0) // HALF) == (lax.broadcasted_iota(jnp.int32, (w, w), 1) // HALF)
    ones = jnp.where(same, 1.0, 0.0).astype(jnp.bfloat16)
    total = None
    for _ in range(terms):
        term = x.astype(jnp.bfloat16)
        x = x - term.astype(F32)
        part = jnp.dot(term, ones, preferred_element_type=F32)
        total = part if total is None else total + part
    return total


def _rope(t, cos, sa, sb):
    return t * cos + pltpu.roll(t, LANE - 16, axis=1) * sa + pltpu.roll(t, 16, axis=1) * sb


def _rope_t(d, cos, sa, sb):
    return d * cos + pltpu.roll(d * sa, 16, axis=1) + pltpu.roll(d * sb, LANE - 16, axis=1)


def _shift_down(u, k):
    rows = lax.broadcasted_iota(jnp.int32, u.shape, 0)
    return jnp.where(rows >= k, pltpu.roll(u, k, axis=0), 0.0)


def _shift_up(u, k):
    n = u.shape[0]
    rows = lax.broadcasted_iota(jnp.int32, u.shape, 0)
    return jnp.where(rows < n - k, pltpu.roll(u, n - k, axis=0), 0.0)


def _tile(n, want):
    t = min(n, want)
    assert n % t == 0, (n, want)
    return t


def _inproj_fwd(x, g, wp):
    T = x.shape[0]
    tm, tn = _tile(T, 2048), 512

    def body(x_ref, g_ref, w_ref, proj_ref, ht_ref, h_ref):
        @pl.when(pl.program_id(1) == 0)
        def _():
            n = min(tm, 512)
            for r0 in range(0, tm, n):
                xv = x_ref[r0:r0 + n, :]
                h = xv * _rstd(xv, D_MODEL) * g_ref[...]
                h_ref[r0:r0 + n, :] = h.astype(BF16)
                ht_ref[:, r0:r0 + n] = h.T.astype(BF16)

        proj_ref[...] = jnp.dot(h_ref[...], w_ref[...], preferred_element_type=F32).astype(BF16)

    return pl.pallas_call(
        body, name="inproj_fwd", grid=(T // tm, PP // tn),
        in_specs=[pl.BlockSpec((tm, D_MODEL), lambda i, j: (i, 0)),
                  pl.BlockSpec((1, D_MODEL), lambda i, j: (0, 0)),
                  pl.BlockSpec((D_MODEL, tn), lambda i, j: (0, j))],
        out_specs=[pl.BlockSpec((tm, tn), lambda i, j: (i, j)),
                   pl.BlockSpec((D_MODEL, tm), lambda i, j: (0, i))],
        out_shape=[jax.ShapeDtypeStruct((T, PP), BF16), jax.ShapeDtypeStruct((D_MODEL, T), BF16)],
        scratch_shapes=[pltpu.VMEM((tm, D_MODEL), BF16)],
        compiler_params=_cp(),
    )(x, g, wp)


def _matmul_nn(at, b, name):
    K, T = at.shape
    N = b.shape[1]
    tt, tn = _tile(T, 1024), _tile(N, 2816)
    nk = T // tt

    def body(a_ref, b_ref, o_ref, acc_ref):
        k = pl.program_id(1)

        @pl.when(k == 0)
        def _():
            acc_ref[...] = jnp.zeros_like(acc_ref)

        acc_ref[...] += jnp.dot(a_ref[...], b_ref[...], preferred_element_type=F32)

        @pl.when(k == nk - 1)
        def _():
            o_ref[...] = acc_ref[...].astype(BF16)

    return pl.pallas_call(
        body, name=name, grid=(N // tn, nk),
        in_specs=[pl.BlockSpec((K, tt), lambda j, k: (0, k)),
                  pl.BlockSpec((tt, tn), lambda j, k: (k, j))],
        out_specs=pl.BlockSpec((K, tn), lambda j, k: (0, j)),
        out_shape=jax.ShapeDtypeStruct((K, N), BF16),
        scratch_shapes=[pltpu.VMEM((K, tn), F32)],
        compiler_params=_cp(),
    )(at, b)


def _matmul_tn(a, b, name):
    T, K = a.shape
    N = b.shape[1]
    tt, tn = _tile(T, 512), _tile(N, 1024)

    def body(a_ref, b_ref, o_ref):
        @pl.when(pl.program_id(1) == 0)
        def _():
            o_ref[...] = jnp.zeros_like(o_ref)

        o_ref[...] += _mm_tn(a_ref[...], b_ref[...])

    return pl.pallas_call(
        body, name=name, grid=(N // tn, T // tt),
        in_specs=[pl.BlockSpec((tt, K), lambda j, k: (k, 0)),
                  pl.BlockSpec((tt, tn), lambda j, k: (k, j))],
        out_specs=pl.BlockSpec((K, tn), lambda j, k: (0, j)),
        out_shape=jax.ShapeDtypeStruct((K, N), F32),
        compiler_params=_cp(),
    )(a, b)


def _inproj_bwd_x(dproj, wp, x, g, dout):
    T = x.shape[0]
    tm, tk = _tile(T, 1024), 1024
    nk = PP // tk

    def body(dp_ref, w_ref, x_ref, g_ref, do_ref, dx_ref, dg_ref, acc_ref):
        i, k = pl.program_id(0), pl.program_id(1)

        @pl.when(k == 0)
        def _():
            acc_ref[...] = jnp.zeros_like(acc_ref)

        @pl.when((k == 0) & (i == 0))
        def _():
            dg_ref[...] = jnp.zeros_like(dg_ref)

        acc_ref[...] += _mm_nt(dp_ref[...], w_ref[...])

        @pl.when(k == nk - 1)
        def _():
            dh = acc_ref[...]
            xv = x_ref[...]
            r = _rstd(xv, D_MODEL)
            gy = dh * g_ref[...]
            dot = jnp.sum(xv * gy, axis=-1, keepdims=True) * (1.0 / D_MODEL)
            dx_ref[...] = do_ref[...] + r * gy - xv * (r * r * r) * dot
            dg_ref[...] += jnp.sum(dh * xv * r, axis=0, keepdims=True)

    return pl.pallas_call(
        body, name="inproj_bwd_x", grid=(T // tm, nk),
        in_specs=[pl.BlockSpec((tm, tk), lambda i, k: (i, k)),
                  pl.BlockSpec((D_MODEL, tk), lambda i, k: (0, k)),
                  pl.BlockSpec((tm, D_MODEL), lambda i, k: (i, 0)),
                  pl.BlockSpec((1, D_MODEL), lambda i, k: (0, 0)),
                  pl.BlockSpec((tm, D_MODEL), lambda i, k: (i, 0))],
        out_specs=[pl.BlockSpec((tm, D_MODEL), lambda i, k: (i, 0)),
                   pl.BlockSpec((1, D_MODEL), lambda i, k: (0, 0))],
        out_shape=[jax.ShapeDtypeStruct((T, D_MODEL), F32), jax.ShapeDtypeStruct((1, D_MODEL), F32)],
        scratch_shapes=[pltpu.VMEM((tm, D_MODEL), F32)],
        compiler_params=_cp(),
    )(dproj, wp, x, g, dout)


A_SEGS = (CB_AB, CB_AC, CB_AX, CB_AZ)


def _mixa_fwd(proj, cw, cb, B, S):
    nc = CONV_WIDTH // LANE

    def body(ab_ref, ac_ref, ax_ref, az_ref, cw_ref, cb_ref, y_ref):
        ab, ac, ax, az = (r[...].astype(F32) for r in (ab_ref, ac_ref, ax_ref, az_ref))
        u = ac * ax
        conv = cb_ref[...] + cw_ref[0:1, :] * _shift_down(u, 2) + cw_ref[1:2, :] * _shift_down(u, 1) + cw_ref[2:3, :] * u
        y_ref[...] = (ab * conv * _silu(az)).astype(BF16)

    return pl.pallas_call(
        body, name="mixa_fwd", grid=(B, nc),
        in_specs=[pl.BlockSpec((S, LANE), lambda b, j, c0=c0: (b, c0 + j)) for c0 in A_SEGS]
                 + [pl.BlockSpec((CONV_K, LANE), lambda b, j: (0, j)),
                    pl.BlockSpec((1, LANE), lambda b, j: (0, j))],
        out_specs=pl.BlockSpec((S, LANE), lambda b, j: (b, j)),
        out_shape=jax.ShapeDtypeStruct((B * S, CONV_WIDTH), BF16),
        compiler_params=_cp(),
    )(proj, proj, proj, proj, cw, cb)


def _mixa_bwd(dproj, dy, proj, cw, cb, B, S):
    nc = CONV_WIDTH // LANE

    def body(dpin_ref, dy_ref, ab_ref, ac_ref, ax_ref, az_ref, cw_ref, cb_ref, dp_ref, st_ref, stage, sems):
        del dpin_ref
        j, b = pl.program_id(0), pl.program_id(1)
        ab, ac, ax, az = (r[...].astype(F32) for r in (ab_ref, ac_ref, ax_ref, az_ref))
        u = ac * ax
        u1, u2 = _shift_down(u, 1), _shift_down(u, 2)
        w0, w1, w2 = cw_ref[0:1, :], cw_ref[1:2, :], cw_ref[2:3, :]
        conv = cb_ref[...] + w0 * u2 + w1 * u1 + w2 * u
        s, ds_az = _silu_and_grad(az)
        d = dy_ref[...]
        dconv = d * ab * s
        du = w2 * dconv + w1 * _shift_up(dconv, 1) + w0 * _shift_up(dconv, 2)
        grads = (d * conv * s, du * ax, du * ac, d * ab * conv * ds_az)

        def fill(slot):
            for k, v in enumerate(grads):
                stage[slot, k] = v.astype(BF16)

        def copies_of(step):
            sj, sb = step // B, step % B
            return _put_copies([stage.at[:, k] for k in range(4)], dp_ref, sems, step % 2,
                               pl.ds(pl.multiple_of(sb * S, S), S),
                               [pl.multiple_of((c0 + sj) * LANE, LANE) for c0 in A_SEGS])

        _put_pipeline(j * B + b, nc * B, copies_of, fill)
        row = lax.broadcasted_iota(jnp.int32, (8, LANE), 0)
        st = jnp.zeros((8, LANE), F32)
        for r, v in enumerate((dconv * u2, dconv * u1, dconv * u, dconv)):
            st = st + jnp.where(row == r, jnp.sum(v, axis=0, keepdims=True), 0.0)

        @pl.when(pl.program_id(1) == 0)
        def _():
            st_ref[...] = st

        @pl.when(pl.program_id(1) != 0)
        def _():
            st_ref[...] += st

    return pl.pallas_call(
        body, name="mixa_bwd", grid=(nc, B),
        in_specs=[pl.BlockSpec(memory_space=pl.ANY),
                  pl.BlockSpec((S, LANE), lambda j, b: (b, j))]
                 + [pl.BlockSpec((S, LANE), lambda j, b, c0=c0: (b, c0 + j)) for c0 in A_SEGS]
                 + [pl.BlockSpec((CONV_K, LANE), lambda j, b: (0, j)),
                    pl.BlockSpec((1, LANE), lambda j, b: (0, j))],
        out_specs=[pl.BlockSpec(memory_space=pl.ANY),
                   pl.BlockSpec((8, LANE), lambda j, b: (0, j))],
        out_shape=[jax.ShapeDtypeStruct(dproj.shape, BF16), jax.ShapeDtypeStruct((8, CONV_WIDTH), F32)],
        scratch_shapes=[pltpu.VMEM((2, 4, S, LANE), BF16), pltpu.SemaphoreType.DMA((2, 4))],
        input_output_aliases={0: 0},
        compiler_params=_cp(),
    )(dproj, dy, proj, proj, proj, proj, cw, cb)


def _mla_prep_fwd(proj, gq, gkv, wuqp, wkp, wv, gmq, gmk, cos, sa, sb, S):
    T = proj.shape[0]
    ts = _tile(S, 512)
    ns = S // ts
    W = MLA_HEADS * LANE

    def body(p_ref, gq_ref, gkv_ref, wuq_ref, wk_ref, wv_ref, gmq_ref, gmk_ref, cos_ref, sa_ref, sb_ref,
             q_ref, k_ref, v_ref):
        cq = p_ref[:, 0:2 * LANE].astype(F32)
        ckv = p_ref[:, 2 * LANE:3 * LANE].astype(F32)
        kpe = pltpu.roll(p_ref[:, 3 * LANE:4 * LANE].astype(F32), HALF, axis=1)
        cqn = cq * _rstd(cq, MLA_Q_LORA) * gq_ref[...]
        ckn = (ckv * _rstd(ckv, MLA_KV_LORA) * gkv_ref[...]).astype(BF16)
        q0 = _mm(cqn, wuq_ref[...])
        kn = _mm(ckn, wk_ref[...])
        v_ref[...] = _mm(ckn, wv_ref[...]).astype(BF16)
        c, a, b = cos_ref[...], sa_ref[...], sb_ref[...]
        kpe_rot = _rope(kpe * gmk_ref[...], c, a, b)
        for h in range(MLA_HEADS):
            q0h = q0[:, h * LANE:(h + 1) * LANE]
            q_ref[h] = _rope(q0h * _rstd(q0h, MLA_QK) * gmq_ref[...], c, a, b).astype(BF16)
            knh = kn[:, h * LANE:(h + 1) * LANE]
            k_ref[h] = (_rstd(knh + kpe, MLA_QK) * (knh * gmk_ref[...] + kpe_rot)).astype(BF16)

    def whole(r, c):
        return pl.BlockSpec((r, c), lambda i: (0, 0))

    tab = pl.BlockSpec((ts, LANE), lambda i: (i % ns, 0))
    return pl.pallas_call(
        body, name="mla_prep_fwd", grid=(T // ts,),
        in_specs=[pl.BlockSpec((ts, 4 * LANE), lambda i: (i, CB_CQ // 4)),
                  whole(1, MLA_Q_LORA), whole(1, MLA_KV_LORA), whole(MLA_Q_LORA, W), whole(MLA_KV_LORA, W),
                  whole(MLA_KV_LORA, MLA_HEADS * MLA_V), whole(1, LANE), whole(1, LANE), tab, tab, tab],
        out_specs=[pl.BlockSpec((MLA_HEADS, ts, LANE), lambda i: (0, i, 0)),
                   pl.BlockSpec((MLA_HEADS, ts, LANE), lambda i: (0, i, 0)),
                   pl.BlockSpec((ts, MLA_HEADS * MLA_V), lambda i: (i, 0))],
        out_shape=[jax.ShapeDtypeStruct((MLA_HEADS, T, LANE), BF16), jax.ShapeDtypeStruct((MLA_HEADS, T, LANE), BF16),
                   jax.ShapeDtypeStruct((T, MLA_HEADS * MLA_V), BF16)],
        compiler_params=_cp(),
    )(proj, gq, gkv, wuqp, wkp, wv, gmq, gmk, cos, sa, sb)


def _mla_prep_bwd(dproj, dq, dk, dv, proj, gq, gkv, wuqp, wkp, wv, gmq, gmk, cos, sa, sb, S):
    T = proj.shape[0]
    ts = _tile(S, 256)
    ns = S // ts
    W = MLA_HEADS * LANE

    def body(dpin_ref, dq_ref, dk_ref, dv_ref, p_ref, gq_ref, gkv_ref, wuq_ref, wk_ref, wv_ref, gmq_ref, gmk_ref,
             cos_ref, sa_ref, sb_ref,
             dp_ref, dwuq_ref, dwk_ref, dwv_ref, dgq_ref, dgkv_ref, dgmq_ref, dgmk_ref, dq0_ref, dkn_ref):
        del dpin_ref

        @pl.when(pl.program_id(0) == 0)
        def _():
            for r in (dwuq_ref, dwk_ref, dwv_ref, dgq_ref, dgkv_ref, dgmq_ref, dgmk_ref):
                r[...] = jnp.zeros_like(r)

        cq = p_ref[:, 0:2 * LANE].astype(F32)
        ckv = p_ref[:, 2 * LANE:3 * LANE].astype(F32)
        kpe = pltpu.roll(p_ref[:, 3 * LANE:4 * LANE].astype(F32), HALF, axis=1)
        rq = _rstd(cq, MLA_Q_LORA)
        rkv = _rstd(ckv, MLA_KV_LORA)
        gq, gkv, gmq, gmk = gq_ref[...], gkv_ref[...], gmq_ref[...], gmk_ref[...]
        cqn = (cq * rq * gq).astype(BF16)
        ckn = (ckv * rkv * gkv).astype(BF16)
        q0 = _mm(cqn, wuq_ref[...])
        kn = _mm(ckn, wk_ref[...])
        c, a, b = cos_ref[...], sa_ref[...], sb_ref[...]
        lane = lax.broadcasted_iota(jnp.int32, (ts, LANE), 1)
        dgmq = jnp.zeros((1, LANE), F32)
        dgmk = jnp.zeros((1, LANE), F32)
        nope = lane < MLA_NOPE
        kpe_rot = _rope(kpe * gmk, c, a, b)
        dk_sum = jnp.zeros((ts, LANE), F32)
        back = jnp.zeros((ts, 1), F32)
        for h in range(MLA_HEADS):
            q0h = q0[:, h * LANE:(h + 1) * LANE]
            r = _rstd(q0h, MLA_QK)
            d1 = _rope_t(dq_ref[h], c, a, b)
            gy = d1 * gmq
            dq0_ref[:, h * LANE:(h + 1) * LANE] = (
                r * gy - q0h * (r * r * r) * (jnp.sum(q0h * gy, axis=-1, keepdims=True) * (1.0 / MLA_QK))).astype(BF16)
            dgmq = dgmq + jnp.sum(d1 * q0h * r, axis=0, keepdims=True)
            knh = kn[:, h * LANE:(h + 1) * LANE]
            dkh = dk_ref[h]
            r = _rstd(knh + kpe, MLA_QK)
            r3dot = (r * r * r) * (jnp.sum((knh * gmk + kpe_rot) * dkh, axis=-1, keepdims=True) * (1.0 / MLA_QK))
            dkn_ref[:, h * LANE:(h + 1) * LANE] = jnp.where(nope, r * gmk * dkh - knh * r3dot, 0.0).astype(BF16)
            dgmk = dgmk + jnp.sum(jnp.where(nope, dkh * knh * r, 0.0), axis=0, keepdims=True)
            dk_sum = dk_sum + r * dkh
            back = back + r3dot
        rot = jnp.where(nope | (lane >= MLA_QK), 0.0, _rope_t(dk_sum, c, a, b))
        dkpe = gmk * rot - kpe * back
        dgmk = dgmk + jnp.sum(kpe * rot, axis=0, keepdims=True)
        dq0 = dq0_ref[...]
        dkn = dkn_ref[...]
        dvv = dv_ref[...]
        dwuq_ref[...] += _mm_tn(cqn, dq0)
        dwk_ref[...] += _mm_tn(ckn, dkn)
        dwv_ref[...] += _mm_tn(ckn, dvv)
        dgmq_ref[...] += dgmq
        dgmk_ref[...] += dgmk
        dcqn = _mm_nt(dq0, wuq_ref[...])
        gy = dcqn * gq
        dp_ref[:, 0:2 * LANE] = (
            rq * gy - cq * (rq * rq * rq) * (jnp.sum(cq * gy, axis=-1, keepdims=True) * (1.0 / MLA_Q_LORA))).astype(BF16)
        dgq_ref[...] += jnp.sum(dcqn * cq * rq, axis=0, keepdims=True)
        dckn = _mm_nt(dkn, wk_ref[...]) + _mm_nt(dvv, wv_ref[...])
        gy = dckn * gkv
        dp_ref[:, 2 * LANE:3 * LANE] = (
            rkv * gy - ckv * (rkv * rkv * rkv) * (jnp.sum(ckv * gy, axis=-1, keepdims=True) * (1.0 / MLA_KV_LORA))).astype(BF16)
        dgkv_ref[...] += jnp.sum(dckn * ckv * rkv, axis=0, keepdims=True)
        dp_ref[:, 3 * LANE:4 * LANE] = pltpu.roll(dkpe, HALF, axis=1).astype(BF16)

    def whole(r, c):
        return pl.BlockSpec((r, c), lambda i: (0, 0))

    tab = pl.BlockSpec((ts, LANE), lambda i: (i % ns, 0))
    heads = pl.BlockSpec((MLA_HEADS, ts, LANE), lambda i: (0, i, 0))
    return pl.pallas_call(
        body, name="mla_prep_bwd", grid=(T // ts,),
        in_specs=[pl.BlockSpec(memory_space=pl.ANY), heads, heads,
                  pl.BlockSpec((ts, MLA_HEADS * MLA_V), lambda i: (i, 0)),
                  pl.BlockSpec((ts, 4 * LANE), lambda i: (i, CB_CQ // 4)),
                  whole(1, MLA_Q_LORA), whole(1, MLA_KV_LORA), whole(MLA_Q_LORA, W), whole(MLA_KV_LORA, W),
                  whole(MLA_KV_LORA, MLA_HEADS * MLA_V), whole(1, LANE), whole(1, LANE), tab, tab, tab],
        out_specs=[pl.BlockSpec((ts, 4 * LANE), lambda i: (i, CB_CQ // 4)),
                   whole(MLA_Q_LORA, W), whole(MLA_KV_LORA, W), whole(MLA_KV_LORA, MLA_HEADS * MLA_V),
                   whole(1, MLA_Q_LORA), whole(1, MLA_KV_LORA), whole(1, LANE), whole(1, LANE)],
        out_shape=[jax.ShapeDtypeStruct(dproj.shape, BF16),
                   jax.ShapeDtypeStruct((MLA_Q_LORA, W), F32), jax.ShapeDtypeStruct((MLA_KV_LORA, W), F32),
                   jax.ShapeDtypeStruct((MLA_KV_LORA, MLA_HEADS * MLA_V), F32),
                   jax.ShapeDtypeStruct((1, MLA_Q_LORA), F32), jax.ShapeDtypeStruct((1, MLA_KV_LORA), F32),
                   jax.ShapeDtypeStruct((1, LANE), F32), jax.ShapeDtypeStruct((1, LANE), F32)],
        scratch_shapes=[pltpu.VMEM((ts, W), BF16), pltpu.VMEM((ts, W), BF16)],
        input_output_aliases={0: 0},
        compiler_params=_cp(),
    )(dproj, dq, dk, dv, proj, gq, gkv, wuqp, wkp, wv, gmq, gmk, cos, sa, sb)


def _dil_prep_fwd(proj, gq, gk):
    T = proj.shape[0]
    ts = _tile(T, 512)

    def body(pq_ref, pk_ref, gq_ref, gk_ref, q_ref, k_ref):
        for c in range(NPAIR):
            cs = slice(c * LANE, (c + 1) * LANE)
            t = jnp.concatenate([pq_ref[:, cs], pk_ref[:, cs]], axis=1).astype(F32)
            y = t * lax.rsqrt(_head_bcast_sum(t * t, terms=2) * (1.0 / DIL_HEAD_DIM) + EPS)
            q_ref[:, cs] = (y[:, 0:LANE] * gq_ref[:, cs]).astype(BF16)
            k_ref[:, cs] = (y[:, LANE:2 * LANE] * gk_ref[:, cs]).astype(BF16)

    col = pl.BlockSpec((1, DIL_WIDTH), lambda i, g: (0, g))
    out = pl.BlockSpec((ts, DIL_WIDTH), lambda i, g: (i, g))
    seg = lambda c0: pl.BlockSpec((ts, DIL_WIDTH), lambda i, g: (i, c0 // NPAIR + g))
    return pl.pallas_call(
        body, name="dil_prep_fwd", grid=(T // ts, DIL_GROUPS),
        in_specs=[seg(CB_DQ), seg(CB_DK), col, col],
        out_specs=[out, out],
        out_shape=[jax.ShapeDtypeStruct((T, DIL_QK), BF16)] * 2,
        compiler_params=_cp(),
    )(proj, proj, gq, gk)


def _dil_prep_bwd(dproj, ddq, ddk, ddv, proj, gq, gk):
    T = proj.shape[0]
    ts = _tile(T, 512)
    nt = T // ts

    def body(dpin_ref, ddq_ref, ddk_ref, ddv_ref, pq_ref, pk_ref, gq_ref, gk_ref, dp_ref, dgq_ref, dgk_ref,
             stage, sems):
        del dpin_ref
        g, i = pl.program_id(0), pl.program_id(1)

        @pl.when(i == 0)
        def _():
            dgq_ref[...] = jnp.zeros_like(dgq_ref)
            dgk_ref[...] = jnp.zeros_like(dgk_ref)

        def fill(slot):
            stage[slot, 2] = ddv_ref[...].astype(BF16)
            for c in range(NPAIR):
                cs = slice(c * LANE, (c + 1) * LANE)
                t = jnp.concatenate([pq_ref[:, cs], pk_ref[:, cs]], axis=1).astype(F32)
                d = jnp.concatenate([ddq_ref[:, cs], ddk_ref[:, cs]], axis=1)
                gy = d * jnp.concatenate([gq_ref[:, cs], gk_ref[:, cs]], axis=1)
                r = lax.rsqrt(_head_bcast_sum(t * t, terms=2) * (1.0 / DIL_HEAD_DIM) + EPS)
                dot = _head_bcast_sum(t * gy, terms=2) * (1.0 / DIL_HEAD_DIM)
                dx = (r * gy - t * (r * r * r) * dot).astype(BF16)
                stage[slot, 0, :, cs] = dx[:, 0:LANE]
                stage[slot, 1, :, cs] = dx[:, LANE:2 * LANE]
                part = jnp.sum(d * t * r, axis=0, keepdims=True)
                dgq_ref[:, cs] += part[:, 0:LANE]
                dgk_ref[:, cs] += part[:, LANE:2 * LANE]

        def copies_of(step):
            sg, si = step // nt, step % nt
            return _put_copies([stage.at[:, k] for k in range(3)], dp_ref, sems, step % 2,
                               pl.ds(pl.multiple_of(si * ts, ts), ts),
                               [pl.multiple_of((c0 + NPAIR * sg) * LANE, LANE) for c0 in (CB_DQ, CB_DK, CB_DV)])

        _put_pipeline(g * nt + i, DIL_GROUPS * nt, copies_of, fill)

    col = pl.BlockSpec((1, DIL_WIDTH), lambda g, i: (0, g))
    tok = pl.BlockSpec((ts, DIL_WIDTH), lambda g, i: (i, g))
    seg = lambda c0: pl.BlockSpec((ts, DIL_WIDTH), lambda g, i: (i, c0 // NPAIR + g))
    return pl.pallas_call(
        body, name="dil_prep_bwd", grid=(DIL_GROUPS, nt),
        in_specs=[pl.BlockSpec(memory_space=pl.ANY), tok, tok, tok, seg(CB_DQ), seg(CB_DK), col, col],
        out_specs=[pl.BlockSpec(memory_space=pl.ANY), col, col],
        out_shape=[jax.ShapeDtypeStruct(dproj.shape, BF16), jax.ShapeDtypeStruct((1, DIL_QK), F32),
                   jax.ShapeDtypeStruct((1, DIL_QK), F32)],
        scratch_shapes=[pltpu.VMEM((2, 3, ts, DIL_WIDTH), BF16), pltpu.SemaphoreType.DMA((2, 3))],
        input_output_aliases={0: 0},
        compiler_params=_cp(),
    )(dproj, ddq, ddk, ddv, proj, proj, gq, gk)


COPY_ROWS = 256


def _to_classes(src_ref, dst_ref, d, L, scale=None):
    m = min(L, max(8, COPY_ROWS // d))
    for c0 in range(0, L, m):
        x = src_ref[c0 * d:(c0 + m) * d, :].astype(F32)
        if scale is not None:
            x = x * scale
        if d > 1:
            x = jnp.swapaxes(x.reshape(m, d, LANE), 0, 1)
        for r in range(d):
            dst_ref[r * L + c0:r * L + c0 + m, :] = (x[r] if d > 1 else x).astype(dst_ref.dtype)


def _from_classes(src_ref, dst_ref, d, L):
    n = min(L, COPY_ROWS)
    for r in range(d):
        for c0 in range(0, L, n):
            rows = pl.ds(r + c0 * d, n, stride=d) if d > 1 else pl.ds(c0, n)
            dst_ref[rows, :] = src_ref[r * L + c0:r * L + c0 + n, :].astype(dst_ref.dtype)


MLA_TQ, MLA_TK = 512, 512


def _causal_bias(tq, tk, shift):
    row = lax.broadcasted_iota(jnp.int32, (tq, tk), 0)
    col = lax.broadcasted_iota(jnp.int32, (tq, tk), 1)
    return jnp.where(row >= col + shift, 0.0, NEG)


def _mla_specs(S):
    heads = pl.BlockSpec((2, S, LANE), lambda b, j: (j, b, 0))
    pair = pl.BlockSpec((S, LANE), lambda b, j: (b, j))
    return heads, pair


def _mla_attn_fwd(q, k, v, B, S):
    tq = _tile(S, MLA_TQ)
    tk = _tile(tq, MLA_TK)
    nd = tq // tk
    scale = MLA_QK ** -0.5
    heads, pair = _mla_specs(S)

    def body(q_ref, k_ref, v_ref, o_ref, lse_ref):
        lo, lok = _lane_lo((tq, LANE)), _lane_lo((tk, LANE))
        diag = [_causal_bias(tq, tk, i * tk) for i in range(nd)]

        def block(g, _):
            row0 = pl.multiple_of(g * tq, tq)
            rows = pl.ds(row0, tq)
            qs = [q_ref[hh, rows, :] for hh in range(2)]

            one = jnp.ones((), BF16)

            def step(off, carries, bias):
                off = pl.multiple_of(off, tk)
                vt = v_ref[pl.ds(off, tk), :]
                vh = (jnp.where(lok, vt, one), jnp.where(lok, one, vt))
                out = []
                for hh, (m, acc) in enumerate(carries):
                    s = _mm_nt(qs[hh], k_ref[hh, pl.ds(off, tk), :]) * scale
                    if bias is not None:
                        s = s + bias
                    m_new = jnp.maximum(m, jnp.max(s, axis=-1, keepdims=True))
                    p = jnp.exp(s - m_new)
                    out.append((m_new, jnp.exp(m - m_new) * acc + _mm(p, vh[hh])))
                return tuple(out)

            init = (jnp.full((tq, 1), NEG, F32), jnp.zeros((tq, LANE), F32))
            carries = lax.fori_loop(0, g * nd, lambda i, c: step(i * tk, c, None), (init, init))
            for i in range(nd):
                carries = step(row0 + i * tk, carries, diag[i])
            (ma, acca), (mb, accb) = carries
            la, lb = pltpu.roll(acca, HALF, axis=1), pltpu.roll(accb, HALF, axis=1)
            o_ref[rows, :] = jnp.where(lo, acca / la, accb / lb)
            lse_ref[rows, :] = jnp.where(lo, ma + jnp.log(la), mb + jnp.log(lb))
            return 0

        lax.fori_loop(0, S // tq, block, 0)

    return pl.pallas_call(
        body, name="mla_attn_fwd", grid=(B, NPAIR), in_specs=[heads, heads, pair], out_specs=[pair, pair],
        out_shape=[jax.ShapeDtypeStruct((B * S, MLA_HEADS * MLA_V), F32)] * 2,
        compiler_params=_cp(),
    )(q, k, v)


DIL_UNROLL = 16


def _dil_geometry(gi, S):
    span, d = DIL_PATTERNS[gi]
    L = S // d
    t = _tile(L, 128)
    window = span // d
    back = min(-(-window // t) * t, L - t)
    return d, L, t, window, back


def _dil_specs(gi, S):
    qk = pl.BlockSpec((S, LANE), lambda b, j: (b, NPAIR * gi + j))
    v = pl.BlockSpec((S, LANE), lambda b, j: (b, CB_DV + NPAIR * gi + j))
    pair = pl.BlockSpec((S, LANE), lambda b, j: (b, j))
    return qk, v, pair


def _dil_bias(bias_ref, sl_ref, j, t, kw, back, window):
    row = lax.broadcasted_iota(jnp.int32, (2 * t, kw), 0)
    col = lax.broadcasted_iota(jnp.int32, (2 * t, kw), 1)
    second = row >= t
    slope = jnp.where(second, sl_ref[j, 1], sl_ref[j, 0])
    for n in range(bias_ref.shape[0]):
        dist = jnp.where(second, row - t, row) + n * back - col
        bias_ref[n] = jnp.where((dist >= 0) & (dist <= window), -slope * dist.astype(F32), NEG)


def _stack_heads(x, lo):
    zero = jnp.zeros((), x.dtype)
    return jnp.concatenate([jnp.where(lo, x, zero), jnp.where(lo, zero, x)], axis=0)


def _dil_attn_fwd(gi, slopes, qn, kn, proj, B, S):
    d, L, t, window, back = _dil_geometry(gi, S)
    kw, nq = back + t, L // t
    nbias = 2 if back else 1
    qk, vspec, pair = _dil_specs(gi, S)

    def body(sl_ref, q_ref, k_ref, v_ref, o_ref, lse_ref, qs, ks, vs, os_, ls, bias_ref):
        _to_classes(q_ref, qs, d, L, DIL_HEAD_DIM ** -0.5)
        _to_classes(k_ref, ks, d, L)
        _to_classes(v_ref, vs, d, L)
        _dil_bias(bias_ref, sl_ref, pl.program_id(1), t, kw, back, window)
        lo = _lane_lo((t, LANE))

        def block(g, _):
            qb = g % nq if d > 1 else g
            row0 = pl.multiple_of(g * t, t)
            rows = pl.ds(row0, t)
            early = qb * t < back
            keys = pl.ds(pl.multiple_of(jnp.where(early, row0 - qb * t, row0 - back), t), kw)
            s = _mm_nt(_stack_heads(qs[rows, :], lo), ks[keys, :]) + bias_ref[jnp.where(early, 0, nbias - 1)]
            m = jnp.max(s, axis=-1, keepdims=True)
            p = jnp.exp(s - m)
            l = jnp.sum(p, axis=-1, keepdims=True)
            o2 = _mm(p, vs[keys, :]) / l
            lse2 = m + jnp.log(l)
            os_[rows, :] = jnp.where(lo, o2[:t], o2[t:])
            ls[rows, :] = jnp.where(lo, lse2[:t], lse2[t:])
            return 0

        lax.fori_loop(0, d * nq, block, 0, unroll=DIL_UNROLL if d * nq % DIL_UNROLL == 0 else 1)
        _from_classes(os_, o_ref, d, L)
        _from_classes(ls, lse_ref, d, L)

    return pl.pallas_call(
        body, name=f"dil_attn_fwd_{gi}", grid=(B, NPAIR),
        in_specs=[pl.BlockSpec(memory_space=pltpu.SMEM), qk, qk, vspec], out_specs=[pair, pair],
        out_shape=[jax.ShapeDtypeStruct((B * S, DIL_WIDTH), F32)] * 2,
        scratch_shapes=[pltpu.VMEM((S, LANE), BF16)] * 3 + [pltpu.VMEM((S, LANE), F32)] * 2
                       + [pltpu.VMEM((nbias, 2 * t, kw), F32)],
        compiler_params=_cp(),
    )(slopes, qn, kn, proj)


def _mla_attn_bwd(q, k, v, do, lse, delta, B, S):
    T = B * S
    tq = _tile(S, MLA_TQ)
    tk = _tile(tq, MLA_TK)
    nd = tq // tk
    scale = MLA_QK ** -0.5
    heads, pair = _mla_specs(S)

    def body(q_ref, k_ref, v_ref, do_ref, lse_ref, dl_ref, dq_ref, dk_ref, dv_ref):
        dk_ref[...] = jnp.zeros_like(dk_ref)
        dv_ref[...] = jnp.zeros_like(dv_ref)
        lo = _lane_lo((tq, LANE))
        diag = [_causal_bias(tq, tk, i * tk) for i in range(nd)]

        def block(g, _):
            row0 = pl.multiple_of(g * tq, tq)
            rows = pl.ds(row0, tq)
            per_head = []
            for hh in range(2):
                sel = lo if hh == 0 else jnp.logical_not(lo)
                per_head.append((q_ref[hh, rows, :], jnp.where(sel, do_ref[rows, :], jnp.zeros((), BF16)),
                                 jnp.max(jnp.where(sel, lse_ref[rows, :], NEG), axis=-1, keepdims=True),
                                 jnp.max(jnp.where(sel, dl_ref[rows, :], NEG), axis=-1, keepdims=True)))

            def step(off, dq_accs, bias):
                cols = pl.ds(pl.multiple_of(off, tk), tk)
                vt = v_ref[cols, :]
                out, dv = [], None
                for hh, (qh, doh, lse_h, dl_h) in enumerate(per_head):
                    kh = k_ref[hh, cols, :]
                    s = _mm_nt(qh, kh) * scale
                    if bias is not None:
                        s = s + bias
                    p = jnp.exp(s - lse_h)
                    ds = (p * (_mm_nt(doh, vt) - dl_h)).astype(BF16)
                    dk_ref[hh, cols, :] += _mm_tn(ds, qh) * scale
                    part = _mm_tn(p, doh)
                    dv = part if dv is None else dv + part
                    out.append(dq_accs[hh] + _mm(ds, kh))
                dv_ref[cols, :] += dv
                return tuple(out)

            zero = jnp.zeros((tq, LANE), F32)
            dq_accs = lax.fori_loop(0, g * nd, lambda i, a: step(i * tk, a, None), (zero, zero))
            for i in range(nd):
                dq_accs = step(row0 + i * tk, dq_accs, diag[i])
            for hh in range(2):
                dq_ref[hh, rows, :] = dq_accs[hh] * scale
            return 0

        lax.fori_loop(0, S // tq, block, 0)

    return pl.pallas_call(
        body, name="mla_attn_bwd", grid=(B, NPAIR), in_specs=[heads, heads, pair, pair, pair, pair],
        out_specs=[heads, heads, pair],
        out_shape=[jax.ShapeDtypeStruct((MLA_HEADS, T, LANE), F32), jax.ShapeDtypeStruct((MLA_HEADS, T, LANE), F32),
                   jax.ShapeDtypeStruct((T, MLA_HEADS * MLA_V), F32)],
        compiler_params=_cp(),
    )(q, k, v, do, lse, delta)


def _dil_attn_bwd(gi, slopes, qn, kn, proj, do, lse, delta, through, B, S):
    d, L, t, window, back = _dil_geometry(gi, S)
    kw, nq = back + t, L // t
    nbias = 2 if back else 1
    scale = DIL_HEAD_DIM ** -0.5
    qk, vspec, pair = _dil_specs(gi, S)

    def body(*refs):
        refs = list(refs)
        sl_ref, q_ref, k_ref, v_ref, do_ref, lse_ref, dl_ref = refs[:7]
        dq_ref, dk_ref, dv_ref, qs, ks, vs, dos, lss, dls, dqs, dks, dvs, bias_ref = refs[-13:]
        _to_classes(q_ref, qs, d, L, scale)
        for src, dst in ((k_ref, ks), (v_ref, vs), (do_ref, dos), (lse_ref, lss), (dl_ref, dls)):
            _to_classes(src, dst, d, L)
        _dil_bias(bias_ref, sl_ref, pl.program_id(1), t, kw, back, window)
        dks[...] = jnp.zeros_like(dks)
        dvs[...] = jnp.zeros_like(dvs)
        lo = _lane_lo((t, LANE))

        def stats(ref, rows):
            x = ref[rows, :]
            return jnp.concatenate([jnp.max(jnp.where(lo, x, NEG), axis=-1, keepdims=True),
                                    jnp.max(jnp.where(lo, NEG, x), axis=-1, keepdims=True)], axis=0)

        def block(g, _):
            qb = g % nq if d > 1 else g
            row0 = pl.multiple_of(g * t, t)
            rows = pl.ds(row0, t)
            early = qb * t < back
            keys = pl.ds(pl.multiple_of(jnp.where(early, row0 - qb * t, row0 - back), t), kw)
            q2 = _stack_heads(qs[rows, :], lo)
            do2 = _stack_heads(dos[rows, :], lo)
            kt = ks[keys, :]
            s = _mm_nt(q2, kt) + bias_ref[jnp.where(early, 0, nbias - 1)]
            p = jnp.exp(s - stats(lss, rows))
            ds = (p * (_mm_nt(do2, vs[keys, :]) - stats(dls, rows))).astype(BF16)
            dq2 = _mm(ds, kt) * scale
            dqs[rows, :] = jnp.where(lo, dq2[:t], dq2[t:])
            dks[keys, :] += _mm_tn(ds, q2)
            dvs[keys, :] += _mm_tn(p, do2)
            return 0

        lax.fori_loop(0, d * nq, block, 0, unroll=DIL_UNROLL if d * nq % DIL_UNROLL == 0 else 1)
        for src, dst in ((dqs, dq_ref), (dks, dk_ref), (dvs, dv_ref)):
            _from_classes(src, dst, d, L)

    in_specs = [pl.BlockSpec(memory_space=pltpu.SMEM), qk, qk, vspec, pair, pair, pair]
    args = [slopes, qn, kn, proj, do, lse, delta]
    aliases = {}
    if through is not None:
        aliases = {len(args) + i: i for i in range(3)}
        in_specs = in_specs + [pl.BlockSpec(memory_space=pl.ANY)] * 3
        args = args + list(through)
    return pl.pallas_call(
        body, name=f"dil_attn_bwd_{gi}", grid=(B, NPAIR), in_specs=in_specs, out_specs=[qk, qk, qk],
        out_shape=[jax.ShapeDtypeStruct((B * S, DIL_QK), F32)] * 3,
        scratch_shapes=[pltpu.VMEM((S, LANE), BF16)] * 4 + [pltpu.VMEM((S, LANE), F32)] * 5
                       + [pltpu.VMEM((nbias, 2 * t, kw), F32)],
        input_output_aliases=aliases,
        compiler_params=_cp(),
    )(*args)


def _merge_proj_specs(ts):
    wide = lambda c0, w: pl.BlockSpec((ts, w), lambda i: (i, c0 * LANE // w))
    return [wide(CB_BZ, DIL_WIDTH), wide(CB_CZ, DIL_WIDTH)] + [wide(CB_GATE + 8 * i, D_MODEL) for i in range(3)]


def _merge_common(p_refs, bg_ref, ob_ref, og_refs, lse_refs):
    bz = p_refs[0][...].astype(F32)
    cz = p_refs[1][...].astype(F32)
    gates = [_sigmoid(p_refs[2 + i][...].astype(F32) + bg_ref[:, i * D_MODEL:(i + 1) * D_MODEL]) for i in range(3)]
    ob = ob_ref[...]
    lses = [r[...] for r in lse_refs]
    mx = jnp.maximum(jnp.maximum(lses[0], lses[1]), lses[2])
    es = [jnp.exp(v - mx) for v in lses]
    inv = 1.0 / (es[0] + es[1] + es[2])
    alphas = [e * inv for e in es]
    oc = alphas[0] * og_refs[0][...] + alphas[1] * og_refs[1][...] + alphas[2] * og_refs[2][...]
    return bz, cz, gates, ob, alphas, oc


def _merge_fwd(x, proj, b_gate, ya, ob, ogs, lses, woa, wob, woc, wo):
    T = x.shape[0]
    ts = _tile(T, 256)

    def body(x_ref, p0, p1, p2, p3, p4, bg_ref, ya_ref, ob_ref, og0, og1, og2, l0, l1, l2,
             woa_ref, wob_ref, woc_ref, wo_ref, out_ref):
        bz, cz, gates, obv, alphas, oc = _merge_common((p0, p1, p2, p3, p4), bg_ref, ob_ref, (og0, og1, og2),
                                                       (l0, l1, l2))
        yb = obv * _silu(bz)
        yc = oc * _silu(cz)
        merged = (gates[0] * _mm(ya_ref[...], woa_ref[...]) + gates[1] * _mm(yb, wob_ref[...])
                  + gates[2] * _mm(yc, woc_ref[...]))
        out_ref[...] = x_ref[...] + _mm(merged, wo_ref[...])

    def whole(r, c):
        return pl.BlockSpec((r, c), lambda i: (0, 0))

    tok = lambda w: pl.BlockSpec((ts, w), lambda i: (i, 0))
    return pl.pallas_call(
        body, name="merge_fwd", grid=(T // ts,),
        in_specs=[tok(D_MODEL)] + _merge_proj_specs(ts) + [whole(1, 3 * D_MODEL), tok(CONV_WIDTH)]
                 + [tok(DIL_WIDTH)] * 7 + [whole(CONV_WIDTH, D_MODEL)] * 3 + [whole(D_MODEL, D_MODEL)],
        out_specs=tok(D_MODEL),
        out_shape=jax.ShapeDtypeStruct((T, D_MODEL), F32),
        compiler_params=_cp(),
    )(x, *[proj] * 5, b_gate, ya, ob, *ogs, *lses, woa, wob, woc, wo)


def _merge_bwd(dout, proj, b_gate, ya, ob, ogs, lses, woa, wob, woc, wo):
    T = dout.shape[0]
    ts = _tile(T, 256)
    nt = T // ts

    def body(do_ref, p0, p1, p2, p3, p4, bg_ref, ya_ref, ob_ref, og0, og1, og2, l0, l1, l2,
             woa_ref, wob_ref, woc_ref, wo_ref,
             dp_ref, dya_ref, dob_ref, dlb_ref, dg0, dg1, dg2, dl0, dl1, dl2,
             mg_ref, dpa_ref, dpb_ref, dpc_ref, yb_ref, yc_ref, dbg_ref, st_bz, st_cz, st_gate, sems):
        step = pl.program_id(0)
        slot = step % 2

        def copies_of(s):
            return _put_copies([st_bz, st_cz, st_gate], dp_ref, sems, s % 2, pl.ds(pl.multiple_of(s * ts, ts), ts),
                               [CB_BZ * LANE, CB_CZ * LANE, CB_GATE * LANE])

        @pl.when(step >= 2)
        def _():
            for cp in copies_of(step - 2):
                cp.wait()

        bz, cz, gates, obv, alphas, oc = _merge_common((p0, p1, p2, p3, p4), bg_ref, ob_ref, (og0, og1, og2),
                                                       (l0, l1, l2))
        (sb, dsb), (sc, dsc) = _silu_and_grad(bz), _silu_and_grad(cz)
        yb = obv * sb
        yc = oc * sc
        ps = [_mm(ya_ref[...], woa_ref[...]), _mm(yb, wob_ref[...]), _mm(yc, woc_ref[...])]
        mg_ref[...] = (gates[0] * ps[0] + gates[1] * ps[1] + gates[2] * ps[2]).astype(BF16)
        yb_ref[...] = yb.astype(BF16)
        yc_ref[...] = yc.astype(BF16)
        dm = _mm_nt(do_ref[...], wo_ref[...])
        dps = []
        first = pl.program_id(0) == 0
        for i, dref in enumerate((dpa_ref, dpb_ref, dpc_ref)):
            g = gates[i]
            dpi = (dm * g).astype(BF16)
            dref[...] = dpi
            dps.append(dpi)
            dgp = dm * ps[i] * g * (1.0 - g)
            st_gate[slot, :, i * D_MODEL:(i + 1) * D_MODEL] = dgp.astype(BF16)
            part = jnp.sum(dgp, axis=0, keepdims=True)

            @pl.when(first)
            def _():
                dbg_ref[:, i * D_MODEL:(i + 1) * D_MODEL] = part

            @pl.when(jnp.logical_not(first))
            def _():
                dbg_ref[:, i * D_MODEL:(i + 1) * D_MODEL] += part

        dya_ref[...] = _mm_nt(dps[0], woa_ref[...])
        dyb = _mm_nt(dps[1], wob_ref[...])
        dyc = _mm_nt(dps[2], woc_ref[...])
        st_bz[slot] = (dyb * obv * dsb).astype(BF16)
        st_cz[slot] = (dyc * oc * dsc).astype(BF16)
        for cp in copies_of(step):
            cp.start()
        dob = dyb * sb
        doc = dyc * sc
        dob_ref[...] = dob.astype(BF16)
        for c in range(NPAIR):
            cs = slice(c * LANE, (c + 1) * LANE)
            dlb_ref[:, cs] = _head_bcast_sum(dob[:, cs] * obv[:, cs])
            dd = _head_bcast_sum(doc[:, cs] * oc[:, cs])
            for a, dref, lref in zip(alphas, (dg0, dg1, dg2), (dl0, dl1, dl2)):
                dref[:, cs] = (a[:, cs] * doc[:, cs]).astype(BF16)
                lref[:, cs] = a[:, cs] * dd

        @pl.when(step == nt - 1)
        def _():
            if nt >= 2:
                for cp in copies_of(step - 1):
                    cp.wait()
            for cp in copies_of(step):
                cp.wait()

    def whole(r, c):
        return pl.BlockSpec((r, c), lambda i: (0, 0))

    tok = lambda w: pl.BlockSpec((ts, w), lambda i: (i, 0))
    sd = jax.ShapeDtypeStruct
    W = DIL_WIDTH
    return pl.pallas_call(
        body, name="merge_bwd", grid=(nt,),
        in_specs=[tok(D_MODEL)] + _merge_proj_specs(ts) + [whole(1, 3 * D_MODEL), tok(CONV_WIDTH)] + [tok(W)] * 7
                 + [whole(CONV_WIDTH, D_MODEL)] * 3 + [whole(D_MODEL, D_MODEL)],
        out_specs=[pl.BlockSpec(memory_space=pl.ANY), tok(CONV_WIDTH), tok(W), tok(W)] + [tok(W)] * 6
                  + [tok(D_MODEL)] * 4 + [tok(W), tok(W), whole(1, 3 * D_MODEL)],
        out_shape=[sd((T, PP), BF16), sd((T, CONV_WIDTH), F32), sd((T, W), BF16), sd((T, W), F32)]
                  + [sd((T, W), BF16)] * 3 + [sd((T, W), F32)] * 3
                  + [sd((T, D_MODEL), BF16)] * 4 + [sd((T, W), BF16)] * 2 + [sd((1, 3 * D_MODEL), F32)],
        scratch_shapes=[pltpu.VMEM((2, ts, W), BF16), pltpu.VMEM((2, ts, W), BF16),
                        pltpu.VMEM((2, ts, 3 * D_MODEL), BF16), pltpu.SemaphoreType.DMA((2, 3))],
        compiler_params=_cp(),
    )(dout, *[proj] * 5, b_gate, ya, ob, *ogs, *lses, woa, wob, woc, wo)


def _loss_head(y, target):
    T = y.shape[0]
    ts = _tile(T, 512)

    def body(y_ref, t_ref, d_ref, l_ref):
        e = y_ref[...] - t_ref[...]
        d_ref[...] = e * (1.0 / D_MODEL)
        l_ref[...] = jnp.zeros((1, 8, LANE), F32) + jnp.sum(e * e)

    tok = pl.BlockSpec((ts, D_MODEL), lambda i: (i, 0))
    return pl.pallas_call(
        body, name="loss_head", grid=(T // ts,), in_specs=[tok, tok],
        out_specs=[tok, pl.BlockSpec((1, 8, LANE), lambda i: (i, 0, 0))],
        out_shape=[jax.ShapeDtypeStruct((T, D_MODEL), F32), jax.ShapeDtypeStruct((T // ts, 8, LANE), F32)],
        compiler_params=_cp(),
    )(y, target)


def _my_index():
    return 4 * lax.axis_index("x") + 2 * lax.axis_index("y") + lax.axis_index("c")


def _peers():
    x, y, c = (lax.axis_index(a) for a in AXES)
    out = []
    for kk in range(1, N_DEV):
        px = 1 - x if kk & 4 else x
        py = 1 - y if kk & 2 else y
        pc = 1 - c if kk & 1 else c
        out.append(((px, py, pc), 4 * px + 2 * py + pc))
    return out


def _exchange(arrays, name, gather):
    n = len(arrays)

    def body(*refs):
        srcs, outs = refs[:n], refs[n:2 * n]
        send_sems, recv_sems, local_sems = refs[2 * n:]
        me = _my_index()
        peers = _peers()
        started = []
        for a, (src, out) in enumerate(zip(srcs, outs)):
            mine = pltpu.make_async_copy(src if gather else src.at[me], out.at[me], local_sems.at[a])
            mine.start()
            started.append(mine)
        sends = []
        for i, (pos, idx) in enumerate(peers):
            for a, (src, out) in enumerate(zip(srcs, outs)):
                cp = pltpu.make_async_remote_copy(
                    src_ref=src if gather else src.at[idx], dst_ref=out.at[me], send_sem=send_sems.at[a, i],
                    recv_sem=recv_sems.at[a, i], device_id=pos, device_id_type=pl.DeviceIdType.MESH)
                cp.start()
                sends.append(cp)
        for i, (pos, idx) in enumerate(peers):
            for a, (src, out) in enumerate(zip(srcs, outs)):
                pltpu.make_async_remote_copy(
                    src_ref=src if gather else src.at[idx], dst_ref=out.at[idx], send_sem=send_sems.at[a, i],
                    recv_sem=recv_sems.at[a, i], device_id=pos, device_id_type=pl.DeviceIdType.MESH).wait_recv()
        for cp in sends:
            cp.wait_send()
        for mine in started:
            mine.wait()

    any_space = pl.BlockSpec(memory_space=pl.ANY)
    return pl.pallas_call(
        body, name=name, in_specs=[any_space] * n, out_specs=[any_space] * n,
        out_shape=[jax.ShapeDtypeStruct(((N_DEV,) + a.shape) if gather else a.shape, a.dtype) for a in arrays],
        scratch_shapes=[pltpu.SemaphoreType.DMA((n, N_DEV - 1)), pltpu.SemaphoreType.DMA((n, N_DEV - 1)),
                        pltpu.SemaphoreType.DMA((n,))],
    )(*arrays)


N_CHIP = 4


def _chip_places():
    x, y, c = (lax.axis_index(a) for a in AXES)
    return (x, y, c), (x, y, 1 - c), [(1 - x, y, c), (x, 1 - y, c), (1 - x, 1 - y, c)]


def _index_of(pos):
    return 4 * pos[0] + 2 * pos[1] + pos[2]


def _sibling_swap(arrays, name):
    n = len(arrays)

    def body(*refs):
        srcs, outs = refs[:n], refs[n:2 * n]
        send_sems, recv_sems = refs[2 * n:]
        (x, y, c), sibling, _ = _chip_places()
        sends = []
        for a, (src, out) in enumerate(zip(srcs, outs)):
            for q in range(N_CHIP):
                def copy(core, a=a, q=q, src=src, out=out):
                    return pltpu.make_async_remote_copy(
                        src_ref=src.at[2 * q + core], dst_ref=out.at[q], send_sem=send_sems.at[N_CHIP * a + q],
                        recv_sem=recv_sems.at[N_CHIP * a + q], device_id=sibling, device_id_type=pl.DeviceIdType.MESH)
                mine = copy(1 - c)
                mine.start()
                sends.append((mine, copy(c)))
        for mine, arrival in sends:
            arrival.wait_recv()
            mine.wait_send()

    any_space = pl.BlockSpec(memory_space=pl.ANY)
    return pl.pallas_call(
        body, name=name, in_specs=[any_space] * n, out_specs=[any_space] * n,
        out_shape=[jax.ShapeDtypeStruct((N_CHIP,) + a.shape[1:], a.dtype) for a in arrays],
        scratch_shapes=[pltpu.SemaphoreType.DMA((N_CHIP * n,)), pltpu.SemaphoreType.DMA((N_CHIP * n,))],
    )(*arrays)


def _chip_pair_sum(part, got, name):
    R, C = part.shape[1:]
    tr = R
    while tr * C * part.dtype.itemsize > REDUCE_BLOCK_BYTES // 4 and tr % 32 == 0:
        tr //= 2
    c = lax.axis_index("c")

    def body(c_ref, p_ref, g_ref, o_ref):
        del c_ref
        o_ref[...] = (p_ref[...].astype(F32) + g_ref[...].astype(F32)).astype(o_ref.dtype)

    return pl.pallas_call(
        body, name=name, grid_spec=pltpu.PrefetchScalarGridSpec(
            num_scalar_prefetch=1, grid=(N_CHIP, R // tr),
            in_specs=[pl.BlockSpec((None, tr, C), lambda q, i, cr: (2 * q + cr[0], i, 0)),
                      pl.BlockSpec((None, tr, C), lambda q, i, cr: (q, i, 0))],
            out_specs=pl.BlockSpec((None, tr, C), lambda q, i, cr: (q, i, 0))),
        out_shape=jax.ShapeDtypeStruct((N_CHIP, R, C), part.dtype),
        compiler_params=_cp(),
    )(jnp.reshape(c, (1,)).astype(jnp.int32), part, got)


def _peer_count(mode):
    return {"chips": N_CHIP - 1, "near": N_CHIP}.get(mode, N_DEV - 1)


def _remote_copies(srcs, lands, send_sems, recv_sems, mode):
    if mode == "chips":
        (x, y, _), _, others = _chip_places()
        my_slot, peers = 2 * x + y, [(chip, 2 * chip[0] + chip[1]) for chip in others]
    elif mode == "near":
        me, sibling, others = _chip_places()
        my_slot, peers = _index_of(me), [(pos, _index_of(pos)) for pos in [sibling] + others]
    else:
        my_slot, peers = _my_index(), _peers()
    whole = mode in ("gather", "near")
    out = []
    for i, (pos, idx) in enumerate(peers):
        for a, (src, land) in enumerate(zip(srcs, lands)):
            def copy(slot, a=a, src=src, land=land, i=i, pos=pos, idx=idx):
                return pltpu.make_async_remote_copy(
                    src_ref=src if whole else src.at[idx], dst_ref=land.at[slot],
                    send_sem=send_sems.at[a * len(peers) + i], recv_sem=recv_sems.at[a * len(peers) + i],
                    device_id=pos, device_id_type=pl.DeviceIdType.MESH)
            out.append((copy(my_slot), copy(idx)))
    return out


def _exchange_start(arrays, name, mode):
    n = len(arrays)
    hbm = pl.BlockSpec(memory_space=pltpu.HBM)
    sem = pl.BlockSpec(memory_space=pltpu.SEMAPHORE)
    lands = [lax.empty(((N_DEV,) + a.shape) if mode in ("gather", "near") else a.shape, a.dtype) for a in arrays]

    def body(*refs):
        srcs, lands_ = refs[:n], refs[n:2 * n]
        send_sems, recv_sems = refs[2 * n:2 * n + 2]
        for mine, _ in _remote_copies(srcs, lands_, send_sems, recv_sems, mode):
            mine.start()
        refs[-1][...] = jnp.zeros_like(refs[-1])

    sems = pltpu.SemaphoreType.DMA((n * _peer_count(mode),))
    buffers = [pltpu.HBM(a.shape, a.dtype) for a in list(arrays) + lands]
    res = pl.pallas_call(
        body, name=name, in_specs=[hbm] * (2 * n), out_specs=[sem, sem] + [hbm] * (2 * n) + [pl.BlockSpec(memory_space=pltpu.VMEM)],
        out_shape=[sems, sems] + buffers + [jax.ShapeDtypeStruct((8, LANE), F32)],
        input_output_aliases={i: 2 + i for i in range(2 * n)},
        compiler_params=pltpu.CompilerParams(has_side_effects=pltpu.SideEffectType.DATAFLOW_SIDE_EFFECTING),
    )(*[pltpu.with_memory_space_constraint(a, pltpu.HBM) for a in list(arrays) + lands])
    return (res[0], res[1], res[2:2 + n], res[2 + n:2 + 2 * n]), res[-1]


def _exchange_wait(handle, after, name, mode):
    send_sems, recv_sems, srcs, lands = handle
    n = len(srcs)
    after = list(after) if isinstance(after, (list, tuple)) else [after]
    hbm = pl.BlockSpec(memory_space=pltpu.HBM)
    sem = pl.BlockSpec(memory_space=pltpu.SEMAPHORE)

    def body(*refs):
        for mine, arrival in _remote_copies(refs[:n], refs[n:2 * n], refs[2 * n], refs[2 * n + 1], mode):
            mine.wait_send()
            arrival.wait_recv()

    res = pl.pallas_call(
        body, name=name, in_specs=[hbm] * (2 * n) + [sem, sem] + [pl.BlockSpec(memory_space=pl.ANY)] * len(after),
        out_specs=[hbm] * (2 * n), out_shape=[pltpu.HBM(a.shape, a.dtype) for a in list(srcs) + list(lands)],
        input_output_aliases={i: i for i in range(2 * n)},
        compiler_params=pltpu.CompilerParams(has_side_effects=pltpu.SideEffectType.DATAFLOW_SIDE_EFFECTING),
    )(*srcs, *lands, send_sems, recv_sems, *after)
    return res[n:]


def _sibling_forward(lands, name):
    n = len(lands)

    def body(*refs):
        ins, outs, send_sems, recv_sems = refs[:n], refs[n:2 * n], refs[2 * n], refs[2 * n + 1]
        (x, y, c), sibling, others = _chip_places()
        copies = []
        for a, (src, out) in enumerate(zip(ins, outs)):
            for j, chip in enumerate(others):
                def copy(core, a=a, j=j, chip=chip, src=src, out=out):
                    slot = _index_of((chip[0], chip[1], core))
                    return pltpu.make_async_remote_copy(
                        src_ref=src.at[slot], dst_ref=out.at[slot], send_sem=send_sems.at[3 * a + j],
                        recv_sem=recv_sems.at[3 * a + j], device_id=sibling, device_id_type=pl.DeviceIdType.MESH)
                mine = copy(c)
                mine.start()
                copies.append((mine, copy(1 - c)))
        for mine, arrival in copies:
            arrival.wait_recv()
        for mine, arrival in copies:
            mine.wait_send()

    any_space = pl.BlockSpec(memory_space=pl.ANY)
    return pl.pallas_call(
        body, name=name, in_specs=[any_space] * n, out_specs=[any_space] * n,
        out_shape=[jax.ShapeDtypeStruct(a.shape, a.dtype) for a in lands],
        scratch_shapes=[pltpu.SemaphoreType.DMA((3 * n,)), pltpu.SemaphoreType.DMA((3 * n,))],
        input_output_aliases={i: i for i in range(n)},
    )(*lands)


def _own_slot(land, mine, slot=None):
    slot = _my_index() if slot is None else slot
    return lax.dynamic_update_slice(land, mine, (slot,) + (0,) * (land.ndim - 1))


def _adamw(w, g, m, v):
    m = ADAM_B1 * m + (1.0 - ADAM_B1) * g
    v = ADAM_B2 * v + (1.0 - ADAM_B2) * (g * g)
    m_hat = m / (1.0 - ADAM_B1 ** ADAM_STEP)
    v_hat = v / (1.0 - ADAM_B2 ** ADAM_STEP)
    delta = -ADAM_LR * (m_hat / (jnp.sqrt(v_hat) + ADAM_EPS) + ADAM_WD * w)
    return delta, m, v


def _reduce_adamw(parts, w, m, v, name):
    nparts = len(parts)
    R, C = parts[0].shape[1:]
    tr = R
    while N_DEV * tr * C * parts[0].dtype.itemsize > REDUCE_BLOCK_BYTES and tr % 32 == 0:
        tr //= 2
    steps = R // tr

    def body(*refs):
        w_ref, m_ref, v_ref, g_ref, d_ref, nm_ref, nv_ref = refs[nparts:]
        for k, p_ref in enumerate(refs[:nparts]):
            @pl.when(pl.program_id(0) // steps == k)
            def _():
                g = p_ref[0].astype(F32)
                for s in range(1, p_ref.shape[0]):
                    g = g + p_ref[s].astype(F32)
                g_ref[...] = g
                d_ref[...], nm_ref[...], nv_ref[...] = _adamw(w_ref[...], g, m_ref[...], v_ref[...])

    def part_spec(k):
        return pl.BlockSpec((parts[k].shape[0], tr, C), lambda i: (0, jnp.clip(i - k * steps, 0, steps - 1), 0))

    row = pl.BlockSpec((tr, C), lambda i: (i, 0))
    return pl.pallas_call(
        body, name=name, grid=(nparts * steps,),
        in_specs=[part_spec(k) for k in range(nparts)] + [row, row, row],
        out_specs=[row] * 4, out_shape=[jax.ShapeDtypeStruct((nparts * R, C), F32)] * 4,
        compiler_params=_cp(),
    )(*parts, w, m, v)


BIG = ("w_in", "w_uq", "w_ukv", "w_out_a", "w_out_b", "w_out_c", "w_o")
SMALL = ("norm_g", "b_gate", "conv_w", "conv_b", "q_a_norm_g", "kv_a_norm_g", "mla_q_norm_g", "mla_k_norm_g",
         "dil_q_norm_g", "dil_k_norm_g")
PACK_ROWS = 128
REDUCE_BLOCK_BYTES = 6 * 1024 * 1024


def _pack_local(tensors):
    flat = jnp.concatenate([t.reshape(-1) for t in tensors])
    pad = (-flat.shape[0]) % (PACK_ROWS * LANE)
    return jnp.concatenate([flat, jnp.zeros((pad,), flat.dtype)]).reshape(-1, LANE)


def _unpack_local(rows, like):
    flat = rows.reshape(-1)
    out, off = [], 0
    for t in like:
        out.append(flat[off:off + t.size].reshape(t.shape))
        off += t.size
    return out


def _cols_to_slots(a):
    k = a.shape[0]
    return a.reshape(k, N_DEV, -1).transpose(1, 0, 2)


def _slots_to_cols(s):
    return s.transpose(1, 0, 2).reshape(s.shape[1], -1)


def _rope_tables(S):
    inv = ROPE_THETA ** (-jnp.arange(0, MLA_ROPE, 2, dtype=F32) / MLA_ROPE)
    ang = jnp.arange(S, dtype=F32)[:, None] * inv[None, :]
    cos, sin = jnp.cos(ang), jnp.sin(ang)
    one = jnp.ones((S, MLA_NOPE), F32)
    z16, z32, z64 = (jnp.zeros((S, n), F32) for n in (16, 32, 64))
    cosp = jnp.concatenate([one, cos, cos, jnp.ones((S, 32), F32)], axis=1)
    sa = jnp.concatenate([z64, -sin, z16, z32], axis=1)
    sb = jnp.concatenate([z64, z16, sin, z32], axis=1)
    return cosp, sa, sb


def _alibi_slopes():
    n = DIL_GROUPS * DIL_HEADS
    m = 2.0 ** (-8.0 * jnp.arange(1, n + 1, dtype=F32) / n)
    return m.reshape(DIL_GROUPS, NPAIR, 2)


def _pad_slots(s):
    n, k, c = s.shape
    return _slots_to_cols(jnp.concatenate([s, jnp.zeros((n, k, LANE - c), s.dtype)], axis=2))


def _layer_params(gw, small, l):
    p = {}
    p["wp"] = _pad_columns(gw["w_in"])
    p["norm_g"] = small["norm_g"][l][None]
    p["b_gate"] = small["b_gate"][l][None]
    p["conv_w"] = gw["conv_w"].transpose(1, 0, 2).reshape(CONV_K, CONV_WIDTH)
    p["conv_b"] = small["conv_b"][l][None]
    p["gq"] = small["q_a_norm_g"][l][None]
    p["gkv"] = small["kv_a_norm_g"][l][None]
    p["wuqp"] = _pad_slots(gw["w_uq"])
    kv = gw["w_ukv"]
    p["wkp"] = _pad_slots(kv[:, :, :MLA_NOPE])
    p["wv"] = kv[:, :, MLA_NOPE:].transpose(1, 0, 2).reshape(MLA_KV_LORA, MLA_HEADS * MLA_V)
    zpad = jnp.zeros((1, LANE - MLA_QK), F32)
    p["gmq"] = jnp.concatenate([small["mla_q_norm_g"][l][None], zpad], axis=1)
    p["gmk"] = jnp.concatenate([small["mla_k_norm_g"][l][None], zpad], axis=1)
    tile = lambda g: jnp.broadcast_to(g[:, None, :], (DIL_GROUPS, DIL_HEADS, DIL_HEAD_DIM)).reshape(1, DIL_QK)
    p["gdq"] = tile(small["dil_q_norm_g"][l])
    p["gdk"] = tile(small["dil_k_norm_g"][l])
    p["woa"], p["wob"], p["woc"] = (_slots_to_cols(gw[n]) for n in ("w_out_a", "w_out_b", "w_out_c"))
    p["wo"] = gw["w_o"].reshape(D_MODEL, D_MODEL)
    return p


def _layer_fwd(x, p, tabs, slopes, B, S):
    proj, ht = _inproj_fwd(x, p["norm_g"], p["wp"])
    ya = _mixa_fwd(proj, p["conv_w"], p["conv_b"], B, S)
    q, k, v = _mla_prep_fwd(proj, p["gq"], p["gkv"], p["wuqp"], p["wkp"], p["wv"], p["gmq"], p["gmk"], *tabs, S)
    ob, lse_b = _mla_attn_fwd(q, k, v, B, S)
    qn, kn = _dil_prep_fwd(proj, p["gdq"], p["gdk"])
    ogs, lses = [], []
    for gi in range(DIL_GROUPS):
        o, lse = _dil_attn_fwd(gi, slopes[gi], qn, kn, proj, B, S)
        ogs.append(o)
        lses.append(lse)
    out = _merge_fwd(x, proj, p["b_gate"], ya, ob, ogs, lses, p["woa"], p["wob"], p["woc"], p["wo"])
    saved = dict(x=x, proj=proj, ht=ht, ya=ya, q=q, k=k, v=v, ob=ob, lse_b=lse_b, qn=qn, kn=kn, ogs=ogs, lses=lses)
    return out, saved


def _layer_bwd(dout, sv, p, tabs, slopes, B, S, big_ready=None):
    proj = sv["proj"]
    (dproj, dya, dob, dlb, dg0, dg1, dg2, dl0, dl1, dl2, merged, dpa, dpb, dpc, yb, yc, dbg) = _merge_bwd(
        dout, proj, p["b_gate"], sv["ya"], sv["ob"], sv["ogs"], sv["lses"], p["woa"], p["wob"], p["woc"], p["wo"])
    g = {}
    g["w_o"] = _matmul_tn(merged, dout, "dw_o").reshape(N_DEV, D_MODEL // N_DEV, D_MODEL)
    g["w_out_a"] = _cols_to_slots(_matmul_tn(sv["ya"], dpa, "dw_out_a"))
    g["w_out_b"] = _cols_to_slots(_matmul_tn(yb, dpb, "dw_out_b"))
    g["w_out_c"] = _cols_to_slots(_matmul_tn(yc, dpc, "dw_out_c"))
    g["b_gate"] = dbg[0]
    dproj, st = _mixa_bwd(dproj, dya, proj, p["conv_w"], p["conv_b"], B, S)
    g["conv_w"] = st[0:CONV_K]
    g["conv_b"] = st[CONV_K]
    dq, dk, dv = _mla_attn_bwd(sv["q"], sv["k"], sv["v"], dob, sv["lse_b"], dlb, B, S)
    dproj, dwuqp, dwkp, dwv, dgq, dgkv, dgmq, dgmk = _mla_prep_bwd(
        dproj, dq, dk, dv, proj, p["gq"], p["gkv"], p["wuqp"], p["wkp"], p["wv"], p["gmq"], p["gmk"], *tabs, S)
    g["w_uq"] = _cols_to_slots(dwuqp)[:, :, :MLA_QK]
    g["w_ukv"] = jnp.concatenate([_cols_to_slots(dwkp)[:, :, :MLA_NOPE], _cols_to_slots(dwv)], axis=2)
    g["q_a_norm_g"], g["kv_a_norm_g"] = dgq[0], dgkv[0]
    g["mla_q_norm_g"], g["mla_k_norm_g"] = dgmq[0, :MLA_QK], dgmk[0, :MLA_QK]
    dqkv = None
    for gi, (dog, dlg) in enumerate(((dg0, dl0), (dg1, dl1), (dg2, dl2))):
        dqkv = _dil_attn_bwd(gi, slopes[gi], sv["qn"], sv["kn"], proj, dog, sv["lses"][gi], dlg, dqkv, B, S)
    dproj, dgdq, dgdk = _dil_prep_bwd(dproj, *dqkv, proj, p["gdq"], p["gdk"])
    g["dil_q_norm_g"] = dgdq.reshape(DIL_GROUPS, DIL_HEADS, DIL_HEAD_DIM).sum(axis=1)
    g["dil_k_norm_g"] = dgdk.reshape(DIL_GROUPS, DIL_HEADS, DIL_HEAD_DIM).sum(axis=1)
    g["w_in"] = _unpad_columns(_matmul_nn(sv["ht"], dproj, "dw_in"))
    token = None if big_ready is None else big_ready(g)
    dx, dng = _inproj_bwd_x(dproj, p["wp"], sv["x"], _after(token, p["norm_g"]), dout)
    g["norm_g"] = dng[0]
    return dx, g


def _after(token, a):
    return a if token is None else a + token[0:1, 0:1]


def _local_step(x, target, small, B, S, weights_of, grads_out, big_ready=None):
    tabs = _rope_tables(S)
    sl = _alibi_slopes()
    slopes = [sl[gi] * float(DIL_PATTERNS[gi][1]) for gi in range(DIL_GROUPS)]
    params, saved = [], []
    for l in range(DEPTH):
        gw, token = weights_of(l, x)
        p = _layer_params(gw, small, l)
        p["norm_g"] = _after(token, p["norm_g"])
        x, sv = _layer_fwd(x, p, tabs, slopes, B, S)
        params.append(p)
        saved.append(sv)
    dout, lparts = _loss_head(x, target)
    sq = jnp.sum(lparts[:, 0, 0])
    token = None
    for l in reversed(range(DEPTH)):
        p = dict(params[l], b_gate=_after(token, params[l]["b_gate"]))
        ready = None if big_ready is None else (lambda g, l=l: big_ready(l, g))
        dout, g = _layer_bwd(dout, saved[l], p, tabs, slopes, B, S, ready)
        token = grads_out(l, g, dout)
    return sq, dout


def kernel(x, norm_g, w_in, b_gate, conv_w, conv_b, q_a_norm_g, w_uq, kv_a_norm_g, w_ukv, mla_q_norm_g, mla_k_norm_g, dil_q_norm_g, dil_k_norm_g, w_out_a, w_out_b, w_out_c, w_o, loss_target, m_norm_g, m_w_in, m_b_gate, m_conv_w, m_conv_b, m_q_a_norm_g, m_w_uq, m_kv_a_norm_g, m_w_ukv, m_mla_q_norm_g, m_mla_k_norm_g, m_dil_q_norm_g, m_dil_k_norm_g, m_w_out_a, m_w_out_b, m_w_out_c, m_w_o, v_norm_g, v_w_in, v_b_gate, v_conv_w, v_conv_b, v_q_a_norm_g, v_w_uq, v_kv_a_norm_g, v_w_ukv, v_mla_q_norm_g, v_mla_k_norm_g, v_dil_q_norm_g, v_dil_k_norm_g, v_w_out_a, v_w_out_b, v_w_out_c, v_w_o):
    names = ("norm_g", "w_in", "b_gate", "conv_w", "conv_b", "q_a_norm_g", "w_uq", "kv_a_norm_g", "w_ukv",
             "mla_q_norm_g", "mla_k_norm_g", "dil_q_norm_g", "dil_k_norm_g", "w_out_a", "w_out_b", "w_out_c", "w_o")
    w = dict(zip(names, (norm_g, w_in, b_gate, conv_w, conv_b, q_a_norm_g, w_uq, kv_a_norm_g, w_ukv, mla_q_norm_g,
                         mla_k_norm_g, dil_q_norm_g, dil_k_norm_g, w_out_a, w_out_b, w_out_c, w_o)))
    m = dict(zip(names, (m_norm_g, m_w_in, m_b_gate, m_conv_w, m_conv_b, m_q_a_norm_g, m_w_uq, m_kv_a_norm_g, m_w_ukv,
                         m_mla_q_norm_g, m_mla_k_norm_g, m_dil_q_norm_g, m_dil_k_norm_g, m_w_out_a, m_w_out_b,
                         m_w_out_c, m_w_o)))
    v = dict(zip(names, (v_norm_g, v_w_in, v_b_gate, v_conv_w, v_conv_b, v_q_a_norm_g, v_w_uq, v_kv_a_norm_g, v_w_ukv,
                         v_mla_q_norm_g, v_mla_k_norm_g, v_dil_q_norm_g, v_dil_k_norm_g, v_w_out_a, v_w_out_b,
                         v_w_out_c, v_w_o)))
    B, S, _ = x.shape
    me = _my_index()
    cshard = CONV_WIDTH // N_DEV

    shards = [[w[n][0].astype(BF16) for n in BIG]]
    state = {}

    def widen(t):
        return lax.dynamic_update_slice(jnp.zeros((DEPTH, CONV_K, CONV_WIDTH), F32), t, (0, 0, me * cshard))

    pick = lambda d: [widen(d[n]) if n == "conv_w" else d[n] for n in SMALL]

    def weights_of(l, after):
        if l == 0:
            first = shards[0] + [conv_w]
            handle, token = _exchange_start(first, "all_gather_weights_0_start", "near")
            zero = token[0:1, 0:1]
            state["shards1"] = [(w[n][1] + zero).astype(BF16) for n in BIG]
            for n in BIG:
                state["rows", n] = [a.reshape(-1, a.shape[-1]) + zero for a in (w[n], m[n], v[n])]
            state["small"] = [_pack_local(pick(d)) + zero for d in (w, m, v)]
            busy = state["shards1"] + [a for n in BIG for a in state["rows", n]] + state["small"]
            landed = _exchange_wait(handle, busy, "all_gather_weights_0_wait", "near")
            landed = _sibling_forward(landed, "all_gather_weights_0_forward")
            got = [_own_slot(a, s[None]) for a, s in zip(landed, first)]
            state["gather"], token = _exchange_start(state["shards1"], "all_gather_weights_1_start", "gather")
            state["conv_w"] = got[-1]
        else:
            landed = _exchange_wait(state["gather"], after, "all_gather_weights_1_wait", "gather")
            got, token = [_own_slot(a, s[None]) for a, s in zip(landed, state["shards1"])], None
        gw = dict(zip(BIG, got))
        gw["conv_w"] = state["conv_w"][:, l]
        return gw, token

    recv, small_parts = {}, {}
    my_chip = 2 * lax.axis_index("x") + lax.axis_index("y")

    def big_ready(l, g):
        send = [g[n].astype(BF16) for n in BIG]
        if l == DEPTH - 1:
            state["scatter"], token = _exchange_start(send, "exchange_weight_grads_1_start", "scatter")
        else:
            swapped = _sibling_swap(send, "exchange_weight_grads_0_sibling")
            send = [_chip_pair_sum(s, t, "chip_pair_sum_" + n) for n, s, t in zip(BIG, send, swapped)]
            state["chips"], token = _exchange_start(send, "exchange_weight_grads_0_start", "chips")
        state["sent", l] = send
        return token

    def grads_out(l, g, after):
        small_parts[l] = [g[n] for n in SMALL]
        if l == DEPTH - 1:
            return None
        for k, key, mode, slot in ((DEPTH - 1, "scatter", "scatter", me), (0, "chips", "chips", my_chip)):
            landed = _exchange_wait(state[key], after, f"exchange_weight_grads_{k}_wait", mode)
            mine = [lax.dynamic_slice_in_dim(s, slot, 1, axis=0) for s in state["sent", k]]
            recv[k] = [_own_slot(a, s, slot) for a, s in zip(landed, mine)]
        return None

    sq, grad_x = _local_step(x.reshape(B * S, D_MODEL), loss_target.reshape(B * S, D_MODEL), w, B, S,
                             weights_of, grads_out, big_ready)
    loss = lax.psum(sq * (0.5 / D_MODEL), AXES)

    res = {}
    for i, n in enumerate(BIG):
        outs = _reduce_adamw([recv[l][i] for l in range(DEPTH)], *state["rows", n], "reduce_adamw_" + n)
        res[n] = tuple(a.reshape(w[n].shape) for a in outs)
    part = {n: jnp.stack([small_parts[l][i] for l in range(DEPTH)]) for i, n in enumerate(SMALL)}

    small_like = [part[n] for n in SMALL]
    parts, = _exchange([_pack_local(small_like)], "all_gather_small_grads", gather=True)
    gs, ds, ms, vs = _reduce_adamw([parts], *state["small"], "reduce_adamw_small")
    for n, t in zip(SMALL, zip(*(_unpack_local(a, small_like) for a in (gs, ds, ms, vs)))):
        if n == "conv_w":
            t = tuple(lax.dynamic_slice(a, (0, 0, me * cshard), (DEPTH, CONV_K, cshard)) for a in t)
        res[n] = t

    out = [loss, grad_x.reshape(B, S, D_MODEL)]
    for i in range(4):
        out += [res[n][i] for n in names]
    return tuple(out)
```

```python
import jax
import jax.numpy as jnp
from jax import lax
from jax.experimental import pallas as pl
from jax.experimental.pallas import tpu as pltpu

F32 = jnp.float32
BF16 = jnp.bfloat16

D_MODEL = 1024
DEPTH = 2
CONV_WIDTH = 512
CONV_K = 3
MLA_HEADS = 8
MLA_Q_LORA = 256
MLA_KV_LORA = 128
MLA_NOPE = 64
MLA_ROPE = 32
MLA_V = 64
MLA_QK = MLA_NOPE + MLA_ROPE
ROPE_THETA = 10000.0
DIL_PATTERNS = ((128, 1), (512, 4), (2048, 16))
DIL_GROUPS = 3
DIL_HEADS = 8
DIL_HEAD_DIM = 64
DIL_WIDTH = DIL_HEADS * DIL_HEAD_DIM
DIL_QK = DIL_GROUPS * DIL_WIDTH
EPS = 1e-6
N_IN = 11168

ADAM_LR = 0.001
ADAM_B1 = 0.9
ADAM_B2 = 0.999
ADAM_EPS = 1e-08
ADAM_WD = 0.01
ADAM_STEP = 10

N_DEV = 8
AXES = ("x", "y", "c")
LANE = 128
HALF = 64
NPAIR = 4

CB_AB, CB_AC, CB_AX, CB_AZ = 0, 4, 8, 12
CB_CQ, CB_CKV, CB_KPE = 16, 18, 19
CB_BZ = 20
CB_DQ, CB_DK, CB_DV = 24, 36, 48
CB_CZ, CB_GATE = 60, 64
NCB = 88
PP = NCB * LANE
KPE_END = CB_KPE * LANE + MLA_ROPE
SHARD_COLS = N_IN // N_DEV
NEG = -1e30
VMEM_LIMIT = 56 * 1024 * 1024


def _pad_columns(shards):
    parts = []
    for p in range(N_DEV):
        cut = min(max(KPE_END - p * SHARD_COLS, 0), SHARD_COLS)
        if 0 < cut < SHARD_COLS:
            parts += [shards[p, :, :cut], jnp.zeros((shards.shape[1], LANE - MLA_ROPE), shards.dtype), shards[p, :, cut:]]
        else:
            parts.append(shards[p])
    return jnp.concatenate(parts, axis=1)


def _unpad_columns(wp):
    def columns(a, b):
        gap = LANE - MLA_ROPE
        if b <= KPE_END:
            return wp[:, a:b]
        if a >= KPE_END:
            return wp[:, a + gap:b + gap]
        return jnp.concatenate([wp[:, a:KPE_END], wp[:, KPE_END + gap:b + gap]], axis=1)

    return jnp.stack([columns(p * SHARD_COLS, (p + 1) * SHARD_COLS) for p in range(N_DEV)])


def _put_copies(stages, dst_ref, sems, slot, rows, cols):
    return [pltpu.make_async_copy(st.at[slot], dst_ref.at[rows, pl.ds(c0, st.shape[-1])], sems.at[slot, k])
            for k, (st, c0) in enumerate(zip(stages, cols))]


def _put_pipeline(step, nsteps, copies_of, fill):
    @pl.when(step >= 2)
    def _():
        for cp in copies_of(step - 2):
            cp.wait()

    fill(step % 2)
    for cp in copies_of(step):
        cp.start()

    @pl.when(step == nsteps - 1)
    def _():
        if nsteps >= 2:
            for cp in copies_of(step - 1):
                cp.wait()
        for cp in copies_of(step):
            cp.wait()


def _cp():
    return pltpu.CompilerParams(vmem_limit_bytes=VMEM_LIMIT)


def _rstd(x, n):
    return lax.rsqrt(jnp.sum(x * x, axis=-1, keepdims=True) * (1.0 / n) + EPS)


def _sigmoid(z):
    return 1.0 / (1.0 + jnp.exp(-z))


def _silu(z):
    return z * _sigmoid(z)


def _silu_and_grad(z):
    s = _sigmoid(z)
    return z * s, s * (1.0 + z * (1.0 - s))


def _mm(a, b):
    return jnp.dot(a.astype(BF16), b.astype(BF16), preferred_element_type=F32)


def _mm_nt(a, b):
    return lax.dot_general(a.astype(BF16), b.astype(BF16), (((1,), (1,)), ((), ())), preferred_element_type=F32)


def _mm_tn(a, b):
    return lax.dot_general(a.astype(BF16), b.astype(BF16), (((0,), (0,)), ((), ())), preferred_element_type=F32)


def _lane_lo(shape):
    return lax.broadcasted_iota(jnp.int32, shape, len(shape) - 1) < HALF


def _head_bcast_sum(x, terms=3):
    w = x.shape[-1]
    same = (lax.broadcasted_iota(jnp.int32, (w, w), 0) // HALF) == (lax.broadcasted_iota(jnp.int32, (w, w), 1) // HALF)
    ones = jnp.where(same, 1.0, 0.0).astype(jnp.bfloat16)
    total = None
    for _ in range(terms):
        term = x.astype(jnp.bfloat16)
        x = x - term.astype(F32)
        part = jnp.dot(term, ones, preferred_element_type=F32)
        total = part if total is None else total + part
    return total


def _rope(t, cos, sa, sb):
    return t * cos + pltpu.roll(t, LANE - 16, axis=1) * sa + pltpu.roll(t, 16, axis=1) * sb


def _rope_t(d, cos, sa, sb):
    return d * cos + pltpu.roll(d * sa, 16, axis=1) + pltpu.roll(d * sb, LANE - 16, axis=1)


def _shift_down(u, k):
    rows = lax.broadcasted_iota(jnp.int32, u.shape, 0)
    return jnp.where(rows >= k, pltpu.roll(u, k, axis=0), 0.0)


def _shift_up(u, k):
    n = u.shape[0]
    rows = lax.broadcasted_iota(jnp.int32, u.shape, 0)
    return jnp.where(rows < n - k, pltpu.roll(u, n - k, axis=0), 0.0)


def _tile(n, want):
    t = min(n, want)
    assert n % t == 0, (n, want)
    return t


def _inproj_fwd(x, g, wp):
    T = x.shape[0]
    tm, tn = _tile(T, 2048), 512

    def body(x_ref, g_ref, w_ref, proj_ref, ht_ref, h_ref):
        @pl.when(pl.program_id(1) == 0)
        def _():
            n = min(tm, 512)
            for r0 in range(0, tm, n):
                xv = x_ref[r0:r0 + n, :]
                h = xv * _rstd(xv, D_MODEL) * g_ref[...]
                h_ref[r0:r0 + n, :] = h.astype(BF16)
                ht_ref[:, r0:r0 + n] = h.T.astype(BF16)

        proj_ref[...] = jnp.dot(h_ref[...], w_ref[...], preferred_element_type=F32).astype(BF16)

    return pl.pallas_call(
        body, name="inproj_fwd", grid=(T // tm, PP // tn),
        in_specs=[pl.BlockSpec((tm, D_MODEL), lambda i, j: (i, 0)),
                  pl.BlockSpec((1, D_MODEL), lambda i, j: (0, 0)),
                  pl.BlockSpec((D_MODEL, tn), lambda i, j: (0, j))],
        out_specs=[pl.BlockSpec((tm, tn), lambda i, j: (i, j)),
                   pl.BlockSpec((D_MODEL, tm), lambda i, j: (0, i))],
        out_shape=[jax.ShapeDtypeStruct((T, PP), BF16), jax.ShapeDtypeStruct((D_MODEL, T), BF16)],
        scratch_shapes=[pltpu.VMEM((tm, D_MODEL), BF16)],
        compiler_params=_cp(),
    )(x, g, wp)


def _matmul_nn(at, b, name):
    K, T = at.shape
    N = b.shape[1]
    tt, tn = _tile(T, 1024), _tile(N, 2816)
    nk = T // tt

    def body(a_ref, b_ref, o_ref, acc_ref):
        k = pl.program_id(1)

        @pl.when(k == 0)
        def _():
            acc_ref[...] = jnp.zeros_like(acc_ref)

        acc_ref[...] += jnp.dot(a_ref[...], b_ref[...], preferred_element_type=F32)

        @pl.when(k == nk - 1)
        def _():
            o_ref[...] = acc_ref[...].astype(BF16)

    return pl.pallas_call(
        body, name=name, grid=(N // tn, nk),
        in_specs=[pl.BlockSpec((K, tt), lambda j, k: (0, k)),
                  pl.BlockSpec((tt, tn), lambda j, k: (k, j))],
        out_specs=pl.BlockSpec((K, tn), lambda j, k: (0, j)),
        out_shape=jax.ShapeDtypeStruct((K, N), BF16),
        scratch_shapes=[pltpu.VMEM((K, tn), F32)],
        compiler_params=_cp(),
    )(at, b)


def _matmul_tn(a, b, name):
    T, K = a.shape
    N = b.shape[1]
    tt, tn = _tile(T, 512), _tile(N, 1024)

    def body(a_ref, b_ref, o_ref):
        @pl.when(pl.program_id(1) == 0)
        def _():
            o_ref[...] = jnp.zeros_like(o_ref)

        o_ref[...] += _mm_tn(a_ref[...], b_ref[...])

    return pl.pallas_call(
        body, name=name, grid=(N // tn, T // tt),
        in_specs=[pl.BlockSpec((tt, K), lambda j, k: (k, 0)),
                  pl.BlockSpec((tt, tn), lambda j, k: (k, j))],
        out_specs=pl.BlockSpec((K, tn), lambda j, k: (0, j)),
        out_shape=jax.ShapeDtypeStruct((K, N), F32),
        compiler_params=_cp(),
    )(a, b)


def _inproj_bwd_x(dproj, wp, x, g, dout):
    T = x.shape[0]
    tm, tk = _tile(T, 1024), 1024
    nk = PP // tk

    def body(dp_ref, w_ref, x_ref, g_ref, do_ref, dx_ref, dg_ref, acc_ref):
        i, k = pl.program_id(0), pl.program_id(1)

        @pl.when(k == 0)
        def _():
            acc_ref[...] = jnp.zeros_like(acc_ref)

        @pl.when((k == 0) & (i == 0))
        def _():
            dg_ref[...] = jnp.zeros_like(dg_ref)

        acc_ref[...] += _mm_nt(dp_ref[...], w_ref[...])

        @pl.when(k == nk - 1)
        def _():
            dh = acc_ref[...]
            xv = x_ref[...]
            r = _rstd(xv, D_MODEL)
            gy = dh * g_ref[...]
            dot = jnp.sum(xv * gy, axis=-1, keepdims=True) * (1.0 / D_MODEL)
            dx_ref[...] = do_ref[...] + r * gy - xv * (r * r * r) * dot
            dg_ref[...] += jnp.sum(dh * xv * r, axis=0, keepdims=True)

    return pl.pallas_call(
        body, name="inproj_bwd_x", grid=(T // tm, nk),
        in_specs=[pl.BlockSpec((tm, tk), lambda i, k: (i, k)),
                  pl.BlockSpec((D_MODEL, tk), lambda i, k: (0, k)),
                  pl.BlockSpec((tm, D_MODEL), lambda i, k: (i, 0)),
                  pl.BlockSpec((1, D_MODEL), lambda i, k: (0, 0)),
                  pl.BlockSpec((tm, D_MODEL), lambda i, k: (i, 0))],
        out_specs=[pl.BlockSpec((tm, D_MODEL), lambda i, k: (i, 0)),
                   pl.BlockSpec((1, D_MODEL), lambda i, k: (0, 0))],
        out_shape=[jax.ShapeDtypeStruct((T, D_MODEL), F32), jax.ShapeDtypeStruct((1, D_MODEL), F32)],
        scratch_shapes=[pltpu.VMEM((tm, D_MODEL), F32)],
        compiler_params=_cp(),
    )(dproj, wp, x, g, dout)


A_SEGS = (CB_AB, CB_AC, CB_AX, CB_AZ)


def _mixa_fwd(proj, cw, cb, B, S):
    nc = CONV_WIDTH // LANE

    def body(ab_ref, ac_ref, ax_ref, az_ref, cw_ref, cb_ref, y_ref):
        ab, ac, ax, az = (r[...].astype(F32) for r in (ab_ref, ac_ref, ax_ref, az_ref))
        u = ac * ax
        conv = cb_ref[...] + cw_ref[0:1, :] * _shift_down(u, 2) + cw_ref[1:2, :] * _shift_down(u, 1) + cw_ref[2:3, :] * u
        y_ref[...] = (ab * conv * _silu(az)).astype(BF16)

    return pl.pallas_call(
        body, name="mixa_fwd", grid=(B, nc),
        in_specs=[pl.BlockSpec((S, LANE), lambda b, j, c0=c0: (b, c0 + j)) for c0 in A_SEGS]
                 + [pl.BlockSpec((CONV_K, LANE), lambda b, j: (0, j)),
                    pl.BlockSpec((1, LANE), lambda b, j: (0, j))],
        out_specs=pl.BlockSpec((S, LANE), lambda b, j: (b, j)),
        out_shape=jax.ShapeDtypeStruct((B * S, CONV_WIDTH), BF16),
        compiler_params=_cp(),
    )(proj, proj, proj, proj, cw, cb)


def _mixa_bwd(dproj, dy, proj, cw, cb, B, S):
    nc = CONV_WIDTH // LANE

    def body(dpin_ref, dy_ref, ab_ref, ac_ref, ax_ref, az_ref, cw_ref, cb_ref, dp_ref, st_ref, stage, sems):
        del dpin_ref
        j, b = pl.program_id(0), pl.program_id(1)
        ab, ac, ax, az = (r[...].astype(F32) for r in (ab_ref, ac_ref, ax_ref, az_ref))
        u = ac * ax
        u1, u2 = _shift_down(u, 1), _shift_down(u, 2)
        w0, w1, w2 = cw_ref[0:1, :], cw_ref[1:2, :], cw_ref[2:3, :]
        conv = cb_ref[...] + w0 * u2 + w1 * u1 + w2 * u
        s, ds_az = _silu_and_grad(az)
        d = dy_ref[...]
        dconv = d * ab * s
        du = w2 * dconv + w1 * _shift_up(dconv, 1) + w0 * _shift_up(dconv, 2)
        grads = (d * conv * s, du * ax, du * ac, d * ab * conv * ds_az)

        def fill(slot):
            for k, v in enumerate(grads):
                stage[slot, k] = v.astype(BF16)

        def copies_of(step):
            sj, sb = step // B, step % B
            return _put_copies([stage.at[:, k] for k in range(4)], dp_ref, sems, step % 2,
                               pl.ds(pl.multiple_of(sb * S, S), S),
                               [pl.multiple_of((c0 + sj) * LANE, LANE) for c0 in A_SEGS])

        _put_pipeline(j * B + b, nc * B, copies_of, fill)
        row = lax.broadcasted_iota(jnp.int32, (8, LANE), 0)
        st = jnp.zeros((8, LANE), F32)
        for r, v in enumerate((dconv * u2, dconv * u1, dconv * u, dconv)):
            st = st + jnp.where(row == r, jnp.sum(v, axis=0, keepdims=True), 0.0)

        @pl.when(pl.program_id(1) == 0)
        def _():
            st_ref[...] = st

        @pl.when(pl.program_id(1) != 0)
        def _():
            st_ref[...] += st

    return pl.pallas_call(
        body, name="mixa_bwd", grid=(nc, B),
        in_specs=[pl.BlockSpec(memory_space=pl.ANY),
                  pl.BlockSpec((S, LANE), lambda j, b: (b, j))]
                 + [pl.BlockSpec((S, LANE), lambda j, b, c0=c0: (b, c0 + j)) for c0 in A_SEGS]
                 + [pl.BlockSpec((CONV_K, LANE), lambda j, b: (0, j)),
                    pl.BlockSpec((1, LANE), lambda j, b: (0, j))],
        out_specs=[pl.BlockSpec(memory_space=pl.ANY),
                   pl.BlockSpec((8, LANE), lambda j, b: (0, j))],
        out_shape=[jax.ShapeDtypeStruct(dproj.shape, BF16), jax.ShapeDtypeStruct((8, CONV_WIDTH), F32)],
        scratch_shapes=[pltpu.VMEM((2, 4, S, LANE), BF16), pltpu.SemaphoreType.DMA((2, 4))],
        input_output_aliases={0: 0},
        compiler_params=_cp(),
    )(dproj, dy, proj, proj, proj, proj, cw, cb)


def _mla_prep_fwd(proj, gq, gkv, wuqp, wkp, wv, gmq, gmk, cos, sa, sb, S):
    T = proj.shape[0]
    ts = _tile(S, 512)
    ns = S // ts
    W = MLA_HEADS * LANE

    def body(p_ref, gq_ref, gkv_ref, wuq_ref, wk_ref, wv_ref, gmq_ref, gmk_ref, cos_ref, sa_ref, sb_ref,
             q_ref, k_ref, v_ref):
        cq = p_ref[:, 0:2 * LANE].astype(F32)
        ckv = p_ref[:, 2 * LANE:3 * LANE].astype(F32)
        kpe = pltpu.roll(p_ref[:, 3 * LANE:4 * LANE].astype(F32), HALF, axis=1)
        cqn = cq * _rstd(cq, MLA_Q_LORA) * gq_ref[...]
        ckn = (ckv * _rstd(ckv, MLA_KV_LORA) * gkv_ref[...]).astype(BF16)
        q0 = _mm(cqn, wuq_ref[...])
        kn = _mm(ckn, wk_ref[...])
        v_ref[...] = _mm(ckn, wv_ref[...]).astype(BF16)
        c, a, b = cos_ref[...], sa_ref[...], sb_ref[...]
        kpe_rot = _rope(kpe * gmk_ref[...], c, a, b)
        for h in range(MLA_HEADS):
            q0h = q0[:, h * LANE:(h + 1) * LANE]
            q_ref[h] = _rope(q0h * _rstd(q0h, MLA_QK) * gmq_ref[...], c, a, b).astype(BF16)
            knh = kn[:, h * LANE:(h + 1) * LANE]
            k_ref[h] = (_rstd(knh + kpe, MLA_QK) * (knh * gmk_ref[...] + kpe_rot)).astype(BF16)

    def whole(r, c):
        return pl.BlockSpec((r, c), lambda i: (0, 0))

    tab = pl.BlockSpec((ts, LANE), lambda i: (i % ns, 0))
    return pl.pallas_call(
        body, name="mla_prep_fwd", grid=(T // ts,),
        in_specs=[pl.BlockSpec((ts, 4 * LANE), lambda i: (i, CB_CQ // 4)),
                  whole(1, MLA_Q_LORA), whole(1, MLA_KV_LORA), whole(MLA_Q_LORA, W), whole(MLA_KV_LORA, W),
                  whole(MLA_KV_LORA, MLA_HEADS * MLA_V), whole(1, LANE), whole(1, LANE), tab, tab, tab],
        out_specs=[pl.BlockSpec((MLA_HEADS, ts, LANE), lambda i: (0, i, 0)),
                   pl.BlockSpec((MLA_HEADS, ts, LANE), lambda i: (0, i, 0)),
                   pl.BlockSpec((ts, MLA_HEADS * MLA_V), lambda i: (i, 0))],
        out_shape=[jax.ShapeDtypeStruct((MLA_HEADS, T, LANE), BF16), jax.ShapeDtypeStruct((MLA_HEADS, T, LANE), BF16),
                   jax.ShapeDtypeStruct((T, MLA_HEADS * MLA_V), BF16)],
        compiler_params=_cp(),
    )(proj, gq, gkv, wuqp, wkp, wv, gmq, gmk, cos, sa, sb)


def _mla_prep_bwd(dproj, dq, dk, dv, proj, gq, gkv, wuqp, wkp, wv, gmq, gmk, cos, sa, sb, S):
    T = proj.shape[0]
    ts = _tile(S, 256)
    ns = S // ts
    W = MLA_HEADS * LANE

    def body(dpin_ref, dq_ref, dk_ref, dv_ref, p_ref, gq_ref, gkv_ref, wuq_ref, wk_ref, wv_ref, gmq_ref, gmk_ref,
             cos_ref, sa_ref, sb_ref,
             dp_ref, dwuq_ref, dwk_ref, dwv_ref, dgq_ref, dgkv_ref, dgmq_ref, dgmk_ref, dq0_ref, dkn_ref):
        del dpin_ref

        @pl.when(pl.program_id(0) == 0)
        def _():
            for r in (dwuq_ref, dwk_ref, dwv_ref, dgq_ref, dgkv_ref, dgmq_ref, dgmk_ref):
                r[...] = jnp.zeros_like(r)

        cq = p_ref[:, 0:2 * LANE].astype(F32)
        ckv = p_ref[:, 2 * LANE:3 * LANE].astype(F32)
        kpe = pltpu.roll(p_ref[:, 3 * LANE:4 * LANE].astype(F32), HALF, axis=1)
        rq = _rstd(cq, MLA_Q_LORA)
        rkv = _rstd(ckv, MLA_KV_LORA)
        gq, gkv, gmq, gmk = gq_ref[...], gkv_ref[...], gmq_ref[...], gmk_ref[...]
        cqn = (cq * rq * gq).astype(BF16)
        ckn = (ckv * rkv * gkv).astype(BF16)
        q0 = _mm(cqn, wuq_ref[...])
        kn = _mm(ckn, wk_ref[...])
        c, a, b = cos_ref[...], sa_ref[...], sb_ref[...]
        lane = lax.broadcasted_iota(jnp.int32, (ts, LANE), 1)
        dgmq = jnp.zeros((1, LANE), F32)
        dgmk = jnp.zeros((1, LANE), F32)
        nope = lane < MLA_NOPE
        kpe_rot = _rope(kpe * gmk, c, a, b)
        dk_sum = jnp.zeros((ts, LANE), F32)
        back = jnp.zeros((ts, 1), F32)
        for h in range(MLA_HEADS):
            q0h = q0[:, h * LANE:(h + 1) * LANE]
            r = _rstd(q0h, MLA_QK)
            d1 = _rope_t(dq_ref[h], c, a, b)
            gy = d1 * gmq
            dq0_ref[:, h * LANE:(h + 1) * LANE] = (
                r * gy - q0h * (r * r * r) * (jnp.sum(q0h * gy, axis=-1, keepdims=True) * (1.0 / MLA_QK))).astype(BF16)
            dgmq = dgmq + jnp.sum(d1 * q0h * r, axis=0, keepdims=True)
            knh = kn[:, h * LANE:(h + 1) * LANE]
            dkh = dk_ref[h]
            r = _rstd(knh + kpe, MLA_QK)
            r3dot = (r * r * r) * (jnp.sum((knh * gmk + kpe_rot) * dkh, axis=-1, keepdims=True) * (1.0 / MLA_QK))
            dkn_ref[:, h * LANE:(h + 1) * LANE] = jnp.where(nope, r * gmk * dkh - knh * r3dot, 0.0).astype(BF16)
            dgmk = dgmk + jnp.sum(jnp.where(nope, dkh * knh * r, 0.0), axis=0, keepdims=True)
            dk_sum = dk_sum + r * dkh
            back = back + r3dot
        rot = jnp.where(nope | (lane >= MLA_QK), 0.0, _rope_t(dk_sum, c, a, b))
        dkpe = gmk * rot - kpe * back
        dgmk = dgmk + jnp.sum(kpe * rot, axis=0, keepdims=True)
        dq0 = dq0_ref[...]
        dkn = dkn_ref[...]
        dvv = dv_ref[...]
        dwuq_ref[...] += _mm_tn(cqn, dq0)
        dwk_ref[...] += _mm_tn(ckn, dkn)
        dwv_ref[...] += _mm_tn(ckn, dvv)
        dgmq_ref[...] += dgmq
        dgmk_ref[...] += dgmk
        dcqn = _mm_nt(dq0, wuq_ref[...])
        gy = dcqn * gq
        dp_ref[:, 0:2 * LANE] = (
            rq * gy - cq * (rq * rq * rq) * (jnp.sum(cq * gy, axis=-1, keepdims=True) * (1.0 / MLA_Q_LORA))).astype(BF16)
        dgq_ref[...] += jnp.sum(dcqn * cq * rq, axis=0, keepdims=True)
        dckn = _mm_nt(dkn, wk_ref[...]) + _mm_nt(dvv, wv_ref[...])
        gy = dckn * gkv
        dp_ref[:, 2 * LANE:3 * LANE] = (
            rkv * gy - ckv * (rkv * rkv * rkv) * (jnp.sum(ckv * gy, axis=-1, keepdims=True) * (1.0 / MLA_KV_LORA))).astype(BF16)
        dgkv_ref[...] += jnp.sum(dckn * ckv * rkv, axis=0, keepdims=True)
        dp_ref[:, 3 * LANE:4 * LANE] = pltpu.roll(dkpe, HALF, axis=1).astype(BF16)

    def whole(r, c):
        return pl.BlockSpec((r, c), lambda i: (0, 0))

    tab = pl.BlockSpec((ts, LANE), lambda i: (i % ns, 0))
    heads = pl.BlockSpec((MLA_HEADS, ts, LANE), lambda i: (0, i, 0))
    return pl.pallas_call(
        body, name="mla_prep_bwd", grid=(T // ts,),
        in_specs=[pl.BlockSpec(memory_space=pl.ANY), heads, heads,
                  pl.BlockSpec((ts, MLA_HEADS * MLA_V), lambda i: (i, 0)),
                  pl.BlockSpec((ts, 4 * LANE), lambda i: (i, CB_CQ // 4)),
                  whole(1, MLA_Q_LORA), whole(1, MLA_KV_LORA), whole(MLA_Q_LORA, W), whole(MLA_KV_LORA, W),
                  whole(MLA_KV_LORA, MLA_HEADS * MLA_V), whole(1, LANE), whole(1, LANE), tab, tab, tab],
        out_specs=[pl.BlockSpec((ts, 4 * LANE), lambda i: (i, CB_CQ // 4)),
                   whole(MLA_Q_LORA, W), whole(MLA_KV_LORA, W), whole(MLA_KV_LORA, MLA_HEADS * MLA_V),
                   whole(1, MLA_Q_LORA), whole(1, MLA_KV_LORA), whole(1, LANE), whole(1, LANE)],
        out_shape=[jax.ShapeDtypeStruct(dproj.shape, BF16),
                   jax.ShapeDtypeStruct((MLA_Q_LORA, W), F32), jax.ShapeDtypeStruct((MLA_KV_LORA, W), F32),
                   jax.ShapeDtypeStruct((MLA_KV_LORA, MLA_HEADS * MLA_V), F32),
                   jax.ShapeDtypeStruct((1, MLA_Q_LORA), F32), jax.ShapeDtypeStruct((1, MLA_KV_LORA), F32),
                   jax.ShapeDtypeStruct((1, LANE), F32), jax.ShapeDtypeStruct((1, LANE), F32)],
        scratch_shapes=[pltpu.VMEM((ts, W), BF16), pltpu.VMEM((ts, W), BF16)],
        input_output_aliases={0: 0},
        compiler_params=_cp(),
    )(dproj, dq, dk, dv, proj, gq, gkv, wuqp, wkp, wv, gmq, gmk, cos, sa, sb)


def _dil_prep_fwd(proj, gq, gk):
    T = proj.shape[0]
    ts = _tile(T, 512)

    def body(pq_ref, pk_ref, gq_ref, gk_ref, q_ref, k_ref):
        for c in range(NPAIR):
            cs = slice(c * LANE, (c + 1) * LANE)
            t = jnp.concatenate([pq_ref[:, cs], pk_ref[:, cs]], axis=1).astype(F32)
            y = t * lax.rsqrt(_head_bcast_sum(t * t, terms=2) * (1.0 / DIL_HEAD_DIM) + EPS)
            q_ref[:, cs] = (y[:, 0:LANE] * gq_ref[:, cs]).astype(BF16)
            k_ref[:, cs] = (y[:, LANE:2 * LANE] * gk_ref[:, cs]).astype(BF16)

    col = pl.BlockSpec((1, DIL_WIDTH), lambda i, g: (0, g))
    out = pl.BlockSpec((ts, DIL_WIDTH), lambda i, g: (i, g))
    seg = lambda c0: pl.BlockSpec((ts, DIL_WIDTH), lambda i, g: (i, c0 // NPAIR + g))
    return pl.pallas_call(
        body, name="dil_prep_fwd", grid=(T // ts, DIL_GROUPS),
        in_specs=[seg(CB_DQ), seg(CB_DK), col, col],
        out_specs=[out, out],
        out_shape=[jax.ShapeDtypeStruct((T, DIL_QK), BF16)] * 2,
        compiler_params=_cp(),
    )(proj, proj, gq, gk)


def _dil_prep_bwd(dproj, ddq, ddk, ddv, proj, gq, gk):
    T = proj.shape[0]
    ts = _tile(T, 512)
    nt = T // ts

    def body(dpin_ref, ddq_ref, ddk_ref, ddv_ref, pq_ref, pk_ref, gq_ref, gk_ref, dp_ref, dgq_ref, dgk_ref,
             stage, sems):
        del dpin_ref
        g, i = pl.program_id(0), pl.program_id(1)

        @pl.when(i == 0)
        def _():
            dgq_ref[...] = jnp.zeros_like(dgq_ref)
            dgk_ref[...] = jnp.zeros_like(dgk_ref)

        def fill(slot):
            stage[slot, 2] = ddv_ref[...].astype(BF16)
            for c in range(NPAIR):
                cs = slice(c * LANE, (c + 1) * LANE)
                t = jnp.concatenate([pq_ref[:, cs], pk_ref[:, cs]], axis=1).astype(F32)
                d = jnp.concatenate([ddq_ref[:, cs], ddk_ref[:, cs]], axis=1)
                gy = d * jnp.concatenate([gq_ref[:, cs], gk_ref[:, cs]], axis=1)
                r = lax.rsqrt(_head_bcast_sum(t * t, terms=2) * (1.0 / DIL_HEAD_DIM) + EPS)
                dot = _head_bcast_sum(t * gy, terms=2) * (1.0 / DIL_HEAD_DIM)
                dx = (r * gy - t * (r * r * r) * dot).astype(BF16)
                stage[slot, 0, :, cs] = dx[:, 0:LANE]
                stage[slot, 1, :, cs] = dx[:, LANE:2 * LANE]
                part = jnp.sum(d * t * r, axis=0, keepdims=True)
                dgq_ref[:, cs] += part[:, 0:LANE]
                dgk_ref[:, cs] += part[:, LANE:2 * LANE]

        def copies_of(step):
            sg, si = step // nt, step % nt
            return _put_copies([stage.at[:, k] for k in range(3)], dp_ref, sems, step % 2,
                               pl.ds(pl.multiple_of(si * ts, ts), ts),
                               [pl.multiple_of((c0 + NPAIR * sg) * LANE, LANE) for c0 in (CB_DQ, CB_DK, CB_DV)])

        _put_pipeline(g * nt + i, DIL_GROUPS * nt, copies_of, fill)

    col = pl.BlockSpec((1, DIL_WIDTH), lambda g, i: (0, g))
    tok = pl.BlockSpec((ts, DIL_WIDTH), lambda g, i: (i, g))
    seg = lambda c0: pl.BlockSpec((ts, DIL_WIDTH), lambda g, i: (i, c0 // NPAIR + g))
    return pl.pallas_call(
        body, name="dil_prep_bwd", grid=(DIL_GROUPS, nt),
        in_specs=[pl.BlockSpec(memory_space=pl.ANY), tok, tok, tok, seg(CB_DQ), seg(CB_DK), col, col],
        out_specs=[pl.BlockSpec(memory_space=pl.ANY), col, col],
        out_shape=[jax.ShapeDtypeStruct(dproj.shape, BF16), jax.ShapeDtypeStruct((1, DIL_QK), F32),
                   jax.ShapeDtypeStruct((1, DIL_QK), F32)],
        scratch_shapes=[pltpu.VMEM((2, 3, ts, DIL_WIDTH), BF16), pltpu.SemaphoreType.DMA((2, 3))],
        input_output_aliases={0: 0},
        compiler_params=_cp(),
    )(dproj, ddq, ddk, ddv, proj, proj, gq, gk)


COPY_ROWS = 256


def _to_classes(src_ref, dst_ref, d, L, scale=None):
    m = min(L, max(8, COPY_ROWS // d))
    for c0 in range(0, L, m):
        x = src_ref[c0 * d:(c0 + m) * d, :].astype(F32)
        if scale is not None:
            x = x * scale
        if d > 1:
            x = jnp.swapaxes(x.reshape(m, d, LANE), 0, 1)
        for r in range(d):
            dst_ref[r * L + c0:r * L + c0 + m, :] = (x[r] if d > 1 else x).astype(dst_ref.dtype)


def _from_classes(src_ref, dst_ref, d, L):
    n = min(L, COPY_ROWS)
    for r in range(d):
        for c0 in range(0, L, n):
            rows = pl.ds(r + c0 * d, n, stride=d) if d > 1 else pl.ds(c0, n)
            dst_ref[rows, :] = src_ref[r * L + c0:r * L + c0 + n, :].astype(dst_ref.dtype)


MLA_TQ, MLA_TK = 512, 512


def _causal_bias(tq, tk, shift):
    row = lax.broadcasted_iota(jnp.int32, (tq, tk), 0)
    col = lax.broadcasted_iota(jnp.int32, (tq, tk), 1)
    return jnp.where(row >= col + shift, 0.0, NEG)


def _mla_specs(S):
    heads = pl.BlockSpec((2, S, LANE), lambda b, j: (j, b, 0))
    pair = pl.BlockSpec((S, LANE), lambda b, j: (b, j))
    return heads, pair


def _mla_attn_fwd(q, k, v, B, S):
    tq = _tile(S, MLA_TQ)
    tk = _tile(tq, MLA_TK)
    nd = tq // tk
    scale = MLA_QK ** -0.5
    heads, pair = _mla_specs(S)

    def body(q_ref, k_ref, v_ref, o_ref, lse_ref):
        lo, lok = _lane_lo((tq, LANE)), _lane_lo((tk, LANE))
        diag = [_causal_bias(tq, tk, i * tk) for i in range(nd)]

        def block(g, _):
            row0 = pl.multiple_of(g * tq, tq)
            rows = pl.ds(row0, tq)
            qs = [q_ref[hh, rows, :] for hh in range(2)]

            one = jnp.ones((), BF16)

            def step(off, carries, bias):
                off = pl.multiple_of(off, tk)
                vt = v_ref[pl.ds(off, tk), :]
                vh = (jnp.where(lok, vt, one), jnp.where(lok, one, vt))
                out = []
                for hh, (m, acc) in enumerate(carries):
                    s = _mm_nt(qs[hh], k_ref[hh, pl.ds(off, tk), :]) * scale
                    if bias is not None:
                        s = s + bias
                    m_new = jnp.maximum(m, jnp.max(s, axis=-1, keepdims=True))
                    p = jnp.exp(s - m_new)
                    out.append((m_new, jnp.exp(m - m_new) * acc + _mm(p, vh[hh])))
                return tuple(out)

            init = (jnp.full((tq, 1), NEG, F32), jnp.zeros((tq, LANE), F32))
            carries = lax.fori_loop(0, g * nd, lambda i, c: step(i * tk, c, None), (init, init))
            for i in range(nd):
                carries = step(row0 + i * tk, carries, diag[i])
            (ma, acca), (mb, accb) = carries
            la, lb = pltpu.roll(acca, HALF, axis=1), pltpu.roll(accb, HALF, axis=1)
            o_ref[rows, :] = jnp.where(lo, acca / la, accb / lb)
            lse_ref[rows, :] = jnp.where(lo, ma + jnp.log(la), mb + jnp.log(lb))
            return 0

        lax.fori_loop(0, S // tq, block, 0)

    return pl.pallas_call(
        body, name="mla_attn_fwd", grid=(B, NPAIR), in_specs=[heads, heads, pair], out_specs=[pair, pair],
        out_shape=[jax.ShapeDtypeStruct((B * S, MLA_HEADS * MLA_V), F32)] * 2,
        compiler_params=_cp(),
    )(q, k, v)


DIL_UNROLL = 16


def _dil_geometry(gi, S):
    span, d = DIL_PATTERNS[gi]
    L = S // d
    t = _tile(L, 128)
    window = span // d
    back = min(-(-window // t) * t, L - t)
    return d, L, t, window, back


def _dil_specs(gi, S):
    qk = pl.BlockSpec((S, LANE), lambda b, j: (b, NPAIR * gi + j))
    v = pl.BlockSpec((S, LANE), lambda b, j: (b, CB_DV + NPAIR * gi + j))
    pair = pl.BlockSpec((S, LANE), lambda b, j: (b, j))
    return qk, v, pair


def _dil_bias(bias_ref, sl_ref, j, t, kw, back, window):
    row = lax.broadcasted_iota(jnp.int32, (2 * t, kw), 0)
    col = lax.broadcasted_iota(jnp.int32, (2 * t, kw), 1)
    second = row >= t
    slope = jnp.where(second, sl_ref[j, 1], sl_ref[j, 0])
    for n in range(bias_ref.shape[0]):
        dist = jnp.where(second, row - t, row) + n * back - col
        bias_ref[n] = jnp.where((dist >= 0) & (dist <= window), -slope * dist.astype(F32), NEG)


def _stack_heads(x, lo):
    zero = jnp.zeros((), x.dtype)
    return jnp.concatenate([jnp.where(lo, x, zero), jnp.where(lo, zero, x)], axis=0)


def _dil_attn_fwd(gi, slopes, qn, kn, proj, B, S):
    d, L, t, window, back = _dil_geometry(gi, S)
    kw, nq = back + t, L // t
    nbias = 2 if back else 1
    qk, vspec, pair = _dil_specs(gi, S)

    def body(sl_ref, q_ref, k_ref, v_ref, o_ref, lse_ref, qs, ks, vs, os_, ls, bias_ref):
        _to_classes(q_ref, qs, d, L, DIL_HEAD_DIM ** -0.5)
        _to_classes(k_ref, ks, d, L)
        _to_classes(v_ref, vs, d, L)
        _dil_bias(bias_ref, sl_ref, pl.program_id(1), t, kw, back, window)
        lo = _lane_lo((t, LANE))

        def block(g, _):
            qb = g % nq if d > 1 else g
            row0 = pl.multiple_of(g * t, t)
            rows = pl.ds(row0, t)
            early = qb * t < back
            keys = pl.ds(pl.multiple_of(jnp.where(early, row0 - qb * t, row0 - back), t), kw)
            s = _mm_nt(_stack_heads(qs[rows, :], lo), ks[keys, :]) + bias_ref[jnp.where(early, 0, nbias - 1)]
            m = jnp.max(s, axis=-1, keepdims=True)
            p = jnp.exp(s - m)
            l = jnp.sum(p, axis=-1, keepdims=True)
            o2 = _mm(p, vs[keys, :]) / l
            lse2 = m + jnp.log(l)
            os_[rows, :] = jnp.where(lo, o2[:t], o2[t:])
            ls[rows, :] = jnp.where(lo, lse2[:t], lse2[t:])
            return 0

        lax.fori_loop(0, d * nq, block, 0, unroll=DIL_UNROLL if d * nq % DIL_UNROLL == 0 else 1)
        _from_classes(os_, o_ref, d, L)
        _from_classes(ls, lse_ref, d, L)

    return pl.pallas_call(
        body, name=f"dil_attn_fwd_{gi}", grid=(B, NPAIR),
        in_specs=[pl.BlockSpec(memory_space=pltpu.SMEM), qk, qk, vspec], out_specs=[pair, pair],
        out_shape=[jax.ShapeDtypeStruct((B * S, DIL_WIDTH), F32)] * 2,
        scratch_shapes=[pltpu.VMEM((S, LANE), BF16)] * 3 + [pltpu.VMEM((S, LANE), F32)] * 2
                       + [pltpu.VMEM((nbias, 2 * t, kw), F32)],
        compiler_params=_cp(),
    )(slopes, qn, kn, proj)


def _mla_attn_bwd(q, k, v, do, lse, delta, B, S):
    T = B * S
    tq = _tile(S, MLA_TQ)
    tk = _tile(tq, MLA_TK)
    nd = tq // tk
    scale = MLA_QK ** -0.5
    heads, pair = _mla_specs(S)

    def body(q_ref, k_ref, v_ref, do_ref, lse_ref, dl_ref, dq_ref, dk_ref, dv_ref):
        dk_ref[...] = jnp.zeros_like(dk_ref)
        dv_ref[...] = jnp.zeros_like(dv_ref)
        lo = _lane_lo((tq, LANE))
        diag = [_causal_bias(tq, tk, i * tk) for i in range(nd)]

        def block(g, _):
            row0 = pl.multiple_of(g * tq, tq)
            rows = pl.ds(row0, tq)
            per_head = []
            for hh in range(2):
                sel = lo if hh == 0 else jnp.logical_not(lo)
                per_head.append((q_ref[hh, rows, :], jnp.where(sel, do_ref[rows, :], jnp.zeros((), BF16)),
                                 jnp.max(jnp.where(sel, lse_ref[rows, :], NEG), axis=-1, keepdims=True),
                                 jnp.max(jnp.where(sel, dl_ref[rows, :], NEG), axis=-1, keepdims=True)))

            def step(off, dq_accs, bias):
                cols = pl.ds(pl.multiple_of(off, tk), tk)
                vt = v_ref[cols, :]
                out, dv = [], None
                for hh, (qh, doh, lse_h, dl_h) in enumerate(per_head):
                    kh = k_ref[hh, cols, :]
                    s = _mm_nt(qh, kh) * scale
                    if bias is not None:
                        s = s + bias
                    p = jnp.exp(s - lse_h)
                    ds = (p * (_mm_nt(doh, vt) - dl_h)).astype(BF16)
                    dk_ref[hh, cols, :] += _mm_tn(ds, qh) * scale
                    part = _mm_tn(p, doh)
                    dv = part if dv is None else dv + part
                    out.append(dq_accs[hh] + _mm(ds, kh))
                dv_ref[cols, :] += dv
                return tuple(out)

            zero = jnp.zeros((tq, LANE), F32)
            dq_accs = lax.fori_loop(0, g * nd, lambda i, a: step(i * tk, a, None), (zero, zero))
            for i in range(nd):
                dq_accs = step(row0 + i * tk, dq_accs, diag[i])
            for hh in range(2):
                dq_ref[hh, rows, :] = dq_accs[hh] * scale
            return 0

        lax.fori_loop(0, S // tq, block, 0)

    return pl.pallas_call(
        body, name="mla_attn_bwd", grid=(B, NPAIR), in_specs=[heads, heads, pair, pair, pair, pair],
        out_specs=[heads, heads, pair],
        out_shape=[jax.ShapeDtypeStruct((MLA_HEADS, T, LANE), F32), jax.ShapeDtypeStruct((MLA_HEADS, T, LANE), F32),
                   jax.ShapeDtypeStruct((T, MLA_HEADS * MLA_V), F32)],
        compiler_params=_cp(),
    )(q, k, v, do, lse, delta)


def _dil_attn_bwd(gi, slopes, qn, kn, proj, do, lse, delta, through, B, S):
    d, L, t, window, back = _dil_geometry(gi, S)
    kw, nq = back + t, L // t
    nbias = 2 if back else 1
    scale = DIL_HEAD_DIM ** -0.5
    qk, vspec, pair = _dil_specs(gi, S)

    def body(*refs):
        refs = list(refs)
        sl_ref, q_ref, k_ref, v_ref, do_ref, lse_ref, dl_ref = refs[:7]
        dq_ref, dk_ref, dv_ref, qs, ks, vs, dos, lss, dls, dqs, dks, dvs, bias_ref = refs[-13:]
        _to_classes(q_ref, qs, d, L, scale)
        for src, dst in ((k_ref, ks), (v_ref, vs), (do_ref, dos), (lse_ref, lss), (dl_ref, dls)):
            _to_classes(src, dst, d, L)
        _dil_bias(bias_ref, sl_ref, pl.program_id(1), t, kw, back, window)
        dks[...] = jnp.zeros_like(dks)
        dvs[...] = jnp.zeros_like(dvs)
        lo = _lane_lo((t, LANE))

        def stats(ref, rows):
            x = ref[rows, :]
            return jnp.concatenate([jnp.max(jnp.where(lo, x, NEG), axis=-1, keepdims=True),
                                    jnp.max(jnp.where(lo, NEG, x), axis=-1, keepdims=True)], axis=0)

        def block(g, _):
            qb = g % nq if d > 1 else g
            row0 = pl.multiple_of(g * t, t)
            rows = pl.ds(row0, t)
            early = qb * t < back
            keys = pl.ds(pl.multiple_of(jnp.where(early, row0 - qb * t, row0 - back), t), kw)
            q2 = _stack_heads(qs[rows, :], lo)
            do2 = _stack_heads(dos[rows, :], lo)
            kt = ks[keys, :]
            s = _mm_nt(q2, kt) + bias_ref[jnp.where(early, 0, nbias - 1)]
            p = jnp.exp(s - stats(lss, rows))
            ds = (p * (_mm_nt(do2, vs[keys, :]) - stats(dls, rows))).astype(BF16)
            dq2 = _mm(ds, kt) * scale
            dqs[rows, :] = jnp.where(lo, dq2[:t], dq2[t:])
            dks[keys, :] += _mm_tn(ds, q2)
            dvs[keys, :] += _mm_tn(p, do2)
            return 0

        lax.fori_loop(0, d * nq, block, 0, unroll=DIL_UNROLL if d * nq % DIL_UNROLL == 0 else 1)
        for src, dst in ((dqs, dq_ref), (dks, dk_ref), (dvs, dv_ref)):
            _from_classes(src, dst, d, L)

    in_specs = [pl.BlockSpec(memory_space=pltpu.SMEM), qk, qk, vspec, pair, pair, pair]
    args = [slopes, qn, kn, proj, do, lse, delta]
    aliases = {}
    if through is not None:
        aliases = {len(args) + i: i for i in range(3)}
        in_specs = in_specs + [pl.BlockSpec(memory_space=pl.ANY)] * 3
        args = args + list(through)
    return pl.pallas_call(
        body, name=f"dil_attn_bwd_{gi}", grid=(B, NPAIR), in_specs=in_specs, out_specs=[qk, qk, qk],
        out_shape=[jax.ShapeDtypeStruct((B * S, DIL_QK), F32)] * 3,
        scratch_shapes=[pltpu.VMEM((S, LANE), BF16)] * 4 + [pltpu.VMEM((S, LANE), F32)] * 5
                       + [pltpu.VMEM((nbias, 2 * t, kw), F32)],
        input_output_aliases=aliases,
        compiler_params=_cp(),
    )(*args)


def _merge_proj_specs(ts):
    wide = lambda c0, w: pl.BlockSpec((ts, w), lambda i: (i, c0 * LANE // w))
    return [wide(CB_BZ, DIL_WIDTH), wide(CB_CZ, DIL_WIDTH)] + [wide(CB_GATE + 8 * i, D_MODEL) for i in range(3)]


def _merge_common(p_refs, bg_ref, ob_ref, og_refs, lse_refs):
    bz = p_refs[0][...].astype(F32)
    cz = p_refs[1][...].astype(F32)
    gates = [_sigmoid(p_refs[2 + i][...].astype(F32) + bg_ref[:, i * D_MODEL:(i + 1) * D_MODEL]) for i in range(3)]
    ob = ob_ref[...]
    lses = [r[...] for r in lse_refs]
    mx = jnp.maximum(jnp.maximum(lses[0], lses[1]), lses[2])
    es = [jnp.exp(v - mx) for v in lses]
    inv = 1.0 / (es[0] + es[1] + es[2])
    alphas = [e * inv for e in es]
    oc = alphas[0] * og_refs[0][...] + alphas[1] * og_refs[1][...] + alphas[2] * og_refs[2][...]
    return bz, cz, gates, ob, alphas, oc


def _merge_fwd(x, proj, b_gate, ya, ob, ogs, lses, woa, wob, woc, wo):
    T = x.shape[0]
    ts = _tile(T, 256)

    def body(x_ref, p0, p1, p2, p3, p4, bg_ref, ya_ref, ob_ref, og0, og1, og2, l0, l1, l2,
             woa_ref, wob_ref, woc_ref, wo_ref, out_ref):
        bz, cz, gates, obv, alphas, oc = _merge_common((p0, p1, p2, p3, p4), bg_ref, ob_ref, (og0, og1, og2),
                                                       (l0, l1, l2))
        yb = obv * _silu(bz)
        yc = oc * _silu(cz)
        merged = (gates[0] * _mm(ya_ref[...], woa_ref[...]) + gates[1] * _mm(yb, wob_ref[...])
                  + gates[2] * _mm(yc, woc_ref[...]))
        out_ref[...] = x_ref[...] + _mm(merged, wo_ref[...])

    def whole(r, c):
        return pl.BlockSpec((r, c), lambda i: (0, 0))

    tok = lambda w: pl.BlockSpec((ts, w), lambda i: (i, 0))
    return pl.pallas_call(
        body, name="merge_fwd", grid=(T // ts,),
        in_specs=[tok(D_MODEL)] + _merge_proj_specs(ts) + [whole(1, 3 * D_MODEL), tok(CONV_WIDTH)]
                 + [tok(DIL_WIDTH)] * 7 + [whole(CONV_WIDTH, D_MODEL)] * 3 + [whole(D_MODEL, D_MODEL)],
        out_specs=tok(D_MODEL),
        out_shape=jax.ShapeDtypeStruct((T, D_MODEL), F32),
        compiler_params=_cp(),
    )(x, *[proj] * 5, b_gate, ya, ob, *ogs, *lses, woa, wob, woc, wo)


def _merge_bwd(dout, proj, b_gate, ya, ob, ogs, lses, woa, wob, woc, wo):
    T = dout.shape[0]
    ts = _tile(T, 256)
    nt = T // ts

    def body(do_ref, p0, p1, p2, p3, p4, bg_ref, ya_ref, ob_ref, og0, og1, og2, l0, l1, l2,
             woa_ref, wob_ref, woc_ref, wo_ref,
             dp_ref, dya_ref, dob_ref, dlb_ref, dg0, dg1, dg2, dl0, dl1, dl2,
             mg_ref, dpa_ref, dpb_ref, dpc_ref, yb_ref, yc_ref, dbg_ref, st_bz, st_cz, st_gate, sems):
        step = pl.program_id(0)
        slot = step % 2

        def copies_of(s):
            return _put_copies([st_bz, st_cz, st_gate], dp_ref, sems, s % 2, pl.ds(pl.multiple_of(s * ts, ts), ts),
                               [CB_BZ * LANE, CB_CZ * LANE, CB_GATE * LANE])

        @pl.when(step >= 2)
        def _():
            for cp in copies_of(step - 2):
                cp.wait()

        bz, cz, gates, obv, alphas, oc = _merge_common((p0, p1, p2, p3, p4), bg_ref, ob_ref, (og0, og1, og2),
                                                       (l0, l1, l2))
        (sb, dsb), (sc, dsc) = _silu_and_grad(bz), _silu_and_grad(cz)
        yb = obv * sb
        yc = oc * sc
        ps = [_mm(ya_ref[...], woa_ref[...]), _mm(yb, wob_ref[...]), _mm(yc, woc_ref[...])]
        mg_ref[...] = (gates[0] * ps[0] + gates[1] * ps[1] + gates[2] * ps[2]).astype(BF16)
        yb_ref[...] = yb.astype(BF16)
        yc_ref[...] = yc.astype(BF16)
        dm = _mm_nt(do_ref[...], wo_ref[...])
        dps = []
        first = pl.program_id(0) == 0
        for i, dref in enumerate((dpa_ref, dpb_ref, dpc_ref)):
            g = gates[i]
            dpi = (dm * g).astype(BF16)
            dref[...] = dpi
            dps.append(dpi)
            dgp = dm * ps[i] * g * (1.0 - g)
            st_gate[slot, :, i * D_MODEL:(i + 1) * D_MODEL] = dgp.astype(BF16)
            part = jnp.sum(dgp, axis=0, keepdims=True)

            @pl.when(first)
            def _():
                dbg_ref[:, i * D_MODEL:(i + 1) * D_MODEL] = part

            @pl.when(jnp.logical_not(first))
            def _():
                dbg_ref[:, i * D_MODEL:(i + 1) * D_MODEL] += part

        dya_ref[...] = _mm_nt(dps[0], woa_ref[...])
        dyb = _mm_nt(dps[1], wob_ref[...])
        dyc = _mm_nt(dps[2], woc_ref[...])
        st_bz[slot] = (dyb * obv * dsb).astype(BF16)
        st_cz[slot] = (dyc * oc * dsc).astype(BF16)
        for cp in copies_of(step):
            cp.start()
        dob = dyb * sb
        doc = dyc * sc
        dob_ref[...] = dob.astype(BF16)
        for c in range(NPAIR):
            cs = slice(c * LANE, (c + 1) * LANE)
            dlb_ref[:, cs] = _head_bcast_sum(dob[:, cs] * obv[:, cs])
            dd = _head_bcast_sum(doc[:, cs] * oc[:, cs])
            for a, dref, lref in zip(alphas, (dg0, dg1, dg2), (dl0, dl1, dl2)):
                dref[:, cs] = (a[:, cs] * doc[:, cs]).astype(BF16)
                lref[:, cs] = a[:, cs] * dd

        @pl.when(step == nt - 1)
        def _():
            if nt >= 2:
                for cp in copies_of(step - 1):
                    cp.wait()
            for cp in copies_of(step):
                cp.wait()

    def whole(r, c):
        return pl.BlockSpec((r, c), lambda i: (0, 0))

    tok = lambda w: pl.BlockSpec((ts, w), lambda i: (i, 0))
    sd = jax.ShapeDtypeStruct
    W = DIL_WIDTH
    return pl.pallas_call(
        body, name="merge_bwd", grid=(nt,),
        in_specs=[tok(D_MODEL)] + _merge_proj_specs(ts) + [whole(1, 3 * D_MODEL), tok(CONV_WIDTH)] + [tok(W)] * 7
                 + [whole(CONV_WIDTH, D_MODEL)] * 3 + [whole(D_MODEL, D_MODEL)],
        out_specs=[pl.BlockSpec(memory_space=pl.ANY), tok(CONV_WIDTH), tok(W), tok(W)] + [tok(W)] * 6
                  + [tok(D_MODEL)] * 4 + [tok(W), tok(W), whole(1, 3 * D_MODEL)],
        out_shape=[sd((T, PP), BF16), sd((T, CONV_WIDTH), F32), sd((T, W), BF16), sd((T, W), F32)]
                  + [sd((T, W), BF16)] * 3 + [sd((T, W), F32)] * 3
                  + [sd((T, D_MODEL), BF16)] * 4 + [sd((T, W), BF16)] * 2 + [sd((1, 3 * D_MODEL), F32)],
        scratch_shapes=[pltpu.VMEM((2, ts, W), BF16), pltpu.VMEM((2, ts, W), BF16),
                        pltpu.VMEM((2, ts, 3 * D_MODEL), BF16), pltpu.SemaphoreType.DMA((2, 3))],
        compiler_params=_cp(),
    )(dout, *[proj] * 5, b_gate, ya, ob, *ogs, *lses, woa, wob, woc, wo)


def _loss_head(y, target):
    T = y.shape[0]
    ts = _tile(T, 512)

    def body(y_ref, t_ref, d_ref, l_ref):
        e = y_ref[...] - t_ref[...]
        d_ref[...] = e * (1.0 / D_MODEL)
        l_ref[...] = jnp.zeros((1, 8, LANE), F32) + jnp.sum(e * e)

    tok = pl.BlockSpec((ts, D_MODEL), lambda i: (i, 0))
    return pl.pallas_call(
        body, name="loss_head", grid=(T // ts,), in_specs=[tok, tok],
        out_specs=[tok, pl.BlockSpec((1, 8, LANE), lambda i: (i, 0, 0))],
        out_shape=[jax.ShapeDtypeStruct((T, D_MODEL), F32), jax.ShapeDtypeStruct((T // ts, 8, LANE), F32)],
        compiler_params=_cp(),
    )(y, target)


def _my_index():
    return 4 * lax.axis_index("x") + 2 * lax.axis_index("y") + lax.axis_index("c")


def _peers():
    x, y, c = (lax.axis_index(a) for a in AXES)
    out = []
    for kk in range(1, N_DEV):
        px = 1 - x if kk & 4 else x
        py = 1 - y if kk & 2 else y
        pc = 1 - c if kk & 1 else c
        out.append(((px, py, pc), 4 * px + 2 * py + pc))
    return out


def _exchange(arrays, name, gather):
    n = len(arrays)

    def body(*refs):
        srcs, outs = refs[:n], refs[n:2 * n]
        send_sems, recv_sems, local_sems = refs[2 * n:]
        me = _my_index()
        peers = _peers()
        started = []
        for a, (src, out) in enumerate(zip(srcs, outs)):
            mine = pltpu.make_async_copy(src if gather else src.at[me], out.at[me], local_sems.at[a])
            mine.start()
            started.append(mine)
        sends = []
        for i, (pos, idx) in enumerate(peers):
            for a, (src, out) in enumerate(zip(srcs, outs)):
                cp = pltpu.make_async_remote_copy(
                    src_ref=src if gather else src.at[idx], dst_ref=out.at[me], send_sem=send_sems.at[a, i],
                    recv_sem=recv_sems.at[a, i], device_id=pos, device_id_type=pl.DeviceIdType.MESH)
                cp.start()
                sends.append(cp)
        for i, (pos, idx) in enumerate(peers):
            for a, (src, out) in enumerate(zip(srcs, outs)):
                pltpu.make_async_remote_copy(
                    src_ref=src if gather else src.at[idx], dst_ref=out.at[idx], send_sem=send_sems.at[a, i],
                    recv_sem=recv_sems.at[a, i], device_id=pos, device_id_type=pl.DeviceIdType.MESH).wait_recv()
        for cp in sends:
            cp.wait_send()
        for mine in started:
            mine.wait()

    any_space = pl.BlockSpec(memory_space=pl.ANY)
    return pl.pallas_call(
        body, name=name, in_specs=[any_space] * n, out_specs=[any_space] * n,
        out_shape=[jax.ShapeDtypeStruct(((N_DEV,) + a.shape) if gather else a.shape, a.dtype) for a in arrays],
        scratch_shapes=[pltpu.SemaphoreType.DMA((n, N_DEV - 1)), pltpu.SemaphoreType.DMA((n, N_DEV - 1)),
                        pltpu.SemaphoreType.DMA((n,))],
    )(*arrays)


N_CHIP = 4


def _chip_places():
    x, y, c = (lax.axis_index(a) for a in AXES)
    return (x, y, c), (x, y, 1 - c), [(1 - x, y, c), (x, 1 - y, c), (1 - x, 1 - y, c)]


def _index_of(pos):
    return 4 * pos[0] + 2 * pos[1] + pos[2]


def _sibling_swap(arrays, name):
    n = len(arrays)

    def body(*refs):
        srcs, outs = refs[:n], refs[n:2 * n]
        send_sems, recv_sems = refs[2 * n:]
        (x, y, c), sibling, _ = _chip_places()
        sends = []
        for a, (src, out) in enumerate(zip(srcs, outs)):
            for q in range(N_CHIP):
                def copy(core, a=a, q=q, src=src, out=out):
                    return pltpu.make_async_remote_copy(
                        src_ref=src.at[2 * q + core], dst_ref=out.at[q], send_sem=send_sems.at[N_CHIP * a + q],
                        recv_sem=recv_sems.at[N_CHIP * a + q], device_id=sibling, device_id_type=pl.DeviceIdType.MESH)
                mine = copy(1 - c)
                mine.start()
                sends.append((mine, copy(c)))
        for mine, arrival in sends:
            arrival.wait_recv()
            mine.wait_send()

    any_space = pl.BlockSpec(memory_space=pl.ANY)
    return pl.pallas_call(
        body, name=name, in_specs=[any_space] * n, out_specs=[any_space] * n,
        out_shape=[jax.ShapeDtypeStruct((N_CHIP,) + a.shape[1:], a.dtype) for a in arrays],
        scratch_shapes=[pltpu.SemaphoreType.DMA((N_CHIP * n,)), pltpu.SemaphoreType.DMA((N_CHIP * n,))],
    )(*arrays)


def _chip_pair_sum(part, got, name):
    R, C = part.shape[1:]
    tr = R
    while tr * C * part.dtype.itemsize > REDUCE_BLOCK_BYTES // 4 and tr % 32 == 0:
        tr //= 2
    c = lax.axis_index("c")

    def body(c_ref, p_ref, g_ref, o_ref):
        del c_ref
        o_ref[...] = (p_ref[...].astype(F32) + g_ref[...].astype(F32)).astype(o_ref.dtype)

    return pl.pallas_call(
        body, name=name, grid_spec=pltpu.PrefetchScalarGridSpec(
            num_scalar_prefetch=1, grid=(N_CHIP, R // tr),
            in_specs=[pl.BlockSpec((None, tr, C), lambda q, i, cr: (2 * q + cr[0], i, 0)),
                      pl.BlockSpec((None, tr, C), lambda q, i, cr: (q, i, 0))],
            out_specs=pl.BlockSpec((None, tr, C), lambda q, i, cr: (q, i, 0))),
        out_shape=jax.ShapeDtypeStruct((N_CHIP, R, C), part.dtype),
        compiler_params=_cp(),
    )(jnp.reshape(c, (1,)).astype(jnp.int32), part, got)


def _peer_count(mode):
    return {"chips": N_CHIP - 1, "near": N_CHIP}.get(mode, N_DEV - 1)


def _remote_copies(srcs, lands, send_sems, recv_sems, mode):
    if mode == "chips":
        (x, y, _), _, others = _chip_places()
        my_slot, peers = 2 * x + y, [(chip, 2 * chip[0] + chip[1]) for chip in others]
    elif mode == "near":
        me, sibling, others = _chip_places()
        my_slot, peers = _index_of(me), [(pos, _index_of(pos)) for pos in [sibling] + others]
    else:
        my_slot, peers = _my_index(), _peers()
    whole = mode in ("gather", "near")
    out = []
    for i, (pos, idx) in enumerate(peers):
        for a, (src, land) in enumerate(zip(srcs, lands)):
            def copy(slot, a=a, src=src, land=land, i=i, pos=pos, idx=idx):
                return pltpu.make_async_remote_copy(
                    src_ref=src if whole else src.at[idx], dst_ref=land.at[slot],
                    send_sem=send_sems.at[a * len(peers) + i], recv_sem=recv_sems.at[a * len(peers) + i],
                    device_id=pos, device_id_type=pl.DeviceIdType.MESH)
            out.append((copy(my_slot), copy(idx)))
    return out


def _exchange_start(arrays, name, mode):
    n = len(arrays)
    hbm = pl.BlockSpec(memory_space=pltpu.HBM)
    sem = pl.BlockSpec(memory_space=pltpu.SEMAPHORE)
    lands = [lax.empty(((N_DEV,) + a.shape) if mode in ("gather", "near") else a.shape, a.dtype) for a in arrays]

    def body(*refs):
        srcs, lands_ = refs[:n], refs[n:2 * n]
        send_sems, recv_sems = refs[2 * n:2 * n + 2]
        for mine, _ in _remote_copies(srcs, lands_, send_sems, recv_sems, mode):
            mine.start()
        refs[-1][...] = jnp.zeros_like(refs[-1])

    sems = pltpu.SemaphoreType.DMA((n * _peer_count(mode),))
    buffers = [pltpu.HBM(a.shape, a.dtype) for a in list(arrays) + lands]
    res = pl.pallas_call(
        body, name=name, in_specs=[hbm] * (2 * n), out_specs=[sem, sem] + [hbm] * (2 * n) + [pl.BlockSpec(memory_space=pltpu.VMEM)],
        out_shape=[sems, sems] + buffers + [jax.ShapeDtypeStruct((8, LANE), F32)],
        input_output_aliases={i: 2 + i for i in range(2 * n)},
        compiler_params=pltpu.CompilerParams(has_side_effects=pltpu.SideEffectType.DATAFLOW_SIDE_EFFECTING),
    )(*[pltpu.with_memory_space_constraint(a, pltpu.HBM) for a in list(arrays) + lands])
    return (res[0], res[1], res[2:2 + n], res[2 + n:2 + 2 * n]), res[-1]


def _exchange_wait(handle, after, name, mode):
    send_sems, recv_sems, srcs, lands = handle
    n = len(srcs)
    after = list(after) if isinstance(after, (list, tuple)) else [after]
    hbm = pl.BlockSpec(memory_space=pltpu.HBM)
    sem = pl.BlockSpec(memory_space=pltpu.SEMAPHORE)

    def body(*refs):
        for mine, arrival in _remote_copies(refs[:n], refs[n:2 * n], refs[2 * n], refs[2 * n + 1], mode):
            mine.wait_send()
            arrival.wait_recv()

    res = pl.pallas_call(
        body, name=name, in_specs=[hbm] * (2 * n) + [sem, sem] + [pl.BlockSpec(memory_space=pl.ANY)] * len(after),
        out_specs=[hbm] * (2 * n), out_shape=[pltpu.HBM(a.shape, a.dtype) for a in list(srcs) + list(lands)],
        input_output_aliases={i: i for i in range(2 * n)},
        compiler_params=pltpu.CompilerParams(has_side_effects=pltpu.SideEffectType.DATAFLOW_SIDE_EFFECTING),
    )(*srcs, *lands, send_sems, recv_sems, *after)
    return res[n:]


def _sibling_forward(lands, name):
    n = len(lands)

    def body(*refs):
        ins, outs, send_sems, recv_sems = refs[:n], refs[n:2 * n], refs[2 * n], refs[2 * n + 1]
        (x, y, c), sibling, others = _chip_places()
        copies = []
        for a, (src, out) in enumerate(zip(ins, outs)):
            for j, chip in enumerate(others):
                def copy(core, a=a, j=j, chip=chip, src=src, out=out):
                    slot = _index_of((chip[0], chip[1], core))
                    return pltpu.make_async_remote_copy(
                        src_ref=src.at[slot], dst_ref=out.at[slot], send_sem=send_sems.at[3 * a + j],
                        recv_sem=recv_sems.at[3 * a + j], device_id=sibling, device_id_type=pl.DeviceIdType.MESH)
                mine = copy(c)
                mine.start()
                copies.append((mine, copy(1 - c)))
        for mine, arrival in copies:
            arrival.wait_recv()
        for mine, arrival in copies:
            mine.wait_send()

    any_space = pl.BlockSpec(memory_space=pl.ANY)
    return pl.pallas_call(
        body, name=name, in_specs=[any_space] * n, out_specs=[any_space] * n,
        out_shape=[jax.ShapeDtypeStruct(a.shape, a.dtype) for a in lands],
        scratch_shapes=[pltpu.SemaphoreType.DMA((3 * n,)), pltpu.SemaphoreType.DMA((3 * n,))],
        input_output_aliases={i: i for i in range(n)},
    )(*lands)


def _own_slot(land, mine, slot=None):
    slot = _my_index() if slot is None else slot
    return lax.dynamic_update_slice(land, mine, (slot,) + (0,) * (land.ndim - 1))


def _adamw(w, g, m, v):
    m = ADAM_B1 * m + (1.0 - ADAM_B1) * g
    v = ADAM_B2 * v + (1.0 - ADAM_B2) * (g * g)
    m_hat = m / (1.0 - ADAM_B1 ** ADAM_STEP)
    v_hat = v / (1.0 - ADAM_B2 ** ADAM_STEP)
    delta = -ADAM_LR * (m_hat / (jnp.sqrt(v_hat) + ADAM_EPS) + ADAM_WD * w)
    return delta, m, v


def _reduce_adamw(parts, w, m, v, name, after=None):
    nparts = len(parts)
    R, C = parts[0].shape[1:]
    tr = R
    while N_DEV * tr * C * parts[0].dtype.itemsize > REDUCE_BLOCK_BYTES and tr % 32 == 0:
        tr //= 2
    steps = R // tr
    extra = [] if after is None else [after]

    def body(*refs):
        w_ref, m_ref, v_ref, g_ref, d_ref, nm_ref, nv_ref = refs[nparts + len(extra):]
        for k, p_ref in enumerate(refs[:nparts]):
            @pl.when(pl.program_id(0) // steps == k)
            def _():
                g = p_ref[0].astype(F32)
                for s in range(1, p_ref.shape[0]):
                    g = g + p_ref[s].astype(F32)
                g_ref[...] = g
                d_ref[...], nm_ref[...], nv_ref[...] = _adamw(w_ref[...], g, m_ref[...], v_ref[...])

    def part_spec(k):
        return pl.BlockSpec((parts[k].shape[0], tr, C), lambda i: (0, jnp.clip(i - k * steps, 0, steps - 1), 0))

    row = pl.BlockSpec((tr, C), lambda i: (i, 0))
    return pl.pallas_call(
        body, name=name, grid=(nparts * steps,),
        in_specs=[part_spec(k) for k in range(nparts)] + [pl.BlockSpec(memory_space=pl.ANY)] * len(extra)
                 + [row, row, row],
        out_specs=[row] * 4, out_shape=[jax.ShapeDtypeStruct((nparts * R, C), F32)] * 4,
        compiler_params=_cp(),
    )(*parts, *extra, w, m, v)


BIG = ("w_in", "w_uq", "w_ukv", "w_out_a", "w_out_b", "w_out_c", "w_o")
SMALL = ("norm_g", "b_gate", "conv_w", "conv_b", "q_a_norm_g", "kv_a_norm_g", "mla_q_norm_g", "mla_k_norm_g",
         "dil_q_norm_g", "dil_k_norm_g")
PACK_ROWS = 128
REDUCE_BLOCK_BYTES = 6 * 1024 * 1024


def _pack_local(tensors):
    flat = jnp.concatenate([t.reshape(-1) for t in tensors])
    pad = (-flat.shape[0]) % (PACK_ROWS * LANE)
    return jnp.concatenate([flat, jnp.zeros((pad,), flat.dtype)]).reshape(-1, LANE)


def _unpack_local(rows, like):
    flat = rows.reshape(-1)
    out, off = [], 0
    for t in like:
        out.append(flat[off:off + t.size].reshape(t.shape))
        off += t.size
    return out


def _cols_to_slots(a):
    k = a.shape[0]
    return a.reshape(k, N_DEV, -1).transpose(1, 0, 2)


def _slots_to_cols(s):
    return s.transpose(1, 0, 2).reshape(s.shape[1], -1)


def _rope_tables(S):
    inv = ROPE_THETA ** (-jnp.arange(0, MLA_ROPE, 2, dtype=F32) / MLA_ROPE)
    ang = jnp.arange(S, dtype=F32)[:, None] * inv[None, :]
    cos, sin = jnp.cos(ang), jnp.sin(ang)
    one = jnp.ones((S, MLA_NOPE), F32)
    z16, z32, z64 = (jnp.zeros((S, n), F32) for n in (16, 32, 64))
    cosp = jnp.concatenate([one, cos, cos, jnp.ones((S, 32), F32)], axis=1)
    sa = jnp.concatenate([z64, -sin, z16, z32], axis=1)
    sb = jnp.concatenate([z64, z16, sin, z32], axis=1)
    return cosp, sa, sb


def _alibi_slopes():
    n = DIL_GROUPS * DIL_HEADS
    m = 2.0 ** (-8.0 * jnp.arange(1, n + 1, dtype=F32) / n)
    return m.reshape(DIL_GROUPS, NPAIR, 2)


def _pad_slots(s):
    n, k, c = s.shape
    return _slots_to_cols(jnp.concatenate([s, jnp.zeros((n, k, LANE - c), s.dtype)], axis=2))


def _layer_params(gw, small, l):
    p = {}
    p["wp"] = _pad_columns(gw["w_in"])
    p["norm_g"] = small["norm_g"][l][None]
    p["b_gate"] = small["b_gate"][l][None]
    p["conv_w"] = gw["conv_w"].transpose(1, 0, 2).reshape(CONV_K, CONV_WIDTH)
    p["conv_b"] = small["conv_b"][l][None]
    p["gq"] = small["q_a_norm_g"][l][None]
    p["gkv"] = small["kv_a_norm_g"][l][None]
    p["wuqp"] = _pad_slots(gw["w_uq"])
    kv = gw["w_ukv"]
    p["wkp"] = _pad_slots(kv[:, :, :MLA_NOPE])
    p["wv"] = kv[:, :, MLA_NOPE:].transpose(1, 0, 2).reshape(MLA_KV_LORA, MLA_HEADS * MLA_V)
    zpad = jnp.zeros((1, LANE - MLA_QK), F32)
    p["gmq"] = jnp.concatenate([small["mla_q_norm_g"][l][None], zpad], axis=1)
    p["gmk"] = jnp.concatenate([small["mla_k_norm_g"][l][None], zpad], axis=1)
    tile = lambda g: jnp.broadcast_to(g[:, None, :], (DIL_GROUPS, DIL_HEADS, DIL_HEAD_DIM)).reshape(1, DIL_QK)
    p["gdq"] = tile(small["dil_q_norm_g"][l])
    p["gdk"] = tile(small["dil_k_norm_g"][l])
    p["woa"], p["wob"], p["woc"] = (_slots_to_cols(gw[n]) for n in ("w_out_a", "w_out_b", "w_out_c"))
    p["wo"] = gw["w_o"].reshape(D_MODEL, D_MODEL)
    return p


def _layer_fwd(x, p, tabs, slopes, B, S):
    proj, ht = _inproj_fwd(x, p["norm_g"], p["wp"])
    ya = _mixa_fwd(proj, p["conv_w"], p["conv_b"], B, S)
    q, k, v = _mla_prep_fwd(proj, p["gq"], p["gkv"], p["wuqp"], p["wkp"], p["wv"], p["gmq"], p["gmk"], *tabs, S)
    ob, lse_b = _mla_attn_fwd(q, k, v, B, S)
    qn, kn = _dil_prep_fwd(proj, p["gdq"], p["gdk"])
    ogs, lses = [], []
    for gi in range(DIL_GROUPS):
        o, lse = _dil_attn_fwd(gi, slopes[gi], qn, kn, proj, B, S)
        ogs.append(o)
        lses.append(lse)
    out = _merge_fwd(x, proj, p["b_gate"], ya, ob, ogs, lses, p["woa"], p["wob"], p["woc"], p["wo"])
    saved = dict(x=x, proj=proj, ht=ht, ya=ya, q=q, k=k, v=v, ob=ob, lse_b=lse_b, qn=qn, kn=kn, ogs=ogs, lses=lses)
    return out, saved


def _layer_bwd(dout, sv, p, tabs, slopes, B, S, big_ready=None):
    proj = sv["proj"]
    (dproj, dya, dob, dlb, dg0, dg1, dg2, dl0, dl1, dl2, merged, dpa, dpb, dpc, yb, yc, dbg) = _merge_bwd(
        dout, proj, p["b_gate"], sv["ya"], sv["ob"], sv["ogs"], sv["lses"], p["woa"], p["wob"], p["woc"], p["wo"])
    g = {}
    g["w_o"] = _matmul_tn(merged, dout, "dw_o").reshape(N_DEV, D_MODEL // N_DEV, D_MODEL)
    g["w_out_a"] = _cols_to_slots(_matmul_tn(sv["ya"], dpa, "dw_out_a"))
    g["w_out_b"] = _cols_to_slots(_matmul_tn(yb, dpb, "dw_out_b"))
    g["w_out_c"] = _cols_to_slots(_matmul_tn(yc, dpc, "dw_out_c"))
    g["b_gate"] = dbg[0]
    dproj, st = _mixa_bwd(dproj, dya, proj, p["conv_w"], p["conv_b"], B, S)
    g["conv_w"] = st[0:CONV_K]
    g["conv_b"] = st[CONV_K]
    dq, dk, dv = _mla_attn_bwd(sv["q"], sv["k"], sv["v"], dob, sv["lse_b"], dlb, B, S)
    dproj, dwuqp, dwkp, dwv, dgq, dgkv, dgmq, dgmk = _mla_prep_bwd(
        dproj, dq, dk, dv, proj, p["gq"], p["gkv"], p["wuqp"], p["wkp"], p["wv"], p["gmq"], p["gmk"], *tabs, S)
    g["w_uq"] = _cols_to_slots(dwuqp)[:, :, :MLA_QK]
    g["w_ukv"] = jnp.concatenate([_cols_to_slots(dwkp)[:, :, :MLA_NOPE], _cols_to_slots(dwv)], axis=2)
    g["q_a_norm_g"], g["kv_a_norm_g"] = dgq[0], dgkv[0]
    g["mla_q_norm_g"], g["mla_k_norm_g"] = dgmq[0, :MLA_QK], dgmk[0, :MLA_QK]
    dqkv = None
    for gi, (dog, dlg) in enumerate(((dg0, dl0), (dg1, dl1), (dg2, dl2))):
        dqkv = _dil_attn_bwd(gi, slopes[gi], sv["qn"], sv["kn"], proj, dog, sv["lses"][gi], dlg, dqkv, B, S)
    dproj, dgdq, dgdk = _dil_prep_bwd(dproj, *dqkv, proj, p["gdq"], p["gdk"])
    g["dil_q_norm_g"] = dgdq.reshape(DIL_GROUPS, DIL_HEADS, DIL_HEAD_DIM).sum(axis=1)
    g["dil_k_norm_g"] = dgdk.reshape(DIL_GROUPS, DIL_HEADS, DIL_HEAD_DIM).sum(axis=1)
    g["w_in"] = _unpad_columns(_matmul_nn(sv["ht"], dproj, "dw_in"))
    token = None if big_ready is None else big_ready(g)
    dx, dng = _inproj_bwd_x(dproj, p["wp"], sv["x"], _after(token, p["norm_g"]), dout)
    g["norm_g"] = dng[0]
    return dx, g


def _after(token, a):
    return a if token is None else a + token[0:1, 0:1]


def _local_step(x, target, small, B, S, weights_of, grads_out, big_ready=None):
    tabs = _rope_tables(S)
    sl = _alibi_slopes()
    slopes = [sl[gi] * float(DIL_PATTERNS[gi][1]) for gi in range(DIL_GROUPS)]
    params, saved = [], []
    for l in range(DEPTH):
        gw, token = weights_of(l, x)
        p = _layer_params(gw, small, l)
        p["norm_g"] = _after(token, p["norm_g"])
        x, sv = _layer_fwd(x, p, tabs, slopes, B, S)
        params.append(p)
        saved.append(sv)
    dout, lparts = _loss_head(x, target)
    sq = jnp.sum(lparts[:, 0, 0])
    token = None
    for l in reversed(range(DEPTH)):
        p = dict(params[l], b_gate=_after(token, params[l]["b_gate"]))
        ready = None if big_ready is None else (lambda g, l=l: big_ready(l, g))
        dout, g = _layer_bwd(dout, saved[l], p, tabs, slopes, B, S, ready)
        token = grads_out(l, g, dout)
    return sq, dout


def kernel(x, norm_g, w_in, b_gate, conv_w, conv_b, q_a_norm_g, w_uq, kv_a_norm_g, w_ukv, mla_q_norm_g, mla_k_norm_g, dil_q_norm_g, dil_k_norm_g, w_out_a, w_out_b, w_out_c, w_o, loss_target, m_norm_g, m_w_in, m_b_gate, m_conv_w, m_conv_b, m_q_a_norm_g, m_w_uq, m_kv_a_norm_g, m_w_ukv, m_mla_q_norm_g, m_mla_k_norm_g, m_dil_q_norm_g, m_dil_k_norm_g, m_w_out_a, m_w_out_b, m_w_out_c, m_w_o, v_norm_g, v_w_in, v_b_gate, v_conv_w, v_conv_b, v_q_a_norm_g, v_w_uq, v_kv_a_norm_g, v_w_ukv, v_mla_q_norm_g, v_mla_k_norm_g, v_dil_q_norm_g, v_dil_k_norm_g, v_w_out_a, v_w_out_b, v_w_out_c, v_w_o):
    names = ("norm_g", "w_in", "b_gate", "conv_w", "conv_b", "q_a_norm_g", "w_uq", "kv_a_norm_g", "w_ukv",
             "mla_q_norm_g", "mla_k_norm_g", "dil_q_norm_g", "dil_k_norm_g", "w_out_a", "w_out_b", "w_out_c", "w_o")
    w = dict(zip(names, (norm_g, w_in, b_gate, conv_w, conv_b, q_a_norm_g, w_uq, kv_a_norm_g, w_ukv, mla_q_norm_g,
                         mla_k_norm_g, dil_q_norm_g, dil_k_norm_g, w_out_a, w_out_b, w_out_c, w_o)))
    m = dict(zip(names, (m_norm_g, m_w_in, m_b_gate, m_conv_w, m_conv_b, m_q_a_norm_g, m_w_uq, m_kv_a_norm_g, m_w_ukv,
                         m_mla_q_norm_g, m_mla_k_norm_g, m_dil_q_norm_g, m_dil_k_norm_g, m_w_out_a, m_w_out_b,
                         m_w_out_c, m_w_o)))
    v = dict(zip(names, (v_norm_g, v_w_in, v_b_gate, v_conv_w, v_conv_b, v_q_a_norm_g, v_w_uq, v_kv_a_norm_g, v_w_ukv,
                         v_mla_q_norm_g, v_mla_k_norm_g, v_dil_q_norm_g, v_dil_k_norm_g, v_w_out_a, v_w_out_b,
                         v_w_out_c, v_w_o)))
    B, S, _ = x.shape
    me = _my_index()
    cshard = CONV_WIDTH // N_DEV

    shards = [[w[n][0].astype(BF16) for n in BIG]]
    state = {}

    def widen(t):
        return lax.dynamic_update_slice(jnp.zeros((DEPTH, CONV_K, CONV_WIDTH), F32), t, (0, 0, me * cshard))

    pick = lambda d: [widen(d[n]) if n == "conv_w" else d[n] for n in SMALL]

    def weights_of(l, after):
        if l == 0:
            first = shards[0] + [conv_w]
            handle, token = _exchange_start(first, "all_gather_weights_0_start", "near")
            zero = token[0:1, 0:1]
            state["shards1"] = [(w[n][1] + zero).astype(BF16) for n in BIG]
            for n in BIG:
                state["rows", n] = [a.reshape(-1, a.shape[-1]) + zero for a in (w[n], m[n], v[n])]
            state["small"] = [_pack_local(pick(d)) + zero for d in (w, m, v)]
            busy = state["shards1"] + [a for n in BIG for a in state["rows", n]] + state["small"]
            landed = _exchange_wait(handle, busy, "all_gather_weights_0_wait", "near")
            landed = _sibling_forward(landed, "all_gather_weights_0_forward")
            got = [_own_slot(a, s[None]) for a, s in zip(landed, first)]
            state["gather"], token = _exchange_start(state["shards1"], "all_gather_weights_1_start", "gather")
            state["conv_w"] = got[-1]
        else:
            landed = _exchange_wait(state["gather"], after, "all_gather_weights_1_wait", "gather")
            got, token = [_own_slot(a, s[None]) for a, s in zip(landed, state["shards1"])], None
        gw = dict(zip(BIG, got))
        gw["conv_w"] = state["conv_w"][:, l]
        return gw, token

    recv, small_parts = {}, {}
    my_chip = 2 * lax.axis_index("x") + lax.axis_index("y")

    def big_ready(l, g):
        send = [g[n].astype(BF16) for n in BIG]
        if l == DEPTH - 1:
            state["scatter"], token = _exchange_start(send, "exchange_weight_grads_1_start", "scatter")
        else:
            swapped = _sibling_swap(send, "exchange_weight_grads_0_sibling")
            send = [_chip_pair_sum(s, t, "chip_pair_sum_" + n) for n, s, t in zip(BIG, send, swapped)]
            state["chips"], token = _exchange_start(send, "exchange_weight_grads_0_start", "chips")
        state["sent", l] = send
        return token

    def grads_out(l, g, after):
        small_parts[l] = [g[n] for n in SMALL]
        if l == DEPTH - 1:
            return None
        for k, key, mode, slot in ((DEPTH - 1, "scatter", "scatter", me), (0, "chips", "chips", my_chip)):
            landed = _exchange_wait(state[key], after, f"exchange_weight_grads_{k}_wait", mode)
            mine = [lax.dynamic_slice_in_dim(s, slot, 1, axis=0) for s in state["sent", k]]
            recv[k] = [_own_slot(a, s, slot) for a, s in zip(landed, mine)]
        return None

    sq, grad_x = _local_step(x.reshape(B * S, D_MODEL), loss_target.reshape(B * S, D_MODEL), w, B, S,
                             weights_of, grads_out, big_ready)
    loss = lax.psum(sq * (0.5 / D_MODEL), AXES)

    part = {n: jnp.stack([small_parts[l][i] for l in range(DEPTH)]) for i, n in enumerate(SMALL)}
    small_like = [part[n] for n in SMALL]
    pack = _pack_local(small_like)
    handle, token = _exchange_start([pack], "all_gather_small_grads_start", "gather")

    res, done = {}, []
    for i, n in enumerate(BIG):
        outs = _reduce_adamw([recv[l][i] for l in range(DEPTH)], *state["rows", n], "reduce_adamw_" + n, token)
        res[n] = tuple(a.reshape(w[n].shape) for a in outs)
        done.append(outs[0])

    landed, = _exchange_wait(handle, done, "all_gather_small_grads_wait", "gather")
    parts = _own_slot(landed, pack[None])
    gs, ds, ms, vs = _reduce_adamw([parts], *state["small"], "reduce_adamw_small")
    for n, t in zip(SMALL, zip(*(_unpack_local(a, small_like) for a in (gs, ds, ms, vs)))):
        if n == "conv_w":
            t = tuple(lax.dynamic_slice(a, (0, 0, me * cshard), (DEPTH, CONV_K, cshard)) for a in t)
        res[n] = t

    out = [loss, grad_x.reshape(B, S, D_MODEL)]
    for i in range(4):
        out += [res[n][i] for n in names]
    return tuple(out)
```

```python
import jax
import jax.numpy as jnp
from jax import lax
from jax.experimental import pallas as pl
from jax.experimental.pallas import tpu as pltpu

F32 = jnp.float32
BF16 = jnp.bfloat16

D_MODEL = 1024
DEPTH = 2
CONV_WIDTH = 512
CONV_K = 3
MLA_HEADS = 8
MLA_Q_LORA = 256
MLA_KV_LORA = 128
MLA_NOPE = 64
MLA_ROPE = 32
MLA_V = 64
MLA_QK = MLA_NOPE + MLA_ROPE
ROPE_THETA = 10000.0
DIL_PATTERNS = ((128, 1), (512, 4), (2048, 16))
DIL_GROUPS = 3
DIL_HEADS = 8
DIL_HEAD_DIM = 64
DIL_WIDTH = DIL_HEADS * DIL_HEAD_DIM
DIL_QK = DIL_GROUPS * DIL_WIDTH
EPS = 1e-6
N_IN = 11168

ADAM_LR = 0.001
ADAM_B1 = 0.9
ADAM_B2 = 0.999
ADAM_EPS = 1e-08
ADAM_WD = 0.01
ADAM_STEP = 10

N_DEV = 8
AXES = ("x", "y", "c")
LANE = 128
HALF = 64
NPAIR = 4

CB_AB, CB_AC, CB_AX, CB_AZ = 0, 4, 8, 12
CB_CQ, CB_CKV, CB_KPE = 16, 18, 19
CB_BZ = 20
CB_DQ, CB_DK, CB_DV = 24, 36, 48
CB_CZ, CB_GATE = 60, 64
NCB = 88
PP = NCB * LANE
KPE_END = CB_KPE * LANE + MLA_ROPE
SHARD_COLS = N_IN // N_DEV
NEG = -1e30
VMEM_LIMIT = 56 * 1024 * 1024


def _pad_columns(shards):
    parts = []
    for p in range(N_DEV):
        cut = min(max(KPE_END - p * SHARD_COLS, 0), SHARD_COLS)
        if 0 < cut < SHARD_COLS:
            parts += [shards[p, :, :cut], jnp.zeros((shards.shape[1], LANE - MLA_ROPE), shards.dtype), shards[p, :, cut:]]
        else:
            parts.append(shards[p])
    return jnp.concatenate(parts, axis=1)


def _unpad_columns(wp):
    def columns(a, b):
        gap = LANE - MLA_ROPE
        if b <= KPE_END:
            return wp[:, a:b]
        if a >= KPE_END:
            return wp[:, a + gap:b + gap]
        return jnp.concatenate([wp[:, a:KPE_END], wp[:, KPE_END + gap:b + gap]], axis=1)

    return jnp.stack([columns(p * SHARD_COLS, (p + 1) * SHARD_COLS) for p in range(N_DEV)])


def _put_copies(stages, dst_ref, sems, slot, rows, cols):
    return [pltpu.make_async_copy(st.at[slot], dst_ref.at[rows, pl.ds(c0, st.shape[-1])], sems.at[slot, k])
            for k, (st, c0) in enumerate(zip(stages, cols))]


def _put_pipeline(step, nsteps, copies_of, fill):
    @pl.when(step >= 2)
    def _():
        for cp in copies_of(step - 2):
            cp.wait()

    fill(step % 2)
    for cp in copies_of(step):
        cp.start()

    @pl.when(step == nsteps - 1)
    def _():
        if nsteps >= 2:
            for cp in copies_of(step - 1):
                cp.wait()
        for cp in copies_of(step):
            cp.wait()


def _cp():
    return pltpu.CompilerParams(vmem_limit_bytes=VMEM_LIMIT)


def _rstd(x, n):
    return lax.rsqrt(jnp.sum(x * x, axis=-1, keepdims=True) * (1.0 / n) + EPS)


def _sigmoid(z):
    return 1.0 / (1.0 + jnp.exp(-z))


def _silu(z):
    return z * _sigmoid(z)


def _silu_and_grad(z):
    s = _sigmoid(z)
    return z * s, s * (1.0 + z * (1.0 - s))


def _mm(a, b):
    return jnp.dot(a.astype(BF16), b.astype(BF16), preferred_element_type=F32)


def _mm_nt(a, b):
    return lax.dot_general(a.astype(BF16), b.astype(BF16), (((1,), (1,)), ((), ())), preferred_element_type=F32)


def _mm_tn(a, b):
    return lax.dot_general(a.astype(BF16), b.astype(BF16), (((0,), (0,)), ((), ())), preferred_element_type=F32)


def _lane_lo(shape):
    return lax.broadcasted_iota(jnp.int32, shape, len(shape) - 1) < HALF


def _head_bcast_sum(x, terms=3):
    w = x.shape[-1]
    same = (lax.broadcasted_iota(jnp.int32, (w, w), 0) // HALF) == (lax.broadcasted_iota(jnp.int32, (w, w), 1) // HALF)
    ones = jnp.where(same, 1.0, 0.0).astype(jnp.bfloat16)
    total = None
    for _ in range(terms):
        term = x.astype(jnp.bfloat16)
        x = x - term.astype(F32)
        part = jnp.dot(term, ones, preferred_element_type=F32)
        total = part if total is None else total + part
    return total


def _rope(t, cos, sa, sb):
    return t * cos + pltpu.roll(t, LANE - 16, axis=1) * sa + pltpu.roll(t, 16, axis=1) * sb


def _rope_t(d, cos, sa, sb):
    return d * cos + pltpu.roll(d * sa, 16, axis=1) + pltpu.roll(d * sb, LANE - 16, axis=1)


def _shift_down(u, k):
    rows = lax.broadcasted_iota(jnp.int32, u.shape, 0)
    return jnp.where(rows >= k, pltpu.roll(u, k, axis=0), 0.0)


def _shift_up(u, k):
    n = u.shape[0]
    rows = lax.broadcasted_iota(jnp.int32, u.shape, 0)
    return jnp.where(rows < n - k, pltpu.roll(u, n - k, axis=0), 0.0)


def _tile(n, want):
    t = min(n, want)
    assert n % t == 0, (n, want)
    return t


def _inproj_fwd(x, g, wp):
    T = x.shape[0]
    tm, tn = _tile(T, 2048), 512

    def body(x_ref, g_ref, w_ref, proj_ref, ht_ref, h_ref):
        @pl.when(pl.program_id(1) == 0)
        def _():
            n = min(tm, 512)
            for r0 in range(0, tm, n):
                xv = x_ref[r0:r0 + n, :]
                h = xv * _rstd(xv, D_MODEL) * g_ref[...]
                h_ref[r0:r0 + n, :] = h.astype(BF16)
                ht_ref[:, r0:r0 + n] = h.T.astype(BF16)

        proj_ref[...] = jnp.dot(h_ref[...], w_ref[...], preferred_element_type=F32).astype(BF16)

    return pl.pallas_call(
        body, name="inproj_fwd", grid=(T // tm, PP // tn),
        in_specs=[pl.BlockSpec((tm, D_MODEL), lambda i, j: (i, 0)),
                  pl.BlockSpec((1, D_MODEL), lambda i, j: (0, 0)),
                  pl.BlockSpec((D_MODEL, tn), lambda i, j: (0, j))],
        out_specs=[pl.BlockSpec((tm, tn), lambda i, j: (i, j)),
                   pl.BlockSpec((D_MODEL, tm), lambda i, j: (0, i))],
        out_shape=[jax.ShapeDtypeStruct((T, PP), BF16), jax.ShapeDtypeStruct((D_MODEL, T), BF16)],
        scratch_shapes=[pltpu.VMEM((tm, D_MODEL), BF16)],
        compiler_params=_cp(),
    )(x, g, wp)


def _matmul_nn(at, b, name):
    K, T = at.shape
    N = b.shape[1]
    tt, tn = _tile(T, 1024), _tile(N, 2816)
    nk = T // tt

    def body(a_ref, b_ref, o_ref, acc_ref):
        k = pl.program_id(1)

        @pl.when(k == 0)
        def _():
            acc_ref[...] = jnp.zeros_like(acc_ref)

        acc_ref[...] += jnp.dot(a_ref[...], b_ref[...], preferred_element_type=F32)

        @pl.when(k == nk - 1)
        def _():
            o_ref[...] = acc_ref[...].astype(BF16)

    return pl.pallas_call(
        body, name=name, grid=(N // tn, nk),
        in_specs=[pl.BlockSpec((K, tt), lambda j, k: (0, k)),
                  pl.BlockSpec((tt, tn), lambda j, k: (k, j))],
        out_specs=pl.BlockSpec((K, tn), lambda j, k: (0, j)),
        out_shape=jax.ShapeDtypeStruct((K, N), BF16),
        scratch_shapes=[pltpu.VMEM((K, tn), F32)],
        compiler_params=_cp(),
    )(at, b)


def _matmul_tn(a, b, name):
    T, K = a.shape
    N = b.shape[1]
    tt, tn = _tile(T, 512), _tile(N, 1024)

    def body(a_ref, b_ref, o_ref):
        @pl.when(pl.program_id(1) == 0)
        def _():
            o_ref[...] = jnp.zeros_like(o_ref)

        o_ref[...] += _mm_tn(a_ref[...], b_ref[...])

    return pl.pallas_call(
        body, name=name, grid=(N // tn, T // tt),
        in_specs=[pl.BlockSpec((tt, K), lambda j, k: (k, 0)),
                  pl.BlockSpec((tt, tn), lambda j, k: (k, j))],
        out_specs=pl.BlockSpec((K, tn), lambda j, k: (0, j)),
        out_shape=jax.ShapeDtypeStruct((K, N), F32),
        compiler_params=_cp(),
    )(a, b)


def _inproj_bwd_x(dproj, wp, x, g, dout):
    T = x.shape[0]
    tm, tk = _tile(T, 1024), 1024
    nk = PP // tk

    def body(dp_ref, w_ref, x_ref, g_ref, do_ref, dx_ref, dg_ref, acc_ref):
        i, k = pl.program_id(0), pl.program_id(1)

        @pl.when(k == 0)
        def _():
            acc_ref[...] = jnp.zeros_like(acc_ref)

        @pl.when((k == 0) & (i == 0))
        def _():
            dg_ref[...] = jnp.zeros_like(dg_ref)

        acc_ref[...] += _mm_nt(dp_ref[...], w_ref[...])

        @pl.when(k == nk - 1)
        def _():
            dh = acc_ref[...]
            xv = x_ref[...]
            r = _rstd(xv, D_MODEL)
            gy = dh * g_ref[...]
            dot = jnp.sum(xv * gy, axis=-1, keepdims=True) * (1.0 / D_MODEL)
            dx_ref[...] = do_ref[...] + r * gy - xv * (r * r * r) * dot
            dg_ref[...] += jnp.sum(dh * xv * r, axis=0, keepdims=True)

    return pl.pallas_call(
        body, name="inproj_bwd_x", grid=(T // tm, nk),
        in_specs=[pl.BlockSpec((tm, tk), lambda i, k: (i, k)),
                  pl.BlockSpec((D_MODEL, tk), lambda i, k: (0, k)),
                  pl.BlockSpec((tm, D_MODEL), lambda i, k: (i, 0)),
                  pl.BlockSpec((1, D_MODEL), lambda i, k: (0, 0)),
                  pl.BlockSpec((tm, D_MODEL), lambda i, k: (i, 0))],
        out_specs=[pl.BlockSpec((tm, D_MODEL), lambda i, k: (i, 0)),
                   pl.BlockSpec((1, D_MODEL), lambda i, k: (0, 0))],
        out_shape=[jax.ShapeDtypeStruct((T, D_MODEL), F32), jax.ShapeDtypeStruct((1, D_MODEL), F32)],
        scratch_shapes=[pltpu.VMEM((tm, D_MODEL), F32)],
        compiler_params=_cp(),
    )(dproj, wp, x, g, dout)


A_SEGS = (CB_AB, CB_AC, CB_AX, CB_AZ)


def _mixa_fwd(proj, cw, cb, B, S):
    nc = CONV_WIDTH // LANE

    def body(ab_ref, ac_ref, ax_ref, az_ref, cw_ref, cb_ref, y_ref):
        ab, ac, ax, az = (r[...].astype(F32) for r in (ab_ref, ac_ref, ax_ref, az_ref))
        u = ac * ax
        conv = cb_ref[...] + cw_ref[0:1, :] * _shift_down(u, 2) + cw_ref[1:2, :] * _shift_down(u, 1) + cw_ref[2:3, :] * u
        y_ref[...] = (ab * conv * _silu(az)).astype(BF16)

    return pl.pallas_call(
        body, name="mixa_fwd", grid=(B, nc),
        in_specs=[pl.BlockSpec((S, LANE), lambda b, j, c0=c0: (b, c0 + j)) for c0 in A_SEGS]
                 + [pl.BlockSpec((CONV_K, LANE), lambda b, j: (0, j)),
                    pl.BlockSpec((1, LANE), lambda b, j: (0, j))],
        out_specs=pl.BlockSpec((S, LANE), lambda b, j: (b, j)),
        out_shape=jax.ShapeDtypeStruct((B * S, CONV_WIDTH), BF16),
        compiler_params=_cp(),
    )(proj, proj, proj, proj, cw, cb)


def _mixa_bwd(dproj, dy, proj, cw, cb, B, S):
    nc = CONV_WIDTH // LANE

    def body(dpin_ref, dy_ref, ab_ref, ac_ref, ax_ref, az_ref, cw_ref, cb_ref, dp_ref, st_ref, stage, sems):
        del dpin_ref
        j, b = pl.program_id(0), pl.program_id(1)
        ab, ac, ax, az = (r[...].astype(F32) for r in (ab_ref, ac_ref, ax_ref, az_ref))
        u = ac * ax
        u1, u2 = _shift_down(u, 1), _shift_down(u, 2)
        w0, w1, w2 = cw_ref[0:1, :], cw_ref[1:2, :], cw_ref[2:3, :]
        conv = cb_ref[...] + w0 * u2 + w1 * u1 + w2 * u
        s, ds_az = _silu_and_grad(az)
        d = dy_ref[...]
        dconv = d * ab * s
        du = w2 * dconv + w1 * _shift_up(dconv, 1) + w0 * _shift_up(dconv, 2)
        grads = (d * conv * s, du * ax, du * ac, d * ab * conv * ds_az)

        def fill(slot):
            for k, v in enumerate(grads):
                stage[slot, k] = v.astype(BF16)

        def copies_of(step):
            sj, sb = step // B, step % B
            return _put_copies([stage.at[:, k] for k in range(4)], dp_ref, sems, step % 2,
                               pl.ds(pl.multiple_of(sb * S, S), S),
                               [pl.multiple_of((c0 + sj) * LANE, LANE) for c0 in A_SEGS])

        _put_pipeline(j * B + b, nc * B, copies_of, fill)
        row = lax.broadcasted_iota(jnp.int32, (8, LANE), 0)
        st = jnp.zeros((8, LANE), F32)
        for r, v in enumerate((dconv * u2, dconv * u1, dconv * u, dconv)):
            st = st + jnp.where(row == r, jnp.sum(v, axis=0, keepdims=True), 0.0)

        @pl.when(pl.program_id(1) == 0)
        def _():
            st_ref[...] = st

        @pl.when(pl.program_id(1) != 0)
        def _():
            st_ref[...] += st

    return pl.pallas_call(
        body, name="mixa_bwd", grid=(nc, B),
        in_specs=[pl.BlockSpec(memory_space=pl.ANY),
                  pl.BlockSpec((S, LANE), lambda j, b: (b, j))]
                 + [pl.BlockSpec((S, LANE), lambda j, b, c0=c0: (b, c0 + j)) for c0 in A_SEGS]
                 + [pl.BlockSpec((CONV_K, LANE), lambda j, b: (0, j)),
                    pl.BlockSpec((1, LANE), lambda j, b: (0, j))],
        out_specs=[pl.BlockSpec(memory_space=pl.ANY),
                   pl.BlockSpec((8, LANE), lambda j, b: (0, j))],
        out_shape=[jax.ShapeDtypeStruct(dproj.shape, BF16), jax.ShapeDtypeStruct((8, CONV_WIDTH), F32)],
        scratch_shapes=[pltpu.VMEM((2, 4, S, LANE), BF16), pltpu.SemaphoreType.DMA((2, 4))],
        input_output_aliases={0: 0},
        compiler_params=_cp(),
    )(dproj, dy, proj, proj, proj, proj, cw, cb)


def _mla_prep_fwd(proj, gq, gkv, wuqp, wkp, wv, gmq, gmk, cos, sa, sb, S):
    T = proj.shape[0]
    ts = _tile(S, 512)
    ns = S // ts
    W = MLA_HEADS * LANE

    def body(p_ref, gq_ref, gkv_ref, wuq_ref, wk_ref, wv_ref, gmq_ref, gmk_ref, cos_ref, sa_ref, sb_ref,
             q_ref, k_ref, v_ref):
        cq = p_ref[:, 0:2 * LANE].astype(F32)
        ckv = p_ref[:, 2 * LANE:3 * LANE].astype(F32)
        kpe = pltpu.roll(p_ref[:, 3 * LANE:4 * LANE].astype(F32), HALF, axis=1)
        cqn = cq * _rstd(cq, MLA_Q_LORA) * gq_ref[...]
        ckn = (ckv * _rstd(ckv, MLA_KV_LORA) * gkv_ref[...]).astype(BF16)
        q0 = _mm(cqn, wuq_ref[...])
        kn = _mm(ckn, wk_ref[...])
        v_ref[...] = _mm(ckn, wv_ref[...]).astype(BF16)
        c, a, b = cos_ref[...], sa_ref[...], sb_ref[...]
        kpe_rot = _rope(kpe * gmk_ref[...], c, a, b)
        for h in range(MLA_HEADS):
            q0h = q0[:, h * LANE:(h + 1) * LANE]
            q_ref[h] = _rope(q0h * _rstd(q0h, MLA_QK) * gmq_ref[...], c, a, b).astype(BF16)
            knh = kn[:, h * LANE:(h + 1) * LANE]
            k_ref[h] = (_rstd(knh + kpe, MLA_QK) * (knh * gmk_ref[...] + kpe_rot)).astype(BF16)

    def whole(r, c):
        return pl.BlockSpec((r, c), lambda i: (0, 0))

    tab = pl.BlockSpec((ts, LANE), lambda i: (i % ns, 0))
    return pl.pallas_call(
        body, name="mla_prep_fwd", grid=(T // ts,),
        in_specs=[pl.BlockSpec((ts, 4 * LANE), lambda i: (i, CB_CQ // 4)),
                  whole(1, MLA_Q_LORA), whole(1, MLA_KV_LORA), whole(MLA_Q_LORA, W), whole(MLA_KV_LORA, W),
                  whole(MLA_KV_LORA, MLA_HEADS * MLA_V), whole(1, LANE), whole(1, LANE), tab, tab, tab],
        out_specs=[pl.BlockSpec((MLA_HEADS, ts, LANE), lambda i: (0, i, 0)),
                   pl.BlockSpec((MLA_HEADS, ts, LANE), lambda i: (0, i, 0)),
                   pl.BlockSpec((ts, MLA_HEADS * MLA_V), lambda i: (i, 0))],
        out_shape=[jax.ShapeDtypeStruct((MLA_HEADS, T, LANE), BF16), jax.ShapeDtypeStruct((MLA_HEADS, T, LANE), BF16),
                   jax.ShapeDtypeStruct((T, MLA_HEADS * MLA_V), BF16)],
        compiler_params=_cp(),
    )(proj, gq, gkv, wuqp, wkp, wv, gmq, gmk, cos, sa, sb)


def _mla_prep_bwd(dproj, dq, dk, dv, proj, gq, gkv, wuqp, wkp, wv, gmq, gmk, cos, sa, sb, S):
    T = proj.shape[0]
    ts = _tile(S, 256)
    ns = S // ts
    W = MLA_HEADS * LANE

    def body(dpin_ref, dq_ref, dk_ref, dv_ref, p_ref, gq_ref, gkv_ref, wuq_ref, wk_ref, wv_ref, gmq_ref, gmk_ref,
             cos_ref, sa_ref, sb_ref,
             dp_ref, dwuq_ref, dwk_ref, dwv_ref, dgq_ref, dgkv_ref, dgmq_ref, dgmk_ref, dq0_ref, dkn_ref):
        del dpin_ref

        @pl.when(pl.program_id(0) == 0)
        def _():
            for r in (dwuq_ref, dwk_ref, dwv_ref, dgq_ref, dgkv_ref, dgmq_ref, dgmk_ref):
                r[...] = jnp.zeros_like(r)

        cq = p_ref[:, 0:2 * LANE].astype(F32)
        ckv = p_ref[:, 2 * LANE:3 * LANE].astype(F32)
        kpe = pltpu.roll(p_ref[:, 3 * LANE:4 * LANE].astype(F32), HALF, axis=1)
        rq = _rstd(cq, MLA_Q_LORA)
        rkv = _rstd(ckv, MLA_KV_LORA)
        gq, gkv, gmq, gmk = gq_ref[...], gkv_ref[...], gmq_ref[...], gmk_ref[...]
        cqn = (cq * rq * gq).astype(BF16)
        ckn = (ckv * rkv * gkv).astype(BF16)
        q0 = _mm(cqn, wuq_ref[...])
        kn = _mm(ckn, wk_ref[...])
        c, a, b = cos_ref[...], sa_ref[...], sb_ref[...]
        lane = lax.broadcasted_iota(jnp.int32, (ts, LANE), 1)
        dgmq = jnp.zeros((1, LANE), F32)
        dgmk = jnp.zeros((1, LANE), F32)
        nope = lane < MLA_NOPE
        kpe_rot = _rope(kpe * gmk, c, a, b)
        dk_sum = jnp.zeros((ts, LANE), F32)
        back = jnp.zeros((ts, 1), F32)
        for h in range(MLA_HEADS):
            q0h = q0[:, h * LANE:(h + 1) * LANE]
            r = _rstd(q0h, MLA_QK)
            d1 = _rope_t(dq_ref[h], c, a, b)
            gy = d1 * gmq
            dq0_ref[:, h * LANE:(h + 1) * LANE] = (
                r * gy - q0h * (r * r * r) * (jnp.sum(q0h * gy, axis=-1, keepdims=True) * (1.0 / MLA_QK))).astype(BF16)
            dgmq = dgmq + jnp.sum(d1 * q0h * r, axis=0, keepdims=True)
            knh = kn[:, h * LANE:(h + 1) * LANE]
            dkh = dk_ref[h]
            r = _rstd(knh + kpe, MLA_QK)
            r3dot = (r * r * r) * (jnp.sum((knh * gmk + kpe_rot) * dkh, axis=-1, keepdims=True) * (1.0 / MLA_QK))
            dkn_ref[:, h * LANE:(h + 1) * LANE] = jnp.where(nope, r * gmk * dkh - knh * r3dot, 0.0).astype(BF16)
            dgmk = dgmk + jnp.sum(jnp.where(nope, dkh * knh * r, 0.0), axis=0, keepdims=True)
            dk_sum = dk_sum + r * dkh
            back = back + r3dot
        rot = jnp.where(nope | (lane >= MLA_QK), 0.0, _rope_t(dk_sum, c, a, b))
        dkpe = gmk * rot - kpe * back
        dgmk = dgmk + jnp.sum(kpe * rot, axis=0, keepdims=True)
        dq0 = dq0_ref[...]
        dkn = dkn_ref[...]
        dvv = dv_ref[...]
        dwuq_ref[...] += _mm_tn(cqn, dq0)
        dwk_ref[...] += _mm_tn(ckn, dkn)
        dwv_ref[...] += _mm_tn(ckn, dvv)
        dgmq_ref[...] += dgmq
        dgmk_ref[...] += dgmk
        dcqn = _mm_nt(dq0, wuq_ref[...])
        gy = dcqn * gq
        dp_ref[:, 0:2 * LANE] = (
            rq * gy - cq * (rq * rq * rq) * (jnp.sum(cq * gy, axis=-1, keepdims=True) * (1.0 / MLA_Q_LORA))).astype(BF16)
        dgq_ref[...] += jnp.sum(dcqn * cq * rq, axis=0, keepdims=True)
        dckn = _mm_nt(dkn, wk_ref[...]) + _mm_nt(dvv, wv_ref[...])
        gy = dckn * gkv
        dp_ref[:, 2 * LANE:3 * LANE] = (
            rkv * gy - ckv * (rkv * rkv * rkv) * (jnp.sum(ckv * gy, axis=-1, keepdims=True) * (1.0 / MLA_KV_LORA))).astype(BF16)
        dgkv_ref[...] += jnp.sum(dckn * ckv * rkv, axis=0, keepdims=True)
        dp_ref[:, 3 * LANE:4 * LANE] = pltpu.roll(dkpe, HALF, axis=1).astype(BF16)

    def whole(r, c):
        return pl.BlockSpec((r, c), lambda i: (0, 0))

    tab = pl.BlockSpec((ts, LANE), lambda i: (i % ns, 0))
    heads = pl.BlockSpec((MLA_HEADS, ts, LANE), lambda i: (0, i, 0))
    return pl.pallas_call(
        body, name="mla_prep_bwd", grid=(T // ts,),
        in_specs=[pl.BlockSpec(memory_space=pl.ANY), heads, heads,
                  pl.BlockSpec((ts, MLA_HEADS * MLA_V), lambda i: (i, 0)),
                  pl.BlockSpec((ts, 4 * LANE), lambda i: (i, CB_CQ // 4)),
                  whole(1, MLA_Q_LORA), whole(1, MLA_KV_LORA), whole(MLA_Q_LORA, W), whole(MLA_KV_LORA, W),
                  whole(MLA_KV_LORA, MLA_HEADS * MLA_V), whole(1, LANE), whole(1, LANE), tab, tab, tab],
        out_specs=[pl.BlockSpec((ts, 4 * LANE), lambda i: (i, CB_CQ // 4)),
                   whole(MLA_Q_LORA, W), whole(MLA_KV_LORA, W), whole(MLA_KV_LORA, MLA_HEADS * MLA_V),
                   whole(1, MLA_Q_LORA), whole(1, MLA_KV_LORA), whole(1, LANE), whole(1, LANE)],
        out_shape=[jax.ShapeDtypeStruct(dproj.shape, BF16),
                   jax.ShapeDtypeStruct((MLA_Q_LORA, W), F32), jax.ShapeDtypeStruct((MLA_KV_LORA, W), F32),
                   jax.ShapeDtypeStruct((MLA_KV_LORA, MLA_HEADS * MLA_V), F32),
                   jax.ShapeDtypeStruct((1, MLA_Q_LORA), F32), jax.ShapeDtypeStruct((1, MLA_KV_LORA), F32),
                   jax.ShapeDtypeStruct((1, LANE), F32), jax.ShapeDtypeStruct((1, LANE), F32)],
        scratch_shapes=[pltpu.VMEM((ts, W), BF16), pltpu.VMEM((ts, W), BF16)],
        input_output_aliases={0: 0},
        compiler_params=_cp(),
    )(dproj, dq, dk, dv, proj, gq, gkv, wuqp, wkp, wv, gmq, gmk, cos, sa, sb)


def _dil_prep_fwd(proj, gq, gk):
    T = proj.shape[0]
    ts = _tile(T, 512)

    def body(pq_ref, pk_ref, gq_ref, gk_ref, q_ref, k_ref):
        for c in range(NPAIR):
            cs = slice(c * LANE, (c + 1) * LANE)
            t = jnp.concatenate([pq_ref[:, cs], pk_ref[:, cs]], axis=1).astype(F32)
            y = t * lax.rsqrt(_head_bcast_sum(t * t, terms=2) * (1.0 / DIL_HEAD_DIM) + EPS)
            q_ref[:, cs] = (y[:, 0:LANE] * gq_ref[:, cs]).astype(BF16)
            k_ref[:, cs] = (y[:, LANE:2 * LANE] * gk_ref[:, cs]).astype(BF16)

    col = pl.BlockSpec((1, DIL_WIDTH), lambda i, g: (0, g))
    out = pl.BlockSpec((ts, DIL_WIDTH), lambda i, g: (i, g))
    seg = lambda c0: pl.BlockSpec((ts, DIL_WIDTH), lambda i, g: (i, c0 // NPAIR + g))
    return pl.pallas_call(
        body, name="dil_prep_fwd", grid=(T // ts, DIL_GROUPS),
        in_specs=[seg(CB_DQ), seg(CB_DK), col, col],
        out_specs=[out, out],
        out_shape=[jax.ShapeDtypeStruct((T, DIL_QK), BF16)] * 2,
        compiler_params=_cp(),
    )(proj, proj, gq, gk)


def _dil_prep_bwd(dproj, ddq, ddk, ddv, proj, gq, gk):
    T = proj.shape[0]
    ts = _tile(T, 512)
    nt = T // ts

    def body(dpin_ref, ddq_ref, ddk_ref, ddv_ref, pq_ref, pk_ref, gq_ref, gk_ref, dp_ref, dgq_ref, dgk_ref,
             stage, sems):
        del dpin_ref
        g, i = pl.program_id(0), pl.program_id(1)

        @pl.when(i == 0)
        def _():
            dgq_ref[...] = jnp.zeros_like(dgq_ref)
            dgk_ref[...] = jnp.zeros_like(dgk_ref)

        def fill(slot):
            stage[slot, 2] = ddv_ref[...].astype(BF16)
            for c in range(NPAIR):
                cs = slice(c * LANE, (c + 1) * LANE)
                t = jnp.concatenate([pq_ref[:, cs], pk_ref[:, cs]], axis=1).astype(F32)
                d = jnp.concatenate([ddq_ref[:, cs], ddk_ref[:, cs]], axis=1)
                gy = d * jnp.concatenate([gq_ref[:, cs], gk_ref[:, cs]], axis=1)
                r = lax.rsqrt(_head_bcast_sum(t * t, terms=2) * (1.0 / DIL_HEAD_DIM) + EPS)
                dot = _head_bcast_sum(t * gy, terms=2) * (1.0 / DIL_HEAD_DIM)
                dx = (r * gy - t * (r * r * r) * dot).astype(BF16)
                stage[slot, 0, :, cs] = dx[:, 0:LANE]
                stage[slot, 1, :, cs] = dx[:, LANE:2 * LANE]
                part = jnp.sum(d * t * r, axis=0, keepdims=True)
                dgq_ref[:, cs] += part[:, 0:LANE]
                dgk_ref[:, cs] += part[:, LANE:2 * LANE]

        def copies_of(step):
            sg, si = step // nt, step % nt
            return _put_copies([stage.at[:, k] for k in range(3)], dp_ref, sems, step % 2,
                               pl.ds(pl.multiple_of(si * ts, ts), ts),
                               [pl.multiple_of((c0 + NPAIR * sg) * LANE, LANE) for c0 in (CB_DQ, CB_DK, CB_DV)])

        _put_pipeline(g * nt + i, DIL_GROUPS * nt, copies_of, fill)

    col = pl.BlockSpec((1, DIL_WIDTH), lambda g, i: (0, g))
    tok = pl.BlockSpec((ts, DIL_WIDTH), lambda g, i: (i, g))
    seg = lambda c0: pl.BlockSpec((ts, DIL_WIDTH), lambda g, i: (i, c0 // NPAIR + g))
    return pl.pallas_call(
        body, name="dil_prep_bwd", grid=(DIL_GROUPS, nt),
        in_specs=[pl.BlockSpec(memory_space=pl.ANY), tok, tok, tok, seg(CB_DQ), seg(CB_DK), col, col],
        out_specs=[pl.BlockSpec(memory_space=pl.ANY), col, col],
        out_shape=[jax.ShapeDtypeStruct(dproj.shape, BF16), jax.ShapeDtypeStruct((1, DIL_QK), F32),
                   jax.ShapeDtypeStruct((1, DIL_QK), F32)],
        scratch_shapes=[pltpu.VMEM((2, 3, ts, DIL_WIDTH), BF16), pltpu.SemaphoreType.DMA((2, 3))],
        input_output_aliases={0: 0},
        compiler_params=_cp(),
    )(dproj, ddq, ddk, ddv, proj, proj, gq, gk)


COPY_ROWS = 256


def _to_classes(src_ref, dst_ref, d, L, scale=None):
    m = min(L, max(8, COPY_ROWS // d))
    for c0 in range(0, L, m):
        x = src_ref[c0 * d:(c0 + m) * d, :].astype(F32)
        if scale is not None:
            x = x * scale
        if d > 1:
            x = jnp.swapaxes(x.reshape(m, d, LANE), 0, 1)
        for r in range(d):
            dst_ref[r * L + c0:r * L + c0 + m, :] = (x[r] if d > 1 else x).astype(dst_ref.dtype)


def _from_classes(src_ref, dst_ref, d, L):
    n = min(L, COPY_ROWS)
    for r in range(d):
        for c0 in range(0, L, n):
            rows = pl.ds(r + c0 * d, n, stride=d) if d > 1 else pl.ds(c0, n)
            dst_ref[rows, :] = src_ref[r * L + c0:r * L + c0 + n, :].astype(dst_ref.dtype)


MLA_TQ, MLA_TK = 512, 512


def _causal_bias(tq, tk, shift):
    row = lax.broadcasted_iota(jnp.int32, (tq, tk), 0)
    col = lax.broadcasted_iota(jnp.int32, (tq, tk), 1)
    return jnp.where(row >= col + shift, 0.0, NEG)


def _mla_specs(S):
    heads = pl.BlockSpec((2, S, LANE), lambda b, j: (j, b, 0))
    pair = pl.BlockSpec((S, LANE), lambda b, j: (b, j))
    return heads, pair


def _mla_attn_fwd(q, k, v, B, S):
    tq = _tile(S, MLA_TQ)
    tk = _tile(tq, MLA_TK)
    nd = tq // tk
    scale = MLA_QK ** -0.5
    heads, pair = _mla_specs(S)

    def body(q_ref, k_ref, v_ref, o_ref, lse_ref):
        lo, lok = _lane_lo((tq, LANE)), _lane_lo((tk, LANE))
        diag = [_causal_bias(tq, tk, i * tk) for i in range(nd)]

        def block(g, _):
            row0 = pl.multiple_of(g * tq, tq)
            rows = pl.ds(row0, tq)
            qs = [q_ref[hh, rows, :] for hh in range(2)]

            one = jnp.ones((), BF16)

            def step(off, carries, bias):
                off = pl.multiple_of(off, tk)
                vt = v_ref[pl.ds(off, tk), :]
                vh = (jnp.where(lok, vt, one), jnp.where(lok, one, vt))
                out = []
                for hh, (m, acc) in enumerate(carries):
                    s = _mm_nt(qs[hh], k_ref[hh, pl.ds(off, tk), :]) * scale
                    if bias is not None:
                        s = s + bias
                    m_new = jnp.maximum(m, jnp.max(s, axis=-1, keepdims=True))
                    p = jnp.exp(s - m_new)
                    out.append((m_new, jnp.exp(m - m_new) * acc + _mm(p, vh[hh])))
                return tuple(out)

            init = (jnp.full((tq, 1), NEG, F32), jnp.zeros((tq, LANE), F32))
            carries = lax.fori_loop(0, g * nd, lambda i, c: step(i * tk, c, None), (init, init))
            for i in range(nd):
                carries = step(row0 + i * tk, carries, diag[i])
            (ma, acca), (mb, accb) = carries
            la, lb = pltpu.roll(acca, HALF, axis=1), pltpu.roll(accb, HALF, axis=1)
            o_ref[rows, :] = jnp.where(lo, acca / la, accb / lb)
            lse_ref[rows, :] = jnp.where(lo, ma + jnp.log(la), mb + jnp.log(lb))
            return 0

        lax.fori_loop(0, S // tq, block, 0)

    return pl.pallas_call(
        body, name="mla_attn_fwd", grid=(B, NPAIR), in_specs=[heads, heads, pair], out_specs=[pair, pair],
        out_shape=[jax.ShapeDtypeStruct((B * S, MLA_HEADS * MLA_V), F32)] * 2,
        compiler_params=_cp(),
    )(q, k, v)


DIL_UNROLL = 16


def _dil_geometry(gi, S):
    span, d = DIL_PATTERNS[gi]
    L = S // d
    t = _tile(L, 128)
    window = span // d
    back = min(-(-window // t) * t, L - t)
    return d, L, t, window, back


def _dil_specs(gi, S):
    qk = pl.BlockSpec((S, LANE), lambda b, j: (b, NPAIR * gi + j))
    v = pl.BlockSpec((S, LANE), lambda b, j: (b, CB_DV + NPAIR * gi + j))
    pair = pl.BlockSpec((S, LANE), lambda b, j: (b, j))
    return qk, v, pair


def _dil_bias(bias_ref, sl_ref, j, t, kw, back, window):
    row = lax.broadcasted_iota(jnp.int32, (2 * t, kw), 0)
    col = lax.broadcasted_iota(jnp.int32, (2 * t, kw), 1)
    second = row >= t
    slope = jnp.where(second, sl_ref[j, 1], sl_ref[j, 0])
    for n in range(bias_ref.shape[0]):
        dist = jnp.where(second, row - t, row) + n * back - col
        bias_ref[n] = jnp.where((dist >= 0) & (dist <= window), -slope * dist.astype(F32), NEG)


def _stack_heads(x, lo):
    zero = jnp.zeros((), x.dtype)
    return jnp.concatenate([jnp.where(lo, x, zero), jnp.where(lo, zero, x)], axis=0)


def _dil_attn_fwd(gi, slopes, qn, kn, proj, B, S):
    d, L, t, window, back = _dil_geometry(gi, S)
    kw, nq = back + t, L // t
    nbias = 2 if back else 1
    qk, vspec, pair = _dil_specs(gi, S)

    def body(sl_ref, q_ref, k_ref, v_ref, o_ref, lse_ref, qs, ks, vs, os_, ls, bias_ref):
        _to_classes(q_ref, qs, d, L, DIL_HEAD_DIM ** -0.5)
        _to_classes(k_ref, ks, d, L)
        _to_classes(v_ref, vs, d, L)
        _dil_bias(bias_ref, sl_ref, pl.program_id(1), t, kw, back, window)
        lo = _lane_lo((t, LANE))

        def block(g, _):
            qb = g % nq if d > 1 else g
            row0 = pl.multiple_of(g * t, t)
            rows = pl.ds(row0, t)
            early = qb * t < back
            keys = pl.ds(pl.multiple_of(jnp.where(early, row0 - qb * t, row0 - back), t), kw)
            s = _mm_nt(_stack_heads(qs[rows, :], lo), ks[keys, :]) + bias_ref[jnp.where(early, 0, nbias - 1)]
            m = jnp.max(s, axis=-1, keepdims=True)
            p = jnp.exp(s - m)
            l = jnp.sum(p, axis=-1, keepdims=True)
            o2 = _mm(p, vs[keys, :]) / l
            lse2 = m + jnp.log(l)
            os_[rows, :] = jnp.where(lo, o2[:t], o2[t:])
            ls[rows, :] = jnp.where(lo, lse2[:t], lse2[t:])
            return 0

        lax.fori_loop(0, d * nq, block, 0, unroll=DIL_UNROLL if d * nq % DIL_UNROLL == 0 else 1)
        _from_classes(os_, o_ref, d, L)
        _from_classes(ls, lse_ref, d, L)

    return pl.pallas_call(
        body, name=f"dil_attn_fwd_{gi}", grid=(B, NPAIR),
        in_specs=[pl.BlockSpec(memory_space=pltpu.SMEM), qk, qk, vspec], out_specs=[pair, pair],
        out_shape=[jax.ShapeDtypeStruct((B * S, DIL_WIDTH), F32)] * 2,
        scratch_shapes=[pltpu.VMEM((S, LANE), BF16)] * 3 + [pltpu.VMEM((S, LANE), F32)] * 2
                       + [pltpu.VMEM((nbias, 2 * t, kw), F32)],
        compiler_params=_cp(),
    )(slopes, qn, kn, proj)


def _mla_attn_bwd(q, k, v, do, lse, delta, B, S):
    T = B * S
    tq = _tile(S, MLA_TQ)
    tk = _tile(tq, MLA_TK)
    nd = tq // tk
    scale = MLA_QK ** -0.5
    heads, pair = _mla_specs(S)

    def body(q_ref, k_ref, v_ref, do_ref, lse_ref, dl_ref, dq_ref, dk_ref, dv_ref):
        dk_ref[...] = jnp.zeros_like(dk_ref)
        dv_ref[...] = jnp.zeros_like(dv_ref)
        lo = _lane_lo((tq, LANE))
        diag = [_causal_bias(tq, tk, i * tk) for i in range(nd)]

        def block(g, _):
            row0 = pl.multiple_of(g * tq, tq)
            rows = pl.ds(row0, tq)
            per_head = []
            for hh in range(2):
                sel = lo if hh == 0 else jnp.logical_not(lo)
                per_head.append((q_ref[hh, rows, :], jnp.where(sel, do_ref[rows, :], jnp.zeros((), BF16)),
                                 jnp.max(jnp.where(sel, lse_ref[rows, :], NEG), axis=-1, keepdims=True),
                                 jnp.max(jnp.where(sel, dl_ref[rows, :], NEG), axis=-1, keepdims=True)))

            def step(off, dq_accs, bias):
                cols = pl.ds(pl.multiple_of(off, tk), tk)
                vt = v_ref[cols, :]
                out, dv = [], None
                for hh, (qh, doh, lse_h, dl_h) in enumerate(per_head):
                    kh = k_ref[hh, cols, :]
                    s = _mm_nt(qh, kh) * scale
                    if bias is not None:
                        s = s + bias
                    p = jnp.exp(s - lse_h)
                    ds = (p * (_mm_nt(doh, vt) - dl_h)).astype(BF16)
                    dk_ref[hh, cols, :] += _mm_tn(ds, qh) * scale
                    part = _mm_tn(p, doh)
                    dv = part if dv is None else dv + part
                    out.append(dq_accs[hh] + _mm(ds, kh))
                dv_ref[cols, :] += dv
                return tuple(out)

            zero = jnp.zeros((tq, LANE), F32)
            dq_accs = lax.fori_loop(0, g * nd, lambda i, a: step(i * tk, a, None), (zero, zero))
            for i in range(nd):
                dq_accs = step(row0 + i * tk, dq_accs, diag[i])
            for hh in range(2):
                dq_ref[hh, rows, :] = dq_accs[hh] * scale
            return 0

        lax.fori_loop(0, S // tq, block, 0)

    return pl.pallas_call(
        body, name="mla_attn_bwd", grid=(B, NPAIR), in_specs=[heads, heads, pair, pair, pair, pair],
        out_specs=[heads, heads, pair],
        out_shape=[jax.ShapeDtypeStruct((MLA_HEADS, T, LANE), F32), jax.ShapeDtypeStruct((MLA_HEADS, T, LANE), F32),
                   jax.ShapeDtypeStruct((T, MLA_HEADS * MLA_V), F32)],
        compiler_params=_cp(),
    )(q, k, v, do, lse, delta)


def _dil_attn_bwd(gi, slopes, qn, kn, proj, do, lse, delta, through, B, S):
    d, L, t, window, back = _dil_geometry(gi, S)
    kw, nq = back + t, L // t
    nbias = 2 if back else 1
    scale = DIL_HEAD_DIM ** -0.5
    qk, vspec, pair = _dil_specs(gi, S)

    def body(*refs):
        refs = list(refs)
        sl_ref, q_ref, k_ref, v_ref, do_ref, lse_ref, dl_ref = refs[:7]
        dq_ref, dk_ref, dv_ref, qs, ks, vs, dos, lss, dls, dqs, dks, dvs, bias_ref = refs[-13:]
        _to_classes(q_ref, qs, d, L, scale)
        for src, dst in ((k_ref, ks), (v_ref, vs), (do_ref, dos), (lse_ref, lss), (dl_ref, dls)):
            _to_classes(src, dst, d, L)
        _dil_bias(bias_ref, sl_ref, pl.program_id(1), t, kw, back, window)
        dks[...] = jnp.zeros_like(dks)
        dvs[...] = jnp.zeros_like(dvs)
        lo = _lane_lo((t, LANE))

        def stats(ref, rows):
            x = ref[rows, :]
            return jnp.concatenate([jnp.max(jnp.where(lo, x, NEG), axis=-1, keepdims=True),
                                    jnp.max(jnp.where(lo, NEG, x), axis=-1, keepdims=True)], axis=0)

        def block(g, _):
            qb = g % nq if d > 1 else g
            row0 = pl.multiple_of(g * t, t)
            rows = pl.ds(row0, t)
            early = qb * t < back
            keys = pl.ds(pl.multiple_of(jnp.where(early, row0 - qb * t, row0 - back), t), kw)
            q2 = _stack_heads(qs[rows, :], lo)
            do2 = _stack_heads(dos[rows, :], lo)
            kt = ks[keys, :]
            s = _mm_nt(q2, kt) + bias_ref[jnp.where(early, 0, nbias - 1)]
            p = jnp.exp(s - stats(lss, rows))
            ds = (p * (_mm_nt(do2, vs[keys, :]) - stats(dls, rows))).astype(BF16)
            dq2 = _mm(ds, kt) * scale
            dqs[rows, :] = jnp.where(lo, dq2[:t], dq2[t:])
            dks[keys, :] += _mm_tn(ds, q2)
            dvs[keys, :] += _mm_tn(p, do2)
            return 0

        lax.fori_loop(0, d * nq, block, 0, unroll=DIL_UNROLL if d * nq % DIL_UNROLL == 0 else 1)
        for src, dst in ((dqs, dq_ref), (dks, dk_ref), (dvs, dv_ref)):
            _from_classes(src, dst, d, L)

    in_specs = [pl.BlockSpec(memory_space=pltpu.SMEM), qk, qk, vspec, pair, pair, pair]
    args = [slopes, qn, kn, proj, do, lse, delta]
    aliases = {}
    if through is not None:
        aliases = {len(args) + i: i for i in range(3)}
        in_specs = in_specs + [pl.BlockSpec(memory_space=pl.ANY)] * 3
        args = args + list(through)
    return pl.pallas_call(
        body, name=f"dil_attn_bwd_{gi}", grid=(B, NPAIR), in_specs=in_specs, out_specs=[qk, qk, qk],
        out_shape=[jax.ShapeDtypeStruct((B * S, DIL_QK), F32)] * 3,
        scratch_shapes=[pltpu.VMEM((S, LANE), BF16)] * 4 + [pltpu.VMEM((S, LANE), F32)] * 5
                       + [pltpu.VMEM((nbias, 2 * t, kw), F32)],
        input_output_aliases=aliases,
        compiler_params=_cp(),
    )(*args)


def _merge_proj_specs(ts):
    wide = lambda c0, w: pl.BlockSpec((ts, w), lambda i: (i, c0 * LANE // w))
    return [wide(CB_BZ, DIL_WIDTH), wide(CB_CZ, DIL_WIDTH)] + [wide(CB_GATE + 8 * i, D_MODEL) for i in range(3)]


def _merge_common(p_refs, bg_ref, ob_ref, og_refs, lse_refs):
    bz = p_refs[0][...].astype(F32)
    cz = p_refs[1][...].astype(F32)
    gates = [_sigmoid(p_refs[2 + i][...].astype(F32) + bg_ref[:, i * D_MODEL:(i + 1) * D_MODEL]) for i in range(3)]
    ob = ob_ref[...]
    lses = [r[...] for r in lse_refs]
    mx = jnp.maximum(jnp.maximum(lses[0], lses[1]), lses[2])
    es = [jnp.exp(v - mx) for v in lses]
    inv = 1.0 / (es[0] + es[1] + es[2])
    alphas = [e * inv for e in es]
    oc = alphas[0] * og_refs[0][...] + alphas[1] * og_refs[1][...] + alphas[2] * og_refs[2][...]
    return bz, cz, gates, ob, alphas, oc


def _merge_fwd(x, proj, b_gate, ya, ob, ogs, lses, woa, wob, woc, wo):
    T = x.shape[0]
    ts = _tile(T, 256)

    def body(x_ref, p0, p1, p2, p3, p4, bg_ref, ya_ref, ob_ref, og0, og1, og2, l0, l1, l2,
             woa_ref, wob_ref, woc_ref, wo_ref, out_ref):
        bz, cz, gates, obv, alphas, oc = _merge_common((p0, p1, p2, p3, p4), bg_ref, ob_ref, (og0, og1, og2),
                                                       (l0, l1, l2))
        yb = obv * _silu(bz)
        yc = oc * _silu(cz)
        merged = (gates[0] * _mm(ya_ref[...], woa_ref[...]) + gates[1] * _mm(yb, wob_ref[...])
                  + gates[2] * _mm(yc, woc_ref[...]))
        out_ref[...] = x_ref[...] + _mm(merged, wo_ref[...])

    def whole(r, c):
        return pl.BlockSpec((r, c), lambda i: (0, 0))

    tok = lambda w: pl.BlockSpec((ts, w), lambda i: (i, 0))
    return pl.pallas_call(
        body, name="merge_fwd", grid=(T // ts,),
        in_specs=[tok(D_MODEL)] + _merge_proj_specs(ts) + [whole(1, 3 * D_MODEL), tok(CONV_WIDTH)]
                 + [tok(DIL_WIDTH)] * 7 + [whole(CONV_WIDTH, D_MODEL)] * 3 + [whole(D_MODEL, D_MODEL)],
        out_specs=tok(D_MODEL),
        out_shape=jax.ShapeDtypeStruct((T, D_MODEL), F32),
        compiler_params=_cp(),
    )(x, *[proj] * 5, b_gate, ya, ob, *ogs, *lses, woa, wob, woc, wo)


def _merge_bwd(dout, proj, b_gate, ya, ob, ogs, lses, woa, wob, woc, wo):
    T = dout.shape[0]
    ts = _tile(T, 256)
    nt = T // ts

    def body(do_ref, p0, p1, p2, p3, p4, bg_ref, ya_ref, ob_ref, og0, og1, og2, l0, l1, l2,
             woa_ref, wob_ref, woc_ref, wo_ref,
             dp_ref, dya_ref, dob_ref, dlb_ref, dg0, dg1, dg2, dl0, dl1, dl2,
             mg_ref, dpa_ref, dpb_ref, dpc_ref, yb_ref, yc_ref, dbg_ref, st_bz, st_cz, st_gate, sems):
        step = pl.program_id(0)
        slot = step % 2

        def copies_of(s):
            return _put_copies([st_bz, st_cz, st_gate], dp_ref, sems, s % 2, pl.ds(pl.multiple_of(s * ts, ts), ts),
                               [CB_BZ * LANE, CB_CZ * LANE, CB_GATE * LANE])

        @pl.when(step >= 2)
        def _():
            for cp in copies_of(step - 2):
                cp.wait()

        bz, cz, gates, obv, alphas, oc = _merge_common((p0, p1, p2, p3, p4), bg_ref, ob_ref, (og0, og1, og2),
                                                       (l0, l1, l2))
        (sb, dsb), (sc, dsc) = _silu_and_grad(bz), _silu_and_grad(cz)
        yb = obv * sb
        yc = oc * sc
        ps = [_mm(ya_ref[...], woa_ref[...]), _mm(yb, wob_ref[...]), _mm(yc, woc_ref[...])]
        mg_ref[...] = (gates[0] * ps[0] + gates[1] * ps[1] + gates[2] * ps[2]).astype(BF16)
        yb_ref[...] = yb.astype(BF16)
        yc_ref[...] = yc.astype(BF16)
        dm = _mm_nt(do_ref[...], wo_ref[...])
        dps = []
        first = pl.program_id(0) == 0
        for i, dref in enumerate((dpa_ref, dpb_ref, dpc_ref)):
            g = gates[i]
            dpi = (dm * g).astype(BF16)
            dref[...] = dpi
            dps.append(dpi)
            dgp = dm * ps[i] * g * (1.0 - g)
            st_gate[slot, :, i * D_MODEL:(i + 1) * D_MODEL] = dgp.astype(BF16)
            part = jnp.sum(dgp, axis=0, keepdims=True)

            @pl.when(first)
            def _():
                dbg_ref[:, i * D_MODEL:(i + 1) * D_MODEL] = part

            @pl.when(jnp.logical_not(first))
            def _():
                dbg_ref[:, i * D_MODEL:(i + 1) * D_MODEL] += part

        dya_ref[...] = _mm_nt(dps[0], woa_ref[...])
        dyb = _mm_nt(dps[1], wob_ref[...])
        dyc = _mm_nt(dps[2], woc_ref[...])
        st_bz[slot] = (dyb * obv * dsb).astype(BF16)
        st_cz[slot] = (dyc * oc * dsc).astype(BF16)
        for cp in copies_of(step):
            cp.start()
        dob = dyb * sb
        doc = dyc * sc
        dob_ref[...] = dob.astype(BF16)
        for c in range(NPAIR):
            cs = slice(c * LANE, (c + 1) * LANE)
            dlb_ref[:, cs] = _head_bcast_sum(dob[:, cs] * obv[:, cs])
            dd = _head_bcast_sum(doc[:, cs] * oc[:, cs])
            for a, dref, lref in zip(alphas, (dg0, dg1, dg2), (dl0, dl1, dl2)):
                dref[:, cs] = (a[:, cs] * doc[:, cs]).astype(BF16)
                lref[:, cs] = a[:, cs] * dd

        @pl.when(step == nt - 1)
        def _():
            if nt >= 2:
                for cp in copies_of(step - 1):
                    cp.wait()
            for cp in copies_of(step):
                cp.wait()

    def whole(r, c):
        return pl.BlockSpec((r, c), lambda i: (0, 0))

    tok = lambda w: pl.BlockSpec((ts, w), lambda i: (i, 0))
    sd = jax.ShapeDtypeStruct
    W = DIL_WIDTH
    return pl.pallas_call(
        body, name="merge_bwd", grid=(nt,),
        in_specs=[tok(D_MODEL)] + _merge_proj_specs(ts) + [whole(1, 3 * D_MODEL), tok(CONV_WIDTH)] + [tok(W)] * 7
                 + [whole(CONV_WIDTH, D_MODEL)] * 3 + [whole(D_MODEL, D_MODEL)],
        out_specs=[pl.BlockSpec(memory_space=pl.ANY), tok(CONV_WIDTH), tok(W), tok(W)] + [tok(W)] * 6
                  + [tok(D_MODEL)] * 4 + [tok(W), tok(W), whole(1, 3 * D_MODEL)],
        out_shape=[sd((T, PP), BF16), sd((T, CONV_WIDTH), F32), sd((T, W), BF16), sd((T, W), F32)]
                  + [sd((T, W), BF16)] * 3 + [sd((T, W), F32)] * 3
                  + [sd((T, D_MODEL), BF16)] * 4 + [sd((T, W), BF16)] * 2 + [sd((1, 3 * D_MODEL), F32)],
        scratch_shapes=[pltpu.VMEM((2, ts, W), BF16), pltpu.VMEM((2, ts, W), BF16),
                        pltpu.VMEM((2, ts, 3 * D_MODEL), BF16), pltpu.SemaphoreType.DMA((2, 3))],
        compiler_params=_cp(),
    )(dout, *[proj] * 5, b_gate, ya, ob, *ogs, *lses, woa, wob, woc, wo)


def _loss_head(y, target):
    T = y.shape[0]
    ts = _tile(T, 512)

    def body(y_ref, t_ref, d_ref, l_ref):
        e = y_ref[...] - t_ref[...]
        d_ref[...] = e * (1.0 / D_MODEL)
        l_ref[...] = jnp.zeros((1, 8, LANE), F32) + jnp.sum(e * e)

    tok = pl.BlockSpec((ts, D_MODEL), lambda i: (i, 0))
    return pl.pallas_call(
        body, name="loss_head", grid=(T // ts,), in_specs=[tok, tok],
        out_specs=[tok, pl.BlockSpec((1, 8, LANE), lambda i: (i, 0, 0))],
        out_shape=[jax.ShapeDtypeStruct((T, D_MODEL), F32), jax.ShapeDtypeStruct((T // ts, 8, LANE), F32)],
        compiler_params=_cp(),
    )(y, target)


def _my_index():
    return 4 * lax.axis_index("x") + 2 * lax.axis_index("y") + lax.axis_index("c")


def _peers():
    x, y, c = (lax.axis_index(a) for a in AXES)
    out = []
    for kk in range(1, N_DEV):
        px = 1 - x if kk & 4 else x
        py = 1 - y if kk & 2 else y
        pc = 1 - c if kk & 1 else c
        out.append(((px, py, pc), 4 * px + 2 * py + pc))
    return out


def _exchange(arrays, name, gather):
    n = len(arrays)

    def body(*refs):
        srcs, outs = refs[:n], refs[n:2 * n]
        send_sems, recv_sems, local_sems = refs[2 * n:]
        me = _my_index()
        peers = _peers()
        started = []
        for a, (src, out) in enumerate(zip(srcs, outs)):
            mine = pltpu.make_async_copy(src if gather else src.at[me], out.at[me], local_sems.at[a])
            mine.start()
            started.append(mine)
        sends = []
        for i, (pos, idx) in enumerate(peers):
            for a, (src, out) in enumerate(zip(srcs, outs)):
                cp = pltpu.make_async_remote_copy(
                    src_ref=src if gather else src.at[idx], dst_ref=out.at[me], send_sem=send_sems.at[a, i],
                    recv_sem=recv_sems.at[a, i], device_id=pos, device_id_type=pl.DeviceIdType.MESH)
                cp.start()
                sends.append(cp)
        for i, (pos, idx) in enumerate(peers):
            for a, (src, out) in enumerate(zip(srcs, outs)):
                pltpu.make_async_remote_copy(
                    src_ref=src if gather else src.at[idx], dst_ref=out.at[idx], send_sem=send_sems.at[a, i],
                    recv_sem=recv_sems.at[a, i], device_id=pos, device_id_type=pl.DeviceIdType.MESH).wait_recv()
        for cp in sends:
            cp.wait_send()
        for mine in started:
            mine.wait()

    any_space = pl.BlockSpec(memory_space=pl.ANY)
    return pl.pallas_call(
        body, name=name, in_specs=[any_space] * n, out_specs=[any_space] * n,
        out_shape=[jax.ShapeDtypeStruct(((N_DEV,) + a.shape) if gather else a.shape, a.dtype) for a in arrays],
        scratch_shapes=[pltpu.SemaphoreType.DMA((n, N_DEV - 1)), pltpu.SemaphoreType.DMA((n, N_DEV - 1)),
                        pltpu.SemaphoreType.DMA((n,))],
    )(*arrays)


N_CHIP = 4


def _chip_places():
    x, y, c = (lax.axis_index(a) for a in AXES)
    return (x, y, c), (x, y, 1 - c), [(1 - x, y, c), (x, 1 - y, c), (1 - x, 1 - y, c)]


def _index_of(pos):
    return 4 * pos[0] + 2 * pos[1] + pos[2]


def _sibling_swap(arrays, name):
    n = len(arrays)

    def body(*refs):
        srcs, outs = refs[:n], refs[n:2 * n]
        send_sems, recv_sems = refs[2 * n:]
        (x, y, c), sibling, _ = _chip_places()
        sends = []
        for a, (src, out) in enumerate(zip(srcs, outs)):
            for q in range(N_CHIP):
                def copy(core, a=a, q=q, src=src, out=out):
                    return pltpu.make_async_remote_copy(
                        src_ref=src.at[2 * q + core], dst_ref=out.at[q], send_sem=send_sems.at[N_CHIP * a + q],
                        recv_sem=recv_sems.at[N_CHIP * a + q], device_id=sibling, device_id_type=pl.DeviceIdType.MESH)
                mine = copy(1 - c)
                mine.start()
                sends.append((mine, copy(c)))
        for mine, arrival in sends:
            arrival.wait_recv()
            mine.wait_send()

    any_space = pl.BlockSpec(memory_space=pl.ANY)
    return pl.pallas_call(
        body, name=name, in_specs=[any_space] * n, out_specs=[any_space] * n,
        out_shape=[jax.ShapeDtypeStruct((N_CHIP,) + a.shape[1:], a.dtype) for a in arrays],
        scratch_shapes=[pltpu.SemaphoreType.DMA((N_CHIP * n,)), pltpu.SemaphoreType.DMA((N_CHIP * n,))],
    )(*arrays)


def _chip_pair_sum(part, got, name):
    R, C = part.shape[1:]
    tr = R
    while tr * C * part.dtype.itemsize > REDUCE_BLOCK_BYTES // 4 and tr % 32 == 0:
        tr //= 2
    c = lax.axis_index("c")

    def body(c_ref, p_ref, g_ref, o_ref):
        del c_ref
        o_ref[...] = (p_ref[...].astype(F32) + g_ref[...].astype(F32)).astype(o_ref.dtype)

    return pl.pallas_call(
        body, name=name, grid_spec=pltpu.PrefetchScalarGridSpec(
            num_scalar_prefetch=1, grid=(N_CHIP, R // tr),
            in_specs=[pl.BlockSpec((None, tr, C), lambda q, i, cr: (2 * q + cr[0], i, 0)),
                      pl.BlockSpec((None, tr, C), lambda q, i, cr: (q, i, 0))],
            out_specs=pl.BlockSpec((None, tr, C), lambda q, i, cr: (q, i, 0))),
        out_shape=jax.ShapeDtypeStruct((N_CHIP, R, C), part.dtype),
        compiler_params=_cp(),
    )(jnp.reshape(c, (1,)).astype(jnp.int32), part, got)


def _peer_count(mode):
    return {"chips": N_CHIP - 1, "near": N_CHIP}.get(mode, N_DEV - 1)


def _remote_copies(srcs, lands, send_sems, recv_sems, mode):
    if mode == "chips":
        (x, y, _), _, others = _chip_places()
        my_slot, peers = 2 * x + y, [(chip, 2 * chip[0] + chip[1]) for chip in others]
    elif mode == "near":
        me, sibling, others = _chip_places()
        my_slot, peers = _index_of(me), [(pos, _index_of(pos)) for pos in [sibling] + others]
    else:
        my_slot, peers = _my_index(), _peers()
    whole = mode in ("gather", "near")
    out = []
    for i, (pos, idx) in enumerate(peers):
        for a, (src, land) in enumerate(zip(srcs, lands)):
            def copy(slot, a=a, src=src, land=land, i=i, pos=pos, idx=idx):
                return pltpu.make_async_remote_copy(
                    src_ref=src if whole else src.at[idx], dst_ref=land.at[slot],
                    send_sem=send_sems.at[a * len(peers) + i], recv_sem=recv_sems.at[a * len(peers) + i],
                    device_id=pos, device_id_type=pl.DeviceIdType.MESH)
            out.append((copy(my_slot), copy(idx)))
    return out


def _exchange_start(arrays, name, mode):
    n = len(arrays)
    hbm = pl.BlockSpec(memory_space=pltpu.HBM)
    sem = pl.BlockSpec(memory_space=pltpu.SEMAPHORE)
    lands = [lax.empty(((N_DEV,) + a.shape) if mode in ("gather", "near") else a.shape, a.dtype) for a in arrays]

    def body(*refs):
        srcs, lands_ = refs[:n], refs[n:2 * n]
        send_sems, recv_sems = refs[2 * n:2 * n + 2]
        for mine, _ in _remote_copies(srcs, lands_, send_sems, recv_sems, mode):
            mine.start()
        refs[-1][...] = jnp.zeros_like(refs[-1])

    sems = pltpu.SemaphoreType.DMA((n * _peer_count(mode),))
    buffers = [pltpu.HBM(a.shape, a.dtype) for a in list(arrays) + lands]
    res = pl.pallas_call(
        body, name=name, in_specs=[hbm] * (2 * n), out_specs=[sem, sem] + [hbm] * (2 * n) + [pl.BlockSpec(memory_space=pltpu.VMEM)],
        out_shape=[sems, sems] + buffers + [jax.ShapeDtypeStruct((8, LANE), F32)],
        input_output_aliases={i: 2 + i for i in range(2 * n)},
        compiler_params=pltpu.CompilerParams(has_side_effects=pltpu.SideEffectType.DATAFLOW_SIDE_EFFECTING),
    )(*[pltpu.with_memory_space_constraint(a, pltpu.HBM) for a in list(arrays) + lands])
    return (res[0], res[1], res[2:2 + n], res[2 + n:2 + 2 * n]), res[-1]


def _exchange_wait(handle, after, name, mode):
    send_sems, recv_sems, srcs, lands = handle
    n = len(srcs)
    after = list(after) if isinstance(after, (list, tuple)) else [after]
    hbm = pl.BlockSpec(memory_space=pltpu.HBM)
    sem = pl.BlockSpec(memory_space=pltpu.SEMAPHORE)

    def body(*refs):
        for mine, arrival in _remote_copies(refs[:n], refs[n:2 * n], refs[2 * n], refs[2 * n + 1], mode):
            mine.wait_send()
            arrival.wait_recv()

    res = pl.pallas_call(
        body, name=name, in_specs=[hbm] * (2 * n) + [sem, sem] + [pl.BlockSpec(memory_space=pl.ANY)] * len(after),
        out_specs=[hbm] * (2 * n), out_shape=[pltpu.HBM(a.shape, a.dtype) for a in list(srcs) + list(lands)],
        input_output_aliases={i: i for i in range(2 * n)},
        compiler_params=pltpu.CompilerParams(has_side_effects=pltpu.SideEffectType.DATAFLOW_SIDE_EFFECTING),
    )(*srcs, *lands, send_sems, recv_sems, *after)
    return res[n:]


def _sibling_forward(lands, name):
    n = len(lands)

    def body(*refs):
        ins, outs, send_sems, recv_sems = refs[:n], refs[n:2 * n], refs[2 * n], refs[2 * n + 1]
        (x, y, c), sibling, others = _chip_places()
        copies = []
        for a, (src, out) in enumerate(zip(ins, outs)):
            for j, chip in enumerate(others):
                def copy(core, a=a, j=j, chip=chip, src=src, out=out):
                    slot = _index_of((chip[0], chip[1], core))
                    return pltpu.make_async_remote_copy(
                        src_ref=src.at[slot], dst_ref=out.at[slot], send_sem=send_sems.at[3 * a + j],
                        recv_sem=recv_sems.at[3 * a + j], device_id=sibling, device_id_type=pl.DeviceIdType.MESH)
                mine = copy(c)
                mine.start()
                copies.append((mine, copy(1 - c)))
        for mine, arrival in copies:
            arrival.wait_recv()
        for mine, arrival in copies:
            mine.wait_send()

    any_space = pl.BlockSpec(memory_space=pl.ANY)
    return pl.pallas_call(
        body, name=name, in_specs=[any_space] * n, out_specs=[any_space] * n,
        out_shape=[jax.ShapeDtypeStruct(a.shape, a.dtype) for a in lands],
        scratch_shapes=[pltpu.SemaphoreType.DMA((3 * n,)), pltpu.SemaphoreType.DMA((3 * n,))],
        input_output_aliases={i: i for i in range(n)},
    )(*lands)


def _own_slot(land, mine, slot=None):
    slot = _my_index() if slot is None else slot
    return lax.dynamic_update_slice(land, mine, (slot,) + (0,) * (land.ndim - 1))


def _adamw(w, g, m, v):
    m = ADAM_B1 * m + (1.0 - ADAM_B1) * g
    v = ADAM_B2 * v + (1.0 - ADAM_B2) * (g * g)
    m_hat = m / (1.0 - ADAM_B1 ** ADAM_STEP)
    v_hat = v / (1.0 - ADAM_B2 ** ADAM_STEP)
    delta = -ADAM_LR * (m_hat / (jnp.sqrt(v_hat) + ADAM_EPS) + ADAM_WD * w)
    return delta, m, v


def _reduce_adamw(part, w, m, v, name, run=0, after=None, into=()):
    R, C = part.shape[1:]
    tr = R
    while N_DEV * tr * C * part.dtype.itemsize > REDUCE_BLOCK_BYTES and tr % 32 == 0:
        tr //= 2
    steps = R // tr
    unread = ([] if after is None else [after]) + list(into)
    first = 1 + len(unread) - len(into)

    def body(p_ref, *refs):
        w_ref, m_ref, v_ref, g_ref, d_ref, nm_ref, nv_ref = refs[len(unread):]
        g = p_ref[0].astype(F32)
        for s in range(1, p_ref.shape[0]):
            g = g + p_ref[s].astype(F32)
        g_ref[...] = g
        d_ref[...], nm_ref[...], nv_ref[...] = _adamw(w_ref[...], g, m_ref[...], v_ref[...])

    row = pl.BlockSpec((tr, C), lambda i: (run * steps + i, 0))
    return pl.pallas_call(
        body, name=name, grid=(steps,),
        in_specs=[pl.BlockSpec((part.shape[0], tr, C), lambda i: (0, i, 0))]
                 + [pl.BlockSpec(memory_space=pl.ANY)] * len(unread) + [row, row, row],
        out_specs=[row] * 4, out_shape=[jax.ShapeDtypeStruct(w.shape, F32)] * 4,
        input_output_aliases={first + k: k for k in range(len(into))},
        compiler_params=_cp(),
    )(part, *unread, w, m, v)


BIG = ("w_in", "w_uq", "w_ukv", "w_out_a", "w_out_b", "w_out_c", "w_o")
SMALL = ("norm_g", "b_gate", "conv_w", "conv_b", "q_a_norm_g", "kv_a_norm_g", "mla_q_norm_g", "mla_k_norm_g",
         "dil_q_norm_g", "dil_k_norm_g")
PACK_ROWS = 128
REDUCE_BLOCK_BYTES = 6 * 1024 * 1024


def _pack_local(tensors):
    flat = jnp.concatenate([t.reshape(-1) for t in tensors])
    pad = (-flat.shape[0]) % (PACK_ROWS * LANE)
    return jnp.concatenate([flat, jnp.zeros((pad,), flat.dtype)]).reshape(-1, LANE)


def _unpack_local(rows, like):
    flat = rows.reshape(-1)
    out, off = [], 0
    for t in like:
        out.append(flat[off:off + t.size].reshape(t.shape))
        off += t.size
    return out


def _cols_to_slots(a):
    k = a.shape[0]
    return a.reshape(k, N_DEV, -1).transpose(1, 0, 2)


def _slots_to_cols(s):
    return s.transpose(1, 0, 2).reshape(s.shape[1], -1)


def _rope_tables(S):
    inv = ROPE_THETA ** (-jnp.arange(0, MLA_ROPE, 2, dtype=F32) / MLA_ROPE)
    ang = jnp.arange(S, dtype=F32)[:, None] * inv[None, :]
    cos, sin = jnp.cos(ang), jnp.sin(ang)
    one = jnp.ones((S, MLA_NOPE), F32)
    z16, z32, z64 = (jnp.zeros((S, n), F32) for n in (16, 32, 64))
    cosp = jnp.concatenate([one, cos, cos, jnp.ones((S, 32), F32)], axis=1)
    sa = jnp.concatenate([z64, -sin, z16, z32], axis=1)
    sb = jnp.concatenate([z64, z16, sin, z32], axis=1)
    return cosp, sa, sb


def _alibi_slopes():
    n = DIL_GROUPS * DIL_HEADS
    m = 2.0 ** (-8.0 * jnp.arange(1, n + 1, dtype=F32) / n)
    return m.reshape(DIL_GROUPS, NPAIR, 2)


def _pad_slots(s):
    n, k, c = s.shape
    return _slots_to_cols(jnp.concatenate([s, jnp.zeros((n, k, LANE - c), s.dtype)], axis=2))


def _layer_params(gw, small, l):
    p = {}
    p["wp"] = _pad_columns(gw["w_in"])
    p["norm_g"] = small["norm_g"][l][None]
    p["b_gate"] = small["b_gate"][l][None]
    p["conv_w"] = gw["conv_w"].transpose(1, 0, 2).reshape(CONV_K, CONV_WIDTH)
    p["conv_b"] = small["conv_b"][l][None]
    p["gq"] = small["q_a_norm_g"][l][None]
    p["gkv"] = small["kv_a_norm_g"][l][None]
    p["wuqp"] = _pad_slots(gw["w_uq"])
    kv = gw["w_ukv"]
    p["wkp"] = _pad_slots(kv[:, :, :MLA_NOPE])
    p["wv"] = kv[:, :, MLA_NOPE:].transpose(1, 0, 2).reshape(MLA_KV_LORA, MLA_HEADS * MLA_V)
    zpad = jnp.zeros((1, LANE - MLA_QK), F32)
    p["gmq"] = jnp.concatenate([small["mla_q_norm_g"][l][None], zpad], axis=1)
    p["gmk"] = jnp.concatenate([small["mla_k_norm_g"][l][None], zpad], axis=1)
    tile = lambda g: jnp.broadcast_to(g[:, None, :], (DIL_GROUPS, DIL_HEADS, DIL_HEAD_DIM)).reshape(1, DIL_QK)
    p["gdq"] = tile(small["dil_q_norm_g"][l])
    p["gdk"] = tile(small["dil_k_norm_g"][l])
    p["woa"], p["wob"], p["woc"] = (_slots_to_cols(gw[n]) for n in ("w_out_a", "w_out_b", "w_out_c"))
    p["wo"] = gw["w_o"].reshape(D_MODEL, D_MODEL)
    return p


def _layer_fwd(x, p, tabs, slopes, B, S):
    proj, ht = _inproj_fwd(x, p["norm_g"], p["wp"])
    ya = _mixa_fwd(proj, p["conv_w"], p["conv_b"], B, S)
    q, k, v = _mla_prep_fwd(proj, p["gq"], p["gkv"], p["wuqp"], p["wkp"], p["wv"], p["gmq"], p["gmk"], *tabs, S)
    ob, lse_b = _mla_attn_fwd(q, k, v, B, S)
    qn, kn = _dil_prep_fwd(proj, p["gdq"], p["gdk"])
    ogs, lses = [], []
    for gi in range(DIL_GROUPS):
        o, lse = _dil_attn_fwd(gi, slopes[gi], qn, kn, proj, B, S)
        ogs.append(o)
        lses.append(lse)
    out = _merge_fwd(x, proj, p["b_gate"], ya, ob, ogs, lses, p["woa"], p["wob"], p["woc"], p["wo"])
    saved = dict(x=x, proj=proj, ht=ht, ya=ya, q=q, k=k, v=v, ob=ob, lse_b=lse_b, qn=qn, kn=kn, ogs=ogs, lses=lses)
    return out, saved


def _layer_bwd(dout, sv, p, tabs, slopes, B, S, big_ready=None):
    proj = sv["proj"]
    (dproj, dya, dob, dlb, dg0, dg1, dg2, dl0, dl1, dl2, merged, dpa, dpb, dpc, yb, yc, dbg) = _merge_bwd(
        dout, proj, p["b_gate"], sv["ya"], sv["ob"], sv["ogs"], sv["lses"], p["woa"], p["wob"], p["woc"], p["wo"])
    g = {}
    g["w_o"] = _matmul_tn(merged, dout, "dw_o").reshape(N_DEV, D_MODEL // N_DEV, D_MODEL)
    g["w_out_a"] = _cols_to_slots(_matmul_tn(sv["ya"], dpa, "dw_out_a"))
    g["w_out_b"] = _cols_to_slots(_matmul_tn(yb, dpb, "dw_out_b"))
    g["w_out_c"] = _cols_to_slots(_matmul_tn(yc, dpc, "dw_out_c"))
    g["b_gate"] = dbg[0]
    dproj, st = _mixa_bwd(dproj, dya, proj, p["conv_w"], p["conv_b"], B, S)
    g["conv_w"] = st[0:CONV_K]
    g["conv_b"] = st[CONV_K]
    dq, dk, dv = _mla_attn_bwd(sv["q"], sv["k"], sv["v"], dob, sv["lse_b"], dlb, B, S)
    dproj, dwuqp, dwkp, dwv, dgq, dgkv, dgmq, dgmk = _mla_prep_bwd(
        dproj, dq, dk, dv, proj, p["gq"], p["gkv"], p["wuqp"], p["wkp"], p["wv"], p["gmq"], p["gmk"], *tabs, S)
    g["w_uq"] = _cols_to_slots(dwuqp)[:, :, :MLA_QK]
    g["w_ukv"] = jnp.concatenate([_cols_to_slots(dwkp)[:, :, :MLA_NOPE], _cols_to_slots(dwv)], axis=2)
    g["q_a_norm_g"], g["kv_a_norm_g"] = dgq[0], dgkv[0]
    g["mla_q_norm_g"], g["mla_k_norm_g"] = dgmq[0, :MLA_QK], dgmk[0, :MLA_QK]
    dqkv = None
    for gi, (dog, dlg) in enumerate(((dg0, dl0), (dg1, dl1), (dg2, dl2))):
        dqkv = _dil_attn_bwd(gi, slopes[gi], sv["qn"], sv["kn"], proj, dog, sv["lses"][gi], dlg, dqkv, B, S)
    dproj, dgdq, dgdk = _dil_prep_bwd(dproj, *dqkv, proj, p["gdq"], p["gdk"])
    g["dil_q_norm_g"] = dgdq.reshape(DIL_GROUPS, DIL_HEADS, DIL_HEAD_DIM).sum(axis=1)
    g["dil_k_norm_g"] = dgdk.reshape(DIL_GROUPS, DIL_HEADS, DIL_HEAD_DIM).sum(axis=1)
    g["w_in"] = _unpad_columns(_matmul_nn(sv["ht"], dproj, "dw_in"))
    token = None if big_ready is None else big_ready(g)
    dx, dng = _inproj_bwd_x(dproj, p["wp"], sv["x"], _after(token, p["norm_g"]), dout)
    g["norm_g"] = dng[0]
    return dx, g


def _after(token, a):
    return a if token is None else a + token[0:1, 0:1]


def _local_step(x, target, small, B, S, weights_of, grads_out, big_ready=None):
    tabs = _rope_tables(S)
    sl = _alibi_slopes()
    slopes = [sl[gi] * float(DIL_PATTERNS[gi][1]) for gi in range(DIL_GROUPS)]
    params, saved = [], []
    for l in range(DEPTH):
        gw, token = weights_of(l, x)
        p = _layer_params(gw, small, l)
        p["norm_g"] = _after(token, p["norm_g"])
        x, sv = _layer_fwd(x, p, tabs, slopes, B, S)
        params.append(p)
        saved.append(sv)
    dout, lparts = _loss_head(x, target)
    sq = jnp.sum(lparts[:, 0, 0])
    token = None
    for l in reversed(range(DEPTH)):
        p = dict(params[l], b_gate=_after(token, params[l]["b_gate"]))
        ready = None if big_ready is None else (lambda g, l=l: big_ready(l, g))
        dout, g = _layer_bwd(dout, saved[l], p, tabs, slopes, B, S, ready)
        token = grads_out(l, g, dout)
    return sq, dout


def kernel(x, norm_g, w_in, b_gate, conv_w, conv_b, q_a_norm_g, w_uq, kv_a_norm_g, w_ukv, mla_q_norm_g, mla_k_norm_g, dil_q_norm_g, dil_k_norm_g, w_out_a, w_out_b, w_out_c, w_o, loss_target, m_norm_g, m_w_in, m_b_gate, m_conv_w, m_conv_b, m_q_a_norm_g, m_w_uq, m_kv_a_norm_g, m_w_ukv, m_mla_q_norm_g, m_mla_k_norm_g, m_dil_q_norm_g, m_dil_k_norm_g, m_w_out_a, m_w_out_b, m_w_out_c, m_w_o, v_norm_g, v_w_in, v_b_gate, v_conv_w, v_conv_b, v_q_a_norm_g, v_w_uq, v_kv_a_norm_g, v_w_ukv, v_mla_q_norm_g, v_mla_k_norm_g, v_dil_q_norm_g, v_dil_k_norm_g, v_w_out_a, v_w_out_b, v_w_out_c, v_w_o):
    names = ("norm_g", "w_in", "b_gate", "conv_w", "conv_b", "q_a_norm_g", "w_uq", "kv_a_norm_g", "w_ukv",
             "mla_q_norm_g", "mla_k_norm_g", "dil_q_norm_g", "dil_k_norm_g", "w_out_a", "w_out_b", "w_out_c", "w_o")
    w = dict(zip(names, (norm_g, w_in, b_gate, conv_w, conv_b, q_a_norm_g, w_uq, kv_a_norm_g, w_ukv, mla_q_norm_g,
                         mla_k_norm_g, dil_q_norm_g, dil_k_norm_g, w_out_a, w_out_b, w_out_c, w_o)))
    m = dict(zip(names, (m_norm_g, m_w_in, m_b_gate, m_conv_w, m_conv_b, m_q_a_norm_g, m_w_uq, m_kv_a_norm_g, m_w_ukv,
                         m_mla_q_norm_g, m_mla_k_norm_g, m_dil_q_norm_g, m_dil_k_norm_g, m_w_out_a, m_w_out_b,
                         m_w_out_c, m_w_o)))
    v = dict(zip(names, (v_norm_g, v_w_in, v_b_gate, v_conv_w, v_conv_b, v_q_a_norm_g, v_w_uq, v_kv_a_norm_g, v_w_ukv,
                         v_mla_q_norm_g, v_mla_k_norm_g, v_dil_q_norm_g, v_dil_k_norm_g, v_w_out_a, v_w_out_b,
                         v_w_out_c, v_w_o)))
    B, S, _ = x.shape
    me = _my_index()
    cshard = CONV_WIDTH // N_DEV

    shards = [[w[n][0].astype(BF16) for n in BIG]]
    state = {}

    def widen(t):
        return lax.dynamic_update_slice(jnp.zeros((DEPTH, CONV_K, CONV_WIDTH), F32), t, (0, 0, me * cshard))

    pick = lambda d: [widen(d[n]) if n == "conv_w" else d[n] for n in SMALL]

    def weights_of(l, after):
        if l == 0:
            first = shards[0] + [conv_w]
            handle, token = _exchange_start(first, "all_gather_weights_0_start", "near")
            zero = token[0:1, 0:1]
            state["shards1"] = [(w[n][1] + zero).astype(BF16) for n in BIG]
            for n in BIG:
                state["rows", n] = [a.reshape(-1, a.shape[-1]) + zero for a in (w[n], m[n], v[n])]
            state["small"] = [_pack_local(pick(d)) + zero for d in (w, m, v)]
            busy = state["shards1"] + [a for n in BIG for a in state["rows", n]] + state["small"]
            landed = _exchange_wait(handle, busy, "all_gather_weights_0_wait", "near")
            landed = _sibling_forward(landed, "all_gather_weights_0_forward")
            got = [_own_slot(a, s[None]) for a, s in zip(landed, first)]
            state["gather"], token = _exchange_start(state["shards1"], "all_gather_weights_1_start", "gather")
            state["conv_w"] = got[-1]
        else:
            landed = _exchange_wait(state["gather"], after, "all_gather_weights_1_wait", "gather")
            got, token = [_own_slot(a, s[None]) for a, s in zip(landed, state["shards1"])], None
        gw = dict(zip(BIG, got))
        gw["conv_w"] = state["conv_w"][:, l]
        return gw, token

    recv, small_parts = {}, {}
    my_chip = 2 * lax.axis_index("x") + lax.axis_index("y")

    def big_ready(l, g):
        send = [g[n].astype(BF16) for n in BIG]
        if l == DEPTH - 1:
            state["scatter"], token = _exchange_start(send, "exchange_weight_grads_1_start", "scatter")
        else:
            swapped = _sibling_swap(send, "exchange_weight_grads_0_sibling")
            send = [_chip_pair_sum(s, t, "chip_pair_sum_" + n) for n, s, t in zip(BIG, send, swapped)]
            state["chips"], token = _exchange_start(send, "exchange_weight_grads_0_start", "chips")
        state["sent", l] = send
        return token

    def grads_out(l, g, after):
        small_parts[l] = [g[n] for n in SMALL]
        state["after", l] = after
        return None

    def landed_grads(k, key, mode, slot, after):
        landed = _exchange_wait(state[key], after, f"exchange_weight_grads_{k}_wait", mode)
        mine = [lax.dynamic_slice_in_dim(s, slot, 1, axis=0) for s in state["sent", k]]
        return [_own_slot(a, s, slot) for a, s in zip(landed, mine)]

    sq, grad_x = _local_step(x.reshape(B * S, D_MODEL), loss_target.reshape(B * S, D_MODEL), w, B, S,
                             weights_of, grads_out, big_ready)
    loss = lax.psum(sq * (0.5 / D_MODEL), AXES)

    part = {n: jnp.stack([small_parts[l][i] for l in range(DEPTH)]) for i, n in enumerate(SMALL)}
    small_like = [part[n] for n in SMALL]
    pack = _pack_local(small_like)
    handle, token = _exchange_start([pack], "all_gather_small_grads_start", "gather")

    last = DEPTH - 1
    recv = landed_grads(last, "scatter", "scatter", me, [state["after", 0], token])
    half = {n: _reduce_adamw(recv[i], *state["rows", n], f"reduce_adamw_{n}_{last}", run=last)
            for i, n in enumerate(BIG)}
    recv = landed_grads(0, "chips", "chips", my_chip, [half[n][0] for n in BIG])
    res, done = {}, []
    for i, n in enumerate(BIG):
        outs = _reduce_adamw(recv[i], *state["rows", n], f"reduce_adamw_{n}_0", run=0, into=half[n])
        res[n] = tuple(a.reshape(w[n].shape) for a in outs)
        done.append(outs[0])

    landed, = _exchange_wait(handle, done, "all_gather_small_grads_wait", "gather")
    parts = _own_slot(landed, pack[None])
    gs, ds, ms, vs = _reduce_adamw(parts, *state["small"], "reduce_adamw_small")
    for n, t in zip(SMALL, zip(*(_unpack_local(a, small_like) for a in (gs, ds, ms, vs)))):
        if n == "conv_w":
            t = tuple(lax.dynamic_slice(a, (0, 0, me * cshard), (DEPTH, CONV_K, cshard)) for a in t)
        res[n] = t

    out = [loss, grad_x.reshape(B, S, D_MODEL)]
    for i in range(4):
        out += [res[n][i] for n in names]
    return tuple(out)
```

```python
import jax
import jax.numpy as jnp
from jax import lax
from jax.experimental import pallas as pl
from jax.experimental.pallas import tpu as pltpu

F32 = jnp.float32
BF16 = jnp.bfloat16

D_MODEL = 1024
DEPTH = 2
CONV_WIDTH = 512
CONV_K = 3
MLA_HEADS = 8
MLA_Q_LORA = 256
MLA_KV_LORA = 128
MLA_NOPE = 64
MLA_ROPE = 32
MLA_V = 64
MLA_QK = MLA_NOPE + MLA_ROPE
ROPE_THETA = 10000.0
DIL_PATTERNS = ((128, 1), (512, 4), (2048, 16))
DIL_GROUPS = 3
DIL_HEADS = 8
DIL_HEAD_DIM = 64
DIL_WIDTH = DIL_HEADS * DIL_HEAD_DIM
DIL_QK = DIL_GROUPS * DIL_WIDTH
EPS = 1e-6
N_IN = 11168

ADAM_LR = 0.001
ADAM_B1 = 0.9
ADAM_B2 = 0.999
ADAM_EPS = 1e-08
ADAM_WD = 0.01
ADAM_STEP = 10

N_DEV = 8
AXES = ("x", "y", "c")
LANE = 128
HALF = 64
NPAIR = 4

CB_AB, CB_AC, CB_AX, CB_AZ = 0, 4, 8, 12
CB_CQ, CB_CKV, CB_KPE = 16, 18, 19
CB_BZ = 20
CB_DQ, CB_DK, CB_DV = 24, 36, 48
CB_CZ, CB_GATE = 60, 64
NCB = 88
PP = NCB * LANE
KPE_END = CB_KPE * LANE + MLA_ROPE
SHARD_COLS = N_IN // N_DEV
NEG = -1e30
VMEM_LIMIT = 56 * 1024 * 1024


def _pad_columns(shards):
    parts = []
    for p in range(N_DEV):
        cut = min(max(KPE_END - p * SHARD_COLS, 0), SHARD_COLS)
        if 0 < cut < SHARD_COLS:
            parts += [shards[p, :, :cut], jnp.zeros((shards.shape[1], LANE - MLA_ROPE), shards.dtype), shards[p, :, cut:]]
        else:
            parts.append(shards[p])
    return jnp.concatenate(parts, axis=1)


def _unpad_columns(wp):
    def columns(a, b):
        gap = LANE - MLA_ROPE
        if b <= KPE_END:
            return wp[:, a:b]
        if a >= KPE_END:
            return wp[:, a + gap:b + gap]
        return jnp.concatenate([wp[:, a:KPE_END], wp[:, KPE_END + gap:b + gap]], axis=1)

    return jnp.stack([columns(p * SHARD_COLS, (p + 1) * SHARD_COLS) for p in range(N_DEV)])


def _put_copies(stages, dst_ref, sems, slot, rows, cols):
    return [pltpu.make_async_copy(st.at[slot], dst_ref.at[rows, pl.ds(c0, st.shape[-1])], sems.at[slot, k])
            for k, (st, c0) in enumerate(zip(stages, cols))]


def _put_pipeline(step, nsteps, copies_of, fill):
    @pl.when(step >= 2)
    def _():
        for cp in copies_of(step - 2):
            cp.wait()

    fill(step % 2)
    for cp in copies_of(step):
        cp.start()

    @pl.when(step == nsteps - 1)
    def _():
        if nsteps >= 2:
            for cp in copies_of(step - 1):
                cp.wait()
        for cp in copies_of(step):
            cp.wait()


def _cp():
    return pltpu.CompilerParams(vmem_limit_bytes=VMEM_LIMIT)


def _rstd(x, n):
    return lax.rsqrt(jnp.sum(x * x, axis=-1, keepdims=True) * (1.0 / n) + EPS)


def _sigmoid(z):
    return 1.0 / (1.0 + jnp.exp(-z))


def _silu(z):
    return z * _sigmoid(z)


def _silu_and_grad(z):
    s = _sigmoid(z)
    return z * s, s * (1.0 + z * (1.0 - s))


def _mm(a, b):
    return jnp.dot(a.astype(BF16), b.astype(BF16), preferred_element_type=F32)


def _mm_nt(a, b):
    return lax.dot_general(a.astype(BF16), b.astype(BF16), (((1,), (1,)), ((), ())), preferred_element_type=F32)


def _mm_tn(a, b):
    return lax.dot_general(a.astype(BF16), b.astype(BF16), (((0,), (0,)), ((), ())), preferred_element_type=F32)


def _lane_lo(shape):
    return lax.broadcasted_iota(jnp.int32, shape, len(shape) - 1) < HALF


def _head_bcast_sum(x, terms=3):
    w = x.shape[-1]
    same = (lax.broadcasted_iota(jnp.int32, (w, w), 0) // HALF) == (lax.broadcasted_iota(jnp.int32, (w, w), 1) // HALF)
    ones = jnp.where(same, 1.0, 0.0).astype(jnp.bfloat16)
    total = None
    for _ in range(terms):
        term = x.astype(jnp.bfloat16)
        x = x - term.astype(F32)
        part = jnp.dot(term, ones, preferred_element_type=F32)
        total = part if total is None else total + part
    return total


def _rope(t, cos, sa, sb):
    return t * cos + pltpu.roll(t, LANE - 16, axis=1) * sa + pltpu.roll(t, 16, axis=1) * sb


def _rope_t(d, cos, sa, sb):
    return d * cos + pltpu.roll(d * sa, 16, axis=1) + pltpu.roll(d * sb, LANE - 16, axis=1)


def _shift_down(u, k):
    rows = lax.broadcasted_iota(jnp.int32, u.shape, 0)
    return jnp.where(rows >= k, pltpu.roll(u, k, axis=0), 0.0)


def _shift_up(u, k):
    n = u.shape[0]
    rows = lax.broadcasted_iota(jnp.int32, u.shape, 0)
    return jnp.where(rows < n - k, pltpu.roll(u, n - k, axis=0), 0.0)


def _tile(n, want):
    t = min(n, want)
    assert n % t == 0, (n, want)
    return t


def _inproj_fwd(x, g, wp):
    T = x.shape[0]
    tm, tn = _tile(T, 2048), 512

    def body(x_ref, g_ref, w_ref, proj_ref, ht_ref, h_ref):
        @pl.when(pl.program_id(1) == 0)
        def _():
            n = min(tm, 512)
            for r0 in range(0, tm, n):
                xv = x_ref[r0:r0 + n, :]
                h = xv * _rstd(xv, D_MODEL) * g_ref[...]
                h_ref[r0:r0 + n, :] = h.astype(BF16)
                ht_ref[:, r0:r0 + n] = h.T.astype(BF16)

        proj_ref[...] = jnp.dot(h_ref[...], w_ref[...], preferred_element_type=F32).astype(BF16)

    return pl.pallas_call(
        body, name="inproj_fwd", grid=(T // tm, PP // tn),
        in_specs=[pl.BlockSpec((tm, D_MODEL), lambda i, j: (i, 0)),
                  pl.BlockSpec((1, D_MODEL), lambda i, j: (0, 0)),
                  pl.BlockSpec((D_MODEL, tn), lambda i, j: (0, j))],
        out_specs=[pl.BlockSpec((tm, tn), lambda i, j: (i, j)),
                   pl.BlockSpec((D_MODEL, tm), lambda i, j: (0, i))],
        out_shape=[jax.ShapeDtypeStruct((T, PP), BF16), jax.ShapeDtypeStruct((D_MODEL, T), BF16)],
        scratch_shapes=[pltpu.VMEM((tm, D_MODEL), BF16)],
        compiler_params=_cp(),
    )(x, g, wp)


def _matmul_nn(at, b, name, after=None):
    K, T = at.shape
    N = b.shape[1]
    tt, tn = _tile(T, 1024), _tile(N, 2816)
    nk = T // tt
    unread = [] if after is None else [after]

    def body(a_ref, b_ref, *refs):
        o_ref, acc_ref = refs[len(unread):]
        k = pl.program_id(1)

        @pl.when(k == 0)
        def _():
            acc_ref[...] = jnp.zeros_like(acc_ref)

        acc_ref[...] += jnp.dot(a_ref[...], b_ref[...], preferred_element_type=F32)

        @pl.when(k == nk - 1)
        def _():
            o_ref[...] = acc_ref[...].astype(BF16)

    return pl.pallas_call(
        body, name=name, grid=(N // tn, nk),
        in_specs=[pl.BlockSpec((K, tt), lambda j, k: (0, k)),
                  pl.BlockSpec((tt, tn), lambda j, k: (k, j))] + [pl.BlockSpec(memory_space=pl.ANY)] * len(unread),
        out_specs=pl.BlockSpec((K, tn), lambda j, k: (0, j)),
        out_shape=jax.ShapeDtypeStruct((K, N), BF16),
        scratch_shapes=[pltpu.VMEM((K, tn), F32)],
        compiler_params=_cp(),
    )(at, b, *unread)


def _matmul_tn(a, b, name):
    T, K = a.shape
    N = b.shape[1]
    tt, tn = _tile(T, 512), _tile(N, 1024)

    def body(a_ref, b_ref, o_ref):
        @pl.when(pl.program_id(1) == 0)
        def _():
            o_ref[...] = jnp.zeros_like(o_ref)

        o_ref[...] += _mm_tn(a_ref[...], b_ref[...])

    return pl.pallas_call(
        body, name=name, grid=(N // tn, T // tt),
        in_specs=[pl.BlockSpec((tt, K), lambda j, k: (k, 0)),
                  pl.BlockSpec((tt, tn), lambda j, k: (k, j))],
        out_specs=pl.BlockSpec((K, tn), lambda j, k: (0, j)),
        out_shape=jax.ShapeDtypeStruct((K, N), F32),
        compiler_params=_cp(),
    )(a, b)


def _inproj_bwd_x(dproj, wp, x, g, dout):
    T = x.shape[0]
    tm, tk = _tile(T, 1024), 1024
    nk = PP // tk

    def body(dp_ref, w_ref, x_ref, g_ref, do_ref, dx_ref, dg_ref, acc_ref):
        i, k = pl.program_id(0), pl.program_id(1)

        @pl.when(k == 0)
        def _():
            acc_ref[...] = jnp.zeros_like(acc_ref)

        @pl.when((k == 0) & (i == 0))
        def _():
            dg_ref[...] = jnp.zeros_like(dg_ref)

        acc_ref[...] += _mm_nt(dp_ref[...], w_ref[...])

        @pl.when(k == nk - 1)
        def _():
            dh = acc_ref[...]
            xv = x_ref[...]
            r = _rstd(xv, D_MODEL)
            gy = dh * g_ref[...]
            dot = jnp.sum(xv * gy, axis=-1, keepdims=True) * (1.0 / D_MODEL)
            dx_ref[...] = do_ref[...] + r * gy - xv * (r * r * r) * dot
            dg_ref[...] += jnp.sum(dh * xv * r, axis=0, keepdims=True)

    return pl.pallas_call(
        body, name="inproj_bwd_x", grid=(T // tm, nk),
        in_specs=[pl.BlockSpec((tm, tk), lambda i, k: (i, k)),
                  pl.BlockSpec((D_MODEL, tk), lambda i, k: (0, k)),
                  pl.BlockSpec((tm, D_MODEL), lambda i, k: (i, 0)),
                  pl.BlockSpec((1, D_MODEL), lambda i, k: (0, 0)),
                  pl.BlockSpec((tm, D_MODEL), lambda i, k: (i, 0))],
        out_specs=[pl.BlockSpec((tm, D_MODEL), lambda i, k: (i, 0)),
                   pl.BlockSpec((1, D_MODEL), lambda i, k: (0, 0))],
        out_shape=[jax.ShapeDtypeStruct((T, D_MODEL), F32), jax.ShapeDtypeStruct((1, D_MODEL), F32)],
        scratch_shapes=[pltpu.VMEM((tm, D_MODEL), F32)],
        compiler_params=_cp(),
    )(dproj, wp, x, g, dout)


A_SEGS = (CB_AB, CB_AC, CB_AX, CB_AZ)


def _mixa_fwd(proj, cw, cb, B, S):
    nc = CONV_WIDTH // LANE

    def body(ab_ref, ac_ref, ax_ref, az_ref, cw_ref, cb_ref, y_ref):
        ab, ac, ax, az = (r[...].astype(F32) for r in (ab_ref, ac_ref, ax_ref, az_ref))
        u = ac * ax
        conv = cb_ref[...] + cw_ref[0:1, :] * _shift_down(u, 2) + cw_ref[1:2, :] * _shift_down(u, 1) + cw_ref[2:3, :] * u
        y_ref[...] = (ab * conv * _silu(az)).astype(BF16)

    return pl.pallas_call(
        body, name="mixa_fwd", grid=(B, nc),
        in_specs=[pl.BlockSpec((S, LANE), lambda b, j, c0=c0: (b, c0 + j)) for c0 in A_SEGS]
                 + [pl.BlockSpec((CONV_K, LANE), lambda b, j: (0, j)),
                    pl.BlockSpec((1, LANE), lambda b, j: (0, j))],
        out_specs=pl.BlockSpec((S, LANE), lambda b, j: (b, j)),
        out_shape=jax.ShapeDtypeStruct((B * S, CONV_WIDTH), BF16),
        compiler_params=_cp(),
    )(proj, proj, proj, proj, cw, cb)


def _mixa_bwd(dproj, dy, proj, cw, cb, B, S):
    nc = CONV_WIDTH // LANE

    def body(dpin_ref, dy_ref, ab_ref, ac_ref, ax_ref, az_ref, cw_ref, cb_ref, dp_ref, st_ref, stage, sems):
        del dpin_ref
        j, b = pl.program_id(0), pl.program_id(1)
        ab, ac, ax, az = (r[...].astype(F32) for r in (ab_ref, ac_ref, ax_ref, az_ref))
        u = ac * ax
        u1, u2 = _shift_down(u, 1), _shift_down(u, 2)
        w0, w1, w2 = cw_ref[0:1, :], cw_ref[1:2, :], cw_ref[2:3, :]
        conv = cb_ref[...] + w0 * u2 + w1 * u1 + w2 * u
        s, ds_az = _silu_and_grad(az)
        d = dy_ref[...]
        dconv = d * ab * s
        du = w2 * dconv + w1 * _shift_up(dconv, 1) + w0 * _shift_up(dconv, 2)
        grads = (d * conv * s, du * ax, du * ac, d * ab * conv * ds_az)

        def fill(slot):
            for k, v in enumerate(grads):
                stage[slot, k] = v.astype(BF16)

        def copies_of(step):
            sj, sb = step // B, step % B
            return _put_copies([stage.at[:, k] for k in range(4)], dp_ref, sems, step % 2,
                               pl.ds(pl.multiple_of(sb * S, S), S),
                               [pl.multiple_of((c0 + sj) * LANE, LANE) for c0 in A_SEGS])

        _put_pipeline(j * B + b, nc * B, copies_of, fill)
        row = lax.broadcasted_iota(jnp.int32, (8, LANE), 0)
        st = jnp.zeros((8, LANE), F32)
        for r, v in enumerate((dconv * u2, dconv * u1, dconv * u, dconv)):
            st = st + jnp.where(row == r, jnp.sum(v, axis=0, keepdims=True), 0.0)

        @pl.when(pl.program_id(1) == 0)
        def _():
            st_ref[...] = st

        @pl.when(pl.program_id(1) != 0)
        def _():
            st_ref[...] += st

    return pl.pallas_call(
        body, name="mixa_bwd", grid=(nc, B),
        in_specs=[pl.BlockSpec(memory_space=pl.ANY),
                  pl.BlockSpec((S, LANE), lambda j, b: (b, j))]
                 + [pl.BlockSpec((S, LANE), lambda j, b, c0=c0: (b, c0 + j)) for c0 in A_SEGS]
                 + [pl.BlockSpec((CONV_K, LANE), lambda j, b: (0, j)),
                    pl.BlockSpec((1, LANE), lambda j, b: (0, j))],
        out_specs=[pl.BlockSpec(memory_space=pl.ANY),
                   pl.BlockSpec((8, LANE), lambda j, b: (0, j))],
        out_shape=[jax.ShapeDtypeStruct(dproj.shape, BF16), jax.ShapeDtypeStruct((8, CONV_WIDTH), F32)],
        scratch_shapes=[pltpu.VMEM((2, 4, S, LANE), BF16), pltpu.SemaphoreType.DMA((2, 4))],
        input_output_aliases={0: 0},
        compiler_params=_cp(),
    )(dproj, dy, proj, proj, proj, proj, cw, cb)


def _mla_prep_fwd(proj, gq, gkv, wuqp, wkp, wv, gmq, gmk, cos, sa, sb, S):
    T = proj.shape[0]
    ts = _tile(S, 512)
    ns = S // ts
    W = MLA_HEADS * LANE

    def body(p_ref, gq_ref, gkv_ref, wuq_ref, wk_ref, wv_ref, gmq_ref, gmk_ref, cos_ref, sa_ref, sb_ref,
             q_ref, k_ref, v_ref):
        cq = p_ref[:, 0:2 * LANE].astype(F32)
        ckv = p_ref[:, 2 * LANE:3 * LANE].astype(F32)
        kpe = pltpu.roll(p_ref[:, 3 * LANE:4 * LANE].astype(F32), HALF, axis=1)
        cqn = cq * _rstd(cq, MLA_Q_LORA) * gq_ref[...]
        ckn = (ckv * _rstd(ckv, MLA_KV_LORA) * gkv_ref[...]).astype(BF16)
        q0 = _mm(cqn, wuq_ref[...])
        kn = _mm(ckn, wk_ref[...])
        v_ref[...] = _mm(ckn, wv_ref[...]).astype(BF16)
        c, a, b = cos_ref[...], sa_ref[...], sb_ref[...]
        kpe_rot = _rope(kpe * gmk_ref[...], c, a, b)
        for h in range(MLA_HEADS):
            q0h = q0[:, h * LANE:(h + 1) * LANE]
            q_ref[h] = _rope(q0h * _rstd(q0h, MLA_QK) * gmq_ref[...], c, a, b).astype(BF16)
            knh = kn[:, h * LANE:(h + 1) * LANE]
            k_ref[h] = (_rstd(knh + kpe, MLA_QK) * (knh * gmk_ref[...] + kpe_rot)).astype(BF16)

    def whole(r, c):
        return pl.BlockSpec((r, c), lambda i: (0, 0))

    tab = pl.BlockSpec((ts, LANE), lambda i: (i % ns, 0))
    return pl.pallas_call(
        body, name="mla_prep_fwd", grid=(T // ts,),
        in_specs=[pl.BlockSpec((ts, 4 * LANE), lambda i: (i, CB_CQ // 4)),
                  whole(1, MLA_Q_LORA), whole(1, MLA_KV_LORA), whole(MLA_Q_LORA, W), whole(MLA_KV_LORA, W),
                  whole(MLA_KV_LORA, MLA_HEADS * MLA_V), whole(1, LANE), whole(1, LANE), tab, tab, tab],
        out_specs=[pl.BlockSpec((MLA_HEADS, ts, LANE), lambda i: (0, i, 0)),
                   pl.BlockSpec((MLA_HEADS, ts, LANE), lambda i: (0, i, 0)),
                   pl.BlockSpec((ts, MLA_HEADS * MLA_V), lambda i: (i, 0))],
        out_shape=[jax.ShapeDtypeStruct((MLA_HEADS, T, LANE), BF16), jax.ShapeDtypeStruct((MLA_HEADS, T, LANE), BF16),
                   jax.ShapeDtypeStruct((T, MLA_HEADS * MLA_V), BF16)],
        compiler_params=_cp(),
    )(proj, gq, gkv, wuqp, wkp, wv, gmq, gmk, cos, sa, sb)


def _mla_prep_bwd(dproj, dq, dk, dv, proj, gq, gkv, wuqp, wkp, wv, gmq, gmk, cos, sa, sb, S):
    T = proj.shape[0]
    ts = _tile(S, 256)
    ns = S // ts
    W = MLA_HEADS * LANE

    def body(dpin_ref, dq_ref, dk_ref, dv_ref, p_ref, gq_ref, gkv_ref, wuq_ref, wk_ref, wv_ref, gmq_ref, gmk_ref,
             cos_ref, sa_ref, sb_ref,
             dp_ref, dwuq_ref, dwk_ref, dwv_ref, dgq_ref, dgkv_ref, dgmq_ref, dgmk_ref, dq0_ref, dkn_ref):
        del dpin_ref

        @pl.when(pl.program_id(0) == 0)
        def _():
            for r in (dwuq_ref, dwk_ref, dwv_ref, dgq_ref, dgkv_ref, dgmq_ref, dgmk_ref):
                r[...] = jnp.zeros_like(r)

        cq = p_ref[:, 0:2 * LANE].astype(F32)
        ckv = p_ref[:, 2 * LANE:3 * LANE].astype(F32)
        kpe = pltpu.roll(p_ref[:, 3 * LANE:4 * LANE].astype(F32), HALF, axis=1)
        rq = _rstd(cq, MLA_Q_LORA)
        rkv = _rstd(ckv, MLA_KV_LORA)
        gq, gkv, gmq, gmk = gq_ref[...], gkv_ref[...], gmq_ref[...], gmk_ref[...]
        cqn = (cq * rq * gq).astype(BF16)
        ckn = (ckv * rkv * gkv).astype(BF16)
        q0 = _mm(cqn, wuq_ref[...])
        kn = _mm(ckn, wk_ref[...])
        c, a, b = cos_ref[...], sa_ref[...], sb_ref[...]
        lane = lax.broadcasted_iota(jnp.int32, (ts, LANE), 1)
        dgmq = jnp.zeros((1, LANE), F32)
        dgmk = jnp.zeros((1, LANE), F32)
        nope = lane < MLA_NOPE
        kpe_rot = _rope(kpe * gmk, c, a, b)
        dk_sum = jnp.zeros((ts, LANE), F32)
        back = jnp.zeros((ts, 1), F32)
        for h in range(MLA_HEADS):
            q0h = q0[:, h * LANE:(h + 1) * LANE]
            r = _rstd(q0h, MLA_QK)
            d1 = _rope_t(dq_ref[h], c, a, b)
            gy = d1 * gmq
            dq0_ref[:, h * LANE:(h + 1) * LANE] = (
                r * gy - q0h * (r * r * r) * (jnp.sum(q0h * gy, axis=-1, keepdims=True) * (1.0 / MLA_QK))).astype(BF16)
            dgmq = dgmq + jnp.sum(d1 * q0h * r, axis=0, keepdims=True)
            knh = kn[:, h * LANE:(h + 1) * LANE]
            dkh = dk_ref[h]
            r = _rstd(knh + kpe, MLA_QK)
            r3dot = (r * r * r) * (jnp.sum((knh * gmk + kpe_rot) * dkh, axis=-1, keepdims=True) * (1.0 / MLA_QK))
            dkn_ref[:, h * LANE:(h + 1) * LANE] = jnp.where(nope, r * gmk * dkh - knh * r3dot, 0.0).astype(BF16)
            dgmk = dgmk + jnp.sum(jnp.where(nope, dkh * knh * r, 0.0), axis=0, keepdims=True)
            dk_sum = dk_sum + r * dkh
            back = back + r3dot
        rot = jnp.where(nope | (lane >= MLA_QK), 0.0, _rope_t(dk_sum, c, a, b))
        dkpe = gmk * rot - kpe * back
        dgmk = dgmk + jnp.sum(kpe * rot, axis=0, keepdims=True)
        dq0 = dq0_ref[...]
        dkn = dkn_ref[...]
        dvv = dv_ref[...]
        dwuq_ref[...] += _mm_tn(cqn, dq0)
        dwk_ref[...] += _mm_tn(ckn, dkn)
        dwv_ref[...] += _mm_tn(ckn, dvv)
        dgmq_ref[...] += dgmq
        dgmk_ref[...] += dgmk
        dcqn = _mm_nt(dq0, wuq_ref[...])
        gy = dcqn * gq
        dp_ref[:, 0:2 * LANE] = (
            rq * gy - cq * (rq * rq * rq) * (jnp.sum(cq * gy, axis=-1, keepdims=True) * (1.0 / MLA_Q_LORA))).astype(BF16)
        dgq_ref[...] += jnp.sum(dcqn * cq * rq, axis=0, keepdims=True)
        dckn = _mm_nt(dkn, wk_ref[...]) + _mm_nt(dvv, wv_ref[...])
        gy = dckn * gkv
        dp_ref[:, 2 * LANE:3 * LANE] = (
            rkv * gy - ckv * (rkv * rkv * rkv) * (jnp.sum(ckv * gy, axis=-1, keepdims=True) * (1.0 / MLA_KV_LORA))).astype(BF16)
        dgkv_ref[...] += jnp.sum(dckn * ckv * rkv, axis=0, keepdims=True)
        dp_ref[:, 3 * LANE:4 * LANE] = pltpu.roll(dkpe, HALF, axis=1).astype(BF16)

    def whole(r, c):
        return pl.BlockSpec((r, c), lambda i: (0, 0))

    tab = pl.BlockSpec((ts, LANE), lambda i: (i % ns, 0))
    heads = pl.BlockSpec((MLA_HEADS, ts, LANE), lambda i: (0, i, 0))
    return pl.pallas_call(
        body, name="mla_prep_bwd", grid=(T // ts,),
        in_specs=[pl.BlockSpec(memory_space=pl.ANY), heads, heads,
                  pl.BlockSpec((ts, MLA_HEADS * MLA_V), lambda i: (i, 0)),
                  pl.BlockSpec((ts, 4 * LANE), lambda i: (i, CB_CQ // 4)),
                  whole(1, MLA_Q_LORA), whole(1, MLA_KV_LORA), whole(MLA_Q_LORA, W), whole(MLA_KV_LORA, W),
                  whole(MLA_KV_LORA, MLA_HEADS * MLA_V), whole(1, LANE), whole(1, LANE), tab, tab, tab],
        out_specs=[pl.BlockSpec((ts, 4 * LANE), lambda i: (i, CB_CQ // 4)),
                   whole(MLA_Q_LORA, W), whole(MLA_KV_LORA, W), whole(MLA_KV_LORA, MLA_HEADS * MLA_V),
                   whole(1, MLA_Q_LORA), whole(1, MLA_KV_LORA), whole(1, LANE), whole(1, LANE)],
        out_shape=[jax.ShapeDtypeStruct(dproj.shape, BF16),
                   jax.ShapeDtypeStruct((MLA_Q_LORA, W), F32), jax.ShapeDtypeStruct((MLA_KV_LORA, W), F32),
                   jax.ShapeDtypeStruct((MLA_KV_LORA, MLA_HEADS * MLA_V), F32),
                   jax.ShapeDtypeStruct((1, MLA_Q_LORA), F32), jax.ShapeDtypeStruct((1, MLA_KV_LORA), F32),
                   jax.ShapeDtypeStruct((1, LANE), F32), jax.ShapeDtypeStruct((1, LANE), F32)],
        scratch_shapes=[pltpu.VMEM((ts, W), BF16), pltpu.VMEM((ts, W), BF16)],
        input_output_aliases={0: 0},
        compiler_params=_cp(),
    )(dproj, dq, dk, dv, proj, gq, gkv, wuqp, wkp, wv, gmq, gmk, cos, sa, sb)


def _dil_prep_fwd(proj, gq, gk):
    T = proj.shape[0]
    ts = _tile(T, 512)

    def body(pq_ref, pk_ref, gq_ref, gk_ref, q_ref, k_ref):
        for c in range(NPAIR):
            cs = slice(c * LANE, (c + 1) * LANE)
            t = jnp.concatenate([pq_ref[:, cs], pk_ref[:, cs]], axis=1).astype(F32)
            y = t * lax.rsqrt(_head_bcast_sum(t * t, terms=2) * (1.0 / DIL_HEAD_DIM) + EPS)
            q_ref[:, cs] = (y[:, 0:LANE] * gq_ref[:, cs]).astype(BF16)
            k_ref[:, cs] = (y[:, LANE:2 * LANE] * gk_ref[:, cs]).astype(BF16)

    col = pl.BlockSpec((1, DIL_WIDTH), lambda i, g: (0, g))
    out = pl.BlockSpec((ts, DIL_WIDTH), lambda i, g: (i, g))
    seg = lambda c0: pl.BlockSpec((ts, DIL_WIDTH), lambda i, g: (i, c0 // NPAIR + g))
    return pl.pallas_call(
        body, name="dil_prep_fwd", grid=(T // ts, DIL_GROUPS),
        in_specs=[seg(CB_DQ), seg(CB_DK), col, col],
        out_specs=[out, out],
        out_shape=[jax.ShapeDtypeStruct((T, DIL_QK), BF16)] * 2,
        compiler_params=_cp(),
    )(proj, proj, gq, gk)


def _dil_prep_bwd(dproj, ddq, ddk, ddv, proj, gq, gk):
    T = proj.shape[0]
    ts = _tile(T, 512)
    nt = T // ts

    def body(dpin_ref, ddq_ref, ddk_ref, ddv_ref, pq_ref, pk_ref, gq_ref, gk_ref, dp_ref, dgq_ref, dgk_ref,
             stage, sems):
        del dpin_ref
        g, i = pl.program_id(0), pl.program_id(1)

        @pl.when(i == 0)
        def _():
            dgq_ref[...] = jnp.zeros_like(dgq_ref)
            dgk_ref[...] = jnp.zeros_like(dgk_ref)

        def fill(slot):
            stage[slot, 2] = ddv_ref[...].astype(BF16)
            for c in range(NPAIR):
                cs = slice(c * LANE, (c + 1) * LANE)
                t = jnp.concatenate([pq_ref[:, cs], pk_ref[:, cs]], axis=1).astype(F32)
                d = jnp.concatenate([ddq_ref[:, cs], ddk_ref[:, cs]], axis=1)
                gy = d * jnp.concatenate([gq_ref[:, cs], gk_ref[:, cs]], axis=1)
                r = lax.rsqrt(_head_bcast_sum(t * t, terms=2) * (1.0 / DIL_HEAD_DIM) + EPS)
                dot = _head_bcast_sum(t * gy, terms=2) * (1.0 / DIL_HEAD_DIM)
                dx = (r * gy - t * (r * r * r) * dot).astype(BF16)
                stage[slot, 0, :, cs] = dx[:, 0:LANE]
                stage[slot, 1, :, cs] = dx[:, LANE:2 * LANE]
                part = jnp.sum(d * t * r, axis=0, keepdims=True)
                dgq_ref[:, cs] += part[:, 0:LANE]
                dgk_ref[:, cs] += part[:, LANE:2 * LANE]

        def copies_of(step):
            sg, si = step // nt, step % nt
            return _put_copies([stage.at[:, k] for k in range(3)], dp_ref, sems, step % 2,
                               pl.ds(pl.multiple_of(si * ts, ts), ts),
                               [pl.multiple_of((c0 + NPAIR * sg) * LANE, LANE) for c0 in (CB_DQ, CB_DK, CB_DV)])

        _put_pipeline(g * nt + i, DIL_GROUPS * nt, copies_of, fill)

    col = pl.BlockSpec((1, DIL_WIDTH), lambda g, i: (0, g))
    tok = pl.BlockSpec((ts, DIL_WIDTH), lambda g, i: (i, g))
    seg = lambda c0: pl.BlockSpec((ts, DIL_WIDTH), lambda g, i: (i, c0 // NPAIR + g))
    return pl.pallas_call(
        body, name="dil_prep_bwd", grid=(DIL_GROUPS, nt),
        in_specs=[pl.BlockSpec(memory_space=pl.ANY), tok, tok, tok, seg(CB_DQ), seg(CB_DK), col, col],
        out_specs=[pl.BlockSpec(memory_space=pl.ANY), col, col],
        out_shape=[jax.ShapeDtypeStruct(dproj.shape, BF16), jax.ShapeDtypeStruct((1, DIL_QK), F32),
                   jax.ShapeDtypeStruct((1, DIL_QK), F32)],
        scratch_shapes=[pltpu.VMEM((2, 3, ts, DIL_WIDTH), BF16), pltpu.SemaphoreType.DMA((2, 3))],
        input_output_aliases={0: 0},
        compiler_params=_cp(),
    )(dproj, ddq, ddk, ddv, proj, proj, gq, gk)


COPY_ROWS = 256


def _to_classes(src_ref, dst_ref, d, L, scale=None):
    m = min(L, max(8, COPY_ROWS // d))
    for c0 in range(0, L, m):
        x = src_ref[c0 * d:(c0 + m) * d, :].astype(F32)
        if scale is not None:
            x = x * scale
        if d > 1:
            x = jnp.swapaxes(x.reshape(m, d, LANE), 0, 1)
        for r in range(d):
            dst_ref[r * L + c0:r * L + c0 + m, :] = (x[r] if d > 1 else x).astype(dst_ref.dtype)


def _from_classes(src_ref, dst_ref, d, L):
    n = min(L, COPY_ROWS)
    for r in range(d):
        for c0 in range(0, L, n):
            rows = pl.ds(r + c0 * d, n, stride=d) if d > 1 else pl.ds(c0, n)
            dst_ref[rows, :] = src_ref[r * L + c0:r * L + c0 + n, :].astype(dst_ref.dtype)


MLA_TQ, MLA_TK = 512, 512


def _causal_bias(tq, tk, shift):
    row = lax.broadcasted_iota(jnp.int32, (tq, tk), 0)
    col = lax.broadcasted_iota(jnp.int32, (tq, tk), 1)
    return jnp.where(row >= col + shift, 0.0, NEG)


def _mla_specs(S):
    heads = pl.BlockSpec((2, S, LANE), lambda b, j: (j, b, 0))
    pair = pl.BlockSpec((S, LANE), lambda b, j: (b, j))
    return heads, pair


def _mla_attn_fwd(q, k, v, B, S):
    tq = _tile(S, MLA_TQ)
    tk = _tile(tq, MLA_TK)
    nd = tq // tk
    scale = MLA_QK ** -0.5
    heads, pair = _mla_specs(S)

    def body(q_ref, k_ref, v_ref, o_ref, lse_ref):
        lo, lok = _lane_lo((tq, LANE)), _lane_lo((tk, LANE))
        diag = [_causal_bias(tq, tk, i * tk) for i in range(nd)]

        def block(g, _):
            row0 = pl.multiple_of(g * tq, tq)
            rows = pl.ds(row0, tq)
            qs = [q_ref[hh, rows, :] for hh in range(2)]

            one = jnp.ones((), BF16)

            def step(off, carries, bias):
                off = pl.multiple_of(off, tk)
                vt = v_ref[pl.ds(off, tk), :]
                vh = (jnp.where(lok, vt, one), jnp.where(lok, one, vt))
                out = []
                for hh, (m, acc) in enumerate(carries):
                    s = _mm_nt(qs[hh], k_ref[hh, pl.ds(off, tk), :]) * scale
                    if bias is not None:
                        s = s + bias
                    m_new = jnp.maximum(m, jnp.max(s, axis=-1, keepdims=True))
                    p = jnp.exp(s - m_new)
                    out.append((m_new, jnp.exp(m - m_new) * acc + _mm(p, vh[hh])))
                return tuple(out)

            init = (jnp.full((tq, 1), NEG, F32), jnp.zeros((tq, LANE), F32))
            carries = lax.fori_loop(0, g * nd, lambda i, c: step(i * tk, c, None), (init, init))
            for i in range(nd):
                carries = step(row0 + i * tk, carries, diag[i])
            (ma, acca), (mb, accb) = carries
            la, lb = pltpu.roll(acca, HALF, axis=1), pltpu.roll(accb, HALF, axis=1)
            o_ref[rows, :] = jnp.where(lo, acca / la, accb / lb)
            lse_ref[rows, :] = jnp.where(lo, ma + jnp.log(la), mb + jnp.log(lb))
            return 0

        lax.fori_loop(0, S // tq, block, 0)

    return pl.pallas_call(
        body, name="mla_attn_fwd", grid=(B, NPAIR), in_specs=[heads, heads, pair], out_specs=[pair, pair],
        out_shape=[jax.ShapeDtypeStruct((B * S, MLA_HEADS * MLA_V), F32)] * 2,
        compiler_params=_cp(),
    )(q, k, v)


DIL_UNROLL = 16


def _dil_geometry(gi, S):
    span, d = DIL_PATTERNS[gi]
    L = S // d
    t = _tile(L, 128)
    window = span // d
    back = min(-(-window // t) * t, L - t)
    return d, L, t, window, back


def _dil_specs(gi, S):
    qk = pl.BlockSpec((S, LANE), lambda b, j: (b, NPAIR * gi + j))
    v = pl.BlockSpec((S, LANE), lambda b, j: (b, CB_DV + NPAIR * gi + j))
    pair = pl.BlockSpec((S, LANE), lambda b, j: (b, j))
    return qk, v, pair


def _dil_bias(bias_ref, sl_ref, j, t, kw, back, window):
    row = lax.broadcasted_iota(jnp.int32, (2 * t, kw), 0)
    col = lax.broadcasted_iota(jnp.int32, (2 * t, kw), 1)
    second = row >= t
    slope = jnp.where(second, sl_ref[j, 1], sl_ref[j, 0])
    for n in range(bias_ref.shape[0]):
        dist = jnp.where(second, row - t, row) + n * back - col
        bias_ref[n] = jnp.where((dist >= 0) & (dist <= window), -slope * dist.astype(F32), NEG)


def _stack_heads(x, lo):
    zero = jnp.zeros((), x.dtype)
    return jnp.concatenate([jnp.where(lo, x, zero), jnp.where(lo, zero, x)], axis=0)


def _dil_attn_fwd(gi, slopes, qn, kn, proj, B, S):
    d, L, t, window, back = _dil_geometry(gi, S)
    kw, nq = back + t, L // t
    nbias = 2 if back else 1
    qk, vspec, pair = _dil_specs(gi, S)

    def body(sl_ref, q_ref, k_ref, v_ref, o_ref, lse_ref, qs, ks, vs, os_, ls, bias_ref):
        _to_classes(q_ref, qs, d, L, DIL_HEAD_DIM ** -0.5)
        _to_classes(k_ref, ks, d, L)
        _to_classes(v_ref, vs, d, L)
        _dil_bias(bias_ref, sl_ref, pl.program_id(1), t, kw, back, window)
        lo = _lane_lo((t, LANE))

        def block(g, _):
            qb = g % nq if d > 1 else g
            row0 = pl.multiple_of(g * t, t)
            rows = pl.ds(row0, t)
            early = qb * t < back
            keys = pl.ds(pl.multiple_of(jnp.where(early, row0 - qb * t, row0 - back), t), kw)
            s = _mm_nt(_stack_heads(qs[rows, :], lo), ks[keys, :]) + bias_ref[jnp.where(early, 0, nbias - 1)]
            m = jnp.max(s, axis=-1, keepdims=True)
            p = jnp.exp(s - m)
            l = jnp.sum(p, axis=-1, keepdims=True)
            o2 = _mm(p, vs[keys, :]) / l
            lse2 = m + jnp.log(l)
            os_[rows, :] = jnp.where(lo, o2[:t], o2[t:])
            ls[rows, :] = jnp.where(lo, lse2[:t], lse2[t:])
            return 0

        lax.fori_loop(0, d * nq, block, 0, unroll=DIL_UNROLL if d * nq % DIL_UNROLL == 0 else 1)
        _from_classes(os_, o_ref, d, L)
        _from_classes(ls, lse_ref, d, L)

    return pl.pallas_call(
        body, name=f"dil_attn_fwd_{gi}", grid=(B, NPAIR),
        in_specs=[pl.BlockSpec(memory_space=pltpu.SMEM), qk, qk, vspec], out_specs=[pair, pair],
        out_shape=[jax.ShapeDtypeStruct((B * S, DIL_WIDTH), F32)] * 2,
        scratch_shapes=[pltpu.VMEM((S, LANE), BF16)] * 3 + [pltpu.VMEM((S, LANE), F32)] * 2
                       + [pltpu.VMEM((nbias, 2 * t, kw), F32)],
        compiler_params=_cp(),
    )(slopes, qn, kn, proj)


def _mla_attn_bwd(q, k, v, do, lse, delta, B, S):
    T = B * S
    tq = _tile(S, MLA_TQ)
    tk = _tile(tq, MLA_TK)
    nd = tq // tk
    scale = MLA_QK ** -0.5
    heads, pair = _mla_specs(S)

    def body(q_ref, k_ref, v_ref, do_ref, lse_ref, dl_ref, dq_ref, dk_ref, dv_ref):
        dk_ref[...] = jnp.zeros_like(dk_ref)
        dv_ref[...] = jnp.zeros_like(dv_ref)
        lo = _lane_lo((tq, LANE))
        diag = [_causal_bias(tq, tk, i * tk) for i in range(nd)]

        def block(g, _):
            row0 = pl.multiple_of(g * tq, tq)
            rows = pl.ds(row0, tq)
            per_head = []
            for hh in range(2):
                sel = lo if hh == 0 else jnp.logical_not(lo)
                per_head.append((q_ref[hh, rows, :], jnp.where(sel, do_ref[rows, :], jnp.zeros((), BF16)),
                                 jnp.max(jnp.where(sel, lse_ref[rows, :], NEG), axis=-1, keepdims=True),
                                 jnp.max(jnp.where(sel, dl_ref[rows, :], NEG), axis=-1, keepdims=True)))

            def step(off, dq_accs, bias):
                cols = pl.ds(pl.multiple_of(off, tk), tk)
                vt = v_ref[cols, :]
                out, dv = [], None
                for hh, (qh, doh, lse_h, dl_h) in enumerate(per_head):
                    kh = k_ref[hh, cols, :]
                    s = _mm_nt(qh, kh) * scale
                    if bias is not None:
                        s = s + bias
                    p = jnp.exp(s - lse_h)
                    ds = (p * (_mm_nt(doh, vt) - dl_h)).astype(BF16)
                    dk_ref[hh, cols, :] += _mm_tn(ds, qh) * scale
                    part = _mm_tn(p, doh)
                    dv = part if dv is None else dv + part
                    out.append(dq_accs[hh] + _mm(ds, kh))
                dv_ref[cols, :] += dv
                return tuple(out)

            zero = jnp.zeros((tq, LANE), F32)
            dq_accs = lax.fori_loop(0, g * nd, lambda i, a: step(i * tk, a, None), (zero, zero))
            for i in range(nd):
                dq_accs = step(row0 + i * tk, dq_accs, diag[i])
            for hh in range(2):
                dq_ref[hh, rows, :] = dq_accs[hh] * scale
            return 0

        lax.fori_loop(0, S // tq, block, 0)

    return pl.pallas_call(
        body, name="mla_attn_bwd", grid=(B, NPAIR), in_specs=[heads, heads, pair, pair, pair, pair],
        out_specs=[heads, heads, pair],
        out_shape=[jax.ShapeDtypeStruct((MLA_HEADS, T, LANE), F32), jax.ShapeDtypeStruct((MLA_HEADS, T, LANE), F32),
                   jax.ShapeDtypeStruct((T, MLA_HEADS * MLA_V), F32)],
        compiler_params=_cp(),
    )(q, k, v, do, lse, delta)


def _dil_attn_bwd(gi, slopes, qn, kn, proj, do, lse, delta, through, B, S):
    d, L, t, window, back = _dil_geometry(gi, S)
    kw, nq = back + t, L // t
    nbias = 2 if back else 1
    scale = DIL_HEAD_DIM ** -0.5
    qk, vspec, pair = _dil_specs(gi, S)

    def body(*refs):
        refs = list(refs)
        sl_ref, q_ref, k_ref, v_ref, do_ref, lse_ref, dl_ref = refs[:7]
        dq_ref, dk_ref, dv_ref, qs, ks, vs, dos, lss, dls, dqs, dks, dvs, bias_ref = refs[-13:]
        _to_classes(q_ref, qs, d, L, scale)
        for src, dst in ((k_ref, ks), (v_ref, vs), (do_ref, dos), (lse_ref, lss), (dl_ref, dls)):
            _to_classes(src, dst, d, L)
        _dil_bias(bias_ref, sl_ref, pl.program_id(1), t, kw, back, window)
        dks[...] = jnp.zeros_like(dks)
        dvs[...] = jnp.zeros_like(dvs)
        lo = _lane_lo((t, LANE))

        def stats(ref, rows):
            x = ref[rows, :]
            return jnp.concatenate([jnp.max(jnp.where(lo, x, NEG), axis=-1, keepdims=True),
                                    jnp.max(jnp.where(lo, NEG, x), axis=-1, keepdims=True)], axis=0)

        def block(g, _):
            qb = g % nq if d > 1 else g
            row0 = pl.multiple_of(g * t, t)
            rows = pl.ds(row0, t)
            early = qb * t < back
            keys = pl.ds(pl.multiple_of(jnp.where(early, row0 - qb * t, row0 - back), t), kw)
            q2 = _stack_heads(qs[rows, :], lo)
            do2 = _stack_heads(dos[rows, :], lo)
            kt = ks[keys, :]
            s = _mm_nt(q2, kt) + bias_ref[jnp.where(early, 0, nbias - 1)]
            p = jnp.exp(s - stats(lss, rows))
            ds = (p * (_mm_nt(do2, vs[keys, :]) - stats(dls, rows))).astype(BF16)
            dq2 = _mm(ds, kt) * scale
            dqs[rows, :] = jnp.where(lo, dq2[:t], dq2[t:])
            dks[keys, :] += _mm_tn(ds, q2)
            dvs[keys, :] += _mm_tn(p, do2)
            return 0

        lax.fori_loop(0, d * nq, block, 0, unroll=DIL_UNROLL if d * nq % DIL_UNROLL == 0 else 1)
        for src, dst in ((dqs, dq_ref), (dks, dk_ref), (dvs, dv_ref)):
            _from_classes(src, dst, d, L)

    in_specs = [pl.BlockSpec(memory_space=pltpu.SMEM), qk, qk, vspec, pair, pair, pair]
    args = [slopes, qn, kn, proj, do, lse, delta]
    aliases = {}
    if through is not None:
        aliases = {len(args) + i: i for i in range(3)}
        in_specs = in_specs + [pl.BlockSpec(memory_space=pl.ANY)] * 3
        args = args + list(through)
    return pl.pallas_call(
        body, name=f"dil_attn_bwd_{gi}", grid=(B, NPAIR), in_specs=in_specs, out_specs=[qk, qk, qk],
        out_shape=[jax.ShapeDtypeStruct((B * S, DIL_QK), F32)] * 3,
        scratch_shapes=[pltpu.VMEM((S, LANE), BF16)] * 4 + [pltpu.VMEM((S, LANE), F32)] * 5
                       + [pltpu.VMEM((nbias, 2 * t, kw), F32)],
        input_output_aliases=aliases,
        compiler_params=_cp(),
    )(*args)


def _merge_proj_specs(ts):
    wide = lambda c0, w: pl.BlockSpec((ts, w), lambda i: (i, c0 * LANE // w))
    return [wide(CB_BZ, DIL_WIDTH), wide(CB_CZ, DIL_WIDTH)] + [wide(CB_GATE + 8 * i, D_MODEL) for i in range(3)]


def _merge_common(p_refs, bg_ref, ob_ref, og_refs, lse_refs):
    bz = p_refs[0][...].astype(F32)
    cz = p_refs[1][...].astype(F32)
    gates = [_sigmoid(p_refs[2 + i][...].astype(F32) + bg_ref[:, i * D_MODEL:(i + 1) * D_MODEL]) for i in range(3)]
    ob = ob_ref[...]
    lses = [r[...] for r in lse_refs]
    mx = jnp.maximum(jnp.maximum(lses[0], lses[1]), lses[2])
    es = [jnp.exp(v - mx) for v in lses]
    inv = 1.0 / (es[0] + es[1] + es[2])
    alphas = [e * inv for e in es]
    oc = alphas[0] * og_refs[0][...] + alphas[1] * og_refs[1][...] + alphas[2] * og_refs[2][...]
    return bz, cz, gates, ob, alphas, oc


def _merge_fwd(x, proj, b_gate, ya, ob, ogs, lses, woa, wob, woc, wo):
    T = x.shape[0]
    ts = _tile(T, 256)

    def body(x_ref, p0, p1, p2, p3, p4, bg_ref, ya_ref, ob_ref, og0, og1, og2, l0, l1, l2,
             woa_ref, wob_ref, woc_ref, wo_ref, out_ref):
        bz, cz, gates, obv, alphas, oc = _merge_common((p0, p1, p2, p3, p4), bg_ref, ob_ref, (og0, og1, og2),
                                                       (l0, l1, l2))
        yb = obv * _silu(bz)
        yc = oc * _silu(cz)
        merged = (gates[0] * _mm(ya_ref[...], woa_ref[...]) + gates[1] * _mm(yb, wob_ref[...])
                  + gates[2] * _mm(yc, woc_ref[...]))
        out_ref[...] = x_ref[...] + _mm(merged, wo_ref[...])

    def whole(r, c):
        return pl.BlockSpec((r, c), lambda i: (0, 0))

    tok = lambda w: pl.BlockSpec((ts, w), lambda i: (i, 0))
    return pl.pallas_call(
        body, name="merge_fwd", grid=(T // ts,),
        in_specs=[tok(D_MODEL)] + _merge_proj_specs(ts) + [whole(1, 3 * D_MODEL), tok(CONV_WIDTH)]
                 + [tok(DIL_WIDTH)] * 7 + [whole(CONV_WIDTH, D_MODEL)] * 3 + [whole(D_MODEL, D_MODEL)],
        out_specs=tok(D_MODEL),
        out_shape=jax.ShapeDtypeStruct((T, D_MODEL), F32),
        compiler_params=_cp(),
    )(x, *[proj] * 5, b_gate, ya, ob, *ogs, *lses, woa, wob, woc, wo)


def _merge_bwd(dout, proj, b_gate, ya, ob, ogs, lses, woa, wob, woc, wo):
    T = dout.shape[0]
    ts = _tile(T, 256)
    nt = T // ts

    def body(do_ref, p0, p1, p2, p3, p4, bg_ref, ya_ref, ob_ref, og0, og1, og2, l0, l1, l2,
             woa_ref, wob_ref, woc_ref, wo_ref,
             dp_ref, dya_ref, dob_ref, dlb_ref, dg0, dg1, dg2, dl0, dl1, dl2,
             mg_ref, dpa_ref, dpb_ref, dpc_ref, yb_ref, yc_ref, dbg_ref, st_bz, st_cz, st_gate, sems):
        step = pl.program_id(0)
        slot = step % 2

        def copies_of(s):
            return _put_copies([st_bz, st_cz, st_gate], dp_ref, sems, s % 2, pl.ds(pl.multiple_of(s * ts, ts), ts),
                               [CB_BZ * LANE, CB_CZ * LANE, CB_GATE * LANE])

        @pl.when(step >= 2)
        def _():
            for cp in copies_of(step - 2):
                cp.wait()

        bz, cz, gates, obv, alphas, oc = _merge_common((p0, p1, p2, p3, p4), bg_ref, ob_ref, (og0, og1, og2),
                                                       (l0, l1, l2))
        (sb, dsb), (sc, dsc) = _silu_and_grad(bz), _silu_and_grad(cz)
        yb = obv * sb
        yc = oc * sc
        ps = [_mm(ya_ref[...], woa_ref[...]), _mm(yb, wob_ref[...]), _mm(yc, woc_ref[...])]
        mg_ref[...] = (gates[0] * ps[0] + gates[1] * ps[1] + gates[2] * ps[2]).astype(BF16)
        yb_ref[...] = yb.astype(BF16)
        yc_ref[...] = yc.astype(BF16)
        dm = _mm_nt(do_ref[...], wo_ref[...])
        dps = []
        first = pl.program_id(0) == 0
        for i, dref in enumerate((dpa_ref, dpb_ref, dpc_ref)):
            g = gates[i]
            dpi = (dm * g).astype(BF16)
            dref[...] = dpi
            dps.append(dpi)
            dgp = dm * ps[i] * g * (1.0 - g)
            st_gate[slot, :, i * D_MODEL:(i + 1) * D_MODEL] = dgp.astype(BF16)
            part = jnp.sum(dgp, axis=0, keepdims=True)

            @pl.when(first)
            def _():
                dbg_ref[:, i * D_MODEL:(i + 1) * D_MODEL] = part

            @pl.when(jnp.logical_not(first))
            def _():
                dbg_ref[:, i * D_MODEL:(i + 1) * D_MODEL] += part

        dya_ref[...] = _mm_nt(dps[0], woa_ref[...])
        dyb = _mm_nt(dps[1], wob_ref[...])
        dyc = _mm_nt(dps[2], woc_ref[...])
        st_bz[slot] = (dyb * obv * dsb).astype(BF16)
        st_cz[slot] = (dyc * oc * dsc).astype(BF16)
        for cp in copies_of(step):
            cp.start()
        dob = dyb * sb
        doc = dyc * sc
        dob_ref[...] = dob.astype(BF16)
        for c in range(NPAIR):
            cs = slice(c * LANE, (c + 1) * LANE)
            dlb_ref[:, cs] = _head_bcast_sum(dob[:, cs] * obv[:, cs])
            dd = _head_bcast_sum(doc[:, cs] * oc[:, cs])
            for a, dref, lref in zip(alphas, (dg0, dg1, dg2), (dl0, dl1, dl2)):
                dref[:, cs] = (a[:, cs] * doc[:, cs]).astype(BF16)
                lref[:, cs] = a[:, cs] * dd

        @pl.when(step == nt - 1)
        def _():
            if nt >= 2:
                for cp in copies_of(step - 1):
                    cp.wait()
            for cp in copies_of(step):
                cp.wait()

    def whole(r, c):
        return pl.BlockSpec((r, c), lambda i: (0, 0))

    tok = lambda w: pl.BlockSpec((ts, w), lambda i: (i, 0))
    sd = jax.ShapeDtypeStruct
    W = DIL_WIDTH
    return pl.pallas_call(
        body, name="merge_bwd", grid=(nt,),
        in_specs=[tok(D_MODEL)] + _merge_proj_specs(ts) + [whole(1, 3 * D_MODEL), tok(CONV_WIDTH)] + [tok(W)] * 7
                 + [whole(CONV_WIDTH, D_MODEL)] * 3 + [whole(D_MODEL, D_MODEL)],
        out_specs=[pl.BlockSpec(memory_space=pl.ANY), tok(CONV_WIDTH), tok(W), tok(W)] + [tok(W)] * 6
                  + [tok(D_MODEL)] * 4 + [tok(W), tok(W), whole(1, 3 * D_MODEL)],
        out_shape=[sd((T, PP), BF16), sd((T, CONV_WIDTH), F32), sd((T, W), BF16), sd((T, W), F32)]
                  + [sd((T, W), BF16)] * 3 + [sd((T, W), F32)] * 3
                  + [sd((T, D_MODEL), BF16)] * 4 + [sd((T, W), BF16)] * 2 + [sd((1, 3 * D_MODEL), F32)],
        scratch_shapes=[pltpu.VMEM((2, ts, W), BF16), pltpu.VMEM((2, ts, W), BF16),
                        pltpu.VMEM((2, ts, 3 * D_MODEL), BF16), pltpu.SemaphoreType.DMA((2, 3))],
        compiler_params=_cp(),
    )(dout, *[proj] * 5, b_gate, ya, ob, *ogs, *lses, woa, wob, woc, wo)


def _loss_head(y, target):
    T = y.shape[0]
    ts = _tile(T, 512)

    def body(y_ref, t_ref, d_ref, l_ref):
        e = y_ref[...] - t_ref[...]
        d_ref[...] = e * (1.0 / D_MODEL)
        l_ref[...] = jnp.zeros((1, 8, LANE), F32) + jnp.sum(e * e)

    tok = pl.BlockSpec((ts, D_MODEL), lambda i: (i, 0))
    return pl.pallas_call(
        body, name="loss_head", grid=(T // ts,), in_specs=[tok, tok],
        out_specs=[tok, pl.BlockSpec((1, 8, LANE), lambda i: (i, 0, 0))],
        out_shape=[jax.ShapeDtypeStruct((T, D_MODEL), F32), jax.ShapeDtypeStruct((T // ts, 8, LANE), F32)],
        compiler_params=_cp(),
    )(y, target)


def _my_index():
    return 4 * lax.axis_index("x") + 2 * lax.axis_index("y") + lax.axis_index("c")


def _peers():
    x, y, c = (lax.axis_index(a) for a in AXES)
    out = []
    for kk in range(1, N_DEV):
        px = 1 - x if kk & 4 else x
        py = 1 - y if kk & 2 else y
        pc = 1 - c if kk & 1 else c
        out.append(((px, py, pc), 4 * px + 2 * py + pc))
    return out


def _exchange(arrays, name, gather):
    n = len(arrays)

    def body(*refs):
        srcs, outs = refs[:n], refs[n:2 * n]
        send_sems, recv_sems, local_sems = refs[2 * n:]
        me = _my_index()
        peers = _peers()
        started = []
        for a, (src, out) in enumerate(zip(srcs, outs)):
            mine = pltpu.make_async_copy(src if gather else src.at[me], out.at[me], local_sems.at[a])
            mine.start()
            started.append(mine)
        sends = []
        for i, (pos, idx) in enumerate(peers):
            for a, (src, out) in enumerate(zip(srcs, outs)):
                cp = pltpu.make_async_remote_copy(
                    src_ref=src if gather else src.at[idx], dst_ref=out.at[me], send_sem=send_sems.at[a, i],
                    recv_sem=recv_sems.at[a, i], device_id=pos, device_id_type=pl.DeviceIdType.MESH)
                cp.start()
                sends.append(cp)
        for i, (pos, idx) in enumerate(peers):
            for a, (src, out) in enumerate(zip(srcs, outs)):
                pltpu.make_async_remote_copy(
                    src_ref=src if gather else src.at[idx], dst_ref=out.at[idx], send_sem=send_sems.at[a, i],
                    recv_sem=recv_sems.at[a, i], device_id=pos, device_id_type=pl.DeviceIdType.MESH).wait_recv()
        for cp in sends:
            cp.wait_send()
        for mine in started:
            mine.wait()

    any_space = pl.BlockSpec(memory_space=pl.ANY)
    return pl.pallas_call(
        body, name=name, in_specs=[any_space] * n, out_specs=[any_space] * n,
        out_shape=[jax.ShapeDtypeStruct(((N_DEV,) + a.shape) if gather else a.shape, a.dtype) for a in arrays],
        scratch_shapes=[pltpu.SemaphoreType.DMA((n, N_DEV - 1)), pltpu.SemaphoreType.DMA((n, N_DEV - 1)),
                        pltpu.SemaphoreType.DMA((n,))],
    )(*arrays)


N_CHIP = 4


def _chip_places():
    x, y, c = (lax.axis_index(a) for a in AXES)
    return (x, y, c), (x, y, 1 - c), [(1 - x, y, c), (x, 1 - y, c), (1 - x, 1 - y, c)]


def _index_of(pos):
    return 4 * pos[0] + 2 * pos[1] + pos[2]


def _sibling_swap(arrays, name):
    n = len(arrays)

    def body(*refs):
        srcs, outs = refs[:n], refs[n:2 * n]
        send_sems, recv_sems = refs[2 * n:]
        (x, y, c), sibling, _ = _chip_places()
        sends = []
        for a, (src, out) in enumerate(zip(srcs, outs)):
            for q in range(N_CHIP):
                def copy(core, a=a, q=q, src=src, out=out):
                    return pltpu.make_async_remote_copy(
                        src_ref=src.at[2 * q + core], dst_ref=out.at[q], send_sem=send_sems.at[N_CHIP * a + q],
                        recv_sem=recv_sems.at[N_CHIP * a + q], device_id=sibling, device_id_type=pl.DeviceIdType.MESH)
                mine = copy(1 - c)
                mine.start()
                sends.append((mine, copy(c)))
        for mine, arrival in sends:
            arrival.wait_recv()
            mine.wait_send()

    any_space = pl.BlockSpec(memory_space=pl.ANY)
    return pl.pallas_call(
        body, name=name, in_specs=[any_space] * n, out_specs=[any_space] * n,
        out_shape=[jax.ShapeDtypeStruct((N_CHIP,) + a.shape[1:], a.dtype) for a in arrays],
        scratch_shapes=[pltpu.SemaphoreType.DMA((N_CHIP * n,)), pltpu.SemaphoreType.DMA((N_CHIP * n,))],
    )(*arrays)


def _chip_pair_sum(part, got, name):
    R, C = part.shape[1:]
    tr = R
    while tr * C * part.dtype.itemsize > REDUCE_BLOCK_BYTES // 4 and tr % 32 == 0:
        tr //= 2
    c = lax.axis_index("c")

    def body(c_ref, p_ref, g_ref, o_ref):
        del c_ref
        o_ref[...] = (p_ref[...].astype(F32) + g_ref[...].astype(F32)).astype(o_ref.dtype)

    return pl.pallas_call(
        body, name=name, grid_spec=pltpu.PrefetchScalarGridSpec(
            num_scalar_prefetch=1, grid=(N_CHIP, R // tr),
            in_specs=[pl.BlockSpec((None, tr, C), lambda q, i, cr: (2 * q + cr[0], i, 0)),
                      pl.BlockSpec((None, tr, C), lambda q, i, cr: (q, i, 0))],
            out_specs=pl.BlockSpec((None, tr, C), lambda q, i, cr: (q, i, 0))),
        out_shape=jax.ShapeDtypeStruct((N_CHIP, R, C), part.dtype),
        compiler_params=_cp(),
    )(jnp.reshape(c, (1,)).astype(jnp.int32), part, got)


def _peer_count(mode):
    return {"chips": N_CHIP - 1, "near": N_CHIP}.get(mode, N_DEV - 1)


def _remote_copies(srcs, lands, send_sems, recv_sems, mode):
    if mode == "chips":
        (x, y, _), _, others = _chip_places()
        my_slot, peers = 2 * x + y, [(chip, 2 * chip[0] + chip[1]) for chip in others]
    elif mode == "near":
        me, sibling, others = _chip_places()
        my_slot, peers = _index_of(me), [(pos, _index_of(pos)) for pos in [sibling] + others]
    else:
        my_slot, peers = _my_index(), _peers()
    whole = mode in ("gather", "near")
    out = []
    for i, (pos, idx) in enumerate(peers):
        for a, (src, land) in enumerate(zip(srcs, lands)):
            def copy(slot, a=a, src=src, land=land, i=i, pos=pos, idx=idx):
                return pltpu.make_async_remote_copy(
                    src_ref=src if whole else src.at[idx], dst_ref=land.at[slot],
                    send_sem=send_sems.at[a * len(peers) + i], recv_sem=recv_sems.at[a * len(peers) + i],
                    device_id=pos, device_id_type=pl.DeviceIdType.MESH)
            out.append((copy(my_slot), copy(idx)))
    return out


def _exchange_start(arrays, name, mode):
    n = len(arrays)
    hbm = pl.BlockSpec(memory_space=pltpu.HBM)
    sem = pl.BlockSpec(memory_space=pltpu.SEMAPHORE)
    lands = [lax.empty(((N_DEV,) + a.shape) if mode in ("gather", "near") else a.shape, a.dtype) for a in arrays]

    def body(*refs):
        srcs, lands_ = refs[:n], refs[n:2 * n]
        send_sems, recv_sems = refs[2 * n:2 * n + 2]
        for mine, _ in _remote_copies(srcs, lands_, send_sems, recv_sems, mode):
            mine.start()
        refs[-1][...] = jnp.zeros_like(refs[-1])

    sems = pltpu.SemaphoreType.DMA((n * _peer_count(mode),))
    buffers = [pltpu.HBM(a.shape, a.dtype) for a in list(arrays) + lands]
    res = pl.pallas_call(
        body, name=name, in_specs=[hbm] * (2 * n), out_specs=[sem, sem] + [hbm] * (2 * n) + [pl.BlockSpec(memory_space=pltpu.VMEM)],
        out_shape=[sems, sems] + buffers + [jax.ShapeDtypeStruct((8, LANE), F32)],
        input_output_aliases={i: 2 + i for i in range(2 * n)},
        compiler_params=pltpu.CompilerParams(has_side_effects=pltpu.SideEffectType.DATAFLOW_SIDE_EFFECTING),
    )(*[pltpu.with_memory_space_constraint(a, pltpu.HBM) for a in list(arrays) + lands])
    return (res[0], res[1], res[2:2 + n], res[2 + n:2 + 2 * n]), res[-1]


def _exchange_wait(handle, after, name, mode):
    send_sems, recv_sems, srcs, lands = handle
    n = len(srcs)
    after = list(after) if isinstance(after, (list, tuple)) else [after]
    hbm = pl.BlockSpec(memory_space=pltpu.HBM)
    sem = pl.BlockSpec(memory_space=pltpu.SEMAPHORE)

    def body(*refs):
        for mine, arrival in _remote_copies(refs[:n], refs[n:2 * n], refs[2 * n], refs[2 * n + 1], mode):
            mine.wait_send()
            arrival.wait_recv()

    res = pl.pallas_call(
        body, name=name, in_specs=[hbm] * (2 * n) + [sem, sem] + [pl.BlockSpec(memory_space=pl.ANY)] * len(after),
        out_specs=[hbm] * (2 * n), out_shape=[pltpu.HBM(a.shape, a.dtype) for a in list(srcs) + list(lands)],
        input_output_aliases={i: i for i in range(2 * n)},
        compiler_params=pltpu.CompilerParams(has_side_effects=pltpu.SideEffectType.DATAFLOW_SIDE_EFFECTING),
    )(*srcs, *lands, send_sems, recv_sems, *after)
    return res[n:]


def _sibling_forward(lands, name):
    n = len(lands)

    def body(*refs):
        ins, outs, send_sems, recv_sems = refs[:n], refs[n:2 * n], refs[2 * n], refs[2 * n + 1]
        (x, y, c), sibling, others = _chip_places()
        copies = []
        for a, (src, out) in enumerate(zip(ins, outs)):
            for j, chip in enumerate(others):
                def copy(core, a=a, j=j, chip=chip, src=src, out=out):
                    slot = _index_of((chip[0], chip[1], core))
                    return pltpu.make_async_remote_copy(
                        src_ref=src.at[slot], dst_ref=out.at[slot], send_sem=send_sems.at[3 * a + j],
                        recv_sem=recv_sems.at[3 * a + j], device_id=sibling, device_id_type=pl.DeviceIdType.MESH)
                mine = copy(c)
                mine.start()
                copies.append((mine, copy(1 - c)))
        for mine, arrival in copies:
            arrival.wait_recv()
        for mine, arrival in copies:
            mine.wait_send()

    any_space = pl.BlockSpec(memory_space=pl.ANY)
    return pl.pallas_call(
        body, name=name, in_specs=[any_space] * n, out_specs=[any_space] * n,
        out_shape=[jax.ShapeDtypeStruct(a.shape, a.dtype) for a in lands],
        scratch_shapes=[pltpu.SemaphoreType.DMA((3 * n,)), pltpu.SemaphoreType.DMA((3 * n,))],
        input_output_aliases={i: i for i in range(n)},
    )(*lands)


def _own_slot(land, mine, slot=None):
    slot = _my_index() if slot is None else slot
    return lax.dynamic_update_slice(land, mine, (slot,) + (0,) * (land.ndim - 1))


def _adamw(w, g, m, v):
    m = ADAM_B1 * m + (1.0 - ADAM_B1) * g
    v = ADAM_B2 * v + (1.0 - ADAM_B2) * (g * g)
    m_hat = m / (1.0 - ADAM_B1 ** ADAM_STEP)
    v_hat = v / (1.0 - ADAM_B2 ** ADAM_STEP)
    delta = -ADAM_LR * (m_hat / (jnp.sqrt(v_hat) + ADAM_EPS) + ADAM_WD * w)
    return delta, m, v


def _reduce_adamw(parts, w, m, v, name, after=None):
    nparts = len(parts)
    R, C = parts[0].shape[1:]
    tr = R
    while N_DEV * tr * C * parts[0].dtype.itemsize > REDUCE_BLOCK_BYTES and tr % 32 == 0:
        tr //= 2
    steps = R // tr
    extra = [] if after is None else [after]

    def body(*refs):
        w_ref, m_ref, v_ref, g_ref, d_ref, nm_ref, nv_ref = refs[nparts + len(extra):]
        for k, p_ref in enumerate(refs[:nparts]):
            @pl.when(pl.program_id(0) // steps == k)
            def _():
                g = p_ref[0].astype(F32)
                for s in range(1, p_ref.shape[0]):
                    g = g + p_ref[s].astype(F32)
                g_ref[...] = g
                d_ref[...], nm_ref[...], nv_ref[...] = _adamw(w_ref[...], g, m_ref[...], v_ref[...])

    def part_spec(k):
        return pl.BlockSpec((parts[k].shape[0], tr, C), lambda i: (0, jnp.clip(i - k * steps, 0, steps - 1), 0))

    row = pl.BlockSpec((tr, C), lambda i: (i, 0))
    return pl.pallas_call(
        body, name=name, grid=(nparts * steps,),
        in_specs=[part_spec(k) for k in range(nparts)] + [pl.BlockSpec(memory_space=pl.ANY)] * len(extra)
                 + [row, row, row],
        out_specs=[row] * 4, out_shape=[jax.ShapeDtypeStruct((nparts * R, C), F32)] * 4,
        compiler_params=_cp(),
    )(*parts, *extra, w, m, v)


BIG = ("w_in", "w_uq", "w_ukv", "w_out_a", "w_out_b", "w_out_c", "w_o")
SMALL = ("norm_g", "b_gate", "conv_w", "conv_b", "q_a_norm_g", "kv_a_norm_g", "mla_q_norm_g", "mla_k_norm_g",
         "dil_q_norm_g", "dil_k_norm_g")
PACK_ROWS = 128
REDUCE_BLOCK_BYTES = 6 * 1024 * 1024


def _pack_local(tensors):
    flat = jnp.concatenate([t.reshape(-1) for t in tensors])
    pad = (-flat.shape[0]) % (PACK_ROWS * LANE)
    return jnp.concatenate([flat, jnp.zeros((pad,), flat.dtype)]).reshape(-1, LANE)


def _unpack_local(rows, like):
    flat = rows.reshape(-1)
    out, off = [], 0
    for t in like:
        out.append(flat[off:off + t.size].reshape(t.shape))
        off += t.size
    return out


def _cols_to_slots(a):
    k = a.shape[0]
    return a.reshape(k, N_DEV, -1).transpose(1, 0, 2)


def _slots_to_cols(s):
    return s.transpose(1, 0, 2).reshape(s.shape[1], -1)


def _rope_tables(S):
    inv = ROPE_THETA ** (-jnp.arange(0, MLA_ROPE, 2, dtype=F32) / MLA_ROPE)
    ang = jnp.arange(S, dtype=F32)[:, None] * inv[None, :]
    cos, sin = jnp.cos(ang), jnp.sin(ang)
    one = jnp.ones((S, MLA_NOPE), F32)
    z16, z32, z64 = (jnp.zeros((S, n), F32) for n in (16, 32, 64))
    cosp = jnp.concatenate([one, cos, cos, jnp.ones((S, 32), F32)], axis=1)
    sa = jnp.concatenate([z64, -sin, z16, z32], axis=1)
    sb = jnp.concatenate([z64, z16, sin, z32], axis=1)
    return cosp, sa, sb


def _alibi_slopes():
    n = DIL_GROUPS * DIL_HEADS
    m = 2.0 ** (-8.0 * jnp.arange(1, n + 1, dtype=F32) / n)
    return m.reshape(DIL_GROUPS, NPAIR, 2)


def _pad_slots(s):
    n, k, c = s.shape
    return _slots_to_cols(jnp.concatenate([s, jnp.zeros((n, k, LANE - c), s.dtype)], axis=2))


def _layer_params(gw, small, l):
    p = {}
    p["wp"] = _pad_columns(gw["w_in"])
    p["norm_g"] = small["norm_g"][l][None]
    p["b_gate"] = small["b_gate"][l][None]
    p["conv_w"] = gw["conv_w"].transpose(1, 0, 2).reshape(CONV_K, CONV_WIDTH)
    p["conv_b"] = small["conv_b"][l][None]
    p["gq"] = small["q_a_norm_g"][l][None]
    p["gkv"] = small["kv_a_norm_g"][l][None]
    p["wuqp"] = _pad_slots(gw["w_uq"])
    kv = gw["w_ukv"]
    p["wkp"] = _pad_slots(kv[:, :, :MLA_NOPE])
    p["wv"] = kv[:, :, MLA_NOPE:].transpose(1, 0, 2).reshape(MLA_KV_LORA, MLA_HEADS * MLA_V)
    zpad = jnp.zeros((1, LANE - MLA_QK), F32)
    p["gmq"] = jnp.concatenate([small["mla_q_norm_g"][l][None], zpad], axis=1)
    p["gmk"] = jnp.concatenate([small["mla_k_norm_g"][l][None], zpad], axis=1)
    tile = lambda g: jnp.broadcast_to(g[:, None, :], (DIL_GROUPS, DIL_HEADS, DIL_HEAD_DIM)).reshape(1, DIL_QK)
    p["gdq"] = tile(small["dil_q_norm_g"][l])
    p["gdk"] = tile(small["dil_k_norm_g"][l])
    p["woa"], p["wob"], p["woc"] = (_slots_to_cols(gw[n]) for n in ("w_out_a", "w_out_b", "w_out_c"))
    p["wo"] = gw["w_o"].reshape(D_MODEL, D_MODEL)
    return p


def _layer_fwd(x, p, tabs, slopes, B, S):
    proj, ht = _inproj_fwd(x, p["norm_g"], p["wp"])
    ya = _mixa_fwd(proj, p["conv_w"], p["conv_b"], B, S)
    q, k, v = _mla_prep_fwd(proj, p["gq"], p["gkv"], p["wuqp"], p["wkp"], p["wv"], p["gmq"], p["gmk"], *tabs, S)
    ob, lse_b = _mla_attn_fwd(q, k, v, B, S)
    qn, kn = _dil_prep_fwd(proj, p["gdq"], p["gdk"])
    ogs, lses = [], []
    for gi in range(DIL_GROUPS):
        o, lse = _dil_attn_fwd(gi, slopes[gi], qn, kn, proj, B, S)
        ogs.append(o)
        lses.append(lse)
    out = _merge_fwd(x, proj, p["b_gate"], ya, ob, ogs, lses, p["woa"], p["wob"], p["woc"], p["wo"])
    saved = dict(x=x, proj=proj, ht=ht, ya=ya, q=q, k=k, v=v, ob=ob, lse_b=lse_b, qn=qn, kn=kn, ogs=ogs, lses=lses)
    return out, saved


def _layer_bwd(dout, sv, p, tabs, slopes, B, S, big_ready=None):
    proj = sv["proj"]
    (dproj, dya, dob, dlb, dg0, dg1, dg2, dl0, dl1, dl2, merged, dpa, dpb, dpc, yb, yc, dbg) = _merge_bwd(
        dout, proj, p["b_gate"], sv["ya"], sv["ob"], sv["ogs"], sv["lses"], p["woa"], p["wob"], p["woc"], p["wo"])
    g = {}
    g["w_o"] = _matmul_tn(merged, dout, "dw_o").reshape(N_DEV, D_MODEL // N_DEV, D_MODEL)
    g["w_out_a"] = _cols_to_slots(_matmul_tn(sv["ya"], dpa, "dw_out_a"))
    g["w_out_b"] = _cols_to_slots(_matmul_tn(yb, dpb, "dw_out_b"))
    g["w_out_c"] = _cols_to_slots(_matmul_tn(yc, dpc, "dw_out_c"))
    g["b_gate"] = dbg[0]
    dproj, st = _mixa_bwd(dproj, dya, proj, p["conv_w"], p["conv_b"], B, S)
    g["conv_w"] = st[0:CONV_K]
    g["conv_b"] = st[CONV_K]
    dq, dk, dv = _mla_attn_bwd(sv["q"], sv["k"], sv["v"], dob, sv["lse_b"], dlb, B, S)
    dproj, dwuqp, dwkp, dwv, dgq, dgkv, dgmq, dgmk = _mla_prep_bwd(
        dproj, dq, dk, dv, proj, p["gq"], p["gkv"], p["wuqp"], p["wkp"], p["wv"], p["gmq"], p["gmk"], *tabs, S)
    g["w_uq"] = _cols_to_slots(dwuqp)[:, :, :MLA_QK]
    g["w_ukv"] = jnp.concatenate([_cols_to_slots(dwkp)[:, :, :MLA_NOPE], _cols_to_slots(dwv)], axis=2)
    g["q_a_norm_g"], g["kv_a_norm_g"] = dgq[0], dgkv[0]
    g["mla_q_norm_g"], g["mla_k_norm_g"] = dgmq[0, :MLA_QK], dgmk[0, :MLA_QK]
    dqkv = None
    for gi, (dog, dlg) in enumerate(((dg0, dl0), (dg1, dl1), (dg2, dl2))):
        dqkv = _dil_attn_bwd(gi, slopes[gi], sv["qn"], sv["kn"], proj, dog, sv["lses"][gi], dlg, dqkv, B, S)
    dproj, dgdq, dgdk = _dil_prep_bwd(dproj, *dqkv, proj, p["gdq"], p["gdk"])
    g["dil_q_norm_g"] = dgdq.reshape(DIL_GROUPS, DIL_HEADS, DIL_HEAD_DIM).sum(axis=1)
    g["dil_k_norm_g"] = dgdk.reshape(DIL_GROUPS, DIL_HEADS, DIL_HEAD_DIM).sum(axis=1)
    token = None if big_ready is None else big_ready(g)
    g["w_in"] = _unpad_columns(_matmul_nn(sv["ht"], dproj, "dw_in", token))
    token = None if big_ready is None else big_ready(g)
    dx, dng = _inproj_bwd_x(dproj, p["wp"], sv["x"], _after(token, p["norm_g"]), dout)
    g["norm_g"] = dng[0]
    return dx, g


def _after(token, a):
    return a if token is None else a + token[0:1, 0:1]


def _local_step(x, target, small, B, S, weights_of, grads_out, big_ready=None):
    tabs = _rope_tables(S)
    sl = _alibi_slopes()
    slopes = [sl[gi] * float(DIL_PATTERNS[gi][1]) for gi in range(DIL_GROUPS)]
    params, saved = [], []
    for l in range(DEPTH):
        gw, token = weights_of(l, x)
        p = _layer_params(gw, small, l)
        p["norm_g"] = _after(token, p["norm_g"])
        x, sv = _layer_fwd(x, p, tabs, slopes, B, S)
        params.append(p)
        saved.append(sv)
    dout, lparts = _loss_head(x, target)
    sq = jnp.sum(lparts[:, 0, 0])
    token = None
    for l in reversed(range(DEPTH)):
        p = dict(params[l], b_gate=_after(token, params[l]["b_gate"]))
        ready = None if big_ready is None else (lambda g, l=l: big_ready(l, g))
        dout, g = _layer_bwd(dout, saved[l], p, tabs, slopes, B, S, ready)
        token = grads_out(l, g, dout)
    return sq, dout


def kernel(x, norm_g, w_in, b_gate, conv_w, conv_b, q_a_norm_g, w_uq, kv_a_norm_g, w_ukv, mla_q_norm_g, mla_k_norm_g, dil_q_norm_g, dil_k_norm_g, w_out_a, w_out_b, w_out_c, w_o, loss_target, m_norm_g, m_w_in, m_b_gate, m_conv_w, m_conv_b, m_q_a_norm_g, m_w_uq, m_kv_a_norm_g, m_w_ukv, m_mla_q_norm_g, m_mla_k_norm_g, m_dil_q_norm_g, m_dil_k_norm_g, m_w_out_a, m_w_out_b, m_w_out_c, m_w_o, v_norm_g, v_w_in, v_b_gate, v_conv_w, v_conv_b, v_q_a_norm_g, v_w_uq, v_kv_a_norm_g, v_w_ukv, v_mla_q_norm_g, v_mla_k_norm_g, v_dil_q_norm_g, v_dil_k_norm_g, v_w_out_a, v_w_out_b, v_w_out_c, v_w_o):
    names = ("norm_g", "w_in", "b_gate", "conv_w", "conv_b", "q_a_norm_g", "w_uq", "kv_a_norm_g", "w_ukv",
             "mla_q_norm_g", "mla_k_norm_g", "dil_q_norm_g", "dil_k_norm_g", "w_out_a", "w_out_b", "w_out_c", "w_o")
    w = dict(zip(names, (norm_g, w_in, b_gate, conv_w, conv_b, q_a_norm_g, w_uq, kv_a_norm_g, w_ukv, mla_q_norm_g,
                         mla_k_norm_g, dil_q_norm_g, dil_k_norm_g, w_out_a, w_out_b, w_out_c, w_o)))
    m = dict(zip(names, (m_norm_g, m_w_in, m_b_gate, m_conv_w, m_conv_b, m_q_a_norm_g, m_w_uq, m_kv_a_norm_g, m_w_ukv,
                         m_mla_q_norm_g, m_mla_k_norm_g, m_dil_q_norm_g, m_dil_k_norm_g, m_w_out_a, m_w_out_b,
                         m_w_out_c, m_w_o)))
    v = dict(zip(names, (v_norm_g, v_w_in, v_b_gate, v_conv_w, v_conv_b, v_q_a_norm_g, v_w_uq, v_kv_a_norm_g, v_w_ukv,
                         v_mla_q_norm_g, v_mla_k_norm_g, v_dil_q_norm_g, v_dil_k_norm_g, v_w_out_a, v_w_out_b,
                         v_w_out_c, v_w_o)))
    B, S, _ = x.shape
    me = _my_index()
    cshard = CONV_WIDTH // N_DEV

    shards = [[w[n][0].astype(BF16) for n in BIG]]
    state = {}

    def widen(t):
        return lax.dynamic_update_slice(jnp.zeros((DEPTH, CONV_K, CONV_WIDTH), F32), t, (0, 0, me * cshard))

    pick = lambda d: [widen(d[n]) if n == "conv_w" else d[n] for n in SMALL]

    def weights_of(l, after):
        if l == 0:
            first = shards[0] + [conv_w]
            handle, token = _exchange_start(first, "all_gather_weights_0_start", "near")
            zero = token[0:1, 0:1]
            state["shards1"] = [(w[n][1] + zero).astype(BF16) for n in BIG]
            for n in BIG:
                state["rows", n] = [a.reshape(-1, a.shape[-1]) + zero for a in (w[n], m[n], v[n])]
            state["small"] = [_pack_local(pick(d)) + zero for d in (w, m, v)]
            busy = state["shards1"] + [a for n in BIG for a in state["rows", n]] + state["small"]
            landed = _exchange_wait(handle, busy, "all_gather_weights_0_wait", "near")
            landed = _sibling_forward(landed, "all_gather_weights_0_forward")
            got = [_own_slot(a, s[None]) for a, s in zip(landed, first)]
            state["gather"], token = _exchange_start(state["shards1"], "all_gather_weights_1_start", "gather")
            state["conv_w"] = got[-1]
        else:
            landed = _exchange_wait(state["gather"], after, "all_gather_weights_1_wait", "gather")
            got, token = [_own_slot(a, s[None]) for a, s in zip(landed, state["shards1"])], None
        gw = dict(zip(BIG, got))
        gw["conv_w"] = state["conv_w"][:, l]
        return gw, token

    recv, small_parts = {}, {}
    my_chip = 2 * lax.axis_index("x") + lax.axis_index("y")

    REST = tuple(n for n in BIG if n != "w_in")

    def big_ready(l, g):
        if l == DEPTH - 1:
            if "w_in" not in g:
                return None
            send = [g[n].astype(BF16) for n in BIG]
            state["scatter"], token = _exchange_start(send, "exchange_weight_grads_1_start", "scatter")
            state["sent", "scatter"] = send
            return token
        tag, group = ("w_in", ("w_in",)) if "w_in" in g else ("rest", REST)
        send = [g[n].astype(BF16) for n in group]
        swapped = _sibling_swap(send, "exchange_weight_grads_0_sibling_" + tag)
        send = [_chip_pair_sum(s, t, "chip_pair_sum_" + n) for n, s, t in zip(group, send, swapped)]
        state[tag], token = _exchange_start(send, "exchange_weight_grads_0_start_" + tag, "chips")
        state["sent", tag] = send
        return token

    def grads_out(l, g, after):
        small_parts[l] = [g[n] for n in SMALL]
        if l == DEPTH - 1:
            return None
        got = {}
        for key, mode, slot in (("scatter", "scatter", me), ("rest", "chips", my_chip), ("w_in", "chips", my_chip)):
            k = DEPTH - 1 if key == "scatter" else 0
            tag = "" if key == "scatter" else "_" + key
            landed = _exchange_wait(state[key], after, f"exchange_weight_grads_{k}_wait{tag}", mode)
            mine = [lax.dynamic_slice_in_dim(s, slot, 1, axis=0) for s in state["sent", key]]
            got[key] = [_own_slot(a, s, slot) for a, s in zip(landed, mine)]
        recv[DEPTH - 1] = got["scatter"]
        recv[0] = got["w_in"] + got["rest"]
        assert BIG == ("w_in",) + REST
        return None

    sq, grad_x = _local_step(x.reshape(B * S, D_MODEL), loss_target.reshape(B * S, D_MODEL), w, B, S,
                             weights_of, grads_out, big_ready)
    loss = lax.psum(sq * (0.5 / D_MODEL), AXES)

    part = {n: jnp.stack([small_parts[l][i] for l in range(DEPTH)]) for i, n in enumerate(SMALL)}
    small_like = [part[n] for n in SMALL]
    pack = _pack_local(small_like)
    handle, token = _exchange_start([pack], "all_gather_small_grads_start", "gather")

    res, done = {}, []
    for i, n in enumerate(BIG):
        outs = _reduce_adamw([recv[l][i] for l in range(DEPTH)], *state["rows", n], "reduce_adamw_" + n, token)
        res[n] = tuple(a.reshape(w[n].shape) for a in outs)
        done.append(outs[0])

    landed, = _exchange_wait(handle, done, "all_gather_small_grads_wait", "gather")
    parts = _own_slot(landed, pack[None])
    gs, ds, ms, vs = _reduce_adamw([parts], *state["small"], "reduce_adamw_small")
    for n, t in zip(SMALL, zip(*(_unpack_local(a, small_like) for a in (gs, ds, ms, vs)))):
        if n == "conv_w":
            t = tuple(lax.dynamic_slice(a, (0, 0, me * cshard), (DEPTH, CONV_K, cshard)) for a in t)
        res[n] = t

    out = [loss, grad_x.reshape(B, S, D_MODEL)]
    for i in range(4):
        out += [res[n][i] for n in names]
    return tuple(out)
```

```python
import jax
import jax.numpy as jnp
from jax import lax
from jax.experimental import pallas as pl
from jax.experimental.pallas import tpu as pltpu

F32 = jnp.float32
BF16 = jnp.bfloat16

D_MODEL = 1024
DEPTH = 2
CONV_WIDTH = 512
CONV_K = 3
MLA_HEADS = 8
MLA_Q_LORA = 256
MLA_KV_LORA = 128
MLA_NOPE = 64
MLA_ROPE = 32
MLA_V = 64
MLA_QK = MLA_NOPE + MLA_ROPE
ROPE_THETA = 10000.0
DIL_PATTERNS = ((128, 1), (512, 4), (2048, 16))
DIL_GROUPS = 3
DIL_HEADS = 8
DIL_HEAD_DIM = 64
DIL_WIDTH = DIL_HEADS * DIL_HEAD_DIM
DIL_QK = DIL_GROUPS * DIL_WIDTH
EPS = 1e-6
N_IN = 11168

ADAM_LR = 0.001
ADAM_B1 = 0.9
ADAM_B2 = 0.999
ADAM_EPS = 1e-08
ADAM_WD = 0.01
ADAM_STEP = 10

N_DEV = 8
AXES = ("x", "y", "c")
LANE = 128
HALF = 64
NPAIR = 4

CB_AB, CB_AC, CB_AX, CB_AZ = 0, 4, 8, 12
CB_CQ, CB_CKV, CB_KPE = 16, 18, 19
CB_BZ = 20
CB_DQ, CB_DK, CB_DV = 24, 36, 48
CB_CZ, CB_GATE = 60, 64
NCB = 88
PP = NCB * LANE
KPE_END = CB_KPE * LANE + MLA_ROPE
SHARD_COLS = N_IN // N_DEV
NEG = -1e30
VMEM_LIMIT = 56 * 1024 * 1024


def _pad_columns(shards):
    K = shards.shape[1]
    gap = LANE - MLA_ROPE
    tr = _tile(K, 128)

    def body(s_ref, o_ref):
        for p in range(N_DEV):
            cut = min(max(KPE_END - p * SHARD_COLS, 0), SHARD_COLS)
            for a, b in ((0, cut), (cut, SHARD_COLS)):
                if a < b:
                    to = p * SHARD_COLS + a + (gap if p * SHARD_COLS + a >= KPE_END else 0)
                    o_ref[:, to:to + b - a] = s_ref[p, :, a:b]
        o_ref[:, KPE_END:KPE_END + gap] = jnp.zeros((tr, gap), o_ref.dtype)

    return pl.pallas_call(
        body, name="pad_columns", grid=(K // tr,),
        in_specs=[pl.BlockSpec((N_DEV, tr, SHARD_COLS), lambda i: (0, i, 0))],
        out_specs=pl.BlockSpec((tr, PP), lambda i: (i, 0)),
        out_shape=jax.ShapeDtypeStruct((K, PP), shards.dtype),
        compiler_params=_cp(),
    )(shards)


def _unpad_columns(wp):
    def columns(a, b):
        gap = LANE - MLA_ROPE
        if b <= KPE_END:
            return wp[:, a:b]
        if a >= KPE_END:
            return wp[:, a + gap:b + gap]
        return jnp.concatenate([wp[:, a:KPE_END], wp[:, KPE_END + gap:b + gap]], axis=1)

    return jnp.stack([columns(p * SHARD_COLS, (p + 1) * SHARD_COLS) for p in range(N_DEV)])


def _put_copies(stages, dst_ref, sems, slot, rows, cols):
    return [pltpu.make_async_copy(st.at[slot], dst_ref.at[rows, pl.ds(c0, st.shape[-1])], sems.at[slot, k])
            for k, (st, c0) in enumerate(zip(stages, cols))]


def _put_pipeline(step, nsteps, copies_of, fill):
    @pl.when(step >= 2)
    def _():
        for cp in copies_of(step - 2):
            cp.wait()

    fill(step % 2)
    for cp in copies_of(step):
        cp.start()

    @pl.when(step == nsteps - 1)
    def _():
        if nsteps >= 2:
            for cp in copies_of(step - 1):
                cp.wait()
        for cp in copies_of(step):
            cp.wait()


def _cp():
    return pltpu.CompilerParams(vmem_limit_bytes=VMEM_LIMIT)


def _rstd(x, n):
    return lax.rsqrt(jnp.sum(x * x, axis=-1, keepdims=True) * (1.0 / n) + EPS)


def _sigmoid(z):
    return 1.0 / (1.0 + jnp.exp(-z))


def _silu(z):
    return z * _sigmoid(z)


def _silu_and_grad(z):
    s = _sigmoid(z)
    return z * s, s * (1.0 + z * (1.0 - s))


def _mm(a, b):
    return jnp.dot(a.astype(BF16), b.astype(BF16), preferred_element_type=F32)


def _mm_nt(a, b):
    return lax.dot_general(a.astype(BF16), b.astype(BF16), (((1,), (1,)), ((), ())), preferred_element_type=F32)


def _mm_tn(a, b):
    return lax.dot_general(a.astype(BF16), b.astype(BF16), (((0,), (0,)), ((), ())), preferred_element_type=F32)


def _lane_lo(shape):
    return lax.broadcasted_iota(jnp.int32, shape, len(shape) - 1) < HALF


def _head_bcast_sum(x, terms=3):
    w = x.shape[-1]
    same = (lax.broadcasted_iota(jnp.int32, (w, w), 0) // HALF) == (lax.broadcasted_iota(jnp.int32, (w, w), 1) // HALF)
    ones = jnp.where(same, 1.0, 0.0).astype(jnp.bfloat16)
    total = None
    for _ in range(terms):
        term = x.astype(jnp.bfloat16)
        x = x - term.astype(F32)
        part = jnp.dot(term, ones, preferred_element_type=F32)
        total = part if total is None else total + part
    return total


def _rope(t, cos, sa, sb):
    return t * cos + pltpu.roll(t, LANE - 16, axis=1) * sa + pltpu.roll(t, 16, axis=1) * sb


def _rope_t(d, cos, sa, sb):
    return d * cos + pltpu.roll(d * sa, 16, axis=1) + pltpu.roll(d * sb, LANE - 16, axis=1)


def _shift_down(u, k):
    rows = lax.broadcasted_iota(jnp.int32, u.shape, 0)
    return jnp.where(rows >= k, pltpu.roll(u, k, axis=0), 0.0)


def _shift_up(u, k):
    n = u.shape[0]
    rows = lax.broadcasted_iota(jnp.int32, u.shape, 0)
    return jnp.where(rows < n - k, pltpu.roll(u, n - k, axis=0), 0.0)


def _tile(n, want):
    t = min(n, want)
    assert n % t == 0, (n, want)
    return t


def _inproj_fwd(x, g, wp):
    T = x.shape[0]
    tm, tn = _tile(T, 2048), 512

    def body(x_ref, g_ref, w_ref, proj_ref, ht_ref, h_ref):
        @pl.when(pl.program_id(1) == 0)
        def _():
            n = min(tm, 512)
            for r0 in range(0, tm, n):
                xv = x_ref[r0:r0 + n, :]
                h = xv * _rstd(xv, D_MODEL) * g_ref[...]
                h_ref[r0:r0 + n, :] = h.astype(BF16)
                ht_ref[:, r0:r0 + n] = h.T.astype(BF16)

        proj_ref[...] = jnp.dot(h_ref[...], w_ref[...], preferred_element_type=F32).astype(BF16)

    return pl.pallas_call(
        body, name="inproj_fwd", grid=(T // tm, PP // tn),
        in_specs=[pl.BlockSpec((tm, D_MODEL), lambda i, j: (i, 0)),
                  pl.BlockSpec((1, D_MODEL), lambda i, j: (0, 0)),
                  pl.BlockSpec((D_MODEL, tn), lambda i, j: (0, j))],
        out_specs=[pl.BlockSpec((tm, tn), lambda i, j: (i, j)),
                   pl.BlockSpec((D_MODEL, tm), lambda i, j: (0, i))],
        out_shape=[jax.ShapeDtypeStruct((T, PP), BF16), jax.ShapeDtypeStruct((D_MODEL, T), BF16)],
        scratch_shapes=[pltpu.VMEM((tm, D_MODEL), BF16)],
        compiler_params=_cp(),
    )(x, g, wp)


def _matmul_nn(at, b, name, after=None):
    K, T = at.shape
    N = b.shape[1]
    tt, tn = _tile(T, 1024), _tile(N, 2816)
    nk = T // tt
    unread = [] if after is None else [after]

    def body(a_ref, b_ref, *refs):
        o_ref, acc_ref = refs[len(unread):]
        k = pl.program_id(1)

        @pl.when(k == 0)
        def _():
            acc_ref[...] = jnp.zeros_like(acc_ref)

        acc_ref[...] += jnp.dot(a_ref[...], b_ref[...], preferred_element_type=F32)

        @pl.when(k == nk - 1)
        def _():
            o_ref[...] = acc_ref[...].astype(BF16)

    return pl.pallas_call(
        body, name=name, grid=(N // tn, nk),
        in_specs=[pl.BlockSpec((K, tt), lambda j, k: (0, k)),
                  pl.BlockSpec((tt, tn), lambda j, k: (k, j))] + [pl.BlockSpec(memory_space=pl.ANY)] * len(unread),
        out_specs=pl.BlockSpec((K, tn), lambda j, k: (0, j)),
        out_shape=jax.ShapeDtypeStruct((K, N), BF16),
        scratch_shapes=[pltpu.VMEM((K, tn), F32)],
        compiler_params=_cp(),
    )(at, b, *unread)


def _matmul_tn(a, b, name):
    T, K = a.shape
    N = b.shape[1]
    tt, tn = _tile(T, 512), _tile(N, 1024)

    def body(a_ref, b_ref, o_ref):
        @pl.when(pl.program_id(1) == 0)
        def _():
            o_ref[...] = jnp.zeros_like(o_ref)

        o_ref[...] += _mm_tn(a_ref[...], b_ref[...])

    return pl.pallas_call(
        body, name=name, grid=(N // tn, T // tt),
        in_specs=[pl.BlockSpec((tt, K), lambda j, k: (k, 0)),
                  pl.BlockSpec((tt, tn), lambda j, k: (k, j))],
        out_specs=pl.BlockSpec((K, tn), lambda j, k: (0, j)),
        out_shape=jax.ShapeDtypeStruct((K, N), F32),
        compiler_params=_cp(),
    )(a, b)


def _inproj_bwd_x(dproj, wp, x, g, dout):
    T = x.shape[0]
    tm, tk = _tile(T, 1024), 1024
    nk = PP // tk

    def body(dp_ref, w_ref, x_ref, g_ref, do_ref, dx_ref, dg_ref, acc_ref):
        i, k = pl.program_id(0), pl.program_id(1)

        @pl.when(k == 0)
        def _():
            acc_ref[...] = jnp.zeros_like(acc_ref)

        @pl.when((k == 0) & (i == 0))
        def _():
            dg_ref[...] = jnp.zeros_like(dg_ref)

        acc_ref[...] += _mm_nt(dp_ref[...], w_ref[...])

        @pl.when(k == nk - 1)
        def _():
            dh = acc_ref[...]
            xv = x_ref[...]
            r = _rstd(xv, D_MODEL)
            gy = dh * g_ref[...]
            dot = jnp.sum(xv * gy, axis=-1, keepdims=True) * (1.0 / D_MODEL)
            dx_ref[...] = do_ref[...] + r * gy - xv * (r * r * r) * dot
            dg_ref[...] += jnp.sum(dh * xv * r, axis=0, keepdims=True)

    return pl.pallas_call(
        body, name="inproj_bwd_x", grid=(T // tm, nk),
        in_specs=[pl.BlockSpec((tm, tk), lambda i, k: (i, k)),
                  pl.BlockSpec((D_MODEL, tk), lambda i, k: (0, k)),
                  pl.BlockSpec((tm, D_MODEL), lambda i, k: (i, 0)),
                  pl.BlockSpec((1, D_MODEL), lambda i, k: (0, 0)),
                  pl.BlockSpec((tm, D_MODEL), lambda i, k: (i, 0))],
        out_specs=[pl.BlockSpec((tm, D_MODEL), lambda i, k: (i, 0)),
                   pl.BlockSpec((1, D_MODEL), lambda i, k: (0, 0))],
        out_shape=[jax.ShapeDtypeStruct((T, D_MODEL), F32), jax.ShapeDtypeStruct((1, D_MODEL), F32)],
        scratch_shapes=[pltpu.VMEM((tm, D_MODEL), F32)],
        compiler_params=_cp(),
    )(dproj, wp, x, g, dout)


A_SEGS = (CB_AB, CB_AC, CB_AX, CB_AZ)


def _mixa_fwd(proj, cw, cb, B, S):
    nc = CONV_WIDTH // LANE

    def body(ab_ref, ac_ref, ax_ref, az_ref, cw_ref, cb_ref, y_ref):
        ab, ac, ax, az = (r[...].astype(F32) for r in (ab_ref, ac_ref, ax_ref, az_ref))
        u = ac * ax
        conv = cb_ref[...] + cw_ref[0:1, :] * _shift_down(u, 2) + cw_ref[1:2, :] * _shift_down(u, 1) + cw_ref[2:3, :] * u
        y_ref[...] = (ab * conv * _silu(az)).astype(BF16)

    return pl.pallas_call(
        body, name="mixa_fwd", grid=(B, nc),
        in_specs=[pl.BlockSpec((S, LANE), lambda b, j, c0=c0: (b, c0 + j)) for c0 in A_SEGS]
                 + [pl.BlockSpec((CONV_K, LANE), lambda b, j: (0, j)),
                    pl.BlockSpec((1, LANE), lambda b, j: (0, j))],
        out_specs=pl.BlockSpec((S, LANE), lambda b, j: (b, j)),
        out_shape=jax.ShapeDtypeStruct((B * S, CONV_WIDTH), BF16),
        compiler_params=_cp(),
    )(proj, proj, proj, proj, cw, cb)


def _mixa_bwd(dproj, dy, proj, cw, cb, B, S):
    nc = CONV_WIDTH // LANE

    def body(dpin_ref, dy_ref, ab_ref, ac_ref, ax_ref, az_ref, cw_ref, cb_ref, dp_ref, st_ref, stage, sems):
        del dpin_ref
        j, b = pl.program_id(0), pl.program_id(1)
        ab, ac, ax, az = (r[...].astype(F32) for r in (ab_ref, ac_ref, ax_ref, az_ref))
        u = ac * ax
        u1, u2 = _shift_down(u, 1), _shift_down(u, 2)
        w0, w1, w2 = cw_ref[0:1, :], cw_ref[1:2, :], cw_ref[2:3, :]
        conv = cb_ref[...] + w0 * u2 + w1 * u1 + w2 * u
        s, ds_az = _silu_and_grad(az)
        d = dy_ref[...]
        dconv = d * ab * s
        du = w2 * dconv + w1 * _shift_up(dconv, 1) + w0 * _shift_up(dconv, 2)
        grads = (d * conv * s, du * ax, du * ac, d * ab * conv * ds_az)

        def fill(slot):
            for k, v in enumerate(grads):
                stage[slot, k] = v.astype(BF16)

        def copies_of(step):
            sj, sb = step // B, step % B
            return _put_copies([stage.at[:, k] for k in range(4)], dp_ref, sems, step % 2,
                               pl.ds(pl.multiple_of(sb * S, S), S),
                               [pl.multiple_of((c0 + sj) * LANE, LANE) for c0 in A_SEGS])

        _put_pipeline(j * B + b, nc * B, copies_of, fill)
        row = lax.broadcasted_iota(jnp.int32, (8, LANE), 0)
        st = jnp.zeros((8, LANE), F32)
        for r, v in enumerate((dconv * u2, dconv * u1, dconv * u, dconv)):
            st = st + jnp.where(row == r, jnp.sum(v, axis=0, keepdims=True), 0.0)

        @pl.when(pl.program_id(1) == 0)
        def _():
            st_ref[...] = st

        @pl.when(pl.program_id(1) != 0)
        def _():
            st_ref[...] += st

    return pl.pallas_call(
        body, name="mixa_bwd", grid=(nc, B),
        in_specs=[pl.BlockSpec(memory_space=pl.ANY),
                  pl.BlockSpec((S, LANE), lambda j, b: (b, j))]
                 + [pl.BlockSpec((S, LANE), lambda j, b, c0=c0: (b, c0 + j)) for c0 in A_SEGS]
                 + [pl.BlockSpec((CONV_K, LANE), lambda j, b: (0, j)),
                    pl.BlockSpec((1, LANE), lambda j, b: (0, j))],
        out_specs=[pl.BlockSpec(memory_space=pl.ANY),
                   pl.BlockSpec((8, LANE), lambda j, b: (0, j))],
        out_shape=[jax.ShapeDtypeStruct(dproj.shape, BF16), jax.ShapeDtypeStruct((8, CONV_WIDTH), F32)],
        scratch_shapes=[pltpu.VMEM((2, 4, S, LANE), BF16), pltpu.SemaphoreType.DMA((2, 4))],
        input_output_aliases={0: 0},
        compiler_params=_cp(),
    )(dproj, dy, proj, proj, proj, proj, cw, cb)


def _mla_prep_fwd(proj, gq, gkv, wuqp, wkp, wv, gmq, gmk, cos, sa, sb, S):
    T = proj.shape[0]
    ts = _tile(S, 512)
    ns = S // ts
    W = MLA_HEADS * LANE

    def body(p_ref, gq_ref, gkv_ref, wuq_ref, wk_ref, wv_ref, gmq_ref, gmk_ref, cos_ref, sa_ref, sb_ref,
             q_ref, k_ref, v_ref):
        cq = p_ref[:, 0:2 * LANE].astype(F32)
        ckv = p_ref[:, 2 * LANE:3 * LANE].astype(F32)
        kpe = pltpu.roll(p_ref[:, 3 * LANE:4 * LANE].astype(F32), HALF, axis=1)
        cqn = cq * _rstd(cq, MLA_Q_LORA) * gq_ref[...]
        ckn = (ckv * _rstd(ckv, MLA_KV_LORA) * gkv_ref[...]).astype(BF16)
        q0 = _mm(cqn, wuq_ref[...])
        kn = _mm(ckn, wk_ref[...])
        v_ref[...] = _mm(ckn, wv_ref[...]).astype(BF16)
        c, a, b = cos_ref[...], sa_ref[...], sb_ref[...]
        kpe_rot = _rope(kpe * gmk_ref[...], c, a, b)
        for h in range(MLA_HEADS):
            q0h = q0[:, h * LANE:(h + 1) * LANE]
            q_ref[h] = _rope(q0h * _rstd(q0h, MLA_QK) * gmq_ref[...], c, a, b).astype(BF16)
            knh = kn[:, h * LANE:(h + 1) * LANE]
            k_ref[h] = (_rstd(knh + kpe, MLA_QK) * (knh * gmk_ref[...] + kpe_rot)).astype(BF16)

    def whole(r, c):
        return pl.BlockSpec((r, c), lambda i: (0, 0))

    tab = pl.BlockSpec((ts, LANE), lambda i: (i % ns, 0))
    return pl.pallas_call(
        body, name="mla_prep_fwd", grid=(T // ts,),
        in_specs=[pl.BlockSpec((ts, 4 * LANE), lambda i: (i, CB_CQ // 4)),
                  whole(1, MLA_Q_LORA), whole(1, MLA_KV_LORA), whole(MLA_Q_LORA, W), whole(MLA_KV_LORA, W),
                  whole(MLA_KV_LORA, MLA_HEADS * MLA_V), whole(1, LANE), whole(1, LANE), tab, tab, tab],
        out_specs=[pl.BlockSpec((MLA_HEADS, ts, LANE), lambda i: (0, i, 0)),
                   pl.BlockSpec((MLA_HEADS, ts, LANE), lambda i: (0, i, 0)),
                   pl.BlockSpec((ts, MLA_HEADS * MLA_V), lambda i: (i, 0))],
        out_shape=[jax.ShapeDtypeStruct((MLA_HEADS, T, LANE), BF16), jax.ShapeDtypeStruct((MLA_HEADS, T, LANE), BF16),
                   jax.ShapeDtypeStruct((T, MLA_HEADS * MLA_V), BF16)],
        compiler_params=_cp(),
    )(proj, gq, gkv, wuqp, wkp, wv, gmq, gmk, cos, sa, sb)


def _mla_prep_bwd(dproj, dq, dk, dv, proj, gq, gkv, wuqp, wkp, wv, gmq, gmk, cos, sa, sb, S):
    T = proj.shape[0]
    ts = _tile(S, 256)
    ns = S // ts
    W = MLA_HEADS * LANE

    def body(dpin_ref, dq_ref, dk_ref, dv_ref, p_ref, gq_ref, gkv_ref, wuq_ref, wk_ref, wv_ref, gmq_ref, gmk_ref,
             cos_ref, sa_ref, sb_ref,
             dp_ref, dwuq_ref, dwk_ref, dwv_ref, dgq_ref, dgkv_ref, dgmq_ref, dgmk_ref, dq0_ref, dkn_ref):
        del dpin_ref

        @pl.when(pl.program_id(0) == 0)
        def _():
            for r in (dwuq_ref, dwk_ref, dwv_ref, dgq_ref, dgkv_ref, dgmq_ref, dgmk_ref):
                r[...] = jnp.zeros_like(r)

        cq = p_ref[:, 0:2 * LANE].astype(F32)
        ckv = p_ref[:, 2 * LANE:3 * LANE].astype(F32)
        kpe = pltpu.roll(p_ref[:, 3 * LANE:4 * LANE].astype(F32), HALF, axis=1)
        rq = _rstd(cq, MLA_Q_LORA)
        rkv = _rstd(ckv, MLA_KV_LORA)
        gq, gkv, gmq, gmk = gq_ref[...], gkv_ref[...], gmq_ref[...], gmk_ref[...]
        cqn = (cq * rq * gq).astype(BF16)
        ckn = (ckv * rkv * gkv).astype(BF16)
        q0 = _mm(cqn, wuq_ref[...])
        kn = _mm(ckn, wk_ref[...])
        c, a, b = cos_ref[...], sa_ref[...], sb_ref[...]
        lane = lax.broadcasted_iota(jnp.int32, (ts, LANE), 1)
        dgmq = jnp.zeros((1, LANE), F32)
        dgmk = jnp.zeros((1, LANE), F32)
        nope = lane < MLA_NOPE
        kpe_rot = _rope(kpe * gmk, c, a, b)
        dk_sum = jnp.zeros((ts, LANE), F32)
        back = jnp.zeros((ts, 1), F32)
        for h in range(MLA_HEADS):
            q0h = q0[:, h * LANE:(h + 1) * LANE]
            r = _rstd(q0h, MLA_QK)
            d1 = _rope_t(dq_ref[h], c, a, b)
            gy = d1 * gmq
            dq0_ref[:, h * LANE:(h + 1) * LANE] = (
                r * gy - q0h * (r * r * r) * (jnp.sum(q0h * gy, axis=-1, keepdims=True) * (1.0 / MLA_QK))).astype(BF16)
            dgmq = dgmq + jnp.sum(d1 * q0h * r, axis=0, keepdims=True)
            knh = kn[:, h * LANE:(h + 1) * LANE]
            dkh = dk_ref[h]
            r = _rstd(knh + kpe, MLA_QK)
            r3dot = (r * r * r) * (jnp.sum((knh * gmk + kpe_rot) * dkh, axis=-1, keepdims=True) * (1.0 / MLA_QK))
            dkn_ref[:, h * LANE:(h + 1) * LANE] = jnp.where(nope, r * gmk * dkh - knh * r3dot, 0.0).astype(BF16)
            dgmk = dgmk + jnp.sum(jnp.where(nope, dkh * knh * r, 0.0), axis=0, keepdims=True)
            dk_sum = dk_sum + r * dkh
            back = back + r3dot
        rot = jnp.where(nope | (lane >= MLA_QK), 0.0, _rope_t(dk_sum, c, a, b))
        dkpe = gmk * rot - kpe * back
        dgmk = dgmk + jnp.sum(kpe * rot, axis=0, keepdims=True)
        dq0 = dq0_ref[...]
        dkn = dkn_ref[...]
        dvv = dv_ref[...]
        dwuq_ref[...] += _mm_tn(cqn, dq0)
        dwk_ref[...] += _mm_tn(ckn, dkn)
        dwv_ref[...] += _mm_tn(ckn, dvv)
        dgmq_ref[...] += dgmq
        dgmk_ref[...] += dgmk
        dcqn = _mm_nt(dq0, wuq_ref[...])
        gy = dcqn * gq
        dp_ref[:, 0:2 * LANE] = (
            rq * gy - cq * (rq * rq * rq) * (jnp.sum(cq * gy, axis=-1, keepdims=True) * (1.0 / MLA_Q_LORA))).astype(BF16)
        dgq_ref[...] += jnp.sum(dcqn * cq * rq, axis=0, keepdims=True)
        dckn = _mm_nt(dkn, wk_ref[...]) + _mm_nt(dvv, wv_ref[...])
        gy = dckn * gkv
        dp_ref[:, 2 * LANE:3 * LANE] = (
            rkv * gy - ckv * (rkv * rkv * rkv) * (jnp.sum(ckv * gy, axis=-1, keepdims=True) * (1.0 / MLA_KV_LORA))).astype(BF16)
        dgkv_ref[...] += jnp.sum(dckn * ckv * rkv, axis=0, keepdims=True)
        dp_ref[:, 3 * LANE:4 * LANE] = pltpu.roll(dkpe, HALF, axis=1).astype(BF16)

    def whole(r, c):
        return pl.BlockSpec((r, c), lambda i: (0, 0))

    tab = pl.BlockSpec((ts, LANE), lambda i: (i % ns, 0))
    heads = pl.BlockSpec((MLA_HEADS, ts, LANE), lambda i: (0, i, 0))
    return pl.pallas_call(
        body, name="mla_prep_bwd", grid=(T // ts,),
        in_specs=[pl.BlockSpec(memory_space=pl.ANY), heads, heads,
                  pl.BlockSpec((ts, MLA_HEADS * MLA_V), lambda i: (i, 0)),
                  pl.BlockSpec((ts, 4 * LANE), lambda i: (i, CB_CQ // 4)),
                  whole(1, MLA_Q_LORA), whole(1, MLA_KV_LORA), whole(MLA_Q_LORA, W), whole(MLA_KV_LORA, W),
                  whole(MLA_KV_LORA, MLA_HEADS * MLA_V), whole(1, LANE), whole(1, LANE), tab, tab, tab],
        out_specs=[pl.BlockSpec((ts, 4 * LANE), lambda i: (i, CB_CQ // 4)),
                   whole(MLA_Q_LORA, W), whole(MLA_KV_LORA, W), whole(MLA_KV_LORA, MLA_HEADS * MLA_V),
                   whole(1, MLA_Q_LORA), whole(1, MLA_KV_LORA), whole(1, LANE), whole(1, LANE)],
        out_shape=[jax.ShapeDtypeStruct(dproj.shape, BF16),
                   jax.ShapeDtypeStruct((MLA_Q_LORA, W), F32), jax.ShapeDtypeStruct((MLA_KV_LORA, W), F32),
                   jax.ShapeDtypeStruct((MLA_KV_LORA, MLA_HEADS * MLA_V), F32),
                   jax.ShapeDtypeStruct((1, MLA_Q_LORA), F32), jax.ShapeDtypeStruct((1, MLA_KV_LORA), F32),
                   jax.ShapeDtypeStruct((1, LANE), F32), jax.ShapeDtypeStruct((1, LANE), F32)],
        scratch_shapes=[pltpu.VMEM((ts, W), BF16), pltpu.VMEM((ts, W), BF16)],
        input_output_aliases={0: 0},
        compiler_params=_cp(),
    )(dproj, dq, dk, dv, proj, gq, gkv, wuqp, wkp, wv, gmq, gmk, cos, sa, sb)


def _dil_prep_fwd(proj, gq, gk):
    T = proj.shape[0]
    ts = _tile(T, 512)

    def body(pq_ref, pk_ref, gq_ref, gk_ref, q_ref, k_ref):
        for c in range(NPAIR):
            cs = slice(c * LANE, (c + 1) * LANE)
            t = jnp.concatenate([pq_ref[:, cs], pk_ref[:, cs]], axis=1).astype(F32)
            y = t * lax.rsqrt(_head_bcast_sum(t * t, terms=2) * (1.0 / DIL_HEAD_DIM) + EPS)
            q_ref[:, cs] = (y[:, 0:LANE] * gq_ref[:, cs]).astype(BF16)
            k_ref[:, cs] = (y[:, LANE:2 * LANE] * gk_ref[:, cs]).astype(BF16)

    col = pl.BlockSpec((1, DIL_WIDTH), lambda i, g: (0, g))
    out = pl.BlockSpec((ts, DIL_WIDTH), lambda i, g: (i, g))
    seg = lambda c0: pl.BlockSpec((ts, DIL_WIDTH), lambda i, g: (i, c0 // NPAIR + g))
    return pl.pallas_call(
        body, name="dil_prep_fwd", grid=(T // ts, DIL_GROUPS),
        in_specs=[seg(CB_DQ), seg(CB_DK), col, col],
        out_specs=[out, out],
        out_shape=[jax.ShapeDtypeStruct((T, DIL_QK), BF16)] * 2,
        compiler_params=_cp(),
    )(proj, proj, gq, gk)


def _dil_prep_bwd(dproj, ddq, ddk, ddv, proj, gq, gk):
    T = proj.shape[0]
    ts = _tile(T, 512)
    nt = T // ts

    def body(dpin_ref, ddq_ref, ddk_ref, ddv_ref, pq_ref, pk_ref, gq_ref, gk_ref, dp_ref, dgq_ref, dgk_ref,
             stage, sems):
        del dpin_ref
        g, i = pl.program_id(0), pl.program_id(1)

        @pl.when(i == 0)
        def _():
            dgq_ref[...] = jnp.zeros_like(dgq_ref)
            dgk_ref[...] = jnp.zeros_like(dgk_ref)

        def fill(slot):
            stage[slot, 2] = ddv_ref[...].astype(BF16)
            for c in range(NPAIR):
                cs = slice(c * LANE, (c + 1) * LANE)
                t = jnp.concatenate([pq_ref[:, cs], pk_ref[:, cs]], axis=1).astype(F32)
                d = jnp.concatenate([ddq_ref[:, cs], ddk_ref[:, cs]], axis=1)
                gy = d * jnp.concatenate([gq_ref[:, cs], gk_ref[:, cs]], axis=1)
                r = lax.rsqrt(_head_bcast_sum(t * t, terms=2) * (1.0 / DIL_HEAD_DIM) + EPS)
                dot = _head_bcast_sum(t * gy, terms=2) * (1.0 / DIL_HEAD_DIM)
                dx = (r * gy - t * (r * r * r) * dot).astype(BF16)
                stage[slot, 0, :, cs] = dx[:, 0:LANE]
                stage[slot, 1, :, cs] = dx[:, LANE:2 * LANE]
                part = jnp.sum(d * t * r, axis=0, keepdims=True)
                dgq_ref[:, cs] += part[:, 0:LANE]
                dgk_ref[:, cs] += part[:, LANE:2 * LANE]

        def copies_of(step):
            sg, si = step // nt, step % nt
            return _put_copies([stage.at[:, k] for k in range(3)], dp_ref, sems, step % 2,
                               pl.ds(pl.multiple_of(si * ts, ts), ts),
                               [pl.multiple_of((c0 + NPAIR * sg) * LANE, LANE) for c0 in (CB_DQ, CB_DK, CB_DV)])

        _put_pipeline(g * nt + i, DIL_GROUPS * nt, copies_of, fill)

    col = pl.BlockSpec((1, DIL_WIDTH), lambda g, i: (0, g))
    tok = pl.BlockSpec((ts, DIL_WIDTH), lambda g, i: (i, g))
    seg = lambda c0: pl.BlockSpec((ts, DIL_WIDTH), lambda g, i: (i, c0 // NPAIR + g))
    return pl.pallas_call(
        body, name="dil_prep_bwd", grid=(DIL_GROUPS, nt),
        in_specs=[pl.BlockSpec(memory_space=pl.ANY), tok, tok, tok, seg(CB_DQ), seg(CB_DK), col, col],
        out_specs=[pl.BlockSpec(memory_space=pl.ANY), col, col],
        out_shape=[jax.ShapeDtypeStruct(dproj.shape, BF16), jax.ShapeDtypeStruct((1, DIL_QK), F32),
                   jax.ShapeDtypeStruct((1, DIL_QK), F32)],
        scratch_shapes=[pltpu.VMEM((2, 3, ts, DIL_WIDTH), BF16), pltpu.SemaphoreType.DMA((2, 3))],
        input_output_aliases={0: 0},
        compiler_params=_cp(),
    )(dproj, ddq, ddk, ddv, proj, proj, gq, gk)


COPY_ROWS = 256


def _to_classes(src_ref, dst_ref, d, L, scale=None):
    m = min(L, max(8, COPY_ROWS // d))
    for c0 in range(0, L, m):
        x = src_ref[c0 * d:(c0 + m) * d, :].astype(F32)
        if scale is not None:
            x = x * scale
        if d > 1:
            x = jnp.swapaxes(x.reshape(m, d, LANE), 0, 1)
        for r in range(d):
            dst_ref[r * L + c0:r * L + c0 + m, :] = (x[r] if d > 1 else x).astype(dst_ref.dtype)


def _from_classes(src_ref, dst_ref, d, L):
    n = min(L, COPY_ROWS)
    for r in range(d):
        for c0 in range(0, L, n):
            rows = pl.ds(r + c0 * d, n, stride=d) if d > 1 else pl.ds(c0, n)
            dst_ref[rows, :] = src_ref[r * L + c0:r * L + c0 + n, :].astype(dst_ref.dtype)


MLA_TQ, MLA_TK = 512, 512


def _causal_bias(tq, tk, shift):
    row = lax.broadcasted_iota(jnp.int32, (tq, tk), 0)
    col = lax.broadcasted_iota(jnp.int32, (tq, tk), 1)
    return jnp.where(row >= col + shift, 0.0, NEG)


def _mla_specs(S):
    heads = pl.BlockSpec((2, S, LANE), lambda b, j: (j, b, 0))
    pair = pl.BlockSpec((S, LANE), lambda b, j: (b, j))
    return heads, pair


def _mla_attn_fwd(q, k, v, B, S):
    tq = _tile(S, MLA_TQ)
    tk = _tile(tq, MLA_TK)
    nd = tq // tk
    scale = MLA_QK ** -0.5
    heads, pair = _mla_specs(S)

    def body(q_ref, k_ref, v_ref, o_ref, lse_ref):
        lo, lok = _lane_lo((tq, LANE)), _lane_lo((tk, LANE))
        diag = [_causal_bias(tq, tk, i * tk) for i in range(nd)]

        def block(g, _):
            row0 = pl.multiple_of(g * tq, tq)
            rows = pl.ds(row0, tq)
            qs = [q_ref[hh, rows, :] for hh in range(2)]

            one = jnp.ones((), BF16)

            def step(off, carries, bias):
                off = pl.multiple_of(off, tk)
                vt = v_ref[pl.ds(off, tk), :]
                vh = (jnp.where(lok, vt, one), jnp.where(lok, one, vt))
                out = []
                for hh, (m, acc) in enumerate(carries):
                    s = _mm_nt(qs[hh], k_ref[hh, pl.ds(off, tk), :]) * scale
                    if bias is not None:
                        s = s + bias
                    m_new = jnp.maximum(m, jnp.max(s, axis=-1, keepdims=True))
                    p = jnp.exp(s - m_new)
                    out.append((m_new, jnp.exp(m - m_new) * acc + _mm(p, vh[hh])))
                return tuple(out)

            init = (jnp.full((tq, 1), NEG, F32), jnp.zeros((tq, LANE), F32))
            carries = lax.fori_loop(0, g * nd, lambda i, c: step(i * tk, c, None), (init, init))
            for i in range(nd):
                carries = step(row0 + i * tk, carries, diag[i])
            (ma, acca), (mb, accb) = carries
            la, lb = pltpu.roll(acca, HALF, axis=1), pltpu.roll(accb, HALF, axis=1)
            o_ref[rows, :] = jnp.where(lo, acca / la, accb / lb)
            lse_ref[rows, :] = jnp.where(lo, ma + jnp.log(la), mb + jnp.log(lb))
            return 0

        lax.fori_loop(0, S // tq, block, 0)

    return pl.pallas_call(
        body, name="mla_attn_fwd", grid=(B, NPAIR), in_specs=[heads, heads, pair], out_specs=[pair, pair],
        out_shape=[jax.ShapeDtypeStruct((B * S, MLA_HEADS * MLA_V), F32)] * 2,
        compiler_params=_cp(),
    )(q, k, v)


DIL_UNROLL = 16


def _dil_geometry(gi, S):
    span, d = DIL_PATTERNS[gi]
    L = S // d
    t = _tile(L, 128)
    window = span // d
    back = min(-(-window // t) * t, L - t)
    return d, L, t, window, back


def _dil_specs(gi, S):
    qk = pl.BlockSpec((S, LANE), lambda b, j: (b, NPAIR * gi + j))
    v = pl.BlockSpec((S, LANE), lambda b, j: (b, CB_DV + NPAIR * gi + j))
    pair = pl.BlockSpec((S, LANE), lambda b, j: (b, j))
    return qk, v, pair


def _dil_bias(bias_ref, sl_ref, j, t, kw, back, window):
    row = lax.broadcasted_iota(jnp.int32, (2 * t, kw), 0)
    col = lax.broadcasted_iota(jnp.int32, (2 * t, kw), 1)
    second = row >= t
    slope = jnp.where(second, sl_ref[j, 1], sl_ref[j, 0])
    for n in range(bias_ref.shape[0]):
        dist = jnp.where(second, row - t, row) + n * back - col
        bias_ref[n] = jnp.where((dist >= 0) & (dist <= window), -slope * dist.astype(F32), NEG)


def _stack_heads(x, lo):
    zero = jnp.zeros((), x.dtype)
    return jnp.concatenate([jnp.where(lo, x, zero), jnp.where(lo, zero, x)], axis=0)


def _dil_attn_fwd(gi, slopes, qn, kn, proj, B, S):
    d, L, t, window, back = _dil_geometry(gi, S)
    kw, nq = back + t, L // t
    nbias = 2 if back else 1
    qk, vspec, pair = _dil_specs(gi, S)

    def body(sl_ref, q_ref, k_ref, v_ref, o_ref, lse_ref, qs, ks, vs, os_, ls, bias_ref):
        _to_classes(q_ref, qs, d, L, DIL_HEAD_DIM ** -0.5)
        _to_classes(k_ref, ks, d, L)
        _to_classes(v_ref, vs, d, L)
        _dil_bias(bias_ref, sl_ref, pl.program_id(1), t, kw, back, window)
        lo = _lane_lo((t, LANE))

        def block(g, _):
            qb = g % nq if d > 1 else g
            row0 = pl.multiple_of(g * t, t)
            rows = pl.ds(row0, t)
            early = qb * t < back
            keys = pl.ds(pl.multiple_of(jnp.where(early, row0 - qb * t, row0 - back), t), kw)
            s = _mm_nt(_stack_heads(qs[rows, :], lo), ks[keys, :]) + bias_ref[jnp.where(early, 0, nbias - 1)]
            m = jnp.max(s, axis=-1, keepdims=True)
            p = jnp.exp(s - m)
            l = jnp.sum(p, axis=-1, keepdims=True)
            o2 = _mm(p, vs[keys, :]) / l
            lse2 = m + jnp.log(l)
            os_[rows, :] = jnp.where(lo, o2[:t], o2[t:])
            ls[rows, :] = jnp.where(lo, lse2[:t], lse2[t:])
            return 0

        lax.fori_loop(0, d * nq, block, 0, unroll=DIL_UNROLL if d * nq % DIL_UNROLL == 0 else 1)
        _from_classes(os_, o_ref, d, L)
        _from_classes(ls, lse_ref, d, L)

    return pl.pallas_call(
        body, name=f"dil_attn_fwd_{gi}", grid=(B, NPAIR),
        in_specs=[pl.BlockSpec(memory_space=pltpu.SMEM), qk, qk, vspec], out_specs=[pair, pair],
        out_shape=[jax.ShapeDtypeStruct((B * S, DIL_WIDTH), F32)] * 2,
        scratch_shapes=[pltpu.VMEM((S, LANE), BF16)] * 3 + [pltpu.VMEM((S, LANE), F32)] * 2
                       + [pltpu.VMEM((nbias, 2 * t, kw), F32)],
        compiler_params=_cp(),
    )(slopes, qn, kn, proj)


def _mla_attn_bwd(q, k, v, do, lse, delta, B, S):
    T = B * S
    tq = _tile(S, MLA_TQ)
    tk = _tile(tq, MLA_TK)
    nd = tq // tk
    scale = MLA_QK ** -0.5
    heads, pair = _mla_specs(S)

    def body(q_ref, k_ref, v_ref, do_ref, lse_ref, dl_ref, dq_ref, dk_ref, dv_ref):
        dk_ref[...] = jnp.zeros_like(dk_ref)
        dv_ref[...] = jnp.zeros_like(dv_ref)
        lo = _lane_lo((tq, LANE))
        diag = [_causal_bias(tq, tk, i * tk) for i in range(nd)]

        def block(g, _):
            row0 = pl.multiple_of(g * tq, tq)
            rows = pl.ds(row0, tq)
            per_head = []
            for hh in range(2):
                sel = lo if hh == 0 else jnp.logical_not(lo)
                per_head.append((q_ref[hh, rows, :], jnp.where(sel, do_ref[rows, :], jnp.zeros((), BF16)),
                                 jnp.max(jnp.where(sel, lse_ref[rows, :], NEG), axis=-1, keepdims=True),
                                 jnp.max(jnp.where(sel, dl_ref[rows, :], NEG), axis=-1, keepdims=True)))

            def step(off, dq_accs, bias):
                cols = pl.ds(pl.multiple_of(off, tk), tk)
                vt = v_ref[cols, :]
                out, dv = [], None
                for hh, (qh, doh, lse_h, dl_h) in enumerate(per_head):
                    kh = k_ref[hh, cols, :]
                    s = _mm_nt(qh, kh) * scale
                    if bias is not None:
                        s = s + bias
                    p = jnp.exp(s - lse_h)
                    ds = (p * (_mm_nt(doh, vt) - dl_h)).astype(BF16)
                    dk_ref[hh, cols, :] += _mm_tn(ds, qh) * scale
                    part = _mm_tn(p, doh)
                    dv = part if dv is None else dv + part
                    out.append(dq_accs[hh] + _mm(ds, kh))
                dv_ref[cols, :] += dv
                return tuple(out)

            zero = jnp.zeros((tq, LANE), F32)
            dq_accs = lax.fori_loop(0, g * nd, lambda i, a: step(i * tk, a, None), (zero, zero))
            for i in range(nd):
                dq_accs = step(row0 + i * tk, dq_accs, diag[i])
            for hh in range(2):
                dq_ref[hh, rows, :] = dq_accs[hh] * scale
            return 0

        lax.fori_loop(0, S // tq, block, 0)

    return pl.pallas_call(
        body, name="mla_attn_bwd", grid=(B, NPAIR), in_specs=[heads, heads, pair, pair, pair, pair],
        out_specs=[heads, heads, pair],
        out_shape=[jax.ShapeDtypeStruct((MLA_HEADS, T, LANE), F32), jax.ShapeDtypeStruct((MLA_HEADS, T, LANE), F32),
                   jax.ShapeDtypeStruct((T, MLA_HEADS * MLA_V), F32)],
        compiler_params=_cp(),
    )(q, k, v, do, lse, delta)


def _dil_attn_bwd(gi, slopes, qn, kn, proj, do, lse, delta, through, B, S):
    d, L, t, window, back = _dil_geometry(gi, S)
    kw, nq = back + t, L // t
    nbias = 2 if back else 1
    scale = DIL_HEAD_DIM ** -0.5
    qk, vspec, pair = _dil_specs(gi, S)

    def body(*refs):
        refs = list(refs)
        sl_ref, q_ref, k_ref, v_ref, do_ref, lse_ref, dl_ref = refs[:7]
        dq_ref, dk_ref, dv_ref, qs, ks, vs, dos, lss, dls, dqs, dks, dvs, bias_ref = refs[-13:]
        _to_classes(q_ref, qs, d, L, scale)
        for src, dst in ((k_ref, ks), (v_ref, vs), (do_ref, dos), (lse_ref, lss), (dl_ref, dls)):
            _to_classes(src, dst, d, L)
        _dil_bias(bias_ref, sl_ref, pl.program_id(1), t, kw, back, window)
        dks[...] = jnp.zeros_like(dks)
        dvs[...] = jnp.zeros_like(dvs)
        lo = _lane_lo((t, LANE))

        def stats(ref, rows):
            x = ref[rows, :]
            return jnp.concatenate([jnp.max(jnp.where(lo, x, NEG), axis=-1, keepdims=True),
                                    jnp.max(jnp.where(lo, NEG, x), axis=-1, keepdims=True)], axis=0)

        def block(g, _):
            qb = g % nq if d > 1 else g
            row0 = pl.multiple_of(g * t, t)
            rows = pl.ds(row0, t)
            early = qb * t < back
            keys = pl.ds(pl.multiple_of(jnp.where(early, row0 - qb * t, row0 - back), t), kw)
            q2 = _stack_heads(qs[rows, :], lo)
            do2 = _stack_heads(dos[rows, :], lo)
            kt = ks[keys, :]
            s = _mm_nt(q2, kt) + bias_ref[jnp.where(early, 0, nbias - 1)]
            p = jnp.exp(s - stats(lss, rows))
            ds = (p * (_mm_nt(do2, vs[keys, :]) - stats(dls, rows))).astype(BF16)
            dq2 = _mm(ds, kt) * scale
            dqs[rows, :] = jnp.where(lo, dq2[:t], dq2[t:])
            dks[keys, :] += _mm_tn(ds, q2)
            dvs[keys, :] += _mm_tn(p, do2)
            return 0

        lax.fori_loop(0, d * nq, block, 0, unroll=DIL_UNROLL if d * nq % DIL_UNROLL == 0 else 1)
        for src, dst in ((dqs, dq_ref), (dks, dk_ref), (dvs, dv_ref)):
            _from_classes(src, dst, d, L)

    in_specs = [pl.BlockSpec(memory_space=pltpu.SMEM), qk, qk, vspec, pair, pair, pair]
    args = [slopes, qn, kn, proj, do, lse, delta]
    aliases = {}
    if through is not None:
        aliases = {len(args) + i: i for i in range(3)}
        in_specs = in_specs + [pl.BlockSpec(memory_space=pl.ANY)] * 3
        args = args + list(through)
    return pl.pallas_call(
        body, name=f"dil_attn_bwd_{gi}", grid=(B, NPAIR), in_specs=in_specs, out_specs=[qk, qk, qk],
        out_shape=[jax.ShapeDtypeStruct((B * S, DIL_QK), F32)] * 3,
        scratch_shapes=[pltpu.VMEM((S, LANE), BF16)] * 4 + [pltpu.VMEM((S, LANE), F32)] * 5
                       + [pltpu.VMEM((nbias, 2 * t, kw), F32)],
        input_output_aliases=aliases,
        compiler_params=_cp(),
    )(*args)


def _merge_proj_specs(ts):
    wide = lambda c0, w: pl.BlockSpec((ts, w), lambda i: (i, c0 * LANE // w))
    return [wide(CB_BZ, DIL_WIDTH), wide(CB_CZ, DIL_WIDTH)] + [wide(CB_GATE + 8 * i, D_MODEL) for i in range(3)]


def _merge_common(p_refs, bg_ref, ob_ref, og_refs, lse_refs):
    bz = p_refs[0][...].astype(F32)
    cz = p_refs[1][...].astype(F32)
    gates = [_sigmoid(p_refs[2 + i][...].astype(F32) + bg_ref[:, i * D_MODEL:(i + 1) * D_MODEL]) for i in range(3)]
    ob = ob_ref[...]
    lses = [r[...] for r in lse_refs]
    mx = jnp.maximum(jnp.maximum(lses[0], lses[1]), lses[2])
    es = [jnp.exp(v - mx) for v in lses]
    inv = 1.0 / (es[0] + es[1] + es[2])
    alphas = [e * inv for e in es]
    oc = alphas[0] * og_refs[0][...] + alphas[1] * og_refs[1][...] + alphas[2] * og_refs[2][...]
    return bz, cz, gates, ob, alphas, oc


def _merge_fwd(x, proj, b_gate, ya, ob, ogs, lses, woa, wob, woc, wo):
    T = x.shape[0]
    ts = _tile(T, 256)

    def body(x_ref, p0, p1, p2, p3, p4, bg_ref, ya_ref, ob_ref, og0, og1, og2, l0, l1, l2,
             woa_ref, wob_ref, woc_ref, wo_ref, out_ref):
        bz, cz, gates, obv, alphas, oc = _merge_common((p0, p1, p2, p3, p4), bg_ref, ob_ref, (og0, og1, og2),
                                                       (l0, l1, l2))
        yb = obv * _silu(bz)
        yc = oc * _silu(cz)
        merged = (gates[0] * _mm(ya_ref[...], woa_ref[...]) + gates[1] * _mm(yb, wob_ref[...])
                  + gates[2] * _mm(yc, woc_ref[...]))
        out_ref[...] = x_ref[...] + _mm(merged, wo_ref[...])

    def whole(r, c):
        return pl.BlockSpec((r, c), lambda i: (0, 0))

    tok = lambda w: pl.BlockSpec((ts, w), lambda i: (i, 0))
    return pl.pallas_call(
        body, name="merge_fwd", grid=(T // ts,),
        in_specs=[tok(D_MODEL)] + _merge_proj_specs(ts) + [whole(1, 3 * D_MODEL), tok(CONV_WIDTH)]
                 + [tok(DIL_WIDTH)] * 7 + [whole(CONV_WIDTH, D_MODEL)] * 3 + [whole(D_MODEL, D_MODEL)],
        out_specs=tok(D_MODEL),
        out_shape=jax.ShapeDtypeStruct((T, D_MODEL), F32),
        compiler_params=_cp(),
    )(x, *[proj] * 5, b_gate, ya, ob, *ogs, *lses, woa, wob, woc, wo)


def _merge_bwd(dout, proj, b_gate, ya, ob, ogs, lses, woa, wob, woc, wo):
    T = dout.shape[0]
    ts = _tile(T, 256)
    nt = T // ts

    def body(do_ref, p0, p1, p2, p3, p4, bg_ref, ya_ref, ob_ref, og0, og1, og2, l0, l1, l2,
             woa_ref, wob_ref, woc_ref, wo_ref,
             dp_ref, dya_ref, dob_ref, dlb_ref, dg0, dg1, dg2, dl0, dl1, dl2,
             mg_ref, dpa_ref, dpb_ref, dpc_ref, yb_ref, yc_ref, dbg_ref, st_bz, st_cz, st_gate, sems):
        step = pl.program_id(0)
        slot = step % 2

        def copies_of(s):
            return _put_copies([st_bz, st_cz, st_gate], dp_ref, sems, s % 2, pl.ds(pl.multiple_of(s * ts, ts), ts),
                               [CB_BZ * LANE, CB_CZ * LANE, CB_GATE * LANE])

        @pl.when(step >= 2)
        def _():
            for cp in copies_of(step - 2):
                cp.wait()

        bz, cz, gates, obv, alphas, oc = _merge_common((p0, p1, p2, p3, p4), bg_ref, ob_ref, (og0, og1, og2),
                                                       (l0, l1, l2))
        (sb, dsb), (sc, dsc) = _silu_and_grad(bz), _silu_and_grad(cz)
        yb = obv * sb
        yc = oc * sc
        ps = [_mm(ya_ref[...], woa_ref[...]), _mm(yb, wob_ref[...]), _mm(yc, woc_ref[...])]
        mg_ref[...] = (gates[0] * ps[0] + gates[1] * ps[1] + gates[2] * ps[2]).astype(BF16)
        yb_ref[...] = yb.astype(BF16)
        yc_ref[...] = yc.astype(BF16)
        dm = _mm_nt(do_ref[...], wo_ref[...])
        dps = []
        first = pl.program_id(0) == 0
        for i, dref in enumerate((dpa_ref, dpb_ref, dpc_ref)):
            g = gates[i]
            dpi = (dm * g).astype(BF16)
            dref[...] = dpi
            dps.append(dpi)
            dgp = dm * ps[i] * g * (1.0 - g)
            st_gate[slot, :, i * D_MODEL:(i + 1) * D_MODEL] = dgp.astype(BF16)
            part = jnp.sum(dgp, axis=0, keepdims=True)

            @pl.when(first)
            def _():
                dbg_ref[:, i * D_MODEL:(i + 1) * D_MODEL] = part

            @pl.when(jnp.logical_not(first))
            def _():
                dbg_ref[:, i * D_MODEL:(i + 1) * D_MODEL] += part

        dya_ref[...] = _mm_nt(dps[0], woa_ref[...])
        dyb = _mm_nt(dps[1], wob_ref[...])
        dyc = _mm_nt(dps[2], woc_ref[...])
        st_bz[slot] = (dyb * obv * dsb).astype(BF16)
        st_cz[slot] = (dyc * oc * dsc).astype(BF16)
        for cp in copies_of(step):
            cp.start()
        dob = dyb * sb
        doc = dyc * sc
        dob_ref[...] = dob.astype(BF16)
        for c in range(NPAIR):
            cs = slice(c * LANE, (c + 1) * LANE)
            dlb_ref[:, cs] = _head_bcast_sum(dob[:, cs] * obv[:, cs])
            dd = _head_bcast_sum(doc[:, cs] * oc[:, cs])
            for a, dref, lref in zip(alphas, (dg0, dg1, dg2), (dl0, dl1, dl2)):
                dref[:, cs] = (a[:, cs] * doc[:, cs]).astype(BF16)
                lref[:, cs] = a[:, cs] * dd

        @pl.when(step == nt - 1)
        def _():
            if nt >= 2:
                for cp in copies_of(step - 1):
                    cp.wait()
            for cp in copies_of(step):
                cp.wait()

    def whole(r, c):
        return pl.BlockSpec((r, c), lambda i: (0, 0))

    tok = lambda w: pl.BlockSpec((ts, w), lambda i: (i, 0))
    sd = jax.ShapeDtypeStruct
    W = DIL_WIDTH
    return pl.pallas_call(
        body, name="merge_bwd", grid=(nt,),
        in_specs=[tok(D_MODEL)] + _merge_proj_specs(ts) + [whole(1, 3 * D_MODEL), tok(CONV_WIDTH)] + [tok(W)] * 7
                 + [whole(CONV_WIDTH, D_MODEL)] * 3 + [whole(D_MODEL, D_MODEL)],
        out_specs=[pl.BlockSpec(memory_space=pl.ANY), tok(CONV_WIDTH), tok(W), tok(W)] + [tok(W)] * 6
                  + [tok(D_MODEL)] * 4 + [tok(W), tok(W), whole(1, 3 * D_MODEL)],
        out_shape=[sd((T, PP), BF16), sd((T, CONV_WIDTH), F32), sd((T, W), BF16), sd((T, W), F32)]
                  + [sd((T, W), BF16)] * 3 + [sd((T, W), F32)] * 3
                  + [sd((T, D_MODEL), BF16)] * 4 + [sd((T, W), BF16)] * 2 + [sd((1, 3 * D_MODEL), F32)],
        scratch_shapes=[pltpu.VMEM((2, ts, W), BF16), pltpu.VMEM((2, ts, W), BF16),
                        pltpu.VMEM((2, ts, 3 * D_MODEL), BF16), pltpu.SemaphoreType.DMA((2, 3))],
        compiler_params=_cp(),
    )(dout, *[proj] * 5, b_gate, ya, ob, *ogs, *lses, woa, wob, woc, wo)


def _loss_head(y, target):
    T = y.shape[0]
    ts = _tile(T, 512)

    def body(y_ref, t_ref, d_ref, l_ref):
        e = y_ref[...] - t_ref[...]
        d_ref[...] = e * (1.0 / D_MODEL)
        l_ref[...] = jnp.zeros((1, 8, LANE), F32) + jnp.sum(e * e)

    tok = pl.BlockSpec((ts, D_MODEL), lambda i: (i, 0))
    return pl.pallas_call(
        body, name="loss_head", grid=(T // ts,), in_specs=[tok, tok],
        out_specs=[tok, pl.BlockSpec((1, 8, LANE), lambda i: (i, 0, 0))],
        out_shape=[jax.ShapeDtypeStruct((T, D_MODEL), F32), jax.ShapeDtypeStruct((T // ts, 8, LANE), F32)],
        compiler_params=_cp(),
    )(y, target)


def _my_index():
    return 4 * lax.axis_index("x") + 2 * lax.axis_index("y") + lax.axis_index("c")


def _peers():
    x, y, c = (lax.axis_index(a) for a in AXES)
    out = []
    for kk in range(1, N_DEV):
        px = 1 - x if kk & 4 else x
        py = 1 - y if kk & 2 else y
        pc = 1 - c if kk & 1 else c
        out.append(((px, py, pc), 4 * px + 2 * py + pc))
    return out


def _exchange(arrays, name, gather):
    n = len(arrays)

    def body(*refs):
        srcs, outs = refs[:n], refs[n:2 * n]
        send_sems, recv_sems, local_sems = refs[2 * n:]
        me = _my_index()
        peers = _peers()
        started = []
        for a, (src, out) in enumerate(zip(srcs, outs)):
            mine = pltpu.make_async_copy(src if gather else src.at[me], out.at[me], local_sems.at[a])
            mine.start()
            started.append(mine)
        sends = []
        for i, (pos, idx) in enumerate(peers):
            for a, (src, out) in enumerate(zip(srcs, outs)):
                cp = pltpu.make_async_remote_copy(
                    src_ref=src if gather else src.at[idx], dst_ref=out.at[me], send_sem=send_sems.at[a, i],
                    recv_sem=recv_sems.at[a, i], device_id=pos, device_id_type=pl.DeviceIdType.MESH)
                cp.start()
                sends.append(cp)
        for i, (pos, idx) in enumerate(peers):
            for a, (src, out) in enumerate(zip(srcs, outs)):
                pltpu.make_async_remote_copy(
                    src_ref=src if gather else src.at[idx], dst_ref=out.at[idx], send_sem=send_sems.at[a, i],
                    recv_sem=recv_sems.at[a, i], device_id=pos, device_id_type=pl.DeviceIdType.MESH).wait_recv()
        for cp in sends:
            cp.wait_send()
        for mine in started:
            mine.wait()

    any_space = pl.BlockSpec(memory_space=pl.ANY)
    return pl.pallas_call(
        body, name=name, in_specs=[any_space] * n, out_specs=[any_space] * n,
        out_shape=[jax.ShapeDtypeStruct(((N_DEV,) + a.shape) if gather else a.shape, a.dtype) for a in arrays],
        scratch_shapes=[pltpu.SemaphoreType.DMA((n, N_DEV - 1)), pltpu.SemaphoreType.DMA((n, N_DEV - 1)),
                        pltpu.SemaphoreType.DMA((n,))],
    )(*arrays)


N_CHIP = 4


def _chip_places():
    x, y, c = (lax.axis_index(a) for a in AXES)
    return (x, y, c), (x, y, 1 - c), [(1 - x, y, c), (x, 1 - y, c), (1 - x, 1 - y, c)]


def _index_of(pos):
    return 4 * pos[0] + 2 * pos[1] + pos[2]


def _sibling_swap(arrays, name):
    n = len(arrays)

    def body(*refs):
        srcs, outs = refs[:n], refs[n:2 * n]
        send_sems, recv_sems = refs[2 * n:]
        (x, y, c), sibling, _ = _chip_places()
        sends = []
        for a, (src, out) in enumerate(zip(srcs, outs)):
            for q in range(N_CHIP):
                def copy(core, a=a, q=q, src=src, out=out):
                    return pltpu.make_async_remote_copy(
                        src_ref=src.at[2 * q + core], dst_ref=out.at[q], send_sem=send_sems.at[N_CHIP * a + q],
                        recv_sem=recv_sems.at[N_CHIP * a + q], device_id=sibling, device_id_type=pl.DeviceIdType.MESH)
                mine = copy(1 - c)
                mine.start()
                sends.append((mine, copy(c)))
        for mine, arrival in sends:
            arrival.wait_recv()
            mine.wait_send()

    any_space = pl.BlockSpec(memory_space=pl.ANY)
    return pl.pallas_call(
        body, name=name, in_specs=[any_space] * n, out_specs=[any_space] * n,
        out_shape=[jax.ShapeDtypeStruct((N_CHIP,) + a.shape[1:], a.dtype) for a in arrays],
        scratch_shapes=[pltpu.SemaphoreType.DMA((N_CHIP * n,)), pltpu.SemaphoreType.DMA((N_CHIP * n,))],
    )(*arrays)


def _chip_pair_sum(part, got, name):
    R, C = part.shape[1:]
    tr = R
    while tr * C * part.dtype.itemsize > REDUCE_BLOCK_BYTES // 4 and tr % 32 == 0:
        tr //= 2
    c = lax.axis_index("c")

    def body(c_ref, p_ref, g_ref, o_ref):
        del c_ref
        o_ref[...] = (p_ref[...].astype(F32) + g_ref[...].astype(F32)).astype(o_ref.dtype)

    return pl.pallas_call(
        body, name=name, grid_spec=pltpu.PrefetchScalarGridSpec(
            num_scalar_prefetch=1, grid=(N_CHIP, R // tr),
            in_specs=[pl.BlockSpec((None, tr, C), lambda q, i, cr: (2 * q + cr[0], i, 0)),
                      pl.BlockSpec((None, tr, C), lambda q, i, cr: (q, i, 0))],
            out_specs=pl.BlockSpec((None, tr, C), lambda q, i, cr: (q, i, 0))),
        out_shape=jax.ShapeDtypeStruct((N_CHIP, R, C), part.dtype),
        compiler_params=_cp(),
    )(jnp.reshape(c, (1,)).astype(jnp.int32), part, got)


def _peer_count(mode):
    return {"chips": N_CHIP - 1, "near": N_CHIP}.get(mode, N_DEV - 1)


def _remote_copies(srcs, lands, send_sems, recv_sems, mode):
    if mode == "chips":
        (x, y, _), _, others = _chip_places()
        my_slot, peers = 2 * x + y, [(chip, 2 * chip[0] + chip[1]) for chip in others]
    elif mode == "near":
        me, sibling, others = _chip_places()
        my_slot, peers = _index_of(me), [(pos, _index_of(pos)) for pos in [sibling] + others]
    else:
        my_slot, peers = _my_index(), _peers()
    whole = mode in ("gather", "near")
    out = []
    for i, (pos, idx) in enumerate(peers):
        for a, (src, land) in enumerate(zip(srcs, lands)):
            def copy(slot, a=a, src=src, land=land, i=i, pos=pos, idx=idx):
                return pltpu.make_async_remote_copy(
                    src_ref=src if whole else src.at[idx], dst_ref=land.at[slot],
                    send_sem=send_sems.at[a * len(peers) + i], recv_sem=recv_sems.at[a * len(peers) + i],
                    device_id=pos, device_id_type=pl.DeviceIdType.MESH)
            out.append((copy(my_slot), copy(idx)))
    return out


def _exchange_start(arrays, name, mode):
    n = len(arrays)
    hbm = pl.BlockSpec(memory_space=pltpu.HBM)
    sem = pl.BlockSpec(memory_space=pltpu.SEMAPHORE)
    lands = [lax.empty(((N_DEV,) + a.shape) if mode in ("gather", "near") else a.shape, a.dtype) for a in arrays]

    def body(*refs):
        srcs, lands_ = refs[:n], refs[n:2 * n]
        send_sems, recv_sems = refs[2 * n:2 * n + 2]
        for mine, _ in _remote_copies(srcs, lands_, send_sems, recv_sems, mode):
            mine.start()
        refs[-1][...] = jnp.zeros_like(refs[-1])

    sems = pltpu.SemaphoreType.DMA((n * _peer_count(mode),))
    buffers = [pltpu.HBM(a.shape, a.dtype) for a in list(arrays) + lands]
    res = pl.pallas_call(
        body, name=name, in_specs=[hbm] * (2 * n), out_specs=[sem, sem] + [hbm] * (2 * n) + [pl.BlockSpec(memory_space=pltpu.VMEM)],
        out_shape=[sems, sems] + buffers + [jax.ShapeDtypeStruct((8, LANE), F32)],
        input_output_aliases={i: 2 + i for i in range(2 * n)},
        compiler_params=pltpu.CompilerParams(has_side_effects=pltpu.SideEffectType.DATAFLOW_SIDE_EFFECTING),
    )(*[pltpu.with_memory_space_constraint(a, pltpu.HBM) for a in list(arrays) + lands])
    return (res[0], res[1], res[2:2 + n], res[2 + n:2 + 2 * n]), res[-1]


def _exchange_wait(handle, after, name, mode):
    send_sems, recv_sems, srcs, lands = handle
    n = len(srcs)
    after = list(after) if isinstance(after, (list, tuple)) else [after]
    hbm = pl.BlockSpec(memory_space=pltpu.HBM)
    sem = pl.BlockSpec(memory_space=pltpu.SEMAPHORE)

    def body(*refs):
        for mine, arrival in _remote_copies(refs[:n], refs[n:2 * n], refs[2 * n], refs[2 * n + 1], mode):
            mine.wait_send()
            arrival.wait_recv()

    res = pl.pallas_call(
        body, name=name, in_specs=[hbm] * (2 * n) + [sem, sem] + [pl.BlockSpec(memory_space=pl.ANY)] * len(after),
        out_specs=[hbm] * (2 * n), out_shape=[pltpu.HBM(a.shape, a.dtype) for a in list(srcs) + list(lands)],
        input_output_aliases={i: i for i in range(2 * n)},
        compiler_params=pltpu.CompilerParams(has_side_effects=pltpu.SideEffectType.DATAFLOW_SIDE_EFFECTING),
    )(*srcs, *lands, send_sems, recv_sems, *after)
    return res[n:]


def _sibling_forward(lands, name):
    n = len(lands)

    def body(*refs):
        ins, outs, send_sems, recv_sems = refs[:n], refs[n:2 * n], refs[2 * n], refs[2 * n + 1]
        (x, y, c), sibling, others = _chip_places()
        copies = []
        for a, (src, out) in enumerate(zip(ins, outs)):
            for j, chip in enumerate(others):
                def copy(core, a=a, j=j, chip=chip, src=src, out=out):
                    slot = _index_of((chip[0], chip[1], core))
                    return pltpu.make_async_remote_copy(
                        src_ref=src.at[slot], dst_ref=out.at[slot], send_sem=send_sems.at[3 * a + j],
                        recv_sem=recv_sems.at[3 * a + j], device_id=sibling, device_id_type=pl.DeviceIdType.MESH)
                mine = copy(c)
                mine.start()
                copies.append((mine, copy(1 - c)))
        for mine, arrival in copies:
            arrival.wait_recv()
        for mine, arrival in copies:
            mine.wait_send()

    any_space = pl.BlockSpec(memory_space=pl.ANY)
    return pl.pallas_call(
        body, name=name, in_specs=[any_space] * n, out_specs=[any_space] * n,
        out_shape=[jax.ShapeDtypeStruct(a.shape, a.dtype) for a in lands],
        scratch_shapes=[pltpu.SemaphoreType.DMA((3 * n,)), pltpu.SemaphoreType.DMA((3 * n,))],
        input_output_aliases={i: i for i in range(n)},
    )(*lands)


def _own_slot(land, mine, slot=None):
    slot = _my_index() if slot is None else slot
    return lax.dynamic_update_slice(land, mine, (slot,) + (0,) * (land.ndim - 1))


def _adamw(w, g, m, v):
    m = ADAM_B1 * m + (1.0 - ADAM_B1) * g
    v = ADAM_B2 * v + (1.0 - ADAM_B2) * (g * g)
    m_hat = m / (1.0 - ADAM_B1 ** ADAM_STEP)
    v_hat = v / (1.0 - ADAM_B2 ** ADAM_STEP)
    delta = -ADAM_LR * (m_hat / (jnp.sqrt(v_hat) + ADAM_EPS) + ADAM_WD * w)
    return delta, m, v


def _reduce_adamw(parts, w, m, v, name, after=None):
    nparts = len(parts)
    R, C = parts[0].shape[1:]
    tr = R
    while N_DEV * tr * C * parts[0].dtype.itemsize > REDUCE_BLOCK_BYTES and tr % 32 == 0:
        tr //= 2
    steps = R // tr
    extra = [] if after is None else [after]

    def body(*refs):
        w_ref, m_ref, v_ref, g_ref, d_ref, nm_ref, nv_ref = refs[nparts + len(extra):]
        for k, p_ref in enumerate(refs[:nparts]):
            @pl.when(pl.program_id(0) // steps == k)
            def _():
                g = p_ref[0].astype(F32)
                for s in range(1, p_ref.shape[0]):
                    g = g + p_ref[s].astype(F32)
                g_ref[...] = g
                d_ref[...], nm_ref[...], nv_ref[...] = _adamw(w_ref[...], g, m_ref[...], v_ref[...])

    def part_spec(k):
        return pl.BlockSpec((parts[k].shape[0], tr, C), lambda i: (0, jnp.clip(i - k * steps, 0, steps - 1), 0))

    row = pl.BlockSpec((tr, C), lambda i: (i, 0))
    return pl.pallas_call(
        body, name=name, grid=(nparts * steps,),
        in_specs=[part_spec(k) for k in range(nparts)] + [pl.BlockSpec(memory_space=pl.ANY)] * len(extra)
                 + [row, row, row],
        out_specs=[row] * 4, out_shape=[jax.ShapeDtypeStruct((nparts * R, C), F32)] * 4,
        compiler_params=_cp(),
    )(*parts, *extra, w, m, v)


BIG = ("w_in", "w_uq", "w_ukv", "w_out_a", "w_out_b", "w_out_c", "w_o")
SMALL = ("norm_g", "b_gate", "conv_w", "conv_b", "q_a_norm_g", "kv_a_norm_g", "mla_q_norm_g", "mla_k_norm_g",
         "dil_q_norm_g", "dil_k_norm_g")
PACK_ROWS = 128
REDUCE_BLOCK_BYTES = 6 * 1024 * 1024


def _pack_local(tensors):
    flat = jnp.concatenate([t.reshape(-1) for t in tensors])
    pad = (-flat.shape[0]) % (PACK_ROWS * LANE)
    return jnp.concatenate([flat, jnp.zeros((pad,), flat.dtype)]).reshape(-1, LANE)


def _unpack_local(rows, like):
    flat = rows.reshape(-1)
    out, off = [], 0
    for t in like:
        out.append(flat[off:off + t.size].reshape(t.shape))
        off += t.size
    return out


def _cols_to_slots(a):
    k = a.shape[0]
    return a.reshape(k, N_DEV, -1).transpose(1, 0, 2)


def _slots_to_cols(s):
    return s.transpose(1, 0, 2).reshape(s.shape[1], -1)


def _rope_tables(S):
    inv = ROPE_THETA ** (-jnp.arange(0, MLA_ROPE, 2, dtype=F32) / MLA_ROPE)
    ang = jnp.arange(S, dtype=F32)[:, None] * inv[None, :]
    cos, sin = jnp.cos(ang), jnp.sin(ang)
    one = jnp.ones((S, MLA_NOPE), F32)
    z16, z32, z64 = (jnp.zeros((S, n), F32) for n in (16, 32, 64))
    cosp = jnp.concatenate([one, cos, cos, jnp.ones((S, 32), F32)], axis=1)
    sa = jnp.concatenate([z64, -sin, z16, z32], axis=1)
    sb = jnp.concatenate([z64, z16, sin, z32], axis=1)
    return cosp, sa, sb


def _alibi_slopes():
    n = DIL_GROUPS * DIL_HEADS
    m = 2.0 ** (-8.0 * jnp.arange(1, n + 1, dtype=F32) / n)
    return m.reshape(DIL_GROUPS, NPAIR, 2)


def _pad_slots(s):
    n, k, c = s.shape
    return _slots_to_cols(jnp.concatenate([s, jnp.zeros((n, k, LANE - c), s.dtype)], axis=2))


def _layer_params(gw, small, l):
    p = {}
    p["wp"] = _pad_columns(gw["w_in"])
    p["norm_g"] = small["norm_g"][l][None]
    p["b_gate"] = small["b_gate"][l][None]
    p["conv_w"] = gw["conv_w"].transpose(1, 0, 2).reshape(CONV_K, CONV_WIDTH)
    p["conv_b"] = small["conv_b"][l][None]
    p["gq"] = small["q_a_norm_g"][l][None]
    p["gkv"] = small["kv_a_norm_g"][l][None]
    p["wuqp"] = _pad_slots(gw["w_uq"])
    kv = gw["w_ukv"]
    p["wkp"] = _pad_slots(kv[:, :, :MLA_NOPE])
    p["wv"] = kv[:, :, MLA_NOPE:].transpose(1, 0, 2).reshape(MLA_KV_LORA, MLA_HEADS * MLA_V)
    zpad = jnp.zeros((1, LANE - MLA_QK), F32)
    p["gmq"] = jnp.concatenate([small["mla_q_norm_g"][l][None], zpad], axis=1)
    p["gmk"] = jnp.concatenate([small["mla_k_norm_g"][l][None], zpad], axis=1)
    tile = lambda g: jnp.broadcast_to(g[:, None, :], (DIL_GROUPS, DIL_HEADS, DIL_HEAD_DIM)).reshape(1, DIL_QK)
    p["gdq"] = tile(small["dil_q_norm_g"][l])
    p["gdk"] = tile(small["dil_k_norm_g"][l])
    p["woa"], p["wob"], p["woc"] = (_slots_to_cols(gw[n]) for n in ("w_out_a", "w_out_b", "w_out_c"))
    p["wo"] = gw["w_o"].reshape(D_MODEL, D_MODEL)
    return p


def _layer_fwd(x, p, tabs, slopes, B, S):
    proj, ht = _inproj_fwd(x, p["norm_g"], p["wp"])
    ya = _mixa_fwd(proj, p["conv_w"], p["conv_b"], B, S)
    q, k, v = _mla_prep_fwd(proj, p["gq"], p["gkv"], p["wuqp"], p["wkp"], p["wv"], p["gmq"], p["gmk"], *tabs, S)
    ob, lse_b = _mla_attn_fwd(q, k, v, B, S)
    qn, kn = _dil_prep_fwd(proj, p["gdq"], p["gdk"])
    ogs, lses = [], []
    for gi in range(DIL_GROUPS):
        o, lse = _dil_attn_fwd(gi, slopes[gi], qn, kn, proj, B, S)
        ogs.append(o)
        lses.append(lse)
    out = _merge_fwd(x, proj, p["b_gate"], ya, ob, ogs, lses, p["woa"], p["wob"], p["woc"], p["wo"])
    saved = dict(x=x, proj=proj, ht=ht, ya=ya, q=q, k=k, v=v, ob=ob, lse_b=lse_b, qn=qn, kn=kn, ogs=ogs, lses=lses)
    return out, saved


def _layer_bwd(dout, sv, p, tabs, slopes, B, S, big_ready=None):
    proj = sv["proj"]
    (dproj, dya, dob, dlb, dg0, dg1, dg2, dl0, dl1, dl2, merged, dpa, dpb, dpc, yb, yc, dbg) = _merge_bwd(
        dout, proj, p["b_gate"], sv["ya"], sv["ob"], sv["ogs"], sv["lses"], p["woa"], p["wob"], p["woc"], p["wo"])
    g = {}
    g["w_o"] = _matmul_tn(merged, dout, "dw_o").reshape(N_DEV, D_MODEL // N_DEV, D_MODEL)
    g["w_out_a"] = _cols_to_slots(_matmul_tn(sv["ya"], dpa, "dw_out_a"))
    g["w_out_b"] = _cols_to_slots(_matmul_tn(yb, dpb, "dw_out_b"))
    g["w_out_c"] = _cols_to_slots(_matmul_tn(yc, dpc, "dw_out_c"))
    g["b_gate"] = dbg[0]
    dproj, st = _mixa_bwd(dproj, dya, proj, p["conv_w"], p["conv_b"], B, S)
    g["conv_w"] = st[0:CONV_K]
    g["conv_b"] = st[CONV_K]
    dq, dk, dv = _mla_attn_bwd(sv["q"], sv["k"], sv["v"], dob, sv["lse_b"], dlb, B, S)
    dproj, dwuqp, dwkp, dwv, dgq, dgkv, dgmq, dgmk = _mla_prep_bwd(
        dproj, dq, dk, dv, proj, p["gq"], p["gkv"], p["wuqp"], p["wkp"], p["wv"], p["gmq"], p["gmk"], *tabs, S)
    g["w_uq"] = _cols_to_slots(dwuqp)[:, :, :MLA_QK]
    g["w_ukv"] = jnp.concatenate([_cols_to_slots(dwkp)[:, :, :MLA_NOPE], _cols_to_slots(dwv)], axis=2)
    g["q_a_norm_g"], g["kv_a_norm_g"] = dgq[0], dgkv[0]
    g["mla_q_norm_g"], g["mla_k_norm_g"] = dgmq[0, :MLA_QK], dgmk[0, :MLA_QK]
    dqkv = None
    for gi, (dog, dlg) in enumerate(((dg0, dl0), (dg1, dl1), (dg2, dl2))):
        dqkv = _dil_attn_bwd(gi, slopes[gi], sv["qn"], sv["kn"], proj, dog, sv["lses"][gi], dlg, dqkv, B, S)
    dproj, dgdq, dgdk = _dil_prep_bwd(dproj, *dqkv, proj, p["gdq"], p["gdk"])
    g["dil_q_norm_g"] = dgdq.reshape(DIL_GROUPS, DIL_HEADS, DIL_HEAD_DIM).sum(axis=1)
    g["dil_k_norm_g"] = dgdk.reshape(DIL_GROUPS, DIL_HEADS, DIL_HEAD_DIM).sum(axis=1)
    token = None if big_ready is None else big_ready(g)
    g["w_in"] = _unpad_columns(_matmul_nn(sv["ht"], dproj, "dw_in", token))
    token = None if big_ready is None else big_ready(g)
    dx, dng = _inproj_bwd_x(dproj, p["wp"], sv["x"], _after(token, p["norm_g"]), dout)
    g["norm_g"] = dng[0]
    return dx, g


def _after(token, a):
    return a if token is None else a + token[0:1, 0:1]


def _local_step(x, target, small, B, S, weights_of, grads_out, big_ready=None):
    tabs = _rope_tables(S)
    sl = _alibi_slopes()
    slopes = [sl[gi] * float(DIL_PATTERNS[gi][1]) for gi in range(DIL_GROUPS)]
    params, saved = [], []
    for l in range(DEPTH):
        gw, token = weights_of(l, x)
        p = _layer_params(gw, small, l)
        p["norm_g"] = _after(token, p["norm_g"])
        x, sv = _layer_fwd(x, p, tabs, slopes, B, S)
        params.append(p)
        saved.append(sv)
    dout, lparts = _loss_head(x, target)
    sq = jnp.sum(lparts[:, 0, 0])
    token = None
    for l in reversed(range(DEPTH)):
        p = dict(params[l], b_gate=_after(token, params[l]["b_gate"]))
        ready = None if big_ready is None else (lambda g, l=l: big_ready(l, g))
        dout, g = _layer_bwd(dout, saved[l], p, tabs, slopes, B, S, ready)
        token = grads_out(l, g, dout)
    return sq, dout


def kernel(x, norm_g, w_in, b_gate, conv_w, conv_b, q_a_norm_g, w_uq, kv_a_norm_g, w_ukv, mla_q_norm_g, mla_k_norm_g, dil_q_norm_g, dil_k_norm_g, w_out_a, w_out_b, w_out_c, w_o, loss_target, m_norm_g, m_w_in, m_b_gate, m_conv_w, m_conv_b, m_q_a_norm_g, m_w_uq, m_kv_a_norm_g, m_w_ukv, m_mla_q_norm_g, m_mla_k_norm_g, m_dil_q_norm_g, m_dil_k_norm_g, m_w_out_a, m_w_out_b, m_w_out_c, m_w_o, v_norm_g, v_w_in, v_b_gate, v_conv_w, v_conv_b, v_q_a_norm_g, v_w_uq, v_kv_a_norm_g, v_w_ukv, v_mla_q_norm_g, v_mla_k_norm_g, v_dil_q_norm_g, v_dil_k_norm_g, v_w_out_a, v_w_out_b, v_w_out_c, v_w_o):
    names = ("norm_g", "w_in", "b_gate", "conv_w", "conv_b", "q_a_norm_g", "w_uq", "kv_a_norm_g", "w_ukv",
             "mla_q_norm_g", "mla_k_norm_g", "dil_q_norm_g", "dil_k_norm_g", "w_out_a", "w_out_b", "w_out_c", "w_o")
    w = dict(zip(names, (norm_g, w_in, b_gate, conv_w, conv_b, q_a_norm_g, w_uq, kv_a_norm_g, w_ukv, mla_q_norm_g,
                         mla_k_norm_g, dil_q_norm_g, dil_k_norm_g, w_out_a, w_out_b, w_out_c, w_o)))
    m = dict(zip(names, (m_norm_g, m_w_in, m_b_gate, m_conv_w, m_conv_b, m_q_a_norm_g, m_w_uq, m_kv_a_norm_g, m_w_ukv,
                         m_mla_q_norm_g, m_mla_k_norm_g, m_dil_q_norm_g, m_dil_k_norm_g, m_w_out_a, m_w_out_b,
                         m_w_out_c, m_w_o)))
    v = dict(zip(names, (v_norm_g, v_w_in, v_b_gate, v_conv_w, v_conv_b, v_q_a_norm_g, v_w_uq, v_kv_a_norm_g, v_w_ukv,
                         v_mla_q_norm_g, v_mla_k_norm_g, v_dil_q_norm_g, v_dil_k_norm_g, v_w_out_a, v_w_out_b,
                         v_w_out_c, v_w_o)))
    B, S, _ = x.shape
    me = _my_index()
    cshard = CONV_WIDTH // N_DEV

    shards = [[w[n][0].astype(BF16) for n in BIG]]
    state = {}

    def widen(t):
        return lax.dynamic_update_slice(jnp.zeros((DEPTH, CONV_K, CONV_WIDTH), F32), t, (0, 0, me * cshard))

    pick = lambda d: [widen(d[n]) if n == "conv_w" else d[n] for n in SMALL]

    def weights_of(l, after):
        if l == 0:
            first = shards[0] + [conv_w]
            handle, token = _exchange_start(first, "all_gather_weights_0_start", "near")
            zero = token[0:1, 0:1]
            state["shards1"] = [(w[n][1] + zero).astype(BF16) for n in BIG]
            for n in BIG:
                state["rows", n] = [a.reshape(-1, a.shape[-1]) + zero for a in (w[n], m[n], v[n])]
            state["small"] = [_pack_local(pick(d)) + zero for d in (w, m, v)]
            busy = state["shards1"] + [a for n in BIG for a in state["rows", n]] + state["small"]
            landed = _exchange_wait(handle, busy, "all_gather_weights_0_wait", "near")
            landed = _sibling_forward(landed, "all_gather_weights_0_forward")
            got = [_own_slot(a, s[None]) for a, s in zip(landed, first)]
            state["gather"], token = _exchange_start(state["shards1"], "all_gather_weights_1_start", "gather")
            state["conv_w"] = got[-1]
        else:
            landed = _exchange_wait(state["gather"], after, "all_gather_weights_1_wait", "gather")
            got, token = [_own_slot(a, s[None]) for a, s in zip(landed, state["shards1"])], None
        gw = dict(zip(BIG, got))
        gw["conv_w"] = state["conv_w"][:, l]
        return gw, token

    recv, small_parts = {}, {}
    my_chip = 2 * lax.axis_index("x") + lax.axis_index("y")

    REST = tuple(n for n in BIG if n != "w_in")

    def big_ready(l, g):
        if l == DEPTH - 1:
            if "w_in" not in g:
                return None
            send = [g[n].astype(BF16) for n in BIG]
            state["scatter"], token = _exchange_start(send, "exchange_weight_grads_1_start", "scatter")
            state["sent", "scatter"] = send
            return token
        tag, group = ("w_in", ("w_in",)) if "w_in" in g else ("rest", REST)
        send = [g[n].astype(BF16) for n in group]
        swapped = _sibling_swap(send, "exchange_weight_grads_0_sibling_" + tag)
        send = [_chip_pair_sum(s, t, "chip_pair_sum_" + n) for n, s, t in zip(group, send, swapped)]
        state[tag], token = _exchange_start(send, "exchange_weight_grads_0_start_" + tag, "chips")
        state["sent", tag] = send
        return token

    def grads_out(l, g, after):
        small_parts[l] = [g[n] for n in SMALL]
        if l == DEPTH - 1:
            return None
        got = {}
        for key, mode, slot in (("scatter", "scatter", me), ("rest", "chips", my_chip), ("w_in", "chips", my_chip)):
            k = DEPTH - 1 if key == "scatter" else 0
            tag = "" if key == "scatter" else "_" + key
            landed = _exchange_wait(state[key], after, f"exchange_weight_grads_{k}_wait{tag}", mode)
            mine = [lax.dynamic_slice_in_dim(s, slot, 1, axis=0) for s in state["sent", key]]
            got[key] = [_own_slot(a, s, slot) for a, s in zip(landed, mine)]
        recv[DEPTH - 1] = got["scatter"]
        recv[0] = got["w_in"] + got["rest"]
        assert BIG == ("w_in",) + REST
        return None

    sq, grad_x = _local_step(x.reshape(B * S, D_MODEL), loss_target.reshape(B * S, D_MODEL), w, B, S,
                             weights_of, grads_out, big_ready)
    loss = lax.psum(sq * (0.5 / D_MODEL), AXES)

    part = {n: jnp.stack([small_parts[l][i] for l in range(DEPTH)]) for i, n in enumerate(SMALL)}
    small_like = [part[n] for n in SMALL]
    pack = _pack_local(small_like)
    handle, token = _exchange_start([pack], "all_gather_small_grads_start", "gather")

    res, done = {}, []
    for i, n in enumerate(BIG):
        outs = _reduce_adamw([recv[l][i] for l in range(DEPTH)], *state["rows", n], "reduce_adamw_" + n, token)
        res[n] = tuple(a.reshape(w[n].shape) for a in outs)
        done.append(outs[0])

    landed, = _exchange_wait(handle, done, "all_gather_small_grads_wait", "gather")
    parts = _own_slot(landed, pack[None])
    gs, ds, ms, vs = _reduce_adamw([parts], *state["small"], "reduce_adamw_small")
    for n, t in zip(SMALL, zip(*(_unpack_local(a, small_like) for a in (gs, ds, ms, vs)))):
        if n == "conv_w":
            t = tuple(lax.dynamic_slice(a, (0, 0, me * cshard), (DEPTH, CONV_K, cshard)) for a in t)
        res[n] = t

    out = [loss, grad_x.reshape(B, S, D_MODEL)]
    for i in range(4):
        out += [res[n][i] for n in names]
    return tuple(out)
```

```python
import jax
import jax.numpy as jnp
from jax import lax
from jax.experimental import pallas as pl
from jax.experimental.pallas import tpu as pltpu

F32 = jnp.float32
BF16 = jnp.bfloat16

D_MODEL = 1024
DEPTH = 2
CONV_WIDTH = 512
CONV_K = 3
MLA_HEADS = 8
MLA_Q_LORA = 256
MLA_KV_LORA = 128
MLA_NOPE = 64
MLA_ROPE = 32
MLA_V = 64
MLA_QK = MLA_NOPE + MLA_ROPE
ROPE_THETA = 10000.0
DIL_PATTERNS = ((128, 1), (512, 4), (2048, 16))
DIL_GROUPS = 3
DIL_HEADS = 8
DIL_HEAD_DIM = 64
DIL_WIDTH = DIL_HEADS * DIL_HEAD_DIM
DIL_QK = DIL_GROUPS * DIL_WIDTH
EPS = 1e-6
N_IN = 11168

ADAM_LR = 0.001
ADAM_B1 = 0.9
ADAM_B2 = 0.999
ADAM_EPS = 1e-08
ADAM_WD = 0.01
ADAM_STEP = 10

N_DEV = 8
AXES = ("x", "y", "c")
LANE = 128
HALF = 64
NPAIR = 4

CB_AB, CB_AC, CB_AX, CB_AZ = 0, 4, 8, 12
CB_CQ, CB_CKV, CB_KPE = 16, 18, 19
CB_BZ = 20
CB_DQ, CB_DK, CB_DV = 24, 36, 48
CB_CZ, CB_GATE = 60, 64
NCB = 88
PP = NCB * LANE
KPE_END = CB_KPE * LANE + MLA_ROPE
SHARD_COLS = N_IN // N_DEV
NEG = -1e30
VMEM_LIMIT = 56 * 1024 * 1024


def _pad_columns(shards):
    K = shards.shape[1]
    gap = LANE - MLA_ROPE
    tr = _tile(K, 128)

    def body(s_ref, o_ref):
        for p in range(N_DEV):
            cut = min(max(KPE_END - p * SHARD_COLS, 0), SHARD_COLS)
            for a, b in ((0, cut), (cut, SHARD_COLS)):
                if a < b:
                    to = p * SHARD_COLS + a + (gap if p * SHARD_COLS + a >= KPE_END else 0)
                    o_ref[:, to:to + b - a] = s_ref[p, :, a:b]
        o_ref[:, KPE_END:KPE_END + gap] = jnp.zeros((tr, gap), o_ref.dtype)

    return pl.pallas_call(
        body, name="pad_columns", grid=(K // tr,),
        in_specs=[pl.BlockSpec((N_DEV, tr, SHARD_COLS), lambda i: (0, i, 0))],
        out_specs=pl.BlockSpec((tr, PP), lambda i: (i, 0)),
        out_shape=jax.ShapeDtypeStruct((K, PP), shards.dtype),
        compiler_params=_cp(),
    )(shards)


def _unpad_columns(wp):
    K = wp.shape[0]
    gap = LANE - MLA_ROPE
    tr = _tile(K, 128)

    def body(w_ref, o_ref):
        for p in range(N_DEV):
            cut = min(max(KPE_END - p * SHARD_COLS, 0), SHARD_COLS)
            for a, b in ((0, cut), (cut, SHARD_COLS)):
                if a < b:
                    at = p * SHARD_COLS + a + (gap if p * SHARD_COLS + a >= KPE_END else 0)
                    o_ref[p, :, a:b] = w_ref[:, at:at + b - a]

    return pl.pallas_call(
        body, name="unpad_columns", grid=(K // tr,),
        in_specs=[pl.BlockSpec((tr, PP), lambda i: (i, 0))],
        out_specs=pl.BlockSpec((N_DEV, tr, SHARD_COLS), lambda i: (0, i, 0)),
        out_shape=jax.ShapeDtypeStruct((N_DEV, K, SHARD_COLS), wp.dtype),
        compiler_params=_cp(),
    )(wp)


def _put_copies(stages, dst_ref, sems, slot, rows, cols):
    return [pltpu.make_async_copy(st.at[slot], dst_ref.at[rows, pl.ds(c0, st.shape[-1])], sems.at[slot, k])
            for k, (st, c0) in enumerate(zip(stages, cols))]


def _put_pipeline(step, nsteps, copies_of, fill):
    @pl.when(step >= 2)
    def _():
        for cp in copies_of(step - 2):
            cp.wait()

    fill(step % 2)
    for cp in copies_of(step):
        cp.start()

    @pl.when(step == nsteps - 1)
    def _():
        if nsteps >= 2:
            for cp in copies_of(step - 1):
                cp.wait()
        for cp in copies_of(step):
            cp.wait()


def _cp():
    return pltpu.CompilerParams(vmem_limit_bytes=VMEM_LIMIT)


def _rstd(x, n):
    return lax.rsqrt(jnp.sum(x * x, axis=-1, keepdims=True) * (1.0 / n) + EPS)


def _sigmoid(z):
    return 1.0 / (1.0 + jnp.exp(-z))


def _silu(z):
    return z * _sigmoid(z)


def _silu_and_grad(z):
    s = _sigmoid(z)
    return z * s, s * (1.0 + z * (1.0 - s))


def _mm(a, b):
    return jnp.dot(a.astype(BF16), b.astype(BF16), preferred_element_type=F32)


def _mm_nt(a, b):
    return lax.dot_general(a.astype(BF16), b.astype(BF16), (((1,), (1,)), ((), ())), preferred_element_type=F32)


def _mm_tn(a, b):
    return lax.dot_general(a.astype(BF16), b.astype(BF16), (((0,), (0,)), ((), ())), preferred_element_type=F32)


def _lane_lo(shape):
    return lax.broadcasted_iota(jnp.int32, shape, len(shape) - 1) < HALF


def _head_bcast_sum(x, terms=3):
    w = x.shape[-1]
    same = (lax.broadcasted_iota(jnp.int32, (w, w), 0) // HALF) == (lax.broadcasted_iota(jnp.int32, (w, w), 1) // HALF)
    ones = jnp.where(same, 1.0, 0.0).astype(jnp.bfloat16)
    total = None
    for _ in range(terms):
        term = x.astype(jnp.bfloat16)
        x = x - term.astype(F32)
        part = jnp.dot(term, ones, preferred_element_type=F32)
        total = part if total is None else total + part
    return total


def _rope(t, cos, sa, sb):
    return t * cos + pltpu.roll(t, LANE - 16, axis=1) * sa + pltpu.roll(t, 16, axis=1) * sb


def _rope_t(d, cos, sa, sb):
    return d * cos + pltpu.roll(d * sa, 16, axis=1) + pltpu.roll(d * sb, LANE - 16, axis=1)


def _shift_down(u, k):
    rows = lax.broadcasted_iota(jnp.int32, u.shape, 0)
    return jnp.where(rows >= k, pltpu.roll(u, k, axis=0), 0.0)


def _shift_up(u, k):
    n = u.shape[0]
    rows = lax.broadcasted_iota(jnp.int32, u.shape, 0)
    return jnp.where(rows < n - k, pltpu.roll(u, n - k, axis=0), 0.0)


def _tile(n, want):
    t = min(n, want)
    assert n % t == 0, (n, want)
    return t


def _inproj_fwd(x, g, wp):
    T = x.shape[0]
    tm, tn = _tile(T, 2048), 512

    def body(x_ref, g_ref, w_ref, proj_ref, ht_ref, h_ref):
        @pl.when(pl.program_id(1) == 0)
        def _():
            n = min(tm, 512)
            for r0 in range(0, tm, n):
                xv = x_ref[r0:r0 + n, :]
                h = xv * _rstd(xv, D_MODEL) * g_ref[...]
                h_ref[r0:r0 + n, :] = h.astype(BF16)
                ht_ref[:, r0:r0 + n] = h.T.astype(BF16)

        proj_ref[...] = jnp.dot(h_ref[...], w_ref[...], preferred_element_type=F32).astype(BF16)

    return pl.pallas_call(
        body, name="inproj_fwd", grid=(T // tm, PP // tn),
        in_specs=[pl.BlockSpec((tm, D_MODEL), lambda i, j: (i, 0)),
                  pl.BlockSpec((1, D_MODEL), lambda i, j: (0, 0)),
                  pl.BlockSpec((D_MODEL, tn), lambda i, j: (0, j))],
        out_specs=[pl.BlockSpec((tm, tn), lambda i, j: (i, j)),
                   pl.BlockSpec((D_MODEL, tm), lambda i, j: (0, i))],
        out_shape=[jax.ShapeDtypeStruct((T, PP), BF16), jax.ShapeDtypeStruct((D_MODEL, T), BF16)],
        scratch_shapes=[pltpu.VMEM((tm, D_MODEL), BF16)],
        compiler_params=_cp(),
    )(x, g, wp)


def _matmul_nn(at, b, name, after=None):
    K, T = at.shape
    N = b.shape[1]
    tt, tn = _tile(T, 1024), _tile(N, 2816)
    nk = T // tt
    unread = [] if after is None else [after]

    def body(a_ref, b_ref, *refs):
        o_ref, acc_ref = refs[len(unread):]
        k = pl.program_id(1)

        @pl.when(k == 0)
        def _():
            acc_ref[...] = jnp.zeros_like(acc_ref)

        acc_ref[...] += jnp.dot(a_ref[...], b_ref[...], preferred_element_type=F32)

        @pl.when(k == nk - 1)
        def _():
            o_ref[...] = acc_ref[...].astype(BF16)

    return pl.pallas_call(
        body, name=name, grid=(N // tn, nk),
        in_specs=[pl.BlockSpec((K, tt), lambda j, k: (0, k)),
                  pl.BlockSpec((tt, tn), lambda j, k: (k, j))] + [pl.BlockSpec(memory_space=pl.ANY)] * len(unread),
        out_specs=pl.BlockSpec((K, tn), lambda j, k: (0, j)),
        out_shape=jax.ShapeDtypeStruct((K, N), BF16),
        scratch_shapes=[pltpu.VMEM((K, tn), F32)],
        compiler_params=_cp(),
    )(at, b, *unread)


def _matmul_tn(a, b, name):
    T, K = a.shape
    N = b.shape[1]
    tt, tn = _tile(T, 512), _tile(N, 1024)

    def body(a_ref, b_ref, o_ref):
        @pl.when(pl.program_id(1) == 0)
        def _():
            o_ref[...] = jnp.zeros_like(o_ref)

        o_ref[...] += _mm_tn(a_ref[...], b_ref[...])

    return pl.pallas_call(
        body, name=name, grid=(N // tn, T // tt),
        in_specs=[pl.BlockSpec((tt, K), lambda j, k: (k, 0)),
                  pl.BlockSpec((tt, tn), lambda j, k: (k, j))],
        out_specs=pl.BlockSpec((K, tn), lambda j, k: (0, j)),
        out_shape=jax.ShapeDtypeStruct((K, N), F32),
        compiler_params=_cp(),
    )(a, b)


def _inproj_bwd_x(dproj, wp, x, g, dout):
    T = x.shape[0]
    tm, tk = _tile(T, 1024), 1024
    nk = PP // tk

    def body(dp_ref, w_ref, x_ref, g_ref, do_ref, dx_ref, dg_ref, acc_ref):
        i, k = pl.program_id(0), pl.program_id(1)

        @pl.when(k == 0)
        def _():
            acc_ref[...] = jnp.zeros_like(acc_ref)

        @pl.when((k == 0) & (i == 0))
        def _():
            dg_ref[...] = jnp.zeros_like(dg_ref)

        acc_ref[...] += _mm_nt(dp_ref[...], w_ref[...])

        @pl.when(k == nk - 1)
        def _():
            dh = acc_ref[...]
            xv = x_ref[...]
            r = _rstd(xv, D_MODEL)
            gy = dh * g_ref[...]
            dot = jnp.sum(xv * gy, axis=-1, keepdims=True) * (1.0 / D_MODEL)
            dx_ref[...] = do_ref[...] + r * gy - xv * (r * r * r) * dot
            dg_ref[...] += jnp.sum(dh * xv * r, axis=0, keepdims=True)

    return pl.pallas_call(
        body, name="inproj_bwd_x", grid=(T // tm, nk),
        in_specs=[pl.BlockSpec((tm, tk), lambda i, k: (i, k)),
                  pl.BlockSpec((D_MODEL, tk), lambda i, k: (0, k)),
                  pl.BlockSpec((tm, D_MODEL), lambda i, k: (i, 0)),
                  pl.BlockSpec((1, D_MODEL), lambda i, k: (0, 0)),
                  pl.BlockSpec((tm, D_MODEL), lambda i, k: (i, 0))],
        out_specs=[pl.BlockSpec((tm, D_MODEL), lambda i, k: (i, 0)),
                   pl.BlockSpec((1, D_MODEL), lambda i, k: (0, 0))],
        out_shape=[jax.ShapeDtypeStruct((T, D_MODEL), F32), jax.ShapeDtypeStruct((1, D_MODEL), F32)],
        scratch_shapes=[pltpu.VMEM((tm, D_MODEL), F32)],
        compiler_params=_cp(),
    )(dproj, wp, x, g, dout)


A_SEGS = (CB_AB, CB_AC, CB_AX, CB_AZ)


def _mixa_fwd(proj, cw, cb, B, S):
    nc = CONV_WIDTH // LANE

    def body(ab_ref, ac_ref, ax_ref, az_ref, cw_ref, cb_ref, y_ref):
        ab, ac, ax, az = (r[...].astype(F32) for r in (ab_ref, ac_ref, ax_ref, az_ref))
        u = ac * ax
        conv = cb_ref[...] + cw_ref[0:1, :] * _shift_down(u, 2) + cw_ref[1:2, :] * _shift_down(u, 1) + cw_ref[2:3, :] * u
        y_ref[...] = (ab * conv * _silu(az)).astype(BF16)

    return pl.pallas_call(
        body, name="mixa_fwd", grid=(B, nc),
        in_specs=[pl.BlockSpec((S, LANE), lambda b, j, c0=c0: (b, c0 + j)) for c0 in A_SEGS]
                 + [pl.BlockSpec((CONV_K, LANE), lambda b, j: (0, j)),
                    pl.BlockSpec((1, LANE), lambda b, j: (0, j))],
        out_specs=pl.BlockSpec((S, LANE), lambda b, j: (b, j)),
        out_shape=jax.ShapeDtypeStruct((B * S, CONV_WIDTH), BF16),
        compiler_params=_cp(),
    )(proj, proj, proj, proj, cw, cb)


def _mixa_bwd(dproj, dy, proj, cw, cb, B, S):
    nc = CONV_WIDTH // LANE

    def body(dpin_ref, dy_ref, ab_ref, ac_ref, ax_ref, az_ref, cw_ref, cb_ref, dp_ref, st_ref, stage, sems):
        del dpin_ref
        j, b = pl.program_id(0), pl.program_id(1)
        ab, ac, ax, az = (r[...].astype(F32) for r in (ab_ref, ac_ref, ax_ref, az_ref))
        u = ac * ax
        u1, u2 = _shift_down(u, 1), _shift_down(u, 2)
        w0, w1, w2 = cw_ref[0:1, :], cw_ref[1:2, :], cw_ref[2:3, :]
        conv = cb_ref[...] + w0 * u2 + w1 * u1 + w2 * u
        s, ds_az = _silu_and_grad(az)
        d = dy_ref[...]
        dconv = d * ab * s
        du = w2 * dconv + w1 * _shift_up(dconv, 1) + w0 * _shift_up(dconv, 2)
        grads = (d * conv * s, du * ax, du * ac, d * ab * conv * ds_az)

        def fill(slot):
            for k, v in enumerate(grads):
                stage[slot, k] = v.astype(BF16)

        def copies_of(step):
            sj, sb = step // B, step % B
            return _put_copies([stage.at[:, k] for k in range(4)], dp_ref, sems, step % 2,
                               pl.ds(pl.multiple_of(sb * S, S), S),
                               [pl.multiple_of((c0 + sj) * LANE, LANE) for c0 in A_SEGS])

        _put_pipeline(j * B + b, nc * B, copies_of, fill)
        row = lax.broadcasted_iota(jnp.int32, (8, LANE), 0)
        st = jnp.zeros((8, LANE), F32)
        for r, v in enumerate((dconv * u2, dconv * u1, dconv * u, dconv)):
            st = st + jnp.where(row == r, jnp.sum(v, axis=0, keepdims=True), 0.0)

        @pl.when(pl.program_id(1) == 0)
        def _():
            st_ref[...] = st

        @pl.when(pl.program_id(1) != 0)
        def _():
            st_ref[...] += st

    return pl.pallas_call(
        body, name="mixa_bwd", grid=(nc, B),
        in_specs=[pl.BlockSpec(memory_space=pl.ANY),
                  pl.BlockSpec((S, LANE), lambda j, b: (b, j))]
                 + [pl.BlockSpec((S, LANE), lambda j, b, c0=c0: (b, c0 + j)) for c0 in A_SEGS]
                 + [pl.BlockSpec((CONV_K, LANE), lambda j, b: (0, j)),
                    pl.BlockSpec((1, LANE), lambda j, b: (0, j))],
        out_specs=[pl.BlockSpec(memory_space=pl.ANY),
                   pl.BlockSpec((8, LANE), lambda j, b: (0, j))],
        out_shape=[jax.ShapeDtypeStruct(dproj.shape, BF16), jax.ShapeDtypeStruct((8, CONV_WIDTH), F32)],
        scratch_shapes=[pltpu.VMEM((2, 4, S, LANE), BF16), pltpu.SemaphoreType.DMA((2, 4))],
        input_output_aliases={0: 0},
        compiler_params=_cp(),
    )(dproj, dy, proj, proj, proj, proj, cw, cb)


def _mla_prep_fwd(proj, gq, gkv, wuqp, wkp, wv, gmq, gmk, cos, sa, sb, S):
    T = proj.shape[0]
    ts = _tile(S, 512)
    ns = S // ts
    W = MLA_HEADS * LANE

    def body(p_ref, gq_ref, gkv_ref, wuq_ref, wk_ref, wv_ref, gmq_ref, gmk_ref, cos_ref, sa_ref, sb_ref,
             q_ref, k_ref, v_ref):
        cq = p_ref[:, 0:2 * LANE].astype(F32)
        ckv = p_ref[:, 2 * LANE:3 * LANE].astype(F32)
        kpe = pltpu.roll(p_ref[:, 3 * LANE:4 * LANE].astype(F32), HALF, axis=1)
        cqn = cq * _rstd(cq, MLA_Q_LORA) * gq_ref[...]
        ckn = (ckv * _rstd(ckv, MLA_KV_LORA) * gkv_ref[...]).astype(BF16)
        q0 = _mm(cqn, wuq_ref[...])
        kn = _mm(ckn, wk_ref[...])
        v_ref[...] = _mm(ckn, wv_ref[...]).astype(BF16)
        c, a, b = cos_ref[...], sa_ref[...], sb_ref[...]
        kpe_rot = _rope(kpe * gmk_ref[...], c, a, b)
        for h in range(MLA_HEADS):
            q0h = q0[:, h * LANE:(h + 1) * LANE]
            q_ref[h] = _rope(q0h * _rstd(q0h, MLA_QK) * gmq_ref[...], c, a, b).astype(BF16)
            knh = kn[:, h * LANE:(h + 1) * LANE]
            k_ref[h] = (_rstd(knh + kpe, MLA_QK) * (knh * gmk_ref[...] + kpe_rot)).astype(BF16)

    def whole(r, c):
        return pl.BlockSpec((r, c), lambda i: (0, 0))

    tab = pl.BlockSpec((ts, LANE), lambda i: (i % ns, 0))
    return pl.pallas_call(
        body, name="mla_prep_fwd", grid=(T // ts,),
        in_specs=[pl.BlockSpec((ts, 4 * LANE), lambda i: (i, CB_CQ // 4)),
                  whole(1, MLA_Q_LORA), whole(1, MLA_KV_LORA), whole(MLA_Q_LORA, W), whole(MLA_KV_LORA, W),
                  whole(MLA_KV_LORA, MLA_HEADS * MLA_V), whole(1, LANE), whole(1, LANE), tab, tab, tab],
        out_specs=[pl.BlockSpec((MLA_HEADS, ts, LANE), lambda i: (0, i, 0)),
                   pl.BlockSpec((MLA_HEADS, ts, LANE), lambda i: (0, i, 0)),
                   pl.BlockSpec((ts, MLA_HEADS * MLA_V), lambda i: (i, 0))],
        out_shape=[jax.ShapeDtypeStruct((MLA_HEADS, T, LANE), BF16), jax.ShapeDtypeStruct((MLA_HEADS, T, LANE), BF16),
                   jax.ShapeDtypeStruct((T, MLA_HEADS * MLA_V), BF16)],
        compiler_params=_cp(),
    )(proj, gq, gkv, wuqp, wkp, wv, gmq, gmk, cos, sa, sb)


def _mla_prep_bwd(dproj, dq, dk, dv, proj, gq, gkv, wuqp, wkp, wv, gmq, gmk, cos, sa, sb, S):
    T = proj.shape[0]
    ts = _tile(S, 256)
    ns = S // ts
    W = MLA_HEADS * LANE

    def body(dpin_ref, dq_ref, dk_ref, dv_ref, p_ref, gq_ref, gkv_ref, wuq_ref, wk_ref, wv_ref, gmq_ref, gmk_ref,
             cos_ref, sa_ref, sb_ref,
             dp_ref, dwuq_ref, dwk_ref, dwv_ref, dgq_ref, dgkv_ref, dgmq_ref, dgmk_ref, dq0_ref, dkn_ref):
        del dpin_ref

        @pl.when(pl.program_id(0) == 0)
        def _():
            for r in (dwuq_ref, dwk_ref, dwv_ref, dgq_ref, dgkv_ref, dgmq_ref, dgmk_ref):
                r[...] = jnp.zeros_like(r)

        cq = p_ref[:, 0:2 * LANE].astype(F32)
        ckv = p_ref[:, 2 * LANE:3 * LANE].astype(F32)
        kpe = pltpu.roll(p_ref[:, 3 * LANE:4 * LANE].astype(F32), HALF, axis=1)
        rq = _rstd(cq, MLA_Q_LORA)
        rkv = _rstd(ckv, MLA_KV_LORA)
        gq, gkv, gmq, gmk = gq_ref[...], gkv_ref[...], gmq_ref[...], gmk_ref[...]
        cqn = (cq * rq * gq).astype(BF16)
        ckn = (ckv * rkv * gkv).astype(BF16)
        q0 = _mm(cqn, wuq_ref[...])
        kn = _mm(ckn, wk_ref[...])
        c, a, b = cos_ref[...], sa_ref[...], sb_ref[...]
        lane = lax.broadcasted_iota(jnp.int32, (ts, LANE), 1)
        dgmq = jnp.zeros((1, LANE), F32)
        dgmk = jnp.zeros((1, LANE), F32)
        nope = lane < MLA_NOPE
        kpe_rot = _rope(kpe * gmk, c, a, b)
        dk_sum = jnp.zeros((ts, LANE), F32)
        back = jnp.zeros((ts, 1), F32)
        for h in range(MLA_HEADS):
            q0h = q0[:, h * LANE:(h + 1) * LANE]
            r = _rstd(q0h, MLA_QK)
            d1 = _rope_t(dq_ref[h], c, a, b)
            gy = d1 * gmq
            dq0_ref[:, h * LANE:(h + 1) * LANE] = (
                r * gy - q0h * (r * r * r) * (jnp.sum(q0h * gy, axis=-1, keepdims=True) * (1.0 / MLA_QK))).astype(BF16)
            dgmq = dgmq + jnp.sum(d1 * q0h * r, axis=0, keepdims=True)
            knh = kn[:, h * LANE:(h + 1) * LANE]
            dkh = dk_ref[h]
            r = _rstd(knh + kpe, MLA_QK)
            r3dot = (r * r * r) * (jnp.sum((knh * gmk + kpe_rot) * dkh, axis=-1, keepdims=True) * (1.0 / MLA_QK))
            dkn_ref[:, h * LANE:(h + 1) * LANE] = jnp.where(nope, r * gmk * dkh - knh * r3dot, 0.0).astype(BF16)
            dgmk = dgmk + jnp.sum(jnp.where(nope, dkh * knh * r, 0.0), axis=0, keepdims=True)
            dk_sum = dk_sum + r * dkh
            back = back + r3dot
        rot = jnp.where(nope | (lane >= MLA_QK), 0.0, _rope_t(dk_sum, c, a, b))
        dkpe = gmk * rot - kpe * back
        dgmk = dgmk + jnp.sum(kpe * rot, axis=0, keepdims=True)
        dq0 = dq0_ref[...]
        dkn = dkn_ref[...]
        dvv = dv_ref[...]
        dwuq_ref[...] += _mm_tn(cqn, dq0)
        dwk_ref[...] += _mm_tn(ckn, dkn)
        dwv_ref[...] += _mm_tn(ckn, dvv)
        dgmq_ref[...] += dgmq
        dgmk_ref[...] += dgmk
        dcqn = _mm_nt(dq0, wuq_ref[...])
        gy = dcqn * gq
        dp_ref[:, 0:2 * LANE] = (
            rq * gy - cq * (rq * rq * rq) * (jnp.sum(cq * gy, axis=-1, keepdims=True) * (1.0 / MLA_Q_LORA))).astype(BF16)
        dgq_ref[...] += jnp.sum(dcqn * cq * rq, axis=0, keepdims=True)
        dckn = _mm_nt(dkn, wk_ref[...]) + _mm_nt(dvv, wv_ref[...])
        gy = dckn * gkv
        dp_ref[:, 2 * LANE:3 * LANE] = (
            rkv * gy - ckv * (rkv * rkv * rkv) * (jnp.sum(ckv * gy, axis=-1, keepdims=True) * (1.0 / MLA_KV_LORA))).astype(BF16)
        dgkv_ref[...] += jnp.sum(dckn * ckv * rkv, axis=0, keepdims=True)
        dp_ref[:, 3 * LANE:4 * LANE] = pltpu.roll(dkpe, HALF, axis=1).astype(BF16)

    def whole(r, c):
        return pl.BlockSpec((r, c), lambda i: (0, 0))

    tab = pl.BlockSpec((ts, LANE), lambda i: (i % ns, 0))
    heads = pl.BlockSpec((MLA_HEADS, ts, LANE), lambda i: (0, i, 0))
    return pl.pallas_call(
        body, name="mla_prep_bwd", grid=(T // ts,),
        in_specs=[pl.BlockSpec(memory_space=pl.ANY), heads, heads,
                  pl.BlockSpec((ts, MLA_HEADS * MLA_V), lambda i: (i, 0)),
                  pl.BlockSpec((ts, 4 * LANE), lambda i: (i, CB_CQ // 4)),
                  whole(1, MLA_Q_LORA), whole(1, MLA_KV_LORA), whole(MLA_Q_LORA, W), whole(MLA_KV_LORA, W),
                  whole(MLA_KV_LORA, MLA_HEADS * MLA_V), whole(1, LANE), whole(1, LANE), tab, tab, tab],
        out_specs=[pl.BlockSpec((ts, 4 * LANE), lambda i: (i, CB_CQ // 4)),
                   whole(MLA_Q_LORA, W), whole(MLA_KV_LORA, W), whole(MLA_KV_LORA, MLA_HEADS * MLA_V),
                   whole(1, MLA_Q_LORA), whole(1, MLA_KV_LORA), whole(1, LANE), whole(1, LANE)],
        out_shape=[jax.ShapeDtypeStruct(dproj.shape, BF16),
                   jax.ShapeDtypeStruct((MLA_Q_LORA, W), F32), jax.ShapeDtypeStruct((MLA_KV_LORA, W), F32),
                   jax.ShapeDtypeStruct((MLA_KV_LORA, MLA_HEADS * MLA_V), F32),
                   jax.ShapeDtypeStruct((1, MLA_Q_LORA), F32), jax.ShapeDtypeStruct((1, MLA_KV_LORA), F32),
                   jax.ShapeDtypeStruct((1, LANE), F32), jax.ShapeDtypeStruct((1, LANE), F32)],
        scratch_shapes=[pltpu.VMEM((ts, W), BF16), pltpu.VMEM((ts, W), BF16)],
        input_output_aliases={0: 0},
        compiler_params=_cp(),
    )(dproj, dq, dk, dv, proj, gq, gkv, wuqp, wkp, wv, gmq, gmk, cos, sa, sb)


def _dil_prep_fwd(proj, gq, gk):
    T = proj.shape[0]
    ts = _tile(T, 512)

    def body(pq_ref, pk_ref, gq_ref, gk_ref, q_ref, k_ref):
        for c in range(NPAIR):
            cs = slice(c * LANE, (c + 1) * LANE)
            t = jnp.concatenate([pq_ref[:, cs], pk_ref[:, cs]], axis=1).astype(F32)
            y = t * lax.rsqrt(_head_bcast_sum(t * t, terms=2) * (1.0 / DIL_HEAD_DIM) + EPS)
            q_ref[:, cs] = (y[:, 0:LANE] * gq_ref[:, cs]).astype(BF16)
            k_ref[:, cs] = (y[:, LANE:2 * LANE] * gk_ref[:, cs]).astype(BF16)

    col = pl.BlockSpec((1, DIL_WIDTH), lambda i, g: (0, g))
    out = pl.BlockSpec((ts, DIL_WIDTH), lambda i, g: (i, g))
    seg = lambda c0: pl.BlockSpec((ts, DIL_WIDTH), lambda i, g: (i, c0 // NPAIR + g))
    return pl.pallas_call(
        body, name="dil_prep_fwd", grid=(T // ts, DIL_GROUPS),
        in_specs=[seg(CB_DQ), seg(CB_DK), col, col],
        out_specs=[out, out],
        out_shape=[jax.ShapeDtypeStruct((T, DIL_QK), BF16)] * 2,
        compiler_params=_cp(),
    )(proj, proj, gq, gk)


def _dil_prep_bwd(dproj, ddq, ddk, ddv, proj, gq, gk):
    T = proj.shape[0]
    ts = _tile(T, 512)
    nt = T // ts

    def body(dpin_ref, ddq_ref, ddk_ref, ddv_ref, pq_ref, pk_ref, gq_ref, gk_ref, dp_ref, dgq_ref, dgk_ref,
             stage, sems):
        del dpin_ref
        g, i = pl.program_id(0), pl.program_id(1)

        @pl.when(i == 0)
        def _():
            dgq_ref[...] = jnp.zeros_like(dgq_ref)
            dgk_ref[...] = jnp.zeros_like(dgk_ref)

        def fill(slot):
            stage[slot, 2] = ddv_ref[...].astype(BF16)
            for c in range(NPAIR):
                cs = slice(c * LANE, (c + 1) * LANE)
                t = jnp.concatenate([pq_ref[:, cs], pk_ref[:, cs]], axis=1).astype(F32)
                d = jnp.concatenate([ddq_ref[:, cs], ddk_ref[:, cs]], axis=1)
                gy = d * jnp.concatenate([gq_ref[:, cs], gk_ref[:, cs]], axis=1)
                r = lax.rsqrt(_head_bcast_sum(t * t, terms=2) * (1.0 / DIL_HEAD_DIM) + EPS)
                dot = _head_bcast_sum(t * gy, terms=2) * (1.0 / DIL_HEAD_DIM)
                dx = (r * gy - t * (r * r * r) * dot).astype(BF16)
                stage[slot, 0, :, cs] = dx[:, 0:LANE]
                stage[slot, 1, :, cs] = dx[:, LANE:2 * LANE]
                part = jnp.sum(d * t * r, axis=0, keepdims=True)
                dgq_ref[:, cs] += part[:, 0:LANE]
                dgk_ref[:, cs] += part[:, LANE:2 * LANE]

        def copies_of(step):
            sg, si = step // nt, step % nt
            return _put_copies([stage.at[:, k] for k in range(3)], dp_ref, sems, step % 2,
                               pl.ds(pl.multiple_of(si * ts, ts), ts),
                               [pl.multiple_of((c0 + NPAIR * sg) * LANE, LANE) for c0 in (CB_DQ, CB_DK, CB_DV)])

        _put_pipeline(g * nt + i, DIL_GROUPS * nt, copies_of, fill)

    col = pl.BlockSpec((1, DIL_WIDTH), lambda g, i: (0, g))
    tok = pl.BlockSpec((ts, DIL_WIDTH), lambda g, i: (i, g))
    seg = lambda c0: pl.BlockSpec((ts, DIL_WIDTH), lambda g, i: (i, c0 // NPAIR + g))
    return pl.pallas_call(
        body, name="dil_prep_bwd", grid=(DIL_GROUPS, nt),
        in_specs=[pl.BlockSpec(memory_space=pl.ANY), tok, tok, tok, seg(CB_DQ), seg(CB_DK), col, col],
        out_specs=[pl.BlockSpec(memory_space=pl.ANY), col, col],
        out_shape=[jax.ShapeDtypeStruct(dproj.shape, BF16), jax.ShapeDtypeStruct((1, DIL_QK), F32),
                   jax.ShapeDtypeStruct((1, DIL_QK), F32)],
        scratch_shapes=[pltpu.VMEM((2, 3, ts, DIL_WIDTH), BF16), pltpu.SemaphoreType.DMA((2, 3))],
        input_output_aliases={0: 0},
        compiler_params=_cp(),
    )(dproj, ddq, ddk, ddv, proj, proj, gq, gk)


COPY_ROWS = 256


def _to_classes(src_ref, dst_ref, d, L, scale=None):
    m = min(L, max(8, COPY_ROWS // d))
    for c0 in range(0, L, m):
        x = src_ref[c0 * d:(c0 + m) * d, :].astype(F32)
        if scale is not None:
            x = x * scale
        if d > 1:
            x = jnp.swapaxes(x.reshape(m, d, LANE), 0, 1)
        for r in range(d):
            dst_ref[r * L + c0:r * L + c0 + m, :] = (x[r] if d > 1 else x).astype(dst_ref.dtype)


def _from_classes(src_ref, dst_ref, d, L):
    n = min(L, COPY_ROWS)
    for r in range(d):
        for c0 in range(0, L, n):
            rows = pl.ds(r + c0 * d, n, stride=d) if d > 1 else pl.ds(c0, n)
            dst_ref[rows, :] = src_ref[r * L + c0:r * L + c0 + n, :].astype(dst_ref.dtype)


MLA_TQ, MLA_TK = 512, 512


def _causal_bias(tq, tk, shift):
    row = lax.broadcasted_iota(jnp.int32, (tq, tk), 0)
    col = lax.broadcasted_iota(jnp.int32, (tq, tk), 1)
    return jnp.where(row >= col + shift, 0.0, NEG)


def _mla_specs(S):
    heads = pl.BlockSpec((2, S, LANE), lambda b, j: (j, b, 0))
    pair = pl.BlockSpec((S, LANE), lambda b, j: (b, j))
    return heads, pair


def _mla_attn_fwd(q, k, v, B, S):
    tq = _tile(S, MLA_TQ)
    tk = _tile(tq, MLA_TK)
    nd = tq // tk
    scale = MLA_QK ** -0.5
    heads, pair = _mla_specs(S)

    def body(q_ref, k_ref, v_ref, o_ref, lse_ref):
        lo, lok = _lane_lo((tq, LANE)), _lane_lo((tk, LANE))
        diag = [_causal_bias(tq, tk, i * tk) for i in range(nd)]

        def block(g, _):
            row0 = pl.multiple_of(g * tq, tq)
            rows = pl.ds(row0, tq)
            qs = [q_ref[hh, rows, :] for hh in range(2)]

            one = jnp.ones((), BF16)

            def step(off, carries, bias):
                off = pl.multiple_of(off, tk)
                vt = v_ref[pl.ds(off, tk), :]
                vh = (jnp.where(lok, vt, one), jnp.where(lok, one, vt))
                out = []
                for hh, (m, acc) in enumerate(carries):
                    s = _mm_nt(qs[hh], k_ref[hh, pl.ds(off, tk), :]) * scale
                    if bias is not None:
                        s = s + bias
                    m_new = jnp.maximum(m, jnp.max(s, axis=-1, keepdims=True))
                    p = jnp.exp(s - m_new)
                    out.append((m_new, jnp.exp(m - m_new) * acc + _mm(p, vh[hh])))
                return tuple(out)

            init = (jnp.full((tq, 1), NEG, F32), jnp.zeros((tq, LANE), F32))
            carries = lax.fori_loop(0, g * nd, lambda i, c: step(i * tk, c, None), (init, init))
            for i in range(nd):
                carries = step(row0 + i * tk, carries, diag[i])
            (ma, acca), (mb, accb) = carries
            la, lb = pltpu.roll(acca, HALF, axis=1), pltpu.roll(accb, HALF, axis=1)
            o_ref[rows, :] = jnp.where(lo, acca / la, accb / lb)
            lse_ref[rows, :] = jnp.where(lo, ma + jnp.log(la), mb + jnp.log(lb))
            return 0

        lax.fori_loop(0, S // tq, block, 0)

    return pl.pallas_call(
        body, name="mla_attn_fwd", grid=(B, NPAIR), in_specs=[heads, heads, pair], out_specs=[pair, pair],
        out_shape=[jax.ShapeDtypeStruct((B * S, MLA_HEADS * MLA_V), F32)] * 2,
        compiler_params=_cp(),
    )(q, k, v)


DIL_UNROLL = 16


def _dil_geometry(gi, S):
    span, d = DIL_PATTERNS[gi]
    L = S // d
    t = _tile(L, 128)
    window = span // d
    back = min(-(-window // t) * t, L - t)
    return d, L, t, window, back


def _dil_specs(gi, S):
    qk = pl.BlockSpec((S, LANE), lambda b, j: (b, NPAIR * gi + j))
    v = pl.BlockSpec((S, LANE), lambda b, j: (b, CB_DV + NPAIR * gi + j))
    pair = pl.BlockSpec((S, LANE), lambda b, j: (b, j))
    return qk, v, pair


def _dil_bias(bias_ref, sl_ref, j, t, kw, back, window):
    row = lax.broadcasted_iota(jnp.int32, (2 * t, kw), 0)
    col = lax.broadcasted_iota(jnp.int32, (2 * t, kw), 1)
    second = row >= t
    slope = jnp.where(second, sl_ref[j, 1], sl_ref[j, 0])
    for n in range(bias_ref.shape[0]):
        dist = jnp.where(second, row - t, row) + n * back - col
        bias_ref[n] = jnp.where((dist >= 0) & (dist <= window), -slope * dist.astype(F32), NEG)


def _stack_heads(x, lo):
    zero = jnp.zeros((), x.dtype)
    return jnp.concatenate([jnp.where(lo, x, zero), jnp.where(lo, zero, x)], axis=0)


def _dil_attn_fwd(gi, slopes, qn, kn, proj, B, S):
    d, L, t, window, back = _dil_geometry(gi, S)
    kw, nq = back + t, L // t
    nbias = 2 if back else 1
    qk, vspec, pair = _dil_specs(gi, S)

    def body(sl_ref, q_ref, k_ref, v_ref, o_ref, lse_ref, qs, ks, vs, os_, ls, bias_ref):
        _to_classes(q_ref, qs, d, L, DIL_HEAD_DIM ** -0.5)
        _to_classes(k_ref, ks, d, L)
        _to_classes(v_ref, vs, d, L)
        _dil_bias(bias_ref, sl_ref, pl.program_id(1), t, kw, back, window)
        lo = _lane_lo((t, LANE))

        def block(g, _):
            qb = g % nq if d > 1 else g
            row0 = pl.multiple_of(g * t, t)
            rows = pl.ds(row0, t)
            early = qb * t < back
            keys = pl.ds(pl.multiple_of(jnp.where(early, row0 - qb * t, row0 - back), t), kw)
            s = _mm_nt(_stack_heads(qs[rows, :], lo), ks[keys, :]) + bias_ref[jnp.where(early, 0, nbias - 1)]
            m = jnp.max(s, axis=-1, keepdims=True)
            p = jnp.exp(s - m)
            l = jnp.sum(p, axis=-1, keepdims=True)
            o2 = _mm(p, vs[keys, :]) / l
            lse2 = m + jnp.log(l)
            os_[rows, :] = jnp.where(lo, o2[:t], o2[t:])
            ls[rows, :] = jnp.where(lo, lse2[:t], lse2[t:])
            return 0

        lax.fori_loop(0, d * nq, block, 0, unroll=DIL_UNROLL if d * nq % DIL_UNROLL == 0 else 1)
        _from_classes(os_, o_ref, d, L)
        _from_classes(ls, lse_ref, d, L)

    return pl.pallas_call(
        body, name=f"dil_attn_fwd_{gi}", grid=(B, NPAIR),
        in_specs=[pl.BlockSpec(memory_space=pltpu.SMEM), qk, qk, vspec], out_specs=[pair, pair],
        out_shape=[jax.ShapeDtypeStruct((B * S, DIL_WIDTH), F32)] * 2,
        scratch_shapes=[pltpu.VMEM((S, LANE), BF16)] * 3 + [pltpu.VMEM((S, LANE), F32)] * 2
                       + [pltpu.VMEM((nbias, 2 * t, kw), F32)],
        compiler_params=_cp(),
    )(slopes, qn, kn, proj)


def _mla_attn_bwd(q, k, v, do, lse, delta, B, S):
    T = B * S
    tq = _tile(S, MLA_TQ)
    tk = _tile(tq, MLA_TK)
    nd = tq // tk
    scale = MLA_QK ** -0.5
    heads, pair = _mla_specs(S)

    def body(q_ref, k_ref, v_ref, do_ref, lse_ref, dl_ref, dq_ref, dk_ref, dv_ref):
        dk_ref[...] = jnp.zeros_like(dk_ref)
        dv_ref[...] = jnp.zeros_like(dv_ref)
        lo = _lane_lo((tq, LANE))
        diag = [_causal_bias(tq, tk, i * tk) for i in range(nd)]

        def block(g, _):
            row0 = pl.multiple_of(g * tq, tq)
            rows = pl.ds(row0, tq)
            per_head = []
            for hh in range(2):
                sel = lo if hh == 0 else jnp.logical_not(lo)
                per_head.append((q_ref[hh, rows, :], jnp.where(sel, do_ref[rows, :], jnp.zeros((), BF16)),
                                 jnp.max(jnp.where(sel, lse_ref[rows, :], NEG), axis=-1, keepdims=True),
                                 jnp.max(jnp.where(sel, dl_ref[rows, :], NEG), axis=-1, keepdims=True)))

            def step(off, dq_accs, bias):
                cols = pl.ds(pl.multiple_of(off, tk), tk)
                vt = v_ref[cols, :]
                out, dv = [], None
                for hh, (qh, doh, lse_h, dl_h) in enumerate(per_head):
                    kh = k_ref[hh, cols, :]
                    s = _mm_nt(qh, kh) * scale
                    if bias is not None:
                        s = s + bias
                    p = jnp.exp(s - lse_h)
                    ds = (p * (_mm_nt(doh, vt) - dl_h)).astype(BF16)
                    dk_ref[hh, cols, :] += _mm_tn(ds, qh) * scale
                    part = _mm_tn(p, doh)
                    dv = part if dv is None else dv + part
                    out.append(dq_accs[hh] + _mm(ds, kh))
                dv_ref[cols, :] += dv
                return tuple(out)

            zero = jnp.zeros((tq, LANE), F32)
            dq_accs = lax.fori_loop(0, g * nd, lambda i, a: step(i * tk, a, None), (zero, zero))
            for i in range(nd):
                dq_accs = step(row0 + i * tk, dq_accs, diag[i])
            for hh in range(2):
                dq_ref[hh, rows, :] = dq_accs[hh] * scale
            return 0

        lax.fori_loop(0, S // tq, block, 0)

    return pl.pallas_call(
        body, name="mla_attn_bwd", grid=(B, NPAIR), in_specs=[heads, heads, pair, pair, pair, pair],
        out_specs=[heads, heads, pair],
        out_shape=[jax.ShapeDtypeStruct((MLA_HEADS, T, LANE), F32), jax.ShapeDtypeStruct((MLA_HEADS, T, LANE), F32),
                   jax.ShapeDtypeStruct((T, MLA_HEADS * MLA_V), F32)],
        compiler_params=_cp(),
    )(q, k, v, do, lse, delta)


def _dil_attn_bwd(gi, slopes, qn, kn, proj, do, lse, delta, through, B, S):
    d, L, t, window, back = _dil_geometry(gi, S)
    kw, nq = back + t, L // t
    nbias = 2 if back else 1
    scale = DIL_HEAD_DIM ** -0.5
    qk, vspec, pair = _dil_specs(gi, S)

    def body(*refs):
        refs = list(refs)
        sl_ref, q_ref, k_ref, v_ref, do_ref, lse_ref, dl_ref = refs[:7]
        dq_ref, dk_ref, dv_ref, qs, ks, vs, dos, lss, dls, dqs, dks, dvs, bias_ref = refs[-13:]
        _to_classes(q_ref, qs, d, L, scale)
        for src, dst in ((k_ref, ks), (v_ref, vs), (do_ref, dos), (lse_ref, lss), (dl_ref, dls)):
            _to_classes(src, dst, d, L)
        _dil_bias(bias_ref, sl_ref, pl.program_id(1), t, kw, back, window)
        dks[...] = jnp.zeros_like(dks)
        dvs[...] = jnp.zeros_like(dvs)
        lo = _lane_lo((t, LANE))

        def stats(ref, rows):
            x = ref[rows, :]
            return jnp.concatenate([jnp.max(jnp.where(lo, x, NEG), axis=-1, keepdims=True),
                                    jnp.max(jnp.where(lo, NEG, x), axis=-1, keepdims=True)], axis=0)

        def block(g, _):
            qb = g % nq if d > 1 else g
            row0 = pl.multiple_of(g * t, t)
            rows = pl.ds(row0, t)
            early = qb * t < back
            keys = pl.ds(pl.multiple_of(jnp.where(early, row0 - qb * t, row0 - back), t), kw)
            q2 = _stack_heads(qs[rows, :], lo)
            do2 = _stack_heads(dos[rows, :], lo)
            kt = ks[keys, :]
            s = _mm_nt(q2, kt) + bias_ref[jnp.where(early, 0, nbias - 1)]
            p = jnp.exp(s - stats(lss, rows))
            ds = (p * (_mm_nt(do2, vs[keys, :]) - stats(dls, rows))).astype(BF16)
            dq2 = _mm(ds, kt) * scale
            dqs[rows, :] = jnp.where(lo, dq2[:t], dq2[t:])
            dks[keys, :] += _mm_tn(ds, q2)
            dvs[keys, :] += _mm_tn(p, do2)
            return 0

        lax.fori_loop(0, d * nq, block, 0, unroll=DIL_UNROLL if d * nq % DIL_UNROLL == 0 else 1)
        for src, dst in ((dqs, dq_ref), (dks, dk_ref), (dvs, dv_ref)):
            _from_classes(src, dst, d, L)

    in_specs = [pl.BlockSpec(memory_space=pltpu.SMEM), qk, qk, vspec, pair, pair, pair]
    args = [slopes, qn, kn, proj, do, lse, delta]
    aliases = {}
    if through is not None:
        aliases = {len(args) + i: i for i in range(3)}
        in_specs = in_specs + [pl.BlockSpec(memory_space=pl.ANY)] * 3
        args = args + list(through)
    return pl.pallas_call(
        body, name=f"dil_attn_bwd_{gi}", grid=(B, NPAIR), in_specs=in_specs, out_specs=[qk, qk, qk],
        out_shape=[jax.ShapeDtypeStruct((B * S, DIL_QK), F32)] * 3,
        scratch_shapes=[pltpu.VMEM((S, LANE), BF16)] * 4 + [pltpu.VMEM((S, LANE), F32)] * 5
                       + [pltpu.VMEM((nbias, 2 * t, kw), F32)],
        input_output_aliases=aliases,
        compiler_params=_cp(),
    )(*args)


def _merge_proj_specs(ts):
    wide = lambda c0, w: pl.BlockSpec((ts, w), lambda i: (i, c0 * LANE // w))
    return [wide(CB_BZ, DIL_WIDTH), wide(CB_CZ, DIL_WIDTH)] + [wide(CB_GATE + 8 * i, D_MODEL) for i in range(3)]


def _merge_common(p_refs, bg_ref, ob_ref, og_refs, lse_refs):
    bz = p_refs[0][...].astype(F32)
    cz = p_refs[1][...].astype(F32)
    gates = [_sigmoid(p_refs[2 + i][...].astype(F32) + bg_ref[:, i * D_MODEL:(i + 1) * D_MODEL]) for i in range(3)]
    ob = ob_ref[...]
    lses = [r[...] for r in lse_refs]
    mx = jnp.maximum(jnp.maximum(lses[0], lses[1]), lses[2])
    es = [jnp.exp(v - mx) for v in lses]
    inv = 1.0 / (es[0] + es[1] + es[2])
    alphas = [e * inv for e in es]
    oc = alphas[0] * og_refs[0][...] + alphas[1] * og_refs[1][...] + alphas[2] * og_refs[2][...]
    return bz, cz, gates, ob, alphas, oc


def _merge_fwd(x, proj, b_gate, ya, ob, ogs, lses, woa, wob, woc, wo):
    T = x.shape[0]
    ts = _tile(T, 256)

    def body(x_ref, p0, p1, p2, p3, p4, bg_ref, ya_ref, ob_ref, og0, og1, og2, l0, l1, l2,
             woa_ref, wob_ref, woc_ref, wo_ref, out_ref):
        bz, cz, gates, obv, alphas, oc = _merge_common((p0, p1, p2, p3, p4), bg_ref, ob_ref, (og0, og1, og2),
                                                       (l0, l1, l2))
        yb = obv * _silu(bz)
        yc = oc * _silu(cz)
        merged = (gates[0] * _mm(ya_ref[...], woa_ref[...]) + gates[1] * _mm(yb, wob_ref[...])
                  + gates[2] * _mm(yc, woc_ref[...]))
        out_ref[...] = x_ref[...] + _mm(merged, wo_ref[...])

    def whole(r, c):
        return pl.BlockSpec((r, c), lambda i: (0, 0))

    tok = lambda w: pl.BlockSpec((ts, w), lambda i: (i, 0))
    return pl.pallas_call(
        body, name="merge_fwd", grid=(T // ts,),
        in_specs=[tok(D_MODEL)] + _merge_proj_specs(ts) + [whole(1, 3 * D_MODEL), tok(CONV_WIDTH)]
                 + [tok(DIL_WIDTH)] * 7 + [whole(CONV_WIDTH, D_MODEL)] * 3 + [whole(D_MODEL, D_MODEL)],
        out_specs=tok(D_MODEL),
        out_shape=jax.ShapeDtypeStruct((T, D_MODEL), F32),
        compiler_params=_cp(),
    )(x, *[proj] * 5, b_gate, ya, ob, *ogs, *lses, woa, wob, woc, wo)


def _merge_bwd(dout, proj, b_gate, ya, ob, ogs, lses, woa, wob, woc, wo):
    T = dout.shape[0]
    ts = _tile(T, 256)
    nt = T // ts

    def body(do_ref, p0, p1, p2, p3, p4, bg_ref, ya_ref, ob_ref, og0, og1, og2, l0, l1, l2,
             woa_ref, wob_ref, woc_ref, wo_ref,
             dp_ref, dya_ref, dob_ref, dlb_ref, dg0, dg1, dg2, dl0, dl1, dl2,
             mg_ref, dpa_ref, dpb_ref, dpc_ref, yb_ref, yc_ref, dbg_ref, st_bz, st_cz, st_gate, sems):
        step = pl.program_id(0)
        slot = step % 2

        def copies_of(s):
            return _put_copies([st_bz, st_cz, st_gate], dp_ref, sems, s % 2, pl.ds(pl.multiple_of(s * ts, ts), ts),
                               [CB_BZ * LANE, CB_CZ * LANE, CB_GATE * LANE])

        @pl.when(step >= 2)
        def _():
            for cp in copies_of(step - 2):
                cp.wait()

        bz, cz, gates, obv, alphas, oc = _merge_common((p0, p1, p2, p3, p4), bg_ref, ob_ref, (og0, og1, og2),
                                                       (l0, l1, l2))
        (sb, dsb), (sc, dsc) = _silu_and_grad(bz), _silu_and_grad(cz)
        yb = obv * sb
        yc = oc * sc
        ps = [_mm(ya_ref[...], woa_ref[...]), _mm(yb, wob_ref[...]), _mm(yc, woc_ref[...])]
        mg_ref[...] = (gates[0] * ps[0] + gates[1] * ps[1] + gates[2] * ps[2]).astype(BF16)
        yb_ref[...] = yb.astype(BF16)
        yc_ref[...] = yc.astype(BF16)
        dm = _mm_nt(do_ref[...], wo_ref[...])
        dps = []
        first = pl.program_id(0) == 0
        for i, dref in enumerate((dpa_ref, dpb_ref, dpc_ref)):
            g = gates[i]
            dpi = (dm * g).astype(BF16)
            dref[...] = dpi
            dps.append(dpi)
            dgp = dm * ps[i] * g * (1.0 - g)
            st_gate[slot, :, i * D_MODEL:(i + 1) * D_MODEL] = dgp.astype(BF16)
            part = jnp.sum(dgp, axis=0, keepdims=True)

            @pl.when(first)
            def _():
                dbg_ref[:, i * D_MODEL:(i + 1) * D_MODEL] = part

            @pl.when(jnp.logical_not(first))
            def _():
                dbg_ref[:, i * D_MODEL:(i + 1) * D_MODEL] += part

        dya_ref[...] = _mm_nt(dps[0], woa_ref[...])
        dyb = _mm_nt(dps[1], wob_ref[...])
        dyc = _mm_nt(dps[2], woc_ref[...])
        st_bz[slot] = (dyb * obv * dsb).astype(BF16)
        st_cz[slot] = (dyc * oc * dsc).astype(BF16)
        for cp in copies_of(step):
            cp.start()
        dob = dyb * sb
        doc = dyc * sc
        dob_ref[...] = dob.astype(BF16)
        for c in range(NPAIR):
            cs = slice(c * LANE, (c + 1) * LANE)
            dlb_ref[:, cs] = _head_bcast_sum(dob[:, cs] * obv[:, cs])
            dd = _head_bcast_sum(doc[:, cs] * oc[:, cs])
            for a, dref, lref in zip(alphas, (dg0, dg1, dg2), (dl0, dl1, dl2)):
                dref[:, cs] = (a[:, cs] * doc[:, cs]).astype(BF16)
                lref[:, cs] = a[:, cs] * dd

        @pl.when(step == nt - 1)
        def _():
            if nt >= 2:
                for cp in copies_of(step - 1):
                    cp.wait()
            for cp in copies_of(step):
                cp.wait()

    def whole(r, c):
        return pl.BlockSpec((r, c), lambda i: (0, 0))

    tok = lambda w: pl.BlockSpec((ts, w), lambda i: (i, 0))
    sd = jax.ShapeDtypeStruct
    W = DIL_WIDTH
    return pl.pallas_call(
        body, name="merge_bwd", grid=(nt,),
        in_specs=[tok(D_MODEL)] + _merge_proj_specs(ts) + [whole(1, 3 * D_MODEL), tok(CONV_WIDTH)] + [tok(W)] * 7
                 + [whole(CONV_WIDTH, D_MODEL)] * 3 + [whole(D_MODEL, D_MODEL)],
        out_specs=[pl.BlockSpec(memory_space=pl.ANY), tok(CONV_WIDTH), tok(W), tok(W)] + [tok(W)] * 6
                  + [tok(D_MODEL)] * 4 + [tok(W), tok(W), whole(1, 3 * D_MODEL)],
        out_shape=[sd((T, PP), BF16), sd((T, CONV_WIDTH), F32), sd((T, W), BF16), sd((T, W), F32)]
                  + [sd((T, W), BF16)] * 3 + [sd((T, W), F32)] * 3
                  + [sd((T, D_MODEL), BF16)] * 4 + [sd((T, W), BF16)] * 2 + [sd((1, 3 * D_MODEL), F32)],
        scratch_shapes=[pltpu.VMEM((2, ts, W), BF16), pltpu.VMEM((2, ts, W), BF16),
                        pltpu.VMEM((2, ts, 3 * D_MODEL), BF16), pltpu.SemaphoreType.DMA((2, 3))],
        compiler_params=_cp(),
    )(dout, *[proj] * 5, b_gate, ya, ob, *ogs, *lses, woa, wob, woc, wo)


def _loss_head(y, target):
    T = y.shape[0]
    ts = _tile(T, 512)

    def body(y_ref, t_ref, d_ref, l_ref):
        e = y_ref[...] - t_ref[...]
        d_ref[...] = e * (1.0 / D_MODEL)
        l_ref[...] = jnp.zeros((1, 8, LANE), F32) + jnp.sum(e * e)

    tok = pl.BlockSpec((ts, D_MODEL), lambda i: (i, 0))
    return pl.pallas_call(
        body, name="loss_head", grid=(T // ts,), in_specs=[tok, tok],
        out_specs=[tok, pl.BlockSpec((1, 8, LANE), lambda i: (i, 0, 0))],
        out_shape=[jax.ShapeDtypeStruct((T, D_MODEL), F32), jax.ShapeDtypeStruct((T // ts, 8, LANE), F32)],
        compiler_params=_cp(),
    )(y, target)


def _my_index():
    return 4 * lax.axis_index("x") + 2 * lax.axis_index("y") + lax.axis_index("c")


def _peers():
    x, y, c = (lax.axis_index(a) for a in AXES)
    out = []
    for kk in range(1, N_DEV):
        px = 1 - x if kk & 4 else x
        py = 1 - y if kk & 2 else y
        pc = 1 - c if kk & 1 else c
        out.append(((px, py, pc), 4 * px + 2 * py + pc))
    return out


def _exchange(arrays, name, gather):
    n = len(arrays)

    def body(*refs):
        srcs, outs = refs[:n], refs[n:2 * n]
        send_sems, recv_sems, local_sems = refs[2 * n:]
        me = _my_index()
        peers = _peers()
        started = []
        for a, (src, out) in enumerate(zip(srcs, outs)):
            mine = pltpu.make_async_copy(src if gather else src.at[me], out.at[me], local_sems.at[a])
            mine.start()
            started.append(mine)
        sends = []
        for i, (pos, idx) in enumerate(peers):
            for a, (src, out) in enumerate(zip(srcs, outs)):
                cp = pltpu.make_async_remote_copy(
                    src_ref=src if gather else src.at[idx], dst_ref=out.at[me], send_sem=send_sems.at[a, i],
                    recv_sem=recv_sems.at[a, i], device_id=pos, device_id_type=pl.DeviceIdType.MESH)
                cp.start()
                sends.append(cp)
        for i, (pos, idx) in enumerate(peers):
            for a, (src, out) in enumerate(zip(srcs, outs)):
                pltpu.make_async_remote_copy(
                    src_ref=src if gather else src.at[idx], dst_ref=out.at[idx], send_sem=send_sems.at[a, i],
                    recv_sem=recv_sems.at[a, i], device_id=pos, device_id_type=pl.DeviceIdType.MESH).wait_recv()
        for cp in sends:
            cp.wait_send()
        for mine in started:
            mine.wait()

    any_space = pl.BlockSpec(memory_space=pl.ANY)
    return pl.pallas_call(
        body, name=name, in_specs=[any_space] * n, out_specs=[any_space] * n,
        out_shape=[jax.ShapeDtypeStruct(((N_DEV,) + a.shape) if gather else a.shape, a.dtype) for a in arrays],
        scratch_shapes=[pltpu.SemaphoreType.DMA((n, N_DEV - 1)), pltpu.SemaphoreType.DMA((n, N_DEV - 1)),
                        pltpu.SemaphoreType.DMA((n,))],
    )(*arrays)


N_CHIP = 4


def _chip_places():
    x, y, c = (lax.axis_index(a) for a in AXES)
    return (x, y, c), (x, y, 1 - c), [(1 - x, y, c), (x, 1 - y, c), (1 - x, 1 - y, c)]


def _index_of(pos):
    return 4 * pos[0] + 2 * pos[1] + pos[2]


def _sibling_swap(arrays, name):
    n = len(arrays)

    def body(*refs):
        srcs, outs = refs[:n], refs[n:2 * n]
        send_sems, recv_sems = refs[2 * n:]
        (x, y, c), sibling, _ = _chip_places()
        sends = []
        for a, (src, out) in enumerate(zip(srcs, outs)):
            for q in range(N_CHIP):
                def copy(core, a=a, q=q, src=src, out=out):
                    return pltpu.make_async_remote_copy(
                        src_ref=src.at[2 * q + core], dst_ref=out.at[q], send_sem=send_sems.at[N_CHIP * a + q],
                        recv_sem=recv_sems.at[N_CHIP * a + q], device_id=sibling, device_id_type=pl.DeviceIdType.MESH)
                mine = copy(1 - c)
                mine.start()
                sends.append((mine, copy(c)))
        for mine, arrival in sends:
            arrival.wait_recv()
            mine.wait_send()

    any_space = pl.BlockSpec(memory_space=pl.ANY)
    return pl.pallas_call(
        body, name=name, in_specs=[any_space] * n, out_specs=[any_space] * n,
        out_shape=[jax.ShapeDtypeStruct((N_CHIP,) + a.shape[1:], a.dtype) for a in arrays],
        scratch_shapes=[pltpu.SemaphoreType.DMA((N_CHIP * n,)), pltpu.SemaphoreType.DMA((N_CHIP * n,))],
    )(*arrays)


def _chip_pair_sum(part, got, name):
    R, C = part.shape[1:]
    tr = R
    while tr * C * part.dtype.itemsize > REDUCE_BLOCK_BYTES // 4 and tr % 32 == 0:
        tr //= 2
    c = lax.axis_index("c")

    def body(c_ref, p_ref, g_ref, o_ref):
        del c_ref
        o_ref[...] = (p_ref[...].astype(F32) + g_ref[...].astype(F32)).astype(o_ref.dtype)

    return pl.pallas_call(
        body, name=name, grid_spec=pltpu.PrefetchScalarGridSpec(
            num_scalar_prefetch=1, grid=(N_CHIP, R // tr),
            in_specs=[pl.BlockSpec((None, tr, C), lambda q, i, cr: (2 * q + cr[0], i, 0)),
                      pl.BlockSpec((None, tr, C), lambda q, i, cr: (q, i, 0))],
            out_specs=pl.BlockSpec((None, tr, C), lambda q, i, cr: (q, i, 0))),
        out_shape=jax.ShapeDtypeStruct((N_CHIP, R, C), part.dtype),
        compiler_params=_cp(),
    )(jnp.reshape(c, (1,)).astype(jnp.int32), part, got)


def _peer_count(mode):
    return {"chips": N_CHIP - 1, "near": N_CHIP}.get(mode, N_DEV - 1)


def _remote_copies(srcs, lands, send_sems, recv_sems, mode):
    if mode == "chips":
        (x, y, _), _, others = _chip_places()
        my_slot, peers = 2 * x + y, [(chip, 2 * chip[0] + chip[1]) for chip in others]
    elif mode == "near":
        me, sibling, others = _chip_places()
        my_slot, peers = _index_of(me), [(pos, _index_of(pos)) for pos in [sibling] + others]
    else:
        my_slot, peers = _my_index(), _peers()
    whole = mode in ("gather", "near")
    out = []
    for i, (pos, idx) in enumerate(peers):
        for a, (src, land) in enumerate(zip(srcs, lands)):
            def copy(slot, a=a, src=src, land=land, i=i, pos=pos, idx=idx):
                return pltpu.make_async_remote_copy(
                    src_ref=src if whole else src.at[idx], dst_ref=land.at[slot],
                    send_sem=send_sems.at[a * len(peers) + i], recv_sem=recv_sems.at[a * len(peers) + i],
                    device_id=pos, device_id_type=pl.DeviceIdType.MESH)
            out.append((copy(my_slot), copy(idx)))
    return out


def _exchange_start(arrays, name, mode):
    n = len(arrays)
    hbm = pl.BlockSpec(memory_space=pltpu.HBM)
    sem = pl.BlockSpec(memory_space=pltpu.SEMAPHORE)
    lands = [lax.empty(((N_DEV,) + a.shape) if mode in ("gather", "near") else a.shape, a.dtype) for a in arrays]

    def body(*refs):
        srcs, lands_ = refs[:n], refs[n:2 * n]
        send_sems, recv_sems = refs[2 * n:2 * n + 2]
        for mine, _ in _remote_copies(srcs, lands_, send_sems, recv_sems, mode):
            mine.start()
        refs[-1][...] = jnp.zeros_like(refs[-1])

    sems = pltpu.SemaphoreType.DMA((n * _peer_count(mode),))
    buffers = [pltpu.HBM(a.shape, a.dtype) for a in list(arrays) + lands]
    res = pl.pallas_call(
        body, name=name, in_specs=[hbm] * (2 * n), out_specs=[sem, sem] + [hbm] * (2 * n) + [pl.BlockSpec(memory_space=pltpu.VMEM)],
        out_shape=[sems, sems] + buffers + [jax.ShapeDtypeStruct((8, LANE), F32)],
        input_output_aliases={i: 2 + i for i in range(2 * n)},
        compiler_params=pltpu.CompilerParams(has_side_effects=pltpu.SideEffectType.DATAFLOW_SIDE_EFFECTING),
    )(*[pltpu.with_memory_space_constraint(a, pltpu.HBM) for a in list(arrays) + lands])
    return (res[0], res[1], res[2:2 + n], res[2 + n:2 + 2 * n]), res[-1]


def _exchange_wait(handle, after, name, mode):
    send_sems, recv_sems, srcs, lands = handle
    n = len(srcs)
    after = list(after) if isinstance(after, (list, tuple)) else [after]
    hbm = pl.BlockSpec(memory_space=pltpu.HBM)
    sem = pl.BlockSpec(memory_space=pltpu.SEMAPHORE)

    def body(*refs):
        for mine, arrival in _remote_copies(refs[:n], refs[n:2 * n], refs[2 * n], refs[2 * n + 1], mode):
            mine.wait_send()
            arrival.wait_recv()

    res = pl.pallas_call(
        body, name=name, in_specs=[hbm] * (2 * n) + [sem, sem] + [pl.BlockSpec(memory_space=pl.ANY)] * len(after),
        out_specs=[hbm] * (2 * n), out_shape=[pltpu.HBM(a.shape, a.dtype) for a in list(srcs) + list(lands)],
        input_output_aliases={i: i for i in range(2 * n)},
        compiler_params=pltpu.CompilerParams(has_side_effects=pltpu.SideEffectType.DATAFLOW_SIDE_EFFECTING),
    )(*srcs, *lands, send_sems, recv_sems, *after)
    return res[n:]


def _sibling_forward(lands, name):
    n = len(lands)

    def body(*refs):
        ins, outs, send_sems, recv_sems = refs[:n], refs[n:2 * n], refs[2 * n], refs[2 * n + 1]
        (x, y, c), sibling, others = _chip_places()
        copies = []
        for a, (src, out) in enumerate(zip(ins, outs)):
            for j, chip in enumerate(others):
                def copy(core, a=a, j=j, chip=chip, src=src, out=out):
                    slot = _index_of((chip[0], chip[1], core))
                    return pltpu.make_async_remote_copy(
                        src_ref=src.at[slot], dst_ref=out.at[slot], send_sem=send_sems.at[3 * a + j],
                        recv_sem=recv_sems.at[3 * a + j], device_id=sibling, device_id_type=pl.DeviceIdType.MESH)
                mine = copy(c)
                mine.start()
                copies.append((mine, copy(1 - c)))
        for mine, arrival in copies:
            arrival.wait_recv()
        for mine, arrival in copies:
            mine.wait_send()

    any_space = pl.BlockSpec(memory_space=pl.ANY)
    return pl.pallas_call(
        body, name=name, in_specs=[any_space] * n, out_specs=[any_space] * n,
        out_shape=[jax.ShapeDtypeStruct(a.shape, a.dtype) for a in lands],
        scratch_shapes=[pltpu.SemaphoreType.DMA((3 * n,)), pltpu.SemaphoreType.DMA((3 * n,))],
        input_output_aliases={i: i for i in range(n)},
    )(*lands)


def _own_slot(land, mine, slot=None):
    slot = _my_index() if slot is None else slot
    return lax.dynamic_update_slice(land, mine, (slot,) + (0,) * (land.ndim - 1))


def _adamw(w, g, m, v):
    m = ADAM_B1 * m + (1.0 - ADAM_B1) * g
    v = ADAM_B2 * v + (1.0 - ADAM_B2) * (g * g)
    m_hat = m / (1.0 - ADAM_B1 ** ADAM_STEP)
    v_hat = v / (1.0 - ADAM_B2 ** ADAM_STEP)
    delta = -ADAM_LR * (m_hat / (jnp.sqrt(v_hat) + ADAM_EPS) + ADAM_WD * w)
    return delta, m, v


def _reduce_adamw(parts, w, m, v, name, after=None):
    nparts = len(parts)
    R, C = parts[0].shape[1:]
    tr = R
    while N_DEV * tr * C * parts[0].dtype.itemsize > REDUCE_BLOCK_BYTES and tr % 32 == 0:
        tr //= 2
    steps = R // tr
    extra = [] if after is None else [after]

    def body(*refs):
        w_ref, m_ref, v_ref, g_ref, d_ref, nm_ref, nv_ref = refs[nparts + len(extra):]
        for k, p_ref in enumerate(refs[:nparts]):
            @pl.when(pl.program_id(0) // steps == k)
            def _():
                g = p_ref[0].astype(F32)
                for s in range(1, p_ref.shape[0]):
                    g = g + p_ref[s].astype(F32)
                g_ref[...] = g
                d_ref[...], nm_ref[...], nv_ref[...] = _adamw(w_ref[...], g, m_ref[...], v_ref[...])

    def part_spec(k):
        return pl.BlockSpec((parts[k].shape[0], tr, C), lambda i: (0, jnp.clip(i - k * steps, 0, steps - 1), 0))

    row = pl.BlockSpec((tr, C), lambda i: (i, 0))
    return pl.pallas_call(
        body, name=name, grid=(nparts * steps,),
        in_specs=[part_spec(k) for k in range(nparts)] + [pl.BlockSpec(memory_space=pl.ANY)] * len(extra)
                 + [row, row, row],
        out_specs=[row] * 4, out_shape=[jax.ShapeDtypeStruct((nparts * R, C), F32)] * 4,
        compiler_params=_cp(),
    )(*parts, *extra, w, m, v)


BIG = ("w_in", "w_uq", "w_ukv", "w_out_a", "w_out_b", "w_out_c", "w_o")
SMALL = ("norm_g", "b_gate", "conv_w", "conv_b", "q_a_norm_g", "kv_a_norm_g", "mla_q_norm_g", "mla_k_norm_g",
         "dil_q_norm_g", "dil_k_norm_g")
PACK_ROWS = 128
REDUCE_BLOCK_BYTES = 6 * 1024 * 1024


def _pack_local(tensors):
    flat = jnp.concatenate([t.reshape(-1) for t in tensors])
    pad = (-flat.shape[0]) % (PACK_ROWS * LANE)
    return jnp.concatenate([flat, jnp.zeros((pad,), flat.dtype)]).reshape(-1, LANE)


def _unpack_local(rows, like):
    flat = rows.reshape(-1)
    out, off = [], 0
    for t in like:
        out.append(flat[off:off + t.size].reshape(t.shape))
        off += t.size
    return out


def _cols_to_slots(a):
    k = a.shape[0]
    return a.reshape(k, N_DEV, -1).transpose(1, 0, 2)


def _slots_to_cols(s):
    return s.transpose(1, 0, 2).reshape(s.shape[1], -1)


def _rope_tables(S):
    inv = ROPE_THETA ** (-jnp.arange(0, MLA_ROPE, 2, dtype=F32) / MLA_ROPE)
    ang = jnp.arange(S, dtype=F32)[:, None] * inv[None, :]
    cos, sin = jnp.cos(ang), jnp.sin(ang)
    one = jnp.ones((S, MLA_NOPE), F32)
    z16, z32, z64 = (jnp.zeros((S, n), F32) for n in (16, 32, 64))
    cosp = jnp.concatenate([one, cos, cos, jnp.ones((S, 32), F32)], axis=1)
    sa = jnp.concatenate([z64, -sin, z16, z32], axis=1)
    sb = jnp.concatenate([z64, z16, sin, z32], axis=1)
    return cosp, sa, sb


def _alibi_slopes():
    n = DIL_GROUPS * DIL_HEADS
    m = 2.0 ** (-8.0 * jnp.arange(1, n + 1, dtype=F32) / n)
    return m.reshape(DIL_GROUPS, NPAIR, 2)


def _pad_slots(s):
    n, k, c = s.shape
    return _slots_to_cols(jnp.concatenate([s, jnp.zeros((n, k, LANE - c), s.dtype)], axis=2))


def _layer_params(gw, small, l):
    p = {}
    p["wp"] = _pad_columns(gw["w_in"])
    p["norm_g"] = small["norm_g"][l][None]
    p["b_gate"] = small["b_gate"][l][None]
    p["conv_w"] = gw["conv_w"].transpose(1, 0, 2).reshape(CONV_K, CONV_WIDTH)
    p["conv_b"] = small["conv_b"][l][None]
    p["gq"] = small["q_a_norm_g"][l][None]
    p["gkv"] = small["kv_a_norm_g"][l][None]
    p["wuqp"] = _pad_slots(gw["w_uq"])
    kv = gw["w_ukv"]
    p["wkp"] = _pad_slots(kv[:, :, :MLA_NOPE])
    p["wv"] = kv[:, :, MLA_NOPE:].transpose(1, 0, 2).reshape(MLA_KV_LORA, MLA_HEADS * MLA_V)
    zpad = jnp.zeros((1, LANE - MLA_QK), F32)
    p["gmq"] = jnp.concatenate([small["mla_q_norm_g"][l][None], zpad], axis=1)
    p["gmk"] = jnp.concatenate([small["mla_k_norm_g"][l][None], zpad], axis=1)
    tile = lambda g: jnp.broadcast_to(g[:, None, :], (DIL_GROUPS, DIL_HEADS, DIL_HEAD_DIM)).reshape(1, DIL_QK)
    p["gdq"] = tile(small["dil_q_norm_g"][l])
    p["gdk"] = tile(small["dil_k_norm_g"][l])
    p["woa"], p["wob"], p["woc"] = (_slots_to_cols(gw[n]) for n in ("w_out_a", "w_out_b", "w_out_c"))
    p["wo"] = gw["w_o"].reshape(D_MODEL, D_MODEL)
    return p


def _layer_fwd(x, p, tabs, slopes, B, S):
    proj, ht = _inproj_fwd(x, p["norm_g"], p["wp"])
    ya = _mixa_fwd(proj, p["conv_w"], p["conv_b"], B, S)
    q, k, v = _mla_prep_fwd(proj, p["gq"], p["gkv"], p["wuqp"], p["wkp"], p["wv"], p["gmq"], p["gmk"], *tabs, S)
    ob, lse_b = _mla_attn_fwd(q, k, v, B, S)
    qn, kn = _dil_prep_fwd(proj, p["gdq"], p["gdk"])
    ogs, lses = [], []
    for gi in range(DIL_GROUPS):
        o, lse = _dil_attn_fwd(gi, slopes[gi], qn, kn, proj, B, S)
        ogs.append(o)
        lses.append(lse)
    out = _merge_fwd(x, proj, p["b_gate"], ya, ob, ogs, lses, p["woa"], p["wob"], p["woc"], p["wo"])
    saved = dict(x=x, proj=proj, ht=ht, ya=ya, q=q, k=k, v=v, ob=ob, lse_b=lse_b, qn=qn, kn=kn, ogs=ogs, lses=lses)
    return out, saved


def _layer_bwd(dout, sv, p, tabs, slopes, B, S, big_ready=None):
    proj = sv["proj"]
    (dproj, dya, dob, dlb, dg0, dg1, dg2, dl0, dl1, dl2, merged, dpa, dpb, dpc, yb, yc, dbg) = _merge_bwd(
        dout, proj, p["b_gate"], sv["ya"], sv["ob"], sv["ogs"], sv["lses"], p["woa"], p["wob"], p["woc"], p["wo"])
    g = {}
    g["w_o"] = _matmul_tn(merged, dout, "dw_o").reshape(N_DEV, D_MODEL // N_DEV, D_MODEL)
    g["w_out_a"] = _cols_to_slots(_matmul_tn(sv["ya"], dpa, "dw_out_a"))
    g["w_out_b"] = _cols_to_slots(_matmul_tn(yb, dpb, "dw_out_b"))
    g["w_out_c"] = _cols_to_slots(_matmul_tn(yc, dpc, "dw_out_c"))
    g["b_gate"] = dbg[0]
    dproj, st = _mixa_bwd(dproj, dya, proj, p["conv_w"], p["conv_b"], B, S)
    g["conv_w"] = st[0:CONV_K]
    g["conv_b"] = st[CONV_K]
    dq, dk, dv = _mla_attn_bwd(sv["q"], sv["k"], sv["v"], dob, sv["lse_b"], dlb, B, S)
    dproj, dwuqp, dwkp, dwv, dgq, dgkv, dgmq, dgmk = _mla_prep_bwd(
        dproj, dq, dk, dv, proj, p["gq"], p["gkv"], p["wuqp"], p["wkp"], p["wv"], p["gmq"], p["gmk"], *tabs, S)
    g["w_uq"] = _cols_to_slots(dwuqp)[:, :, :MLA_QK]
    g["w_ukv"] = jnp.concatenate([_cols_to_slots(dwkp)[:, :, :MLA_NOPE], _cols_to_slots(dwv)], axis=2)
    g["q_a_norm_g"], g["kv_a_norm_g"] = dgq[0], dgkv[0]
    g["mla_q_norm_g"], g["mla_k_norm_g"] = dgmq[0, :MLA_QK], dgmk[0, :MLA_QK]
    dqkv = None
    for gi, (dog, dlg) in enumerate(((dg0, dl0), (dg1, dl1), (dg2, dl2))):
        dqkv = _dil_attn_bwd(gi, slopes[gi], sv["qn"], sv["kn"], proj, dog, sv["lses"][gi], dlg, dqkv, B, S)
    dproj, dgdq, dgdk = _dil_prep_bwd(dproj, *dqkv, proj, p["gdq"], p["gdk"])
    g["dil_q_norm_g"] = dgdq.reshape(DIL_GROUPS, DIL_HEADS, DIL_HEAD_DIM).sum(axis=1)
    g["dil_k_norm_g"] = dgdk.reshape(DIL_GROUPS, DIL_HEADS, DIL_HEAD_DIM).sum(axis=1)
    token = None if big_ready is None else big_ready(g)
    g["w_in"] = _unpad_columns(_matmul_nn(sv["ht"], dproj, "dw_in", token))
    token = None if big_ready is None else big_ready(g)
    dx, dng = _inproj_bwd_x(dproj, p["wp"], sv["x"], _after(token, p["norm_g"]), dout)
    g["norm_g"] = dng[0]
    return dx, g


def _after(token, a):
    return a if token is None else a + token[0:1, 0:1]


def _local_step(x, target, small, B, S, weights_of, grads_out, big_ready=None):
    tabs = _rope_tables(S)
    sl = _alibi_slopes()
    slopes = [sl[gi] * float(DIL_PATTERNS[gi][1]) for gi in range(DIL_GROUPS)]
    params, saved = [], []
    for l in range(DEPTH):
        gw, token = weights_of(l, x)
        p = _layer_params(gw, small, l)
        p["norm_g"] = _after(token, p["norm_g"])
        x, sv = _layer_fwd(x, p, tabs, slopes, B, S)
        params.append(p)
        saved.append(sv)
    dout, lparts = _loss_head(x, target)
    sq = jnp.sum(lparts[:, 0, 0])
    token = None
    for l in reversed(range(DEPTH)):
        p = dict(params[l], b_gate=_after(token, params[l]["b_gate"]))
        ready = None if big_ready is None else (lambda g, l=l: big_ready(l, g))
        dout, g = _layer_bwd(dout, saved[l], p, tabs, slopes, B, S, ready)
        token = grads_out(l, g, dout)
    return sq, dout


def kernel(x, norm_g, w_in, b_gate, conv_w, conv_b, q_a_norm_g, w_uq, kv_a_norm_g, w_ukv, mla_q_norm_g, mla_k_norm_g, dil_q_norm_g, dil_k_norm_g, w_out_a, w_out_b, w_out_c, w_o, loss_target, m_norm_g, m_w_in, m_b_gate, m_conv_w, m_conv_b, m_q_a_norm_g, m_w_uq, m_kv_a_norm_g, m_w_ukv, m_mla_q_norm_g, m_mla_k_norm_g, m_dil_q_norm_g, m_dil_k_norm_g, m_w_out_a, m_w_out_b, m_w_out_c, m_w_o, v_norm_g, v_w_in, v_b_gate, v_conv_w, v_conv_b, v_q_a_norm_g, v_w_uq, v_kv_a_norm_g, v_w_ukv, v_mla_q_norm_g, v_mla_k_norm_g, v_dil_q_norm_g, v_dil_k_norm_g, v_w_out_a, v_w_out_b, v_w_out_c, v_w_o):
    names = ("norm_g", "w_in", "b_gate", "conv_w", "conv_b", "q_a_norm_g", "w_uq", "kv_a_norm_g", "w_ukv",
             "mla_q_norm_g", "mla_k_norm_g", "dil_q_norm_g", "dil_k_norm_g", "w_out_a", "w_out_b", "w_out_c", "w_o")
    w = dict(zip(names, (norm_g, w_in, b_gate, conv_w, conv_b, q_a_norm_g, w_uq, kv_a_norm_g, w_ukv, mla_q_norm_g,
                         mla_k_norm_g, dil_q_norm_g, dil_k_norm_g, w_out_a, w_out_b, w_out_c, w_o)))
    m = dict(zip(names, (m_norm_g, m_w_in, m_b_gate, m_conv_w, m_conv_b, m_q_a_norm_g, m_w_uq, m_kv_a_norm_g, m_w_ukv,
                         m_mla_q_norm_g, m_mla_k_norm_g, m_dil_q_norm_g, m_dil_k_norm_g, m_w_out_a, m_w_out_b,
                         m_w_out_c, m_w_o)))
    v = dict(zip(names, (v_norm_g, v_w_in, v_b_gate, v_conv_w, v_conv_b, v_q_a_norm_g, v_w_uq, v_kv_a_norm_g, v_w_ukv,
                         v_mla_q_norm_g, v_mla_k_norm_g, v_dil_q_norm_g, v_dil_k_norm_g, v_w_out_a, v_w_out_b,
                         v_w_out_c, v_w_o)))
    B, S, _ = x.shape
    me = _my_index()
    cshard = CONV_WIDTH // N_DEV

    shards = [[w[n][0].astype(BF16) for n in BIG]]
    state = {}

    def widen(t):
        return lax.dynamic_update_slice(jnp.zeros((DEPTH, CONV_K, CONV_WIDTH), F32), t, (0, 0, me * cshard))

    pick = lambda d: [widen(d[n]) if n == "conv_w" else d[n] for n in SMALL]

    def weights_of(l, after):
        if l == 0:
            first = shards[0] + [conv_w]
            handle, token = _exchange_start(first, "all_gather_weights_0_start", "near")
            zero = token[0:1, 0:1]
            state["shards1"] = [(w[n][1] + zero).astype(BF16) for n in BIG]
            for n in BIG:
                state["rows", n] = [a.reshape(-1, a.shape[-1]) + zero for a in (w[n], m[n], v[n])]
            state["small"] = [_pack_local(pick(d)) + zero for d in (w, m, v)]
            busy = state["shards1"] + [a for n in BIG for a in state["rows", n]] + state["small"]
            landed = _exchange_wait(handle, busy, "all_gather_weights_0_wait", "near")
            landed = _sibling_forward(landed, "all_gather_weights_0_forward")
            got = [_own_slot(a, s[None]) for a, s in zip(landed, first)]
            state["gather"], token = _exchange_start(state["shards1"], "all_gather_weights_1_start", "gather")
            state["conv_w"] = got[-1]
        else:
            landed = _exchange_wait(state["gather"], after, "all_gather_weights_1_wait", "gather")
            got, token = [_own_slot(a, s[None]) for a, s in zip(landed, state["shards1"])], None
        gw = dict(zip(BIG, got))
        gw["conv_w"] = state["conv_w"][:, l]
        return gw, token

    recv, small_parts = {}, {}
    my_chip = 2 * lax.axis_index("x") + lax.axis_index("y")

    REST = tuple(n for n in BIG if n != "w_in")

    def big_ready(l, g):
        if l == DEPTH - 1:
            if "w_in" not in g:
                return None
            send = [g[n].astype(BF16) for n in BIG]
            state["scatter"], token = _exchange_start(send, "exchange_weight_grads_1_start", "scatter")
            state["sent", "scatter"] = send
            return token
        tag, group = ("w_in", ("w_in",)) if "w_in" in g else ("rest", REST)
        send = [g[n].astype(BF16) for n in group]
        swapped = _sibling_swap(send, "exchange_weight_grads_0_sibling_" + tag)
        send = [_chip_pair_sum(s, t, "chip_pair_sum_" + n) for n, s, t in zip(group, send, swapped)]
        state[tag], token = _exchange_start(send, "exchange_weight_grads_0_start_" + tag, "chips")
        state["sent", tag] = send
        return token

    def grads_out(l, g, after):
        small_parts[l] = [g[n] for n in SMALL]
        if l == DEPTH - 1:
            return None
        got = {}
        for key, mode, slot in (("scatter", "scatter", me), ("rest", "chips", my_chip), ("w_in", "chips", my_chip)):
            k = DEPTH - 1 if key == "scatter" else 0
            tag = "" if key == "scatter" else "_" + key
            landed = _exchange_wait(state[key], after, f"exchange_weight_grads_{k}_wait{tag}", mode)
            mine = [lax.dynamic_slice_in_dim(s, slot, 1, axis=0) for s in state["sent", key]]
            got[key] = [_own_slot(a, s, slot) for a, s in zip(landed, mine)]
        recv[DEPTH - 1] = got["scatter"]
        recv[0] = got["w_in"] + got["rest"]
        assert BIG == ("w_in",) + REST
        return None

    sq, grad_x = _local_step(x.reshape(B * S, D_MODEL), loss_target.reshape(B * S, D_MODEL), w, B, S,
                             weights_of, grads_out, big_ready)
    loss = lax.psum(sq * (0.5 / D_MODEL), AXES)

    part = {n: jnp.stack([small_parts[l][i] for l in range(DEPTH)]) for i, n in enumerate(SMALL)}
    small_like = [part[n] for n in SMALL]
    pack = _pack_local(small_like)
    handle, token = _exchange_start([pack], "all_gather_small_grads_start", "gather")

    res, done = {}, []
    for i, n in enumerate(BIG):
        outs = _reduce_adamw([recv[l][i] for l in range(DEPTH)], *state["rows", n], "reduce_adamw_" + n, token)
        res[n] = tuple(a.reshape(w[n].shape) for a in outs)
        done.append(outs[0])

    landed, = _exchange_wait(handle, done, "all_gather_small_grads_wait", "gather")
    parts = _own_slot(landed, pack[None])
    gs, ds, ms, vs = _reduce_adamw([parts], *state["small"], "reduce_adamw_small")
    for n, t in zip(SMALL, zip(*(_unpack_local(a, small_like) for a in (gs, ds, ms, vs)))):
        if n == "conv_w":
            t = tuple(lax.dynamic_slice(a, (0, 0, me * cshard), (DEPTH, CONV_K, cshard)) for a in t)
        res[n] = t

    out = [loss, grad_x.reshape(B, S, D_MODEL)]
    for i in range(4):
        out += [res[n][i] for n in names]
    return tuple(out)
```

```python
import jax
import jax.numpy as jnp
from jax import lax
from jax.experimental import pallas as pl
from jax.experimental.pallas import tpu as pltpu

F32 = jnp.float32
BF16 = jnp.bfloat16

D_MODEL = 1024
DEPTH = 2
CONV_WIDTH = 512
CONV_K = 3
MLA_HEADS = 8
MLA_Q_LORA = 256
MLA_KV_LORA = 128
MLA_NOPE = 64
MLA_ROPE = 32
MLA_V = 64
MLA_QK = MLA_NOPE + MLA_ROPE
ROPE_THETA = 10000.0
DIL_PATTERNS = ((128, 1), (512, 4), (2048, 16))
DIL_GROUPS = 3
DIL_HEADS = 8
DIL_HEAD_DIM = 64
DIL_WIDTH = DIL_HEADS * DIL_HEAD_DIM
DIL_QK = DIL_GROUPS * DIL_WIDTH
EPS = 1e-6
N_IN = 11168

ADAM_LR = 0.001
ADAM_B1 = 0.9
ADAM_B2 = 0.999
ADAM_EPS = 1e-08
ADAM_WD = 0.01
ADAM_STEP = 10

N_DEV = 8
AXES = ("x", "y", "c")
LANE = 128
HALF = 64
NPAIR = 4

CB_AB, CB_AC, CB_AX, CB_AZ = 0, 4, 8, 12
CB_CQ, CB_CKV, CB_KPE = 16, 18, 19
CB_BZ = 20
CB_DQ, CB_DK, CB_DV = 24, 36, 48
CB_CZ, CB_GATE = 60, 64
NCB = 88
PP = NCB * LANE
KPE_END = CB_KPE * LANE + MLA_ROPE
SHARD_COLS = N_IN // N_DEV
NEG = -1e30
VMEM_LIMIT = 56 * 1024 * 1024


def _pad_columns(shards):
    K = shards.shape[1]
    gap = LANE - MLA_ROPE
    tr = _tile(K, 128)

    def body(s_ref, o_ref):
        for p in range(N_DEV):
            cut = min(max(KPE_END - p * SHARD_COLS, 0), SHARD_COLS)
            for a, b in ((0, cut), (cut, SHARD_COLS)):
                if a < b:
                    to = p * SHARD_COLS + a + (gap if p * SHARD_COLS + a >= KPE_END else 0)
                    o_ref[:, to:to + b - a] = s_ref[p, :, a:b]
        o_ref[:, KPE_END:KPE_END + gap] = jnp.zeros((tr, gap), o_ref.dtype)

    return pl.pallas_call(
        body, name="pad_columns", grid=(K // tr,),
        in_specs=[pl.BlockSpec((N_DEV, tr, SHARD_COLS), lambda i: (0, i, 0))],
        out_specs=pl.BlockSpec((tr, PP), lambda i: (i, 0)),
        out_shape=jax.ShapeDtypeStruct((K, PP), shards.dtype),
        compiler_params=_cp(),
    )(shards)


def _unpad_columns(wp):
    K = wp.shape[0]
    gap = LANE - MLA_ROPE
    tr = _tile(K, 128)

    def body(w_ref, o_ref):
        for p in range(N_DEV):
            cut = min(max(KPE_END - p * SHARD_COLS, 0), SHARD_COLS)
            for a, b in ((0, cut), (cut, SHARD_COLS)):
                if a < b:
                    at = p * SHARD_COLS + a + (gap if p * SHARD_COLS + a >= KPE_END else 0)
                    o_ref[p, :, a:b] = w_ref[:, at:at + b - a]

    return pl.pallas_call(
        body, name="unpad_columns", grid=(K // tr,),
        in_specs=[pl.BlockSpec((tr, PP), lambda i: (i, 0))],
        out_specs=pl.BlockSpec((N_DEV, tr, SHARD_COLS), lambda i: (0, i, 0)),
        out_shape=jax.ShapeDtypeStruct((N_DEV, K, SHARD_COLS), wp.dtype),
        compiler_params=_cp(),
    )(wp)


def _put_copies(stages, dst_ref, sems, slot, rows, cols):
    return [pltpu.make_async_copy(st.at[slot], dst_ref.at[rows, pl.ds(c0, st.shape[-1])], sems.at[slot, k])
            for k, (st, c0) in enumerate(zip(stages, cols))]


def _put_pipeline(step, nsteps, copies_of, fill):
    @pl.when(step >= 2)
    def _():
        for cp in copies_of(step - 2):
            cp.wait()

    fill(step % 2)
    for cp in copies_of(step):
        cp.start()

    @pl.when(step == nsteps - 1)
    def _():
        if nsteps >= 2:
            for cp in copies_of(step - 1):
                cp.wait()
        for cp in copies_of(step):
            cp.wait()


def _cp():
    return pltpu.CompilerParams(vmem_limit_bytes=VMEM_LIMIT)


def _rstd(x, n):
    return lax.rsqrt(jnp.sum(x * x, axis=-1, keepdims=True) * (1.0 / n) + EPS)


def _sigmoid(z):
    return 1.0 / (1.0 + jnp.exp(-z))


def _silu(z):
    return z * _sigmoid(z)


def _silu_and_grad(z):
    s = _sigmoid(z)
    return z * s, s * (1.0 + z * (1.0 - s))


def _mm(a, b):
    return jnp.dot(a.astype(BF16), b.astype(BF16), preferred_element_type=F32)


def _mm_nt(a, b):
    return lax.dot_general(a.astype(BF16), b.astype(BF16), (((1,), (1,)), ((), ())), preferred_element_type=F32)


def _mm_tn(a, b):
    return lax.dot_general(a.astype(BF16), b.astype(BF16), (((0,), (0,)), ((), ())), preferred_element_type=F32)


def _lane_lo(shape):
    return lax.broadcasted_iota(jnp.int32, shape, len(shape) - 1) < HALF


def _head_bcast_sum(x, terms=3):
    w = x.shape[-1]
    same = (lax.broadcasted_iota(jnp.int32, (w, w), 0) // HALF) == (lax.broadcasted_iota(jnp.int32, (w, w), 1) // HALF)
    ones = jnp.where(same, 1.0, 0.0).astype(jnp.bfloat16)
    total = None
    for _ in range(terms):
        term = x.astype(jnp.bfloat16)
        x = x - term.astype(F32)
        part = jnp.dot(term, ones, preferred_element_type=F32)
        total = part if total is None else total + part
    return total


def _rope(t, cos, sa, sb):
    return t * cos + pltpu.roll(t, LANE - 16, axis=1) * sa + pltpu.roll(t, 16, axis=1) * sb


def _rope_t(d, cos, sa, sb):
    return d * cos + pltpu.roll(d * sa, 16, axis=1) + pltpu.roll(d * sb, LANE - 16, axis=1)


def _shift_down(u, k):
    rows = lax.broadcasted_iota(jnp.int32, u.shape, 0)
    return jnp.where(rows >= k, pltpu.roll(u, k, axis=0), 0.0)


def _shift_up(u, k):
    n = u.shape[0]
    rows = lax.broadcasted_iota(jnp.int32, u.shape, 0)
    return jnp.where(rows < n - k, pltpu.roll(u, n - k, axis=0), 0.0)


def _tile(n, want):
    t = min(n, want)
    assert n % t == 0, (n, want)
    return t


def _inproj_fwd(x, g, wp):
    T = x.shape[0]
    tm, tn = _tile(T, 2048), 512

    def body(x_ref, g_ref, w_ref, proj_ref, ht_ref, h_ref):
        @pl.when(pl.program_id(1) == 0)
        def _():
            n = min(tm, 512)
            for r0 in range(0, tm, n):
                xv = x_ref[r0:r0 + n, :]
                h = xv * _rstd(xv, D_MODEL) * g_ref[...]
                h_ref[r0:r0 + n, :] = h.astype(BF16)
                ht_ref[:, r0:r0 + n] = h.T.astype(BF16)

        proj_ref[...] = jnp.dot(h_ref[...], w_ref[...], preferred_element_type=F32).astype(BF16)

    return pl.pallas_call(
        body, name="inproj_fwd", grid=(T // tm, PP // tn),
        in_specs=[pl.BlockSpec((tm, D_MODEL), lambda i, j: (i, 0)),
                  pl.BlockSpec((1, D_MODEL), lambda i, j: (0, 0)),
                  pl.BlockSpec((D_MODEL, tn), lambda i, j: (0, j))],
        out_specs=[pl.BlockSpec((tm, tn), lambda i, j: (i, j)),
                   pl.BlockSpec((D_MODEL, tm), lambda i, j: (0, i))],
        out_shape=[jax.ShapeDtypeStruct((T, PP), BF16), jax.ShapeDtypeStruct((D_MODEL, T), BF16)],
        scratch_shapes=[pltpu.VMEM((tm, D_MODEL), BF16)],
        compiler_params=_cp(),
    )(x, g, wp)


def _matmul_nn(at, b, name, after=None):
    K, T = at.shape
    N = b.shape[1]
    tt, tn = _tile(T, 1024), _tile(N, 2816)
    nk = T // tt
    unread = [] if after is None else [after]

    def body(a_ref, b_ref, *refs):
        o_ref, acc_ref = refs[len(unread):]
        k = pl.program_id(1)

        @pl.when(k == 0)
        def _():
            acc_ref[...] = jnp.zeros_like(acc_ref)

        acc_ref[...] += jnp.dot(a_ref[...], b_ref[...], preferred_element_type=F32)

        @pl.when(k == nk - 1)
        def _():
            o_ref[...] = acc_ref[...].astype(BF16)

    return pl.pallas_call(
        body, name=name, grid=(N // tn, nk),
        in_specs=[pl.BlockSpec((K, tt), lambda j, k: (0, k)),
                  pl.BlockSpec((tt, tn), lambda j, k: (k, j))] + [pl.BlockSpec(memory_space=pl.ANY)] * len(unread),
        out_specs=pl.BlockSpec((K, tn), lambda j, k: (0, j)),
        out_shape=jax.ShapeDtypeStruct((K, N), BF16),
        scratch_shapes=[pltpu.VMEM((K, tn), F32)],
        compiler_params=_cp(),
    )(at, b, *unread)


def _matmul_tn(a, b, name):
    T, K = a.shape
    N = b.shape[1]
    tt, tn = _tile(T, 512), _tile(N, 1024)

    def body(a_ref, b_ref, o_ref):
        @pl.when(pl.program_id(1) == 0)
        def _():
            o_ref[...] = jnp.zeros_like(o_ref)

        o_ref[...] += _mm_tn(a_ref[...], b_ref[...])

    return pl.pallas_call(
        body, name=name, grid=(N // tn, T // tt),
        in_specs=[pl.BlockSpec((tt, K), lambda j, k: (k, 0)),
                  pl.BlockSpec((tt, tn), lambda j, k: (k, j))],
        out_specs=pl.BlockSpec((K, tn), lambda j, k: (0, j)),
        out_shape=jax.ShapeDtypeStruct((K, N), F32),
        compiler_params=_cp(),
    )(a, b)


def _inproj_bwd_x(dproj, wp, x, g, dout):
    T = x.shape[0]
    tm, tk = _tile(T, 1024), 1024
    nk = PP // tk

    def body(dp_ref, w_ref, x_ref, g_ref, do_ref, dx_ref, dg_ref, acc_ref):
        i, k = pl.program_id(0), pl.program_id(1)

        @pl.when(k == 0)
        def _():
            acc_ref[...] = jnp.zeros_like(acc_ref)

        @pl.when((k == 0) & (i == 0))
        def _():
            dg_ref[...] = jnp.zeros_like(dg_ref)

        acc_ref[...] += _mm_nt(dp_ref[...], w_ref[...])

        @pl.when(k == nk - 1)
        def _():
            dh = acc_ref[...]
            xv = x_ref[...]
            r = _rstd(xv, D_MODEL)
            gy = dh * g_ref[...]
            dot = jnp.sum(xv * gy, axis=-1, keepdims=True) * (1.0 / D_MODEL)
            dx_ref[...] = do_ref[...] + r * gy - xv * (r * r * r) * dot
            dg_ref[...] += jnp.sum(dh * xv * r, axis=0, keepdims=True)

    return pl.pallas_call(
        body, name="inproj_bwd_x", grid=(T // tm, nk),
        in_specs=[pl.BlockSpec((tm, tk), lambda i, k: (i, k)),
                  pl.BlockSpec((D_MODEL, tk), lambda i, k: (0, k)),
                  pl.BlockSpec((tm, D_MODEL), lambda i, k: (i, 0)),
                  pl.BlockSpec((1, D_MODEL), lambda i, k: (0, 0)),
                  pl.BlockSpec((tm, D_MODEL), lambda i, k: (i, 0))],
        out_specs=[pl.BlockSpec((tm, D_MODEL), lambda i, k: (i, 0)),
                   pl.BlockSpec((1, D_MODEL), lambda i, k: (0, 0))],
        out_shape=[jax.ShapeDtypeStruct((T, D_MODEL), F32), jax.ShapeDtypeStruct((1, D_MODEL), F32)],
        scratch_shapes=[pltpu.VMEM((tm, D_MODEL), F32)],
        compiler_params=_cp(),
    )(dproj, wp, x, g, dout)


A_SEGS = (CB_AB, CB_AC, CB_AX, CB_AZ)


def _mixa_fwd(proj, cw, cb, B, S):
    nc = CONV_WIDTH // LANE

    def body(ab_ref, ac_ref, ax_ref, az_ref, cw_ref, cb_ref, y_ref):
        ab, ac, ax, az = (r[...].astype(F32) for r in (ab_ref, ac_ref, ax_ref, az_ref))
        u = ac * ax
        conv = cb_ref[...] + cw_ref[0:1, :] * _shift_down(u, 2) + cw_ref[1:2, :] * _shift_down(u, 1) + cw_ref[2:3, :] * u
        y_ref[...] = (ab * conv * _silu(az)).astype(BF16)

    return pl.pallas_call(
        body, name="mixa_fwd", grid=(B, nc),
        in_specs=[pl.BlockSpec((S, LANE), lambda b, j, c0=c0: (b, c0 + j)) for c0 in A_SEGS]
                 + [pl.BlockSpec((CONV_K, LANE), lambda b, j: (0, j)),
                    pl.BlockSpec((1, LANE), lambda b, j: (0, j))],
        out_specs=pl.BlockSpec((S, LANE), lambda b, j: (b, j)),
        out_shape=jax.ShapeDtypeStruct((B * S, CONV_WIDTH), BF16),
        compiler_params=_cp(),
    )(proj, proj, proj, proj, cw, cb)


def _mixa_bwd(dproj, dy, proj, cw, cb, B, S):
    nc = CONV_WIDTH // LANE

    def body(dpin_ref, dy_ref, ab_ref, ac_ref, ax_ref, az_ref, cw_ref, cb_ref, dp_ref, st_ref, stage, sems):
        del dpin_ref
        j, b = pl.program_id(0), pl.program_id(1)
        ab, ac, ax, az = (r[...].astype(F32) for r in (ab_ref, ac_ref, ax_ref, az_ref))
        u = ac * ax
        u1, u2 = _shift_down(u, 1), _shift_down(u, 2)
        w0, w1, w2 = cw_ref[0:1, :], cw_ref[1:2, :], cw_ref[2:3, :]
        conv = cb_ref[...] + w0 * u2 + w1 * u1 + w2 * u
        s, ds_az = _silu_and_grad(az)
        d = dy_ref[...]
        dconv = d * ab * s
        du = w2 * dconv + w1 * _shift_up(dconv, 1) + w0 * _shift_up(dconv, 2)
        grads = (d * conv * s, du * ax, du * ac, d * ab * conv * ds_az)

        def fill(slot):
            for k, v in enumerate(grads):
                stage[slot, k] = v.astype(BF16)

        def copies_of(step):
            sj, sb = step // B, step % B
            return _put_copies([stage.at[:, k] for k in range(4)], dp_ref, sems, step % 2,
                               pl.ds(pl.multiple_of(sb * S, S), S),
                               [pl.multiple_of((c0 + sj) * LANE, LANE) for c0 in A_SEGS])

        _put_pipeline(j * B + b, nc * B, copies_of, fill)
        row = lax.broadcasted_iota(jnp.int32, (8, LANE), 0)
        st = jnp.zeros((8, LANE), F32)
        for r, v in enumerate((dconv * u2, dconv * u1, dconv * u, dconv)):
            st = st + jnp.where(row == r, jnp.sum(v, axis=0, keepdims=True), 0.0)

        @pl.when(pl.program_id(1) == 0)
        def _():
            st_ref[...] = st

        @pl.when(pl.program_id(1) != 0)
        def _():
            st_ref[...] += st

    return pl.pallas_call(
        body, name="mixa_bwd", grid=(nc, B),
        in_specs=[pl.BlockSpec(memory_space=pl.ANY),
                  pl.BlockSpec((S, LANE), lambda j, b: (b, j))]
                 + [pl.BlockSpec((S, LANE), lambda j, b, c0=c0: (b, c0 + j)) for c0 in A_SEGS]
                 + [pl.BlockSpec((CONV_K, LANE), lambda j, b: (0, j)),
                    pl.BlockSpec((1, LANE), lambda j, b: (0, j))],
        out_specs=[pl.BlockSpec(memory_space=pl.ANY),
                   pl.BlockSpec((8, LANE), lambda j, b: (0, j))],
        out_shape=[jax.ShapeDtypeStruct(dproj.shape, BF16), jax.ShapeDtypeStruct((8, CONV_WIDTH), F32)],
        scratch_shapes=[pltpu.VMEM((2, 4, S, LANE), BF16), pltpu.SemaphoreType.DMA((2, 4))],
        input_output_aliases={0: 0},
        compiler_params=_cp(),
    )(dproj, dy, proj, proj, proj, proj, cw, cb)


def _mla_prep_fwd(proj, gq, gkv, wuqp, wkp, wv, gmq, gmk, cos, sa, sb, S):
    T = proj.shape[0]
    ts = _tile(S, 512)
    ns = S // ts
    W = MLA_HEADS * LANE

    def body(p_ref, gq_ref, gkv_ref, wuq_ref, wk_ref, wv_ref, gmq_ref, gmk_ref, cos_ref, sa_ref, sb_ref,
             q_ref, k_ref, v_ref):
        cq = p_ref[:, 0:2 * LANE].astype(F32)
        ckv = p_ref[:, 2 * LANE:3 * LANE].astype(F32)
        kpe = pltpu.roll(p_ref[:, 3 * LANE:4 * LANE].astype(F32), HALF, axis=1)
        cqn = cq * _rstd(cq, MLA_Q_LORA) * gq_ref[...]
        ckn = (ckv * _rstd(ckv, MLA_KV_LORA) * gkv_ref[...]).astype(BF16)
        q0 = _mm(cqn, wuq_ref[...])
        kn = _mm(ckn, wk_ref[...])
        v_ref[...] = _mm(ckn, wv_ref[...]).astype(BF16)
        c, a, b = cos_ref[...], sa_ref[...], sb_ref[...]
        kpe_rot = _rope(kpe * gmk_ref[...], c, a, b)
        for h in range(MLA_HEADS):
            q0h = q0[:, h * LANE:(h + 1) * LANE]
            q_ref[h] = _rope(q0h * _rstd(q0h, MLA_QK) * gmq_ref[...], c, a, b).astype(BF16)
            knh = kn[:, h * LANE:(h + 1) * LANE]
            k_ref[h] = (_rstd(knh + kpe, MLA_QK) * (knh * gmk_ref[...] + kpe_rot)).astype(BF16)

    def whole(r, c):
        return pl.BlockSpec((r, c), lambda i: (0, 0))

    tab = pl.BlockSpec((ts, LANE), lambda i: (i % ns, 0))
    return pl.pallas_call(
        body, name="mla_prep_fwd", grid=(T // ts,),
        in_specs=[pl.BlockSpec((ts, 4 * LANE), lambda i: (i, CB_CQ // 4)),
                  whole(1, MLA_Q_LORA), whole(1, MLA_KV_LORA), whole(MLA_Q_LORA, W), whole(MLA_KV_LORA, W),
                  whole(MLA_KV_LORA, MLA_HEADS * MLA_V), whole(1, LANE), whole(1, LANE), tab, tab, tab],
        out_specs=[pl.BlockSpec((MLA_HEADS, ts, LANE), lambda i: (0, i, 0)),
                   pl.BlockSpec((MLA_HEADS, ts, LANE), lambda i: (0, i, 0)),
                   pl.BlockSpec((ts, MLA_HEADS * MLA_V), lambda i: (i, 0))],
        out_shape=[jax.ShapeDtypeStruct((MLA_HEADS, T, LANE), BF16), jax.ShapeDtypeStruct((MLA_HEADS, T, LANE), BF16),
                   jax.ShapeDtypeStruct((T, MLA_HEADS * MLA_V), BF16)],
        compiler_params=_cp(),
    )(proj, gq, gkv, wuqp, wkp, wv, gmq, gmk, cos, sa, sb)


def _mla_prep_bwd(dproj, dq, dk, dv, proj, gq, gkv, wuqp, wkp, wv, gmq, gmk, cos, sa, sb, S):
    T = proj.shape[0]
    ts = _tile(S, 256)
    ns = S // ts
    W = MLA_HEADS * LANE

    def body(dpin_ref, dq_ref, dk_ref, dv_ref, p_ref, gq_ref, gkv_ref, wuq_ref, wk_ref, wv_ref, gmq_ref, gmk_ref,
             cos_ref, sa_ref, sb_ref,
             dp_ref, dwuq_ref, dwk_ref, dwv_ref, dgq_ref, dgkv_ref, dgmq_ref, dgmk_ref, dq0_ref, dkn_ref):
        del dpin_ref

        @pl.when(pl.program_id(0) == 0)
        def _():
            for r in (dwuq_ref, dwk_ref, dwv_ref, dgq_ref, dgkv_ref, dgmq_ref, dgmk_ref):
                r[...] = jnp.zeros_like(r)

        cq = p_ref[:, 0:2 * LANE].astype(F32)
        ckv = p_ref[:, 2 * LANE:3 * LANE].astype(F32)
        kpe = pltpu.roll(p_ref[:, 3 * LANE:4 * LANE].astype(F32), HALF, axis=1)
        rq = _rstd(cq, MLA_Q_LORA)
        rkv = _rstd(ckv, MLA_KV_LORA)
        gq, gkv, gmq, gmk = gq_ref[...], gkv_ref[...], gmq_ref[...], gmk_ref[...]
        cqn = (cq * rq * gq).astype(BF16)
        ckn = (ckv * rkv * gkv).astype(BF16)
        q0 = _mm(cqn, wuq_ref[...])
        kn = _mm(ckn, wk_ref[...])
        c, a, b = cos_ref[...], sa_ref[...], sb_ref[...]
        lane = lax.broadcasted_iota(jnp.int32, (ts, LANE), 1)
        dgmq = jnp.zeros((1, LANE), F32)
        dgmk = jnp.zeros((1, LANE), F32)
        nope = lane < MLA_NOPE
        kpe_rot = _rope(kpe * gmk, c, a, b)
        dk_sum = jnp.zeros((ts, LANE), F32)
        back = jnp.zeros((ts, 1), F32)
        for h in range(MLA_HEADS):
            q0h = q0[:, h * LANE:(h + 1) * LANE]
            r = _rstd(q0h, MLA_QK)
            d1 = _rope_t(dq_ref[h], c, a, b)
            gy = d1 * gmq
            dq0_ref[:, h * LANE:(h + 1) * LANE] = (
                r * gy - q0h * (r * r * r) * (jnp.sum(q0h * gy, axis=-1, keepdims=True) * (1.0 / MLA_QK))).astype(BF16)
            dgmq = dgmq + jnp.sum(d1 * q0h * r, axis=0, keepdims=True)
            knh = kn[:, h * LANE:(h + 1) * LANE]
            dkh = dk_ref[h]
            r = _rstd(knh + kpe, MLA_QK)
            r3dot = (r * r * r) * (jnp.sum((knh * gmk + kpe_rot) * dkh, axis=-1, keepdims=True) * (1.0 / MLA_QK))
            dkn_ref[:, h * LANE:(h + 1) * LANE] = jnp.where(nope, r * gmk * dkh - knh * r3dot, 0.0).astype(BF16)
            dgmk = dgmk + jnp.sum(jnp.where(nope, dkh * knh * r, 0.0), axis=0, keepdims=True)
            dk_sum = dk_sum + r * dkh
            back = back + r3dot
        rot = jnp.where(nope | (lane >= MLA_QK), 0.0, _rope_t(dk_sum, c, a, b))
        dkpe = gmk * rot - kpe * back
        dgmk = dgmk + jnp.sum(kpe * rot, axis=0, keepdims=True)
        dq0 = dq0_ref[...]
        dkn = dkn_ref[...]
        dvv = dv_ref[...]
        dwuq_ref[...] += _mm_tn(cqn, dq0)
        dwk_ref[...] += _mm_tn(ckn, dkn)
        dwv_ref[...] += _mm_tn(ckn, dvv)
        dgmq_ref[...] += dgmq
        dgmk_ref[...] += dgmk
        dcqn = _mm_nt(dq0, wuq_ref[...])
        gy = dcqn * gq
        dp_ref[:, 0:2 * LANE] = (
            rq * gy - cq * (rq * rq * rq) * (jnp.sum(cq * gy, axis=-1, keepdims=True) * (1.0 / MLA_Q_LORA))).astype(BF16)
        dgq_ref[...] += jnp.sum(dcqn * cq * rq, axis=0, keepdims=True)
        dckn = _mm_nt(dkn, wk_ref[...]) + _mm_nt(dvv, wv_ref[...])
        gy = dckn * gkv
        dp_ref[:, 2 * LANE:3 * LANE] = (
            rkv * gy - ckv * (rkv * rkv * rkv) * (jnp.sum(ckv * gy, axis=-1, keepdims=True) * (1.0 / MLA_KV_LORA))).astype(BF16)
        dgkv_ref[...] += jnp.sum(dckn * ckv * rkv, axis=0, keepdims=True)
        dp_ref[:, 3 * LANE:4 * LANE] = pltpu.roll(dkpe, HALF, axis=1).astype(BF16)

    def whole(r, c):
        return pl.BlockSpec((r, c), lambda i: (0, 0))

    tab = pl.BlockSpec((ts, LANE), lambda i: (i % ns, 0))
    heads = pl.BlockSpec((MLA_HEADS, ts, LANE), lambda i: (0, i, 0))
    return pl.pallas_call(
        body, name="mla_prep_bwd", grid=(T // ts,),
        in_specs=[pl.BlockSpec(memory_space=pl.ANY), heads, heads,
                  pl.BlockSpec((ts, MLA_HEADS * MLA_V), lambda i: (i, 0)),
                  pl.BlockSpec((ts, 4 * LANE), lambda i: (i, CB_CQ // 4)),
                  whole(1, MLA_Q_LORA), whole(1, MLA_KV_LORA), whole(MLA_Q_LORA, W), whole(MLA_KV_LORA, W),
                  whole(MLA_KV_LORA, MLA_HEADS * MLA_V), whole(1, LANE), whole(1, LANE), tab, tab, tab],
        out_specs=[pl.BlockSpec((ts, 4 * LANE), lambda i: (i, CB_CQ // 4)),
                   whole(MLA_Q_LORA, W), whole(MLA_KV_LORA, W), whole(MLA_KV_LORA, MLA_HEADS * MLA_V),
                   whole(1, MLA_Q_LORA), whole(1, MLA_KV_LORA), whole(1, LANE), whole(1, LANE)],
        out_shape=[jax.ShapeDtypeStruct(dproj.shape, BF16),
                   jax.ShapeDtypeStruct((MLA_Q_LORA, W), F32), jax.ShapeDtypeStruct((MLA_KV_LORA, W), F32),
                   jax.ShapeDtypeStruct((MLA_KV_LORA, MLA_HEADS * MLA_V), F32),
                   jax.ShapeDtypeStruct((1, MLA_Q_LORA), F32), jax.ShapeDtypeStruct((1, MLA_KV_LORA), F32),
                   jax.ShapeDtypeStruct((1, LANE), F32), jax.ShapeDtypeStruct((1, LANE), F32)],
        scratch_shapes=[pltpu.VMEM((ts, W), BF16), pltpu.VMEM((ts, W), BF16)],
        input_output_aliases={0: 0},
        compiler_params=_cp(),
    )(dproj, dq, dk, dv, proj, gq, gkv, wuqp, wkp, wv, gmq, gmk, cos, sa, sb)


def _dil_prep_fwd(proj, gq, gk):
    T = proj.shape[0]
    ts = _tile(T, 512)

    def body(pq_ref, pk_ref, gq_ref, gk_ref, q_ref, k_ref):
        for c in range(NPAIR):
            cs = slice(c * LANE, (c + 1) * LANE)
            t = jnp.concatenate([pq_ref[:, cs], pk_ref[:, cs]], axis=1).astype(F32)
            y = t * lax.rsqrt(_head_bcast_sum(t * t, terms=2) * (1.0 / DIL_HEAD_DIM) + EPS)
            q_ref[:, cs] = (y[:, 0:LANE] * gq_ref[:, cs]).astype(BF16)
            k_ref[:, cs] = (y[:, LANE:2 * LANE] * gk_ref[:, cs]).astype(BF16)

    col = pl.BlockSpec((1, DIL_WIDTH), lambda i, g: (0, g))
    out = pl.BlockSpec((ts, DIL_WIDTH), lambda i, g: (i, g))
    seg = lambda c0: pl.BlockSpec((ts, DIL_WIDTH), lambda i, g: (i, c0 // NPAIR + g))
    return pl.pallas_call(
        body, name="dil_prep_fwd", grid=(T // ts, DIL_GROUPS),
        in_specs=[seg(CB_DQ), seg(CB_DK), col, col],
        out_specs=[out, out],
        out_shape=[jax.ShapeDtypeStruct((T, DIL_QK), BF16)] * 2,
        compiler_params=_cp(),
    )(proj, proj, gq, gk)


def _dil_prep_bwd(dproj, ddq, ddk, ddv, proj, gq, gk):
    T = proj.shape[0]
    ts = _tile(T, 512)
    nt = T // ts

    def body(dpin_ref, ddq_ref, ddk_ref, ddv_ref, pq_ref, pk_ref, gq_ref, gk_ref, dp_ref, dgq_ref, dgk_ref,
             stage, sems):
        del dpin_ref
        g, i = pl.program_id(0), pl.program_id(1)

        @pl.when(i == 0)
        def _():
            dgq_ref[...] = jnp.zeros_like(dgq_ref)
            dgk_ref[...] = jnp.zeros_like(dgk_ref)

        def fill(slot):
            stage[slot, 2] = ddv_ref[...].astype(BF16)
            for c in range(NPAIR):
                cs = slice(c * LANE, (c + 1) * LANE)
                t = jnp.concatenate([pq_ref[:, cs], pk_ref[:, cs]], axis=1).astype(F32)
                d = jnp.concatenate([ddq_ref[:, cs], ddk_ref[:, cs]], axis=1)
                gy = d * jnp.concatenate([gq_ref[:, cs], gk_ref[:, cs]], axis=1)
                r = lax.rsqrt(_head_bcast_sum(t * t, terms=2) * (1.0 / DIL_HEAD_DIM) + EPS)
                dot = _head_bcast_sum(t * gy, terms=2) * (1.0 / DIL_HEAD_DIM)
                dx = (r * gy - t * (r * r * r) * dot).astype(BF16)
                stage[slot, 0, :, cs] = dx[:, 0:LANE]
                stage[slot, 1, :, cs] = dx[:, LANE:2 * LANE]
                part = jnp.sum(d * t * r, axis=0, keepdims=True)
                dgq_ref[:, cs] += part[:, 0:LANE]
                dgk_ref[:, cs] += part[:, LANE:2 * LANE]

        def copies_of(step):
            sg, si = step // nt, step % nt
            return _put_copies([stage.at[:, k] for k in range(3)], dp_ref, sems, step % 2,
                               pl.ds(pl.multiple_of(si * ts, ts), ts),
                               [pl.multiple_of((c0 + NPAIR * sg) * LANE, LANE) for c0 in (CB_DQ, CB_DK, CB_DV)])

        _put_pipeline(g * nt + i, DIL_GROUPS * nt, copies_of, fill)

    col = pl.BlockSpec((1, DIL_WIDTH), lambda g, i: (0, g))
    tok = pl.BlockSpec((ts, DIL_WIDTH), lambda g, i: (i, g))
    seg = lambda c0: pl.BlockSpec((ts, DIL_WIDTH), lambda g, i: (i, c0 // NPAIR + g))
    return pl.pallas_call(
        body, name="dil_prep_bwd", grid=(DIL_GROUPS, nt),
        in_specs=[pl.BlockSpec(memory_space=pl.ANY), tok, tok, tok, seg(CB_DQ), seg(CB_DK), col, col],
        out_specs=[pl.BlockSpec(memory_space=pl.ANY), col, col],
        out_shape=[jax.ShapeDtypeStruct(dproj.shape, BF16), jax.ShapeDtypeStruct((1, DIL_QK), F32),
                   jax.ShapeDtypeStruct((1, DIL_QK), F32)],
        scratch_shapes=[pltpu.VMEM((2, 3, ts, DIL_WIDTH), BF16), pltpu.SemaphoreType.DMA((2, 3))],
        input_output_aliases={0: 0},
        compiler_params=_cp(),
    )(dproj, ddq, ddk, ddv, proj, proj, gq, gk)


COPY_ROWS = 256


def _to_classes(src_ref, dst_ref, d, L, scale=None):
    m = min(L, max(8, COPY_ROWS // d))
    for c0 in range(0, L, m):
        x = src_ref[c0 * d:(c0 + m) * d, :].astype(F32)
        if scale is not None:
            x = x * scale
        if d > 1:
            x = jnp.swapaxes(x.reshape(m, d, LANE), 0, 1)
        for r in range(d):
            dst_ref[r * L + c0:r * L + c0 + m, :] = (x[r] if d > 1 else x).astype(dst_ref.dtype)


def _from_classes(src_ref, dst_ref, d, L):
    n = min(L, COPY_ROWS)
    for r in range(d):
        for c0 in range(0, L, n):
            rows = pl.ds(r + c0 * d, n, stride=d) if d > 1 else pl.ds(c0, n)
            dst_ref[rows, :] = src_ref[r * L + c0:r * L + c0 + n, :].astype(dst_ref.dtype)


MLA_TQ, MLA_TK = 512, 512


def _causal_bias(tq, tk, shift):
    row = lax.broadcasted_iota(jnp.int32, (tq, tk), 0)
    col = lax.broadcasted_iota(jnp.int32, (tq, tk), 1)
    return jnp.where(row >= col + shift, 0.0, NEG)


def _mla_specs(S):
    heads = pl.BlockSpec((2, S, LANE), lambda b, j: (j, b, 0))
    pair = pl.BlockSpec((S, LANE), lambda b, j: (b, j))
    return heads, pair


def _mla_attn_fwd(q, k, v, B, S):
    tq = _tile(S, MLA_TQ)
    tk = _tile(tq, MLA_TK)
    nd = tq // tk
    scale = MLA_QK ** -0.5
    heads, pair = _mla_specs(S)

    def body(q_ref, k_ref, v_ref, o_ref, lse_ref):
        lo, lok = _lane_lo((tq, LANE)), _lane_lo((tk, LANE))
        diag = [_causal_bias(tq, tk, i * tk) for i in range(nd)]

        def block(g, _):
            row0 = pl.multiple_of(g * tq, tq)
            rows = pl.ds(row0, tq)
            qs = [q_ref[hh, rows, :] for hh in range(2)]

            one = jnp.ones((), BF16)

            def step(off, carries, bias):
                off = pl.multiple_of(off, tk)
                vt = v_ref[pl.ds(off, tk), :]
                vh = (jnp.where(lok, vt, one), jnp.where(lok, one, vt))
                out = []
                for hh, (m, acc) in enumerate(carries):
                    s = _mm_nt(qs[hh], k_ref[hh, pl.ds(off, tk), :]) * scale
                    if bias is not None:
                        s = s + bias
                    m_new = jnp.maximum(m, jnp.max(s, axis=-1, keepdims=True))
                    p = jnp.exp(s - m_new)
                    out.append((m_new, jnp.exp(m - m_new) * acc + _mm(p, vh[hh])))
                return tuple(out)

            init = (jnp.full((tq, 1), NEG, F32), jnp.zeros((tq, LANE), F32))
            carries = lax.fori_loop(0, g * nd, lambda i, c: step(i * tk, c, None), (init, init))
            for i in range(nd):
                carries = step(row0 + i * tk, carries, diag[i])
            (ma, acca), (mb, accb) = carries
            la, lb = pltpu.roll(acca, HALF, axis=1), pltpu.roll(accb, HALF, axis=1)
            o_ref[rows, :] = jnp.where(lo, acca / la, accb / lb)
            lse_ref[rows, :] = jnp.where(lo, ma + jnp.log(la), mb + jnp.log(lb))
            return 0

        lax.fori_loop(0, S // tq, block, 0)

    return pl.pallas_call(
        body, name="mla_attn_fwd", grid=(B, NPAIR), in_specs=[heads, heads, pair], out_specs=[pair, pair],
        out_shape=[jax.ShapeDtypeStruct((B * S, MLA_HEADS * MLA_V), F32)] * 2,
        compiler_params=_cp(),
    )(q, k, v)


DIL_UNROLL = 16


def _dil_geometry(gi, S):
    span, d = DIL_PATTERNS[gi]
    L = S // d
    t = _tile(L, 128)
    window = span // d
    back = min(-(-window // t) * t, L - t)
    return d, L, t, window, back


def _dil_specs(gi, S):
    qk = pl.BlockSpec((S, LANE), lambda b, j: (b, NPAIR * gi + j))
    v = pl.BlockSpec((S, LANE), lambda b, j: (b, CB_DV + NPAIR * gi + j))
    pair = pl.BlockSpec((S, LANE), lambda b, j: (b, j))
    return qk, v, pair


def _dil_bias(bias_ref, sl_ref, j, t, kw, back, window):
    row = lax.broadcasted_iota(jnp.int32, (2 * t, kw), 0)
    col = lax.broadcasted_iota(jnp.int32, (2 * t, kw), 1)
    second = row >= t
    slope = jnp.where(second, sl_ref[j, 1], sl_ref[j, 0])
    for n in range(bias_ref.shape[0]):
        dist = jnp.where(second, row - t, row) + n * back - col
        bias_ref[n] = jnp.where((dist >= 0) & (dist <= window), -slope * dist.astype(F32), NEG)


def _stack_heads(x, lo):
    zero = jnp.zeros((), x.dtype)
    return jnp.concatenate([jnp.where(lo, x, zero), jnp.where(lo, zero, x)], axis=0)


def _dil_attn_fwd(gi, slopes, qn, kn, proj, B, S):
    d, L, t, window, back = _dil_geometry(gi, S)
    kw, nq = back + t, L // t
    nbias = 2 if back else 1
    qk, vspec, pair = _dil_specs(gi, S)

    def body(sl_ref, q_ref, k_ref, v_ref, o_ref, lse_ref, qs, ks, vs, os_, ls, bias_ref):
        _to_classes(q_ref, qs, d, L, DIL_HEAD_DIM ** -0.5)
        _to_classes(k_ref, ks, d, L)
        _to_classes(v_ref, vs, d, L)
        _dil_bias(bias_ref, sl_ref, pl.program_id(1), t, kw, back, window)
        lo = _lane_lo((t, LANE))

        def block(g, _):
            qb = g % nq if d > 1 else g
            row0 = pl.multiple_of(g * t, t)
            rows = pl.ds(row0, t)
            early = qb * t < back
            keys = pl.ds(pl.multiple_of(jnp.where(early, row0 - qb * t, row0 - back), t), kw)
            s = _mm_nt(_stack_heads(qs[rows, :], lo), ks[keys, :]) + bias_ref[jnp.where(early, 0, nbias - 1)]
            m = jnp.max(s, axis=-1, keepdims=True)
            p = jnp.exp(s - m)
            l = jnp.sum(p, axis=-1, keepdims=True)
            o2 = _mm(p, vs[keys, :]) / l
            lse2 = m + jnp.log(l)
            os_[rows, :] = jnp.where(lo, o2[:t], o2[t:])
            ls[rows, :] = jnp.where(lo, lse2[:t], lse2[t:])
            return 0

        lax.fori_loop(0, d * nq, block, 0, unroll=DIL_UNROLL if d * nq % DIL_UNROLL == 0 else 1)
        _from_classes(os_, o_ref, d, L)
        _from_classes(ls, lse_ref, d, L)

    return pl.pallas_call(
        body, name=f"dil_attn_fwd_{gi}", grid=(B, NPAIR),
        in_specs=[pl.BlockSpec(memory_space=pltpu.SMEM), qk, qk, vspec], out_specs=[pair, pair],
        out_shape=[jax.ShapeDtypeStruct((B * S, DIL_WIDTH), F32)] * 2,
        scratch_shapes=[pltpu.VMEM((S, LANE), BF16)] * 3 + [pltpu.VMEM((S, LANE), F32)] * 2
                       + [pltpu.VMEM((nbias, 2 * t, kw), F32)],
        compiler_params=_cp(),
    )(slopes, qn, kn, proj)


def _mla_attn_bwd(q, k, v, do, lse, delta, B, S):
    T = B * S
    tq = _tile(S, MLA_TQ)
    tk = _tile(tq, MLA_TK)
    nd = tq // tk
    scale = MLA_QK ** -0.5
    heads, pair = _mla_specs(S)

    def body(q_ref, k_ref, v_ref, do_ref, lse_ref, dl_ref, dq_ref, dk_ref, dv_ref):
        dk_ref[...] = jnp.zeros_like(dk_ref)
        dv_ref[...] = jnp.zeros_like(dv_ref)
        lo = _lane_lo((tq, LANE))
        diag = [_causal_bias(tq, tk, i * tk) for i in range(nd)]

        def block(g, _):
            row0 = pl.multiple_of(g * tq, tq)
            rows = pl.ds(row0, tq)
            per_head = []
            for hh in range(2):
                sel = lo if hh == 0 else jnp.logical_not(lo)
                per_head.append((q_ref[hh, rows, :], jnp.where(sel, do_ref[rows, :], jnp.zeros((), BF16)),
                                 jnp.max(jnp.where(sel, lse_ref[rows, :], NEG), axis=-1, keepdims=True),
                                 jnp.max(jnp.where(sel, dl_ref[rows, :], NEG), axis=-1, keepdims=True)))

            def step(off, dq_accs, bias):
                cols = pl.ds(pl.multiple_of(off, tk), tk)
                vt = v_ref[cols, :]
                out, dv = [], None
                for hh, (qh, doh, lse_h, dl_h) in enumerate(per_head):
                    kh = k_ref[hh, cols, :]
                    s = _mm_nt(qh, kh) * scale
                    if bias is not None:
                        s = s + bias
                    p = jnp.exp(s - lse_h)
                    ds = (p * (_mm_nt(doh, vt) - dl_h)).astype(BF16)
                    dk_ref[hh, cols, :] += _mm_tn(ds, qh) * scale
                    part = _mm_tn(p, doh)
                    dv = part if dv is None else dv + part
                    out.append(dq_accs[hh] + _mm(ds, kh))
                dv_ref[cols, :] += dv
                return tuple(out)

            zero = jnp.zeros((tq, LANE), F32)
            dq_accs = lax.fori_loop(0, g * nd, lambda i, a: step(i * tk, a, None), (zero, zero))
            for i in range(nd):
                dq_accs = step(row0 + i * tk, dq_accs, diag[i])
            for hh in range(2):
                dq_ref[hh, rows, :] = dq_accs[hh] * scale
            return 0

        lax.fori_loop(0, S // tq, block, 0)

    return pl.pallas_call(
        body, name="mla_attn_bwd", grid=(B, NPAIR), in_specs=[heads, heads, pair, pair, pair, pair],
        out_specs=[heads, heads, pair],
        out_shape=[jax.ShapeDtypeStruct((MLA_HEADS, T, LANE), F32), jax.ShapeDtypeStruct((MLA_HEADS, T, LANE), F32),
                   jax.ShapeDtypeStruct((T, MLA_HEADS * MLA_V), F32)],
        compiler_params=_cp(),
    )(q, k, v, do, lse, delta)


def _dil_attn_bwd(gi, slopes, qn, kn, proj, do, lse, delta, through, B, S):
    d, L, t, window, back = _dil_geometry(gi, S)
    kw, nq = back + t, L // t
    nbias = 2 if back else 1
    scale = DIL_HEAD_DIM ** -0.5
    qk, vspec, pair = _dil_specs(gi, S)

    def body(*refs):
        refs = list(refs)
        sl_ref, q_ref, k_ref, v_ref, do_ref, lse_ref, dl_ref = refs[:7]
        dq_ref, dk_ref, dv_ref, qs, ks, vs, dos, lss, dls, dqs, dks, dvs, bias_ref = refs[-13:]
        _to_classes(q_ref, qs, d, L, scale)
        for src, dst in ((k_ref, ks), (v_ref, vs), (do_ref, dos), (lse_ref, lss), (dl_ref, dls)):
            _to_classes(src, dst, d, L)
        _dil_bias(bias_ref, sl_ref, pl.program_id(1), t, kw, back, window)
        dks[...] = jnp.zeros_like(dks)
        dvs[...] = jnp.zeros_like(dvs)
        lo = _lane_lo((t, LANE))

        def stats(ref, rows):
            x = ref[rows, :]
            return jnp.concatenate([jnp.max(jnp.where(lo, x, NEG), axis=-1, keepdims=True),
                                    jnp.max(jnp.where(lo, NEG, x), axis=-1, keepdims=True)], axis=0)

        def block(g, _):
            qb = g % nq if d > 1 else g
            row0 = pl.multiple_of(g * t, t)
            rows = pl.ds(row0, t)
            early = qb * t < back
            keys = pl.ds(pl.multiple_of(jnp.where(early, row0 - qb * t, row0 - back), t), kw)
            q2 = _stack_heads(qs[rows, :], lo)
            do2 = _stack_heads(dos[rows, :], lo)
            kt = ks[keys, :]
            s = _mm_nt(q2, kt) + bias_ref[jnp.where(early, 0, nbias - 1)]
            p = jnp.exp(s - stats(lss, rows))
            ds = (p * (_mm_nt(do2, vs[keys, :]) - stats(dls, rows))).astype(BF16)
            dq2 = _mm(ds, kt) * scale
            dqs[rows, :] = jnp.where(lo, dq2[:t], dq2[t:])
            dks[keys, :] += _mm_tn(ds, q2)
            dvs[keys, :] += _mm_tn(p, do2)
            return 0

        lax.fori_loop(0, d * nq, block, 0, unroll=DIL_UNROLL if d * nq % DIL_UNROLL == 0 else 1)
        for src, dst in ((dqs, dq_ref), (dks, dk_ref), (dvs, dv_ref)):
            _from_classes(src, dst, d, L)

    in_specs = [pl.BlockSpec(memory_space=pltpu.SMEM), qk, qk, vspec, pair, pair, pair]
    args = [slopes, qn, kn, proj, do, lse, delta]
    aliases = {}
    if through is not None:
        aliases = {len(args) + i: i for i in range(3)}
        in_specs = in_specs + [pl.BlockSpec(memory_space=pl.ANY)] * 3
        args = args + list(through)
    return pl.pallas_call(
        body, name=f"dil_attn_bwd_{gi}", grid=(B, NPAIR), in_specs=in_specs, out_specs=[qk, qk, qk],
        out_shape=[jax.ShapeDtypeStruct((B * S, DIL_QK), F32)] * 3,
        scratch_shapes=[pltpu.VMEM((S, LANE), BF16)] * 4 + [pltpu.VMEM((S, LANE), F32)] * 5
                       + [pltpu.VMEM((nbias, 2 * t, kw), F32)],
        input_output_aliases=aliases,
        compiler_params=_cp(),
    )(*args)


def _merge_proj_specs(ts):
    wide = lambda c0, w: pl.BlockSpec((ts, w), lambda i: (i, c0 * LANE // w))
    return [wide(CB_BZ, DIL_WIDTH), wide(CB_CZ, DIL_WIDTH)] + [wide(CB_GATE + 8 * i, D_MODEL) for i in range(3)]


def _merge_common(p_refs, bg_ref, ob_ref, og_refs, lse_refs):
    bz = p_refs[0][...].astype(F32)
    cz = p_refs[1][...].astype(F32)
    gates = [_sigmoid(p_refs[2 + i][...].astype(F32) + bg_ref[:, i * D_MODEL:(i + 1) * D_MODEL]) for i in range(3)]
    ob = ob_ref[...]
    lses = [r[...] for r in lse_refs]
    mx = jnp.maximum(jnp.maximum(lses[0], lses[1]), lses[2])
    es = [jnp.exp(v - mx) for v in lses]
    inv = 1.0 / (es[0] + es[1] + es[2])
    alphas = [e * inv for e in es]
    oc = alphas[0] * og_refs[0][...] + alphas[1] * og_refs[1][...] + alphas[2] * og_refs[2][...]
    return bz, cz, gates, ob, alphas, oc


def _merge_fwd(x, proj, b_gate, ya, ob, ogs, lses, woa, wob, woc, wo):
    T = x.shape[0]
    ts = _tile(T, 256)

    def body(x_ref, p0, p1, p2, p3, p4, bg_ref, ya_ref, ob_ref, og0, og1, og2, l0, l1, l2,
             woa_ref, wob_ref, woc_ref, wo_ref, out_ref):
        bz, cz, gates, obv, alphas, oc = _merge_common((p0, p1, p2, p3, p4), bg_ref, ob_ref, (og0, og1, og2),
                                                       (l0, l1, l2))
        yb = obv * _silu(bz)
        yc = oc * _silu(cz)
        merged = (gates[0] * _mm(ya_ref[...], woa_ref[...]) + gates[1] * _mm(yb, wob_ref[...])
                  + gates[2] * _mm(yc, woc_ref[...]))
        out_ref[...] = x_ref[...] + _mm(merged, wo_ref[...])

    def whole(r, c):
        return pl.BlockSpec((r, c), lambda i: (0, 0))

    tok = lambda w: pl.BlockSpec((ts, w), lambda i: (i, 0))
    return pl.pallas_call(
        body, name="merge_fwd", grid=(T // ts,),
        in_specs=[tok(D_MODEL)] + _merge_proj_specs(ts) + [whole(1, 3 * D_MODEL), tok(CONV_WIDTH)]
                 + [tok(DIL_WIDTH)] * 7 + [whole(CONV_WIDTH, D_MODEL)] * 3 + [whole(D_MODEL, D_MODEL)],
        out_specs=tok(D_MODEL),
        out_shape=jax.ShapeDtypeStruct((T, D_MODEL), F32),
        compiler_params=_cp(),
    )(x, *[proj] * 5, b_gate, ya, ob, *ogs, *lses, woa, wob, woc, wo)


def _merge_bwd(dout, proj, b_gate, ya, ob, ogs, lses, woa, wob, woc, wo):
    T = dout.shape[0]
    ts = _tile(T, 256)
    nt = T // ts

    def body(do_ref, p0, p1, p2, p3, p4, bg_ref, ya_ref, ob_ref, og0, og1, og2, l0, l1, l2,
             woa_ref, wob_ref, woc_ref, wo_ref,
             dp_ref, dya_ref, dob_ref, dlb_ref, dg0, dg1, dg2, dl0, dl1, dl2,
             mg_ref, dpa_ref, dpb_ref, dpc_ref, yb_ref, yc_ref, dbg_ref, st_bz, st_cz, st_gate, sems):
        step = pl.program_id(0)
        slot = step % 2

        def copies_of(s):
            return _put_copies([st_bz, st_cz, st_gate], dp_ref, sems, s % 2, pl.ds(pl.multiple_of(s * ts, ts), ts),
                               [CB_BZ * LANE, CB_CZ * LANE, CB_GATE * LANE])

        @pl.when(step >= 2)
        def _():
            for cp in copies_of(step - 2):
                cp.wait()

        bz, cz, gates, obv, alphas, oc = _merge_common((p0, p1, p2, p3, p4), bg_ref, ob_ref, (og0, og1, og2),
                                                       (l0, l1, l2))
        (sb, dsb), (sc, dsc) = _silu_and_grad(bz), _silu_and_grad(cz)
        yb = obv * sb
        yc = oc * sc
        ps = [_mm(ya_ref[...], woa_ref[...]), _mm(yb, wob_ref[...]), _mm(yc, woc_ref[...])]
        mg_ref[...] = (gates[0] * ps[0] + gates[1] * ps[1] + gates[2] * ps[2]).astype(BF16)
        yb_ref[...] = yb.astype(BF16)
        yc_ref[...] = yc.astype(BF16)
        dm = _mm_nt(do_ref[...], wo_ref[...])
        dps = []
        first = pl.program_id(0) == 0
        for i, dref in enumerate((dpa_ref, dpb_ref, dpc_ref)):
            g = gates[i]
            dpi = (dm * g).astype(BF16)
            dref[...] = dpi
            dps.append(dpi)
            dgp = dm * ps[i] * g * (1.0 - g)
            st_gate[slot, :, i * D_MODEL:(i + 1) * D_MODEL] = dgp.astype(BF16)
            part = jnp.sum(dgp, axis=0, keepdims=True)

            @pl.when(first)
            def _():
                dbg_ref[:, i * D_MODEL:(i + 1) * D_MODEL] = part

            @pl.when(jnp.logical_not(first))
            def _():
                dbg_ref[:, i * D_MODEL:(i + 1) * D_MODEL] += part

        dya_ref[...] = _mm_nt(dps[0], woa_ref[...])
        dyb = _mm_nt(dps[1], wob_ref[...])
        dyc = _mm_nt(dps[2], woc_ref[...])
        st_bz[slot] = (dyb * obv * dsb).astype(BF16)
        st_cz[slot] = (dyc * oc * dsc).astype(BF16)
        for cp in copies_of(step):
            cp.start()
        dob = dyb * sb
        doc = dyc * sc
        dob_ref[...] = dob.astype(BF16)
        for c in range(NPAIR):
            cs = slice(c * LANE, (c + 1) * LANE)
            dlb_ref[:, cs] = _head_bcast_sum(dob[:, cs] * obv[:, cs])
            dd = _head_bcast_sum(doc[:, cs] * oc[:, cs])
            for a, dref, lref in zip(alphas, (dg0, dg1, dg2), (dl0, dl1, dl2)):
                dref[:, cs] = (a[:, cs] * doc[:, cs]).astype(BF16)
                lref[:, cs] = a[:, cs] * dd

        @pl.when(step == nt - 1)
        def _():
            if nt >= 2:
                for cp in copies_of(step - 1):
                    cp.wait()
            for cp in copies_of(step):
                cp.wait()

    def whole(r, c):
        return pl.BlockSpec((r, c), lambda i: (0, 0))

    tok = lambda w: pl.BlockSpec((ts, w), lambda i: (i, 0))
    sd = jax.ShapeDtypeStruct
    W = DIL_WIDTH
    return pl.pallas_call(
        body, name="merge_bwd", grid=(nt,),
        in_specs=[tok(D_MODEL)] + _merge_proj_specs(ts) + [whole(1, 3 * D_MODEL), tok(CONV_WIDTH)] + [tok(W)] * 7
                 + [whole(CONV_WIDTH, D_MODEL)] * 3 + [whole(D_MODEL, D_MODEL)],
        out_specs=[pl.BlockSpec(memory_space=pl.ANY), tok(CONV_WIDTH), tok(W), tok(W)] + [tok(W)] * 6
                  + [tok(D_MODEL)] * 4 + [tok(W), tok(W), whole(1, 3 * D_MODEL)],
        out_shape=[sd((T, PP), BF16), sd((T, CONV_WIDTH), F32), sd((T, W), BF16), sd((T, W), F32)]
                  + [sd((T, W), BF16)] * 3 + [sd((T, W), F32)] * 3
                  + [sd((T, D_MODEL), BF16)] * 4 + [sd((T, W), BF16)] * 2 + [sd((1, 3 * D_MODEL), F32)],
        scratch_shapes=[pltpu.VMEM((2, ts, W), BF16), pltpu.VMEM((2, ts, W), BF16),
                        pltpu.VMEM((2, ts, 3 * D_MODEL), BF16), pltpu.SemaphoreType.DMA((2, 3))],
        compiler_params=_cp(),
    )(dout, *[proj] * 5, b_gate, ya, ob, *ogs, *lses, woa, wob, woc, wo)


def _loss_head(y, target):
    T = y.shape[0]
    ts = _tile(T, 512)

    def body(y_ref, t_ref, d_ref, l_ref):
        e = y_ref[...] - t_ref[...]
        d_ref[...] = e * (1.0 / D_MODEL)
        l_ref[...] = jnp.zeros((1, 8, LANE), F32) + jnp.sum(e * e)

    tok = pl.BlockSpec((ts, D_MODEL), lambda i: (i, 0))
    return pl.pallas_call(
        body, name="loss_head", grid=(T // ts,), in_specs=[tok, tok],
        out_specs=[tok, pl.BlockSpec((1, 8, LANE), lambda i: (i, 0, 0))],
        out_shape=[jax.ShapeDtypeStruct((T, D_MODEL), F32), jax.ShapeDtypeStruct((T // ts, 8, LANE), F32)],
        compiler_params=_cp(),
    )(y, target)


def _my_index():
    return 4 * lax.axis_index("x") + 2 * lax.axis_index("y") + lax.axis_index("c")


def _peers():
    x, y, c = (lax.axis_index(a) for a in AXES)
    out = []
    for kk in range(1, N_DEV):
        px = 1 - x if kk & 4 else x
        py = 1 - y if kk & 2 else y
        pc = 1 - c if kk & 1 else c
        out.append(((px, py, pc), 4 * px + 2 * py + pc))
    return out


N_CHIP = 4


def _chip_places():
    x, y, c = (lax.axis_index(a) for a in AXES)
    return (x, y, c), (x, y, 1 - c), [(1 - x, y, c), (x, 1 - y, c), (1 - x, 1 - y, c)]


def _index_of(pos):
    return 4 * pos[0] + 2 * pos[1] + pos[2]


def _sibling_swap(arrays, name):
    n = len(arrays)

    def body(*refs):
        srcs, outs = refs[:n], refs[n:2 * n]
        send_sems, recv_sems = refs[2 * n:]
        (x, y, c), sibling, _ = _chip_places()
        sends = []
        for a, (src, out) in enumerate(zip(srcs, outs)):
            for q in range(N_CHIP):
                def copy(core, a=a, q=q, src=src, out=out):
                    return pltpu.make_async_remote_copy(
                        src_ref=src.at[2 * q + core], dst_ref=out.at[q], send_sem=send_sems.at[N_CHIP * a + q],
                        recv_sem=recv_sems.at[N_CHIP * a + q], device_id=sibling, device_id_type=pl.DeviceIdType.MESH)
                mine = copy(1 - c)
                mine.start()
                sends.append((mine, copy(c)))
        for mine, arrival in sends:
            arrival.wait_recv()
            mine.wait_send()

    any_space = pl.BlockSpec(memory_space=pl.ANY)
    return pl.pallas_call(
        body, name=name, in_specs=[any_space] * n, out_specs=[any_space] * n,
        out_shape=[jax.ShapeDtypeStruct((N_CHIP,) + a.shape[1:], a.dtype) for a in arrays],
        scratch_shapes=[pltpu.SemaphoreType.DMA((N_CHIP * n,)), pltpu.SemaphoreType.DMA((N_CHIP * n,))],
    )(*arrays)


def _chip_pair_sum(part, got, name):
    R, C = part.shape[1:]
    tr = R
    while tr * C * part.dtype.itemsize > REDUCE_BLOCK_BYTES // 4 and tr % 32 == 0:
        tr //= 2
    c = lax.axis_index("c")

    def body(c_ref, p_ref, g_ref, o_ref):
        del c_ref
        o_ref[...] = (p_ref[...].astype(F32) + g_ref[...].astype(F32)).astype(o_ref.dtype)

    return pl.pallas_call(
        body, name=name, grid_spec=pltpu.PrefetchScalarGridSpec(
            num_scalar_prefetch=1, grid=(N_CHIP, R // tr),
            in_specs=[pl.BlockSpec((None, tr, C), lambda q, i, cr: (2 * q + cr[0], i, 0)),
                      pl.BlockSpec((None, tr, C), lambda q, i, cr: (q, i, 0))],
            out_specs=pl.BlockSpec((None, tr, C), lambda q, i, cr: (q, i, 0))),
        out_shape=jax.ShapeDtypeStruct((N_CHIP, R, C), part.dtype),
        compiler_params=_cp(),
    )(jnp.reshape(c, (1,)).astype(jnp.int32), part, got)


def _peer_count(mode):
    return {"chips": N_CHIP - 1, "near": N_CHIP}.get(mode, N_DEV - 1)


def _remote_copies(srcs, lands, send_sems, recv_sems, mode):
    if mode == "chips":
        (x, y, _), _, others = _chip_places()
        my_slot, peers = 2 * x + y, [(chip, 2 * chip[0] + chip[1]) for chip in others]
    elif mode == "near":
        me, sibling, others = _chip_places()
        my_slot, peers = _index_of(me), [(pos, _index_of(pos)) for pos in [sibling] + others]
    else:
        my_slot, peers = _my_index(), _peers()
    whole = mode in ("gather", "near")
    out = []
    for i, (pos, idx) in enumerate(peers):
        for a, (src, land) in enumerate(zip(srcs, lands)):
            def copy(slot, a=a, src=src, land=land, i=i, pos=pos, idx=idx):
                return pltpu.make_async_remote_copy(
                    src_ref=src if whole else src.at[idx], dst_ref=land.at[slot],
                    send_sem=send_sems.at[a * len(peers) + i], recv_sem=recv_sems.at[a * len(peers) + i],
                    device_id=pos, device_id_type=pl.DeviceIdType.MESH)
            out.append((copy(my_slot), copy(idx)))
    return out


def _exchange_start(arrays, name, mode):
    n = len(arrays)
    hbm = pl.BlockSpec(memory_space=pltpu.HBM)
    sem = pl.BlockSpec(memory_space=pltpu.SEMAPHORE)
    lands = [lax.empty(((N_DEV,) + a.shape) if mode in ("gather", "near") else a.shape, a.dtype) for a in arrays]

    def body(*refs):
        srcs, lands_ = refs[:n], refs[n:2 * n]
        send_sems, recv_sems = refs[2 * n:2 * n + 2]
        for mine, _ in _remote_copies(srcs, lands_, send_sems, recv_sems, mode):
            mine.start()
        refs[-1][...] = jnp.zeros_like(refs[-1])

    sems = pltpu.SemaphoreType.DMA((n * _peer_count(mode),))
    buffers = [pltpu.HBM(a.shape, a.dtype) for a in list(arrays) + lands]
    res = pl.pallas_call(
        body, name=name, in_specs=[hbm] * (2 * n), out_specs=[sem, sem] + [hbm] * (2 * n) + [pl.BlockSpec(memory_space=pltpu.VMEM)],
        out_shape=[sems, sems] + buffers + [jax.ShapeDtypeStruct((8, LANE), F32)],
        input_output_aliases={i: 2 + i for i in range(2 * n)},
        compiler_params=pltpu.CompilerParams(has_side_effects=pltpu.SideEffectType.DATAFLOW_SIDE_EFFECTING),
    )(*[pltpu.with_memory_space_constraint(a, pltpu.HBM) for a in list(arrays) + lands])
    return (res[0], res[1], res[2:2 + n], res[2 + n:2 + 2 * n]), res[-1]


def _exchange_wait(handle, after, name, mode):
    send_sems, recv_sems, srcs, lands = handle
    n = len(srcs)
    after = list(after) if isinstance(after, (list, tuple)) else [after]
    hbm = pl.BlockSpec(memory_space=pltpu.HBM)
    sem = pl.BlockSpec(memory_space=pltpu.SEMAPHORE)

    def body(*refs):
        for mine, arrival in _remote_copies(refs[:n], refs[n:2 * n], refs[2 * n], refs[2 * n + 1], mode):
            mine.wait_send()
            arrival.wait_recv()

    res = pl.pallas_call(
        body, name=name, in_specs=[hbm] * (2 * n) + [sem, sem] + [pl.BlockSpec(memory_space=pl.ANY)] * len(after),
        out_specs=[hbm] * (2 * n), out_shape=[pltpu.HBM(a.shape, a.dtype) for a in list(srcs) + list(lands)],
        input_output_aliases={i: i for i in range(2 * n)},
        compiler_params=pltpu.CompilerParams(has_side_effects=pltpu.SideEffectType.DATAFLOW_SIDE_EFFECTING),
    )(*srcs, *lands, send_sems, recv_sems, *after)
    return res[n:]


def _sibling_forward(lands, name):
    n = len(lands)

    def body(*refs):
        ins, outs, send_sems, recv_sems = refs[:n], refs[n:2 * n], refs[2 * n], refs[2 * n + 1]
        (x, y, c), sibling, others = _chip_places()
        copies = []
        for a, (src, out) in enumerate(zip(ins, outs)):
            for j, chip in enumerate(others):
                def copy(core, a=a, j=j, chip=chip, src=src, out=out):
                    slot = _index_of((chip[0], chip[1], core))
                    return pltpu.make_async_remote_copy(
                        src_ref=src.at[slot], dst_ref=out.at[slot], send_sem=send_sems.at[3 * a + j],
                        recv_sem=recv_sems.at[3 * a + j], device_id=sibling, device_id_type=pl.DeviceIdType.MESH)
                mine = copy(c)
                mine.start()
                copies.append((mine, copy(1 - c)))
        for mine, arrival in copies:
            arrival.wait_recv()
        for mine, arrival in copies:
            mine.wait_send()

    any_space = pl.BlockSpec(memory_space=pl.ANY)
    return pl.pallas_call(
        body, name=name, in_specs=[any_space] * n, out_specs=[any_space] * n,
        out_shape=[jax.ShapeDtypeStruct(a.shape, a.dtype) for a in lands],
        scratch_shapes=[pltpu.SemaphoreType.DMA((3 * n,)), pltpu.SemaphoreType.DMA((3 * n,))],
        input_output_aliases={i: i for i in range(n)},
    )(*lands)


def _own_slot(land, mine, slot=None):
    slot = _my_index() if slot is None else slot
    return lax.dynamic_update_slice(land, mine, (slot,) + (0,) * (land.ndim - 1))


def _adamw(w, g, m, v):
    m = ADAM_B1 * m + (1.0 - ADAM_B1) * g
    v = ADAM_B2 * v + (1.0 - ADAM_B2) * (g * g)
    m_hat = m / (1.0 - ADAM_B1 ** ADAM_STEP)
    v_hat = v / (1.0 - ADAM_B2 ** ADAM_STEP)
    delta = -ADAM_LR * (m_hat / (jnp.sqrt(v_hat) + ADAM_EPS) + ADAM_WD * w)
    return delta, m, v


def _reduce_adamw(parts, w, m, v, name, after=None):
    nparts = len(parts)
    R, C = parts[0].shape[1:]
    tr = R
    while N_DEV * tr * C * parts[0].dtype.itemsize > REDUCE_BLOCK_BYTES and tr % 32 == 0:
        tr //= 2
    steps = R // tr
    extra = [] if after is None else [after]

    def body(*refs):
        w_ref, m_ref, v_ref, g_ref, d_ref, nm_ref, nv_ref = refs[nparts + len(extra):]
        for k, p_ref in enumerate(refs[:nparts]):
            @pl.when(pl.program_id(0) // steps == k)
            def _():
                g = p_ref[0].astype(F32)
                for s in range(1, p_ref.shape[0]):
                    g = g + p_ref[s].astype(F32)
                g_ref[...] = g
                d_ref[...], nm_ref[...], nv_ref[...] = _adamw(w_ref[...], g, m_ref[...], v_ref[...])

    def part_spec(k):
        return pl.BlockSpec((parts[k].shape[0], tr, C), lambda i: (0, jnp.clip(i - k * steps, 0, steps - 1), 0))

    row = pl.BlockSpec((tr, C), lambda i: (i, 0))
    return pl.pallas_call(
        body, name=name, grid=(nparts * steps,),
        in_specs=[part_spec(k) for k in range(nparts)] + [pl.BlockSpec(memory_space=pl.ANY)] * len(extra)
                 + [row, row, row],
        out_specs=[row] * 4, out_shape=[jax.ShapeDtypeStruct((nparts * R, C), F32)] * 4,
        compiler_params=_cp(),
    )(*parts, *extra, w, m, v)


BIG = ("w_in", "w_uq", "w_ukv", "w_out_a", "w_out_b", "w_out_c", "w_o")
SMALL = ("norm_g", "b_gate", "conv_w", "conv_b", "q_a_norm_g", "kv_a_norm_g", "mla_q_norm_g", "mla_k_norm_g",
         "dil_q_norm_g", "dil_k_norm_g")
PACK_ROWS = 128
REDUCE_BLOCK_BYTES = 6 * 1024 * 1024


def _pack_local(tensors):
    flat = jnp.concatenate([t.reshape(-1) for t in tensors])
    pad = (-flat.shape[0]) % (PACK_ROWS * LANE)
    return jnp.concatenate([flat, jnp.zeros((pad,), flat.dtype)]).reshape(-1, LANE)


def _unpack_local(rows, like):
    flat = rows.reshape(-1)
    out, off = [], 0
    for t in like:
        out.append(flat[off:off + t.size].reshape(t.shape))
        off += t.size
    return out


def _cols_to_slots(a):
    k = a.shape[0]
    return a.reshape(k, N_DEV, -1).transpose(1, 0, 2)


def _slots_to_cols(s):
    return s.transpose(1, 0, 2).reshape(s.shape[1], -1)


def _rope_tables(S):
    inv = ROPE_THETA ** (-jnp.arange(0, MLA_ROPE, 2, dtype=F32) / MLA_ROPE)
    ang = jnp.arange(S, dtype=F32)[:, None] * inv[None, :]
    cos, sin = jnp.cos(ang), jnp.sin(ang)
    one = jnp.ones((S, MLA_NOPE), F32)
    z16, z32, z64 = (jnp.zeros((S, n), F32) for n in (16, 32, 64))
    cosp = jnp.concatenate([one, cos, cos, jnp.ones((S, 32), F32)], axis=1)
    sa = jnp.concatenate([z64, -sin, z16, z32], axis=1)
    sb = jnp.concatenate([z64, z16, sin, z32], axis=1)
    return cosp, sa, sb


def _alibi_slopes():
    n = DIL_GROUPS * DIL_HEADS
    m = 2.0 ** (-8.0 * jnp.arange(1, n + 1, dtype=F32) / n)
    return m.reshape(DIL_GROUPS, NPAIR, 2)


def _pad_slots(s):
    n, k, c = s.shape
    return _slots_to_cols(jnp.concatenate([s, jnp.zeros((n, k, LANE - c), s.dtype)], axis=2))


def _layer_params(gw, small, l):
    p = {}
    p["wp"] = _pad_columns(gw["w_in"])
    p["norm_g"] = small["norm_g"][l][None]
    p["b_gate"] = small["b_gate"][l][None]
    p["conv_w"] = gw["conv_w"].transpose(1, 0, 2).reshape(CONV_K, CONV_WIDTH)
    p["conv_b"] = small["conv_b"][l][None]
    p["gq"] = small["q_a_norm_g"][l][None]
    p["gkv"] = small["kv_a_norm_g"][l][None]
    p["wuqp"] = _pad_slots(gw["w_uq"])
    kv = gw["w_ukv"]
    p["wkp"] = _pad_slots(kv[:, :, :MLA_NOPE])
    p["wv"] = kv[:, :, MLA_NOPE:].transpose(1, 0, 2).reshape(MLA_KV_LORA, MLA_HEADS * MLA_V)
    zpad = jnp.zeros((1, LANE - MLA_QK), F32)
    p["gmq"] = jnp.concatenate([small["mla_q_norm_g"][l][None], zpad], axis=1)
    p["gmk"] = jnp.concatenate([small["mla_k_norm_g"][l][None], zpad], axis=1)
    tile = lambda g: jnp.broadcast_to(g[:, None, :], (DIL_GROUPS, DIL_HEADS, DIL_HEAD_DIM)).reshape(1, DIL_QK)
    p["gdq"] = tile(small["dil_q_norm_g"][l])
    p["gdk"] = tile(small["dil_k_norm_g"][l])
    p["woa"], p["wob"], p["woc"] = (_slots_to_cols(gw[n]) for n in ("w_out_a", "w_out_b", "w_out_c"))
    p["wo"] = gw["w_o"].reshape(D_MODEL, D_MODEL)
    return p


def _layer_fwd(x, p, tabs, slopes, B, S):
    proj, ht = _inproj_fwd(x, p["norm_g"], p["wp"])
    ya = _mixa_fwd(proj, p["conv_w"], p["conv_b"], B, S)
    q, k, v = _mla_prep_fwd(proj, p["gq"], p["gkv"], p["wuqp"], p["wkp"], p["wv"], p["gmq"], p["gmk"], *tabs, S)
    ob, lse_b = _mla_attn_fwd(q, k, v, B, S)
    qn, kn = _dil_prep_fwd(proj, p["gdq"], p["gdk"])
    ogs, lses = [], []
    for gi in range(DIL_GROUPS):
        o, lse = _dil_attn_fwd(gi, slopes[gi], qn, kn, proj, B, S)
        ogs.append(o)
        lses.append(lse)
    out = _merge_fwd(x, proj, p["b_gate"], ya, ob, ogs, lses, p["woa"], p["wob"], p["woc"], p["wo"])
    saved = dict(x=x, proj=proj, ht=ht, ya=ya, q=q, k=k, v=v, ob=ob, lse_b=lse_b, qn=qn, kn=kn, ogs=ogs, lses=lses)
    return out, saved


def _layer_bwd(dout, sv, p, tabs, slopes, B, S, big_ready=None):
    proj = sv["proj"]
    (dproj, dya, dob, dlb, dg0, dg1, dg2, dl0, dl1, dl2, merged, dpa, dpb, dpc, yb, yc, dbg) = _merge_bwd(
        dout, proj, p["b_gate"], sv["ya"], sv["ob"], sv["ogs"], sv["lses"], p["woa"], p["wob"], p["woc"], p["wo"])
    g = {}
    g["w_o"] = _matmul_tn(merged, dout, "dw_o").reshape(N_DEV, D_MODEL // N_DEV, D_MODEL)
    g["w_out_a"] = _cols_to_slots(_matmul_tn(sv["ya"], dpa, "dw_out_a"))
    g["w_out_b"] = _cols_to_slots(_matmul_tn(yb, dpb, "dw_out_b"))
    g["w_out_c"] = _cols_to_slots(_matmul_tn(yc, dpc, "dw_out_c"))
    g["b_gate"] = dbg[0]
    dproj, st = _mixa_bwd(dproj, dya, proj, p["conv_w"], p["conv_b"], B, S)
    g["conv_w"] = st[0:CONV_K]
    g["conv_b"] = st[CONV_K]
    dq, dk, dv = _mla_attn_bwd(sv["q"], sv["k"], sv["v"], dob, sv["lse_b"], dlb, B, S)
    dproj, dwuqp, dwkp, dwv, dgq, dgkv, dgmq, dgmk = _mla_prep_bwd(
        dproj, dq, dk, dv, proj, p["gq"], p["gkv"], p["wuqp"], p["wkp"], p["wv"], p["gmq"], p["gmk"], *tabs, S)
    g["w_uq"] = _cols_to_slots(dwuqp)[:, :, :MLA_QK]
    g["w_ukv"] = jnp.concatenate([_cols_to_slots(dwkp)[:, :, :MLA_NOPE], _cols_to_slots(dwv)], axis=2)
    g["q_a_norm_g"], g["kv_a_norm_g"] = dgq[0], dgkv[0]
    g["mla_q_norm_g"], g["mla_k_norm_g"] = dgmq[0, :MLA_QK], dgmk[0, :MLA_QK]
    dqkv = None
    for gi, (dog, dlg) in enumerate(((dg0, dl0), (dg1, dl1), (dg2, dl2))):
        dqkv = _dil_attn_bwd(gi, slopes[gi], sv["qn"], sv["kn"], proj, dog, sv["lses"][gi], dlg, dqkv, B, S)
    dproj, dgdq, dgdk = _dil_prep_bwd(dproj, *dqkv, proj, p["gdq"], p["gdk"])
    g["dil_q_norm_g"] = dgdq.reshape(DIL_GROUPS, DIL_HEADS, DIL_HEAD_DIM).sum(axis=1)
    g["dil_k_norm_g"] = dgdk.reshape(DIL_GROUPS, DIL_HEADS, DIL_HEAD_DIM).sum(axis=1)
    token = None if big_ready is None else big_ready(g)
    g["w_in"] = _unpad_columns(_matmul_nn(sv["ht"], dproj, "dw_in", token))
    token = None if big_ready is None else big_ready(g)
    dx, dng = _inproj_bwd_x(dproj, p["wp"], sv["x"], _after(token, p["norm_g"]), dout)
    g["norm_g"] = dng[0]
    return dx, g


def _after(token, a):
    return a if token is None else a + token[0:1, 0:1]


def _local_step(x, target, small, B, S, weights_of, grads_out, big_ready=None):
    tabs = _rope_tables(S)
    sl = _alibi_slopes()
    slopes = [sl[gi] * float(DIL_PATTERNS[gi][1]) for gi in range(DIL_GROUPS)]
    params, saved = [], []
    for l in range(DEPTH):
        gw, token = weights_of(l, x)
        p = _layer_params(gw, small, l)
        p["norm_g"] = _after(token, p["norm_g"])
        x, sv = _layer_fwd(x, p, tabs, slopes, B, S)
        params.append(p)
        saved.append(sv)
    dout, lparts = _loss_head(x, target)
    sq = jnp.sum(lparts[:, 0, 0])
    token = None
    for l in reversed(range(DEPTH)):
        p = dict(params[l], b_gate=_after(token, params[l]["b_gate"]))
        ready = None if big_ready is None else (lambda g, l=l: big_ready(l, g))
        dout, g = _layer_bwd(dout, saved[l], p, tabs, slopes, B, S, ready)
        token = grads_out(l, g, dout)
    return sq, dout


def kernel(x, norm_g, w_in, b_gate, conv_w, conv_b, q_a_norm_g, w_uq, kv_a_norm_g, w_ukv, mla_q_norm_g, mla_k_norm_g, dil_q_norm_g, dil_k_norm_g, w_out_a, w_out_b, w_out_c, w_o, loss_target, m_norm_g, m_w_in, m_b_gate, m_conv_w, m_conv_b, m_q_a_norm_g, m_w_uq, m_kv_a_norm_g, m_w_ukv, m_mla_q_norm_g, m_mla_k_norm_g, m_dil_q_norm_g, m_dil_k_norm_g, m_w_out_a, m_w_out_b, m_w_out_c, m_w_o, v_norm_g, v_w_in, v_b_gate, v_conv_w, v_conv_b, v_q_a_norm_g, v_w_uq, v_kv_a_norm_g, v_w_ukv, v_mla_q_norm_g, v_mla_k_norm_g, v_dil_q_norm_g, v_dil_k_norm_g, v_w_out_a, v_w_out_b, v_w_out_c, v_w_o):
    names = ("norm_g", "w_in", "b_gate", "conv_w", "conv_b", "q_a_norm_g", "w_uq", "kv_a_norm_g", "w_ukv",
             "mla_q_norm_g", "mla_k_norm_g", "dil_q_norm_g", "dil_k_norm_g", "w_out_a", "w_out_b", "w_out_c", "w_o")
    w = dict(zip(names, (norm_g, w_in, b_gate, conv_w, conv_b, q_a_norm_g, w_uq, kv_a_norm_g, w_ukv, mla_q_norm_g,
                         mla_k_norm_g, dil_q_norm_g, dil_k_norm_g, w_out_a, w_out_b, w_out_c, w_o)))
    m = dict(zip(names, (m_norm_g, m_w_in, m_b_gate, m_conv_w, m_conv_b, m_q_a_norm_g, m_w_uq, m_kv_a_norm_g, m_w_ukv,
                         m_mla_q_norm_g, m_mla_k_norm_g, m_dil_q_norm_g, m_dil_k_norm_g, m_w_out_a, m_w_out_b,
                         m_w_out_c, m_w_o)))
    v = dict(zip(names, (v_norm_g, v_w_in, v_b_gate, v_conv_w, v_conv_b, v_q_a_norm_g, v_w_uq, v_kv_a_norm_g, v_w_ukv,
                         v_mla_q_norm_g, v_mla_k_norm_g, v_dil_q_norm_g, v_dil_k_norm_g, v_w_out_a, v_w_out_b,
                         v_w_out_c, v_w_o)))
    B, S, _ = x.shape
    me = _my_index()
    cshard = CONV_WIDTH // N_DEV

    shards = [[w[n][0].astype(BF16) for n in BIG]]
    state = {}

    def widen(t):
        return lax.dynamic_update_slice(jnp.zeros((DEPTH, CONV_K, CONV_WIDTH), F32), t, (0, 0, me * cshard))

    pick = lambda d: [widen(d[n]) if n == "conv_w" else d[n] for n in SMALL]

    def weights_of(l, after):
        if l == 0:
            first = shards[0] + [conv_w]
            handle, token = _exchange_start(first, "all_gather_weights_0_start", "near")
            zero = token[0:1, 0:1]
            state["shards1"] = [(w[n][1] + zero).astype(BF16) for n in BIG]
            for n in BIG:
                state["rows", n] = [a.reshape(-1, a.shape[-1]) + zero for a in (w[n], m[n], v[n])]
            state["small"] = [_pack_local(pick(d)) + zero for d in (w, m, v)]
            busy = state["shards1"] + [a for n in BIG for a in state["rows", n]] + state["small"]
            landed = _exchange_wait(handle, busy, "all_gather_weights_0_wait", "near")
            landed = _sibling_forward(landed, "all_gather_weights_0_forward")
            got = [_own_slot(a, s[None]) for a, s in zip(landed, first)]
            state["gather"], token = _exchange_start(state["shards1"], "all_gather_weights_1_start", "gather")
            state["conv_w"] = got[-1]
        else:
            landed = _exchange_wait(state["gather"], after, "all_gather_weights_1_wait", "gather")
            got, token = [_own_slot(a, s[None]) for a, s in zip(landed, state["shards1"])], None
        gw = dict(zip(BIG, got))
        gw["conv_w"] = state["conv_w"][:, l]
        return gw, token

    recv, small_parts = {}, {}
    my_chip = 2 * lax.axis_index("x") + lax.axis_index("y")

    REST = tuple(n for n in BIG if n != "w_in")

    def big_ready(l, g):
        if l == DEPTH - 1:
            if "w_in" not in g:
                return None
            send = [g[n].astype(BF16) for n in BIG]
            state["scatter"], token = _exchange_start(send, "exchange_weight_grads_1_start", "scatter")
            state["sent", "scatter"] = send
            return token
        tag, group = ("w_in", ("w_in",)) if "w_in" in g else ("rest", REST)
        send = [g[n].astype(BF16) for n in group]
        swapped = _sibling_swap(send, "exchange_weight_grads_0_sibling_" + tag)
        send = [_chip_pair_sum(s, t, "chip_pair_sum_" + n) for n, s, t in zip(group, send, swapped)]
        state[tag], token = _exchange_start(send, "exchange_weight_grads_0_start_" + tag, "chips")
        state["sent", tag] = send
        return token

    def grads_out(l, g, after):
        small_parts[l] = [g[n] for n in SMALL]
        if l == DEPTH - 1:
            return None
        got = {}
        for key, mode, slot in (("scatter", "scatter", me), ("rest", "chips", my_chip), ("w_in", "chips", my_chip)):
            k = DEPTH - 1 if key == "scatter" else 0
            tag = "" if key == "scatter" else "_" + key
            landed = _exchange_wait(state[key], after, f"exchange_weight_grads_{k}_wait{tag}", mode)
            mine = [lax.dynamic_slice_in_dim(s, slot, 1, axis=0) for s in state["sent", key]]
            got[key] = [_own_slot(a, s, slot) for a, s in zip(landed, mine)]
        recv[DEPTH - 1] = got["scatter"]
        recv[0] = got["w_in"] + got["rest"]
        assert BIG == ("w_in",) + REST
        return None

    sq, grad_x = _local_step(x.reshape(B * S, D_MODEL), loss_target.reshape(B * S, D_MODEL), w, B, S,
                             weights_of, grads_out, big_ready)

    part = {n: jnp.stack([small_parts[l][i] for l in range(DEPTH)]) for i, n in enumerate(SMALL)}
    small_like = [part[n] for n in SMALL] + [sq.reshape(1)]
    pack = _pack_local(small_like)
    assert pack.shape == state["small"][0].shape
    handle, token = _exchange_start([pack], "all_gather_small_grads_start", "gather")

    res, done = {}, []
    for i, n in enumerate(BIG):
        outs = _reduce_adamw([recv[l][i] for l in range(DEPTH)], *state["rows", n], "reduce_adamw_" + n, token)
        res[n] = tuple(a.reshape(w[n].shape) for a in outs)
        done.append(outs[0])

    landed, = _exchange_wait(handle, done, "all_gather_small_grads_wait", "gather")
    parts = _own_slot(landed, pack[None])
    gs, ds, ms, vs = _reduce_adamw([parts], *state["small"], "reduce_adamw_small")
    for n, t in zip(SMALL, zip(*(_unpack_local(a, small_like) for a in (gs, ds, ms, vs)))):
        if n == "conv_w":
            t = tuple(lax.dynamic_slice(a, (0, 0, me * cshard), (DEPTH, CONV_K, cshard)) for a in t)
        res[n] = t
    loss = _unpack_local(gs, small_like)[-1].reshape(()) * (0.5 / D_MODEL)

    out = [loss, grad_x.reshape(B, S, D_MODEL)]
    for i in range(4):
        out += [res[n][i] for n in names]
    return tuple(out)
```

```python
import jax
import jax.numpy as jnp
from jax import lax
from jax.experimental import pallas as pl
from jax.experimental.pallas import tpu as pltpu

F32 = jnp.float32
BF16 = jnp.bfloat16

D_MODEL = 1024
DEPTH = 2
CONV_WIDTH = 512
CONV_K = 3
MLA_HEADS = 8
MLA_Q_LORA = 256
MLA_KV_LORA = 128
MLA_NOPE = 64
MLA_ROPE = 32
MLA_V = 64
MLA_QK = MLA_NOPE + MLA_ROPE
ROPE_THETA = 10000.0
DIL_PATTERNS = ((128, 1), (512, 4), (2048, 16))
DIL_GROUPS = 3
DIL_HEADS = 8
DIL_HEAD_DIM = 64
DIL_WIDTH = DIL_HEADS * DIL_HEAD_DIM
DIL_QK = DIL_GROUPS * DIL_WIDTH
EPS = 1e-6
N_IN = 11168

ADAM_LR = 0.001
ADAM_B1 = 0.9
ADAM_B2 = 0.999
ADAM_EPS = 1e-08
ADAM_WD = 0.01
ADAM_STEP = 10

N_DEV = 8
AXES = ("x", "y", "c")
LANE = 128
HALF = 64
NPAIR = 4

CB_AB, CB_AC, CB_AX, CB_AZ = 0, 4, 8, 12
CB_CQ, CB_CKV, CB_KPE = 16, 18, 19
CB_BZ = 20
CB_DQ, CB_DK, CB_DV = 24, 36, 48
CB_CZ, CB_GATE = 60, 64
NCB = 88
PP = NCB * LANE
KPE_END = CB_KPE * LANE + MLA_ROPE
SHARD_COLS = N_IN // N_DEV
NEG = -1e30
VMEM_LIMIT = 56 * 1024 * 1024


def _pad_columns(shards):
    K = shards.shape[1]
    gap = LANE - MLA_ROPE
    tr = _tile(K, 128)

    def body(s_ref, o_ref):
        for p in range(N_DEV):
            cut = min(max(KPE_END - p * SHARD_COLS, 0), SHARD_COLS)
            for a, b in ((0, cut), (cut, SHARD_COLS)):
                if a < b:
                    to = p * SHARD_COLS + a + (gap if p * SHARD_COLS + a >= KPE_END else 0)
                    o_ref[:, to:to + b - a] = s_ref[p, :, a:b]
        o_ref[:, KPE_END:KPE_END + gap] = jnp.zeros((tr, gap), o_ref.dtype)

    return pl.pallas_call(
        body, name="pad_columns", grid=(K // tr,),
        in_specs=[pl.BlockSpec((N_DEV, tr, SHARD_COLS), lambda i: (0, i, 0))],
        out_specs=pl.BlockSpec((tr, PP), lambda i: (i, 0)),
        out_shape=jax.ShapeDtypeStruct((K, PP), shards.dtype),
        compiler_params=_cp(),
    )(shards)


def _unpad_columns(wp):
    K = wp.shape[0]
    gap = LANE - MLA_ROPE
    tr = _tile(K, 128)

    def body(w_ref, o_ref):
        for p in range(N_DEV):
            cut = min(max(KPE_END - p * SHARD_COLS, 0), SHARD_COLS)
            for a, b in ((0, cut), (cut, SHARD_COLS)):
                if a < b:
                    at = p * SHARD_COLS + a + (gap if p * SHARD_COLS + a >= KPE_END else 0)
                    o_ref[p, :, a:b] = w_ref[:, at:at + b - a]

    return pl.pallas_call(
        body, name="unpad_columns", grid=(K // tr,),
        in_specs=[pl.BlockSpec((tr, PP), lambda i: (i, 0))],
        out_specs=pl.BlockSpec((N_DEV, tr, SHARD_COLS), lambda i: (0, i, 0)),
        out_shape=jax.ShapeDtypeStruct((N_DEV, K, SHARD_COLS), wp.dtype),
        compiler_params=_cp(),
    )(wp)


def _put_copies(stages, dst_ref, sems, slot, rows, cols):
    return [pltpu.make_async_copy(st.at[slot], dst_ref.at[rows, pl.ds(c0, st.shape[-1])], sems.at[slot, k])
            for k, (st, c0) in enumerate(zip(stages, cols))]


def _put_pipeline(step, nsteps, copies_of, fill):
    @pl.when(step >= 2)
    def _():
        for cp in copies_of(step - 2):
            cp.wait()

    fill(step % 2)
    for cp in copies_of(step):
        cp.start()

    @pl.when(step == nsteps - 1)
    def _():
        if nsteps >= 2:
            for cp in copies_of(step - 1):
                cp.wait()
        for cp in copies_of(step):
            cp.wait()


def _cp():
    return pltpu.CompilerParams(vmem_limit_bytes=VMEM_LIMIT)


def _rstd(x, n):
    return lax.rsqrt(jnp.sum(x * x, axis=-1, keepdims=True) * (1.0 / n) + EPS)


def _sigmoid(z):
    return 1.0 / (1.0 + jnp.exp(-z))


def _silu(z):
    return z * _sigmoid(z)


def _silu_and_grad(z):
    s = _sigmoid(z)
    return z * s, s * (1.0 + z * (1.0 - s))


def _mm(a, b):
    return jnp.dot(a.astype(BF16), b.astype(BF16), preferred_element_type=F32)


def _mm_nt(a, b):
    return lax.dot_general(a.astype(BF16), b.astype(BF16), (((1,), (1,)), ((), ())), preferred_element_type=F32)


def _mm_tn(a, b):
    return lax.dot_general(a.astype(BF16), b.astype(BF16), (((0,), (0,)), ((), ())), preferred_element_type=F32)


def _lane_lo(shape):
    return lax.broadcasted_iota(jnp.int32, shape, len(shape) - 1) < HALF


def _head_bcast_sum(x, terms=3):
    w = x.shape[-1]
    same = (lax.broadcasted_iota(jnp.int32, (w, w), 0) // HALF) == (lax.broadcasted_iota(jnp.int32, (w, w), 1) // HALF)
    ones = jnp.where(same, 1.0, 0.0).astype(jnp.bfloat16)
    total = None
    for _ in range(terms):
        term = x.astype(jnp.bfloat16)
        x = x - term.astype(F32)
        part = jnp.dot(term, ones, preferred_element_type=F32)
        total = part if total is None else total + part
    return total


def _rope(t, cos, sa, sb):
    return t * cos + pltpu.roll(t, LANE - 16, axis=1) * sa + pltpu.roll(t, 16, axis=1) * sb


def _rope_t(d, cos, sa, sb):
    return d * cos + pltpu.roll(d * sa, 16, axis=1) + pltpu.roll(d * sb, LANE - 16, axis=1)


def _shift_down(u, k):
    rows = lax.broadcasted_iota(jnp.int32, u.shape, 0)
    return jnp.where(rows >= k, pltpu.roll(u, k, axis=0), 0.0)


def _shift_up(u, k):
    n = u.shape[0]
    rows = lax.broadcasted_iota(jnp.int32, u.shape, 0)
    return jnp.where(rows < n - k, pltpu.roll(u, n - k, axis=0), 0.0)


def _tile(n, want):
    t = min(n, want)
    assert n % t == 0, (n, want)
    return t


def _inproj_fwd(x, g, wp):
    T = x.shape[0]
    tm, tn = _tile(T, 2048), 512

    def body(x_ref, g_ref, w_ref, proj_ref, ht_ref, h_ref):
        @pl.when(pl.program_id(1) == 0)
        def _():
            n = min(tm, 512)
            for r0 in range(0, tm, n):
                xv = x_ref[r0:r0 + n, :]
                h = xv * _rstd(xv, D_MODEL) * g_ref[...]
                h_ref[r0:r0 + n, :] = h.astype(BF16)
                ht_ref[:, r0:r0 + n] = h.T.astype(BF16)

        proj_ref[...] = jnp.dot(h_ref[...], w_ref[...], preferred_element_type=F32).astype(BF16)

    return pl.pallas_call(
        body, name="inproj_fwd", grid=(T // tm, PP // tn),
        in_specs=[pl.BlockSpec((tm, D_MODEL), lambda i, j: (i, 0)),
                  pl.BlockSpec((1, D_MODEL), lambda i, j: (0, 0)),
                  pl.BlockSpec((D_MODEL, tn), lambda i, j: (0, j))],
        out_specs=[pl.BlockSpec((tm, tn), lambda i, j: (i, j)),
                   pl.BlockSpec((D_MODEL, tm), lambda i, j: (0, i))],
        out_shape=[jax.ShapeDtypeStruct((T, PP), BF16), jax.ShapeDtypeStruct((D_MODEL, T), BF16)],
        scratch_shapes=[pltpu.VMEM((tm, D_MODEL), BF16)],
        compiler_params=_cp(),
    )(x, g, wp)


def _matmul_nn(at, b, name, after=None):
    K, T = at.shape
    N = b.shape[1]
    tt, tn = _tile(T, 1024), _tile(N, 2816)
    nk = T // tt
    unread = [] if after is None else [after]

    def body(a_ref, b_ref, *refs):
        o_ref, acc_ref = refs[len(unread):]
        k = pl.program_id(1)

        @pl.when(k == 0)
        def _():
            acc_ref[...] = jnp.zeros_like(acc_ref)

        acc_ref[...] += jnp.dot(a_ref[...], b_ref[...], preferred_element_type=F32)

        @pl.when(k == nk - 1)
        def _():
            o_ref[...] = acc_ref[...].astype(BF16)

    return pl.pallas_call(
        body, name=name, grid=(N // tn, nk),
        in_specs=[pl.BlockSpec((K, tt), lambda j, k: (0, k)),
                  pl.BlockSpec((tt, tn), lambda j, k: (k, j))] + [pl.BlockSpec(memory_space=pl.ANY)] * len(unread),
        out_specs=pl.BlockSpec((K, tn), lambda j, k: (0, j)),
        out_shape=jax.ShapeDtypeStruct((K, N), BF16),
        scratch_shapes=[pltpu.VMEM((K, tn), F32)],
        compiler_params=_cp(),
    )(at, b, *unread)


def _matmul_tn(a, b, name):
    T, K = a.shape
    N = b.shape[1]
    tt, tn = _tile(T, 512), _tile(N, 1024)

    def body(a_ref, b_ref, o_ref):
        @pl.when(pl.program_id(1) == 0)
        def _():
            o_ref[...] = jnp.zeros_like(o_ref)

        o_ref[...] += _mm_tn(a_ref[...], b_ref[...])

    return pl.pallas_call(
        body, name=name, grid=(N // tn, T // tt),
        in_specs=[pl.BlockSpec((tt, K), lambda j, k: (k, 0)),
                  pl.BlockSpec((tt, tn), lambda j, k: (k, j))],
        out_specs=pl.BlockSpec((K, tn), lambda j, k: (0, j)),
        out_shape=jax.ShapeDtypeStruct((K, N), F32),
        compiler_params=_cp(),
    )(a, b)


def _inproj_bwd_x(dproj, wp, x, g, dout):
    T = x.shape[0]
    tm, tk = _tile(T, 1024), 1024
    nk = PP // tk

    def body(dp_ref, w_ref, x_ref, g_ref, do_ref, dx_ref, dg_ref, acc_ref):
        i, k = pl.program_id(0), pl.program_id(1)

        @pl.when(k == 0)
        def _():
            acc_ref[...] = jnp.zeros_like(acc_ref)

        @pl.when((k == 0) & (i == 0))
        def _():
            dg_ref[...] = jnp.zeros_like(dg_ref)

        acc_ref[...] += _mm_nt(dp_ref[...], w_ref[...])

        @pl.when(k == nk - 1)
        def _():
            dh = acc_ref[...]
            xv = x_ref[...]
            r = _rstd(xv, D_MODEL)
            gy = dh * g_ref[...]
            dot = jnp.sum(xv * gy, axis=-1, keepdims=True) * (1.0 / D_MODEL)
            dx_ref[...] = do_ref[...] + r * gy - xv * (r * r * r) * dot
            dg_ref[...] += jnp.sum(dh * xv * r, axis=0, keepdims=True)

    return pl.pallas_call(
        body, name="inproj_bwd_x", grid=(T // tm, nk),
        in_specs=[pl.BlockSpec((tm, tk), lambda i, k: (i, k)),
                  pl.BlockSpec((D_MODEL, tk), lambda i, k: (0, k)),
                  pl.BlockSpec((tm, D_MODEL), lambda i, k: (i, 0)),
                  pl.BlockSpec((1, D_MODEL), lambda i, k: (0, 0)),
                  pl.BlockSpec((tm, D_MODEL), lambda i, k: (i, 0))],
        out_specs=[pl.BlockSpec((tm, D_MODEL), lambda i, k: (i, 0)),
                   pl.BlockSpec((1, D_MODEL), lambda i, k: (0, 0))],
        out_shape=[jax.ShapeDtypeStruct((T, D_MODEL), F32), jax.ShapeDtypeStruct((1, D_MODEL), F32)],
        scratch_shapes=[pltpu.VMEM((tm, D_MODEL), F32)],
        compiler_params=_cp(),
    )(dproj, wp, x, g, dout)


A_SEGS = (CB_AB, CB_AC, CB_AX, CB_AZ)


def _mixa_fwd(proj, cw, cb, B, S):
    nc = CONV_WIDTH // LANE

    def body(ab_ref, ac_ref, ax_ref, az_ref, cw_ref, cb_ref, y_ref):
        ab, ac, ax, az = (r[...].astype(F32) for r in (ab_ref, ac_ref, ax_ref, az_ref))
        u = ac * ax
        conv = cb_ref[...] + cw_ref[0:1, :] * _shift_down(u, 2) + cw_ref[1:2, :] * _shift_down(u, 1) + cw_ref[2:3, :] * u
        y_ref[...] = (ab * conv * _silu(az)).astype(BF16)

    return pl.pallas_call(
        body, name="mixa_fwd", grid=(B, nc),
        in_specs=[pl.BlockSpec((S, LANE), lambda b, j, c0=c0: (b, c0 + j)) for c0 in A_SEGS]
                 + [pl.BlockSpec((CONV_K, LANE), lambda b, j: (0, j)),
                    pl.BlockSpec((1, LANE), lambda b, j: (0, j))],
        out_specs=pl.BlockSpec((S, LANE), lambda b, j: (b, j)),
        out_shape=jax.ShapeDtypeStruct((B * S, CONV_WIDTH), BF16),
        compiler_params=_cp(),
    )(proj, proj, proj, proj, cw, cb)


def _mixa_bwd(dproj, dy, proj, cw, cb, B, S):
    nc = CONV_WIDTH // LANE

    def body(dpin_ref, dy_ref, ab_ref, ac_ref, ax_ref, az_ref, cw_ref, cb_ref, dp_ref, st_ref, stage, sems):
        del dpin_ref
        j, b = pl.program_id(0), pl.program_id(1)
        ab, ac, ax, az = (r[...].astype(F32) for r in (ab_ref, ac_ref, ax_ref, az_ref))
        u = ac * ax
        u1, u2 = _shift_down(u, 1), _shift_down(u, 2)
        w0, w1, w2 = cw_ref[0:1, :], cw_ref[1:2, :], cw_ref[2:3, :]
        conv = cb_ref[...] + w0 * u2 + w1 * u1 + w2 * u
        s, ds_az = _silu_and_grad(az)
        d = dy_ref[...]
        dconv = d * ab * s
        du = w2 * dconv + w1 * _shift_up(dconv, 1) + w0 * _shift_up(dconv, 2)
        grads = (d * conv * s, du * ax, du * ac, d * ab * conv * ds_az)

        def fill(slot):
            for k, v in enumerate(grads):
                stage[slot, k] = v.astype(BF16)

        def copies_of(step):
            sj, sb = step // B, step % B
            return _put_copies([stage.at[:, k] for k in range(4)], dp_ref, sems, step % 2,
                               pl.ds(pl.multiple_of(sb * S, S), S),
                               [pl.multiple_of((c0 + sj) * LANE, LANE) for c0 in A_SEGS])

        _put_pipeline(j * B + b, nc * B, copies_of, fill)
        row = lax.broadcasted_iota(jnp.int32, (8, LANE), 0)
        st = jnp.zeros((8, LANE), F32)
        for r, v in enumerate((dconv * u2, dconv * u1, dconv * u, dconv)):
            st = st + jnp.where(row == r, jnp.sum(v, axis=0, keepdims=True), 0.0)

        @pl.when(pl.program_id(1) == 0)
        def _():
            st_ref[...] = st

        @pl.when(pl.program_id(1) != 0)
        def _():
            st_ref[...] += st

    return pl.pallas_call(
        body, name="mixa_bwd", grid=(nc, B),
        in_specs=[pl.BlockSpec(memory_space=pl.ANY),
                  pl.BlockSpec((S, LANE), lambda j, b: (b, j))]
                 + [pl.BlockSpec((S, LANE), lambda j, b, c0=c0: (b, c0 + j)) for c0 in A_SEGS]
                 + [pl.BlockSpec((CONV_K, LANE), lambda j, b: (0, j)),
                    pl.BlockSpec((1, LANE), lambda j, b: (0, j))],
        out_specs=[pl.BlockSpec(memory_space=pl.ANY),
                   pl.BlockSpec((8, LANE), lambda j, b: (0, j))],
        out_shape=[jax.ShapeDtypeStruct(dproj.shape, BF16), jax.ShapeDtypeStruct((8, CONV_WIDTH), F32)],
        scratch_shapes=[pltpu.VMEM((2, 4, S, LANE), BF16), pltpu.SemaphoreType.DMA((2, 4))],
        input_output_aliases={0: 0},
        compiler_params=_cp(),
    )(dproj, dy, proj, proj, proj, proj, cw, cb)


def _mla_prep_fwd(proj, gq, gkv, wuqp, wkp, wv, gmq, gmk, cos, sa, sb, S):
    T = proj.shape[0]
    ts = _tile(S, 512)
    ns = S // ts
    W = MLA_HEADS * LANE

    def body(p_ref, gq_ref, gkv_ref, wuq_ref, wk_ref, wv_ref, gmq_ref, gmk_ref, cos_ref, sa_ref, sb_ref,
             q_ref, k_ref, v_ref):
        cq = p_ref[:, 0:2 * LANE].astype(F32)
        ckv = p_ref[:, 2 * LANE:3 * LANE].astype(F32)
        kpe = pltpu.roll(p_ref[:, 3 * LANE:4 * LANE].astype(F32), HALF, axis=1)
        cqn = cq * _rstd(cq, MLA_Q_LORA) * gq_ref[...]
        ckn = (ckv * _rstd(ckv, MLA_KV_LORA) * gkv_ref[...]).astype(BF16)
        q0 = _mm(cqn, wuq_ref[...])
        kn = _mm(ckn, wk_ref[...])
        v_ref[...] = _mm(ckn, wv_ref[...]).astype(BF16)
        c, a, b = cos_ref[...], sa_ref[...], sb_ref[...]
        kpe_rot = _rope(kpe * gmk_ref[...], c, a, b)
        for h in range(MLA_HEADS):
            q0h = q0[:, h * LANE:(h + 1) * LANE]
            q_ref[h] = _rope(q0h * _rstd(q0h, MLA_QK) * gmq_ref[...], c, a, b).astype(BF16)
            knh = kn[:, h * LANE:(h + 1) * LANE]
            k_ref[h] = (_rstd(knh + kpe, MLA_QK) * (knh * gmk_ref[...] + kpe_rot)).astype(BF16)

    def whole(r, c):
        return pl.BlockSpec((r, c), lambda i: (0, 0))

    tab = pl.BlockSpec((ts, LANE), lambda i: (i % ns, 0))
    return pl.pallas_call(
        body, name="mla_prep_fwd", grid=(T // ts,),
        in_specs=[pl.BlockSpec((ts, 4 * LANE), lambda i: (i, CB_CQ // 4)),
                  whole(1, MLA_Q_LORA), whole(1, MLA_KV_LORA), whole(MLA_Q_LORA, W), whole(MLA_KV_LORA, W),
                  whole(MLA_KV_LORA, MLA_HEADS * MLA_V), whole(1, LANE), whole(1, LANE), tab, tab, tab],
        out_specs=[pl.BlockSpec((MLA_HEADS, ts, LANE), lambda i: (0, i, 0)),
                   pl.BlockSpec((MLA_HEADS, ts, LANE), lambda i: (0, i, 0)),
                   pl.BlockSpec((ts, MLA_HEADS * MLA_V), lambda i: (i, 0))],
        out_shape=[jax.ShapeDtypeStruct((MLA_HEADS, T, LANE), BF16), jax.ShapeDtypeStruct((MLA_HEADS, T, LANE), BF16),
                   jax.ShapeDtypeStruct((T, MLA_HEADS * MLA_V), BF16)],
        compiler_params=_cp(),
    )(proj, gq, gkv, wuqp, wkp, wv, gmq, gmk, cos, sa, sb)


def _mla_prep_bwd(dproj, dq, dk, dv, proj, gq, gkv, wuqp, wkp, wv, gmq, gmk, cos, sa, sb, S):
    T = proj.shape[0]
    ts = _tile(S, 256)
    ns = S // ts
    W = MLA_HEADS * LANE

    def body(dpin_ref, dq_ref, dk_ref, dv_ref, p_ref, gq_ref, gkv_ref, wuq_ref, wk_ref, wv_ref, gmq_ref, gmk_ref,
             cos_ref, sa_ref, sb_ref,
             dp_ref, dwuq_ref, dwk_ref, dwv_ref, dgq_ref, dgkv_ref, dgmq_ref, dgmk_ref, dq0_ref, dkn_ref):
        del dpin_ref

        @pl.when(pl.program_id(0) == 0)
        def _():
            for r in (dwuq_ref, dwk_ref, dwv_ref, dgq_ref, dgkv_ref, dgmq_ref, dgmk_ref):
                r[...] = jnp.zeros_like(r)

        cq = p_ref[:, 0:2 * LANE].astype(F32)
        ckv = p_ref[:, 2 * LANE:3 * LANE].astype(F32)
        kpe = pltpu.roll(p_ref[:, 3 * LANE:4 * LANE].astype(F32), HALF, axis=1)
        rq = _rstd(cq, MLA_Q_LORA)
        rkv = _rstd(ckv, MLA_KV_LORA)
        gq, gkv, gmq, gmk = gq_ref[...], gkv_ref[...], gmq_ref[...], gmk_ref[...]
        cqn = (cq * rq * gq).astype(BF16)
        ckn = (ckv * rkv * gkv).astype(BF16)
        q0 = _mm(cqn, wuq_ref[...])
        kn = _mm(ckn, wk_ref[...])
        c, a, b = cos_ref[...], sa_ref[...], sb_ref[...]
        lane = lax.broadcasted_iota(jnp.int32, (ts, LANE), 1)
        dgmq = jnp.zeros((1, LANE), F32)
        dgmk = jnp.zeros((1, LANE), F32)
        nope = lane < MLA_NOPE
        kpe_rot = _rope(kpe * gmk, c, a, b)
        dk_sum = jnp.zeros((ts, LANE), F32)
        back = jnp.zeros((ts, 1), F32)
        for h in range(MLA_HEADS):
            q0h = q0[:, h * LANE:(h + 1) * LANE]
            r = _rstd(q0h, MLA_QK)
            d1 = _rope_t(dq_ref[h], c, a, b)
            gy = d1 * gmq
            dq0_ref[:, h * LANE:(h + 1) * LANE] = (
                r * gy - q0h * (r * r * r) * (jnp.sum(q0h * gy, axis=-1, keepdims=True) * (1.0 / MLA_QK))).astype(BF16)
            dgmq = dgmq + jnp.sum(d1 * q0h * r, axis=0, keepdims=True)
            knh = kn[:, h * LANE:(h + 1) * LANE]
            dkh = dk_ref[h]
            r = _rstd(knh + kpe, MLA_QK)
            r3dot = (r * r * r) * (jnp.sum((knh * gmk + kpe_rot) * dkh, axis=-1, keepdims=True) * (1.0 / MLA_QK))
            dkn_ref[:, h * LANE:(h + 1) * LANE] = jnp.where(nope, r * gmk * dkh - knh * r3dot, 0.0).astype(BF16)
            dgmk = dgmk + jnp.sum(jnp.where(nope, dkh * knh * r, 0.0), axis=0, keepdims=True)
            dk_sum = dk_sum + r * dkh
            back = back + r3dot
        rot = jnp.where(nope | (lane >= MLA_QK), 0.0, _rope_t(dk_sum, c, a, b))
        dkpe = gmk * rot - kpe * back
        dgmk = dgmk + jnp.sum(kpe * rot, axis=0, keepdims=True)
        dq0 = dq0_ref[...]
        dkn = dkn_ref[...]
        dvv = dv_ref[...]
        dwuq_ref[...] += _mm_tn(cqn, dq0)
        dwk_ref[...] += _mm_tn(ckn, dkn)
        dwv_ref[...] += _mm_tn(ckn, dvv)
        dgmq_ref[...] += dgmq
        dgmk_ref[...] += dgmk
        dcqn = _mm_nt(dq0, wuq_ref[...])
        gy = dcqn * gq
        dp_ref[:, 0:2 * LANE] = (
            rq * gy - cq * (rq * rq * rq) * (jnp.sum(cq * gy, axis=-1, keepdims=True) * (1.0 / MLA_Q_LORA))).astype(BF16)
        dgq_ref[...] += jnp.sum(dcqn * cq * rq, axis=0, keepdims=True)
        dckn = _mm_nt(dkn, wk_ref[...]) + _mm_nt(dvv, wv_ref[...])
        gy = dckn * gkv
        dp_ref[:, 2 * LANE:3 * LANE] = (
            rkv * gy - ckv * (rkv * rkv * rkv) * (jnp.sum(ckv * gy, axis=-1, keepdims=True) * (1.0 / MLA_KV_LORA))).astype(BF16)
        dgkv_ref[...] += jnp.sum(dckn * ckv * rkv, axis=0, keepdims=True)
        dp_ref[:, 3 * LANE:4 * LANE] = pltpu.roll(dkpe, HALF, axis=1).astype(BF16)

    def whole(r, c):
        return pl.BlockSpec((r, c), lambda i: (0, 0))

    tab = pl.BlockSpec((ts, LANE), lambda i: (i % ns, 0))
    heads = pl.BlockSpec((MLA_HEADS, ts, LANE), lambda i: (0, i, 0))
    return pl.pallas_call(
        body, name="mla_prep_bwd", grid=(T // ts,),
        in_specs=[pl.BlockSpec(memory_space=pl.ANY), heads, heads,
                  pl.BlockSpec((ts, MLA_HEADS * MLA_V), lambda i: (i, 0)),
                  pl.BlockSpec((ts, 4 * LANE), lambda i: (i, CB_CQ // 4)),
                  whole(1, MLA_Q_LORA), whole(1, MLA_KV_LORA), whole(MLA_Q_LORA, W), whole(MLA_KV_LORA, W),
                  whole(MLA_KV_LORA, MLA_HEADS * MLA_V), whole(1, LANE), whole(1, LANE), tab, tab, tab],
        out_specs=[pl.BlockSpec((ts, 4 * LANE), lambda i: (i, CB_CQ // 4)),
                   whole(MLA_Q_LORA, W), whole(MLA_KV_LORA, W), whole(MLA_KV_LORA, MLA_HEADS * MLA_V),
                   whole(1, MLA_Q_LORA), whole(1, MLA_KV_LORA), whole(1, LANE), whole(1, LANE)],
        out_shape=[jax.ShapeDtypeStruct(dproj.shape, BF16),
                   jax.ShapeDtypeStruct((MLA_Q_LORA, W), F32), jax.ShapeDtypeStruct((MLA_KV_LORA, W), F32),
                   jax.ShapeDtypeStruct((MLA_KV_LORA, MLA_HEADS * MLA_V), F32),
                   jax.ShapeDtypeStruct((1, MLA_Q_LORA), F32), jax.ShapeDtypeStruct((1, MLA_KV_LORA), F32),
                   jax.ShapeDtypeStruct((1, LANE), F32), jax.ShapeDtypeStruct((1, LANE), F32)],
        scratch_shapes=[pltpu.VMEM((ts, W), BF16), pltpu.VMEM((ts, W), BF16)],
        input_output_aliases={0: 0},
        compiler_params=_cp(),
    )(dproj, dq, dk, dv, proj, gq, gkv, wuqp, wkp, wv, gmq, gmk, cos, sa, sb)


def _dil_prep_fwd(proj, gq, gk):
    T = proj.shape[0]
    ts = _tile(T, 512)

    def body(pq_ref, pk_ref, gq_ref, gk_ref, q_ref, k_ref):
        for c in range(NPAIR):
            cs = slice(c * LANE, (c + 1) * LANE)
            t = jnp.concatenate([pq_ref[:, cs], pk_ref[:, cs]], axis=1).astype(F32)
            y = t * lax.rsqrt(_head_bcast_sum(t * t, terms=2) * (1.0 / DIL_HEAD_DIM) + EPS)
            q_ref[:, cs] = (y[:, 0:LANE] * gq_ref[:, cs]).astype(BF16)
            k_ref[:, cs] = (y[:, LANE:2 * LANE] * gk_ref[:, cs]).astype(BF16)

    col = pl.BlockSpec((1, DIL_WIDTH), lambda i, g: (0, g))
    out = pl.BlockSpec((ts, DIL_WIDTH), lambda i, g: (i, g))
    seg = lambda c0: pl.BlockSpec((ts, DIL_WIDTH), lambda i, g: (i, c0 // NPAIR + g))
    return pl.pallas_call(
        body, name="dil_prep_fwd", grid=(T // ts, DIL_GROUPS),
        in_specs=[seg(CB_DQ), seg(CB_DK), col, col],
        out_specs=[out, out],
        out_shape=[jax.ShapeDtypeStruct((T, DIL_QK), BF16)] * 2,
        compiler_params=_cp(),
    )(proj, proj, gq, gk)


def _dil_prep_bwd(dproj, ddq, ddk, ddv, proj, gq, gk):
    T = proj.shape[0]
    ts = _tile(T, 512)
    nt = T // ts

    def body(dpin_ref, ddq_ref, ddk_ref, ddv_ref, pq_ref, pk_ref, gq_ref, gk_ref, dp_ref, dgq_ref, dgk_ref,
             stage, sems):
        del dpin_ref
        g, i = pl.program_id(0), pl.program_id(1)

        @pl.when(i == 0)
        def _():
            dgq_ref[...] = jnp.zeros_like(dgq_ref)
            dgk_ref[...] = jnp.zeros_like(dgk_ref)

        def fill(slot):
            stage[slot, 2] = ddv_ref[...].astype(BF16)
            for c in range(NPAIR):
                cs = slice(c * LANE, (c + 1) * LANE)
                t = jnp.concatenate([pq_ref[:, cs], pk_ref[:, cs]], axis=1).astype(F32)
                d = jnp.concatenate([ddq_ref[:, cs], ddk_ref[:, cs]], axis=1)
                gy = d * jnp.concatenate([gq_ref[:, cs], gk_ref[:, cs]], axis=1)
                r = lax.rsqrt(_head_bcast_sum(t * t, terms=2) * (1.0 / DIL_HEAD_DIM) + EPS)
                dot = _head_bcast_sum(t * gy, terms=2) * (1.0 / DIL_HEAD_DIM)
                dx = (r * gy - t * (r * r * r) * dot).astype(BF16)
                stage[slot, 0, :, cs] = dx[:, 0:LANE]
                stage[slot, 1, :, cs] = dx[:, LANE:2 * LANE]
                part = jnp.sum(d * t * r, axis=0, keepdims=True)
                dgq_ref[:, cs] += part[:, 0:LANE]
                dgk_ref[:, cs] += part[:, LANE:2 * LANE]

        def copies_of(step):
            sg, si = step // nt, step % nt
            return _put_copies([stage.at[:, k] for k in range(3)], dp_ref, sems, step % 2,
                               pl.ds(pl.multiple_of(si * ts, ts), ts),
                               [pl.multiple_of((c0 + NPAIR * sg) * LANE, LANE) for c0 in (CB_DQ, CB_DK, CB_DV)])

        _put_pipeline(g * nt + i, DIL_GROUPS * nt, copies_of, fill)

    col = pl.BlockSpec((1, DIL_WIDTH), lambda g, i: (0, g))
    tok = pl.BlockSpec((ts, DIL_WIDTH), lambda g, i: (i, g))
    seg = lambda c0: pl.BlockSpec((ts, DIL_WIDTH), lambda g, i: (i, c0 // NPAIR + g))
    return pl.pallas_call(
        body, name="dil_prep_bwd", grid=(DIL_GROUPS, nt),
        in_specs=[pl.BlockSpec(memory_space=pl.ANY), tok, tok, tok, seg(CB_DQ), seg(CB_DK), col, col],
        out_specs=[pl.BlockSpec(memory_space=pl.ANY), col, col],
        out_shape=[jax.ShapeDtypeStruct(dproj.shape, BF16), jax.ShapeDtypeStruct((1, DIL_QK), F32),
                   jax.ShapeDtypeStruct((1, DIL_QK), F32)],
        scratch_shapes=[pltpu.VMEM((2, 3, ts, DIL_WIDTH), BF16), pltpu.SemaphoreType.DMA((2, 3))],
        input_output_aliases={0: 0},
        compiler_params=_cp(),
    )(dproj, ddq, ddk, ddv, proj, proj, gq, gk)


COPY_ROWS = 256


def _to_classes(src_ref, dst_ref, d, L, scale=None):
    m = min(L, max(8, COPY_ROWS // d))
    for c0 in range(0, L, m):
        x = src_ref[c0 * d:(c0 + m) * d, :].astype(F32)
        if scale is not None:
            x = x * scale
        if d > 1:
            x = jnp.swapaxes(x.reshape(m, d, LANE), 0, 1)
        for r in range(d):
            dst_ref[r * L + c0:r * L + c0 + m, :] = (x[r] if d > 1 else x).astype(dst_ref.dtype)


def _from_classes(src_ref, dst_ref, d, L):
    n = min(L, COPY_ROWS)
    for r in range(d):
        for c0 in range(0, L, n):
            rows = pl.ds(r + c0 * d, n, stride=d) if d > 1 else pl.ds(c0, n)
            dst_ref[rows, :] = src_ref[r * L + c0:r * L + c0 + n, :].astype(dst_ref.dtype)


MLA_TQ, MLA_TK = 512, 512


def _causal_bias(tq, tk, shift):
    row = lax.broadcasted_iota(jnp.int32, (tq, tk), 0)
    col = lax.broadcasted_iota(jnp.int32, (tq, tk), 1)
    return jnp.where(row >= col + shift, 0.0, NEG)


def _mla_specs(S):
    heads = pl.BlockSpec((2, S, LANE), lambda b, j: (j, b, 0))
    pair = pl.BlockSpec((S, LANE), lambda b, j: (b, j))
    return heads, pair


def _mla_attn_fwd(q, k, v, B, S):
    tq = _tile(S, MLA_TQ)
    tk = _tile(tq, MLA_TK)
    nd = tq // tk
    scale = MLA_QK ** -0.5
    heads, pair = _mla_specs(S)

    def body(q_ref, k_ref, v_ref, o_ref, lse_ref):
        lo, lok = _lane_lo((tq, LANE)), _lane_lo((tk, LANE))
        diag = [_causal_bias(tq, tk, i * tk) for i in range(nd)]

        def block(g, _):
            row0 = pl.multiple_of(g * tq, tq)
            rows = pl.ds(row0, tq)
            qs = [q_ref[hh, rows, :] for hh in range(2)]

            one = jnp.ones((), BF16)

            def step(off, carries, bias):
                off = pl.multiple_of(off, tk)
                vt = v_ref[pl.ds(off, tk), :]
                vh = (jnp.where(lok, vt, one), jnp.where(lok, one, vt))
                out = []
                for hh, (m, acc) in enumerate(carries):
                    s = _mm_nt(qs[hh], k_ref[hh, pl.ds(off, tk), :]) * scale
                    if bias is not None:
                        s = s + bias
                    m_new = jnp.maximum(m, jnp.max(s, axis=-1, keepdims=True))
                    p = jnp.exp(s - m_new)
                    out.append((m_new, jnp.exp(m - m_new) * acc + _mm(p, vh[hh])))
                return tuple(out)

            init = (jnp.full((tq, 1), NEG, F32), jnp.zeros((tq, LANE), F32))
            carries = lax.fori_loop(0, g * nd, lambda i, c: step(i * tk, c, None), (init, init))
            for i in range(nd):
                carries = step(row0 + i * tk, carries, diag[i])
            (ma, acca), (mb, accb) = carries
            la, lb = pltpu.roll(acca, HALF, axis=1), pltpu.roll(accb, HALF, axis=1)
            o_ref[rows, :] = jnp.where(lo, acca / la, accb / lb)
            lse_ref[rows, :] = jnp.where(lo, ma + jnp.log(la), mb + jnp.log(lb))
            return 0

        lax.fori_loop(0, S // tq, block, 0)

    return pl.pallas_call(
        body, name="mla_attn_fwd", grid=(B, NPAIR), in_specs=[heads, heads, pair], out_specs=[pair, pair],
        out_shape=[jax.ShapeDtypeStruct((B * S, MLA_HEADS * MLA_V), F32)] * 2,
        compiler_params=_cp(),
    )(q, k, v)


DIL_UNROLL = 16


def _dil_geometry(gi, S):
    span, d = DIL_PATTERNS[gi]
    L = S // d
    t = _tile(L, 128)
    window = span // d
    back = min(-(-window // t) * t, L - t)
    return d, L, t, window, back


def _dil_specs(gi, S):
    qk = pl.BlockSpec((S, LANE), lambda b, j: (b, NPAIR * gi + j))
    v = pl.BlockSpec((S, LANE), lambda b, j: (b, CB_DV + NPAIR * gi + j))
    pair = pl.BlockSpec((S, LANE), lambda b, j: (b, j))
    return qk, v, pair


def _dil_bias(bias_ref, sl_ref, j, t, kw, back, window):
    row = lax.broadcasted_iota(jnp.int32, (2 * t, kw), 0)
    col = lax.broadcasted_iota(jnp.int32, (2 * t, kw), 1)
    second = row >= t
    slope = jnp.where(second, sl_ref[j, 1], sl_ref[j, 0])
    for n in range(bias_ref.shape[0]):
        dist = jnp.where(second, row - t, row) + n * back - col
        bias_ref[n] = jnp.where((dist >= 0) & (dist <= window), -slope * dist.astype(F32), NEG)


def _stack_heads(x, lo):
    zero = jnp.zeros((), x.dtype)
    return jnp.concatenate([jnp.where(lo, x, zero), jnp.where(lo, zero, x)], axis=0)


def _dil_attn_fwd(gi, slopes, qn, kn, proj, B, S):
    d, L, t, window, back = _dil_geometry(gi, S)
    kw, nq = back + t, L // t
    nbias = 2 if back else 1
    qk, vspec, pair = _dil_specs(gi, S)

    def body(sl_ref, q_ref, k_ref, v_ref, o_ref, lse_ref, qs, ks, vs, os_, ls, bias_ref):
        _to_classes(q_ref, qs, d, L, DIL_HEAD_DIM ** -0.5)
        _to_classes(k_ref, ks, d, L)
        _to_classes(v_ref, vs, d, L)
        _dil_bias(bias_ref, sl_ref, pl.program_id(1), t, kw, back, window)
        lo = _lane_lo((t, LANE))

        def block(g, _):
            qb = g % nq if d > 1 else g
            row0 = pl.multiple_of(g * t, t)
            rows = pl.ds(row0, t)
            early = qb * t < back
            keys = pl.ds(pl.multiple_of(jnp.where(early, row0 - qb * t, row0 - back), t), kw)
            s = _mm_nt(_stack_heads(qs[rows, :], lo), ks[keys, :]) + bias_ref[jnp.where(early, 0, nbias - 1)]
            m = jnp.max(s, axis=-1, keepdims=True)
            p = jnp.exp(s - m)
            l = jnp.sum(p, axis=-1, keepdims=True)
            o2 = _mm(p, vs[keys, :]) / l
            lse2 = m + jnp.log(l)
            os_[rows, :] = jnp.where(lo, o2[:t], o2[t:])
            ls[rows, :] = jnp.where(lo, lse2[:t], lse2[t:])
            return 0

        lax.fori_loop(0, d * nq, block, 0, unroll=DIL_UNROLL if d * nq % DIL_UNROLL == 0 else 1)
        _from_classes(os_, o_ref, d, L)
        _from_classes(ls, lse_ref, d, L)

    return pl.pallas_call(
        body, name=f"dil_attn_fwd_{gi}", grid=(B, NPAIR),
        in_specs=[pl.BlockSpec(memory_space=pltpu.SMEM), qk, qk, vspec], out_specs=[pair, pair],
        out_shape=[jax.ShapeDtypeStruct((B * S, DIL_WIDTH), F32)] * 2,
        scratch_shapes=[pltpu.VMEM((S, LANE), BF16)] * 3 + [pltpu.VMEM((S, LANE), F32)] * 2
                       + [pltpu.VMEM((nbias, 2 * t, kw), F32)],
        compiler_params=_cp(),
    )(slopes, qn, kn, proj)


def _mla_attn_bwd(q, k, v, do, lse, delta, B, S):
    T = B * S
    tq = _tile(S, MLA_TQ)
    tk = _tile(tq, MLA_TK)
    nd = tq // tk
    scale = MLA_QK ** -0.5
    heads, pair = _mla_specs(S)

    def body(q_ref, k_ref, v_ref, do_ref, lse_ref, dl_ref, dq_ref, dk_ref, dv_ref):
        dk_ref[...] = jnp.zeros_like(dk_ref)
        dv_ref[...] = jnp.zeros_like(dv_ref)
        lo = _lane_lo((tq, LANE))
        diag = [_causal_bias(tq, tk, i * tk) for i in range(nd)]

        def block(g, _):
            row0 = pl.multiple_of(g * tq, tq)
            rows = pl.ds(row0, tq)
            per_head = []
            for hh in range(2):
                sel = lo if hh == 0 else jnp.logical_not(lo)
                per_head.append((q_ref[hh, rows, :], jnp.where(sel, do_ref[rows, :], jnp.zeros((), BF16)),
                                 jnp.max(jnp.where(sel, lse_ref[rows, :], NEG), axis=-1, keepdims=True),
                                 jnp.max(jnp.where(sel, dl_ref[rows, :], NEG), axis=-1, keepdims=True)))

            def step(off, dq_accs, bias):
                cols = pl.ds(pl.multiple_of(off, tk), tk)
                vt = v_ref[cols, :]
                out, dv = [], None
                for hh, (qh, doh, lse_h, dl_h) in enumerate(per_head):
                    kh = k_ref[hh, cols, :]
                    s = _mm_nt(qh, kh) * scale
                    if bias is not None:
                        s = s + bias
                    p = jnp.exp(s - lse_h)
                    ds = (p * (_mm_nt(doh, vt) - dl_h)).astype(BF16)
                    dk_ref[hh, cols, :] += _mm_tn(ds, qh) * scale
                    part = _mm_tn(p, doh)
                    dv = part if dv is None else dv + part
                    out.append(dq_accs[hh] + _mm(ds, kh))
                dv_ref[cols, :] += dv
                return tuple(out)

            zero = jnp.zeros((tq, LANE), F32)
            dq_accs = lax.fori_loop(0, g * nd, lambda i, a: step(i * tk, a, None), (zero, zero))
            for i in range(nd):
                dq_accs = step(row0 + i * tk, dq_accs, diag[i])
            for hh in range(2):
                dq_ref[hh, rows, :] = dq_accs[hh] * scale
            return 0

        lax.fori_loop(0, S // tq, block, 0)

    return pl.pallas_call(
        body, name="mla_attn_bwd", grid=(B, NPAIR), in_specs=[heads, heads, pair, pair, pair, pair],
        out_specs=[heads, heads, pair],
        out_shape=[jax.ShapeDtypeStruct((MLA_HEADS, T, LANE), F32), jax.ShapeDtypeStruct((MLA_HEADS, T, LANE), F32),
                   jax.ShapeDtypeStruct((T, MLA_HEADS * MLA_V), F32)],
        compiler_params=_cp(),
    )(q, k, v, do, lse, delta)


def _dil_attn_bwd(gi, slopes, qn, kn, proj, do, lse, delta, through, B, S):
    d, L, t, window, back = _dil_geometry(gi, S)
    kw, nq = back + t, L // t
    nbias = 2 if back else 1
    scale = DIL_HEAD_DIM ** -0.5
    qk, vspec, pair = _dil_specs(gi, S)

    def body(*refs):
        refs = list(refs)
        sl_ref, q_ref, k_ref, v_ref, do_ref, lse_ref, dl_ref = refs[:7]
        dq_ref, dk_ref, dv_ref, qs, ks, vs, dos, lss, dls, dqs, dks, dvs, bias_ref = refs[-13:]
        _to_classes(q_ref, qs, d, L, scale)
        for src, dst in ((k_ref, ks), (v_ref, vs), (do_ref, dos), (lse_ref, lss), (dl_ref, dls)):
            _to_classes(src, dst, d, L)
        _dil_bias(bias_ref, sl_ref, pl.program_id(1), t, kw, back, window)
        dks[...] = jnp.zeros_like(dks)
        dvs[...] = jnp.zeros_like(dvs)
        lo = _lane_lo((t, LANE))

        def stats(ref, rows):
            x = ref[rows, :]
            return jnp.concatenate([jnp.max(jnp.where(lo, x, NEG), axis=-1, keepdims=True),
                                    jnp.max(jnp.where(lo, NEG, x), axis=-1, keepdims=True)], axis=0)

        def block(g, _):
            qb = g % nq if d > 1 else g
            row0 = pl.multiple_of(g * t, t)
            rows = pl.ds(row0, t)
            early = qb * t < back
            keys = pl.ds(pl.multiple_of(jnp.where(early, row0 - qb * t, row0 - back), t), kw)
            q2 = _stack_heads(qs[rows, :], lo)
            do2 = _stack_heads(dos[rows, :], lo)
            kt = ks[keys, :]
            s = _mm_nt(q2, kt) + bias_ref[jnp.where(early, 0, nbias - 1)]
            p = jnp.exp(s - stats(lss, rows))
            ds = (p * (_mm_nt(do2, vs[keys, :]) - stats(dls, rows))).astype(BF16)
            dq2 = _mm(ds, kt) * scale
            dqs[rows, :] = jnp.where(lo, dq2[:t], dq2[t:])
            dks[keys, :] += _mm_tn(ds, q2)
            dvs[keys, :] += _mm_tn(p, do2)
            return 0

        lax.fori_loop(0, d * nq, block, 0, unroll=DIL_UNROLL if d * nq % DIL_UNROLL == 0 else 1)
        for src, dst in ((dqs, dq_ref), (dks, dk_ref), (dvs, dv_ref)):
            _from_classes(src, dst, d, L)

    in_specs = [pl.BlockSpec(memory_space=pltpu.SMEM), qk, qk, vspec, pair, pair, pair]
    args = [slopes, qn, kn, proj, do, lse, delta]
    aliases = {}
    if through is not None:
        aliases = {len(args) + i: i for i in range(3)}
        in_specs = in_specs + [pl.BlockSpec(memory_space=pl.ANY)] * 3
        args = args + list(through)
    return pl.pallas_call(
        body, name=f"dil_attn_bwd_{gi}", grid=(B, NPAIR), in_specs=in_specs, out_specs=[qk, qk, qk],
        out_shape=[jax.ShapeDtypeStruct((B * S, DIL_QK), F32)] * 3,
        scratch_shapes=[pltpu.VMEM((S, LANE), BF16)] * 4 + [pltpu.VMEM((S, LANE), F32)] * 5
                       + [pltpu.VMEM((nbias, 2 * t, kw), F32)],
        input_output_aliases=aliases,
        compiler_params=_cp(),
    )(*args)


def _merge_proj_specs(ts):
    wide = lambda c0, w: pl.BlockSpec((ts, w), lambda i: (i, c0 * LANE // w))
    return [wide(CB_BZ, DIL_WIDTH), wide(CB_CZ, DIL_WIDTH)] + [wide(CB_GATE + 8 * i, D_MODEL) for i in range(3)]


def _merge_common(p_refs, bg_ref, ob_ref, og_refs, lse_refs):
    bz = p_refs[0][...].astype(F32)
    cz = p_refs[1][...].astype(F32)
    gates = [_sigmoid(p_refs[2 + i][...].astype(F32) + bg_ref[:, i * D_MODEL:(i + 1) * D_MODEL]) for i in range(3)]
    ob = ob_ref[...]
    lses = [r[...] for r in lse_refs]
    mx = jnp.maximum(jnp.maximum(lses[0], lses[1]), lses[2])
    es = [jnp.exp(v - mx) for v in lses]
    inv = 1.0 / (es[0] + es[1] + es[2])
    alphas = [e * inv for e in es]
    oc = alphas[0] * og_refs[0][...] + alphas[1] * og_refs[1][...] + alphas[2] * og_refs[2][...]
    return bz, cz, gates, ob, alphas, oc


def _merge_fwd(x, proj, b_gate, ya, ob, ogs, lses, woa, wob, woc, wo):
    T = x.shape[0]
    ts = _tile(T, 256)

    def body(x_ref, p0, p1, p2, p3, p4, bg_ref, ya_ref, ob_ref, og0, og1, og2, l0, l1, l2,
             woa_ref, wob_ref, woc_ref, wo_ref, out_ref):
        bz, cz, gates, obv, alphas, oc = _merge_common((p0, p1, p2, p3, p4), bg_ref, ob_ref, (og0, og1, og2),
                                                       (l0, l1, l2))
        yb = obv * _silu(bz)
        yc = oc * _silu(cz)
        merged = (gates[0] * _mm(ya_ref[...], woa_ref[...]) + gates[1] * _mm(yb, wob_ref[...])
                  + gates[2] * _mm(yc, woc_ref[...]))
        out_ref[...] = x_ref[...] + _mm(merged, wo_ref[...])

    def whole(r, c):
        return pl.BlockSpec((r, c), lambda i: (0, 0))

    tok = lambda w: pl.BlockSpec((ts, w), lambda i: (i, 0))
    return pl.pallas_call(
        body, name="merge_fwd", grid=(T // ts,),
        in_specs=[tok(D_MODEL)] + _merge_proj_specs(ts) + [whole(1, 3 * D_MODEL), tok(CONV_WIDTH)]
                 + [tok(DIL_WIDTH)] * 7 + [whole(CONV_WIDTH, D_MODEL)] * 3 + [whole(D_MODEL, D_MODEL)],
        out_specs=tok(D_MODEL),
        out_shape=jax.ShapeDtypeStruct((T, D_MODEL), F32),
        compiler_params=_cp(),
    )(x, *[proj] * 5, b_gate, ya, ob, *ogs, *lses, woa, wob, woc, wo)


def _merge_bwd(dout, proj, b_gate, ya, ob, ogs, lses, woa, wob, woc, wo):
    T = dout.shape[0]
    ts = _tile(T, 256)
    nt = T // ts

    def body(do_ref, p0, p1, p2, p3, p4, bg_ref, ya_ref, ob_ref, og0, og1, og2, l0, l1, l2,
             woa_ref, wob_ref, woc_ref, wo_ref,
             dp_ref, dya_ref, dob_ref, dlb_ref, dg0, dg1, dg2, dl0, dl1, dl2,
             mg_ref, dpa_ref, dpb_ref, dpc_ref, yb_ref, yc_ref, dbg_ref, st_bz, st_cz, st_gate, sems):
        step = pl.program_id(0)
        slot = step % 2

        def copies_of(s):
            return _put_copies([st_bz, st_cz, st_gate], dp_ref, sems, s % 2, pl.ds(pl.multiple_of(s * ts, ts), ts),
                               [CB_BZ * LANE, CB_CZ * LANE, CB_GATE * LANE])

        @pl.when(step >= 2)
        def _():
            for cp in copies_of(step - 2):
                cp.wait()

        bz, cz, gates, obv, alphas, oc = _merge_common((p0, p1, p2, p3, p4), bg_ref, ob_ref, (og0, og1, og2),
                                                       (l0, l1, l2))
        (sb, dsb), (sc, dsc) = _silu_and_grad(bz), _silu_and_grad(cz)
        yb = obv * sb
        yc = oc * sc
        ps = [_mm(ya_ref[...], woa_ref[...]), _mm(yb, wob_ref[...]), _mm(yc, woc_ref[...])]
        mg_ref[...] = (gates[0] * ps[0] + gates[1] * ps[1] + gates[2] * ps[2]).astype(BF16)
        yb_ref[...] = yb.astype(BF16)
        yc_ref[...] = yc.astype(BF16)
        dm = _mm_nt(do_ref[...], wo_ref[...])
        dps = []
        first = pl.program_id(0) == 0
        for i, dref in enumerate((dpa_ref, dpb_ref, dpc_ref)):
            g = gates[i]
            dpi = (dm * g).astype(BF16)
            dref[...] = dpi
            dps.append(dpi)
            dgp = dm * ps[i] * g * (1.0 - g)
            st_gate[slot, :, i * D_MODEL:(i + 1) * D_MODEL] = dgp.astype(BF16)
            part = jnp.sum(dgp, axis=0, keepdims=True)

            @pl.when(first)
            def _():
                dbg_ref[:, i * D_MODEL:(i + 1) * D_MODEL] = part

            @pl.when(jnp.logical_not(first))
            def _():
                dbg_ref[:, i * D_MODEL:(i + 1) * D_MODEL] += part

        dya_ref[...] = _mm_nt(dps[0], woa_ref[...])
        dyb = _mm_nt(dps[1], wob_ref[...])
        dyc = _mm_nt(dps[2], woc_ref[...])
        st_bz[slot] = (dyb * obv * dsb).astype(BF16)
        st_cz[slot] = (dyc * oc * dsc).astype(BF16)
        for cp in copies_of(step):
            cp.start()
        dob = dyb * sb
        doc = dyc * sc
        dob_ref[...] = dob.astype(BF16)
        for c in range(NPAIR):
            cs = slice(c * LANE, (c + 1) * LANE)
            dlb_ref[:, cs] = _head_bcast_sum(dob[:, cs] * obv[:, cs])
            dd = _head_bcast_sum(doc[:, cs] * oc[:, cs])
            for a, dref, lref in zip(alphas, (dg0, dg1, dg2), (dl0, dl1, dl2)):
                dref[:, cs] = (a[:, cs] * doc[:, cs]).astype(BF16)
                lref[:, cs] = a[:, cs] * dd

        @pl.when(step == nt - 1)
        def _():
            if nt >= 2:
                for cp in copies_of(step - 1):
                    cp.wait()
            for cp in copies_of(step):
                cp.wait()

    def whole(r, c):
        return pl.BlockSpec((r, c), lambda i: (0, 0))

    tok = lambda w: pl.BlockSpec((ts, w), lambda i: (i, 0))
    sd = jax.ShapeDtypeStruct
    W = DIL_WIDTH
    return pl.pallas_call(
        body, name="merge_bwd", grid=(nt,),
        in_specs=[tok(D_MODEL)] + _merge_proj_specs(ts) + [whole(1, 3 * D_MODEL), tok(CONV_WIDTH)] + [tok(W)] * 7
                 + [whole(CONV_WIDTH, D_MODEL)] * 3 + [whole(D_MODEL, D_MODEL)],
        out_specs=[pl.BlockSpec(memory_space=pl.ANY), tok(CONV_WIDTH), tok(W), tok(W)] + [tok(W)] * 6
                  + [tok(D_MODEL)] * 4 + [tok(W), tok(W), whole(1, 3 * D_MODEL)],
        out_shape=[sd((T, PP), BF16), sd((T, CONV_WIDTH), F32), sd((T, W), BF16), sd((T, W), F32)]
                  + [sd((T, W), BF16)] * 3 + [sd((T, W), F32)] * 3
                  + [sd((T, D_MODEL), BF16)] * 4 + [sd((T, W), BF16)] * 2 + [sd((1, 3 * D_MODEL), F32)],
        scratch_shapes=[pltpu.VMEM((2, ts, W), BF16), pltpu.VMEM((2, ts, W), BF16),
                        pltpu.VMEM((2, ts, 3 * D_MODEL), BF16), pltpu.SemaphoreType.DMA((2, 3))],
        compiler_params=_cp(),
    )(dout, *[proj] * 5, b_gate, ya, ob, *ogs, *lses, woa, wob, woc, wo)


def _loss_head(y, target):
    T = y.shape[0]
    ts = _tile(T, 512)

    def body(y_ref, t_ref, d_ref, l_ref):
        e = y_ref[...] - t_ref[...]
        d_ref[...] = e * (1.0 / D_MODEL)
        l_ref[...] = jnp.zeros((1, 8, LANE), F32) + jnp.sum(e * e)

    tok = pl.BlockSpec((ts, D_MODEL), lambda i: (i, 0))
    return pl.pallas_call(
        body, name="loss_head", grid=(T // ts,), in_specs=[tok, tok],
        out_specs=[tok, pl.BlockSpec((1, 8, LANE), lambda i: (i, 0, 0))],
        out_shape=[jax.ShapeDtypeStruct((T, D_MODEL), F32), jax.ShapeDtypeStruct((T // ts, 8, LANE), F32)],
        compiler_params=_cp(),
    )(y, target)


def _my_index():
    return 4 * lax.axis_index("x") + 2 * lax.axis_index("y") + lax.axis_index("c")


def _peers():
    x, y, c = (lax.axis_index(a) for a in AXES)
    out = []
    for kk in range(1, N_DEV):
        px = 1 - x if kk & 4 else x
        py = 1 - y if kk & 2 else y
        pc = 1 - c if kk & 1 else c
        out.append(((px, py, pc), 4 * px + 2 * py + pc))
    return out


N_CHIP = 4


def _chip_places():
    x, y, c = (lax.axis_index(a) for a in AXES)
    return (x, y, c), (x, y, 1 - c), [(1 - x, y, c), (x, 1 - y, c), (1 - x, 1 - y, c)]


def _index_of(pos):
    return 4 * pos[0] + 2 * pos[1] + pos[2]


def _sibling_swap(arrays, name):
    n = len(arrays)

    def body(*refs):
        srcs, outs = refs[:n], refs[n:2 * n]
        send_sems, recv_sems = refs[2 * n:]
        (x, y, c), sibling, _ = _chip_places()
        sends = []
        for a, (src, out) in enumerate(zip(srcs, outs)):
            for q in range(N_CHIP):
                def copy(core, a=a, q=q, src=src, out=out):
                    return pltpu.make_async_remote_copy(
                        src_ref=src.at[2 * q + core], dst_ref=out.at[q], send_sem=send_sems.at[N_CHIP * a + q],
                        recv_sem=recv_sems.at[N_CHIP * a + q], device_id=sibling, device_id_type=pl.DeviceIdType.MESH)
                mine = copy(1 - c)
                mine.start()
                sends.append((mine, copy(c)))
        for mine, arrival in sends:
            arrival.wait_recv()
            mine.wait_send()

    any_space = pl.BlockSpec(memory_space=pl.ANY)
    return pl.pallas_call(
        body, name=name, in_specs=[any_space] * n, out_specs=[any_space] * n,
        out_shape=[jax.ShapeDtypeStruct((N_CHIP,) + a.shape[1:], a.dtype) for a in arrays],
        scratch_shapes=[pltpu.SemaphoreType.DMA((N_CHIP * n,)), pltpu.SemaphoreType.DMA((N_CHIP * n,))],
    )(*arrays)


def _chip_pair_sum(part, got, name):
    R, C = part.shape[1:]
    tr = R
    while tr * C * part.dtype.itemsize > REDUCE_BLOCK_BYTES // 4 and tr % 32 == 0:
        tr //= 2
    c = lax.axis_index("c")

    def body(c_ref, p_ref, g_ref, o_ref):
        del c_ref
        o_ref[...] = (p_ref[...].astype(F32) + g_ref[...].astype(F32)).astype(o_ref.dtype)

    return pl.pallas_call(
        body, name=name, grid_spec=pltpu.PrefetchScalarGridSpec(
            num_scalar_prefetch=1, grid=(N_CHIP, R // tr),
            in_specs=[pl.BlockSpec((None, tr, C), lambda q, i, cr: (2 * q + cr[0], i, 0)),
                      pl.BlockSpec((None, tr, C), lambda q, i, cr: (q, i, 0))],
            out_specs=pl.BlockSpec((None, tr, C), lambda q, i, cr: (q, i, 0))),
        out_shape=jax.ShapeDtypeStruct((N_CHIP, R, C), part.dtype),
        compiler_params=_cp(),
    )(jnp.reshape(c, (1,)).astype(jnp.int32), part, got)


def _peer_count(mode):
    return {"chips": N_CHIP - 1, "near": N_CHIP}.get(mode, N_DEV - 1)


def _remote_copies(srcs, lands, send_sems, recv_sems, mode):
    if mode == "chips":
        (x, y, _), _, others = _chip_places()
        my_slot, peers = 2 * x + y, [(chip, 2 * chip[0] + chip[1]) for chip in others]
    elif mode == "near":
        me, sibling, others = _chip_places()
        my_slot, peers = _index_of(me), [(pos, _index_of(pos)) for pos in [sibling] + others]
    else:
        my_slot, peers = _my_index(), _peers()
    whole = mode in ("gather", "near")
    out = []
    for i, (pos, idx) in enumerate(peers):
        for a, (src, land) in enumerate(zip(srcs, lands)):
            def copy(slot, a=a, src=src, land=land, i=i, pos=pos, idx=idx):
                return pltpu.make_async_remote_copy(
                    src_ref=src if whole else src.at[idx], dst_ref=land.at[slot],
                    send_sem=send_sems.at[a * len(peers) + i], recv_sem=recv_sems.at[a * len(peers) + i],
                    device_id=pos, device_id_type=pl.DeviceIdType.MESH)
            out.append((copy(my_slot), copy(idx)))
    return out


def _exchange_start(arrays, name, mode):
    n = len(arrays)
    hbm = pl.BlockSpec(memory_space=pltpu.HBM)
    sem = pl.BlockSpec(memory_space=pltpu.SEMAPHORE)
    lands = [lax.empty(((N_DEV,) + a.shape) if mode in ("gather", "near") else a.shape, a.dtype) for a in arrays]

    def body(*refs):
        srcs, lands_ = refs[:n], refs[n:2 * n]
        send_sems, recv_sems = refs[2 * n:2 * n + 2]
        for mine, _ in _remote_copies(srcs, lands_, send_sems, recv_sems, mode):
            mine.start()
        refs[-1][...] = jnp.zeros_like(refs[-1])

    sems = pltpu.SemaphoreType.DMA((n * _peer_count(mode),))
    buffers = [pltpu.HBM(a.shape, a.dtype) for a in list(arrays) + lands]
    res = pl.pallas_call(
        body, name=name, in_specs=[hbm] * (2 * n), out_specs=[sem, sem] + [hbm] * (2 * n) + [pl.BlockSpec(memory_space=pltpu.VMEM)],
        out_shape=[sems, sems] + buffers + [jax.ShapeDtypeStruct((8, LANE), F32)],
        input_output_aliases={i: 2 + i for i in range(2 * n)},
        compiler_params=pltpu.CompilerParams(has_side_effects=pltpu.SideEffectType.DATAFLOW_SIDE_EFFECTING),
    )(*[pltpu.with_memory_space_constraint(a, pltpu.HBM) for a in list(arrays) + lands])
    return (res[0], res[1], res[2:2 + n], res[2 + n:2 + 2 * n]), res[-1]


def _exchange_wait(handle, after, name, mode):
    send_sems, recv_sems, srcs, lands = handle
    n = len(srcs)
    after = list(after) if isinstance(after, (list, tuple)) else [after]
    hbm = pl.BlockSpec(memory_space=pltpu.HBM)
    sem = pl.BlockSpec(memory_space=pltpu.SEMAPHORE)

    def body(*refs):
        for mine, arrival in _remote_copies(refs[:n], refs[n:2 * n], refs[2 * n], refs[2 * n + 1], mode):
            mine.wait_send()
            arrival.wait_recv()

    res = pl.pallas_call(
        body, name=name, in_specs=[hbm] * (2 * n) + [sem, sem] + [pl.BlockSpec(memory_space=pl.ANY)] * len(after),
        out_specs=[hbm] * (2 * n), out_shape=[pltpu.HBM(a.shape, a.dtype) for a in list(srcs) + list(lands)],
        input_output_aliases={i: i for i in range(2 * n)},
        compiler_params=pltpu.CompilerParams(has_side_effects=pltpu.SideEffectType.DATAFLOW_SIDE_EFFECTING),
    )(*srcs, *lands, send_sems, recv_sems, *after)
    return res[n:]


def _sibling_forward(lands, name):
    n = len(lands)

    def body(*refs):
        ins, outs, send_sems, recv_sems = refs[:n], refs[n:2 * n], refs[2 * n], refs[2 * n + 1]
        (x, y, c), sibling, others = _chip_places()
        copies = []
        for a, (src, out) in enumerate(zip(ins, outs)):
            for j, chip in enumerate(others):
                def copy(core, a=a, j=j, chip=chip, src=src, out=out):
                    slot = _index_of((chip[0], chip[1], core))
                    return pltpu.make_async_remote_copy(
                        src_ref=src.at[slot], dst_ref=out.at[slot], send_sem=send_sems.at[3 * a + j],
                        recv_sem=recv_sems.at[3 * a + j], device_id=sibling, device_id_type=pl.DeviceIdType.MESH)
                mine = copy(c)
                mine.start()
                copies.append((mine, copy(1 - c)))
        for mine, arrival in copies:
            arrival.wait_recv()
        for mine, arrival in copies:
            mine.wait_send()

    any_space = pl.BlockSpec(memory_space=pl.ANY)
    return pl.pallas_call(
        body, name=name, in_specs=[any_space] * n, out_specs=[any_space] * n,
        out_shape=[jax.ShapeDtypeStruct(a.shape, a.dtype) for a in lands],
        scratch_shapes=[pltpu.SemaphoreType.DMA((3 * n,)), pltpu.SemaphoreType.DMA((3 * n,))],
        input_output_aliases={i: i for i in range(n)},
    )(*lands)


def _own_slot(land, mine, slot=None):
    slot = _my_index() if slot is None else slot
    return lax.dynamic_update_slice(land, mine, (slot,) + (0,) * (land.ndim - 1))


def _adamw(w, g, m, v):
    m = ADAM_B1 * m + (1.0 - ADAM_B1) * g
    v = ADAM_B2 * v + (1.0 - ADAM_B2) * (g * g)
    m_hat = m / (1.0 - ADAM_B1 ** ADAM_STEP)
    v_hat = v / (1.0 - ADAM_B2 ** ADAM_STEP)
    delta = -ADAM_LR * (m_hat / (jnp.sqrt(v_hat) + ADAM_EPS) + ADAM_WD * w)
    return delta, m, v


def _reduce_adamw(parts, w, m, v, name, after=None):
    nparts = len(parts)
    R, C = parts[0].shape[1:]
    tr = R
    while N_DEV * tr * C * parts[0].dtype.itemsize > REDUCE_BLOCK_BYTES and tr % 32 == 0:
        tr //= 2
    steps = R // tr
    extra = [] if after is None else [after]

    def body(*refs):
        w_ref, m_ref, v_ref, g_ref, d_ref, nm_ref, nv_ref = refs[nparts + len(extra):]
        for k, p_ref in enumerate(refs[:nparts]):
            @pl.when(pl.program_id(0) // steps == k)
            def _():
                g = p_ref[0].astype(F32)
                for s in range(1, p_ref.shape[0]):
                    g = g + p_ref[s].astype(F32)
                g_ref[...] = g
                d_ref[...], nm_ref[...], nv_ref[...] = _adamw(w_ref[...], g, m_ref[...], v_ref[...])

    def part_spec(k):
        return pl.BlockSpec((parts[k].shape[0], tr, C), lambda i: (0, jnp.clip(i - k * steps, 0, steps - 1), 0))

    row = pl.BlockSpec((tr, C), lambda i: (i, 0))
    return pl.pallas_call(
        body, name=name, grid=(nparts * steps,),
        in_specs=[part_spec(k) for k in range(nparts)] + [pl.BlockSpec(memory_space=pl.ANY)] * len(extra)
                 + [row, row, row],
        out_specs=[row] * 4, out_shape=[jax.ShapeDtypeStruct((nparts * R, C), F32)] * 4,
        compiler_params=_cp(),
    )(*parts, *extra, w, m, v)


BIG = ("w_in", "w_uq", "w_ukv", "w_out_a", "w_out_b", "w_out_c", "w_o")
SMALL = ("norm_g", "b_gate", "conv_w", "conv_b", "q_a_norm_g", "kv_a_norm_g", "mla_q_norm_g", "mla_k_norm_g",
         "dil_q_norm_g", "dil_k_norm_g")
PACK_ROWS = 128
REDUCE_BLOCK_BYTES = 6 * 1024 * 1024


def _pack_local(tensors):
    flat = jnp.concatenate([t.reshape(-1) for t in tensors])
    pad = (-flat.shape[0]) % (PACK_ROWS * LANE)
    return jnp.concatenate([flat, jnp.zeros((pad,), flat.dtype)]).reshape(-1, LANE)


def _unpack_local(rows, like):
    flat = rows.reshape(-1)
    out, off = [], 0
    for t in like:
        out.append(flat[off:off + t.size].reshape(t.shape))
        off += t.size
    return out


def _cols_to_slots(a):
    k = a.shape[0]
    return a.reshape(k, N_DEV, -1).transpose(1, 0, 2)


def _slots_to_cols(s):
    return s.transpose(1, 0, 2).reshape(s.shape[1], -1)


def _rope_tables(S):
    inv = ROPE_THETA ** (-jnp.arange(0, MLA_ROPE, 2, dtype=F32) / MLA_ROPE)
    ang = jnp.arange(S, dtype=F32)[:, None] * inv[None, :]
    cos, sin = jnp.cos(ang), jnp.sin(ang)
    one = jnp.ones((S, MLA_NOPE), F32)
    z16, z32, z64 = (jnp.zeros((S, n), F32) for n in (16, 32, 64))
    cosp = jnp.concatenate([one, cos, cos, jnp.ones((S, 32), F32)], axis=1)
    sa = jnp.concatenate([z64, -sin, z16, z32], axis=1)
    sb = jnp.concatenate([z64, z16, sin, z32], axis=1)
    return cosp, sa, sb


def _alibi_slopes():
    n = DIL_GROUPS * DIL_HEADS
    m = 2.0 ** (-8.0 * jnp.arange(1, n + 1, dtype=F32) / n)
    return m.reshape(DIL_GROUPS, NPAIR, 2)


def _pad_slots(s):
    n, k, c = s.shape
    return _slots_to_cols(jnp.concatenate([s, jnp.zeros((n, k, LANE - c), s.dtype)], axis=2))


def _layer_params(gw, small, l):
    p = {}
    p["wp"] = _pad_columns(gw["w_in"])
    p["norm_g"] = small["norm_g"][l][None]
    p["b_gate"] = small["b_gate"][l][None]
    p["conv_w"] = gw["conv_w"].transpose(1, 0, 2).reshape(CONV_K, CONV_WIDTH)
    p["conv_b"] = small["conv_b"][l][None]
    p["gq"] = small["q_a_norm_g"][l][None]
    p["gkv"] = small["kv_a_norm_g"][l][None]
    p["wuqp"] = _pad_slots(gw["w_uq"])
    kv = gw["w_ukv"]
    p["wkp"] = _pad_slots(kv[:, :, :MLA_NOPE])
    p["wv"] = kv[:, :, MLA_NOPE:].transpose(1, 0, 2).reshape(MLA_KV_LORA, MLA_HEADS * MLA_V)
    zpad = jnp.zeros((1, LANE - MLA_QK), F32)
    p["gmq"] = jnp.concatenate([small["mla_q_norm_g"][l][None], zpad], axis=1)
    p["gmk"] = jnp.concatenate([small["mla_k_norm_g"][l][None], zpad], axis=1)
    tile = lambda g: jnp.broadcast_to(g[:, None, :], (DIL_GROUPS, DIL_HEADS, DIL_HEAD_DIM)).reshape(1, DIL_QK)
    p["gdq"] = tile(small["dil_q_norm_g"][l])
    p["gdk"] = tile(small["dil_k_norm_g"][l])
    p["woa"], p["wob"], p["woc"] = (_slots_to_cols(gw[n]) for n in ("w_out_a", "w_out_b", "w_out_c"))
    p["wo"] = gw["w_o"].reshape(D_MODEL, D_MODEL)
    return p


def _layer_fwd(x, p, tabs, slopes, B, S):
    proj, ht = _inproj_fwd(x, p["norm_g"], p["wp"])
    ya = _mixa_fwd(proj, p["conv_w"], p["conv_b"], B, S)
    q, k, v = _mla_prep_fwd(proj, p["gq"], p["gkv"], p["wuqp"], p["wkp"], p["wv"], p["gmq"], p["gmk"], *tabs, S)
    ob, lse_b = _mla_attn_fwd(q, k, v, B, S)
    qn, kn = _dil_prep_fwd(proj, p["gdq"], p["gdk"])
    ogs, lses = [], []
    for gi in range(DIL_GROUPS):
        o, lse = _dil_attn_fwd(gi, slopes[gi], qn, kn, proj, B, S)
        ogs.append(o)
        lses.append(lse)
    out = _merge_fwd(x, proj, p["b_gate"], ya, ob, ogs, lses, p["woa"], p["wob"], p["woc"], p["wo"])
    saved = dict(x=x, proj=proj, ht=ht, ya=ya, q=q, k=k, v=v, ob=ob, lse_b=lse_b, qn=qn, kn=kn, ogs=ogs, lses=lses)
    return out, saved


def _layer_bwd(dout, sv, p, tabs, slopes, B, S, big_ready=None):
    proj = sv["proj"]
    (dproj, dya, dob, dlb, dg0, dg1, dg2, dl0, dl1, dl2, merged, dpa, dpb, dpc, yb, yc, dbg) = _merge_bwd(
        dout, proj, p["b_gate"], sv["ya"], sv["ob"], sv["ogs"], sv["lses"], p["woa"], p["wob"], p["woc"], p["wo"])
    g = {}
    g["w_o"] = _matmul_tn(merged, dout, "dw_o").reshape(N_DEV, D_MODEL // N_DEV, D_MODEL)
    g["w_out_a"] = _cols_to_slots(_matmul_tn(sv["ya"], dpa, "dw_out_a"))
    g["w_out_b"] = _cols_to_slots(_matmul_tn(yb, dpb, "dw_out_b"))
    g["w_out_c"] = _cols_to_slots(_matmul_tn(yc, dpc, "dw_out_c"))
    g["b_gate"] = dbg[0]
    dproj, st = _mixa_bwd(dproj, dya, proj, p["conv_w"], p["conv_b"], B, S)
    g["conv_w"] = st[0:CONV_K]
    g["conv_b"] = st[CONV_K]
    dq, dk, dv = _mla_attn_bwd(sv["q"], sv["k"], sv["v"], dob, sv["lse_b"], dlb, B, S)
    dproj, dwuqp, dwkp, dwv, dgq, dgkv, dgmq, dgmk = _mla_prep_bwd(
        dproj, dq, dk, dv, proj, p["gq"], p["gkv"], p["wuqp"], p["wkp"], p["wv"], p["gmq"], p["gmk"], *tabs, S)
    g["w_uq"] = _cols_to_slots(dwuqp)[:, :, :MLA_QK]
    g["w_ukv"] = jnp.concatenate([_cols_to_slots(dwkp)[:, :, :MLA_NOPE], _cols_to_slots(dwv)], axis=2)
    g["q_a_norm_g"], g["kv_a_norm_g"] = dgq[0], dgkv[0]
    g["mla_q_norm_g"], g["mla_k_norm_g"] = dgmq[0, :MLA_QK], dgmk[0, :MLA_QK]
    dqkv = None
    for gi, (dog, dlg) in enumerate(((dg0, dl0), (dg1, dl1), (dg2, dl2))):
        dqkv = _dil_attn_bwd(gi, slopes[gi], sv["qn"], sv["kn"], proj, dog, sv["lses"][gi], dlg, dqkv, B, S)
    dproj, dgdq, dgdk = _dil_prep_bwd(dproj, *dqkv, proj, p["gdq"], p["gdk"])
    g["dil_q_norm_g"] = dgdq.reshape(DIL_GROUPS, DIL_HEADS, DIL_HEAD_DIM).sum(axis=1)
    g["dil_k_norm_g"] = dgdk.reshape(DIL_GROUPS, DIL_HEADS, DIL_HEAD_DIM).sum(axis=1)
    token = None if big_ready is None else big_ready(g)
    g["w_in"] = _unpad_columns(_matmul_nn(sv["ht"], dproj, "dw_in", token))
    token = None if big_ready is None else big_ready(g)
    dx, dng = _inproj_bwd_x(dproj, p["wp"], sv["x"], _after(token, p["norm_g"]), dout)
    g["norm_g"] = dng[0]
    return dx, g


def _after(token, a):
    return a if token is None else a + token[0:1, 0:1]


def _local_step(x, target, small, B, S, weights_of, grads_out, big_ready=None):
    tabs = _rope_tables(S)
    sl = _alibi_slopes()
    slopes = [sl[gi] * float(DIL_PATTERNS[gi][1]) for gi in range(DIL_GROUPS)]
    params, saved = [], []
    for l in range(DEPTH):
        gw, token = weights_of(l, x)
        p = _layer_params(gw, small, l)
        p["norm_g"] = _after(token, p["norm_g"])
        x, sv = _layer_fwd(x, p, tabs, slopes, B, S)
        params.append(p)
        saved.append(sv)
    dout, lparts = _loss_head(x, target)
    sq = jnp.sum(lparts[:, 0, 0])
    token = None
    for l in reversed(range(DEPTH)):
        p = dict(params[l], b_gate=_after(token, params[l]["b_gate"]))
        ready = None if big_ready is None else (lambda g, l=l: big_ready(l, g))
        dout, g = _layer_bwd(dout, saved[l], p, tabs, slopes, B, S, ready)
        token = grads_out(l, g, dout)
    return sq, dout


def kernel(x, norm_g, w_in, b_gate, conv_w, conv_b, q_a_norm_g, w_uq, kv_a_norm_g, w_ukv, mla_q_norm_g, mla_k_norm_g, dil_q_norm_g, dil_k_norm_g, w_out_a, w_out_b, w_out_c, w_o, loss_target, m_norm_g, m_w_in, m_b_gate, m_conv_w, m_conv_b, m_q_a_norm_g, m_w_uq, m_kv_a_norm_g, m_w_ukv, m_mla_q_norm_g, m_mla_k_norm_g, m_dil_q_norm_g, m_dil_k_norm_g, m_w_out_a, m_w_out_b, m_w_out_c, m_w_o, v_norm_g, v_w_in, v_b_gate, v_conv_w, v_conv_b, v_q_a_norm_g, v_w_uq, v_kv_a_norm_g, v_w_ukv, v_mla_q_norm_g, v_mla_k_norm_g, v_dil_q_norm_g, v_dil_k_norm_g, v_w_out_a, v_w_out_b, v_w_out_c, v_w_o):
    names = ("norm_g", "w_in", "b_gate", "conv_w", "conv_b", "q_a_norm_g", "w_uq", "kv_a_norm_g", "w_ukv",
             "mla_q_norm_g", "mla_k_norm_g", "dil_q_norm_g", "dil_k_norm_g", "w_out_a", "w_out_b", "w_out_c", "w_o")
    w = dict(zip(names, (norm_g, w_in, b_gate, conv_w, conv_b, q_a_norm_g, w_uq, kv_a_norm_g, w_ukv, mla_q_norm_g,
                         mla_k_norm_g, dil_q_norm_g, dil_k_norm_g, w_out_a, w_out_b, w_out_c, w_o)))
    m = dict(zip(names, (m_norm_g, m_w_in, m_b_gate, m_conv_w, m_conv_b, m_q_a_norm_g, m_w_uq, m_kv_a_norm_g, m_w_ukv,
                         m_mla_q_norm_g, m_mla_k_norm_g, m_dil_q_norm_g, m_dil_k_norm_g, m_w_out_a, m_w_out_b,
                         m_w_out_c, m_w_o)))
    v = dict(zip(names, (v_norm_g, v_w_in, v_b_gate, v_conv_w, v_conv_b, v_q_a_norm_g, v_w_uq, v_kv_a_norm_g, v_w_ukv,
                         v_mla_q_norm_g, v_mla_k_norm_g, v_dil_q_norm_g, v_dil_k_norm_g, v_w_out_a, v_w_out_b,
                         v_w_out_c, v_w_o)))
    B, S, _ = x.shape
    me = _my_index()
    cshard = CONV_WIDTH // N_DEV

    shards = [[w[n][0].astype(BF16) for n in BIG]]
    state = {}

    def widen(t):
        return lax.dynamic_update_slice(jnp.zeros((DEPTH, CONV_K, CONV_WIDTH), F32), t, (0, 0, me * cshard))

    pick = lambda d: [widen(d[n]) if n == "conv_w" else d[n] for n in SMALL]

    def weights_of(l, after):
        if l == 0:
            first = shards[0] + [conv_w]
            handle, token = _exchange_start(first, "all_gather_weights_0_start", "near")
            zero = token[0:1, 0:1]
            state["shards1"] = [(w[n][1] + zero).astype(BF16) for n in BIG]
            for n in BIG:
                state["rows", n] = [a.reshape(-1, a.shape[-1]) + zero for a in (w[n], m[n], v[n])]
            state["small"] = [_pack_local(pick(d)) + zero for d in (w, m, v)]
            busy = state["shards1"] + [a for n in BIG for a in state["rows", n]] + state["small"]
            landed = _exchange_wait(handle, busy, "all_gather_weights_0_wait", "near")
            landed = _sibling_forward(landed, "all_gather_weights_0_forward")
            got = [_own_slot(a, s[None]) for a, s in zip(landed, first)]
            state["gather"], token = _exchange_start(state["shards1"], "all_gather_weights_1_start", "gather")
            state["conv_w"] = got[-1]
        else:
            landed = _exchange_wait(state["gather"], after, "all_gather_weights_1_wait", "gather")
            got, token = [_own_slot(a, s[None]) for a, s in zip(landed, state["shards1"])], None
        gw = dict(zip(BIG, got))
        gw["conv_w"] = state["conv_w"][:, l]
        return gw, token

    recv, small_parts = {}, {}
    my_chip = 2 * lax.axis_index("x") + lax.axis_index("y")

    REST = tuple(n for n in BIG if n != "w_in")

    def big_ready(l, g):
        if l == DEPTH - 1:
            if "w_in" not in g:
                return None
            send = [g[n].astype(BF16) for n in BIG]
            state["scatter"], token = _exchange_start(send, "exchange_weight_grads_1_start", "scatter")
            state["sent", "scatter"] = send
            return token
        if "w_in" not in g:
            send = [g[n].astype(BF16) for n in REST]
            state["rest"], token = _exchange_start(send, "exchange_weight_grads_0_start_rest", "scatter")
            state["sent", "rest"] = send
            return token
        send = [g["w_in"].astype(BF16)]
        swapped = _sibling_swap(send, "exchange_weight_grads_0_sibling_w_in")
        send = [_chip_pair_sum(send[0], swapped[0], "chip_pair_sum_w_in")]
        state["w_in"], token = _exchange_start(send, "exchange_weight_grads_0_start_w_in", "chips")
        state["sent", "w_in"] = send
        return token

    def grads_out(l, g, after):
        small_parts[l] = [g[n] for n in SMALL]
        if l == DEPTH - 1:
            return None
        got = {}
        for key, mode, slot in (("scatter", "scatter", me), ("rest", "scatter", me), ("w_in", "chips", my_chip)):
            k = DEPTH - 1 if key == "scatter" else 0
            tag = "" if key == "scatter" else "_" + key
            landed = _exchange_wait(state[key], after, f"exchange_weight_grads_{k}_wait{tag}", mode)
            mine = [lax.dynamic_slice_in_dim(s, slot, 1, axis=0) for s in state["sent", key]]
            got[key] = [_own_slot(a, s, slot) for a, s in zip(landed, mine)]
        recv[DEPTH - 1] = got["scatter"]
        recv[0] = got["w_in"] + got["rest"]
        assert BIG == ("w_in",) + REST
        return None

    sq, grad_x = _local_step(x.reshape(B * S, D_MODEL), loss_target.reshape(B * S, D_MODEL), w, B, S,
                             weights_of, grads_out, big_ready)

    part = {n: jnp.stack([small_parts[l][i] for l in range(DEPTH)]) for i, n in enumerate(SMALL)}
    small_like = [part[n] for n in SMALL] + [sq.reshape(1)]
    pack = _pack_local(small_like)
    assert pack.shape == state["small"][0].shape
    handle, token = _exchange_start([pack], "all_gather_small_grads_start", "gather")

    res, done = {}, []
    for i, n in enumerate(BIG):
        outs = _reduce_adamw([recv[l][i] for l in range(DEPTH)], *state["rows", n], "reduce_adamw_" + n, token)
        res[n] = tuple(a.reshape(w[n].shape) for a in outs)
        done.append(outs[0])

    landed, = _exchange_wait(handle, done, "all_gather_small_grads_wait", "gather")
    parts = _own_slot(landed, pack[None])
    gs, ds, ms, vs = _reduce_adamw([parts], *state["small"], "reduce_adamw_small")
    for n, t in zip(SMALL, zip(*(_unpack_local(a, small_like) for a in (gs, ds, ms, vs)))):
        if n == "conv_w":
            t = tuple(lax.dynamic_slice(a, (0, 0, me * cshard), (DEPTH, CONV_K, cshard)) for a in t)
        res[n] = t
    loss = _unpack_local(gs, small_like)[-1].reshape(()) * (0.5 / D_MODEL)

    out = [loss, grad_x.reshape(B, S, D_MODEL)]
    for i in range(4):
        out += [res[n][i] for n in names]
    return tuple(out)
```
